```python
import math
import jax, jax.numpy as jnp
from jax import lax
import numpy as np

D_MODEL = 1024
BATCH = 8
SEQ = 4096
DEPTH = 4

A_HEAD_DIM = 64
A_Q_HEADS = 8
A_KV_HEADS = 2
WINDOW = 128
ROPE_THETA = 10000.0
B_HEADS = 4
B_HEAD_DIM = 128
B_CONV = 4
B_CHUNK = 64
LRU_WIDTH = D_MODEL
LRU_BLOCKS = 4
LRU_CONV = 4
LRU_C = 8.0
D_FF = 4 * D_MODEL
A_Q_W = A_Q_HEADS * A_HEAD_DIM
A_KV_W = A_KV_HEADS * A_HEAD_DIM
B_W = B_HEADS * B_HEAD_DIM
HYB_SPLITS = (A_Q_W, A_KV_W, A_KV_W, 3 * B_W, B_W, B_HEADS, B_HEADS)
HYB_PROJ = A_Q_W + 2 * A_KV_W + 4 * B_W + 2 * B_HEADS
MIX_W = A_Q_W + B_W
N_HYB = (DEPTH + 1) // 2
N_REC = DEPTH // 2
DN_ALPHA = (2 * DEPTH) ** 0.25
DN_BETA = (8 * DEPTH) ** -0.25
LN_EPS = 1e-5
NORM_EPS = 1e-6

kernel_name = 'hybrid_swa_deltanet_rglru_deepnorm_trunk'


def split_cols(t, sizes):
    out, start = [], 0
    for s in sizes:
        out.append(t[..., start:start + s])
        start += s
    return out


def layer_norm(x, g, b):
    xf = x.astype(jnp.float32)
    mu = jnp.mean(xf, axis=-1, keepdims=True)
    var = jnp.mean(jnp.square(xf - mu), axis=-1, keepdims=True)
    y = (xf - mu) * lax.rsqrt(var + LN_EPS) * g.astype(jnp.float32) + b.astype(jnp.float32)
    return y.astype(x.dtype)


def l2_normalize(x):
    xf = x.astype(jnp.float32)
    return xf * lax.rsqrt(jnp.sum(xf * xf, axis=-1, keepdims=True) + NORM_EPS)


def apply_rope(x):
    T, dh = x.shape[1], x.shape[-1]
    half = dh // 2
    inv_freq = ROPE_THETA ** (-jnp.arange(half, dtype=jnp.float32) / half)
    ang = jnp.arange(T, dtype=jnp.float32)[:, None] * inv_freq[None, :]
    cos = jnp.cos(ang)[None, :, None, :]
    sin = jnp.sin(ang)[None, :, None, :]
    xf = x.astype(jnp.float32)
    x1, x2 = xf[..., :half], xf[..., half:]
    return jnp.concatenate([x1 * cos - x2 * sin, x2 * cos + x1 * sin], axis=-1).astype(x.dtype)


def causal_dwconv(x, w, b=None):
    K = w.shape[0]
    T = x.shape[1]
    xp = jnp.pad(x, ((0, 0), (K - 1, 0), (0, 0)))
    y = sum(xp[:, j:j + T] * w[j] for j in range(K))
    if b is not None:
        y = y + b
    return y


def sliding_window_attention(q, k, v, sinks):
    f32 = jnp.float32
    bsz, T, HQ, DH = q.shape
    HKV = k.shape[2]
    G = HQ // HKV
    W = WINDOW
    NB = T // W
    qb = q.reshape(bsz, NB, W, HKV, G, DH)

    def band(t):
        prev = jnp.pad(t, ((0, 0), (W, 0), (0, 0), (0, 0)))[:, :T]
        return jnp.concatenate([prev.reshape(bsz, NB, W, HKV, DH),
                                t.reshape(bsz, NB, W, HKV, DH)], axis=2)

    kb, vb = band(k), band(v)
    s = jnp.einsum('bnqhgd,bnkhd->bnhgqk', qb, kb).astype(f32) * (DH ** -0.5)
    dist = (jnp.arange(W)[:, None] + W) - jnp.arange(2 * W)[None, :]
    in_window = (dist >= 0) & (dist < W)
    key_pos = (jnp.arange(NB)[:, None] - 1) * W + jnp.arange(2 * W)[None, :]
    mask = in_window[None] & (key_pos >= 0)[:, None, :]
    s = jnp.where(mask[None, :, None, None], s, -jnp.inf)
    sink = jnp.broadcast_to(sinks.astype(f32).reshape(1, 1, HKV, G, 1, 1), s.shape[:-1] + (1,))
    p = jax.nn.softmax(jnp.concatenate([s, sink], axis=-1), axis=-1)[..., :-1]
    o = jnp.einsum('bnhgqk,bnkhd->bnqhgd', p.astype(v.dtype), vb)
    return o.reshape(bsz, T, HQ * DH)


def gated_delta_rule(q, k, v, g, beta):
    f32 = jnp.float32
    bsz, T, H, DK = q.shape
    DV = v.shape[-1]
    C = B_CHUNK
    N = T // C

    def chunk(t):
        t = t.astype(f32).reshape((bsz, N, C, H) + t.shape[3:])
        return jnp.moveaxis(t, 3, 1)

    q = chunk(q) * (DK ** -0.5)
    k = chunk(k)
    v = chunk(v)
    g = jnp.cumsum(chunk(g), axis=-1)
    beta = chunk(beta)
    k_beta = k * beta[..., None]
    v_beta = v * beta[..., None]
    idx = jnp.arange(C)
    incl = idx[:, None] >= idx[None, :]
    strict = idx[:, None] > idx[None, :]
    diff = g[..., :, None] - g[..., None, :]
    decay_incl = jnp.exp(jnp.where(incl, diff, -jnp.inf))
    decay_strict = jnp.where(strict, decay_incl, 0.0)
    a_mat = jnp.einsum('bhnik,bhnjk->bhnij', k_beta, k) * decay_strict
    eye = jnp.eye(C, dtype=f32)
    t_mat = lax.linalg.triangular_solve(a_mat + eye, jnp.broadcast_to(eye, a_mat.shape),
                                        left_side=True, lower=True, unit_diagonal=True)
    u = jnp.einsum('bhnij,bhnjv->bhniv', t_mat, v_beta)
    w = jnp.einsum('bhnij,bhnjk->bhnik', t_mat, k_beta * jnp.exp(g)[..., None])
    qk = jnp.einsum('bhnik,bhnjk->bhnij', q, k) * decay_incl
    q_g = q * jnp.exp(g)[..., None]
    g_last = g[..., -1]
    k_tail = k * jnp.exp(g_last[..., None] - g)[..., None]
    xs = tuple(jnp.moveaxis(t, 2, 0) for t in (q_g, k_tail, u, w, qk, g_last))

    def step(S, inp):
        q_c, kt_c, u_c, w_c, qk_c, gl_c = inp
        v_new = u_c - jnp.einsum('bhck,bhkv->bhcv', w_c, S)
        o_c = jnp.einsum('bhck,bhkv->bhcv', q_c, S) + jnp.einsum('bhij,bhjv->bhiv', qk_c, v_new)
        S = S * jnp.exp(gl_c)[..., None, None] + jnp.einsum('bhck,bhcv->bhkv', kt_c, v_new)
        return S, o_c

    S0 = jnp.zeros((bsz, H, DK, DV), f32)
    _, o = lax.scan(step, S0, xs)
    return jnp.transpose(o, (1, 0, 3, 2, 4)).reshape(bsz, T, H, DV)


def hybrid_mixer(x, w_in, sinks, conv_w, a_log, dt_bias, norm_w, w_out):
    f32 = jnp.float32
    bsz, T, _ = x.shape
    proj = x @ w_in
    qa, ka, va, qkv_b, z, beta_logit, a_logit = split_cols(proj, HYB_SPLITS)
    qa = apply_rope(qa.reshape(bsz, T, A_Q_HEADS, A_HEAD_DIM))
    ka = apply_rope(ka.reshape(bsz, T, A_KV_HEADS, A_HEAD_DIM))
    va = va.reshape(bsz, T, A_KV_HEADS, A_HEAD_DIM)
    o_a = sliding_window_attention(qa, ka, va, sinks)
    qkv_b = jax.nn.silu(causal_dwconv(qkv_b, conv_w))
    qb, kb, vb = split_cols(qkv_b, (B_W, B_W, B_W))
    qb = l2_normalize(qb.reshape(bsz, T, B_HEADS, B_HEAD_DIM))
    kb = l2_normalize(kb.reshape(bsz, T, B_HEADS, B_HEAD_DIM))
    vb = vb.reshape(bsz, T, B_HEADS, B_HEAD_DIM)
    beta = jax.nn.sigmoid(beta_logit.astype(f32))
    g = -jnp.exp(a_log.astype(f32)) * jax.nn.softplus(a_logit.astype(f32) + dt_bias.astype(f32))
    o_b = gated_delta_rule(qb, kb, vb, g, beta)
    o_b = o_b * lax.rsqrt(jnp.mean(o_b * o_b, axis=-1, keepdims=True) + NORM_EPS) * norm_w.astype(f32)
    o_b = o_b * jax.nn.silu(z.astype(f32).reshape(bsz, T, B_HEADS, B_HEAD_DIM))
    o_b = o_b.reshape(bsz, T, B_W).astype(x.dtype)
    return jnp.concatenate([o_a, o_b], axis=-1) @ w_out


def rg_lru(x, w_a, b_a, w_x, b_x, lam):
    f32 = jnp.float32
    bsz, T, W = x.shape
    xh = x.reshape(bsz, T, LRU_BLOCKS, W // LRU_BLOCKS)
    r = jax.nn.sigmoid((jnp.einsum('bthi,hij->bthj', xh, w_a).reshape(bsz, T, W) + b_a).astype(f32))
    i = jax.nn.sigmoid((jnp.einsum('bthi,hij->bthj', xh, w_x).reshape(bsz, T, W) + b_x).astype(f32))
    log_a = -LRU_C * r * jax.nn.softplus(-lam.astype(f32))
    a = jnp.exp(log_a)
    b = jnp.sqrt(-jnp.expm1(2.0 * log_a)) * (i * x.astype(f32))

    def combine(c1, c2):
        a1, b1 = c1
        a2, b2 = c2
        return a1 * a2, a2 * b1 + b2

    _, h = lax.associative_scan(combine, (a, b), axis=1)
    return h.astype(x.dtype)


def recurrent_mixer(x, w_in, conv_w, conv_b, w_a, b_a, w_x, b_x, lam, w_out):
    proj = x @ w_in
    xr, gate = split_cols(proj, (LRU_WIDTH, LRU_WIDTH))
    xr = causal_dwconv(xr, conv_w, conv_b)
    h = rg_lru(xr, w_a, b_a, w_x, b_x, lam)
    return (h * jax.nn.gelu(gate)) @ w_out


def sqrelu_mlp(x, w1, w2):
    h = jax.nn.relu(x @ w1)
    return (h * h) @ w2


def _fwd_setup_inputs(seed: int = 0) -> dict:
    key = jax.random.key(seed)
    ks = jax.random.split(key, 24)
    f32 = jnp.float32

    def nrm(k, shape, scale):
        return jax.random.normal(k, shape, f32) * scale

    bw = LRU_WIDTH // LRU_BLOCKS
    x = nrm(ks[0], (BATCH, SEQ, D_MODEL), 1.0)
    hyb_w_in = nrm(ks[1], (N_HYB, D_MODEL, HYB_PROJ), D_MODEL ** -0.5)
    hyb_sinks = nrm(ks[2], (N_HYB, A_Q_HEADS), 0.5)
    hyb_conv_w = nrm(ks[3], (N_HYB, B_CONV, 3 * B_W), B_CONV ** -0.5)
    hyb_a_log = jnp.log(jax.random.uniform(ks[4], (N_HYB, B_HEADS), f32, 1.0, 16.0))
    dt = jnp.exp(jax.random.uniform(ks[5], (N_HYB, B_HEADS), f32, math.log(1e-3), math.log(1e-1)))
    hyb_dt_bias = dt + jnp.log(-jnp.expm1(-dt))
    hyb_norm_w = 1.0 + nrm(ks[6], (N_HYB, B_HEAD_DIM), 0.02)
    hyb_w_out = nrm(ks[7], (N_HYB, MIX_W, D_MODEL), DN_BETA * MIX_W ** -0.5)
    rec_w_in = nrm(ks[8], (N_REC, D_MODEL, 2 * LRU_WIDTH), D_MODEL ** -0.5)
    rec_conv_w = nrm(ks[9], (N_REC, LRU_CONV, LRU_WIDTH), LRU_CONV ** -0.5)
    rec_conv_b = nrm(ks[10], (N_REC, LRU_WIDTH), 0.02)
    rec_w_a = nrm(ks[11], (N_REC, LRU_BLOCKS, bw, bw), bw ** -0.5)
    rec_b_a = nrm(ks[12], (N_REC, LRU_WIDTH), 0.02)
    rec_w_x = nrm(ks[13], (N_REC, LRU_BLOCKS, bw, bw), bw ** -0.5)
    rec_b_x = nrm(ks[14], (N_REC, LRU_WIDTH), 0.02)
    u = jax.random.uniform(ks[15], (N_REC, LRU_WIDTH), f32, 0.9, 0.999)
    a0 = u ** (1.0 / LRU_C)
    rec_lambda = jnp.log(a0) - jnp.log1p(-a0)
    rec_w_out = nrm(ks[16], (N_REC, LRU_WIDTH, D_MODEL), DN_BETA * LRU_WIDTH ** -0.5)
    ln1_g = 1.0 + nrm(ks[17], (DEPTH, D_MODEL), 0.02)
    ln1_b = nrm(ks[18], (DEPTH, D_MODEL), 0.02)
    mlp_w1 = nrm(ks[19], (DEPTH, D_MODEL, D_FF), D_MODEL ** -0.5)
    mlp_w2 = nrm(ks[20], (DEPTH, D_FF, D_MODEL), DN_BETA * D_FF ** -0.5)
    ln2_g = 1.0 + nrm(ks[21], (DEPTH, D_MODEL), 0.02)
    ln2_b = nrm(ks[22], (DEPTH, D_MODEL), 0.02)
    return {'x': x, 'hyb_w_in': hyb_w_in, 'hyb_sinks': hyb_sinks, 'hyb_conv_w': hyb_conv_w,
            'hyb_a_log': hyb_a_log, 'hyb_dt_bias': hyb_dt_bias, 'hyb_norm_w': hyb_norm_w,
            'hyb_w_out': hyb_w_out, 'rec_w_in': rec_w_in, 'rec_conv_w': rec_conv_w,
            'rec_conv_b': rec_conv_b, 'rec_w_a': rec_w_a, 'rec_b_a': rec_b_a, 'rec_w_x': rec_w_x,
            'rec_b_x': rec_b_x, 'rec_lambda': rec_lambda, 'rec_w_out': rec_w_out,
            'ln1_g': ln1_g, 'ln1_b': ln1_b, 'mlp_w1': mlp_w1, 'mlp_w2': mlp_w2,
            'ln2_g': ln2_g, 'ln2_b': ln2_b}


def _fwd_reference(x, hyb_w_in, hyb_sinks, hyb_conv_w, hyb_a_log, hyb_dt_bias, hyb_norm_w, hyb_w_out,
              rec_w_in, rec_conv_w, rec_conv_b, rec_w_a, rec_b_a, rec_w_x, rec_b_x, rec_lambda,
              rec_w_out, ln1_g, ln1_b, mlp_w1, mlp_w2, ln2_g, ln2_b):
    for layer in range(DEPTH):
        j = layer // 2
        if layer % 2 == 0:
            mix = hybrid_mixer(x, hyb_w_in[j], hyb_sinks[j], hyb_conv_w[j], hyb_a_log[j],
                               hyb_dt_bias[j], hyb_norm_w[j], hyb_w_out[j])
        else:
            mix = recurrent_mixer(x, rec_w_in[j], rec_conv_w[j], rec_conv_b[j], rec_w_a[j],
                                  rec_b_a[j], rec_w_x[j], rec_b_x[j], rec_lambda[j], rec_w_out[j])
        x = layer_norm(DN_ALPHA * x + mix, ln1_g[layer], ln1_b[layer])
        x = layer_norm(DN_ALPHA * x + sqrelu_mlp(x, mlp_w1[layer], mlp_w2[layer]), ln2_g[layer], ln2_b[layer])
    return x


import jax as _jax
import jax.numpy as _jnp

TWIN_FORMAT = 'train_step'
FWD_PARAMS = ['x', 'hyb_w_in', 'hyb_sinks', 'hyb_conv_w', 'hyb_a_log', 'hyb_dt_bias', 'hyb_norm_w', 'hyb_w_out', 'rec_w_in', 'rec_conv_w', 'rec_conv_b', 'rec_w_a', 'rec_b_a', 'rec_w_x', 'rec_b_x', 'rec_lambda', 'rec_w_out', 'ln1_g', 'ln1_b', 'mlp_w1', 'mlp_w2', 'ln2_g', 'ln2_b']
TWIN_WEIGHTS = ['hyb_w_in', 'hyb_sinks', 'hyb_conv_w', 'hyb_a_log', 'hyb_dt_bias', 'hyb_norm_w', 'hyb_w_out', 'rec_w_in', 'rec_conv_w', 'rec_conv_b', 'rec_w_a', 'rec_b_a', 'rec_w_x', 'rec_b_x', 'rec_lambda', 'rec_w_out', 'ln1_g', 'ln1_b', 'mlp_w1', 'mlp_w2', 'ln2_g', 'ln2_b']
TWIN_DIFF_INPUT = 'x'
TWIN_INPUTS = ['x', 'hyb_w_in', 'hyb_sinks', 'hyb_conv_w', 'hyb_a_log', 'hyb_dt_bias', 'hyb_norm_w', 'hyb_w_out', 'rec_w_in', 'rec_conv_w', 'rec_conv_b', 'rec_w_a', 'rec_b_a', 'rec_w_x', 'rec_b_x', 'rec_lambda', 'rec_w_out', 'ln1_g', 'ln1_b', 'mlp_w1', 'mlp_w2', 'ln2_g', 'ln2_b', 'loss_target', 'm_hyb_w_in', 'm_hyb_sinks', 'm_hyb_conv_w', 'm_hyb_a_log', 'm_hyb_dt_bias', 'm_hyb_norm_w', 'm_hyb_w_out', 'm_rec_w_in', 'm_rec_conv_w', 'm_rec_conv_b', 'm_rec_w_a', 'm_rec_b_a', 'm_rec_w_x', 'm_rec_b_x', 'm_rec_lambda', 'm_rec_w_out', 'm_ln1_g', 'm_ln1_b', 'm_mlp_w1', 'm_mlp_w2', 'm_ln2_g', 'm_ln2_b', 'v_hyb_w_in', 'v_hyb_sinks', 'v_hyb_conv_w', 'v_hyb_a_log', 'v_hyb_dt_bias', 'v_hyb_norm_w', 'v_hyb_w_out', 'v_rec_w_in', 'v_rec_conv_w', 'v_rec_conv_b', 'v_rec_w_a', 'v_rec_b_a', 'v_rec_w_x', 'v_rec_b_x', 'v_rec_lambda', 'v_rec_w_out', 'v_ln1_g', 'v_ln1_b', 'v_mlp_w1', 'v_mlp_w2', 'v_ln2_g', 'v_ln2_b']
TWIN_OUTPUTS = ['loss', 'grad_x', 'grad_hyb_w_in', 'grad_hyb_sinks', 'grad_hyb_conv_w', 'grad_hyb_a_log', 'grad_hyb_dt_bias', 'grad_hyb_norm_w', 'grad_hyb_w_out', 'grad_rec_w_in', 'grad_rec_conv_w', 'grad_rec_conv_b', 'grad_rec_w_a', 'grad_rec_b_a', 'grad_rec_w_x', 'grad_rec_b_x', 'grad_rec_lambda', 'grad_rec_w_out', 'grad_ln1_g', 'grad_ln1_b', 'grad_mlp_w1', 'grad_mlp_w2', 'grad_ln2_g', 'grad_ln2_b', 'delta_hyb_w_in', 'delta_hyb_sinks', 'delta_hyb_conv_w', 'delta_hyb_a_log', 'delta_hyb_dt_bias', 'delta_hyb_norm_w', 'delta_hyb_w_out', 'delta_rec_w_in', 'delta_rec_conv_w', 'delta_rec_conv_b', 'delta_rec_w_a', 'delta_rec_b_a', 'delta_rec_w_x', 'delta_rec_b_x', 'delta_rec_lambda', 'delta_rec_w_out', 'delta_ln1_g', 'delta_ln1_b', 'delta_mlp_w1', 'delta_mlp_w2', 'delta_ln2_g', 'delta_ln2_b', 'new_m_hyb_w_in', 'new_m_hyb_sinks', 'new_m_hyb_conv_w', 'new_m_hyb_a_log', 'new_m_hyb_dt_bias', 'new_m_hyb_norm_w', 'new_m_hyb_w_out', 'new_m_rec_w_in', 'new_m_rec_conv_w', 'new_m_rec_conv_b', 'new_m_rec_w_a', 'new_m_rec_b_a', 'new_m_rec_w_x', 'new_m_rec_b_x', 'new_m_rec_lambda', 'new_m_rec_w_out', 'new_m_ln1_g', 'new_m_ln1_b', 'new_m_mlp_w1', 'new_m_mlp_w2', 'new_m_ln2_g', 'new_m_ln2_b', 'new_v_hyb_w_in', 'new_v_hyb_sinks', 'new_v_hyb_conv_w', 'new_v_hyb_a_log', 'new_v_hyb_dt_bias', 'new_v_hyb_norm_w', 'new_v_hyb_w_out', 'new_v_rec_w_in', 'new_v_rec_conv_w', 'new_v_rec_conv_b', 'new_v_rec_w_a', 'new_v_rec_b_a', 'new_v_rec_w_x', 'new_v_rec_b_x', 'new_v_rec_lambda', 'new_v_rec_w_out', 'new_v_ln1_g', 'new_v_ln1_b', 'new_v_mlp_w1', 'new_v_mlp_w2', 'new_v_ln2_g', 'new_v_ln2_b']
TWIN_LEAF_KINDS = {'loss': 'loss', 'grad_x': 'grad_x', 'grad_hyb_w_in': 'grad_w', 'grad_hyb_sinks': 'grad_w', 'grad_hyb_conv_w': 'grad_w', 'grad_hyb_a_log': 'grad_w', 'grad_hyb_dt_bias': 'grad_w', 'grad_hyb_norm_w': 'grad_w', 'grad_hyb_w_out': 'grad_w', 'grad_rec_w_in': 'grad_w', 'grad_rec_conv_w': 'grad_w', 'grad_rec_conv_b': 'grad_w', 'grad_rec_w_a': 'grad_w', 'grad_rec_b_a': 'grad_w', 'grad_rec_w_x': 'grad_w', 'grad_rec_b_x': 'grad_w', 'grad_rec_lambda': 'grad_w', 'grad_rec_w_out': 'grad_w', 'grad_ln1_g': 'grad_w', 'grad_ln1_b': 'grad_w', 'grad_mlp_w1': 'grad_w', 'grad_mlp_w2': 'grad_w', 'grad_ln2_g': 'grad_w', 'grad_ln2_b': 'grad_w', 'delta_hyb_w_in': 'delta_w', 'delta_hyb_sinks': 'delta_w', 'delta_hyb_conv_w': 'delta_w', 'delta_hyb_a_log': 'delta_w', 'delta_hyb_dt_bias': 'delta_w', 'delta_hyb_norm_w': 'delta_w', 'delta_hyb_w_out': 'delta_w', 'delta_rec_w_in': 'delta_w', 'delta_rec_conv_w': 'delta_w', 'delta_rec_conv_b': 'delta_w', 'delta_rec_w_a': 'delta_w', 'delta_rec_b_a': 'delta_w', 'delta_rec_w_x': 'delta_w', 'delta_rec_b_x': 'delta_w', 'delta_rec_lambda': 'delta_w', 'delta_rec_w_out': 'delta_w', 'delta_ln1_g': 'delta_w', 'delta_ln1_b': 'delta_w', 'delta_mlp_w1': 'delta_w', 'delta_mlp_w2': 'delta_w', 'delta_ln2_g': 'delta_w', 'delta_ln2_b': 'delta_w', 'new_m_hyb_w_in': 'new_m', 'new_m_hyb_sinks': 'new_m', 'new_m_hyb_conv_w': 'new_m', 'new_m_hyb_a_log': 'new_m', 'new_m_hyb_dt_bias': 'new_m', 'new_m_hyb_norm_w': 'new_m', 'new_m_hyb_w_out': 'new_m', 'new_m_rec_w_in': 'new_m', 'new_m_rec_conv_w': 'new_m', 'new_m_rec_conv_b': 'new_m', 'new_m_rec_w_a': 'new_m', 'new_m_rec_b_a': 'new_m', 'new_m_rec_w_x': 'new_m', 'new_m_rec_b_x': 'new_m', 'new_m_rec_lambda': 'new_m', 'new_m_rec_w_out': 'new_m', 'new_m_ln1_g': 'new_m', 'new_m_ln1_b': 'new_m', 'new_m_mlp_w1': 'new_m', 'new_m_mlp_w2': 'new_m', 'new_m_ln2_g': 'new_m', 'new_m_ln2_b': 'new_m', 'new_v_hyb_w_in': 'new_v', 'new_v_hyb_sinks': 'new_v', 'new_v_hyb_conv_w': 'new_v', 'new_v_hyb_a_log': 'new_v', 'new_v_hyb_dt_bias': 'new_v', 'new_v_hyb_norm_w': 'new_v', 'new_v_hyb_w_out': 'new_v', 'new_v_rec_w_in': 'new_v', 'new_v_rec_conv_w': 'new_v', 'new_v_rec_conv_b': 'new_v', 'new_v_rec_w_a': 'new_v', 'new_v_rec_b_a': 'new_v', 'new_v_rec_w_x': 'new_v', 'new_v_rec_b_x': 'new_v', 'new_v_rec_lambda': 'new_v', 'new_v_rec_w_out': 'new_v', 'new_v_ln1_g': 'new_v', 'new_v_ln1_b': 'new_v', 'new_v_mlp_w1': 'new_v', 'new_v_mlp_w2': 'new_v', 'new_v_ln2_g': 'new_v', 'new_v_ln2_b': 'new_v'}


def _forward(args):
    return _fwd_reference(*[args[k] for k in FWD_PARAMS])


def _output_shape():
    out = _jax.eval_shape(lambda: _forward(_fwd_setup_inputs(0)))
    return out.shape, out.dtype

N_MICROBATCH = 1
ADAM_LR = 0.001
ADAM_B1 = 0.9
ADAM_B2 = 0.999
ADAM_EPS = 1e-08
ADAM_WD = 0.01
ADAM_STEP = 10
PER_EXAMPLE_BATCH_AXIS = {'x': 0, 'loss_target': 0}
SHARED_INPUTS = []
_WEIGHT_DTYPES = {'hyb_w_in': _jnp.float32, 'hyb_sinks': _jnp.float32, 'hyb_conv_w': _jnp.float32, 'hyb_a_log': _jnp.float32, 'hyb_dt_bias': _jnp.float32, 'hyb_norm_w': _jnp.float32, 'hyb_w_out': _jnp.float32, 'rec_w_in': _jnp.float32, 'rec_conv_w': _jnp.float32, 'rec_conv_b': _jnp.float32, 'rec_w_a': _jnp.float32, 'rec_b_a': _jnp.float32, 'rec_w_x': _jnp.float32, 'rec_b_x': _jnp.float32, 'rec_lambda': _jnp.float32, 'rec_w_out': _jnp.float32, 'ln1_g': _jnp.float32, 'ln1_b': _jnp.float32, 'mlp_w1': _jnp.float32, 'mlp_w2': _jnp.float32, 'ln2_g': _jnp.float32, 'ln2_b': _jnp.float32}
MOMENT_SCALE = {'hyb_w_in': 2.091272e-02, 'hyb_sinks': 9.885709e-03, 'hyb_conv_w': 2.138829e-02, 'hyb_a_log': 3.468093e-01, 'hyb_dt_bias': 3.055823e-01, 'hyb_norm_w': 6.060794e-02, 'hyb_w_out': 5.667734e-02, 'rec_w_in': 4.940852e-02, 'rec_conv_w': 6.218566e-02, 'rec_conv_b': 3.460245e-01, 'rec_w_a': 6.296118e-03, 'rec_b_a': 1.088695e-02, 'rec_w_x': 1.164327e-02, 'rec_b_x': 2.296870e-02, 'rec_lambda': 2.809956e-02, 'rec_w_out': 1.155303e-01, 'ln1_g': 7.730382e-01, 'ln1_b': 5.420525e-01, 'mlp_w1': 3.158853e-02, 'mlp_w2': 1.826274e-01, 'ln2_g': 1.613092e+01, 'ln2_b': 4.033288e+00}


def _to_microbatches(a, axis):
    t = _jnp.moveaxis(a, axis, 0)
    t = t.reshape((N_MICROBATCH, t.shape[0] // N_MICROBATCH) + t.shape[1:])
    return _jnp.moveaxis(t, 1, axis + 1)


def setup_inputs(seed: int = 0) -> dict:
    inp = _fwd_setup_inputs(seed)
    key = _jax.random.fold_in(_jax.random.key(seed), 7919)
    shape, _ = _output_shape()
    out = dict(inp)
    out["loss_target"] = _jax.random.normal(_jax.random.fold_in(key, 0), shape, _jnp.float32)
    for i, name in enumerate(TWIN_WEIGHTS):
        w = inp[name].astype(_jnp.float32)
        if MOMENT_SCALE is None:
            s = _jnp.sqrt(_jnp.mean(_jnp.square(w)) + 1e-30)
        else:
            s = MOMENT_SCALE[name]
        km, kv = _jax.random.split(_jax.random.fold_in(key, i + 1))
        out[name] = w
        out["m_" + name] = s * _jax.random.normal(km, w.shape, _jnp.float32)
        out["v_" + name] = (s * s) * _jax.random.uniform(kv, w.shape, _jnp.float32, 0.5, 1.5)
    if N_MICROBATCH > 1:
        for name, axis in PER_EXAMPLE_BATCH_AXIS.items():
            out[name] = _to_microbatches(out[name], axis)
    return {'x': out['x'], 'hyb_w_in': out['hyb_w_in'], 'hyb_sinks': out['hyb_sinks'], 'hyb_conv_w': out['hyb_conv_w'], 'hyb_a_log': out['hyb_a_log'], 'hyb_dt_bias': out['hyb_dt_bias'], 'hyb_norm_w': out['hyb_norm_w'], 'hyb_w_out': out['hyb_w_out'], 'rec_w_in': out['rec_w_in'], 'rec_conv_w': out['rec_conv_w'], 'rec_conv_b': out['rec_conv_b'], 'rec_w_a': out['rec_w_a'], 'rec_b_a': out['rec_b_a'], 'rec_w_x': out['rec_w_x'], 'rec_b_x': out['rec_b_x'], 'rec_lambda': out['rec_lambda'], 'rec_w_out': out['rec_w_out'], 'ln1_g': out['ln1_g'], 'ln1_b': out['ln1_b'], 'mlp_w1': out['mlp_w1'], 'mlp_w2': out['mlp_w2'], 'ln2_g': out['ln2_g'], 'ln2_b': out['ln2_b'], 'loss_target': out['loss_target'], 'm_hyb_w_in': out['m_hyb_w_in'], 'm_hyb_sinks': out['m_hyb_sinks'], 'm_hyb_conv_w': out['m_hyb_conv_w'], 'm_hyb_a_log': out['m_hyb_a_log'], 'm_hyb_dt_bias': out['m_hyb_dt_bias'], 'm_hyb_norm_w': out['m_hyb_norm_w'], 'm_hyb_w_out': out['m_hyb_w_out'], 'm_rec_w_in': out['m_rec_w_in'], 'm_rec_conv_w': out['m_rec_conv_w'], 'm_rec_conv_b': out['m_rec_conv_b'], 'm_rec_w_a': out['m_rec_w_a'], 'm_rec_b_a': out['m_rec_b_a'], 'm_rec_w_x': out['m_rec_w_x'], 'm_rec_b_x': out['m_rec_b_x'], 'm_rec_lambda': out['m_rec_lambda'], 'm_rec_w_out': out['m_rec_w_out'], 'm_ln1_g': out['m_ln1_g'], 'm_ln1_b': out['m_ln1_b'], 'm_mlp_w1': out['m_mlp_w1'], 'm_mlp_w2': out['m_mlp_w2'], 'm_ln2_g': out['m_ln2_g'], 'm_ln2_b': out['m_ln2_b'], 'v_hyb_w_in': out['v_hyb_w_in'], 'v_hyb_sinks': out['v_hyb_sinks'], 'v_hyb_conv_w': out['v_hyb_conv_w'], 'v_hyb_a_log': out['v_hyb_a_log'], 'v_hyb_dt_bias': out['v_hyb_dt_bias'], 'v_hyb_norm_w': out['v_hyb_norm_w'], 'v_hyb_w_out': out['v_hyb_w_out'], 'v_rec_w_in': out['v_rec_w_in'], 'v_rec_conv_w': out['v_rec_conv_w'], 'v_rec_conv_b': out['v_rec_conv_b'], 'v_rec_w_a': out['v_rec_w_a'], 'v_rec_b_a': out['v_rec_b_a'], 'v_rec_w_x': out['v_rec_w_x'], 'v_rec_b_x': out['v_rec_b_x'], 'v_rec_lambda': out['v_rec_lambda'], 'v_rec_w_out': out['v_rec_w_out'], 'v_ln1_g': out['v_ln1_g'], 'v_ln1_b': out['v_ln1_b'], 'v_mlp_w1': out['v_mlp_w1'], 'v_mlp_w2': out['v_mlp_w2'], 'v_ln2_g': out['v_ln2_g'], 'v_ln2_b': out['v_ln2_b']}


def _loss(weights, diff, rest, loss_target):
    with _jax.named_scope("forward"):
        args = {**rest, TWIN_DIFF_INPUT: diff, **{k: w.astype(_WEIGHT_DTYPES[k]) for k, w in weights.items()}}
        y = _forward(args)
    with _jax.named_scope("loss_head"):
        err = _jnp.square(y.astype(_jnp.float32) - loss_target)
        return 0.5 * _jnp.sum(_jnp.mean(err, axis=-1)) if err.ndim else 0.5 * err


def _adamw(w, g, m, v):
    m = ADAM_B1 * m + (1.0 - ADAM_B1) * g
    v = ADAM_B2 * v + (1.0 - ADAM_B2) * _jnp.square(g)
    m_hat = m / (1.0 - ADAM_B1 ** ADAM_STEP)
    v_hat = v / (1.0 - ADAM_B2 ** ADAM_STEP)
    delta = -ADAM_LR * (m_hat / (_jnp.sqrt(v_hat) + ADAM_EPS) + ADAM_WD * w)
    return delta, m, v


def reference(x, hyb_w_in, hyb_sinks, hyb_conv_w, hyb_a_log, hyb_dt_bias, hyb_norm_w, hyb_w_out, rec_w_in, rec_conv_w, rec_conv_b, rec_w_a, rec_b_a, rec_w_x, rec_b_x, rec_lambda, rec_w_out, ln1_g, ln1_b, mlp_w1, mlp_w2, ln2_g, ln2_b, loss_target, m_hyb_w_in, m_hyb_sinks, m_hyb_conv_w, m_hyb_a_log, m_hyb_dt_bias, m_hyb_norm_w, m_hyb_w_out, m_rec_w_in, m_rec_conv_w, m_rec_conv_b, m_rec_w_a, m_rec_b_a, m_rec_w_x, m_rec_b_x, m_rec_lambda, m_rec_w_out, m_ln1_g, m_ln1_b, m_mlp_w1, m_mlp_w2, m_ln2_g, m_ln2_b, v_hyb_w_in, v_hyb_sinks, v_hyb_conv_w, v_hyb_a_log, v_hyb_dt_bias, v_hyb_norm_w, v_hyb_w_out, v_rec_w_in, v_rec_conv_w, v_rec_conv_b, v_rec_w_a, v_rec_b_a, v_rec_w_x, v_rec_b_x, v_rec_lambda, v_rec_w_out, v_ln1_g, v_ln1_b, v_mlp_w1, v_mlp_w2, v_ln2_g, v_ln2_b):
    given = dict(x=x, hyb_w_in=hyb_w_in, hyb_sinks=hyb_sinks, hyb_conv_w=hyb_conv_w, hyb_a_log=hyb_a_log, hyb_dt_bias=hyb_dt_bias, hyb_norm_w=hyb_norm_w, hyb_w_out=hyb_w_out, rec_w_in=rec_w_in, rec_conv_w=rec_conv_w, rec_conv_b=rec_conv_b, rec_w_a=rec_w_a, rec_b_a=rec_b_a, rec_w_x=rec_w_x, rec_b_x=rec_b_x, rec_lambda=rec_lambda, rec_w_out=rec_w_out, ln1_g=ln1_g, ln1_b=ln1_b, mlp_w1=mlp_w1, mlp_w2=mlp_w2, ln2_g=ln2_g, ln2_b=ln2_b, loss_target=loss_target, m_hyb_w_in=m_hyb_w_in, m_hyb_sinks=m_hyb_sinks, m_hyb_conv_w=m_hyb_conv_w, m_hyb_a_log=m_hyb_a_log, m_hyb_dt_bias=m_hyb_dt_bias, m_hyb_norm_w=m_hyb_norm_w, m_hyb_w_out=m_hyb_w_out, m_rec_w_in=m_rec_w_in, m_rec_conv_w=m_rec_conv_w, m_rec_conv_b=m_rec_conv_b, m_rec_w_a=m_rec_w_a, m_rec_b_a=m_rec_b_a, m_rec_w_x=m_rec_w_x, m_rec_b_x=m_rec_b_x, m_rec_lambda=m_rec_lambda, m_rec_w_out=m_rec_w_out, m_ln1_g=m_ln1_g, m_ln1_b=m_ln1_b, m_mlp_w1=m_mlp_w1, m_mlp_w2=m_mlp_w2, m_ln2_g=m_ln2_g, m_ln2_b=m_ln2_b, v_hyb_w_in=v_hyb_w_in, v_hyb_sinks=v_hyb_sinks, v_hyb_conv_w=v_hyb_conv_w, v_hyb_a_log=v_hyb_a_log, v_hyb_dt_bias=v_hyb_dt_bias, v_hyb_norm_w=v_hyb_norm_w, v_hyb_w_out=v_hyb_w_out, v_rec_w_in=v_rec_w_in, v_rec_conv_w=v_rec_conv_w, v_rec_conv_b=v_rec_conv_b, v_rec_w_a=v_rec_w_a, v_rec_b_a=v_rec_b_a, v_rec_w_x=v_rec_w_x, v_rec_b_x=v_rec_b_x, v_rec_lambda=v_rec_lambda, v_rec_w_out=v_rec_w_out, v_ln1_g=v_ln1_g, v_ln1_b=v_ln1_b, v_mlp_w1=v_mlp_w1, v_mlp_w2=v_mlp_w2, v_ln2_g=v_ln2_g, v_ln2_b=v_ln2_b)
    weights = {n: given[n] for n in TWIN_WEIGHTS}
    shared = {n: given[n] for n in SHARED_INPUTS}
    per_example = {n: given[n] for n in ['x']}
    grad_fn = _jax.value_and_grad(_loss, argnums=(0, 1))

    def one_microbatch(ex, loss_target):
        ex = dict(ex)
        diff = ex.pop(TWIN_DIFF_INPUT)
        return grad_fn(weights, diff, {**shared, **ex}, loss_target)

    if N_MICROBATCH == 1:
        loss, (grad_w, grad_x) = one_microbatch(per_example, given["loss_target"])
    else:
        def body(carry, xs):
            loss_sum, grad_sum = carry
            l_k, (gw_k, gx_k) = one_microbatch(xs[0], xs[1])
            with _jax.named_scope("update"):
                return (loss_sum + l_k, _jax.tree.map(_jnp.add, grad_sum, gw_k)), gx_k

        init = (_jnp.zeros((), _jnp.float32), _jax.tree.map(_jnp.zeros_like, weights))
        (loss, grad_w), grad_x = _jax.lax.scan(body, init, (per_example, given["loss_target"]))
    with _jax.named_scope("update"):
        delta_w, new_m, new_v = {}, {}, {}
        for n in TWIN_WEIGHTS:
            delta_w[n], new_m[n], new_v[n] = _adamw(weights[n], grad_w[n], given["m_" + n], given["v_" + n])
    return (loss, grad_x, *[grad_w[n] for n in TWIN_WEIGHTS], *[delta_w[n] for n in TWIN_WEIGHTS],
            *[new_m[n] for n in TWIN_WEIGHTS], *[new_v[n] for n in TWIN_WEIGHTS])
```

```python
import functools
import math

import jax
import jax.numpy as jnp
from jax import lax
from jax.experimental import pallas as pl
from jax.experimental.pallas import tpu as pltpu

f32 = jnp.float32
bf16 = jnp.bfloat16

N_DEV = 8
D_MODEL = 1024
DEPTH = 4
A_HEAD_DIM = 64
A_Q_HEADS = 8
WINDOW = 128
ROPE_THETA = 10000.0
B_HEADS = 4
B_HEAD_DIM = 128
B_CHUNK = 64
LRU_BLOCKS = 4
LRU_C = 8.0
D_FF = 4 * D_MODEL
HYB_PROJ = 2824
HYB_PROJ_PAD = 3072
DN_ALPHA = (2 * DEPTH) ** 0.25
LN_EPS = 1e-5
NORM_EPS = 1e-6
ADAM_LR = 0.001
ADAM_B1 = 0.9
ADAM_B2 = 0.999
ADAM_EPS = 1e-08
ADAM_WD = 0.01
ADAM_STEP = 10

LANE = 128
SUBLANE = 8
VMEM_LIMIT = 48 * 1024 * 1024

CB_QA, CB_KA, CB_VA, CB_CONV, CB_Z, CB_LG = 0, 4, 5, 6, 18, 22

MESH_AXES = ("x", "y", "c")


def _cparams(*sem):
    return pltpu.CompilerParams(dimension_semantics=sem, vmem_limit_bytes=VMEM_LIMIT)


def _dot(a, b, dims, precision=None):
    return lax.dot_general(a, b, (dims, ((), ())), preferred_element_type=f32, precision=precision)


NN = ((1,), (0,))
NT = ((1,), (1,))
TN = ((0,), (0,))


def _mat_spec(arr, kind, lead, br, bc, rb, cb):
    if kind == "plain":
        return pl.BlockSpec((br, bc), lambda i, j, k: (rb(i, j, k), cb(i, j, k)))
    if kind == "lead":
        return pl.BlockSpec((None, br, bc), lambda i, j, k: (lead, rb(i, j, k), cb(i, j, k)))
    if kind == "devcol":
        assert bc == arr.shape[-1]
        return pl.BlockSpec((None, None, br, bc), lambda i, j, k: (cb(i, j, k), lead, rb(i, j, k), 0))
    assert kind == "devrow" and br == arr.shape[-2]
    return pl.BlockSpec((None, None, br, bc), lambda i, j, k: (rb(i, j, k), lead, 0, cb(i, j, k)))


def _mm(name, a, b, mode, *, b_kind="plain", b_lead=0, o_kind="plain", epilogue=None, extras=(), out_dtypes=(f32,),
        tm=512, tn=512, tk=1024):
    if b_kind in ("plain", "lead"):
        b_rows, b_cols = b.shape[-2:]
    elif b_kind == "devcol":
        b_rows, b_cols = b.shape[-2], N_DEV * b.shape[-1]
    else:
        b_rows, b_cols = N_DEV * b.shape[-2], b.shape[-1]
    if mode == "nn":
        (M, K), (K2, N) = a.shape, (b_rows, b_cols)
    elif mode == "nt":
        (M, K), (N, K2) = a.shape, (b_rows, b_cols)
    else:
        (K, M), (K2, N) = a.shape, (b_rows, b_cols)
    assert K == K2, (name, a.shape, b.shape, mode)
    tm, tn, tk = min(tm, M), min(tn, N), min(tk, K)
    cols_are_n = mode != "nt"
    if b_kind == "devcol":
        tn, tk = (b.shape[-1], tk) if cols_are_n else (tn, b.shape[-1])
    if b_kind == "devrow":
        tn, tk = (tn, b.shape[-2]) if cols_are_n else (b.shape[-2], tk)
    if o_kind == "devcol":
        tn = N // N_DEV
    assert M % tm == 0 and N % tn == 0 and K % tk == 0, (name, M, N, K, tm, tn, tk)
    nk = K // tk
    dims = {"nn": NN, "nt": NT, "tn": TN}[mode]
    n_ex, n_out = len(extras), len(out_dtypes)

    def body(*refs):
        a_ref, b_ref = refs[:2]
        ex = refs[2:2 + n_ex]
        outs = refs[2 + n_ex:2 + n_ex + n_out]
        acc = refs[-1]
        k = pl.program_id(2)

        @pl.when(k == 0)
        def _():
            acc[...] = jnp.zeros_like(acc)

        acc[...] += _dot(a_ref[...].astype(bf16), b_ref[...].astype(bf16), dims)

        @pl.when(k == nk - 1)
        def _():
            r = acc[...]
            res = epilogue(r, *[e[...] for e in ex]) if epilogue is not None else (r,)
            for o, v in zip(outs, res):
                o[...] = v.astype(o.dtype)

    if mode == "tn":
        a_spec = pl.BlockSpec((tk, tm), lambda i, j, k: (k, i))
    else:
        a_spec = pl.BlockSpec((tm, tk), lambda i, j, k: (i, k))
    jb, kb = (lambda i, j, k: j), (lambda i, j, k: k)
    if mode == "nt":
        b_spec = _mat_spec(b, b_kind, b_lead, tn, tk, jb, kb)
    else:
        b_spec = _mat_spec(b, b_kind, b_lead, tk, tn, kb, jb)
    e_spec = pl.BlockSpec((tm, tn), lambda i, j, k: (i, j))
    if o_kind == "plain":
        o_spec, o_shape = e_spec, (M, N)
    else:
        o_spec, o_shape = pl.BlockSpec((None, tm, tn), lambda i, j, k: (j, i, 0)), (N_DEV, M, tn)
    res = pl.pallas_call(
        body, name=name,
        grid=(M // tm, N // tn, nk),
        in_specs=[a_spec, b_spec] + [e_spec] * n_ex,
        out_specs=[o_spec] * n_out,
        out_shape=[jax.ShapeDtypeStruct(o_shape, dt) for dt in out_dtypes],
        scratch_shapes=[pltpu.VMEM((tm, tn), f32)],
        compiler_params=_cparams("parallel", "parallel", "arbitrary"),
    )(a, b, *extras)
    return res[0] if n_out == 1 else res


def _row_spec(tm, cb, width):
    assert (cb * LANE) % width == 0
    blk = (cb * LANE) // width
    return pl.BlockSpec((tm, width), lambda i: (i, blk))


def _whole_spec(p):
    nd = p.ndim
    return pl.BlockSpec(p.shape, lambda i: (0,) * nd)


def _tl_fwd(name, fn, rows, params, out_widths, out_dtypes, tm=256):
    T = rows[0][0].shape[0]
    tm = min(tm, T)
    nr, npar = len(rows), len(params)

    def body(*refs):
        vals = [r[...] for r in refs[:nr + npar]]
        outs = fn(*vals)
        for o, v in zip(refs[nr + npar:], outs):
            o[...] = v.astype(o.dtype)

    res = pl.pallas_call(
        body, name=name, grid=(T // tm,),
        in_specs=[_row_spec(tm, cb, w) for (_, cb, w) in rows] + [_whole_spec(p) for p in params],
        out_specs=[pl.BlockSpec((tm, w), lambda i: (i, 0)) for w in out_widths],
        out_shape=[jax.ShapeDtypeStruct((T, w), dt) for w, dt in zip(out_widths, out_dtypes)],
        compiler_params=_cparams("parallel"),
    )(*[r[0] for r in rows], *params)
    return res


def _tl_bwd(name, fn, rows, params, cot_rows, cot_fn=None, tm=256):
    T = rows[0][0].shape[0]
    tm = min(tm, T)
    nr, npar, nc = len(rows), len(params), len(cot_rows)

    def body(*refs):
        vals = [r[...] for r in refs[:nr + npar]]
        cots = [r[...] for r in refs[nr + npar:nr + npar + nc]]
        outs = refs[nr + npar + nc:]
        cot = tuple(cot_fn(*cots)) if cot_fn is not None else tuple(cots)
        _, vjp = jax.vjp(fn, *vals)
        grads = vjp(cot)
        for o, g in zip(outs[:nr], grads[:nr]):
            o[...] = g.astype(o.dtype)
        i = pl.program_id(0)
        for o, g in zip(outs[nr:], grads[nr:]):
            @pl.when(i == 0)
            def _(o=o):
                o[...] = jnp.zeros_like(o)
            o[...] += g

    res = pl.pallas_call(
        body, name=name, grid=(T // tm,),
        in_specs=[_row_spec(tm, cb, w) for (_, cb, w) in rows] + [_whole_spec(p) for p in params]
        + [_row_spec(tm, cb, w) for (_, cb, w) in cot_rows],
        out_specs=[pl.BlockSpec((tm, w), lambda i: (i, 0)) for (_, _, w) in rows] + [_whole_spec(p) for p in params],
        out_shape=[jax.ShapeDtypeStruct((T, w), f32) for (_, _, w) in rows]
        + [jax.ShapeDtypeStruct(p.shape, f32) for p in params],
        compiler_params=_cparams("arbitrary"),
    )(*[r[0] for r in rows], *params, *[r[0] for r in cot_rows])
    return res[:nr], res[nr:]


def _ln_res_fn(x, mix, g, b):
    pre = DN_ALPHA * x + mix
    mu = jnp.mean(pre, axis=-1, keepdims=True)
    var = jnp.mean(jnp.square(pre - mu), axis=-1, keepdims=True)
    return ((pre - mu) * lax.rsqrt(var + LN_EPS) * g + b,)


@jax.custom_jvp
def _expm1(x):
    small = jnp.abs(x) < 0.3
    xs = jnp.where(small, x, 0.0)
    poly = xs * (1.0 + xs * (1 / 2 + xs * (1 / 6 + xs * (1 / 24 + xs * (1 / 120 + xs * (
        1 / 720 + xs * (1 / 5040 + xs * (1 / 40320 + xs * (1 / 362880)))))))))
    return jnp.where(small, poly, jnp.exp(x) - 1.0)


@_expm1.defjvp
def _expm1_jvp(primals, tangents):
    (x,), (t,) = primals, tangents
    return _expm1(x), t * jnp.exp(x)


def _rglru_pre_fn(pre_r, pre_i, xc, b_a, b_x, lam):
    r = jax.nn.sigmoid(pre_r + b_a)
    i = jax.nn.sigmoid(pre_i + b_x)
    log_a = -LRU_C * r * jax.nn.softplus(-lam)
    a = jnp.exp(log_a)
    b = jnp.sqrt(-_expm1(2.0 * log_a)) * (i * xc)
    return a, b


def _rec_gate_fn(h, gate):
    return (h * jax.nn.gelu(gate),)


def _loss_head(y, t, tm=256):
    T, Dm = y.shape

    def body(y_ref, t_ref, dy_ref, loss_ref):
        e = y_ref[...] - t_ref[...]
        dy_ref[...] = e * (1.0 / Dm)

        @pl.when(pl.program_id(0) == 0)
        def _():
            loss_ref[...] = jnp.zeros_like(loss_ref)

        loss_ref[...] += 0.5 * jnp.sum(jnp.mean(e * e, axis=-1, keepdims=True), axis=0, keepdims=True)

    dy, loss = pl.pallas_call(
        body, name="loss_head", grid=(T // tm,),
        in_specs=[pl.BlockSpec((tm, Dm), lambda i: (i, 0))] * 2,
        out_specs=[pl.BlockSpec((tm, Dm), lambda i: (i, 0)), pl.BlockSpec((SUBLANE, LANE), lambda i: (0, 0))],
        out_shape=[jax.ShapeDtypeStruct((T, Dm), f32), jax.ShapeDtypeStruct((SUBLANE, LANE), f32)],
        compiler_params=_cparams("arbitrary"),
    )(y, t)
    return loss[0, 0], dy


def _conv_fwd(name, x, cb0, nblk, w, bias, tm=512):
    T = x.shape[0]
    tm = min(tm, T)
    hb = tm // SUBLANE
    has_b = bias is not None

    def body(*refs):
        cur, prev, w_ref = refs[:3]
        b_ref = refs[3] if has_b else None
        o = refs[-1]
        i = pl.program_id(1)
        p = jnp.where(i > 0, prev[...], 0.0)
        xcat = jnp.concatenate([p, cur[...]], axis=0)
        acc = cur[...] * w_ref[3:4, :]
        for j in range(3):
            acc = acc + pltpu.roll(xcat, 3 - j, axis=0)[SUBLANE:] * w_ref[j:j + 1, :]
        if has_b:
            acc = acc + b_ref[...]
        o[...] = acc

    in_specs = [
        pl.BlockSpec((tm, LANE), lambda c, i: (i, cb0 + c)),
        pl.BlockSpec((SUBLANE, LANE), lambda c, i: (jnp.maximum(i * hb - 1, 0), cb0 + c)),
        pl.BlockSpec((4, LANE), lambda c, i: (0, c)),
    ]
    args = [x, x, w]
    if has_b:
        in_specs.append(pl.BlockSpec((1, LANE), lambda c, i: (0, c)))
        args.append(bias)
    return pl.pallas_call(
        body, name=name, grid=(nblk, T // tm),
        in_specs=in_specs,
        out_specs=pl.BlockSpec((tm, LANE), lambda c, i: (i, c)),
        out_shape=jax.ShapeDtypeStruct((T, nblk * LANE), f32),
        compiler_params=_cparams("parallel", "parallel"),
    )(*args)


def _conv_bwd(name, dy, x, cb0, nblk, w, tm=512):
    T = x.shape[0]
    tm = min(tm, T)
    hb = tm // SUBLANE
    nt = T // tm

    def body(dcur, dnext, xcur, xprev, w_ref, dx_ref, dw_ref, db_ref):
        i = pl.program_id(1)
        d = dcur[...]
        dn = jnp.where(i < nt - 1, dnext[...], 0.0)
        dcat = jnp.concatenate([d, dn], axis=0)
        acc = d * w_ref[3:4, :]
        for j in range(3):
            s = 3 - j
            acc = acc + pltpu.roll(dcat, tm + SUBLANE - s, axis=0)[:tm] * w_ref[j:j + 1, :]
        dx_ref[...] = acc

        p = jnp.where(i > 0, xprev[...], 0.0)
        xcat = jnp.concatenate([p, xcur[...]], axis=0)
        rows = [jnp.sum(d * pltpu.roll(xcat, 3 - j, axis=0)[SUBLANE:], axis=0, keepdims=True) for j in range(3)]
        rows.append(jnp.sum(d * xcur[...], axis=0, keepdims=True))
        rows.append(jnp.zeros((SUBLANE - 4, LANE), f32))

        @pl.when(i == 0)
        def _():
            dw_ref[...] = jnp.zeros_like(dw_ref)
            db_ref[...] = jnp.zeros_like(db_ref)

        dw_ref[...] += jnp.concatenate(rows, axis=0)
        db_ref[...] += jnp.broadcast_to(jnp.sum(d, axis=0, keepdims=True), (SUBLANE, LANE))

    nh = T // SUBLANE
    dx, dw, db = pl.pallas_call(
        body, name=name, grid=(nblk, nt),
        in_specs=[
            pl.BlockSpec((tm, LANE), lambda c, i: (i, c)),
            pl.BlockSpec((SUBLANE, LANE), lambda c, i: (jnp.minimum((i + 1) * hb, nh - 1), c)),
            pl.BlockSpec((tm, LANE), lambda c, i: (i, cb0 + c)),
            pl.BlockSpec((SUBLANE, LANE), lambda c, i: (jnp.maximum(i * hb - 1, 0), cb0 + c)),
            pl.BlockSpec((4, LANE), lambda c, i: (0, c)),
        ],
        out_specs=[
            pl.BlockSpec((tm, LANE), lambda c, i: (i, c)),
            pl.BlockSpec((SUBLANE, LANE), lambda c, i: (0, c)),
            pl.BlockSpec((SUBLANE, LANE), lambda c, i: (0, c)),
        ],
        out_shape=[jax.ShapeDtypeStruct((T, nblk * LANE), f32),
                   jax.ShapeDtypeStruct((SUBLANE, nblk * LANE), f32),
                   jax.ShapeDtypeStruct((SUBLANE, nblk * LANE), f32)],
        compiler_params=_cparams("parallel", "arbitrary"),
    )(dy, dy, x, x, w)
    return dx, dw[:4], db[0]


@functools.partial(jax.custom_vjp, nondiff_argnums=(1,))
def _lroll(x, s):
    return pltpu.roll(x, s, axis=1)


def _lroll_fwd(x, s):
    return _lroll(x, s), None


def _lroll_bwd(s, _, g):
    return (_lroll(g, (LANE - s) % LANE),)


_lroll.defvjp(_lroll_fwd, _lroll_bwd)


def _rope_tables(T):
    half = A_HEAD_DIM // 2
    inv_freq = ROPE_THETA ** (-jnp.arange(half, dtype=f32) / half)
    ang = jnp.arange(T, dtype=f32)[:, None] * inv_freq[None, :]
    cos, sin = jnp.cos(ang), jnp.sin(ang)
    return jnp.tile(jnp.concatenate([cos, cos], axis=1), (1, 2)), jnp.tile(jnp.concatenate([-sin, sin], axis=1), (1, 2))


def _attn_block_fn(n, q, kp, kc, vp, vc, cq, sq, cp, sp, sinks):
    W = WINDOW
    lane = lax.broadcasted_iota(jnp.int32, (W, LANE), 1)
    lo_half = (lane % A_HEAD_DIM) < (A_HEAD_DIM // 2)
    lane8 = lax.broadcasted_iota(jnp.int32, sinks.shape, 1)

    def rope(x, c, s):
        return x * c + jnp.where(lo_half, _lroll(x, LANE - A_HEAD_DIM // 2), _lroll(x, A_HEAD_DIM // 2)) * s

    k2 = jnp.concatenate([rope(kp, cp, sp), rope(kc, cq, sq)], axis=0).astype(bf16)
    v2 = jnp.concatenate([vp, vc], axis=0).astype(bf16)
    row = lax.broadcasted_iota(jnp.int32, (W, 2 * W), 0)
    col = lax.broadcasted_iota(jnp.int32, (W, 2 * W), 1)
    dist = row + W - col
    mask = (dist >= 0) & (dist < W) & ((col >= W) | (n > 0))
    outs = []
    for t in range(4):
        qt = rope(q[:, LANE * t:LANE * (t + 1)], cq, sq)
        g = t // 2
        ot = jnp.zeros((W, LANE), f32)
        for hh in range(2):
            qa = jnp.where((lane // A_HEAD_DIM) == hh, qt, 0.0)
            if hh != g:
                qa = _lroll(qa, A_HEAD_DIM)
            s = _dot(qa.astype(bf16), k2, NT) * (A_HEAD_DIM ** -0.5)
            s = jnp.where(mask, s, -jnp.inf)
            sink = jnp.sum(jnp.where(lane8 == 2 * t + hh, sinks, 0.0), axis=1, keepdims=True)
            m = jnp.maximum(jnp.max(s, axis=-1, keepdims=True), sink)
            e = jnp.exp(s - m)
            p = e / (jnp.sum(e, axis=-1, keepdims=True) + jnp.exp(sink - m))
            o = _dot(p.astype(bf16), v2, NN)
            o = jnp.where((lane // A_HEAD_DIM) == g, o, 0.0)
            if hh != g:
                o = _lroll(o, A_HEAD_DIM)
            ot = ot + o
        outs.append(ot)
    return jnp.concatenate(outs, axis=1)


def _attn_specs():
    W = WINDOW
    prev = lambda n: jnp.maximum(n - 1, 0)
    return [
        pl.BlockSpec((W, 4 * LANE), lambda n: (n, CB_QA // 4)),
        pl.BlockSpec((W, LANE), lambda n: (prev(n), CB_KA)),
        pl.BlockSpec((W, LANE), lambda n: (n, CB_KA)),
        pl.BlockSpec((W, LANE), lambda n: (prev(n), CB_VA)),
        pl.BlockSpec((W, LANE), lambda n: (n, CB_VA)),
        pl.BlockSpec((W, LANE), lambda n: (n, 0)),
        pl.BlockSpec((W, LANE), lambda n: (n, 0)),
        pl.BlockSpec((W, LANE), lambda n: (prev(n), 0)),
        pl.BlockSpec((W, LANE), lambda n: (prev(n), 0)),
        pl.BlockSpec((1, A_Q_HEADS), lambda n: (0, 0)),
    ]


def _attn_fwd(name, proj, cos, sin, sinks):
    T = proj.shape[0]
    W = WINDOW

    def body(*refs):
        o = refs[-1]
        o[...] = _attn_block_fn(pl.program_id(0), *[r[...] for r in refs[:-1]])

    return pl.pallas_call(
        body, name=name, grid=(T // W,),
        in_specs=_attn_specs(),
        out_specs=pl.BlockSpec((W, 4 * LANE), lambda n: (n, 0)),
        out_shape=jax.ShapeDtypeStruct((T, 4 * LANE), f32),
        compiler_params=_cparams("parallel"),
    )(proj, proj, proj, proj, proj, cos, sin, cos, sin, sinks)


def _attn_bwd(name, proj, cos, sin, sinks, d_oab):
    T = proj.shape[0]
    W = WINDOW

    def body(*refs):
        ins = [r[...] for r in refs[:10]]
        do = refs[10][...]
        dq_ref, dk_ref, dv_ref, ds_ref = refs[11:]
        n = pl.program_id(0)
        _, vjp = jax.vjp(functools.partial(_attn_block_fn, n), *ins)
        dq, dkp, dkc, dvp, dvc, _, _, _, _, dsk = vjp(do)
        dq_ref[...] = dq

        @pl.when(n == 0)
        def _():
            dk_ref[...] = jnp.zeros_like(dk_ref)
            dv_ref[...] = jnp.zeros_like(dv_ref)
            ds_ref[...] = jnp.zeros_like(ds_ref)

        cur = pl.ds(pl.multiple_of(n * W, W), W)
        dk_ref[cur, :] += dkc
        dv_ref[cur, :] += dvc
        ds_ref[...] += dsk

        @pl.when(n > 0)
        def _():
            prv = pl.ds(pl.multiple_of((n - 1) * W, W), W)
            dk_ref[prv, :] += dkp
            dv_ref[prv, :] += dvp

    return pl.pallas_call(
        body, name=name, grid=(T // W,),
        in_specs=_attn_specs() + [pl.BlockSpec((W, 4 * LANE), lambda n: (n, 0))],
        out_specs=[pl.BlockSpec((W, 4 * LANE), lambda n: (n, 0)),
                   pl.BlockSpec((T, LANE), lambda n: (0, 0)),
                   pl.BlockSpec((T, LANE), lambda n: (0, 0)),
                   pl.BlockSpec((1, A_Q_HEADS), lambda n: (0, 0))],
        out_shape=[jax.ShapeDtypeStruct((T, 4 * LANE), f32), jax.ShapeDtypeStruct((T, LANE), f32),
                   jax.ShapeDtypeStruct((T, LANE), f32), jax.ShapeDtypeStruct((1, A_Q_HEADS), f32)],
        compiler_params=_cparams("arbitrary"),
    )(proj, proj, proj, proj, proj, cos, sin, cos, sin, sinks, d_oab)


@jax.custom_vjp
def _tri_inv(a):
    C = a.shape[0]
    r = lax.broadcasted_iota(jnp.int32, (C, C), 0)
    c = lax.broadcasted_iota(jnp.int32, (C, C), 1)
    t = jnp.where(r == c, 1.0, 0.0).astype(f32)
    for j in range(C - 1):
        t = t - a[:, j:j + 1] * t[j:j + 1, :]
    return t


def _tri_inv_fwd(a):
    t = _tri_inv(a)
    return t, t


def _tri_inv_bwd(t, g):
    C = t.shape[0]
    r = lax.broadcasted_iota(jnp.int32, (C, C), 0)
    c = lax.broadcasted_iota(jnp.int32, (C, C), 1)
    x = _dot(t, g, TN, precision=lax.Precision.HIGHEST)
    y = _dot(x, t, NT, precision=lax.Precision.HIGHEST)
    return (jnp.where(r > c, -y, 0.0),)


_tri_inv.defvjp(_tri_inv_fwd, _tri_inv_bwd)


def _silu(x):
    return x * jax.nn.sigmoid(x)


def _l2n(x):
    return x * lax.rsqrt(jnp.sum(x * x, axis=-1, keepdims=True) + NORM_EPS)


def _delta_chunk_fn(h, cq, ck, cv, z, lg, a_log, dt_bias, norm_w, S):
    C = B_CHUNK
    lane = lax.broadcasted_iota(jnp.int32, (C, LANE), 1)
    bl = jnp.sum(jnp.where(lane == h, lg, 0.0), axis=1, keepdims=True)
    al = jnp.sum(jnp.where(lane == B_HEADS + h, lg, 0.0), axis=1, keepdims=True)
    q = _l2n(_silu(cq)) * (B_HEAD_DIM ** -0.5)
    k = _l2n(_silu(ck))
    v = _silu(cv)
    beta = jax.nn.sigmoid(bl)
    g = -jnp.exp(a_log) * jax.nn.softplus(al + dt_bias)
    r = lax.broadcasted_iota(jnp.int32, (C, C), 0)
    c = lax.broadcasted_iota(jnp.int32, (C, C), 1)
    eye = r == c
    g_row = jnp.sum(jnp.where(eye, g, 0.0), axis=0, keepdims=True)
    gc = jnp.sum(jnp.where(c <= r, g_row, 0.0), axis=1, keepdims=True)
    gc_row = jnp.sum(jnp.where(eye, gc, 0.0), axis=0, keepdims=True)
    decay_incl = jnp.exp(jnp.where(r >= c, gc - gc_row, -jnp.inf))
    decay_strict = jnp.where(r > c, decay_incl, 0.0)
    kb = k * beta
    vb = v * beta
    kbf = k.astype(bf16)
    a_mat = _dot(kb.astype(bf16), kbf, NT) * decay_strict
    t_mat = _tri_inv(a_mat).astype(bf16)
    eg = jnp.exp(gc)
    u = _dot(t_mat, vb.astype(bf16), NN)
    w = _dot(t_mat, (kb * eg).astype(bf16), NN)
    qk = _dot(q.astype(bf16), kbf, NT) * decay_incl
    g_last = jnp.sum(g, axis=0, keepdims=True)
    k_tail = k * jnp.exp(g_last - gc)
    Sb = S.astype(bf16)
    v_new = u - _dot(w.astype(bf16), Sb, NN)
    o = _dot((q * eg).astype(bf16), Sb, NN) + _dot(qk.astype(bf16), v_new.astype(bf16), NN)
    S_new = S * jnp.exp(g_last) + _dot(k_tail.astype(bf16), v_new.astype(bf16), TN)
    ob = o * lax.rsqrt(jnp.mean(o * o, axis=-1, keepdims=True) + NORM_EPS) * norm_w
    return ob * _silu(z), S_new


def _delta_in_specs(rev, N):
    C = B_CHUNK
    ix = (lambda n: N - 1 - n) if rev else (lambda n: n)
    return [
        pl.BlockSpec((C, LANE), lambda h, n: (ix(n), h)),
        pl.BlockSpec((C, LANE), lambda h, n: (ix(n), B_HEADS + h)),
        pl.BlockSpec((C, LANE), lambda h, n: (ix(n), 2 * B_HEADS + h)),
        pl.BlockSpec((C, LANE), lambda h, n: (ix(n), CB_Z + h)),
        pl.BlockSpec((C, LANE), lambda h, n: (ix(n), CB_LG)),
        pl.BlockSpec((None, 1, 1), lambda h, n: (h, 0, 0)),
        pl.BlockSpec((None, 1, 1), lambda h, n: (h, 0, 0)),
        pl.BlockSpec((1, LANE), lambda h, n: (0, 0)),
    ]


def _delta_fwd(name, c, proj, a_log, dt_bias, norm_w):
    T = c.shape[0]
    C = B_CHUNK
    N = T // C
    Dh = B_HEAD_DIM

    def body(cq, ck, cv, z, lg, al, dt, nw, o_ref, s_ref, S):
        h, n = pl.program_id(0), pl.program_id(1)

        @pl.when(n == 0)
        def _():
            S[...] = jnp.zeros_like(S)

        s0 = S[...]
        s_ref[...] = s0
        ob, s1 = _delta_chunk_fn(h, cq[...], ck[...], cv[...], z[...], lg[...], al[...], dt[...], nw[...], s0)
        o_ref[...] = ob
        S[...] = s1

    return pl.pallas_call(
        body, name=name, grid=(B_HEADS, N),
        in_specs=_delta_in_specs(False, N),
        out_specs=[pl.BlockSpec((C, LANE), lambda h, n: (n, h)),
                   pl.BlockSpec((None, None, Dh, Dh), lambda h, n: (h, n, 0, 0))],
        out_shape=[jax.ShapeDtypeStruct((T, B_HEADS * Dh), f32), jax.ShapeDtypeStruct((B_HEADS, N, Dh, Dh), f32)],
        scratch_shapes=[pltpu.VMEM((Dh, Dh), f32)],
        compiler_params=_cparams("parallel", "arbitrary"),
    )(c, c, c, proj, proj, a_log, dt_bias, norm_w)


def _delta_bwd(name, c, proj, a_log, dt_bias, norm_w, s_saved, d_oab):
    T = c.shape[0]
    C = B_CHUNK
    N = T // C
    Dh = B_HEAD_DIM
    H = B_HEADS

    def body(cq, ck, cv, z, lg, al, dt, nw, s_ref, do_ref, dcq, dck, dcv, dz, dlg, dal, ddt, dnw, dS):
        h, n = pl.program_id(0), pl.program_id(1)

        @pl.when(n == 0)
        def _():
            dS[...] = jnp.zeros_like(dS)
            dal[...] = jnp.zeros_like(dal)
            ddt[...] = jnp.zeros_like(ddt)
            dnw[...] = jnp.zeros_like(dnw)

        _, vjp = jax.vjp(functools.partial(_delta_chunk_fn, h), cq[...], ck[...], cv[...], z[...], lg[...],
                         al[...], dt[...], nw[...], s_ref[...])
        g = vjp((do_ref[...], dS[...]))
        dcq[...] = g[0]
        dck[...] = g[1]
        dcv[...] = g[2]
        dz[...] = g[3]
        dlg[...] = g[4]
        dal[...] += g[5]
        ddt[...] += g[6]
        dnw[...] += g[7]
        dS[...] = g[8]

    rn = lambda n: N - 1 - n
    res = pl.pallas_call(
        body, name=name, grid=(H, N),
        in_specs=_delta_in_specs(True, N) + [
            pl.BlockSpec((None, None, Dh, Dh), lambda h, n: (h, rn(n), 0, 0)),
            pl.BlockSpec((C, LANE), lambda h, n: (rn(n), H + h)),
        ],
        out_specs=[
            pl.BlockSpec((C, LANE), lambda h, n: (rn(n), h)),
            pl.BlockSpec((C, LANE), lambda h, n: (rn(n), h)),
            pl.BlockSpec((C, LANE), lambda h, n: (rn(n), h)),
            pl.BlockSpec((C, LANE), lambda h, n: (rn(n), h)),
            pl.BlockSpec((None, C, LANE), lambda h, n: (h, rn(n), 0)),
            pl.BlockSpec((None, 1, 1), lambda h, n: (h, 0, 0)),
            pl.BlockSpec((None, 1, 1), lambda h, n: (h, 0, 0)),
            pl.BlockSpec((None, 1, LANE), lambda h, n: (h, 0, 0)),
        ],
        out_shape=[jax.ShapeDtypeStruct((T, H * Dh), f32)] * 4 + [
            jax.ShapeDtypeStruct((H, T, LANE), f32), jax.ShapeDtypeStruct((H, 1, 1), f32),
            jax.ShapeDtypeStruct((H, 1, 1), f32), jax.ShapeDtypeStruct((H, 1, LANE), f32)],
        scratch_shapes=[pltpu.VMEM((Dh, Dh), f32)],
        compiler_params=_cparams("parallel", "arbitrary"),
    )(c, c, c, proj, proj, a_log, dt_bias, norm_w, s_saved, d_oab)
    dcq, dck, dcv, dz, dlg, dal, ddt, dnw = res
    return jnp.concatenate([dcq, dck, dcv], axis=1), dz, dlg, dal, ddt, dnw


def _blockdiag_fwd(name, xc, w_a, w_x, tm=512):
    T, Wd = xc.shape
    bw = Wd // LRU_BLOCKS
    tm = min(tm, T)

    def body(x_ref, wa_ref, wx_ref, oa, ox):
        xb = x_ref[...].astype(bf16)
        oa[...] = _dot(xb, wa_ref[...].astype(bf16), NN)
        ox[...] = _dot(xb, wx_ref[...].astype(bf16), NN)

    xs = pl.BlockSpec((tm, bw), lambda i, h: (i, h))
    ws = pl.BlockSpec((None, bw, bw), lambda i, h: (h, 0, 0))
    return pl.pallas_call(
        body, name=name, grid=(T // tm, LRU_BLOCKS), in_specs=[xs, ws, ws], out_specs=[xs, xs],
        out_shape=[jax.ShapeDtypeStruct((T, Wd), f32)] * 2,
        compiler_params=_cparams("parallel", "parallel"),
    )(xc, w_a, w_x)


def _blockdiag_bwd_dx(name, dpr, dpi, w_a, w_x, addend, tm=512):
    T, Wd = dpr.shape
    bw = Wd // LRU_BLOCKS
    tm = min(tm, T)

    def body(dr, di, wa_ref, wx_ref, add, o):
        o[...] = (add[...] + _dot(dr[...].astype(bf16), wa_ref[...].astype(bf16), NT)
                  + _dot(di[...].astype(bf16), wx_ref[...].astype(bf16), NT))

    xs = pl.BlockSpec((tm, bw), lambda i, h: (i, h))
    ws = pl.BlockSpec((None, bw, bw), lambda i, h: (h, 0, 0))
    return pl.pallas_call(
        body, name=name, grid=(T // tm, LRU_BLOCKS), in_specs=[xs, xs, ws, ws, xs], out_specs=xs,
        out_shape=jax.ShapeDtypeStruct((T, Wd), f32),
        compiler_params=_cparams("parallel", "parallel"),
    )(dpr, dpi, w_a, w_x, addend)


def _blockdiag_bwd_dw(name, xc, dpr, dpi, tk=512):
    T, Wd = xc.shape
    bw = Wd // LRU_BLOCKS
    tk = min(tk, T)

    def body(x_ref, dr, di, oa, ox):
        @pl.when(pl.program_id(1) == 0)
        def _():
            oa[...] = jnp.zeros_like(oa)
            ox[...] = jnp.zeros_like(ox)

        xb = x_ref[...].astype(bf16)
        oa[...] += _dot(xb, dr[...].astype(bf16), TN)
        ox[...] += _dot(xb, di[...].astype(bf16), TN)

    xs = pl.BlockSpec((tk, bw), lambda h, k: (k, h))
    ws = pl.BlockSpec((None, bw, bw), lambda h, k: (h, 0, 0))
    return pl.pallas_call(
        body, name=name, grid=(LRU_BLOCKS, T // tk), in_specs=[xs, xs, xs], out_specs=[ws, ws],
        out_shape=[jax.ShapeDtypeStruct((LRU_BLOCKS, bw, bw), f32)] * 2,
        compiler_params=_cparams("parallel", "arbitrary"),
    )(xc, dpr, dpi)


def _scan(name, a, b, reverse, tt=512, cb=512):
    T, Wd = a.shape
    tt, cb = min(tt, T), min(cb, Wd)
    nt = T // tt
    ng = tt // SUBLANE

    def body(a_ref, b_ref, o_ref, carry):
        @pl.when(pl.program_id(1) == 0)
        def _():
            carry[...] = jnp.zeros_like(carry)

        row = lax.broadcasted_iota(jnp.int32, (SUBLANE, cb), 0)

        def step(gi, hp):
            g = (ng - 1 - gi) if reverse else gi
            off = pl.multiple_of(g * SUBLANE, SUBLANE)
            A = a_ref[pl.ds(off, SUBLANE), :]
            B = b_ref[pl.ds(off, SUBLANE), :]
            for s in (1, 2, 4):
                sh = (SUBLANE - s) if reverse else s
                As = pltpu.roll(A, sh, axis=0)
                Bs = pltpu.roll(B, sh, axis=0)
                valid = (row < SUBLANE - s) if reverse else (row >= s)
                B = jnp.where(valid, A * Bs + B, B)
                A = jnp.where(valid, A * As, A)
            hcur = A * hp + B
            o_ref[pl.ds(off, SUBLANE), :] = hcur
            edge = hcur[0:1, :] if reverse else hcur[SUBLANE - 1:SUBLANE, :]
            return jnp.broadcast_to(edge, (SUBLANE, cb))

        carry[...] = lax.fori_loop(0, ng, step, carry[...])

    ti = (lambda c, i: (nt - 1 - i, c)) if reverse else (lambda c, i: (i, c))
    spec = pl.BlockSpec((tt, cb), ti)
    return pl.pallas_call(
        body, name=name, grid=(Wd // cb, nt), in_specs=[spec, spec], out_specs=spec,
        out_shape=jax.ShapeDtypeStruct((T, Wd), f32),
        scratch_shapes=[pltpu.VMEM((SUBLANE, cb), f32)],
        compiler_params=_cparams("parallel", "arbitrary"),
    )(a, b)


def _relu2_epilogue(r):
    h = jnp.maximum(r, 0.0)
    return r, h * h


def _drelu2_epilogue(r, a):
    return (r * (2.0 * jnp.maximum(a, 0.0)),)


def _add_epilogue(r, e):
    return (r + e,)


def _merge_cols(name, g, tm=256):
    _, L, R, s = g.shape

    def body(g_ref, o_ref):
        for d in range(N_DEV):
            o_ref[:, s * d:s * (d + 1)] = g_ref[d].astype(bf16)
        o_ref[:, N_DEV * s:] = jnp.zeros((tm, HYB_PROJ_PAD - N_DEV * s), bf16)

    return pl.pallas_call(
        body, name=name, grid=(L, R // tm),
        in_specs=[pl.BlockSpec((N_DEV, None, tm, s), lambda l, i: (0, l, i, 0))],
        out_specs=pl.BlockSpec((None, tm, HYB_PROJ_PAD), lambda l, i: (l, i, 0)),
        out_shape=jax.ShapeDtypeStruct((L, R, HYB_PROJ_PAD), bf16),
        compiler_params=_cparams("parallel", "parallel"),
    )(g)


def _split_cols(name, dw, tm=256):
    R = dw.shape[0]
    s = HYB_PROJ // N_DEV

    def body(g_ref, o_ref):
        for d in range(N_DEV):
            o_ref[d] = g_ref[:, s * d:s * (d + 1)].astype(bf16)

    return pl.pallas_call(
        body, name=name, grid=(R // tm,),
        in_specs=[pl.BlockSpec((tm, HYB_PROJ_PAD), lambda i: (i, 0))],
        out_specs=pl.BlockSpec((N_DEV, tm, s), lambda i: (0, i, 0)),
        out_shape=jax.ShapeDtypeStruct((N_DEV, R, s), bf16),
        compiler_params=_cparams("parallel"),
    )(dw)


def _rows_to_dev(dw):
    nb, r, c = dw.shape
    t = dw.reshape(nb, N_DEV, r // N_DEV, c)
    return jnp.moveaxis(t, 1, 0).reshape(N_DEV, nb * (r // N_DEV), c).astype(bf16)


def _hybrid_fwd(tag, x, W, j, cos, sin):
    proj = _mm(f"{tag}_proj", x, W["hyb_w_in"], "nn", b_kind="lead", b_lead=j)
    o_a = _attn_fwd(f"{tag}_attn", proj, cos, sin, W["hyb_sinks"][j][None, :])
    c = _conv_fwd(f"{tag}_conv", proj, CB_CONV, 12, W["hyb_conv_w"][j], None)
    o_b, s_saved = _delta_fwd(f"{tag}_delta", c, proj, W["hyb_a_log"][j].reshape(B_HEADS, 1, 1),
                              W["hyb_dt_bias"][j].reshape(B_HEADS, 1, 1), W["hyb_norm_w"][j][None, :])
    o_ab = jnp.concatenate([o_a, o_b], axis=1)
    mix = _mm(f"{tag}_out", o_ab, W["hyb_w_out"], "nn", b_kind="lead", b_lead=j)
    return mix, (proj, c, s_saved, o_ab)


def _hybrid_bwd(tag, x, dmix, addend, W, j, cos, sin, saved, G):
    proj, c, s_saved, o_ab = saved
    T = x.shape[0]
    d_oab = _mm(f"{tag}_dout", dmix, W["hyb_w_out"], "nt", b_kind="lead", b_lead=j)
    G["hyb_w_out"][j] = _mm(f"{tag}_dwout", o_ab, dmix, "tn", out_dtypes=(bf16,)).reshape(N_DEV, -1, D_MODEL)
    dq, dk, dv, dsinks = _attn_bwd(f"{tag}_dattn", proj, cos, sin, W["hyb_sinks"][j][None, :], d_oab)
    a_log = W["hyb_a_log"][j].reshape(B_HEADS, 1, 1)
    dt_bias = W["hyb_dt_bias"][j].reshape(B_HEADS, 1, 1)
    dc, dz, dlg, dal, ddt, dnw = _delta_bwd(f"{tag}_ddelta", c, proj, a_log, dt_bias, W["hyb_norm_w"][j][None, :],
                                            s_saved, d_oab)
    dconv_in, dconv_w, _ = _conv_bwd(f"{tag}_dconv", dc, proj, CB_CONV, 12, W["hyb_conv_w"][j])
    dproj = jnp.concatenate([dq, dk, dv, dconv_in, dz, jnp.sum(dlg, axis=0),
                             jnp.zeros((T, HYB_PROJ_PAD - (CB_LG + 1) * LANE), f32)], axis=1)
    dx = _mm(f"{tag}_dx", dproj, W["hyb_w_in"], "nt", b_kind="lead", b_lead=j, epilogue=_add_epilogue,
             extras=(addend,))
    G["hyb_w_in"][j] = _split_cols(f"{tag}_dwin_split", _mm(f"{tag}_dwin", x, dproj, "tn"))
    G["hyb_sinks"][j] = dsinks[0]
    G["hyb_conv_w"][j] = dconv_w
    G["hyb_a_log"][j] = dal.reshape(B_HEADS)
    G["hyb_dt_bias"][j] = ddt.reshape(B_HEADS)
    G["hyb_norm_w"][j] = jnp.sum(dnw, axis=(0, 1))
    return dx


def _rec_fwd(tag, x, W, j):
    Wd = D_MODEL
    proj = _mm(f"{tag}_proj", x, W["rec_w_in"], "nn", b_kind="devcol", b_lead=j)
    xc = _conv_fwd(f"{tag}_conv", proj, 0, Wd // LANE, W["rec_conv_w"][j], W["rec_conv_b"][j][None, :])
    pre_r, pre_i = _blockdiag_fwd(f"{tag}_gates", xc, W["rec_w_a"][j], W["rec_w_x"][j])
    pars = [W["rec_b_a"][j][None, :], W["rec_b_x"][j][None, :], W["rec_lambda"][j][None, :]]
    a, b = _tl_fwd(f"{tag}_pre", _rglru_pre_fn, [(pre_r, 0, Wd), (pre_i, 0, Wd), (xc, 0, Wd)], pars, [Wd, Wd], [f32, f32])
    h = _scan(f"{tag}_scan", a, b, False)
    (hg,) = _tl_fwd(f"{tag}_gate", _rec_gate_fn, [(h, 0, Wd), (proj, Wd // LANE, Wd)], [], [Wd], [f32])
    mix = _mm(f"{tag}_out", hg, W["rec_w_out"], "nn", b_kind="lead", b_lead=j)
    return mix, (proj, xc, pre_r, pre_i, a, h, hg)


def _rec_bwd(tag, x, dmix, addend, W, j, saved, G):
    proj, xc, pre_r, pre_i, a, h, hg = saved
    Wd = D_MODEL
    dhg = _mm(f"{tag}_dout", dmix, W["rec_w_out"], "nt", b_kind="lead", b_lead=j)
    G["rec_w_out"][j] = _mm(f"{tag}_dwout", hg, dmix, "tn", out_dtypes=(bf16,)).reshape(N_DEV, -1, D_MODEL)
    (dh, dgate), _ = _tl_bwd(f"{tag}_dgate", _rec_gate_fn, [(h, 0, Wd), (proj, Wd // LANE, Wd)], [], [(dhg, 0, Wd)])
    a_next = jnp.concatenate([a[1:], jnp.zeros((1, Wd), f32)], axis=0)
    h_prev = jnp.concatenate([jnp.zeros((1, Wd), f32), h[:-1]], axis=0)
    lam_t = _scan(f"{tag}_dscan", a_next, dh, True)
    pars = [W["rec_b_a"][j][None, :], W["rec_b_x"][j][None, :], W["rec_lambda"][j][None, :]]
    (dpr, dpi, dxc1), (db_a, db_x, dlam) = _tl_bwd(
        f"{tag}_dpre", _rglru_pre_fn, [(pre_r, 0, Wd), (pre_i, 0, Wd), (xc, 0, Wd)], pars,
        [(lam_t, 0, Wd), (h_prev, 0, Wd)], cot_fn=lambda lt, hp: (lt * hp, lt))
    dxc = _blockdiag_bwd_dx(f"{tag}_dgates_dx", dpr, dpi, W["rec_w_a"][j], W["rec_w_x"][j], dxc1)
    dwa, dwx = _blockdiag_bwd_dw(f"{tag}_dgates_dw", xc, dpr, dpi)
    G["rec_w_a"][j], G["rec_w_x"][j] = _rows_to_dev(dwa), _rows_to_dev(dwx)
    dxr, dconv_w, dconv_b = _conv_bwd(f"{tag}_dconv", dxc, proj, 0, Wd // LANE, W["rec_conv_w"][j])
    dproj = jnp.concatenate([dxr, dgate], axis=1)
    dx = _mm(f"{tag}_dx", dproj, W["rec_w_in"], "nt", b_kind="devcol", b_lead=j, epilogue=_add_epilogue,
             extras=(addend,))
    G["rec_w_in"][j] = _mm(f"{tag}_dwin", x, dproj, "tn", o_kind="devcol", out_dtypes=(bf16,))
    G["rec_conv_w"][j] = dconv_w
    G["rec_conv_b"][j] = dconv_b
    G["rec_b_a"][j] = db_a[0]
    G["rec_b_x"][j] = db_x[0]
    G["rec_lambda"][j] = dlam[0]
    return dx


def _local_step(x, target, W):
    T = x.shape[0]
    cos, sin = _rope_tables(T)
    saved = []
    for layer in range(DEPTH):
        j = layer // 2
        tag = f"L{layer}"
        if layer % 2 == 0:
            mix, sv = _hybrid_fwd(tag, x, W, j, cos, sin)
        else:
            mix, sv = _rec_fwd(tag, x, W, j)
        ln1 = [W["ln1_g"][layer][None, :], W["ln1_b"][layer][None, :]]
        (x1,) = _tl_fwd(f"{tag}_ln1", _ln_res_fn, [(x, 0, D_MODEL), (mix, 0, D_MODEL)], ln1, [D_MODEL], [f32])
        a, h2 = _mm(f"{tag}_mlp1", x1, W["mlp_w1"], "nn", b_kind="devcol", b_lead=layer, epilogue=_relu2_epilogue,
                    out_dtypes=(f32, bf16))
        y = _mm(f"{tag}_mlp2", h2, W["mlp_w2"], "nn", b_kind="devrow", b_lead=layer)
        ln2 = [W["ln2_g"][layer][None, :], W["ln2_b"][layer][None, :]]
        (x2,) = _tl_fwd(f"{tag}_ln2", _ln_res_fn, [(x1, 0, D_MODEL), (y, 0, D_MODEL)], ln2, [D_MODEL], [f32])
        saved.append((x, sv, mix, x1, a, h2, y))
        x = x2
    loss, dx = _loss_head(x, target)

    G = {k: [None] * (DEPTH if k.startswith(("ln", "mlp")) else DEPTH // 2) for k in (
        "hyb_w_in", "hyb_sinks", "hyb_conv_w", "hyb_a_log", "hyb_dt_bias", "hyb_norm_w", "hyb_w_out",
        "rec_w_in", "rec_conv_w", "rec_conv_b", "rec_w_a", "rec_b_a", "rec_w_x", "rec_b_x", "rec_lambda", "rec_w_out",
        "ln1_g", "ln1_b", "mlp_w1", "mlp_w2", "ln2_g", "ln2_b")}
    for layer in reversed(range(DEPTH)):
        j = layer // 2
        tag = f"L{layer}"
        x0, sv, mix, x1, a, h2, y = saved[layer]
        ln2 = [W["ln2_g"][layer][None, :], W["ln2_b"][layer][None, :]]
        (dx1_a, dy), (dg2, db2) = _tl_bwd(f"{tag}_dln2", _ln_res_fn, [(x1, 0, D_MODEL), (y, 0, D_MODEL)], ln2,
                                          [(dx, 0, D_MODEL)])
        G["ln2_g"][layer], G["ln2_b"][layer] = dg2[0], db2[0]
        da = _mm(f"{tag}_dmlp2", dy, W["mlp_w2"], "nt", b_kind="devrow", b_lead=layer, epilogue=_drelu2_epilogue,
                 extras=(a,), out_dtypes=(bf16,))
        G["mlp_w2"][layer] = _mm(f"{tag}_dw2", h2, dy, "tn", out_dtypes=(bf16,)).reshape(N_DEV, -1, D_MODEL)
        dx1 = _mm(f"{tag}_dmlp1", da, W["mlp_w1"], "nt", b_kind="devcol", b_lead=layer, epilogue=_add_epilogue,
                  extras=(dx1_a,))
        G["mlp_w1"][layer] = _mm(f"{tag}_dw1", x1, da, "tn", o_kind="devcol", out_dtypes=(bf16,))
        ln1 = [W["ln1_g"][layer][None, :], W["ln1_b"][layer][None, :]]
        (dx0_a, dmix), (dg1, db1) = _tl_bwd(f"{tag}_dln1", _ln_res_fn, [(x0, 0, D_MODEL), (mix, 0, D_MODEL)], ln1,
                                            [(dx1, 0, D_MODEL)])
        G["ln1_g"][layer], G["ln1_b"][layer] = dg1[0], db1[0]
        if layer % 2 == 0:
            dx = _hybrid_bwd(tag, x0, dmix, dx0_a, W, j, cos, sin, sv, G)
        else:
            dx = _rec_bwd(tag, x0, dmix, dx0_a, W, j, sv, G)
    big = {k for k, _ in BIG}
    return loss, dx, {k: (v if k in big else jnp.stack(v)) for k, v in G.items()}


def _my_coords():
    return lax.axis_index("x"), lax.axis_index("y"), lax.axis_index("c")


def _all_gather(name, arrays):
    na = len(arrays)

    def body(*refs):
        x_refs, out_refs = refs[:na], refs[na:2 * na]
        send_sems, recv_sems, local_sems = refs[2 * na:]
        x, y, c = _my_coords()
        me, sibling = (x, y, c), (x, y, 1 - c)
        chips = [(1 - x, y), (x, 1 - y), (1 - x, 1 - y)]

        def blk(a, px, py, pc):
            return out_refs[a].at[4 * px + 2 * py + pc]

        def copy(a, k, block, to, src=None):
            return pltpu.make_async_remote_copy(
                src_ref=blk(a, *block) if src is None else src, dst_ref=blk(a, *block),
                send_sem=send_sems.at[a, k], recv_sem=recv_sems.at[a, k],
                device_id=to, device_id_type=pl.DeviceIdType.MESH)

        mine = [pltpu.make_async_copy(x_refs[a], blk(a, *me), local_sems.at[a]) for a in range(na)]
        for cp in mine:
            cp.start()
        first = []
        for a in range(na):
            first.append(copy(a, 0, me, sibling, src=x_refs[a]))
            first += [copy(a, 1 + j, me, (*chip, c), src=x_refs[a]) for j, chip in enumerate(chips)]
        for cp in first:
            cp.start()
        passed = []
        for a in range(na):
            for j, chip in enumerate(chips):
                copy(a, 1 + j, (*chip, c), me).wait_recv()
                passed.append(copy(a, 4 + j, (*chip, c), sibling))
                passed[-1].start()
        for a in range(na):
            copy(a, 0, sibling, me).wait_recv()
            for j, chip in enumerate(chips):
                copy(a, 4 + j, (*chip, 1 - c), me).wait_recv()
        for cp in first + passed:
            cp.wait_send()
        for cp in mine:
            cp.wait()

    return pl.pallas_call(
        body, name=name,
        out_shape=[jax.ShapeDtypeStruct((N_DEV,) + a.shape, a.dtype) for a in arrays],
        in_specs=[pl.BlockSpec(memory_space=pl.ANY)] * na,
        out_specs=[pl.BlockSpec(memory_space=pl.ANY)] * na,
        scratch_shapes=[pltpu.SemaphoreType.DMA((na, 7)), pltpu.SemaphoreType.DMA((na, 7)),
                        pltpu.SemaphoreType.DMA((na,))],
    )(*arrays)


def _all_to_all(name, groups):
    flat = [(p, l, arr) for p, layers in enumerate(groups) for l, arr in enumerate(layers)]
    na = len(flat)

    def body(*refs):
        g_refs, out_refs = refs[:na], refs[na:na + len(groups)]
        send_sems, recv_sems, local_sems = refs[na + len(groups):]
        x, y, c = _my_coords()
        me = 4 * x + 2 * y + c
        mine, copies = [], []
        for a, (p, l, _) in enumerate(flat):
            mine.append(pltpu.make_async_copy(g_refs[a].at[me], out_refs[p].at[me, l], local_sems.at[a]))
            for k in range(1, N_DEV):
                px = (1 - x) if (k >> 2) & 1 else x
                py = (1 - y) if (k >> 1) & 1 else y
                pc = (1 - c) if k & 1 else c
                copies.append(pltpu.make_async_remote_copy(
                    src_ref=g_refs[a].at[4 * px + 2 * py + pc], dst_ref=out_refs[p].at[me, l],
                    send_sem=send_sems.at[a, k - 1], recv_sem=recv_sems.at[a, k - 1],
                    device_id=(px, py, pc), device_id_type=pl.DeviceIdType.MESH))
        for cp in mine + copies:
            cp.start()
        for cp in copies:
            cp.wait_recv()
        for cp in copies:
            cp.wait_send()
        for cp in mine:
            cp.wait()

    return pl.pallas_call(
        body, name=name,
        out_shape=[jax.ShapeDtypeStruct((N_DEV, len(layers)) + layers[0].shape[1:], layers[0].dtype)
                   for layers in groups],
        in_specs=[pl.BlockSpec(memory_space=pl.ANY)] * na,
        out_specs=[pl.BlockSpec(memory_space=pl.ANY)] * len(groups),
        scratch_shapes=[pltpu.SemaphoreType.DMA((na, 7)), pltpu.SemaphoreType.DMA((na, 7)),
                        pltpu.SemaphoreType.DMA((na,))],
    )(*[arr for _, _, arr in flat])


def _sum_blocks(name, land):
    _, R, n = land.shape
    tr = R

    def body(l_ref, o_ref):
        acc = l_ref[0].astype(f32)
        for s in range(1, N_DEV):
            acc = acc + l_ref[s].astype(f32)
        o_ref[...] = acc

    return pl.pallas_call(
        body, name=name, grid=(R // tr,),
        in_specs=[pl.BlockSpec((N_DEV, tr, n), lambda i: (0, i, 0))],
        out_specs=pl.BlockSpec((tr, n), lambda i: (i, 0)),
        out_shape=jax.ShapeDtypeStruct((R, n), f32),
        compiler_params=_cparams("parallel"),
    )(land)


def _adamw(name, w, g, m, v):
    shape = w.shape
    last = shape[-1]
    rows = math.prod(shape[:-1])
    tm = 256 if rows % 256 == 0 and rows > 256 else rows
    w2, g2, m2, v2 = (t.reshape(rows, last) for t in (w, g, m, v))

    def body(w_ref, g_ref, m_ref, v_ref, d_ref, mo_ref, vo_ref):
        gg = g_ref[...]
        mn = ADAM_B1 * m_ref[...] + (1.0 - ADAM_B1) * gg
        vn = ADAM_B2 * v_ref[...] + (1.0 - ADAM_B2) * jnp.square(gg)
        m_hat = mn / (1.0 - ADAM_B1 ** ADAM_STEP)
        v_hat = vn / (1.0 - ADAM_B2 ** ADAM_STEP)
        d_ref[...] = -ADAM_LR * (m_hat / (jnp.sqrt(v_hat) + ADAM_EPS) + ADAM_WD * w_ref[...])
        mo_ref[...] = mn
        vo_ref[...] = vn

    spec = pl.BlockSpec((tm, last), lambda i: (i, 0))
    d, mn, vn = pl.pallas_call(
        body, name=name, grid=(rows // tm,), in_specs=[spec] * 4, out_specs=[spec] * 3,
        out_shape=[jax.ShapeDtypeStruct((rows, last), f32)] * 3,
        compiler_params=_cparams("parallel"),
    )(w2, g2, m2, v2)
    return d.reshape(shape), mn.reshape(shape), vn.reshape(shape)


def _adamw_land(name, land, w, m, v, tm=256):
    _, L, R, C = land.shape
    tm = min(tm, R)

    def body(l_ref, w_ref, m_ref, v_ref, g_ref, d_ref, mo_ref, vo_ref):
        gg = l_ref[0].astype(f32)
        for s in range(1, N_DEV):
            gg = gg + l_ref[s].astype(f32)
        g_ref[...] = gg
        mn = ADAM_B1 * m_ref[...] + (1.0 - ADAM_B1) * gg
        vn = ADAM_B2 * v_ref[...] + (1.0 - ADAM_B2) * jnp.square(gg)
        m_hat = mn / (1.0 - ADAM_B1 ** ADAM_STEP)
        v_hat = vn / (1.0 - ADAM_B2 ** ADAM_STEP)
        d_ref[...] = -ADAM_LR * (m_hat / (jnp.sqrt(v_hat) + ADAM_EPS) + ADAM_WD * w_ref[...])
        mo_ref[...] = mn
        vo_ref[...] = vn

    spec = pl.BlockSpec((None, tm, C), lambda l, i: (l, i, 0))
    return pl.pallas_call(
        body, name=name, grid=(L, R // tm),
        in_specs=[pl.BlockSpec((N_DEV, None, tm, C), lambda l, i: (0, l, i, 0))] + [spec] * 3,
        out_specs=[spec] * 4,
        out_shape=[jax.ShapeDtypeStruct((L, R, C), f32)] * 4,
        compiler_params=_cparams("parallel", "parallel"),
    )(land, w, m, v)


BIG = [("hyb_w_in", 2), ("hyb_w_out", 1), ("rec_w_in", 2), ("rec_w_out", 1), ("rec_w_a", 2), ("rec_w_x", 2),
       ("mlp_w1", 2), ("mlp_w2", 1)]
SMALL = [("hyb_conv_w", 2), ("rec_conv_w", 2), ("rec_conv_b", 1), ("rec_b_a", 1), ("rec_b_x", 1), ("rec_lambda", 1)]
REPL = ["hyb_sinks", "hyb_a_log", "hyb_dt_bias", "hyb_norm_w", "ln1_g", "ln1_b", "ln2_g", "ln2_b"]
WEIGHTS = ["hyb_w_in", "hyb_sinks", "hyb_conv_w", "hyb_a_log", "hyb_dt_bias", "hyb_norm_w", "hyb_w_out", "rec_w_in",
           "rec_conv_w", "rec_conv_b", "rec_w_a", "rec_b_a", "rec_w_x", "rec_b_x", "rec_lambda", "rec_w_out",
           "ln1_g", "ln1_b", "mlp_w1", "mlp_w2", "ln2_g", "ln2_b"]


def _pack_rows(parts, dtype, row_mult):
    lead = parts[0].shape[:-1]
    flat = jnp.concatenate([p.astype(dtype) for p in parts], axis=-1)
    n = flat.shape[-1]
    unit = row_mult * LANE
    pad = (-n) % unit
    if pad:
        flat = jnp.concatenate([flat, jnp.zeros(lead + (pad,), dtype)], axis=-1)
    return flat.reshape(lead + ((n + pad) // LANE, LANE))


def _gather_full(gathered, shard_shapes, table):
    flat = gathered.reshape(N_DEV, -1)
    out, off = {}, 0
    for name, ax in table:
        shp = shard_shapes[name]
        n = math.prod(shp)
        arr = flat[:, off:off + n].reshape((N_DEV,) + shp)
        off += n
        arr = jnp.moveaxis(arr, 0, ax)
        out[name] = arr.reshape(shp[:ax] + (N_DEV * shp[ax],) + shp[ax + 1:])
    return out


def _matmul_layouts(gw):
    out = {"hyb_w_in": _merge_cols("hyb_w_in_merge", gw["hyb_w_in"])}
    for k in ("hyb_w_out", "rec_w_out"):
        out[k] = jnp.swapaxes(gw[k], 0, 1).reshape(DEPTH // 2, D_MODEL, D_MODEL)
    bw = D_MODEL // LRU_BLOCKS
    for k in ("rec_w_a", "rec_w_x"):
        out[k] = jnp.moveaxis(gw[k], 0, 2).reshape(DEPTH // 2, LRU_BLOCKS, bw, bw)
    for k in ("rec_w_in", "mlp_w1", "mlp_w2"):
        out[k] = gw[k]
    return out


def kernel(x, hyb_w_in, hyb_sinks, hyb_conv_w, hyb_a_log, hyb_dt_bias, hyb_norm_w, hyb_w_out, rec_w_in, rec_conv_w, rec_conv_b, rec_w_a, rec_b_a, rec_w_x, rec_b_x, rec_lambda, rec_w_out, ln1_g, ln1_b, mlp_w1, mlp_w2, ln2_g, ln2_b, loss_target, m_hyb_w_in, m_hyb_sinks, m_hyb_conv_w, m_hyb_a_log, m_hyb_dt_bias, m_hyb_norm_w, m_hyb_w_out, m_rec_w_in, m_rec_conv_w, m_rec_conv_b, m_rec_w_a, m_rec_b_a, m_rec_w_x, m_rec_b_x, m_rec_lambda, m_rec_w_out, m_ln1_g, m_ln1_b, m_mlp_w1, m_mlp_w2, m_ln2_g, m_ln2_b, v_hyb_w_in, v_hyb_sinks, v_hyb_conv_w, v_hyb_a_log, v_hyb_dt_bias, v_hyb_norm_w, v_hyb_w_out, v_rec_w_in, v_rec_conv_w, v_rec_conv_b, v_rec_w_a, v_rec_b_a, v_rec_w_x, v_rec_b_x, v_rec_lambda, v_rec_w_out, v_ln1_g, v_ln1_b, v_mlp_w1, v_mlp_w2, v_ln2_g, v_ln2_b):
    args = locals()
    w = {k: args[k] for k in WEIGHTS}
    m = {k: args["m_" + k] for k in WEIGHTS}
    v = {k: args["v_" + k] for k in WEIGHTS}
    shard_shapes = {k: tuple(t.shape) for k, t in w.items()}
    xi, yi, ci = _my_coords()
    me = 4 * xi + 2 * yi + ci

    gathered = _all_gather("gather_w", [w[k].astype(bf16) for k, _ in BIG]
                           + [_pack_rows([w[k].reshape(-1) for k, _ in SMALL], f32, SUBLANE)])
    W = _gather_full(gathered[-1], shard_shapes, SMALL)
    W.update({k: w[k] for k in REPL})
    W.update(_matmul_layouts(dict(zip([k for k, _ in BIG], gathered[:-1]))))

    loss_local, grad_x, G = _local_step(x[0], loss_target[0], W)
    loss = lax.psum(loss_local, MESH_AXES)

    lands = _all_to_all("a2a_grads", [G[k] for k, _ in BIG])
    rest = _pack_rows([G[k].reshape(-1) for k, _ in SMALL] + [G[k].reshape(-1) for k in REPL], f32, SUBLANE)
    g_rest = _sum_blocks("sum_rest", _all_gather("gather_rest", [rest])[0]).reshape(-1)

    grads, delta, new_m, new_v = {}, {}, {}, {}
    for (k, _), land in zip(BIG, lands):
        shp = shard_shapes[k]
        s3 = (shp[0], math.prod(shp[1:-1]), shp[-1])
        res = _adamw_land("adamw_" + k, land.reshape((N_DEV,) + s3), w[k].reshape(s3), m[k].reshape(s3),
                          v[k].reshape(s3))
        grads[k], delta[k], new_m[k], new_v[k] = (r.reshape(shp) for r in res)
    off = 0
    for k, ax in SMALL:
        full_shape = G[k].shape
        n = math.prod(full_shape)
        full = g_rest[off:off + n].reshape(full_shape)
        off += n
        s = shard_shapes[k][ax]
        grads[k] = lax.dynamic_slice_in_dim(full, me * s, s, axis=ax)
    for k in REPL:
        n = math.prod(shard_shapes[k])
        grads[k] = g_rest[off:off + n].reshape(shard_shapes[k])
        off += n

    for k in [k for k, _ in SMALL] + REPL:
        delta[k], new_m[k], new_v[k] = _adamw("adamw_" + k, w[k], grads[k], m[k], v[k])

    return (loss, grad_x[None], *[grads[k] for k in WEIGHTS], *[delta[k] for k in WEIGHTS],
            *[new_m[k] for k in WEIGHTS], *[new_v[k] for k in WEIGHTS])
```

```python
import functools
import math

import jax
import jax.numpy as jnp
from jax import lax
from jax.experimental import pallas as pl
from jax.experimental.pallas import tpu as pltpu

f32 = jnp.float32
bf16 = jnp.bfloat16

N_DEV = 8
D_MODEL = 1024
DEPTH = 4
A_HEAD_DIM = 64
A_Q_HEADS = 8
WINDOW = 128
ROPE_THETA = 10000.0
B_HEADS = 4
B_HEAD_DIM = 128
B_CHUNK = 64
LRU_BLOCKS = 4
LRU_C = 8.0
D_FF = 4 * D_MODEL
HYB_PROJ = 2824
HYB_PROJ_PAD = 3072
DN_ALPHA = (2 * DEPTH) ** 0.25
LN_EPS = 1e-5
NORM_EPS = 1e-6
ADAM_LR = 0.001
ADAM_B1 = 0.9
ADAM_B2 = 0.999
ADAM_EPS = 1e-08
ADAM_WD = 0.01
ADAM_STEP = 10

LANE = 128
SUBLANE = 8
VMEM_LIMIT = 48 * 1024 * 1024

CB_QA, CB_KA, CB_VA, CB_CONV, CB_Z, CB_LG = 0, 4, 5, 6, 18, 22

MESH_AXES = ("x", "y", "c")


def _cparams(*sem):
    return pltpu.CompilerParams(dimension_semantics=sem, vmem_limit_bytes=VMEM_LIMIT)


def _dot(a, b, dims, precision=None):
    return lax.dot_general(a, b, (dims, ((), ())), preferred_element_type=f32, precision=precision)


NN = ((1,), (0,))
NT = ((1,), (1,))
TN = ((0,), (0,))


def _mat_spec(arr, kind, lead, br, bc, rb, cb):
    if kind == "plain":
        return pl.BlockSpec((br, bc), lambda i, j, k: (rb(i, j, k), cb(i, j, k)))
    if kind == "lead":
        return pl.BlockSpec((None, br, bc), lambda i, j, k: (lead, rb(i, j, k), cb(i, j, k)))
    if kind == "devcol":
        assert bc == arr.shape[-1]
        return pl.BlockSpec((None, None, br, bc), lambda i, j, k: (cb(i, j, k), lead, rb(i, j, k), 0))
    assert kind == "devrow" and br == arr.shape[-2]
    return pl.BlockSpec((None, None, br, bc), lambda i, j, k: (rb(i, j, k), lead, 0, cb(i, j, k)))


def _mm(name, a, b, mode, *, b_kind="plain", b_lead=0, o_kind="plain", epilogue=None, extras=(), out_dtypes=(f32,),
        tm=1024, tn=1024, tk=None):
    if tk is None:
        tk = 512 if mode == "tn" else 1024
    if b_kind in ("plain", "lead"):
        b_rows, b_cols = b.shape[-2:]
    elif b_kind == "devcol":
        b_rows, b_cols = b.shape[-2], N_DEV * b.shape[-1]
    else:
        b_rows, b_cols = N_DEV * b.shape[-2], b.shape[-1]
    if mode == "nn":
        (M, K), (K2, N) = a.shape, (b_rows, b_cols)
    elif mode == "nt":
        (M, K), (N, K2) = a.shape, (b_rows, b_cols)
    else:
        (K, M), (K2, N) = a.shape, (b_rows, b_cols)
    assert K == K2, (name, a.shape, b.shape, mode)
    tm, tn, tk = min(tm, M), min(tn, N), min(tk, K)
    cols_are_n = mode != "nt"
    if b_kind == "devcol":
        tn, tk = (b.shape[-1], tk) if cols_are_n else (tn, b.shape[-1])
    if b_kind == "devrow":
        tn, tk = (tn, b.shape[-2]) if cols_are_n else (b.shape[-2], tk)
    shard = N // N_DEV
    if o_kind == "devcol":
        tn = max(shard, tn // shard * shard)
    assert M % tm == 0 and N % tn == 0 and K % tk == 0, (name, M, N, K, tm, tn, tk)
    nk = K // tk
    dims = {"nn": NN, "nt": NT, "tn": TN}[mode]
    n_ex, n_out = len(extras), len(out_dtypes)

    def body(*refs):
        a_ref, b_ref = refs[:2]
        ex = refs[2:2 + n_ex]
        outs = refs[2 + n_ex:2 + n_ex + n_out]
        acc = refs[-1]
        k = pl.program_id(2)

        @pl.when(k == 0)
        def _():
            acc[...] = jnp.zeros_like(acc)

        acc[...] += _dot(a_ref[...].astype(bf16), b_ref[...].astype(bf16), dims)

        @pl.when(k == nk - 1)
        def _():
            r = acc[...]
            res = epilogue(r, *[e[...] for e in ex]) if epilogue is not None else (r,)
            for o, v in zip(outs, res):
                if o_kind == "plain":
                    o[...] = v.astype(o.dtype)
                else:
                    for q in range(tn // shard):
                        o[q] = v[:, q * shard:(q + 1) * shard].astype(o.dtype)

    if mode == "tn":
        a_spec = pl.BlockSpec((tk, tm), lambda i, j, k: (k, i))
    else:
        a_spec = pl.BlockSpec((tm, tk), lambda i, j, k: (i, k))
    jb, kb = (lambda i, j, k: j), (lambda i, j, k: k)
    if mode == "nt":
        b_spec = _mat_spec(b, b_kind, b_lead, tn, tk, jb, kb)
    else:
        b_spec = _mat_spec(b, b_kind, b_lead, tk, tn, kb, jb)
    e_spec = pl.BlockSpec((tm, tn), lambda i, j, k: (i, j))
    if o_kind == "plain":
        o_spec, o_shape = e_spec, (M, N)
    else:
        o_spec, o_shape = pl.BlockSpec((tn // shard, tm, shard), lambda i, j, k: (j, i, 0)), (N_DEV, M, shard)
    res = pl.pallas_call(
        body, name=name,
        grid=(M // tm, N // tn, nk),
        in_specs=[a_spec, b_spec] + [e_spec] * n_ex,
        out_specs=[o_spec] * n_out,
        out_shape=[jax.ShapeDtypeStruct(o_shape, dt) for dt in out_dtypes],
        scratch_shapes=[pltpu.VMEM((tm, tn), f32)],
        compiler_params=_cparams("parallel", "parallel", "arbitrary"),
    )(a, b, *extras)
    return res[0] if n_out == 1 else res


def _row_spec(tm, cb, width):
    assert (cb * LANE) % width == 0
    blk = (cb * LANE) // width
    return pl.BlockSpec((tm, width), lambda i: (i, blk))


def _whole_spec(p):
    nd = p.ndim
    return pl.BlockSpec(p.shape, lambda i: (0,) * nd)


def _tl_fwd(name, fn, rows, params, out_widths, out_dtypes, tm=256):
    T = rows[0][0].shape[0]
    tm = min(tm, T)
    nr, npar = len(rows), len(params)

    def body(*refs):
        vals = [r[...] for r in refs[:nr + npar]]
        outs = fn(*vals)
        for o, v in zip(refs[nr + npar:], outs):
            o[...] = v.astype(o.dtype)

    res = pl.pallas_call(
        body, name=name, grid=(T // tm,),
        in_specs=[_row_spec(tm, cb, w) for (_, cb, w) in rows] + [_whole_spec(p) for p in params],
        out_specs=[pl.BlockSpec((tm, w), lambda i: (i, 0)) for w in out_widths],
        out_shape=[jax.ShapeDtypeStruct((T, w), dt) for w, dt in zip(out_widths, out_dtypes)],
        compiler_params=_cparams("parallel"),
    )(*[r[0] for r in rows], *params)
    return res


def _tl_bwd(name, fn, rows, params, cot_rows, cot_fn=None, tm=256):
    T = rows[0][0].shape[0]
    tm = min(tm, T)
    nr, npar, nc = len(rows), len(params), len(cot_rows)

    def body(*refs):
        vals = [r[...] for r in refs[:nr + npar]]
        cots = [r[...] for r in refs[nr + npar:nr + npar + nc]]
        outs = refs[nr + npar + nc:]
        cot = tuple(cot_fn(*cots)) if cot_fn is not None else tuple(cots)
        _, vjp = jax.vjp(fn, *vals)
        grads = vjp(cot)
        for o, g in zip(outs[:nr], grads[:nr]):
            o[...] = g.astype(o.dtype)
        i = pl.program_id(0)
        for o, g in zip(outs[nr:], grads[nr:]):
            @pl.when(i == 0)
            def _(o=o):
                o[...] = jnp.zeros_like(o)
            o[...] += g

    res = pl.pallas_call(
        body, name=name, grid=(T // tm,),
        in_specs=[_row_spec(tm, cb, w) for (_, cb, w) in rows] + [_whole_spec(p) for p in params]
        + [_row_spec(tm, cb, w) for (_, cb, w) in cot_rows],
        out_specs=[pl.BlockSpec((tm, w), lambda i: (i, 0)) for (_, _, w) in rows] + [_whole_spec(p) for p in params],
        out_shape=[jax.ShapeDtypeStruct((T, w), f32) for (_, _, w) in rows]
        + [jax.ShapeDtypeStruct(p.shape, f32) for p in params],
        compiler_params=_cparams("arbitrary"),
    )(*[r[0] for r in rows], *params, *[r[0] for r in cot_rows])
    return res[:nr], res[nr:]


def _ln_res_fn(x, mix, g, b):
    pre = DN_ALPHA * x + mix
    mu = jnp.mean(pre, axis=-1, keepdims=True)
    var = jnp.mean(jnp.square(pre - mu), axis=-1, keepdims=True)
    return ((pre - mu) * lax.rsqrt(var + LN_EPS) * g + b,)


@jax.custom_jvp
def _expm1(x):
    small = jnp.abs(x) < 0.3
    xs = jnp.where(small, x, 0.0)
    poly = xs * (1.0 + xs * (1 / 2 + xs * (1 / 6 + xs * (1 / 24 + xs * (1 / 120 + xs * (
        1 / 720 + xs * (1 / 5040 + xs * (1 / 40320 + xs * (1 / 362880)))))))))
    return jnp.where(small, poly, jnp.exp(x) - 1.0)


@_expm1.defjvp
def _expm1_jvp(primals, tangents):
    (x,), (t,) = primals, tangents
    return _expm1(x), t * jnp.exp(x)


def _rglru_pre_fn(pre_r, pre_i, xc, b_a, b_x, lam):
    r = jax.nn.sigmoid(pre_r + b_a)
    i = jax.nn.sigmoid(pre_i + b_x)
    log_a = -LRU_C * r * jax.nn.softplus(-lam)
    a = jnp.exp(log_a)
    b = jnp.sqrt(-_expm1(2.0 * log_a)) * (i * xc)
    return a, b


def _rec_gate_fn(h, gate):
    return (h * jax.nn.gelu(gate),)


def _loss_head(y, t, tm=256):
    T, Dm = y.shape

    def body(y_ref, t_ref, dy_ref, loss_ref):
        e = y_ref[...] - t_ref[...]
        dy_ref[...] = e * (1.0 / Dm)

        @pl.when(pl.program_id(0) == 0)
        def _():
            loss_ref[...] = jnp.zeros_like(loss_ref)

        loss_ref[...] += 0.5 * jnp.sum(jnp.mean(e * e, axis=-1, keepdims=True), axis=0, keepdims=True)

    dy, loss = pl.pallas_call(
        body, name="loss_head", grid=(T // tm,),
        in_specs=[pl.BlockSpec((tm, Dm), lambda i: (i, 0))] * 2,
        out_specs=[pl.BlockSpec((tm, Dm), lambda i: (i, 0)), pl.BlockSpec((SUBLANE, LANE), lambda i: (0, 0))],
        out_shape=[jax.ShapeDtypeStruct((T, Dm), f32), jax.ShapeDtypeStruct((SUBLANE, LANE), f32)],
        compiler_params=_cparams("arbitrary"),
    )(y, t)
    return loss[0, 0], dy


def _conv_fwd(name, x, cb0, nblk, w, bias, tm=512):
    T = x.shape[0]
    tm = min(tm, T)
    hb = tm // SUBLANE
    has_b = bias is not None

    def body(*refs):
        cur, prev, w_ref = refs[:3]
        b_ref = refs[3] if has_b else None
        o = refs[-1]
        i = pl.program_id(1)
        p = jnp.where(i > 0, prev[...], 0.0)
        xcat = jnp.concatenate([p, cur[...]], axis=0)
        acc = cur[...] * w_ref[3:4, :]
        for j in range(3):
            acc = acc + pltpu.roll(xcat, 3 - j, axis=0)[SUBLANE:] * w_ref[j:j + 1, :]
        if has_b:
            acc = acc + b_ref[...]
        o[...] = acc

    in_specs = [
        pl.BlockSpec((tm, LANE), lambda c, i: (i, cb0 + c)),
        pl.BlockSpec((SUBLANE, LANE), lambda c, i: (jnp.maximum(i * hb - 1, 0), cb0 + c)),
        pl.BlockSpec((4, LANE), lambda c, i: (0, c)),
    ]
    args = [x, x, w]
    if has_b:
        in_specs.append(pl.BlockSpec((1, LANE), lambda c, i: (0, c)))
        args.append(bias)
    return pl.pallas_call(
        body, name=name, grid=(nblk, T // tm),
        in_specs=in_specs,
        out_specs=pl.BlockSpec((tm, LANE), lambda c, i: (i, c)),
        out_shape=jax.ShapeDtypeStruct((T, nblk * LANE), f32),
        compiler_params=_cparams("parallel", "parallel"),
    )(*args)


def _conv_bwd(name, dy, x, cb0, nblk, w, tm=512):
    T = x.shape[0]
    tm = min(tm, T)
    hb = tm // SUBLANE
    nt = T // tm

    def body(dcur, dnext, xcur, xprev, w_ref, dx_ref, dw_ref, db_ref):
        i = pl.program_id(1)
        d = dcur[...]
        dn = jnp.where(i < nt - 1, dnext[...], 0.0)
        dcat = jnp.concatenate([d, dn], axis=0)
        acc = d * w_ref[3:4, :]
        for j in range(3):
            s = 3 - j
            acc = acc + pltpu.roll(dcat, tm + SUBLANE - s, axis=0)[:tm] * w_ref[j:j + 1, :]
        dx_ref[...] = acc

        p = jnp.where(i > 0, xprev[...], 0.0)
        xcat = jnp.concatenate([p, xcur[...]], axis=0)
        rows = [jnp.sum(d * pltpu.roll(xcat, 3 - j, axis=0)[SUBLANE:], axis=0, keepdims=True) for j in range(3)]
        rows.append(jnp.sum(d * xcur[...], axis=0, keepdims=True))
        rows.append(jnp.zeros((SUBLANE - 4, LANE), f32))

        @pl.when(i == 0)
        def _():
            dw_ref[...] = jnp.zeros_like(dw_ref)
            db_ref[...] = jnp.zeros_like(db_ref)

        dw_ref[...] += jnp.concatenate(rows, axis=0)
        db_ref[...] += jnp.broadcast_to(jnp.sum(d, axis=0, keepdims=True), (SUBLANE, LANE))

    nh = T // SUBLANE
    dx, dw, db = pl.pallas_call(
        body, name=name, grid=(nblk, nt),
        in_specs=[
            pl.BlockSpec((tm, LANE), lambda c, i: (i, c)),
            pl.BlockSpec((SUBLANE, LANE), lambda c, i: (jnp.minimum((i + 1) * hb, nh - 1), c)),
            pl.BlockSpec((tm, LANE), lambda c, i: (i, cb0 + c)),
            pl.BlockSpec((SUBLANE, LANE), lambda c, i: (jnp.maximum(i * hb - 1, 0), cb0 + c)),
            pl.BlockSpec((4, LANE), lambda c, i: (0, c)),
        ],
        out_specs=[
            pl.BlockSpec((tm, LANE), lambda c, i: (i, c)),
            pl.BlockSpec((SUBLANE, LANE), lambda c, i: (0, c)),
            pl.BlockSpec((SUBLANE, LANE), lambda c, i: (0, c)),
        ],
        out_shape=[jax.ShapeDtypeStruct((T, nblk * LANE), f32),
                   jax.ShapeDtypeStruct((SUBLANE, nblk * LANE), f32),
                   jax.ShapeDtypeStruct((SUBLANE, nblk * LANE), f32)],
        compiler_params=_cparams("parallel", "arbitrary"),
    )(dy, dy, x, x, w)
    return dx, dw[:4], db[0]


@functools.partial(jax.custom_vjp, nondiff_argnums=(1,))
def _lroll(x, s):
    return pltpu.roll(x, s, axis=1)


def _lroll_fwd(x, s):
    return _lroll(x, s), None


def _lroll_bwd(s, _, g):
    return (_lroll(g, (LANE - s) % LANE),)


_lroll.defvjp(_lroll_fwd, _lroll_bwd)


def _rope_tables(T):
    half = A_HEAD_DIM // 2
    inv_freq = ROPE_THETA ** (-jnp.arange(half, dtype=f32) / half)
    ang = jnp.arange(T, dtype=f32)[:, None] * inv_freq[None, :]
    cos, sin = jnp.cos(ang), jnp.sin(ang)
    return jnp.tile(jnp.concatenate([cos, cos], axis=1), (1, 2)), jnp.tile(jnp.concatenate([-sin, sin], axis=1), (1, 2))


def _attn_block_fn(n, q, kp, kc, vp, vc, cq, sq, cp, sp, sinks):
    W = WINDOW
    lane = lax.broadcasted_iota(jnp.int32, (W, LANE), 1)
    lo_half = (lane % A_HEAD_DIM) < (A_HEAD_DIM // 2)
    lane8 = lax.broadcasted_iota(jnp.int32, sinks.shape, 1)

    def rope(x, c, s):
        return x * c + jnp.where(lo_half, _lroll(x, LANE - A_HEAD_DIM // 2), _lroll(x, A_HEAD_DIM // 2)) * s

    k2 = jnp.concatenate([rope(kp, cp, sp), rope(kc, cq, sq)], axis=0).astype(bf16)
    v2 = jnp.concatenate([vp, vc], axis=0).astype(bf16)
    row = lax.broadcasted_iota(jnp.int32, (W, 2 * W), 0)
    col = lax.broadcasted_iota(jnp.int32, (W, 2 * W), 1)
    dist = row + W - col
    mask = (dist >= 0) & (dist < W) & ((col >= W) | (n > 0))
    outs = []
    for t in range(4):
        qt = rope(q[:, LANE * t:LANE * (t + 1)], cq, sq)
        g = t // 2
        ot = jnp.zeros((W, LANE), f32)
        for hh in range(2):
            qa = jnp.where((lane // A_HEAD_DIM) == hh, qt, 0.0)
            if hh != g:
                qa = _lroll(qa, A_HEAD_DIM)
            s = _dot(qa.astype(bf16), k2, NT) * (A_HEAD_DIM ** -0.5)
            s = jnp.where(mask, s, -jnp.inf)
            sink = jnp.sum(jnp.where(lane8 == 2 * t + hh, sinks, 0.0), axis=1, keepdims=True)
            m = jnp.maximum(jnp.max(s, axis=-1, keepdims=True), sink)
            e = jnp.exp(s - m)
            p = e / (jnp.sum(e, axis=-1, keepdims=True) + jnp.exp(sink - m))
            o = _dot(p.astype(bf16), v2, NN)
            o = jnp.where((lane // A_HEAD_DIM) == g, o, 0.0)
            if hh != g:
                o = _lroll(o, A_HEAD_DIM)
            ot = ot + o
        outs.append(ot)
    return jnp.concatenate(outs, axis=1)


def _attn_specs():
    W = WINDOW
    prev = lambda n: jnp.maximum(n - 1, 0)
    return [
        pl.BlockSpec((W, 4 * LANE), lambda n: (n, CB_QA // 4)),
        pl.BlockSpec((W, LANE), lambda n: (prev(n), CB_KA)),
        pl.BlockSpec((W, LANE), lambda n: (n, CB_KA)),
        pl.BlockSpec((W, LANE), lambda n: (prev(n), CB_VA)),
        pl.BlockSpec((W, LANE), lambda n: (n, CB_VA)),
        pl.BlockSpec((W, LANE), lambda n: (n, 0)),
        pl.BlockSpec((W, LANE), lambda n: (n, 0)),
        pl.BlockSpec((W, LANE), lambda n: (prev(n), 0)),
        pl.BlockSpec((W, LANE), lambda n: (prev(n), 0)),
        pl.BlockSpec((1, A_Q_HEADS), lambda n: (0, 0)),
    ]


def _attn_fwd(name, proj, cos, sin, sinks):
    T = proj.shape[0]
    W = WINDOW

    def body(*refs):
        o = refs[-1]
        o[...] = _attn_block_fn(pl.program_id(0), *[r[...] for r in refs[:-1]])

    return pl.pallas_call(
        body, name=name, grid=(T // W,),
        in_specs=_attn_specs(),
        out_specs=pl.BlockSpec((W, 4 * LANE), lambda n: (n, 0)),
        out_shape=jax.ShapeDtypeStruct((T, 4 * LANE), f32),
        compiler_params=_cparams("parallel"),
    )(proj, proj, proj, proj, proj, cos, sin, cos, sin, sinks)


def _attn_bwd(name, proj, cos, sin, sinks, d_oab):
    T = proj.shape[0]
    W = WINDOW

    def body(*refs):
        ins = [r[...] for r in refs[:10]]
        do = refs[10][...]
        dq_ref, dk_ref, dv_ref, ds_ref = refs[11:]
        n = pl.program_id(0)
        _, vjp = jax.vjp(functools.partial(_attn_block_fn, n), *ins)
        dq, dkp, dkc, dvp, dvc, _, _, _, _, dsk = vjp(do)
        dq_ref[...] = dq

        @pl.when(n == 0)
        def _():
            dk_ref[...] = jnp.zeros_like(dk_ref)
            dv_ref[...] = jnp.zeros_like(dv_ref)
            ds_ref[...] = jnp.zeros_like(ds_ref)

        cur = pl.ds(pl.multiple_of(n * W, W), W)
        dk_ref[cur, :] += dkc
        dv_ref[cur, :] += dvc
        ds_ref[...] += dsk

        @pl.when(n > 0)
        def _():
            prv = pl.ds(pl.multiple_of((n - 1) * W, W), W)
            dk_ref[prv, :] += dkp
            dv_ref[prv, :] += dvp

    return pl.pallas_call(
        body, name=name, grid=(T // W,),
        in_specs=_attn_specs() + [pl.BlockSpec((W, 4 * LANE), lambda n: (n, 0))],
        out_specs=[pl.BlockSpec((W, 4 * LANE), lambda n: (n, 0)),
                   pl.BlockSpec((T, LANE), lambda n: (0, 0)),
                   pl.BlockSpec((T, LANE), lambda n: (0, 0)),
                   pl.BlockSpec((1, A_Q_HEADS), lambda n: (0, 0))],
        out_shape=[jax.ShapeDtypeStruct((T, 4 * LANE), f32), jax.ShapeDtypeStruct((T, LANE), f32),
                   jax.ShapeDtypeStruct((T, LANE), f32), jax.ShapeDtypeStruct((1, A_Q_HEADS), f32)],
        compiler_params=_cparams("arbitrary"),
    )(proj, proj, proj, proj, proj, cos, sin, cos, sin, sinks, d_oab)


@jax.custom_vjp
def _tri_inv(a):
    C = a.shape[0]
    r = lax.broadcasted_iota(jnp.int32, (C, C), 0)
    c = lax.broadcasted_iota(jnp.int32, (C, C), 1)
    t = jnp.where(r == c, 1.0, 0.0).astype(f32)
    for j in range(C - 1):
        t = t - a[:, j:j + 1] * t[j:j + 1, :]
    return t


def _tri_inv_fwd(a):
    t = _tri_inv(a)
    return t, t


def _tri_inv_bwd(t, g):
    C = t.shape[0]
    r = lax.broadcasted_iota(jnp.int32, (C, C), 0)
    c = lax.broadcasted_iota(jnp.int32, (C, C), 1)
    x = _dot(t, g, TN, precision=lax.Precision.HIGHEST)
    y = _dot(x, t, NT, precision=lax.Precision.HIGHEST)
    return (jnp.where(r > c, -y, 0.0),)


_tri_inv.defvjp(_tri_inv_fwd, _tri_inv_bwd)


def _silu(x):
    return x * jax.nn.sigmoid(x)


def _l2n(x):
    return x * lax.rsqrt(jnp.sum(x * x, axis=-1, keepdims=True) + NORM_EPS)


def _delta_chunk_fn(h, cq, ck, cv, z, lg, a_log, dt_bias, norm_w, S):
    C = B_CHUNK
    lane = lax.broadcasted_iota(jnp.int32, (C, LANE), 1)
    bl = jnp.sum(jnp.where(lane == h, lg, 0.0), axis=1, keepdims=True)
    al = jnp.sum(jnp.where(lane == B_HEADS + h, lg, 0.0), axis=1, keepdims=True)
    q = _l2n(_silu(cq)) * (B_HEAD_DIM ** -0.5)
    k = _l2n(_silu(ck))
    v = _silu(cv)
    beta = jax.nn.sigmoid(bl)
    g = -jnp.exp(a_log) * jax.nn.softplus(al + dt_bias)
    r = lax.broadcasted_iota(jnp.int32, (C, C), 0)
    c = lax.broadcasted_iota(jnp.int32, (C, C), 1)
    eye = r == c
    g_row = jnp.sum(jnp.where(eye, g, 0.0), axis=0, keepdims=True)
    gc = jnp.sum(jnp.where(c <= r, g_row, 0.0), axis=1, keepdims=True)
    gc_row = jnp.sum(jnp.where(eye, gc, 0.0), axis=0, keepdims=True)
    decay_incl = jnp.exp(jnp.where(r >= c, gc - gc_row, -jnp.inf))
    decay_strict = jnp.where(r > c, decay_incl, 0.0)
    kb = k * beta
    vb = v * beta
    kbf = k.astype(bf16)
    a_mat = _dot(kb.astype(bf16), kbf, NT) * decay_strict
    t_mat = _tri_inv(a_mat).astype(bf16)
    eg = jnp.exp(gc)
    u = _dot(t_mat, vb.astype(bf16), NN)
    w = _dot(t_mat, (kb * eg).astype(bf16), NN)
    qk = _dot(q.astype(bf16), kbf, NT) * decay_incl
    g_last = jnp.sum(g, axis=0, keepdims=True)
    k_tail = k * jnp.exp(g_last - gc)
    Sb = S.astype(bf16)
    v_new = u - _dot(w.astype(bf16), Sb, NN)
    o = _dot((q * eg).astype(bf16), Sb, NN) + _dot(qk.astype(bf16), v_new.astype(bf16), NN)
    S_new = S * jnp.exp(g_last) + _dot(k_tail.astype(bf16), v_new.astype(bf16), TN)
    ob = o * lax.rsqrt(jnp.mean(o * o, axis=-1, keepdims=True) + NORM_EPS) * norm_w
    return ob * _silu(z), S_new


def _delta_in_specs(rev, N):
    C = B_CHUNK
    ix = (lambda n: N - 1 - n) if rev else (lambda n: n)
    specs = [pl.BlockSpec((C, 3 * B_HEADS * LANE), lambda n: (ix(n), 0))]
    specs += [pl.BlockSpec((C, LANE), lambda n, h=h: (ix(n), CB_Z + h)) for h in range(B_HEADS)]
    specs += [
        pl.BlockSpec((C, LANE), lambda n: (ix(n), CB_LG)),
        pl.BlockSpec((B_HEADS, 1, 1), lambda n: (0, 0, 0)),
        pl.BlockSpec((B_HEADS, 1, 1), lambda n: (0, 0, 0)),
        pl.BlockSpec((1, LANE), lambda n: (0, 0)),
    ]
    return specs


def _delta_head_inputs(h, c_ref, z_refs, lg, al, dt, nw):
    H = B_HEADS
    return (c_ref[:, LANE * h:LANE * (h + 1)], c_ref[:, LANE * (H + h):LANE * (H + h + 1)],
            c_ref[:, LANE * (2 * H + h):LANE * (2 * H + h + 1)], z_refs[h][...], lg[...], al[h], dt[h], nw[...])


def _delta_fwd(name, c, proj, a_log, dt_bias, norm_w):
    T = c.shape[0]
    C = B_CHUNK
    N = T // C
    Dh = B_HEAD_DIM
    H = B_HEADS

    def body(*refs):
        c_ref, z_refs, (lg, al, dt, nw) = refs[0], refs[1:1 + H], refs[1 + H:5 + H]
        o_ref, s_ref, S = refs[5 + H:]

        @pl.when(pl.program_id(0) == 0)
        def _():
            S[...] = jnp.zeros_like(S)

        for h in range(H):
            s0 = S[h]
            s_ref[h] = s0
            ob, s1 = _delta_chunk_fn(h, *_delta_head_inputs(h, c_ref, z_refs, lg, al, dt, nw), s0)
            o_ref[:, LANE * h:LANE * (h + 1)] = ob
            S[h] = s1

    return pl.pallas_call(
        body, name=name, grid=(N,),
        in_specs=_delta_in_specs(False, N),
        out_specs=[pl.BlockSpec((C, H * LANE), lambda n: (n, 0)),
                   pl.BlockSpec((H, None, Dh, Dh), lambda n: (0, n, 0, 0))],
        out_shape=[jax.ShapeDtypeStruct((T, H * Dh), f32), jax.ShapeDtypeStruct((H, N, Dh, Dh), f32)],
        scratch_shapes=[pltpu.VMEM((H, Dh, Dh), f32)],
        compiler_params=_cparams("arbitrary"),
    )(c, *([proj] * H), proj, a_log, dt_bias, norm_w)


def _delta_bwd(name, c, proj, a_log, dt_bias, norm_w, s_saved, d_oab):
    T = c.shape[0]
    C = B_CHUNK
    N = T // C
    Dh = B_HEAD_DIM
    H = B_HEADS

    def body(*refs):
        c_ref, z_refs, (lg, al, dt, nw) = refs[0], refs[1:1 + H], refs[1 + H:5 + H]
        s_ref, do_ref = refs[5 + H:7 + H]
        dc, dz, dlg, dal, ddt, dnw, dS = refs[7 + H:]

        @pl.when(pl.program_id(0) == 0)
        def _():
            dS[...] = jnp.zeros_like(dS)
            dal[...] = jnp.zeros_like(dal)
            ddt[...] = jnp.zeros_like(ddt)
            dnw[...] = jnp.zeros_like(dnw)

        dlg_acc = jnp.zeros((C, LANE), f32)
        for h in range(H):
            _, vjp = jax.vjp(functools.partial(_delta_chunk_fn, h),
                             *_delta_head_inputs(h, c_ref, z_refs, lg, al, dt, nw), s_ref[h])
            g = vjp((do_ref[:, LANE * h:LANE * (h + 1)], dS[h]))
            dc[:, LANE * h:LANE * (h + 1)] = g[0]
            dc[:, LANE * (H + h):LANE * (H + h + 1)] = g[1]
            dc[:, LANE * (2 * H + h):LANE * (2 * H + h + 1)] = g[2]
            dz[:, LANE * h:LANE * (h + 1)] = g[3]
            dlg_acc = dlg_acc + g[4]
            dal[h] += g[5]
            ddt[h] += g[6]
            dnw[...] += g[7]
            dS[h] = g[8]
        dlg[...] = dlg_acc

    rn = lambda n: N - 1 - n
    return pl.pallas_call(
        body, name=name, grid=(N,),
        in_specs=_delta_in_specs(True, N) + [
            pl.BlockSpec((H, None, Dh, Dh), lambda n: (0, rn(n), 0, 0)),
            pl.BlockSpec((C, H * LANE), lambda n: (rn(n), 1)),
        ],
        out_specs=[
            pl.BlockSpec((C, 3 * H * LANE), lambda n: (rn(n), 0)),
            pl.BlockSpec((C, H * LANE), lambda n: (rn(n), 0)),
            pl.BlockSpec((C, LANE), lambda n: (rn(n), 0)),
            pl.BlockSpec((H, 1, 1), lambda n: (0, 0, 0)),
            pl.BlockSpec((H, 1, 1), lambda n: (0, 0, 0)),
            pl.BlockSpec((1, LANE), lambda n: (0, 0)),
        ],
        out_shape=[jax.ShapeDtypeStruct((T, 3 * H * Dh), f32), jax.ShapeDtypeStruct((T, H * Dh), f32),
                   jax.ShapeDtypeStruct((T, LANE), f32), jax.ShapeDtypeStruct((H, 1, 1), f32),
                   jax.ShapeDtypeStruct((H, 1, 1), f32), jax.ShapeDtypeStruct((1, LANE), f32)],
        scratch_shapes=[pltpu.VMEM((H, Dh, Dh), f32)],
        compiler_params=_cparams("arbitrary"),
    )(c, *([proj] * H), proj, a_log, dt_bias, norm_w, s_saved, d_oab)


def _blockdiag_fwd(name, xc, w_a, w_x, tm=512):
    T, Wd = xc.shape
    bw = Wd // LRU_BLOCKS
    tm = min(tm, T)

    def body(x_ref, wa_ref, wx_ref, oa, ox):
        xb = x_ref[...].astype(bf16)
        oa[...] = _dot(xb, wa_ref[...].astype(bf16), NN)
        ox[...] = _dot(xb, wx_ref[...].astype(bf16), NN)

    xs = pl.BlockSpec((tm, bw), lambda i, h: (i, h))
    ws = pl.BlockSpec((None, bw, bw), lambda i, h: (h, 0, 0))
    return pl.pallas_call(
        body, name=name, grid=(T // tm, LRU_BLOCKS), in_specs=[xs, ws, ws], out_specs=[xs, xs],
        out_shape=[jax.ShapeDtypeStruct((T, Wd), f32)] * 2,
        compiler_params=_cparams("parallel", "parallel"),
    )(xc, w_a, w_x)


def _blockdiag_bwd_dx(name, dpr, dpi, w_a, w_x, addend, tm=512):
    T, Wd = dpr.shape
    bw = Wd // LRU_BLOCKS
    tm = min(tm, T)

    def body(dr, di, wa_ref, wx_ref, add, o):
        o[...] = (add[...] + _dot(dr[...].astype(bf16), wa_ref[...].astype(bf16), NT)
                  + _dot(di[...].astype(bf16), wx_ref[...].astype(bf16), NT))

    xs = pl.BlockSpec((tm, bw), lambda i, h: (i, h))
    ws = pl.BlockSpec((None, bw, bw), lambda i, h: (h, 0, 0))
    return pl.pallas_call(
        body, name=name, grid=(T // tm, LRU_BLOCKS), in_specs=[xs, xs, ws, ws, xs], out_specs=xs,
        out_shape=jax.ShapeDtypeStruct((T, Wd), f32),
        compiler_params=_cparams("parallel", "parallel"),
    )(dpr, dpi, w_a, w_x, addend)


def _blockdiag_bwd_dw(name, xc, dpr, dpi, tk=512):
    T, Wd = xc.shape
    bw = Wd // LRU_BLOCKS
    tk = min(tk, T)

    def body(x_ref, dr, di, oa, ox):
        @pl.when(pl.program_id(1) == 0)
        def _():
            oa[...] = jnp.zeros_like(oa)
            ox[...] = jnp.zeros_like(ox)

        xb = x_ref[...].astype(bf16)
        oa[...] += _dot(xb, dr[...].astype(bf16), TN)
        ox[...] += _dot(xb, di[...].astype(bf16), TN)

    xs = pl.BlockSpec((tk, bw), lambda h, k: (k, h))
    ws = pl.BlockSpec((None, bw, bw), lambda h, k: (h, 0, 0))
    return pl.pallas_call(
        body, name=name, grid=(LRU_BLOCKS, T // tk), in_specs=[xs, xs, xs], out_specs=[ws, ws],
        out_shape=[jax.ShapeDtypeStruct((LRU_BLOCKS, bw, bw), f32)] * 2,
        compiler_params=_cparams("parallel", "arbitrary"),
    )(xc, dpr, dpi)


def _scan(name, a, b, reverse, tt=512, cb=512):
    T, Wd = a.shape
    tt, cb = min(tt, T), min(cb, Wd)
    nt = T // tt
    ng = tt // SUBLANE

    def body(a_ref, b_ref, o_ref, carry):
        @pl.when(pl.program_id(1) == 0)
        def _():
            carry[...] = jnp.zeros_like(carry)

        row = lax.broadcasted_iota(jnp.int32, (SUBLANE, cb), 0)

        def step(gi, hp):
            g = (ng - 1 - gi) if reverse else gi
            off = pl.multiple_of(g * SUBLANE, SUBLANE)
            A = a_ref[pl.ds(off, SUBLANE), :]
            B = b_ref[pl.ds(off, SUBLANE), :]
            for s in (1, 2, 4):
                sh = (SUBLANE - s) if reverse else s
                As = pltpu.roll(A, sh, axis=0)
                Bs = pltpu.roll(B, sh, axis=0)
                valid = (row < SUBLANE - s) if reverse else (row >= s)
                B = jnp.where(valid, A * Bs + B, B)
                A = jnp.where(valid, A * As, A)
            hcur = A * hp + B
            o_ref[pl.ds(off, SUBLANE), :] = hcur
            edge = hcur[0:1, :] if reverse else hcur[SUBLANE - 1:SUBLANE, :]
            return jnp.broadcast_to(edge, (SUBLANE, cb))

        carry[...] = lax.fori_loop(0, ng, step, carry[...])

    ti = (lambda c, i: (nt - 1 - i, c)) if reverse else (lambda c, i: (i, c))
    spec = pl.BlockSpec((tt, cb), ti)
    return pl.pallas_call(
        body, name=name, grid=(Wd // cb, nt), in_specs=[spec, spec], out_specs=spec,
        out_shape=jax.ShapeDtypeStruct((T, Wd), f32),
        scratch_shapes=[pltpu.VMEM((SUBLANE, cb), f32)],
        compiler_params=_cparams("parallel", "arbitrary"),
    )(a, b)


def _relu2_epilogue(r):
    h = jnp.maximum(r, 0.0)
    return r, h * h


def _drelu2_epilogue(r, a):
    return (r * (2.0 * jnp.maximum(a, 0.0)),)


def _add_epilogue(r, e):
    return (r + e,)


def _merge_cols(name, g, tm=256):
    _, L, R, s = g.shape

    def body(g_ref, o_ref):
        for d in range(N_DEV):
            o_ref[:, s * d:s * (d + 1)] = g_ref[d].astype(bf16)
        o_ref[:, N_DEV * s:] = jnp.zeros((tm, HYB_PROJ_PAD - N_DEV * s), bf16)

    return pl.pallas_call(
        body, name=name, grid=(L, R // tm),
        in_specs=[pl.BlockSpec((N_DEV, None, tm, s), lambda l, i: (0, l, i, 0))],
        out_specs=pl.BlockSpec((None, tm, HYB_PROJ_PAD), lambda l, i: (l, i, 0)),
        out_shape=jax.ShapeDtypeStruct((L, R, HYB_PROJ_PAD), bf16),
        compiler_params=_cparams("parallel", "parallel"),
    )(g)


def _split_cols(name, dw, tm=256):
    R = dw.shape[0]
    s = HYB_PROJ // N_DEV

    def body(g_ref, o_ref):
        for d in range(N_DEV):
            o_ref[d] = g_ref[:, s * d:s * (d + 1)].astype(bf16)

    return pl.pallas_call(
        body, name=name, grid=(R // tm,),
        in_specs=[pl.BlockSpec((tm, HYB_PROJ_PAD), lambda i: (i, 0))],
        out_specs=pl.BlockSpec((N_DEV, tm, s), lambda i: (0, i, 0)),
        out_shape=jax.ShapeDtypeStruct((N_DEV, R, s), bf16),
        compiler_params=_cparams("parallel"),
    )(dw)


def _rows_to_dev(dw):
    nb, r, c = dw.shape
    t = dw.reshape(nb, N_DEV, r // N_DEV, c)
    return jnp.moveaxis(t, 1, 0).reshape(N_DEV, nb * (r // N_DEV), c).astype(bf16)


def _hybrid_fwd(tag, x, W, j, cos, sin):
    proj = _mm(f"{tag}_proj", x, W["hyb_w_in"], "nn", b_kind="lead", b_lead=j)
    o_a = _attn_fwd(f"{tag}_attn", proj, cos, sin, W["hyb_sinks"][j][None, :])
    c = _conv_fwd(f"{tag}_conv", proj, CB_CONV, 12, W["hyb_conv_w"][j], None)
    o_b, s_saved = _delta_fwd(f"{tag}_delta", c, proj, W["hyb_a_log"][j].reshape(B_HEADS, 1, 1),
                              W["hyb_dt_bias"][j].reshape(B_HEADS, 1, 1), W["hyb_norm_w"][j][None, :])
    o_ab = jnp.concatenate([o_a, o_b], axis=1)
    mix = _mm(f"{tag}_out", o_ab, W["hyb_w_out"], "nn", b_kind="lead", b_lead=j)
    return mix, (proj, c, s_saved, o_ab)


def _hybrid_bwd(tag, x, dmix, addend, W, j, cos, sin, saved, G):
    proj, c, s_saved, o_ab = saved
    T = x.shape[0]
    d_oab = _mm(f"{tag}_dout", dmix, W["hyb_w_out"], "nt", b_kind="lead", b_lead=j)
    G["hyb_w_out"][j] = _mm(f"{tag}_dwout", o_ab, dmix, "tn", out_dtypes=(bf16,)).reshape(N_DEV, -1, D_MODEL)
    dq, dk, dv, dsinks = _attn_bwd(f"{tag}_dattn", proj, cos, sin, W["hyb_sinks"][j][None, :], d_oab)
    a_log = W["hyb_a_log"][j].reshape(B_HEADS, 1, 1)
    dt_bias = W["hyb_dt_bias"][j].reshape(B_HEADS, 1, 1)
    dc, dz, dlg, dal, ddt, dnw = _delta_bwd(f"{tag}_ddelta", c, proj, a_log, dt_bias, W["hyb_norm_w"][j][None, :],
                                            s_saved, d_oab)
    dconv_in, dconv_w, _ = _conv_bwd(f"{tag}_dconv", dc, proj, CB_CONV, 12, W["hyb_conv_w"][j])
    dproj = jnp.concatenate([dq, dk, dv, dconv_in, dz, dlg,
                             jnp.zeros((T, HYB_PROJ_PAD - (CB_LG + 1) * LANE), f32)], axis=1)
    dx = _mm(f"{tag}_dx", dproj, W["hyb_w_in"], "nt", b_kind="lead", b_lead=j, epilogue=_add_epilogue,
             extras=(addend,))
    G["hyb_w_in"][j] = _split_cols(f"{tag}_dwin_split", _mm(f"{tag}_dwin", x, dproj, "tn"))
    G["hyb_sinks"][j] = dsinks[0]
    G["hyb_conv_w"][j] = dconv_w
    G["hyb_a_log"][j] = dal.reshape(B_HEADS)
    G["hyb_dt_bias"][j] = ddt.reshape(B_HEADS)
    G["hyb_norm_w"][j] = dnw[0]
    return dx


def _rec_fwd(tag, x, W, j):
    Wd = D_MODEL
    proj = _mm(f"{tag}_proj", x, W["rec_w_in"], "nn", b_kind="devcol", b_lead=j)
    xc = _conv_fwd(f"{tag}_conv", proj, 0, Wd // LANE, W["rec_conv_w"][j], W["rec_conv_b"][j][None, :])
    pre_r, pre_i = _blockdiag_fwd(f"{tag}_gates", xc, W["rec_w_a"][j], W["rec_w_x"][j])
    pars = [W["rec_b_a"][j][None, :], W["rec_b_x"][j][None, :], W["rec_lambda"][j][None, :]]
    a, b = _tl_fwd(f"{tag}_pre", _rglru_pre_fn, [(pre_r, 0, Wd), (pre_i, 0, Wd), (xc, 0, Wd)], pars, [Wd, Wd], [f32, f32])
    h = _scan(f"{tag}_scan", a, b, False)
    (hg,) = _tl_fwd(f"{tag}_gate", _rec_gate_fn, [(h, 0, Wd), (proj, Wd // LANE, Wd)], [], [Wd], [f32])
    mix = _mm(f"{tag}_out", hg, W["rec_w_out"], "nn", b_kind="lead", b_lead=j)
    return mix, (proj, xc, pre_r, pre_i, a, h, hg)


def _rec_bwd(tag, x, dmix, addend, W, j, saved, G):
    proj, xc, pre_r, pre_i, a, h, hg = saved
    Wd = D_MODEL
    dhg = _mm(f"{tag}_dout", dmix, W["rec_w_out"], "nt", b_kind="lead", b_lead=j)
    G["rec_w_out"][j] = _mm(f"{tag}_dwout", hg, dmix, "tn", out_dtypes=(bf16,)).reshape(N_DEV, -1, D_MODEL)
    (dh, dgate), _ = _tl_bwd(f"{tag}_dgate", _rec_gate_fn, [(h, 0, Wd), (proj, Wd // LANE, Wd)], [], [(dhg, 0, Wd)])
    a_next = jnp.concatenate([a[1:], jnp.zeros((1, Wd), f32)], axis=0)
    h_prev = jnp.concatenate([jnp.zeros((1, Wd), f32), h[:-1]], axis=0)
    lam_t = _scan(f"{tag}_dscan", a_next, dh, True)
    pars = [W["rec_b_a"][j][None, :], W["rec_b_x"][j][None, :], W["rec_lambda"][j][None, :]]
    (dpr, dpi, dxc1), (db_a, db_x, dlam) = _tl_bwd(
        f"{tag}_dpre", _rglru_pre_fn, [(pre_r, 0, Wd), (pre_i, 0, Wd), (xc, 0, Wd)], pars,
        [(lam_t, 0, Wd), (h_prev, 0, Wd)], cot_fn=lambda lt, hp: (lt * hp, lt))
    dxc = _blockdiag_bwd_dx(f"{tag}_dgates_dx", dpr, dpi, W["rec_w_a"][j], W["rec_w_x"][j], dxc1)
    dwa, dwx = _blockdiag_bwd_dw(f"{tag}_dgates_dw", xc, dpr, dpi)
    G["rec_w_a"][j], G["rec_w_x"][j] = _rows_to_dev(dwa), _rows_to_dev(dwx)
    dxr, dconv_w, dconv_b = _conv_bwd(f"{tag}_dconv", dxc, proj, 0, Wd // LANE, W["rec_conv_w"][j])
    dproj = jnp.concatenate([dxr, dgate], axis=1)
    dx = _mm(f"{tag}_dx", dproj, W["rec_w_in"], "nt", b_kind="devcol", b_lead=j, epilogue=_add_epilogue,
             extras=(addend,))
    G["rec_w_in"][j] = _mm(f"{tag}_dwin", x, dproj, "tn", o_kind="devcol", out_dtypes=(bf16,))
    G["rec_conv_w"][j] = dconv_w
    G["rec_conv_b"][j] = dconv_b
    G["rec_b_a"][j] = db_a[0]
    G["rec_b_x"][j] = db_x[0]
    G["rec_lambda"][j] = dlam[0]
    return dx


def _local_step(x, target, W):
    T = x.shape[0]
    cos, sin = _rope_tables(T)
    saved = []
    for layer in range(DEPTH):
        j = layer // 2
        tag = f"L{layer}"
        if layer % 2 == 0:
            mix, sv = _hybrid_fwd(tag, x, W, j, cos, sin)
        else:
            mix, sv = _rec_fwd(tag, x, W, j)
        ln1 = [W["ln1_g"][layer][None, :], W["ln1_b"][layer][None, :]]
        (x1,) = _tl_fwd(f"{tag}_ln1", _ln_res_fn, [(x, 0, D_MODEL), (mix, 0, D_MODEL)], ln1, [D_MODEL], [f32])
        a, h2 = _mm(f"{tag}_mlp1", x1, W["mlp_w1"], "nn", b_kind="devcol", b_lead=layer, epilogue=_relu2_epilogue,
                    out_dtypes=(f32, bf16))
        y = _mm(f"{tag}_mlp2", h2, W["mlp_w2"], "nn", b_kind="devrow", b_lead=layer)
        ln2 = [W["ln2_g"][layer][None, :], W["ln2_b"][layer][None, :]]
        (x2,) = _tl_fwd(f"{tag}_ln2", _ln_res_fn, [(x1, 0, D_MODEL), (y, 0, D_MODEL)], ln2, [D_MODEL], [f32])
        saved.append((x, sv, mix, x1, a, h2, y))
        x = x2
    loss, dx = _loss_head(x, target)

    G = {k: [None] * (DEPTH if k.startswith(("ln", "mlp")) else DEPTH // 2) for k in (
        "hyb_w_in", "hyb_sinks", "hyb_conv_w", "hyb_a_log", "hyb_dt_bias", "hyb_norm_w", "hyb_w_out",
        "rec_w_in", "rec_conv_w", "rec_conv_b", "rec_w_a", "rec_b_a", "rec_w_x", "rec_b_x", "rec_lambda", "rec_w_out",
        "ln1_g", "ln1_b", "mlp_w1", "mlp_w2", "ln2_g", "ln2_b")}
    for layer in reversed(range(DEPTH)):
        j = layer // 2
        tag = f"L{layer}"
        x0, sv, mix, x1, a, h2, y = saved[layer]
        ln2 = [W["ln2_g"][layer][None, :], W["ln2_b"][layer][None, :]]
        (dx1_a, dy), (dg2, db2) = _tl_bwd(f"{tag}_dln2", _ln_res_fn, [(x1, 0, D_MODEL), (y, 0, D_MODEL)], ln2,
                                          [(dx, 0, D_MODEL)])
        G["ln2_g"][layer], G["ln2_b"][layer] = dg2[0], db2[0]
        da = _mm(f"{tag}_dmlp2", dy, W["mlp_w2"], "nt", b_kind="devrow", b_lead=layer, epilogue=_drelu2_epilogue,
                 extras=(a,), out_dtypes=(bf16,))
        G["mlp_w2"][layer] = _mm(f"{tag}_dw2", h2, dy, "tn", out_dtypes=(bf16,)).reshape(N_DEV, -1, D_MODEL)
        dx1 = _mm(f"{tag}_dmlp1", da, W["mlp_w1"], "nt", b_kind="devcol", b_lead=layer, epilogue=_add_epilogue,
                  extras=(dx1_a,))
        G["mlp_w1"][layer] = _mm(f"{tag}_dw1", x1, da, "tn", o_kind="devcol", out_dtypes=(bf16,))
        ln1 = [W["ln1_g"][layer][None, :], W["ln1_b"][layer][None, :]]
        (dx0_a, dmix), (dg1, db1) = _tl_bwd(f"{tag}_dln1", _ln_res_fn, [(x0, 0, D_MODEL), (mix, 0, D_MODEL)], ln1,
                                            [(dx1, 0, D_MODEL)])
        G["ln1_g"][layer], G["ln1_b"][layer] = dg1[0], db1[0]
        if layer % 2 == 0:
            dx = _hybrid_bwd(tag, x0, dmix, dx0_a, W, j, cos, sin, sv, G)
        else:
            dx = _rec_bwd(tag, x0, dmix, dx0_a, W, j, sv, G)
    big = {k for k, _ in BIG}
    return loss, dx, {k: (v if k in big else jnp.stack(v)) for k, v in G.items()}


def _my_coords():
    return lax.axis_index("x"), lax.axis_index("y"), lax.axis_index("c")


def _all_gather(name, arrays):
    na = len(arrays)

    def body(*refs):
        x_refs, out_refs = refs[:na], refs[na:2 * na]
        send_sems, recv_sems, local_sems = refs[2 * na:]
        x, y, c = _my_coords()
        me, sibling = (x, y, c), (x, y, 1 - c)
        chips = [(1 - x, y), (x, 1 - y), (1 - x, 1 - y)]

        def blk(a, px, py, pc):
            return out_refs[a].at[4 * px + 2 * py + pc]

        def copy(a, k, block, to, src=None):
            return pltpu.make_async_remote_copy(
                src_ref=blk(a, *block) if src is None else src, dst_ref=blk(a, *block),
                send_sem=send_sems.at[a, k], recv_sem=recv_sems.at[a, k],
                device_id=to, device_id_type=pl.DeviceIdType.MESH)

        mine = [pltpu.make_async_copy(x_refs[a], blk(a, *me), local_sems.at[a]) for a in range(na)]
        for cp in mine:
            cp.start()
        first = []
        for a in range(na):
            first.append(copy(a, 0, me, sibling, src=x_refs[a]))
            first += [copy(a, 1 + j, me, (*chip, c), src=x_refs[a]) for j, chip in enumerate(chips)]
        for cp in first:
            cp.start()
        passed = []
        for a in range(na):
            for j, chip in enumerate(chips):
                copy(a, 1 + j, (*chip, c), me).wait_recv()
                passed.append(copy(a, 4 + j, (*chip, c), sibling))
                passed[-1].start()
        for a in range(na):
            copy(a, 0, sibling, me).wait_recv()
            for j, chip in enumerate(chips):
                copy(a, 4 + j, (*chip, 1 - c), me).wait_recv()
        for cp in first + passed:
            cp.wait_send()
        for cp in mine:
            cp.wait()

    return pl.pallas_call(
        body, name=name,
        out_shape=[jax.ShapeDtypeStruct((N_DEV,) + a.shape, a.dtype) for a in arrays],
        in_specs=[pl.BlockSpec(memory_space=pl.ANY)] * na,
        out_specs=[pl.BlockSpec(memory_space=pl.ANY)] * na,
        scratch_shapes=[pltpu.SemaphoreType.DMA((na, 7)), pltpu.SemaphoreType.DMA((na, 7)),
                        pltpu.SemaphoreType.DMA((na,))],
    )(*arrays)


def _all_to_all(name, groups):
    flat = [(p, l, arr) for p, layers in enumerate(groups) for l, arr in enumerate(layers)]
    na = len(flat)

    def body(*refs):
        g_refs, out_refs = refs[:na], refs[na:na + len(groups)]
        send_sems, recv_sems, local_sems = refs[na + len(groups):]
        x, y, c = _my_coords()
        me = 4 * x + 2 * y + c
        mine, copies = [], []
        for a, (p, l, _) in enumerate(flat):
            mine.append(pltpu.make_async_copy(g_refs[a].at[me], out_refs[p].at[me, l], local_sems.at[a]))
            for k in range(1, N_DEV):
                px = (1 - x) if (k >> 2) & 1 else x
                py = (1 - y) if (k >> 1) & 1 else y
                pc = (1 - c) if k & 1 else c
                copies.append(pltpu.make_async_remote_copy(
                    src_ref=g_refs[a].at[4 * px + 2 * py + pc], dst_ref=out_refs[p].at[me, l],
                    send_sem=send_sems.at[a, k - 1], recv_sem=recv_sems.at[a, k - 1],
                    device_id=(px, py, pc), device_id_type=pl.DeviceIdType.MESH))
        for cp in mine + copies:
            cp.start()
        for cp in copies:
            cp.wait_recv()
        for cp in copies:
            cp.wait_send()
        for cp in mine:
            cp.wait()

    return pl.pallas_call(
        body, name=name,
        out_shape=[jax.ShapeDtypeStruct((N_DEV, len(layers)) + layers[0].shape[1:], layers[0].dtype)
                   for layers in groups],
        in_specs=[pl.BlockSpec(memory_space=pl.ANY)] * na,
        out_specs=[pl.BlockSpec(memory_space=pl.ANY)] * len(groups),
        scratch_shapes=[pltpu.SemaphoreType.DMA((na, 7)), pltpu.SemaphoreType.DMA((na, 7)),
                        pltpu.SemaphoreType.DMA((na,))],
    )(*[arr for _, _, arr in flat])


def _sum_blocks(name, land):
    _, R, n = land.shape
    tr = R

    def body(l_ref, o_ref):
        acc = l_ref[0].astype(f32)
        for s in range(1, N_DEV):
            acc = acc + l_ref[s].astype(f32)
        o_ref[...] = acc

    return pl.pallas_call(
        body, name=name, grid=(R // tr,),
        in_specs=[pl.BlockSpec((N_DEV, tr, n), lambda i: (0, i, 0))],
        out_specs=pl.BlockSpec((tr, n), lambda i: (i, 0)),
        out_shape=jax.ShapeDtypeStruct((R, n), f32),
        compiler_params=_cparams("parallel"),
    )(land)


def _adamw(name, w, g, m, v):
    shape = w.shape
    last = shape[-1]
    rows = math.prod(shape[:-1])
    tm = 256 if rows % 256 == 0 and rows > 256 else rows
    w2, g2, m2, v2 = (t.reshape(rows, last) for t in (w, g, m, v))

    def body(w_ref, g_ref, m_ref, v_ref, d_ref, mo_ref, vo_ref):
        gg = g_ref[...]
        mn = ADAM_B1 * m_ref[...] + (1.0 - ADAM_B1) * gg
        vn = ADAM_B2 * v_ref[...] + (1.0 - ADAM_B2) * jnp.square(gg)
        m_hat = mn / (1.0 - ADAM_B1 ** ADAM_STEP)
        v_hat = vn / (1.0 - ADAM_B2 ** ADAM_STEP)
        d_ref[...] = -ADAM_LR * (m_hat / (jnp.sqrt(v_hat) + ADAM_EPS) + ADAM_WD * w_ref[...])
        mo_ref[...] = mn
        vo_ref[...] = vn

    spec = pl.BlockSpec((tm, last), lambda i: (i, 0))
    d, mn, vn = pl.pallas_call(
        body, name=name, grid=(rows // tm,), in_specs=[spec] * 4, out_specs=[spec] * 3,
        out_shape=[jax.ShapeDtypeStruct((rows, last), f32)] * 3,
        compiler_params=_cparams("parallel"),
    )(w2, g2, m2, v2)
    return d.reshape(shape), mn.reshape(shape), vn.reshape(shape)


def _adamw_land(name, land, w, m, v, tm=256):
    _, L, R, C = land.shape
    tm = min(tm, R)

    def body(l_ref, w_ref, m_ref, v_ref, g_ref, d_ref, mo_ref, vo_ref):
        gg = l_ref[0].astype(f32)
        for s in range(1, N_DEV):
            gg = gg + l_ref[s].astype(f32)
        g_ref[...] = gg
        mn = ADAM_B1 * m_ref[...] + (1.0 - ADAM_B1) * gg
        vn = ADAM_B2 * v_ref[...] + (1.0 - ADAM_B2) * jnp.square(gg)
        m_hat = mn / (1.0 - ADAM_B1 ** ADAM_STEP)
        v_hat = vn / (1.0 - ADAM_B2 ** ADAM_STEP)
        d_ref[...] = -ADAM_LR * (m_hat / (jnp.sqrt(v_hat) + ADAM_EPS) + ADAM_WD * w_ref[...])
        mo_ref[...] = mn
        vo_ref[...] = vn

    spec = pl.BlockSpec((None, tm, C), lambda l, i: (l, i, 0))
    return pl.pallas_call(
        body, name=name, grid=(L, R // tm),
        in_specs=[pl.BlockSpec((N_DEV, None, tm, C), lambda l, i: (0, l, i, 0))] + [spec] * 3,
        out_specs=[spec] * 4,
        out_shape=[jax.ShapeDtypeStruct((L, R, C), f32)] * 4,
        compiler_params=_cparams("parallel", "parallel"),
    )(land, w, m, v)


BIG = [("hyb_w_in", 2), ("hyb_w_out", 1), ("rec_w_in", 2), ("rec_w_out", 1), ("rec_w_a", 2), ("rec_w_x", 2),
       ("mlp_w1", 2), ("mlp_w2", 1)]
SMALL = [("hyb_conv_w", 2), ("rec_conv_w", 2), ("rec_conv_b", 1), ("rec_b_a", 1), ("rec_b_x", 1), ("rec_lambda", 1)]
REPL = ["hyb_sinks", "hyb_a_log", "hyb_dt_bias", "hyb_norm_w", "ln1_g", "ln1_b", "ln2_g", "ln2_b"]
WEIGHTS = ["hyb_w_in", "hyb_sinks", "hyb_conv_w", "hyb_a_log", "hyb_dt_bias", "hyb_norm_w", "hyb_w_out", "rec_w_in",
           "rec_conv_w", "rec_conv_b", "rec_w_a", "rec_b_a", "rec_w_x", "rec_b_x", "rec_lambda", "rec_w_out",
           "ln1_g", "ln1_b", "mlp_w1", "mlp_w2", "ln2_g", "ln2_b"]


def _pack_rows(parts, dtype, row_mult):
    lead = parts[0].shape[:-1]
    flat = jnp.concatenate([p.astype(dtype) for p in parts], axis=-1)
    n = flat.shape[-1]
    unit = row_mult * LANE
    pad = (-n) % unit
    if pad:
        flat = jnp.concatenate([flat, jnp.zeros(lead + (pad,), dtype)], axis=-1)
    return flat.reshape(lead + ((n + pad) // LANE, LANE))


def _gather_full(gathered, shard_shapes, table):
    flat = gathered.reshape(N_DEV, -1)
    out, off = {}, 0
    for name, ax in table:
        shp = shard_shapes[name]
        n = math.prod(shp)
        arr = flat[:, off:off + n].reshape((N_DEV,) + shp)
        off += n
        arr = jnp.moveaxis(arr, 0, ax)
        out[name] = arr.reshape(shp[:ax] + (N_DEV * shp[ax],) + shp[ax + 1:])
    return out


def _matmul_layouts(gw):
    out = {"hyb_w_in": _merge_cols("hyb_w_in_merge", gw["hyb_w_in"])}
    for k in ("hyb_w_out", "rec_w_out"):
        out[k] = jnp.swapaxes(gw[k], 0, 1).reshape(DEPTH // 2, D_MODEL, D_MODEL)
    bw = D_MODEL // LRU_BLOCKS
    for k in ("rec_w_a", "rec_w_x"):
        out[k] = jnp.moveaxis(gw[k], 0, 2).reshape(DEPTH // 2, LRU_BLOCKS, bw, bw)
    for k in ("rec_w_in", "mlp_w1", "mlp_w2"):
        out[k] = gw[k]
    return out


def kernel(x, hyb_w_in, hyb_sinks, hyb_conv_w, hyb_a_log, hyb_dt_bias, hyb_norm_w, hyb_w_out, rec_w_in, rec_conv_w, rec_conv_b, rec_w_a, rec_b_a, rec_w_x, rec_b_x, rec_lambda, rec_w_out, ln1_g, ln1_b, mlp_w1, mlp_w2, ln2_g, ln2_b, loss_target, m_hyb_w_in, m_hyb_sinks, m_hyb_conv_w, m_hyb_a_log, m_hyb_dt_bias, m_hyb_norm_w, m_hyb_w_out, m_rec_w_in, m_rec_conv_w, m_rec_conv_b, m_rec_w_a, m_rec_b_a, m_rec_w_x, m_rec_b_x, m_rec_lambda, m_rec_w_out, m_ln1_g, m_ln1_b, m_mlp_w1, m_mlp_w2, m_ln2_g, m_ln2_b, v_hyb_w_in, v_hyb_sinks, v_hyb_conv_w, v_hyb_a_log, v_hyb_dt_bias, v_hyb_norm_w, v_hyb_w_out, v_rec_w_in, v_rec_conv_w, v_rec_conv_b, v_rec_w_a, v_rec_b_a, v_rec_w_x, v_rec_b_x, v_rec_lambda, v_rec_w_out, v_ln1_g, v_ln1_b, v_mlp_w1, v_mlp_w2, v_ln2_g, v_ln2_b):
    args = locals()
    w = {k: args[k] for k in WEIGHTS}
    m = {k: args["m_" + k] for k in WEIGHTS}
    v = {k: args["v_" + k] for k in WEIGHTS}
    shard_shapes = {k: tuple(t.shape) for k, t in w.items()}
    xi, yi, ci = _my_coords()
    me = 4 * xi + 2 * yi + ci

    gathered = _all_gather("gather_w", [w[k].astype(bf16) for k, _ in BIG]
                           + [_pack_rows([w[k].reshape(-1) for k, _ in SMALL], f32, SUBLANE)])
    W = _gather_full(gathered[-1], shard_shapes, SMALL)
    W.update({k: w[k] for k in REPL})
    W.update(_matmul_layouts(dict(zip([k for k, _ in BIG], gathered[:-1]))))

    loss_local, grad_x, G = _local_step(x[0], loss_target[0], W)
    loss = lax.psum(loss_local, MESH_AXES)

    lands = _all_to_all("a2a_grads", [G[k] for k, _ in BIG])
    rest = _pack_rows([G[k].reshape(-1) for k, _ in SMALL] + [G[k].reshape(-1) for k in REPL], f32, SUBLANE)
    g_rest = _sum_blocks("sum_rest", _all_gather("gather_rest", [rest])[0]).reshape(-1)

    grads, delta, new_m, new_v = {}, {}, {}, {}
    for (k, _), land in zip(BIG, lands):
        shp = shard_shapes[k]
        s3 = (shp[0], math.prod(shp[1:-1]), shp[-1])
        res = _adamw_land("adamw_" + k, land.reshape((N_DEV,) + s3), w[k].reshape(s3), m[k].reshape(s3),
                          v[k].reshape(s3))
        grads[k], delta[k], new_m[k], new_v[k] = (r.reshape(shp) for r in res)
    off = 0
    for k, ax in SMALL:
        full_shape = G[k].shape
        n = math.prod(full_shape)
        full = g_rest[off:off + n].reshape(full_shape)
        off += n
        s = shard_shapes[k][ax]
        grads[k] = lax.dynamic_slice_in_dim(full, me * s, s, axis=ax)
    for k in REPL:
        n = math.prod(shard_shapes[k])
        grads[k] = g_rest[off:off + n].reshape(shard_shapes[k])
        off += n

    for k in [k for k, _ in SMALL] + REPL:
        delta[k], new_m[k], new_v[k] = _adamw("adamw_" + k, w[k], grads[k], m[k], v[k])

    return (loss, grad_x[None], *[grads[k] for k in WEIGHTS], *[delta[k] for k in WEIGHTS],
            *[new_m[k] for k in WEIGHTS], *[new_v[k] for k in WEIGHTS])
```

```python
import functools
import math

import jax
import jax.numpy as jnp
from jax import lax
from jax.experimental import pallas as pl
from jax.experimental.pallas import tpu as pltpu

f32 = jnp.float32
bf16 = jnp.bfloat16

N_DEV = 8
D_MODEL = 1024
DEPTH = 4
A_HEAD_DIM = 64
A_Q_HEADS = 8
WINDOW = 128
ROPE_THETA = 10000.0
B_HEADS = 4
B_HEAD_DIM = 128
B_CHUNK = 64
LRU_BLOCKS = 4
LRU_C = 8.0
D_FF = 4 * D_MODEL
HYB_PROJ = 2824
HYB_PROJ_PAD = 3072
DN_ALPHA = (2 * DEPTH) ** 0.25
LN_EPS = 1e-5
NORM_EPS = 1e-6
ADAM_LR = 0.001
ADAM_B1 = 0.9
ADAM_B2 = 0.999
ADAM_EPS = 1e-08
ADAM_WD = 0.01
ADAM_STEP = 10

LANE = 128
SUBLANE = 8
VMEM_LIMIT = 48 * 1024 * 1024

CB_QA, CB_KA, CB_VA, CB_CONV, CB_Z, CB_LG = 0, 4, 5, 6, 18, 22

MESH_AXES = ("x", "y", "c")


def _cparams(*sem):
    return pltpu.CompilerParams(dimension_semantics=sem, vmem_limit_bytes=VMEM_LIMIT)


def _dot(a, b, dims, precision=None):
    return lax.dot_general(a, b, (dims, ((), ())), preferred_element_type=f32, precision=precision)


NN = ((1,), (0,))
NT = ((1,), (1,))
TN = ((0,), (0,))


def _mat_spec(arr, kind, lead, br, bc, rb, cb):
    if kind == "plain":
        return pl.BlockSpec((br, bc), lambda i, j, k: (rb(i, j, k), cb(i, j, k)))
    if kind == "lead":
        return pl.BlockSpec((None, br, bc), lambda i, j, k: (lead, rb(i, j, k), cb(i, j, k)))
    if kind == "devcol":
        assert bc == arr.shape[-1]
        return pl.BlockSpec((None, None, br, bc), lambda i, j, k: (cb(i, j, k), lead, rb(i, j, k), 0))
    assert kind == "devrow" and br == arr.shape[-2]
    return pl.BlockSpec((None, None, br, bc), lambda i, j, k: (rb(i, j, k), lead, 0, cb(i, j, k)))


def _mm(name, a, b, mode, *, b_kind="plain", b_lead=0, o_kind="plain", epilogue=None, extras=(), out_dtypes=(f32,),
        tm=1024, tn=1024, tk=None):
    if tk is None:
        tk = 512 if mode == "tn" else 1024
    if b_kind in ("plain", "lead"):
        b_rows, b_cols = b.shape[-2:]
    elif b_kind == "devcol":
        b_rows, b_cols = b.shape[-2], N_DEV * b.shape[-1]
    else:
        b_rows, b_cols = N_DEV * b.shape[-2], b.shape[-1]
    if mode == "nn":
        (M, K), (K2, N) = a.shape, (b_rows, b_cols)
    elif mode == "nt":
        (M, K), (N, K2) = a.shape, (b_rows, b_cols)
    else:
        (K, M), (K2, N) = a.shape, (b_rows, b_cols)
    assert K == K2, (name, a.shape, b.shape, mode)
    tm, tn, tk = min(tm, M), min(tn, N), min(tk, K)
    cols_are_n = mode != "nt"
    if b_kind == "devcol":
        tn, tk = (b.shape[-1], tk) if cols_are_n else (tn, b.shape[-1])
    if b_kind == "devrow":
        tn, tk = (tn, b.shape[-2]) if cols_are_n else (b.shape[-2], tk)
    shard = N // N_DEV
    if o_kind == "devcol":
        tn = max(shard, tn // shard * shard)
    assert M % tm == 0 and N % tn == 0 and K % tk == 0, (name, M, N, K, tm, tn, tk)
    nk = K // tk
    dims = {"nn": NN, "nt": NT, "tn": TN}[mode]
    n_ex, n_out = len(extras), len(out_dtypes)

    def body(*refs):
        a_ref, b_ref = refs[:2]
        ex = refs[2:2 + n_ex]
        outs = refs[2 + n_ex:2 + n_ex + n_out]
        acc = refs[-1]
        k = pl.program_id(2)

        @pl.when(k == 0)
        def _():
            acc[...] = jnp.zeros_like(acc)

        acc[...] += _dot(a_ref[...].astype(bf16), b_ref[...].astype(bf16), dims)

        @pl.when(k == nk - 1)
        def _():
            r = acc[...]
            res = epilogue(r, *[e[...] for e in ex]) if epilogue is not None else (r,)
            for o, v in zip(outs, res):
                if o_kind == "plain":
                    o[...] = v.astype(o.dtype)
                else:
                    for q in range(tn // shard):
                        o[q] = v[:, q * shard:(q + 1) * shard].astype(o.dtype)

    if mode == "tn":
        a_spec = pl.BlockSpec((tk, tm), lambda i, j, k: (k, i))
    else:
        a_spec = pl.BlockSpec((tm, tk), lambda i, j, k: (i, k))
    jb, kb = (lambda i, j, k: j), (lambda i, j, k: k)
    if mode == "nt":
        b_spec = _mat_spec(b, b_kind, b_lead, tn, tk, jb, kb)
    else:
        b_spec = _mat_spec(b, b_kind, b_lead, tk, tn, kb, jb)
    e_spec = pl.BlockSpec((tm, tn), lambda i, j, k: (i, j))
    if o_kind == "plain":
        o_spec, o_shape = e_spec, (M, N)
    else:
        o_spec, o_shape = pl.BlockSpec((tn // shard, tm, shard), lambda i, j, k: (j, i, 0)), (N_DEV, M, shard)
    res = pl.pallas_call(
        body, name=name,
        grid=(M // tm, N // tn, nk),
        in_specs=[a_spec, b_spec] + [e_spec] * n_ex,
        out_specs=[o_spec] * n_out,
        out_shape=[jax.ShapeDtypeStruct(o_shape, dt) for dt in out_dtypes],
        scratch_shapes=[pltpu.VMEM((tm, tn), f32)],
        compiler_params=_cparams("parallel", "parallel", "arbitrary"),
    )(a, b, *extras)
    return res[0] if n_out == 1 else res


def _row_spec(tm, cb, width):
    assert (cb * LANE) % width == 0
    blk = (cb * LANE) // width
    return pl.BlockSpec((tm, width), lambda i: (i, blk))


def _whole_spec(p):
    nd = p.ndim
    return pl.BlockSpec(p.shape, lambda i: (0,) * nd)


def _tl_fwd(name, fn, rows, params, out_widths, out_dtypes, tm=256):
    T = rows[0][0].shape[0]
    tm = min(tm, T)
    nr, npar = len(rows), len(params)

    def body(*refs):
        vals = [r[...] for r in refs[:nr + npar]]
        outs = fn(*vals)
        for o, v in zip(refs[nr + npar:], outs):
            o[...] = v.astype(o.dtype)

    res = pl.pallas_call(
        body, name=name, grid=(T // tm,),
        in_specs=[_row_spec(tm, cb, w) for (_, cb, w) in rows] + [_whole_spec(p) for p in params],
        out_specs=[pl.BlockSpec((tm, w), lambda i: (i, 0)) for w in out_widths],
        out_shape=[jax.ShapeDtypeStruct((T, w), dt) for w, dt in zip(out_widths, out_dtypes)],
        compiler_params=_cparams("parallel"),
    )(*[r[0] for r in rows], *params)
    return res


def _tl_bwd(name, fn, rows, params, cot_rows, cot_fn=None, tm=256):
    T = rows[0][0].shape[0]
    tm = min(tm, T)
    nr, npar, nc = len(rows), len(params), len(cot_rows)

    def body(*refs):
        vals = [r[...] for r in refs[:nr + npar]]
        cots = [r[...] for r in refs[nr + npar:nr + npar + nc]]
        outs = refs[nr + npar + nc:]
        cot = tuple(cot_fn(*cots)) if cot_fn is not None else tuple(cots)
        _, vjp = jax.vjp(fn, *vals)
        grads = vjp(cot)
        for o, g in zip(outs[:nr], grads[:nr]):
            o[...] = g.astype(o.dtype)
        i = pl.program_id(0)
        for o, g in zip(outs[nr:], grads[nr:]):
            @pl.when(i == 0)
            def _(o=o):
                o[...] = jnp.zeros_like(o)
            o[...] += g

    res = pl.pallas_call(
        body, name=name, grid=(T // tm,),
        in_specs=[_row_spec(tm, cb, w) for (_, cb, w) in rows] + [_whole_spec(p) for p in params]
        + [_row_spec(tm, cb, w) for (_, cb, w) in cot_rows],
        out_specs=[pl.BlockSpec((tm, w), lambda i: (i, 0)) for (_, _, w) in rows] + [_whole_spec(p) for p in params],
        out_shape=[jax.ShapeDtypeStruct((T, w), f32) for (_, _, w) in rows]
        + [jax.ShapeDtypeStruct(p.shape, f32) for p in params],
        compiler_params=_cparams("arbitrary"),
    )(*[r[0] for r in rows], *params, *[r[0] for r in cot_rows])
    return res[:nr], res[nr:]


def _ln_res_fn(x, mix, g, b):
    pre = DN_ALPHA * x + mix
    mu = jnp.mean(pre, axis=-1, keepdims=True)
    var = jnp.mean(jnp.square(pre - mu), axis=-1, keepdims=True)
    return ((pre - mu) * lax.rsqrt(var + LN_EPS) * g + b,)


@jax.custom_jvp
def _expm1(x):
    small = jnp.abs(x) < 0.3
    xs = jnp.where(small, x, 0.0)
    poly = xs * (1.0 + xs * (1 / 2 + xs * (1 / 6 + xs * (1 / 24 + xs * (1 / 120 + xs * (
        1 / 720 + xs * (1 / 5040 + xs * (1 / 40320 + xs * (1 / 362880)))))))))
    return jnp.where(small, poly, jnp.exp(x) - 1.0)


@_expm1.defjvp
def _expm1_jvp(primals, tangents):
    (x,), (t,) = primals, tangents
    return _expm1(x), t * jnp.exp(x)


def _rglru_pre_fn(pre_r, pre_i, xc, b_a, b_x, lam):
    r = jax.nn.sigmoid(pre_r + b_a)
    i = jax.nn.sigmoid(pre_i + b_x)
    log_a = -LRU_C * r * jax.nn.softplus(-lam)
    a = jnp.exp(log_a)
    b = jnp.sqrt(-_expm1(2.0 * log_a)) * (i * xc)
    return a, b


def _rec_gate_fn(h, gate):
    return (h * jax.nn.gelu(gate),)


def _loss_head(y, t, tm=256):
    T, Dm = y.shape

    def body(y_ref, t_ref, dy_ref, loss_ref):
        e = y_ref[...] - t_ref[...]
        dy_ref[...] = e * (1.0 / Dm)

        @pl.when(pl.program_id(0) == 0)
        def _():
            loss_ref[...] = jnp.zeros_like(loss_ref)

        loss_ref[...] += 0.5 * jnp.sum(jnp.mean(e * e, axis=-1, keepdims=True), axis=0, keepdims=True)

    dy, loss = pl.pallas_call(
        body, name="loss_head", grid=(T // tm,),
        in_specs=[pl.BlockSpec((tm, Dm), lambda i: (i, 0))] * 2,
        out_specs=[pl.BlockSpec((tm, Dm), lambda i: (i, 0)), pl.BlockSpec((SUBLANE, LANE), lambda i: (0, 0))],
        out_shape=[jax.ShapeDtypeStruct((T, Dm), f32), jax.ShapeDtypeStruct((SUBLANE, LANE), f32)],
        compiler_params=_cparams("arbitrary"),
    )(y, t)
    return loss[0, 0], dy


def _conv_fwd(name, x, cb0, nblk, w, bias, tm=512):
    T = x.shape[0]
    tm = min(tm, T)
    hb = tm // SUBLANE
    has_b = bias is not None

    def body(*refs):
        cur, prev, w_ref = refs[:3]
        b_ref = refs[3] if has_b else None
        o = refs[-1]
        i = pl.program_id(1)
        p = jnp.where(i > 0, prev[...], 0.0)
        xcat = jnp.concatenate([p, cur[...]], axis=0)
        acc = cur[...] * w_ref[3:4, :]
        for j in range(3):
            acc = acc + pltpu.roll(xcat, 3 - j, axis=0)[SUBLANE:] * w_ref[j:j + 1, :]
        if has_b:
            acc = acc + b_ref[...]
        o[...] = acc

    in_specs = [
        pl.BlockSpec((tm, LANE), lambda c, i: (i, cb0 + c)),
        pl.BlockSpec((SUBLANE, LANE), lambda c, i: (jnp.maximum(i * hb - 1, 0), cb0 + c)),
        pl.BlockSpec((4, LANE), lambda c, i: (0, c)),
    ]
    args = [x, x, w]
    if has_b:
        in_specs.append(pl.BlockSpec((1, LANE), lambda c, i: (0, c)))
        args.append(bias)
    return pl.pallas_call(
        body, name=name, grid=(nblk, T // tm),
        in_specs=in_specs,
        out_specs=pl.BlockSpec((tm, LANE), lambda c, i: (i, c)),
        out_shape=jax.ShapeDtypeStruct((T, nblk * LANE), f32),
        compiler_params=_cparams("parallel", "parallel"),
    )(*args)


def _conv_bwd(name, dy, x, cb0, nblk, w, tm=512):
    T = x.shape[0]
    tm = min(tm, T)
    hb = tm // SUBLANE
    nt = T // tm

    def body(dcur, dnext, xcur, xprev, w_ref, dx_ref, dw_ref, db_ref):
        i = pl.program_id(1)
        d = dcur[...]
        dn = jnp.where(i < nt - 1, dnext[...], 0.0)
        dcat = jnp.concatenate([d, dn], axis=0)
        acc = d * w_ref[3:4, :]
        for j in range(3):
            s = 3 - j
            acc = acc + pltpu.roll(dcat, tm + SUBLANE - s, axis=0)[:tm] * w_ref[j:j + 1, :]
        dx_ref[...] = acc

        p = jnp.where(i > 0, xprev[...], 0.0)
        xcat = jnp.concatenate([p, xcur[...]], axis=0)
        rows = [jnp.sum(d * pltpu.roll(xcat, 3 - j, axis=0)[SUBLANE:], axis=0, keepdims=True) for j in range(3)]
        rows.append(jnp.sum(d * xcur[...], axis=0, keepdims=True))
        rows.append(jnp.zeros((SUBLANE - 4, LANE), f32))

        @pl.when(i == 0)
        def _():
            dw_ref[...] = jnp.zeros_like(dw_ref)
            db_ref[...] = jnp.zeros_like(db_ref)

        dw_ref[...] += jnp.concatenate(rows, axis=0)
        db_ref[...] += jnp.broadcast_to(jnp.sum(d, axis=0, keepdims=True), (SUBLANE, LANE))

    nh = T // SUBLANE
    dx, dw, db = pl.pallas_call(
        body, name=name, grid=(nblk, nt),
        in_specs=[
            pl.BlockSpec((tm, LANE), lambda c, i: (i, c)),
            pl.BlockSpec((SUBLANE, LANE), lambda c, i: (jnp.minimum((i + 1) * hb, nh - 1), c)),
            pl.BlockSpec((tm, LANE), lambda c, i: (i, cb0 + c)),
            pl.BlockSpec((SUBLANE, LANE), lambda c, i: (jnp.maximum(i * hb - 1, 0), cb0 + c)),
            pl.BlockSpec((4, LANE), lambda c, i: (0, c)),
        ],
        out_specs=[
            pl.BlockSpec((tm, LANE), lambda c, i: (i, c)),
            pl.BlockSpec((SUBLANE, LANE), lambda c, i: (0, c)),
            pl.BlockSpec((SUBLANE, LANE), lambda c, i: (0, c)),
        ],
        out_shape=[jax.ShapeDtypeStruct((T, nblk * LANE), f32),
                   jax.ShapeDtypeStruct((SUBLANE, nblk * LANE), f32),
                   jax.ShapeDtypeStruct((SUBLANE, nblk * LANE), f32)],
        compiler_params=_cparams("parallel", "arbitrary"),
    )(dy, dy, x, x, w)
    return dx, dw[:4], db[0]


@functools.partial(jax.custom_vjp, nondiff_argnums=(1,))
def _lroll(x, s):
    return pltpu.roll(x, s, axis=1)


def _lroll_fwd(x, s):
    return _lroll(x, s), None


def _lroll_bwd(s, _, g):
    return (_lroll(g, (LANE - s) % LANE),)


_lroll.defvjp(_lroll_fwd, _lroll_bwd)


def _rope_tables(T):
    half = A_HEAD_DIM // 2
    inv_freq = ROPE_THETA ** (-jnp.arange(half, dtype=f32) / half)
    ang = jnp.arange(T, dtype=f32)[:, None] * inv_freq[None, :]
    cos, sin = jnp.cos(ang), jnp.sin(ang)
    return jnp.tile(jnp.concatenate([cos, cos], axis=1), (1, 2)), jnp.tile(jnp.concatenate([-sin, sin], axis=1), (1, 2))


def _attn_block_fn(n, q, kp, kc, vp, vc, cq, sq, cp, sp, sinks):
    W = WINDOW
    lane = lax.broadcasted_iota(jnp.int32, (W, LANE), 1)
    lo_half = (lane % A_HEAD_DIM) < (A_HEAD_DIM // 2)
    lane8 = lax.broadcasted_iota(jnp.int32, sinks.shape, 1)

    def rope(x, c, s):
        return x * c + jnp.where(lo_half, _lroll(x, LANE - A_HEAD_DIM // 2), _lroll(x, A_HEAD_DIM // 2)) * s

    k2 = jnp.concatenate([rope(kp, cp, sp), rope(kc, cq, sq)], axis=0).astype(bf16)
    v2 = jnp.concatenate([vp, vc], axis=0).astype(bf16)
    row = lax.broadcasted_iota(jnp.int32, (W, 2 * W), 0)
    col = lax.broadcasted_iota(jnp.int32, (W, 2 * W), 1)
    dist = row + W - col
    mask = (dist >= 0) & (dist < W) & ((col >= W) | (n > 0))
    outs = []
    for t in range(4):
        qt = rope(q[:, LANE * t:LANE * (t + 1)], cq, sq)
        g = t // 2
        ot = jnp.zeros((W, LANE), f32)
        for hh in range(2):
            qa = jnp.where((lane // A_HEAD_DIM) == hh, qt, 0.0)
            if hh != g:
                qa = _lroll(qa, A_HEAD_DIM)
            s = _dot(qa.astype(bf16), k2, NT) * (A_HEAD_DIM ** -0.5)
            s = jnp.where(mask, s, -jnp.inf)
            sink = jnp.sum(jnp.where(lane8 == 2 * t + hh, sinks, 0.0), axis=1, keepdims=True)
            m = jnp.maximum(jnp.max(s, axis=-1, keepdims=True), sink)
            e = jnp.exp(s - m)
            p = e / (jnp.sum(e, axis=-1, keepdims=True) + jnp.exp(sink - m))
            o = _dot(p.astype(bf16), v2, NN)
            o = jnp.where((lane // A_HEAD_DIM) == g, o, 0.0)
            if hh != g:
                o = _lroll(o, A_HEAD_DIM)
            ot = ot + o
        outs.append(ot)
    return jnp.concatenate(outs, axis=1)


def _attn_specs():
    W = WINDOW
    prev = lambda n: jnp.maximum(n - 1, 0)
    return [
        pl.BlockSpec((W, 4 * LANE), lambda n: (n, CB_QA // 4)),
        pl.BlockSpec((W, LANE), lambda n: (prev(n), CB_KA)),
        pl.BlockSpec((W, LANE), lambda n: (n, CB_KA)),
        pl.BlockSpec((W, LANE), lambda n: (prev(n), CB_VA)),
        pl.BlockSpec((W, LANE), lambda n: (n, CB_VA)),
        pl.BlockSpec((W, LANE), lambda n: (n, 0)),
        pl.BlockSpec((W, LANE), lambda n: (n, 0)),
        pl.BlockSpec((W, LANE), lambda n: (prev(n), 0)),
        pl.BlockSpec((W, LANE), lambda n: (prev(n), 0)),
        pl.BlockSpec((1, A_Q_HEADS), lambda n: (0, 0)),
    ]


def _attn_fwd(name, proj, cos, sin, sinks):
    T = proj.shape[0]
    W = WINDOW

    def body(*refs):
        o = refs[-1]
        o[...] = _attn_block_fn(pl.program_id(0), *[r[...] for r in refs[:-1]])

    return pl.pallas_call(
        body, name=name, grid=(T // W,),
        in_specs=_attn_specs(),
        out_specs=pl.BlockSpec((W, 4 * LANE), lambda n: (n, 0)),
        out_shape=jax.ShapeDtypeStruct((T, 4 * LANE), f32),
        compiler_params=_cparams("parallel"),
    )(proj, proj, proj, proj, proj, cos, sin, cos, sin, sinks)


def _attn_bwd(name, proj, cos, sin, sinks, d_oab):
    T = proj.shape[0]
    W = WINDOW

    def body(*refs):
        ins = [r[...] for r in refs[:10]]
        do = refs[10][...]
        dq_ref, dk_ref, dv_ref, ds_ref = refs[11:]
        n = pl.program_id(0)
        _, vjp = jax.vjp(functools.partial(_attn_block_fn, n), *ins)
        dq, dkp, dkc, dvp, dvc, _, _, _, _, dsk = vjp(do)
        dq_ref[...] = dq

        @pl.when(n == 0)
        def _():
            dk_ref[...] = jnp.zeros_like(dk_ref)
            dv_ref[...] = jnp.zeros_like(dv_ref)
            ds_ref[...] = jnp.zeros_like(ds_ref)

        cur = pl.ds(pl.multiple_of(n * W, W), W)
        dk_ref[cur, :] += dkc
        dv_ref[cur, :] += dvc
        ds_ref[...] += dsk

        @pl.when(n > 0)
        def _():
            prv = pl.ds(pl.multiple_of((n - 1) * W, W), W)
            dk_ref[prv, :] += dkp
            dv_ref[prv, :] += dvp

    return pl.pallas_call(
        body, name=name, grid=(T // W,),
        in_specs=_attn_specs() + [pl.BlockSpec((W, 4 * LANE), lambda n: (n, 0))],
        out_specs=[pl.BlockSpec((W, 4 * LANE), lambda n: (n, 0)),
                   pl.BlockSpec((T, LANE), lambda n: (0, 0)),
                   pl.BlockSpec((T, LANE), lambda n: (0, 0)),
                   pl.BlockSpec((1, A_Q_HEADS), lambda n: (0, 0))],
        out_shape=[jax.ShapeDtypeStruct((T, 4 * LANE), f32), jax.ShapeDtypeStruct((T, LANE), f32),
                   jax.ShapeDtypeStruct((T, LANE), f32), jax.ShapeDtypeStruct((1, A_Q_HEADS), f32)],
        compiler_params=_cparams("arbitrary"),
    )(proj, proj, proj, proj, proj, cos, sin, cos, sin, sinks, d_oab)


def _bdot(spec, a, b, precision=None):
    return jnp.einsum(spec, a, b, preferred_element_type=f32, precision=precision)


@jax.custom_vjp
def _tri_inv(a):
    C = a.shape[-1]
    r = lax.broadcasted_iota(jnp.int32, (C, C), 0)
    c = lax.broadcasted_iota(jnp.int32, (C, C), 1)
    t = jnp.broadcast_to(jnp.where(r == c, 1.0, 0.0).astype(f32), a.shape)
    for j in range(C - 1):
        t = t - a[:, :, j:j + 1] * t[:, j:j + 1, :]
    return t


def _tri_inv_fwd(a):
    t = _tri_inv(a)
    return t, t


def _tri_inv_bwd(t, g):
    C = t.shape[-1]
    r = lax.broadcasted_iota(jnp.int32, (C, C), 0)
    c = lax.broadcasted_iota(jnp.int32, (C, C), 1)
    x = _bdot("hki,hkj->hij", t, g, precision=lax.Precision.HIGHEST)
    y = _bdot("hik,hjk->hij", x, t, precision=lax.Precision.HIGHEST)
    return (jnp.where(r > c, -y, 0.0),)


_tri_inv.defvjp(_tri_inv_fwd, _tri_inv_bwd)


@jax.custom_vjp
def _tri_inv_saved(a, t):
    return t


_tri_inv_saved.defvjp(lambda a, t: (t, t), lambda t, g: (_tri_inv_bwd(t, g)[0], jnp.zeros_like(t)))


def _silu(x):
    return x * jax.nn.sigmoid(x)


def _l2n(x):
    return x * lax.rsqrt(jnp.sum(x * x, axis=-1, keepdims=True) + NORM_EPS)


def _delta_chunk_fn(cq, ck, cv, z, lg, a_log, dt_bias, norm_w, S, t_saved=None, want_t=False):
    C = B_CHUNK
    lane = lax.broadcasted_iota(jnp.int32, (C, LANE), 1)
    bl = jnp.stack([jnp.sum(jnp.where(lane == h, lg, 0.0), axis=1, keepdims=True) for h in range(B_HEADS)])
    al = jnp.stack([jnp.sum(jnp.where(lane == B_HEADS + h, lg, 0.0), axis=1, keepdims=True) for h in range(B_HEADS)])
    q = _l2n(_silu(cq)) * (B_HEAD_DIM ** -0.5)
    k = _l2n(_silu(ck))
    v = _silu(cv)
    beta = jax.nn.sigmoid(bl)
    g = -jnp.exp(a_log) * jax.nn.softplus(al + dt_bias)
    r = lax.broadcasted_iota(jnp.int32, (C, C), 0)
    c = lax.broadcasted_iota(jnp.int32, (C, C), 1)
    eye = r == c
    g_row = jnp.sum(jnp.where(eye, g, 0.0), axis=1, keepdims=True)
    gc = jnp.sum(jnp.where(c <= r, g_row, 0.0), axis=2, keepdims=True)
    gc_row = jnp.sum(jnp.where(eye, gc, 0.0), axis=1, keepdims=True)
    decay_incl = jnp.exp(jnp.where(r >= c, gc - gc_row, -jnp.inf))
    decay_strict = jnp.where(r > c, decay_incl, 0.0)
    kb = k * beta
    vb = v * beta
    kbf = k.astype(bf16)
    a_mat = _bdot("hik,hjk->hij", kb.astype(bf16), kbf) * decay_strict
    t_f32 = _tri_inv(a_mat) if t_saved is None else _tri_inv_saved(a_mat, t_saved)
    t_mat = t_f32.astype(bf16)
    eg = jnp.exp(gc)
    u = _bdot("hij,hjv->hiv", t_mat, vb.astype(bf16))
    w = _bdot("hij,hjk->hik", t_mat, (kb * eg).astype(bf16))
    qk = _bdot("hik,hjk->hij", q.astype(bf16), kbf) * decay_incl
    g_last = jnp.sum(g, axis=1, keepdims=True)
    k_tail = k * jnp.exp(g_last - gc)
    Sb = S.astype(bf16)
    v_new = u - _bdot("hck,hkv->hcv", w.astype(bf16), Sb)
    o = _bdot("hck,hkv->hcv", (q * eg).astype(bf16), Sb) + _bdot("hij,hjv->hiv", qk.astype(bf16), v_new.astype(bf16))
    S_new = S * jnp.exp(g_last) + _bdot("hck,hcv->hkv", k_tail.astype(bf16), v_new.astype(bf16))
    ob = o * lax.rsqrt(jnp.mean(o * o, axis=-1, keepdims=True) + NORM_EPS) * norm_w
    return (ob * _silu(z), S_new) + ((t_f32,) if want_t else ())


def _delta_in_specs(rev, N):
    C = B_CHUNK
    ix = (lambda n: N - 1 - n) if rev else (lambda n: n)
    specs = [pl.BlockSpec((C, 3 * B_HEADS * LANE), lambda n: (ix(n), 0))]
    specs += [pl.BlockSpec((C, LANE), lambda n, h=h: (ix(n), CB_Z + h)) for h in range(B_HEADS)]
    specs += [
        pl.BlockSpec((C, LANE), lambda n: (ix(n), CB_LG)),
        pl.BlockSpec((B_HEADS, 1, 1), lambda n: (0, 0, 0)),
        pl.BlockSpec((B_HEADS, 1, 1), lambda n: (0, 0, 0)),
        pl.BlockSpec((1, LANE), lambda n: (0, 0)),
    ]
    return specs


def _delta_inputs(c_ref, z_refs, lg, al, dt, nw):
    H = B_HEADS
    part = lambda p: jnp.stack([c_ref[:, LANE * (p * H + h):LANE * (p * H + h + 1)] for h in range(H)])
    return (part(0), part(1), part(2), jnp.stack([z[...] for z in z_refs]), lg[...], al[...], dt[...], nw[...])


def _delta_fwd(name, c, proj, a_log, dt_bias, norm_w):
    T = c.shape[0]
    C = B_CHUNK
    N = T // C
    Dh = B_HEAD_DIM
    H = B_HEADS

    def body(*refs):
        c_ref, z_refs, (lg, al, dt, nw) = refs[0], refs[1:1 + H], refs[1 + H:5 + H]
        o_ref, s_ref, t_ref, S = refs[5 + H:]

        @pl.when(pl.program_id(0) == 0)
        def _():
            S[...] = jnp.zeros_like(S)

        s0 = S[...]
        s_ref[...] = s0
        ob, s1, t = _delta_chunk_fn(*_delta_inputs(c_ref, z_refs, lg, al, dt, nw), s0, want_t=True)
        for h in range(H):
            o_ref[:, LANE * h:LANE * (h + 1)] = ob[h]
        t_ref[...] = t
        S[...] = s1

    return pl.pallas_call(
        body, name=name, grid=(N,),
        in_specs=_delta_in_specs(False, N),
        out_specs=[pl.BlockSpec((C, H * LANE), lambda n: (n, 0)),
                   pl.BlockSpec((H, None, Dh, Dh), lambda n: (0, n, 0, 0)),
                   pl.BlockSpec((H, None, C, C), lambda n: (0, n, 0, 0))],
        out_shape=[jax.ShapeDtypeStruct((T, H * Dh), f32), jax.ShapeDtypeStruct((H, N, Dh, Dh), f32),
                   jax.ShapeDtypeStruct((H, N, C, C), f32)],
        scratch_shapes=[pltpu.VMEM((H, Dh, Dh), f32)],
        compiler_params=_cparams("arbitrary"),
    )(c, *([proj] * H), proj, a_log, dt_bias, norm_w)


def _delta_bwd(name, c, proj, a_log, dt_bias, norm_w, s_saved, t_saved, d_oab):
    T = c.shape[0]
    C = B_CHUNK
    N = T // C
    Dh = B_HEAD_DIM
    H = B_HEADS

    def body(*refs):
        c_ref, z_refs, (lg, al, dt, nw) = refs[0], refs[1:1 + H], refs[1 + H:5 + H]
        s_ref, t_ref, do_ref = refs[5 + H:8 + H]
        dc, dz, dlg, dal, ddt, dnw, dS = refs[8 + H:]

        @pl.when(pl.program_id(0) == 0)
        def _():
            dS[...] = jnp.zeros_like(dS)
            dal[...] = jnp.zeros_like(dal)
            ddt[...] = jnp.zeros_like(ddt)
            dnw[...] = jnp.zeros_like(dnw)

        _, vjp = jax.vjp(functools.partial(_delta_chunk_fn, t_saved=t_ref[...]),
                         *_delta_inputs(c_ref, z_refs, lg, al, dt, nw), s_ref[...])
        do = jnp.stack([do_ref[:, LANE * h:LANE * (h + 1)] for h in range(H)])
        g = vjp((do, dS[...]))
        for h in range(H):
            for p in range(3):
                dc[:, LANE * (p * H + h):LANE * (p * H + h + 1)] = g[p][h]
            dz[:, LANE * h:LANE * (h + 1)] = g[3][h]
        dlg[...] = g[4]
        dal[...] += g[5]
        ddt[...] += g[6]
        dnw[...] += g[7]
        dS[...] = g[8]

    rn = lambda n: N - 1 - n
    return pl.pallas_call(
        body, name=name, grid=(N,),
        in_specs=_delta_in_specs(True, N) + [
            pl.BlockSpec((H, None, Dh, Dh), lambda n: (0, rn(n), 0, 0)),
            pl.BlockSpec((H, None, C, C), lambda n: (0, rn(n), 0, 0)),
            pl.BlockSpec((C, H * LANE), lambda n: (rn(n), 1)),
        ],
        out_specs=[
            pl.BlockSpec((C, 3 * H * LANE), lambda n: (rn(n), 0)),
            pl.BlockSpec((C, H * LANE), lambda n: (rn(n), 0)),
            pl.BlockSpec((C, LANE), lambda n: (rn(n), 0)),
            pl.BlockSpec((H, 1, 1), lambda n: (0, 0, 0)),
            pl.BlockSpec((H, 1, 1), lambda n: (0, 0, 0)),
            pl.BlockSpec((1, LANE), lambda n: (0, 0)),
        ],
        out_shape=[jax.ShapeDtypeStruct((T, 3 * H * Dh), f32), jax.ShapeDtypeStruct((T, H * Dh), f32),
                   jax.ShapeDtypeStruct((T, LANE), f32), jax.ShapeDtypeStruct((H, 1, 1), f32),
                   jax.ShapeDtypeStruct((H, 1, 1), f32), jax.ShapeDtypeStruct((1, LANE), f32)],
        scratch_shapes=[pltpu.VMEM((H, Dh, Dh), f32)],
        compiler_params=_cparams("arbitrary"),
    )(c, *([proj] * H), proj, a_log, dt_bias, norm_w, s_saved, t_saved, d_oab)


def _blockdiag_fwd(name, xc, w_a, w_x, tm=512):
    T, Wd = xc.shape
    bw = Wd // LRU_BLOCKS
    tm = min(tm, T)

    def body(x_ref, wa_ref, wx_ref, oa, ox):
        xb = x_ref[...].astype(bf16)
        oa[...] = _dot(xb, wa_ref[...].astype(bf16), NN)
        ox[...] = _dot(xb, wx_ref[...].astype(bf16), NN)

    xs = pl.BlockSpec((tm, bw), lambda i, h: (i, h))
    ws = pl.BlockSpec((None, bw, bw), lambda i, h: (h, 0, 0))
    return pl.pallas_call(
        body, name=name, grid=(T // tm, LRU_BLOCKS), in_specs=[xs, ws, ws], out_specs=[xs, xs],
        out_shape=[jax.ShapeDtypeStruct((T, Wd), f32)] * 2,
        compiler_params=_cparams("parallel", "parallel"),
    )(xc, w_a, w_x)


def _blockdiag_bwd_dx(name, dpr, dpi, w_a, w_x, addend, tm=512):
    T, Wd = dpr.shape
    bw = Wd // LRU_BLOCKS
    tm = min(tm, T)

    def body(dr, di, wa_ref, wx_ref, add, o):
        o[...] = (add[...] + _dot(dr[...].astype(bf16), wa_ref[...].astype(bf16), NT)
                  + _dot(di[...].astype(bf16), wx_ref[...].astype(bf16), NT))

    xs = pl.BlockSpec((tm, bw), lambda i, h: (i, h))
    ws = pl.BlockSpec((None, bw, bw), lambda i, h: (h, 0, 0))
    return pl.pallas_call(
        body, name=name, grid=(T // tm, LRU_BLOCKS), in_specs=[xs, xs, ws, ws, xs], out_specs=xs,
        out_shape=jax.ShapeDtypeStruct((T, Wd), f32),
        compiler_params=_cparams("parallel", "parallel"),
    )(dpr, dpi, w_a, w_x, addend)


def _blockdiag_bwd_dw(name, xc, dpr, dpi, tk=512):
    T, Wd = xc.shape
    bw = Wd // LRU_BLOCKS
    tk = min(tk, T)

    def body(x_ref, dr, di, oa, ox):
        @pl.when(pl.program_id(1) == 0)
        def _():
            oa[...] = jnp.zeros_like(oa)
            ox[...] = jnp.zeros_like(ox)

        xb = x_ref[...].astype(bf16)
        oa[...] += _dot(xb, dr[...].astype(bf16), TN)
        ox[...] += _dot(xb, di[...].astype(bf16), TN)

    xs = pl.BlockSpec((tk, bw), lambda h, k: (k, h))
    ws = pl.BlockSpec((None, bw, bw), lambda h, k: (h, 0, 0))
    return pl.pallas_call(
        body, name=name, grid=(LRU_BLOCKS, T // tk), in_specs=[xs, xs, xs], out_specs=[ws, ws],
        out_shape=[jax.ShapeDtypeStruct((LRU_BLOCKS, bw, bw), f32)] * 2,
        compiler_params=_cparams("parallel", "arbitrary"),
    )(xc, dpr, dpi)


def _scan(name, a, b, reverse, tt=512, cb=512):
    T, Wd = a.shape
    tt, cb = min(tt, T), min(cb, Wd)
    nt = T // tt
    ng = tt // SUBLANE

    def body(a_ref, b_ref, o_ref, carry):
        @pl.when(pl.program_id(1) == 0)
        def _():
            carry[...] = jnp.zeros_like(carry)

        row = lax.broadcasted_iota(jnp.int32, (SUBLANE, cb), 0)

        def step(gi, hp):
            g = (ng - 1 - gi) if reverse else gi
            off = pl.multiple_of(g * SUBLANE, SUBLANE)
            A = a_ref[pl.ds(off, SUBLANE), :]
            B = b_ref[pl.ds(off, SUBLANE), :]
            for s in (1, 2, 4):
                sh = (SUBLANE - s) if reverse else s
                As = pltpu.roll(A, sh, axis=0)
                Bs = pltpu.roll(B, sh, axis=0)
                valid = (row < SUBLANE - s) if reverse else (row >= s)
                B = jnp.where(valid, A * Bs + B, B)
                A = jnp.where(valid, A * As, A)
            hcur = A * hp + B
            o_ref[pl.ds(off, SUBLANE), :] = hcur
            edge = hcur[0:1, :] if reverse else hcur[SUBLANE - 1:SUBLANE, :]
            return jnp.broadcast_to(edge, (SUBLANE, cb))

        carry[...] = lax.fori_loop(0, ng, step, carry[...])

    ti = (lambda c, i: (nt - 1 - i, c)) if reverse else (lambda c, i: (i, c))
    spec = pl.BlockSpec((tt, cb), ti)
    return pl.pallas_call(
        body, name=name, grid=(Wd // cb, nt), in_specs=[spec, spec], out_specs=spec,
        out_shape=jax.ShapeDtypeStruct((T, Wd), f32),
        scratch_shapes=[pltpu.VMEM((SUBLANE, cb), f32)],
        compiler_params=_cparams("parallel", "arbitrary"),
    )(a, b)


def _relu2_epilogue(r):
    h = jnp.maximum(r, 0.0)
    return r, h * h


def _drelu2_epilogue(r, a):
    return (r * (2.0 * jnp.maximum(a, 0.0)),)


def _add_epilogue(r, e):
    return (r + e,)


def _merge_cols(name, g, tm=256):
    _, L, R, s = g.shape

    def body(g_ref, o_ref):
        for d in range(N_DEV):
            o_ref[:, s * d:s * (d + 1)] = g_ref[d].astype(bf16)
        o_ref[:, N_DEV * s:] = jnp.zeros((tm, HYB_PROJ_PAD - N_DEV * s), bf16)

    return pl.pallas_call(
        body, name=name, grid=(L, R // tm),
        in_specs=[pl.BlockSpec((N_DEV, None, tm, s), lambda l, i: (0, l, i, 0))],
        out_specs=pl.BlockSpec((None, tm, HYB_PROJ_PAD), lambda l, i: (l, i, 0)),
        out_shape=jax.ShapeDtypeStruct((L, R, HYB_PROJ_PAD), bf16),
        compiler_params=_cparams("parallel", "parallel"),
    )(g)


def _split_cols(name, dw, tm=256):
    R = dw.shape[0]
    s = HYB_PROJ // N_DEV

    def body(g_ref, o_ref):
        for d in range(N_DEV):
            o_ref[d] = g_ref[:, s * d:s * (d + 1)].astype(bf16)

    return pl.pallas_call(
        body, name=name, grid=(R // tm,),
        in_specs=[pl.BlockSpec((tm, HYB_PROJ_PAD), lambda i: (i, 0))],
        out_specs=pl.BlockSpec((N_DEV, tm, s), lambda i: (0, i, 0)),
        out_shape=jax.ShapeDtypeStruct((N_DEV, R, s), bf16),
        compiler_params=_cparams("parallel"),
    )(dw)


def _rows_to_dev(dw):
    nb, r, c = dw.shape
    t = dw.reshape(nb, N_DEV, r // N_DEV, c)
    return jnp.moveaxis(t, 1, 0).reshape(N_DEV, nb * (r // N_DEV), c).astype(bf16)


def _hybrid_fwd(tag, x, W, j, cos, sin):
    proj = _mm(f"{tag}_proj", x, W["hyb_w_in"], "nn", b_kind="lead", b_lead=j)
    o_a = _attn_fwd(f"{tag}_attn", proj, cos, sin, W["hyb_sinks"][j][None, :])
    c = _conv_fwd(f"{tag}_conv", proj, CB_CONV, 12, W["hyb_conv_w"][j], None)
    o_b, s_saved, t_saved = _delta_fwd(f"{tag}_delta", c, proj, W["hyb_a_log"][j].reshape(B_HEADS, 1, 1),
                                       W["hyb_dt_bias"][j].reshape(B_HEADS, 1, 1), W["hyb_norm_w"][j][None, :])
    o_ab = jnp.concatenate([o_a, o_b], axis=1)
    mix = _mm(f"{tag}_out", o_ab, W["hyb_w_out"], "nn", b_kind="lead", b_lead=j)
    return mix, (proj, c, s_saved, t_saved, o_ab)


def _hybrid_bwd(tag, x, dmix, addend, W, j, cos, sin, saved, G):
    proj, c, s_saved, t_saved, o_ab = saved
    T = x.shape[0]
    d_oab = _mm(f"{tag}_dout", dmix, W["hyb_w_out"], "nt", b_kind="lead", b_lead=j)
    G["hyb_w_out"][j] = _mm(f"{tag}_dwout", o_ab, dmix, "tn", out_dtypes=(bf16,)).reshape(N_DEV, -1, D_MODEL)
    dq, dk, dv, dsinks = _attn_bwd(f"{tag}_dattn", proj, cos, sin, W["hyb_sinks"][j][None, :], d_oab)
    a_log = W["hyb_a_log"][j].reshape(B_HEADS, 1, 1)
    dt_bias = W["hyb_dt_bias"][j].reshape(B_HEADS, 1, 1)
    dc, dz, dlg, dal, ddt, dnw = _delta_bwd(f"{tag}_ddelta", c, proj, a_log, dt_bias, W["hyb_norm_w"][j][None, :],
                                            s_saved, t_saved, d_oab)
    dconv_in, dconv_w, _ = _conv_bwd(f"{tag}_dconv", dc, proj, CB_CONV, 12, W["hyb_conv_w"][j])
    dproj = jnp.concatenate([dq, dk, dv, dconv_in, dz, dlg,
                             jnp.zeros((T, HYB_PROJ_PAD - (CB_LG + 1) * LANE), f32)], axis=1)
    dx = _mm(f"{tag}_dx", dproj, W["hyb_w_in"], "nt", b_kind="lead", b_lead=j, epilogue=_add_epilogue,
             extras=(addend,))
    G["hyb_w_in"][j] = _split_cols(f"{tag}_dwin_split", _mm(f"{tag}_dwin", x, dproj, "tn"))
    G["hyb_sinks"][j] = dsinks[0]
    G["hyb_conv_w"][j] = dconv_w
    G["hyb_a_log"][j] = dal.reshape(B_HEADS)
    G["hyb_dt_bias"][j] = ddt.reshape(B_HEADS)
    G["hyb_norm_w"][j] = dnw[0]
    return dx


def _rec_fwd(tag, x, W, j):
    Wd = D_MODEL
    proj = _mm(f"{tag}_proj", x, W["rec_w_in"], "nn", b_kind="devcol", b_lead=j)
    xc = _conv_fwd(f"{tag}_conv", proj, 0, Wd // LANE, W["rec_conv_w"][j], W["rec_conv_b"][j][None, :])
    pre_r, pre_i = _blockdiag_fwd(f"{tag}_gates", xc, W["rec_w_a"][j], W["rec_w_x"][j])
    pars = [W["rec_b_a"][j][None, :], W["rec_b_x"][j][None, :], W["rec_lambda"][j][None, :]]
    a, b = _tl_fwd(f"{tag}_pre", _rglru_pre_fn, [(pre_r, 0, Wd), (pre_i, 0, Wd), (xc, 0, Wd)], pars, [Wd, Wd], [f32, f32])
    h = _scan(f"{tag}_scan", a, b, False)
    (hg,) = _tl_fwd(f"{tag}_gate", _rec_gate_fn, [(h, 0, Wd), (proj, Wd // LANE, Wd)], [], [Wd], [f32])
    mix = _mm(f"{tag}_out", hg, W["rec_w_out"], "nn", b_kind="lead", b_lead=j)
    return mix, (proj, xc, pre_r, pre_i, a, h, hg)


def _rec_bwd(tag, x, dmix, addend, W, j, saved, G):
    proj, xc, pre_r, pre_i, a, h, hg = saved
    Wd = D_MODEL
    dhg = _mm(f"{tag}_dout", dmix, W["rec_w_out"], "nt", b_kind="lead", b_lead=j)
    G["rec_w_out"][j] = _mm(f"{tag}_dwout", hg, dmix, "tn", out_dtypes=(bf16,)).reshape(N_DEV, -1, D_MODEL)
    (dh, dgate), _ = _tl_bwd(f"{tag}_dgate", _rec_gate_fn, [(h, 0, Wd), (proj, Wd // LANE, Wd)], [], [(dhg, 0, Wd)])
    a_next = jnp.concatenate([a[1:], jnp.zeros((1, Wd), f32)], axis=0)
    h_prev = jnp.concatenate([jnp.zeros((1, Wd), f32), h[:-1]], axis=0)
    lam_t = _scan(f"{tag}_dscan", a_next, dh, True)
    pars = [W["rec_b_a"][j][None, :], W["rec_b_x"][j][None, :], W["rec_lambda"][j][None, :]]
    (dpr, dpi, dxc1), (db_a, db_x, dlam) = _tl_bwd(
        f"{tag}_dpre", _rglru_pre_fn, [(pre_r, 0, Wd), (pre_i, 0, Wd), (xc, 0, Wd)], pars,
        [(lam_t, 0, Wd), (h_prev, 0, Wd)], cot_fn=lambda lt, hp: (lt * hp, lt))
    dxc = _blockdiag_bwd_dx(f"{tag}_dgates_dx", dpr, dpi, W["rec_w_a"][j], W["rec_w_x"][j], dxc1)
    dwa, dwx = _blockdiag_bwd_dw(f"{tag}_dgates_dw", xc, dpr, dpi)
    G["rec_w_a"][j], G["rec_w_x"][j] = _rows_to_dev(dwa), _rows_to_dev(dwx)
    dxr, dconv_w, dconv_b = _conv_bwd(f"{tag}_dconv", dxc, proj, 0, Wd // LANE, W["rec_conv_w"][j])
    dproj = jnp.concatenate([dxr, dgate], axis=1)
    dx = _mm(f"{tag}_dx", dproj, W["rec_w_in"], "nt", b_kind="devcol", b_lead=j, epilogue=_add_epilogue,
             extras=(addend,))
    G["rec_w_in"][j] = _mm(f"{tag}_dwin", x, dproj, "tn", o_kind="devcol", out_dtypes=(bf16,))
    G["rec_conv_w"][j] = dconv_w
    G["rec_conv_b"][j] = dconv_b
    G["rec_b_a"][j] = db_a[0]
    G["rec_b_x"][j] = db_x[0]
    G["rec_lambda"][j] = dlam[0]
    return dx


def _local_step(x, target, W):
    T = x.shape[0]
    cos, sin = _rope_tables(T)
    saved = []
    for layer in range(DEPTH):
        j = layer // 2
        tag = f"L{layer}"
        if layer % 2 == 0:
            mix, sv = _hybrid_fwd(tag, x, W, j, cos, sin)
        else:
            mix, sv = _rec_fwd(tag, x, W, j)
        ln1 = [W["ln1_g"][layer][None, :], W["ln1_b"][layer][None, :]]
        (x1,) = _tl_fwd(f"{tag}_ln1", _ln_res_fn, [(x, 0, D_MODEL), (mix, 0, D_MODEL)], ln1, [D_MODEL], [f32])
        a, h2 = _mm(f"{tag}_mlp1", x1, W["mlp_w1"], "nn", b_kind="devcol", b_lead=layer, epilogue=_relu2_epilogue,
                    out_dtypes=(f32, bf16))
        y = _mm(f"{tag}_mlp2", h2, W["mlp_w2"], "nn", b_kind="devrow", b_lead=layer)
        ln2 = [W["ln2_g"][layer][None, :], W["ln2_b"][layer][None, :]]
        (x2,) = _tl_fwd(f"{tag}_ln2", _ln_res_fn, [(x1, 0, D_MODEL), (y, 0, D_MODEL)], ln2, [D_MODEL], [f32])
        saved.append((x, sv, mix, x1, a, h2, y))
        x = x2
    loss, dx = _loss_head(x, target)

    G = {k: [None] * (DEPTH if k.startswith(("ln", "mlp")) else DEPTH // 2) for k in (
        "hyb_w_in", "hyb_sinks", "hyb_conv_w", "hyb_a_log", "hyb_dt_bias", "hyb_norm_w", "hyb_w_out",
        "rec_w_in", "rec_conv_w", "rec_conv_b", "rec_w_a", "rec_b_a", "rec_w_x", "rec_b_x", "rec_lambda", "rec_w_out",
        "ln1_g", "ln1_b", "mlp_w1", "mlp_w2", "ln2_g", "ln2_b")}
    for layer in reversed(range(DEPTH)):
        j = layer // 2
        tag = f"L{layer}"
        x0, sv, mix, x1, a, h2, y = saved[layer]
        ln2 = [W["ln2_g"][layer][None, :], W["ln2_b"][layer][None, :]]
        (dx1_a, dy), (dg2, db2) = _tl_bwd(f"{tag}_dln2", _ln_res_fn, [(x1, 0, D_MODEL), (y, 0, D_MODEL)], ln2,
                                          [(dx, 0, D_MODEL)])
        G["ln2_g"][layer], G["ln2_b"][layer] = dg2[0], db2[0]
        da = _mm(f"{tag}_dmlp2", dy, W["mlp_w2"], "nt", b_kind="devrow", b_lead=layer, epilogue=_drelu2_epilogue,
                 extras=(a,), out_dtypes=(bf16,))
        G["mlp_w2"][layer] = _mm(f"{tag}_dw2", h2, dy, "tn", out_dtypes=(bf16,)).reshape(N_DEV, -1, D_MODEL)
        dx1 = _mm(f"{tag}_dmlp1", da, W["mlp_w1"], "nt", b_kind="devcol", b_lead=layer, epilogue=_add_epilogue,
                  extras=(dx1_a,))
        G["mlp_w1"][layer] = _mm(f"{tag}_dw1", x1, da, "tn", o_kind="devcol", out_dtypes=(bf16,))
        ln1 = [W["ln1_g"][layer][None, :], W["ln1_b"][layer][None, :]]
        (dx0_a, dmix), (dg1, db1) = _tl_bwd(f"{tag}_dln1", _ln_res_fn, [(x0, 0, D_MODEL), (mix, 0, D_MODEL)], ln1,
                                            [(dx1, 0, D_MODEL)])
        G["ln1_g"][layer], G["ln1_b"][layer] = dg1[0], db1[0]
        if layer % 2 == 0:
            dx = _hybrid_bwd(tag, x0, dmix, dx0_a, W, j, cos, sin, sv, G)
        else:
            dx = _rec_bwd(tag, x0, dmix, dx0_a, W, j, sv, G)
    big = {k for k, _ in BIG}
    return loss, dx, {k: (v if k in big else jnp.stack(v)) for k, v in G.items()}


def _my_coords():
    return lax.axis_index("x"), lax.axis_index("y"), lax.axis_index("c")


def _all_gather(name, arrays):
    na = len(arrays)

    def body(*refs):
        x_refs, out_refs = refs[:na], refs[na:2 * na]
        send_sems, recv_sems, local_sems = refs[2 * na:]
        x, y, c = _my_coords()
        me, sibling = (x, y, c), (x, y, 1 - c)
        chips = [(1 - x, y), (x, 1 - y), (1 - x, 1 - y)]

        def blk(a, px, py, pc):
            return out_refs[a].at[4 * px + 2 * py + pc]

        def copy(a, k, block, to, src=None):
            return pltpu.make_async_remote_copy(
                src_ref=blk(a, *block) if src is None else src, dst_ref=blk(a, *block),
                send_sem=send_sems.at[a, k], recv_sem=recv_sems.at[a, k],
                device_id=to, device_id_type=pl.DeviceIdType.MESH)

        mine = [pltpu.make_async_copy(x_refs[a], blk(a, *me), local_sems.at[a]) for a in range(na)]
        for cp in mine:
            cp.start()
        first = []
        for a in range(na):
            first.append(copy(a, 0, me, sibling, src=x_refs[a]))
            first += [copy(a, 1 + j, me, (*chip, c), src=x_refs[a]) for j, chip in enumerate(chips)]
        for cp in first:
            cp.start()
        passed = []
        for a in range(na):
            for j, chip in enumerate(chips):
                copy(a, 1 + j, (*chip, c), me).wait_recv()
                passed.append(copy(a, 4 + j, (*chip, c), sibling))
                passed[-1].start()
        for a in range(na):
            copy(a, 0, sibling, me).wait_recv()
            for j, chip in enumerate(chips):
                copy(a, 4 + j, (*chip, 1 - c), me).wait_recv()
        for cp in first + passed:
            cp.wait_send()
        for cp in mine:
            cp.wait()

    return pl.pallas_call(
        body, name=name,
        out_shape=[jax.ShapeDtypeStruct((N_DEV,) + a.shape, a.dtype) for a in arrays],
        in_specs=[pl.BlockSpec(memory_space=pl.ANY)] * na,
        out_specs=[pl.BlockSpec(memory_space=pl.ANY)] * na,
        scratch_shapes=[pltpu.SemaphoreType.DMA((na, 7)), pltpu.SemaphoreType.DMA((na, 7)),
                        pltpu.SemaphoreType.DMA((na,))],
    )(*arrays)


def _all_to_all(name, groups):
    flat = [(p, l, arr) for p, layers in enumerate(groups) for l, arr in enumerate(layers)]
    na = len(flat)

    def body(*refs):
        g_refs, out_refs = refs[:na], refs[na:na + len(groups)]
        send_sems, recv_sems, local_sems = refs[na + len(groups):]
        x, y, c = _my_coords()
        me = 4 * x + 2 * y + c
        mine, copies = [], []
        for a, (p, l, _) in enumerate(flat):
            mine.append(pltpu.make_async_copy(g_refs[a].at[me], out_refs[p].at[me, l], local_sems.at[a]))
            for k in range(1, N_DEV):
                px = (1 - x) if (k >> 2) & 1 else x
                py = (1 - y) if (k >> 1) & 1 else y
                pc = (1 - c) if k & 1 else c
                copies.append(pltpu.make_async_remote_copy(
                    src_ref=g_refs[a].at[4 * px + 2 * py + pc], dst_ref=out_refs[p].at[me, l],
                    send_sem=send_sems.at[a, k - 1], recv_sem=recv_sems.at[a, k - 1],
                    device_id=(px, py, pc), device_id_type=pl.DeviceIdType.MESH))
        for cp in mine + copies:
            cp.start()
        for cp in copies:
            cp.wait_recv()
        for cp in copies:
            cp.wait_send()
        for cp in mine:
            cp.wait()

    return pl.pallas_call(
        body, name=name,
        out_shape=[jax.ShapeDtypeStruct((N_DEV, len(layers)) + layers[0].shape[1:], layers[0].dtype)
                   for layers in groups],
        in_specs=[pl.BlockSpec(memory_space=pl.ANY)] * na,
        out_specs=[pl.BlockSpec(memory_space=pl.ANY)] * len(groups),
        scratch_shapes=[pltpu.SemaphoreType.DMA((na, 7)), pltpu.SemaphoreType.DMA((na, 7)),
                        pltpu.SemaphoreType.DMA((na,))],
    )(*[arr for _, _, arr in flat])


def _sum_blocks(name, land):
    _, R, n = land.shape
    tr = R

    def body(l_ref, o_ref):
        acc = l_ref[0].astype(f32)
        for s in range(1, N_DEV):
            acc = acc + l_ref[s].astype(f32)
        o_ref[...] = acc

    return pl.pallas_call(
        body, name=name, grid=(R // tr,),
        in_specs=[pl.BlockSpec((N_DEV, tr, n), lambda i: (0, i, 0))],
        out_specs=pl.BlockSpec((tr, n), lambda i: (i, 0)),
        out_shape=jax.ShapeDtypeStruct((R, n), f32),
        compiler_params=_cparams("parallel"),
    )(land)


def _adamw(name, w, g, m, v):
    shape = w.shape
    last = shape[-1]
    rows = math.prod(shape[:-1])
    tm = 256 if rows % 256 == 0 and rows > 256 else rows
    w2, g2, m2, v2 = (t.reshape(rows, last) for t in (w, g, m, v))

    def body(w_ref, g_ref, m_ref, v_ref, d_ref, mo_ref, vo_ref):
        gg = g_ref[...]
        mn = ADAM_B1 * m_ref[...] + (1.0 - ADAM_B1) * gg
        vn = ADAM_B2 * v_ref[...] + (1.0 - ADAM_B2) * jnp.square(gg)
        m_hat = mn / (1.0 - ADAM_B1 ** ADAM_STEP)
        v_hat = vn / (1.0 - ADAM_B2 ** ADAM_STEP)
        d_ref[...] = -ADAM_LR * (m_hat / (jnp.sqrt(v_hat) + ADAM_EPS) + ADAM_WD * w_ref[...])
        mo_ref[...] = mn
        vo_ref[...] = vn

    spec = pl.BlockSpec((tm, last), lambda i: (i, 0))
    d, mn, vn = pl.pallas_call(
        body, name=name, grid=(rows // tm,), in_specs=[spec] * 4, out_specs=[spec] * 3,
        out_shape=[jax.ShapeDtypeStruct((rows, last), f32)] * 3,
        compiler_params=_cparams("parallel"),
    )(w2, g2, m2, v2)
    return d.reshape(shape), mn.reshape(shape), vn.reshape(shape)


def _adamw_land(name, land, w, m, v, tm=256):
    _, L, R, C = land.shape
    tm = min(tm, R)

    def body(l_ref, w_ref, m_ref, v_ref, g_ref, d_ref, mo_ref, vo_ref):
        gg = l_ref[0].astype(f32)
        for s in range(1, N_DEV):
            gg = gg + l_ref[s].astype(f32)
        g_ref[...] = gg
        mn = ADAM_B1 * m_ref[...] + (1.0 - ADAM_B1) * gg
        vn = ADAM_B2 * v_ref[...] + (1.0 - ADAM_B2) * jnp.square(gg)
        m_hat = mn / (1.0 - ADAM_B1 ** ADAM_STEP)
        v_hat = vn / (1.0 - ADAM_B2 ** ADAM_STEP)
        d_ref[...] = -ADAM_LR * (m_hat / (jnp.sqrt(v_hat) + ADAM_EPS) + ADAM_WD * w_ref[...])
        mo_ref[...] = mn
        vo_ref[...] = vn

    spec = pl.BlockSpec((None, tm, C), lambda l, i: (l, i, 0))
    return pl.pallas_call(
        body, name=name, grid=(L, R // tm),
        in_specs=[pl.BlockSpec((N_DEV, None, tm, C), lambda l, i: (0, l, i, 0))] + [spec] * 3,
        out_specs=[spec] * 4,
        out_shape=[jax.ShapeDtypeStruct((L, R, C), f32)] * 4,
        compiler_params=_cparams("parallel", "parallel"),
    )(land, w, m, v)


BIG = [("hyb_w_in", 2), ("hyb_w_out", 1), ("rec_w_in", 2), ("rec_w_out", 1), ("rec_w_a", 2), ("rec_w_x", 2),
       ("mlp_w1", 2), ("mlp_w2", 1)]
SMALL = [("hyb_conv_w", 2), ("rec_conv_w", 2), ("rec_conv_b", 1), ("rec_b_a", 1), ("rec_b_x", 1), ("rec_lambda", 1)]
REPL = ["hyb_sinks", "hyb_a_log", "hyb_dt_bias", "hyb_norm_w", "ln1_g", "ln1_b", "ln2_g", "ln2_b"]
WEIGHTS = ["hyb_w_in", "hyb_sinks", "hyb_conv_w", "hyb_a_log", "hyb_dt_bias", "hyb_norm_w", "hyb_w_out", "rec_w_in",
           "rec_conv_w", "rec_conv_b", "rec_w_a", "rec_b_a", "rec_w_x", "rec_b_x", "rec_lambda", "rec_w_out",
           "ln1_g", "ln1_b", "mlp_w1", "mlp_w2", "ln2_g", "ln2_b"]


def _pack_rows(parts, dtype, row_mult):
    lead = parts[0].shape[:-1]
    flat = jnp.concatenate([p.astype(dtype) for p in parts], axis=-1)
    n = flat.shape[-1]
    unit = row_mult * LANE
    pad = (-n) % unit
    if pad:
        flat = jnp.concatenate([flat, jnp.zeros(lead + (pad,), dtype)], axis=-1)
    return flat.reshape(lead + ((n + pad) // LANE, LANE))


def _gather_full(gathered, shard_shapes, table):
    flat = gathered.reshape(N_DEV, -1)
    out, off = {}, 0
    for name, ax in table:
        shp = shard_shapes[name]
        n = math.prod(shp)
        arr = flat[:, off:off + n].reshape((N_DEV,) + shp)
        off += n
        arr = jnp.moveaxis(arr, 0, ax)
        out[name] = arr.reshape(shp[:ax] + (N_DEV * shp[ax],) + shp[ax + 1:])
    return out


def _matmul_layouts(gw):
    out = {"hyb_w_in": _merge_cols("hyb_w_in_merge", gw["hyb_w_in"])}
    for k in ("hyb_w_out", "rec_w_out"):
        out[k] = jnp.swapaxes(gw[k], 0, 1).reshape(DEPTH // 2, D_MODEL, D_MODEL)
    bw = D_MODEL // LRU_BLOCKS
    for k in ("rec_w_a", "rec_w_x"):
        out[k] = jnp.moveaxis(gw[k], 0, 2).reshape(DEPTH // 2, LRU_BLOCKS, bw, bw)
    for k in ("rec_w_in", "mlp_w1", "mlp_w2"):
        out[k] = gw[k]
    return out


def kernel(x, hyb_w_in, hyb_sinks, hyb_conv_w, hyb_a_log, hyb_dt_bias, hyb_norm_w, hyb_w_out, rec_w_in, rec_conv_w, rec_conv_b, rec_w_a, rec_b_a, rec_w_x, rec_b_x, rec_lambda, rec_w_out, ln1_g, ln1_b, mlp_w1, mlp_w2, ln2_g, ln2_b, loss_target, m_hyb_w_in, m_hyb_sinks, m_hyb_conv_w, m_hyb_a_log, m_hyb_dt_bias, m_hyb_norm_w, m_hyb_w_out, m_rec_w_in, m_rec_conv_w, m_rec_conv_b, m_rec_w_a, m_rec_b_a, m_rec_w_x, m_rec_b_x, m_rec_lambda, m_rec_w_out, m_ln1_g, m_ln1_b, m_mlp_w1, m_mlp_w2, m_ln2_g, m_ln2_b, v_hyb_w_in, v_hyb_sinks, v_hyb_conv_w, v_hyb_a_log, v_hyb_dt_bias, v_hyb_norm_w, v_hyb_w_out, v_rec_w_in, v_rec_conv_w, v_rec_conv_b, v_rec_w_a, v_rec_b_a, v_rec_w_x, v_rec_b_x, v_rec_lambda, v_rec_w_out, v_ln1_g, v_ln1_b, v_mlp_w1, v_mlp_w2, v_ln2_g, v_ln2_b):
    args = locals()
    w = {k: args[k] for k in WEIGHTS}
    m = {k: args["m_" + k] for k in WEIGHTS}
    v = {k: args["v_" + k] for k in WEIGHTS}
    shard_shapes = {k: tuple(t.shape) for k, t in w.items()}
    xi, yi, ci = _my_coords()
    me = 4 * xi + 2 * yi + ci

    gathered = _all_gather("gather_w", [w[k].astype(bf16) for k, _ in BIG]
                           + [_pack_rows([w[k].reshape(-1) for k, _ in SMALL], f32, SUBLANE)])
    W = _gather_full(gathered[-1], shard_shapes, SMALL)
    W.update({k: w[k] for k in REPL})
    W.update(_matmul_layouts(dict(zip([k for k, _ in BIG], gathered[:-1]))))

    loss_local, grad_x, G = _local_step(x[0], loss_target[0], W)
    loss = lax.psum(loss_local, MESH_AXES)

    lands = _all_to_all("a2a_grads", [G[k] for k, _ in BIG])
    rest = _pack_rows([G[k].reshape(-1) for k, _ in SMALL] + [G[k].reshape(-1) for k in REPL], f32, SUBLANE)
    g_rest = _sum_blocks("sum_rest", _all_gather("gather_rest", [rest])[0]).reshape(-1)

    grads, delta, new_m, new_v = {}, {}, {}, {}
    for (k, _), land in zip(BIG, lands):
        shp = shard_shapes[k]
        s3 = (shp[0], math.prod(shp[1:-1]), shp[-1])
        res = _adamw_land("adamw_" + k, land.reshape((N_DEV,) + s3), w[k].reshape(s3), m[k].reshape(s3),
                          v[k].reshape(s3))
        grads[k], delta[k], new_m[k], new_v[k] = (r.reshape(shp) for r in res)
    off = 0
    for k, ax in SMALL:
        full_shape = G[k].shape
        n = math.prod(full_shape)
        full = g_rest[off:off + n].reshape(full_shape)
        off += n
        s = shard_shapes[k][ax]
        grads[k] = lax.dynamic_slice_in_dim(full, me * s, s, axis=ax)
    for k in REPL:
        n = math.prod(shard_shapes[k])
        grads[k] = g_rest[off:off + n].reshape(shard_shapes[k])
        off += n

    for k in [k for k, _ in SMALL] + REPL:
        delta[k], new_m[k], new_v[k] = _adamw("adamw_" + k, w[k], grads[k], m[k], v[k])

    return (loss, grad_x[None], *[grads[k] for k in WEIGHTS], *[delta[k] for k in WEIGHTS],
            *[new_m[k] for k in WEIGHTS], *[new_v[k] for k in WEIGHTS])
```

```python
import functools
import math

import jax
import jax.numpy as jnp
from jax import lax
from jax.experimental import pallas as pl
from jax.experimental.pallas import tpu as pltpu

f32 = jnp.float32
bf16 = jnp.bfloat16

N_DEV = 8
D_MODEL = 1024
DEPTH = 4
A_HEAD_DIM = 64
A_Q_HEADS = 8
WINDOW = 128
ROPE_THETA = 10000.0
B_HEADS = 4
B_HEAD_DIM = 128
B_CHUNK = 64
LRU_BLOCKS = 4
LRU_C = 8.0
D_FF = 4 * D_MODEL
HYB_PROJ = 2824
HYB_PROJ_PAD = 3072
DN_ALPHA = (2 * DEPTH) ** 0.25
LN_EPS = 1e-5
NORM_EPS = 1e-6
ADAM_LR = 0.001
ADAM_B1 = 0.9
ADAM_B2 = 0.999
ADAM_EPS = 1e-08
ADAM_WD = 0.01
ADAM_STEP = 10

LANE = 128
SUBLANE = 8
VMEM_LIMIT = 48 * 1024 * 1024

CB_QA, CB_KA, CB_VA, CB_CONV, CB_Z, CB_LG = 0, 4, 5, 6, 18, 22

MESH_AXES = ("x", "y", "c")


def _cparams(*sem):
    return pltpu.CompilerParams(dimension_semantics=sem, vmem_limit_bytes=VMEM_LIMIT)


def _dot(a, b, dims, precision=None):
    return lax.dot_general(a, b, (dims, ((), ())), preferred_element_type=f32, precision=precision)


NN = ((1,), (0,))
NT = ((1,), (1,))
TN = ((0,), (0,))


def _mat_spec(arr, kind, lead, br, bc, rb, cb):
    if kind == "plain":
        return pl.BlockSpec((br, bc), lambda i, j, k: (rb(i, j, k), cb(i, j, k)))
    if kind == "lead":
        return pl.BlockSpec((None, br, bc), lambda i, j, k: (lead, rb(i, j, k), cb(i, j, k)))
    if kind == "devcol":
        assert bc == arr.shape[-1]
        return pl.BlockSpec((None, None, br, bc), lambda i, j, k: (cb(i, j, k), lead, rb(i, j, k), 0))
    assert kind == "devrow" and br == arr.shape[-2]
    return pl.BlockSpec((None, None, br, bc), lambda i, j, k: (rb(i, j, k), lead, 0, cb(i, j, k)))


def _mm(name, a, b, mode, *, b_kind="plain", b_lead=0, o_kind="plain", epilogue=None, extras=(), out_dtypes=(f32,),
        tm=1024, tn=1024, tk=None):
    if tk is None:
        tk = 512 if mode == "tn" else 1024
    if b_kind in ("plain", "lead"):
        b_rows, b_cols = b.shape[-2:]
    elif b_kind == "devcol":
        b_rows, b_cols = b.shape[-2], N_DEV * b.shape[-1]
    else:
        b_rows, b_cols = N_DEV * b.shape[-2], b.shape[-1]
    if mode == "nn":
        (M, K), (K2, N) = a.shape, (b_rows, b_cols)
    elif mode == "nt":
        (M, K), (N, K2) = a.shape, (b_rows, b_cols)
    else:
        (K, M), (K2, N) = a.shape, (b_rows, b_cols)
    assert K == K2, (name, a.shape, b.shape, mode)
    tm, tn, tk = min(tm, M), min(tn, N), min(tk, K)
    cols_are_n = mode != "nt"
    if b_kind == "devcol":
        tn, tk = (b.shape[-1], tk) if cols_are_n else (tn, b.shape[-1])
    if b_kind == "devrow":
        tn, tk = (tn, b.shape[-2]) if cols_are_n else (b.shape[-2], tk)
    shard = N // N_DEV
    if o_kind == "devcol":
        tn = max(shard, tn // shard * shard)
    assert M % tm == 0 and N % tn == 0 and K % tk == 0, (name, M, N, K, tm, tn, tk)
    nk = K // tk
    dims = {"nn": NN, "nt": NT, "tn": TN}[mode]
    n_ex, n_out = len(extras), len(out_dtypes)

    def body(*refs):
        a_ref, b_ref = refs[:2]
        ex = refs[2:2 + n_ex]
        outs = refs[2 + n_ex:2 + n_ex + n_out]
        acc = refs[-1]
        k = pl.program_id(2)

        @pl.when(k == 0)
        def _():
            acc[...] = jnp.zeros_like(acc)

        acc[...] += _dot(a_ref[...].astype(bf16), b_ref[...].astype(bf16), dims)

        @pl.when(k == nk - 1)
        def _():
            r = acc[...]
            res = epilogue(r, *[e[...] for e in ex]) if epilogue is not None else (r,)
            for o, v in zip(outs, res):
                if o_kind == "plain":
                    o[...] = v.astype(o.dtype)
                else:
                    for q in range(tn // shard):
                        o[q] = v[:, q * shard:(q + 1) * shard].astype(o.dtype)

    if mode == "tn":
        a_spec = pl.BlockSpec((tk, tm), lambda i, j, k: (k, i))
    else:
        a_spec = pl.BlockSpec((tm, tk), lambda i, j, k: (i, k))
    jb, kb = (lambda i, j, k: j), (lambda i, j, k: k)
    if mode == "nt":
        b_spec = _mat_spec(b, b_kind, b_lead, tn, tk, jb, kb)
    else:
        b_spec = _mat_spec(b, b_kind, b_lead, tk, tn, kb, jb)
    e_spec = pl.BlockSpec((tm, tn), lambda i, j, k: (i, j))
    if o_kind == "plain":
        o_spec, o_shape = e_spec, (M, N)
    else:
        o_spec, o_shape = pl.BlockSpec((tn // shard, tm, shard), lambda i, j, k: (j, i, 0)), (N_DEV, M, shard)
    res = pl.pallas_call(
        body, name=name,
        grid=(M // tm, N // tn, nk),
        in_specs=[a_spec, b_spec] + [e_spec] * n_ex,
        out_specs=[o_spec] * n_out,
        out_shape=[jax.ShapeDtypeStruct(o_shape, dt) for dt in out_dtypes],
        scratch_shapes=[pltpu.VMEM((tm, tn), f32)],
        compiler_params=_cparams("parallel", "parallel", "arbitrary"),
    )(a, b, *extras)
    return res[0] if n_out == 1 else res


def _row_spec(tm, cb, width):
    assert (cb * LANE) % width == 0
    blk = (cb * LANE) // width
    return pl.BlockSpec((tm, width), lambda i: (i, blk))


def _whole_spec(p):
    nd = p.ndim
    return pl.BlockSpec(p.shape, lambda i: (0,) * nd)


def _tl_fwd(name, fn, rows, params, out_widths, out_dtypes, tm=256):
    T = rows[0][0].shape[0]
    tm = min(tm, T)
    nr, npar = len(rows), len(params)

    def body(*refs):
        vals = [r[...] for r in refs[:nr + npar]]
        outs = fn(*vals)
        for o, v in zip(refs[nr + npar:], outs):
            o[...] = v.astype(o.dtype)

    res = pl.pallas_call(
        body, name=name, grid=(T // tm,),
        in_specs=[_row_spec(tm, cb, w) for (_, cb, w) in rows] + [_whole_spec(p) for p in params],
        out_specs=[pl.BlockSpec((tm, w), lambda i: (i, 0)) for w in out_widths],
        out_shape=[jax.ShapeDtypeStruct((T, w), dt) for w, dt in zip(out_widths, out_dtypes)],
        compiler_params=_cparams("parallel"),
    )(*[r[0] for r in rows], *params)
    return res


def _tl_bwd(name, fn, rows, params, cot_rows, cot_fn=None, tm=256):
    T = rows[0][0].shape[0]
    tm = min(tm, T)
    nr, npar, nc = len(rows), len(params), len(cot_rows)

    def body(*refs):
        vals = [r[...] for r in refs[:nr + npar]]
        cots = [r[...] for r in refs[nr + npar:nr + npar + nc]]
        outs = refs[nr + npar + nc:]
        cot = tuple(cot_fn(*cots)) if cot_fn is not None else tuple(cots)
        _, vjp = jax.vjp(fn, *vals)
        grads = vjp(cot)
        for o, g in zip(outs[:nr], grads[:nr]):
            o[...] = g.astype(o.dtype)
        i = pl.program_id(0)
        for o, g in zip(outs[nr:], grads[nr:]):
            @pl.when(i == 0)
            def _(o=o):
                o[...] = jnp.zeros_like(o)
            o[...] += g

    res = pl.pallas_call(
        body, name=name, grid=(T // tm,),
        in_specs=[_row_spec(tm, cb, w) for (_, cb, w) in rows] + [_whole_spec(p) for p in params]
        + [_row_spec(tm, cb, w) for (_, cb, w) in cot_rows],
        out_specs=[pl.BlockSpec((tm, w), lambda i: (i, 0)) for (_, _, w) in rows] + [_whole_spec(p) for p in params],
        out_shape=[jax.ShapeDtypeStruct((T, w), f32) for (_, _, w) in rows]
        + [jax.ShapeDtypeStruct(p.shape, f32) for p in params],
        compiler_params=_cparams("arbitrary"),
    )(*[r[0] for r in rows], *params, *[r[0] for r in cot_rows])
    return res[:nr], res[nr:]


def _ln_res_fn(x, mix, g, b):
    pre = DN_ALPHA * x + mix
    mu = jnp.mean(pre, axis=-1, keepdims=True)
    var = jnp.mean(jnp.square(pre - mu), axis=-1, keepdims=True)
    return ((pre - mu) * lax.rsqrt(var + LN_EPS) * g + b,)


@jax.custom_jvp
def _expm1(x):
    small = jnp.abs(x) < 0.3
    xs = jnp.where(small, x, 0.0)
    poly = xs * (1.0 + xs * (1 / 2 + xs * (1 / 6 + xs * (1 / 24 + xs * (1 / 120 + xs * (
        1 / 720 + xs * (1 / 5040 + xs * (1 / 40320 + xs * (1 / 362880)))))))))
    return jnp.where(small, poly, jnp.exp(x) - 1.0)


@_expm1.defjvp
def _expm1_jvp(primals, tangents):
    (x,), (t,) = primals, tangents
    return _expm1(x), t * jnp.exp(x)


def _rglru_pre_fn(pre_r, pre_i, xc, b_a, b_x, lam):
    r = jax.nn.sigmoid(pre_r + b_a)
    i = jax.nn.sigmoid(pre_i + b_x)
    log_a = -LRU_C * r * jax.nn.softplus(-lam)
    a = jnp.exp(log_a)
    b = jnp.sqrt(-_expm1(2.0 * log_a)) * (i * xc)
    return a, b


def _rec_gate_fn(h, gate):
    return (h * jax.nn.gelu(gate),)


def _loss_head(y, t, tm=256):
    T, Dm = y.shape

    def body(y_ref, t_ref, dy_ref, loss_ref):
        e = y_ref[...] - t_ref[...]
        dy_ref[...] = e * (1.0 / Dm)

        @pl.when(pl.program_id(0) == 0)
        def _():
            loss_ref[...] = jnp.zeros_like(loss_ref)

        loss_ref[...] += 0.5 * jnp.sum(jnp.mean(e * e, axis=-1, keepdims=True), axis=0, keepdims=True)

    dy, loss = pl.pallas_call(
        body, name="loss_head", grid=(T // tm,),
        in_specs=[pl.BlockSpec((tm, Dm), lambda i: (i, 0))] * 2,
        out_specs=[pl.BlockSpec((tm, Dm), lambda i: (i, 0)), pl.BlockSpec((SUBLANE, LANE), lambda i: (0, 0))],
        out_shape=[jax.ShapeDtypeStruct((T, Dm), f32), jax.ShapeDtypeStruct((SUBLANE, LANE), f32)],
        compiler_params=_cparams("arbitrary"),
    )(y, t)
    return loss[0, 0], dy


def _conv_fwd(name, x, cb0, nblk, w, bias, tm=512):
    T = x.shape[0]
    tm = min(tm, T)
    hb = tm // SUBLANE
    has_b = bias is not None

    def body(*refs):
        cur, prev, w_ref = refs[:3]
        b_ref = refs[3] if has_b else None
        o = refs[-1]
        i = pl.program_id(1)
        p = jnp.where(i > 0, prev[...], 0.0)
        xcat = jnp.concatenate([p, cur[...]], axis=0)
        acc = cur[...] * w_ref[3:4, :]
        for j in range(3):
            acc = acc + pltpu.roll(xcat, 3 - j, axis=0)[SUBLANE:] * w_ref[j:j + 1, :]
        if has_b:
            acc = acc + b_ref[...]
        o[...] = acc

    in_specs = [
        pl.BlockSpec((tm, LANE), lambda c, i: (i, cb0 + c)),
        pl.BlockSpec((SUBLANE, LANE), lambda c, i: (jnp.maximum(i * hb - 1, 0), cb0 + c)),
        pl.BlockSpec((4, LANE), lambda c, i: (0, c)),
    ]
    args = [x, x, w]
    if has_b:
        in_specs.append(pl.BlockSpec((1, LANE), lambda c, i: (0, c)))
        args.append(bias)
    return pl.pallas_call(
        body, name=name, grid=(nblk, T // tm),
        in_specs=in_specs,
        out_specs=pl.BlockSpec((tm, LANE), lambda c, i: (i, c)),
        out_shape=jax.ShapeDtypeStruct((T, nblk * LANE), f32),
        compiler_params=_cparams("parallel", "parallel"),
    )(*args)


def _conv_bwd(name, dy, x, cb0, nblk, w, tm=512):
    T = x.shape[0]
    tm = min(tm, T)
    hb = tm // SUBLANE
    nt = T // tm

    def body(dcur, dnext, xcur, xprev, w_ref, dx_ref, dw_ref, db_ref):
        i = pl.program_id(1)
        d = dcur[...]
        dn = jnp.where(i < nt - 1, dnext[...], 0.0)
        dcat = jnp.concatenate([d, dn], axis=0)
        acc = d * w_ref[3:4, :]
        for j in range(3):
            s = 3 - j
            acc = acc + pltpu.roll(dcat, tm + SUBLANE - s, axis=0)[:tm] * w_ref[j:j + 1, :]
        dx_ref[...] = acc

        p = jnp.where(i > 0, xprev[...], 0.0)
        xcat = jnp.concatenate([p, xcur[...]], axis=0)
        rows = [jnp.sum(d * pltpu.roll(xcat, 3 - j, axis=0)[SUBLANE:], axis=0, keepdims=True) for j in range(3)]
        rows.append(jnp.sum(d * xcur[...], axis=0, keepdims=True))
        rows.append(jnp.zeros((SUBLANE - 4, LANE), f32))

        @pl.when(i == 0)
        def _():
            dw_ref[...] = jnp.zeros_like(dw_ref)
            db_ref[...] = jnp.zeros_like(db_ref)

        dw_ref[...] += jnp.concatenate(rows, axis=0)
        db_ref[...] += jnp.broadcast_to(jnp.sum(d, axis=0, keepdims=True), (SUBLANE, LANE))

    nh = T // SUBLANE
    dx, dw, db = pl.pallas_call(
        body, name=name, grid=(nblk, nt),
        in_specs=[
            pl.BlockSpec((tm, LANE), lambda c, i: (i, c)),
            pl.BlockSpec((SUBLANE, LANE), lambda c, i: (jnp.minimum((i + 1) * hb, nh - 1), c)),
            pl.BlockSpec((tm, LANE), lambda c, i: (i, cb0 + c)),
            pl.BlockSpec((SUBLANE, LANE), lambda c, i: (jnp.maximum(i * hb - 1, 0), cb0 + c)),
            pl.BlockSpec((4, LANE), lambda c, i: (0, c)),
        ],
        out_specs=[
            pl.BlockSpec((tm, LANE), lambda c, i: (i, c)),
            pl.BlockSpec((SUBLANE, LANE), lambda c, i: (0, c)),
            pl.BlockSpec((SUBLANE, LANE), lambda c, i: (0, c)),
        ],
        out_shape=[jax.ShapeDtypeStruct((T, nblk * LANE), f32),
                   jax.ShapeDtypeStruct((SUBLANE, nblk * LANE), f32),
                   jax.ShapeDtypeStruct((SUBLANE, nblk * LANE), f32)],
        compiler_params=_cparams("parallel", "arbitrary"),
    )(dy, dy, x, x, w)
    return dx, dw[:4], db[0]


@functools.partial(jax.custom_vjp, nondiff_argnums=(1,))
def _lroll(x, s):
    return pltpu.roll(x, s, axis=1)


def _lroll_fwd(x, s):
    return _lroll(x, s), None


def _lroll_bwd(s, _, g):
    return (_lroll(g, (LANE - s) % LANE),)


_lroll.defvjp(_lroll_fwd, _lroll_bwd)


def _rope_tables(T):
    half = A_HEAD_DIM // 2
    inv_freq = ROPE_THETA ** (-jnp.arange(half, dtype=f32) / half)
    ang = jnp.arange(T, dtype=f32)[:, None] * inv_freq[None, :]
    cos, sin = jnp.cos(ang), jnp.sin(ang)
    return jnp.tile(jnp.concatenate([cos, cos], axis=1), (1, 2)), jnp.tile(jnp.concatenate([-sin, sin], axis=1), (1, 2))


def _attn_block_fn(n, q, kp, kc, vp, vc, cq, sq, cp, sp, sinks):
    W = WINDOW
    lane = lax.broadcasted_iota(jnp.int32, (W, LANE), 1)
    lo_half = (lane % A_HEAD_DIM) < (A_HEAD_DIM // 2)
    lane8 = lax.broadcasted_iota(jnp.int32, sinks.shape, 1)

    def rope(x, c, s):
        return x * c + jnp.where(lo_half, _lroll(x, LANE - A_HEAD_DIM // 2), _lroll(x, A_HEAD_DIM // 2)) * s

    k2 = jnp.concatenate([rope(kp, cp, sp), rope(kc, cq, sq)], axis=0).astype(bf16)
    v2 = jnp.concatenate([vp, vc], axis=0).astype(bf16)
    row = lax.broadcasted_iota(jnp.int32, (W, 2 * W), 0)
    col = lax.broadcasted_iota(jnp.int32, (W, 2 * W), 1)
    dist = row + W - col
    mask = (dist >= 0) & (dist < W) & ((col >= W) | (n > 0))
    outs = []
    for t in range(4):
        qt = rope(q[:, LANE * t:LANE * (t + 1)], cq, sq)
        g = t // 2
        ot = jnp.zeros((W, LANE), f32)
        for hh in range(2):
            qa = jnp.where((lane // A_HEAD_DIM) == hh, qt, 0.0)
            if hh != g:
                qa = _lroll(qa, A_HEAD_DIM)
            s = _dot(qa.astype(bf16), k2, NT) * (A_HEAD_DIM ** -0.5)
            s = jnp.where(mask, s, -jnp.inf)
            sink = jnp.sum(jnp.where(lane8 == 2 * t + hh, sinks, 0.0), axis=1, keepdims=True)
            m = jnp.maximum(jnp.max(s, axis=-1, keepdims=True), sink)
            e = jnp.exp(s - m)
            p = e / (jnp.sum(e, axis=-1, keepdims=True) + jnp.exp(sink - m))
            o = _dot(p.astype(bf16), v2, NN)
            o = jnp.where((lane // A_HEAD_DIM) == g, o, 0.0)
            if hh != g:
                o = _lroll(o, A_HEAD_DIM)
            ot = ot + o
        outs.append(ot)
    return jnp.concatenate(outs, axis=1)


def _attn_specs():
    W = WINDOW
    prev = lambda n: jnp.maximum(n - 1, 0)
    return [
        pl.BlockSpec((W, 4 * LANE), lambda n: (n, CB_QA // 4)),
        pl.BlockSpec((W, LANE), lambda n: (prev(n), CB_KA)),
        pl.BlockSpec((W, LANE), lambda n: (n, CB_KA)),
        pl.BlockSpec((W, LANE), lambda n: (prev(n), CB_VA)),
        pl.BlockSpec((W, LANE), lambda n: (n, CB_VA)),
        pl.BlockSpec((W, LANE), lambda n: (n, 0)),
        pl.BlockSpec((W, LANE), lambda n: (n, 0)),
        pl.BlockSpec((W, LANE), lambda n: (prev(n), 0)),
        pl.BlockSpec((W, LANE), lambda n: (prev(n), 0)),
        pl.BlockSpec((1, A_Q_HEADS), lambda n: (0, 0)),
    ]


def _attn_fwd(name, proj, cos, sin, sinks):
    T = proj.shape[0]
    W = WINDOW

    def body(*refs):
        o = refs[-1]
        o[...] = _attn_block_fn(pl.program_id(0), *[r[...] for r in refs[:-1]])

    return pl.pallas_call(
        body, name=name, grid=(T // W,),
        in_specs=_attn_specs(),
        out_specs=pl.BlockSpec((W, 4 * LANE), lambda n: (n, 0)),
        out_shape=jax.ShapeDtypeStruct((T, 4 * LANE), f32),
        compiler_params=_cparams("parallel"),
    )(proj, proj, proj, proj, proj, cos, sin, cos, sin, sinks)


def _attn_bwd(name, proj, cos, sin, sinks, d_oab):
    T = proj.shape[0]
    W = WINDOW

    def body(*refs):
        ins = [r[...] for r in refs[:10]]
        do = refs[10][...]
        dq_ref, dk_ref, dv_ref, ds_ref = refs[11:]
        n = pl.program_id(0)
        _, vjp = jax.vjp(functools.partial(_attn_block_fn, n), *ins)
        dq, dkp, dkc, dvp, dvc, _, _, _, _, dsk = vjp(do)
        dq_ref[...] = dq

        @pl.when(n == 0)
        def _():
            dk_ref[...] = jnp.zeros_like(dk_ref)
            dv_ref[...] = jnp.zeros_like(dv_ref)
            ds_ref[...] = jnp.zeros_like(ds_ref)

        cur = pl.ds(pl.multiple_of(n * W, W), W)
        dk_ref[cur, :] += dkc
        dv_ref[cur, :] += dvc
        ds_ref[...] += dsk

        @pl.when(n > 0)
        def _():
            prv = pl.ds(pl.multiple_of((n - 1) * W, W), W)
            dk_ref[prv, :] += dkp
            dv_ref[prv, :] += dvp

    return pl.pallas_call(
        body, name=name, grid=(T // W,),
        in_specs=_attn_specs() + [pl.BlockSpec((W, 4 * LANE), lambda n: (n, 0))],
        out_specs=[pl.BlockSpec((W, 4 * LANE), lambda n: (n, 0)),
                   pl.BlockSpec((T, LANE), lambda n: (0, 0)),
                   pl.BlockSpec((T, LANE), lambda n: (0, 0)),
                   pl.BlockSpec((1, A_Q_HEADS), lambda n: (0, 0))],
        out_shape=[jax.ShapeDtypeStruct((T, 4 * LANE), f32), jax.ShapeDtypeStruct((T, LANE), f32),
                   jax.ShapeDtypeStruct((T, LANE), f32), jax.ShapeDtypeStruct((1, A_Q_HEADS), f32)],
        compiler_params=_cparams("arbitrary"),
    )(proj, proj, proj, proj, proj, cos, sin, cos, sin, sinks, d_oab)


def _bdot(spec, a, b, precision=None):
    return jnp.einsum(spec, a, b, preferred_element_type=f32, precision=precision)


@jax.custom_vjp
def _tri_inv(a):
    C = a.shape[-1]
    r = lax.broadcasted_iota(jnp.int32, (C, C), 0)
    c = lax.broadcasted_iota(jnp.int32, (C, C), 1)
    t = jnp.broadcast_to(jnp.where(r == c, 1.0, 0.0).astype(f32), a.shape)
    for j in range(C - 1):
        t = t - a[:, :, j:j + 1] * t[:, j:j + 1, :]
    return t


def _tri_inv_fwd(a):
    t = _tri_inv(a)
    return t, t


def _tri_inv_bwd(t, g):
    C = t.shape[-1]
    r = lax.broadcasted_iota(jnp.int32, (C, C), 0)
    c = lax.broadcasted_iota(jnp.int32, (C, C), 1)
    x = _bdot("hki,hkj->hij", t, g, precision=lax.Precision.HIGHEST)
    y = _bdot("hik,hjk->hij", x, t, precision=lax.Precision.HIGHEST)
    return (jnp.where(r > c, -y, 0.0),)


_tri_inv.defvjp(_tri_inv_fwd, _tri_inv_bwd)


@jax.custom_vjp
def _tri_inv_saved(a, t):
    return t


_tri_inv_saved.defvjp(lambda a, t: (t, t), lambda t, g: (_tri_inv_bwd(t, g)[0], jnp.zeros_like(t)))


def _silu(x):
    return x * jax.nn.sigmoid(x)


def _l2n(x):
    return x * lax.rsqrt(jnp.sum(x * x, axis=-1, keepdims=True) + NORM_EPS)


def _delta_chunk_fn(cq, ck, cv, z, lg, a_log, dt_bias, norm_w, S, t_saved=None, want_t=False):
    C = B_CHUNK
    lane = lax.broadcasted_iota(jnp.int32, (C, LANE), 1)
    pick = lambda l0: jnp.concatenate(
        [jnp.sum(jnp.where(lane == l0 + h, lg, 0.0), axis=1, keepdims=True)[None] for h in range(B_HEADS)], axis=0)
    bl, al = pick(0), pick(B_HEADS)
    q = _l2n(_silu(cq)) * (B_HEAD_DIM ** -0.5)
    k = _l2n(_silu(ck))
    v = _silu(cv)
    beta = jax.nn.sigmoid(bl)
    g = -jnp.exp(a_log) * jax.nn.softplus(al + dt_bias)
    r = lax.broadcasted_iota(jnp.int32, (C, C), 0)
    c = lax.broadcasted_iota(jnp.int32, (C, C), 1)
    eye = r == c
    g_row = jnp.sum(jnp.where(eye, g, 0.0), axis=1, keepdims=True)
    gc = jnp.sum(jnp.where(c <= r, g_row, 0.0), axis=2, keepdims=True)
    gc_row = jnp.sum(jnp.where(eye, gc, 0.0), axis=1, keepdims=True)
    decay_incl = jnp.exp(jnp.where(r >= c, gc - gc_row, -jnp.inf))
    decay_strict = jnp.where(r > c, decay_incl, 0.0)
    kb = k * beta
    vb = v * beta
    kbf = k.astype(bf16)
    a_mat = _bdot("hik,hjk->hij", kb.astype(bf16), kbf) * decay_strict
    t_f32 = _tri_inv(a_mat) if t_saved is None else _tri_inv_saved(a_mat, t_saved)
    t_mat = t_f32.astype(bf16)
    eg = jnp.exp(gc)
    u = _bdot("hij,hjv->hiv", t_mat, vb.astype(bf16))
    w = _bdot("hij,hjk->hik", t_mat, (kb * eg).astype(bf16))
    qk = _bdot("hik,hjk->hij", q.astype(bf16), kbf) * decay_incl
    g_last = jnp.sum(g, axis=1, keepdims=True)
    k_tail = k * jnp.exp(g_last - gc)
    Sb = S.astype(bf16)
    v_new = u - _bdot("hck,hkv->hcv", w.astype(bf16), Sb)
    o = _bdot("hck,hkv->hcv", (q * eg).astype(bf16), Sb) + _bdot("hij,hjv->hiv", qk.astype(bf16), v_new.astype(bf16))
    S_new = S * jnp.exp(g_last) + _bdot("hck,hcv->hkv", k_tail.astype(bf16), v_new.astype(bf16))
    ob = o * lax.rsqrt(jnp.mean(o * o, axis=-1, keepdims=True) + NORM_EPS) * norm_w
    return (ob * _silu(z), S_new) + ((t_f32,) if want_t else ())


def _delta_in_specs(rev, N):
    C = B_CHUNK
    ix = (lambda n: N - 1 - n) if rev else (lambda n: n)
    specs = [pl.BlockSpec((C, 3 * B_HEADS * LANE), lambda n: (ix(n), 0))]
    specs += [pl.BlockSpec((C, LANE), lambda n, h=h: (ix(n), CB_Z + h)) for h in range(B_HEADS)]
    specs += [
        pl.BlockSpec((C, LANE), lambda n: (ix(n), CB_LG)),
        pl.BlockSpec((B_HEADS, 1, 1), lambda n: (0, 0, 0)),
        pl.BlockSpec((B_HEADS, 1, 1), lambda n: (0, 0, 0)),
        pl.BlockSpec((1, LANE), lambda n: (0, 0)),
    ]
    return specs


def _delta_inputs(c_ref, z_refs, lg, al, dt, nw):
    H = B_HEADS
    part = lambda p: jnp.stack([c_ref[:, LANE * (p * H + h):LANE * (p * H + h + 1)] for h in range(H)])
    return (part(0), part(1), part(2), jnp.stack([z[...] for z in z_refs]), lg[...], al[...], dt[...], nw[...])


def _delta_fwd(name, c, proj, a_log, dt_bias, norm_w):
    T = c.shape[0]
    C = B_CHUNK
    N = T // C
    Dh = B_HEAD_DIM
    H = B_HEADS

    def body(*refs):
        c_ref, z_refs, (lg, al, dt, nw) = refs[0], refs[1:1 + H], refs[1 + H:5 + H]
        o_ref, s_ref, t_ref, S = refs[5 + H:]

        @pl.when(pl.program_id(0) == 0)
        def _():
            S[...] = jnp.zeros_like(S)

        s0 = S[...]
        s_ref[...] = s0
        ob, s1, t = _delta_chunk_fn(*_delta_inputs(c_ref, z_refs, lg, al, dt, nw), s0, want_t=True)
        for h in range(H):
            o_ref[:, LANE * h:LANE * (h + 1)] = ob[h]
        t_ref[...] = t
        S[...] = s1

    return pl.pallas_call(
        body, name=name, grid=(N,),
        in_specs=_delta_in_specs(False, N),
        out_specs=[pl.BlockSpec((C, H * LANE), lambda n: (n, 0)),
                   pl.BlockSpec((H, None, Dh, Dh), lambda n: (0, n, 0, 0)),
                   pl.BlockSpec((H, None, C, C), lambda n: (0, n, 0, 0))],
        out_shape=[jax.ShapeDtypeStruct((T, H * Dh), f32), jax.ShapeDtypeStruct((H, N, Dh, Dh), f32),
                   jax.ShapeDtypeStruct((H, N, C, C), f32)],
        scratch_shapes=[pltpu.VMEM((H, Dh, Dh), f32)],
        compiler_params=_cparams("arbitrary"),
    )(c, *([proj] * H), proj, a_log, dt_bias, norm_w)


def _delta_bwd(name, c, proj, a_log, dt_bias, norm_w, s_saved, t_saved, d_oab):
    T = c.shape[0]
    C = B_CHUNK
    N = T // C
    Dh = B_HEAD_DIM
    H = B_HEADS

    def body(*refs):
        c_ref, z_refs, (lg, al, dt, nw) = refs[0], refs[1:1 + H], refs[1 + H:5 + H]
        s_ref, t_ref, do_ref = refs[5 + H:8 + H]
        dc, dz, dlg, dal, ddt, dnw, dS = refs[8 + H:]

        @pl.when(pl.program_id(0) == 0)
        def _():
            dS[...] = jnp.zeros_like(dS)
            dal[...] = jnp.zeros_like(dal)
            ddt[...] = jnp.zeros_like(ddt)
            dnw[...] = jnp.zeros_like(dnw)

        _, vjp = jax.vjp(functools.partial(_delta_chunk_fn, t_saved=t_ref[...]),
                         *_delta_inputs(c_ref, z_refs, lg, al, dt, nw), s_ref[...])
        do = jnp.stack([do_ref[:, LANE * h:LANE * (h + 1)] for h in range(H)])
        g = vjp((do, dS[...]))
        for h in range(H):
            for p in range(3):
                dc[:, LANE * (p * H + h):LANE * (p * H + h + 1)] = g[p][h]
            dz[:, LANE * h:LANE * (h + 1)] = g[3][h]
        dlg[...] = g[4]
        dal[...] += g[5]
        ddt[...] += g[6]
        dnw[...] += g[7]
        dS[...] = g[8]

    rn = lambda n: N - 1 - n
    return pl.pallas_call(
        body, name=name, grid=(N,),
        in_specs=_delta_in_specs(True, N) + [
            pl.BlockSpec((H, None, Dh, Dh), lambda n: (0, rn(n), 0, 0)),
            pl.BlockSpec((H, None, C, C), lambda n: (0, rn(n), 0, 0)),
            pl.BlockSpec((C, H * LANE), lambda n: (rn(n), 1)),
        ],
        out_specs=[
            pl.BlockSpec((C, 3 * H * LANE), lambda n: (rn(n), 0)),
            pl.BlockSpec((C, H * LANE), lambda n: (rn(n), 0)),
            pl.BlockSpec((C, LANE), lambda n: (rn(n), 0)),
            pl.BlockSpec((H, 1, 1), lambda n: (0, 0, 0)),
            pl.BlockSpec((H, 1, 1), lambda n: (0, 0, 0)),
            pl.BlockSpec((1, LANE), lambda n: (0, 0)),
        ],
        out_shape=[jax.ShapeDtypeStruct((T, 3 * H * Dh), f32), jax.ShapeDtypeStruct((T, H * Dh), f32),
                   jax.ShapeDtypeStruct((T, LANE), f32), jax.ShapeDtypeStruct((H, 1, 1), f32),
                   jax.ShapeDtypeStruct((H, 1, 1), f32), jax.ShapeDtypeStruct((1, LANE), f32)],
        scratch_shapes=[pltpu.VMEM((H, Dh, Dh), f32)],
        compiler_params=_cparams("arbitrary"),
    )(c, *([proj] * H), proj, a_log, dt_bias, norm_w, s_saved, t_saved, d_oab)


def _blockdiag_fwd(name, xc, w_a, w_x, tm=512):
    T, Wd = xc.shape
    bw = Wd // LRU_BLOCKS
    tm = min(tm, T)

    def body(x_ref, wa_ref, wx_ref, oa, ox):
        xb = x_ref[...].astype(bf16)
        oa[...] = _dot(xb, wa_ref[...].astype(bf16), NN)
        ox[...] = _dot(xb, wx_ref[...].astype(bf16), NN)

    xs = pl.BlockSpec((tm, bw), lambda i, h: (i, h))
    ws = pl.BlockSpec((None, bw, bw), lambda i, h: (h, 0, 0))
    return pl.pallas_call(
        body, name=name, grid=(T // tm, LRU_BLOCKS), in_specs=[xs, ws, ws], out_specs=[xs, xs],
        out_shape=[jax.ShapeDtypeStruct((T, Wd), f32)] * 2,
        compiler_params=_cparams("parallel", "parallel"),
    )(xc, w_a, w_x)


def _blockdiag_bwd_dx(name, dpr, dpi, w_a, w_x, addend, tm=512):
    T, Wd = dpr.shape
    bw = Wd // LRU_BLOCKS
    tm = min(tm, T)

    def body(dr, di, wa_ref, wx_ref, add, o):
        o[...] = (add[...] + _dot(dr[...].astype(bf16), wa_ref[...].astype(bf16), NT)
                  + _dot(di[...].astype(bf16), wx_ref[...].astype(bf16), NT))

    xs = pl.BlockSpec((tm, bw), lambda i, h: (i, h))
    ws = pl.BlockSpec((None, bw, bw), lambda i, h: (h, 0, 0))
    return pl.pallas_call(
        body, name=name, grid=(T // tm, LRU_BLOCKS), in_specs=[xs, xs, ws, ws, xs], out_specs=xs,
        out_shape=jax.ShapeDtypeStruct((T, Wd), f32),
        compiler_params=_cparams("parallel", "parallel"),
    )(dpr, dpi, w_a, w_x, addend)


def _blockdiag_bwd_dw(name, xc, dpr, dpi, tk=512):
    T, Wd = xc.shape
    bw = Wd // LRU_BLOCKS
    tk = min(tk, T)

    def body(x_ref, dr, di, oa, ox):
        @pl.when(pl.program_id(1) == 0)
        def _():
            oa[...] = jnp.zeros_like(oa)
            ox[...] = jnp.zeros_like(ox)

        xb = x_ref[...].astype(bf16)
        oa[...] += _dot(xb, dr[...].astype(bf16), TN)
        ox[...] += _dot(xb, di[...].astype(bf16), TN)

    xs = pl.BlockSpec((tk, bw), lambda h, k: (k, h))
    ws = pl.BlockSpec((None, bw, bw), lambda h, k: (h, 0, 0))
    return pl.pallas_call(
        body, name=name, grid=(LRU_BLOCKS, T // tk), in_specs=[xs, xs, xs], out_specs=[ws, ws],
        out_shape=[jax.ShapeDtypeStruct((LRU_BLOCKS, bw, bw), f32)] * 2,
        compiler_params=_cparams("parallel", "arbitrary"),
    )(xc, dpr, dpi)


def _scan(name, a, b, reverse, tt=512, cb=512):
    T, Wd = a.shape
    tt, cb = min(tt, T), min(cb, Wd)
    nt = T // tt
    ng = tt // SUBLANE

    def body(a_ref, b_ref, o_ref, carry):
        @pl.when(pl.program_id(1) == 0)
        def _():
            carry[...] = jnp.zeros_like(carry)

        row = lax.broadcasted_iota(jnp.int32, (SUBLANE, cb), 0)

        def step(gi, hp):
            g = (ng - 1 - gi) if reverse else gi
            off = pl.multiple_of(g * SUBLANE, SUBLANE)
            A = a_ref[pl.ds(off, SUBLANE), :]
            B = b_ref[pl.ds(off, SUBLANE), :]
            for s in (1, 2, 4):
                sh = (SUBLANE - s) if reverse else s
                As = pltpu.roll(A, sh, axis=0)
                Bs = pltpu.roll(B, sh, axis=0)
                valid = (row < SUBLANE - s) if reverse else (row >= s)
                B = jnp.where(valid, A * Bs + B, B)
                A = jnp.where(valid, A * As, A)
            hcur = A * hp + B
            o_ref[pl.ds(off, SUBLANE), :] = hcur
            edge = hcur[0:1, :] if reverse else hcur[SUBLANE - 1:SUBLANE, :]
            return jnp.broadcast_to(edge, (SUBLANE, cb))

        carry[...] = lax.fori_loop(0, ng, step, carry[...])

    ti = (lambda c, i: (nt - 1 - i, c)) if reverse else (lambda c, i: (i, c))
    spec = pl.BlockSpec((tt, cb), ti)
    return pl.pallas_call(
        body, name=name, grid=(Wd // cb, nt), in_specs=[spec, spec], out_specs=spec,
        out_shape=jax.ShapeDtypeStruct((T, Wd), f32),
        scratch_shapes=[pltpu.VMEM((SUBLANE, cb), f32)],
        compiler_params=_cparams("parallel", "arbitrary"),
    )(a, b)


def _relu2_epilogue(r):
    h = jnp.maximum(r, 0.0)
    return r, h * h


def _drelu2_epilogue(r, a):
    return (r * (2.0 * jnp.maximum(a, 0.0)),)


def _add_epilogue(r, e):
    return (r + e,)


def _merge_cols(name, g, tm=256):
    _, L, R, s = g.shape

    def body(g_ref, o_ref):
        for d in range(N_DEV):
            o_ref[:, s * d:s * (d + 1)] = g_ref[d].astype(bf16)
        o_ref[:, N_DEV * s:] = jnp.zeros((tm, HYB_PROJ_PAD - N_DEV * s), bf16)

    return pl.pallas_call(
        body, name=name, grid=(L, R // tm),
        in_specs=[pl.BlockSpec((N_DEV, None, tm, s), lambda l, i: (0, l, i, 0))],
        out_specs=pl.BlockSpec((None, tm, HYB_PROJ_PAD), lambda l, i: (l, i, 0)),
        out_shape=jax.ShapeDtypeStruct((L, R, HYB_PROJ_PAD), bf16),
        compiler_params=_cparams("parallel", "parallel"),
    )(g)


def _split_cols(name, dw, tm=256):
    R = dw.shape[0]
    s = HYB_PROJ // N_DEV

    def body(g_ref, o_ref):
        for d in range(N_DEV):
            o_ref[d] = g_ref[:, s * d:s * (d + 1)].astype(bf16)

    return pl.pallas_call(
        body, name=name, grid=(R // tm,),
        in_specs=[pl.BlockSpec((tm, HYB_PROJ_PAD), lambda i: (i, 0))],
        out_specs=pl.BlockSpec((N_DEV, tm, s), lambda i: (0, i, 0)),
        out_shape=jax.ShapeDtypeStruct((N_DEV, R, s), bf16),
        compiler_params=_cparams("parallel"),
    )(dw)


def _rows_to_dev(dw):
    nb, r, c = dw.shape
    t = dw.reshape(nb, N_DEV, r // N_DEV, c)
    return jnp.moveaxis(t, 1, 0).reshape(N_DEV, nb * (r // N_DEV), c).astype(bf16)


def _hybrid_fwd(tag, x, W, j, cos, sin):
    proj = _mm(f"{tag}_proj", x, W["hyb_w_in"][j], "nn", b_kind="lead", b_lead=0)
    o_a = _attn_fwd(f"{tag}_attn", proj, cos, sin, W["hyb_sinks"][j][None, :])
    c = _conv_fwd(f"{tag}_conv", proj, CB_CONV, 12, W["hyb_conv_w"][j], None)
    o_b, s_saved, t_saved = _delta_fwd(f"{tag}_delta", c, proj, W["hyb_a_log"][j].reshape(B_HEADS, 1, 1),
                                       W["hyb_dt_bias"][j].reshape(B_HEADS, 1, 1), W["hyb_norm_w"][j][None, :])
    o_ab = jnp.concatenate([o_a, o_b], axis=1)
    mix = _mm(f"{tag}_out", o_ab, W["hyb_w_out"][j], "nn", b_kind="lead", b_lead=0)
    return mix, (proj, c, s_saved, t_saved, o_ab)


def _hybrid_bwd(tag, x, dmix, addend, W, j, cos, sin, saved, G):
    proj, c, s_saved, t_saved, o_ab = saved
    T = x.shape[0]
    d_oab = _mm(f"{tag}_dout", dmix, W["hyb_w_out"][j], "nt", b_kind="lead", b_lead=0)
    G["hyb_w_out"][j] = _mm(f"{tag}_dwout", o_ab, dmix, "tn", out_dtypes=(bf16,)).reshape(N_DEV, -1, D_MODEL)
    dq, dk, dv, dsinks = _attn_bwd(f"{tag}_dattn", proj, cos, sin, W["hyb_sinks"][j][None, :], d_oab)
    a_log = W["hyb_a_log"][j].reshape(B_HEADS, 1, 1)
    dt_bias = W["hyb_dt_bias"][j].reshape(B_HEADS, 1, 1)
    dc, dz, dlg, dal, ddt, dnw = _delta_bwd(f"{tag}_ddelta", c, proj, a_log, dt_bias, W["hyb_norm_w"][j][None, :],
                                            s_saved, t_saved, d_oab)
    dconv_in, dconv_w, _ = _conv_bwd(f"{tag}_dconv", dc, proj, CB_CONV, 12, W["hyb_conv_w"][j])
    dproj = jnp.concatenate([dq, dk, dv, dconv_in, dz, dlg,
                             jnp.zeros((T, HYB_PROJ_PAD - (CB_LG + 1) * LANE), f32)], axis=1)
    dx = _mm(f"{tag}_dx", dproj, W["hyb_w_in"][j], "nt", b_kind="lead", b_lead=0, epilogue=_add_epilogue,
             extras=(addend,))
    G["hyb_w_in"][j] = _split_cols(f"{tag}_dwin_split", _mm(f"{tag}_dwin", x, dproj, "tn"))
    G["hyb_sinks"][j] = dsinks[0]
    G["hyb_conv_w"][j] = dconv_w
    G["hyb_a_log"][j] = dal.reshape(B_HEADS)
    G["hyb_dt_bias"][j] = ddt.reshape(B_HEADS)
    G["hyb_norm_w"][j] = dnw[0]
    return dx


def _rec_fwd(tag, x, W, j):
    Wd = D_MODEL
    proj = _mm(f"{tag}_proj", x, W["rec_w_in"][j], "nn", b_kind="devcol", b_lead=0)
    xc = _conv_fwd(f"{tag}_conv", proj, 0, Wd // LANE, W["rec_conv_w"][j], W["rec_conv_b"][j][None, :])
    pre_r, pre_i = _blockdiag_fwd(f"{tag}_gates", xc, W["rec_w_a"][j][0], W["rec_w_x"][j][0])
    pars = [W["rec_b_a"][j][None, :], W["rec_b_x"][j][None, :], W["rec_lambda"][j][None, :]]
    a, b = _tl_fwd(f"{tag}_pre", _rglru_pre_fn, [(pre_r, 0, Wd), (pre_i, 0, Wd), (xc, 0, Wd)], pars, [Wd, Wd], [f32, f32])
    h = _scan(f"{tag}_scan", a, b, False)
    (hg,) = _tl_fwd(f"{tag}_gate", _rec_gate_fn, [(h, 0, Wd), (proj, Wd // LANE, Wd)], [], [Wd], [f32])
    mix = _mm(f"{tag}_out", hg, W["rec_w_out"][j], "nn", b_kind="lead", b_lead=0)
    return mix, (proj, xc, pre_r, pre_i, a, h, hg)


def _rec_bwd(tag, x, dmix, addend, W, j, saved, G):
    proj, xc, pre_r, pre_i, a, h, hg = saved
    Wd = D_MODEL
    dhg = _mm(f"{tag}_dout", dmix, W["rec_w_out"][j], "nt", b_kind="lead", b_lead=0)
    G["rec_w_out"][j] = _mm(f"{tag}_dwout", hg, dmix, "tn", out_dtypes=(bf16,)).reshape(N_DEV, -1, D_MODEL)
    (dh, dgate), _ = _tl_bwd(f"{tag}_dgate", _rec_gate_fn, [(h, 0, Wd), (proj, Wd // LANE, Wd)], [], [(dhg, 0, Wd)])
    a_next = jnp.concatenate([a[1:], jnp.zeros((1, Wd), f32)], axis=0)
    h_prev = jnp.concatenate([jnp.zeros((1, Wd), f32), h[:-1]], axis=0)
    lam_t = _scan(f"{tag}_dscan", a_next, dh, True)
    pars = [W["rec_b_a"][j][None, :], W["rec_b_x"][j][None, :], W["rec_lambda"][j][None, :]]
    (dpr, dpi, dxc1), (db_a, db_x, dlam) = _tl_bwd(
        f"{tag}_dpre", _rglru_pre_fn, [(pre_r, 0, Wd), (pre_i, 0, Wd), (xc, 0, Wd)], pars,
        [(lam_t, 0, Wd), (h_prev, 0, Wd)], cot_fn=lambda lt, hp: (lt * hp, lt))
    dxc = _blockdiag_bwd_dx(f"{tag}_dgates_dx", dpr, dpi, W["rec_w_a"][j][0], W["rec_w_x"][j][0], dxc1)
    dwa, dwx = _blockdiag_bwd_dw(f"{tag}_dgates_dw", xc, dpr, dpi)
    G["rec_w_a"][j], G["rec_w_x"][j] = _rows_to_dev(dwa), _rows_to_dev(dwx)
    dxr, dconv_w, dconv_b = _conv_bwd(f"{tag}_dconv", dxc, proj, 0, Wd // LANE, W["rec_conv_w"][j])
    dproj = jnp.concatenate([dxr, dgate], axis=1)
    dx = _mm(f"{tag}_dx", dproj, W["rec_w_in"][j], "nt", b_kind="devcol", b_lead=0, epilogue=_add_epilogue,
             extras=(addend,))
    G["rec_w_in"][j] = _mm(f"{tag}_dwin", x, dproj, "tn", o_kind="devcol", out_dtypes=(bf16,))
    G["rec_conv_w"][j] = dconv_w
    G["rec_conv_b"][j] = dconv_b
    G["rec_b_a"][j] = db_a[0]
    G["rec_b_x"][j] = db_x[0]
    G["rec_lambda"][j] = dlam[0]
    return dx


def _local_step(x, target, W, load_layer, layer_grads_ready):
    T = x.shape[0]
    cos, sin = _rope_tables(T)
    saved = []
    for layer in range(DEPTH):
        j = layer // 2
        tag = f"L{layer}"
        load_layer(layer, x)
        if layer % 2 == 0:
            mix, sv = _hybrid_fwd(tag, x, W, j, cos, sin)
        else:
            mix, sv = _rec_fwd(tag, x, W, j)
        ln1 = [W["ln1_g"][layer][None, :], W["ln1_b"][layer][None, :]]
        (x1,) = _tl_fwd(f"{tag}_ln1", _ln_res_fn, [(x, 0, D_MODEL), (mix, 0, D_MODEL)], ln1, [D_MODEL], [f32])
        a, h2 = _mm(f"{tag}_mlp1", x1, W["mlp_w1"][layer], "nn", b_kind="devcol", b_lead=0, epilogue=_relu2_epilogue,
                    out_dtypes=(f32, bf16))
        y = _mm(f"{tag}_mlp2", h2, W["mlp_w2"][layer], "nn", b_kind="devrow", b_lead=0)
        ln2 = [W["ln2_g"][layer][None, :], W["ln2_b"][layer][None, :]]
        (x2,) = _tl_fwd(f"{tag}_ln2", _ln_res_fn, [(x1, 0, D_MODEL), (y, 0, D_MODEL)], ln2, [D_MODEL], [f32])
        saved.append((x, sv, mix, x1, a, h2, y))
        x = x2
    loss, dx = _loss_head(x, target)

    G = {k: [None] * (DEPTH if k.startswith(("ln", "mlp")) else DEPTH // 2) for k in (
        "hyb_w_in", "hyb_sinks", "hyb_conv_w", "hyb_a_log", "hyb_dt_bias", "hyb_norm_w", "hyb_w_out",
        "rec_w_in", "rec_conv_w", "rec_conv_b", "rec_w_a", "rec_b_a", "rec_w_x", "rec_b_x", "rec_lambda", "rec_w_out",
        "ln1_g", "ln1_b", "mlp_w1", "mlp_w2", "ln2_g", "ln2_b")}
    order = jnp.zeros((1, 1), f32)
    for layer in reversed(range(DEPTH)):
        j = layer // 2
        tag = f"L{layer}"
        x0, sv, mix, x1, a, h2, y = saved[layer]
        ln2 = [W["ln2_g"][layer][None, :] + order, W["ln2_b"][layer][None, :]]
        (dx1_a, dy), (dg2, db2) = _tl_bwd(f"{tag}_dln2", _ln_res_fn, [(x1, 0, D_MODEL), (y, 0, D_MODEL)], ln2,
                                          [(dx, 0, D_MODEL)])
        G["ln2_g"][layer], G["ln2_b"][layer] = dg2[0], db2[0]
        da = _mm(f"{tag}_dmlp2", dy, W["mlp_w2"][layer], "nt", b_kind="devrow", b_lead=0, epilogue=_drelu2_epilogue,
                 extras=(a,), out_dtypes=(bf16,))
        G["mlp_w2"][layer] = _mm(f"{tag}_dw2", h2, dy, "tn", out_dtypes=(bf16,)).reshape(N_DEV, -1, D_MODEL)
        dx1 = _mm(f"{tag}_dmlp1", da, W["mlp_w1"][layer], "nt", b_kind="devcol", b_lead=0, epilogue=_add_epilogue,
                  extras=(dx1_a,))
        G["mlp_w1"][layer] = _mm(f"{tag}_dw1", x1, da, "tn", o_kind="devcol", out_dtypes=(bf16,))
        ln1 = [W["ln1_g"][layer][None, :], W["ln1_b"][layer][None, :]]
        (dx0_a, dmix), (dg1, db1) = _tl_bwd(f"{tag}_dln1", _ln_res_fn, [(x0, 0, D_MODEL), (mix, 0, D_MODEL)], ln1,
                                            [(dx1, 0, D_MODEL)])
        G["ln1_g"][layer], G["ln1_b"][layer] = dg1[0], db1[0]
        if layer % 2 == 0:
            dx = _hybrid_bwd(tag, x0, dmix, dx0_a, W, j, cos, sin, sv, G)
        else:
            dx = _rec_bwd(tag, x0, dmix, dx0_a, W, j, sv, G)
        order = layer_grads_ready(layer, {k: G[k][i] for k, i in _layer_weights(layer)})
    big = {k for k, _ in BIG}
    return loss, dx, {k: jnp.stack(v) for k, v in G.items() if k not in big}


def _layer_weights(layer):
    j = layer // 2
    mixer = ["hyb_w_in", "hyb_w_out"] if layer % 2 == 0 else ["rec_w_in", "rec_w_out", "rec_w_a", "rec_w_x"]
    return [(k, j) for k in mixer] + [("mlp_w1", layer), ("mlp_w2", layer)]


def _my_coords():
    return lax.axis_index("x"), lax.axis_index("y"), lax.axis_index("c")


def _all_gather(name, arrays):
    na = len(arrays)

    def body(*refs):
        x_refs, out_refs = refs[:na], refs[na:2 * na]
        send_sems, recv_sems, local_sems = refs[2 * na:]
        x, y, c = _my_coords()
        me, sibling = (x, y, c), (x, y, 1 - c)
        chips = [(1 - x, y), (x, 1 - y), (1 - x, 1 - y)]

        def blk(a, px, py, pc):
            return out_refs[a].at[4 * px + 2 * py + pc]

        def copy(a, k, block, to, src=None):
            return pltpu.make_async_remote_copy(
                src_ref=blk(a, *block) if src is None else src, dst_ref=blk(a, *block),
                send_sem=send_sems.at[a, k], recv_sem=recv_sems.at[a, k],
                device_id=to, device_id_type=pl.DeviceIdType.MESH)

        mine = [pltpu.make_async_copy(x_refs[a], blk(a, *me), local_sems.at[a]) for a in range(na)]
        for cp in mine:
            cp.start()
        first = []
        for a in range(na):
            first.append(copy(a, 0, me, sibling, src=x_refs[a]))
            first += [copy(a, 1 + j, me, (*chip, c), src=x_refs[a]) for j, chip in enumerate(chips)]
        for cp in first:
            cp.start()
        passed = []
        for a in range(na):
            for j, chip in enumerate(chips):
                copy(a, 1 + j, (*chip, c), me).wait_recv()
                passed.append(copy(a, 4 + j, (*chip, c), sibling))
                passed[-1].start()
        for a in range(na):
            copy(a, 0, sibling, me).wait_recv()
            for j, chip in enumerate(chips):
                copy(a, 4 + j, (*chip, 1 - c), me).wait_recv()
        for cp in first + passed:
            cp.wait_send()
        for cp in mine:
            cp.wait()

    return pl.pallas_call(
        body, name=name,
        out_shape=[jax.ShapeDtypeStruct((N_DEV,) + a.shape, a.dtype) for a in arrays],
        in_specs=[pl.BlockSpec(memory_space=pl.ANY)] * na,
        out_specs=[pl.BlockSpec(memory_space=pl.ANY)] * na,
        scratch_shapes=[pltpu.SemaphoreType.DMA((na, 7)), pltpu.SemaphoreType.DMA((na, 7)),
                        pltpu.SemaphoreType.DMA((na,))],
    )(*arrays)


_HBM = pl.BlockSpec(memory_space=pltpu.HBM)
_SEM = pl.BlockSpec(memory_space=pltpu.SEMAPHORE)


def _own_block(name, kind, srcs, lands):
    na = len(srcs)

    def body(*refs):
        x_refs, out_refs, sems = refs[:na], refs[2 * na:3 * na], refs[3 * na]
        x, y, c = _my_coords()
        me = 4 * x + 2 * y + c
        cps = [pltpu.make_async_copy(x_refs[a] if kind == "gather" else x_refs[a].at[me], out_refs[a].at[me],
                                     sems.at[a]) for a in range(na)]
        for cp in cps:
            cp.start()
        for cp in cps:
            cp.wait()

    return pl.pallas_call(
        body, name=name,
        out_shape=[jax.ShapeDtypeStruct(l.shape, l.dtype) for l in lands],
        in_specs=[pl.BlockSpec(memory_space=pl.ANY)] * (2 * na),
        out_specs=[pl.BlockSpec(memory_space=pl.ANY)] * na,
        input_output_aliases={na + a: a for a in range(na)},
        scratch_shapes=[pltpu.SemaphoreType.DMA((na,))],
    )(*srcs, *lands)


def _push_copies(kind, x_refs, land_refs, send_sems, recv_sems):
    x, y, c = _my_coords()
    me = 4 * x + 2 * y + c
    cps = []
    for a in range(len(x_refs)):
        for k in range(1, N_DEV):
            px = (1 - x) if (k >> 2) & 1 else x
            py = (1 - y) if (k >> 1) & 1 else y
            pc = (1 - c) if k & 1 else c
            cps.append(pltpu.make_async_remote_copy(
                src_ref=x_refs[a] if kind == "gather" else x_refs[a].at[4 * px + 2 * py + pc],
                dst_ref=land_refs[a].at[me],
                send_sem=send_sems.at[a * (N_DEV - 1) + k - 1], recv_sem=recv_sems.at[a * (N_DEV - 1) + k - 1],
                device_id=(px, py, pc), device_id_type=pl.DeviceIdType.MESH))
    return cps


def _push_start(name, kind, srcs, lands):
    na = len(srcs)

    def body(*refs):
        x_refs, land_refs = refs[:na], refs[na:2 * na]
        send_sems, recv_sems = refs[2 * na], refs[2 * na + 1]
        for cp in _push_copies(kind, x_refs, land_refs, send_sems, recv_sems):
            cp.start()
        token = refs[-1]
        token[...] = jnp.zeros_like(token)

    arrays = list(srcs) + list(lands)
    res = pl.pallas_call(
        body, name=name,
        out_shape=(pltpu.SemaphoreType.DMA((na * (N_DEV - 1),)), pltpu.SemaphoreType.DMA((na * (N_DEV - 1),)),
                   *[pltpu.HBM(t.shape, t.dtype) for t in arrays], jax.ShapeDtypeStruct((SUBLANE, LANE), f32)),
        in_specs=[_HBM] * (2 * na),
        out_specs=(_SEM, _SEM, *[_HBM] * (2 * na), pl.BlockSpec(memory_space=pltpu.VMEM)),
        input_output_aliases={i: 2 + i for i in range(2 * na)},
        compiler_params=pltpu.CompilerParams(has_side_effects=pltpu.SideEffectType.DATAFLOW_SIDE_EFFECTING),
    )(*[pltpu.with_memory_space_constraint(t, pltpu.HBM) for t in arrays])
    return res[0], res[1], res[2:2 + na], res[2 + na:2 + 2 * na], res[-1][:1, :1]


def _push_wait(name, kind, send_sems, recv_sems, srcs, lands, after):
    na = len(srcs)

    def body(*refs):
        x_refs, land_refs = refs[:na], refs[na:2 * na]
        for cp in _push_copies(kind, x_refs, land_refs, refs[2 * na], refs[2 * na + 1]):
            cp.wait_send()
            cp.wait_recv()

    arrays = list(srcs) + list(lands)
    res = pl.pallas_call(
        body, name=name,
        out_shape=tuple(pltpu.HBM(t.shape, t.dtype) for t in arrays),
        in_specs=[_HBM] * (2 * na) + [_SEM, _SEM, pl.BlockSpec(memory_space=pl.ANY)],
        out_specs=tuple([_HBM] * (2 * na)),
        input_output_aliases={i: i for i in range(2 * na)},
        compiler_params=pltpu.CompilerParams(has_side_effects=pltpu.SideEffectType.DATAFLOW_SIDE_EFFECTING),
    )(*arrays, send_sems, recv_sems, after)
    return res[na:]


def _sum_blocks(name, land):
    _, R, n = land.shape
    tr = R

    def body(l_ref, o_ref):
        acc = l_ref[0].astype(f32)
        for s in range(1, N_DEV):
            acc = acc + l_ref[s].astype(f32)
        o_ref[...] = acc

    return pl.pallas_call(
        body, name=name, grid=(R // tr,),
        in_specs=[pl.BlockSpec((N_DEV, tr, n), lambda i: (0, i, 0))],
        out_specs=pl.BlockSpec((tr, n), lambda i: (i, 0)),
        out_shape=jax.ShapeDtypeStruct((R, n), f32),
        compiler_params=_cparams("parallel"),
    )(land)


def _adamw(name, w, g, m, v):
    shape = w.shape
    last = shape[-1]
    rows = math.prod(shape[:-1])
    tm = 256 if rows % 256 == 0 and rows > 256 else rows
    w2, g2, m2, v2 = (t.reshape(rows, last) for t in (w, g, m, v))

    def body(w_ref, g_ref, m_ref, v_ref, d_ref, mo_ref, vo_ref):
        gg = g_ref[...]
        mn = ADAM_B1 * m_ref[...] + (1.0 - ADAM_B1) * gg
        vn = ADAM_B2 * v_ref[...] + (1.0 - ADAM_B2) * jnp.square(gg)
        m_hat = mn / (1.0 - ADAM_B1 ** ADAM_STEP)
        v_hat = vn / (1.0 - ADAM_B2 ** ADAM_STEP)
        d_ref[...] = -ADAM_LR * (m_hat / (jnp.sqrt(v_hat) + ADAM_EPS) + ADAM_WD * w_ref[...])
        mo_ref[...] = mn
        vo_ref[...] = vn

    spec = pl.BlockSpec((tm, last), lambda i: (i, 0))
    d, mn, vn = pl.pallas_call(
        body, name=name, grid=(rows // tm,), in_specs=[spec] * 4, out_specs=[spec] * 3,
        out_shape=[jax.ShapeDtypeStruct((rows, last), f32)] * 3,
        compiler_params=_cparams("parallel"),
    )(w2, g2, m2, v2)
    return d.reshape(shape), mn.reshape(shape), vn.reshape(shape)


def _adamw_land(name, lands, w, m, v, tm=256):
    L = len(lands)
    _, R, C = lands[0].shape
    tm = min(tm, R)

    def body(*refs):
        l_refs, (w_ref, m_ref, v_ref, g_ref, d_ref, mo_ref, vo_ref) = refs[:L], refs[L:]
        for k in range(L):
            @pl.when(pl.program_id(0) == k)
            def _(k=k):
                gg = l_refs[k][0].astype(f32)
                for s in range(1, N_DEV):
                    gg = gg + l_refs[k][s].astype(f32)
                g_ref[...] = gg
                mn = ADAM_B1 * m_ref[...] + (1.0 - ADAM_B1) * gg
                vn = ADAM_B2 * v_ref[...] + (1.0 - ADAM_B2) * jnp.square(gg)
                m_hat = mn / (1.0 - ADAM_B1 ** ADAM_STEP)
                v_hat = vn / (1.0 - ADAM_B2 ** ADAM_STEP)
                d_ref[...] = -ADAM_LR * (m_hat / (jnp.sqrt(v_hat) + ADAM_EPS) + ADAM_WD * w_ref[...])
                mo_ref[...] = mn
                vo_ref[...] = vn

    land_specs = [pl.BlockSpec((N_DEV, tm, C), lambda l, i, k=k: (0, jnp.where(l == k, i, 0), 0)) for k in range(L)]
    spec = pl.BlockSpec((None, tm, C), lambda l, i: (l, i, 0))
    return pl.pallas_call(
        body, name=name, grid=(L, R // tm),
        in_specs=land_specs + [spec] * 3,
        out_specs=[spec] * 4,
        out_shape=[jax.ShapeDtypeStruct((L, R, C), f32)] * 4,
        compiler_params=_cparams("arbitrary", "arbitrary"),
    )(*lands, w, m, v)


BIG = [("hyb_w_in", 2), ("hyb_w_out", 1), ("rec_w_in", 2), ("rec_w_out", 1), ("rec_w_a", 2), ("rec_w_x", 2),
       ("mlp_w1", 2), ("mlp_w2", 1)]
SMALL = [("hyb_conv_w", 2), ("rec_conv_w", 2), ("rec_conv_b", 1), ("rec_b_a", 1), ("rec_b_x", 1), ("rec_lambda", 1)]
REPL = ["hyb_sinks", "hyb_a_log", "hyb_dt_bias", "hyb_norm_w", "ln1_g", "ln1_b", "ln2_g", "ln2_b"]
WEIGHTS = ["hyb_w_in", "hyb_sinks", "hyb_conv_w", "hyb_a_log", "hyb_dt_bias", "hyb_norm_w", "hyb_w_out", "rec_w_in",
           "rec_conv_w", "rec_conv_b", "rec_w_a", "rec_b_a", "rec_w_x", "rec_b_x", "rec_lambda", "rec_w_out",
           "ln1_g", "ln1_b", "mlp_w1", "mlp_w2", "ln2_g", "ln2_b"]


def _pack_rows(parts, dtype, row_mult):
    lead = parts[0].shape[:-1]
    flat = jnp.concatenate([p.astype(dtype) for p in parts], axis=-1)
    n = flat.shape[-1]
    unit = row_mult * LANE
    pad = (-n) % unit
    if pad:
        flat = jnp.concatenate([flat, jnp.zeros(lead + (pad,), dtype)], axis=-1)
    return flat.reshape(lead + ((n + pad) // LANE, LANE))


def _gather_full(gathered, shard_shapes, table):
    flat = gathered.reshape(N_DEV, -1)
    out, off = {}, 0
    for name, ax in table:
        shp = shard_shapes[name]
        n = math.prod(shp)
        arr = flat[:, off:off + n].reshape((N_DEV,) + shp)
        off += n
        arr = jnp.moveaxis(arr, 0, ax)
        out[name] = arr.reshape(shp[:ax] + (N_DEV * shp[ax],) + shp[ax + 1:])
    return out


def _matmul_layouts(tag, gw):
    out = {}
    bw = D_MODEL // LRU_BLOCKS
    for k, g in gw.items():
        L = g.shape[1]
        if k == "hyb_w_in":
            out[k] = _merge_cols(f"{tag}_w_in_merge", g)
        elif k in ("hyb_w_out", "rec_w_out"):
            out[k] = jnp.swapaxes(g, 0, 1).reshape(L, D_MODEL, D_MODEL)
        elif k in ("rec_w_a", "rec_w_x"):
            out[k] = jnp.moveaxis(g, 0, 2).reshape(L, LRU_BLOCKS, bw, bw)
        else:
            out[k] = g
    return out


def kernel(x, hyb_w_in, hyb_sinks, hyb_conv_w, hyb_a_log, hyb_dt_bias, hyb_norm_w, hyb_w_out, rec_w_in, rec_conv_w, rec_conv_b, rec_w_a, rec_b_a, rec_w_x, rec_b_x, rec_lambda, rec_w_out, ln1_g, ln1_b, mlp_w1, mlp_w2, ln2_g, ln2_b, loss_target, m_hyb_w_in, m_hyb_sinks, m_hyb_conv_w, m_hyb_a_log, m_hyb_dt_bias, m_hyb_norm_w, m_hyb_w_out, m_rec_w_in, m_rec_conv_w, m_rec_conv_b, m_rec_w_a, m_rec_b_a, m_rec_w_x, m_rec_b_x, m_rec_lambda, m_rec_w_out, m_ln1_g, m_ln1_b, m_mlp_w1, m_mlp_w2, m_ln2_g, m_ln2_b, v_hyb_w_in, v_hyb_sinks, v_hyb_conv_w, v_hyb_a_log, v_hyb_dt_bias, v_hyb_norm_w, v_hyb_w_out, v_rec_w_in, v_rec_conv_w, v_rec_conv_b, v_rec_w_a, v_rec_b_a, v_rec_w_x, v_rec_b_x, v_rec_lambda, v_rec_w_out, v_ln1_g, v_ln1_b, v_mlp_w1, v_mlp_w2, v_ln2_g, v_ln2_b):
    args = locals()
    w = {k: args[k] for k in WEIGHTS}
    m = {k: args["m_" + k] for k in WEIGHTS}
    v = {k: args["v_" + k] for k in WEIGHTS}
    shard_shapes = {k: tuple(t.shape) for k, t in w.items()}
    xi, yi, ci = _my_coords()
    me = 4 * xi + 2 * yi + ci

    def shards(layer):
        return [w[k][i:i + 1].astype(bf16) for k, i in _layer_weights(layer)]

    gathered0 = _all_gather("gather_l0", shards(0) + [_pack_rows([w[k].reshape(-1) for k, _ in SMALL], f32, SUBLANE)])
    W = _gather_full(gathered0[-1], shard_shapes, SMALL)
    W.update({k: w[k] for k in REPL})
    W.update({k: {} for k, _ in BIG})
    in_flight = {}
    order = jnp.zeros((1, 1), f32)
    for layer in range(1, DEPTH):
        srcs = shards(layer)
        lands = _own_block(f"gather_l{layer}_own", "gather", srcs,
                           [lax.empty((N_DEV,) + s.shape, bf16) for s in srcs])
        *in_flight[layer], zero = _push_start(f"gather_l{layer}_start", "gather", srcs, lands)
        order = order + zero
    W["ln1_g"] = W["ln1_g"] + order

    def load_layer(layer, after):
        names = _layer_weights(layer)
        got = gathered0[:-1] if layer == 0 else _push_wait(f"gather_l{layer}_wait", "gather", *in_flight[layer], after)
        for (k, i), arr in zip(names, _matmul_layouts(f"L{layer}", {k: g for (k, _), g in zip(names, got)}).values()):
            W[k][i] = arr

    grads_in_flight = {}

    def layer_grads_ready(layer, g):
        srcs = [g[k] for k, _ in _layer_weights(layer)]
        lands = _own_block(f"scatter_l{layer}_own", "scatter", srcs, [lax.empty(s.shape, bf16) for s in srcs])
        *grads_in_flight[layer], zero = _push_start(f"scatter_l{layer}_start", "scatter", srcs, lands)
        return zero

    loss_local, grad_x, G = _local_step(x[0], loss_target[0], W, load_layer, layer_grads_ready)
    loss = lax.psum(loss_local, MESH_AXES)

    landed = {}
    for layer in range(DEPTH):
        got = _push_wait(f"scatter_l{layer}_wait", "scatter", *grads_in_flight[layer], grad_x)
        landed.update({ki: arr for ki, arr in zip(_layer_weights(layer), got)})
    rest = _pack_rows([G[k].reshape(-1) for k, _ in SMALL] + [G[k].reshape(-1) for k in REPL], f32, SUBLANE)
    g_rest = _sum_blocks("sum_rest", _all_gather("gather_rest", [rest])[0]).reshape(-1)

    grads, delta, new_m, new_v = {}, {}, {}, {}
    for k, _ in BIG:
        shp = shard_shapes[k]
        s3 = (shp[0], math.prod(shp[1:-1]), shp[-1])
        lands = [landed[(k, i)].reshape((N_DEV,) + s3[1:]) for i in range(shp[0])]
        res = _adamw_land("adamw_" + k, lands, w[k].reshape(s3), m[k].reshape(s3), v[k].reshape(s3))
        grads[k], delta[k], new_m[k], new_v[k] = (r.reshape(shp) for r in res)
    off = 0
    for k, ax in SMALL:
        full_shape = G[k].shape
        n = math.prod(full_shape)
        full = g_rest[off:off + n].reshape(full_shape)
        off += n
        s = shard_shapes[k][ax]
        grads[k] = lax.dynamic_slice_in_dim(full, me * s, s, axis=ax)
    for k in REPL:
        n = math.prod(shard_shapes[k])
        grads[k] = g_rest[off:off + n].reshape(shard_shapes[k])
        off += n

    for k in [k for k, _ in SMALL] + REPL:
        delta[k], new_m[k], new_v[k] = _adamw("adamw_" + k, w[k], grads[k], m[k], v[k])

    return (loss, grad_x[None], *[grads[k] for k in WEIGHTS], *[delta[k] for k in WEIGHTS],
            *[new_m[k] for k in WEIGHTS], *[new_v[k] for k in WEIGHTS])
```

```python
import functools
import math

import jax
import jax.numpy as jnp
from jax import lax
from jax.experimental import pallas as pl
from jax.experimental.pallas import tpu as pltpu

f32 = jnp.float32
bf16 = jnp.bfloat16

N_DEV = 8
D_MODEL = 1024
DEPTH = 4
A_HEAD_DIM = 64
A_Q_HEADS = 8
WINDOW = 128
ROPE_THETA = 10000.0
B_HEADS = 4
B_HEAD_DIM = 128
B_CHUNK = 64
LRU_BLOCKS = 4
LRU_C = 8.0
D_FF = 4 * D_MODEL
HYB_PROJ = 2824
HYB_PROJ_PAD = 3072
DN_ALPHA = (2 * DEPTH) ** 0.25
LN_EPS = 1e-5
NORM_EPS = 1e-6
ADAM_LR = 0.001
ADAM_B1 = 0.9
ADAM_B2 = 0.999
ADAM_EPS = 1e-08
ADAM_WD = 0.01
ADAM_STEP = 10

LANE = 128
SUBLANE = 8
VMEM_LIMIT = 48 * 1024 * 1024

CB_QA, CB_KA, CB_VA, CB_CONV, CB_Z, CB_LG = 0, 4, 5, 6, 18, 22

MESH_AXES = ("x", "y", "c")


def _cparams(*sem):
    return pltpu.CompilerParams(dimension_semantics=sem, vmem_limit_bytes=VMEM_LIMIT)


def _dot(a, b, dims, precision=None):
    return lax.dot_general(a, b, (dims, ((), ())), preferred_element_type=f32, precision=precision)


NN = ((1,), (0,))
NT = ((1,), (1,))
TN = ((0,), (0,))


def _mat_spec(arr, kind, lead, br, bc, rb, cb):
    if kind == "plain":
        return pl.BlockSpec((br, bc), lambda i, j, k: (rb(i, j, k), cb(i, j, k)))
    if kind == "lead":
        return pl.BlockSpec((None, br, bc), lambda i, j, k: (lead, rb(i, j, k), cb(i, j, k)))
    if kind == "devcol":
        assert bc == arr.shape[-1]
        return pl.BlockSpec((None, None, br, bc), lambda i, j, k: (cb(i, j, k), lead, rb(i, j, k), 0))
    assert kind == "devrow" and br == arr.shape[-2]
    return pl.BlockSpec((None, None, br, bc), lambda i, j, k: (rb(i, j, k), lead, 0, cb(i, j, k)))


def _mm(name, a, b, mode, *, b_kind="plain", b_lead=0, o_kind="plain", epilogue=None, extras=(), out_dtypes=(f32,),
        tm=1024, tn=1024, tk=None):
    if tk is None:
        tk = 512 if mode == "tn" else 1024
    if b_kind in ("plain", "lead"):
        b_rows, b_cols = b.shape[-2:]
    elif b_kind == "devcol":
        b_rows, b_cols = b.shape[-2], N_DEV * b.shape[-1]
    else:
        b_rows, b_cols = N_DEV * b.shape[-2], b.shape[-1]
    if mode == "nn":
        (M, K), (K2, N) = a.shape, (b_rows, b_cols)
    elif mode == "nt":
        (M, K), (N, K2) = a.shape, (b_rows, b_cols)
    else:
        (K, M), (K2, N) = a.shape, (b_rows, b_cols)
    assert K == K2, (name, a.shape, b.shape, mode)
    tm, tn, tk = min(tm, M), min(tn, N), min(tk, K)
    cols_are_n = mode != "nt"
    if b_kind == "devcol":
        tn, tk = (b.shape[-1], tk) if cols_are_n else (tn, b.shape[-1])
    if b_kind == "devrow":
        tn, tk = (tn, b.shape[-2]) if cols_are_n else (b.shape[-2], tk)
    shard = N // N_DEV
    if o_kind == "devcol":
        tn = max(shard, tn // shard * shard)
    assert M % tm == 0 and N % tn == 0 and K % tk == 0, (name, M, N, K, tm, tn, tk)
    nk = K // tk
    dims = {"nn": NN, "nt": NT, "tn": TN}[mode]
    n_ex, n_out = len(extras), len(out_dtypes)

    def body(*refs):
        a_ref, b_ref = refs[:2]
        ex = refs[2:2 + n_ex]
        outs = refs[2 + n_ex:2 + n_ex + n_out]
        acc = refs[-1]
        k = pl.program_id(2)

        @pl.when(k == 0)
        def _():
            acc[...] = jnp.zeros_like(acc)

        acc[...] += _dot(a_ref[...].astype(bf16), b_ref[...].astype(bf16), dims)

        @pl.when(k == nk - 1)
        def _():
            r = acc[...]
            res = epilogue(r, *[e[...] for e in ex]) if epilogue is not None else (r,)
            for o, v in zip(outs, res):
                if o_kind == "plain":
                    o[...] = v.astype(o.dtype)
                else:
                    for q in range(tn // shard):
                        o[q] = v[:, q * shard:(q + 1) * shard].astype(o.dtype)

    if mode == "tn":
        a_spec = pl.BlockSpec((tk, tm), lambda i, j, k: (k, i))
    else:
        a_spec = pl.BlockSpec((tm, tk), lambda i, j, k: (i, k))
    jb, kb = (lambda i, j, k: j), (lambda i, j, k: k)
    if mode == "nt":
        b_spec = _mat_spec(b, b_kind, b_lead, tn, tk, jb, kb)
    else:
        b_spec = _mat_spec(b, b_kind, b_lead, tk, tn, kb, jb)
    e_spec = pl.BlockSpec((tm, tn), lambda i, j, k: (i, j))
    if o_kind == "plain":
        o_spec, o_shape = e_spec, (M, N)
    else:
        o_spec, o_shape = pl.BlockSpec((tn // shard, tm, shard), lambda i, j, k: (j, i, 0)), (N_DEV, M, shard)
    res = pl.pallas_call(
        body, name=name,
        grid=(M // tm, N // tn, nk),
        in_specs=[a_spec, b_spec] + [e_spec] * n_ex,
        out_specs=[o_spec] * n_out,
        out_shape=[jax.ShapeDtypeStruct(o_shape, dt) for dt in out_dtypes],
        scratch_shapes=[pltpu.VMEM((tm, tn), f32)],
        compiler_params=_cparams("parallel", "parallel", "arbitrary"),
    )(a, b, *extras)
    return res[0] if n_out == 1 else res


def _row_spec(tm, cb, width):
    assert (cb * LANE) % width == 0
    blk = (cb * LANE) // width
    return pl.BlockSpec((tm, width), lambda i: (i, blk))


def _whole_spec(p):
    nd = p.ndim
    return pl.BlockSpec(p.shape, lambda i: (0,) * nd)


def _tl_fwd(name, fn, rows, params, out_widths, out_dtypes, tm=256):
    T = rows[0][0].shape[0]
    tm = min(tm, T)
    nr, npar = len(rows), len(params)

    def body(*refs):
        vals = [r[...] for r in refs[:nr + npar]]
        outs = fn(*vals)
        for o, v in zip(refs[nr + npar:], outs):
            o[...] = v.astype(o.dtype)

    res = pl.pallas_call(
        body, name=name, grid=(T // tm,),
        in_specs=[_row_spec(tm, cb, w) for (_, cb, w) in rows] + [_whole_spec(p) for p in params],
        out_specs=[pl.BlockSpec((tm, w), lambda i: (i, 0)) for w in out_widths],
        out_shape=[jax.ShapeDtypeStruct((T, w), dt) for w, dt in zip(out_widths, out_dtypes)],
        compiler_params=_cparams("parallel"),
    )(*[r[0] for r in rows], *params)
    return res


def _tl_bwd(name, fn, rows, params, cot_rows, cot_fn=None, tm=256):
    T = rows[0][0].shape[0]
    tm = min(tm, T)
    nr, npar, nc = len(rows), len(params), len(cot_rows)

    def body(*refs):
        vals = [r[...] for r in refs[:nr + npar]]
        cots = [r[...] for r in refs[nr + npar:nr + npar + nc]]
        outs = refs[nr + npar + nc:]
        cot = tuple(cot_fn(*cots)) if cot_fn is not None else tuple(cots)
        _, vjp = jax.vjp(fn, *vals)
        grads = vjp(cot)
        for o, g in zip(outs[:nr], grads[:nr]):
            o[...] = g.astype(o.dtype)
        i = pl.program_id(0)
        for o, g in zip(outs[nr:], grads[nr:]):
            @pl.when(i == 0)
            def _(o=o):
                o[...] = jnp.zeros_like(o)
            o[...] += g

    res = pl.pallas_call(
        body, name=name, grid=(T // tm,),
        in_specs=[_row_spec(tm, cb, w) for (_, cb, w) in rows] + [_whole_spec(p) for p in params]
        + [_row_spec(tm, cb, w) for (_, cb, w) in cot_rows],
        out_specs=[pl.BlockSpec((tm, w), lambda i: (i, 0)) for (_, _, w) in rows] + [_whole_spec(p) for p in params],
        out_shape=[jax.ShapeDtypeStruct((T, w), f32) for (_, _, w) in rows]
        + [jax.ShapeDtypeStruct(p.shape, f32) for p in params],
        compiler_params=_cparams("arbitrary"),
    )(*[r[0] for r in rows], *params, *[r[0] for r in cot_rows])
    return res[:nr], res[nr:]


def _ln_res_fn(x, mix, g, b):
    pre = DN_ALPHA * x + mix
    mu = jnp.mean(pre, axis=-1, keepdims=True)
    var = jnp.mean(jnp.square(pre - mu), axis=-1, keepdims=True)
    return ((pre - mu) * lax.rsqrt(var + LN_EPS) * g + b,)


@jax.custom_jvp
def _expm1(x):
    small = jnp.abs(x) < 0.3
    xs = jnp.where(small, x, 0.0)
    poly = xs * (1.0 + xs * (1 / 2 + xs * (1 / 6 + xs * (1 / 24 + xs * (1 / 120 + xs * (
        1 / 720 + xs * (1 / 5040 + xs * (1 / 40320 + xs * (1 / 362880)))))))))
    return jnp.where(small, poly, jnp.exp(x) - 1.0)


@_expm1.defjvp
def _expm1_jvp(primals, tangents):
    (x,), (t,) = primals, tangents
    return _expm1(x), t * jnp.exp(x)


def _rglru_pre_fn(pre_r, pre_i, xc, b_a, b_x, lam):
    r = jax.nn.sigmoid(pre_r + b_a)
    i = jax.nn.sigmoid(pre_i + b_x)
    log_a = -LRU_C * r * jax.nn.softplus(-lam)
    a = jnp.exp(log_a)
    b = jnp.sqrt(-_expm1(2.0 * log_a)) * (i * xc)
    return a, b


def _rec_gate_fn(h, gate):
    return (h * jax.nn.gelu(gate),)


def _loss_head(y, t, tm=256):
    T, Dm = y.shape

    def body(y_ref, t_ref, dy_ref, loss_ref):
        e = y_ref[...] - t_ref[...]
        dy_ref[...] = e * (1.0 / Dm)

        @pl.when(pl.program_id(0) == 0)
        def _():
            loss_ref[...] = jnp.zeros_like(loss_ref)

        loss_ref[...] += 0.5 * jnp.sum(jnp.mean(e * e, axis=-1, keepdims=True), axis=0, keepdims=True)

    dy, loss = pl.pallas_call(
        body, name="loss_head", grid=(T // tm,),
        in_specs=[pl.BlockSpec((tm, Dm), lambda i: (i, 0))] * 2,
        out_specs=[pl.BlockSpec((tm, Dm), lambda i: (i, 0)), pl.BlockSpec((SUBLANE, LANE), lambda i: (0, 0))],
        out_shape=[jax.ShapeDtypeStruct((T, Dm), f32), jax.ShapeDtypeStruct((SUBLANE, LANE), f32)],
        compiler_params=_cparams("arbitrary"),
    )(y, t)
    return loss[0, 0], dy


def _conv_fwd(name, x, cb0, nblk, w, bias, tm=512):
    T = x.shape[0]
    tm = min(tm, T)
    hb = tm // SUBLANE
    has_b = bias is not None

    def body(*refs):
        cur, prev, w_ref = refs[:3]
        b_ref = refs[3] if has_b else None
        o = refs[-1]
        i = pl.program_id(1)
        p = jnp.where(i > 0, prev[...], 0.0)
        xcat = jnp.concatenate([p, cur[...]], axis=0)
        acc = cur[...] * w_ref[3:4, :]
        for j in range(3):
            acc = acc + pltpu.roll(xcat, 3 - j, axis=0)[SUBLANE:] * w_ref[j:j + 1, :]
        if has_b:
            acc = acc + b_ref[...]
        o[...] = acc

    in_specs = [
        pl.BlockSpec((tm, LANE), lambda c, i: (i, cb0 + c)),
        pl.BlockSpec((SUBLANE, LANE), lambda c, i: (jnp.maximum(i * hb - 1, 0), cb0 + c)),
        pl.BlockSpec((4, LANE), lambda c, i: (0, c)),
    ]
    args = [x, x, w]
    if has_b:
        in_specs.append(pl.BlockSpec((1, LANE), lambda c, i: (0, c)))
        args.append(bias)
    return pl.pallas_call(
        body, name=name, grid=(nblk, T // tm),
        in_specs=in_specs,
        out_specs=pl.BlockSpec((tm, LANE), lambda c, i: (i, c)),
        out_shape=jax.ShapeDtypeStruct((T, nblk * LANE), f32),
        compiler_params=_cparams("parallel", "parallel"),
    )(*args)


def _conv_bwd(name, dy, x, cb0, nblk, w, tm=512):
    T = x.shape[0]
    tm = min(tm, T)
    hb = tm // SUBLANE
    nt = T // tm

    def body(dcur, dnext, xcur, xprev, w_ref, dx_ref, dw_ref, db_ref):
        i = pl.program_id(1)
        d = dcur[...]
        dn = jnp.where(i < nt - 1, dnext[...], 0.0)
        dcat = jnp.concatenate([d, dn], axis=0)
        acc = d * w_ref[3:4, :]
        for j in range(3):
            s = 3 - j
            acc = acc + pltpu.roll(dcat, tm + SUBLANE - s, axis=0)[:tm] * w_ref[j:j + 1, :]
        dx_ref[...] = acc

        p = jnp.where(i > 0, xprev[...], 0.0)
        xcat = jnp.concatenate([p, xcur[...]], axis=0)
        rows = [jnp.sum(d * pltpu.roll(xcat, 3 - j, axis=0)[SUBLANE:], axis=0, keepdims=True) for j in range(3)]
        rows.append(jnp.sum(d * xcur[...], axis=0, keepdims=True))
        rows.append(jnp.zeros((SUBLANE - 4, LANE), f32))

        @pl.when(i == 0)
        def _():
            dw_ref[...] = jnp.zeros_like(dw_ref)
            db_ref[...] = jnp.zeros_like(db_ref)

        dw_ref[...] += jnp.concatenate(rows, axis=0)
        db_ref[...] += jnp.broadcast_to(jnp.sum(d, axis=0, keepdims=True), (SUBLANE, LANE))

    nh = T // SUBLANE
    dx, dw, db = pl.pallas_call(
        body, name=name, grid=(nblk, nt),
        in_specs=[
            pl.BlockSpec((tm, LANE), lambda c, i: (i, c)),
            pl.BlockSpec((SUBLANE, LANE), lambda c, i: (jnp.minimum((i + 1) * hb, nh - 1), c)),
            pl.BlockSpec((tm, LANE), lambda c, i: (i, cb0 + c)),
            pl.BlockSpec((SUBLANE, LANE), lambda c, i: (jnp.maximum(i * hb - 1, 0), cb0 + c)),
            pl.BlockSpec((4, LANE), lambda c, i: (0, c)),
        ],
        out_specs=[
            pl.BlockSpec((tm, LANE), lambda c, i: (i, c)),
            pl.BlockSpec((SUBLANE, LANE), lambda c, i: (0, c)),
            pl.BlockSpec((SUBLANE, LANE), lambda c, i: (0, c)),
        ],
        out_shape=[jax.ShapeDtypeStruct((T, nblk * LANE), f32),
                   jax.ShapeDtypeStruct((SUBLANE, nblk * LANE), f32),
                   jax.ShapeDtypeStruct((SUBLANE, nblk * LANE), f32)],
        compiler_params=_cparams("parallel", "arbitrary"),
    )(dy, dy, x, x, w)
    return dx, dw[:4], db[0]


@functools.partial(jax.custom_vjp, nondiff_argnums=(1,))
def _lroll(x, s):
    return pltpu.roll(x, s, axis=1)


def _lroll_fwd(x, s):
    return _lroll(x, s), None


def _lroll_bwd(s, _, g):
    return (_lroll(g, (LANE - s) % LANE),)


_lroll.defvjp(_lroll_fwd, _lroll_bwd)


def _rope_tables(T):
    half = A_HEAD_DIM // 2
    inv_freq = ROPE_THETA ** (-jnp.arange(half, dtype=f32) / half)
    ang = jnp.arange(T, dtype=f32)[:, None] * inv_freq[None, :]
    cos, sin = jnp.cos(ang), jnp.sin(ang)
    return jnp.tile(jnp.concatenate([cos, cos], axis=1), (1, 2)), jnp.tile(jnp.concatenate([-sin, sin], axis=1), (1, 2))


def _attn_block_fn(n, q, kp, kc, vp, vc, cq, sq, cp, sp, sinks):
    W = WINDOW
    lane = lax.broadcasted_iota(jnp.int32, (W, LANE), 1)
    lo_half = (lane % A_HEAD_DIM) < (A_HEAD_DIM // 2)
    lane8 = lax.broadcasted_iota(jnp.int32, sinks.shape, 1)

    def rope(x, c, s):
        return x * c + jnp.where(lo_half, _lroll(x, LANE - A_HEAD_DIM // 2), _lroll(x, A_HEAD_DIM // 2)) * s

    k2 = jnp.concatenate([rope(kp, cp, sp), rope(kc, cq, sq)], axis=0).astype(bf16)
    v2 = jnp.concatenate([vp, vc], axis=0).astype(bf16)
    row = lax.broadcasted_iota(jnp.int32, (W, 2 * W), 0)
    col = lax.broadcasted_iota(jnp.int32, (W, 2 * W), 1)
    dist = row + W - col
    mask = (dist >= 0) & (dist < W) & ((col >= W) | (n > 0))
    outs = []
    for t in range(4):
        qt = rope(q[:, LANE * t:LANE * (t + 1)], cq, sq)
        g = t // 2
        ot = jnp.zeros((W, LANE), f32)
        for hh in range(2):
            qa = jnp.where((lane // A_HEAD_DIM) == hh, qt, 0.0)
            if hh != g:
                qa = _lroll(qa, A_HEAD_DIM)
            s = _dot(qa.astype(bf16), k2, NT) * (A_HEAD_DIM ** -0.5)
            s = jnp.where(mask, s, -jnp.inf)
            sink = jnp.sum(jnp.where(lane8 == 2 * t + hh, sinks, 0.0), axis=1, keepdims=True)
            m = jnp.maximum(jnp.max(s, axis=-1, keepdims=True), sink)
            e = jnp.exp(s - m)
            p = e / (jnp.sum(e, axis=-1, keepdims=True) + jnp.exp(sink - m))
            o = _dot(p.astype(bf16), v2, NN)
            o = jnp.where((lane // A_HEAD_DIM) == g, o, 0.0)
            if hh != g:
                o = _lroll(o, A_HEAD_DIM)
            ot = ot + o
        outs.append(ot)
    return jnp.concatenate(outs, axis=1)


def _attn_specs():
    W = WINDOW
    prev = lambda n: jnp.maximum(n - 1, 0)
    return [
        pl.BlockSpec((W, 4 * LANE), lambda n: (n, CB_QA // 4)),
        pl.BlockSpec((W, LANE), lambda n: (prev(n), CB_KA)),
        pl.BlockSpec((W, LANE), lambda n: (n, CB_KA)),
        pl.BlockSpec((W, LANE), lambda n: (prev(n), CB_VA)),
        pl.BlockSpec((W, LANE), lambda n: (n, CB_VA)),
        pl.BlockSpec((W, LANE), lambda n: (n, 0)),
        pl.BlockSpec((W, LANE), lambda n: (n, 0)),
        pl.BlockSpec((W, LANE), lambda n: (prev(n), 0)),
        pl.BlockSpec((W, LANE), lambda n: (prev(n), 0)),
        pl.BlockSpec((1, A_Q_HEADS), lambda n: (0, 0)),
    ]


def _attn_fwd(name, proj, cos, sin, sinks):
    T = proj.shape[0]
    W = WINDOW

    def body(*refs):
        o = refs[-1]
        o[...] = _attn_block_fn(pl.program_id(0), *[r[...] for r in refs[:-1]])

    return pl.pallas_call(
        body, name=name, grid=(T // W,),
        in_specs=_attn_specs(),
        out_specs=pl.BlockSpec((W, 4 * LANE), lambda n: (n, 0)),
        out_shape=jax.ShapeDtypeStruct((T, 4 * LANE), f32),
        compiler_params=_cparams("parallel"),
    )(proj, proj, proj, proj, proj, cos, sin, cos, sin, sinks)


def _attn_bwd(name, proj, cos, sin, sinks, d_oab):
    T = proj.shape[0]
    W = WINDOW

    def body(*refs):
        ins = [r[...] for r in refs[:10]]
        do = refs[10][...]
        dq_ref, dk_ref, dv_ref, ds_ref = refs[11:]
        n = pl.program_id(0)
        _, vjp = jax.vjp(functools.partial(_attn_block_fn, n), *ins)
        dq, dkp, dkc, dvp, dvc, _, _, _, _, dsk = vjp(do)
        dq_ref[...] = dq

        @pl.when(n == 0)
        def _():
            dk_ref[...] = jnp.zeros_like(dk_ref)
            dv_ref[...] = jnp.zeros_like(dv_ref)
            ds_ref[...] = jnp.zeros_like(ds_ref)

        cur = pl.ds(pl.multiple_of(n * W, W), W)
        dk_ref[cur, :] += dkc
        dv_ref[cur, :] += dvc
        ds_ref[...] += dsk

        @pl.when(n > 0)
        def _():
            prv = pl.ds(pl.multiple_of((n - 1) * W, W), W)
            dk_ref[prv, :] += dkp
            dv_ref[prv, :] += dvp

    return pl.pallas_call(
        body, name=name, grid=(T // W,),
        in_specs=_attn_specs() + [pl.BlockSpec((W, 4 * LANE), lambda n: (n, 0))],
        out_specs=[pl.BlockSpec((W, 4 * LANE), lambda n: (n, 0)),
                   pl.BlockSpec((T, LANE), lambda n: (0, 0)),
                   pl.BlockSpec((T, LANE), lambda n: (0, 0)),
                   pl.BlockSpec((1, A_Q_HEADS), lambda n: (0, 0))],
        out_shape=[jax.ShapeDtypeStruct((T, 4 * LANE), f32), jax.ShapeDtypeStruct((T, LANE), f32),
                   jax.ShapeDtypeStruct((T, LANE), f32), jax.ShapeDtypeStruct((1, A_Q_HEADS), f32)],
        compiler_params=_cparams("arbitrary"),
    )(proj, proj, proj, proj, proj, cos, sin, cos, sin, sinks, d_oab)


def _bdot(spec, a, b, precision=None):
    return jnp.einsum(spec, a, b, preferred_element_type=f32, precision=precision)


@jax.custom_vjp
def _tri_inv(a):
    C = a.shape[-1]
    r = lax.broadcasted_iota(jnp.int32, (C, C), 0)
    c = lax.broadcasted_iota(jnp.int32, (C, C), 1)
    t = jnp.broadcast_to(jnp.where(r == c, 1.0, 0.0).astype(f32), a.shape)
    for j in range(C - 1):
        t = t - a[:, :, j:j + 1] * t[:, j:j + 1, :]
    return t


def _tri_inv_fwd(a):
    t = _tri_inv(a)
    return t, t


def _tri_inv_bwd(t, g):
    C = t.shape[-1]
    r = lax.broadcasted_iota(jnp.int32, (C, C), 0)
    c = lax.broadcasted_iota(jnp.int32, (C, C), 1)
    x = _bdot("hki,hkj->hij", t, g, precision=lax.Precision.HIGHEST)
    y = _bdot("hik,hjk->hij", x, t, precision=lax.Precision.HIGHEST)
    return (jnp.where(r > c, -y, 0.0),)


_tri_inv.defvjp(_tri_inv_fwd, _tri_inv_bwd)


@jax.custom_vjp
def _tri_inv_saved(a, t):
    return t


_tri_inv_saved.defvjp(lambda a, t: (t, t), lambda t, g: (_tri_inv_bwd(t, g)[0], jnp.zeros_like(t)))


def _silu(x):
    return x * jax.nn.sigmoid(x)


def _l2n(x):
    return x * lax.rsqrt(jnp.sum(x * x, axis=-1, keepdims=True) + NORM_EPS)


def _delta_chunk_fn(cq, ck, cv, z, lg, a_log, dt_bias, norm_w, S, t_saved=None, want_t=False):
    C = B_CHUNK
    lane = lax.broadcasted_iota(jnp.int32, (C, LANE), 1)
    pick = lambda l0: jnp.concatenate(
        [jnp.sum(jnp.where(lane == l0 + h, lg, 0.0), axis=1, keepdims=True)[None] for h in range(B_HEADS)], axis=0)
    bl, al = pick(0), pick(B_HEADS)
    q = _l2n(_silu(cq)) * (B_HEAD_DIM ** -0.5)
    k = _l2n(_silu(ck))
    v = _silu(cv)
    beta = jax.nn.sigmoid(bl)
    g = -jnp.exp(a_log) * jax.nn.softplus(al + dt_bias)
    r = lax.broadcasted_iota(jnp.int32, (C, C), 0)
    c = lax.broadcasted_iota(jnp.int32, (C, C), 1)
    eye = r == c
    g_row = jnp.sum(jnp.where(eye, g, 0.0), axis=1, keepdims=True)
    gc = jnp.sum(jnp.where(c <= r, g_row, 0.0), axis=2, keepdims=True)
    gc_row = jnp.sum(jnp.where(eye, gc, 0.0), axis=1, keepdims=True)
    decay_incl = jnp.exp(jnp.where(r >= c, gc - gc_row, -jnp.inf))
    decay_strict = jnp.where(r > c, decay_incl, 0.0)
    kb = k * beta
    vb = v * beta
    kbf = k.astype(bf16)
    a_mat = _bdot("hik,hjk->hij", kb.astype(bf16), kbf) * decay_strict
    t_f32 = _tri_inv(a_mat) if t_saved is None else _tri_inv_saved(a_mat, t_saved)
    t_mat = t_f32.astype(bf16)
    eg = jnp.exp(gc)
    u = _bdot("hij,hjv->hiv", t_mat, vb.astype(bf16))
    w = _bdot("hij,hjk->hik", t_mat, (kb * eg).astype(bf16))
    qk = _bdot("hik,hjk->hij", q.astype(bf16), kbf) * decay_incl
    g_last = jnp.sum(g, axis=1, keepdims=True)
    k_tail = k * jnp.exp(g_last - gc)
    Sb = S.astype(bf16)
    v_new = u - _bdot("hck,hkv->hcv", w.astype(bf16), Sb)
    o = _bdot("hck,hkv->hcv", (q * eg).astype(bf16), Sb) + _bdot("hij,hjv->hiv", qk.astype(bf16), v_new.astype(bf16))
    S_new = S * jnp.exp(g_last) + _bdot("hck,hcv->hkv", k_tail.astype(bf16), v_new.astype(bf16))
    ob = o * lax.rsqrt(jnp.mean(o * o, axis=-1, keepdims=True) + NORM_EPS) * norm_w
    return (ob * _silu(z), S_new) + ((t_f32,) if want_t else ())


def _delta_in_specs(rev, N):
    C = B_CHUNK
    ix = (lambda n: N - 1 - n) if rev else (lambda n: n)
    specs = [pl.BlockSpec((C, 3 * B_HEADS * LANE), lambda n: (ix(n), 0))]
    specs += [pl.BlockSpec((C, LANE), lambda n, h=h: (ix(n), CB_Z + h)) for h in range(B_HEADS)]
    specs += [
        pl.BlockSpec((C, LANE), lambda n: (ix(n), CB_LG)),
        pl.BlockSpec((B_HEADS, 1, 1), lambda n: (0, 0, 0)),
        pl.BlockSpec((B_HEADS, 1, 1), lambda n: (0, 0, 0)),
        pl.BlockSpec((1, LANE), lambda n: (0, 0)),
    ]
    return specs


def _delta_inputs(c_ref, z_refs, lg, al, dt, nw):
    H = B_HEADS
    part = lambda p: jnp.stack([c_ref[:, LANE * (p * H + h):LANE * (p * H + h + 1)] for h in range(H)])
    return (part(0), part(1), part(2), jnp.stack([z[...] for z in z_refs]), lg[...], al[...], dt[...], nw[...])


def _delta_fwd(name, c, proj, a_log, dt_bias, norm_w):
    T = c.shape[0]
    C = B_CHUNK
    N = T // C
    Dh = B_HEAD_DIM
    H = B_HEADS

    def body(*refs):
        c_ref, z_refs, (lg, al, dt, nw) = refs[0], refs[1:1 + H], refs[1 + H:5 + H]
        o_ref, s_ref, t_ref, S = refs[5 + H:]

        @pl.when(pl.program_id(0) == 0)
        def _():
            S[...] = jnp.zeros_like(S)

        s0 = S[...]
        s_ref[...] = s0
        ob, s1, t = _delta_chunk_fn(*_delta_inputs(c_ref, z_refs, lg, al, dt, nw), s0, want_t=True)
        for h in range(H):
            o_ref[:, LANE * h:LANE * (h + 1)] = ob[h]
        t_ref[...] = t
        S[...] = s1

    return pl.pallas_call(
        body, name=name, grid=(N,),
        in_specs=_delta_in_specs(False, N),
        out_specs=[pl.BlockSpec((C, H * LANE), lambda n: (n, 0)),
                   pl.BlockSpec((H, None, Dh, Dh), lambda n: (0, n, 0, 0)),
                   pl.BlockSpec((H, None, C, C), lambda n: (0, n, 0, 0))],
        out_shape=[jax.ShapeDtypeStruct((T, H * Dh), f32), jax.ShapeDtypeStruct((H, N, Dh, Dh), f32),
                   jax.ShapeDtypeStruct((H, N, C, C), f32)],
        scratch_shapes=[pltpu.VMEM((H, Dh, Dh), f32)],
        compiler_params=_cparams("arbitrary"),
    )(c, *([proj] * H), proj, a_log, dt_bias, norm_w)


def _delta_bwd(name, c, proj, a_log, dt_bias, norm_w, s_saved, t_saved, d_oab):
    T = c.shape[0]
    C = B_CHUNK
    N = T // C
    Dh = B_HEAD_DIM
    H = B_HEADS

    def body(*refs):
        c_ref, z_refs, (lg, al, dt, nw) = refs[0], refs[1:1 + H], refs[1 + H:5 + H]
        s_ref, t_ref, do_ref = refs[5 + H:8 + H]
        dc, dz, dlg, dal, ddt, dnw, dS = refs[8 + H:]

        @pl.when(pl.program_id(0) == 0)
        def _():
            dS[...] = jnp.zeros_like(dS)
            dal[...] = jnp.zeros_like(dal)
            ddt[...] = jnp.zeros_like(ddt)
            dnw[...] = jnp.zeros_like(dnw)

        _, vjp = jax.vjp(functools.partial(_delta_chunk_fn, t_saved=t_ref[...]),
                         *_delta_inputs(c_ref, z_refs, lg, al, dt, nw), s_ref[...])
        do = jnp.stack([do_ref[:, LANE * h:LANE * (h + 1)] for h in range(H)])
        g = vjp((do, dS[...]))
        for h in range(H):
            for p in range(3):
                dc[:, LANE * (p * H + h):LANE * (p * H + h + 1)] = g[p][h]
            dz[:, LANE * h:LANE * (h + 1)] = g[3][h]
        dlg[...] = g[4]
        dal[...] += g[5]
        ddt[...] += g[6]
        dnw[...] += g[7]
        dS[...] = g[8]

    rn = lambda n: N - 1 - n
    return pl.pallas_call(
        body, name=name, grid=(N,),
        in_specs=_delta_in_specs(True, N) + [
            pl.BlockSpec((H, None, Dh, Dh), lambda n: (0, rn(n), 0, 0)),
            pl.BlockSpec((H, None, C, C), lambda n: (0, rn(n), 0, 0)),
            pl.BlockSpec((C, H * LANE), lambda n: (rn(n), 1)),
        ],
        out_specs=[
            pl.BlockSpec((C, 3 * H * LANE), lambda n: (rn(n), 0)),
            pl.BlockSpec((C, H * LANE), lambda n: (rn(n), 0)),
            pl.BlockSpec((C, LANE), lambda n: (rn(n), 0)),
            pl.BlockSpec((H, 1, 1), lambda n: (0, 0, 0)),
            pl.BlockSpec((H, 1, 1), lambda n: (0, 0, 0)),
            pl.BlockSpec((1, LANE), lambda n: (0, 0)),
        ],
        out_shape=[jax.ShapeDtypeStruct((T, 3 * H * Dh), f32), jax.ShapeDtypeStruct((T, H * Dh), f32),
                   jax.ShapeDtypeStruct((T, LANE), f32), jax.ShapeDtypeStruct((H, 1, 1), f32),
                   jax.ShapeDtypeStruct((H, 1, 1), f32), jax.ShapeDtypeStruct((1, LANE), f32)],
        scratch_shapes=[pltpu.VMEM((H, Dh, Dh), f32)],
        compiler_params=_cparams("arbitrary"),
    )(c, *([proj] * H), proj, a_log, dt_bias, norm_w, s_saved, t_saved, d_oab)


def _blockdiag_fwd(name, xc, w_a, w_x, tm=512):
    T, Wd = xc.shape
    bw = Wd // LRU_BLOCKS
    tm = min(tm, T)

    def body(x_ref, wa_ref, wx_ref, oa, ox):
        xb = x_ref[...].astype(bf16)
        oa[...] = _dot(xb, wa_ref[...].astype(bf16), NN)
        ox[...] = _dot(xb, wx_ref[...].astype(bf16), NN)

    xs = pl.BlockSpec((tm, bw), lambda i, h: (i, h))
    ws = pl.BlockSpec((None, bw, bw), lambda i, h: (h, 0, 0))
    return pl.pallas_call(
        body, name=name, grid=(T // tm, LRU_BLOCKS), in_specs=[xs, ws, ws], out_specs=[xs, xs],
        out_shape=[jax.ShapeDtypeStruct((T, Wd), f32)] * 2,
        compiler_params=_cparams("parallel", "parallel"),
    )(xc, w_a, w_x)


def _blockdiag_bwd_dx(name, dpr, dpi, w_a, w_x, addend, tm=512):
    T, Wd = dpr.shape
    bw = Wd // LRU_BLOCKS
    tm = min(tm, T)

    def body(dr, di, wa_ref, wx_ref, add, o):
        o[...] = (add[...] + _dot(dr[...].astype(bf16), wa_ref[...].astype(bf16), NT)
                  + _dot(di[...].astype(bf16), wx_ref[...].astype(bf16), NT))

    xs = pl.BlockSpec((tm, bw), lambda i, h: (i, h))
    ws = pl.BlockSpec((None, bw, bw), lambda i, h: (h, 0, 0))
    return pl.pallas_call(
        body, name=name, grid=(T // tm, LRU_BLOCKS), in_specs=[xs, xs, ws, ws, xs], out_specs=xs,
        out_shape=jax.ShapeDtypeStruct((T, Wd), f32),
        compiler_params=_cparams("parallel", "parallel"),
    )(dpr, dpi, w_a, w_x, addend)


def _blockdiag_bwd_dw(name, xc, dpr, dpi, tk=512):
    T, Wd = xc.shape
    bw = Wd // LRU_BLOCKS
    tk = min(tk, T)

    def body(x_ref, dr, di, oa, ox):
        @pl.when(pl.program_id(1) == 0)
        def _():
            oa[...] = jnp.zeros_like(oa)
            ox[...] = jnp.zeros_like(ox)

        xb = x_ref[...].astype(bf16)
        oa[...] += _dot(xb, dr[...].astype(bf16), TN)
        ox[...] += _dot(xb, di[...].astype(bf16), TN)

    xs = pl.BlockSpec((tk, bw), lambda h, k: (k, h))
    ws = pl.BlockSpec((None, bw, bw), lambda h, k: (h, 0, 0))
    return pl.pallas_call(
        body, name=name, grid=(LRU_BLOCKS, T // tk), in_specs=[xs, xs, xs], out_specs=[ws, ws],
        out_shape=[jax.ShapeDtypeStruct((LRU_BLOCKS, bw, bw), f32)] * 2,
        compiler_params=_cparams("parallel", "arbitrary"),
    )(xc, dpr, dpi)


def _scan(name, a, b, reverse, tt=512, cb=512):
    T, Wd = a.shape
    tt, cb = min(tt, T), min(cb, Wd)
    nt = T // tt
    ng = tt // SUBLANE

    def body(a_ref, b_ref, o_ref, carry):
        @pl.when(pl.program_id(1) == 0)
        def _():
            carry[...] = jnp.zeros_like(carry)

        row = lax.broadcasted_iota(jnp.int32, (SUBLANE, cb), 0)

        def step(gi, hp):
            g = (ng - 1 - gi) if reverse else gi
            off = pl.multiple_of(g * SUBLANE, SUBLANE)
            A = a_ref[pl.ds(off, SUBLANE), :]
            B = b_ref[pl.ds(off, SUBLANE), :]
            for s in (1, 2, 4):
                sh = (SUBLANE - s) if reverse else s
                As = pltpu.roll(A, sh, axis=0)
                Bs = pltpu.roll(B, sh, axis=0)
                valid = (row < SUBLANE - s) if reverse else (row >= s)
                B = jnp.where(valid, A * Bs + B, B)
                A = jnp.where(valid, A * As, A)
            hcur = A * hp + B
            o_ref[pl.ds(off, SUBLANE), :] = hcur
            edge = hcur[0:1, :] if reverse else hcur[SUBLANE - 1:SUBLANE, :]
            return jnp.broadcast_to(edge, (SUBLANE, cb))

        carry[...] = lax.fori_loop(0, ng, step, carry[...])

    ti = (lambda c, i: (nt - 1 - i, c)) if reverse else (lambda c, i: (i, c))
    spec = pl.BlockSpec((tt, cb), ti)
    return pl.pallas_call(
        body, name=name, grid=(Wd // cb, nt), in_specs=[spec, spec], out_specs=spec,
        out_shape=jax.ShapeDtypeStruct((T, Wd), f32),
        scratch_shapes=[pltpu.VMEM((SUBLANE, cb), f32)],
        compiler_params=_cparams("parallel", "arbitrary"),
    )(a, b)


def _relu2_epilogue(r):
    h = jnp.maximum(r, 0.0)
    return r, h * h


def _drelu2_epilogue(r, a):
    return (r * (2.0 * jnp.maximum(a, 0.0)),)


def _add_epilogue(r, e):
    return (r + e,)


def _merge_cols(name, g, tm=256):
    _, L, R, s = g.shape

    def body(g_ref, o_ref):
        for d in range(N_DEV):
            o_ref[:, s * d:s * (d + 1)] = g_ref[d].astype(bf16)
        o_ref[:, N_DEV * s:] = jnp.zeros((tm, HYB_PROJ_PAD - N_DEV * s), bf16)

    return pl.pallas_call(
        body, name=name, grid=(L, R // tm),
        in_specs=[pl.BlockSpec((N_DEV, None, tm, s), lambda l, i: (0, l, i, 0))],
        out_specs=pl.BlockSpec((None, tm, HYB_PROJ_PAD), lambda l, i: (l, i, 0)),
        out_shape=jax.ShapeDtypeStruct((L, R, HYB_PROJ_PAD), bf16),
        compiler_params=_cparams("parallel", "parallel"),
    )(g)


def _split_cols(name, dw, tm=256):
    R = dw.shape[0]
    s = HYB_PROJ // N_DEV

    def body(g_ref, o_ref):
        for d in range(N_DEV):
            o_ref[d] = g_ref[:, s * d:s * (d + 1)].astype(bf16)

    return pl.pallas_call(
        body, name=name, grid=(R // tm,),
        in_specs=[pl.BlockSpec((tm, HYB_PROJ_PAD), lambda i: (i, 0))],
        out_specs=pl.BlockSpec((N_DEV, tm, s), lambda i: (0, i, 0)),
        out_shape=jax.ShapeDtypeStruct((N_DEV, R, s), bf16),
        compiler_params=_cparams("parallel"),
    )(dw)


def _rows_to_dev(dw):
    nb, r, c = dw.shape
    t = dw.reshape(nb, N_DEV, r // N_DEV, c)
    return jnp.moveaxis(t, 1, 0).reshape(N_DEV, nb * (r // N_DEV), c).astype(bf16)


def _hybrid_fwd(tag, x, W, j, cos, sin):
    proj = _mm(f"{tag}_proj", x, W["hyb_w_in"][j], "nn", b_kind="lead", b_lead=0)
    o_a = _attn_fwd(f"{tag}_attn", proj, cos, sin, W["hyb_sinks"][j][None, :])
    c = _conv_fwd(f"{tag}_conv", proj, CB_CONV, 12, W["hyb_conv_w"][j], None)
    o_b, s_saved, t_saved = _delta_fwd(f"{tag}_delta", c, proj, W["hyb_a_log"][j].reshape(B_HEADS, 1, 1),
                                       W["hyb_dt_bias"][j].reshape(B_HEADS, 1, 1), W["hyb_norm_w"][j][None, :])
    o_ab = jnp.concatenate([o_a, o_b], axis=1)
    mix = _mm(f"{tag}_out", o_ab, W["hyb_w_out"][j], "nn", b_kind="lead", b_lead=0)
    return mix, (proj, c, s_saved, t_saved, o_ab)


def _hybrid_bwd(tag, x, dmix, addend, W, j, cos, sin, saved, G):
    proj, c, s_saved, t_saved, o_ab = saved
    T = x.shape[0]
    d_oab = _mm(f"{tag}_dout", dmix, W["hyb_w_out"][j], "nt", b_kind="lead", b_lead=0)
    G["hyb_w_out"][j] = _mm(f"{tag}_dwout", o_ab, dmix, "tn", out_dtypes=(bf16,)).reshape(N_DEV, -1, D_MODEL)
    dq, dk, dv, dsinks = _attn_bwd(f"{tag}_dattn", proj, cos, sin, W["hyb_sinks"][j][None, :], d_oab)
    a_log = W["hyb_a_log"][j].reshape(B_HEADS, 1, 1)
    dt_bias = W["hyb_dt_bias"][j].reshape(B_HEADS, 1, 1)
    dc, dz, dlg, dal, ddt, dnw = _delta_bwd(f"{tag}_ddelta", c, proj, a_log, dt_bias, W["hyb_norm_w"][j][None, :],
                                            s_saved, t_saved, d_oab)
    dconv_in, dconv_w, _ = _conv_bwd(f"{tag}_dconv", dc, proj, CB_CONV, 12, W["hyb_conv_w"][j])
    dproj = jnp.concatenate([dq, dk, dv, dconv_in, dz, dlg,
                             jnp.zeros((T, HYB_PROJ_PAD - (CB_LG + 1) * LANE), f32)], axis=1)
    dx = _mm(f"{tag}_dx", dproj, W["hyb_w_in"][j], "nt", b_kind="lead", b_lead=0, epilogue=_add_epilogue,
             extras=(addend,))
    G["hyb_w_in"][j] = _split_cols(f"{tag}_dwin_split", _mm(f"{tag}_dwin", x, dproj, "tn"))
    G["hyb_sinks"][j] = dsinks[0]
    G["hyb_conv_w"][j] = dconv_w
    G["hyb_a_log"][j] = dal.reshape(B_HEADS)
    G["hyb_dt_bias"][j] = ddt.reshape(B_HEADS)
    G["hyb_norm_w"][j] = dnw[0]
    return dx


def _rec_fwd(tag, x, W, j):
    Wd = D_MODEL
    proj = _mm(f"{tag}_proj", x, W["rec_w_in"][j], "nn", b_kind="devcol", b_lead=0)
    xc = _conv_fwd(f"{tag}_conv", proj, 0, Wd // LANE, W["rec_conv_w"][j], W["rec_conv_b"][j][None, :])
    pre_r, pre_i = _blockdiag_fwd(f"{tag}_gates", xc, W["rec_w_a"][j][0], W["rec_w_x"][j][0])
    pars = [W["rec_b_a"][j][None, :], W["rec_b_x"][j][None, :], W["rec_lambda"][j][None, :]]
    a, b = _tl_fwd(f"{tag}_pre", _rglru_pre_fn, [(pre_r, 0, Wd), (pre_i, 0, Wd), (xc, 0, Wd)], pars, [Wd, Wd], [f32, f32])
    h = _scan(f"{tag}_scan", a, b, False)
    (hg,) = _tl_fwd(f"{tag}_gate", _rec_gate_fn, [(h, 0, Wd), (proj, Wd // LANE, Wd)], [], [Wd], [f32])
    mix = _mm(f"{tag}_out", hg, W["rec_w_out"][j], "nn", b_kind="lead", b_lead=0)
    return mix, (proj, xc, pre_r, pre_i, a, h, hg)


def _rec_bwd(tag, x, dmix, addend, W, j, saved, G):
    proj, xc, pre_r, pre_i, a, h, hg = saved
    Wd = D_MODEL
    dhg = _mm(f"{tag}_dout", dmix, W["rec_w_out"][j], "nt", b_kind="lead", b_lead=0)
    G["rec_w_out"][j] = _mm(f"{tag}_dwout", hg, dmix, "tn", out_dtypes=(bf16,)).reshape(N_DEV, -1, D_MODEL)
    (dh, dgate), _ = _tl_bwd(f"{tag}_dgate", _rec_gate_fn, [(h, 0, Wd), (proj, Wd // LANE, Wd)], [], [(dhg, 0, Wd)])
    a_next = jnp.concatenate([a[1:], jnp.zeros((1, Wd), f32)], axis=0)
    h_prev = jnp.concatenate([jnp.zeros((1, Wd), f32), h[:-1]], axis=0)
    lam_t = _scan(f"{tag}_dscan", a_next, dh, True)
    pars = [W["rec_b_a"][j][None, :], W["rec_b_x"][j][None, :], W["rec_lambda"][j][None, :]]
    (dpr, dpi, dxc1), (db_a, db_x, dlam) = _tl_bwd(
        f"{tag}_dpre", _rglru_pre_fn, [(pre_r, 0, Wd), (pre_i, 0, Wd), (xc, 0, Wd)], pars,
        [(lam_t, 0, Wd), (h_prev, 0, Wd)], cot_fn=lambda lt, hp: (lt * hp, lt))
    dxc = _blockdiag_bwd_dx(f"{tag}_dgates_dx", dpr, dpi, W["rec_w_a"][j][0], W["rec_w_x"][j][0], dxc1)
    dwa, dwx = _blockdiag_bwd_dw(f"{tag}_dgates_dw", xc, dpr, dpi)
    G["rec_w_a"][j], G["rec_w_x"][j] = _rows_to_dev(dwa), _rows_to_dev(dwx)
    dxr, dconv_w, dconv_b = _conv_bwd(f"{tag}_dconv", dxc, proj, 0, Wd // LANE, W["rec_conv_w"][j])
    dproj = jnp.concatenate([dxr, dgate], axis=1)
    dx = _mm(f"{tag}_dx", dproj, W["rec_w_in"][j], "nt", b_kind="devcol", b_lead=0, epilogue=_add_epilogue,
             extras=(addend,))
    G["rec_w_in"][j] = _mm(f"{tag}_dwin", x, dproj, "tn", o_kind="devcol", out_dtypes=(bf16,))
    G["rec_conv_w"][j] = dconv_w
    G["rec_conv_b"][j] = dconv_b
    G["rec_b_a"][j] = db_a[0]
    G["rec_b_x"][j] = db_x[0]
    G["rec_lambda"][j] = dlam[0]
    return dx


def _local_step(x, target, W, load_layer, grads_ready):
    T = x.shape[0]
    cos, sin = _rope_tables(T)
    saved = []
    for layer in range(DEPTH):
        j = layer // 2
        tag = f"L{layer}"
        load_layer(layer, x)
        if layer % 2 == 0:
            mix, sv = _hybrid_fwd(tag, x, W, j, cos, sin)
        else:
            mix, sv = _rec_fwd(tag, x, W, j)
        ln1 = [W["ln1_g"][layer][None, :], W["ln1_b"][layer][None, :]]
        (x1,) = _tl_fwd(f"{tag}_ln1", _ln_res_fn, [(x, 0, D_MODEL), (mix, 0, D_MODEL)], ln1, [D_MODEL], [f32])
        a, h2 = _mm(f"{tag}_mlp1", x1, W["mlp_w1"][layer], "nn", b_kind="devcol", b_lead=0, epilogue=_relu2_epilogue,
                    out_dtypes=(f32, bf16))
        y = _mm(f"{tag}_mlp2", h2, W["mlp_w2"][layer], "nn", b_kind="devrow", b_lead=0)
        ln2 = [W["ln2_g"][layer][None, :], W["ln2_b"][layer][None, :]]
        (x2,) = _tl_fwd(f"{tag}_ln2", _ln_res_fn, [(x1, 0, D_MODEL), (y, 0, D_MODEL)], ln2, [D_MODEL], [f32])
        saved.append((x, sv, mix, x1, a, h2, y))
        x = x2
    loss, dx = _loss_head(x, target)

    G = {k: [None] * (DEPTH if k.startswith(("ln", "mlp")) else DEPTH // 2) for k in (
        "hyb_w_in", "hyb_sinks", "hyb_conv_w", "hyb_a_log", "hyb_dt_bias", "hyb_norm_w", "hyb_w_out",
        "rec_w_in", "rec_conv_w", "rec_conv_b", "rec_w_a", "rec_b_a", "rec_w_x", "rec_b_x", "rec_lambda", "rec_w_out",
        "ln1_g", "ln1_b", "mlp_w1", "mlp_w2", "ln2_g", "ln2_b")}
    order = jnp.zeros((1, 1), f32)
    for layer in reversed(range(DEPTH)):
        j = layer // 2
        tag = f"L{layer}"
        x0, sv, mix, x1, a, h2, y = saved[layer]
        ln2 = [W["ln2_g"][layer][None, :] + order, W["ln2_b"][layer][None, :]]
        (dx1_a, dy), (dg2, db2) = _tl_bwd(f"{tag}_dln2", _ln_res_fn, [(x1, 0, D_MODEL), (y, 0, D_MODEL)], ln2,
                                          [(dx, 0, D_MODEL)])
        G["ln2_g"][layer], G["ln2_b"][layer] = dg2[0], db2[0]
        da = _mm(f"{tag}_dmlp2", dy, W["mlp_w2"][layer], "nt", b_kind="devrow", b_lead=0, epilogue=_drelu2_epilogue,
                 extras=(a,), out_dtypes=(bf16,))
        G["mlp_w2"][layer] = _mm(f"{tag}_dw2", h2, dy, "tn", out_dtypes=(bf16,)).reshape(N_DEV, -1, D_MODEL)
        dx1 = _mm(f"{tag}_dmlp1", da, W["mlp_w1"][layer], "nt", b_kind="devcol", b_lead=0, epilogue=_add_epilogue,
                  extras=(dx1_a,))
        G["mlp_w1"][layer] = _mm(f"{tag}_dw1", x1, da, "tn", o_kind="devcol", out_dtypes=(bf16,))
        order = grads_ready(f"l{layer}_mlp", {(k, layer): G[k][layer] for k in ("mlp_w1", "mlp_w2")})
        ln1 = [W["ln1_g"][layer][None, :] + order, W["ln1_b"][layer][None, :]]
        (dx0_a, dmix), (dg1, db1) = _tl_bwd(f"{tag}_dln1", _ln_res_fn, [(x0, 0, D_MODEL), (mix, 0, D_MODEL)], ln1,
                                            [(dx1, 0, D_MODEL)])
        G["ln1_g"][layer], G["ln1_b"][layer] = dg1[0], db1[0]
        if layer % 2 == 0:
            dx = _hybrid_bwd(tag, x0, dmix, dx0_a, W, j, cos, sin, sv, G)
        else:
            dx = _rec_bwd(tag, x0, dmix, dx0_a, W, j, sv, G)
        order = grads_ready(f"l{layer}_mixer", {(k, i): G[k][i] for k, i in _layer_weights(layer)[:-2]})
    big = {k for k, _ in BIG}
    return loss, dx, {k: jnp.stack(v) for k, v in G.items() if k not in big}


def _layer_weights(layer):
    j = layer // 2
    mixer = ["hyb_w_in", "hyb_w_out"] if layer % 2 == 0 else ["rec_w_in", "rec_w_out", "rec_w_a", "rec_w_x"]
    return [(k, j) for k in mixer] + [("mlp_w1", layer), ("mlp_w2", layer)]


def _my_coords():
    return lax.axis_index("x"), lax.axis_index("y"), lax.axis_index("c")


def _all_gather(name, arrays):
    na = len(arrays)

    def body(*refs):
        x_refs, out_refs = refs[:na], refs[na:2 * na]
        send_sems, recv_sems, local_sems = refs[2 * na:]
        x, y, c = _my_coords()
        me, sibling = (x, y, c), (x, y, 1 - c)
        chips = [(1 - x, y), (x, 1 - y), (1 - x, 1 - y)]

        def blk(a, px, py, pc):
            return out_refs[a].at[4 * px + 2 * py + pc]

        def copy(a, k, block, to, src=None):
            return pltpu.make_async_remote_copy(
                src_ref=blk(a, *block) if src is None else src, dst_ref=blk(a, *block),
                send_sem=send_sems.at[a, k], recv_sem=recv_sems.at[a, k],
                device_id=to, device_id_type=pl.DeviceIdType.MESH)

        mine = [pltpu.make_async_copy(x_refs[a], blk(a, *me), local_sems.at[a]) for a in range(na)]
        for cp in mine:
            cp.start()
        first = []
        for a in range(na):
            first.append(copy(a, 0, me, sibling, src=x_refs[a]))
            first += [copy(a, 1 + j, me, (*chip, c), src=x_refs[a]) for j, chip in enumerate(chips)]
        for cp in first:
            cp.start()
        passed = []
        for a in range(na):
            for j, chip in enumerate(chips):
                copy(a, 1 + j, (*chip, c), me).wait_recv()
                passed.append(copy(a, 4 + j, (*chip, c), sibling))
                passed[-1].start()
        for a in range(na):
            copy(a, 0, sibling, me).wait_recv()
            for j, chip in enumerate(chips):
                copy(a, 4 + j, (*chip, 1 - c), me).wait_recv()
        for cp in first + passed:
            cp.wait_send()
        for cp in mine:
            cp.wait()

    return pl.pallas_call(
        body, name=name,
        out_shape=[jax.ShapeDtypeStruct((N_DEV,) + a.shape, a.dtype) for a in arrays],
        in_specs=[pl.BlockSpec(memory_space=pl.ANY)] * na,
        out_specs=[pl.BlockSpec(memory_space=pl.ANY)] * na,
        scratch_shapes=[pltpu.SemaphoreType.DMA((na, 7)), pltpu.SemaphoreType.DMA((na, 7)),
                        pltpu.SemaphoreType.DMA((na,))],
    )(*arrays)


_HBM = pl.BlockSpec(memory_space=pltpu.HBM)
_SEM = pl.BlockSpec(memory_space=pltpu.SEMAPHORE)


def _push_copies(kind, x_refs, land_refs, send_sems, recv_sems, local_sems):
    x, y, c = _my_coords()
    me = 4 * x + 2 * y + c
    remote, local = [], []
    for a in range(len(x_refs)):
        local.append(pltpu.make_async_copy(x_refs[a] if kind == "gather" else x_refs[a].at[me], land_refs[a].at[me],
                                           local_sems.at[a]))
        for k in range(1, N_DEV):
            px = (1 - x) if (k >> 2) & 1 else x
            py = (1 - y) if (k >> 1) & 1 else y
            pc = (1 - c) if k & 1 else c
            remote.append(pltpu.make_async_remote_copy(
                src_ref=x_refs[a] if kind == "gather" else x_refs[a].at[4 * px + 2 * py + pc],
                dst_ref=land_refs[a].at[me],
                send_sem=send_sems.at[a * (N_DEV - 1) + k - 1], recv_sem=recv_sems.at[a * (N_DEV - 1) + k - 1],
                device_id=(px, py, pc), device_id_type=pl.DeviceIdType.MESH))
    return remote, local


_SIDE_EFFECT = pltpu.CompilerParams(has_side_effects=pltpu.SideEffectType.DATAFLOW_SIDE_EFFECTING)


def _push_start(name, kind, srcs, lands):
    na = len(srcs)

    def body(*refs):
        remote, local = _push_copies(kind, refs[:na], refs[na:2 * na], *refs[2 * na:2 * na + 3])
        for cp in remote + local:
            cp.start()
        token = refs[-1]
        token[...] = jnp.zeros_like(token)

    arrays = list(srcs) + list(lands)
    n_remote = na * (N_DEV - 1)
    res = pl.pallas_call(
        body, name=name,
        out_shape=(pltpu.SemaphoreType.DMA((n_remote,)), pltpu.SemaphoreType.DMA((n_remote,)),
                   pltpu.SemaphoreType.DMA((na,)), *[pltpu.HBM(t.shape, t.dtype) for t in arrays],
                   jax.ShapeDtypeStruct((SUBLANE, LANE), f32)),
        in_specs=[_HBM] * (2 * na),
        out_specs=(_SEM, _SEM, _SEM, *[_HBM] * (2 * na), pl.BlockSpec(memory_space=pltpu.VMEM)),
        input_output_aliases={i: 3 + i for i in range(2 * na)},
        compiler_params=_SIDE_EFFECT,
    )(*[pltpu.with_memory_space_constraint(t, pltpu.HBM) for t in arrays])
    return list(res[:3]), res[3:3 + na], res[3 + na:3 + 2 * na], res[-1][:1, :1]


def _push_wait(name, kind, sems, srcs, lands, after):
    na = len(srcs)

    def body(*refs):
        remote, local = _push_copies(kind, refs[:na], refs[na:2 * na], *refs[2 * na:2 * na + 3])
        for cp in remote:
            cp.wait_send()
            cp.wait_recv()
        for cp in local:
            cp.wait()

    arrays = list(srcs) + list(lands)
    res = pl.pallas_call(
        body, name=name,
        out_shape=tuple(pltpu.HBM(t.shape, t.dtype) for t in arrays),
        in_specs=[_HBM] * (2 * na) + [_SEM] * 3 + [pl.BlockSpec(memory_space=pl.ANY)],
        out_specs=tuple([_HBM] * (2 * na)),
        input_output_aliases={i: i for i in range(2 * na)},
        compiler_params=_SIDE_EFFECT,
    )(*arrays, *sems, after)
    return res[na:]


def _sum_blocks(name, land):
    _, R, n = land.shape
    tr = R

    def body(l_ref, o_ref):
        acc = l_ref[0].astype(f32)
        for s in range(1, N_DEV):
            acc = acc + l_ref[s].astype(f32)
        o_ref[...] = acc

    return pl.pallas_call(
        body, name=name, grid=(R // tr,),
        in_specs=[pl.BlockSpec((N_DEV, tr, n), lambda i: (0, i, 0))],
        out_specs=pl.BlockSpec((tr, n), lambda i: (i, 0)),
        out_shape=jax.ShapeDtypeStruct((R, n), f32),
        compiler_params=_cparams("parallel"),
    )(land)


def _adamw(name, w, g, m, v):
    shape = w.shape
    last = shape[-1]
    rows = math.prod(shape[:-1])
    tm = 256 if rows % 256 == 0 and rows > 256 else rows
    w2, g2, m2, v2 = (t.reshape(rows, last) for t in (w, g, m, v))

    def body(w_ref, g_ref, m_ref, v_ref, d_ref, mo_ref, vo_ref):
        gg = g_ref[...]
        mn = ADAM_B1 * m_ref[...] + (1.0 - ADAM_B1) * gg
        vn = ADAM_B2 * v_ref[...] + (1.0 - ADAM_B2) * jnp.square(gg)
        m_hat = mn / (1.0 - ADAM_B1 ** ADAM_STEP)
        v_hat = vn / (1.0 - ADAM_B2 ** ADAM_STEP)
        d_ref[...] = -ADAM_LR * (m_hat / (jnp.sqrt(v_hat) + ADAM_EPS) + ADAM_WD * w_ref[...])
        mo_ref[...] = mn
        vo_ref[...] = vn

    spec = pl.BlockSpec((tm, last), lambda i: (i, 0))
    d, mn, vn = pl.pallas_call(
        body, name=name, grid=(rows // tm,), in_specs=[spec] * 4, out_specs=[spec] * 3,
        out_shape=[jax.ShapeDtypeStruct((rows, last), f32)] * 3,
        compiler_params=_cparams("parallel"),
    )(w2, g2, m2, v2)
    return d.reshape(shape), mn.reshape(shape), vn.reshape(shape)


def _adamw_land(name, lands, w, m, v, tm=256):
    L = len(lands)
    _, R, C = lands[0].shape
    tm = min(tm, R)

    def body(*refs):
        l_refs, (w_ref, m_ref, v_ref, g_ref, d_ref, mo_ref, vo_ref) = refs[:L], refs[L:]
        for k in range(L):
            @pl.when(pl.program_id(0) == k)
            def _(k=k):
                gg = l_refs[k][0].astype(f32)
                for s in range(1, N_DEV):
                    gg = gg + l_refs[k][s].astype(f32)
                g_ref[...] = gg
                mn = ADAM_B1 * m_ref[...] + (1.0 - ADAM_B1) * gg
                vn = ADAM_B2 * v_ref[...] + (1.0 - ADAM_B2) * jnp.square(gg)
                m_hat = mn / (1.0 - ADAM_B1 ** ADAM_STEP)
                v_hat = vn / (1.0 - ADAM_B2 ** ADAM_STEP)
                d_ref[...] = -ADAM_LR * (m_hat / (jnp.sqrt(v_hat) + ADAM_EPS) + ADAM_WD * w_ref[...])
                mo_ref[...] = mn
                vo_ref[...] = vn

    land_specs = [pl.BlockSpec((N_DEV, tm, C), lambda l, i, k=k: (0, jnp.where(l == k, i, 0), 0)) for k in range(L)]
    spec = pl.BlockSpec((None, tm, C), lambda l, i: (l, i, 0))
    return pl.pallas_call(
        body, name=name, grid=(L, R // tm),
        in_specs=land_specs + [spec] * 3,
        out_specs=[spec] * 4,
        out_shape=[jax.ShapeDtypeStruct((L, R, C), f32)] * 4,
        compiler_params=_cparams("arbitrary", "arbitrary"),
    )(*lands, w, m, v)


BIG = [("hyb_w_in", 2), ("hyb_w_out", 1), ("rec_w_in", 2), ("rec_w_out", 1), ("rec_w_a", 2), ("rec_w_x", 2),
       ("mlp_w1", 2), ("mlp_w2", 1)]
SMALL = [("hyb_conv_w", 2), ("rec_conv_w", 2), ("rec_conv_b", 1), ("rec_b_a", 1), ("rec_b_x", 1), ("rec_lambda", 1)]
REPL = ["hyb_sinks", "hyb_a_log", "hyb_dt_bias", "hyb_norm_w", "ln1_g", "ln1_b", "ln2_g", "ln2_b"]
WEIGHTS = ["hyb_w_in", "hyb_sinks", "hyb_conv_w", "hyb_a_log", "hyb_dt_bias", "hyb_norm_w", "hyb_w_out", "rec_w_in",
           "rec_conv_w", "rec_conv_b", "rec_w_a", "rec_b_a", "rec_w_x", "rec_b_x", "rec_lambda", "rec_w_out",
           "ln1_g", "ln1_b", "mlp_w1", "mlp_w2", "ln2_g", "ln2_b"]


def _pack_rows(parts, dtype, row_mult):
    lead = parts[0].shape[:-1]
    flat = jnp.concatenate([p.astype(dtype) for p in parts], axis=-1)
    n = flat.shape[-1]
    unit = row_mult * LANE
    pad = (-n) % unit
    if pad:
        flat = jnp.concatenate([flat, jnp.zeros(lead + (pad,), dtype)], axis=-1)
    return flat.reshape(lead + ((n + pad) // LANE, LANE))


def _gather_full(gathered, shard_shapes, table):
    flat = gathered.reshape(N_DEV, -1)
    out, off = {}, 0
    for name, ax in table:
        shp = shard_shapes[name]
        n = math.prod(shp)
        arr = flat[:, off:off + n].reshape((N_DEV,) + shp)
        off += n
        arr = jnp.moveaxis(arr, 0, ax)
        out[name] = arr.reshape(shp[:ax] + (N_DEV * shp[ax],) + shp[ax + 1:])
    return out


def _matmul_layouts(tag, gw):
    out = {}
    bw = D_MODEL // LRU_BLOCKS
    for k, g in gw.items():
        L = g.shape[1]
        if k == "hyb_w_in":
            out[k] = _merge_cols(f"{tag}_w_in_merge", g)
        elif k in ("hyb_w_out", "rec_w_out"):
            out[k] = jnp.swapaxes(g, 0, 1).reshape(L, D_MODEL, D_MODEL)
        elif k in ("rec_w_a", "rec_w_x"):
            out[k] = jnp.moveaxis(g, 0, 2).reshape(L, LRU_BLOCKS, bw, bw)
        else:
            out[k] = g
    return out


def kernel(x, hyb_w_in, hyb_sinks, hyb_conv_w, hyb_a_log, hyb_dt_bias, hyb_norm_w, hyb_w_out, rec_w_in, rec_conv_w, rec_conv_b, rec_w_a, rec_b_a, rec_w_x, rec_b_x, rec_lambda, rec_w_out, ln1_g, ln1_b, mlp_w1, mlp_w2, ln2_g, ln2_b, loss_target, m_hyb_w_in, m_hyb_sinks, m_hyb_conv_w, m_hyb_a_log, m_hyb_dt_bias, m_hyb_norm_w, m_hyb_w_out, m_rec_w_in, m_rec_conv_w, m_rec_conv_b, m_rec_w_a, m_rec_b_a, m_rec_w_x, m_rec_b_x, m_rec_lambda, m_rec_w_out, m_ln1_g, m_ln1_b, m_mlp_w1, m_mlp_w2, m_ln2_g, m_ln2_b, v_hyb_w_in, v_hyb_sinks, v_hyb_conv_w, v_hyb_a_log, v_hyb_dt_bias, v_hyb_norm_w, v_hyb_w_out, v_rec_w_in, v_rec_conv_w, v_rec_conv_b, v_rec_w_a, v_rec_b_a, v_rec_w_x, v_rec_b_x, v_rec_lambda, v_rec_w_out, v_ln1_g, v_ln1_b, v_mlp_w1, v_mlp_w2, v_ln2_g, v_ln2_b):
    args = locals()
    w = {k: args[k] for k in WEIGHTS}
    m = {k: args["m_" + k] for k in WEIGHTS}
    v = {k: args["v_" + k] for k in WEIGHTS}
    shard_shapes = {k: tuple(t.shape) for k, t in w.items()}
    xi, yi, ci = _my_coords()
    me = 4 * xi + 2 * yi + ci

    def shards(layer):
        return [w[k][i:i + 1].astype(bf16) for k, i in _layer_weights(layer)]

    gathered0 = _all_gather("gather_l0", shards(0) + [_pack_rows([w[k].reshape(-1) for k, _ in SMALL], f32, SUBLANE)])
    W = _gather_full(gathered0[-1], shard_shapes, SMALL)
    W.update({k: w[k] for k in REPL})
    W.update({k: {} for k, _ in BIG})
    in_flight = {}
    order = jnp.zeros((1, 1), f32)
    for layer in range(1, DEPTH):
        srcs = shards(layer)
        *in_flight[layer], zero = _push_start(f"gather_l{layer}_start", "gather", srcs,
                                              [lax.empty((N_DEV,) + s.shape, bf16) for s in srcs])
        order = order + zero
    W["ln1_g"] = W["ln1_g"] + order

    def load_layer(layer, after):
        names = _layer_weights(layer)
        got = gathered0[:-1] if layer == 0 else _push_wait(f"gather_l{layer}_wait", "gather", *in_flight[layer], after)
        for (k, i), arr in zip(names, _matmul_layouts(f"L{layer}", {k: g for (k, _), g in zip(names, got)}).values()):
            W[k][i] = arr

    grads_in_flight = {}

    def grads_ready(tag, g):
        srcs = list(g.values())
        *pending, zero = _push_start(f"scatter_{tag}_start", "scatter", srcs, [lax.empty(s.shape, bf16) for s in srcs])
        grads_in_flight[tag] = (list(g.keys()), pending)
        return zero

    loss_local, grad_x, G = _local_step(x[0], loss_target[0], W, load_layer, grads_ready)
    loss = lax.psum(loss_local, MESH_AXES)

    landed = {}
    for tag, (keys, pending) in grads_in_flight.items():
        landed.update(zip(keys, _push_wait(f"scatter_{tag}_wait", "scatter", *pending, grad_x)))
    rest = _pack_rows([G[k].reshape(-1) for k, _ in SMALL] + [G[k].reshape(-1) for k in REPL], f32, SUBLANE)
    g_rest = _sum_blocks("sum_rest", _all_gather("gather_rest", [rest])[0]).reshape(-1)

    grads, delta, new_m, new_v = {}, {}, {}, {}
    for k, _ in BIG:
        shp = shard_shapes[k]
        s3 = (shp[0], math.prod(shp[1:-1]), shp[-1])
        lands = [landed[(k, i)].reshape((N_DEV,) + s3[1:]) for i in range(shp[0])]
        res = _adamw_land("adamw_" + k, lands, w[k].reshape(s3), m[k].reshape(s3), v[k].reshape(s3))
        grads[k], delta[k], new_m[k], new_v[k] = (r.reshape(shp) for r in res)
    off = 0
    for k, ax in SMALL:
        full_shape = G[k].shape
        n = math.prod(full_shape)
        full = g_rest[off:off + n].reshape(full_shape)
        off += n
        s = shard_shapes[k][ax]
        grads[k] = lax.dynamic_slice_in_dim(full, me * s, s, axis=ax)
    for k in REPL:
        n = math.prod(shard_shapes[k])
        grads[k] = g_rest[off:off + n].reshape(shard_shapes[k])
        off += n

    for k in [k for k, _ in SMALL] + REPL:
        delta[k], new_m[k], new_v[k] = _adamw("adamw_" + k, w[k], grads[k], m[k], v[k])

    return (loss, grad_x[None], *[grads[k] for k in WEIGHTS], *[delta[k] for k in WEIGHTS],
            *[new_m[k] for k in WEIGHTS], *[new_v[k] for k in WEIGHTS])
```

```python
import functools
import math

import jax
import jax.numpy as jnp
from jax import lax
from jax.experimental import pallas as pl
from jax.experimental.pallas import tpu as pltpu

f32 = jnp.float32
bf16 = jnp.bfloat16

N_DEV = 8
D_MODEL = 1024
DEPTH = 4
A_HEAD_DIM = 64
A_Q_HEADS = 8
WINDOW = 128
ROPE_THETA = 10000.0
B_HEADS = 4
B_HEAD_DIM = 128
B_CHUNK = 64
LRU_BLOCKS = 4
LRU_C = 8.0
D_FF = 4 * D_MODEL
HYB_PROJ = 2824
HYB_PROJ_PAD = 3072
DN_ALPHA = (2 * DEPTH) ** 0.25
LN_EPS = 1e-5
NORM_EPS = 1e-6
ADAM_LR = 0.001
ADAM_B1 = 0.9
ADAM_B2 = 0.999
ADAM_EPS = 1e-08
ADAM_WD = 0.01
ADAM_STEP = 10

LANE = 128
SUBLANE = 8
VMEM_LIMIT = 48 * 1024 * 1024

CB_QA, CB_KA, CB_VA, CB_CONV, CB_Z, CB_LG = 0, 4, 5, 6, 18, 22

MESH_AXES = ("x", "y", "c")


def _cparams(*sem):
    return pltpu.CompilerParams(dimension_semantics=sem, vmem_limit_bytes=VMEM_LIMIT)


def _dot(a, b, dims, precision=None):
    return lax.dot_general(a, b, (dims, ((), ())), preferred_element_type=f32, precision=precision)


NN = ((1,), (0,))
NT = ((1,), (1,))
TN = ((0,), (0,))


def _mat_spec(arr, kind, lead, br, bc, rb, cb):
    if kind == "plain":
        return pl.BlockSpec((br, bc), lambda i, j, k: (rb(i, j, k), cb(i, j, k)))
    if kind == "lead":
        return pl.BlockSpec((None, br, bc), lambda i, j, k: (lead, rb(i, j, k), cb(i, j, k)))
    if kind == "devcol":
        assert bc == arr.shape[-1]
        return pl.BlockSpec((None, None, br, bc), lambda i, j, k: (cb(i, j, k), lead, rb(i, j, k), 0))
    assert kind == "devrow" and br == arr.shape[-2]
    return pl.BlockSpec((None, None, br, bc), lambda i, j, k: (rb(i, j, k), lead, 0, cb(i, j, k)))


def _mm(name, a, b, mode, *, b_kind="plain", b_lead=0, o_kind="plain", epilogue=None, extras=(), params=(),
        out_dtypes=(f32,), tm=1024, tn=1024, tk=None):
    if tk is None:
        tk = 512 if mode == "tn" else 1024
    if b_kind in ("plain", "lead"):
        b_rows, b_cols = b.shape[-2:]
    elif b_kind == "devcol":
        b_rows, b_cols = b.shape[-2], N_DEV * b.shape[-1]
    else:
        b_rows, b_cols = N_DEV * b.shape[-2], b.shape[-1]
    if mode == "nn":
        (M, K), (K2, N) = a.shape, (b_rows, b_cols)
    elif mode == "nt":
        (M, K), (N, K2) = a.shape, (b_rows, b_cols)
    else:
        (K, M), (K2, N) = a.shape, (b_rows, b_cols)
    assert K == K2, (name, a.shape, b.shape, mode)
    tm, tn, tk = min(tm, M), min(tn, N), min(tk, K)
    cols_are_n = mode != "nt"
    if b_kind == "devcol":
        tn, tk = (b.shape[-1], tk) if cols_are_n else (tn, b.shape[-1])
    if b_kind == "devrow":
        tn, tk = (tn, b.shape[-2]) if cols_are_n else (b.shape[-2], tk)
    shard = N // N_DEV
    if o_kind == "devcol":
        tn = max(shard, tn // shard * shard)
    assert M % tm == 0 and N % tn == 0 and K % tk == 0, (name, M, N, K, tm, tn, tk)
    nk = K // tk
    dims = {"nn": NN, "nt": NT, "tn": TN}[mode]
    n_ex, n_out = len(extras) + len(params), len(out_dtypes)

    def body(*refs):
        a_ref, b_ref = refs[:2]
        ex = refs[2:2 + n_ex]
        outs = refs[2 + n_ex:2 + n_ex + n_out]
        acc = refs[-1]
        k = pl.program_id(2)

        @pl.when(k == 0)
        def _():
            acc[...] = jnp.zeros_like(acc)

        acc[...] += _dot(a_ref[...].astype(bf16), b_ref[...].astype(bf16), dims)

        @pl.when(k == nk - 1)
        def _():
            r = acc[...]
            res = epilogue(r, *[e[...] for e in ex]) if epilogue is not None else (r,)
            for o, v in zip(outs, res):
                if o_kind == "plain":
                    o[...] = v.astype(o.dtype)
                else:
                    for q in range(tn // shard):
                        o[q] = v[:, q * shard:(q + 1) * shard].astype(o.dtype)

    if mode == "tn":
        a_spec = pl.BlockSpec((tk, tm), lambda i, j, k: (k, i))
    else:
        a_spec = pl.BlockSpec((tm, tk), lambda i, j, k: (i, k))
    jb, kb = (lambda i, j, k: j), (lambda i, j, k: k)
    if mode == "nt":
        b_spec = _mat_spec(b, b_kind, b_lead, tn, tk, jb, kb)
    else:
        b_spec = _mat_spec(b, b_kind, b_lead, tk, tn, kb, jb)
    e_spec = pl.BlockSpec((tm, tn), lambda i, j, k: (i, j))
    if o_kind == "plain":
        o_spec, o_shape = e_spec, (M, N)
    else:
        o_spec, o_shape = pl.BlockSpec((tn // shard, tm, shard), lambda i, j, k: (j, i, 0)), (N_DEV, M, shard)
    res = pl.pallas_call(
        body, name=name,
        grid=(M // tm, N // tn, nk),
        in_specs=[a_spec, b_spec] + [e_spec] * len(extras)
        + [pl.BlockSpec(p.shape, lambda i, j, k: (0, 0)) for p in params],
        out_specs=[o_spec] * n_out,
        out_shape=[jax.ShapeDtypeStruct(o_shape, dt) for dt in out_dtypes],
        scratch_shapes=[pltpu.VMEM((tm, tn), f32)],
        compiler_params=_cparams("parallel", "parallel", "arbitrary"),
    )(a, b, *extras, *params)
    return res[0] if n_out == 1 else res


def _row_spec(tm, cb, width):
    assert (cb * LANE) % width == 0
    blk = (cb * LANE) // width
    return pl.BlockSpec((tm, width), lambda i: (i, blk))


def _whole_spec(p):
    nd = p.ndim
    return pl.BlockSpec(p.shape, lambda i: (0,) * nd)


def _tl_fwd(name, fn, rows, params, out_widths, out_dtypes, tm=256):
    T = rows[0][0].shape[0]
    tm = min(tm, T)
    nr, npar = len(rows), len(params)

    def body(*refs):
        vals = [r[...] for r in refs[:nr + npar]]
        outs = fn(*vals)
        for o, v in zip(refs[nr + npar:], outs):
            o[...] = v.astype(o.dtype)

    res = pl.pallas_call(
        body, name=name, grid=(T // tm,),
        in_specs=[_row_spec(tm, cb, w) for (_, cb, w) in rows] + [_whole_spec(p) for p in params],
        out_specs=[pl.BlockSpec((tm, w), lambda i: (i, 0)) for w in out_widths],
        out_shape=[jax.ShapeDtypeStruct((T, w), dt) for w, dt in zip(out_widths, out_dtypes)],
        compiler_params=_cparams("parallel"),
    )(*[r[0] for r in rows], *params)
    return res


def _tl_bwd(name, fn, rows, params, cot_rows, cot_fn=None, tm=256):
    T = rows[0][0].shape[0]
    tm = min(tm, T)
    nr, npar, nc = len(rows), len(params), len(cot_rows)

    def body(*refs):
        vals = [r[...] for r in refs[:nr + npar]]
        cots = [r[...] for r in refs[nr + npar:nr + npar + nc]]
        outs = refs[nr + npar + nc:]
        cot = tuple(cot_fn(*cots)) if cot_fn is not None else tuple(cots)
        _, vjp = jax.vjp(fn, *vals)
        grads = vjp(cot)
        for o, g in zip(outs[:nr], grads[:nr]):
            o[...] = g.astype(o.dtype)
        i = pl.program_id(0)
        for o, g in zip(outs[nr:], grads[nr:]):
            @pl.when(i == 0)
            def _(o=o):
                o[...] = jnp.zeros_like(o)
            o[...] += g

    res = pl.pallas_call(
        body, name=name, grid=(T // tm,),
        in_specs=[_row_spec(tm, cb, w) for (_, cb, w) in rows] + [_whole_spec(p) for p in params]
        + [_row_spec(tm, cb, w) for (_, cb, w) in cot_rows],
        out_specs=[pl.BlockSpec((tm, w), lambda i: (i, 0)) for (_, _, w) in rows] + [_whole_spec(p) for p in params],
        out_shape=[jax.ShapeDtypeStruct((T, w), f32) for (_, _, w) in rows]
        + [jax.ShapeDtypeStruct(p.shape, f32) for p in params],
        compiler_params=_cparams("arbitrary"),
    )(*[r[0] for r in rows], *params, *[r[0] for r in cot_rows])
    return res[:nr], res[nr:]


def _ln_res_fn(x, mix, g, b):
    pre = DN_ALPHA * x + mix
    mu = jnp.mean(pre, axis=-1, keepdims=True)
    var = jnp.mean(jnp.square(pre - mu), axis=-1, keepdims=True)
    return ((pre - mu) * lax.rsqrt(var + LN_EPS) * g + b,)


@jax.custom_jvp
def _expm1(x):
    small = jnp.abs(x) < 0.3
    xs = jnp.where(small, x, 0.0)
    poly = xs * (1.0 + xs * (1 / 2 + xs * (1 / 6 + xs * (1 / 24 + xs * (1 / 120 + xs * (
        1 / 720 + xs * (1 / 5040 + xs * (1 / 40320 + xs * (1 / 362880)))))))))
    return jnp.where(small, poly, jnp.exp(x) - 1.0)


@_expm1.defjvp
def _expm1_jvp(primals, tangents):
    (x,), (t,) = primals, tangents
    return _expm1(x), t * jnp.exp(x)


def _rglru_pre_fn(pre_r, pre_i, xc, b_a, b_x, lam):
    r = jax.nn.sigmoid(pre_r + b_a)
    i = jax.nn.sigmoid(pre_i + b_x)
    log_a = -LRU_C * r * jax.nn.softplus(-lam)
    a = jnp.exp(log_a)
    b = jnp.sqrt(-_expm1(2.0 * log_a)) * (i * xc)
    return a, b


def _rec_gate_fn(h, gate):
    return (h * jax.nn.gelu(gate),)


def _loss_head(y, t, tm=256):
    T, Dm = y.shape

    def body(y_ref, t_ref, dy_ref, loss_ref):
        e = y_ref[...] - t_ref[...]
        dy_ref[...] = e * (1.0 / Dm)

        @pl.when(pl.program_id(0) == 0)
        def _():
            loss_ref[...] = jnp.zeros_like(loss_ref)

        loss_ref[...] += 0.5 * jnp.sum(jnp.mean(e * e, axis=-1, keepdims=True), axis=0, keepdims=True)

    dy, loss = pl.pallas_call(
        body, name="loss_head", grid=(T // tm,),
        in_specs=[pl.BlockSpec((tm, Dm), lambda i: (i, 0))] * 2,
        out_specs=[pl.BlockSpec((tm, Dm), lambda i: (i, 0)), pl.BlockSpec((SUBLANE, LANE), lambda i: (0, 0))],
        out_shape=[jax.ShapeDtypeStruct((T, Dm), f32), jax.ShapeDtypeStruct((SUBLANE, LANE), f32)],
        compiler_params=_cparams("arbitrary"),
    )(y, t)
    return loss[0, 0], dy


def _conv_fwd(name, x, cb0, nblk, w, bias, tm=2048):
    T = x.shape[0]
    tm = min(tm, T)
    hb = tm // SUBLANE
    has_b = bias is not None

    def body(*refs):
        cur, prev, w_ref = refs[:3]
        b_ref = refs[3] if has_b else None
        o = refs[-1]
        i = pl.program_id(1)
        p = jnp.where(i > 0, prev[...], 0.0)
        xcat = jnp.concatenate([p, cur[...]], axis=0)
        acc = cur[...] * w_ref[3:4, :]
        for j in range(3):
            acc = acc + pltpu.roll(xcat, 3 - j, axis=0)[SUBLANE:] * w_ref[j:j + 1, :]
        if has_b:
            acc = acc + b_ref[...]
        o[...] = acc

    in_specs = [
        pl.BlockSpec((tm, LANE), lambda c, i: (i, cb0 + c)),
        pl.BlockSpec((SUBLANE, LANE), lambda c, i: (jnp.maximum(i * hb - 1, 0), cb0 + c)),
        pl.BlockSpec((4, LANE), lambda c, i: (0, c)),
    ]
    args = [x, x, w]
    if has_b:
        in_specs.append(pl.BlockSpec((1, LANE), lambda c, i: (0, c)))
        args.append(bias)
    return pl.pallas_call(
        body, name=name, grid=(nblk, T // tm),
        in_specs=in_specs,
        out_specs=pl.BlockSpec((tm, LANE), lambda c, i: (i, c)),
        out_shape=jax.ShapeDtypeStruct((T, nblk * LANE), f32),
        compiler_params=_cparams("parallel", "parallel"),
    )(*args)


def _conv_bwd(name, dy, x, cb0, nblk, w, tm=2048):
    T = x.shape[0]
    tm = min(tm, T)
    hb = tm // SUBLANE
    nt = T // tm

    def body(dcur, dnext, xcur, xprev, w_ref, dx_ref, dw_ref, db_ref):
        i = pl.program_id(1)
        d = dcur[...]
        dn = jnp.where(i < nt - 1, dnext[...], 0.0)
        dcat = jnp.concatenate([d, dn], axis=0)
        acc = d * w_ref[3:4, :]
        for j in range(3):
            s = 3 - j
            acc = acc + pltpu.roll(dcat, tm + SUBLANE - s, axis=0)[:tm] * w_ref[j:j + 1, :]
        dx_ref[...] = acc

        p = jnp.where(i > 0, xprev[...], 0.0)
        xcat = jnp.concatenate([p, xcur[...]], axis=0)
        rows = [jnp.sum(d * pltpu.roll(xcat, 3 - j, axis=0)[SUBLANE:], axis=0, keepdims=True) for j in range(3)]
        rows.append(jnp.sum(d * xcur[...], axis=0, keepdims=True))
        rows.append(jnp.zeros((SUBLANE - 4, LANE), f32))

        @pl.when(i == 0)
        def _():
            dw_ref[...] = jnp.zeros_like(dw_ref)
            db_ref[...] = jnp.zeros_like(db_ref)

        dw_ref[...] += jnp.concatenate(rows, axis=0)
        db_ref[...] += jnp.broadcast_to(jnp.sum(d, axis=0, keepdims=True), (SUBLANE, LANE))

    nh = T // SUBLANE
    dx, dw, db = pl.pallas_call(
        body, name=name, grid=(nblk, nt),
        in_specs=[
            pl.BlockSpec((tm, LANE), lambda c, i: (i, c)),
            pl.BlockSpec((SUBLANE, LANE), lambda c, i: (jnp.minimum((i + 1) * hb, nh - 1), c)),
            pl.BlockSpec((tm, LANE), lambda c, i: (i, cb0 + c)),
            pl.BlockSpec((SUBLANE, LANE), lambda c, i: (jnp.maximum(i * hb - 1, 0), cb0 + c)),
            pl.BlockSpec((4, LANE), lambda c, i: (0, c)),
        ],
        out_specs=[
            pl.BlockSpec((tm, LANE), lambda c, i: (i, c)),
            pl.BlockSpec((SUBLANE, LANE), lambda c, i: (0, c)),
            pl.BlockSpec((SUBLANE, LANE), lambda c, i: (0, c)),
        ],
        out_shape=[jax.ShapeDtypeStruct((T, nblk * LANE), f32),
                   jax.ShapeDtypeStruct((SUBLANE, nblk * LANE), f32),
                   jax.ShapeDtypeStruct((SUBLANE, nblk * LANE), f32)],
        compiler_params=_cparams("parallel", "arbitrary"),
    )(dy, dy, x, x, w)
    return dx, dw[:4], db[0]


@functools.partial(jax.custom_vjp, nondiff_argnums=(1,))
def _lroll(x, s):
    return pltpu.roll(x, s, axis=1)


def _lroll_fwd(x, s):
    return _lroll(x, s), None


def _lroll_bwd(s, _, g):
    return (_lroll(g, (LANE - s) % LANE),)


_lroll.defvjp(_lroll_fwd, _lroll_bwd)


def _rope_tables(T):
    half = A_HEAD_DIM // 2
    inv_freq = ROPE_THETA ** (-jnp.arange(half, dtype=f32) / half)
    ang = jnp.arange(T, dtype=f32)[:, None] * inv_freq[None, :]
    cos, sin = jnp.cos(ang), jnp.sin(ang)
    return jnp.tile(jnp.concatenate([cos, cos], axis=1), (1, 2)), jnp.tile(jnp.concatenate([-sin, sin], axis=1), (1, 2))


def _attn_block_fn(n, q, kp, kc, vp, vc, cq, sq, cp, sp, sinks):
    W = WINDOW
    lane = lax.broadcasted_iota(jnp.int32, (W, LANE), 1)
    lo_half = (lane % A_HEAD_DIM) < (A_HEAD_DIM // 2)
    lane8 = lax.broadcasted_iota(jnp.int32, sinks.shape, 1)

    def rope(x, c, s):
        return x * c + jnp.where(lo_half, _lroll(x, LANE - A_HEAD_DIM // 2), _lroll(x, A_HEAD_DIM // 2)) * s

    k2 = jnp.concatenate([rope(kp, cp, sp), rope(kc, cq, sq)], axis=0).astype(bf16)
    v2 = jnp.concatenate([vp, vc], axis=0).astype(bf16)
    row = lax.broadcasted_iota(jnp.int32, (W, 2 * W), 0)
    col = lax.broadcasted_iota(jnp.int32, (W, 2 * W), 1)
    dist = row + W - col
    mask = (dist >= 0) & (dist < W) & ((col >= W) | (n > 0))
    outs = []
    for t in range(4):
        qt = rope(q[:, LANE * t:LANE * (t + 1)], cq, sq)
        g = t // 2
        ot = jnp.zeros((W, LANE), f32)
        for hh in range(2):
            qa = jnp.where((lane // A_HEAD_DIM) == hh, qt, 0.0)
            if hh != g:
                qa = _lroll(qa, A_HEAD_DIM)
            s = _dot(qa.astype(bf16), k2, NT) * (A_HEAD_DIM ** -0.5)
            s = jnp.where(mask, s, -jnp.inf)
            sink = jnp.sum(jnp.where(lane8 == 2 * t + hh, sinks, 0.0), axis=1, keepdims=True)
            m = jnp.maximum(jnp.max(s, axis=-1, keepdims=True), sink)
            e = jnp.exp(s - m)
            p = e / (jnp.sum(e, axis=-1, keepdims=True) + jnp.exp(sink - m))
            o = _dot(p.astype(bf16), v2, NN)
            o = jnp.where((lane // A_HEAD_DIM) == g, o, 0.0)
            if hh != g:
                o = _lroll(o, A_HEAD_DIM)
            ot = ot + o
        outs.append(ot)
    return jnp.concatenate(outs, axis=1)


def _attn_specs():
    W = WINDOW
    prev = lambda n: jnp.maximum(n - 1, 0)
    return [
        pl.BlockSpec((W, 4 * LANE), lambda n: (n, CB_QA // 4)),
        pl.BlockSpec((W, LANE), lambda n: (prev(n), CB_KA)),
        pl.BlockSpec((W, LANE), lambda n: (n, CB_KA)),
        pl.BlockSpec((W, LANE), lambda n: (prev(n), CB_VA)),
        pl.BlockSpec((W, LANE), lambda n: (n, CB_VA)),
        pl.BlockSpec((W, LANE), lambda n: (n, 0)),
        pl.BlockSpec((W, LANE), lambda n: (n, 0)),
        pl.BlockSpec((W, LANE), lambda n: (prev(n), 0)),
        pl.BlockSpec((W, LANE), lambda n: (prev(n), 0)),
        pl.BlockSpec((1, A_Q_HEADS), lambda n: (0, 0)),
    ]


def _attn_fwd(name, proj, cos, sin, sinks):
    T = proj.shape[0]
    W = WINDOW

    def body(*refs):
        o = refs[-1]
        o[...] = _attn_block_fn(pl.program_id(0), *[r[...] for r in refs[:-1]])

    return pl.pallas_call(
        body, name=name, grid=(T // W,),
        in_specs=_attn_specs(),
        out_specs=pl.BlockSpec((W, 4 * LANE), lambda n: (n, 0)),
        out_shape=jax.ShapeDtypeStruct((T, 4 * LANE), f32),
        compiler_params=_cparams("parallel"),
    )(proj, proj, proj, proj, proj, cos, sin, cos, sin, sinks)


def _attn_bwd(name, proj, cos, sin, sinks, d_oab):
    T = proj.shape[0]
    W = WINDOW

    def body(*refs):
        ins = [r[...] for r in refs[:10]]
        do = refs[10][...]
        dq_ref, dk_ref, dv_ref, ds_ref = refs[11:]
        n = pl.program_id(0)
        _, vjp = jax.vjp(functools.partial(_attn_block_fn, n), *ins)
        dq, dkp, dkc, dvp, dvc, _, _, _, _, dsk = vjp(do)
        dq_ref[...] = dq

        @pl.when(n == 0)
        def _():
            dk_ref[...] = jnp.zeros_like(dk_ref)
            dv_ref[...] = jnp.zeros_like(dv_ref)
            ds_ref[...] = jnp.zeros_like(ds_ref)

        cur = pl.ds(pl.multiple_of(n * W, W), W)
        dk_ref[cur, :] += dkc
        dv_ref[cur, :] += dvc
        ds_ref[...] += dsk

        @pl.when(n > 0)
        def _():
            prv = pl.ds(pl.multiple_of((n - 1) * W, W), W)
            dk_ref[prv, :] += dkp
            dv_ref[prv, :] += dvp

    return pl.pallas_call(
        body, name=name, grid=(T // W,),
        in_specs=_attn_specs() + [pl.BlockSpec((W, 4 * LANE), lambda n: (n, 0))],
        out_specs=[pl.BlockSpec((W, 4 * LANE), lambda n: (n, 0)),
                   pl.BlockSpec((T, LANE), lambda n: (0, 0)),
                   pl.BlockSpec((T, LANE), lambda n: (0, 0)),
                   pl.BlockSpec((1, A_Q_HEADS), lambda n: (0, 0))],
        out_shape=[jax.ShapeDtypeStruct((T, 4 * LANE), f32), jax.ShapeDtypeStruct((T, LANE), f32),
                   jax.ShapeDtypeStruct((T, LANE), f32), jax.ShapeDtypeStruct((1, A_Q_HEADS), f32)],
        compiler_params=_cparams("arbitrary"),
    )(proj, proj, proj, proj, proj, cos, sin, cos, sin, sinks, d_oab)


def _bdot(spec, a, b, precision=None):
    return jnp.einsum(spec, a, b, preferred_element_type=f32, precision=precision)


@jax.custom_vjp
def _tri_inv(a):
    C = a.shape[-1]
    r = lax.broadcasted_iota(jnp.int32, (C, C), 0)
    c = lax.broadcasted_iota(jnp.int32, (C, C), 1)
    t = jnp.broadcast_to(jnp.where(r == c, 1.0, 0.0).astype(f32), a.shape)
    for j in range(C - 1):
        t = t - a[:, :, j:j + 1] * t[:, j:j + 1, :]
    return t


def _tri_inv_fwd(a):
    t = _tri_inv(a)
    return t, t


def _tri_inv_bwd(t, g):
    C = t.shape[-1]
    r = lax.broadcasted_iota(jnp.int32, (C, C), 0)
    c = lax.broadcasted_iota(jnp.int32, (C, C), 1)
    x = _bdot("hki,hkj->hij", t, g, precision=lax.Precision.HIGHEST)
    y = _bdot("hik,hjk->hij", x, t, precision=lax.Precision.HIGHEST)
    return (jnp.where(r > c, -y, 0.0),)


_tri_inv.defvjp(_tri_inv_fwd, _tri_inv_bwd)


@jax.custom_vjp
def _tri_inv_saved(a, t):
    return t


_tri_inv_saved.defvjp(lambda a, t: (t, t), lambda t, g: (_tri_inv_bwd(t, g)[0], jnp.zeros_like(t)))


def _silu(x):
    return x * jax.nn.sigmoid(x)


def _l2n(x):
    return x * lax.rsqrt(jnp.sum(x * x, axis=-1, keepdims=True) + NORM_EPS)


def _delta_chunk_fn(cq, ck, cv, z, lg, a_log, dt_bias, norm_w, S, t_saved=None, want_t=False):
    C = B_CHUNK
    lane = lax.broadcasted_iota(jnp.int32, (C, LANE), 1)
    pick = lambda l0: jnp.concatenate(
        [jnp.sum(jnp.where(lane == l0 + h, lg, 0.0), axis=1, keepdims=True)[None] for h in range(B_HEADS)], axis=0)
    bl, al = pick(0), pick(B_HEADS)
    q = _l2n(_silu(cq)) * (B_HEAD_DIM ** -0.5)
    k = _l2n(_silu(ck))
    v = _silu(cv)
    beta = jax.nn.sigmoid(bl)
    g = -jnp.exp(a_log) * jax.nn.softplus(al + dt_bias)
    r = lax.broadcasted_iota(jnp.int32, (C, C), 0)
    c = lax.broadcasted_iota(jnp.int32, (C, C), 1)
    eye = r == c
    g_row = jnp.sum(jnp.where(eye, g, 0.0), axis=1, keepdims=True)
    gc = jnp.sum(jnp.where(c <= r, g_row, 0.0), axis=2, keepdims=True)
    gc_row = jnp.sum(jnp.where(eye, gc, 0.0), axis=1, keepdims=True)
    decay_incl = jnp.exp(jnp.where(r >= c, gc - gc_row, -jnp.inf))
    decay_strict = jnp.where(r > c, decay_incl, 0.0)
    kb = k * beta
    vb = v * beta
    kbf = k.astype(bf16)
    a_mat = _bdot("hik,hjk->hij", kb.astype(bf16), kbf) * decay_strict
    t_f32 = _tri_inv(a_mat) if t_saved is None else _tri_inv_saved(a_mat, t_saved)
    t_mat = t_f32.astype(bf16)
    eg = jnp.exp(gc)
    u = _bdot("hij,hjv->hiv", t_mat, vb.astype(bf16))
    w = _bdot("hij,hjk->hik", t_mat, (kb * eg).astype(bf16))
    qk = _bdot("hik,hjk->hij", q.astype(bf16), kbf) * decay_incl
    g_last = jnp.sum(g, axis=1, keepdims=True)
    k_tail = k * jnp.exp(g_last - gc)
    Sb = S.astype(bf16)
    v_new = u - _bdot("hck,hkv->hcv", w.astype(bf16), Sb)
    o = _bdot("hck,hkv->hcv", (q * eg).astype(bf16), Sb) + _bdot("hij,hjv->hiv", qk.astype(bf16), v_new.astype(bf16))
    S_new = S * jnp.exp(g_last) + _bdot("hck,hcv->hkv", k_tail.astype(bf16), v_new.astype(bf16))
    ob = o * lax.rsqrt(jnp.mean(o * o, axis=-1, keepdims=True) + NORM_EPS) * norm_w
    return (ob * _silu(z), S_new) + ((t_f32,) if want_t else ())


def _delta_in_specs(rev, N):
    C = B_CHUNK
    ix = (lambda n: N - 1 - n) if rev else (lambda n: n)
    specs = [pl.BlockSpec((C, 3 * B_HEADS * LANE), lambda n: (ix(n), 0))]
    specs += [pl.BlockSpec((C, LANE), lambda n, h=h: (ix(n), CB_Z + h)) for h in range(B_HEADS)]
    specs += [
        pl.BlockSpec((C, LANE), lambda n: (ix(n), CB_LG)),
        pl.BlockSpec((B_HEADS, 1, 1), lambda n: (0, 0, 0)),
        pl.BlockSpec((B_HEADS, 1, 1), lambda n: (0, 0, 0)),
        pl.BlockSpec((1, LANE), lambda n: (0, 0)),
    ]
    return specs


def _delta_inputs(c_ref, z_refs, lg, al, dt, nw):
    H = B_HEADS
    part = lambda p: jnp.stack([c_ref[:, LANE * (p * H + h):LANE * (p * H + h + 1)] for h in range(H)])
    return (part(0), part(1), part(2), jnp.stack([z[...] for z in z_refs]), lg[...], al[...], dt[...], nw[...])


def _delta_fwd(name, c, proj, a_log, dt_bias, norm_w):
    T = c.shape[0]
    C = B_CHUNK
    N = T // C
    Dh = B_HEAD_DIM
    H = B_HEADS

    def body(*refs):
        c_ref, z_refs, (lg, al, dt, nw) = refs[0], refs[1:1 + H], refs[1 + H:5 + H]
        o_ref, s_ref, t_ref, S = refs[5 + H:]

        @pl.when(pl.program_id(0) == 0)
        def _():
            S[...] = jnp.zeros_like(S)

        s0 = S[...]
        s_ref[...] = s0
        ob, s1, t = _delta_chunk_fn(*_delta_inputs(c_ref, z_refs, lg, al, dt, nw), s0, want_t=True)
        for h in range(H):
            o_ref[:, LANE * h:LANE * (h + 1)] = ob[h]
        t_ref[...] = t
        S[...] = s1

    return pl.pallas_call(
        body, name=name, grid=(N,),
        in_specs=_delta_in_specs(False, N),
        out_specs=[pl.BlockSpec((C, H * LANE), lambda n: (n, 0)),
                   pl.BlockSpec((H, None, Dh, Dh), lambda n: (0, n, 0, 0)),
                   pl.BlockSpec((H, None, C, C), lambda n: (0, n, 0, 0))],
        out_shape=[jax.ShapeDtypeStruct((T, H * Dh), f32), jax.ShapeDtypeStruct((H, N, Dh, Dh), f32),
                   jax.ShapeDtypeStruct((H, N, C, C), f32)],
        scratch_shapes=[pltpu.VMEM((H, Dh, Dh), f32)],
        compiler_params=_cparams("arbitrary"),
    )(c, *([proj] * H), proj, a_log, dt_bias, norm_w)


def _delta_bwd(name, c, proj, a_log, dt_bias, norm_w, s_saved, t_saved, d_oab):
    T = c.shape[0]
    C = B_CHUNK
    N = T // C
    Dh = B_HEAD_DIM
    H = B_HEADS

    def body(*refs):
        c_ref, z_refs, (lg, al, dt, nw) = refs[0], refs[1:1 + H], refs[1 + H:5 + H]
        s_ref, t_ref, do_ref = refs[5 + H:8 + H]
        dc, dz, dlg, dal, ddt, dnw, dS = refs[8 + H:]

        @pl.when(pl.program_id(0) == 0)
        def _():
            dS[...] = jnp.zeros_like(dS)
            dal[...] = jnp.zeros_like(dal)
            ddt[...] = jnp.zeros_like(ddt)
            dnw[...] = jnp.zeros_like(dnw)

        _, vjp = jax.vjp(functools.partial(_delta_chunk_fn, t_saved=t_ref[...]),
                         *_delta_inputs(c_ref, z_refs, lg, al, dt, nw), s_ref[...])
        do = jnp.stack([do_ref[:, LANE * h:LANE * (h + 1)] for h in range(H)])
        g = vjp((do, dS[...]))
        for h in range(H):
            for p in range(3):
                dc[:, LANE * (p * H + h):LANE * (p * H + h + 1)] = g[p][h]
            dz[:, LANE * h:LANE * (h + 1)] = g[3][h]
        dlg[...] = g[4]
        dal[...] += g[5]
        ddt[...] += g[6]
        dnw[...] += g[7]
        dS[...] = g[8]

    rn = lambda n: N - 1 - n
    return pl.pallas_call(
        body, name=name, grid=(N,),
        in_specs=_delta_in_specs(True, N) + [
            pl.BlockSpec((H, None, Dh, Dh), lambda n: (0, rn(n), 0, 0)),
            pl.BlockSpec((H, None, C, C), lambda n: (0, rn(n), 0, 0)),
            pl.BlockSpec((C, H * LANE), lambda n: (rn(n), 1)),
        ],
        out_specs=[
            pl.BlockSpec((C, 3 * H * LANE), lambda n: (rn(n), 0)),
            pl.BlockSpec((C, H * LANE), lambda n: (rn(n), 0)),
            pl.BlockSpec((C, LANE), lambda n: (rn(n), 0)),
            pl.BlockSpec((H, 1, 1), lambda n: (0, 0, 0)),
            pl.BlockSpec((H, 1, 1), lambda n: (0, 0, 0)),
            pl.BlockSpec((1, LANE), lambda n: (0, 0)),
        ],
        out_shape=[jax.ShapeDtypeStruct((T, 3 * H * Dh), f32), jax.ShapeDtypeStruct((T, H * Dh), f32),
                   jax.ShapeDtypeStruct((T, LANE), f32), jax.ShapeDtypeStruct((H, 1, 1), f32),
                   jax.ShapeDtypeStruct((H, 1, 1), f32), jax.ShapeDtypeStruct((1, LANE), f32)],
        scratch_shapes=[pltpu.VMEM((H, Dh, Dh), f32)],
        compiler_params=_cparams("arbitrary"),
    )(c, *([proj] * H), proj, a_log, dt_bias, norm_w, s_saved, t_saved, d_oab)


def _blockdiag_fwd(name, xc, w_a, w_x, tm=512):
    T, Wd = xc.shape
    bw = Wd // LRU_BLOCKS
    tm = min(tm, T)

    def body(x_ref, wa_ref, wx_ref, oa, ox):
        xb = x_ref[...].astype(bf16)
        oa[...] = _dot(xb, wa_ref[...].astype(bf16), NN)
        ox[...] = _dot(xb, wx_ref[...].astype(bf16), NN)

    xs = pl.BlockSpec((tm, bw), lambda i, h: (i, h))
    ws = pl.BlockSpec((None, bw, bw), lambda i, h: (h, 0, 0))
    return pl.pallas_call(
        body, name=name, grid=(T // tm, LRU_BLOCKS), in_specs=[xs, ws, ws], out_specs=[xs, xs],
        out_shape=[jax.ShapeDtypeStruct((T, Wd), f32)] * 2,
        compiler_params=_cparams("parallel", "parallel"),
    )(xc, w_a, w_x)


def _blockdiag_bwd_dx(name, dpr, dpi, w_a, w_x, addend, tm=512):
    T, Wd = dpr.shape
    bw = Wd // LRU_BLOCKS
    tm = min(tm, T)

    def body(dr, di, wa_ref, wx_ref, add, o):
        o[...] = (add[...] + _dot(dr[...].astype(bf16), wa_ref[...].astype(bf16), NT)
                  + _dot(di[...].astype(bf16), wx_ref[...].astype(bf16), NT))

    xs = pl.BlockSpec((tm, bw), lambda i, h: (i, h))
    ws = pl.BlockSpec((None, bw, bw), lambda i, h: (h, 0, 0))
    return pl.pallas_call(
        body, name=name, grid=(T // tm, LRU_BLOCKS), in_specs=[xs, xs, ws, ws, xs], out_specs=xs,
        out_shape=jax.ShapeDtypeStruct((T, Wd), f32),
        compiler_params=_cparams("parallel", "parallel"),
    )(dpr, dpi, w_a, w_x, addend)


def _blockdiag_bwd_dw(name, xc, dpr, dpi, tk=512):
    T, Wd = xc.shape
    bw = Wd // LRU_BLOCKS
    tk = min(tk, T)

    def body(x_ref, dr, di, oa, ox):
        @pl.when(pl.program_id(1) == 0)
        def _():
            oa[...] = jnp.zeros_like(oa)
            ox[...] = jnp.zeros_like(ox)

        xb = x_ref[...].astype(bf16)
        oa[...] += _dot(xb, dr[...].astype(bf16), TN)
        ox[...] += _dot(xb, di[...].astype(bf16), TN)

    xs = pl.BlockSpec((tk, bw), lambda h, k: (k, h))
    ws = pl.BlockSpec((None, bw, bw), lambda h, k: (h, 0, 0))
    return pl.pallas_call(
        body, name=name, grid=(LRU_BLOCKS, T // tk), in_specs=[xs, xs, xs], out_specs=[ws, ws],
        out_shape=[jax.ShapeDtypeStruct((LRU_BLOCKS, bw, bw), f32)] * 2,
        compiler_params=_cparams("parallel", "arbitrary"),
    )(xc, dpr, dpi)


def _scan(name, a, b, reverse, tt=512, cb=512):
    T, Wd = a.shape
    tt, cb = min(tt, T), min(cb, Wd)
    nt = T // tt
    ng = tt // SUBLANE

    def body(a_ref, b_ref, o_ref, carry):
        @pl.when(pl.program_id(1) == 0)
        def _():
            carry[...] = jnp.zeros_like(carry)

        row = lax.broadcasted_iota(jnp.int32, (SUBLANE, cb), 0)

        def step(gi, hp):
            g = (ng - 1 - gi) if reverse else gi
            off = pl.multiple_of(g * SUBLANE, SUBLANE)
            A = a_ref[pl.ds(off, SUBLANE), :]
            B = b_ref[pl.ds(off, SUBLANE), :]
            for s in (1, 2, 4):
                sh = (SUBLANE - s) if reverse else s
                As = pltpu.roll(A, sh, axis=0)
                Bs = pltpu.roll(B, sh, axis=0)
                valid = (row < SUBLANE - s) if reverse else (row >= s)
                B = jnp.where(valid, A * Bs + B, B)
                A = jnp.where(valid, A * As, A)
            hcur = A * hp + B
            o_ref[pl.ds(off, SUBLANE), :] = hcur
            edge = hcur[0:1, :] if reverse else hcur[SUBLANE - 1:SUBLANE, :]
            return jnp.broadcast_to(edge, (SUBLANE, cb))

        carry[...] = lax.fori_loop(0, ng, step, carry[...])

    ti = (lambda c, i: (nt - 1 - i, c)) if reverse else (lambda c, i: (i, c))
    spec = pl.BlockSpec((tt, cb), ti)
    return pl.pallas_call(
        body, name=name, grid=(Wd // cb, nt), in_specs=[spec, spec], out_specs=spec,
        out_shape=jax.ShapeDtypeStruct((T, Wd), f32),
        scratch_shapes=[pltpu.VMEM((SUBLANE, cb), f32)],
        compiler_params=_cparams("parallel", "arbitrary"),
    )(a, b)


def _relu2_epilogue(r):
    h = jnp.maximum(r, 0.0)
    return r, h * h


def _drelu2_epilogue(r, a):
    return (r * (2.0 * jnp.maximum(a, 0.0)),)


def _add_epilogue(r, e):
    return (r + e,)


def _merge_cols(name, g, tm=256):
    _, L, R, s = g.shape

    def body(g_ref, o_ref):
        for d in range(N_DEV):
            o_ref[:, s * d:s * (d + 1)] = g_ref[d].astype(bf16)
        o_ref[:, N_DEV * s:] = jnp.zeros((tm, HYB_PROJ_PAD - N_DEV * s), bf16)

    return pl.pallas_call(
        body, name=name, grid=(L, R // tm),
        in_specs=[pl.BlockSpec((N_DEV, None, tm, s), lambda l, i: (0, l, i, 0))],
        out_specs=pl.BlockSpec((None, tm, HYB_PROJ_PAD), lambda l, i: (l, i, 0)),
        out_shape=jax.ShapeDtypeStruct((L, R, HYB_PROJ_PAD), bf16),
        compiler_params=_cparams("parallel", "parallel"),
    )(g)


def _split_cols(name, dw, tm=256):
    R = dw.shape[0]
    s = HYB_PROJ // N_DEV

    def body(g_ref, o_ref):
        for d in range(N_DEV):
            o_ref[d] = g_ref[:, s * d:s * (d + 1)].astype(bf16)

    return pl.pallas_call(
        body, name=name, grid=(R // tm,),
        in_specs=[pl.BlockSpec((tm, HYB_PROJ_PAD), lambda i: (i, 0))],
        out_specs=pl.BlockSpec((N_DEV, tm, s), lambda i: (0, i, 0)),
        out_shape=jax.ShapeDtypeStruct((N_DEV, R, s), bf16),
        compiler_params=_cparams("parallel"),
    )(dw)


def _rows_to_dev(dw):
    nb, r, c = dw.shape
    t = dw.reshape(nb, N_DEV, r // N_DEV, c)
    return jnp.moveaxis(t, 1, 0).reshape(N_DEV, nb * (r // N_DEV), c).astype(bf16)


def _ln_epilogue(r, x, g, b):
    return r, _ln_res_fn(x, r, g, b)[0]


def _hybrid_fwd(tag, x, W, j, cos, sin, ln, before_out):
    proj = _mm(f"{tag}_proj", x, W["hyb_w_in"][j], "nn", b_kind="lead", b_lead=0)
    o_a = _attn_fwd(f"{tag}_attn", proj, cos, sin, W["hyb_sinks"][j][None, :])
    c = _conv_fwd(f"{tag}_conv", proj, CB_CONV, 12, W["hyb_conv_w"][j], None)
    o_b, s_saved, t_saved = _delta_fwd(f"{tag}_delta", c, proj, W["hyb_a_log"][j].reshape(B_HEADS, 1, 1),
                                       W["hyb_dt_bias"][j].reshape(B_HEADS, 1, 1), W["hyb_norm_w"][j][None, :])
    o_ab = jnp.concatenate([o_a, o_b], axis=1)
    before_out(o_ab)
    mix, x1 = _mm(f"{tag}_out", o_ab, W["hyb_w_out"][j], "nn", b_kind="lead", b_lead=0, epilogue=_ln_epilogue,
                  extras=(x,), params=ln, out_dtypes=(f32, f32), tm=512)
    return mix, x1, (proj, c, s_saved, t_saved, o_ab)


def _hybrid_bwd(tag, x, dmix, addend, W, j, cos, sin, saved, G, send_early):
    proj, c, s_saved, t_saved, o_ab = saved
    T = x.shape[0]
    d_oab = _mm(f"{tag}_dout", dmix, W["hyb_w_out"][j], "nt", b_kind="lead", b_lead=0)
    G["hyb_w_out"][j] = _mm(f"{tag}_dwout", o_ab, dmix, "tn", out_dtypes=(bf16,)).reshape(N_DEV, -1, D_MODEL)
    sinks = W["hyb_sinks"][j][None, :] + send_early({("hyb_w_out", j): G["hyb_w_out"][j]})
    dq, dk, dv, dsinks = _attn_bwd(f"{tag}_dattn", proj, cos, sin, sinks, d_oab)
    a_log = W["hyb_a_log"][j].reshape(B_HEADS, 1, 1)
    dt_bias = W["hyb_dt_bias"][j].reshape(B_HEADS, 1, 1)
    dc, dz, dlg, dal, ddt, dnw = _delta_bwd(f"{tag}_ddelta", c, proj, a_log, dt_bias, W["hyb_norm_w"][j][None, :],
                                            s_saved, t_saved, d_oab)
    dconv_in, dconv_w, _ = _conv_bwd(f"{tag}_dconv", dc, proj, CB_CONV, 12, W["hyb_conv_w"][j])
    dproj = jnp.concatenate([dq, dk, dv, dconv_in, dz, dlg,
                             jnp.zeros((T, HYB_PROJ_PAD - (CB_LG + 1) * LANE), f32)], axis=1)
    dx = _mm(f"{tag}_dx", dproj, W["hyb_w_in"][j], "nt", b_kind="lead", b_lead=0, epilogue=_add_epilogue,
             extras=(addend,))
    G["hyb_w_in"][j] = _split_cols(f"{tag}_dwin_split", _mm(f"{tag}_dwin", x, dproj, "tn"))
    G["hyb_sinks"][j] = dsinks[0]
    G["hyb_conv_w"][j] = dconv_w
    G["hyb_a_log"][j] = dal.reshape(B_HEADS)
    G["hyb_dt_bias"][j] = ddt.reshape(B_HEADS)
    G["hyb_norm_w"][j] = dnw[0]
    return dx


def _rec_fwd(tag, x, W, j, ln, before_out):
    Wd = D_MODEL
    proj = _mm(f"{tag}_proj", x, W["rec_w_in"][j], "nn", b_kind="devcol", b_lead=0)
    xc = _conv_fwd(f"{tag}_conv", proj, 0, Wd // LANE, W["rec_conv_w"][j], W["rec_conv_b"][j][None, :])
    pre_r, pre_i = _blockdiag_fwd(f"{tag}_gates", xc, W["rec_w_a"][j][0], W["rec_w_x"][j][0])
    pars = [W["rec_b_a"][j][None, :], W["rec_b_x"][j][None, :], W["rec_lambda"][j][None, :]]
    a, b = _tl_fwd(f"{tag}_pre", _rglru_pre_fn, [(pre_r, 0, Wd), (pre_i, 0, Wd), (xc, 0, Wd)], pars, [Wd, Wd], [f32, f32])
    h = _scan(f"{tag}_scan", a, b, False)
    (hg,) = _tl_fwd(f"{tag}_gate", _rec_gate_fn, [(h, 0, Wd), (proj, Wd // LANE, Wd)], [], [Wd], [f32])
    before_out(hg)
    mix, x1 = _mm(f"{tag}_out", hg, W["rec_w_out"][j], "nn", b_kind="lead", b_lead=0, epilogue=_ln_epilogue,
                  extras=(x,), params=ln, out_dtypes=(f32, f32), tm=512)
    return mix, x1, (proj, xc, pre_r, pre_i, a, h, hg)


def _rec_bwd(tag, x, dmix, addend, W, j, saved, G, send_early):
    proj, xc, pre_r, pre_i, a, h, hg = saved
    Wd = D_MODEL
    dhg = _mm(f"{tag}_dout", dmix, W["rec_w_out"][j], "nt", b_kind="lead", b_lead=0)
    G["rec_w_out"][j] = _mm(f"{tag}_dwout", hg, dmix, "tn", out_dtypes=(bf16,)).reshape(N_DEV, -1, D_MODEL)
    sent = send_early({("rec_w_out", j): G["rec_w_out"][j]})
    (dh, dgate), _ = _tl_bwd(f"{tag}_dgate", _rec_gate_fn, [(h, 0, Wd), (proj, Wd // LANE, Wd)], [], [(dhg, 0, Wd)])
    a_next = jnp.concatenate([a[1:], jnp.zeros((1, Wd), f32)], axis=0)
    h_prev = jnp.concatenate([jnp.zeros((1, Wd), f32), h[:-1]], axis=0)
    lam_t = _scan(f"{tag}_dscan", a_next, dh, True)
    pars = [W["rec_b_a"][j][None, :] + sent, W["rec_b_x"][j][None, :], W["rec_lambda"][j][None, :]]
    (dpr, dpi, dxc1), (db_a, db_x, dlam) = _tl_bwd(
        f"{tag}_dpre", _rglru_pre_fn, [(pre_r, 0, Wd), (pre_i, 0, Wd), (xc, 0, Wd)], pars,
        [(lam_t, 0, Wd), (h_prev, 0, Wd)], cot_fn=lambda lt, hp: (lt * hp, lt))
    dxc = _blockdiag_bwd_dx(f"{tag}_dgates_dx", dpr, dpi, W["rec_w_a"][j][0], W["rec_w_x"][j][0], dxc1)
    dwa, dwx = _blockdiag_bwd_dw(f"{tag}_dgates_dw", xc, dpr, dpi)
    G["rec_w_a"][j], G["rec_w_x"][j] = _rows_to_dev(dwa), _rows_to_dev(dwx)
    dxr, dconv_w, dconv_b = _conv_bwd(f"{tag}_dconv", dxc, proj, 0, Wd // LANE, W["rec_conv_w"][j])
    dproj = jnp.concatenate([dxr, dgate], axis=1)
    dx = _mm(f"{tag}_dx", dproj, W["rec_w_in"][j], "nt", b_kind="devcol", b_lead=0, epilogue=_add_epilogue,
             extras=(addend,))
    G["rec_w_in"][j] = _mm(f"{tag}_dwin", x, dproj, "tn", o_kind="devcol", out_dtypes=(bf16,))
    G["rec_conv_w"][j] = dconv_w
    G["rec_conv_b"][j] = dconv_b
    G["rec_b_a"][j] = db_a[0]
    G["rec_b_x"][j] = db_x[0]
    G["rec_lambda"][j] = dlam[0]
    return dx


def _local_step(x, target, W, load_layer, grads_ready):
    T = x.shape[0]
    cos, sin = _rope_tables(T)
    saved = []
    for layer in range(DEPTH):
        j = layer // 2
        tag = f"L{layer}"
        load_layer(layer, 0, x)
        ln1 = (W["ln1_g"][layer][None, :], W["ln1_b"][layer][None, :])
        before_out = functools.partial(load_layer, layer, 1)
        if layer % 2 == 0:
            mix, x1, sv = _hybrid_fwd(tag, x, W, j, cos, sin, ln1, before_out)
        else:
            mix, x1, sv = _rec_fwd(tag, x, W, j, ln1, before_out)
        a, h2 = _mm(f"{tag}_mlp1", x1, W["mlp_w1"][layer], "nn", b_kind="devcol", b_lead=0, epilogue=_relu2_epilogue,
                    out_dtypes=(f32, bf16))
        ln2 = (W["ln2_g"][layer][None, :], W["ln2_b"][layer][None, :])
        y, x2 = _mm(f"{tag}_mlp2", h2, W["mlp_w2"][layer], "nn", b_kind="devrow", b_lead=0, epilogue=_ln_epilogue,
                    extras=(x1,), params=ln2, out_dtypes=(f32, f32), tm=512)
        saved.append((x, sv, mix, x1, a, h2, y))
        x = x2
    loss, dx = _loss_head(x, target)

    G = {k: [None] * (DEPTH if k.startswith(("ln", "mlp")) else DEPTH // 2) for k in (
        "hyb_w_in", "hyb_sinks", "hyb_conv_w", "hyb_a_log", "hyb_dt_bias", "hyb_norm_w", "hyb_w_out",
        "rec_w_in", "rec_conv_w", "rec_conv_b", "rec_w_a", "rec_b_a", "rec_w_x", "rec_b_x", "rec_lambda", "rec_w_out",
        "ln1_g", "ln1_b", "mlp_w1", "mlp_w2", "ln2_g", "ln2_b")}
    order = jnp.zeros((1, 1), f32)
    for layer in reversed(range(DEPTH)):
        j = layer // 2
        tag = f"L{layer}"
        x0, sv, mix, x1, a, h2, y = saved[layer]
        ln2 = [W["ln2_g"][layer][None, :] + order, W["ln2_b"][layer][None, :]]
        (dx1_a, dy), (dg2, db2) = _tl_bwd(f"{tag}_dln2", _ln_res_fn, [(x1, 0, D_MODEL), (y, 0, D_MODEL)], ln2,
                                          [(dx, 0, D_MODEL)])
        G["ln2_g"][layer], G["ln2_b"][layer] = dg2[0], db2[0]
        da = _mm(f"{tag}_dmlp2", dy, W["mlp_w2"][layer], "nt", b_kind="devrow", b_lead=0, epilogue=_drelu2_epilogue,
                 extras=(a,), out_dtypes=(bf16,))
        G["mlp_w2"][layer] = _mm(f"{tag}_dw2", h2, dy, "tn", out_dtypes=(bf16,)).reshape(N_DEV, -1, D_MODEL)
        dx1 = _mm(f"{tag}_dmlp1", da, W["mlp_w1"][layer], "nt", b_kind="devcol", b_lead=0, epilogue=_add_epilogue,
                  extras=(dx1_a,))
        G["mlp_w1"][layer] = _mm(f"{tag}_dw1", x1, da, "tn", o_kind="devcol", out_dtypes=(bf16,))
        ln1 = [W["ln1_g"][layer][None, :], W["ln1_b"][layer][None, :]]
        (dx0_a, dmix), (dg1, db1) = _tl_bwd(f"{tag}_dln1", _ln_res_fn, [(x0, 0, D_MODEL), (mix, 0, D_MODEL)], ln1,
                                            [(dx1, 0, D_MODEL)])
        G["ln1_g"][layer], G["ln1_b"][layer] = dg1[0], db1[0]
        early = functools.partial(grads_ready, f"l{layer}_early",
                                  {(k, layer): G[k][layer] for k in ("mlp_w1", "mlp_w2")})
        if layer % 2 == 0:
            dx = _hybrid_bwd(tag, x0, dmix, dx0_a, W, j, cos, sin, sv, G, early)
        else:
            dx = _rec_bwd(tag, x0, dmix, dx0_a, W, j, sv, G, early)
        order = grads_ready(f"l{layer}_late", {}, {(k, i): G[k][i] for k, i in _layer_weights(layer)[:-2]
                                                  if not k.endswith("w_out")})
    big = {k for k, _ in BIG}
    return loss, dx, {k: jnp.stack(v) for k, v in G.items() if k not in big}


def _layer_weights(layer):
    j = layer // 2
    mixer = ["hyb_w_in", "hyb_w_out"] if layer % 2 == 0 else ["rec_w_in", "rec_w_out", "rec_w_a", "rec_w_x"]
    return [(k, j) for k in mixer] + [("mlp_w1", layer), ("mlp_w2", layer)]


def _my_coords():
    return lax.axis_index("x"), lax.axis_index("y"), lax.axis_index("c")


def _all_gather(name, arrays):
    na = len(arrays)

    def body(*refs):
        x_refs, out_refs = refs[:na], refs[na:2 * na]
        send_sems, recv_sems, local_sems = refs[2 * na:]
        x, y, c = _my_coords()
        me, sibling = (x, y, c), (x, y, 1 - c)
        chips = [(1 - x, y), (x, 1 - y), (1 - x, 1 - y)]

        def blk(a, px, py, pc):
            return out_refs[a].at[4 * px + 2 * py + pc]

        def copy(a, k, block, to, src=None):
            return pltpu.make_async_remote_copy(
                src_ref=blk(a, *block) if src is None else src, dst_ref=blk(a, *block),
                send_sem=send_sems.at[a, k], recv_sem=recv_sems.at[a, k],
                device_id=to, device_id_type=pl.DeviceIdType.MESH)

        mine = [pltpu.make_async_copy(x_refs[a], blk(a, *me), local_sems.at[a]) for a in range(na)]
        for cp in mine:
            cp.start()
        first = []
        for a in range(na):
            first.append(copy(a, 0, me, sibling, src=x_refs[a]))
            first += [copy(a, 1 + j, me, (*chip, c), src=x_refs[a]) for j, chip in enumerate(chips)]
        for cp in first:
            cp.start()
        passed = []
        for a in range(na):
            for j, chip in enumerate(chips):
                copy(a, 1 + j, (*chip, c), me).wait_recv()
                passed.append(copy(a, 4 + j, (*chip, c), sibling))
                passed[-1].start()
        for a in range(na):
            copy(a, 0, sibling, me).wait_recv()
            for j, chip in enumerate(chips):
                copy(a, 4 + j, (*chip, 1 - c), me).wait_recv()
        for cp in first + passed:
            cp.wait_send()
        for cp in mine:
            cp.wait()

    return pl.pallas_call(
        body, name=name,
        out_shape=[jax.ShapeDtypeStruct((N_DEV,) + a.shape, a.dtype) for a in arrays],
        in_specs=[pl.BlockSpec(memory_space=pl.ANY)] * na,
        out_specs=[pl.BlockSpec(memory_space=pl.ANY)] * na,
        scratch_shapes=[pltpu.SemaphoreType.DMA((na, 7)), pltpu.SemaphoreType.DMA((na, 7)),
                        pltpu.SemaphoreType.DMA((na,))],
    )(*arrays)


_HBM = pl.BlockSpec(memory_space=pltpu.HBM)
_SEM = pl.BlockSpec(memory_space=pltpu.SEMAPHORE)


def _push_copies(kind, x_refs, land_refs, send_sems, recv_sems, local_sems):
    x, y, c = _my_coords()
    me = 4 * x + 2 * y + c
    remote, local = [], []
    for a in range(len(x_refs)):
        local.append(pltpu.make_async_copy(x_refs[a] if kind == "gather" else x_refs[a].at[me], land_refs[a].at[me],
                                           local_sems.at[a]))
        for k in range(1, N_DEV):
            px = (1 - x) if (k >> 2) & 1 else x
            py = (1 - y) if (k >> 1) & 1 else y
            pc = (1 - c) if k & 1 else c
            remote.append(pltpu.make_async_remote_copy(
                src_ref=x_refs[a] if kind == "gather" else x_refs[a].at[4 * px + 2 * py + pc],
                dst_ref=land_refs[a].at[me],
                send_sem=send_sems.at[a * (N_DEV - 1) + k - 1], recv_sem=recv_sems.at[a * (N_DEV - 1) + k - 1],
                device_id=(px, py, pc), device_id_type=pl.DeviceIdType.MESH))
    return remote, local


_SIDE_EFFECT = pltpu.CompilerParams(has_side_effects=pltpu.SideEffectType.DATAFLOW_SIDE_EFFECTING)


def _push_start(name, kind, srcs, lands):
    na = len(srcs)

    def body(*refs):
        remote, local = _push_copies(kind, refs[:na], refs[na:2 * na], *refs[2 * na:2 * na + 3])
        for cp in remote + local:
            cp.start()
        token = refs[-1]
        token[...] = jnp.zeros_like(token)

    arrays = list(srcs) + list(lands)
    n_remote = na * (N_DEV - 1)
    res = pl.pallas_call(
        body, name=name,
        out_shape=(pltpu.SemaphoreType.DMA((n_remote,)), pltpu.SemaphoreType.DMA((n_remote,)),
                   pltpu.SemaphoreType.DMA((na,)), *[pltpu.HBM(t.shape, t.dtype) for t in arrays],
                   jax.ShapeDtypeStruct((SUBLANE, LANE), f32)),
        in_specs=[_HBM] * (2 * na),
        out_specs=(_SEM, _SEM, _SEM, *[_HBM] * (2 * na), pl.BlockSpec(memory_space=pltpu.VMEM)),
        input_output_aliases={i: 3 + i for i in range(2 * na)},
        compiler_params=_SIDE_EFFECT,
    )(*[pltpu.with_memory_space_constraint(t, pltpu.HBM) for t in arrays])
    return list(res[:3]), res[3:3 + na], res[3 + na:3 + 2 * na], res[-1][:1, :1]


def _push_wait(name, kind, sems, srcs, lands, after):
    na = len(srcs)

    def body(*refs):
        remote, local = _push_copies(kind, refs[:na], refs[na:2 * na], *refs[2 * na:2 * na + 3])
        for cp in remote:
            cp.wait_send()
            cp.wait_recv()
        for cp in local:
            cp.wait()

    arrays = list(srcs) + list(lands)
    res = pl.pallas_call(
        body, name=name,
        out_shape=tuple(pltpu.HBM(t.shape, t.dtype) for t in arrays),
        in_specs=[_HBM] * (2 * na) + [_SEM] * 3 + [pl.BlockSpec(memory_space=pl.ANY)],
        out_specs=tuple([_HBM] * (2 * na)),
        input_output_aliases={i: i for i in range(2 * na)},
        compiler_params=_SIDE_EFFECT,
    )(*arrays, *sems, after)
    return res[na:]


def _sum_blocks(name, land):
    _, R, n = land.shape
    tr = R

    def body(l_ref, o_ref):
        acc = l_ref[0].astype(f32)
        for s in range(1, N_DEV):
            acc = acc + l_ref[s].astype(f32)
        o_ref[...] = acc

    return pl.pallas_call(
        body, name=name, grid=(R // tr,),
        in_specs=[pl.BlockSpec((N_DEV, tr, n), lambda i: (0, i, 0))],
        out_specs=pl.BlockSpec((tr, n), lambda i: (i, 0)),
        out_shape=jax.ShapeDtypeStruct((R, n), f32),
        compiler_params=_cparams("parallel"),
    )(land)


def _adamw(name, w, g, m, v):
    shape = w.shape
    last = shape[-1]
    rows = math.prod(shape[:-1])
    tm = 256 if rows % 256 == 0 and rows > 256 else rows
    w2, g2, m2, v2 = (t.reshape(rows, last) for t in (w, g, m, v))

    def body(w_ref, g_ref, m_ref, v_ref, d_ref, mo_ref, vo_ref):
        gg = g_ref[...]
        mn = ADAM_B1 * m_ref[...] + (1.0 - ADAM_B1) * gg
        vn = ADAM_B2 * v_ref[...] + (1.0 - ADAM_B2) * jnp.square(gg)
        m_hat = mn / (1.0 - ADAM_B1 ** ADAM_STEP)
        v_hat = vn / (1.0 - ADAM_B2 ** ADAM_STEP)
        d_ref[...] = -ADAM_LR * (m_hat / (jnp.sqrt(v_hat) + ADAM_EPS) + ADAM_WD * w_ref[...])
        mo_ref[...] = mn
        vo_ref[...] = vn

    spec = pl.BlockSpec((tm, last), lambda i: (i, 0))
    d, mn, vn = pl.pallas_call(
        body, name=name, grid=(rows // tm,), in_specs=[spec] * 4, out_specs=[spec] * 3,
        out_shape=[jax.ShapeDtypeStruct((rows, last), f32)] * 3,
        compiler_params=_cparams("parallel"),
    )(w2, g2, m2, v2)
    return d.reshape(shape), mn.reshape(shape), vn.reshape(shape)


def _adamw_land(name, lands, w, m, v, tm=256):
    L = len(lands)
    _, R, C = lands[0].shape
    tm = min(tm, R)

    def body(*refs):
        l_refs, (w_ref, m_ref, v_ref, g_ref, d_ref, mo_ref, vo_ref) = refs[:L], refs[L:]
        for k in range(L):
            @pl.when(pl.program_id(0) == k)
            def _(k=k):
                gg = l_refs[k][0].astype(f32)
                for s in range(1, N_DEV):
                    gg = gg + l_refs[k][s].astype(f32)
                g_ref[...] = gg
                mn = ADAM_B1 * m_ref[...] + (1.0 - ADAM_B1) * gg
                vn = ADAM_B2 * v_ref[...] + (1.0 - ADAM_B2) * jnp.square(gg)
                m_hat = mn / (1.0 - ADAM_B1 ** ADAM_STEP)
                v_hat = vn / (1.0 - ADAM_B2 ** ADAM_STEP)
                d_ref[...] = -ADAM_LR * (m_hat / (jnp.sqrt(v_hat) + ADAM_EPS) + ADAM_WD * w_ref[...])
                mo_ref[...] = mn
                vo_ref[...] = vn

    land_specs = [pl.BlockSpec((N_DEV, tm, C), lambda l, i, k=k: (0, jnp.where(l == k, i, 0), 0)) for k in range(L)]
    spec = pl.BlockSpec((None, tm, C), lambda l, i: (l, i, 0))
    return pl.pallas_call(
        body, name=name, grid=(L, R // tm),
        in_specs=land_specs + [spec] * 3,
        out_specs=[spec] * 4,
        out_shape=[jax.ShapeDtypeStruct((L, R, C), f32)] * 4,
        compiler_params=_cparams("arbitrary", "arbitrary"),
    )(*lands, w, m, v)


BIG = [("hyb_w_in", 2), ("hyb_w_out", 1), ("rec_w_in", 2), ("rec_w_out", 1), ("rec_w_a", 2), ("rec_w_x", 2),
       ("mlp_w1", 2), ("mlp_w2", 1)]
SMALL = [("hyb_conv_w", 2), ("rec_conv_w", 2), ("rec_conv_b", 1), ("rec_b_a", 1), ("rec_b_x", 1), ("rec_lambda", 1)]
REPL = ["hyb_sinks", "hyb_a_log", "hyb_dt_bias", "hyb_norm_w", "ln1_g", "ln1_b", "ln2_g", "ln2_b"]
WEIGHTS = ["hyb_w_in", "hyb_sinks", "hyb_conv_w", "hyb_a_log", "hyb_dt_bias", "hyb_norm_w", "hyb_w_out", "rec_w_in",
           "rec_conv_w", "rec_conv_b", "rec_w_a", "rec_b_a", "rec_w_x", "rec_b_x", "rec_lambda", "rec_w_out",
           "ln1_g", "ln1_b", "mlp_w1", "mlp_w2", "ln2_g", "ln2_b"]


def _pack_rows(parts, dtype, row_mult):
    lead = parts[0].shape[:-1]
    flat = jnp.concatenate([p.astype(dtype) for p in parts], axis=-1)
    n = flat.shape[-1]
    unit = row_mult * LANE
    pad = (-n) % unit
    if pad:
        flat = jnp.concatenate([flat, jnp.zeros(lead + (pad,), dtype)], axis=-1)
    return flat.reshape(lead + ((n + pad) // LANE, LANE))


def _gather_full(gathered, shard_shapes, table):
    flat = gathered.reshape(N_DEV, -1)
    out, off = {}, 0
    for name, ax in table:
        shp = shard_shapes[name]
        n = math.prod(shp)
        arr = flat[:, off:off + n].reshape((N_DEV,) + shp)
        off += n
        arr = jnp.moveaxis(arr, 0, ax)
        out[name] = arr.reshape(shp[:ax] + (N_DEV * shp[ax],) + shp[ax + 1:])
    return out


def _matmul_layouts(tag, gw):
    out = {}
    bw = D_MODEL // LRU_BLOCKS
    for k, g in gw.items():
        L = g.shape[1]
        if k == "hyb_w_in":
            out[k] = _merge_cols(f"{tag}_w_in_merge", g)
        elif k in ("hyb_w_out", "rec_w_out"):
            out[k] = jnp.swapaxes(g, 0, 1).reshape(L, D_MODEL, D_MODEL)
        elif k in ("rec_w_a", "rec_w_x"):
            out[k] = jnp.moveaxis(g, 0, 2).reshape(L, LRU_BLOCKS, bw, bw)
        else:
            out[k] = g
    return out


def kernel(x, hyb_w_in, hyb_sinks, hyb_conv_w, hyb_a_log, hyb_dt_bias, hyb_norm_w, hyb_w_out, rec_w_in, rec_conv_w, rec_conv_b, rec_w_a, rec_b_a, rec_w_x, rec_b_x, rec_lambda, rec_w_out, ln1_g, ln1_b, mlp_w1, mlp_w2, ln2_g, ln2_b, loss_target, m_hyb_w_in, m_hyb_sinks, m_hyb_conv_w, m_hyb_a_log, m_hyb_dt_bias, m_hyb_norm_w, m_hyb_w_out, m_rec_w_in, m_rec_conv_w, m_rec_conv_b, m_rec_w_a, m_rec_b_a, m_rec_w_x, m_rec_b_x, m_rec_lambda, m_rec_w_out, m_ln1_g, m_ln1_b, m_mlp_w1, m_mlp_w2, m_ln2_g, m_ln2_b, v_hyb_w_in, v_hyb_sinks, v_hyb_conv_w, v_hyb_a_log, v_hyb_dt_bias, v_hyb_norm_w, v_hyb_w_out, v_rec_w_in, v_rec_conv_w, v_rec_conv_b, v_rec_w_a, v_rec_b_a, v_rec_w_x, v_rec_b_x, v_rec_lambda, v_rec_w_out, v_ln1_g, v_ln1_b, v_mlp_w1, v_mlp_w2, v_ln2_g, v_ln2_b):
    args = locals()
    w = {k: args[k] for k in WEIGHTS}
    m = {k: args["m_" + k] for k in WEIGHTS}
    v = {k: args["v_" + k] for k in WEIGHTS}
    shard_shapes = {k: tuple(t.shape) for k, t in w.items()}
    xi, yi, ci = _my_coords()
    me = 4 * xi + 2 * yi + ci

    in_flight = {}

    def install(tag, names, got):
        for (k, i), arr in zip(names, _matmul_layouts(tag, {k: g for (k, _), g in zip(names, got)}).values()):
            W[k][i] = arr

    def start_gather(tag, names):
        srcs = [w[k][i:i + 1].astype(bf16) for k, i in names]
        *pending, zero = _push_start(f"gather_{tag}_start", "gather", srcs,
                                     [lax.empty((N_DEV,) + s.shape, bf16) for s in srcs])
        in_flight[tag] = (names, pending)
        return zero

    def finish_gather(tag, after):
        names, pending = in_flight.pop(tag)
        install(tag, names, _push_wait(f"gather_{tag}_wait", "gather", *pending, after))

    def started(layer, zero):
        k = "hyb_sinks" if layer % 2 == 0 else "rec_conv_b"
        W[k] = W[k] + zero

    first = _layer_weights(0)[:1]
    gathered0 = _all_gather("gather_first", [w[k][i:i + 1].astype(bf16) for k, i in first]
                            + [_pack_rows([w[k].reshape(-1) for k, _ in SMALL], f32, SUBLANE)])
    W = _gather_full(gathered0[-1], shard_shapes, SMALL)
    W.update({k: w[k] for k in REPL})
    W.update({k: {} for k, _ in BIG})
    install("l0a", first, gathered0[:-1])
    started(0, start_gather("l0b", _layer_weights(0)[1:]) + start_gather("l1", _layer_weights(1)))

    def load_layer(layer, stage, after):
        if layer == 0 and stage == 1:
            finish_gather("l0b", after)
        if layer > 0 and stage == 0:
            finish_gather(f"l{layer}", after)
            if layer + 1 < DEPTH:
                started(layer, start_gather(f"l{layer + 1}", _layer_weights(layer + 1)))

    grads_in_flight = {}

    def grads_ready(tag, a, b):
        g = {**a, **b}
        srcs = list(g.values())
        *pending, zero = _push_start(f"scatter_{tag}_start", "scatter", srcs, [lax.empty(s.shape, bf16) for s in srcs])
        grads_in_flight[tag] = (list(g.keys()), pending)
        return zero

    loss_local, grad_x, G = _local_step(x[0], loss_target[0], W, load_layer, grads_ready)
    loss = lax.psum(loss_local, MESH_AXES)

    landed = {}

    def land(tag, after):
        keys, pending = grads_in_flight[tag]
        landed.update(zip(keys, _push_wait(f"scatter_{tag}_wait", "scatter", *pending, after)))

    tags = list(grads_in_flight)
    for tag in tags[:-1]:
        land(tag, grad_x)
    rest = _pack_rows([G[k].reshape(-1) for k, _ in SMALL] + [G[k].reshape(-1) for k in REPL], f32, SUBLANE)
    g_rest = _sum_blocks("sum_rest", _all_gather("gather_rest", [rest])[0]).reshape(-1)

    grads, delta, new_m, new_v = {}, {}, {}, {}

    def adamw_big(k):
        shp = shard_shapes[k]
        s3 = (shp[0], math.prod(shp[1:-1]), shp[-1])
        lands = [landed[(k, i)].reshape((N_DEV,) + s3[1:]) for i in range(shp[0])]
        res = _adamw_land("adamw_" + k, lands, w[k].reshape(s3), m[k].reshape(s3), v[k].reshape(s3))
        grads[k], delta[k], new_m[k], new_v[k] = (r.reshape(shp) for r in res)

    late = {k for k, _ in grads_in_flight[tags[-1]][0]}
    for k in [k for k, _ in BIG if k not in late]:
        adamw_big(k)
        done = new_v[k]
    land(tags[-1], done)
    for k in [k for k, _ in BIG if k in late]:
        adamw_big(k)
    off = 0
    for k, ax in SMALL:
        full_shape = G[k].shape
        n = math.prod(full_shape)
        full = g_rest[off:off + n].reshape(full_shape)
        off += n
        s = shard_shapes[k][ax]
        grads[k] = lax.dynamic_slice_in_dim(full, me * s, s, axis=ax)
    for k in REPL:
        n = math.prod(shard_shapes[k])
        grads[k] = g_rest[off:off + n].reshape(shard_shapes[k])
        off += n

    for k in [k for k, _ in SMALL] + REPL:
        delta[k], new_m[k], new_v[k] = _adamw("adamw_" + k, w[k], grads[k], m[k], v[k])

    return (loss, grad_x[None], *[grads[k] for k in WEIGHTS], *[delta[k] for k in WEIGHTS],
            *[new_m[k] for k in WEIGHTS], *[new_v[k] for k in WEIGHTS])
```

```python
import functools
import math

import jax
import jax.numpy as jnp
from jax import lax
from jax.experimental import pallas as pl
from jax.experimental.pallas import tpu as pltpu

f32 = jnp.float32
bf16 = jnp.bfloat16

N_DEV = 8
D_MODEL = 1024
DEPTH = 4
A_HEAD_DIM = 64
A_Q_HEADS = 8
WINDOW = 128
ROPE_THETA = 10000.0
B_HEADS = 4
B_HEAD_DIM = 128
B_CHUNK = 64
LRU_BLOCKS = 4
LRU_C = 8.0
D_FF = 4 * D_MODEL
HYB_PROJ = 2824
HYB_PROJ_PAD = 3072
DN_ALPHA = (2 * DEPTH) ** 0.25
LN_EPS = 1e-5
NORM_EPS = 1e-6
ADAM_LR = 0.001
ADAM_B1 = 0.9
ADAM_B2 = 0.999
ADAM_EPS = 1e-08
ADAM_WD = 0.01
ADAM_STEP = 10

LANE = 128
SUBLANE = 8
VMEM_LIMIT = 48 * 1024 * 1024

CB_QA, CB_KA, CB_VA, CB_CONV, CB_Z, CB_LG = 0, 4, 5, 6, 18, 22

MESH_AXES = ("x", "y", "c")


def _cparams(*sem):
    return pltpu.CompilerParams(dimension_semantics=sem, vmem_limit_bytes=VMEM_LIMIT)


def _dot(a, b, dims, precision=None):
    return lax.dot_general(a, b, (dims, ((), ())), preferred_element_type=f32, precision=precision)


NN = ((1,), (0,))
NT = ((1,), (1,))
TN = ((0,), (0,))


def _mat_spec(arr, kind, lead, br, bc, rb, cb):
    if kind == "plain":
        return pl.BlockSpec((br, bc), lambda i, j, k: (rb(i, j, k), cb(i, j, k)))
    if kind == "lead":
        return pl.BlockSpec((None, br, bc), lambda i, j, k: (lead, rb(i, j, k), cb(i, j, k)))
    if kind == "devcol":
        assert bc == arr.shape[-1]
        return pl.BlockSpec((None, None, br, bc), lambda i, j, k: (cb(i, j, k), lead, rb(i, j, k), 0))
    assert kind == "devrow" and br == arr.shape[-2]
    return pl.BlockSpec((None, None, br, bc), lambda i, j, k: (rb(i, j, k), lead, 0, cb(i, j, k)))


def _mm(name, a, b, mode, *, b_kind="plain", b_lead=0, o_kind="plain", epilogue=None, extras=(), params=(),
        out_dtypes=(f32,), tm=1024, tn=1024, tk=None):
    if tk is None:
        tk = 512 if mode == "tn" else 1024
    if b_kind in ("plain", "lead"):
        b_rows, b_cols = b.shape[-2:]
    elif b_kind == "devcol":
        b_rows, b_cols = b.shape[-2], N_DEV * b.shape[-1]
    else:
        b_rows, b_cols = N_DEV * b.shape[-2], b.shape[-1]
    if mode == "nn":
        (M, K), (K2, N) = a.shape, (b_rows, b_cols)
    elif mode == "nt":
        (M, K), (N, K2) = a.shape, (b_rows, b_cols)
    else:
        (K, M), (K2, N) = a.shape, (b_rows, b_cols)
    assert K == K2, (name, a.shape, b.shape, mode)
    tm, tn, tk = min(tm, M), min(tn, N), min(tk, K)
    cols_are_n = mode != "nt"
    if b_kind == "devcol":
        tn, tk = (b.shape[-1], tk) if cols_are_n else (tn, b.shape[-1])
    if b_kind == "devrow":
        tn, tk = (tn, b.shape[-2]) if cols_are_n else (b.shape[-2], tk)
    shard = N // N_DEV
    if o_kind == "devcol":
        tn = max(shard, tn // shard * shard)
    assert M % tm == 0 and N % tn == 0 and K % tk == 0, (name, M, N, K, tm, tn, tk)
    nk = K // tk
    dims = {"nn": NN, "nt": NT, "tn": TN}[mode]
    n_ex, n_out = len(extras) + len(params), len(out_dtypes)

    def body(*refs):
        a_ref, b_ref = refs[:2]
        ex = refs[2:2 + n_ex]
        outs = refs[2 + n_ex:2 + n_ex + n_out]
        acc = refs[-1]
        k = pl.program_id(2)

        @pl.when(k == 0)
        def _():
            acc[...] = jnp.zeros_like(acc)

        acc[...] += _dot(a_ref[...].astype(bf16), b_ref[...].astype(bf16), dims)

        @pl.when(k == nk - 1)
        def _():
            r = acc[...]
            res = epilogue(r, *[e[...] for e in ex]) if epilogue is not None else (r,)
            for o, v in zip(outs, res):
                if o_kind == "plain":
                    o[...] = v.astype(o.dtype)
                else:
                    for q in range(tn // shard):
                        o[q] = v[:, q * shard:(q + 1) * shard].astype(o.dtype)

    if mode == "tn":
        a_spec = pl.BlockSpec((tk, tm), lambda i, j, k: (k, i))
    else:
        a_spec = pl.BlockSpec((tm, tk), lambda i, j, k: (i, k))
    jb, kb = (lambda i, j, k: j), (lambda i, j, k: k)
    if mode == "nt":
        b_spec = _mat_spec(b, b_kind, b_lead, tn, tk, jb, kb)
    else:
        b_spec = _mat_spec(b, b_kind, b_lead, tk, tn, kb, jb)
    e_spec = pl.BlockSpec((tm, tn), lambda i, j, k: (i, j))
    if o_kind == "plain":
        o_spec, o_shape = e_spec, (M, N)
    else:
        o_spec, o_shape = pl.BlockSpec((tn // shard, tm, shard), lambda i, j, k: (j, i, 0)), (N_DEV, M, shard)
    res = pl.pallas_call(
        body, name=name,
        grid=(M // tm, N // tn, nk),
        in_specs=[a_spec, b_spec] + [e_spec] * len(extras)
        + [pl.BlockSpec(p.shape, lambda i, j, k: (0, 0)) for p in params],
        out_specs=[o_spec] * n_out,
        out_shape=[jax.ShapeDtypeStruct(o_shape, dt) for dt in out_dtypes],
        scratch_shapes=[pltpu.VMEM((tm, tn), f32)],
        compiler_params=_cparams("parallel", "parallel", "arbitrary"),
    )(a, b, *extras, *params)
    return res[0] if n_out == 1 else res


def _row_spec(tm, cb, width):
    assert (cb * LANE) % width == 0
    blk = (cb * LANE) // width
    return pl.BlockSpec((tm, width), lambda i: (i, blk))


def _whole_spec(p):
    nd = p.ndim
    return pl.BlockSpec(p.shape, lambda i: (0,) * nd)


def _tl_fwd(name, fn, rows, params, out_widths, out_dtypes, tm=256):
    T = rows[0][0].shape[0]
    tm = min(tm, T)
    nr, npar = len(rows), len(params)

    def body(*refs):
        vals = [r[...] for r in refs[:nr + npar]]
        outs = fn(*vals)
        for o, v in zip(refs[nr + npar:], outs):
            o[...] = v.astype(o.dtype)

    res = pl.pallas_call(
        body, name=name, grid=(T // tm,),
        in_specs=[_row_spec(tm, cb, w) for (_, cb, w) in rows] + [_whole_spec(p) for p in params],
        out_specs=[pl.BlockSpec((tm, w), lambda i: (i, 0)) for w in out_widths],
        out_shape=[jax.ShapeDtypeStruct((T, w), dt) for w, dt in zip(out_widths, out_dtypes)],
        compiler_params=_cparams("parallel"),
    )(*[r[0] for r in rows], *params)
    return res


def _tl_bwd(name, fn, rows, params, cot_rows, cot_fn=None, tm=256):
    T = rows[0][0].shape[0]
    tm = min(tm, T)
    nr, npar, nc = len(rows), len(params), len(cot_rows)

    def body(*refs):
        vals = [r[...] for r in refs[:nr + npar]]
        cots = [r[...] for r in refs[nr + npar:nr + npar + nc]]
        outs = refs[nr + npar + nc:]
        cot = tuple(cot_fn(*cots)) if cot_fn is not None else tuple(cots)
        _, vjp = jax.vjp(fn, *vals)
        grads = vjp(cot)
        for o, g in zip(outs[:nr], grads[:nr]):
            o[...] = g.astype(o.dtype)
        i = pl.program_id(0)
        for o, g in zip(outs[nr:], grads[nr:]):
            @pl.when(i == 0)
            def _(o=o):
                o[...] = jnp.zeros_like(o)
            o[...] += g

    res = pl.pallas_call(
        body, name=name, grid=(T // tm,),
        in_specs=[_row_spec(tm, cb, w) for (_, cb, w) in rows] + [_whole_spec(p) for p in params]
        + [_row_spec(tm, cb, w) for (_, cb, w) in cot_rows],
        out_specs=[pl.BlockSpec((tm, w), lambda i: (i, 0)) for (_, _, w) in rows] + [_whole_spec(p) for p in params],
        out_shape=[jax.ShapeDtypeStruct((T, w), f32) for (_, _, w) in rows]
        + [jax.ShapeDtypeStruct(p.shape, f32) for p in params],
        compiler_params=_cparams("arbitrary"),
    )(*[r[0] for r in rows], *params, *[r[0] for r in cot_rows])
    return res[:nr], res[nr:]


def _ln_res_fn(x, mix, g, b):
    pre = DN_ALPHA * x + mix
    mu = jnp.mean(pre, axis=-1, keepdims=True)
    var = jnp.mean(jnp.square(pre - mu), axis=-1, keepdims=True)
    return ((pre - mu) * lax.rsqrt(var + LN_EPS) * g + b,)


@jax.custom_jvp
def _expm1(x):
    small = jnp.abs(x) < 0.3
    xs = jnp.where(small, x, 0.0)
    poly = xs * (1.0 + xs * (1 / 2 + xs * (1 / 6 + xs * (1 / 24 + xs * (1 / 120 + xs * (
        1 / 720 + xs * (1 / 5040 + xs * (1 / 40320 + xs * (1 / 362880)))))))))
    return jnp.where(small, poly, jnp.exp(x) - 1.0)


@_expm1.defjvp
def _expm1_jvp(primals, tangents):
    (x,), (t,) = primals, tangents
    return _expm1(x), t * jnp.exp(x)


def _rglru_pre_fn(pre_r, pre_i, xc, b_a, b_x, lam):
    r = jax.nn.sigmoid(pre_r + b_a)
    i = jax.nn.sigmoid(pre_i + b_x)
    log_a = -LRU_C * r * jax.nn.softplus(-lam)
    a = jnp.exp(log_a)
    b = jnp.sqrt(-_expm1(2.0 * log_a)) * (i * xc)
    return a, b


def _rec_gate_fn(h, gate):
    return (h * jax.nn.gelu(gate),)


def _loss_head(y, t, tm=256):
    T, Dm = y.shape

    def body(y_ref, t_ref, dy_ref, loss_ref):
        e = y_ref[...] - t_ref[...]
        dy_ref[...] = e * (1.0 / Dm)

        @pl.when(pl.program_id(0) == 0)
        def _():
            loss_ref[...] = jnp.zeros_like(loss_ref)

        loss_ref[...] += 0.5 * jnp.sum(jnp.mean(e * e, axis=-1, keepdims=True), axis=0, keepdims=True)

    dy, loss = pl.pallas_call(
        body, name="loss_head", grid=(T // tm,),
        in_specs=[pl.BlockSpec((tm, Dm), lambda i: (i, 0))] * 2,
        out_specs=[pl.BlockSpec((tm, Dm), lambda i: (i, 0)), pl.BlockSpec((SUBLANE, LANE), lambda i: (0, 0))],
        out_shape=[jax.ShapeDtypeStruct((T, Dm), f32), jax.ShapeDtypeStruct((SUBLANE, LANE), f32)],
        compiler_params=_cparams("arbitrary"),
    )(y, t)
    return loss[0, 0], dy


def _conv_fwd(name, x, cb0, nblk, w, bias, tm=2048):
    T = x.shape[0]
    tm = min(tm, T)
    hb = tm // SUBLANE
    has_b = bias is not None

    def body(*refs):
        cur, prev, w_ref = refs[:3]
        b_ref = refs[3] if has_b else None
        o = refs[-1]
        i = pl.program_id(1)
        p = jnp.where(i > 0, prev[...], 0.0)
        xcat = jnp.concatenate([p, cur[...]], axis=0)
        acc = cur[...] * w_ref[3:4, :]
        for j in range(3):
            acc = acc + pltpu.roll(xcat, 3 - j, axis=0)[SUBLANE:] * w_ref[j:j + 1, :]
        if has_b:
            acc = acc + b_ref[...]
        o[...] = acc

    in_specs = [
        pl.BlockSpec((tm, LANE), lambda c, i: (i, cb0 + c)),
        pl.BlockSpec((SUBLANE, LANE), lambda c, i: (jnp.maximum(i * hb - 1, 0), cb0 + c)),
        pl.BlockSpec((4, LANE), lambda c, i: (0, c)),
    ]
    args = [x, x, w]
    if has_b:
        in_specs.append(pl.BlockSpec((1, LANE), lambda c, i: (0, c)))
        args.append(bias)
    return pl.pallas_call(
        body, name=name, grid=(nblk, T // tm),
        in_specs=in_specs,
        out_specs=pl.BlockSpec((tm, LANE), lambda c, i: (i, c)),
        out_shape=jax.ShapeDtypeStruct((T, nblk * LANE), f32),
        compiler_params=_cparams("parallel", "parallel"),
    )(*args)


def _conv_bwd(name, dy, x, cb0, nblk, w, tm=2048):
    T = x.shape[0]
    tm = min(tm, T)
    hb = tm // SUBLANE
    nt = T // tm

    def body(dcur, dnext, xcur, xprev, w_ref, dx_ref, dw_ref, db_ref):
        i = pl.program_id(1)
        d = dcur[...]
        dn = jnp.where(i < nt - 1, dnext[...], 0.0)
        dcat = jnp.concatenate([d, dn], axis=0)
        acc = d * w_ref[3:4, :]
        for j in range(3):
            s = 3 - j
            acc = acc + pltpu.roll(dcat, tm + SUBLANE - s, axis=0)[:tm] * w_ref[j:j + 1, :]
        dx_ref[...] = acc

        p = jnp.where(i > 0, xprev[...], 0.0)
        xcat = jnp.concatenate([p, xcur[...]], axis=0)
        rows = [jnp.sum(d * pltpu.roll(xcat, 3 - j, axis=0)[SUBLANE:], axis=0, keepdims=True) for j in range(3)]
        rows.append(jnp.sum(d * xcur[...], axis=0, keepdims=True))
        rows.append(jnp.zeros((SUBLANE - 4, LANE), f32))

        @pl.when(i == 0)
        def _():
            dw_ref[...] = jnp.zeros_like(dw_ref)
            db_ref[...] = jnp.zeros_like(db_ref)

        dw_ref[...] += jnp.concatenate(rows, axis=0)
        db_ref[...] += jnp.broadcast_to(jnp.sum(d, axis=0, keepdims=True), (SUBLANE, LANE))

    nh = T // SUBLANE
    dx, dw, db = pl.pallas_call(
        body, name=name, grid=(nblk, nt),
        in_specs=[
            pl.BlockSpec((tm, LANE), lambda c, i: (i, c)),
            pl.BlockSpec((SUBLANE, LANE), lambda c, i: (jnp.minimum((i + 1) * hb, nh - 1), c)),
            pl.BlockSpec((tm, LANE), lambda c, i: (i, cb0 + c)),
            pl.BlockSpec((SUBLANE, LANE), lambda c, i: (jnp.maximum(i * hb - 1, 0), cb0 + c)),
            pl.BlockSpec((4, LANE), lambda c, i: (0, c)),
        ],
        out_specs=[
            pl.BlockSpec((tm, LANE), lambda c, i: (i, c)),
            pl.BlockSpec((SUBLANE, LANE), lambda c, i: (0, c)),
            pl.BlockSpec((SUBLANE, LANE), lambda c, i: (0, c)),
        ],
        out_shape=[jax.ShapeDtypeStruct((T, nblk * LANE), f32),
                   jax.ShapeDtypeStruct((SUBLANE, nblk * LANE), f32),
                   jax.ShapeDtypeStruct((SUBLANE, nblk * LANE), f32)],
        compiler_params=_cparams("parallel", "arbitrary"),
    )(dy, dy, x, x, w)
    return dx, dw[:4], db[0]


@functools.partial(jax.custom_vjp, nondiff_argnums=(1,))
def _lroll(x, s):
    return pltpu.roll(x, s, axis=1)


def _lroll_fwd(x, s):
    return _lroll(x, s), None


def _lroll_bwd(s, _, g):
    return (_lroll(g, (LANE - s) % LANE),)


_lroll.defvjp(_lroll_fwd, _lroll_bwd)


def _rope_tables(T):
    half = A_HEAD_DIM // 2
    inv_freq = ROPE_THETA ** (-jnp.arange(half, dtype=f32) / half)
    ang = jnp.arange(T, dtype=f32)[:, None] * inv_freq[None, :]
    cos, sin = jnp.cos(ang), jnp.sin(ang)
    return jnp.tile(jnp.concatenate([cos, cos], axis=1), (1, 2)), jnp.tile(jnp.concatenate([-sin, sin], axis=1), (1, 2))


def _attn_block_fn(n, q, kp, kc, vp, vc, cq, sq, cp, sp, sinks):
    W = WINDOW
    lane = lax.broadcasted_iota(jnp.int32, (W, LANE), 1)
    lo_half = (lane % A_HEAD_DIM) < (A_HEAD_DIM // 2)
    lane8 = lax.broadcasted_iota(jnp.int32, sinks.shape, 1)

    def rope(x, c, s):
        return x * c + jnp.where(lo_half, _lroll(x, LANE - A_HEAD_DIM // 2), _lroll(x, A_HEAD_DIM // 2)) * s

    k2 = jnp.concatenate([rope(kp, cp, sp), rope(kc, cq, sq)], axis=0).astype(bf16)
    v2 = jnp.concatenate([vp, vc], axis=0).astype(bf16)
    row = lax.broadcasted_iota(jnp.int32, (W, 2 * W), 0)
    col = lax.broadcasted_iota(jnp.int32, (W, 2 * W), 1)
    dist = row + W - col
    mask = (dist >= 0) & (dist < W) & ((col >= W) | (n > 0))
    outs = []
    for t in range(4):
        qt = rope(q[:, LANE * t:LANE * (t + 1)], cq, sq)
        g = t // 2
        ot = jnp.zeros((W, LANE), f32)
        for hh in range(2):
            qa = jnp.where((lane // A_HEAD_DIM) == hh, qt, 0.0)
            if hh != g:
                qa = _lroll(qa, A_HEAD_DIM)
            s = _dot(qa.astype(bf16), k2, NT) * (A_HEAD_DIM ** -0.5)
            s = jnp.where(mask, s, -jnp.inf)
            sink = jnp.sum(jnp.where(lane8 == 2 * t + hh, sinks, 0.0), axis=1, keepdims=True)
            m = jnp.maximum(jnp.max(s, axis=-1, keepdims=True), sink)
            e = jnp.exp(s - m)
            p = e / (jnp.sum(e, axis=-1, keepdims=True) + jnp.exp(sink - m))
            o = _dot(p.astype(bf16), v2, NN)
            o = jnp.where((lane // A_HEAD_DIM) == g, o, 0.0)
            if hh != g:
                o = _lroll(o, A_HEAD_DIM)
            ot = ot + o
        outs.append(ot)
    return jnp.concatenate(outs, axis=1)


def _attn_specs():
    W = WINDOW
    prev = lambda n: jnp.maximum(n - 1, 0)
    return [
        pl.BlockSpec((W, 4 * LANE), lambda n: (n, CB_QA // 4)),
        pl.BlockSpec((W, LANE), lambda n: (prev(n), CB_KA)),
        pl.BlockSpec((W, LANE), lambda n: (n, CB_KA)),
        pl.BlockSpec((W, LANE), lambda n: (prev(n), CB_VA)),
        pl.BlockSpec((W, LANE), lambda n: (n, CB_VA)),
        pl.BlockSpec((W, LANE), lambda n: (n, 0)),
        pl.BlockSpec((W, LANE), lambda n: (n, 0)),
        pl.BlockSpec((W, LANE), lambda n: (prev(n), 0)),
        pl.BlockSpec((W, LANE), lambda n: (prev(n), 0)),
        pl.BlockSpec((1, A_Q_HEADS), lambda n: (0, 0)),
    ]


def _attn_fwd(name, proj, cos, sin, sinks):
    T = proj.shape[0]
    W = WINDOW

    def body(*refs):
        o = refs[-1]
        o[...] = _attn_block_fn(pl.program_id(0), *[r[...] for r in refs[:-1]])

    return pl.pallas_call(
        body, name=name, grid=(T // W,),
        in_specs=_attn_specs(),
        out_specs=pl.BlockSpec((W, 4 * LANE), lambda n: (n, 0)),
        out_shape=jax.ShapeDtypeStruct((T, 4 * LANE), f32),
        compiler_params=_cparams("parallel"),
    )(proj, proj, proj, proj, proj, cos, sin, cos, sin, sinks)


def _attn_bwd(name, proj, cos, sin, sinks, d_oab):
    T = proj.shape[0]
    W = WINDOW

    def body(*refs):
        ins = [r[...] for r in refs[:10]]
        do = refs[10][...]
        dq_ref, dk_ref, dv_ref, ds_ref = refs[11:]
        n = pl.program_id(0)
        _, vjp = jax.vjp(functools.partial(_attn_block_fn, n), *ins)
        dq, dkp, dkc, dvp, dvc, _, _, _, _, dsk = vjp(do)
        dq_ref[...] = dq

        @pl.when(n == 0)
        def _():
            dk_ref[...] = jnp.zeros_like(dk_ref)
            dv_ref[...] = jnp.zeros_like(dv_ref)
            ds_ref[...] = jnp.zeros_like(ds_ref)

        cur = pl.ds(pl.multiple_of(n * W, W), W)
        dk_ref[cur, :] += dkc
        dv_ref[cur, :] += dvc
        ds_ref[...] += dsk

        @pl.when(n > 0)
        def _():
            prv = pl.ds(pl.multiple_of((n - 1) * W, W), W)
            dk_ref[prv, :] += dkp
            dv_ref[prv, :] += dvp

    return pl.pallas_call(
        body, name=name, grid=(T // W,),
        in_specs=_attn_specs() + [pl.BlockSpec((W, 4 * LANE), lambda n: (n, 0))],
        out_specs=[pl.BlockSpec((W, 4 * LANE), lambda n: (n, 0)),
                   pl.BlockSpec((T, LANE), lambda n: (0, 0)),
                   pl.BlockSpec((T, LANE), lambda n: (0, 0)),
                   pl.BlockSpec((1, A_Q_HEADS), lambda n: (0, 0))],
        out_shape=[jax.ShapeDtypeStruct((T, 4 * LANE), f32), jax.ShapeDtypeStruct((T, LANE), f32),
                   jax.ShapeDtypeStruct((T, LANE), f32), jax.ShapeDtypeStruct((1, A_Q_HEADS), f32)],
        compiler_params=_cparams("arbitrary"),
    )(proj, proj, proj, proj, proj, cos, sin, cos, sin, sinks, d_oab)


def _bdot(spec, a, b, precision=None):
    return jnp.einsum(spec, a, b, preferred_element_type=f32, precision=precision)


@jax.custom_vjp
def _tri_inv(a):
    C = a.shape[-1]
    r = lax.broadcasted_iota(jnp.int32, (C, C), 0)
    c = lax.broadcasted_iota(jnp.int32, (C, C), 1)
    t = jnp.broadcast_to(jnp.where(r == c, 1.0, 0.0).astype(f32), a.shape)
    for j in range(C - 1):
        t = t - a[:, :, j:j + 1] * t[:, j:j + 1, :]
    return t


def _tri_inv_fwd(a):
    t = _tri_inv(a)
    return t, t


def _tri_inv_bwd(t, g):
    C = t.shape[-1]
    r = lax.broadcasted_iota(jnp.int32, (C, C), 0)
    c = lax.broadcasted_iota(jnp.int32, (C, C), 1)
    x = _bdot("hki,hkj->hij", t, g, precision=lax.Precision.HIGHEST)
    y = _bdot("hik,hjk->hij", x, t, precision=lax.Precision.HIGHEST)
    return (jnp.where(r > c, -y, 0.0),)


_tri_inv.defvjp(_tri_inv_fwd, _tri_inv_bwd)


@jax.custom_vjp
def _tri_inv_saved(a, t):
    return t


_tri_inv_saved.defvjp(lambda a, t: (t, t), lambda t, g: (_tri_inv_bwd(t, g)[0], jnp.zeros_like(t)))


def _silu(x):
    return x * jax.nn.sigmoid(x)


def _l2n(x):
    return x * lax.rsqrt(jnp.sum(x * x, axis=-1, keepdims=True) + NORM_EPS)


def _delta_chunk_fn(cq, ck, cv, z, lg, a_log, dt_bias, norm_w, S, t_saved=None, want_t=False):
    C = B_CHUNK
    lane = lax.broadcasted_iota(jnp.int32, (C, LANE), 1)
    pick = lambda l0: jnp.concatenate(
        [jnp.sum(jnp.where(lane == l0 + h, lg, 0.0), axis=1, keepdims=True)[None] for h in range(B_HEADS)], axis=0)
    bl, al = pick(0), pick(B_HEADS)
    q = _l2n(_silu(cq)) * (B_HEAD_DIM ** -0.5)
    k = _l2n(_silu(ck))
    v = _silu(cv)
    beta = jax.nn.sigmoid(bl)
    g = -jnp.exp(a_log) * jax.nn.softplus(al + dt_bias)
    r = lax.broadcasted_iota(jnp.int32, (C, C), 0)
    c = lax.broadcasted_iota(jnp.int32, (C, C), 1)
    eye = r == c
    g_row = jnp.sum(jnp.where(eye, g, 0.0), axis=1, keepdims=True)
    gc = jnp.sum(jnp.where(c <= r, g_row, 0.0), axis=2, keepdims=True)
    gc_row = jnp.sum(jnp.where(eye, gc, 0.0), axis=1, keepdims=True)
    decay_incl = jnp.exp(jnp.where(r >= c, gc - gc_row, -jnp.inf))
    decay_strict = jnp.where(r > c, decay_incl, 0.0)
    kb = k * beta
    vb = v * beta
    kbf = k.astype(bf16)
    a_mat = _bdot("hik,hjk->hij", kb.astype(bf16), kbf) * decay_strict
    t_f32 = _tri_inv(a_mat) if t_saved is None else _tri_inv_saved(a_mat, t_saved)
    t_mat = t_f32.astype(bf16)
    eg = jnp.exp(gc)
    u = _bdot("hij,hjv->hiv", t_mat, vb.astype(bf16))
    w = _bdot("hij,hjk->hik", t_mat, (kb * eg).astype(bf16))
    qk = _bdot("hik,hjk->hij", q.astype(bf16), kbf) * decay_incl
    g_last = jnp.sum(g, axis=1, keepdims=True)
    k_tail = k * jnp.exp(g_last - gc)
    Sb = S.astype(bf16)
    v_new = u - _bdot("hck,hkv->hcv", w.astype(bf16), Sb)
    o = _bdot("hck,hkv->hcv", (q * eg).astype(bf16), Sb) + _bdot("hij,hjv->hiv", qk.astype(bf16), v_new.astype(bf16))
    S_new = S * jnp.exp(g_last) + _bdot("hck,hcv->hkv", k_tail.astype(bf16), v_new.astype(bf16))
    ob = o * lax.rsqrt(jnp.mean(o * o, axis=-1, keepdims=True) + NORM_EPS) * norm_w
    return (ob * _silu(z), S_new) + ((t_f32,) if want_t else ())


def _delta_in_specs(rev, N):
    C = B_CHUNK
    ix = (lambda n: N - 1 - n) if rev else (lambda n: n)
    specs = [pl.BlockSpec((C, 3 * B_HEADS * LANE), lambda n: (ix(n), 0))]
    specs += [pl.BlockSpec((C, LANE), lambda n, h=h: (ix(n), CB_Z + h)) for h in range(B_HEADS)]
    specs += [
        pl.BlockSpec((C, LANE), lambda n: (ix(n), CB_LG)),
        pl.BlockSpec((B_HEADS, 1, 1), lambda n: (0, 0, 0)),
        pl.BlockSpec((B_HEADS, 1, 1), lambda n: (0, 0, 0)),
        pl.BlockSpec((1, LANE), lambda n: (0, 0)),
    ]
    return specs


def _delta_inputs(c_ref, z_refs, lg, al, dt, nw):
    H = B_HEADS
    part = lambda p: jnp.stack([c_ref[:, LANE * (p * H + h):LANE * (p * H + h + 1)] for h in range(H)])
    return (part(0), part(1), part(2), jnp.stack([z[...] for z in z_refs]), lg[...], al[...], dt[...], nw[...])


def _delta_fwd(name, c, proj, a_log, dt_bias, norm_w):
    T = c.shape[0]
    C = B_CHUNK
    N = T // C
    Dh = B_HEAD_DIM
    H = B_HEADS

    def body(*refs):
        c_ref, z_refs, (lg, al, dt, nw) = refs[0], refs[1:1 + H], refs[1 + H:5 + H]
        o_ref, s_ref, t_ref, S = refs[5 + H:]

        @pl.when(pl.program_id(0) == 0)
        def _():
            S[...] = jnp.zeros_like(S)

        s0 = S[...]
        s_ref[...] = s0
        ob, s1, t = _delta_chunk_fn(*_delta_inputs(c_ref, z_refs, lg, al, dt, nw), s0, want_t=True)
        for h in range(H):
            o_ref[:, LANE * h:LANE * (h + 1)] = ob[h]
        t_ref[...] = t
        S[...] = s1

    return pl.pallas_call(
        body, name=name, grid=(N,),
        in_specs=_delta_in_specs(False, N),
        out_specs=[pl.BlockSpec((C, H * LANE), lambda n: (n, 0)),
                   pl.BlockSpec((H, None, Dh, Dh), lambda n: (0, n, 0, 0)),
                   pl.BlockSpec((H, None, C, C), lambda n: (0, n, 0, 0))],
        out_shape=[jax.ShapeDtypeStruct((T, H * Dh), f32), jax.ShapeDtypeStruct((H, N, Dh, Dh), f32),
                   jax.ShapeDtypeStruct((H, N, C, C), f32)],
        scratch_shapes=[pltpu.VMEM((H, Dh, Dh), f32)],
        compiler_params=_cparams("arbitrary"),
    )(c, *([proj] * H), proj, a_log, dt_bias, norm_w)


def _delta_bwd(name, c, proj, a_log, dt_bias, norm_w, s_saved, t_saved, d_oab):
    T = c.shape[0]
    C = B_CHUNK
    N = T // C
    Dh = B_HEAD_DIM
    H = B_HEADS

    def body(*refs):
        c_ref, z_refs, (lg, al, dt, nw) = refs[0], refs[1:1 + H], refs[1 + H:5 + H]
        s_ref, t_ref, do_ref = refs[5 + H:8 + H]
        dc, dz, dlg, dal, ddt, dnw, dS = refs[8 + H:]

        @pl.when(pl.program_id(0) == 0)
        def _():
            dS[...] = jnp.zeros_like(dS)
            dal[...] = jnp.zeros_like(dal)
            ddt[...] = jnp.zeros_like(ddt)
            dnw[...] = jnp.zeros_like(dnw)

        _, vjp = jax.vjp(functools.partial(_delta_chunk_fn, t_saved=t_ref[...]),
                         *_delta_inputs(c_ref, z_refs, lg, al, dt, nw), s_ref[...])
        do = jnp.stack([do_ref[:, LANE * h:LANE * (h + 1)] for h in range(H)])
        g = vjp((do, dS[...]))
        for h in range(H):
            for p in range(3):
                dc[:, LANE * (p * H + h):LANE * (p * H + h + 1)] = g[p][h]
            dz[:, LANE * h:LANE * (h + 1)] = g[3][h]
        dlg[...] = g[4]
        dal[...] += g[5]
        ddt[...] += g[6]
        dnw[...] += g[7]
        dS[...] = g[8]

    rn = lambda n: N - 1 - n
    return pl.pallas_call(
        body, name=name, grid=(N,),
        in_specs=_delta_in_specs(True, N) + [
            pl.BlockSpec((H, None, Dh, Dh), lambda n: (0, rn(n), 0, 0)),
            pl.BlockSpec((H, None, C, C), lambda n: (0, rn(n), 0, 0)),
            pl.BlockSpec((C, H * LANE), lambda n: (rn(n), 1)),
        ],
        out_specs=[
            pl.BlockSpec((C, 3 * H * LANE), lambda n: (rn(n), 0)),
            pl.BlockSpec((C, H * LANE), lambda n: (rn(n), 0)),
            pl.BlockSpec((C, LANE), lambda n: (rn(n), 0)),
            pl.BlockSpec((H, 1, 1), lambda n: (0, 0, 0)),
            pl.BlockSpec((H, 1, 1), lambda n: (0, 0, 0)),
            pl.BlockSpec((1, LANE), lambda n: (0, 0)),
        ],
        out_shape=[jax.ShapeDtypeStruct((T, 3 * H * Dh), f32), jax.ShapeDtypeStruct((T, H * Dh), f32),
                   jax.ShapeDtypeStruct((T, LANE), f32), jax.ShapeDtypeStruct((H, 1, 1), f32),
                   jax.ShapeDtypeStruct((H, 1, 1), f32), jax.ShapeDtypeStruct((1, LANE), f32)],
        scratch_shapes=[pltpu.VMEM((H, Dh, Dh), f32)],
        compiler_params=_cparams("arbitrary"),
    )(c, *([proj] * H), proj, a_log, dt_bias, norm_w, s_saved, t_saved, d_oab)


def _blockdiag_fwd(name, xc, w_a, w_x, tm=512):
    T, Wd = xc.shape
    bw = Wd // LRU_BLOCKS
    tm = min(tm, T)

    def body(x_ref, wa_ref, wx_ref, oa, ox):
        xb = x_ref[...].astype(bf16)
        oa[...] = _dot(xb, wa_ref[...].astype(bf16), NN)
        ox[...] = _dot(xb, wx_ref[...].astype(bf16), NN)

    xs = pl.BlockSpec((tm, bw), lambda i, h: (i, h))
    ws = pl.BlockSpec((None, bw, bw), lambda i, h: (h, 0, 0))
    return pl.pallas_call(
        body, name=name, grid=(T // tm, LRU_BLOCKS), in_specs=[xs, ws, ws], out_specs=[xs, xs],
        out_shape=[jax.ShapeDtypeStruct((T, Wd), f32)] * 2,
        compiler_params=_cparams("parallel", "parallel"),
    )(xc, w_a, w_x)


def _blockdiag_bwd_dx(name, dpr, dpi, w_a, w_x, addend, tm=512):
    T, Wd = dpr.shape
    bw = Wd // LRU_BLOCKS
    tm = min(tm, T)

    def body(dr, di, wa_ref, wx_ref, add, o):
        o[...] = (add[...] + _dot(dr[...].astype(bf16), wa_ref[...].astype(bf16), NT)
                  + _dot(di[...].astype(bf16), wx_ref[...].astype(bf16), NT))

    xs = pl.BlockSpec((tm, bw), lambda i, h: (i, h))
    ws = pl.BlockSpec((None, bw, bw), lambda i, h: (h, 0, 0))
    return pl.pallas_call(
        body, name=name, grid=(T // tm, LRU_BLOCKS), in_specs=[xs, xs, ws, ws, xs], out_specs=xs,
        out_shape=jax.ShapeDtypeStruct((T, Wd), f32),
        compiler_params=_cparams("parallel", "parallel"),
    )(dpr, dpi, w_a, w_x, addend)


def _blockdiag_bwd_dw(name, xc, dpr, dpi, tk=512):
    T, Wd = xc.shape
    bw = Wd // LRU_BLOCKS
    tk = min(tk, T)

    def body(x_ref, dr, di, oa, ox):
        @pl.when(pl.program_id(1) == 0)
        def _():
            oa[...] = jnp.zeros_like(oa)
            ox[...] = jnp.zeros_like(ox)

        xb = x_ref[...].astype(bf16)
        oa[...] += _dot(xb, dr[...].astype(bf16), TN)
        ox[...] += _dot(xb, di[...].astype(bf16), TN)

    xs = pl.BlockSpec((tk, bw), lambda h, k: (k, h))
    ws = pl.BlockSpec((None, bw, bw), lambda h, k: (h, 0, 0))
    return pl.pallas_call(
        body, name=name, grid=(LRU_BLOCKS, T // tk), in_specs=[xs, xs, xs], out_specs=[ws, ws],
        out_shape=[jax.ShapeDtypeStruct((LRU_BLOCKS, bw, bw), f32)] * 2,
        compiler_params=_cparams("parallel", "arbitrary"),
    )(xc, dpr, dpi)


def _scan(name, a, b, reverse, tt=512, cb=512):
    T, Wd = a.shape
    tt, cb = min(tt, T), min(cb, Wd)
    nt = T // tt
    ng = tt // SUBLANE

    def body(a_ref, b_ref, o_ref, carry):
        @pl.when(pl.program_id(1) == 0)
        def _():
            carry[...] = jnp.zeros_like(carry)

        row = lax.broadcasted_iota(jnp.int32, (SUBLANE, cb), 0)

        def step(gi, hp):
            g = (ng - 1 - gi) if reverse else gi
            off = pl.multiple_of(g * SUBLANE, SUBLANE)
            A = a_ref[pl.ds(off, SUBLANE), :]
            B = b_ref[pl.ds(off, SUBLANE), :]
            for s in (1, 2, 4):
                sh = (SUBLANE - s) if reverse else s
                As = pltpu.roll(A, sh, axis=0)
                Bs = pltpu.roll(B, sh, axis=0)
                valid = (row < SUBLANE - s) if reverse else (row >= s)
                B = jnp.where(valid, A * Bs + B, B)
                A = jnp.where(valid, A * As, A)
            hcur = A * hp + B
            o_ref[pl.ds(off, SUBLANE), :] = hcur
            edge = hcur[0:1, :] if reverse else hcur[SUBLANE - 1:SUBLANE, :]
            return jnp.broadcast_to(edge, (SUBLANE, cb))

        carry[...] = lax.fori_loop(0, ng, step, carry[...])

    ti = (lambda c, i: (nt - 1 - i, c)) if reverse else (lambda c, i: (i, c))
    spec = pl.BlockSpec((tt, cb), ti)
    return pl.pallas_call(
        body, name=name, grid=(Wd // cb, nt), in_specs=[spec, spec], out_specs=spec,
        out_shape=jax.ShapeDtypeStruct((T, Wd), f32),
        scratch_shapes=[pltpu.VMEM((SUBLANE, cb), f32)],
        compiler_params=_cparams("parallel", "arbitrary"),
    )(a, b)


def _relu2_epilogue(r):
    h = jnp.maximum(r, 0.0)
    return r, h * h


def _drelu2_epilogue(r, a):
    return (r * (2.0 * jnp.maximum(a, 0.0)),)


def _add_epilogue(r, e):
    return (r + e,)


def _merge_cols(name, g, tm=256):
    _, L, R, s = g.shape

    def body(g_ref, o_ref):
        for d in range(N_DEV):
            o_ref[:, s * d:s * (d + 1)] = g_ref[d].astype(bf16)
        o_ref[:, N_DEV * s:] = jnp.zeros((tm, HYB_PROJ_PAD - N_DEV * s), bf16)

    return pl.pallas_call(
        body, name=name, grid=(L, R // tm),
        in_specs=[pl.BlockSpec((N_DEV, None, tm, s), lambda l, i: (0, l, i, 0))],
        out_specs=pl.BlockSpec((None, tm, HYB_PROJ_PAD), lambda l, i: (l, i, 0)),
        out_shape=jax.ShapeDtypeStruct((L, R, HYB_PROJ_PAD), bf16),
        compiler_params=_cparams("parallel", "parallel"),
    )(g)


def _split_cols(name, dw, tm=256):
    R = dw.shape[0]
    s = HYB_PROJ // N_DEV

    def body(g_ref, o_ref):
        for d in range(N_DEV):
            o_ref[d] = g_ref[:, s * d:s * (d + 1)].astype(bf16)

    return pl.pallas_call(
        body, name=name, grid=(R // tm,),
        in_specs=[pl.BlockSpec((tm, HYB_PROJ_PAD), lambda i: (i, 0))],
        out_specs=pl.BlockSpec((N_DEV, tm, s), lambda i: (0, i, 0)),
        out_shape=jax.ShapeDtypeStruct((N_DEV, R, s), bf16),
        compiler_params=_cparams("parallel"),
    )(dw)


def _rows_to_dev(dw):
    nb, r, c = dw.shape
    t = dw.reshape(nb, N_DEV, r // N_DEV, c)
    return jnp.moveaxis(t, 1, 0).reshape(N_DEV, nb * (r // N_DEV), c).astype(bf16)


def _ln_epilogue(r, x, g, b):
    return r, _ln_res_fn(x, r, g, b)[0]


def _hybrid_fwd(tag, x, W, j, cos, sin, ln, before_out):
    proj = _mm(f"{tag}_proj", x, W["hyb_w_in"][j], "nn", b_kind="lead", b_lead=0)
    o_a = _attn_fwd(f"{tag}_attn", proj, cos, sin, W["hyb_sinks"][j][None, :])
    c = _conv_fwd(f"{tag}_conv", proj, CB_CONV, 12, W["hyb_conv_w"][j], None)
    o_b, s_saved, t_saved = _delta_fwd(f"{tag}_delta", c, proj, W["hyb_a_log"][j].reshape(B_HEADS, 1, 1),
                                       W["hyb_dt_bias"][j].reshape(B_HEADS, 1, 1), W["hyb_norm_w"][j][None, :])
    o_ab = jnp.concatenate([o_a, o_b], axis=1)
    before_out(o_ab)
    mix, x1 = _mm(f"{tag}_out", o_ab, W["hyb_w_out"][j], "nn", b_kind="lead", b_lead=0, epilogue=_ln_epilogue,
                  extras=(x,), params=ln, out_dtypes=(f32, f32), tm=512)
    return mix, x1, (proj, c, s_saved, t_saved, o_ab)


def _hybrid_bwd(tag, x, dmix, addend, W, j, cos, sin, saved, G, send_early):
    proj, c, s_saved, t_saved, o_ab = saved
    T = x.shape[0]
    d_oab = _mm(f"{tag}_dout", dmix, W["hyb_w_out"][j], "nt", b_kind="lead", b_lead=0)
    G["hyb_w_out"][j] = _mm(f"{tag}_dwout", o_ab, dmix, "tn", out_dtypes=(bf16,)).reshape(N_DEV, -1, D_MODEL)
    sinks = W["hyb_sinks"][j][None, :] + send_early({("hyb_w_out", j): G["hyb_w_out"][j]})
    dq, dk, dv, dsinks = _attn_bwd(f"{tag}_dattn", proj, cos, sin, sinks, d_oab)
    a_log = W["hyb_a_log"][j].reshape(B_HEADS, 1, 1)
    dt_bias = W["hyb_dt_bias"][j].reshape(B_HEADS, 1, 1)
    dc, dz, dlg, dal, ddt, dnw = _delta_bwd(f"{tag}_ddelta", c, proj, a_log, dt_bias, W["hyb_norm_w"][j][None, :],
                                            s_saved, t_saved, d_oab)
    dconv_in, dconv_w, _ = _conv_bwd(f"{tag}_dconv", dc, proj, CB_CONV, 12, W["hyb_conv_w"][j])
    dproj = jnp.concatenate([dq, dk, dv, dconv_in, dz, dlg,
                             jnp.zeros((T, HYB_PROJ_PAD - (CB_LG + 1) * LANE), f32)], axis=1)
    dx = _mm(f"{tag}_dx", dproj, W["hyb_w_in"][j], "nt", b_kind="lead", b_lead=0, epilogue=_add_epilogue,
             extras=(addend,))
    G["hyb_w_in"][j] = _split_cols(f"{tag}_dwin_split", _mm(f"{tag}_dwin", x, dproj, "tn"))
    G["hyb_sinks"][j] = dsinks[0]
    G["hyb_conv_w"][j] = dconv_w
    G["hyb_a_log"][j] = dal.reshape(B_HEADS)
    G["hyb_dt_bias"][j] = ddt.reshape(B_HEADS)
    G["hyb_norm_w"][j] = dnw[0]
    return dx


def _rec_fwd(tag, x, W, j, ln, before_out):
    Wd = D_MODEL
    proj = _mm(f"{tag}_proj", x, W["rec_w_in"][j], "nn", b_kind="devcol", b_lead=0)
    xc = _conv_fwd(f"{tag}_conv", proj, 0, Wd // LANE, W["rec_conv_w"][j], W["rec_conv_b"][j][None, :])
    pre_r, pre_i = _blockdiag_fwd(f"{tag}_gates", xc, W["rec_w_a"][j][0], W["rec_w_x"][j][0])
    pars = [W["rec_b_a"][j][None, :], W["rec_b_x"][j][None, :], W["rec_lambda"][j][None, :]]
    a, b = _tl_fwd(f"{tag}_pre", _rglru_pre_fn, [(pre_r, 0, Wd), (pre_i, 0, Wd), (xc, 0, Wd)], pars, [Wd, Wd], [f32, f32])
    h = _scan(f"{tag}_scan", a, b, False)
    (hg,) = _tl_fwd(f"{tag}_gate", _rec_gate_fn, [(h, 0, Wd), (proj, Wd // LANE, Wd)], [], [Wd], [f32])
    before_out(hg)
    mix, x1 = _mm(f"{tag}_out", hg, W["rec_w_out"][j], "nn", b_kind="lead", b_lead=0, epilogue=_ln_epilogue,
                  extras=(x,), params=ln, out_dtypes=(f32, f32), tm=512)
    return mix, x1, (proj, xc, pre_r, pre_i, a, h, hg)


def _rec_bwd(tag, x, dmix, addend, W, j, saved, G, send_early):
    proj, xc, pre_r, pre_i, a, h, hg = saved
    Wd = D_MODEL
    dhg = _mm(f"{tag}_dout", dmix, W["rec_w_out"][j], "nt", b_kind="lead", b_lead=0)
    G["rec_w_out"][j] = _mm(f"{tag}_dwout", hg, dmix, "tn", out_dtypes=(bf16,)).reshape(N_DEV, -1, D_MODEL)
    sent = send_early({("rec_w_out", j): G["rec_w_out"][j]})
    (dh, dgate), _ = _tl_bwd(f"{tag}_dgate", _rec_gate_fn, [(h, 0, Wd), (proj, Wd // LANE, Wd)], [], [(dhg, 0, Wd)])
    a_next = jnp.concatenate([a[1:], jnp.zeros((1, Wd), f32)], axis=0)
    h_prev = jnp.concatenate([jnp.zeros((1, Wd), f32), h[:-1]], axis=0)
    lam_t = _scan(f"{tag}_dscan", a_next, dh, True)
    pars = [W["rec_b_a"][j][None, :] + sent, W["rec_b_x"][j][None, :], W["rec_lambda"][j][None, :]]
    (dpr, dpi, dxc1), (db_a, db_x, dlam) = _tl_bwd(
        f"{tag}_dpre", _rglru_pre_fn, [(pre_r, 0, Wd), (pre_i, 0, Wd), (xc, 0, Wd)], pars,
        [(lam_t, 0, Wd), (h_prev, 0, Wd)], cot_fn=lambda lt, hp: (lt * hp, lt))
    dxc = _blockdiag_bwd_dx(f"{tag}_dgates_dx", dpr, dpi, W["rec_w_a"][j][0], W["rec_w_x"][j][0], dxc1)
    dwa, dwx = _blockdiag_bwd_dw(f"{tag}_dgates_dw", xc, dpr, dpi)
    G["rec_w_a"][j], G["rec_w_x"][j] = _rows_to_dev(dwa), _rows_to_dev(dwx)
    dxr, dconv_w, dconv_b = _conv_bwd(f"{tag}_dconv", dxc, proj, 0, Wd // LANE, W["rec_conv_w"][j])
    dproj = jnp.concatenate([dxr, dgate], axis=1)
    dx = _mm(f"{tag}_dx", dproj, W["rec_w_in"][j], "nt", b_kind="devcol", b_lead=0, epilogue=_add_epilogue,
             extras=(addend,))
    G["rec_w_in"][j] = _mm(f"{tag}_dwin", x, dproj, "tn", o_kind="devcol", out_dtypes=(bf16,))
    G["rec_conv_w"][j] = dconv_w
    G["rec_conv_b"][j] = dconv_b
    G["rec_b_a"][j] = db_a[0]
    G["rec_b_x"][j] = db_x[0]
    G["rec_lambda"][j] = dlam[0]
    return dx


def _local_step(x, target, W, load_layer, grads_ready):
    T = x.shape[0]
    cos, sin = _rope_tables(T)
    saved = []
    for layer in range(DEPTH):
        j = layer // 2
        tag = f"L{layer}"
        load_layer(layer, 0, x)
        ln1 = (W["ln1_g"][layer][None, :], W["ln1_b"][layer][None, :])
        before_out = functools.partial(load_layer, layer, 1)
        if layer % 2 == 0:
            mix, x1, sv = _hybrid_fwd(tag, x, W, j, cos, sin, ln1, before_out)
        else:
            mix, x1, sv = _rec_fwd(tag, x, W, j, ln1, before_out)
        load_layer(layer, 2, x1)
        a, h2 = _mm(f"{tag}_mlp1", x1, W["mlp_w1"][layer], "nn", b_kind="devcol", b_lead=0, epilogue=_relu2_epilogue,
                    out_dtypes=(f32, bf16))
        ln2 = (W["ln2_g"][layer][None, :], W["ln2_b"][layer][None, :])
        y, x2 = _mm(f"{tag}_mlp2", h2, W["mlp_w2"][layer], "nn", b_kind="devrow", b_lead=0, epilogue=_ln_epilogue,
                    extras=(x1,), params=ln2, out_dtypes=(f32, f32))
        saved.append((x, sv, mix, x1, a, h2, y))
        x = x2
    loss, dx = _loss_head(x, target)

    G = {k: [None] * (DEPTH if k.startswith(("ln", "mlp")) else DEPTH // 2) for k in (
        "hyb_w_in", "hyb_sinks", "hyb_conv_w", "hyb_a_log", "hyb_dt_bias", "hyb_norm_w", "hyb_w_out",
        "rec_w_in", "rec_conv_w", "rec_conv_b", "rec_w_a", "rec_b_a", "rec_w_x", "rec_b_x", "rec_lambda", "rec_w_out",
        "ln1_g", "ln1_b", "mlp_w1", "mlp_w2", "ln2_g", "ln2_b")}
    order = jnp.zeros((1, 1), f32)
    for layer in reversed(range(DEPTH)):
        j = layer // 2
        tag = f"L{layer}"
        x0, sv, mix, x1, a, h2, y = saved[layer]
        ln2 = [W["ln2_g"][layer][None, :] + order, W["ln2_b"][layer][None, :]]
        (dx1_a, dy), (dg2, db2) = _tl_bwd(f"{tag}_dln2", _ln_res_fn, [(x1, 0, D_MODEL), (y, 0, D_MODEL)], ln2,
                                          [(dx, 0, D_MODEL)])
        G["ln2_g"][layer], G["ln2_b"][layer] = dg2[0], db2[0]
        da = _mm(f"{tag}_dmlp2", dy, W["mlp_w2"][layer], "nt", b_kind="devrow", b_lead=0, epilogue=_drelu2_epilogue,
                 extras=(a,), out_dtypes=(bf16,))
        G["mlp_w2"][layer] = _mm(f"{tag}_dw2", h2, dy, "tn", out_dtypes=(bf16,)).reshape(N_DEV, -1, D_MODEL)
        dx1 = _mm(f"{tag}_dmlp1", da, W["mlp_w1"][layer], "nt", b_kind="devcol", b_lead=0, epilogue=_add_epilogue,
                  extras=(dx1_a,))
        G["mlp_w1"][layer] = _mm(f"{tag}_dw1", x1, da, "tn", o_kind="devcol", out_dtypes=(bf16,))
        ln1 = [W["ln1_g"][layer][None, :], W["ln1_b"][layer][None, :]]
        (dx0_a, dmix), (dg1, db1) = _tl_bwd(f"{tag}_dln1", _ln_res_fn, [(x0, 0, D_MODEL), (mix, 0, D_MODEL)], ln1,
                                            [(dx1, 0, D_MODEL)])
        G["ln1_g"][layer], G["ln1_b"][layer] = dg1[0], db1[0]
        early = functools.partial(grads_ready, f"l{layer}_early",
                                  {(k, layer): G[k][layer] for k in ("mlp_w1", "mlp_w2")})
        if layer % 2 == 0:
            dx = _hybrid_bwd(tag, x0, dmix, dx0_a, W, j, cos, sin, sv, G, early)
        else:
            dx = _rec_bwd(tag, x0, dmix, dx0_a, W, j, sv, G, early)
        order = grads_ready(f"l{layer}_late", {}, {(k, i): G[k][i] for k, i in _layer_weights(layer)[:-2]
                                                  if not k.endswith("w_out")})
    big = {k for k, _ in BIG}
    return loss, dx, {k: jnp.stack(v) for k, v in G.items() if k not in big}


def _layer_weights(layer):
    j = layer // 2
    mixer = ["hyb_w_in", "hyb_w_out"] if layer % 2 == 0 else ["rec_w_in", "rec_w_out", "rec_w_a", "rec_w_x"]
    return [(k, j) for k in mixer] + [("mlp_w1", layer), ("mlp_w2", layer)]


def _my_coords():
    return lax.axis_index("x"), lax.axis_index("y"), lax.axis_index("c")


def _all_gather(name, arrays):
    na = len(arrays)

    def body(*refs):
        x_refs, out_refs = refs[:na], refs[na:2 * na]
        send_sems, recv_sems, local_sems = refs[2 * na:]
        x, y, c = _my_coords()
        me, sibling = (x, y, c), (x, y, 1 - c)
        chips = [(1 - x, y), (x, 1 - y), (1 - x, 1 - y)]

        def blk(a, px, py, pc):
            return out_refs[a].at[4 * px + 2 * py + pc]

        def copy(a, k, block, to, src=None):
            return pltpu.make_async_remote_copy(
                src_ref=blk(a, *block) if src is None else src, dst_ref=blk(a, *block),
                send_sem=send_sems.at[a, k], recv_sem=recv_sems.at[a, k],
                device_id=to, device_id_type=pl.DeviceIdType.MESH)

        mine = [pltpu.make_async_copy(x_refs[a], blk(a, *me), local_sems.at[a]) for a in range(na)]
        for cp in mine:
            cp.start()
        first = []
        for a in range(na):
            first.append(copy(a, 0, me, sibling, src=x_refs[a]))
            first += [copy(a, 1 + j, me, (*chip, c), src=x_refs[a]) for j, chip in enumerate(chips)]
        for cp in first:
            cp.start()
        passed = []
        for a in range(na):
            for j, chip in enumerate(chips):
                copy(a, 1 + j, (*chip, c), me).wait_recv()
                passed.append(copy(a, 4 + j, (*chip, c), sibling))
                passed[-1].start()
        for a in range(na):
            copy(a, 0, sibling, me).wait_recv()
            for j, chip in enumerate(chips):
                copy(a, 4 + j, (*chip, 1 - c), me).wait_recv()
        for cp in first + passed:
            cp.wait_send()
        for cp in mine:
            cp.wait()

    return pl.pallas_call(
        body, name=name,
        out_shape=[jax.ShapeDtypeStruct((N_DEV,) + a.shape, a.dtype) for a in arrays],
        in_specs=[pl.BlockSpec(memory_space=pl.ANY)] * na,
        out_specs=[pl.BlockSpec(memory_space=pl.ANY)] * na,
        scratch_shapes=[pltpu.SemaphoreType.DMA((na, 7)), pltpu.SemaphoreType.DMA((na, 7)),
                        pltpu.SemaphoreType.DMA((na,))],
    )(*arrays)


_HBM = pl.BlockSpec(memory_space=pltpu.HBM)
_SEM = pl.BlockSpec(memory_space=pltpu.SEMAPHORE)


def _flip(k, x, y, c):
    return ((1 - x) if k & 4 else x, (1 - y) if k & 2 else y, (1 - c) if k & 1 else c)


_PEERS = {"gather": (1, 2, 4, 6), "scatter": (1, 2, 3, 4, 5, 6, 7)}


def _push_copies(kind, x_refs, land_refs, send_sems, recv_sems, local_sems):
    x, y, c = _my_coords()
    me = 4 * x + 2 * y + c
    peers = _PEERS[kind]
    remote, local = [], []
    for a in range(len(x_refs)):
        local.append(pltpu.make_async_copy(x_refs[a] if kind == "gather" else x_refs[a].at[me], land_refs[a].at[me],
                                           local_sems.at[a]))
        for n, k in enumerate(peers):
            px, py, pc = _flip(k, x, y, c)
            remote.append(pltpu.make_async_remote_copy(
                src_ref=x_refs[a] if kind == "gather" else x_refs[a].at[4 * px + 2 * py + pc],
                dst_ref=land_refs[a].at[me],
                send_sem=send_sems.at[a * len(peers) + n], recv_sem=recv_sems.at[a * len(peers) + n],
                device_id=(px, py, pc), device_id_type=pl.DeviceIdType.MESH))
    return remote, local


def _pass_to_sibling(name, lands):
    na = len(lands)
    chips = (2, 4, 6)

    def body(*refs):
        out_refs, send_sems, recv_sems = refs[na:2 * na], refs[2 * na], refs[2 * na + 1]
        x, y, c = _my_coords()
        cps = []
        for a in range(na):
            for n, k in enumerate(chips):
                px, py, _ = _flip(k, x, y, c)
                cps.append(pltpu.make_async_remote_copy(
                    src_ref=out_refs[a].at[4 * px + 2 * py + c], dst_ref=out_refs[a].at[4 * px + 2 * py + c],
                    send_sem=send_sems.at[a * 3 + n], recv_sem=recv_sems.at[a * 3 + n],
                    device_id=(x, y, 1 - c), device_id_type=pl.DeviceIdType.MESH))
        for cp in cps:
            cp.start()
        for a in range(na):
            for n, k in enumerate(chips):
                px, py, _ = _flip(k, x, y, c)
                blk = out_refs[a].at[4 * px + 2 * py + (1 - c)]
                pltpu.make_async_remote_copy(src_ref=blk, dst_ref=blk, send_sem=send_sems.at[a * 3 + n],
                                             recv_sem=recv_sems.at[a * 3 + n], device_id=(x, y, 1 - c),
                                             device_id_type=pl.DeviceIdType.MESH).wait_recv()
        for cp in cps:
            cp.wait_send()

    return pl.pallas_call(
        body, name=name,
        out_shape=[jax.ShapeDtypeStruct(l.shape, l.dtype) for l in lands],
        in_specs=[pl.BlockSpec(memory_space=pl.ANY)] * na,
        out_specs=[pl.BlockSpec(memory_space=pl.ANY)] * na,
        input_output_aliases={a: a for a in range(na)},
        scratch_shapes=[pltpu.SemaphoreType.DMA((3 * na,)), pltpu.SemaphoreType.DMA((3 * na,))],
    )(*lands)


_SIDE_EFFECT = pltpu.CompilerParams(has_side_effects=pltpu.SideEffectType.DATAFLOW_SIDE_EFFECTING)


def _push_start(name, kind, srcs, lands):
    na = len(srcs)

    def body(*refs):
        remote, local = _push_copies(kind, refs[:na], refs[na:2 * na], *refs[2 * na:2 * na + 3])
        for cp in remote + local:
            cp.start()
        token = refs[-1]
        token[...] = jnp.zeros_like(token)

    arrays = list(srcs) + list(lands)
    n_remote = na * len(_PEERS[kind])
    res = pl.pallas_call(
        body, name=name,
        out_shape=(pltpu.SemaphoreType.DMA((n_remote,)), pltpu.SemaphoreType.DMA((n_remote,)),
                   pltpu.SemaphoreType.DMA((na,)), *[pltpu.HBM(t.shape, t.dtype) for t in arrays],
                   jax.ShapeDtypeStruct((SUBLANE, LANE), f32)),
        in_specs=[_HBM] * (2 * na),
        out_specs=(_SEM, _SEM, _SEM, *[_HBM] * (2 * na), pl.BlockSpec(memory_space=pltpu.VMEM)),
        input_output_aliases={i: 3 + i for i in range(2 * na)},
        compiler_params=_SIDE_EFFECT,
    )(*[pltpu.with_memory_space_constraint(t, pltpu.HBM) for t in arrays])
    return list(res[:3]), res[3:3 + na], res[3 + na:3 + 2 * na], res[-1][:1, :1]


def _push_wait(name, kind, sems, srcs, lands, after):
    na = len(srcs)

    def body(*refs):
        remote, local = _push_copies(kind, refs[:na], refs[na:2 * na], *refs[2 * na:2 * na + 3])
        for cp in remote:
            cp.wait_send()
            cp.wait_recv()
        for cp in local:
            cp.wait()

    arrays = list(srcs) + list(lands)
    res = pl.pallas_call(
        body, name=name,
        out_shape=tuple(pltpu.HBM(t.shape, t.dtype) for t in arrays),
        in_specs=[_HBM] * (2 * na) + [_SEM] * 3 + [pl.BlockSpec(memory_space=pl.ANY)],
        out_specs=tuple([_HBM] * (2 * na)),
        input_output_aliases={i: i for i in range(2 * na)},
        compiler_params=_SIDE_EFFECT,
    )(*arrays, *sems, after)
    return res[na:]


def _sum_blocks(name, land):
    _, R, n = land.shape
    tr = R

    def body(l_ref, o_ref):
        acc = l_ref[0].astype(f32)
        for s in range(1, N_DEV):
            acc = acc + l_ref[s].astype(f32)
        o_ref[...] = acc

    return pl.pallas_call(
        body, name=name, grid=(R // tr,),
        in_specs=[pl.BlockSpec((N_DEV, tr, n), lambda i: (0, i, 0))],
        out_specs=pl.BlockSpec((tr, n), lambda i: (i, 0)),
        out_shape=jax.ShapeDtypeStruct((R, n), f32),
        compiler_params=_cparams("parallel"),
    )(land)


def _adamw(name, w, g, m, v):
    shape = w.shape
    last = shape[-1]
    rows = math.prod(shape[:-1])
    tm = 256 if rows % 256 == 0 and rows > 256 else rows
    w2, g2, m2, v2 = (t.reshape(rows, last) for t in (w, g, m, v))

    def body(w_ref, g_ref, m_ref, v_ref, d_ref, mo_ref, vo_ref):
        gg = g_ref[...]
        mn = ADAM_B1 * m_ref[...] + (1.0 - ADAM_B1) * gg
        vn = ADAM_B2 * v_ref[...] + (1.0 - ADAM_B2) * jnp.square(gg)
        m_hat = mn / (1.0 - ADAM_B1 ** ADAM_STEP)
        v_hat = vn / (1.0 - ADAM_B2 ** ADAM_STEP)
        d_ref[...] = -ADAM_LR * (m_hat / (jnp.sqrt(v_hat) + ADAM_EPS) + ADAM_WD * w_ref[...])
        mo_ref[...] = mn
        vo_ref[...] = vn

    spec = pl.BlockSpec((tm, last), lambda i: (i, 0))
    d, mn, vn = pl.pallas_call(
        body, name=name, grid=(rows // tm,), in_specs=[spec] * 4, out_specs=[spec] * 3,
        out_shape=[jax.ShapeDtypeStruct((rows, last), f32)] * 3,
        compiler_params=_cparams("parallel"),
    )(w2, g2, m2, v2)
    return d.reshape(shape), mn.reshape(shape), vn.reshape(shape)


def _adamw_land(name, lands, w, m, v, tm=256):
    L = len(lands)
    _, R, C = lands[0].shape
    tm = min(tm, R)

    def body(*refs):
        l_refs, (w_ref, m_ref, v_ref, g_ref, d_ref, mo_ref, vo_ref) = refs[:L], refs[L:]
        for k in range(L):
            @pl.when(pl.program_id(0) == k)
            def _(k=k):
                gg = l_refs[k][0].astype(f32)
                for s in range(1, N_DEV):
                    gg = gg + l_refs[k][s].astype(f32)
                g_ref[...] = gg
                mn = ADAM_B1 * m_ref[...] + (1.0 - ADAM_B1) * gg
                vn = ADAM_B2 * v_ref[...] + (1.0 - ADAM_B2) * jnp.square(gg)
                m_hat = mn / (1.0 - ADAM_B1 ** ADAM_STEP)
                v_hat = vn / (1.0 - ADAM_B2 ** ADAM_STEP)
                d_ref[...] = -ADAM_LR * (m_hat / (jnp.sqrt(v_hat) + ADAM_EPS) + ADAM_WD * w_ref[...])
                mo_ref[...] = mn
                vo_ref[...] = vn

    land_specs = [pl.BlockSpec((N_DEV, tm, C), lambda l, i, k=k: (0, jnp.where(l == k, i, 0), 0)) for k in range(L)]
    spec = pl.BlockSpec((None, tm, C), lambda l, i: (l, i, 0))
    return pl.pallas_call(
        body, name=name, grid=(L, R // tm),
        in_specs=land_specs + [spec] * 3,
        out_specs=[spec] * 4,
        out_shape=[jax.ShapeDtypeStruct((L, R, C), f32)] * 4,
        compiler_params=_cparams("arbitrary", "arbitrary"),
    )(*lands, w, m, v)


BIG = [("hyb_w_in", 2), ("hyb_w_out", 1), ("rec_w_in", 2), ("rec_w_out", 1), ("rec_w_a", 2), ("rec_w_x", 2),
       ("mlp_w1", 2), ("mlp_w2", 1)]
SMALL = [("hyb_conv_w", 2), ("rec_conv_w", 2), ("rec_conv_b", 1), ("rec_b_a", 1), ("rec_b_x", 1), ("rec_lambda", 1)]
REPL = ["hyb_sinks", "hyb_a_log", "hyb_dt_bias", "hyb_norm_w", "ln1_g", "ln1_b", "ln2_g", "ln2_b"]
WEIGHTS = ["hyb_w_in", "hyb_sinks", "hyb_conv_w", "hyb_a_log", "hyb_dt_bias", "hyb_norm_w", "hyb_w_out", "rec_w_in",
           "rec_conv_w", "rec_conv_b", "rec_w_a", "rec_b_a", "rec_w_x", "rec_b_x", "rec_lambda", "rec_w_out",
           "ln1_g", "ln1_b", "mlp_w1", "mlp_w2", "ln2_g", "ln2_b"]


def _pack_rows(parts, dtype, row_mult):
    lead = parts[0].shape[:-1]
    flat = jnp.concatenate([p.astype(dtype) for p in parts], axis=-1)
    n = flat.shape[-1]
    unit = row_mult * LANE
    pad = (-n) % unit
    if pad:
        flat = jnp.concatenate([flat, jnp.zeros(lead + (pad,), dtype)], axis=-1)
    return flat.reshape(lead + ((n + pad) // LANE, LANE))


def _gather_full(gathered, shard_shapes, table):
    flat = gathered.reshape(N_DEV, -1)
    out, off = {}, 0
    for name, ax in table:
        shp = shard_shapes[name]
        n = math.prod(shp)
        arr = flat[:, off:off + n].reshape((N_DEV,) + shp)
        off += n
        arr = jnp.moveaxis(arr, 0, ax)
        out[name] = arr.reshape(shp[:ax] + (N_DEV * shp[ax],) + shp[ax + 1:])
    return out


def _matmul_layouts(tag, gw):
    out = {}
    bw = D_MODEL // LRU_BLOCKS
    for k, g in gw.items():
        L = g.shape[1]
        if k == "hyb_w_in":
            out[k] = _merge_cols(f"{tag}_w_in_merge", g)
        elif k in ("hyb_w_out", "rec_w_out"):
            out[k] = jnp.swapaxes(g, 0, 1).reshape(L, D_MODEL, D_MODEL)
        elif k in ("rec_w_a", "rec_w_x"):
            out[k] = jnp.moveaxis(g, 0, 2).reshape(L, LRU_BLOCKS, bw, bw)
        else:
            out[k] = g
    return out


def kernel(x, hyb_w_in, hyb_sinks, hyb_conv_w, hyb_a_log, hyb_dt_bias, hyb_norm_w, hyb_w_out, rec_w_in, rec_conv_w, rec_conv_b, rec_w_a, rec_b_a, rec_w_x, rec_b_x, rec_lambda, rec_w_out, ln1_g, ln1_b, mlp_w1, mlp_w2, ln2_g, ln2_b, loss_target, m_hyb_w_in, m_hyb_sinks, m_hyb_conv_w, m_hyb_a_log, m_hyb_dt_bias, m_hyb_norm_w, m_hyb_w_out, m_rec_w_in, m_rec_conv_w, m_rec_conv_b, m_rec_w_a, m_rec_b_a, m_rec_w_x, m_rec_b_x, m_rec_lambda, m_rec_w_out, m_ln1_g, m_ln1_b, m_mlp_w1, m_mlp_w2, m_ln2_g, m_ln2_b, v_hyb_w_in, v_hyb_sinks, v_hyb_conv_w, v_hyb_a_log, v_hyb_dt_bias, v_hyb_norm_w, v_hyb_w_out, v_rec_w_in, v_rec_conv_w, v_rec_conv_b, v_rec_w_a, v_rec_b_a, v_rec_w_x, v_rec_b_x, v_rec_lambda, v_rec_w_out, v_ln1_g, v_ln1_b, v_mlp_w1, v_mlp_w2, v_ln2_g, v_ln2_b):
    args = locals()
    w = {k: args[k] for k in WEIGHTS}
    m = {k: args["m_" + k] for k in WEIGHTS}
    v = {k: args["v_" + k] for k in WEIGHTS}
    shard_shapes = {k: tuple(t.shape) for k, t in w.items()}
    xi, yi, ci = _my_coords()
    me = 4 * xi + 2 * yi + ci

    in_flight = {}

    def install(tag, names, got):
        for (k, i), arr in zip(names, _matmul_layouts(tag, {k: g for (k, _), g in zip(names, got)}).values()):
            W[k][i] = arr

    def start_gather(tag, names):
        srcs = [w[k][i:i + 1].astype(bf16) for k, i in names]
        *pending, zero = _push_start(f"gather_{tag}_start", "gather", srcs,
                                     [lax.empty((N_DEV,) + s.shape, bf16) for s in srcs])
        in_flight[tag] = (names, pending)
        return zero

    def finish_gather(tag, after):
        names, pending = in_flight.pop(tag)
        half = _push_wait(f"gather_{tag}_wait", "gather", *pending, after)
        install(tag, names, _pass_to_sibling(f"gather_{tag}_pass", half))

    def started(layer, zero):
        k = "hyb_sinks" if layer % 2 == 0 else "rec_conv_b"
        W[k] = W[k] + zero

    first = _layer_weights(0)[:2]
    gathered0 = _all_gather("gather_first", [w[k][i:i + 1].astype(bf16) for k, i in first]
                            + [_pack_rows([w[k].reshape(-1) for k, _ in SMALL], f32, SUBLANE)])
    W = _gather_full(gathered0[-1], shard_shapes, SMALL)
    W.update({k: w[k] for k in REPL})
    W.update({k: {} for k, _ in BIG})
    install("l0a", first, gathered0[:-1])
    started(0, start_gather("l0b", _layer_weights(0)[2:]) + start_gather("l1", _layer_weights(1)))

    def load_layer(layer, stage, after):
        if layer == 0 and stage == 2:
            finish_gather("l0b", after)
        if layer > 0 and stage == 0:
            finish_gather(f"l{layer}", after)
            if layer + 1 < DEPTH:
                started(layer, start_gather(f"l{layer + 1}", _layer_weights(layer + 1)))

    grads_in_flight = {}

    def grads_ready(tag, a, b):
        g = {**a, **b}
        srcs = list(g.values())
        *pending, zero = _push_start(f"scatter_{tag}_start", "scatter", srcs, [lax.empty(s.shape, bf16) for s in srcs])
        grads_in_flight[tag] = (list(g.keys()), pending)
        return zero

    loss_local, grad_x, G = _local_step(x[0], loss_target[0], W, load_layer, grads_ready)
    loss = lax.psum(loss_local, MESH_AXES)

    landed = {}

    def land(tag, after):
        keys, pending = grads_in_flight[tag]
        landed.update(zip(keys, _push_wait(f"scatter_{tag}_wait", "scatter", *pending, after)))

    tags = list(grads_in_flight)
    for tag in tags[:-1]:
        land(tag, grad_x)
    rest = _pack_rows([G[k].reshape(-1) for k, _ in SMALL] + [G[k].reshape(-1) for k in REPL], f32, SUBLANE)
    g_rest = _sum_blocks("sum_rest", _all_gather("gather_rest", [rest])[0]).reshape(-1)

    grads, delta, new_m, new_v = {}, {}, {}, {}

    def adamw_big(k):
        shp = shard_shapes[k]
        s3 = (shp[0], math.prod(shp[1:-1]), shp[-1])
        lands = [landed[(k, i)].reshape((N_DEV,) + s3[1:]) for i in range(shp[0])]
        res = _adamw_land("adamw_" + k, lands, w[k].reshape(s3), m[k].reshape(s3), v[k].reshape(s3))
        grads[k], delta[k], new_m[k], new_v[k] = (r.reshape(shp) for r in res)

    late = {k for k, _ in grads_in_flight[tags[-1]][0]}
    for k in [k for k, _ in BIG if k not in late]:
        adamw_big(k)
        done = new_v[k]
    land(tags[-1], done)
    for k in [k for k, _ in BIG if k in late]:
        adamw_big(k)
    off = 0
    for k, ax in SMALL:
        full_shape = G[k].shape
        n = math.prod(full_shape)
        full = g_rest[off:off + n].reshape(full_shape)
        off += n
        s = shard_shapes[k][ax]
        grads[k] = lax.dynamic_slice_in_dim(full, me * s, s, axis=ax)
    for k in REPL:
        n = math.prod(shard_shapes[k])
        grads[k] = g_rest[off:off + n].reshape(shard_shapes[k])
        off += n

    for k in [k for k, _ in SMALL] + REPL:
        delta[k], new_m[k], new_v[k] = _adamw("adamw_" + k, w[k], grads[k], m[k], v[k])

    return (loss, grad_x[None], *[grads[k] for k in WEIGHTS], *[delta[k] for k in WEIGHTS],
            *[new_m[k] for k in WEIGHTS], *[new_v[k] for k in WEIGHTS])
```

```python
import functools
import math

import jax
import jax.numpy as jnp
from jax import lax
from jax.experimental import pallas as pl
from jax.experimental.pallas import tpu as pltpu

f32 = jnp.float32
bf16 = jnp.bfloat16

N_DEV = 8
D_MODEL = 1024
DEPTH = 4
A_HEAD_DIM = 64
A_Q_HEADS = 8
WINDOW = 128
ROPE_THETA = 10000.0
B_HEADS = 4
B_HEAD_DIM = 128
B_CHUNK = 64
LRU_BLOCKS = 4
LRU_C = 8.0
D_FF = 4 * D_MODEL
HYB_PROJ = 2824
HYB_PROJ_PAD = 3072
DN_ALPHA = (2 * DEPTH) ** 0.25
LN_EPS = 1e-5
NORM_EPS = 1e-6
ADAM_LR = 0.001
ADAM_B1 = 0.9
ADAM_B2 = 0.999
ADAM_EPS = 1e-08
ADAM_WD = 0.01
ADAM_STEP = 10

LANE = 128
SUBLANE = 8
VMEM_LIMIT = 48 * 1024 * 1024

CB_QA, CB_KA, CB_VA, CB_CONV, CB_Z, CB_LG = 0, 4, 5, 6, 18, 22

MESH_AXES = ("x", "y", "c")


def _cparams(*sem):
    return pltpu.CompilerParams(dimension_semantics=sem, vmem_limit_bytes=VMEM_LIMIT)


def _dot(a, b, dims, precision=None):
    return lax.dot_general(a, b, (dims, ((), ())), preferred_element_type=f32, precision=precision)


NN = ((1,), (0,))
NT = ((1,), (1,))
TN = ((0,), (0,))


def _mat_spec(arr, kind, lead, br, bc, rb, cb):
    if kind == "plain":
        return pl.BlockSpec((br, bc), lambda i, j, k: (rb(i, j, k), cb(i, j, k)))
    if kind == "lead":
        return pl.BlockSpec((None, br, bc), lambda i, j, k: (lead, rb(i, j, k), cb(i, j, k)))
    if kind == "devcol":
        assert bc == arr.shape[-1]
        return pl.BlockSpec((None, None, br, bc), lambda i, j, k: (cb(i, j, k), lead, rb(i, j, k), 0))
    assert kind == "devrow" and br == arr.shape[-2]
    return pl.BlockSpec((None, None, br, bc), lambda i, j, k: (rb(i, j, k), lead, 0, cb(i, j, k)))


def _mm(name, a, b, mode, *, b_kind="plain", b_lead=0, o_kind="plain", epilogue=None, extras=(), params=(),
        out_dtypes=(f32,), tm=1024, tn=1024, tk=None):
    if tk is None:
        tk = 512 if mode == "tn" else 1024
    if b_kind in ("plain", "lead"):
        b_rows, b_cols = b.shape[-2:]
    elif b_kind == "devcol":
        b_rows, b_cols = b.shape[-2], N_DEV * b.shape[-1]
    else:
        b_rows, b_cols = N_DEV * b.shape[-2], b.shape[-1]
    if mode == "nn":
        (M, K), (K2, N) = a.shape, (b_rows, b_cols)
    elif mode == "nt":
        (M, K), (N, K2) = a.shape, (b_rows, b_cols)
    else:
        (K, M), (K2, N) = a.shape, (b_rows, b_cols)
    assert K == K2, (name, a.shape, b.shape, mode)
    tm, tn, tk = min(tm, M), min(tn, N), min(tk, K)
    cols_are_n = mode != "nt"
    if b_kind == "devcol":
        tn, tk = (b.shape[-1], tk) if cols_are_n else (tn, b.shape[-1])
    if b_kind == "devrow":
        tn, tk = (tn, b.shape[-2]) if cols_are_n else (b.shape[-2], tk)
    shard = N // N_DEV
    if o_kind == "devcol":
        tn = max(shard, tn // shard * shard)
    assert M % tm == 0 and N % tn == 0 and K % tk == 0, (name, M, N, K, tm, tn, tk)
    nk = K // tk
    dims = {"nn": NN, "nt": NT, "tn": TN}[mode]
    n_ex, n_out = len(extras) + len(params), len(out_dtypes)

    def body(*refs):
        a_ref, b_ref = refs[:2]
        ex = refs[2:2 + n_ex]
        outs = refs[2 + n_ex:2 + n_ex + n_out]
        acc = refs[-1]
        k = pl.program_id(2)

        @pl.when(k == 0)
        def _():
            acc[...] = jnp.zeros_like(acc)

        acc[...] += _dot(a_ref[...].astype(bf16), b_ref[...].astype(bf16), dims)

        @pl.when(k == nk - 1)
        def _():
            r = acc[...]
            res = epilogue(r, *[e[...] for e in ex]) if epilogue is not None else (r,)
            for o, v in zip(outs, res):
                if o_kind == "plain":
                    o[...] = v.astype(o.dtype)
                else:
                    for q in range(tn // shard):
                        o[q] = v[:, q * shard:(q + 1) * shard].astype(o.dtype)

    if mode == "tn":
        a_spec = pl.BlockSpec((tk, tm), lambda i, j, k: (k, i))
    else:
        a_spec = pl.BlockSpec((tm, tk), lambda i, j, k: (i, k))
    jb, kb = (lambda i, j, k: j), (lambda i, j, k: k)
    if mode == "nt":
        b_spec = _mat_spec(b, b_kind, b_lead, tn, tk, jb, kb)
    else:
        b_spec = _mat_spec(b, b_kind, b_lead, tk, tn, kb, jb)
    e_spec = pl.BlockSpec((tm, tn), lambda i, j, k: (i, j))
    if o_kind == "plain":
        o_spec, o_shape = e_spec, (M, N)
    else:
        o_spec, o_shape = pl.BlockSpec((tn // shard, tm, shard), lambda i, j, k: (j, i, 0)), (N_DEV, M, shard)
    res = pl.pallas_call(
        body, name=name,
        grid=(M // tm, N // tn, nk),
        in_specs=[a_spec, b_spec] + [e_spec] * len(extras)
        + [pl.BlockSpec(p.shape, lambda i, j, k: (0, 0)) for p in params],
        out_specs=[o_spec] * n_out,
        out_shape=[jax.ShapeDtypeStruct(o_shape, dt) for dt in out_dtypes],
        scratch_shapes=[pltpu.VMEM((tm, tn), f32)],
        compiler_params=_cparams("parallel", "parallel", "arbitrary"),
    )(a, b, *extras, *params)
    return res[0] if n_out == 1 else res


def _row_spec(tm, cb, width):
    assert (cb * LANE) % width == 0
    blk = (cb * LANE) // width
    return pl.BlockSpec((tm, width), lambda i: (i, blk))


def _whole_spec(p):
    nd = p.ndim
    return pl.BlockSpec(p.shape, lambda i: (0,) * nd)


def _tl_fwd(name, fn, rows, params, out_widths, out_dtypes, tm=256):
    T = rows[0][0].shape[0]
    tm = min(tm, T)
    nr, npar = len(rows), len(params)

    def body(*refs):
        vals = [r[...] for r in refs[:nr + npar]]
        outs = fn(*vals)
        for o, v in zip(refs[nr + npar:], outs):
            o[...] = v.astype(o.dtype)

    res = pl.pallas_call(
        body, name=name, grid=(T // tm,),
        in_specs=[_row_spec(tm, cb, w) for (_, cb, w) in rows] + [_whole_spec(p) for p in params],
        out_specs=[pl.BlockSpec((tm, w), lambda i: (i, 0)) for w in out_widths],
        out_shape=[jax.ShapeDtypeStruct((T, w), dt) for w, dt in zip(out_widths, out_dtypes)],
        compiler_params=_cparams("parallel"),
    )(*[r[0] for r in rows], *params)
    return res


def _tl_bwd(name, fn, rows, params, cot_rows, cot_fn=None, tm=256):
    T = rows[0][0].shape[0]
    tm = min(tm, T)
    nr, npar, nc = len(rows), len(params), len(cot_rows)

    def body(*refs):
        vals = [r[...] for r in refs[:nr + npar]]
        cots = [r[...] for r in refs[nr + npar:nr + npar + nc]]
        outs = refs[nr + npar + nc:]
        cot = tuple(cot_fn(*cots)) if cot_fn is not None else tuple(cots)
        _, vjp = jax.vjp(fn, *vals)
        grads = vjp(cot)
        for o, g in zip(outs[:nr], grads[:nr]):
            o[...] = g.astype(o.dtype)
        i = pl.program_id(0)
        for o, g in zip(outs[nr:], grads[nr:]):
            @pl.when(i == 0)
            def _(o=o):
                o[...] = jnp.zeros_like(o)
            o[...] += g

    res = pl.pallas_call(
        body, name=name, grid=(T // tm,),
        in_specs=[_row_spec(tm, cb, w) for (_, cb, w) in rows] + [_whole_spec(p) for p in params]
        + [_row_spec(tm, cb, w) for (_, cb, w) in cot_rows],
        out_specs=[pl.BlockSpec((tm, w), lambda i: (i, 0)) for (_, _, w) in rows] + [_whole_spec(p) for p in params],
        out_shape=[jax.ShapeDtypeStruct((T, w), f32) for (_, _, w) in rows]
        + [jax.ShapeDtypeStruct(p.shape, f32) for p in params],
        compiler_params=_cparams("arbitrary"),
    )(*[r[0] for r in rows], *params, *[r[0] for r in cot_rows])
    return res[:nr], res[nr:]


def _ln_res_fn(x, mix, g, b):
    pre = DN_ALPHA * x + mix
    mu = jnp.mean(pre, axis=-1, keepdims=True)
    var = jnp.mean(jnp.square(pre - mu), axis=-1, keepdims=True)
    return ((pre - mu) * lax.rsqrt(var + LN_EPS) * g + b,)


@jax.custom_jvp
def _expm1(x):
    small = jnp.abs(x) < 0.3
    xs = jnp.where(small, x, 0.0)
    poly = xs * (1.0 + xs * (1 / 2 + xs * (1 / 6 + xs * (1 / 24 + xs * (1 / 120 + xs * (
        1 / 720 + xs * (1 / 5040 + xs * (1 / 40320 + xs * (1 / 362880)))))))))
    return jnp.where(small, poly, jnp.exp(x) - 1.0)


@_expm1.defjvp
def _expm1_jvp(primals, tangents):
    (x,), (t,) = primals, tangents
    return _expm1(x), t * jnp.exp(x)


def _rglru_pre_fn(pre_r, pre_i, xc, b_a, b_x, lam):
    r = jax.nn.sigmoid(pre_r + b_a)
    i = jax.nn.sigmoid(pre_i + b_x)
    log_a = -LRU_C * r * jax.nn.softplus(-lam)
    a = jnp.exp(log_a)
    b = jnp.sqrt(-_expm1(2.0 * log_a)) * (i * xc)
    return a, b


def _rec_gate_fn(h, gate):
    return (h * jax.nn.gelu(gate),)


def _loss_head(y, t, tm=256):
    T, Dm = y.shape

    def body(y_ref, t_ref, dy_ref, loss_ref):
        e = y_ref[...] - t_ref[...]
        dy_ref[...] = e * (1.0 / Dm)

        @pl.when(pl.program_id(0) == 0)
        def _():
            loss_ref[...] = jnp.zeros_like(loss_ref)

        loss_ref[...] += 0.5 * jnp.sum(jnp.mean(e * e, axis=-1, keepdims=True), axis=0, keepdims=True)

    dy, loss = pl.pallas_call(
        body, name="loss_head", grid=(T // tm,),
        in_specs=[pl.BlockSpec((tm, Dm), lambda i: (i, 0))] * 2,
        out_specs=[pl.BlockSpec((tm, Dm), lambda i: (i, 0)), pl.BlockSpec((SUBLANE, LANE), lambda i: (0, 0))],
        out_shape=[jax.ShapeDtypeStruct((T, Dm), f32), jax.ShapeDtypeStruct((SUBLANE, LANE), f32)],
        compiler_params=_cparams("arbitrary"),
    )(y, t)
    return loss[0, 0], dy


def _conv_fwd(name, x, cb0, nblk, w, bias, tm=2048):
    T = x.shape[0]
    tm = min(tm, T)
    hb = tm // SUBLANE
    has_b = bias is not None

    def body(*refs):
        cur, prev, w_ref = refs[:3]
        b_ref = refs[3] if has_b else None
        o = refs[-1]
        i = pl.program_id(1)
        p = jnp.where(i > 0, prev[...], 0.0)
        xcat = jnp.concatenate([p, cur[...]], axis=0)
        acc = cur[...] * w_ref[3:4, :]
        for j in range(3):
            acc = acc + pltpu.roll(xcat, 3 - j, axis=0)[SUBLANE:] * w_ref[j:j + 1, :]
        if has_b:
            acc = acc + b_ref[...]
        o[...] = acc

    in_specs = [
        pl.BlockSpec((tm, LANE), lambda c, i: (i, cb0 + c)),
        pl.BlockSpec((SUBLANE, LANE), lambda c, i: (jnp.maximum(i * hb - 1, 0), cb0 + c)),
        pl.BlockSpec((4, LANE), lambda c, i: (0, c)),
    ]
    args = [x, x, w]
    if has_b:
        in_specs.append(pl.BlockSpec((1, LANE), lambda c, i: (0, c)))
        args.append(bias)
    return pl.pallas_call(
        body, name=name, grid=(nblk, T // tm),
        in_specs=in_specs,
        out_specs=pl.BlockSpec((tm, LANE), lambda c, i: (i, c)),
        out_shape=jax.ShapeDtypeStruct((T, nblk * LANE), f32),
        compiler_params=_cparams("parallel", "parallel"),
    )(*args)


def _conv_bwd(name, dy, x, cb0, nblk, w, tm=2048):
    T = x.shape[0]
    tm = min(tm, T)
    hb = tm // SUBLANE
    nt = T // tm

    def body(dcur, dnext, xcur, xprev, w_ref, dx_ref, dw_ref, db_ref):
        i = pl.program_id(1)
        d = dcur[...]
        dn = jnp.where(i < nt - 1, dnext[...], 0.0)
        dcat = jnp.concatenate([d, dn], axis=0)
        acc = d * w_ref[3:4, :]
        for j in range(3):
            s = 3 - j
            acc = acc + pltpu.roll(dcat, tm + SUBLANE - s, axis=0)[:tm] * w_ref[j:j + 1, :]
        dx_ref[...] = acc

        p = jnp.where(i > 0, xprev[...], 0.0)
        xcat = jnp.concatenate([p, xcur[...]], axis=0)
        rows = [jnp.sum(d * pltpu.roll(xcat, 3 - j, axis=0)[SUBLANE:], axis=0, keepdims=True) for j in range(3)]
        rows.append(jnp.sum(d * xcur[...], axis=0, keepdims=True))
        rows.append(jnp.zeros((SUBLANE - 4, LANE), f32))

        @pl.when(i == 0)
        def _():
            dw_ref[...] = jnp.zeros_like(dw_ref)
            db_ref[...] = jnp.zeros_like(db_ref)

        dw_ref[...] += jnp.concatenate(rows, axis=0)
        db_ref[...] += jnp.broadcast_to(jnp.sum(d, axis=0, keepdims=True), (SUBLANE, LANE))

    nh = T // SUBLANE
    dx, dw, db = pl.pallas_call(
        body, name=name, grid=(nblk, nt),
        in_specs=[
            pl.BlockSpec((tm, LANE), lambda c, i: (i, c)),
            pl.BlockSpec((SUBLANE, LANE), lambda c, i: (jnp.minimum((i + 1) * hb, nh - 1), c)),
            pl.BlockSpec((tm, LANE), lambda c, i: (i, cb0 + c)),
            pl.BlockSpec((SUBLANE, LANE), lambda c, i: (jnp.maximum(i * hb - 1, 0), cb0 + c)),
            pl.BlockSpec((4, LANE), lambda c, i: (0, c)),
        ],
        out_specs=[
            pl.BlockSpec((tm, LANE), lambda c, i: (i, c)),
            pl.BlockSpec((SUBLANE, LANE), lambda c, i: (0, c)),
            pl.BlockSpec((SUBLANE, LANE), lambda c, i: (0, c)),
        ],
        out_shape=[jax.ShapeDtypeStruct((T, nblk * LANE), f32),
                   jax.ShapeDtypeStruct((SUBLANE, nblk * LANE), f32),
                   jax.ShapeDtypeStruct((SUBLANE, nblk * LANE), f32)],
        compiler_params=_cparams("parallel", "arbitrary"),
    )(dy, dy, x, x, w)
    return dx, dw[:4], db[0]


@functools.partial(jax.custom_vjp, nondiff_argnums=(1,))
def _lroll(x, s):
    return pltpu.roll(x, s, axis=1)


def _lroll_fwd(x, s):
    return _lroll(x, s), None


def _lroll_bwd(s, _, g):
    return (_lroll(g, (LANE - s) % LANE),)


_lroll.defvjp(_lroll_fwd, _lroll_bwd)


def _rope_tables(T):
    half = A_HEAD_DIM // 2
    inv_freq = ROPE_THETA ** (-jnp.arange(half, dtype=f32) / half)
    ang = jnp.arange(T, dtype=f32)[:, None] * inv_freq[None, :]
    cos, sin = jnp.cos(ang), jnp.sin(ang)
    return jnp.tile(jnp.concatenate([cos, cos], axis=1), (1, 2)), jnp.tile(jnp.concatenate([-sin, sin], axis=1), (1, 2))


def _attn_block_fn(n, q, kp, kc, vp, vc, cq, sq, cp, sp, sinks):
    W = WINDOW
    lane = lax.broadcasted_iota(jnp.int32, (W, LANE), 1)
    lo_half = (lane % A_HEAD_DIM) < (A_HEAD_DIM // 2)
    lane8 = lax.broadcasted_iota(jnp.int32, sinks.shape, 1)

    def rope(x, c, s):
        return x * c + jnp.where(lo_half, _lroll(x, LANE - A_HEAD_DIM // 2), _lroll(x, A_HEAD_DIM // 2)) * s

    k2 = jnp.concatenate([rope(kp, cp, sp), rope(kc, cq, sq)], axis=0).astype(bf16)
    v2 = jnp.concatenate([vp, vc], axis=0).astype(bf16)
    qs, sink_rows = [], []
    for t in range(4):
        qt = rope(q[:, LANE * t:LANE * (t + 1)], cq, sq)
        g = t // 2
        for hh in range(2):
            qa = jnp.where((lane // A_HEAD_DIM) == hh, qt, 0.0)
            qs.append(_lroll(qa, A_HEAD_DIM) if hh != g else qa)
            sink = jnp.sum(jnp.where(lane8 == 2 * t + hh, sinks, 0.0), axis=1, keepdims=True)
            sink_rows.append(jnp.broadcast_to(sink, (W, 1)))
    qall = jnp.concatenate(qs, axis=0).astype(bf16)
    sink = jnp.concatenate(sink_rows, axis=0)
    row = lax.broadcasted_iota(jnp.int32, (A_Q_HEADS * W, 2 * W), 0) & (W - 1)
    col = lax.broadcasted_iota(jnp.int32, (A_Q_HEADS * W, 2 * W), 1)
    dist = row + W - col
    mask = (dist >= 0) & (dist < W) & ((col >= W) | (n > 0))
    s = jnp.where(mask, _dot(qall, k2, NT) * (A_HEAD_DIM ** -0.5), -jnp.inf)
    m = jnp.maximum(jnp.max(s, axis=-1, keepdims=True), sink)
    e = jnp.exp(s - m)
    p = e / (jnp.sum(e, axis=-1, keepdims=True) + jnp.exp(sink - m))
    o = _dot(p.astype(bf16), v2, NN)
    outs = []
    for t in range(4):
        g = t // 2
        ot = jnp.zeros((W, LANE), f32)
        for hh in range(2):
            j = 2 * t + hh
            oj = jnp.where((lane // A_HEAD_DIM) == g, o[W * j:W * (j + 1)], 0.0)
            ot = ot + (_lroll(oj, A_HEAD_DIM) if hh != g else oj)
        outs.append(ot)
    return jnp.concatenate(outs, axis=1)


def _attn_specs():
    W = WINDOW
    prev = lambda n: jnp.maximum(n - 1, 0)
    return [
        pl.BlockSpec((W, 4 * LANE), lambda n: (n, CB_QA // 4)),
        pl.BlockSpec((W, LANE), lambda n: (prev(n), CB_KA)),
        pl.BlockSpec((W, LANE), lambda n: (n, CB_KA)),
        pl.BlockSpec((W, LANE), lambda n: (prev(n), CB_VA)),
        pl.BlockSpec((W, LANE), lambda n: (n, CB_VA)),
        pl.BlockSpec((W, LANE), lambda n: (n, 0)),
        pl.BlockSpec((W, LANE), lambda n: (n, 0)),
        pl.BlockSpec((W, LANE), lambda n: (prev(n), 0)),
        pl.BlockSpec((W, LANE), lambda n: (prev(n), 0)),
        pl.BlockSpec((1, A_Q_HEADS), lambda n: (0, 0)),
    ]


def _attn_fwd(name, proj, cos, sin, sinks):
    T = proj.shape[0]
    W = WINDOW

    def body(*refs):
        o = refs[-1]
        o[...] = _attn_block_fn(pl.program_id(0), *[r[...] for r in refs[:-1]])

    return pl.pallas_call(
        body, name=name, grid=(T // W,),
        in_specs=_attn_specs(),
        out_specs=pl.BlockSpec((W, 4 * LANE), lambda n: (n, 0)),
        out_shape=jax.ShapeDtypeStruct((T, 4 * LANE), f32),
        compiler_params=_cparams("parallel"),
    )(proj, proj, proj, proj, proj, cos, sin, cos, sin, sinks)


def _attn_bwd(name, proj, cos, sin, sinks, d_oab):
    T = proj.shape[0]
    W = WINDOW

    def body(*refs):
        ins = [r[...] for r in refs[:10]]
        do = refs[10][...]
        dq_ref, dk_ref, dv_ref, ds_ref = refs[11:]
        n = pl.program_id(0)
        _, vjp = jax.vjp(functools.partial(_attn_block_fn, n), *ins)
        dq, dkp, dkc, dvp, dvc, _, _, _, _, dsk = vjp(do)
        dq_ref[...] = dq

        @pl.when(n == 0)
        def _():
            dk_ref[...] = jnp.zeros_like(dk_ref)
            dv_ref[...] = jnp.zeros_like(dv_ref)
            ds_ref[...] = jnp.zeros_like(ds_ref)

        cur = pl.ds(pl.multiple_of(n * W, W), W)
        dk_ref[cur, :] += dkc
        dv_ref[cur, :] += dvc
        ds_ref[...] += dsk

        @pl.when(n > 0)
        def _():
            prv = pl.ds(pl.multiple_of((n - 1) * W, W), W)
            dk_ref[prv, :] += dkp
            dv_ref[prv, :] += dvp

    return pl.pallas_call(
        body, name=name, grid=(T // W,),
        in_specs=_attn_specs() + [pl.BlockSpec((W, 4 * LANE), lambda n: (n, 0))],
        out_specs=[pl.BlockSpec((W, 4 * LANE), lambda n: (n, 0)),
                   pl.BlockSpec((T, LANE), lambda n: (0, 0)),
                   pl.BlockSpec((T, LANE), lambda n: (0, 0)),
                   pl.BlockSpec((1, A_Q_HEADS), lambda n: (0, 0))],
        out_shape=[jax.ShapeDtypeStruct((T, 4 * LANE), f32), jax.ShapeDtypeStruct((T, LANE), f32),
                   jax.ShapeDtypeStruct((T, LANE), f32), jax.ShapeDtypeStruct((1, A_Q_HEADS), f32)],
        compiler_params=_cparams("arbitrary"),
    )(proj, proj, proj, proj, proj, cos, sin, cos, sin, sinks, d_oab)


def _bdot(spec, a, b, precision=None):
    return jnp.einsum(spec, a, b, preferred_element_type=f32, precision=precision)


@jax.custom_vjp
def _tri_inv(a):
    C = a.shape[-1]
    r = lax.broadcasted_iota(jnp.int32, (C, C), 0)
    c = lax.broadcasted_iota(jnp.int32, (C, C), 1)
    t = jnp.broadcast_to(jnp.where(r == c, 1.0, 0.0).astype(f32), a.shape)
    for j in range(C - 1):
        t = t - a[:, :, j:j + 1] * t[:, j:j + 1, :]
    return t


def _tri_inv_fwd(a):
    t = _tri_inv(a)
    return t, t


def _tri_inv_bwd(t, g):
    C = t.shape[-1]
    r = lax.broadcasted_iota(jnp.int32, (C, C), 0)
    c = lax.broadcasted_iota(jnp.int32, (C, C), 1)
    x = _bdot("hki,hkj->hij", t, g, precision=lax.Precision.HIGHEST)
    y = _bdot("hik,hjk->hij", x, t, precision=lax.Precision.HIGHEST)
    return (jnp.where(r > c, -y, 0.0),)


_tri_inv.defvjp(_tri_inv_fwd, _tri_inv_bwd)


@jax.custom_vjp
def _tri_inv_saved(a, t):
    return t


_tri_inv_saved.defvjp(lambda a, t: (t, t), lambda t, g: (_tri_inv_bwd(t, g)[0], jnp.zeros_like(t)))


def _silu(x):
    return x * jax.nn.sigmoid(x)


def _l2n(x):
    return x * lax.rsqrt(jnp.sum(x * x, axis=-1, keepdims=True) + NORM_EPS)


def _delta_chunk_fn(cq, ck, cv, z, lg, a_log, dt_bias, norm_w, S, t_saved=None, want_t=False):
    C = B_CHUNK
    lane = lax.broadcasted_iota(jnp.int32, (C, LANE), 1)
    pick = lambda l0: jnp.concatenate(
        [jnp.sum(jnp.where(lane == l0 + h, lg, 0.0), axis=1, keepdims=True)[None] for h in range(B_HEADS)], axis=0)
    bl, al = pick(0), pick(B_HEADS)
    q = _l2n(_silu(cq)) * (B_HEAD_DIM ** -0.5)
    k = _l2n(_silu(ck))
    v = _silu(cv)
    beta = jax.nn.sigmoid(bl)
    g = -jnp.exp(a_log) * jax.nn.softplus(al + dt_bias)
    r = lax.broadcasted_iota(jnp.int32, (C, C), 0)
    c = lax.broadcasted_iota(jnp.int32, (C, C), 1)
    eye = r == c
    g_row = jnp.sum(jnp.where(eye, g, 0.0), axis=1, keepdims=True)
    gc = jnp.sum(jnp.where(c <= r, g_row, 0.0), axis=2, keepdims=True)
    gc_row = jnp.sum(jnp.where(eye, gc, 0.0), axis=1, keepdims=True)
    decay_incl = jnp.exp(jnp.where(r >= c, gc - gc_row, -jnp.inf))
    decay_strict = jnp.where(r > c, decay_incl, 0.0)
    kb = k * beta
    vb = v * beta
    kbf = k.astype(bf16)
    a_mat = _bdot("hik,hjk->hij", kb.astype(bf16), kbf) * decay_strict
    t_f32 = _tri_inv(a_mat) if t_saved is None else _tri_inv_saved(a_mat, t_saved)
    t_mat = t_f32.astype(bf16)
    eg = jnp.exp(gc)
    u = _bdot("hij,hjv->hiv", t_mat, vb.astype(bf16))
    w = _bdot("hij,hjk->hik", t_mat, (kb * eg).astype(bf16))
    qk = _bdot("hik,hjk->hij", q.astype(bf16), kbf) * decay_incl
    g_last = jnp.sum(g, axis=1, keepdims=True)
    k_tail = k * jnp.exp(g_last - gc)
    Sb = S.astype(bf16)
    v_new = u - _bdot("hck,hkv->hcv", w.astype(bf16), Sb)
    o = _bdot("hck,hkv->hcv", (q * eg).astype(bf16), Sb) + _bdot("hij,hjv->hiv", qk.astype(bf16), v_new.astype(bf16))
    S_new = S * jnp.exp(g_last) + _bdot("hck,hcv->hkv", k_tail.astype(bf16), v_new.astype(bf16))
    ob = o * lax.rsqrt(jnp.mean(o * o, axis=-1, keepdims=True) + NORM_EPS) * norm_w
    return (ob * _silu(z), S_new) + ((t_f32,) if want_t else ())


def _delta_in_specs(rev, N):
    C = B_CHUNK
    ix = (lambda n: N - 1 - n) if rev else (lambda n: n)
    specs = [pl.BlockSpec((C, 3 * B_HEADS * LANE), lambda n: (ix(n), 0))]
    specs += [pl.BlockSpec((C, LANE), lambda n, h=h: (ix(n), CB_Z + h)) for h in range(B_HEADS)]
    specs += [
        pl.BlockSpec((C, LANE), lambda n: (ix(n), CB_LG)),
        pl.BlockSpec((B_HEADS, 1, 1), lambda n: (0, 0, 0)),
        pl.BlockSpec((B_HEADS, 1, 1), lambda n: (0, 0, 0)),
        pl.BlockSpec((1, LANE), lambda n: (0, 0)),
    ]
    return specs


def _delta_inputs(c_ref, z_refs, lg, al, dt, nw):
    H = B_HEADS
    part = lambda p: jnp.stack([c_ref[:, LANE * (p * H + h):LANE * (p * H + h + 1)] for h in range(H)])
    return (part(0), part(1), part(2), jnp.stack([z[...] for z in z_refs]), lg[...], al[...], dt[...], nw[...])


def _delta_fwd(name, c, proj, a_log, dt_bias, norm_w):
    T = c.shape[0]
    C = B_CHUNK
    N = T // C
    Dh = B_HEAD_DIM
    H = B_HEADS

    def body(*refs):
        c_ref, z_refs, (lg, al, dt, nw) = refs[0], refs[1:1 + H], refs[1 + H:5 + H]
        o_ref, s_ref, t_ref, S = refs[5 + H:]

        @pl.when(pl.program_id(0) == 0)
        def _():
            S[...] = jnp.zeros_like(S)

        s0 = S[...]
        s_ref[...] = s0
        ob, s1, t = _delta_chunk_fn(*_delta_inputs(c_ref, z_refs, lg, al, dt, nw), s0, want_t=True)
        for h in range(H):
            o_ref[:, LANE * h:LANE * (h + 1)] = ob[h]
        t_ref[...] = t
        S[...] = s1

    return pl.pallas_call(
        body, name=name, grid=(N,),
        in_specs=_delta_in_specs(False, N),
        out_specs=[pl.BlockSpec((C, H * LANE), lambda n: (n, 0)),
                   pl.BlockSpec((H, None, Dh, Dh), lambda n: (0, n, 0, 0)),
                   pl.BlockSpec((H, None, C, C), lambda n: (0, n, 0, 0))],
        out_shape=[jax.ShapeDtypeStruct((T, H * Dh), f32), jax.ShapeDtypeStruct((H, N, Dh, Dh), f32),
                   jax.ShapeDtypeStruct((H, N, C, C), f32)],
        scratch_shapes=[pltpu.VMEM((H, Dh, Dh), f32)],
        compiler_params=_cparams("arbitrary"),
    )(c, *([proj] * H), proj, a_log, dt_bias, norm_w)


def _delta_bwd(name, c, proj, a_log, dt_bias, norm_w, s_saved, t_saved, d_oab):
    T = c.shape[0]
    C = B_CHUNK
    N = T // C
    Dh = B_HEAD_DIM
    H = B_HEADS

    def body(*refs):
        c_ref, z_refs, (lg, al, dt, nw) = refs[0], refs[1:1 + H], refs[1 + H:5 + H]
        s_ref, t_ref, do_ref = refs[5 + H:8 + H]
        dc, dz, dlg, dal, ddt, dnw, dS = refs[8 + H:]

        @pl.when(pl.program_id(0) == 0)
        def _():
            dS[...] = jnp.zeros_like(dS)
            dal[...] = jnp.zeros_like(dal)
            ddt[...] = jnp.zeros_like(ddt)
            dnw[...] = jnp.zeros_like(dnw)

        _, vjp = jax.vjp(functools.partial(_delta_chunk_fn, t_saved=t_ref[...]),
                         *_delta_inputs(c_ref, z_refs, lg, al, dt, nw), s_ref[...])
        do = jnp.stack([do_ref[:, LANE * h:LANE * (h + 1)] for h in range(H)])
        g = vjp((do, dS[...]))
        for h in range(H):
            for p in range(3):
                dc[:, LANE * (p * H + h):LANE * (p * H + h + 1)] = g[p][h]
            dz[:, LANE * h:LANE * (h + 1)] = g[3][h]
        dlg[...] = g[4]
        dal[...] += g[5]
        ddt[...] += g[6]
        dnw[...] += g[7]
        dS[...] = g[8]

    rn = lambda n: N - 1 - n
    return pl.pallas_call(
        body, name=name, grid=(N,),
        in_specs=_delta_in_specs(True, N) + [
            pl.BlockSpec((H, None, Dh, Dh), lambda n: (0, rn(n), 0, 0)),
            pl.BlockSpec((H, None, C, C), lambda n: (0, rn(n), 0, 0)),
            pl.BlockSpec((C, H * LANE), lambda n: (rn(n), 1)),
        ],
        out_specs=[
            pl.BlockSpec((C, 3 * H * LANE), lambda n: (rn(n), 0)),
            pl.BlockSpec((C, H * LANE), lambda n: (rn(n), 0)),
            pl.BlockSpec((C, LANE), lambda n: (rn(n), 0)),
            pl.BlockSpec((H, 1, 1), lambda n: (0, 0, 0)),
            pl.BlockSpec((H, 1, 1), lambda n: (0, 0, 0)),
            pl.BlockSpec((1, LANE), lambda n: (0, 0)),
        ],
        out_shape=[jax.ShapeDtypeStruct((T, 3 * H * Dh), f32), jax.ShapeDtypeStruct((T, H * Dh), f32),
                   jax.ShapeDtypeStruct((T, LANE), f32), jax.ShapeDtypeStruct((H, 1, 1), f32),
                   jax.ShapeDtypeStruct((H, 1, 1), f32), jax.ShapeDtypeStruct((1, LANE), f32)],
        scratch_shapes=[pltpu.VMEM((H, Dh, Dh), f32)],
        compiler_params=_cparams("arbitrary"),
    )(c, *([proj] * H), proj, a_log, dt_bias, norm_w, s_saved, t_saved, d_oab)


def _blockdiag_fwd(name, xc, w_a, w_x, tm=512):
    T, Wd = xc.shape
    bw = Wd // LRU_BLOCKS
    tm = min(tm, T)

    def body(x_ref, wa_ref, wx_ref, oa, ox):
        xb = x_ref[...].astype(bf16)
        oa[...] = _dot(xb, wa_ref[...].astype(bf16), NN)
        ox[...] = _dot(xb, wx_ref[...].astype(bf16), NN)

    xs = pl.BlockSpec((tm, bw), lambda i, h: (i, h))
    ws = pl.BlockSpec((None, bw, bw), lambda i, h: (h, 0, 0))
    return pl.pallas_call(
        body, name=name, grid=(T // tm, LRU_BLOCKS), in_specs=[xs, ws, ws], out_specs=[xs, xs],
        out_shape=[jax.ShapeDtypeStruct((T, Wd), f32)] * 2,
        compiler_params=_cparams("parallel", "parallel"),
    )(xc, w_a, w_x)


def _blockdiag_bwd_dx(name, dpr, dpi, w_a, w_x, addend, tm=512):
    T, Wd = dpr.shape
    bw = Wd // LRU_BLOCKS
    tm = min(tm, T)

    def body(dr, di, wa_ref, wx_ref, add, o):
        o[...] = (add[...] + _dot(dr[...].astype(bf16), wa_ref[...].astype(bf16), NT)
                  + _dot(di[...].astype(bf16), wx_ref[...].astype(bf16), NT))

    xs = pl.BlockSpec((tm, bw), lambda i, h: (i, h))
    ws = pl.BlockSpec((None, bw, bw), lambda i, h: (h, 0, 0))
    return pl.pallas_call(
        body, name=name, grid=(T // tm, LRU_BLOCKS), in_specs=[xs, xs, ws, ws, xs], out_specs=xs,
        out_shape=jax.ShapeDtypeStruct((T, Wd), f32),
        compiler_params=_cparams("parallel", "parallel"),
    )(dpr, dpi, w_a, w_x, addend)


def _blockdiag_bwd_dw(name, xc, dpr, dpi, tk=512):
    T, Wd = xc.shape
    bw = Wd // LRU_BLOCKS
    tk = min(tk, T)

    def body(x_ref, dr, di, oa, ox):
        @pl.when(pl.program_id(1) == 0)
        def _():
            oa[...] = jnp.zeros_like(oa)
            ox[...] = jnp.zeros_like(ox)

        xb = x_ref[...].astype(bf16)
        oa[...] += _dot(xb, dr[...].astype(bf16), TN)
        ox[...] += _dot(xb, di[...].astype(bf16), TN)

    xs = pl.BlockSpec((tk, bw), lambda h, k: (k, h))
    ws = pl.BlockSpec((None, bw, bw), lambda h, k: (h, 0, 0))
    return pl.pallas_call(
        body, name=name, grid=(LRU_BLOCKS, T // tk), in_specs=[xs, xs, xs], out_specs=[ws, ws],
        out_shape=[jax.ShapeDtypeStruct((LRU_BLOCKS, bw, bw), f32)] * 2,
        compiler_params=_cparams("parallel", "arbitrary"),
    )(xc, dpr, dpi)


def _scan(name, a, b, reverse, tt=512, cb=512):
    T, Wd = a.shape
    tt, cb = min(tt, T), min(cb, Wd)
    nt = T // tt
    ng = tt // SUBLANE

    def body(a_ref, b_ref, *rest):
        outs, (carry, carry_a) = rest[:-2], rest[-2:]

        @pl.when(pl.program_id(1) == 0)
        def _():
            carry[...] = jnp.zeros_like(carry)
            carry_a[...] = jnp.zeros_like(carry_a)

        row = lax.broadcasted_iota(jnp.int32, (SUBLANE, cb), 0)

        def step(gi, c):
            hp, ap = c
            g = (ng - 1 - gi) if reverse else gi
            off = pl.multiple_of(g * SUBLANE, SUBLANE)
            A = a_ref[pl.ds(off, SUBLANE), :]
            B = b_ref[pl.ds(off, SUBLANE), :]
            a_first = jnp.broadcast_to(A[0:1, :], (SUBLANE, cb))
            if reverse:
                A = jnp.where(row == SUBLANE - 1, ap, pltpu.roll(A, SUBLANE - 1, axis=0))
            for s in (1, 2, 4):
                sh = (SUBLANE - s) if reverse else s
                As = pltpu.roll(A, sh, axis=0)
                Bs = pltpu.roll(B, sh, axis=0)
                valid = (row < SUBLANE - s) if reverse else (row >= s)
                B = jnp.where(valid, A * Bs + B, B)
                A = jnp.where(valid, A * As, A)
            hcur = A * hp + B
            outs[0][pl.ds(off, SUBLANE), :] = hcur
            if not reverse:
                outs[1][pl.ds(off, SUBLANE), :] = jnp.where(row == 0, hp, pltpu.roll(hcur, 1, axis=0))
            edge = hcur[0:1, :] if reverse else hcur[SUBLANE - 1:SUBLANE, :]
            return jnp.broadcast_to(edge, (SUBLANE, cb)), a_first

        carry[...], carry_a[...] = lax.fori_loop(0, ng, step, (carry[...], carry_a[...]))

    ti = (lambda c, i: (nt - 1 - i, c)) if reverse else (lambda c, i: (i, c))
    spec = pl.BlockSpec((tt, cb), ti)
    n_out = 1 if reverse else 2
    res = pl.pallas_call(
        body, name=name, grid=(Wd // cb, nt), in_specs=[spec, spec], out_specs=[spec] * n_out,
        out_shape=[jax.ShapeDtypeStruct((T, Wd), f32)] * n_out,
        scratch_shapes=[pltpu.VMEM((SUBLANE, cb), f32), pltpu.VMEM((SUBLANE, cb), f32)],
        compiler_params=_cparams("parallel", "arbitrary"),
    )(a, b)
    return res[0] if reverse else res


def _relu2_epilogue(r):
    h = jnp.maximum(r, 0.0)
    return r, h * h


def _drelu2_epilogue(r, a):
    return (r * (2.0 * jnp.maximum(a, 0.0)),)


def _add_epilogue(r, e):
    return (r + e,)


def _merge_cols(name, g, tm=256):
    _, L, R, s = g.shape

    def body(g_ref, o_ref):
        for d in range(N_DEV):
            o_ref[:, s * d:s * (d + 1)] = g_ref[d].astype(bf16)
        o_ref[:, N_DEV * s:] = jnp.zeros((tm, HYB_PROJ_PAD - N_DEV * s), bf16)

    return pl.pallas_call(
        body, name=name, grid=(L, R // tm),
        in_specs=[pl.BlockSpec((N_DEV, None, tm, s), lambda l, i: (0, l, i, 0))],
        out_specs=pl.BlockSpec((None, tm, HYB_PROJ_PAD), lambda l, i: (l, i, 0)),
        out_shape=jax.ShapeDtypeStruct((L, R, HYB_PROJ_PAD), bf16),
        compiler_params=_cparams("parallel", "parallel"),
    )(g)


def _split_cols(name, dw, tm=256):
    R = dw.shape[0]
    s = HYB_PROJ // N_DEV

    def body(g_ref, o_ref):
        for d in range(N_DEV):
            o_ref[d] = g_ref[:, s * d:s * (d + 1)].astype(bf16)

    return pl.pallas_call(
        body, name=name, grid=(R // tm,),
        in_specs=[pl.BlockSpec((tm, HYB_PROJ_PAD), lambda i: (i, 0))],
        out_specs=pl.BlockSpec((N_DEV, tm, s), lambda i: (0, i, 0)),
        out_shape=jax.ShapeDtypeStruct((N_DEV, R, s), bf16),
        compiler_params=_cparams("parallel"),
    )(dw)


def _rows_to_dev(dw):
    nb, r, c = dw.shape
    t = dw.reshape(nb, N_DEV, r // N_DEV, c)
    return jnp.moveaxis(t, 1, 0).reshape(N_DEV, nb * (r // N_DEV), c).astype(bf16)


def _ln_epilogue(r, x, g, b):
    return r, _ln_res_fn(x, r, g, b)[0]


def _hybrid_fwd(tag, x, W, j, cos, sin, ln, before_out):
    proj = _mm(f"{tag}_proj", x, W["hyb_w_in"][j], "nn", b_kind="lead", b_lead=0)
    o_a = _attn_fwd(f"{tag}_attn", proj, cos, sin, W["hyb_sinks"][j][None, :])
    c = _conv_fwd(f"{tag}_conv", proj, CB_CONV, 12, W["hyb_conv_w"][j], None)
    o_b, s_saved, t_saved = _delta_fwd(f"{tag}_delta", c, proj, W["hyb_a_log"][j].reshape(B_HEADS, 1, 1),
                                       W["hyb_dt_bias"][j].reshape(B_HEADS, 1, 1), W["hyb_norm_w"][j][None, :])
    o_ab = jnp.concatenate([o_a, o_b], axis=1)
    before_out(o_ab)
    mix, x1 = _mm(f"{tag}_out", o_ab, W["hyb_w_out"][j], "nn", b_kind="lead", b_lead=0, epilogue=_ln_epilogue,
                  extras=(x,), params=ln, out_dtypes=(f32, f32), tm=512)
    return mix, x1, (proj, c, s_saved, t_saved, o_ab)


def _hybrid_bwd(tag, x, dmix, addend, W, j, cos, sin, saved, G, send_early):
    proj, c, s_saved, t_saved, o_ab = saved
    T = x.shape[0]
    d_oab = _mm(f"{tag}_dout", dmix, W["hyb_w_out"][j], "nt", b_kind="lead", b_lead=0)
    G["hyb_w_out"][j] = _mm(f"{tag}_dwout", o_ab, dmix, "tn", out_dtypes=(bf16,)).reshape(N_DEV, -1, D_MODEL)
    sinks = W["hyb_sinks"][j][None, :] + send_early({("hyb_w_out", j): G["hyb_w_out"][j]})
    dq, dk, dv, dsinks = _attn_bwd(f"{tag}_dattn", proj, cos, sin, sinks, d_oab)
    a_log = W["hyb_a_log"][j].reshape(B_HEADS, 1, 1)
    dt_bias = W["hyb_dt_bias"][j].reshape(B_HEADS, 1, 1)
    dc, dz, dlg, dal, ddt, dnw = _delta_bwd(f"{tag}_ddelta", c, proj, a_log, dt_bias, W["hyb_norm_w"][j][None, :],
                                            s_saved, t_saved, d_oab)
    dconv_in, dconv_w, _ = _conv_bwd(f"{tag}_dconv", dc, proj, CB_CONV, 12, W["hyb_conv_w"][j])
    dproj = jnp.concatenate([dq, dk, dv, dconv_in, dz, dlg,
                             jnp.zeros((T, HYB_PROJ_PAD - (CB_LG + 1) * LANE), f32)], axis=1)
    dx = _mm(f"{tag}_dx", dproj, W["hyb_w_in"][j], "nt", b_kind="lead", b_lead=0, epilogue=_add_epilogue,
             extras=(addend,))
    G["hyb_w_in"][j] = _split_cols(f"{tag}_dwin_split", _mm(f"{tag}_dwin", x, dproj, "tn"))
    G["hyb_sinks"][j] = dsinks[0]
    G["hyb_conv_w"][j] = dconv_w
    G["hyb_a_log"][j] = dal.reshape(B_HEADS)
    G["hyb_dt_bias"][j] = ddt.reshape(B_HEADS)
    G["hyb_norm_w"][j] = dnw[0]
    return dx


def _rec_fwd(tag, x, W, j, ln, before_out):
    Wd = D_MODEL
    proj = _mm(f"{tag}_proj", x, W["rec_w_in"][j], "nn", b_kind="devcol", b_lead=0)
    xc = _conv_fwd(f"{tag}_conv", proj, 0, Wd // LANE, W["rec_conv_w"][j], W["rec_conv_b"][j][None, :])
    pre_r, pre_i = _blockdiag_fwd(f"{tag}_gates", xc, W["rec_w_a"][j][0], W["rec_w_x"][j][0])
    pars = [W["rec_b_a"][j][None, :], W["rec_b_x"][j][None, :], W["rec_lambda"][j][None, :]]
    a, b = _tl_fwd(f"{tag}_pre", _rglru_pre_fn, [(pre_r, 0, Wd), (pre_i, 0, Wd), (xc, 0, Wd)], pars, [Wd, Wd], [f32, f32])
    h, h_prev = _scan(f"{tag}_scan", a, b, False)
    (hg,) = _tl_fwd(f"{tag}_gate", _rec_gate_fn, [(h, 0, Wd), (proj, Wd // LANE, Wd)], [], [Wd], [f32])
    before_out(hg)
    mix, x1 = _mm(f"{tag}_out", hg, W["rec_w_out"][j], "nn", b_kind="lead", b_lead=0, epilogue=_ln_epilogue,
                  extras=(x,), params=ln, out_dtypes=(f32, f32), tm=512)
    return mix, x1, (proj, xc, pre_r, pre_i, a, h, h_prev, hg)


def _rec_bwd(tag, x, dmix, addend, W, j, saved, G, send_early):
    proj, xc, pre_r, pre_i, a, h, h_prev, hg = saved
    Wd = D_MODEL
    dhg = _mm(f"{tag}_dout", dmix, W["rec_w_out"][j], "nt", b_kind="lead", b_lead=0)
    G["rec_w_out"][j] = _mm(f"{tag}_dwout", hg, dmix, "tn", out_dtypes=(bf16,)).reshape(N_DEV, -1, D_MODEL)
    sent = send_early({("rec_w_out", j): G["rec_w_out"][j]})
    (dh, dgate), _ = _tl_bwd(f"{tag}_dgate", _rec_gate_fn, [(h, 0, Wd), (proj, Wd // LANE, Wd)], [], [(dhg, 0, Wd)])
    lam_t = _scan(f"{tag}_dscan", a, dh, True)
    pars = [W["rec_b_a"][j][None, :] + sent, W["rec_b_x"][j][None, :], W["rec_lambda"][j][None, :]]
    (dpr, dpi, dxc1), (db_a, db_x, dlam) = _tl_bwd(
        f"{tag}_dpre", _rglru_pre_fn, [(pre_r, 0, Wd), (pre_i, 0, Wd), (xc, 0, Wd)], pars,
        [(lam_t, 0, Wd), (h_prev, 0, Wd)], cot_fn=lambda lt, hp: (lt * hp, lt))
    dxc = _blockdiag_bwd_dx(f"{tag}_dgates_dx", dpr, dpi, W["rec_w_a"][j][0], W["rec_w_x"][j][0], dxc1)
    dwa, dwx = _blockdiag_bwd_dw(f"{tag}_dgates_dw", xc, dpr, dpi)
    G["rec_w_a"][j], G["rec_w_x"][j] = _rows_to_dev(dwa), _rows_to_dev(dwx)
    dxr, dconv_w, dconv_b = _conv_bwd(f"{tag}_dconv", dxc, proj, 0, Wd // LANE, W["rec_conv_w"][j])
    dproj = jnp.concatenate([dxr, dgate], axis=1)
    dx = _mm(f"{tag}_dx", dproj, W["rec_w_in"][j], "nt", b_kind="devcol", b_lead=0, epilogue=_add_epilogue,
             extras=(addend,))
    G["rec_w_in"][j] = _mm(f"{tag}_dwin", x, dproj, "tn", o_kind="devcol", out_dtypes=(bf16,))
    G["rec_conv_w"][j] = dconv_w
    G["rec_conv_b"][j] = dconv_b
    G["rec_b_a"][j] = db_a[0]
    G["rec_b_x"][j] = db_x[0]
    G["rec_lambda"][j] = dlam[0]
    return dx


def _local_step(x, target, W, load_layer, grads_ready):
    T = x.shape[0]
    cos, sin = _rope_tables(T)
    saved = []
    for layer in range(DEPTH):
        j = layer // 2
        tag = f"L{layer}"
        load_layer(layer, 0, x)
        ln1 = (W["ln1_g"][layer][None, :], W["ln1_b"][layer][None, :])
        before_out = functools.partial(load_layer, layer, 1)
        if layer % 2 == 0:
            mix, x1, sv = _hybrid_fwd(tag, x, W, j, cos, sin, ln1, before_out)
        else:
            mix, x1, sv = _rec_fwd(tag, x, W, j, ln1, before_out)
        load_layer(layer, 2, x1)
        a, h2 = _mm(f"{tag}_mlp1", x1, W["mlp_w1"][layer], "nn", b_kind="devcol", b_lead=0, epilogue=_relu2_epilogue,
                    out_dtypes=(f32, bf16))
        ln2 = (W["ln2_g"][layer][None, :], W["ln2_b"][layer][None, :])
        y, x2 = _mm(f"{tag}_mlp2", h2, W["mlp_w2"][layer], "nn", b_kind="devrow", b_lead=0, epilogue=_ln_epilogue,
                    extras=(x1,), params=ln2, out_dtypes=(f32, f32))
        saved.append((x, sv, mix, x1, a, h2, y))
        x = x2
    loss, dx = _loss_head(x, target)

    G = {k: [None] * (DEPTH if k.startswith(("ln", "mlp")) else DEPTH // 2) for k in (
        "hyb_w_in", "hyb_sinks", "hyb_conv_w", "hyb_a_log", "hyb_dt_bias", "hyb_norm_w", "hyb_w_out",
        "rec_w_in", "rec_conv_w", "rec_conv_b", "rec_w_a", "rec_b_a", "rec_w_x", "rec_b_x", "rec_lambda", "rec_w_out",
        "ln1_g", "ln1_b", "mlp_w1", "mlp_w2", "ln2_g", "ln2_b")}
    order = jnp.zeros((1, 1), f32)
    for layer in reversed(range(DEPTH)):
        j = layer // 2
        tag = f"L{layer}"
        x0, sv, mix, x1, a, h2, y = saved[layer]
        ln2 = [W["ln2_g"][layer][None, :] + order, W["ln2_b"][layer][None, :]]
        (dx1_a, dy), (dg2, db2) = _tl_bwd(f"{tag}_dln2", _ln_res_fn, [(x1, 0, D_MODEL), (y, 0, D_MODEL)], ln2,
                                          [(dx, 0, D_MODEL)])
        G["ln2_g"][layer], G["ln2_b"][layer] = dg2[0], db2[0]
        da = _mm(f"{tag}_dmlp2", dy, W["mlp_w2"][layer], "nt", b_kind="devrow", b_lead=0, epilogue=_drelu2_epilogue,
                 extras=(a,), out_dtypes=(bf16,))
        G["mlp_w2"][layer] = _mm(f"{tag}_dw2", h2, dy, "tn", out_dtypes=(bf16,)).reshape(N_DEV, -1, D_MODEL)
        dx1 = _mm(f"{tag}_dmlp1", da, W["mlp_w1"][layer], "nt", b_kind="devcol", b_lead=0, epilogue=_add_epilogue,
                  extras=(dx1_a,))
        G["mlp_w1"][layer] = _mm(f"{tag}_dw1", x1, da, "tn", o_kind="devcol", out_dtypes=(bf16,))
        ln1 = [W["ln1_g"][layer][None, :], W["ln1_b"][layer][None, :]]
        (dx0_a, dmix), (dg1, db1) = _tl_bwd(f"{tag}_dln1", _ln_res_fn, [(x0, 0, D_MODEL), (mix, 0, D_MODEL)], ln1,
                                            [(dx1, 0, D_MODEL)])
        G["ln1_g"][layer], G["ln1_b"][layer] = dg1[0], db1[0]
        early = functools.partial(grads_ready, f"l{layer}_early",
                                  {(k, layer): G[k][layer] for k in ("mlp_w1", "mlp_w2")})
        if layer % 2 == 0:
            dx = _hybrid_bwd(tag, x0, dmix, dx0_a, W, j, cos, sin, sv, G, early)
        else:
            dx = _rec_bwd(tag, x0, dmix, dx0_a, W, j, sv, G, early)
        order = grads_ready(f"l{layer}_late", {}, {(k, i): G[k][i] for k, i in _layer_weights(layer)[:-2]
                                                  if not k.endswith("w_out")})
    big = {k for k, _ in BIG}
    return loss, dx, {k: jnp.stack(v) for k, v in G.items() if k not in big}


def _layer_weights(layer):
    j = layer // 2
    mixer = ["hyb_w_in", "hyb_w_out"] if layer % 2 == 0 else ["rec_w_in", "rec_w_out", "rec_w_a", "rec_w_x"]
    return [(k, j) for k in mixer] + [("mlp_w1", layer), ("mlp_w2", layer)]


def _my_coords():
    return lax.axis_index("x"), lax.axis_index("y"), lax.axis_index("c")


def _all_gather(name, arrays):
    na = len(arrays)

    def body(*refs):
        x_refs, out_refs = refs[:na], refs[na:2 * na]
        send_sems, recv_sems, local_sems = refs[2 * na:]
        x, y, c = _my_coords()
        me, sibling = (x, y, c), (x, y, 1 - c)
        chips = [(1 - x, y), (x, 1 - y), (1 - x, 1 - y)]

        def blk(a, px, py, pc):
            return out_refs[a].at[4 * px + 2 * py + pc]

        def copy(a, k, block, to, src=None):
            return pltpu.make_async_remote_copy(
                src_ref=blk(a, *block) if src is None else src, dst_ref=blk(a, *block),
                send_sem=send_sems.at[a, k], recv_sem=recv_sems.at[a, k],
                device_id=to, device_id_type=pl.DeviceIdType.MESH)

        mine = [pltpu.make_async_copy(x_refs[a], blk(a, *me), local_sems.at[a]) for a in range(na)]
        for cp in mine:
            cp.start()
        first = []
        for a in range(na):
            first.append(copy(a, 0, me, sibling, src=x_refs[a]))
            first += [copy(a, 1 + j, me, (*chip, c), src=x_refs[a]) for j, chip in enumerate(chips)]
        for cp in first:
            cp.start()
        passed = []
        for a in range(na):
            for j, chip in enumerate(chips):
                copy(a, 1 + j, (*chip, c), me).wait_recv()
                passed.append(copy(a, 4 + j, (*chip, c), sibling))
                passed[-1].start()
        for a in range(na):
            copy(a, 0, sibling, me).wait_recv()
            for j, chip in enumerate(chips):
                copy(a, 4 + j, (*chip, 1 - c), me).wait_recv()
        for cp in first + passed:
            cp.wait_send()
        for cp in mine:
            cp.wait()

    return pl.pallas_call(
        body, name=name,
        out_shape=[jax.ShapeDtypeStruct((N_DEV,) + a.shape, a.dtype) for a in arrays],
        in_specs=[pl.BlockSpec(memory_space=pl.ANY)] * na,
        out_specs=[pl.BlockSpec(memory_space=pl.ANY)] * na,
        scratch_shapes=[pltpu.SemaphoreType.DMA((na, 7)), pltpu.SemaphoreType.DMA((na, 7)),
                        pltpu.SemaphoreType.DMA((na,))],
    )(*arrays)


_HBM = pl.BlockSpec(memory_space=pltpu.HBM)
_SEM = pl.BlockSpec(memory_space=pltpu.SEMAPHORE)


def _flip(k, x, y, c):
    return ((1 - x) if k & 4 else x, (1 - y) if k & 2 else y, (1 - c) if k & 1 else c)


_PEERS = {"gather": (1, 2, 4, 6), "scatter": (1, 2, 3, 4, 5, 6, 7)}


def _push_copies(kind, x_refs, land_refs, send_sems, recv_sems, local_sems):
    x, y, c = _my_coords()
    me = 4 * x + 2 * y + c
    peers = _PEERS[kind]
    remote, local = [], []
    for a in range(len(x_refs)):
        local.append(pltpu.make_async_copy(x_refs[a] if kind == "gather" else x_refs[a].at[me], land_refs[a].at[me],
                                           local_sems.at[a]))
        for n, k in enumerate(peers):
            px, py, pc = _flip(k, x, y, c)
            remote.append(pltpu.make_async_remote_copy(
                src_ref=x_refs[a] if kind == "gather" else x_refs[a].at[4 * px + 2 * py + pc],
                dst_ref=land_refs[a].at[me],
                send_sem=send_sems.at[a * len(peers) + n], recv_sem=recv_sems.at[a * len(peers) + n],
                device_id=(px, py, pc), device_id_type=pl.DeviceIdType.MESH))
    return remote, local


def _pass_to_sibling(name, lands):
    na = len(lands)
    chips = (2, 4, 6)

    def body(*refs):
        out_refs, send_sems, recv_sems = refs[na:2 * na], refs[2 * na], refs[2 * na + 1]
        x, y, c = _my_coords()
        cps = []
        for a in range(na):
            for n, k in enumerate(chips):
                px, py, _ = _flip(k, x, y, c)
                cps.append(pltpu.make_async_remote_copy(
                    src_ref=out_refs[a].at[4 * px + 2 * py + c], dst_ref=out_refs[a].at[4 * px + 2 * py + c],
                    send_sem=send_sems.at[a * 3 + n], recv_sem=recv_sems.at[a * 3 + n],
                    device_id=(x, y, 1 - c), device_id_type=pl.DeviceIdType.MESH))
        for cp in cps:
            cp.start()
        for a in range(na):
            for n, k in enumerate(chips):
                px, py, _ = _flip(k, x, y, c)
                blk = out_refs[a].at[4 * px + 2 * py + (1 - c)]
                pltpu.make_async_remote_copy(src_ref=blk, dst_ref=blk, send_sem=send_sems.at[a * 3 + n],
                                             recv_sem=recv_sems.at[a * 3 + n], device_id=(x, y, 1 - c),
                                             device_id_type=pl.DeviceIdType.MESH).wait_recv()
        for cp in cps:
            cp.wait_send()

    return pl.pallas_call(
        body, name=name,
        out_shape=[jax.ShapeDtypeStruct(l.shape, l.dtype) for l in lands],
        in_specs=[pl.BlockSpec(memory_space=pl.ANY)] * na,
        out_specs=[pl.BlockSpec(memory_space=pl.ANY)] * na,
        input_output_aliases={a: a for a in range(na)},
        scratch_shapes=[pltpu.SemaphoreType.DMA((3 * na,)), pltpu.SemaphoreType.DMA((3 * na,))],
    )(*lands)


_SIDE_EFFECT = pltpu.CompilerParams(has_side_effects=pltpu.SideEffectType.DATAFLOW_SIDE_EFFECTING)


def _push_start(name, kind, srcs, lands):
    na = len(srcs)

    def body(*refs):
        remote, local = _push_copies(kind, refs[:na], refs[na:2 * na], *refs[2 * na:2 * na + 3])
        for cp in remote + local:
            cp.start()
        token = refs[-1]
        token[...] = jnp.zeros_like(token)

    arrays = list(srcs) + list(lands)
    n_remote = na * len(_PEERS[kind])
    res = pl.pallas_call(
        body, name=name,
        out_shape=(pltpu.SemaphoreType.DMA((n_remote,)), pltpu.SemaphoreType.DMA((n_remote,)),
                   pltpu.SemaphoreType.DMA((na,)), *[pltpu.HBM(t.shape, t.dtype) for t in arrays],
                   jax.ShapeDtypeStruct((SUBLANE, LANE), f32)),
        in_specs=[_HBM] * (2 * na),
        out_specs=(_SEM, _SEM, _SEM, *[_HBM] * (2 * na), pl.BlockSpec(memory_space=pltpu.VMEM)),
        input_output_aliases={i: 3 + i for i in range(2 * na)},
        compiler_params=_SIDE_EFFECT,
    )(*[pltpu.with_memory_space_constraint(t, pltpu.HBM) for t in arrays])
    return list(res[:3]), res[3:3 + na], res[3 + na:3 + 2 * na], res[-1][:1, :1]


def _push_wait(name, kind, sems, srcs, lands, after):
    na = len(srcs)

    def body(*refs):
        remote, local = _push_copies(kind, refs[:na], refs[na:2 * na], *refs[2 * na:2 * na + 3])
        for cp in remote:
            cp.wait_send()
            cp.wait_recv()
        for cp in local:
            cp.wait()

    arrays = list(srcs) + list(lands)
    res = pl.pallas_call(
        body, name=name,
        out_shape=tuple(pltpu.HBM(t.shape, t.dtype) for t in arrays),
        in_specs=[_HBM] * (2 * na) + [_SEM] * 3 + [pl.BlockSpec(memory_space=pl.ANY)],
        out_specs=tuple([_HBM] * (2 * na)),
        input_output_aliases={i: i for i in range(2 * na)},
        compiler_params=_SIDE_EFFECT,
    )(*arrays, *sems, after)
    return res[na:]


def _sum_blocks(name, land):
    _, R, n = land.shape
    tr = R

    def body(l_ref, o_ref):
        acc = l_ref[0].astype(f32)
        for s in range(1, N_DEV):
            acc = acc + l_ref[s].astype(f32)
        o_ref[...] = acc

    return pl.pallas_call(
        body, name=name, grid=(R // tr,),
        in_specs=[pl.BlockSpec((N_DEV, tr, n), lambda i: (0, i, 0))],
        out_specs=pl.BlockSpec((tr, n), lambda i: (i, 0)),
        out_shape=jax.ShapeDtypeStruct((R, n), f32),
        compiler_params=_cparams("parallel"),
    )(land)


def _adamw(name, w, g, m, v):
    shape = w.shape
    last = shape[-1]
    rows = math.prod(shape[:-1])
    tm = 256 if rows % 256 == 0 and rows > 256 else rows
    w2, g2, m2, v2 = (t.reshape(rows, last) for t in (w, g, m, v))

    def body(w_ref, g_ref, m_ref, v_ref, d_ref, mo_ref, vo_ref):
        gg = g_ref[...]
        mn = ADAM_B1 * m_ref[...] + (1.0 - ADAM_B1) * gg
        vn = ADAM_B2 * v_ref[...] + (1.0 - ADAM_B2) * jnp.square(gg)
        m_hat = mn / (1.0 - ADAM_B1 ** ADAM_STEP)
        v_hat = vn / (1.0 - ADAM_B2 ** ADAM_STEP)
        d_ref[...] = -ADAM_LR * (m_hat / (jnp.sqrt(v_hat) + ADAM_EPS) + ADAM_WD * w_ref[...])
        mo_ref[...] = mn
        vo_ref[...] = vn

    spec = pl.BlockSpec((tm, last), lambda i: (i, 0))
    d, mn, vn = pl.pallas_call(
        body, name=name, grid=(rows // tm,), in_specs=[spec] * 4, out_specs=[spec] * 3,
        out_shape=[jax.ShapeDtypeStruct((rows, last), f32)] * 3,
        compiler_params=_cparams("parallel"),
    )(w2, g2, m2, v2)
    return d.reshape(shape), mn.reshape(shape), vn.reshape(shape)


def _adamw_land(name, lands, w, m, v, tm=256):
    L = len(lands)
    _, R, C = lands[0].shape
    tm = min(tm, R)

    def body(*refs):
        l_refs, (w_ref, m_ref, v_ref, g_ref, d_ref, mo_ref, vo_ref) = refs[:L], refs[L:]
        for k in range(L):
            @pl.when(pl.program_id(0) == k)
            def _(k=k):
                gg = l_refs[k][0].astype(f32)
                for s in range(1, N_DEV):
                    gg = gg + l_refs[k][s].astype(f32)
                g_ref[...] = gg
                mn = ADAM_B1 * m_ref[...] + (1.0 - ADAM_B1) * gg
                vn = ADAM_B2 * v_ref[...] + (1.0 - ADAM_B2) * jnp.square(gg)
                m_hat = mn / (1.0 - ADAM_B1 ** ADAM_STEP)
                v_hat = vn / (1.0 - ADAM_B2 ** ADAM_STEP)
                d_ref[...] = -ADAM_LR * (m_hat / (jnp.sqrt(v_hat) + ADAM_EPS) + ADAM_WD * w_ref[...])
                mo_ref[...] = mn
                vo_ref[...] = vn

    land_specs = [pl.BlockSpec((N_DEV, tm, C), lambda l, i, k=k: (0, jnp.where(l == k, i, 0), 0)) for k in range(L)]
    spec = pl.BlockSpec((None, tm, C), lambda l, i: (l, i, 0))
    return pl.pallas_call(
        body, name=name, grid=(L, R // tm),
        in_specs=land_specs + [spec] * 3,
        out_specs=[spec] * 4,
        out_shape=[jax.ShapeDtypeStruct((L, R, C), f32)] * 4,
        compiler_params=_cparams("arbitrary", "arbitrary"),
    )(*lands, w, m, v)


BIG = [("hyb_w_in", 2), ("hyb_w_out", 1), ("rec_w_in", 2), ("rec_w_out", 1), ("rec_w_a", 2), ("rec_w_x", 2),
       ("mlp_w1", 2), ("mlp_w2", 1)]
SMALL = [("hyb_conv_w", 2), ("rec_conv_w", 2), ("rec_conv_b", 1), ("rec_b_a", 1), ("rec_b_x", 1), ("rec_lambda", 1)]
REPL = ["hyb_sinks", "hyb_a_log", "hyb_dt_bias", "hyb_norm_w", "ln1_g", "ln1_b", "ln2_g", "ln2_b"]
WEIGHTS = ["hyb_w_in", "hyb_sinks", "hyb_conv_w", "hyb_a_log", "hyb_dt_bias", "hyb_norm_w", "hyb_w_out", "rec_w_in",
           "rec_conv_w", "rec_conv_b", "rec_w_a", "rec_b_a", "rec_w_x", "rec_b_x", "rec_lambda", "rec_w_out",
           "ln1_g", "ln1_b", "mlp_w1", "mlp_w2", "ln2_g", "ln2_b"]


def _pack_rows(parts, dtype, row_mult):
    lead = parts[0].shape[:-1]
    flat = jnp.concatenate([p.astype(dtype) for p in parts], axis=-1)
    n = flat.shape[-1]
    unit = row_mult * LANE
    pad = (-n) % unit
    if pad:
        flat = jnp.concatenate([flat, jnp.zeros(lead + (pad,), dtype)], axis=-1)
    return flat.reshape(lead + ((n + pad) // LANE, LANE))


def _gather_full(gathered, shard_shapes, table):
    flat = gathered.reshape(N_DEV, -1)
    out, off = {}, 0
    for name, ax in table:
        shp = shard_shapes[name]
        n = math.prod(shp)
        arr = flat[:, off:off + n].reshape((N_DEV,) + shp)
        off += n
        arr = jnp.moveaxis(arr, 0, ax)
        out[name] = arr.reshape(shp[:ax] + (N_DEV * shp[ax],) + shp[ax + 1:])
    return out


def _matmul_layouts(tag, gw):
    out = {}
    bw = D_MODEL // LRU_BLOCKS
    for k, g in gw.items():
        L = g.shape[1]
        if k == "hyb_w_in":
            out[k] = _merge_cols(f"{tag}_w_in_merge", g)
        elif k in ("hyb_w_out", "rec_w_out"):
            out[k] = jnp.swapaxes(g, 0, 1).reshape(L, D_MODEL, D_MODEL)
        elif k in ("rec_w_a", "rec_w_x"):
            out[k] = jnp.moveaxis(g, 0, 2).reshape(L, LRU_BLOCKS, bw, bw)
        else:
            out[k] = g
    return out


def kernel(x, hyb_w_in, hyb_sinks, hyb_conv_w, hyb_a_log, hyb_dt_bias, hyb_norm_w, hyb_w_out, rec_w_in, rec_conv_w, rec_conv_b, rec_w_a, rec_b_a, rec_w_x, rec_b_x, rec_lambda, rec_w_out, ln1_g, ln1_b, mlp_w1, mlp_w2, ln2_g, ln2_b, loss_target, m_hyb_w_in, m_hyb_sinks, m_hyb_conv_w, m_hyb_a_log, m_hyb_dt_bias, m_hyb_norm_w, m_hyb_w_out, m_rec_w_in, m_rec_conv_w, m_rec_conv_b, m_rec_w_a, m_rec_b_a, m_rec_w_x, m_rec_b_x, m_rec_lambda, m_rec_w_out, m_ln1_g, m_ln1_b, m_mlp_w1, m_mlp_w2, m_ln2_g, m_ln2_b, v_hyb_w_in, v_hyb_sinks, v_hyb_conv_w, v_hyb_a_log, v_hyb_dt_bias, v_hyb_norm_w, v_hyb_w_out, v_rec_w_in, v_rec_conv_w, v_rec_conv_b, v_rec_w_a, v_rec_b_a, v_rec_w_x, v_rec_b_x, v_rec_lambda, v_rec_w_out, v_ln1_g, v_ln1_b, v_mlp_w1, v_mlp_w2, v_ln2_g, v_ln2_b):
    args = locals()
    w = {k: args[k] for k in WEIGHTS}
    m = {k: args["m_" + k] for k in WEIGHTS}
    v = {k: args["v_" + k] for k in WEIGHTS}
    shard_shapes = {k: tuple(t.shape) for k, t in w.items()}
    xi, yi, ci = _my_coords()
    me = 4 * xi + 2 * yi + ci

    in_flight = {}

    def install(tag, names, got):
        for (k, i), arr in zip(names, _matmul_layouts(tag, {k: g for (k, _), g in zip(names, got)}).values()):
            W[k][i] = arr

    def start_gather(tag, names):
        srcs = [w[k][i:i + 1].astype(bf16) for k, i in names]
        *pending, zero = _push_start(f"gather_{tag}_start", "gather", srcs,
                                     [lax.empty((N_DEV,) + s.shape, bf16) for s in srcs])
        in_flight[tag] = (names, pending)
        return zero

    def finish_gather(tag, after):
        names, pending = in_flight.pop(tag)
        half = _push_wait(f"gather_{tag}_wait", "gather", *pending, after)
        install(tag, names, _pass_to_sibling(f"gather_{tag}_pass", half))

    def started(k, zero):
        W[k] = W[k] + zero

    def mixer_w(layer):
        return _layer_weights(layer)[:-2]

    def mlp_w(layer):
        return _layer_weights(layer)[-2:]

    gathered0 = _all_gather("gather_first", [w[k][i:i + 1].astype(bf16) for k, i in mixer_w(0)]
                            + [_pack_rows([w[k].reshape(-1) for k, _ in SMALL], f32, SUBLANE)])
    W = _gather_full(gathered0[-1], shard_shapes, SMALL)
    W.update({k: w[k] for k in REPL})
    W.update({k: {} for k, _ in BIG})
    install("l0a", mixer_w(0), gathered0[:-1])
    started("hyb_sinks", start_gather("l0b", mlp_w(0)) + start_gather("l1a", mixer_w(1)))

    def load_layer(layer, stage, after):
        if stage == 0:
            if layer > 0:
                finish_gather(f"l{layer}a", after)
            if 0 < layer < DEPTH - 1:
                started("hyb_sinks" if layer % 2 == 0 else "rec_conv_b",
                        start_gather(f"l{layer + 1}a", mixer_w(layer + 1)))
        if stage == 2:
            finish_gather(f"l{layer}b", after)
            if layer < DEPTH - 1:
                started("ln2_g", start_gather(f"l{layer + 1}b", mlp_w(layer + 1)))

    grads_in_flight = {}

    def grads_ready(tag, a, b):
        g = {**a, **b}
        srcs = list(g.values())
        *pending, zero = _push_start(f"scatter_{tag}_start", "scatter", srcs, [lax.empty(s.shape, bf16) for s in srcs])
        grads_in_flight[tag] = (list(g.keys()), pending)
        return zero

    loss_local, grad_x, G = _local_step(x[0], loss_target[0], W, load_layer, grads_ready)
    loss = lax.psum(loss_local, MESH_AXES)

    landed = {}

    def land(tag, after):
        keys, pending = grads_in_flight[tag]
        landed.update(zip(keys, _push_wait(f"scatter_{tag}_wait", "scatter", *pending, after)))

    tags = list(grads_in_flight)
    for tag in tags[:-1]:
        land(tag, grad_x)
    rest = _pack_rows([G[k].reshape(-1) for k, _ in SMALL] + [G[k].reshape(-1) for k in REPL], f32, SUBLANE)
    g_rest = _sum_blocks("sum_rest", _all_gather("gather_rest", [rest])[0]).reshape(-1)

    grads, delta, new_m, new_v = {}, {}, {}, {}

    def adamw_big(k):
        shp = shard_shapes[k]
        s3 = (shp[0], math.prod(shp[1:-1]), shp[-1])
        lands = [landed[(k, i)].reshape((N_DEV,) + s3[1:]) for i in range(shp[0])]
        res = _adamw_land("adamw_" + k, lands, w[k].reshape(s3), m[k].reshape(s3), v[k].reshape(s3))
        grads[k], delta[k], new_m[k], new_v[k] = (r.reshape(shp) for r in res)

    late = {k for k, _ in grads_in_flight[tags[-1]][0]}
    for k in [k for k, _ in BIG if k not in late]:
        adamw_big(k)
        done = new_v[k]
    land(tags[-1], done)
    for k in [k for k, _ in BIG if k in late]:
        adamw_big(k)
    off = 0
    for k, ax in SMALL:
        full_shape = G[k].shape
        n = math.prod(full_shape)
        full = g_rest[off:off + n].reshape(full_shape)
        off += n
        s = shard_shapes[k][ax]
        grads[k] = lax.dynamic_slice_in_dim(full, me * s, s, axis=ax)
    for k in REPL:
        n = math.prod(shard_shapes[k])
        grads[k] = g_rest[off:off + n].reshape(shard_shapes[k])
        off += n

    for k in [k for k, _ in SMALL] + REPL:
        delta[k], new_m[k], new_v[k] = _adamw("adamw_" + k, w[k], grads[k], m[k], v[k])

    return (loss, grad_x[None], *[grads[k] for k in WEIGHTS], *[delta[k] for k in WEIGHTS],
            *[new_m[k] for k in WEIGHTS], *[new_v[k] for k in WEIGHTS])
```

```python
import functools
import math

import jax
import jax.numpy as jnp
from jax import lax
from jax.experimental import pallas as pl
from jax.experimental.pallas import tpu as pltpu

f32 = jnp.float32
bf16 = jnp.bfloat16

N_DEV = 8
D_MODEL = 1024
DEPTH = 4
A_HEAD_DIM = 64
A_Q_HEADS = 8
WINDOW = 128
ROPE_THETA = 10000.0
B_HEADS = 4
B_HEAD_DIM = 128
B_CHUNK = 64
LRU_BLOCKS = 4
LRU_C = 8.0
D_FF = 4 * D_MODEL
HYB_PROJ = 2824
HYB_PROJ_PAD = 3072
DN_ALPHA = (2 * DEPTH) ** 0.25
LN_EPS = 1e-5
NORM_EPS = 1e-6
ADAM_LR = 0.001
ADAM_B1 = 0.9
ADAM_B2 = 0.999
ADAM_EPS = 1e-08
ADAM_WD = 0.01
ADAM_STEP = 10

LANE = 128
SUBLANE = 8
VMEM_LIMIT = 48 * 1024 * 1024

CB_QA, CB_KA, CB_VA, CB_CONV, CB_Z, CB_LG = 0, 4, 5, 6, 18, 22

MESH_AXES = ("x", "y", "c")


def _cparams(*sem):
    return pltpu.CompilerParams(dimension_semantics=sem, vmem_limit_bytes=VMEM_LIMIT)


def _dot(a, b, dims, precision=None):
    return lax.dot_general(a, b, (dims, ((), ())), preferred_element_type=f32, precision=precision)


NN = ((1,), (0,))
NT = ((1,), (1,))
TN = ((0,), (0,))


def _mat_spec(arr, kind, lead, br, bc, rb, cb):
    if kind == "plain":
        return pl.BlockSpec((br, bc), lambda i, j, k: (rb(i, j, k), cb(i, j, k)))
    if kind == "lead":
        return pl.BlockSpec((None, br, bc), lambda i, j, k: (lead, rb(i, j, k), cb(i, j, k)))
    if kind == "devcol":
        assert bc == arr.shape[-1]
        return pl.BlockSpec((None, None, br, bc), lambda i, j, k: (cb(i, j, k), lead, rb(i, j, k), 0))
    assert kind == "devrow" and br == arr.shape[-2]
    return pl.BlockSpec((None, None, br, bc), lambda i, j, k: (rb(i, j, k), lead, 0, cb(i, j, k)))


def _mm(name, a, b, mode, *, b_kind="plain", b_lead=0, o_kind="plain", epilogue=None, extras=(), params=(),
        out_dtypes=(f32,), tm=1024, tn=1024, tk=None):
    if tk is None:
        tk = 512 if mode == "tn" else 1024
    if b_kind in ("plain", "lead"):
        b_rows, b_cols = b.shape[-2:]
    elif b_kind == "devcol":
        b_rows, b_cols = b.shape[-2], N_DEV * b.shape[-1]
    else:
        b_rows, b_cols = N_DEV * b.shape[-2], b.shape[-1]
    if mode == "nn":
        (M, K), (K2, N) = a.shape, (b_rows, b_cols)
    elif mode == "nt":
        (M, K), (N, K2) = a.shape, (b_rows, b_cols)
    else:
        (K, M), (K2, N) = a.shape, (b_rows, b_cols)
    assert K == K2, (name, a.shape, b.shape, mode)
    tm, tn, tk = min(tm, M), min(tn, N), min(tk, K)
    cols_are_n = mode != "nt"
    if b_kind == "devcol":
        tn, tk = (b.shape[-1], tk) if cols_are_n else (tn, b.shape[-1])
    if b_kind == "devrow":
        tn, tk = (tn, b.shape[-2]) if cols_are_n else (b.shape[-2], tk)
    shard = N // N_DEV
    if o_kind == "devcol":
        tn = max(shard, tn // shard * shard)
    assert M % tm == 0 and N % tn == 0 and K % tk == 0, (name, M, N, K, tm, tn, tk)
    nk = K // tk
    dims = {"nn": NN, "nt": NT, "tn": TN}[mode]
    n_ex, n_out = len(extras) + len(params), len(out_dtypes)

    def body(*refs):
        a_ref, b_ref = refs[:2]
        ex = refs[2:2 + n_ex]
        outs = refs[2 + n_ex:2 + n_ex + n_out]
        acc = refs[-1]
        k = pl.program_id(2)
        part = _dot(a_ref[...].astype(bf16), b_ref[...].astype(bf16), dims)
        if nk > 1:
            @pl.when(k == 0)
            def _():
                acc[...] = part

            @pl.when(k > 0)
            def _():
                acc[...] += part

        @pl.when(k == nk - 1)
        def _():
            r = acc[...] if nk > 1 else part
            res = epilogue(r, *[e[...] for e in ex]) if epilogue is not None else (r,)
            for o, v in zip(outs, res):
                if o_kind == "plain":
                    o[...] = v.astype(o.dtype)
                else:
                    for q in range(tn // shard):
                        o[q] = v[:, q * shard:(q + 1) * shard].astype(o.dtype)

    if mode == "tn":
        a_spec = pl.BlockSpec((tk, tm), lambda i, j, k: (k, i))
    else:
        a_spec = pl.BlockSpec((tm, tk), lambda i, j, k: (i, k))
    jb, kb = (lambda i, j, k: j), (lambda i, j, k: k)
    if mode == "nt":
        b_spec = _mat_spec(b, b_kind, b_lead, tn, tk, jb, kb)
    else:
        b_spec = _mat_spec(b, b_kind, b_lead, tk, tn, kb, jb)
    e_spec = pl.BlockSpec((tm, tn), lambda i, j, k: (i, j))
    if o_kind == "plain":
        o_spec, o_shape = e_spec, (M, N)
    else:
        o_spec, o_shape = pl.BlockSpec((tn // shard, tm, shard), lambda i, j, k: (j, i, 0)), (N_DEV, M, shard)
    res = pl.pallas_call(
        body, name=name,
        grid=(M // tm, N // tn, nk),
        in_specs=[a_spec, b_spec] + [e_spec] * len(extras)
        + [pl.BlockSpec(p.shape, lambda i, j, k: (0, 0)) for p in params],
        out_specs=[o_spec] * n_out,
        out_shape=[jax.ShapeDtypeStruct(o_shape, dt) for dt in out_dtypes],
        scratch_shapes=[pltpu.VMEM((tm, tn), f32)],
        compiler_params=_cparams("parallel", "parallel", "arbitrary"),
    )(a, b, *extras, *params)
    return res[0] if n_out == 1 else res


def _row_spec(tm, cb, width):
    assert (cb * LANE) % width == 0
    blk = (cb * LANE) // width
    return pl.BlockSpec((tm, width), lambda i: (i, blk))


def _whole_spec(p):
    nd = p.ndim
    return pl.BlockSpec(p.shape, lambda i: (0,) * nd)


def _tl_fwd(name, fn, rows, params, out_widths, out_dtypes, tm=256):
    T = rows[0][0].shape[0]
    tm = min(tm, T)
    nr, npar = len(rows), len(params)

    def body(*refs):
        vals = [r[...] for r in refs[:nr + npar]]
        outs = fn(*vals)
        for o, v in zip(refs[nr + npar:], outs):
            o[...] = v.astype(o.dtype)

    res = pl.pallas_call(
        body, name=name, grid=(T // tm,),
        in_specs=[_row_spec(tm, cb, w) for (_, cb, w) in rows] + [_whole_spec(p) for p in params],
        out_specs=[pl.BlockSpec((tm, w), lambda i: (i, 0)) for w in out_widths],
        out_shape=[jax.ShapeDtypeStruct((T, w), dt) for w, dt in zip(out_widths, out_dtypes)],
        compiler_params=_cparams("parallel"),
    )(*[r[0] for r in rows], *params)
    return res


def _tl_bwd(name, fn, rows, params, cot_rows, cot_fn=None, wide=None, tm=256):
    T = rows[0][0].shape[0]
    tm = min(tm, T)
    nr, npar, nc = len(rows), len(params), len(cot_rows)
    place = [(wide or {}).get(k, (w, 0)) for k, (_, _, w) in enumerate(rows)]

    def body(*refs):
        vals = [r[...] for r in refs[:nr + npar]]
        cots = [r[...] for r in refs[nr + npar:nr + npar + nc]]
        outs = refs[nr + npar + nc:]
        cot = tuple(cot_fn(*cots)) if cot_fn is not None else tuple(cots)
        _, vjp = jax.vjp(fn, *vals)
        grads = vjp(cot)
        for o, g in zip(outs[:nr], grads[:nr]):
            o[...] = g.astype(o.dtype)
        i = pl.program_id(0)
        for o, g in zip(outs[nr:], grads[nr:]):
            @pl.when(i == 0)
            def _(o=o):
                o[...] = jnp.zeros_like(o)
            o[...] += g

    res = pl.pallas_call(
        body, name=name, grid=(T // tm,),
        in_specs=[_row_spec(tm, cb, w) for (_, cb, w) in rows] + [_whole_spec(p) for p in params]
        + [_row_spec(tm, cb, w) for (_, cb, w) in cot_rows],
        out_specs=[pl.BlockSpec((tm, w), lambda i, blk=blk: (i, blk)) for (_, _, w), (_, blk) in zip(rows, place)]
        + [_whole_spec(p) for p in params],
        out_shape=[jax.ShapeDtypeStruct((T, total), f32) for total, _ in place]
        + [jax.ShapeDtypeStruct(p.shape, f32) for p in params],
        compiler_params=_cparams("arbitrary"),
    )(*[r[0] for r in rows], *params, *[r[0] for r in cot_rows])
    return res[:nr], res[nr:]


def _ln_res_fn(x, mix, g, b):
    pre = DN_ALPHA * x + mix
    mu = jnp.mean(pre, axis=-1, keepdims=True)
    var = jnp.mean(jnp.square(pre - mu), axis=-1, keepdims=True)
    return ((pre - mu) * lax.rsqrt(var + LN_EPS) * g + b,)


@jax.custom_jvp
def _expm1(x):
    small = jnp.abs(x) < 0.3
    xs = jnp.where(small, x, 0.0)
    poly = xs * (1.0 + xs * (1 / 2 + xs * (1 / 6 + xs * (1 / 24 + xs * (1 / 120 + xs * (
        1 / 720 + xs * (1 / 5040 + xs * (1 / 40320 + xs * (1 / 362880)))))))))
    return jnp.where(small, poly, jnp.exp(x) - 1.0)


@_expm1.defjvp
def _expm1_jvp(primals, tangents):
    (x,), (t,) = primals, tangents
    return _expm1(x), t * jnp.exp(x)


def _rglru_pre_fn(pre_r, pre_i, xc, b_a, b_x, lam):
    r = jax.nn.sigmoid(pre_r + b_a)
    i = jax.nn.sigmoid(pre_i + b_x)
    log_a = -LRU_C * r * jax.nn.softplus(-lam)
    a = jnp.exp(log_a)
    b = jnp.sqrt(-_expm1(2.0 * log_a)) * (i * xc)
    return a, b


def _rec_gate_fn(h, gate):
    return (h * jax.nn.gelu(gate),)


def _loss_head(y, t, tm=256):
    T, Dm = y.shape

    def body(y_ref, t_ref, dy_ref, loss_ref):
        e = y_ref[...] - t_ref[...]
        dy_ref[...] = e * (1.0 / Dm)

        @pl.when(pl.program_id(0) == 0)
        def _():
            loss_ref[...] = jnp.zeros_like(loss_ref)

        loss_ref[...] += 0.5 * jnp.sum(jnp.mean(e * e, axis=-1, keepdims=True), axis=0, keepdims=True)

    dy, loss = pl.pallas_call(
        body, name="loss_head", grid=(T // tm,),
        in_specs=[pl.BlockSpec((tm, Dm), lambda i: (i, 0))] * 2,
        out_specs=[pl.BlockSpec((tm, Dm), lambda i: (i, 0)), pl.BlockSpec((SUBLANE, LANE), lambda i: (0, 0))],
        out_shape=[jax.ShapeDtypeStruct((T, Dm), f32), jax.ShapeDtypeStruct((SUBLANE, LANE), f32)],
        compiler_params=_cparams("arbitrary"),
    )(y, t)
    return loss[0, 0], dy


def _conv_fwd(name, x, cb0, nblk, w, bias, tm=2048):
    T = x.shape[0]
    tm = min(tm, T)
    hb = tm // SUBLANE
    has_b = bias is not None

    def body(*refs):
        cur, prev, w_ref = refs[:3]
        b_ref = refs[3] if has_b else None
        o = refs[-1]
        i = pl.program_id(1)
        p = jnp.where(i > 0, prev[...], 0.0)
        xcat = jnp.concatenate([p, cur[...]], axis=0)
        acc = cur[...] * w_ref[3:4, :]
        for j in range(3):
            acc = acc + pltpu.roll(xcat, 3 - j, axis=0)[SUBLANE:] * w_ref[j:j + 1, :]
        if has_b:
            acc = acc + b_ref[...]
        o[...] = acc

    in_specs = [
        pl.BlockSpec((tm, LANE), lambda c, i: (i, cb0 + c)),
        pl.BlockSpec((SUBLANE, LANE), lambda c, i: (jnp.maximum(i * hb - 1, 0), cb0 + c)),
        pl.BlockSpec((4, LANE), lambda c, i: (0, c)),
    ]
    args = [x, x, w]
    if has_b:
        in_specs.append(pl.BlockSpec((1, LANE), lambda c, i: (0, c)))
        args.append(bias)
    return pl.pallas_call(
        body, name=name, grid=(nblk, T // tm),
        in_specs=in_specs,
        out_specs=pl.BlockSpec((tm, LANE), lambda c, i: (i, c)),
        out_shape=jax.ShapeDtypeStruct((T, nblk * LANE), f32),
        compiler_params=_cparams("parallel", "parallel"),
    )(*args)


def _conv_bwd(name, dy, x, cb0, nblk, w, into, into_cb, tm=2048):
    T = x.shape[0]
    tm = min(tm, T)
    hb = tm // SUBLANE
    nt = T // tm

    def body(dcur, dnext, xcur, xprev, w_ref, _, dx_ref, dw_ref, db_ref):
        i = pl.program_id(1)
        d = dcur[...]
        dn = jnp.where(i < nt - 1, dnext[...], 0.0)
        dcat = jnp.concatenate([d, dn], axis=0)
        acc = d * w_ref[3:4, :]
        for j in range(3):
            s = 3 - j
            acc = acc + pltpu.roll(dcat, tm + SUBLANE - s, axis=0)[:tm] * w_ref[j:j + 1, :]
        dx_ref[...] = acc

        p = jnp.where(i > 0, xprev[...], 0.0)
        xcat = jnp.concatenate([p, xcur[...]], axis=0)
        rows = [jnp.sum(d * pltpu.roll(xcat, 3 - j, axis=0)[SUBLANE:], axis=0, keepdims=True) for j in range(3)]
        rows.append(jnp.sum(d * xcur[...], axis=0, keepdims=True))
        rows.append(jnp.zeros((SUBLANE - 4, LANE), f32))

        @pl.when(i == 0)
        def _():
            dw_ref[...] = jnp.zeros_like(dw_ref)
            db_ref[...] = jnp.zeros_like(db_ref)

        dw_ref[...] += jnp.concatenate(rows, axis=0)
        db_ref[...] += jnp.broadcast_to(jnp.sum(d, axis=0, keepdims=True), (SUBLANE, LANE))

    nh = T // SUBLANE
    dx, dw, db = pl.pallas_call(
        body, name=name, grid=(nblk, nt),
        in_specs=[
            pl.BlockSpec((tm, LANE), lambda c, i: (i, c)),
            pl.BlockSpec((SUBLANE, LANE), lambda c, i: (jnp.minimum((i + 1) * hb, nh - 1), c)),
            pl.BlockSpec((tm, LANE), lambda c, i: (i, cb0 + c)),
            pl.BlockSpec((SUBLANE, LANE), lambda c, i: (jnp.maximum(i * hb - 1, 0), cb0 + c)),
            pl.BlockSpec((4, LANE), lambda c, i: (0, c)),
            pl.BlockSpec(memory_space=pl.ANY),
        ],
        out_specs=[
            pl.BlockSpec((tm, LANE), lambda c, i: (i, into_cb + c)),
            pl.BlockSpec((SUBLANE, LANE), lambda c, i: (0, c)),
            pl.BlockSpec((SUBLANE, LANE), lambda c, i: (0, c)),
        ],
        out_shape=[jax.ShapeDtypeStruct(into.shape, f32),
                   jax.ShapeDtypeStruct((SUBLANE, nblk * LANE), f32),
                   jax.ShapeDtypeStruct((SUBLANE, nblk * LANE), f32)],
        input_output_aliases={5: 0},
        compiler_params=_cparams("parallel", "arbitrary"),
    )(dy, dy, x, x, w, into)
    return dx, dw[:4], db[0]


@functools.partial(jax.custom_vjp, nondiff_argnums=(1,))
def _lroll(x, s):
    return pltpu.roll(x, s, axis=1)


def _lroll_fwd(x, s):
    return _lroll(x, s), None


def _lroll_bwd(s, _, g):
    return (_lroll(g, (LANE - s) % LANE),)


_lroll.defvjp(_lroll_fwd, _lroll_bwd)


def _rope_tables(T):
    half = A_HEAD_DIM // 2
    inv_freq = ROPE_THETA ** (-jnp.arange(half, dtype=f32) / half)
    ang = jnp.arange(T, dtype=f32)[:, None] * inv_freq[None, :]
    cos, sin = jnp.cos(ang), jnp.sin(ang)
    return jnp.tile(jnp.concatenate([cos, cos], axis=1), (1, 2)), jnp.tile(jnp.concatenate([-sin, sin], axis=1), (1, 2))


def _attn_block_fn(n, q, kp, kc, vp, vc, cq, sq, cp, sp, sinks):
    W = WINDOW
    lane = lax.broadcasted_iota(jnp.int32, (W, LANE), 1)
    lo_half = (lane % A_HEAD_DIM) < (A_HEAD_DIM // 2)
    lane8 = lax.broadcasted_iota(jnp.int32, sinks.shape, 1)

    def rope(x, c, s):
        return x * c + jnp.where(lo_half, _lroll(x, LANE - A_HEAD_DIM // 2), _lroll(x, A_HEAD_DIM // 2)) * s

    k2 = jnp.concatenate([rope(kp, cp, sp), rope(kc, cq, sq)], axis=0).astype(bf16)
    v2 = jnp.concatenate([vp, vc], axis=0).astype(bf16)
    qs, sink_rows = [], []
    for t in range(4):
        qt = rope(q[:, LANE * t:LANE * (t + 1)], cq, sq)
        g = t // 2
        for hh in range(2):
            qa = jnp.where((lane // A_HEAD_DIM) == hh, qt, 0.0)
            qs.append(_lroll(qa, A_HEAD_DIM) if hh != g else qa)
            sink = jnp.sum(jnp.where(lane8 == 2 * t + hh, sinks, 0.0), axis=1, keepdims=True)
            sink_rows.append(jnp.broadcast_to(sink, (W, 1)))
    qall = jnp.concatenate(qs, axis=0).astype(bf16)
    sink = jnp.concatenate(sink_rows, axis=0)
    row = lax.broadcasted_iota(jnp.int32, (A_Q_HEADS * W, 2 * W), 0) & (W - 1)
    col = lax.broadcasted_iota(jnp.int32, (A_Q_HEADS * W, 2 * W), 1)
    dist = row + W - col
    mask = (dist >= 0) & (dist < W) & ((col >= W) | (n > 0))
    s = jnp.where(mask, _dot(qall, k2, NT) * (A_HEAD_DIM ** -0.5), -jnp.inf)
    m = jnp.maximum(jnp.max(s, axis=-1, keepdims=True), sink)
    e = jnp.exp(s - m)
    p = e / (jnp.sum(e, axis=-1, keepdims=True) + jnp.exp(sink - m))
    o = _dot(p.astype(bf16), v2, NN)
    outs = []
    for t in range(4):
        g = t // 2
        ot = jnp.zeros((W, LANE), f32)
        for hh in range(2):
            j = 2 * t + hh
            oj = jnp.where((lane // A_HEAD_DIM) == g, o[W * j:W * (j + 1)], 0.0)
            ot = ot + (_lroll(oj, A_HEAD_DIM) if hh != g else oj)
        outs.append(ot)
    return jnp.concatenate(outs, axis=1)


def _attn_specs():
    W = WINDOW
    prev = lambda n: jnp.maximum(n - 1, 0)
    return [
        pl.BlockSpec((W, 4 * LANE), lambda n: (n, CB_QA // 4)),
        pl.BlockSpec((W, LANE), lambda n: (prev(n), CB_KA)),
        pl.BlockSpec((W, LANE), lambda n: (n, CB_KA)),
        pl.BlockSpec((W, LANE), lambda n: (prev(n), CB_VA)),
        pl.BlockSpec((W, LANE), lambda n: (n, CB_VA)),
        pl.BlockSpec((W, LANE), lambda n: (n, 0)),
        pl.BlockSpec((W, LANE), lambda n: (n, 0)),
        pl.BlockSpec((W, LANE), lambda n: (prev(n), 0)),
        pl.BlockSpec((W, LANE), lambda n: (prev(n), 0)),
        pl.BlockSpec((1, A_Q_HEADS), lambda n: (0, 0)),
    ]


def _attn_fwd(name, proj, cos, sin, sinks):
    T = proj.shape[0]
    W = WINDOW

    def body(*refs):
        o = refs[-1]
        o[...] = _attn_block_fn(pl.program_id(0), *[r[...] for r in refs[:-1]])

    return pl.pallas_call(
        body, name=name, grid=(T // W,),
        in_specs=_attn_specs(),
        out_specs=pl.BlockSpec((W, 4 * LANE), lambda n: (n, 0)),
        out_shape=jax.ShapeDtypeStruct((T, 2 * 4 * LANE), f32),
        compiler_params=_cparams("parallel"),
    )(proj, proj, proj, proj, proj, cos, sin, cos, sin, sinks)


def _attn_bwd(name, proj, cos, sin, sinks, d_oab):
    T = proj.shape[0]
    W = WINDOW
    Q = 4 * LANE

    def body(*refs):
        ins = [r[...] for r in refs[:10]]
        do = refs[10][...]
        d_ref, ds_ref = refs[11:]
        n = pl.program_id(0)
        _, vjp = jax.vjp(functools.partial(_attn_block_fn, n), *ins)
        dq, dkp, dkc, dvp, dvc, _, _, _, _, dsk = vjp(do)

        @pl.when(n == 0)
        def _():
            d_ref[:, Q:] = jnp.zeros((T, 2 * LANE), f32)
            ds_ref[...] = jnp.zeros_like(ds_ref)

        cur = pl.ds(pl.multiple_of(n * W, W), W)
        d_ref[cur, :Q] = dq
        d_ref[cur, Q:Q + LANE] += dkc
        d_ref[cur, Q + LANE:] += dvc
        ds_ref[...] += dsk

        @pl.when(n > 0)
        def _():
            prv = pl.ds(pl.multiple_of((n - 1) * W, W), W)
            d_ref[prv, Q:Q + LANE] += dkp
            d_ref[prv, Q + LANE:] += dvp

    return pl.pallas_call(
        body, name=name, grid=(T // W,),
        in_specs=_attn_specs() + [pl.BlockSpec((W, Q), lambda n: (n, 0))],
        out_specs=[pl.BlockSpec((T, Q + 2 * LANE), lambda n: (0, 0)),
                   pl.BlockSpec((1, A_Q_HEADS), lambda n: (0, 0))],
        out_shape=[jax.ShapeDtypeStruct((T, HYB_PROJ_PAD), f32), jax.ShapeDtypeStruct((1, A_Q_HEADS), f32)],
        compiler_params=_cparams("arbitrary"),
    )(proj, proj, proj, proj, proj, cos, sin, cos, sin, sinks, d_oab)


def _bdot(spec, a, b, precision=None):
    return jnp.einsum(spec, a, b, preferred_element_type=f32, precision=precision)


@jax.custom_vjp
def _tri_inv(a):
    C = a.shape[-1]
    r = lax.broadcasted_iota(jnp.int32, (C, C), 0)
    c = lax.broadcasted_iota(jnp.int32, (C, C), 1)
    t = jnp.broadcast_to(jnp.where(r == c, 1.0, 0.0).astype(f32), a.shape)
    for j in range(C - 1):
        t = t - a[:, :, j:j + 1] * t[:, j:j + 1, :]
    return t


def _tri_inv_fwd(a):
    t = _tri_inv(a)
    return t, t


def _tri_inv_bwd(t, g):
    C = t.shape[-1]
    r = lax.broadcasted_iota(jnp.int32, (C, C), 0)
    c = lax.broadcasted_iota(jnp.int32, (C, C), 1)
    x = _bdot("hki,hkj->hij", t, g, precision=lax.Precision.HIGHEST)
    y = _bdot("hik,hjk->hij", x, t, precision=lax.Precision.HIGHEST)
    return (jnp.where(r > c, -y, 0.0),)


_tri_inv.defvjp(_tri_inv_fwd, _tri_inv_bwd)


@jax.custom_vjp
def _tri_inv_saved(a, t):
    return t


_tri_inv_saved.defvjp(lambda a, t: (t, t), lambda t, g: (_tri_inv_bwd(t, g)[0], jnp.zeros_like(t)))


def _silu(x):
    return x * jax.nn.sigmoid(x)


def _l2n(x):
    return x * lax.rsqrt(jnp.sum(x * x, axis=-1, keepdims=True) + NORM_EPS)


def _delta_chunk_fn(cq, ck, cv, z, lg, a_log, dt_bias, norm_w, S, t_saved=None, want_t=False):
    C = B_CHUNK
    lane = lax.broadcasted_iota(jnp.int32, (C, LANE), 1)
    pick = lambda l0: jnp.concatenate(
        [jnp.sum(jnp.where(lane == l0 + h, lg, 0.0), axis=1, keepdims=True)[None] for h in range(B_HEADS)], axis=0)
    bl, al = pick(0), pick(B_HEADS)
    q = _l2n(_silu(cq)) * (B_HEAD_DIM ** -0.5)
    k = _l2n(_silu(ck))
    v = _silu(cv)
    beta = jax.nn.sigmoid(bl)
    g = -jnp.exp(a_log) * jax.nn.softplus(al + dt_bias)
    r = lax.broadcasted_iota(jnp.int32, (C, C), 0)
    c = lax.broadcasted_iota(jnp.int32, (C, C), 1)
    eye = r == c
    g_row = jnp.sum(jnp.where(eye, g, 0.0), axis=1, keepdims=True)
    gc = jnp.sum(jnp.where(c <= r, g_row, 0.0), axis=2, keepdims=True)
    gc_row = jnp.sum(jnp.where(eye, gc, 0.0), axis=1, keepdims=True)
    decay_incl = jnp.exp(jnp.where(r >= c, gc - gc_row, -jnp.inf))
    decay_strict = jnp.where(r > c, decay_incl, 0.0)
    kb = k * beta
    vb = v * beta
    kbf = k.astype(bf16)
    a_mat = _bdot("hik,hjk->hij", kb.astype(bf16), kbf) * decay_strict
    t_f32 = _tri_inv(a_mat) if t_saved is None else _tri_inv_saved(a_mat, t_saved)
    t_mat = t_f32.astype(bf16)
    eg = jnp.exp(gc)
    u = _bdot("hij,hjv->hiv", t_mat, vb.astype(bf16))
    w = _bdot("hij,hjk->hik", t_mat, (kb * eg).astype(bf16))
    qk = _bdot("hik,hjk->hij", q.astype(bf16), kbf) * decay_incl
    g_last = jnp.sum(g, axis=1, keepdims=True)
    k_tail = k * jnp.exp(g_last - gc)
    Sb = S.astype(bf16)
    v_new = u - _bdot("hck,hkv->hcv", w.astype(bf16), Sb)
    o = _bdot("hck,hkv->hcv", (q * eg).astype(bf16), Sb) + _bdot("hij,hjv->hiv", qk.astype(bf16), v_new.astype(bf16))
    S_new = S * jnp.exp(g_last) + _bdot("hck,hcv->hkv", k_tail.astype(bf16), v_new.astype(bf16))
    ob = o * lax.rsqrt(jnp.mean(o * o, axis=-1, keepdims=True) + NORM_EPS) * norm_w
    return (ob * _silu(z), S_new) + ((t_f32,) if want_t else ())


def _delta_in_specs(rev, N):
    C = B_CHUNK
    ix = (lambda n: N - 1 - n) if rev else (lambda n: n)
    specs = [pl.BlockSpec((C, 3 * B_HEADS * LANE), lambda n: (ix(n), 0))]
    specs += [pl.BlockSpec((C, LANE), lambda n, h=h: (ix(n), CB_Z + h)) for h in range(B_HEADS)]
    specs += [
        pl.BlockSpec((C, LANE), lambda n: (ix(n), CB_LG)),
        pl.BlockSpec((B_HEADS, 1, 1), lambda n: (0, 0, 0)),
        pl.BlockSpec((B_HEADS, 1, 1), lambda n: (0, 0, 0)),
        pl.BlockSpec((1, LANE), lambda n: (0, 0)),
    ]
    return specs


def _delta_inputs(c_ref, z_refs, lg, al, dt, nw):
    H = B_HEADS
    part = lambda p: jnp.stack([c_ref[:, LANE * (p * H + h):LANE * (p * H + h + 1)] for h in range(H)])
    return (part(0), part(1), part(2), jnp.stack([z[...] for z in z_refs]), lg[...], al[...], dt[...], nw[...])


def _delta_fwd(name, c, proj, a_log, dt_bias, norm_w, o_ab):
    T = c.shape[0]
    C = B_CHUNK
    N = T // C
    Dh = B_HEAD_DIM
    H = B_HEADS

    def body(*refs):
        c_ref, z_refs, (lg, al, dt, nw) = refs[0], refs[1:1 + H], refs[1 + H:5 + H]
        o_ref, s_ref, t_ref, S = refs[6 + H:]

        @pl.when(pl.program_id(0) == 0)
        def _():
            S[...] = jnp.zeros_like(S)

        s0 = S[...]
        s_ref[...] = s0
        ob, s1, t = _delta_chunk_fn(*_delta_inputs(c_ref, z_refs, lg, al, dt, nw), s0, want_t=True)
        for h in range(H):
            o_ref[:, LANE * h:LANE * (h + 1)] = ob[h]
        t_ref[...] = t
        S[...] = s1

    return pl.pallas_call(
        body, name=name, grid=(N,),
        in_specs=_delta_in_specs(False, N) + [pl.BlockSpec(memory_space=pl.ANY)],
        out_specs=[pl.BlockSpec((C, H * LANE), lambda n: (n, 1)),
                   pl.BlockSpec((H, None, Dh, Dh), lambda n: (0, n, 0, 0)),
                   pl.BlockSpec((H, None, C, C), lambda n: (0, n, 0, 0))],
        out_shape=[jax.ShapeDtypeStruct(o_ab.shape, f32), jax.ShapeDtypeStruct((H, N, Dh, Dh), f32),
                   jax.ShapeDtypeStruct((H, N, C, C), f32)],
        input_output_aliases={5 + H: 0},
        scratch_shapes=[pltpu.VMEM((H, Dh, Dh), f32)],
        compiler_params=_cparams("arbitrary"),
    )(c, *([proj] * H), proj, a_log, dt_bias, norm_w, o_ab)


def _delta_bwd(name, c, proj, a_log, dt_bias, norm_w, s_saved, t_saved, d_oab, dproj):
    T = c.shape[0]
    C = B_CHUNK
    N = T // C
    Dh = B_HEAD_DIM
    H = B_HEADS

    def body(*refs):
        c_ref, z_refs, (lg, al, dt, nw) = refs[0], refs[1:1 + H], refs[1 + H:5 + H]
        s_ref, t_ref, do_ref = refs[5 + H:8 + H]
        dc, dtail, dal, ddt, dnw, dS = refs[9 + H:]

        @pl.when(pl.program_id(0) == 0)
        def _():
            dS[...] = jnp.zeros_like(dS)
            dal[...] = jnp.zeros_like(dal)
            ddt[...] = jnp.zeros_like(ddt)
            dnw[...] = jnp.zeros_like(dnw)

        _, vjp = jax.vjp(functools.partial(_delta_chunk_fn, t_saved=t_ref[...]),
                         *_delta_inputs(c_ref, z_refs, lg, al, dt, nw), s_ref[...])
        do = jnp.stack([do_ref[:, LANE * h:LANE * (h + 1)] for h in range(H)])
        g = vjp((do, dS[...]))
        for h in range(H):
            for p in range(3):
                dc[:, LANE * (p * H + h):LANE * (p * H + h + 1)] = g[p][h]
            dtail[:, LANE * h:LANE * (h + 1)] = g[3][h]
        dtail[:, LANE * H:LANE * (H + 1)] = g[4]
        dtail[:, LANE * (H + 1):] = jnp.zeros((C, LANE), f32)
        dal[...] += g[5]
        ddt[...] += g[6]
        dnw[...] += g[7]
        dS[...] = g[8]

    rn = lambda n: N - 1 - n
    return pl.pallas_call(
        body, name=name, grid=(N,),
        in_specs=_delta_in_specs(True, N) + [
            pl.BlockSpec((H, None, Dh, Dh), lambda n: (0, rn(n), 0, 0)),
            pl.BlockSpec((H, None, C, C), lambda n: (0, rn(n), 0, 0)),
            pl.BlockSpec((C, H * LANE), lambda n: (rn(n), 1)),
            pl.BlockSpec(memory_space=pl.ANY),
        ],
        out_specs=[
            pl.BlockSpec((C, 3 * H * LANE), lambda n: (rn(n), 0)),
            pl.BlockSpec((C, (H + 2) * LANE), lambda n: (rn(n), CB_Z // (H + 2))),
            pl.BlockSpec((H, 1, 1), lambda n: (0, 0, 0)),
            pl.BlockSpec((H, 1, 1), lambda n: (0, 0, 0)),
            pl.BlockSpec((1, LANE), lambda n: (0, 0)),
        ],
        out_shape=[jax.ShapeDtypeStruct((T, 3 * H * Dh), f32), jax.ShapeDtypeStruct(dproj.shape, f32),
                   jax.ShapeDtypeStruct((H, 1, 1), f32), jax.ShapeDtypeStruct((H, 1, 1), f32),
                   jax.ShapeDtypeStruct((1, LANE), f32)],
        input_output_aliases={8 + H: 1},
        scratch_shapes=[pltpu.VMEM((H, Dh, Dh), f32)],
        compiler_params=_cparams("arbitrary"),
    )(c, *([proj] * H), proj, a_log, dt_bias, norm_w, s_saved, t_saved, d_oab, dproj)


def _blockdiag_fwd(name, xc, w_a, w_x, tm=512):
    T, Wd = xc.shape
    bw = Wd // LRU_BLOCKS
    tm = min(tm, T)

    def body(x_ref, wa_ref, wx_ref, oa, ox):
        xb = x_ref[...].astype(bf16)
        oa[...] = _dot(xb, wa_ref[...].astype(bf16), NN)
        ox[...] = _dot(xb, wx_ref[...].astype(bf16), NN)

    xs = pl.BlockSpec((tm, bw), lambda i, h: (i, h))
    ws = pl.BlockSpec((None, bw, bw), lambda i, h: (h, 0, 0))
    return pl.pallas_call(
        body, name=name, grid=(T // tm, LRU_BLOCKS), in_specs=[xs, ws, ws], out_specs=[xs, xs],
        out_shape=[jax.ShapeDtypeStruct((T, Wd), f32)] * 2,
        compiler_params=_cparams("parallel", "parallel"),
    )(xc, w_a, w_x)


def _blockdiag_bwd_dx(name, dpr, dpi, w_a, w_x, addend, tm=512):
    T, Wd = dpr.shape
    bw = Wd // LRU_BLOCKS
    tm = min(tm, T)

    def body(dr, di, wa_ref, wx_ref, add, o):
        o[...] = (add[...] + _dot(dr[...].astype(bf16), wa_ref[...].astype(bf16), NT)
                  + _dot(di[...].astype(bf16), wx_ref[...].astype(bf16), NT))

    xs = pl.BlockSpec((tm, bw), lambda i, h: (i, h))
    ws = pl.BlockSpec((None, bw, bw), lambda i, h: (h, 0, 0))
    return pl.pallas_call(
        body, name=name, grid=(T // tm, LRU_BLOCKS), in_specs=[xs, xs, ws, ws, xs], out_specs=xs,
        out_shape=jax.ShapeDtypeStruct((T, Wd), f32),
        compiler_params=_cparams("parallel", "parallel"),
    )(dpr, dpi, w_a, w_x, addend)


def _blockdiag_bwd_dw(name, xc, dpr, dpi, tk=512):
    T, Wd = xc.shape
    bw = Wd // LRU_BLOCKS
    tk = min(tk, T)

    def body(x_ref, dr, di, oa, ox):
        @pl.when(pl.program_id(1) == 0)
        def _():
            oa[...] = jnp.zeros_like(oa)
            ox[...] = jnp.zeros_like(ox)

        xb = x_ref[...].astype(bf16)
        oa[...] += _dot(xb, dr[...].astype(bf16), TN)
        ox[...] += _dot(xb, di[...].astype(bf16), TN)

    xs = pl.BlockSpec((tk, bw), lambda h, k: (k, h))
    ws = pl.BlockSpec((None, bw, bw), lambda h, k: (h, 0, 0))
    return pl.pallas_call(
        body, name=name, grid=(LRU_BLOCKS, T // tk), in_specs=[xs, xs, xs], out_specs=[ws, ws],
        out_shape=[jax.ShapeDtypeStruct((LRU_BLOCKS, bw, bw), f32)] * 2,
        compiler_params=_cparams("parallel", "arbitrary"),
    )(xc, dpr, dpi)


def _scan(name, a, b, reverse, tt=512, cb=512):
    T, Wd = a.shape
    tt, cb = min(tt, T), min(cb, Wd)
    nt = T // tt
    ng = tt // SUBLANE

    def body(a_ref, b_ref, *rest):
        outs, (carry, carry_a) = rest[:-2], rest[-2:]

        @pl.when(pl.program_id(1) == 0)
        def _():
            carry[...] = jnp.zeros_like(carry)
            carry_a[...] = jnp.zeros_like(carry_a)

        row = lax.broadcasted_iota(jnp.int32, (SUBLANE, cb), 0)

        def step(gi, c):
            hp, ap = c
            g = (ng - 1 - gi) if reverse else gi
            off = pl.multiple_of(g * SUBLANE, SUBLANE)
            A = a_ref[pl.ds(off, SUBLANE), :]
            B = b_ref[pl.ds(off, SUBLANE), :]
            a_first = jnp.broadcast_to(A[0:1, :], (SUBLANE, cb))
            if reverse:
                A = jnp.where(row == SUBLANE - 1, ap, pltpu.roll(A, SUBLANE - 1, axis=0))
            for s in (1, 2, 4):
                sh = (SUBLANE - s) if reverse else s
                As = pltpu.roll(A, sh, axis=0)
                Bs = pltpu.roll(B, sh, axis=0)
                valid = (row < SUBLANE - s) if reverse else (row >= s)
                B = jnp.where(valid, A * Bs + B, B)
                A = jnp.where(valid, A * As, A)
            hcur = A * hp + B
            outs[0][pl.ds(off, SUBLANE), :] = hcur
            if not reverse:
                outs[1][pl.ds(off, SUBLANE), :] = jnp.where(row == 0, hp, pltpu.roll(hcur, 1, axis=0))
            edge = hcur[0:1, :] if reverse else hcur[SUBLANE - 1:SUBLANE, :]
            return jnp.broadcast_to(edge, (SUBLANE, cb)), a_first

        carry[...], carry_a[...] = lax.fori_loop(0, ng, step, (carry[...], carry_a[...]))

    ti = (lambda c, i: (nt - 1 - i, c)) if reverse else (lambda c, i: (i, c))
    spec = pl.BlockSpec((tt, cb), ti)
    n_out = 1 if reverse else 2
    res = pl.pallas_call(
        body, name=name, grid=(Wd // cb, nt), in_specs=[spec, spec], out_specs=[spec] * n_out,
        out_shape=[jax.ShapeDtypeStruct((T, Wd), f32)] * n_out,
        scratch_shapes=[pltpu.VMEM((SUBLANE, cb), f32), pltpu.VMEM((SUBLANE, cb), f32)],
        compiler_params=_cparams("parallel", "arbitrary"),
    )(a, b)
    return res[0] if reverse else res


def _relu2_epilogue(r):
    h = jnp.maximum(r, 0.0)
    return r, h * h


def _drelu2_epilogue(r, a):
    return (r * (2.0 * jnp.maximum(a, 0.0)),)


def _add_epilogue(r, e):
    return (r + e,)


def _merge_cols(name, g, tm=256):
    _, L, R, s = g.shape

    def body(g_ref, o_ref):
        for d in range(N_DEV):
            o_ref[:, s * d:s * (d + 1)] = g_ref[d].astype(bf16)
        o_ref[:, N_DEV * s:] = jnp.zeros((tm, HYB_PROJ_PAD - N_DEV * s), bf16)

    return pl.pallas_call(
        body, name=name, grid=(L, R // tm),
        in_specs=[pl.BlockSpec((N_DEV, None, tm, s), lambda l, i: (0, l, i, 0))],
        out_specs=pl.BlockSpec((None, tm, HYB_PROJ_PAD), lambda l, i: (l, i, 0)),
        out_shape=jax.ShapeDtypeStruct((L, R, HYB_PROJ_PAD), bf16),
        compiler_params=_cparams("parallel", "parallel"),
    )(g)


def _split_cols(name, dw, tm=256):
    R = dw.shape[0]
    s = HYB_PROJ // N_DEV

    def body(g_ref, o_ref):
        for d in range(N_DEV):
            o_ref[d] = g_ref[:, s * d:s * (d + 1)].astype(bf16)

    return pl.pallas_call(
        body, name=name, grid=(R // tm,),
        in_specs=[pl.BlockSpec((tm, HYB_PROJ_PAD), lambda i: (i, 0))],
        out_specs=pl.BlockSpec((N_DEV, tm, s), lambda i: (0, i, 0)),
        out_shape=jax.ShapeDtypeStruct((N_DEV, R, s), bf16),
        compiler_params=_cparams("parallel"),
    )(dw)


def _rows_to_dev(dw):
    nb, r, c = dw.shape
    t = dw.reshape(nb, N_DEV, r // N_DEV, c)
    return jnp.moveaxis(t, 1, 0).reshape(N_DEV, nb * (r // N_DEV), c).astype(bf16)


def _ln_epilogue(r, x, g, b):
    return r, _ln_res_fn(x, r, g, b)[0]


def _hybrid_fwd(tag, x, W, j, cos, sin, ln, before_out):
    proj = _mm(f"{tag}_proj", x, W["hyb_w_in"][j], "nn", b_kind="lead", b_lead=0)
    o_a = _attn_fwd(f"{tag}_attn", proj, cos, sin, W["hyb_sinks"][j][None, :])
    c = _conv_fwd(f"{tag}_conv", proj, CB_CONV, 12, W["hyb_conv_w"][j], None)
    o_ab, s_saved, t_saved = _delta_fwd(f"{tag}_delta", c, proj, W["hyb_a_log"][j].reshape(B_HEADS, 1, 1),
                                        W["hyb_dt_bias"][j].reshape(B_HEADS, 1, 1), W["hyb_norm_w"][j][None, :], o_a)
    before_out(o_ab)
    mix, x1 = _mm(f"{tag}_out", o_ab, W["hyb_w_out"][j], "nn", b_kind="lead", b_lead=0, epilogue=_ln_epilogue,
                  extras=(x,), params=ln, out_dtypes=(f32, f32), tm=512)
    return mix, x1, (proj, c, s_saved, t_saved, o_ab)


def _hybrid_bwd(tag, x, dmix, addend, W, j, cos, sin, saved, G, send_early):
    proj, c, s_saved, t_saved, o_ab = saved
    T = x.shape[0]
    d_oab = _mm(f"{tag}_dout", dmix, W["hyb_w_out"][j], "nt", b_kind="lead", b_lead=0)
    G["hyb_w_out"][j] = _mm(f"{tag}_dwout", o_ab, dmix, "tn", out_dtypes=(bf16,)).reshape(N_DEV, -1, D_MODEL)
    sinks = W["hyb_sinks"][j][None, :] + send_early({("hyb_w_out", j): G["hyb_w_out"][j]})
    dproj, dsinks = _attn_bwd(f"{tag}_dattn", proj, cos, sin, sinks, d_oab)
    a_log = W["hyb_a_log"][j].reshape(B_HEADS, 1, 1)
    dt_bias = W["hyb_dt_bias"][j].reshape(B_HEADS, 1, 1)
    dc, dproj, dal, ddt, dnw = _delta_bwd(f"{tag}_ddelta", c, proj, a_log, dt_bias, W["hyb_norm_w"][j][None, :],
                                          s_saved, t_saved, d_oab, dproj)
    dproj, dconv_w, _ = _conv_bwd(f"{tag}_dconv", dc, proj, CB_CONV, 12, W["hyb_conv_w"][j], dproj, CB_CONV)
    dx = _mm(f"{tag}_dx", dproj, W["hyb_w_in"][j], "nt", b_kind="lead", b_lead=0, epilogue=_add_epilogue,
             extras=(addend,))
    G["hyb_w_in"][j] = _split_cols(f"{tag}_dwin_split", _mm(f"{tag}_dwin", x, dproj, "tn"))
    G["hyb_sinks"][j] = dsinks[0]
    G["hyb_conv_w"][j] = dconv_w
    G["hyb_a_log"][j] = dal.reshape(B_HEADS)
    G["hyb_dt_bias"][j] = ddt.reshape(B_HEADS)
    G["hyb_norm_w"][j] = dnw[0]
    return dx


def _rec_fwd(tag, x, W, j, ln, before_out):
    Wd = D_MODEL
    proj = _mm(f"{tag}_proj", x, W["rec_w_in"][j], "nn", b_kind="devcol", b_lead=0)
    xc = _conv_fwd(f"{tag}_conv", proj, 0, Wd // LANE, W["rec_conv_w"][j], W["rec_conv_b"][j][None, :])
    pre_r, pre_i = _blockdiag_fwd(f"{tag}_gates", xc, W["rec_w_a"][j][0], W["rec_w_x"][j][0])
    pars = [W["rec_b_a"][j][None, :], W["rec_b_x"][j][None, :], W["rec_lambda"][j][None, :]]
    a, b = _tl_fwd(f"{tag}_pre", _rglru_pre_fn, [(pre_r, 0, Wd), (pre_i, 0, Wd), (xc, 0, Wd)], pars, [Wd, Wd], [f32, f32])
    h, h_prev = _scan(f"{tag}_scan", a, b, False)
    (hg,) = _tl_fwd(f"{tag}_gate", _rec_gate_fn, [(h, 0, Wd), (proj, Wd // LANE, Wd)], [], [Wd], [f32])
    before_out(hg)
    mix, x1 = _mm(f"{tag}_out", hg, W["rec_w_out"][j], "nn", b_kind="lead", b_lead=0, epilogue=_ln_epilogue,
                  extras=(x,), params=ln, out_dtypes=(f32, f32), tm=512)
    return mix, x1, (proj, xc, pre_r, pre_i, a, h, h_prev, hg)


def _rec_bwd(tag, x, dmix, addend, W, j, saved, G, send_early):
    proj, xc, pre_r, pre_i, a, h, h_prev, hg = saved
    Wd = D_MODEL
    dhg = _mm(f"{tag}_dout", dmix, W["rec_w_out"][j], "nt", b_kind="lead", b_lead=0)
    G["rec_w_out"][j] = _mm(f"{tag}_dwout", hg, dmix, "tn", out_dtypes=(bf16,)).reshape(N_DEV, -1, D_MODEL)
    sent = send_early({("rec_w_out", j): G["rec_w_out"][j]})
    (dh, dproj), _ = _tl_bwd(f"{tag}_dgate", _rec_gate_fn, [(h, 0, Wd), (proj, Wd // LANE, Wd)], [], [(dhg, 0, Wd)],
                             wide={1: (2 * Wd, 1)})
    lam_t = _scan(f"{tag}_dscan", a, dh, True)
    pars = [W["rec_b_a"][j][None, :] + sent, W["rec_b_x"][j][None, :], W["rec_lambda"][j][None, :]]
    (dpr, dpi, dxc1), (db_a, db_x, dlam) = _tl_bwd(
        f"{tag}_dpre", _rglru_pre_fn, [(pre_r, 0, Wd), (pre_i, 0, Wd), (xc, 0, Wd)], pars,
        [(lam_t, 0, Wd), (h_prev, 0, Wd)], cot_fn=lambda lt, hp: (lt * hp, lt))
    dxc = _blockdiag_bwd_dx(f"{tag}_dgates_dx", dpr, dpi, W["rec_w_a"][j][0], W["rec_w_x"][j][0], dxc1)
    dwa, dwx = _blockdiag_bwd_dw(f"{tag}_dgates_dw", xc, dpr, dpi)
    G["rec_w_a"][j], G["rec_w_x"][j] = _rows_to_dev(dwa), _rows_to_dev(dwx)
    dproj, dconv_w, dconv_b = _conv_bwd(f"{tag}_dconv", dxc, proj, 0, Wd // LANE, W["rec_conv_w"][j], dproj, 0)
    dx = _mm(f"{tag}_dx", dproj, W["rec_w_in"][j], "nt", b_kind="devcol", b_lead=0, epilogue=_add_epilogue,
             extras=(addend,))
    G["rec_w_in"][j] = _mm(f"{tag}_dwin", x, dproj, "tn", o_kind="devcol", out_dtypes=(bf16,))
    G["rec_conv_w"][j] = dconv_w
    G["rec_conv_b"][j] = dconv_b
    G["rec_b_a"][j] = db_a[0]
    G["rec_b_x"][j] = db_x[0]
    G["rec_lambda"][j] = dlam[0]
    return dx


def _local_step(x, target, W, load_layer, grads_ready):
    T = x.shape[0]
    cos, sin = _rope_tables(T)
    saved = []
    for layer in range(DEPTH):
        j = layer // 2
        tag = f"L{layer}"
        load_layer(layer, 0, x)
        ln1 = (W["ln1_g"][layer][None, :], W["ln1_b"][layer][None, :])
        before_out = functools.partial(load_layer, layer, 1)
        if layer % 2 == 0:
            mix, x1, sv = _hybrid_fwd(tag, x, W, j, cos, sin, ln1, before_out)
        else:
            mix, x1, sv = _rec_fwd(tag, x, W, j, ln1, before_out)
        load_layer(layer, 2, x1)
        a, h2 = _mm(f"{tag}_mlp1", x1, W["mlp_w1"][layer], "nn", b_kind="devcol", b_lead=0, epilogue=_relu2_epilogue,
                    out_dtypes=(f32, bf16))
        ln2 = (W["ln2_g"][layer][None, :], W["ln2_b"][layer][None, :])
        y, x2 = _mm(f"{tag}_mlp2", h2, W["mlp_w2"][layer], "nn", b_kind="devrow", b_lead=0, epilogue=_ln_epilogue,
                    extras=(x1,), params=ln2, out_dtypes=(f32, f32))
        saved.append((x, sv, mix, x1, a, h2, y))
        x = x2
    loss, dx = _loss_head(x, target)

    G = {k: [None] * (DEPTH if k.startswith(("ln", "mlp")) else DEPTH // 2) for k in (
        "hyb_w_in", "hyb_sinks", "hyb_conv_w", "hyb_a_log", "hyb_dt_bias", "hyb_norm_w", "hyb_w_out",
        "rec_w_in", "rec_conv_w", "rec_conv_b", "rec_w_a", "rec_b_a", "rec_w_x", "rec_b_x", "rec_lambda", "rec_w_out",
        "ln1_g", "ln1_b", "mlp_w1", "mlp_w2", "ln2_g", "ln2_b")}
    order = jnp.zeros((1, 1), f32)
    for layer in reversed(range(DEPTH)):
        j = layer // 2
        tag = f"L{layer}"
        x0, sv, mix, x1, a, h2, y = saved[layer]
        ln2 = [W["ln2_g"][layer][None, :] + order, W["ln2_b"][layer][None, :]]
        (dx1_a, dy), (dg2, db2) = _tl_bwd(f"{tag}_dln2", _ln_res_fn, [(x1, 0, D_MODEL), (y, 0, D_MODEL)], ln2,
                                          [(dx, 0, D_MODEL)])
        G["ln2_g"][layer], G["ln2_b"][layer] = dg2[0], db2[0]
        da = _mm(f"{tag}_dmlp2", dy, W["mlp_w2"][layer], "nt", b_kind="devrow", b_lead=0, epilogue=_drelu2_epilogue,
                 extras=(a,), out_dtypes=(bf16,))
        G["mlp_w2"][layer] = _mm(f"{tag}_dw2", h2, dy, "tn", out_dtypes=(bf16,)).reshape(N_DEV, -1, D_MODEL)
        dx1 = _mm(f"{tag}_dmlp1", da, W["mlp_w1"][layer], "nt", b_kind="devcol", b_lead=0, epilogue=_add_epilogue,
                  extras=(dx1_a,))
        G["mlp_w1"][layer] = _mm(f"{tag}_dw1", x1, da, "tn", o_kind="devcol", out_dtypes=(bf16,))
        ln1 = [W["ln1_g"][layer][None, :], W["ln1_b"][layer][None, :]]
        (dx0_a, dmix), (dg1, db1) = _tl_bwd(f"{tag}_dln1", _ln_res_fn, [(x0, 0, D_MODEL), (mix, 0, D_MODEL)], ln1,
                                            [(dx1, 0, D_MODEL)])
        G["ln1_g"][layer], G["ln1_b"][layer] = dg1[0], db1[0]
        early = functools.partial(grads_ready, f"l{layer}_early",
                                  {(k, layer): G[k][layer] for k in ("mlp_w1", "mlp_w2")})
        if layer % 2 == 0:
            dx = _hybrid_bwd(tag, x0, dmix, dx0_a, W, j, cos, sin, sv, G, early)
        else:
            dx = _rec_bwd(tag, x0, dmix, dx0_a, W, j, sv, G, early)
        order = grads_ready(f"l{layer}_late", {}, {(k, i): G[k][i] for k, i in _layer_weights(layer)[:-2]
                                                  if not k.endswith("w_out")})
    big = {k for k, _ in BIG}
    return loss, dx, {k: jnp.stack(v) for k, v in G.items() if k not in big}


def _layer_weights(layer):
    j = layer // 2
    mixer = ["hyb_w_in", "hyb_w_out"] if layer % 2 == 0 else ["rec_w_in", "rec_w_out", "rec_w_a", "rec_w_x"]
    return [(k, j) for k in mixer] + [("mlp_w1", layer), ("mlp_w2", layer)]


def _my_coords():
    return lax.axis_index("x"), lax.axis_index("y"), lax.axis_index("c")


def _all_gather(name, arrays):
    na = len(arrays)

    def body(*refs):
        x_refs, out_refs = refs[:na], refs[na:2 * na]
        send_sems, recv_sems, local_sems = refs[2 * na:]
        x, y, c = _my_coords()
        me, sibling = (x, y, c), (x, y, 1 - c)
        chips = [(1 - x, y), (x, 1 - y), (1 - x, 1 - y)]

        def blk(a, px, py, pc):
            return out_refs[a].at[4 * px + 2 * py + pc]

        def copy(a, k, block, to, src=None):
            return pltpu.make_async_remote_copy(
                src_ref=blk(a, *block) if src is None else src, dst_ref=blk(a, *block),
                send_sem=send_sems.at[a, k], recv_sem=recv_sems.at[a, k],
                device_id=to, device_id_type=pl.DeviceIdType.MESH)

        mine = [pltpu.make_async_copy(x_refs[a], blk(a, *me), local_sems.at[a]) for a in range(na)]
        for cp in mine:
            cp.start()
        first = []
        for a in range(na):
            first.append(copy(a, 0, me, sibling, src=x_refs[a]))
            first += [copy(a, 1 + j, me, (*chip, c), src=x_refs[a]) for j, chip in enumerate(chips)]
        for cp in first:
            cp.start()
        passed = []
        for a in range(na):
            for j, chip in enumerate(chips):
                copy(a, 1 + j, (*chip, c), me).wait_recv()
                passed.append(copy(a, 4 + j, (*chip, c), sibling))
                passed[-1].start()
        for a in range(na):
            copy(a, 0, sibling, me).wait_recv()
            for j, chip in enumerate(chips):
                copy(a, 4 + j, (*chip, 1 - c), me).wait_recv()
        for cp in first + passed:
            cp.wait_send()
        for cp in mine:
            cp.wait()

    return pl.pallas_call(
        body, name=name,
        out_shape=[jax.ShapeDtypeStruct((N_DEV,) + a.shape, a.dtype) for a in arrays],
        in_specs=[pl.BlockSpec(memory_space=pl.ANY)] * na,
        out_specs=[pl.BlockSpec(memory_space=pl.ANY)] * na,
        scratch_shapes=[pltpu.SemaphoreType.DMA((na, 7)), pltpu.SemaphoreType.DMA((na, 7)),
                        pltpu.SemaphoreType.DMA((na,))],
    )(*arrays)


_HBM = pl.BlockSpec(memory_space=pltpu.HBM)
_SEM = pl.BlockSpec(memory_space=pltpu.SEMAPHORE)


def _flip(k, x, y, c):
    return ((1 - x) if k & 4 else x, (1 - y) if k & 2 else y, (1 - c) if k & 1 else c)


_PEERS = {"gather": (1, 2, 4, 6), "scatter": (1, 2, 3, 4, 5, 6, 7)}


def _push_copies(kind, x_refs, land_refs, send_sems, recv_sems, local_sems):
    x, y, c = _my_coords()
    me = 4 * x + 2 * y + c
    peers = _PEERS[kind]
    remote, local = [], []
    for a in range(len(x_refs)):
        local.append(pltpu.make_async_copy(x_refs[a] if kind == "gather" else x_refs[a].at[me], land_refs[a].at[me],
                                           local_sems.at[a]))
        for n, k in enumerate(peers):
            px, py, pc = _flip(k, x, y, c)
            remote.append(pltpu.make_async_remote_copy(
                src_ref=x_refs[a] if kind == "gather" else x_refs[a].at[4 * px + 2 * py + pc],
                dst_ref=land_refs[a].at[me],
                send_sem=send_sems.at[a * len(peers) + n], recv_sem=recv_sems.at[a * len(peers) + n],
                device_id=(px, py, pc), device_id_type=pl.DeviceIdType.MESH))
    return remote, local


def _pass_to_sibling(name, lands):
    na = len(lands)
    chips = (2, 4, 6)

    def body(*refs):
        out_refs, send_sems, recv_sems = refs[na:2 * na], refs[2 * na], refs[2 * na + 1]
        x, y, c = _my_coords()
        cps = []
        for a in range(na):
            for n, k in enumerate(chips):
                px, py, _ = _flip(k, x, y, c)
                cps.append(pltpu.make_async_remote_copy(
                    src_ref=out_refs[a].at[4 * px + 2 * py + c], dst_ref=out_refs[a].at[4 * px + 2 * py + c],
                    send_sem=send_sems.at[a * 3 + n], recv_sem=recv_sems.at[a * 3 + n],
                    device_id=(x, y, 1 - c), device_id_type=pl.DeviceIdType.MESH))
        for cp in cps:
            cp.start()
        for a in range(na):
            for n, k in enumerate(chips):
                px, py, _ = _flip(k, x, y, c)
                blk = out_refs[a].at[4 * px + 2 * py + (1 - c)]
                pltpu.make_async_remote_copy(src_ref=blk, dst_ref=blk, send_sem=send_sems.at[a * 3 + n],
                                             recv_sem=recv_sems.at[a * 3 + n], device_id=(x, y, 1 - c),
                                             device_id_type=pl.DeviceIdType.MESH).wait_recv()
        for cp in cps:
            cp.wait_send()

    return pl.pallas_call(
        body, name=name,
        out_shape=[jax.ShapeDtypeStruct(l.shape, l.dtype) for l in lands],
        in_specs=[pl.BlockSpec(memory_space=pl.ANY)] * na,
        out_specs=[pl.BlockSpec(memory_space=pl.ANY)] * na,
        input_output_aliases={a: a for a in range(na)},
        scratch_shapes=[pltpu.SemaphoreType.DMA((3 * na,)), pltpu.SemaphoreType.DMA((3 * na,))],
    )(*lands)


_SIDE_EFFECT = pltpu.CompilerParams(has_side_effects=pltpu.SideEffectType.DATAFLOW_SIDE_EFFECTING)


def _push_start(name, kind, srcs, lands):
    na = len(srcs)

    def body(*refs):
        remote, local = _push_copies(kind, refs[:na], refs[na:2 * na], *refs[2 * na:2 * na + 3])
        for cp in remote + local:
            cp.start()
        token = refs[-1]
        token[...] = jnp.zeros_like(token)

    arrays = list(srcs) + list(lands)
    n_remote = na * len(_PEERS[kind])
    res = pl.pallas_call(
        body, name=name,
        out_shape=(pltpu.SemaphoreType.DMA((n_remote,)), pltpu.SemaphoreType.DMA((n_remote,)),
                   pltpu.SemaphoreType.DMA((na,)), *[pltpu.HBM(t.shape, t.dtype) for t in arrays],
                   jax.ShapeDtypeStruct((SUBLANE, LANE), f32)),
        in_specs=[_HBM] * (2 * na),
        out_specs=(_SEM, _SEM, _SEM, *[_HBM] * (2 * na), pl.BlockSpec(memory_space=pltpu.VMEM)),
        input_output_aliases={i: 3 + i for i in range(2 * na)},
        compiler_params=_SIDE_EFFECT,
    )(*[pltpu.with_memory_space_constraint(t, pltpu.HBM) for t in arrays])
    return list(res[:3]), res[3:3 + na], res[3 + na:3 + 2 * na], res[-1][:1, :1]


def _push_wait(name, kind, sems, srcs, lands, after):
    na = len(srcs)

    def body(*refs):
        remote, local = _push_copies(kind, refs[:na], refs[na:2 * na], *refs[2 * na:2 * na + 3])
        for cp in remote:
            cp.wait_send()
            cp.wait_recv()
        for cp in local:
            cp.wait()

    arrays = list(srcs) + list(lands)
    res = pl.pallas_call(
        body, name=name,
        out_shape=tuple(pltpu.HBM(t.shape, t.dtype) for t in arrays),
        in_specs=[_HBM] * (2 * na) + [_SEM] * 3 + [pl.BlockSpec(memory_space=pl.ANY)],
        out_specs=tuple([_HBM] * (2 * na)),
        input_output_aliases={i: i for i in range(2 * na)},
        compiler_params=_SIDE_EFFECT,
    )(*arrays, *sems, after)
    return res[na:]


def _sum_blocks(name, land):
    _, R, n = land.shape
    tr = R

    def body(l_ref, o_ref):
        acc = l_ref[0].astype(f32)
        for s in range(1, N_DEV):
            acc = acc + l_ref[s].astype(f32)
        o_ref[...] = acc

    return pl.pallas_call(
        body, name=name, grid=(R // tr,),
        in_specs=[pl.BlockSpec((N_DEV, tr, n), lambda i: (0, i, 0))],
        out_specs=pl.BlockSpec((tr, n), lambda i: (i, 0)),
        out_shape=jax.ShapeDtypeStruct((R, n), f32),
        compiler_params=_cparams("parallel"),
    )(land)


def _adamw(name, w, g, m, v):
    shape = w.shape
    last = shape[-1]
    rows = math.prod(shape[:-1])
    tm = 256 if rows % 256 == 0 and rows > 256 else rows
    w2, g2, m2, v2 = (t.reshape(rows, last) for t in (w, g, m, v))

    def body(w_ref, g_ref, m_ref, v_ref, d_ref, mo_ref, vo_ref):
        gg = g_ref[...]
        mn = ADAM_B1 * m_ref[...] + (1.0 - ADAM_B1) * gg
        vn = ADAM_B2 * v_ref[...] + (1.0 - ADAM_B2) * jnp.square(gg)
        m_hat = mn / (1.0 - ADAM_B1 ** ADAM_STEP)
        v_hat = vn / (1.0 - ADAM_B2 ** ADAM_STEP)
        d_ref[...] = -ADAM_LR * (m_hat / (jnp.sqrt(v_hat) + ADAM_EPS) + ADAM_WD * w_ref[...])
        mo_ref[...] = mn
        vo_ref[...] = vn

    spec = pl.BlockSpec((tm, last), lambda i: (i, 0))
    d, mn, vn = pl.pallas_call(
        body, name=name, grid=(rows // tm,), in_specs=[spec] * 4, out_specs=[spec] * 3,
        out_shape=[jax.ShapeDtypeStruct((rows, last), f32)] * 3,
        compiler_params=_cparams("parallel"),
    )(w2, g2, m2, v2)
    return d.reshape(shape), mn.reshape(shape), vn.reshape(shape)


def _adamw_land(name, lands, w, m, v, tm=256):
    L = len(lands)
    _, R, C = lands[0].shape
    tm = min(tm, R)

    def body(*refs):
        l_refs, (w_ref, m_ref, v_ref, g_ref, d_ref, mo_ref, vo_ref) = refs[:L], refs[L:]
        for k in range(L):
            @pl.when(pl.program_id(0) == k)
            def _(k=k):
                gg = l_refs[k][0].astype(f32)
                for s in range(1, N_DEV):
                    gg = gg + l_refs[k][s].astype(f32)
                g_ref[...] = gg
                mn = ADAM_B1 * m_ref[...] + (1.0 - ADAM_B1) * gg
                vn = ADAM_B2 * v_ref[...] + (1.0 - ADAM_B2) * jnp.square(gg)
                m_hat = mn / (1.0 - ADAM_B1 ** ADAM_STEP)
                v_hat = vn / (1.0 - ADAM_B2 ** ADAM_STEP)
                d_ref[...] = -ADAM_LR * (m_hat / (jnp.sqrt(v_hat) + ADAM_EPS) + ADAM_WD * w_ref[...])
                mo_ref[...] = mn
                vo_ref[...] = vn

    land_specs = [pl.BlockSpec((N_DEV, tm, C), lambda l, i, k=k: (0, jnp.where(l == k, i, 0), 0)) for k in range(L)]
    spec = pl.BlockSpec((None, tm, C), lambda l, i: (l, i, 0))
    return pl.pallas_call(
        body, name=name, grid=(L, R // tm),
        in_specs=land_specs + [spec] * 3,
        out_specs=[spec] * 4,
        out_shape=[jax.ShapeDtypeStruct((L, R, C), f32)] * 4,
        compiler_params=_cparams("arbitrary", "arbitrary"),
    )(*lands, w, m, v)


BIG = [("hyb_w_in", 2), ("hyb_w_out", 1), ("rec_w_in", 2), ("rec_w_out", 1), ("rec_w_a", 2), ("rec_w_x", 2),
       ("mlp_w1", 2), ("mlp_w2", 1)]
SMALL = [("hyb_conv_w", 2), ("rec_conv_w", 2), ("rec_conv_b", 1), ("rec_b_a", 1), ("rec_b_x", 1), ("rec_lambda", 1)]
REPL = ["hyb_sinks", "hyb_a_log", "hyb_dt_bias", "hyb_norm_w", "ln1_g", "ln1_b", "ln2_g", "ln2_b"]
WEIGHTS = ["hyb_w_in", "hyb_sinks", "hyb_conv_w", "hyb_a_log", "hyb_dt_bias", "hyb_norm_w", "hyb_w_out", "rec_w_in",
           "rec_conv_w", "rec_conv_b", "rec_w_a", "rec_b_a", "rec_w_x", "rec_b_x", "rec_lambda", "rec_w_out",
           "ln1_g", "ln1_b", "mlp_w1", "mlp_w2", "ln2_g", "ln2_b"]


def _pack_rows(parts, dtype, row_mult):
    lead = parts[0].shape[:-1]
    flat = jnp.concatenate([p.astype(dtype) for p in parts], axis=-1)
    n = flat.shape[-1]
    unit = row_mult * LANE
    pad = (-n) % unit
    if pad:
        flat = jnp.concatenate([flat, jnp.zeros(lead + (pad,), dtype)], axis=-1)
    return flat.reshape(lead + ((n + pad) // LANE, LANE))


def _gather_full(gathered, shard_shapes, table):
    flat = gathered.reshape(N_DEV, -1)
    out, off = {}, 0
    for name, ax in table:
        shp = shard_shapes[name]
        n = math.prod(shp)
        arr = flat[:, off:off + n].reshape((N_DEV,) + shp)
        off += n
        arr = jnp.moveaxis(arr, 0, ax)
        out[name] = arr.reshape(shp[:ax] + (N_DEV * shp[ax],) + shp[ax + 1:])
    return out


def _matmul_layouts(tag, gw):
    out = {}
    bw = D_MODEL // LRU_BLOCKS
    for k, g in gw.items():
        L = g.shape[1]
        if k == "hyb_w_in":
            out[k] = _merge_cols(f"{tag}_w_in_merge", g)
        elif k in ("hyb_w_out", "rec_w_out"):
            out[k] = jnp.swapaxes(g, 0, 1).reshape(L, D_MODEL, D_MODEL)
        elif k in ("rec_w_a", "rec_w_x"):
            out[k] = jnp.moveaxis(g, 0, 2).reshape(L, LRU_BLOCKS, bw, bw)
        else:
            out[k] = g
    return out


def kernel(x, hyb_w_in, hyb_sinks, hyb_conv_w, hyb_a_log, hyb_dt_bias, hyb_norm_w, hyb_w_out, rec_w_in, rec_conv_w, rec_conv_b, rec_w_a, rec_b_a, rec_w_x, rec_b_x, rec_lambda, rec_w_out, ln1_g, ln1_b, mlp_w1, mlp_w2, ln2_g, ln2_b, loss_target, m_hyb_w_in, m_hyb_sinks, m_hyb_conv_w, m_hyb_a_log, m_hyb_dt_bias, m_hyb_norm_w, m_hyb_w_out, m_rec_w_in, m_rec_conv_w, m_rec_conv_b, m_rec_w_a, m_rec_b_a, m_rec_w_x, m_rec_b_x, m_rec_lambda, m_rec_w_out, m_ln1_g, m_ln1_b, m_mlp_w1, m_mlp_w2, m_ln2_g, m_ln2_b, v_hyb_w_in, v_hyb_sinks, v_hyb_conv_w, v_hyb_a_log, v_hyb_dt_bias, v_hyb_norm_w, v_hyb_w_out, v_rec_w_in, v_rec_conv_w, v_rec_conv_b, v_rec_w_a, v_rec_b_a, v_rec_w_x, v_rec_b_x, v_rec_lambda, v_rec_w_out, v_ln1_g, v_ln1_b, v_mlp_w1, v_mlp_w2, v_ln2_g, v_ln2_b):
    args = locals()
    w = {k: args[k] for k in WEIGHTS}
    m = {k: args["m_" + k] for k in WEIGHTS}
    v = {k: args["v_" + k] for k in WEIGHTS}
    shard_shapes = {k: tuple(t.shape) for k, t in w.items()}
    xi, yi, ci = _my_coords()
    me = 4 * xi + 2 * yi + ci

    in_flight = {}

    def install(tag, names, got):
        for (k, i), arr in zip(names, _matmul_layouts(tag, {k: g for (k, _), g in zip(names, got)}).values()):
            W[k][i] = arr

    def start_gather(tag, names):
        srcs = [w[k][i:i + 1].astype(bf16) for k, i in names]
        *pending, zero = _push_start(f"gather_{tag}_start", "gather", srcs,
                                     [lax.empty((N_DEV,) + s.shape, bf16) for s in srcs])
        in_flight[tag] = (names, pending)
        return zero

    def finish_gather(tag, after):
        names, pending = in_flight.pop(tag)
        half = _push_wait(f"gather_{tag}_wait", "gather", *pending, after)
        install(tag, names, _pass_to_sibling(f"gather_{tag}_pass", half))

    def started(k, zero):
        W[k] = W[k] + zero

    def mixer_w(layer):
        return _layer_weights(layer)[:-2]

    def mlp_w(layer):
        return _layer_weights(layer)[-2:]

    gathered0 = _all_gather("gather_first", [w[k][i:i + 1].astype(bf16) for k, i in mixer_w(0)]
                            + [_pack_rows([w[k].reshape(-1) for k, _ in SMALL], f32, SUBLANE)])
    W = _gather_full(gathered0[-1], shard_shapes, SMALL)
    W.update({k: w[k] for k in REPL})
    W.update({k: {} for k, _ in BIG})
    install("l0a", mixer_w(0), gathered0[:-1])
    started("hyb_sinks", start_gather("l0b", mlp_w(0)) + start_gather("l1a", mixer_w(1)))

    def load_layer(layer, stage, after):
        if stage == 0:
            if layer > 0:
                finish_gather(f"l{layer}a", after)
            if 0 < layer < DEPTH - 1:
                started("hyb_sinks" if layer % 2 == 0 else "rec_conv_b",
                        start_gather(f"l{layer + 1}a", mixer_w(layer + 1)))
        if stage == 2:
            finish_gather(f"l{layer}b", after)
            if layer < DEPTH - 1:
                started("ln2_g", start_gather(f"l{layer + 1}b", mlp_w(layer + 1)))

    grads_in_flight = {}

    def grads_ready(tag, a, b):
        g = {**a, **b}
        srcs = list(g.values())
        *pending, zero = _push_start(f"scatter_{tag}_start", "scatter", srcs, [lax.empty(s.shape, bf16) for s in srcs])
        grads_in_flight[tag] = (list(g.keys()), pending)
        return zero

    loss_local, grad_x, G = _local_step(x[0], loss_target[0], W, load_layer, grads_ready)
    loss = lax.psum(loss_local, MESH_AXES)

    landed = {}

    def land(tag, after):
        keys, pending = grads_in_flight[tag]
        landed.update(zip(keys, _push_wait(f"scatter_{tag}_wait", "scatter", *pending, after)))

    tags = list(grads_in_flight)
    for tag in tags[:-1]:
        land(tag, grad_x)
    rest = _pack_rows([G[k].reshape(-1) for k, _ in SMALL] + [G[k].reshape(-1) for k in REPL], f32, SUBLANE)
    g_rest = _sum_blocks("sum_rest", _all_gather("gather_rest", [rest])[0]).reshape(-1)

    grads, delta, new_m, new_v = {}, {}, {}, {}

    def adamw_big(k):
        shp = shard_shapes[k]
        s3 = (shp[0], math.prod(shp[1:-1]), shp[-1])
        lands = [landed[(k, i)].reshape((N_DEV,) + s3[1:]) for i in range(shp[0])]
        res = _adamw_land("adamw_" + k, lands, w[k].reshape(s3), m[k].reshape(s3), v[k].reshape(s3))
        grads[k], delta[k], new_m[k], new_v[k] = (r.reshape(shp) for r in res)

    late = {k for k, _ in grads_in_flight[tags[-1]][0]}
    for k in [k for k, _ in BIG if k not in late]:
        adamw_big(k)
        done = new_v[k]
    land(tags[-1], done)
    for k in [k for k, _ in BIG if k in late]:
        adamw_big(k)
    off = 0
    for k, ax in SMALL:
        full_shape = G[k].shape
        n = math.prod(full_shape)
        full = g_rest[off:off + n].reshape(full_shape)
        off += n
        s = shard_shapes[k][ax]
        grads[k] = lax.dynamic_slice_in_dim(full, me * s, s, axis=ax)
    for k in REPL:
        n = math.prod(shard_shapes[k])
        grads[k] = g_rest[off:off + n].reshape(shard_shapes[k])
        off += n

    for k in [k for k, _ in SMALL] + REPL:
        delta[k], new_m[k], new_v[k] = _adamw("adamw_" + k, w[k], grads[k], m[k], v[k])

    return (loss, grad_x[None], *[grads[k] for k in WEIGHTS], *[delta[k] for k in WEIGHTS],
            *[new_m[k] for k in WEIGHTS], *[new_v[k] for k in WEIGHTS])
```

```python
import functools
import math

import jax
import jax.numpy as jnp
from jax import lax
from jax.experimental import pallas as pl
from jax.experimental.pallas import tpu as pltpu

f32 = jnp.float32
bf16 = jnp.bfloat16

N_DEV = 8
D_MODEL = 1024
DEPTH = 4
A_HEAD_DIM = 64
A_Q_HEADS = 8
WINDOW = 128
ROPE_THETA = 10000.0
B_HEADS = 4
B_HEAD_DIM = 128
B_CHUNK = 64
LRU_BLOCKS = 4
LRU_C = 8.0
D_FF = 4 * D_MODEL
HYB_PROJ = 2824
HYB_PROJ_PAD = 3072
DN_ALPHA = (2 * DEPTH) ** 0.25
LN_EPS = 1e-5
NORM_EPS = 1e-6
ADAM_LR = 0.001
ADAM_B1 = 0.9
ADAM_B2 = 0.999
ADAM_EPS = 1e-08
ADAM_WD = 0.01
ADAM_STEP = 10

LANE = 128
SUBLANE = 8
VMEM_LIMIT = 48 * 1024 * 1024

CB_QA, CB_KA, CB_VA, CB_CONV, CB_Z, CB_LG = 0, 4, 5, 6, 18, 22

MESH_AXES = ("x", "y", "c")


def _cparams(*sem):
    return pltpu.CompilerParams(dimension_semantics=sem, vmem_limit_bytes=VMEM_LIMIT)


def _dot(a, b, dims, precision=None):
    return lax.dot_general(a, b, (dims, ((), ())), preferred_element_type=f32, precision=precision)


NN = ((1,), (0,))
NT = ((1,), (1,))
TN = ((0,), (0,))


def _mat_spec(arr, kind, lead, br, bc, rb, cb):
    if kind == "plain":
        return pl.BlockSpec((br, bc), lambda i, j, k: (rb(i, j, k), cb(i, j, k)))
    if kind == "lead":
        return pl.BlockSpec((None, br, bc), lambda i, j, k: (lead, rb(i, j, k), cb(i, j, k)))
    if kind == "devcol":
        assert bc == arr.shape[-1]
        return pl.BlockSpec((None, None, br, bc), lambda i, j, k: (cb(i, j, k), lead, rb(i, j, k), 0))
    assert kind == "devrow" and br == arr.shape[-2]
    return pl.BlockSpec((None, None, br, bc), lambda i, j, k: (rb(i, j, k), lead, 0, cb(i, j, k)))


def _mm(name, a, b, mode, *, b_kind="plain", b_lead=0, o_kind="plain", epilogue=None, extras=(), params=(),
        out_dtypes=(f32,), tm=1024, tn=1024, tk=None):
    if tk is None:
        tk = 512 if mode == "tn" else 1024
    if b_kind in ("plain", "lead"):
        b_rows, b_cols = b.shape[-2:]
    elif b_kind == "devcol":
        b_rows, b_cols = b.shape[-2], N_DEV * b.shape[-1]
    else:
        b_rows, b_cols = N_DEV * b.shape[-2], b.shape[-1]
    if mode == "nn":
        (M, K), (K2, N) = a.shape, (b_rows, b_cols)
    elif mode == "nt":
        (M, K), (N, K2) = a.shape, (b_rows, b_cols)
    else:
        (K, M), (K2, N) = a.shape, (b_rows, b_cols)
    assert K == K2, (name, a.shape, b.shape, mode)
    tm, tn, tk = min(tm, M), min(tn, N), min(tk, K)
    cols_are_n = mode != "nt"
    if b_kind == "devcol":
        tn, tk = (b.shape[-1], tk) if cols_are_n else (tn, b.shape[-1])
    if b_kind == "devrow":
        tn, tk = (tn, b.shape[-2]) if cols_are_n else (b.shape[-2], tk)
    shard = N // N_DEV
    if o_kind == "devcol":
        tn = max(shard, tn // shard * shard)
    assert M % tm == 0 and N % tn == 0 and K % tk == 0, (name, M, N, K, tm, tn, tk)
    nk = K // tk
    dims = {"nn": NN, "nt": NT, "tn": TN}[mode]
    n_ex, n_out = len(extras) + len(params), len(out_dtypes)

    def body(*refs):
        a_ref, b_ref = refs[:2]
        ex = refs[2:2 + n_ex]
        outs = refs[2 + n_ex:2 + n_ex + n_out]
        acc = refs[-1]
        k = pl.program_id(2)

        @pl.when(k == 0)
        def _():
            acc[...] = jnp.zeros_like(acc)

        acc[...] += _dot(a_ref[...].astype(bf16), b_ref[...].astype(bf16), dims)

        @pl.when(k == nk - 1)
        def _():
            r = acc[...]
            res = epilogue(r, *[e[...] for e in ex]) if epilogue is not None else (r,)
            for o, v in zip(outs, res):
                if o_kind == "plain":
                    o[...] = v.astype(o.dtype)
                else:
                    for q in range(tn // shard):
                        o[q] = v[:, q * shard:(q + 1) * shard].astype(o.dtype)

    if mode == "tn":
        a_spec = pl.BlockSpec((tk, tm), lambda i, j, k: (k, i))
    else:
        a_spec = pl.BlockSpec((tm, tk), lambda i, j, k: (i, k))
    jb, kb = (lambda i, j, k: j), (lambda i, j, k: k)
    if mode == "nt":
        b_spec = _mat_spec(b, b_kind, b_lead, tn, tk, jb, kb)
    else:
        b_spec = _mat_spec(b, b_kind, b_lead, tk, tn, kb, jb)
    e_spec = pl.BlockSpec((tm, tn), lambda i, j, k: (i, j))
    if o_kind == "plain":
        o_spec, o_shape = e_spec, (M, N)
    else:
        o_spec, o_shape = pl.BlockSpec((tn // shard, tm, shard), lambda i, j, k: (j, i, 0)), (N_DEV, M, shard)
    res = pl.pallas_call(
        body, name=name,
        grid=(M // tm, N // tn, nk),
        in_specs=[a_spec, b_spec] + [e_spec] * len(extras)
        + [pl.BlockSpec(p.shape, lambda i, j, k: (0, 0)) for p in params],
        out_specs=[o_spec] * n_out,
        out_shape=[jax.ShapeDtypeStruct(o_shape, dt) for dt in out_dtypes],
        scratch_shapes=[pltpu.VMEM((tm, tn), f32)],
        compiler_params=_cparams("parallel", "parallel", "arbitrary"),
    )(a, b, *extras, *params)
    return res[0] if n_out == 1 else res


def _row_spec(tm, cb, width):
    assert (cb * LANE) % width == 0
    blk = (cb * LANE) // width
    return pl.BlockSpec((tm, width), lambda i: (i, blk))


def _whole_spec(p):
    nd = p.ndim
    return pl.BlockSpec(p.shape, lambda i: (0,) * nd)


def _tl_fwd(name, fn, rows, params, out_widths, out_dtypes, tm=256):
    T = rows[0][0].shape[0]
    tm = min(tm, T)
    nr, npar = len(rows), len(params)

    def body(*refs):
        vals = [r[...] for r in refs[:nr + npar]]
        outs = fn(*vals)
        for o, v in zip(refs[nr + npar:], outs):
            o[...] = v.astype(o.dtype)

    res = pl.pallas_call(
        body, name=name, grid=(T // tm,),
        in_specs=[_row_spec(tm, cb, w) for (_, cb, w) in rows] + [_whole_spec(p) for p in params],
        out_specs=[pl.BlockSpec((tm, w), lambda i: (i, 0)) for w in out_widths],
        out_shape=[jax.ShapeDtypeStruct((T, w), dt) for w, dt in zip(out_widths, out_dtypes)],
        compiler_params=_cparams("parallel"),
    )(*[r[0] for r in rows], *params)
    return res


def _tl_bwd(name, fn, rows, params, cot_rows, cot_fn=None, wide=None, tm=256):
    T = rows[0][0].shape[0]
    tm = min(tm, T)
    nr, npar, nc = len(rows), len(params), len(cot_rows)
    place = [(wide or {}).get(k, (w, 0)) for k, (_, _, w) in enumerate(rows)]

    def body(*refs):
        vals = [r[...] for r in refs[:nr + npar]]
        cots = [r[...] for r in refs[nr + npar:nr + npar + nc]]
        outs = refs[nr + npar + nc:]
        cot = tuple(cot_fn(*cots)) if cot_fn is not None else tuple(cots)
        _, vjp = jax.vjp(fn, *vals)
        grads = vjp(cot)
        for o, g in zip(outs[:nr], grads[:nr]):
            o[...] = g.astype(o.dtype)
        i = pl.program_id(0)
        for o, g in zip(outs[nr:], grads[nr:]):
            @pl.when(i == 0)
            def _(o=o):
                o[...] = jnp.zeros_like(o)
            o[...] += g

    res = pl.pallas_call(
        body, name=name, grid=(T // tm,),
        in_specs=[_row_spec(tm, cb, w) for (_, cb, w) in rows] + [_whole_spec(p) for p in params]
        + [_row_spec(tm, cb, w) for (_, cb, w) in cot_rows],
        out_specs=[pl.BlockSpec((tm, w), lambda i, blk=blk: (i, blk)) for (_, _, w), (_, blk) in zip(rows, place)]
        + [_whole_spec(p) for p in params],
        out_shape=[jax.ShapeDtypeStruct((T, total), f32) for total, _ in place]
        + [jax.ShapeDtypeStruct(p.shape, f32) for p in params],
        compiler_params=_cparams("arbitrary"),
    )(*[r[0] for r in rows], *params, *[r[0] for r in cot_rows])
    return res[:nr], res[nr:]


def _ln_res_fn(x, mix, g, b):
    pre = DN_ALPHA * x + mix
    mu = jnp.mean(pre, axis=-1, keepdims=True)
    var = jnp.mean(jnp.square(pre - mu), axis=-1, keepdims=True)
    return ((pre - mu) * lax.rsqrt(var + LN_EPS) * g + b,)


@jax.custom_jvp
def _expm1(x):
    small = jnp.abs(x) < 0.3
    xs = jnp.where(small, x, 0.0)
    poly = xs * (1.0 + xs * (1 / 2 + xs * (1 / 6 + xs * (1 / 24 + xs * (1 / 120 + xs * (
        1 / 720 + xs * (1 / 5040 + xs * (1 / 40320 + xs * (1 / 362880)))))))))
    return jnp.where(small, poly, jnp.exp(x) - 1.0)


@_expm1.defjvp
def _expm1_jvp(primals, tangents):
    (x,), (t,) = primals, tangents
    return _expm1(x), t * jnp.exp(x)


def _rglru_pre_fn(pre_r, pre_i, xc, b_a, b_x, lam):
    r = jax.nn.sigmoid(pre_r + b_a)
    i = jax.nn.sigmoid(pre_i + b_x)
    log_a = -LRU_C * r * jax.nn.softplus(-lam)
    a = jnp.exp(log_a)
    b = jnp.sqrt(-_expm1(2.0 * log_a)) * (i * xc)
    return a, b


def _rec_gate_fn(h, gate):
    return (h * jax.nn.gelu(gate),)


def _loss_head(y, t, tm=256):
    T, Dm = y.shape

    def body(y_ref, t_ref, dy_ref, loss_ref):
        e = y_ref[...] - t_ref[...]
        dy_ref[...] = e * (1.0 / Dm)

        @pl.when(pl.program_id(0) == 0)
        def _():
            loss_ref[...] = jnp.zeros_like(loss_ref)

        loss_ref[...] += 0.5 * jnp.sum(jnp.mean(e * e, axis=-1, keepdims=True), axis=0, keepdims=True)

    dy, loss = pl.pallas_call(
        body, name="loss_head", grid=(T // tm,),
        in_specs=[pl.BlockSpec((tm, Dm), lambda i: (i, 0))] * 2,
        out_specs=[pl.BlockSpec((tm, Dm), lambda i: (i, 0)), pl.BlockSpec((SUBLANE, LANE), lambda i: (0, 0))],
        out_shape=[jax.ShapeDtypeStruct((T, Dm), f32), jax.ShapeDtypeStruct((SUBLANE, LANE), f32)],
        compiler_params=_cparams("arbitrary"),
    )(y, t)
    return loss[0, 0], dy


def _conv_fwd(name, x, cb0, nblk, w, bias, tm=2048):
    T = x.shape[0]
    tm = min(tm, T)
    hb = tm // SUBLANE
    has_b = bias is not None

    def body(*refs):
        cur, prev, w_ref = refs[:3]
        b_ref = refs[3] if has_b else None
        o = refs[-1]
        i = pl.program_id(1)
        p = jnp.where(i > 0, prev[...], 0.0)
        xcat = jnp.concatenate([p, cur[...]], axis=0)
        acc = cur[...] * w_ref[3:4, :]
        for j in range(3):
            acc = acc + pltpu.roll(xcat, 3 - j, axis=0)[SUBLANE:] * w_ref[j:j + 1, :]
        if has_b:
            acc = acc + b_ref[...]
        o[...] = acc

    in_specs = [
        pl.BlockSpec((tm, LANE), lambda c, i: (i, cb0 + c)),
        pl.BlockSpec((SUBLANE, LANE), lambda c, i: (jnp.maximum(i * hb - 1, 0), cb0 + c)),
        pl.BlockSpec((4, LANE), lambda c, i: (0, c)),
    ]
    args = [x, x, w]
    if has_b:
        in_specs.append(pl.BlockSpec((1, LANE), lambda c, i: (0, c)))
        args.append(bias)
    return pl.pallas_call(
        body, name=name, grid=(nblk, T // tm),
        in_specs=in_specs,
        out_specs=pl.BlockSpec((tm, LANE), lambda c, i: (i, c)),
        out_shape=jax.ShapeDtypeStruct((T, nblk * LANE), f32),
        compiler_params=_cparams("parallel", "parallel"),
    )(*args)


def _conv_bwd(name, dy, x, cb0, nblk, w, into, into_cb, tm=2048):
    T = x.shape[0]
    tm = min(tm, T)
    hb = tm // SUBLANE
    nt = T // tm

    def body(dcur, dnext, xcur, xprev, w_ref, _, dx_ref, dw_ref, db_ref):
        i = pl.program_id(1)
        d = dcur[...]
        dn = jnp.where(i < nt - 1, dnext[...], 0.0)
        dcat = jnp.concatenate([d, dn], axis=0)
        acc = d * w_ref[3:4, :]
        for j in range(3):
            s = 3 - j
            acc = acc + pltpu.roll(dcat, tm + SUBLANE - s, axis=0)[:tm] * w_ref[j:j + 1, :]
        dx_ref[...] = acc

        p = jnp.where(i > 0, xprev[...], 0.0)
        xcat = jnp.concatenate([p, xcur[...]], axis=0)
        rows = [jnp.sum(d * pltpu.roll(xcat, 3 - j, axis=0)[SUBLANE:], axis=0, keepdims=True) for j in range(3)]
        rows.append(jnp.sum(d * xcur[...], axis=0, keepdims=True))
        rows.append(jnp.zeros((SUBLANE - 4, LANE), f32))

        @pl.when(i == 0)
        def _():
            dw_ref[...] = jnp.zeros_like(dw_ref)
            db_ref[...] = jnp.zeros_like(db_ref)

        dw_ref[...] += jnp.concatenate(rows, axis=0)
        db_ref[...] += jnp.broadcast_to(jnp.sum(d, axis=0, keepdims=True), (SUBLANE, LANE))

    nh = T // SUBLANE
    dx, dw, db = pl.pallas_call(
        body, name=name, grid=(nblk, nt),
        in_specs=[
            pl.BlockSpec((tm, LANE), lambda c, i: (i, c)),
            pl.BlockSpec((SUBLANE, LANE), lambda c, i: (jnp.minimum((i + 1) * hb, nh - 1), c)),
            pl.BlockSpec((tm, LANE), lambda c, i: (i, cb0 + c)),
            pl.BlockSpec((SUBLANE, LANE), lambda c, i: (jnp.maximum(i * hb - 1, 0), cb0 + c)),
            pl.BlockSpec((4, LANE), lambda c, i: (0, c)),
            pl.BlockSpec(memory_space=pl.ANY),
        ],
        out_specs=[
            pl.BlockSpec((tm, LANE), lambda c, i: (i, into_cb + c)),
            pl.BlockSpec((SUBLANE, LANE), lambda c, i: (0, c)),
            pl.BlockSpec((SUBLANE, LANE), lambda c, i: (0, c)),
        ],
        out_shape=[jax.ShapeDtypeStruct(into.shape, f32),
                   jax.ShapeDtypeStruct((SUBLANE, nblk * LANE), f32),
                   jax.ShapeDtypeStruct((SUBLANE, nblk * LANE), f32)],
        input_output_aliases={5: 0},
        compiler_params=_cparams("parallel", "arbitrary"),
    )(dy, dy, x, x, w, into)
    return dx, dw[:4], db[0]


@functools.partial(jax.custom_vjp, nondiff_argnums=(1,))
def _lroll(x, s):
    return pltpu.roll(x, s, axis=1)


def _lroll_fwd(x, s):
    return _lroll(x, s), None


def _lroll_bwd(s, _, g):
    return (_lroll(g, (LANE - s) % LANE),)


_lroll.defvjp(_lroll_fwd, _lroll_bwd)


def _rope_tables(T):
    half = A_HEAD_DIM // 2
    inv_freq = ROPE_THETA ** (-jnp.arange(half, dtype=f32) / half)
    ang = jnp.arange(T, dtype=f32)[:, None] * inv_freq[None, :]
    cos, sin = jnp.cos(ang), jnp.sin(ang)
    return jnp.tile(jnp.concatenate([cos, cos], axis=1), (1, 2)), jnp.tile(jnp.concatenate([-sin, sin], axis=1), (1, 2))


def _attn_block_fn(n, q, kp, kc, vp, vc, cq, sq, cp, sp, sinks):
    W = WINDOW
    lane = lax.broadcasted_iota(jnp.int32, (W, LANE), 1)
    lo_half = (lane % A_HEAD_DIM) < (A_HEAD_DIM // 2)
    lane8 = lax.broadcasted_iota(jnp.int32, sinks.shape, 1)

    def rope(x, c, s):
        return x * c + jnp.where(lo_half, _lroll(x, LANE - A_HEAD_DIM // 2), _lroll(x, A_HEAD_DIM // 2)) * s

    k2 = jnp.concatenate([rope(kp, cp, sp), rope(kc, cq, sq)], axis=0).astype(bf16)
    v2 = jnp.concatenate([vp, vc], axis=0).astype(bf16)
    qs, sink_rows = [], []
    for t in range(4):
        qt = rope(q[:, LANE * t:LANE * (t + 1)], cq, sq)
        g = t // 2
        for hh in range(2):
            qa = jnp.where((lane // A_HEAD_DIM) == hh, qt, 0.0)
            qs.append(_lroll(qa, A_HEAD_DIM) if hh != g else qa)
            sink = jnp.sum(jnp.where(lane8 == 2 * t + hh, sinks, 0.0), axis=1, keepdims=True)
            sink_rows.append(jnp.broadcast_to(sink, (W, 1)))
    qall = jnp.concatenate(qs, axis=0).astype(bf16)
    sink = jnp.concatenate(sink_rows, axis=0)
    row = lax.broadcasted_iota(jnp.int32, (A_Q_HEADS * W, 2 * W), 0) & (W - 1)
    col = lax.broadcasted_iota(jnp.int32, (A_Q_HEADS * W, 2 * W), 1)
    dist = row + W - col
    mask = (dist >= 0) & (dist < W) & ((col >= W) | (n > 0))
    s = jnp.where(mask, _dot(qall, k2, NT) * (A_HEAD_DIM ** -0.5), -jnp.inf)
    m = jnp.maximum(jnp.max(s, axis=-1, keepdims=True), sink)
    e = jnp.exp(s - m)
    p = e / (jnp.sum(e, axis=-1, keepdims=True) + jnp.exp(sink - m))
    o = _dot(p.astype(bf16), v2, NN)
    outs = []
    for t in range(4):
        g = t // 2
        ot = jnp.zeros((W, LANE), f32)
        for hh in range(2):
            j = 2 * t + hh
            oj = jnp.where((lane // A_HEAD_DIM) == g, o[W * j:W * (j + 1)], 0.0)
            ot = ot + (_lroll(oj, A_HEAD_DIM) if hh != g else oj)
        outs.append(ot)
    return jnp.concatenate(outs, axis=1)


def _attn_specs():
    W = WINDOW
    prev = lambda n: jnp.maximum(n - 1, 0)
    return [
        pl.BlockSpec((W, 4 * LANE), lambda n: (n, CB_QA // 4)),
        pl.BlockSpec((W, LANE), lambda n: (prev(n), CB_KA)),
        pl.BlockSpec((W, LANE), lambda n: (n, CB_KA)),
        pl.BlockSpec((W, LANE), lambda n: (prev(n), CB_VA)),
        pl.BlockSpec((W, LANE), lambda n: (n, CB_VA)),
        pl.BlockSpec((W, LANE), lambda n: (n, 0)),
        pl.BlockSpec((W, LANE), lambda n: (n, 0)),
        pl.BlockSpec((W, LANE), lambda n: (prev(n), 0)),
        pl.BlockSpec((W, LANE), lambda n: (prev(n), 0)),
        pl.BlockSpec((1, A_Q_HEADS), lambda n: (0, 0)),
    ]


def _attn_fwd(name, proj, cos, sin, sinks):
    T = proj.shape[0]
    W = WINDOW

    def body(*refs):
        o = refs[-1]
        o[...] = _attn_block_fn(pl.program_id(0), *[r[...] for r in refs[:-1]])

    return pl.pallas_call(
        body, name=name, grid=(T // W,),
        in_specs=_attn_specs(),
        out_specs=pl.BlockSpec((W, 4 * LANE), lambda n: (n, 0)),
        out_shape=jax.ShapeDtypeStruct((T, 2 * 4 * LANE), f32),
        compiler_params=_cparams("parallel"),
    )(proj, proj, proj, proj, proj, cos, sin, cos, sin, sinks)


def _attn_bwd(name, proj, cos, sin, sinks, d_oab):
    T = proj.shape[0]
    W = WINDOW
    Q = 4 * LANE

    def body(*refs):
        ins = [r[...] for r in refs[:10]]
        do = refs[10][...]
        d_ref, ds_ref = refs[11:]
        n = pl.program_id(0)
        _, vjp = jax.vjp(functools.partial(_attn_block_fn, n), *ins)
        dq, dkp, dkc, dvp, dvc, _, _, _, _, dsk = vjp(do)

        @pl.when(n == 0)
        def _():
            d_ref[:, Q:] = jnp.zeros((T, 2 * LANE), f32)
            ds_ref[...] = jnp.zeros_like(ds_ref)

        cur = pl.ds(pl.multiple_of(n * W, W), W)
        d_ref[cur, :Q] = dq
        d_ref[cur, Q:Q + LANE] += dkc
        d_ref[cur, Q + LANE:] += dvc
        ds_ref[...] += dsk

        @pl.when(n > 0)
        def _():
            prv = pl.ds(pl.multiple_of((n - 1) * W, W), W)
            d_ref[prv, Q:Q + LANE] += dkp
            d_ref[prv, Q + LANE:] += dvp

    return pl.pallas_call(
        body, name=name, grid=(T // W,),
        in_specs=_attn_specs() + [pl.BlockSpec((W, Q), lambda n: (n, 0))],
        out_specs=[pl.BlockSpec((T, Q + 2 * LANE), lambda n: (0, 0)),
                   pl.BlockSpec((1, A_Q_HEADS), lambda n: (0, 0))],
        out_shape=[jax.ShapeDtypeStruct((T, HYB_PROJ_PAD), f32), jax.ShapeDtypeStruct((1, A_Q_HEADS), f32)],
        compiler_params=_cparams("arbitrary"),
    )(proj, proj, proj, proj, proj, cos, sin, cos, sin, sinks, d_oab)


def _bdot(spec, a, b, precision=None):
    return jnp.einsum(spec, a, b, preferred_element_type=f32, precision=precision)


@jax.custom_vjp
def _tri_inv(a):
    H, C, _ = a.shape
    B = 2 * SUBLANE
    nb = C // B
    r = lax.broadcasted_iota(jnp.int32, (C, C), 0)
    c = lax.broadcasted_iota(jnp.int32, (C, C), 1)
    a4 = jnp.where((r // B) == (c // B), a, 0.0).reshape(H, nb, B, C)
    t4 = jnp.broadcast_to(jnp.where(r == c, 1.0, 0.0).astype(f32), a.shape).reshape(H, nb, B, C)
    for j in range(B - 1):
        col = jnp.concatenate([a4[:, b:b + 1, :, B * b + j:B * b + j + 1] for b in range(nb)], axis=1)
        t4 = t4 - col * t4[:, :, j:j + 1, :]
    x = t4.reshape(H, C, C)
    hi = lax.Precision.HIGHEST
    while B < C:
        m = jnp.where(((r // (2 * B)) == (c // (2 * B))) & ((r // B) > (c // B)), a, 0.0)
        x = x - _bdot("hij,hjk->hik", x, _bdot("hij,hjk->hik", m, x, precision=hi), precision=hi)
        B *= 2
    return x


def _tri_inv_fwd(a):
    t = _tri_inv(a)
    return t, t


def _tri_inv_bwd(t, g):
    C = t.shape[-1]
    r = lax.broadcasted_iota(jnp.int32, (C, C), 0)
    c = lax.broadcasted_iota(jnp.int32, (C, C), 1)
    x = _bdot("hki,hkj->hij", t, g, precision=lax.Precision.HIGHEST)
    y = _bdot("hik,hjk->hij", x, t, precision=lax.Precision.HIGHEST)
    return (jnp.where(r > c, -y, 0.0),)


_tri_inv.defvjp(_tri_inv_fwd, _tri_inv_bwd)


@jax.custom_vjp
def _tri_inv_saved(a, t):
    return t


_tri_inv_saved.defvjp(lambda a, t: (t, t), lambda t, g: (_tri_inv_bwd(t, g)[0], jnp.zeros_like(t)))


def _silu(x):
    return x * jax.nn.sigmoid(x)


def _l2n(x):
    return x * lax.rsqrt(jnp.sum(x * x, axis=-1, keepdims=True) + NORM_EPS)


def _delta_chunk_fn(cq, ck, cv, z, lg, a_log, dt_bias, norm_w, S, t_saved=None, want_t=False):
    C = B_CHUNK
    lane = lax.broadcasted_iota(jnp.int32, (C, LANE), 1)
    pick = lambda l0: jnp.concatenate(
        [jnp.sum(jnp.where(lane == l0 + h, lg, 0.0), axis=1, keepdims=True)[None] for h in range(B_HEADS)], axis=0)
    bl, al = pick(0), pick(B_HEADS)
    q = _l2n(_silu(cq)) * (B_HEAD_DIM ** -0.5)
    k = _l2n(_silu(ck))
    v = _silu(cv)
    beta = jax.nn.sigmoid(bl)
    g = -jnp.exp(a_log) * jax.nn.softplus(al + dt_bias)
    r = lax.broadcasted_iota(jnp.int32, (C, C), 0)
    c = lax.broadcasted_iota(jnp.int32, (C, C), 1)
    eye = r == c
    g_row = jnp.sum(jnp.where(eye, g, 0.0), axis=1, keepdims=True)
    gc = jnp.sum(jnp.where(c <= r, g_row, 0.0), axis=2, keepdims=True)
    gc_row = jnp.sum(jnp.where(eye, gc, 0.0), axis=1, keepdims=True)
    decay_incl = jnp.exp(jnp.where(r >= c, gc - gc_row, -jnp.inf))
    decay_strict = jnp.where(r > c, decay_incl, 0.0)
    kb = k * beta
    vb = v * beta
    kbf = k.astype(bf16)
    a_mat = _bdot("hik,hjk->hij", kb.astype(bf16), kbf) * decay_strict
    t_f32 = _tri_inv(a_mat) if t_saved is None else _tri_inv_saved(a_mat, t_saved)
    t_mat = t_f32.astype(bf16)
    eg = jnp.exp(gc)
    u = _bdot("hij,hjv->hiv", t_mat, vb.astype(bf16))
    w = _bdot("hij,hjk->hik", t_mat, (kb * eg).astype(bf16))
    qk = _bdot("hik,hjk->hij", q.astype(bf16), kbf) * decay_incl
    g_last = jnp.sum(g, axis=1, keepdims=True)
    k_tail = k * jnp.exp(g_last - gc)
    Sb = S.astype(bf16)
    v_new = u - _bdot("hck,hkv->hcv", w.astype(bf16), Sb)
    o = _bdot("hck,hkv->hcv", (q * eg).astype(bf16), Sb) + _bdot("hij,hjv->hiv", qk.astype(bf16), v_new.astype(bf16))
    S_new = S * jnp.exp(g_last) + _bdot("hck,hcv->hkv", k_tail.astype(bf16), v_new.astype(bf16))
    ob = o * lax.rsqrt(jnp.mean(o * o, axis=-1, keepdims=True) + NORM_EPS) * norm_w
    return (ob * _silu(z), S_new) + ((t_f32,) if want_t else ())


def _delta_in_specs(rev, N):
    C = B_CHUNK
    ix = (lambda n: N - 1 - n) if rev else (lambda n: n)
    specs = [pl.BlockSpec((C, 3 * B_HEADS * LANE), lambda n: (ix(n), 0))]
    specs += [pl.BlockSpec((C, LANE), lambda n, h=h: (ix(n), CB_Z + h)) for h in range(B_HEADS)]
    specs += [
        pl.BlockSpec((C, LANE), lambda n: (ix(n), CB_LG)),
        pl.BlockSpec((B_HEADS, 1, 1), lambda n: (0, 0, 0)),
        pl.BlockSpec((B_HEADS, 1, 1), lambda n: (0, 0, 0)),
        pl.BlockSpec((1, LANE), lambda n: (0, 0)),
    ]
    return specs


def _delta_inputs(c_ref, z_refs, lg, al, dt, nw):
    H = B_HEADS
    part = lambda p: jnp.stack([c_ref[:, LANE * (p * H + h):LANE * (p * H + h + 1)] for h in range(H)])
    return (part(0), part(1), part(2), jnp.stack([z[...] for z in z_refs]), lg[...], al[...], dt[...], nw[...])


def _delta_fwd(name, c, proj, a_log, dt_bias, norm_w, o_ab):
    T = c.shape[0]
    C = B_CHUNK
    N = T // C
    Dh = B_HEAD_DIM
    H = B_HEADS

    def body(*refs):
        c_ref, z_refs, (lg, al, dt, nw) = refs[0], refs[1:1 + H], refs[1 + H:5 + H]
        o_ref, s_ref, t_ref, S = refs[6 + H:]

        @pl.when(pl.program_id(0) == 0)
        def _():
            S[...] = jnp.zeros_like(S)

        s0 = S[...]
        s_ref[...] = s0
        ob, s1, t = _delta_chunk_fn(*_delta_inputs(c_ref, z_refs, lg, al, dt, nw), s0, want_t=True)
        for h in range(H):
            o_ref[:, LANE * h:LANE * (h + 1)] = ob[h]
        t_ref[...] = t
        S[...] = s1

    return pl.pallas_call(
        body, name=name, grid=(N,),
        in_specs=_delta_in_specs(False, N) + [pl.BlockSpec(memory_space=pl.ANY)],
        out_specs=[pl.BlockSpec((C, H * LANE), lambda n: (n, 1)),
                   pl.BlockSpec((H, None, Dh, Dh), lambda n: (0, n, 0, 0)),
                   pl.BlockSpec((H, None, C, C), lambda n: (0, n, 0, 0))],
        out_shape=[jax.ShapeDtypeStruct(o_ab.shape, f32), jax.ShapeDtypeStruct((H, N, Dh, Dh), f32),
                   jax.ShapeDtypeStruct((H, N, C, C), f32)],
        input_output_aliases={5 + H: 0},
        scratch_shapes=[pltpu.VMEM((H, Dh, Dh), f32)],
        compiler_params=_cparams("arbitrary"),
    )(c, *([proj] * H), proj, a_log, dt_bias, norm_w, o_ab)


def _delta_bwd(name, c, proj, a_log, dt_bias, norm_w, s_saved, t_saved, d_oab, dproj):
    T = c.shape[0]
    C = B_CHUNK
    N = T // C
    Dh = B_HEAD_DIM
    H = B_HEADS

    def body(*refs):
        c_ref, z_refs, (lg, al, dt, nw) = refs[0], refs[1:1 + H], refs[1 + H:5 + H]
        s_ref, t_ref, do_ref = refs[5 + H:8 + H]
        dc, dtail, dal, ddt, dnw, dS = refs[9 + H:]

        @pl.when(pl.program_id(0) == 0)
        def _():
            dS[...] = jnp.zeros_like(dS)
            dal[...] = jnp.zeros_like(dal)
            ddt[...] = jnp.zeros_like(ddt)
            dnw[...] = jnp.zeros_like(dnw)

        _, vjp = jax.vjp(functools.partial(_delta_chunk_fn, t_saved=t_ref[...]),
                         *_delta_inputs(c_ref, z_refs, lg, al, dt, nw), s_ref[...])
        do = jnp.stack([do_ref[:, LANE * h:LANE * (h + 1)] for h in range(H)])
        g = vjp((do, dS[...]))
        for h in range(H):
            for p in range(3):
                dc[:, LANE * (p * H + h):LANE * (p * H + h + 1)] = g[p][h]
            dtail[:, LANE * h:LANE * (h + 1)] = g[3][h]
        dtail[:, LANE * H:LANE * (H + 1)] = g[4]
        dtail[:, LANE * (H + 1):] = jnp.zeros((C, LANE), f32)
        dal[...] += g[5]
        ddt[...] += g[6]
        dnw[...] += g[7]
        dS[...] = g[8]

    rn = lambda n: N - 1 - n
    return pl.pallas_call(
        body, name=name, grid=(N,),
        in_specs=_delta_in_specs(True, N) + [
            pl.BlockSpec((H, None, Dh, Dh), lambda n: (0, rn(n), 0, 0)),
            pl.BlockSpec((H, None, C, C), lambda n: (0, rn(n), 0, 0)),
            pl.BlockSpec((C, H * LANE), lambda n: (rn(n), 1)),
            pl.BlockSpec(memory_space=pl.ANY),
        ],
        out_specs=[
            pl.BlockSpec((C, 3 * H * LANE), lambda n: (rn(n), 0)),
            pl.BlockSpec((C, (H + 2) * LANE), lambda n: (rn(n), CB_Z // (H + 2))),
            pl.BlockSpec((H, 1, 1), lambda n: (0, 0, 0)),
            pl.BlockSpec((H, 1, 1), lambda n: (0, 0, 0)),
            pl.BlockSpec((1, LANE), lambda n: (0, 0)),
        ],
        out_shape=[jax.ShapeDtypeStruct((T, 3 * H * Dh), f32), jax.ShapeDtypeStruct(dproj.shape, f32),
                   jax.ShapeDtypeStruct((H, 1, 1), f32), jax.ShapeDtypeStruct((H, 1, 1), f32),
                   jax.ShapeDtypeStruct((1, LANE), f32)],
        input_output_aliases={8 + H: 1},
        scratch_shapes=[pltpu.VMEM((H, Dh, Dh), f32)],
        compiler_params=_cparams("arbitrary"),
    )(c, *([proj] * H), proj, a_log, dt_bias, norm_w, s_saved, t_saved, d_oab, dproj)


def _blockdiag_fwd(name, xc, w_a, w_x, tm=512):
    T, Wd = xc.shape
    bw = Wd // LRU_BLOCKS
    tm = min(tm, T)

    def body(x_ref, wa_ref, wx_ref, oa, ox):
        xb = x_ref[...].astype(bf16)
        oa[...] = _dot(xb, wa_ref[...].astype(bf16), NN)
        ox[...] = _dot(xb, wx_ref[...].astype(bf16), NN)

    xs = pl.BlockSpec((tm, bw), lambda i, h: (i, h))
    ws = pl.BlockSpec((None, bw, bw), lambda i, h: (h, 0, 0))
    return pl.pallas_call(
        body, name=name, grid=(T // tm, LRU_BLOCKS), in_specs=[xs, ws, ws], out_specs=[xs, xs],
        out_shape=[jax.ShapeDtypeStruct((T, Wd), f32)] * 2,
        compiler_params=_cparams("parallel", "parallel"),
    )(xc, w_a, w_x)


def _blockdiag_bwd_dx(name, dpr, dpi, w_a, w_x, addend, tm=512):
    T, Wd = dpr.shape
    bw = Wd // LRU_BLOCKS
    tm = min(tm, T)

    def body(dr, di, wa_ref, wx_ref, add, o):
        o[...] = (add[...] + _dot(dr[...].astype(bf16), wa_ref[...].astype(bf16), NT)
                  + _dot(di[...].astype(bf16), wx_ref[...].astype(bf16), NT))

    xs = pl.BlockSpec((tm, bw), lambda i, h: (i, h))
    ws = pl.BlockSpec((None, bw, bw), lambda i, h: (h, 0, 0))
    return pl.pallas_call(
        body, name=name, grid=(T // tm, LRU_BLOCKS), in_specs=[xs, xs, ws, ws, xs], out_specs=xs,
        out_shape=jax.ShapeDtypeStruct((T, Wd), f32),
        compiler_params=_cparams("parallel", "parallel"),
    )(dpr, dpi, w_a, w_x, addend)


def _blockdiag_bwd_dw(name, xc, dpr, dpi, tk=512):
    T, Wd = xc.shape
    bw = Wd // LRU_BLOCKS
    tk = min(tk, T)

    def body(x_ref, dr, di, oa, ox):
        @pl.when(pl.program_id(1) == 0)
        def _():
            oa[...] = jnp.zeros_like(oa)
            ox[...] = jnp.zeros_like(ox)

        xb = x_ref[...].astype(bf16)
        oa[...] += _dot(xb, dr[...].astype(bf16), TN)
        ox[...] += _dot(xb, di[...].astype(bf16), TN)

    xs = pl.BlockSpec((tk, bw), lambda h, k: (k, h))
    ws = pl.BlockSpec((None, bw, bw), lambda h, k: (h, 0, 0))
    return pl.pallas_call(
        body, name=name, grid=(LRU_BLOCKS, T // tk), in_specs=[xs, xs, xs], out_specs=[ws, ws],
        out_shape=[jax.ShapeDtypeStruct((LRU_BLOCKS, bw, bw), f32)] * 2,
        compiler_params=_cparams("parallel", "arbitrary"),
    )(xc, dpr, dpi)


def _scan(name, a, b, reverse, tt=512, cb=512):
    T, Wd = a.shape
    tt, cb = min(tt, T), min(cb, Wd)
    nt = T // tt
    ng = tt // SUBLANE

    def body(a_ref, b_ref, *rest):
        outs, (carry, carry_a) = rest[:-2], rest[-2:]

        @pl.when(pl.program_id(1) == 0)
        def _():
            carry[...] = jnp.zeros_like(carry)
            carry_a[...] = jnp.zeros_like(carry_a)

        row = lax.broadcasted_iota(jnp.int32, (SUBLANE, cb), 0)

        def step(gi, c):
            hp, ap = c
            g = (ng - 1 - gi) if reverse else gi
            off = pl.multiple_of(g * SUBLANE, SUBLANE)
            A = a_ref[pl.ds(off, SUBLANE), :]
            B = b_ref[pl.ds(off, SUBLANE), :]
            a_first = jnp.broadcast_to(A[0:1, :], (SUBLANE, cb))
            if reverse:
                A = jnp.where(row == SUBLANE - 1, ap, pltpu.roll(A, SUBLANE - 1, axis=0))
            for s in (1, 2, 4):
                sh = (SUBLANE - s) if reverse else s
                As = pltpu.roll(A, sh, axis=0)
                Bs = pltpu.roll(B, sh, axis=0)
                valid = (row < SUBLANE - s) if reverse else (row >= s)
                B = jnp.where(valid, A * Bs + B, B)
                A = jnp.where(valid, A * As, A)
            hcur = A * hp + B
            outs[0][pl.ds(off, SUBLANE), :] = hcur
            if not reverse:
                outs[1][pl.ds(off, SUBLANE), :] = jnp.where(row == 0, hp, pltpu.roll(hcur, 1, axis=0))
            edge = hcur[0:1, :] if reverse else hcur[SUBLANE - 1:SUBLANE, :]
            return jnp.broadcast_to(edge, (SUBLANE, cb)), a_first

        carry[...], carry_a[...] = lax.fori_loop(0, ng, step, (carry[...], carry_a[...]))

    ti = (lambda c, i: (nt - 1 - i, c)) if reverse else (lambda c, i: (i, c))
    spec = pl.BlockSpec((tt, cb), ti)
    n_out = 1 if reverse else 2
    res = pl.pallas_call(
        body, name=name, grid=(Wd // cb, nt), in_specs=[spec, spec], out_specs=[spec] * n_out,
        out_shape=[jax.ShapeDtypeStruct((T, Wd), f32)] * n_out,
        scratch_shapes=[pltpu.VMEM((SUBLANE, cb), f32), pltpu.VMEM((SUBLANE, cb), f32)],
        compiler_params=_cparams("parallel", "arbitrary"),
    )(a, b)
    return res[0] if reverse else res


def _relu2_epilogue(r):
    h = jnp.maximum(r, 0.0)
    return r, h * h


def _drelu2_epilogue(r, a):
    return (r * (2.0 * jnp.maximum(a, 0.0)),)


def _add_epilogue(r, e):
    return (r + e,)


def _merge_cols(name, g, tm=256):
    _, L, R, s = g.shape

    def body(g_ref, o_ref):
        for d in range(N_DEV):
            o_ref[:, s * d:s * (d + 1)] = g_ref[d].astype(bf16)
        o_ref[:, N_DEV * s:] = jnp.zeros((tm, HYB_PROJ_PAD - N_DEV * s), bf16)

    return pl.pallas_call(
        body, name=name, grid=(L, R // tm),
        in_specs=[pl.BlockSpec((N_DEV, None, tm, s), lambda l, i: (0, l, i, 0))],
        out_specs=pl.BlockSpec((None, tm, HYB_PROJ_PAD), lambda l, i: (l, i, 0)),
        out_shape=jax.ShapeDtypeStruct((L, R, HYB_PROJ_PAD), bf16),
        compiler_params=_cparams("parallel", "parallel"),
    )(g)


def _split_cols(name, dw, tm=256):
    R = dw.shape[0]
    s = HYB_PROJ // N_DEV

    def body(g_ref, o_ref):
        for d in range(N_DEV):
            o_ref[d] = g_ref[:, s * d:s * (d + 1)].astype(bf16)

    return pl.pallas_call(
        body, name=name, grid=(R // tm,),
        in_specs=[pl.BlockSpec((tm, HYB_PROJ_PAD), lambda i: (i, 0))],
        out_specs=pl.BlockSpec((N_DEV, tm, s), lambda i: (0, i, 0)),
        out_shape=jax.ShapeDtypeStruct((N_DEV, R, s), bf16),
        compiler_params=_cparams("parallel"),
    )(dw)


def _rows_to_dev(dw):
    nb, r, c = dw.shape
    t = dw.reshape(nb, N_DEV, r // N_DEV, c)
    return jnp.moveaxis(t, 1, 0).reshape(N_DEV, nb * (r // N_DEV), c).astype(bf16)


def _ln_epilogue(r, x, g, b):
    return r, _ln_res_fn(x, r, g, b)[0]


def _hybrid_fwd(tag, x, W, j, cos, sin, ln, before_out):
    proj = _mm(f"{tag}_proj", x, W["hyb_w_in"][j], "nn", b_kind="lead", b_lead=0)
    o_a = _attn_fwd(f"{tag}_attn", proj, cos, sin, W["hyb_sinks"][j][None, :])
    c = _conv_fwd(f"{tag}_conv", proj, CB_CONV, 12, W["hyb_conv_w"][j], None)
    o_ab, s_saved, t_saved = _delta_fwd(f"{tag}_delta", c, proj, W["hyb_a_log"][j].reshape(B_HEADS, 1, 1),
                                        W["hyb_dt_bias"][j].reshape(B_HEADS, 1, 1), W["hyb_norm_w"][j][None, :], o_a)
    before_out(o_ab)
    mix, x1 = _mm(f"{tag}_out", o_ab, W["hyb_w_out"][j], "nn", b_kind="lead", b_lead=0, epilogue=_ln_epilogue,
                  extras=(x,), params=ln, out_dtypes=(f32, f32), tm=512)
    return mix, x1, (proj, c, s_saved, t_saved, o_ab)


def _hybrid_bwd(tag, x, dmix, addend, W, j, cos, sin, saved, G, send_early):
    proj, c, s_saved, t_saved, o_ab = saved
    T = x.shape[0]
    d_oab = _mm(f"{tag}_dout", dmix, W["hyb_w_out"][j], "nt", b_kind="lead", b_lead=0)
    G["hyb_w_out"][j] = _mm(f"{tag}_dwout", o_ab, dmix, "tn", out_dtypes=(bf16,)).reshape(N_DEV, -1, D_MODEL)
    sinks = W["hyb_sinks"][j][None, :] + send_early({("hyb_w_out", j): G["hyb_w_out"][j]})
    dproj, dsinks = _attn_bwd(f"{tag}_dattn", proj, cos, sin, sinks, d_oab)
    a_log = W["hyb_a_log"][j].reshape(B_HEADS, 1, 1)
    dt_bias = W["hyb_dt_bias"][j].reshape(B_HEADS, 1, 1)
    dc, dproj, dal, ddt, dnw = _delta_bwd(f"{tag}_ddelta", c, proj, a_log, dt_bias, W["hyb_norm_w"][j][None, :],
                                          s_saved, t_saved, d_oab, dproj)
    dproj, dconv_w, _ = _conv_bwd(f"{tag}_dconv", dc, proj, CB_CONV, 12, W["hyb_conv_w"][j], dproj, CB_CONV)
    dx = _mm(f"{tag}_dx", dproj, W["hyb_w_in"][j], "nt", b_kind="lead", b_lead=0, epilogue=_add_epilogue,
             extras=(addend,))
    G["hyb_w_in"][j] = _split_cols(f"{tag}_dwin_split", _mm(f"{tag}_dwin", x, dproj, "tn", tn=1536))
    G["hyb_sinks"][j] = dsinks[0]
    G["hyb_conv_w"][j] = dconv_w
    G["hyb_a_log"][j] = dal.reshape(B_HEADS)
    G["hyb_dt_bias"][j] = ddt.reshape(B_HEADS)
    G["hyb_norm_w"][j] = dnw[0]
    return dx


def _rec_fwd(tag, x, W, j, ln, before_out):
    Wd = D_MODEL
    proj = _mm(f"{tag}_proj", x, W["rec_w_in"][j], "nn", b_kind="devcol", b_lead=0)
    xc = _conv_fwd(f"{tag}_conv", proj, 0, Wd // LANE, W["rec_conv_w"][j], W["rec_conv_b"][j][None, :])
    pre_r, pre_i = _blockdiag_fwd(f"{tag}_gates", xc, W["rec_w_a"][j][0], W["rec_w_x"][j][0])
    pars = [W["rec_b_a"][j][None, :], W["rec_b_x"][j][None, :], W["rec_lambda"][j][None, :]]
    a, b = _tl_fwd(f"{tag}_pre", _rglru_pre_fn, [(pre_r, 0, Wd), (pre_i, 0, Wd), (xc, 0, Wd)], pars, [Wd, Wd], [f32, f32])
    h, h_prev = _scan(f"{tag}_scan", a, b, False)
    (hg,) = _tl_fwd(f"{tag}_gate", _rec_gate_fn, [(h, 0, Wd), (proj, Wd // LANE, Wd)], [], [Wd], [f32])
    before_out(hg)
    mix, x1 = _mm(f"{tag}_out", hg, W["rec_w_out"][j], "nn", b_kind="lead", b_lead=0, epilogue=_ln_epilogue,
                  extras=(x,), params=ln, out_dtypes=(f32, f32), tm=512)
    return mix, x1, (proj, xc, pre_r, pre_i, a, h, h_prev, hg)


def _rec_bwd(tag, x, dmix, addend, W, j, saved, G, send_early):
    proj, xc, pre_r, pre_i, a, h, h_prev, hg = saved
    Wd = D_MODEL
    dhg = _mm(f"{tag}_dout", dmix, W["rec_w_out"][j], "nt", b_kind="lead", b_lead=0)
    G["rec_w_out"][j] = _mm(f"{tag}_dwout", hg, dmix, "tn", out_dtypes=(bf16,)).reshape(N_DEV, -1, D_MODEL)
    sent = send_early({("rec_w_out", j): G["rec_w_out"][j]})
    (dh, dproj), _ = _tl_bwd(f"{tag}_dgate", _rec_gate_fn, [(h, 0, Wd), (proj, Wd // LANE, Wd)], [], [(dhg, 0, Wd)],
                             wide={1: (2 * Wd, 1)})
    lam_t = _scan(f"{tag}_dscan", a, dh, True)
    pars = [W["rec_b_a"][j][None, :] + sent, W["rec_b_x"][j][None, :], W["rec_lambda"][j][None, :]]
    (dpr, dpi, dxc1), (db_a, db_x, dlam) = _tl_bwd(
        f"{tag}_dpre", _rglru_pre_fn, [(pre_r, 0, Wd), (pre_i, 0, Wd), (xc, 0, Wd)], pars,
        [(lam_t, 0, Wd), (h_prev, 0, Wd)], cot_fn=lambda lt, hp: (lt * hp, lt))
    dxc = _blockdiag_bwd_dx(f"{tag}_dgates_dx", dpr, dpi, W["rec_w_a"][j][0], W["rec_w_x"][j][0], dxc1)
    dwa, dwx = _blockdiag_bwd_dw(f"{tag}_dgates_dw", xc, dpr, dpi)
    G["rec_w_a"][j], G["rec_w_x"][j] = _rows_to_dev(dwa), _rows_to_dev(dwx)
    dproj, dconv_w, dconv_b = _conv_bwd(f"{tag}_dconv", dxc, proj, 0, Wd // LANE, W["rec_conv_w"][j], dproj, 0)
    dx = _mm(f"{tag}_dx", dproj, W["rec_w_in"][j], "nt", b_kind="devcol", b_lead=0, epilogue=_add_epilogue,
             extras=(addend,))
    G["rec_w_in"][j] = _mm(f"{tag}_dwin", x, dproj, "tn", o_kind="devcol", out_dtypes=(bf16,), tn=2048)
    G["rec_conv_w"][j] = dconv_w
    G["rec_conv_b"][j] = dconv_b
    G["rec_b_a"][j] = db_a[0]
    G["rec_b_x"][j] = db_x[0]
    G["rec_lambda"][j] = dlam[0]
    return dx


def _local_step(x, target, W, load_layer, grads_ready):
    T = x.shape[0]
    cos, sin = _rope_tables(T)
    saved = []
    for layer in range(DEPTH):
        j = layer // 2
        tag = f"L{layer}"
        load_layer(layer, 0, x)
        ln1 = (W["ln1_g"][layer][None, :], W["ln1_b"][layer][None, :])
        before_out = functools.partial(load_layer, layer, 1)
        if layer % 2 == 0:
            mix, x1, sv = _hybrid_fwd(tag, x, W, j, cos, sin, ln1, before_out)
        else:
            mix, x1, sv = _rec_fwd(tag, x, W, j, ln1, before_out)
        load_layer(layer, 2, x1)
        a, h2 = _mm(f"{tag}_mlp1", x1, W["mlp_w1"][layer], "nn", b_kind="devcol", b_lead=0, epilogue=_relu2_epilogue,
                    out_dtypes=(f32, bf16))
        ln2 = (W["ln2_g"][layer][None, :], W["ln2_b"][layer][None, :])
        y, x2 = _mm(f"{tag}_mlp2", h2, W["mlp_w2"][layer], "nn", b_kind="devrow", b_lead=0, epilogue=_ln_epilogue,
                    extras=(x1,), params=ln2, out_dtypes=(f32, f32))
        saved.append((x, sv, mix, x1, a, h2, y))
        x = x2
    loss, dx = _loss_head(x, target)

    G = {k: [None] * (DEPTH if k.startswith(("ln", "mlp")) else DEPTH // 2) for k in (
        "hyb_w_in", "hyb_sinks", "hyb_conv_w", "hyb_a_log", "hyb_dt_bias", "hyb_norm_w", "hyb_w_out",
        "rec_w_in", "rec_conv_w", "rec_conv_b", "rec_w_a", "rec_b_a", "rec_w_x", "rec_b_x", "rec_lambda", "rec_w_out",
        "ln1_g", "ln1_b", "mlp_w1", "mlp_w2", "ln2_g", "ln2_b")}
    order = jnp.zeros((1, 1), f32)
    for layer in reversed(range(DEPTH)):
        j = layer // 2
        tag = f"L{layer}"
        x0, sv, mix, x1, a, h2, y = saved[layer]
        ln2 = [W["ln2_g"][layer][None, :] + order, W["ln2_b"][layer][None, :]]
        (dx1_a, dy), (dg2, db2) = _tl_bwd(f"{tag}_dln2", _ln_res_fn, [(x1, 0, D_MODEL), (y, 0, D_MODEL)], ln2,
                                          [(dx, 0, D_MODEL)])
        G["ln2_g"][layer], G["ln2_b"][layer] = dg2[0], db2[0]
        da = _mm(f"{tag}_dmlp2", dy, W["mlp_w2"][layer], "nt", b_kind="devrow", b_lead=0, epilogue=_drelu2_epilogue,
                 extras=(a,), out_dtypes=(bf16,))
        G["mlp_w2"][layer] = _mm(f"{tag}_dw2", h2, dy, "tn", out_dtypes=(bf16,), tm=2048).reshape(N_DEV, -1, D_MODEL)
        dx1 = _mm(f"{tag}_dmlp1", da, W["mlp_w1"][layer], "nt", b_kind="devcol", b_lead=0, epilogue=_add_epilogue,
                  extras=(dx1_a,))
        G["mlp_w1"][layer] = _mm(f"{tag}_dw1", x1, da, "tn", o_kind="devcol", out_dtypes=(bf16,), tn=2048)
        ln1 = [W["ln1_g"][layer][None, :], W["ln1_b"][layer][None, :]]
        (dx0_a, dmix), (dg1, db1) = _tl_bwd(f"{tag}_dln1", _ln_res_fn, [(x0, 0, D_MODEL), (mix, 0, D_MODEL)], ln1,
                                            [(dx1, 0, D_MODEL)])
        G["ln1_g"][layer], G["ln1_b"][layer] = dg1[0], db1[0]
        early = functools.partial(grads_ready, f"l{layer}_early",
                                  {(k, layer): G[k][layer] for k in ("mlp_w1", "mlp_w2")})
        if layer % 2 == 0:
            dx = _hybrid_bwd(tag, x0, dmix, dx0_a, W, j, cos, sin, sv, G, early)
        else:
            dx = _rec_bwd(tag, x0, dmix, dx0_a, W, j, sv, G, early)
        order = grads_ready(f"l{layer}_late", {}, {(k, i): G[k][i] for k, i in _layer_weights(layer)[:-2]
                                                  if not k.endswith("w_out")})
    big = {k for k, _ in BIG}
    return loss, dx, {k: jnp.stack(v) for k, v in G.items() if k not in big}


def _layer_weights(layer):
    j = layer // 2
    mixer = ["hyb_w_in", "hyb_w_out"] if layer % 2 == 0 else ["rec_w_in", "rec_w_out", "rec_w_a", "rec_w_x"]
    return [(k, j) for k in mixer] + [("mlp_w1", layer), ("mlp_w2", layer)]


def _my_coords():
    return lax.axis_index("x"), lax.axis_index("y"), lax.axis_index("c")


def _all_gather(name, arrays):
    na = len(arrays)

    def body(*refs):
        x_refs, out_refs = refs[:na], refs[na:2 * na]
        send_sems, recv_sems, local_sems = refs[2 * na:]
        x, y, c = _my_coords()
        me, sibling = (x, y, c), (x, y, 1 - c)
        chips = [(1 - x, y), (x, 1 - y), (1 - x, 1 - y)]

        def blk(a, px, py, pc):
            return out_refs[a].at[4 * px + 2 * py + pc]

        def copy(a, k, block, to, src=None):
            return pltpu.make_async_remote_copy(
                src_ref=blk(a, *block) if src is None else src, dst_ref=blk(a, *block),
                send_sem=send_sems.at[a, k], recv_sem=recv_sems.at[a, k],
                device_id=to, device_id_type=pl.DeviceIdType.MESH)

        mine = [pltpu.make_async_copy(x_refs[a], blk(a, *me), local_sems.at[a]) for a in range(na)]
        for cp in mine:
            cp.start()
        first = []
        for a in range(na):
            first.append(copy(a, 0, me, sibling, src=x_refs[a]))
            first += [copy(a, 1 + j, me, (*chip, c), src=x_refs[a]) for j, chip in enumerate(chips)]
        for cp in first:
            cp.start()
        passed = []
        for a in range(na):
            for j, chip in enumerate(chips):
                copy(a, 1 + j, (*chip, c), me).wait_recv()
                passed.append(copy(a, 4 + j, (*chip, c), sibling))
                passed[-1].start()
        for a in range(na):
            copy(a, 0, sibling, me).wait_recv()
            for j, chip in enumerate(chips):
                copy(a, 4 + j, (*chip, 1 - c), me).wait_recv()
        for cp in first + passed:
            cp.wait_send()
        for cp in mine:
            cp.wait()

    return pl.pallas_call(
        body, name=name,
        out_shape=[jax.ShapeDtypeStruct((N_DEV,) + a.shape, a.dtype) for a in arrays],
        in_specs=[pl.BlockSpec(memory_space=pl.ANY)] * na,
        out_specs=[pl.BlockSpec(memory_space=pl.ANY)] * na,
        scratch_shapes=[pltpu.SemaphoreType.DMA((na, 7)), pltpu.SemaphoreType.DMA((na, 7)),
                        pltpu.SemaphoreType.DMA((na,))],
    )(*arrays)


_HBM = pl.BlockSpec(memory_space=pltpu.HBM)
_SEM = pl.BlockSpec(memory_space=pltpu.SEMAPHORE)


def _flip(k, x, y, c):
    return ((1 - x) if k & 4 else x, (1 - y) if k & 2 else y, (1 - c) if k & 1 else c)


_PEERS = {"gather": (1, 2, 4, 6), "scatter": (1, 2, 3, 4, 5, 6, 7)}


def _push_copies(kind, x_refs, land_refs, send_sems, recv_sems, local_sems):
    x, y, c = _my_coords()
    me = 4 * x + 2 * y + c
    peers = _PEERS[kind]
    remote, local = [], []
    for a in range(len(x_refs)):
        local.append(pltpu.make_async_copy(x_refs[a] if kind == "gather" else x_refs[a].at[me], land_refs[a].at[me],
                                           local_sems.at[a]))
        for n, k in enumerate(peers):
            px, py, pc = _flip(k, x, y, c)
            remote.append(pltpu.make_async_remote_copy(
                src_ref=x_refs[a] if kind == "gather" else x_refs[a].at[4 * px + 2 * py + pc],
                dst_ref=land_refs[a].at[me],
                send_sem=send_sems.at[a * len(peers) + n], recv_sem=recv_sems.at[a * len(peers) + n],
                device_id=(px, py, pc), device_id_type=pl.DeviceIdType.MESH))
    return remote, local


def _pass_to_sibling(name, lands):
    na = len(lands)
    chips = (2, 4, 6)

    def body(*refs):
        out_refs, send_sems, recv_sems = refs[na:2 * na], refs[2 * na], refs[2 * na + 1]
        x, y, c = _my_coords()
        cps = []
        for a in range(na):
            for n, k in enumerate(chips):
                px, py, _ = _flip(k, x, y, c)
                cps.append(pltpu.make_async_remote_copy(
                    src_ref=out_refs[a].at[4 * px + 2 * py + c], dst_ref=out_refs[a].at[4 * px + 2 * py + c],
                    send_sem=send_sems.at[a * 3 + n], recv_sem=recv_sems.at[a * 3 + n],
                    device_id=(x, y, 1 - c), device_id_type=pl.DeviceIdType.MESH))
        for cp in cps:
            cp.start()
        for a in range(na):
            for n, k in enumerate(chips):
                px, py, _ = _flip(k, x, y, c)
                blk = out_refs[a].at[4 * px + 2 * py + (1 - c)]
                pltpu.make_async_remote_copy(src_ref=blk, dst_ref=blk, send_sem=send_sems.at[a * 3 + n],
                                             recv_sem=recv_sems.at[a * 3 + n], device_id=(x, y, 1 - c),
                                             device_id_type=pl.DeviceIdType.MESH).wait_recv()
        for cp in cps:
            cp.wait_send()

    return pl.pallas_call(
        body, name=name,
        out_shape=[jax.ShapeDtypeStruct(l.shape, l.dtype) for l in lands],
        in_specs=[pl.BlockSpec(memory_space=pl.ANY)] * na,
        out_specs=[pl.BlockSpec(memory_space=pl.ANY)] * na,
        input_output_aliases={a: a for a in range(na)},
        scratch_shapes=[pltpu.SemaphoreType.DMA((3 * na,)), pltpu.SemaphoreType.DMA((3 * na,))],
    )(*lands)


_SIDE_EFFECT = pltpu.CompilerParams(has_side_effects=pltpu.SideEffectType.DATAFLOW_SIDE_EFFECTING)


def _push_start(name, kind, srcs, lands):
    na = len(srcs)

    def body(*refs):
        remote, local = _push_copies(kind, refs[:na], refs[na:2 * na], *refs[2 * na:2 * na + 3])
        for cp in remote + local:
            cp.start()
        token = refs[-1]
        token[...] = jnp.zeros_like(token)

    arrays = list(srcs) + list(lands)
    n_remote = na * len(_PEERS[kind])
    res = pl.pallas_call(
        body, name=name,
        out_shape=(pltpu.SemaphoreType.DMA((n_remote,)), pltpu.SemaphoreType.DMA((n_remote,)),
                   pltpu.SemaphoreType.DMA((na,)), *[pltpu.HBM(t.shape, t.dtype) for t in arrays],
                   jax.ShapeDtypeStruct((SUBLANE, LANE), f32)),
        in_specs=[_HBM] * (2 * na),
        out_specs=(_SEM, _SEM, _SEM, *[_HBM] * (2 * na), pl.BlockSpec(memory_space=pltpu.VMEM)),
        input_output_aliases={i: 3 + i for i in range(2 * na)},
        compiler_params=_SIDE_EFFECT,
    )(*[pltpu.with_memory_space_constraint(t, pltpu.HBM) for t in arrays])
    return list(res[:3]), res[3:3 + na], res[3 + na:3 + 2 * na], res[-1][:1, :1]


def _push_wait(name, kind, sems, srcs, lands, after):
    na = len(srcs)

    def body(*refs):
        remote, local = _push_copies(kind, refs[:na], refs[na:2 * na], *refs[2 * na:2 * na + 3])
        for cp in remote:
            cp.wait_send()
            cp.wait_recv()
        for cp in local:
            cp.wait()

    arrays = list(srcs) + list(lands)
    res = pl.pallas_call(
        body, name=name,
        out_shape=tuple(pltpu.HBM(t.shape, t.dtype) for t in arrays),
        in_specs=[_HBM] * (2 * na) + [_SEM] * 3 + [pl.BlockSpec(memory_space=pl.ANY)],
        out_specs=tuple([_HBM] * (2 * na)),
        input_output_aliases={i: i for i in range(2 * na)},
        compiler_params=_SIDE_EFFECT,
    )(*arrays, *sems, after)
    return res[na:]


def _sum_blocks(name, land):
    _, R, n = land.shape
    tr = R

    def body(l_ref, o_ref):
        acc = l_ref[0].astype(f32)
        for s in range(1, N_DEV):
            acc = acc + l_ref[s].astype(f32)
        o_ref[...] = acc

    return pl.pallas_call(
        body, name=name, grid=(R // tr,),
        in_specs=[pl.BlockSpec((N_DEV, tr, n), lambda i: (0, i, 0))],
        out_specs=pl.BlockSpec((tr, n), lambda i: (i, 0)),
        out_shape=jax.ShapeDtypeStruct((R, n), f32),
        compiler_params=_cparams("parallel"),
    )(land)


def _adamw(name, w, g, m, v):
    shape = w.shape
    last = shape[-1]
    rows = math.prod(shape[:-1])
    tm = 256 if rows % 256 == 0 and rows > 256 else rows
    w2, g2, m2, v2 = (t.reshape(rows, last) for t in (w, g, m, v))

    def body(w_ref, g_ref, m_ref, v_ref, d_ref, mo_ref, vo_ref):
        gg = g_ref[...]
        mn = ADAM_B1 * m_ref[...] + (1.0 - ADAM_B1) * gg
        vn = ADAM_B2 * v_ref[...] + (1.0 - ADAM_B2) * jnp.square(gg)
        m_hat = mn / (1.0 - ADAM_B1 ** ADAM_STEP)
        v_hat = vn / (1.0 - ADAM_B2 ** ADAM_STEP)
        d_ref[...] = -ADAM_LR * (m_hat / (jnp.sqrt(v_hat) + ADAM_EPS) + ADAM_WD * w_ref[...])
        mo_ref[...] = mn
        vo_ref[...] = vn

    spec = pl.BlockSpec((tm, last), lambda i: (i, 0))
    d, mn, vn = pl.pallas_call(
        body, name=name, grid=(rows // tm,), in_specs=[spec] * 4, out_specs=[spec] * 3,
        out_shape=[jax.ShapeDtypeStruct((rows, last), f32)] * 3,
        compiler_params=_cparams("parallel"),
    )(w2, g2, m2, v2)
    return d.reshape(shape), mn.reshape(shape), vn.reshape(shape)


def _adamw_land(name, lands, w, m, v, tm=256):
    L = len(lands)
    _, R, C = lands[0].shape
    tm = min(tm, R)

    def body(*refs):
        l_refs, (w_ref, m_ref, v_ref, g_ref, d_ref, mo_ref, vo_ref) = refs[:L], refs[L:]
        for k in range(L):
            @pl.when(pl.program_id(0) == k)
            def _(k=k):
                gg = l_refs[k][0].astype(f32)
                for s in range(1, N_DEV):
                    gg = gg + l_refs[k][s].astype(f32)
                g_ref[...] = gg
                mn = ADAM_B1 * m_ref[...] + (1.0 - ADAM_B1) * gg
                vn = ADAM_B2 * v_ref[...] + (1.0 - ADAM_B2) * jnp.square(gg)
                m_hat = mn / (1.0 - ADAM_B1 ** ADAM_STEP)
                v_hat = vn / (1.0 - ADAM_B2 ** ADAM_STEP)
                d_ref[...] = -ADAM_LR * (m_hat / (jnp.sqrt(v_hat) + ADAM_EPS) + ADAM_WD * w_ref[...])
                mo_ref[...] = mn
                vo_ref[...] = vn

    land_specs = [pl.BlockSpec((N_DEV, tm, C), lambda l, i, k=k: (0, jnp.where(l == k, i, 0), 0)) for k in range(L)]
    spec = pl.BlockSpec((None, tm, C), lambda l, i: (l, i, 0))
    return pl.pallas_call(
        body, name=name, grid=(L, R // tm),
        in_specs=land_specs + [spec] * 3,
        out_specs=[spec] * 4,
        out_shape=[jax.ShapeDtypeStruct((L, R, C), f32)] * 4,
        compiler_params=_cparams("arbitrary", "arbitrary"),
    )(*lands, w, m, v)


BIG = [("hyb_w_in", 2), ("hyb_w_out", 1), ("rec_w_in", 2), ("rec_w_out", 1), ("rec_w_a", 2), ("rec_w_x", 2),
       ("mlp_w1", 2), ("mlp_w2", 1)]
SMALL = [("hyb_conv_w", 2), ("rec_conv_w", 2), ("rec_conv_b", 1), ("rec_b_a", 1), ("rec_b_x", 1), ("rec_lambda", 1)]
REPL = ["hyb_sinks", "hyb_a_log", "hyb_dt_bias", "hyb_norm_w", "ln1_g", "ln1_b", "ln2_g", "ln2_b"]
WEIGHTS = ["hyb_w_in", "hyb_sinks", "hyb_conv_w", "hyb_a_log", "hyb_dt_bias", "hyb_norm_w", "hyb_w_out", "rec_w_in",
           "rec_conv_w", "rec_conv_b", "rec_w_a", "rec_b_a", "rec_w_x", "rec_b_x", "rec_lambda", "rec_w_out",
           "ln1_g", "ln1_b", "mlp_w1", "mlp_w2", "ln2_g", "ln2_b"]


def _pack_rows(parts, dtype, row_mult):
    lead = parts[0].shape[:-1]
    flat = jnp.concatenate([p.astype(dtype) for p in parts], axis=-1)
    n = flat.shape[-1]
    unit = row_mult * LANE
    pad = (-n) % unit
    if pad:
        flat = jnp.concatenate([flat, jnp.zeros(lead + (pad,), dtype)], axis=-1)
    return flat.reshape(lead + ((n + pad) // LANE, LANE))


def _gather_full(gathered, shard_shapes, table):
    flat = gathered.reshape(N_DEV, -1)
    out, off = {}, 0
    for name, ax in table:
        shp = shard_shapes[name]
        n = math.prod(shp)
        arr = flat[:, off:off + n].reshape((N_DEV,) + shp)
        off += n
        arr = jnp.moveaxis(arr, 0, ax)
        out[name] = arr.reshape(shp[:ax] + (N_DEV * shp[ax],) + shp[ax + 1:])
    return out


def _matmul_layouts(tag, gw):
    out = {}
    bw = D_MODEL // LRU_BLOCKS
    for k, g in gw.items():
        L = g.shape[1]
        if k == "hyb_w_in":
            out[k] = _merge_cols(f"{tag}_w_in_merge", g)
        elif k in ("hyb_w_out", "rec_w_out"):
            out[k] = jnp.swapaxes(g, 0, 1).reshape(L, D_MODEL, D_MODEL)
        elif k in ("rec_w_a", "rec_w_x"):
            out[k] = jnp.moveaxis(g, 0, 2).reshape(L, LRU_BLOCKS, bw, bw)
        else:
            out[k] = g
    return out


def kernel(x, hyb_w_in, hyb_sinks, hyb_conv_w, hyb_a_log, hyb_dt_bias, hyb_norm_w, hyb_w_out, rec_w_in, rec_conv_w, rec_conv_b, rec_w_a, rec_b_a, rec_w_x, rec_b_x, rec_lambda, rec_w_out, ln1_g, ln1_b, mlp_w1, mlp_w2, ln2_g, ln2_b, loss_target, m_hyb_w_in, m_hyb_sinks, m_hyb_conv_w, m_hyb_a_log, m_hyb_dt_bias, m_hyb_norm_w, m_hyb_w_out, m_rec_w_in, m_rec_conv_w, m_rec_conv_b, m_rec_w_a, m_rec_b_a, m_rec_w_x, m_rec_b_x, m_rec_lambda, m_rec_w_out, m_ln1_g, m_ln1_b, m_mlp_w1, m_mlp_w2, m_ln2_g, m_ln2_b, v_hyb_w_in, v_hyb_sinks, v_hyb_conv_w, v_hyb_a_log, v_hyb_dt_bias, v_hyb_norm_w, v_hyb_w_out, v_rec_w_in, v_rec_conv_w, v_rec_conv_b, v_rec_w_a, v_rec_b_a, v_rec_w_x, v_rec_b_x, v_rec_lambda, v_rec_w_out, v_ln1_g, v_ln1_b, v_mlp_w1, v_mlp_w2, v_ln2_g, v_ln2_b):
    args = locals()
    w = {k: args[k] for k in WEIGHTS}
    m = {k: args["m_" + k] for k in WEIGHTS}
    v = {k: args["v_" + k] for k in WEIGHTS}
    shard_shapes = {k: tuple(t.shape) for k, t in w.items()}
    xi, yi, ci = _my_coords()
    me = 4 * xi + 2 * yi + ci

    in_flight = {}

    def install(tag, names, got):
        for (k, i), arr in zip(names, _matmul_layouts(tag, {k: g for (k, _), g in zip(names, got)}).values()):
            W[k][i] = arr

    def start_gather(tag, names):
        srcs = [w[k][i:i + 1].astype(bf16) for k, i in names]
        *pending, zero = _push_start(f"gather_{tag}_start", "gather", srcs,
                                     [lax.empty((N_DEV,) + s.shape, bf16) for s in srcs])
        in_flight[tag] = (names, pending)
        return zero

    def finish_gather(tag, after):
        names, pending = in_flight.pop(tag)
        half = _push_wait(f"gather_{tag}_wait", "gather", *pending, after)
        install(tag, names, _pass_to_sibling(f"gather_{tag}_pass", half))

    def started(k, zero):
        W[k] = W[k] + zero

    def mixer_w(layer):
        return _layer_weights(layer)[:-2]

    def mlp_w(layer):
        return _layer_weights(layer)[-2:]

    gathered0 = _all_gather("gather_first", [w[k][i:i + 1].astype(bf16) for k, i in mixer_w(0)]
                            + [_pack_rows([w[k].reshape(-1) for k, _ in SMALL], f32, SUBLANE)])
    W = _gather_full(gathered0[-1], shard_shapes, SMALL)
    W.update({k: w[k] for k in REPL})
    W.update({k: {} for k, _ in BIG})
    install("l0a", mixer_w(0), gathered0[:-1])
    started("hyb_sinks", start_gather("l0b", mlp_w(0)) + start_gather("l1a", mixer_w(1)))

    def load_layer(layer, stage, after):
        if stage == 0:
            if layer > 0:
                finish_gather(f"l{layer}a", after)
            if 0 < layer < DEPTH - 1:
                started("hyb_sinks" if layer % 2 == 0 else "rec_conv_b",
                        start_gather(f"l{layer + 1}a", mixer_w(layer + 1)))
        if stage == 2:
            finish_gather(f"l{layer}b", after)
            if layer < DEPTH - 1:
                started("ln2_g", start_gather(f"l{layer + 1}b", mlp_w(layer + 1)))

    grads_in_flight = {}

    def grads_ready(tag, a, b):
        g = {**a, **b}
        srcs = list(g.values())
        *pending, zero = _push_start(f"scatter_{tag}_start", "scatter", srcs, [lax.empty(s.shape, bf16) for s in srcs])
        grads_in_flight[tag] = (list(g.keys()), pending)
        return zero

    loss_local, grad_x, G = _local_step(x[0], loss_target[0], W, load_layer, grads_ready)
    loss = lax.psum(loss_local, MESH_AXES)

    landed = {}

    def land(tag, after):
        keys, pending = grads_in_flight[tag]
        landed.update(zip(keys, _push_wait(f"scatter_{tag}_wait", "scatter", *pending, after)))

    tags = list(grads_in_flight)
    for tag in tags[:-1]:
        land(tag, grad_x)
    rest = _pack_rows([G[k].reshape(-1) for k, _ in SMALL] + [G[k].reshape(-1) for k in REPL], f32, SUBLANE)
    g_rest = _sum_blocks("sum_rest", _all_gather("gather_rest", [rest])[0]).reshape(-1)

    grads, delta, new_m, new_v = {}, {}, {}, {}

    def adamw_big(k):
        shp = shard_shapes[k]
        s3 = (shp[0], math.prod(shp[1:-1]), shp[-1])
        lands = [landed[(k, i)].reshape((N_DEV,) + s3[1:]) for i in range(shp[0])]
        res = _adamw_land("adamw_" + k, lands, w[k].reshape(s3), m[k].reshape(s3), v[k].reshape(s3))
        grads[k], delta[k], new_m[k], new_v[k] = (r.reshape(shp) for r in res)

    late = {k for k, _ in grads_in_flight[tags[-1]][0]}
    for k in [k for k, _ in BIG if k not in late]:
        adamw_big(k)
        done = new_v[k]
    land(tags[-1], done)
    for k in [k for k, _ in BIG if k in late]:
        adamw_big(k)
    off = 0
    for k, ax in SMALL:
        full_shape = G[k].shape
        n = math.prod(full_shape)
        full = g_rest[off:off + n].reshape(full_shape)
        off += n
        s = shard_shapes[k][ax]
        grads[k] = lax.dynamic_slice_in_dim(full, me * s, s, axis=ax)
    for k in REPL:
        n = math.prod(shard_shapes[k])
        grads[k] = g_rest[off:off + n].reshape(shard_shapes[k])
        off += n

    for k in [k for k, _ in SMALL] + REPL:
        delta[k], new_m[k], new_v[k] = _adamw("adamw_" + k, w[k], grads[k], m[k], v[k])

    return (loss, grad_x[None], *[grads[k] for k in WEIGHTS], *[delta[k] for k in WEIGHTS],
            *[new_m[k] for k in WEIGHTS], *[new_v[k] for k in WEIGHTS])
```

```python
import functools
import math

import jax
import jax.numpy as jnp
from jax import lax
from jax.experimental import pallas as pl
from jax.experimental.pallas import tpu as pltpu

f32 = jnp.float32
bf16 = jnp.bfloat16

N_DEV = 8
D_MODEL = 1024
DEPTH = 4
A_HEAD_DIM = 64
A_Q_HEADS = 8
WINDOW = 128
ROPE_THETA = 10000.0
B_HEADS = 4
B_HEAD_DIM = 128
B_CHUNK = 64
LRU_BLOCKS = 4
LRU_C = 8.0
D_FF = 4 * D_MODEL
HYB_PROJ = 2824
HYB_PROJ_PAD = 3072
DN_ALPHA = (2 * DEPTH) ** 0.25
LN_EPS = 1e-5
NORM_EPS = 1e-6
ADAM_LR = 0.001
ADAM_B1 = 0.9
ADAM_B2 = 0.999
ADAM_EPS = 1e-08
ADAM_WD = 0.01
ADAM_STEP = 10

LANE = 128
SUBLANE = 8
VMEM_LIMIT = 48 * 1024 * 1024

CB_QA, CB_KA, CB_VA, CB_CONV, CB_Z, CB_LG = 0, 4, 5, 6, 18, 22

MESH_AXES = ("x", "y", "c")


def _cparams(*sem):
    return pltpu.CompilerParams(dimension_semantics=sem, vmem_limit_bytes=VMEM_LIMIT)


def _dot(a, b, dims, precision=None):
    return lax.dot_general(a, b, (dims, ((), ())), preferred_element_type=f32, precision=precision)


NN = ((1,), (0,))
NT = ((1,), (1,))
TN = ((0,), (0,))


def _mat_spec(arr, kind, lead, br, bc, rb, cb):
    if kind == "plain":
        return pl.BlockSpec((br, bc), lambda i, j, k: (rb(i, j, k), cb(i, j, k)))
    if kind == "lead":
        return pl.BlockSpec((None, br, bc), lambda i, j, k: (lead, rb(i, j, k), cb(i, j, k)))
    if kind == "devcol":
        assert bc == arr.shape[-1]
        return pl.BlockSpec((None, None, br, bc), lambda i, j, k: (cb(i, j, k), lead, rb(i, j, k), 0))
    assert kind == "devrow" and br == arr.shape[-2]
    return pl.BlockSpec((None, None, br, bc), lambda i, j, k: (rb(i, j, k), lead, 0, cb(i, j, k)))


def _mm(name, a, b, mode, *, b_kind="plain", b_lead=0, o_kind="plain", epilogue=None, extras=(), params=(),
        out_dtypes=(f32,), tm=1024, tn=1024, tk=None):
    if tk is None:
        tk = 512 if mode == "tn" else 1024
    if b_kind in ("plain", "lead"):
        b_rows, b_cols = b.shape[-2:]
    elif b_kind == "devcol":
        b_rows, b_cols = b.shape[-2], N_DEV * b.shape[-1]
    else:
        b_rows, b_cols = N_DEV * b.shape[-2], b.shape[-1]
    if mode == "nn":
        (M, K), (K2, N) = a.shape, (b_rows, b_cols)
    elif mode == "nt":
        (M, K), (N, K2) = a.shape, (b_rows, b_cols)
    else:
        (K, M), (K2, N) = a.shape, (b_rows, b_cols)
    assert K == K2, (name, a.shape, b.shape, mode)
    tm, tn, tk = min(tm, M), min(tn, N), min(tk, K)
    cols_are_n = mode != "nt"
    if b_kind == "devcol":
        tn, tk = (b.shape[-1], tk) if cols_are_n else (tn, b.shape[-1])
    if b_kind == "devrow":
        tn, tk = (tn, b.shape[-2]) if cols_are_n else (b.shape[-2], tk)
    shard = N // N_DEV
    if o_kind == "devcol":
        tn = max(shard, tn // shard * shard)
    assert M % tm == 0 and N % tn == 0 and K % tk == 0, (name, M, N, K, tm, tn, tk)
    nk = K // tk
    dims = {"nn": NN, "nt": NT, "tn": TN}[mode]
    n_ex, n_out = len(extras) + len(params), len(out_dtypes)

    def body(*refs):
        a_ref, b_ref = refs[:2]
        ex = refs[2:2 + n_ex]
        outs = refs[2 + n_ex:2 + n_ex + n_out]
        acc = refs[-1]
        k = pl.program_id(2)

        @pl.when(k == 0)
        def _():
            acc[...] = jnp.zeros_like(acc)

        acc[...] += _dot(a_ref[...].astype(bf16), b_ref[...].astype(bf16), dims)

        @pl.when(k == nk - 1)
        def _():
            r = acc[...]
            res = epilogue(r, *[e[...] for e in ex]) if epilogue is not None else (r,)
            for o, v in zip(outs, res):
                if o_kind == "plain":
                    o[...] = v.astype(o.dtype)
                else:
                    for q in range(tn // shard):
                        o[q] = v[:, q * shard:(q + 1) * shard].astype(o.dtype)

    if mode == "tn":
        a_spec = pl.BlockSpec((tk, tm), lambda i, j, k: (k, i))
    else:
        a_spec = pl.BlockSpec((tm, tk), lambda i, j, k: (i, k))
    jb, kb = (lambda i, j, k: j), (lambda i, j, k: k)
    if mode == "nt":
        b_spec = _mat_spec(b, b_kind, b_lead, tn, tk, jb, kb)
    else:
        b_spec = _mat_spec(b, b_kind, b_lead, tk, tn, kb, jb)
    e_spec = pl.BlockSpec((tm, tn), lambda i, j, k: (i, j))
    if o_kind == "plain":
        o_spec, o_shape = e_spec, (M, N)
    else:
        o_spec, o_shape = pl.BlockSpec((tn // shard, tm, shard), lambda i, j, k: (j, i, 0)), (N_DEV, M, shard)
    res = pl.pallas_call(
        body, name=name,
        grid=(M // tm, N // tn, nk),
        in_specs=[a_spec, b_spec] + [e_spec] * len(extras)
        + [pl.BlockSpec(p.shape, lambda i, j, k: (0, 0)) for p in params],
        out_specs=[o_spec] * n_out,
        out_shape=[jax.ShapeDtypeStruct(o_shape, dt) for dt in out_dtypes],
        scratch_shapes=[pltpu.VMEM((tm, tn), f32)],
        compiler_params=_cparams("parallel", "parallel", "arbitrary"),
    )(a, b, *extras, *params)
    return res[0] if n_out == 1 else res


def _row_spec(tm, cb, width):
    assert (cb * LANE) % width == 0
    blk = (cb * LANE) // width
    return pl.BlockSpec((tm, width), lambda i: (i, blk))


def _whole_spec(p):
    nd = p.ndim
    return pl.BlockSpec(p.shape, lambda i: (0,) * nd)


def _tl_fwd(name, fn, rows, params, out_widths, out_dtypes, tm=256):
    T = rows[0][0].shape[0]
    tm = min(tm, T)
    nr, npar = len(rows), len(params)

    def body(*refs):
        vals = [r[...] for r in refs[:nr + npar]]
        outs = fn(*vals)
        for o, v in zip(refs[nr + npar:], outs):
            o[...] = v.astype(o.dtype)

    res = pl.pallas_call(
        body, name=name, grid=(T // tm,),
        in_specs=[_row_spec(tm, cb, w) for (_, cb, w) in rows] + [_whole_spec(p) for p in params],
        out_specs=[pl.BlockSpec((tm, w), lambda i: (i, 0)) for w in out_widths],
        out_shape=[jax.ShapeDtypeStruct((T, w), dt) for w, dt in zip(out_widths, out_dtypes)],
        compiler_params=_cparams("parallel"),
    )(*[r[0] for r in rows], *params)
    return res


def _tl_bwd(name, fn, rows, params, cot_rows, cot_fn=None, wide=None, tm=256):
    T = rows[0][0].shape[0]
    tm = min(tm, T)
    nr, npar, nc = len(rows), len(params), len(cot_rows)
    place = [(wide or {}).get(k, (w, 0)) for k, (_, _, w) in enumerate(rows)]

    def body(*refs):
        vals = [r[...] for r in refs[:nr + npar]]
        cots = [r[...] for r in refs[nr + npar:nr + npar + nc]]
        outs = refs[nr + npar + nc:]
        cot = tuple(cot_fn(*cots)) if cot_fn is not None else tuple(cots)
        _, vjp = jax.vjp(fn, *vals)
        grads = vjp(cot)
        for o, g in zip(outs[:nr], grads[:nr]):
            o[...] = g.astype(o.dtype)
        i = pl.program_id(0)
        for o, g in zip(outs[nr:], grads[nr:]):
            @pl.when(i == 0)
            def _(o=o):
                o[...] = jnp.zeros_like(o)
            o[...] += g

    res = pl.pallas_call(
        body, name=name, grid=(T // tm,),
        in_specs=[_row_spec(tm, cb, w) for (_, cb, w) in rows] + [_whole_spec(p) for p in params]
        + [_row_spec(tm, cb, w) for (_, cb, w) in cot_rows],
        out_specs=[pl.BlockSpec((tm, w), lambda i, blk=blk: (i, blk)) for (_, _, w), (_, blk) in zip(rows, place)]
        + [_whole_spec(p) for p in params],
        out_shape=[jax.ShapeDtypeStruct((T, total), f32) for total, _ in place]
        + [jax.ShapeDtypeStruct(p.shape, f32) for p in params],
        compiler_params=_cparams("arbitrary"),
    )(*[r[0] for r in rows], *params, *[r[0] for r in cot_rows])
    return res[:nr], res[nr:]


def _ln_res_fn(x, mix, g, b):
    pre = DN_ALPHA * x + mix
    mu = jnp.mean(pre, axis=-1, keepdims=True)
    var = jnp.mean(jnp.square(pre - mu), axis=-1, keepdims=True)
    return ((pre - mu) * lax.rsqrt(var + LN_EPS) * g + b,)


@jax.custom_jvp
def _expm1(x):
    small = jnp.abs(x) < 0.3
    xs = jnp.where(small, x, 0.0)
    poly = xs * (1.0 + xs * (1 / 2 + xs * (1 / 6 + xs * (1 / 24 + xs * (1 / 120 + xs * (
        1 / 720 + xs * (1 / 5040 + xs * (1 / 40320 + xs * (1 / 362880)))))))))
    return jnp.where(small, poly, jnp.exp(x) - 1.0)


@_expm1.defjvp
def _expm1_jvp(primals, tangents):
    (x,), (t,) = primals, tangents
    return _expm1(x), t * jnp.exp(x)


def _rglru_pre_fn(pre_r, pre_i, xc, b_a, b_x, lam):
    r = jax.nn.sigmoid(pre_r + b_a)
    i = jax.nn.sigmoid(pre_i + b_x)
    log_a = -LRU_C * r * jax.nn.softplus(-lam)
    a = jnp.exp(log_a)
    b = jnp.sqrt(-_expm1(2.0 * log_a)) * (i * xc)
    return a, b


def _rec_gate_fn(h, gate):
    return (h * jax.nn.gelu(gate),)


def _loss_head(y, t, tm=256):
    T, Dm = y.shape

    def body(y_ref, t_ref, dy_ref, loss_ref):
        e = y_ref[...] - t_ref[...]
        dy_ref[...] = e * (1.0 / Dm)

        @pl.when(pl.program_id(0) == 0)
        def _():
            loss_ref[...] = jnp.zeros_like(loss_ref)

        loss_ref[...] += 0.5 * jnp.sum(jnp.mean(e * e, axis=-1, keepdims=True), axis=0, keepdims=True)

    dy, loss = pl.pallas_call(
        body, name="loss_head", grid=(T // tm,),
        in_specs=[pl.BlockSpec((tm, Dm), lambda i: (i, 0))] * 2,
        out_specs=[pl.BlockSpec((tm, Dm), lambda i: (i, 0)), pl.BlockSpec((SUBLANE, LANE), lambda i: (0, 0))],
        out_shape=[jax.ShapeDtypeStruct((T, Dm), f32), jax.ShapeDtypeStruct((SUBLANE, LANE), f32)],
        compiler_params=_cparams("arbitrary"),
    )(y, t)
    return loss[0, 0], dy


def _conv_fwd(name, x, cb0, nblk, w, bias, tm=2048):
    T = x.shape[0]
    tm = min(tm, T)
    hb = tm // SUBLANE
    has_b = bias is not None

    def body(*refs):
        cur, prev, w_ref = refs[:3]
        b_ref = refs[3] if has_b else None
        o = refs[-1]
        i = pl.program_id(1)
        p = jnp.where(i > 0, prev[...], 0.0)
        xcat = jnp.concatenate([p, cur[...]], axis=0)
        acc = cur[...] * w_ref[3:4, :]
        for j in range(3):
            acc = acc + pltpu.roll(xcat, 3 - j, axis=0)[SUBLANE:] * w_ref[j:j + 1, :]
        if has_b:
            acc = acc + b_ref[...]
        o[...] = acc

    in_specs = [
        pl.BlockSpec((tm, LANE), lambda c, i: (i, cb0 + c)),
        pl.BlockSpec((SUBLANE, LANE), lambda c, i: (jnp.maximum(i * hb - 1, 0), cb0 + c)),
        pl.BlockSpec((4, LANE), lambda c, i: (0, c)),
    ]
    args = [x, x, w]
    if has_b:
        in_specs.append(pl.BlockSpec((1, LANE), lambda c, i: (0, c)))
        args.append(bias)
    return pl.pallas_call(
        body, name=name, grid=(nblk, T // tm),
        in_specs=in_specs,
        out_specs=pl.BlockSpec((tm, LANE), lambda c, i: (i, c)),
        out_shape=jax.ShapeDtypeStruct((T, nblk * LANE), f32),
        compiler_params=_cparams("parallel", "parallel"),
    )(*args)


def _conv_bwd(name, dy, x, cb0, nblk, w, into, into_cb, tm=2048):
    T = x.shape[0]
    tm = min(tm, T)
    hb = tm // SUBLANE
    nt = T // tm

    def body(dcur, dnext, xcur, xprev, w_ref, _, dx_ref, dw_ref, db_ref):
        i = pl.program_id(1)
        d = dcur[...]
        dn = jnp.where(i < nt - 1, dnext[...], 0.0)
        dcat = jnp.concatenate([d, dn], axis=0)
        acc = d * w_ref[3:4, :]
        for j in range(3):
            s = 3 - j
            acc = acc + pltpu.roll(dcat, tm + SUBLANE - s, axis=0)[:tm] * w_ref[j:j + 1, :]
        dx_ref[...] = acc

        p = jnp.where(i > 0, xprev[...], 0.0)
        xcat = jnp.concatenate([p, xcur[...]], axis=0)
        rows = [jnp.sum(d * pltpu.roll(xcat, 3 - j, axis=0)[SUBLANE:], axis=0, keepdims=True) for j in range(3)]
        rows.append(jnp.sum(d * xcur[...], axis=0, keepdims=True))
        rows.append(jnp.zeros((SUBLANE - 4, LANE), f32))

        @pl.when(i == 0)
        def _():
            dw_ref[...] = jnp.zeros_like(dw_ref)
            db_ref[...] = jnp.zeros_like(db_ref)

        dw_ref[...] += jnp.concatenate(rows, axis=0)
        db_ref[...] += jnp.broadcast_to(jnp.sum(d, axis=0, keepdims=True), (SUBLANE, LANE))

    nh = T // SUBLANE
    dx, dw, db = pl.pallas_call(
        body, name=name, grid=(nblk, nt),
        in_specs=[
            pl.BlockSpec((tm, LANE), lambda c, i: (i, c)),
            pl.BlockSpec((SUBLANE, LANE), lambda c, i: (jnp.minimum((i + 1) * hb, nh - 1), c)),
            pl.BlockSpec((tm, LANE), lambda c, i: (i, cb0 + c)),
            pl.BlockSpec((SUBLANE, LANE), lambda c, i: (jnp.maximum(i * hb - 1, 0), cb0 + c)),
            pl.BlockSpec((4, LANE), lambda c, i: (0, c)),
            pl.BlockSpec(memory_space=pl.ANY),
        ],
        out_specs=[
            pl.BlockSpec((tm, LANE), lambda c, i: (i, into_cb + c)),
            pl.BlockSpec((SUBLANE, LANE), lambda c, i: (0, c)),
            pl.BlockSpec((SUBLANE, LANE), lambda c, i: (0, c)),
        ],
        out_shape=[jax.ShapeDtypeStruct(into.shape, f32),
                   jax.ShapeDtypeStruct((SUBLANE, nblk * LANE), f32),
                   jax.ShapeDtypeStruct((SUBLANE, nblk * LANE), f32)],
        input_output_aliases={5: 0},
        compiler_params=_cparams("parallel", "arbitrary"),
    )(dy, dy, x, x, w, into)
    return dx, dw[:4], db[0]


@functools.partial(jax.custom_vjp, nondiff_argnums=(1,))
def _lroll(x, s):
    return pltpu.roll(x, s, axis=1)


def _lroll_fwd(x, s):
    return _lroll(x, s), None


def _lroll_bwd(s, _, g):
    return (_lroll(g, (LANE - s) % LANE),)


_lroll.defvjp(_lroll_fwd, _lroll_bwd)


def _rope_tables(T):
    half = A_HEAD_DIM // 2
    inv_freq = ROPE_THETA ** (-jnp.arange(half, dtype=f32) / half)
    ang = jnp.arange(T, dtype=f32)[:, None] * inv_freq[None, :]
    cos, sin = jnp.cos(ang), jnp.sin(ang)
    return jnp.tile(jnp.concatenate([cos, cos], axis=1), (1, 2)), jnp.tile(jnp.concatenate([-sin, sin], axis=1), (1, 2))


def _attn_block_fn(n, q, kp, kc, vp, vc, cq, sq, cp, sp, sinks):
    W = WINDOW
    lane = lax.broadcasted_iota(jnp.int32, (W, LANE), 1)
    lo_half = (lane % A_HEAD_DIM) < (A_HEAD_DIM // 2)
    lane8 = lax.broadcasted_iota(jnp.int32, sinks.shape, 1)

    def rope(x, c, s):
        return x * c + jnp.where(lo_half, _lroll(x, LANE - A_HEAD_DIM // 2), _lroll(x, A_HEAD_DIM // 2)) * s

    k2 = jnp.concatenate([rope(kp, cp, sp), rope(kc, cq, sq)], axis=0).astype(bf16)
    v2 = jnp.concatenate([vp, vc], axis=0).astype(bf16)
    qs = []
    for t in range(4):
        qt = rope(q[:, LANE * t:LANE * (t + 1)], cq, sq)
        g = t // 2
        for hh in range(2):
            qa = jnp.where((lane // A_HEAD_DIM) == hh, qt, 0.0)
            qs.append(_lroll(qa, A_HEAD_DIM) if hh != g else qa)
    s_all = _dot(jnp.concatenate(qs, axis=0).astype(bf16), k2, NT) * (A_HEAD_DIM ** -0.5)
    row = lax.broadcasted_iota(jnp.int32, (W, 2 * W), 0)
    col = lax.broadcasted_iota(jnp.int32, (W, 2 * W), 1)
    dist = row + W - col
    mask = (dist >= 0) & (dist < W) & ((col >= W) | (n > 0))
    ps = []
    for j in range(A_Q_HEADS):
        s = jnp.where(mask, s_all[W * j:W * (j + 1)], -jnp.inf)
        sink = jnp.sum(jnp.where(lane8 == j, sinks, 0.0), axis=1, keepdims=True)
        m = jnp.maximum(jnp.max(s, axis=-1, keepdims=True), sink)
        e = jnp.exp(s - m)
        ps.append((e / (jnp.sum(e, axis=-1, keepdims=True) + jnp.exp(sink - m))).astype(bf16))
    o = _dot(jnp.concatenate(ps, axis=0), v2, NN)
    outs = []
    for t in range(4):
        g = t // 2
        ot = jnp.zeros((W, LANE), f32)
        for hh in range(2):
            j = 2 * t + hh
            oj = jnp.where((lane // A_HEAD_DIM) == g, o[W * j:W * (j + 1)], 0.0)
            ot = ot + (_lroll(oj, A_HEAD_DIM) if hh != g else oj)
        outs.append(ot)
    return jnp.concatenate(outs, axis=1)


def _attn_specs():
    W = WINDOW
    prev = lambda n: jnp.maximum(n - 1, 0)
    return [
        pl.BlockSpec((W, 4 * LANE), lambda n: (n, CB_QA // 4)),
        pl.BlockSpec((W, LANE), lambda n: (prev(n), CB_KA)),
        pl.BlockSpec((W, LANE), lambda n: (n, CB_KA)),
        pl.BlockSpec((W, LANE), lambda n: (prev(n), CB_VA)),
        pl.BlockSpec((W, LANE), lambda n: (n, CB_VA)),
        pl.BlockSpec((W, LANE), lambda n: (n, 0)),
        pl.BlockSpec((W, LANE), lambda n: (n, 0)),
        pl.BlockSpec((W, LANE), lambda n: (prev(n), 0)),
        pl.BlockSpec((W, LANE), lambda n: (prev(n), 0)),
        pl.BlockSpec((1, A_Q_HEADS), lambda n: (0, 0)),
    ]


def _attn_fwd(name, proj, cos, sin, sinks):
    T = proj.shape[0]
    W = WINDOW

    def body(*refs):
        o = refs[-1]
        o[...] = _attn_block_fn(pl.program_id(0), *[r[...] for r in refs[:-1]])

    return pl.pallas_call(
        body, name=name, grid=(T // W,),
        in_specs=_attn_specs(),
        out_specs=pl.BlockSpec((W, 4 * LANE), lambda n: (n, 0)),
        out_shape=jax.ShapeDtypeStruct((T, 2 * 4 * LANE), f32),
        compiler_params=_cparams("parallel"),
    )(proj, proj, proj, proj, proj, cos, sin, cos, sin, sinks)


def _attn_bwd(name, proj, cos, sin, sinks, d_oab):
    T = proj.shape[0]
    W = WINDOW
    Q = 4 * LANE

    def body(*refs):
        ins = [r[...] for r in refs[:10]]
        do = refs[10][...]
        d_ref, ds_ref = refs[11:]
        n = pl.program_id(0)
        _, vjp = jax.vjp(functools.partial(_attn_block_fn, n), *ins)
        dq, dkp, dkc, dvp, dvc, _, _, _, _, dsk = vjp(do)

        @pl.when(n == 0)
        def _():
            d_ref[:, Q:] = jnp.zeros((T, 2 * LANE), f32)
            ds_ref[...] = jnp.zeros_like(ds_ref)

        cur = pl.ds(pl.multiple_of(n * W, W), W)
        d_ref[cur, :Q] = dq
        d_ref[cur, Q:Q + LANE] += dkc
        d_ref[cur, Q + LANE:] += dvc
        ds_ref[...] += dsk

        @pl.when(n > 0)
        def _():
            prv = pl.ds(pl.multiple_of((n - 1) * W, W), W)
            d_ref[prv, Q:Q + LANE] += dkp
            d_ref[prv, Q + LANE:] += dvp

    return pl.pallas_call(
        body, name=name, grid=(T // W,),
        in_specs=_attn_specs() + [pl.BlockSpec((W, Q), lambda n: (n, 0))],
        out_specs=[pl.BlockSpec((T, Q + 2 * LANE), lambda n: (0, 0)),
                   pl.BlockSpec((1, A_Q_HEADS), lambda n: (0, 0))],
        out_shape=[jax.ShapeDtypeStruct((T, HYB_PROJ_PAD), f32), jax.ShapeDtypeStruct((1, A_Q_HEADS), f32)],
        compiler_params=_cparams("arbitrary"),
    )(proj, proj, proj, proj, proj, cos, sin, cos, sin, sinks, d_oab)


def _bdot(spec, a, b, precision=None):
    return jnp.einsum(spec, a, b, preferred_element_type=f32, precision=precision)


@jax.custom_vjp
def _tri_inv(a):
    H, C, _ = a.shape
    B = 2 * SUBLANE
    nb = C // B
    r = lax.broadcasted_iota(jnp.int32, (C, C), 0)
    c = lax.broadcasted_iota(jnp.int32, (C, C), 1)
    a4 = jnp.where((r // B) == (c // B), a, 0.0).reshape(H, nb, B, C)
    t4 = jnp.broadcast_to(jnp.where(r == c, 1.0, 0.0).astype(f32), a.shape).reshape(H, nb, B, C)
    for j in range(B - 1):
        col = jnp.concatenate([a4[:, b:b + 1, :, B * b + j:B * b + j + 1] for b in range(nb)], axis=1)
        t4 = t4 - col * t4[:, :, j:j + 1, :]
    x = t4.reshape(H, C, C)
    hi = lax.Precision.HIGHEST
    while B < C:
        m = jnp.where(((r // (2 * B)) == (c // (2 * B))) & ((r // B) > (c // B)), a, 0.0)
        x = x - _bdot("hij,hjk->hik", x, _bdot("hij,hjk->hik", m, x, precision=hi), precision=hi)
        B *= 2
    return x


def _tri_inv_fwd(a):
    t = _tri_inv(a)
    return t, t


def _tri_inv_bwd(t, g):
    C = t.shape[-1]
    r = lax.broadcasted_iota(jnp.int32, (C, C), 0)
    c = lax.broadcasted_iota(jnp.int32, (C, C), 1)
    x = _bdot("hki,hkj->hij", t, g, precision=lax.Precision.HIGHEST)
    y = _bdot("hik,hjk->hij", x, t, precision=lax.Precision.HIGHEST)
    return (jnp.where(r > c, -y, 0.0),)


_tri_inv.defvjp(_tri_inv_fwd, _tri_inv_bwd)


@jax.custom_vjp
def _tri_inv_saved(a, t):
    return t


_tri_inv_saved.defvjp(lambda a, t: (t, t), lambda t, g: (_tri_inv_bwd(t, g)[0], jnp.zeros_like(t)))


def _silu(x):
    return x * jax.nn.sigmoid(x)


def _l2n(x):
    return x * lax.rsqrt(jnp.sum(x * x, axis=-1, keepdims=True) + NORM_EPS)


def _delta_chunk_fn(cq, ck, cv, z, lg, a_log, dt_bias, norm_w, S, t_saved=None, want_t=False):
    C = B_CHUNK
    lane = lax.broadcasted_iota(jnp.int32, (C, LANE), 1)
    pick = lambda l0: jnp.concatenate(
        [jnp.sum(jnp.where(lane == l0 + h, lg, 0.0), axis=1, keepdims=True)[None] for h in range(B_HEADS)], axis=0)
    bl, al = pick(0), pick(B_HEADS)
    q = _l2n(_silu(cq)) * (B_HEAD_DIM ** -0.5)
    k = _l2n(_silu(ck))
    v = _silu(cv)
    beta = jax.nn.sigmoid(bl)
    g = -jnp.exp(a_log) * jax.nn.softplus(al + dt_bias)
    r = lax.broadcasted_iota(jnp.int32, (C, C), 0)
    c = lax.broadcasted_iota(jnp.int32, (C, C), 1)
    eye = r == c
    g_row = jnp.sum(jnp.where(eye, g, 0.0), axis=1, keepdims=True)
    gc = jnp.sum(jnp.where(c <= r, g_row, 0.0), axis=2, keepdims=True)
    gc_row = jnp.sum(jnp.where(eye, gc, 0.0), axis=1, keepdims=True)
    decay_incl = jnp.exp(jnp.where(r >= c, gc - gc_row, -jnp.inf))
    decay_strict = jnp.where(r > c, decay_incl, 0.0)
    kb = k * beta
    vb = v * beta
    kbf = k.astype(bf16)
    a_mat = _bdot("hik,hjk->hij", kb.astype(bf16), kbf) * decay_strict
    t_f32 = _tri_inv(a_mat) if t_saved is None else _tri_inv_saved(a_mat, t_saved)
    t_mat = t_f32.astype(bf16)
    eg = jnp.exp(gc)
    u = _bdot("hij,hjv->hiv", t_mat, vb.astype(bf16))
    w = _bdot("hij,hjk->hik", t_mat, (kb * eg).astype(bf16))
    qk = _bdot("hik,hjk->hij", q.astype(bf16), kbf) * decay_incl
    g_last = jnp.sum(g, axis=1, keepdims=True)
    k_tail = k * jnp.exp(g_last - gc)
    Sb = S.astype(bf16)
    v_new = u - _bdot("hck,hkv->hcv", w.astype(bf16), Sb)
    o = _bdot("hck,hkv->hcv", (q * eg).astype(bf16), Sb) + _bdot("hij,hjv->hiv", qk.astype(bf16), v_new.astype(bf16))
    S_new = S * jnp.exp(g_last) + _bdot("hck,hcv->hkv", k_tail.astype(bf16), v_new.astype(bf16))
    ob = o * lax.rsqrt(jnp.mean(o * o, axis=-1, keepdims=True) + NORM_EPS) * norm_w
    return (ob * _silu(z), S_new) + ((t_f32,) if want_t else ())


def _delta_in_specs(rev, N):
    C = B_CHUNK
    ix = (lambda n: N - 1 - n) if rev else (lambda n: n)
    specs = [pl.BlockSpec((C, 3 * B_HEADS * LANE), lambda n: (ix(n), 0))]
    specs += [pl.BlockSpec((C, LANE), lambda n, h=h: (ix(n), CB_Z + h)) for h in range(B_HEADS)]
    specs += [
        pl.BlockSpec((C, LANE), lambda n: (ix(n), CB_LG)),
        pl.BlockSpec((B_HEADS, 1, 1), lambda n: (0, 0, 0)),
        pl.BlockSpec((B_HEADS, 1, 1), lambda n: (0, 0, 0)),
        pl.BlockSpec((1, LANE), lambda n: (0, 0)),
    ]
    return specs


def _delta_inputs(c_ref, z_refs, lg, al, dt, nw):
    H = B_HEADS
    part = lambda p: jnp.stack([c_ref[:, LANE * (p * H + h):LANE * (p * H + h + 1)] for h in range(H)])
    return (part(0), part(1), part(2), jnp.stack([z[...] for z in z_refs]), lg[...], al[...], dt[...], nw[...])


def _delta_fwd(name, c, proj, a_log, dt_bias, norm_w, o_ab):
    T = c.shape[0]
    C = B_CHUNK
    N = T // C
    Dh = B_HEAD_DIM
    H = B_HEADS

    def body(*refs):
        c_ref, z_refs, (lg, al, dt, nw) = refs[0], refs[1:1 + H], refs[1 + H:5 + H]
        o_ref, s_ref, t_ref, S = refs[6 + H:]

        @pl.when(pl.program_id(0) == 0)
        def _():
            S[...] = jnp.zeros_like(S)

        s0 = S[...]
        s_ref[...] = s0
        ob, s1, t = _delta_chunk_fn(*_delta_inputs(c_ref, z_refs, lg, al, dt, nw), s0, want_t=True)
        for h in range(H):
            o_ref[:, LANE * h:LANE * (h + 1)] = ob[h]
        t_ref[...] = t
        S[...] = s1

    return pl.pallas_call(
        body, name=name, grid=(N,),
        in_specs=_delta_in_specs(False, N) + [pl.BlockSpec(memory_space=pl.ANY)],
        out_specs=[pl.BlockSpec((C, H * LANE), lambda n: (n, 1)),
                   pl.BlockSpec((H, None, Dh, Dh), lambda n: (0, n, 0, 0)),
                   pl.BlockSpec((H, None, C, C), lambda n: (0, n, 0, 0))],
        out_shape=[jax.ShapeDtypeStruct(o_ab.shape, f32), jax.ShapeDtypeStruct((H, N, Dh, Dh), f32),
                   jax.ShapeDtypeStruct((H, N, C, C), f32)],
        input_output_aliases={5 + H: 0},
        scratch_shapes=[pltpu.VMEM((H, Dh, Dh), f32)],
        compiler_params=_cparams("arbitrary"),
    )(c, *([proj] * H), proj, a_log, dt_bias, norm_w, o_ab)


def _delta_bwd(name, c, proj, a_log, dt_bias, norm_w, s_saved, t_saved, d_oab, dproj):
    T = c.shape[0]
    C = B_CHUNK
    N = T // C
    Dh = B_HEAD_DIM
    H = B_HEADS

    def body(*refs):
        c_ref, z_refs, (lg, al, dt, nw) = refs[0], refs[1:1 + H], refs[1 + H:5 + H]
        s_ref, t_ref, do_ref = refs[5 + H:8 + H]
        dc, dtail, dal, ddt, dnw, dS = refs[9 + H:]

        @pl.when(pl.program_id(0) == 0)
        def _():
            dS[...] = jnp.zeros_like(dS)
            dal[...] = jnp.zeros_like(dal)
            ddt[...] = jnp.zeros_like(ddt)
            dnw[...] = jnp.zeros_like(dnw)

        _, vjp = jax.vjp(functools.partial(_delta_chunk_fn, t_saved=t_ref[...]),
                         *_delta_inputs(c_ref, z_refs, lg, al, dt, nw), s_ref[...])
        do = jnp.stack([do_ref[:, LANE * h:LANE * (h + 1)] for h in range(H)])
        g = vjp((do, dS[...]))
        for h in range(H):
            for p in range(3):
                dc[:, LANE * (p * H + h):LANE * (p * H + h + 1)] = g[p][h]
            dtail[:, LANE * h:LANE * (h + 1)] = g[3][h]
        dtail[:, LANE * H:LANE * (H + 1)] = g[4]
        dtail[:, LANE * (H + 1):] = jnp.zeros((C, LANE), f32)
        dal[...] += g[5]
        ddt[...] += g[6]
        dnw[...] += g[7]
        dS[...] = g[8]

    rn = lambda n: N - 1 - n
    return pl.pallas_call(
        body, name=name, grid=(N,),
        in_specs=_delta_in_specs(True, N) + [
            pl.BlockSpec((H, None, Dh, Dh), lambda n: (0, rn(n), 0, 0)),
            pl.BlockSpec((H, None, C, C), lambda n: (0, rn(n), 0, 0)),
            pl.BlockSpec((C, H * LANE), lambda n: (rn(n), 1)),
            pl.BlockSpec(memory_space=pl.ANY),
        ],
        out_specs=[
            pl.BlockSpec((C, 3 * H * LANE), lambda n: (rn(n), 0)),
            pl.BlockSpec((C, (H + 2) * LANE), lambda n: (rn(n), CB_Z // (H + 2))),
            pl.BlockSpec((H, 1, 1), lambda n: (0, 0, 0)),
            pl.BlockSpec((H, 1, 1), lambda n: (0, 0, 0)),
            pl.BlockSpec((1, LANE), lambda n: (0, 0)),
        ],
        out_shape=[jax.ShapeDtypeStruct((T, 3 * H * Dh), f32), jax.ShapeDtypeStruct(dproj.shape, f32),
                   jax.ShapeDtypeStruct((H, 1, 1), f32), jax.ShapeDtypeStruct((H, 1, 1), f32),
                   jax.ShapeDtypeStruct((1, LANE), f32)],
        input_output_aliases={8 + H: 1},
        scratch_shapes=[pltpu.VMEM((H, Dh, Dh), f32)],
        compiler_params=_cparams("arbitrary"),
    )(c, *([proj] * H), proj, a_log, dt_bias, norm_w, s_saved, t_saved, d_oab, dproj)


def _blockdiag_fwd(name, xc, w_a, w_x, tm=512):
    T, Wd = xc.shape
    bw = Wd // LRU_BLOCKS
    tm = min(tm, T)

    def body(x_ref, wa_ref, wx_ref, oa, ox):
        xb = x_ref[...].astype(bf16)
        oa[...] = _dot(xb, wa_ref[...].astype(bf16), NN)
        ox[...] = _dot(xb, wx_ref[...].astype(bf16), NN)

    xs = pl.BlockSpec((tm, bw), lambda i, h: (i, h))
    ws = pl.BlockSpec((None, bw, bw), lambda i, h: (h, 0, 0))
    return pl.pallas_call(
        body, name=name, grid=(T // tm, LRU_BLOCKS), in_specs=[xs, ws, ws], out_specs=[xs, xs],
        out_shape=[jax.ShapeDtypeStruct((T, Wd), f32)] * 2,
        compiler_params=_cparams("parallel", "parallel"),
    )(xc, w_a, w_x)


def _blockdiag_bwd_dx(name, dpr, dpi, w_a, w_x, addend, tm=512):
    T, Wd = dpr.shape
    bw = Wd // LRU_BLOCKS
    tm = min(tm, T)

    def body(dr, di, wa_ref, wx_ref, add, o):
        o[...] = (add[...] + _dot(dr[...].astype(bf16), wa_ref[...].astype(bf16), NT)
                  + _dot(di[...].astype(bf16), wx_ref[...].astype(bf16), NT))

    xs = pl.BlockSpec((tm, bw), lambda i, h: (i, h))
    ws = pl.BlockSpec((None, bw, bw), lambda i, h: (h, 0, 0))
    return pl.pallas_call(
        body, name=name, grid=(T // tm, LRU_BLOCKS), in_specs=[xs, xs, ws, ws, xs], out_specs=xs,
        out_shape=jax.ShapeDtypeStruct((T, Wd), f32),
        compiler_params=_cparams("parallel", "parallel"),
    )(dpr, dpi, w_a, w_x, addend)


def _blockdiag_bwd_dw(name, xc, dpr, dpi, tk=512):
    T, Wd = xc.shape
    bw = Wd // LRU_BLOCKS
    tk = min(tk, T)

    def body(x_ref, dr, di, oa, ox):
        @pl.when(pl.program_id(1) == 0)
        def _():
            oa[...] = jnp.zeros_like(oa)
            ox[...] = jnp.zeros_like(ox)

        xb = x_ref[...].astype(bf16)
        oa[...] += _dot(xb, dr[...].astype(bf16), TN)
        ox[...] += _dot(xb, di[...].astype(bf16), TN)

    xs = pl.BlockSpec((tk, bw), lambda h, k: (k, h))
    ws = pl.BlockSpec((None, bw, bw), lambda h, k: (h, 0, 0))
    return pl.pallas_call(
        body, name=name, grid=(LRU_BLOCKS, T // tk), in_specs=[xs, xs, xs], out_specs=[ws, ws],
        out_shape=[jax.ShapeDtypeStruct((LRU_BLOCKS, bw, bw), f32)] * 2,
        compiler_params=_cparams("parallel", "arbitrary"),
    )(xc, dpr, dpi)


def _scan(name, a, b, reverse, tt=512, cb=512):
    T, Wd = a.shape
    tt, cb = min(tt, T), min(cb, Wd)
    nt = T // tt
    ng = tt // SUBLANE

    def body(a_ref, b_ref, *rest):
        outs, (carry, carry_a) = rest[:-2], rest[-2:]

        @pl.when(pl.program_id(1) == 0)
        def _():
            carry[...] = jnp.zeros_like(carry)
            carry_a[...] = jnp.zeros_like(carry_a)

        row = lax.broadcasted_iota(jnp.int32, (SUBLANE, cb), 0)

        def step(gi, c):
            hp, ap = c
            g = (ng - 1 - gi) if reverse else gi
            off = pl.multiple_of(g * SUBLANE, SUBLANE)
            A = a_ref[pl.ds(off, SUBLANE), :]
            B = b_ref[pl.ds(off, SUBLANE), :]
            a_first = jnp.broadcast_to(A[0:1, :], (SUBLANE, cb))
            if reverse:
                A = jnp.where(row == SUBLANE - 1, ap, pltpu.roll(A, SUBLANE - 1, axis=0))
            for s in (1, 2, 4):
                sh = (SUBLANE - s) if reverse else s
                As = pltpu.roll(A, sh, axis=0)
                Bs = pltpu.roll(B, sh, axis=0)
                valid = (row < SUBLANE - s) if reverse else (row >= s)
                B = jnp.where(valid, A * Bs + B, B)
                A = jnp.where(valid, A * As, A)
            hcur = A * hp + B
            outs[0][pl.ds(off, SUBLANE), :] = hcur
            if not reverse:
                outs[1][pl.ds(off, SUBLANE), :] = jnp.where(row == 0, hp, pltpu.roll(hcur, 1, axis=0))
            edge = hcur[0:1, :] if reverse else hcur[SUBLANE - 1:SUBLANE, :]
            return jnp.broadcast_to(edge, (SUBLANE, cb)), a_first

        carry[...], carry_a[...] = lax.fori_loop(0, ng, step, (carry[...], carry_a[...]))

    ti = (lambda c, i: (nt - 1 - i, c)) if reverse else (lambda c, i: (i, c))
    spec = pl.BlockSpec((tt, cb), ti)
    n_out = 1 if reverse else 2
    res = pl.pallas_call(
        body, name=name, grid=(Wd // cb, nt), in_specs=[spec, spec], out_specs=[spec] * n_out,
        out_shape=[jax.ShapeDtypeStruct((T, Wd), f32)] * n_out,
        scratch_shapes=[pltpu.VMEM((SUBLANE, cb), f32), pltpu.VMEM((SUBLANE, cb), f32)],
        compiler_params=_cparams("parallel", "arbitrary"),
    )(a, b)
    return res[0] if reverse else res


def _relu2_epilogue(r):
    h = jnp.maximum(r, 0.0)
    return r, h * h


def _drelu2_epilogue(r, a):
    return (r * (2.0 * jnp.maximum(a.astype(f32), 0.0)),)


def _add_epilogue(r, e):
    return (r + e,)


def _merge_cols(name, g, tm=256):
    _, L, R, s = g.shape

    def body(g_ref, o_ref):
        for d in range(N_DEV):
            o_ref[:, s * d:s * (d + 1)] = g_ref[d].astype(bf16)
        o_ref[:, N_DEV * s:] = jnp.zeros((tm, HYB_PROJ_PAD - N_DEV * s), bf16)

    return pl.pallas_call(
        body, name=name, grid=(L, R // tm),
        in_specs=[pl.BlockSpec((N_DEV, None, tm, s), lambda l, i: (0, l, i, 0))],
        out_specs=pl.BlockSpec((None, tm, HYB_PROJ_PAD), lambda l, i: (l, i, 0)),
        out_shape=jax.ShapeDtypeStruct((L, R, HYB_PROJ_PAD), bf16),
        compiler_params=_cparams("parallel", "parallel"),
    )(g)


def _split_cols(name, dw, tm=256):
    R = dw.shape[0]
    s = HYB_PROJ // N_DEV

    def body(g_ref, o_ref):
        for d in range(N_DEV):
            o_ref[d] = g_ref[:, s * d:s * (d + 1)].astype(bf16)

    return pl.pallas_call(
        body, name=name, grid=(R // tm,),
        in_specs=[pl.BlockSpec((tm, HYB_PROJ_PAD), lambda i: (i, 0))],
        out_specs=pl.BlockSpec((N_DEV, tm, s), lambda i: (0, i, 0)),
        out_shape=jax.ShapeDtypeStruct((N_DEV, R, s), bf16),
        compiler_params=_cparams("parallel"),
    )(dw)


def _rows_to_dev(dw):
    nb, r, c = dw.shape
    t = dw.reshape(nb, N_DEV, r // N_DEV, c)
    return jnp.moveaxis(t, 1, 0).reshape(N_DEV, nb * (r // N_DEV), c).astype(bf16)


def _ln_epilogue(r, x, g, b):
    return r, _ln_res_fn(x, r, g, b)[0]


def _hybrid_fwd(tag, x, W, j, cos, sin, ln, before_out):
    proj = _mm(f"{tag}_proj", x, W["hyb_w_in"][j], "nn", b_kind="lead", b_lead=0)
    o_a = _attn_fwd(f"{tag}_attn", proj, cos, sin, W["hyb_sinks"][j][None, :])
    c = _conv_fwd(f"{tag}_conv", proj, CB_CONV, 12, W["hyb_conv_w"][j], None)
    o_ab, s_saved, t_saved = _delta_fwd(f"{tag}_delta", c, proj, W["hyb_a_log"][j].reshape(B_HEADS, 1, 1),
                                        W["hyb_dt_bias"][j].reshape(B_HEADS, 1, 1), W["hyb_norm_w"][j][None, :], o_a)
    before_out(o_ab)
    mix, x1 = _mm(f"{tag}_out", o_ab, W["hyb_w_out"][j], "nn", b_kind="lead", b_lead=0, epilogue=_ln_epilogue,
                  extras=(x,), params=ln, out_dtypes=(f32, f32), tm=512)
    return mix, x1, (proj, c, s_saved, t_saved, o_ab)


def _hybrid_bwd(tag, x, dmix, addend, W, j, cos, sin, saved, G, send_early):
    proj, c, s_saved, t_saved, o_ab = saved
    T = x.shape[0]
    d_oab = _mm(f"{tag}_dout", dmix, W["hyb_w_out"][j], "nt", b_kind="lead", b_lead=0)
    G["hyb_w_out"][j] = _mm(f"{tag}_dwout", o_ab, dmix, "tn", out_dtypes=(bf16,)).reshape(N_DEV, -1, D_MODEL)
    sinks = W["hyb_sinks"][j][None, :] + send_early({("hyb_w_out", j): G["hyb_w_out"][j]})
    dproj, dsinks = _attn_bwd(f"{tag}_dattn", proj, cos, sin, sinks, d_oab)
    a_log = W["hyb_a_log"][j].reshape(B_HEADS, 1, 1)
    dt_bias = W["hyb_dt_bias"][j].reshape(B_HEADS, 1, 1)
    dc, dproj, dal, ddt, dnw = _delta_bwd(f"{tag}_ddelta", c, proj, a_log, dt_bias, W["hyb_norm_w"][j][None, :],
                                          s_saved, t_saved, d_oab, dproj)
    dproj, dconv_w, _ = _conv_bwd(f"{tag}_dconv", dc, proj, CB_CONV, 12, W["hyb_conv_w"][j], dproj, CB_CONV)
    dx = _mm(f"{tag}_dx", dproj, W["hyb_w_in"][j], "nt", b_kind="lead", b_lead=0, epilogue=_add_epilogue,
             extras=(addend,))
    G["hyb_w_in"][j] = _split_cols(f"{tag}_dwin_split", _mm(f"{tag}_dwin", x, dproj, "tn", tn=1536))
    G["hyb_sinks"][j] = dsinks[0]
    G["hyb_conv_w"][j] = dconv_w
    G["hyb_a_log"][j] = dal.reshape(B_HEADS)
    G["hyb_dt_bias"][j] = ddt.reshape(B_HEADS)
    G["hyb_norm_w"][j] = dnw[0]
    return dx


def _rec_fwd(tag, x, W, j, ln, before_out):
    Wd = D_MODEL
    proj = _mm(f"{tag}_proj", x, W["rec_w_in"][j], "nn", b_kind="devcol", b_lead=0)
    xc = _conv_fwd(f"{tag}_conv", proj, 0, Wd // LANE, W["rec_conv_w"][j], W["rec_conv_b"][j][None, :])
    pre_r, pre_i = _blockdiag_fwd(f"{tag}_gates", xc, W["rec_w_a"][j][0], W["rec_w_x"][j][0])
    pars = [W["rec_b_a"][j][None, :], W["rec_b_x"][j][None, :], W["rec_lambda"][j][None, :]]
    a, b = _tl_fwd(f"{tag}_pre", _rglru_pre_fn, [(pre_r, 0, Wd), (pre_i, 0, Wd), (xc, 0, Wd)], pars, [Wd, Wd], [f32, f32])
    h, h_prev = _scan(f"{tag}_scan", a, b, False)
    (hg,) = _tl_fwd(f"{tag}_gate", _rec_gate_fn, [(h, 0, Wd), (proj, Wd // LANE, Wd)], [], [Wd], [f32])
    before_out(hg)
    mix, x1 = _mm(f"{tag}_out", hg, W["rec_w_out"][j], "nn", b_kind="lead", b_lead=0, epilogue=_ln_epilogue,
                  extras=(x,), params=ln, out_dtypes=(f32, f32), tm=512)
    return mix, x1, (proj, xc, pre_r, pre_i, a, h, h_prev, hg)


def _rec_bwd(tag, x, dmix, addend, W, j, saved, G, send_early):
    proj, xc, pre_r, pre_i, a, h, h_prev, hg = saved
    Wd = D_MODEL
    dhg = _mm(f"{tag}_dout", dmix, W["rec_w_out"][j], "nt", b_kind="lead", b_lead=0)
    G["rec_w_out"][j] = _mm(f"{tag}_dwout", hg, dmix, "tn", out_dtypes=(bf16,)).reshape(N_DEV, -1, D_MODEL)
    sent = send_early({("rec_w_out", j): G["rec_w_out"][j]})
    (dh, dproj), _ = _tl_bwd(f"{tag}_dgate", _rec_gate_fn, [(h, 0, Wd), (proj, Wd // LANE, Wd)], [], [(dhg, 0, Wd)],
                             wide={1: (2 * Wd, 1)})
    lam_t = _scan(f"{tag}_dscan", a, dh, True)
    pars = [W["rec_b_a"][j][None, :] + sent, W["rec_b_x"][j][None, :], W["rec_lambda"][j][None, :]]
    (dpr, dpi, dxc1), (db_a, db_x, dlam) = _tl_bwd(
        f"{tag}_dpre", _rglru_pre_fn, [(pre_r, 0, Wd), (pre_i, 0, Wd), (xc, 0, Wd)], pars,
        [(lam_t, 0, Wd), (h_prev, 0, Wd)], cot_fn=lambda lt, hp: (lt * hp, lt))
    dxc = _blockdiag_bwd_dx(f"{tag}_dgates_dx", dpr, dpi, W["rec_w_a"][j][0], W["rec_w_x"][j][0], dxc1)
    dwa, dwx = _blockdiag_bwd_dw(f"{tag}_dgates_dw", xc, dpr, dpi)
    G["rec_w_a"][j], G["rec_w_x"][j] = _rows_to_dev(dwa), _rows_to_dev(dwx)
    dproj, dconv_w, dconv_b = _conv_bwd(f"{tag}_dconv", dxc, proj, 0, Wd // LANE, W["rec_conv_w"][j], dproj, 0)
    dx = _mm(f"{tag}_dx", dproj, W["rec_w_in"][j], "nt", b_kind="devcol", b_lead=0, epilogue=_add_epilogue,
             extras=(addend,))
    G["rec_w_in"][j] = _mm(f"{tag}_dwin", x, dproj, "tn", o_kind="devcol", out_dtypes=(bf16,), tn=2048)
    G["rec_conv_w"][j] = dconv_w
    G["rec_conv_b"][j] = dconv_b
    G["rec_b_a"][j] = db_a[0]
    G["rec_b_x"][j] = db_x[0]
    G["rec_lambda"][j] = dlam[0]
    return dx


def _local_step(x, target, W, load_layer, grads_ready):
    T = x.shape[0]
    cos, sin = _rope_tables(T)
    saved = []
    for layer in range(DEPTH):
        j = layer // 2
        tag = f"L{layer}"
        load_layer(layer, 0, x)
        ln1 = (W["ln1_g"][layer][None, :], W["ln1_b"][layer][None, :])
        before_out = functools.partial(load_layer, layer, 1)
        if layer % 2 == 0:
            mix, x1, sv = _hybrid_fwd(tag, x, W, j, cos, sin, ln1, before_out)
        else:
            mix, x1, sv = _rec_fwd(tag, x, W, j, ln1, before_out)
        load_layer(layer, 2, x1)
        a, h2 = _mm(f"{tag}_mlp1", x1, W["mlp_w1"][layer], "nn", b_kind="devcol", b_lead=0, epilogue=_relu2_epilogue,
                    out_dtypes=(bf16, bf16), tm=2048)
        ln2 =(W["ln2_g"][layer][None, :], W["ln2_b"][layer][None, :])
        y, x2 = _mm(f"{tag}_mlp2", h2, W["mlp_w2"][layer], "nn", b_kind="devrow", b_lead=0, epilogue=_ln_epilogue,
                    extras=(x1,), params=ln2, out_dtypes=(f32, f32))
        saved.append((x, sv, mix, x1, a, h2, y))
        x = x2
    loss, dx = _loss_head(x, target)

    G = {k: [None] * (DEPTH if k.startswith(("ln", "mlp")) else DEPTH // 2) for k in (
        "hyb_w_in", "hyb_sinks", "hyb_conv_w", "hyb_a_log", "hyb_dt_bias", "hyb_norm_w", "hyb_w_out",
        "rec_w_in", "rec_conv_w", "rec_conv_b", "rec_w_a", "rec_b_a", "rec_w_x", "rec_b_x", "rec_lambda", "rec_w_out",
        "ln1_g", "ln1_b", "mlp_w1", "mlp_w2", "ln2_g", "ln2_b")}
    order = jnp.zeros((1, 1), f32)
    for layer in reversed(range(DEPTH)):
        j = layer // 2
        tag = f"L{layer}"
        x0, sv, mix, x1, a, h2, y = saved[layer]
        ln2 = [W["ln2_g"][layer][None, :] + order, W["ln2_b"][layer][None, :]]
        (dx1_a, dy), (dg2, db2) = _tl_bwd(f"{tag}_dln2", _ln_res_fn, [(x1, 0, D_MODEL), (y, 0, D_MODEL)], ln2,
                                          [(dx, 0, D_MODEL)])
        G["ln2_g"][layer], G["ln2_b"][layer] = dg2[0], db2[0]
        da = _mm(f"{tag}_dmlp2", dy, W["mlp_w2"][layer], "nt", b_kind="devrow", b_lead=0, epilogue=_drelu2_epilogue,
                 extras=(a,), out_dtypes=(bf16,), tm=2048)
        G["mlp_w2"][layer] = _mm(f"{tag}_dw2", h2, dy, "tn", out_dtypes=(bf16,), tm=2048).reshape(N_DEV, -1, D_MODEL)
        dx1 = _mm(f"{tag}_dmlp1", da, W["mlp_w1"][layer], "nt", b_kind="devcol", b_lead=0, epilogue=_add_epilogue,
                  extras=(dx1_a,))
        G["mlp_w1"][layer] = _mm(f"{tag}_dw1", x1, da, "tn", o_kind="devcol", out_dtypes=(bf16,), tn=2048)
        ln1 = [W["ln1_g"][layer][None, :], W["ln1_b"][layer][None, :]]
        (dx0_a, dmix), (dg1, db1) = _tl_bwd(f"{tag}_dln1", _ln_res_fn, [(x0, 0, D_MODEL), (mix, 0, D_MODEL)], ln1,
                                            [(dx1, 0, D_MODEL)])
        G["ln1_g"][layer], G["ln1_b"][layer] = dg1[0], db1[0]
        early = functools.partial(grads_ready, f"l{layer}_early",
                                  {(k, layer): G[k][layer] for k in ("mlp_w1", "mlp_w2")})
        if layer % 2 == 0:
            dx = _hybrid_bwd(tag, x0, dmix, dx0_a, W, j, cos, sin, sv, G, early)
        else:
            dx = _rec_bwd(tag, x0, dmix, dx0_a, W, j, sv, G, early)
        order = grads_ready(f"l{layer}_late", {}, {(k, i): G[k][i] for k, i in _layer_weights(layer)[:-2]
                                                  if not k.endswith("w_out")})
    big = {k for k, _ in BIG}
    return loss, dx, {k: jnp.stack(v) for k, v in G.items() if k not in big}


def _layer_weights(layer):
    j = layer // 2
    mixer = ["hyb_w_in", "hyb_w_out"] if layer % 2 == 0 else ["rec_w_in", "rec_w_out", "rec_w_a", "rec_w_x"]
    return [(k, j) for k in mixer] + [("mlp_w1", layer), ("mlp_w2", layer)]


def _my_coords():
    return lax.axis_index("x"), lax.axis_index("y"), lax.axis_index("c")


def _all_gather(name, arrays):
    na = len(arrays)

    def body(*refs):
        x_refs, out_refs = refs[:na], refs[na:2 * na]
        send_sems, recv_sems, local_sems = refs[2 * na:]
        x, y, c = _my_coords()
        me, sibling = (x, y, c), (x, y, 1 - c)
        chips = [(1 - x, y), (x, 1 - y), (1 - x, 1 - y)]

        def blk(a, px, py, pc):
            return out_refs[a].at[4 * px + 2 * py + pc]

        def copy(a, k, block, to, src=None):
            return pltpu.make_async_remote_copy(
                src_ref=blk(a, *block) if src is None else src, dst_ref=blk(a, *block),
                send_sem=send_sems.at[a, k], recv_sem=recv_sems.at[a, k],
                device_id=to, device_id_type=pl.DeviceIdType.MESH)

        mine = [pltpu.make_async_copy(x_refs[a], blk(a, *me), local_sems.at[a]) for a in range(na)]
        for cp in mine:
            cp.start()
        first = []
        for a in range(na):
            first.append(copy(a, 0, me, sibling, src=x_refs[a]))
            first += [copy(a, 1 + j, me, (*chip, c), src=x_refs[a]) for j, chip in enumerate(chips)]
        for cp in first:
            cp.start()
        passed = []
        for a in range(na):
            for j, chip in enumerate(chips):
                copy(a, 1 + j, (*chip, c), me).wait_recv()
                passed.append(copy(a, 4 + j, (*chip, c), sibling))
                passed[-1].start()
        for a in range(na):
            copy(a, 0, sibling, me).wait_recv()
            for j, chip in enumerate(chips):
                copy(a, 4 + j, (*chip, 1 - c), me).wait_recv()
        for cp in first + passed:
            cp.wait_send()
        for cp in mine:
            cp.wait()

    return pl.pallas_call(
        body, name=name,
        out_shape=[jax.ShapeDtypeStruct((N_DEV,) + a.shape, a.dtype) for a in arrays],
        in_specs=[pl.BlockSpec(memory_space=pl.ANY)] * na,
        out_specs=[pl.BlockSpec(memory_space=pl.ANY)] * na,
        scratch_shapes=[pltpu.SemaphoreType.DMA((na, 7)), pltpu.SemaphoreType.DMA((na, 7)),
                        pltpu.SemaphoreType.DMA((na,))],
    )(*arrays)


_HBM = pl.BlockSpec(memory_space=pltpu.HBM)
_SEM = pl.BlockSpec(memory_space=pltpu.SEMAPHORE)


def _flip(k, x, y, c):
    return ((1 - x) if k & 4 else x, (1 - y) if k & 2 else y, (1 - c) if k & 1 else c)


_PEERS = {"gather": (1, 2, 4, 6), "scatter": (1, 2, 3, 4, 5, 6, 7)}


def _push_copies(kind, x_refs, land_refs, send_sems, recv_sems, local_sems):
    x, y, c = _my_coords()
    me = 4 * x + 2 * y + c
    peers = _PEERS[kind]
    remote, local = [], []
    for a in range(len(x_refs)):
        local.append(pltpu.make_async_copy(x_refs[a] if kind == "gather" else x_refs[a].at[me], land_refs[a].at[me],
                                           local_sems.at[a]))
        for n, k in enumerate(peers):
            px, py, pc = _flip(k, x, y, c)
            remote.append(pltpu.make_async_remote_copy(
                src_ref=x_refs[a] if kind == "gather" else x_refs[a].at[4 * px + 2 * py + pc],
                dst_ref=land_refs[a].at[me],
                send_sem=send_sems.at[a * len(peers) + n], recv_sem=recv_sems.at[a * len(peers) + n],
                device_id=(px, py, pc), device_id_type=pl.DeviceIdType.MESH))
    return remote, local


def _pass_to_sibling(name, lands):
    na = len(lands)
    chips = (2, 4, 6)

    def body(*refs):
        out_refs, send_sems, recv_sems = refs[na:2 * na], refs[2 * na], refs[2 * na + 1]
        x, y, c = _my_coords()
        cps = []
        for a in range(na):
            for n, k in enumerate(chips):
                px, py, _ = _flip(k, x, y, c)
                cps.append(pltpu.make_async_remote_copy(
                    src_ref=out_refs[a].at[4 * px + 2 * py + c], dst_ref=out_refs[a].at[4 * px + 2 * py + c],
                    send_sem=send_sems.at[a * 3 + n], recv_sem=recv_sems.at[a * 3 + n],
                    device_id=(x, y, 1 - c), device_id_type=pl.DeviceIdType.MESH))
        for cp in cps:
            cp.start()
        for a in range(na):
            for n, k in enumerate(chips):
                px, py, _ = _flip(k, x, y, c)
                blk = out_refs[a].at[4 * px + 2 * py + (1 - c)]
                pltpu.make_async_remote_copy(src_ref=blk, dst_ref=blk, send_sem=send_sems.at[a * 3 + n],
                                             recv_sem=recv_sems.at[a * 3 + n], device_id=(x, y, 1 - c),
                                             device_id_type=pl.DeviceIdType.MESH).wait_recv()
        for cp in cps:
            cp.wait_send()

    return pl.pallas_call(
        body, name=name,
        out_shape=[jax.ShapeDtypeStruct(l.shape, l.dtype) for l in lands],
        in_specs=[pl.BlockSpec(memory_space=pl.ANY)] * na,
        out_specs=[pl.BlockSpec(memory_space=pl.ANY)] * na,
        input_output_aliases={a: a for a in range(na)},
        scratch_shapes=[pltpu.SemaphoreType.DMA((3 * na,)), pltpu.SemaphoreType.DMA((3 * na,))],
    )(*lands)


_SIDE_EFFECT = pltpu.CompilerParams(has_side_effects=pltpu.SideEffectType.DATAFLOW_SIDE_EFFECTING)


def _push_start(name, kind, srcs, lands):
    na = len(srcs)

    def body(*refs):
        remote, local = _push_copies(kind, refs[:na], refs[na:2 * na], *refs[2 * na:2 * na + 3])
        for cp in remote + local:
            cp.start()
        token = refs[-1]
        token[...] = jnp.zeros_like(token)

    arrays = list(srcs) + list(lands)
    n_remote = na * len(_PEERS[kind])
    res = pl.pallas_call(
        body, name=name,
        out_shape=(pltpu.SemaphoreType.DMA((n_remote,)), pltpu.SemaphoreType.DMA((n_remote,)),
                   pltpu.SemaphoreType.DMA((na,)), *[pltpu.HBM(t.shape, t.dtype) for t in arrays],
                   jax.ShapeDtypeStruct((SUBLANE, LANE), f32)),
        in_specs=[_HBM] * (2 * na),
        out_specs=(_SEM, _SEM, _SEM, *[_HBM] * (2 * na), pl.BlockSpec(memory_space=pltpu.VMEM)),
        input_output_aliases={i: 3 + i for i in range(2 * na)},
        compiler_params=_SIDE_EFFECT,
    )(*[pltpu.with_memory_space_constraint(t, pltpu.HBM) for t in arrays])
    return list(res[:3]), res[3:3 + na], res[3 + na:3 + 2 * na], res[-1][:1, :1]


def _push_wait(name, kind, sems, srcs, lands, after):
    na = len(srcs)

    def body(*refs):
        remote, local = _push_copies(kind, refs[:na], refs[na:2 * na], *refs[2 * na:2 * na + 3])
        for cp in remote:
            cp.wait_send()
            cp.wait_recv()
        for cp in local:
            cp.wait()

    arrays = list(srcs) + list(lands)
    res = pl.pallas_call(
        body, name=name,
        out_shape=tuple(pltpu.HBM(t.shape, t.dtype) for t in arrays),
        in_specs=[_HBM] * (2 * na) + [_SEM] * 3 + [pl.BlockSpec(memory_space=pl.ANY)],
        out_specs=tuple([_HBM] * (2 * na)),
        input_output_aliases={i: i for i in range(2 * na)},
        compiler_params=_SIDE_EFFECT,
    )(*arrays, *sems, after)
    return res[na:]


def _sum_blocks(name, land):
    _, R, n = land.shape
    tr = R

    def body(l_ref, o_ref):
        acc = l_ref[0].astype(f32)
        for s in range(1, N_DEV):
            acc = acc + l_ref[s].astype(f32)
        o_ref[...] = acc

    return pl.pallas_call(
        body, name=name, grid=(R // tr,),
        in_specs=[pl.BlockSpec((N_DEV, tr, n), lambda i: (0, i, 0))],
        out_specs=pl.BlockSpec((tr, n), lambda i: (i, 0)),
        out_shape=jax.ShapeDtypeStruct((R, n), f32),
        compiler_params=_cparams("parallel"),
    )(land)


def _adamw(name, w, g, m, v):
    shape = w.shape
    last = shape[-1]
    rows = math.prod(shape[:-1])
    tm = 256 if rows % 256 == 0 and rows > 256 else rows
    w2, g2, m2, v2 = (t.reshape(rows, last) for t in (w, g, m, v))

    def body(w_ref, g_ref, m_ref, v_ref, d_ref, mo_ref, vo_ref):
        gg = g_ref[...]
        mn = ADAM_B1 * m_ref[...] + (1.0 - ADAM_B1) * gg
        vn = ADAM_B2 * v_ref[...] + (1.0 - ADAM_B2) * jnp.square(gg)
        m_hat = mn / (1.0 - ADAM_B1 ** ADAM_STEP)
        v_hat = vn / (1.0 - ADAM_B2 ** ADAM_STEP)
        d_ref[...] = -ADAM_LR * (m_hat / (jnp.sqrt(v_hat) + ADAM_EPS) + ADAM_WD * w_ref[...])
        mo_ref[...] = mn
        vo_ref[...] = vn

    spec = pl.BlockSpec((tm, last), lambda i: (i, 0))
    d, mn, vn = pl.pallas_call(
        body, name=name, grid=(rows // tm,), in_specs=[spec] * 4, out_specs=[spec] * 3,
        out_shape=[jax.ShapeDtypeStruct((rows, last), f32)] * 3,
        compiler_params=_cparams("parallel"),
    )(w2, g2, m2, v2)
    return d.reshape(shape), mn.reshape(shape), vn.reshape(shape)


def _adamw_land(name, lands, w, m, v, tm=256):
    L = len(lands)
    _, R, C = lands[0].shape
    tm = min(tm, R)

    def body(*refs):
        l_refs, (w_ref, m_ref, v_ref, g_ref, d_ref, mo_ref, vo_ref) = refs[:L], refs[L:]
        for k in range(L):
            @pl.when(pl.program_id(0) == k)
            def _(k=k):
                gg = l_refs[k][0].astype(f32)
                for s in range(1, N_DEV):
                    gg = gg + l_refs[k][s].astype(f32)
                g_ref[...] = gg
                mn = ADAM_B1 * m_ref[...] + (1.0 - ADAM_B1) * gg
                vn = ADAM_B2 * v_ref[...] + (1.0 - ADAM_B2) * jnp.square(gg)
                m_hat = mn / (1.0 - ADAM_B1 ** ADAM_STEP)
                v_hat = vn / (1.0 - ADAM_B2 ** ADAM_STEP)
                d_ref[...] = -ADAM_LR * (m_hat / (jnp.sqrt(v_hat) + ADAM_EPS) + ADAM_WD * w_ref[...])
                mo_ref[...] = mn
                vo_ref[...] = vn

    land_specs = [pl.BlockSpec((N_DEV, tm, C), lambda l, i, k=k: (0, jnp.where(l == k, i, 0), 0)) for k in range(L)]
    spec = pl.BlockSpec((None, tm, C), lambda l, i: (l, i, 0))
    return pl.pallas_call(
        body, name=name, grid=(L, R // tm),
        in_specs=land_specs + [spec] * 3,
        out_specs=[spec] * 4,
        out_shape=[jax.ShapeDtypeStruct((L, R, C), f32)] * 4,
        compiler_params=_cparams("arbitrary", "arbitrary"),
    )(*lands, w, m, v)


BIG = [("hyb_w_in", 2), ("hyb_w_out", 1), ("rec_w_in", 2), ("rec_w_out", 1), ("rec_w_a", 2), ("rec_w_x", 2),
       ("mlp_w1", 2), ("mlp_w2", 1)]
SMALL = [("hyb_conv_w", 2), ("rec_conv_w", 2), ("rec_conv_b", 1), ("rec_b_a", 1), ("rec_b_x", 1), ("rec_lambda", 1)]
REPL = ["hyb_sinks", "hyb_a_log", "hyb_dt_bias", "hyb_norm_w", "ln1_g", "ln1_b", "ln2_g", "ln2_b"]
WEIGHTS = ["hyb_w_in", "hyb_sinks", "hyb_conv_w", "hyb_a_log", "hyb_dt_bias", "hyb_norm_w", "hyb_w_out", "rec_w_in",
           "rec_conv_w", "rec_conv_b", "rec_w_a", "rec_b_a", "rec_w_x", "rec_b_x", "rec_lambda", "rec_w_out",
           "ln1_g", "ln1_b", "mlp_w1", "mlp_w2", "ln2_g", "ln2_b"]


def _pack_rows(parts, dtype, row_mult):
    lead = parts[0].shape[:-1]
    flat = jnp.concatenate([p.astype(dtype) for p in parts], axis=-1)
    n = flat.shape[-1]
    unit = row_mult * LANE
    pad = (-n) % unit
    if pad:
        flat = jnp.concatenate([flat, jnp.zeros(lead + (pad,), dtype)], axis=-1)
    return flat.reshape(lead + ((n + pad) // LANE, LANE))


def _gather_full(gathered, shard_shapes, table):
    flat = gathered.reshape(N_DEV, -1)
    out, off = {}, 0
    for name, ax in table:
        shp = shard_shapes[name]
        n = math.prod(shp)
        arr = flat[:, off:off + n].reshape((N_DEV,) + shp)
        off += n
        arr = jnp.moveaxis(arr, 0, ax)
        out[name] = arr.reshape(shp[:ax] + (N_DEV * shp[ax],) + shp[ax + 1:])
    return out


def _matmul_layouts(tag, gw):
    out = {}
    bw = D_MODEL // LRU_BLOCKS
    for k, g in gw.items():
        L = g.shape[1]
        if k == "hyb_w_in":
            out[k] = _merge_cols(f"{tag}_w_in_merge", g)
        elif k in ("hyb_w_out", "rec_w_out"):
            out[k] = jnp.swapaxes(g, 0, 1).reshape(L, D_MODEL, D_MODEL)
        elif k in ("rec_w_a", "rec_w_x"):
            out[k] = jnp.moveaxis(g, 0, 2).reshape(L, LRU_BLOCKS, bw, bw)
        else:
            out[k] = g
    return out


def kernel(x, hyb_w_in, hyb_sinks, hyb_conv_w, hyb_a_log, hyb_dt_bias, hyb_norm_w, hyb_w_out, rec_w_in, rec_conv_w, rec_conv_b, rec_w_a, rec_b_a, rec_w_x, rec_b_x, rec_lambda, rec_w_out, ln1_g, ln1_b, mlp_w1, mlp_w2, ln2_g, ln2_b, loss_target, m_hyb_w_in, m_hyb_sinks, m_hyb_conv_w, m_hyb_a_log, m_hyb_dt_bias, m_hyb_norm_w, m_hyb_w_out, m_rec_w_in, m_rec_conv_w, m_rec_conv_b, m_rec_w_a, m_rec_b_a, m_rec_w_x, m_rec_b_x, m_rec_lambda, m_rec_w_out, m_ln1_g, m_ln1_b, m_mlp_w1, m_mlp_w2, m_ln2_g, m_ln2_b, v_hyb_w_in, v_hyb_sinks, v_hyb_conv_w, v_hyb_a_log, v_hyb_dt_bias, v_hyb_norm_w, v_hyb_w_out, v_rec_w_in, v_rec_conv_w, v_rec_conv_b, v_rec_w_a, v_rec_b_a, v_rec_w_x, v_rec_b_x, v_rec_lambda, v_rec_w_out, v_ln1_g, v_ln1_b, v_mlp_w1, v_mlp_w2, v_ln2_g, v_ln2_b):
    args = locals()
    w = {k: args[k] for k in WEIGHTS}
    m = {k: args["m_" + k] for k in WEIGHTS}
    v = {k: args["v_" + k] for k in WEIGHTS}
    shard_shapes = {k: tuple(t.shape) for k, t in w.items()}
    xi, yi, ci = _my_coords()
    me = 4 * xi + 2 * yi + ci

    in_flight = {}

    def install(tag, names, got):
        for (k, i), arr in zip(names, _matmul_layouts(tag, {k: g for (k, _), g in zip(names, got)}).values()):
            W[k][i] = arr

    def start_gather(tag, names):
        srcs = [w[k][i:i + 1].astype(bf16) for k, i in names]
        *pending, zero = _push_start(f"gather_{tag}_start", "gather", srcs,
                                     [lax.empty((N_DEV,) + s.shape, bf16) for s in srcs])
        in_flight[tag] = (names, pending)
        return zero

    def finish_gather(tag, after):
        names, pending = in_flight.pop(tag)
        half = _push_wait(f"gather_{tag}_wait", "gather", *pending, after)
        install(tag, names, _pass_to_sibling(f"gather_{tag}_pass", half))

    def started(k, zero):
        W[k] = W[k] + zero

    def mixer_w(layer):
        return _layer_weights(layer)[:-2]

    def mlp_w(layer):
        return _layer_weights(layer)[-2:]

    gathered0 = _all_gather("gather_first", [w[k][i:i + 1].astype(bf16) for k, i in mixer_w(0)]
                            + [_pack_rows([w[k].reshape(-1) for k, _ in SMALL], f32, SUBLANE)])
    W = _gather_full(gathered0[-1], shard_shapes, SMALL)
    W.update({k: w[k] for k in REPL})
    W.update({k: {} for k, _ in BIG})
    install("l0a", mixer_w(0), gathered0[:-1])
    started("hyb_sinks", start_gather("l0b", mlp_w(0)) + start_gather("l1a", mixer_w(1)))

    def load_layer(layer, stage, after):
        if stage == 0:
            if layer > 0:
                finish_gather(f"l{layer}a", after)
            if 0 < layer < DEPTH - 1:
                started("hyb_sinks" if layer % 2 == 0 else "rec_conv_b",
                        start_gather(f"l{layer + 1}a", mixer_w(layer + 1)))
        if stage == 2:
            finish_gather(f"l{layer}b", after)
            if layer < DEPTH - 1:
                started("ln2_g", start_gather(f"l{layer + 1}b", mlp_w(layer + 1)))

    grads_in_flight = {}

    def grads_ready(tag, a, b):
        g = {**a, **b}
        srcs = list(g.values())
        *pending, zero = _push_start(f"scatter_{tag}_start", "scatter", srcs, [lax.empty(s.shape, bf16) for s in srcs])
        grads_in_flight[tag] = (list(g.keys()), pending)
        return zero

    loss_local, grad_x, G = _local_step(x[0], loss_target[0], W, load_layer, grads_ready)
    loss = lax.psum(loss_local, MESH_AXES)

    landed = {}

    def land(tag, after):
        keys, pending = grads_in_flight[tag]
        landed.update(zip(keys, _push_wait(f"scatter_{tag}_wait", "scatter", *pending, after)))

    tags = list(grads_in_flight)
    for tag in tags[:-1]:
        land(tag, grad_x)
    rest = _pack_rows([G[k].reshape(-1) for k, _ in SMALL] + [G[k].reshape(-1) for k in REPL], f32, SUBLANE)
    g_rest = _sum_blocks("sum_rest", _all_gather("gather_rest", [rest])[0]).reshape(-1)

    grads, delta, new_m, new_v = {}, {}, {}, {}

    def adamw_big(k):
        shp = shard_shapes[k]
        s3 = (shp[0], math.prod(shp[1:-1]), shp[-1])
        lands = [landed[(k, i)].reshape((N_DEV,) + s3[1:]) for i in range(shp[0])]
        res = _adamw_land("adamw_" + k, lands, w[k].reshape(s3), m[k].reshape(s3), v[k].reshape(s3))
        grads[k], delta[k], new_m[k], new_v[k] = (r.reshape(shp) for r in res)

    late = {k for k, _ in grads_in_flight[tags[-1]][0]}
    for k in [k for k, _ in BIG if k not in late]:
        adamw_big(k)
        done = new_v[k]
    land(tags[-1], done)
    for k in [k for k, _ in BIG if k in late]:
        adamw_big(k)
    off = 0
    for k, ax in SMALL:
        full_shape = G[k].shape
        n = math.prod(full_shape)
        full = g_rest[off:off + n].reshape(full_shape)
        off += n
        s = shard_shapes[k][ax]
        grads[k] = lax.dynamic_slice_in_dim(full, me * s, s, axis=ax)
    for k in REPL:
        n = math.prod(shard_shapes[k])
        grads[k] = g_rest[off:off + n].reshape(shard_shapes[k])
        off += n

    for k in [k for k, _ in SMALL] + REPL:
        delta[k], new_m[k], new_v[k] = _adamw("adamw_" + k, w[k], grads[k], m[k], v[k])

    return (loss, grad_x[None], *[grads[k] for k in WEIGHTS], *[delta[k] for k in WEIGHTS],
            *[new_m[k] for k in WEIGHTS], *[new_v[k] for k in WEIGHTS])
```

```python
import functools
import math

import jax
import jax.numpy as jnp
from jax import lax
from jax.experimental import pallas as pl
from jax.experimental.pallas import tpu as pltpu

f32 = jnp.float32
bf16 = jnp.bfloat16

N_DEV = 8
D_MODEL = 1024
DEPTH = 4
A_HEAD_DIM = 64
A_Q_HEADS = 8
WINDOW = 128
ROPE_THETA = 10000.0
B_HEADS = 4
B_HEAD_DIM = 128
B_CHUNK = 64
LRU_BLOCKS = 4
LRU_C = 8.0
D_FF = 4 * D_MODEL
HYB_PROJ = 2824
HYB_PROJ_PAD = 3072
DN_ALPHA = (2 * DEPTH) ** 0.25
LN_EPS = 1e-5
NORM_EPS = 1e-6
ADAM_LR = 0.001
ADAM_B1 = 0.9
ADAM_B2 = 0.999
ADAM_EPS = 1e-08
ADAM_WD = 0.01
ADAM_STEP = 10

LANE = 128
SUBLANE = 8
VMEM_LIMIT = 48 * 1024 * 1024

CB_QA, CB_KA, CB_VA, CB_CONV, CB_Z, CB_LG = 0, 4, 5, 6, 18, 22

MESH_AXES = ("x", "y", "c")


def _cparams(*sem):
    return pltpu.CompilerParams(dimension_semantics=sem, vmem_limit_bytes=VMEM_LIMIT)


def _dot(a, b, dims, precision=None):
    return lax.dot_general(a, b, (dims, ((), ())), preferred_element_type=f32, precision=precision)


NN = ((1,), (0,))
NT = ((1,), (1,))
TN = ((0,), (0,))


def _mat_spec(arr, kind, lead, br, bc, rb, cb):
    if kind == "plain":
        return pl.BlockSpec((br, bc), lambda i, j, k: (rb(i, j, k), cb(i, j, k)))
    if kind == "lead":
        return pl.BlockSpec((None, br, bc), lambda i, j, k: (lead, rb(i, j, k), cb(i, j, k)))
    if kind == "devcol":
        assert bc == arr.shape[-1]
        return pl.BlockSpec((None, None, br, bc), lambda i, j, k: (cb(i, j, k), lead, rb(i, j, k), 0))
    assert kind == "devrow" and br == arr.shape[-2]
    return pl.BlockSpec((None, None, br, bc), lambda i, j, k: (rb(i, j, k), lead, 0, cb(i, j, k)))


def _mm(name, a, b, mode, *, b_kind="plain", b_lead=0, o_kind="plain", epilogue=None, extras=(), params=(),
        out_dtypes=(f32,), tm=1024, tn=1024, tk=None):
    if tk is None:
        tk = 512 if mode == "tn" else 1024
    if b_kind in ("plain", "lead"):
        b_rows, b_cols = b.shape[-2:]
    elif b_kind == "devcol":
        b_rows, b_cols = b.shape[-2], N_DEV * b.shape[-1]
    else:
        b_rows, b_cols = N_DEV * b.shape[-2], b.shape[-1]
    if mode == "nn":
        (M, K), (K2, N) = a.shape, (b_rows, b_cols)
    elif mode == "nt":
        (M, K), (N, K2) = a.shape, (b_rows, b_cols)
    else:
        (K, M), (K2, N) = a.shape, (b_rows, b_cols)
    assert K == K2, (name, a.shape, b.shape, mode)
    tm, tn, tk = min(tm, M), min(tn, N), min(tk, K)
    cols_are_n = mode != "nt"
    if b_kind == "devcol":
        tn, tk = (b.shape[-1], tk) if cols_are_n else (tn, b.shape[-1])
    if b_kind == "devrow":
        tn, tk = (tn, b.shape[-2]) if cols_are_n else (b.shape[-2], tk)
    shard = N // N_DEV
    if o_kind == "devcol":
        tn = max(shard, tn // shard * shard)
    assert M % tm == 0 and N % tn == 0 and K % tk == 0, (name, M, N, K, tm, tn, tk)
    nk = K // tk
    dims = {"nn": NN, "nt": NT, "tn": TN}[mode]
    n_ex, n_out = len(extras) + len(params), len(out_dtypes)

    def body(*refs):
        a_ref, b_ref = refs[:2]
        ex = refs[2:2 + n_ex]
        outs = refs[2 + n_ex:2 + n_ex + n_out]
        acc = refs[-1]
        k = pl.program_id(2)

        @pl.when(k == 0)
        def _():
            acc[...] = jnp.zeros_like(acc)

        acc[...] += _dot(a_ref[...].astype(bf16), b_ref[...].astype(bf16), dims)

        @pl.when(k == nk - 1)
        def _():
            r = acc[...]
            res = epilogue(r, *[e[...] for e in ex]) if epilogue is not None else (r,)
            for o, v in zip(outs, res):
                if o_kind == "plain":
                    o[...] = v.astype(o.dtype)
                else:
                    for q in range(tn // shard):
                        o[q] = v[:, q * shard:(q + 1) * shard].astype(o.dtype)

    if mode == "tn":
        a_spec = pl.BlockSpec((tk, tm), lambda i, j, k: (k, i))
    else:
        a_spec = pl.BlockSpec((tm, tk), lambda i, j, k: (i, k))
    jb, kb = (lambda i, j, k: j), (lambda i, j, k: k)
    if mode == "nt":
        b_spec = _mat_spec(b, b_kind, b_lead, tn, tk, jb, kb)
    else:
        b_spec = _mat_spec(b, b_kind, b_lead, tk, tn, kb, jb)
    e_spec = pl.BlockSpec((tm, tn), lambda i, j, k: (i, j))
    if o_kind == "plain":
        o_spec, o_shape = e_spec, (M, N)
    else:
        o_spec, o_shape = pl.BlockSpec((tn // shard, tm, shard), lambda i, j, k: (j, i, 0)), (N_DEV, M, shard)
    res = pl.pallas_call(
        body, name=name,
        grid=(M // tm, N // tn, nk),
        in_specs=[a_spec, b_spec] + [e_spec] * len(extras)
        + [pl.BlockSpec(p.shape, lambda i, j, k: (0, 0)) for p in params],
        out_specs=[o_spec] * n_out,
        out_shape=[jax.ShapeDtypeStruct(o_shape, dt) for dt in out_dtypes],
        scratch_shapes=[pltpu.VMEM((tm, tn), f32)],
        compiler_params=_cparams("parallel", "parallel", "arbitrary"),
    )(a, b, *extras, *params)
    return res[0] if n_out == 1 else res


def _row_spec(tm, cb, width):
    assert (cb * LANE) % width == 0
    blk = (cb * LANE) // width
    return pl.BlockSpec((tm, width), lambda i: (i, blk))


def _whole_spec(p):
    nd = p.ndim
    return pl.BlockSpec(p.shape, lambda i: (0,) * nd)


def _tl_fwd(name, fn, rows, params, out_widths, out_dtypes, tm=256):
    T = rows[0][0].shape[0]
    tm = min(tm, T)
    nr, npar = len(rows), len(params)

    def body(*refs):
        vals = [r[...] for r in refs[:nr + npar]]
        outs = fn(*vals)
        for o, v in zip(refs[nr + npar:], outs):
            o[...] = v.astype(o.dtype)

    res = pl.pallas_call(
        body, name=name, grid=(T // tm,),
        in_specs=[_row_spec(tm, cb, w) for (_, cb, w) in rows] + [_whole_spec(p) for p in params],
        out_specs=[pl.BlockSpec((tm, w), lambda i: (i, 0)) for w in out_widths],
        out_shape=[jax.ShapeDtypeStruct((T, w), dt) for w, dt in zip(out_widths, out_dtypes)],
        compiler_params=_cparams("parallel"),
    )(*[r[0] for r in rows], *params)
    return res


def _tl_bwd(name, fn, rows, params, cot_rows, cot_fn=None, wide=None, tm=256):
    T = rows[0][0].shape[0]
    tm = min(tm, T)
    nr, npar, nc = len(rows), len(params), len(cot_rows)
    place = [(wide or {}).get(k, (w, 0)) for k, (_, _, w) in enumerate(rows)]

    def body(*refs):
        vals = [r[...] for r in refs[:nr + npar]]
        cots = [r[...] for r in refs[nr + npar:nr + npar + nc]]
        outs = refs[nr + npar + nc:]
        cot = tuple(cot_fn(*cots)) if cot_fn is not None else tuple(cots)
        _, vjp = jax.vjp(fn, *vals)
        grads = vjp(cot)
        for o, g in zip(outs[:nr], grads[:nr]):
            o[...] = g.astype(o.dtype)
        i = pl.program_id(0)
        for o, g in zip(outs[nr:], grads[nr:]):
            @pl.when(i == 0)
            def _(o=o):
                o[...] = jnp.zeros_like(o)
            o[...] += g

    res = pl.pallas_call(
        body, name=name, grid=(T // tm,),
        in_specs=[_row_spec(tm, cb, w) for (_, cb, w) in rows] + [_whole_spec(p) for p in params]
        + [_row_spec(tm, cb, w) for (_, cb, w) in cot_rows],
        out_specs=[pl.BlockSpec((tm, w), lambda i, blk=blk: (i, blk)) for (_, _, w), (_, blk) in zip(rows, place)]
        + [_whole_spec(p) for p in params],
        out_shape=[jax.ShapeDtypeStruct((T, total), f32) for total, _ in place]
        + [jax.ShapeDtypeStruct(p.shape, f32) for p in params],
        compiler_params=_cparams("arbitrary"),
    )(*[r[0] for r in rows], *params, *[r[0] for r in cot_rows])
    return res[:nr], res[nr:]


def _ln_res_fn(x, mix, g, b):
    pre = DN_ALPHA * x + mix
    mu = jnp.mean(pre, axis=-1, keepdims=True)
    var = jnp.mean(jnp.square(pre - mu), axis=-1, keepdims=True)
    return ((pre - mu) * lax.rsqrt(var + LN_EPS) * g + b,)


@jax.custom_jvp
def _expm1(x):
    small = jnp.abs(x) < 0.3
    xs = jnp.where(small, x, 0.0)
    poly = xs * (1.0 + xs * (1 / 2 + xs * (1 / 6 + xs * (1 / 24 + xs * (1 / 120 + xs * (
        1 / 720 + xs * (1 / 5040 + xs * (1 / 40320 + xs * (1 / 362880)))))))))
    return jnp.where(small, poly, jnp.exp(x) - 1.0)


@_expm1.defjvp
def _expm1_jvp(primals, tangents):
    (x,), (t,) = primals, tangents
    return _expm1(x), t * jnp.exp(x)


def _rglru_pre_fn(pre_r, pre_i, xc, b_a, b_x, lam):
    r = jax.nn.sigmoid(pre_r + b_a)
    i = jax.nn.sigmoid(pre_i + b_x)
    log_a = -LRU_C * r * jax.nn.softplus(-lam)
    a = jnp.exp(log_a)
    b = jnp.sqrt(-_expm1(2.0 * log_a)) * (i * xc)
    return a, b


def _rec_gate_fn(h, gate):
    return (h * jax.nn.gelu(gate),)


def _loss_head(y, t, tm=256):
    T, Dm = y.shape

    def body(y_ref, t_ref, dy_ref, loss_ref):
        e = y_ref[...] - t_ref[...]
        dy_ref[...] = e * (1.0 / Dm)

        @pl.when(pl.program_id(0) == 0)
        def _():
            loss_ref[...] = jnp.zeros_like(loss_ref)

        loss_ref[...] += 0.5 * jnp.sum(jnp.mean(e * e, axis=-1, keepdims=True), axis=0, keepdims=True)

    dy, loss = pl.pallas_call(
        body, name="loss_head", grid=(T // tm,),
        in_specs=[pl.BlockSpec((tm, Dm), lambda i: (i, 0))] * 2,
        out_specs=[pl.BlockSpec((tm, Dm), lambda i: (i, 0)), pl.BlockSpec((SUBLANE, LANE), lambda i: (0, 0))],
        out_shape=[jax.ShapeDtypeStruct((T, Dm), f32), jax.ShapeDtypeStruct((SUBLANE, LANE), f32)],
        compiler_params=_cparams("arbitrary"),
    )(y, t)
    return loss[0, 0], dy


def _conv_fwd(name, x, cb0, nblk, w, bias, tm=2048):
    T = x.shape[0]
    tm = min(tm, T)
    hb = tm // SUBLANE
    has_b = bias is not None

    def body(*refs):
        cur, prev, w_ref = refs[:3]
        b_ref = refs[3] if has_b else None
        o = refs[-1]
        i = pl.program_id(1)
        p = jnp.where(i > 0, prev[...], 0.0)
        xcat = jnp.concatenate([p, cur[...]], axis=0)
        acc = cur[...] * w_ref[3:4, :]
        for j in range(3):
            acc = acc + pltpu.roll(xcat, 3 - j, axis=0)[SUBLANE:] * w_ref[j:j + 1, :]
        if has_b:
            acc = acc + b_ref[...]
        o[...] = acc

    in_specs = [
        pl.BlockSpec((tm, LANE), lambda c, i: (i, cb0 + c)),
        pl.BlockSpec((SUBLANE, LANE), lambda c, i: (jnp.maximum(i * hb - 1, 0), cb0 + c)),
        pl.BlockSpec((4, LANE), lambda c, i: (0, c)),
    ]
    args = [x, x, w]
    if has_b:
        in_specs.append(pl.BlockSpec((1, LANE), lambda c, i: (0, c)))
        args.append(bias)
    return pl.pallas_call(
        body, name=name, grid=(nblk, T // tm),
        in_specs=in_specs,
        out_specs=pl.BlockSpec((tm, LANE), lambda c, i: (i, c)),
        out_shape=jax.ShapeDtypeStruct((T, nblk * LANE), f32),
        compiler_params=_cparams("parallel", "parallel"),
    )(*args)


def _conv_bwd(name, dy, x, cb0, nblk, w, into, into_cb, tm=2048):
    T = x.shape[0]
    tm = min(tm, T)
    hb = tm // SUBLANE
    nt = T // tm

    def body(dcur, dnext, xcur, xprev, w_ref, _, dx_ref, dw_ref, db_ref):
        i = pl.program_id(1)
        d = dcur[...]
        dn = jnp.where(i < nt - 1, dnext[...], 0.0)
        dcat = jnp.concatenate([d, dn], axis=0)
        acc = d * w_ref[3:4, :]
        for j in range(3):
            s = 3 - j
            acc = acc + pltpu.roll(dcat, tm + SUBLANE - s, axis=0)[:tm] * w_ref[j:j + 1, :]
        dx_ref[...] = acc

        p = jnp.where(i > 0, xprev[...], 0.0)
        xcat = jnp.concatenate([p, xcur[...]], axis=0)
        rows = [jnp.sum(d * pltpu.roll(xcat, 3 - j, axis=0)[SUBLANE:], axis=0, keepdims=True) for j in range(3)]
        rows.append(jnp.sum(d * xcur[...], axis=0, keepdims=True))
        rows.append(jnp.zeros((SUBLANE - 4, LANE), f32))

        @pl.when(i == 0)
        def _():
            dw_ref[...] = jnp.zeros_like(dw_ref)
            db_ref[...] = jnp.zeros_like(db_ref)

        dw_ref[...] += jnp.concatenate(rows, axis=0)
        db_ref[...] += jnp.broadcast_to(jnp.sum(d, axis=0, keepdims=True), (SUBLANE, LANE))

    nh = T // SUBLANE
    dx, dw, db = pl.pallas_call(
        body, name=name, grid=(nblk, nt),
        in_specs=[
            pl.BlockSpec((tm, LANE), lambda c, i: (i, c)),
            pl.BlockSpec((SUBLANE, LANE), lambda c, i: (jnp.minimum((i + 1) * hb, nh - 1), c)),
            pl.BlockSpec((tm, LANE), lambda c, i: (i, cb0 + c)),
            pl.BlockSpec((SUBLANE, LANE), lambda c, i: (jnp.maximum(i * hb - 1, 0), cb0 + c)),
            pl.BlockSpec((4, LANE), lambda c, i: (0, c)),
            pl.BlockSpec(memory_space=pl.ANY),
        ],
        out_specs=[
            pl.BlockSpec((tm, LANE), lambda c, i: (i, into_cb + c)),
            pl.BlockSpec((SUBLANE, LANE), lambda c, i: (0, c)),
            pl.BlockSpec((SUBLANE, LANE), lambda c, i: (0, c)),
        ],
        out_shape=[jax.ShapeDtypeStruct(into.shape, f32),
                   jax.ShapeDtypeStruct((SUBLANE, nblk * LANE), f32),
                   jax.ShapeDtypeStruct((SUBLANE, nblk * LANE), f32)],
        input_output_aliases={5: 0},
        compiler_params=_cparams("parallel", "arbitrary"),
    )(dy, dy, x, x, w, into)
    return dx, dw[:4], db[0]


@functools.partial(jax.custom_vjp, nondiff_argnums=(1,))
def _lroll(x, s):
    return pltpu.roll(x, s, axis=1)


def _lroll_fwd(x, s):
    return _lroll(x, s), None


def _lroll_bwd(s, _, g):
    return (_lroll(g, (LANE - s) % LANE),)


_lroll.defvjp(_lroll_fwd, _lroll_bwd)


def _rope_tables(T):
    half = A_HEAD_DIM // 2
    inv_freq = ROPE_THETA ** (-jnp.arange(half, dtype=f32) / half)
    ang = jnp.arange(T, dtype=f32)[:, None] * inv_freq[None, :]
    cos, sin = jnp.cos(ang), jnp.sin(ang)
    return jnp.tile(jnp.concatenate([cos, cos], axis=1), (1, 2)), jnp.tile(jnp.concatenate([-sin, sin], axis=1), (1, 2))


def _attn_block_fn(n, q, kp, kc, vp, vc, cq, sq, cp, sp, sinks):
    W = WINDOW
    lane = lax.broadcasted_iota(jnp.int32, (W, LANE), 1)
    lo_half = (lane % A_HEAD_DIM) < (A_HEAD_DIM // 2)
    lane8 = lax.broadcasted_iota(jnp.int32, sinks.shape, 1)

    def rope(x, c, s):
        return x * c + jnp.where(lo_half, _lroll(x, LANE - A_HEAD_DIM // 2), _lroll(x, A_HEAD_DIM // 2)) * s

    k2 = jnp.concatenate([rope(kp, cp, sp), rope(kc, cq, sq)], axis=0).astype(bf16)
    v2 = jnp.concatenate([vp, vc], axis=0).astype(bf16)
    qs = []
    for t in range(4):
        qt = rope(q[:, LANE * t:LANE * (t + 1)], cq, sq)
        g = t // 2
        for hh in range(2):
            qa = jnp.where((lane // A_HEAD_DIM) == hh, qt, 0.0)
            qs.append(_lroll(qa, A_HEAD_DIM) if hh != g else qa)
    s_all = _dot(jnp.concatenate(qs, axis=0).astype(bf16), k2, NT) * (A_HEAD_DIM ** -0.5)
    row = lax.broadcasted_iota(jnp.int32, (W, 2 * W), 0)
    col = lax.broadcasted_iota(jnp.int32, (W, 2 * W), 1)
    dist = row + W - col
    mask = (dist >= 0) & (dist < W) & ((col >= W) | (n > 0))
    ps = []
    for j in range(A_Q_HEADS):
        s = jnp.where(mask, s_all[W * j:W * (j + 1)], -jnp.inf)
        sink = jnp.sum(jnp.where(lane8 == j, sinks, 0.0), axis=1, keepdims=True)
        m = jnp.maximum(jnp.max(s, axis=-1, keepdims=True), sink)
        e = jnp.exp(s - m)
        ps.append((e / (jnp.sum(e, axis=-1, keepdims=True) + jnp.exp(sink - m))).astype(bf16))
    o = _dot(jnp.concatenate(ps, axis=0), v2, NN)
    outs = []
    for t in range(4):
        g = t // 2
        ot = jnp.zeros((W, LANE), f32)
        for hh in range(2):
            j = 2 * t + hh
            oj = jnp.where((lane // A_HEAD_DIM) == g, o[W * j:W * (j + 1)], 0.0)
            ot = ot + (_lroll(oj, A_HEAD_DIM) if hh != g else oj)
        outs.append(ot)
    return jnp.concatenate(outs, axis=1)


def _attn_specs():
    W = WINDOW
    prev = lambda n: jnp.maximum(n - 1, 0)
    return [
        pl.BlockSpec((W, 4 * LANE), lambda n: (n, CB_QA // 4)),
        pl.BlockSpec((W, LANE), lambda n: (prev(n), CB_KA)),
        pl.BlockSpec((W, LANE), lambda n: (n, CB_KA)),
        pl.BlockSpec((W, LANE), lambda n: (prev(n), CB_VA)),
        pl.BlockSpec((W, LANE), lambda n: (n, CB_VA)),
        pl.BlockSpec((W, LANE), lambda n: (n, 0)),
        pl.BlockSpec((W, LANE), lambda n: (n, 0)),
        pl.BlockSpec((W, LANE), lambda n: (prev(n), 0)),
        pl.BlockSpec((W, LANE), lambda n: (prev(n), 0)),
        pl.BlockSpec((1, A_Q_HEADS), lambda n: (0, 0)),
    ]


def _attn_fwd(name, proj, cos, sin, sinks):
    T = proj.shape[0]
    W = WINDOW

    def body(*refs):
        o = refs[-1]
        o[...] = _attn_block_fn(pl.program_id(0), *[r[...] for r in refs[:-1]])

    return pl.pallas_call(
        body, name=name, grid=(T // W,),
        in_specs=_attn_specs(),
        out_specs=pl.BlockSpec((W, 4 * LANE), lambda n: (n, 0)),
        out_shape=jax.ShapeDtypeStruct((T, 2 * 4 * LANE), f32),
        compiler_params=_cparams("parallel"),
    )(proj, proj, proj, proj, proj, cos, sin, cos, sin, sinks)


def _attn_bwd(name, proj, cos, sin, sinks, d_oab):
    T = proj.shape[0]
    W = WINDOW
    Q = 4 * LANE

    def body(*refs):
        ins = [r[...] for r in refs[:10]]
        do = refs[10][...]
        d_ref, ds_ref = refs[11:]
        n = pl.program_id(0)
        _, vjp = jax.vjp(functools.partial(_attn_block_fn, n), *ins)
        dq, dkp, dkc, dvp, dvc, _, _, _, _, dsk = vjp(do)

        @pl.when(n == 0)
        def _():
            d_ref[:, Q:] = jnp.zeros((T, 2 * LANE), f32)
            ds_ref[...] = jnp.zeros_like(ds_ref)

        cur = pl.ds(pl.multiple_of(n * W, W), W)
        d_ref[cur, :Q] = dq
        d_ref[cur, Q:Q + LANE] += dkc
        d_ref[cur, Q + LANE:] += dvc
        ds_ref[...] += dsk

        @pl.when(n > 0)
        def _():
            prv = pl.ds(pl.multiple_of((n - 1) * W, W), W)
            d_ref[prv, Q:Q + LANE] += dkp
            d_ref[prv, Q + LANE:] += dvp

    return pl.pallas_call(
        body, name=name, grid=(T // W,),
        in_specs=_attn_specs() + [pl.BlockSpec((W, Q), lambda n: (n, 0))],
        out_specs=[pl.BlockSpec((T, Q + 2 * LANE), lambda n: (0, 0)),
                   pl.BlockSpec((1, A_Q_HEADS), lambda n: (0, 0))],
        out_shape=[jax.ShapeDtypeStruct((T, HYB_PROJ_PAD), f32), jax.ShapeDtypeStruct((1, A_Q_HEADS), f32)],
        compiler_params=_cparams("arbitrary"),
    )(proj, proj, proj, proj, proj, cos, sin, cos, sin, sinks, d_oab)


def _bdot(spec, a, b, precision=None):
    return jnp.einsum(spec, a, b, preferred_element_type=f32, precision=precision)


@jax.custom_vjp
def _tri_inv(a):
    H, C, _ = a.shape
    B = 2 * SUBLANE
    nb = C // B
    r = lax.broadcasted_iota(jnp.int32, (C, C), 0)
    c = lax.broadcasted_iota(jnp.int32, (C, C), 1)
    a4 = jnp.where((r // B) == (c // B), a, 0.0).reshape(H, nb, B, C)
    t4 = jnp.broadcast_to(jnp.where(r == c, 1.0, 0.0).astype(f32), a.shape).reshape(H, nb, B, C)
    for j in range(B - 1):
        col = jnp.concatenate([a4[:, b:b + 1, :, B * b + j:B * b + j + 1] for b in range(nb)], axis=1)
        t4 = t4 - col * t4[:, :, j:j + 1, :]
    x = t4.reshape(H, C, C)
    hi = lax.Precision.HIGHEST
    while B < C:
        m = jnp.where(((r // (2 * B)) == (c // (2 * B))) & ((r // B) > (c // B)), a, 0.0)
        x = x - _bdot("hij,hjk->hik", x, _bdot("hij,hjk->hik", m, x, precision=hi), precision=hi)
        B *= 2
    return x


def _tri_inv_fwd(a):
    t = _tri_inv(a)
    return t, t


def _tri_inv_bwd(t, g):
    C = t.shape[-1]
    r = lax.broadcasted_iota(jnp.int32, (C, C), 0)
    c = lax.broadcasted_iota(jnp.int32, (C, C), 1)
    x = _bdot("hki,hkj->hij", t, g, precision=lax.Precision.HIGHEST)
    y = _bdot("hik,hjk->hij", x, t, precision=lax.Precision.HIGHEST)
    return (jnp.where(r > c, -y, 0.0),)


_tri_inv.defvjp(_tri_inv_fwd, _tri_inv_bwd)


@jax.custom_vjp
def _tri_inv_saved(a, t):
    return t


_tri_inv_saved.defvjp(lambda a, t: (t, t), lambda t, g: (_tri_inv_bwd(t, g)[0], jnp.zeros_like(t)))


def _silu(x):
    return x * jax.nn.sigmoid(x)


def _l2n(x):
    return x * lax.rsqrt(jnp.sum(x * x, axis=-1, keepdims=True) + NORM_EPS)


def _delta_chunk_fn(cq, ck, cv, z, lg, a_log, dt_bias, norm_w, S, t_saved=None, want_t=False):
    C = B_CHUNK
    lane = lax.broadcasted_iota(jnp.int32, (C, LANE), 1)
    pick = lambda l0: jnp.concatenate(
        [jnp.sum(jnp.where(lane == l0 + h, lg, 0.0), axis=1, keepdims=True)[None] for h in range(B_HEADS)], axis=0)
    bl, al = pick(0), pick(B_HEADS)
    q = _l2n(_silu(cq)) * (B_HEAD_DIM ** -0.5)
    k = _l2n(_silu(ck))
    v = _silu(cv)
    beta = jax.nn.sigmoid(bl)
    g = -jnp.exp(a_log) * jax.nn.softplus(al + dt_bias)
    r = lax.broadcasted_iota(jnp.int32, (C, C), 0)
    c = lax.broadcasted_iota(jnp.int32, (C, C), 1)
    eye = r == c
    g_row = jnp.sum(jnp.where(eye, g, 0.0), axis=1, keepdims=True)
    gc = jnp.sum(jnp.where(c <= r, g_row, 0.0), axis=2, keepdims=True)
    gc_row = jnp.sum(jnp.where(eye, gc, 0.0), axis=1, keepdims=True)
    decay_incl = jnp.exp(jnp.where(r >= c, gc - gc_row, -jnp.inf))
    decay_strict = jnp.where(r > c, decay_incl, 0.0)
    kb = k * beta
    vb = v * beta
    kbf = k.astype(bf16)
    a_mat = _bdot("hik,hjk->hij", kb.astype(bf16), kbf) * decay_strict
    t_f32 = _tri_inv(a_mat) if t_saved is None else _tri_inv_saved(a_mat, t_saved)
    t_mat = t_f32.astype(bf16)
    eg = jnp.exp(gc)
    u = _bdot("hij,hjv->hiv", t_mat, vb.astype(bf16))
    w = _bdot("hij,hjk->hik", t_mat, (kb * eg).astype(bf16))
    qk = _bdot("hik,hjk->hij", q.astype(bf16), kbf) * decay_incl
    g_last = jnp.sum(g, axis=1, keepdims=True)
    k_tail = k * jnp.exp(g_last - gc)
    Sb = S.astype(bf16)
    v_new = u - _bdot("hck,hkv->hcv", w.astype(bf16), Sb)
    o = _bdot("hck,hkv->hcv", (q * eg).astype(bf16), Sb) + _bdot("hij,hjv->hiv", qk.astype(bf16), v_new.astype(bf16))
    S_new = S * jnp.exp(g_last) + _bdot("hck,hcv->hkv", k_tail.astype(bf16), v_new.astype(bf16))
    ob = o * lax.rsqrt(jnp.mean(o * o, axis=-1, keepdims=True) + NORM_EPS) * norm_w
    return (ob * _silu(z), S_new) + ((t_f32,) if want_t else ())


def _delta_in_specs(rev, N):
    C = B_CHUNK
    ix = (lambda n: N - 1 - n) if rev else (lambda n: n)
    specs = [pl.BlockSpec((C, 3 * B_HEADS * LANE), lambda n: (ix(n), 0))]
    specs += [pl.BlockSpec((C, LANE), lambda n, h=h: (ix(n), CB_Z + h)) for h in range(B_HEADS)]
    specs += [
        pl.BlockSpec((C, LANE), lambda n: (ix(n), CB_LG)),
        pl.BlockSpec((B_HEADS, 1, 1), lambda n: (0, 0, 0)),
        pl.BlockSpec((B_HEADS, 1, 1), lambda n: (0, 0, 0)),
        pl.BlockSpec((1, LANE), lambda n: (0, 0)),
    ]
    return specs


def _delta_inputs(c_ref, z_refs, lg, al, dt, nw):
    H = B_HEADS
    part = lambda p: jnp.stack([c_ref[:, LANE * (p * H + h):LANE * (p * H + h + 1)] for h in range(H)])
    return (part(0), part(1), part(2), jnp.stack([z[...] for z in z_refs]), lg[...], al[...], dt[...], nw[...])


def _delta_fwd(name, c, proj, a_log, dt_bias, norm_w, o_ab):
    T = c.shape[0]
    C = B_CHUNK
    N = T // C
    Dh = B_HEAD_DIM
    H = B_HEADS

    def body(*refs):
        c_ref, z_refs, (lg, al, dt, nw) = refs[0], refs[1:1 + H], refs[1 + H:5 + H]
        o_ref, s_ref, t_ref, S = refs[6 + H:]

        @pl.when(pl.program_id(0) == 0)
        def _():
            S[...] = jnp.zeros_like(S)

        s0 = S[...]
        s_ref[...] = s0
        ob, s1, t = _delta_chunk_fn(*_delta_inputs(c_ref, z_refs, lg, al, dt, nw), s0, want_t=True)
        for h in range(H):
            o_ref[:, LANE * h:LANE * (h + 1)] = ob[h]
        t_ref[...] = t
        S[...] = s1

    return pl.pallas_call(
        body, name=name, grid=(N,),
        in_specs=_delta_in_specs(False, N) + [pl.BlockSpec(memory_space=pl.ANY)],
        out_specs=[pl.BlockSpec((C, H * LANE), lambda n: (n, 1)),
                   pl.BlockSpec((H, None, Dh, Dh), lambda n: (0, n, 0, 0)),
                   pl.BlockSpec((H, None, C, C), lambda n: (0, n, 0, 0))],
        out_shape=[jax.ShapeDtypeStruct(o_ab.shape, f32), jax.ShapeDtypeStruct((H, N, Dh, Dh), f32),
                   jax.ShapeDtypeStruct((H, N, C, C), f32)],
        input_output_aliases={5 + H: 0},
        scratch_shapes=[pltpu.VMEM((H, Dh, Dh), f32)],
        compiler_params=_cparams("arbitrary"),
    )(c, *([proj] * H), proj, a_log, dt_bias, norm_w, o_ab)


def _delta_bwd(name, c, proj, a_log, dt_bias, norm_w, s_saved, t_saved, d_oab, dproj):
    T = c.shape[0]
    C = B_CHUNK
    N = T // C
    Dh = B_HEAD_DIM
    H = B_HEADS

    def body(*refs):
        c_ref, z_refs, (lg, al, dt, nw) = refs[0], refs[1:1 + H], refs[1 + H:5 + H]
        s_ref, t_ref, do_ref = refs[5 + H:8 + H]
        dc, dtail, dal, ddt, dnw, dS = refs[9 + H:]

        @pl.when(pl.program_id(0) == 0)
        def _():
            dS[...] = jnp.zeros_like(dS)
            dal[...] = jnp.zeros_like(dal)
            ddt[...] = jnp.zeros_like(ddt)
            dnw[...] = jnp.zeros_like(dnw)

        _, vjp = jax.vjp(functools.partial(_delta_chunk_fn, t_saved=t_ref[...]),
                         *_delta_inputs(c_ref, z_refs, lg, al, dt, nw), s_ref[...])
        do = jnp.stack([do_ref[:, LANE * h:LANE * (h + 1)] for h in range(H)])
        g = vjp((do, dS[...]))
        for h in range(H):
            for p in range(3):
                dc[:, LANE * (p * H + h):LANE * (p * H + h + 1)] = g[p][h]
            dtail[:, LANE * h:LANE * (h + 1)] = g[3][h]
        dtail[:, LANE * H:LANE * (H + 1)] = g[4]
        dtail[:, LANE * (H + 1):] = jnp.zeros((C, LANE), f32)
        dal[...] += g[5]
        ddt[...] += g[6]
        dnw[...] += g[7]
        dS[...] = g[8]

    rn = lambda n: N - 1 - n
    return pl.pallas_call(
        body, name=name, grid=(N,),
        in_specs=_delta_in_specs(True, N) + [
            pl.BlockSpec((H, None, Dh, Dh), lambda n: (0, rn(n), 0, 0)),
            pl.BlockSpec((H, None, C, C), lambda n: (0, rn(n), 0, 0)),
            pl.BlockSpec((C, H * LANE), lambda n: (rn(n), 1)),
            pl.BlockSpec(memory_space=pl.ANY),
        ],
        out_specs=[
            pl.BlockSpec((C, 3 * H * LANE), lambda n: (rn(n), 0)),
            pl.BlockSpec((C, (H + 2) * LANE), lambda n: (rn(n), CB_Z // (H + 2))),
            pl.BlockSpec((H, 1, 1), lambda n: (0, 0, 0)),
            pl.BlockSpec((H, 1, 1), lambda n: (0, 0, 0)),
            pl.BlockSpec((1, LANE), lambda n: (0, 0)),
        ],
        out_shape=[jax.ShapeDtypeStruct((T, 3 * H * Dh), f32), jax.ShapeDtypeStruct(dproj.shape, f32),
                   jax.ShapeDtypeStruct((H, 1, 1), f32), jax.ShapeDtypeStruct((H, 1, 1), f32),
                   jax.ShapeDtypeStruct((1, LANE), f32)],
        input_output_aliases={8 + H: 1},
        scratch_shapes=[pltpu.VMEM((H, Dh, Dh), f32)],
        compiler_params=_cparams("arbitrary"),
    )(c, *([proj] * H), proj, a_log, dt_bias, norm_w, s_saved, t_saved, d_oab, dproj)


def _gate_matmuls(xc, wa_ref, wx_ref):
    bw = wa_ref.shape[-1]
    xb = xc.astype(bf16)
    blocks = [xb[:, bw * h:bw * (h + 1)] for h in range(LRU_BLOCKS)]
    return (jnp.concatenate([_dot(blocks[h], wa_ref[h], NN) for h in range(LRU_BLOCKS)], axis=1),
            jnp.concatenate([_dot(blocks[h], wx_ref[h], NN) for h in range(LRU_BLOCKS)], axis=1))


def _gates_fwd(name, xc, w_a, w_x, pars, tm=256):
    T, Wd = xc.shape
    tm = min(tm, T)

    def body(x_ref, wa_ref, wx_ref, ba, bx, lam, a_ref, b_ref):
        x = x_ref[...]
        pr, pi = _gate_matmuls(x, wa_ref, wx_ref)
        a_ref[...], b_ref[...] = _rglru_pre_fn(pr, pi, x, ba[...], bx[...], lam[...])

    row = pl.BlockSpec((tm, Wd), lambda i: (i, 0))
    return pl.pallas_call(
        body, name=name, grid=(T // tm,),
        in_specs=[row, _whole_spec(w_a), _whole_spec(w_x)] + [_whole_spec(p) for p in pars],
        out_specs=[row, row], out_shape=[jax.ShapeDtypeStruct((T, Wd), f32)] * 2,
        compiler_params=_cparams("parallel"),
    )(xc, w_a, w_x, *pars)


def _gates_bwd(name, xc, w_a, w_x, pars, lam_t, h_prev, tm=256):
    T, Wd = xc.shape
    tm = min(tm, T)
    bw = Wd // LRU_BLOCKS

    def body(x_ref, wa_ref, wx_ref, ba, bx, lam, lt_ref, hp_ref, dx_ref, dr_ref, di_ref, dba, dbx, dlam):
        x = x_ref[...]
        pr, pi = _gate_matmuls(x, wa_ref, wx_ref)
        _, vjp = jax.vjp(_rglru_pre_fn, pr, pi, x, ba[...], bx[...], lam[...])
        lt = lt_ref[...]
        dpr, dpi, dxc, g_ba, g_bx, g_lam = vjp((lt * hp_ref[...], lt))
        dprb, dpib = dpr.astype(bf16), dpi.astype(bf16)
        dx_ref[...] = dxc + jnp.concatenate(
            [_dot(dprb[:, bw * h:bw * (h + 1)], wa_ref[h], NT) + _dot(dpib[:, bw * h:bw * (h + 1)], wx_ref[h], NT)
             for h in range(LRU_BLOCKS)], axis=1)
        dr_ref[...] = dprb
        di_ref[...] = dpib

        @pl.when(pl.program_id(0) == 0)
        def _():
            dba[...] = jnp.zeros_like(dba)
            dbx[...] = jnp.zeros_like(dbx)
            dlam[...] = jnp.zeros_like(dlam)

        dba[...] += g_ba
        dbx[...] += g_bx
        dlam[...] += g_lam

    row = pl.BlockSpec((tm, Wd), lambda i: (i, 0))
    vec = pl.BlockSpec((1, Wd), lambda i: (0, 0))
    return pl.pallas_call(
        body, name=name, grid=(T // tm,),
        in_specs=[row, _whole_spec(w_a), _whole_spec(w_x)] + [_whole_spec(p) for p in pars] + [row, row],
        out_specs=[row, row, row, vec, vec, vec],
        out_shape=[jax.ShapeDtypeStruct((T, Wd), f32), jax.ShapeDtypeStruct((T, Wd), bf16),
                   jax.ShapeDtypeStruct((T, Wd), bf16)] + [jax.ShapeDtypeStruct((1, Wd), f32)] * 3,
        compiler_params=_cparams("arbitrary"),
    )(xc, w_a, w_x, *pars, lam_t, h_prev)


def _blockdiag_bwd_dw(name, xc, dpr, dpi, tk=512):
    T, Wd = xc.shape
    bw = Wd // LRU_BLOCKS
    tk = min(tk, T)

    def body(x_ref, dr, di, oa, ox):
        @pl.when(pl.program_id(1) == 0)
        def _():
            oa[...] = jnp.zeros_like(oa)
            ox[...] = jnp.zeros_like(ox)

        xb = x_ref[...].astype(bf16)
        oa[...] += _dot(xb, dr[...].astype(bf16), TN)
        ox[...] += _dot(xb, di[...].astype(bf16), TN)

    xs = pl.BlockSpec((tk, bw), lambda h, k: (k, h))
    ws = pl.BlockSpec((None, bw, bw), lambda h, k: (h, 0, 0))
    return pl.pallas_call(
        body, name=name, grid=(LRU_BLOCKS, T // tk), in_specs=[xs, xs, xs], out_specs=[ws, ws],
        out_shape=[jax.ShapeDtypeStruct((LRU_BLOCKS, bw, bw), f32)] * 2,
        compiler_params=_cparams("parallel", "arbitrary"),
    )(xc, dpr, dpi)


def _scan(name, a, proj, reverse, b=None, h=None, dhg=None, tt=512, cb=512):
    T, Wd = a.shape
    tt, cb = min(tt, T), min(cb, Wd)
    nt = T // tt
    ng = tt // SUBLANE

    def body(a_ref, g_ref, *rest):
        n_in = 2 if reverse else 1
        ins, outs, (carry, carry_a) = rest[:n_in], rest[n_in:-2], rest[-2:]

        @pl.when(pl.program_id(1) == 0)
        def _():
            carry[...] = jnp.zeros_like(carry)
            carry_a[...] = jnp.zeros_like(carry_a)

        row = lax.broadcasted_iota(jnp.int32, (SUBLANE, cb), 0)

        def step(gi, c):
            hp, ap = c
            g = (ng - 1 - gi) if reverse else gi
            rows = pl.ds(pl.multiple_of(g * SUBLANE, SUBLANE), SUBLANE)
            A = a_ref[rows, :]
            gate = g_ref[rows, :]
            a_first = jnp.broadcast_to(A[0:1, :], (SUBLANE, cb))
            if reverse:
                _, vjp = jax.vjp(_rec_gate_fn, ins[0][rows, :], gate)
                B, dgate = vjp((ins[1][rows, :],))
                outs[1][rows, :] = dgate
                A = jnp.where(row == SUBLANE - 1, ap, pltpu.roll(A, SUBLANE - 1, axis=0))
            else:
                B = ins[0][rows, :]
            for s in (1, 2, 4):
                sh = (SUBLANE - s) if reverse else s
                As = pltpu.roll(A, sh, axis=0)
                Bs = pltpu.roll(B, sh, axis=0)
                valid = (row < SUBLANE - s) if reverse else (row >= s)
                B = jnp.where(valid, A * Bs + B, B)
                A = jnp.where(valid, A * As, A)
            hcur = A * hp + B
            outs[0][rows, :] = hcur
            if not reverse:
                outs[1][rows, :] = jnp.where(row == 0, hp, pltpu.roll(hcur, 1, axis=0))
                outs[2][rows, :] = _rec_gate_fn(hcur, gate)[0]
            edge = hcur[0:1, :] if reverse else hcur[SUBLANE - 1:SUBLANE, :]
            return jnp.broadcast_to(edge, (SUBLANE, cb)), a_first

        carry[...], carry_a[...] = lax.fori_loop(0, ng, step, (carry[...], carry_a[...]))

    nc = Wd // cb
    tok = (lambda i: nt - 1 - i) if reverse else (lambda i: i)
    spec = pl.BlockSpec((tt, cb), lambda c, i: (tok(i), c))
    gate_half = pl.BlockSpec((tt, cb), lambda c, i: (tok(i), nc + c))
    if reverse:
        args, out_specs = (a, proj, h, dhg), [spec, gate_half]
        out_shape = [jax.ShapeDtypeStruct((T, Wd), f32), jax.ShapeDtypeStruct((T, 2 * Wd), f32)]
    else:
        args, out_specs = (a, proj, b), [spec] * 3
        out_shape = [jax.ShapeDtypeStruct((T, Wd), f32)] * 3
    return pl.pallas_call(
        body, name=name, grid=(nc, nt), in_specs=[spec, gate_half] + [spec] * (len(args) - 2), out_specs=out_specs,
        out_shape=out_shape,
        scratch_shapes=[pltpu.VMEM((SUBLANE, cb), f32), pltpu.VMEM((SUBLANE, cb), f32)],
        compiler_params=_cparams("parallel", "arbitrary"),
    )(*args)


def _relu2_epilogue(r):
    h = jnp.maximum(r, 0.0)
    return r, h * h


def _drelu2_epilogue(r, a):
    return (r * (2.0 * jnp.maximum(a.astype(f32), 0.0)),)


def _add_epilogue(r, e):
    return (r + e,)


def _merge_cols(name, g, tm=256):
    _, L, R, s = g.shape

    def body(g_ref, o_ref):
        for d in range(N_DEV):
            o_ref[:, s * d:s * (d + 1)] = g_ref[d].astype(bf16)
        o_ref[:, N_DEV * s:] = jnp.zeros((tm, HYB_PROJ_PAD - N_DEV * s), bf16)

    return pl.pallas_call(
        body, name=name, grid=(L, R // tm),
        in_specs=[pl.BlockSpec((N_DEV, None, tm, s), lambda l, i: (0, l, i, 0))],
        out_specs=pl.BlockSpec((None, tm, HYB_PROJ_PAD), lambda l, i: (l, i, 0)),
        out_shape=jax.ShapeDtypeStruct((L, R, HYB_PROJ_PAD), bf16),
        compiler_params=_cparams("parallel", "parallel"),
    )(g)


def _split_cols(name, dw, tm=256):
    R = dw.shape[0]
    s = HYB_PROJ // N_DEV

    def body(g_ref, o_ref):
        for d in range(N_DEV):
            o_ref[d] = g_ref[:, s * d:s * (d + 1)].astype(bf16)

    return pl.pallas_call(
        body, name=name, grid=(R // tm,),
        in_specs=[pl.BlockSpec((tm, HYB_PROJ_PAD), lambda i: (i, 0))],
        out_specs=pl.BlockSpec((N_DEV, tm, s), lambda i: (0, i, 0)),
        out_shape=jax.ShapeDtypeStruct((N_DEV, R, s), bf16),
        compiler_params=_cparams("parallel"),
    )(dw)


def _rows_to_dev(dw):
    nb, r, c = dw.shape
    t = dw.reshape(nb, N_DEV, r // N_DEV, c)
    return jnp.moveaxis(t, 1, 0).reshape(N_DEV, nb * (r // N_DEV), c).astype(bf16)


def _ln_epilogue(r, x, g, b):
    return r, _ln_res_fn(x, r, g, b)[0]


def _hybrid_fwd(tag, x, W, j, cos, sin, ln, before_out):
    proj = _mm(f"{tag}_proj", x, W["hyb_w_in"][j], "nn", b_kind="lead", b_lead=0)
    o_a = _attn_fwd(f"{tag}_attn", proj, cos, sin, W["hyb_sinks"][j][None, :])
    c = _conv_fwd(f"{tag}_conv", proj, CB_CONV, 12, W["hyb_conv_w"][j], None)
    o_ab, s_saved, t_saved = _delta_fwd(f"{tag}_delta", c, proj, W["hyb_a_log"][j].reshape(B_HEADS, 1, 1),
                                        W["hyb_dt_bias"][j].reshape(B_HEADS, 1, 1), W["hyb_norm_w"][j][None, :], o_a)
    before_out(o_ab)
    mix, x1 = _mm(f"{tag}_out", o_ab, W["hyb_w_out"][j], "nn", b_kind="lead", b_lead=0, epilogue=_ln_epilogue,
                  extras=(x,), params=ln, out_dtypes=(f32, f32), tm=512)
    return mix, x1, (proj, c, s_saved, t_saved, o_ab)


def _hybrid_bwd(tag, x, dmix, addend, W, j, cos, sin, saved, G, send_early):
    proj, c, s_saved, t_saved, o_ab = saved
    T = x.shape[0]
    d_oab = _mm(f"{tag}_dout", dmix, W["hyb_w_out"][j], "nt", b_kind="lead", b_lead=0)
    G["hyb_w_out"][j] = _mm(f"{tag}_dwout", o_ab, dmix, "tn", out_dtypes=(bf16,)).reshape(N_DEV, -1, D_MODEL)
    sinks = W["hyb_sinks"][j][None, :] + send_early({("hyb_w_out", j): G["hyb_w_out"][j]})
    dproj, dsinks = _attn_bwd(f"{tag}_dattn", proj, cos, sin, sinks, d_oab)
    a_log = W["hyb_a_log"][j].reshape(B_HEADS, 1, 1)
    dt_bias = W["hyb_dt_bias"][j].reshape(B_HEADS, 1, 1)
    dc, dproj, dal, ddt, dnw = _delta_bwd(f"{tag}_ddelta", c, proj, a_log, dt_bias, W["hyb_norm_w"][j][None, :],
                                          s_saved, t_saved, d_oab, dproj)
    dproj, dconv_w, _ = _conv_bwd(f"{tag}_dconv", dc, proj, CB_CONV, 12, W["hyb_conv_w"][j], dproj, CB_CONV)
    dx = _mm(f"{tag}_dx", dproj, W["hyb_w_in"][j], "nt", b_kind="lead", b_lead=0, epilogue=_add_epilogue,
             extras=(addend,))
    G["hyb_w_in"][j] = _split_cols(f"{tag}_dwin_split", _mm(f"{tag}_dwin", x, dproj, "tn", tn=1536))
    G["hyb_sinks"][j] = dsinks[0]
    G["hyb_conv_w"][j] = dconv_w
    G["hyb_a_log"][j] = dal.reshape(B_HEADS)
    G["hyb_dt_bias"][j] = ddt.reshape(B_HEADS)
    G["hyb_norm_w"][j] = dnw[0]
    return dx


def _rec_fwd(tag, x, W, j, ln, before_out):
    Wd = D_MODEL
    proj = _mm(f"{tag}_proj", x, W["rec_w_in"][j], "nn", b_kind="devcol", b_lead=0)
    xc = _conv_fwd(f"{tag}_conv", proj, 0, Wd // LANE, W["rec_conv_w"][j], W["rec_conv_b"][j][None, :])
    pars = [W["rec_b_a"][j][None, :], W["rec_b_x"][j][None, :], W["rec_lambda"][j][None, :]]
    a, b = _gates_fwd(f"{tag}_gates", xc, W["rec_w_a"][j][0], W["rec_w_x"][j][0], pars)
    h, h_prev, hg = _scan(f"{tag}_scan", a, proj, False, b=b)
    before_out(hg)
    mix, x1 = _mm(f"{tag}_out", hg, W["rec_w_out"][j], "nn", b_kind="lead", b_lead=0, epilogue=_ln_epilogue,
                  extras=(x,), params=ln, out_dtypes=(f32, f32), tm=512)
    return mix, x1, (proj, xc, a, h, h_prev, hg)


def _rec_bwd(tag, x, dmix, addend, W, j, saved, G, send_early):
    proj, xc, a, h, h_prev, hg = saved
    Wd = D_MODEL
    dhg = _mm(f"{tag}_dout", dmix, W["rec_w_out"][j], "nt", b_kind="lead", b_lead=0)
    G["rec_w_out"][j] = _mm(f"{tag}_dwout", hg, dmix, "tn", out_dtypes=(bf16,)).reshape(N_DEV, -1, D_MODEL)
    sent = send_early({("rec_w_out", j): G["rec_w_out"][j]})
    lam_t, dproj = _scan(f"{tag}_dscan", a, proj, True, h=h, dhg=dhg)
    pars = [W["rec_b_a"][j][None, :] + sent, W["rec_b_x"][j][None, :], W["rec_lambda"][j][None, :]]
    dxc, dpr, dpi, db_a, db_x, dlam = _gates_bwd(f"{tag}_dgates", xc, W["rec_w_a"][j][0], W["rec_w_x"][j][0], pars,
                                                 lam_t, h_prev)
    dwa, dwx = _blockdiag_bwd_dw(f"{tag}_dgates_dw", xc, dpr, dpi)
    G["rec_w_a"][j], G["rec_w_x"][j] = _rows_to_dev(dwa), _rows_to_dev(dwx)
    dproj, dconv_w, dconv_b = _conv_bwd(f"{tag}_dconv", dxc, proj, 0, Wd // LANE, W["rec_conv_w"][j], dproj, 0)
    dx = _mm(f"{tag}_dx", dproj, W["rec_w_in"][j], "nt", b_kind="devcol", b_lead=0, epilogue=_add_epilogue,
             extras=(addend,))
    G["rec_w_in"][j] = _mm(f"{tag}_dwin", x, dproj, "tn", o_kind="devcol", out_dtypes=(bf16,), tn=2048)
    G["rec_conv_w"][j] = dconv_w
    G["rec_conv_b"][j] = dconv_b
    G["rec_b_a"][j] = db_a[0]
    G["rec_b_x"][j] = db_x[0]
    G["rec_lambda"][j] = dlam[0]
    return dx


def _local_step(x, target, W, load_layer, grads_ready):
    T = x.shape[0]
    cos, sin = _rope_tables(T)
    saved = []
    for layer in range(DEPTH):
        j = layer // 2
        tag = f"L{layer}"
        load_layer(layer, 0, x)
        ln1 = (W["ln1_g"][layer][None, :], W["ln1_b"][layer][None, :])
        before_out = functools.partial(load_layer, layer, 1)
        if layer % 2 == 0:
            mix, x1, sv = _hybrid_fwd(tag, x, W, j, cos, sin, ln1, before_out)
        else:
            mix, x1, sv = _rec_fwd(tag, x, W, j, ln1, before_out)
        load_layer(layer, 2, x1)
        a, h2 = _mm(f"{tag}_mlp1", x1, W["mlp_w1"][layer], "nn", b_kind="devcol", b_lead=0, epilogue=_relu2_epilogue,
                    out_dtypes=(bf16, bf16), tm=2048)
        ln2 =(W["ln2_g"][layer][None, :], W["ln2_b"][layer][None, :])
        y, x2 = _mm(f"{tag}_mlp2", h2, W["mlp_w2"][layer], "nn", b_kind="devrow", b_lead=0, epilogue=_ln_epilogue,
                    extras=(x1,), params=ln2, out_dtypes=(f32, f32))
        saved.append((x, sv, mix, x1, a, h2, y))
        x = x2
    loss, dx = _loss_head(x, target)

    G = {k: [None] * (DEPTH if k.startswith(("ln", "mlp")) else DEPTH // 2) for k in (
        "hyb_w_in", "hyb_sinks", "hyb_conv_w", "hyb_a_log", "hyb_dt_bias", "hyb_norm_w", "hyb_w_out",
        "rec_w_in", "rec_conv_w", "rec_conv_b", "rec_w_a", "rec_b_a", "rec_w_x", "rec_b_x", "rec_lambda", "rec_w_out",
        "ln1_g", "ln1_b", "mlp_w1", "mlp_w2", "ln2_g", "ln2_b")}
    order = jnp.zeros((1, 1), f32)
    for layer in reversed(range(DEPTH)):
        j = layer // 2
        tag = f"L{layer}"
        x0, sv, mix, x1, a, h2, y = saved[layer]
        ln2 = [W["ln2_g"][layer][None, :] + order, W["ln2_b"][layer][None, :]]
        (dx1_a, dy), (dg2, db2) = _tl_bwd(f"{tag}_dln2", _ln_res_fn, [(x1, 0, D_MODEL), (y, 0, D_MODEL)], ln2,
                                          [(dx, 0, D_MODEL)])
        G["ln2_g"][layer], G["ln2_b"][layer] = dg2[0], db2[0]
        da = _mm(f"{tag}_dmlp2", dy, W["mlp_w2"][layer], "nt", b_kind="devrow", b_lead=0, epilogue=_drelu2_epilogue,
                 extras=(a,), out_dtypes=(bf16,), tm=2048)
        G["mlp_w2"][layer] = _mm(f"{tag}_dw2", h2, dy, "tn", out_dtypes=(bf16,), tm=2048).reshape(N_DEV, -1, D_MODEL)
        dx1 = _mm(f"{tag}_dmlp1", da, W["mlp_w1"][layer], "nt", b_kind="devcol", b_lead=0, epilogue=_add_epilogue,
                  extras=(dx1_a,))
        G["mlp_w1"][layer] = _mm(f"{tag}_dw1", x1, da, "tn", o_kind="devcol", out_dtypes=(bf16,), tn=2048)
        ln1 = [W["ln1_g"][layer][None, :], W["ln1_b"][layer][None, :]]
        (dx0_a, dmix), (dg1, db1) = _tl_bwd(f"{tag}_dln1", _ln_res_fn, [(x0, 0, D_MODEL), (mix, 0, D_MODEL)], ln1,
                                            [(dx1, 0, D_MODEL)])
        G["ln1_g"][layer], G["ln1_b"][layer] = dg1[0], db1[0]
        early = functools.partial(grads_ready, f"l{layer}_early",
                                  {(k, layer): G[k][layer] for k in ("mlp_w1", "mlp_w2")})
        if layer % 2 == 0:
            dx = _hybrid_bwd(tag, x0, dmix, dx0_a, W, j, cos, sin, sv, G, early)
        else:
            dx = _rec_bwd(tag, x0, dmix, dx0_a, W, j, sv, G, early)
        order = grads_ready(f"l{layer}_late", {}, {(k, i): G[k][i] for k, i in _layer_weights(layer)[:-2]
                                                  if not k.endswith("w_out")})
    big = {k for k, _ in BIG}
    return loss, dx, {k: jnp.stack(v) for k, v in G.items() if k not in big}


def _layer_weights(layer):
    j = layer // 2
    mixer = ["hyb_w_in", "hyb_w_out"] if layer % 2 == 0 else ["rec_w_in", "rec_w_out", "rec_w_a", "rec_w_x"]
    return [(k, j) for k in mixer] + [("mlp_w1", layer), ("mlp_w2", layer)]


def _my_coords():
    return lax.axis_index("x"), lax.axis_index("y"), lax.axis_index("c")


def _all_gather(name, arrays):
    na = len(arrays)

    def body(*refs):
        x_refs, out_refs = refs[:na], refs[na:2 * na]
        send_sems, recv_sems, local_sems = refs[2 * na:]
        x, y, c = _my_coords()
        me, sibling = (x, y, c), (x, y, 1 - c)
        chips = [(1 - x, y), (x, 1 - y), (1 - x, 1 - y)]

        def blk(a, px, py, pc):
            return out_refs[a].at[4 * px + 2 * py + pc]

        def copy(a, k, block, to, src=None):
            return pltpu.make_async_remote_copy(
                src_ref=blk(a, *block) if src is None else src, dst_ref=blk(a, *block),
                send_sem=send_sems.at[a, k], recv_sem=recv_sems.at[a, k],
                device_id=to, device_id_type=pl.DeviceIdType.MESH)

        mine = [pltpu.make_async_copy(x_refs[a], blk(a, *me), local_sems.at[a]) for a in range(na)]
        for cp in mine:
            cp.start()
        first = []
        for a in range(na):
            first.append(copy(a, 0, me, sibling, src=x_refs[a]))
            first += [copy(a, 1 + j, me, (*chip, c), src=x_refs[a]) for j, chip in enumerate(chips)]
        for cp in first:
            cp.start()
        passed = []
        for a in range(na):
            for j, chip in enumerate(chips):
                copy(a, 1 + j, (*chip, c), me).wait_recv()
                passed.append(copy(a, 4 + j, (*chip, c), sibling))
                passed[-1].start()
        for a in range(na):
            copy(a, 0, sibling, me).wait_recv()
            for j, chip in enumerate(chips):
                copy(a, 4 + j, (*chip, 1 - c), me).wait_recv()
        for cp in first + passed:
            cp.wait_send()
        for cp in mine:
            cp.wait()

    return pl.pallas_call(
        body, name=name,
        out_shape=[jax.ShapeDtypeStruct((N_DEV,) + a.shape, a.dtype) for a in arrays],
        in_specs=[pl.BlockSpec(memory_space=pl.ANY)] * na,
        out_specs=[pl.BlockSpec(memory_space=pl.ANY)] * na,
        scratch_shapes=[pltpu.SemaphoreType.DMA((na, 7)), pltpu.SemaphoreType.DMA((na, 7)),
                        pltpu.SemaphoreType.DMA((na,))],
    )(*arrays)


_HBM = pl.BlockSpec(memory_space=pltpu.HBM)
_SEM = pl.BlockSpec(memory_space=pltpu.SEMAPHORE)


def _flip(k, x, y, c):
    return ((1 - x) if k & 4 else x, (1 - y) if k & 2 else y, (1 - c) if k & 1 else c)


_PEERS = {"gather": (1, 2, 4, 6), "scatter": (1, 2, 3, 4, 5, 6, 7)}


def _push_copies(kind, x_refs, land_refs, send_sems, recv_sems, local_sems):
    x, y, c = _my_coords()
    me = 4 * x + 2 * y + c
    peers = _PEERS[kind]
    remote, local = [], []
    for a in range(len(x_refs)):
        local.append(pltpu.make_async_copy(x_refs[a] if kind == "gather" else x_refs[a].at[me], land_refs[a].at[me],
                                           local_sems.at[a]))
        for n, k in enumerate(peers):
            px, py, pc = _flip(k, x, y, c)
            remote.append(pltpu.make_async_remote_copy(
                src_ref=x_refs[a] if kind == "gather" else x_refs[a].at[4 * px + 2 * py + pc],
                dst_ref=land_refs[a].at[me],
                send_sem=send_sems.at[a * len(peers) + n], recv_sem=recv_sems.at[a * len(peers) + n],
                device_id=(px, py, pc), device_id_type=pl.DeviceIdType.MESH))
    return remote, local


def _pass_to_sibling(name, lands):
    na = len(lands)
    chips = (2, 4, 6)

    def body(*refs):
        out_refs, send_sems, recv_sems = refs[na:2 * na], refs[2 * na], refs[2 * na + 1]
        x, y, c = _my_coords()
        cps = []
        for a in range(na):
            for n, k in enumerate(chips):
                px, py, _ = _flip(k, x, y, c)
                cps.append(pltpu.make_async_remote_copy(
                    src_ref=out_refs[a].at[4 * px + 2 * py + c], dst_ref=out_refs[a].at[4 * px + 2 * py + c],
                    send_sem=send_sems.at[a * 3 + n], recv_sem=recv_sems.at[a * 3 + n],
                    device_id=(x, y, 1 - c), device_id_type=pl.DeviceIdType.MESH))
        for cp in cps:
            cp.start()
        for a in range(na):
            for n, k in enumerate(chips):
                px, py, _ = _flip(k, x, y, c)
                blk = out_refs[a].at[4 * px + 2 * py + (1 - c)]
                pltpu.make_async_remote_copy(src_ref=blk, dst_ref=blk, send_sem=send_sems.at[a * 3 + n],
                                             recv_sem=recv_sems.at[a * 3 + n], device_id=(x, y, 1 - c),
                                             device_id_type=pl.DeviceIdType.MESH).wait_recv()
        for cp in cps:
            cp.wait_send()

    return pl.pallas_call(
        body, name=name,
        out_shape=[jax.ShapeDtypeStruct(l.shape, l.dtype) for l in lands],
        in_specs=[pl.BlockSpec(memory_space=pl.ANY)] * na,
        out_specs=[pl.BlockSpec(memory_space=pl.ANY)] * na,
        input_output_aliases={a: a for a in range(na)},
        scratch_shapes=[pltpu.SemaphoreType.DMA((3 * na,)), pltpu.SemaphoreType.DMA((3 * na,))],
    )(*lands)


_SIDE_EFFECT = pltpu.CompilerParams(has_side_effects=pltpu.SideEffectType.DATAFLOW_SIDE_EFFECTING)


def _push_start(name, kind, srcs, lands):
    na = len(srcs)

    def body(*refs):
        remote, local = _push_copies(kind, refs[:na], refs[na:2 * na], *refs[2 * na:2 * na + 3])
        for cp in remote + local:
            cp.start()
        token = refs[-1]
        token[...] = jnp.zeros_like(token)

    arrays = list(srcs) + list(lands)
    n_remote = na * len(_PEERS[kind])
    res = pl.pallas_call(
        body, name=name,
        out_shape=(pltpu.SemaphoreType.DMA((n_remote,)), pltpu.SemaphoreType.DMA((n_remote,)),
                   pltpu.SemaphoreType.DMA((na,)), *[pltpu.HBM(t.shape, t.dtype) for t in arrays],
                   jax.ShapeDtypeStruct((SUBLANE, LANE), f32)),
        in_specs=[_HBM] * (2 * na),
        out_specs=(_SEM, _SEM, _SEM, *[_HBM] * (2 * na), pl.BlockSpec(memory_space=pltpu.VMEM)),
        input_output_aliases={i: 3 + i for i in range(2 * na)},
        compiler_params=_SIDE_EFFECT,
    )(*[pltpu.with_memory_space_constraint(t, pltpu.HBM) for t in arrays])
    return list(res[:3]), res[3:3 + na], res[3 + na:3 + 2 * na], res[-1][:1, :1]


def _push_wait(name, kind, sems, srcs, lands, after):
    na = len(srcs)

    def body(*refs):
        remote, local = _push_copies(kind, refs[:na], refs[na:2 * na], *refs[2 * na:2 * na + 3])
        for cp in remote:
            cp.wait_send()
            cp.wait_recv()
        for cp in local:
            cp.wait()

    arrays = list(srcs) + list(lands)
    res = pl.pallas_call(
        body, name=name,
        out_shape=tuple(pltpu.HBM(t.shape, t.dtype) for t in arrays),
        in_specs=[_HBM] * (2 * na) + [_SEM] * 3 + [pl.BlockSpec(memory_space=pl.ANY)],
        out_specs=tuple([_HBM] * (2 * na)),
        input_output_aliases={i: i for i in range(2 * na)},
        compiler_params=_SIDE_EFFECT,
    )(*arrays, *sems, after)
    return res[na:]


def _sum_blocks(name, land):
    _, R, n = land.shape
    tr = R

    def body(l_ref, o_ref):
        acc = l_ref[0].astype(f32)
        for s in range(1, N_DEV):
            acc = acc + l_ref[s].astype(f32)
        o_ref[...] = acc

    return pl.pallas_call(
        body, name=name, grid=(R // tr,),
        in_specs=[pl.BlockSpec((N_DEV, tr, n), lambda i: (0, i, 0))],
        out_specs=pl.BlockSpec((tr, n), lambda i: (i, 0)),
        out_shape=jax.ShapeDtypeStruct((R, n), f32),
        compiler_params=_cparams("parallel"),
    )(land)


def _adamw(name, w, g, m, v):
    shape = w.shape
    last = shape[-1]
    rows = math.prod(shape[:-1])
    tm = 256 if rows % 256 == 0 and rows > 256 else rows
    w2, g2, m2, v2 = (t.reshape(rows, last) for t in (w, g, m, v))

    def body(w_ref, g_ref, m_ref, v_ref, d_ref, mo_ref, vo_ref):
        gg = g_ref[...]
        mn = ADAM_B1 * m_ref[...] + (1.0 - ADAM_B1) * gg
        vn = ADAM_B2 * v_ref[...] + (1.0 - ADAM_B2) * jnp.square(gg)
        m_hat = mn / (1.0 - ADAM_B1 ** ADAM_STEP)
        v_hat = vn / (1.0 - ADAM_B2 ** ADAM_STEP)
        d_ref[...] = -ADAM_LR * (m_hat / (jnp.sqrt(v_hat) + ADAM_EPS) + ADAM_WD * w_ref[...])
        mo_ref[...] = mn
        vo_ref[...] = vn

    spec = pl.BlockSpec((tm, last), lambda i: (i, 0))
    d, mn, vn = pl.pallas_call(
        body, name=name, grid=(rows // tm,), in_specs=[spec] * 4, out_specs=[spec] * 3,
        out_shape=[jax.ShapeDtypeStruct((rows, last), f32)] * 3,
        compiler_params=_cparams("parallel"),
    )(w2, g2, m2, v2)
    return d.reshape(shape), mn.reshape(shape), vn.reshape(shape)


def _adamw_land(name, lands, w, m, v, tm=256):
    L = len(lands)
    _, R, C = lands[0].shape
    tm = min(tm, R)

    def body(*refs):
        l_refs, (w_ref, m_ref, v_ref, g_ref, d_ref, mo_ref, vo_ref) = refs[:L], refs[L:]
        for k in range(L):
            @pl.when(pl.program_id(0) == k)
            def _(k=k):
                gg = l_refs[k][0].astype(f32)
                for s in range(1, N_DEV):
                    gg = gg + l_refs[k][s].astype(f32)
                g_ref[...] = gg
                mn = ADAM_B1 * m_ref[...] + (1.0 - ADAM_B1) * gg
                vn = ADAM_B2 * v_ref[...] + (1.0 - ADAM_B2) * jnp.square(gg)
                m_hat = mn / (1.0 - ADAM_B1 ** ADAM_STEP)
                v_hat = vn / (1.0 - ADAM_B2 ** ADAM_STEP)
                d_ref[...] = -ADAM_LR * (m_hat / (jnp.sqrt(v_hat) + ADAM_EPS) + ADAM_WD * w_ref[...])
                mo_ref[...] = mn
                vo_ref[...] = vn

    land_specs = [pl.BlockSpec((N_DEV, tm, C), lambda l, i, k=k: (0, jnp.where(l == k, i, 0), 0)) for k in range(L)]
    spec = pl.BlockSpec((None, tm, C), lambda l, i: (l, i, 0))
    return pl.pallas_call(
        body, name=name, grid=(L, R // tm),
        in_specs=land_specs + [spec] * 3,
        out_specs=[spec] * 4,
        out_shape=[jax.ShapeDtypeStruct((L, R, C), f32)] * 4,
        compiler_params=_cparams("arbitrary", "arbitrary"),
    )(*lands, w, m, v)


BIG = [("hyb_w_in", 2), ("hyb_w_out", 1), ("rec_w_in", 2), ("rec_w_out", 1), ("rec_w_a", 2), ("rec_w_x", 2),
       ("mlp_w1", 2), ("mlp_w2", 1)]
SMALL = [("hyb_conv_w", 2), ("rec_conv_w", 2), ("rec_conv_b", 1), ("rec_b_a", 1), ("rec_b_x", 1), ("rec_lambda", 1)]
REPL = ["hyb_sinks", "hyb_a_log", "hyb_dt_bias", "hyb_norm_w", "ln1_g", "ln1_b", "ln2_g", "ln2_b"]
WEIGHTS = ["hyb_w_in", "hyb_sinks", "hyb_conv_w", "hyb_a_log", "hyb_dt_bias", "hyb_norm_w", "hyb_w_out", "rec_w_in",
           "rec_conv_w", "rec_conv_b", "rec_w_a", "rec_b_a", "rec_w_x", "rec_b_x", "rec_lambda", "rec_w_out",
           "ln1_g", "ln1_b", "mlp_w1", "mlp_w2", "ln2_g", "ln2_b"]


def _pack_rows(parts, dtype, row_mult):
    lead = parts[0].shape[:-1]
    flat = jnp.concatenate([p.astype(dtype) for p in parts], axis=-1)
    n = flat.shape[-1]
    unit = row_mult * LANE
    pad = (-n) % unit
    if pad:
        flat = jnp.concatenate([flat, jnp.zeros(lead + (pad,), dtype)], axis=-1)
    return flat.reshape(lead + ((n + pad) // LANE, LANE))


def _gather_full(gathered, shard_shapes, table):
    flat = gathered.reshape(N_DEV, -1)
    out, off = {}, 0
    for name, ax in table:
        shp = shard_shapes[name]
        n = math.prod(shp)
        arr = flat[:, off:off + n].reshape((N_DEV,) + shp)
        off += n
        arr = jnp.moveaxis(arr, 0, ax)
        out[name] = arr.reshape(shp[:ax] + (N_DEV * shp[ax],) + shp[ax + 1:])
    return out


def _matmul_layouts(tag, gw):
    out = {}
    bw = D_MODEL // LRU_BLOCKS
    for k, g in gw.items():
        L = g.shape[1]
        if k == "hyb_w_in":
            out[k] = _merge_cols(f"{tag}_w_in_merge", g)
        elif k in ("hyb_w_out", "rec_w_out"):
            out[k] = jnp.swapaxes(g, 0, 1).reshape(L, D_MODEL, D_MODEL)
        elif k in ("rec_w_a", "rec_w_x"):
            out[k] = jnp.moveaxis(g, 0, 2).reshape(L, LRU_BLOCKS, bw, bw)
        else:
            out[k] = g
    return out


def kernel(x, hyb_w_in, hyb_sinks, hyb_conv_w, hyb_a_log, hyb_dt_bias, hyb_norm_w, hyb_w_out, rec_w_in, rec_conv_w, rec_conv_b, rec_w_a, rec_b_a, rec_w_x, rec_b_x, rec_lambda, rec_w_out, ln1_g, ln1_b, mlp_w1, mlp_w2, ln2_g, ln2_b, loss_target, m_hyb_w_in, m_hyb_sinks, m_hyb_conv_w, m_hyb_a_log, m_hyb_dt_bias, m_hyb_norm_w, m_hyb_w_out, m_rec_w_in, m_rec_conv_w, m_rec_conv_b, m_rec_w_a, m_rec_b_a, m_rec_w_x, m_rec_b_x, m_rec_lambda, m_rec_w_out, m_ln1_g, m_ln1_b, m_mlp_w1, m_mlp_w2, m_ln2_g, m_ln2_b, v_hyb_w_in, v_hyb_sinks, v_hyb_conv_w, v_hyb_a_log, v_hyb_dt_bias, v_hyb_norm_w, v_hyb_w_out, v_rec_w_in, v_rec_conv_w, v_rec_conv_b, v_rec_w_a, v_rec_b_a, v_rec_w_x, v_rec_b_x, v_rec_lambda, v_rec_w_out, v_ln1_g, v_ln1_b, v_mlp_w1, v_mlp_w2, v_ln2_g, v_ln2_b):
    args = locals()
    w = {k: args[k] for k in WEIGHTS}
    m = {k: args["m_" + k] for k in WEIGHTS}
    v = {k: args["v_" + k] for k in WEIGHTS}
    shard_shapes = {k: tuple(t.shape) for k, t in w.items()}
    xi, yi, ci = _my_coords()
    me = 4 * xi + 2 * yi + ci

    in_flight = {}

    def install(tag, names, got):
        for (k, i), arr in zip(names, _matmul_layouts(tag, {k: g for (k, _), g in zip(names, got)}).values()):
            W[k][i] = arr

    def start_gather(tag, names):
        srcs = [w[k][i:i + 1].astype(bf16) for k, i in names]
        *pending, zero = _push_start(f"gather_{tag}_start", "gather", srcs,
                                     [lax.empty((N_DEV,) + s.shape, bf16) for s in srcs])
        in_flight[tag] = (names, pending)
        return zero

    def finish_gather(tag, after):
        names, pending = in_flight.pop(tag)
        half = _push_wait(f"gather_{tag}_wait", "gather", *pending, after)
        install(tag, names, _pass_to_sibling(f"gather_{tag}_pass", half))

    def started(k, zero):
        W[k] = W[k] + zero

    def mixer_w(layer):
        return _layer_weights(layer)[:-2]

    def mlp_w(layer):
        return _layer_weights(layer)[-2:]

    gathered0 = _all_gather("gather_first", [w[k][i:i + 1].astype(bf16) for k, i in mixer_w(0)]
                            + [_pack_rows([w[k].reshape(-1) for k, _ in SMALL], f32, SUBLANE)])
    W = _gather_full(gathered0[-1], shard_shapes, SMALL)
    W.update({k: w[k] for k in REPL})
    W.update({k: {} for k, _ in BIG})
    install("l0a", mixer_w(0), gathered0[:-1])
    started("hyb_sinks", start_gather("l0b", mlp_w(0)) + start_gather("l1a", mixer_w(1)))

    def load_layer(layer, stage, after):
        if stage == 0:
            if layer > 0:
                finish_gather(f"l{layer}a", after)
            if 0 < layer < DEPTH - 1:
                started("hyb_sinks" if layer % 2 == 0 else "rec_conv_b",
                        start_gather(f"l{layer + 1}a", mixer_w(layer + 1)))
        if stage == 2:
            finish_gather(f"l{layer}b", after)
            if layer < DEPTH - 1:
                started("ln2_g", start_gather(f"l{layer + 1}b", mlp_w(layer + 1)))

    grads_in_flight = {}

    def grads_ready(tag, a, b):
        g = {**a, **b}
        srcs = list(g.values())
        *pending, zero = _push_start(f"scatter_{tag}_start", "scatter", srcs, [lax.empty(s.shape, bf16) for s in srcs])
        grads_in_flight[tag] = (list(g.keys()), pending)
        return zero

    loss_local, grad_x, G = _local_step(x[0], loss_target[0], W, load_layer, grads_ready)
    loss = lax.psum(loss_local, MESH_AXES)

    landed = {}

    def land(tag, after):
        keys, pending = grads_in_flight[tag]
        landed.update(zip(keys, _push_wait(f"scatter_{tag}_wait", "scatter", *pending, after)))

    tags = list(grads_in_flight)
    for tag in tags[:-1]:
        land(tag, grad_x)
    rest = _pack_rows([G[k].reshape(-1) for k, _ in SMALL] + [G[k].reshape(-1) for k in REPL], f32, SUBLANE)
    g_rest = _sum_blocks("sum_rest", _all_gather("gather_rest", [rest])[0]).reshape(-1)

    grads, delta, new_m, new_v = {}, {}, {}, {}

    def adamw_big(k):
        shp = shard_shapes[k]
        s3 = (shp[0], math.prod(shp[1:-1]), shp[-1])
        lands = [landed[(k, i)].reshape((N_DEV,) + s3[1:]) for i in range(shp[0])]
        res = _adamw_land("adamw_" + k, lands, w[k].reshape(s3), m[k].reshape(s3), v[k].reshape(s3))
        grads[k], delta[k], new_m[k], new_v[k] = (r.reshape(shp) for r in res)

    late = {k for k, _ in grads_in_flight[tags[-1]][0]}
    for k in [k for k, _ in BIG if k not in late]:
        adamw_big(k)
        done = new_v[k]
    land(tags[-1], done)
    for k in [k for k, _ in BIG if k in late]:
        adamw_big(k)
    off = 0
    for k, ax in SMALL:
        full_shape = G[k].shape
        n = math.prod(full_shape)
        full = g_rest[off:off + n].reshape(full_shape)
        off += n
        s = shard_shapes[k][ax]
        grads[k] = lax.dynamic_slice_in_dim(full, me * s, s, axis=ax)
    for k in REPL:
        n = math.prod(shard_shapes[k])
        grads[k] = g_rest[off:off + n].reshape(shard_shapes[k])
        off += n

    for k in [k for k, _ in SMALL] + REPL:
        delta[k], new_m[k], new_v[k] = _adamw("adamw_" + k, w[k], grads[k], m[k], v[k])

    return (loss, grad_x[None], *[grads[k] for k in WEIGHTS], *[delta[k] for k in WEIGHTS],
            *[new_m[k] for k in WEIGHTS], *[new_v[k] for k in WEIGHTS])
```

```python
import functools
import math

import jax
import jax.numpy as jnp
from jax import lax
from jax.experimental import pallas as pl
from jax.experimental.pallas import tpu as pltpu

f32 = jnp.float32
bf16 = jnp.bfloat16

N_DEV = 8
D_MODEL = 1024
DEPTH = 4
A_HEAD_DIM = 64
A_Q_HEADS = 8
WINDOW = 128
ROPE_THETA = 10000.0
B_HEADS = 4
B_HEAD_DIM = 128
B_CHUNK = 64
LRU_BLOCKS = 4
LRU_C = 8.0
D_FF = 4 * D_MODEL
HYB_PROJ = 2824
HYB_PROJ_PAD = 3072
DN_ALPHA = (2 * DEPTH) ** 0.25
LN_EPS = 1e-5
NORM_EPS = 1e-6
ADAM_LR = 0.001
ADAM_B1 = 0.9
ADAM_B2 = 0.999
ADAM_EPS = 1e-08
ADAM_WD = 0.01
ADAM_STEP = 10

LANE = 128
SUBLANE = 8
VMEM_LIMIT = 48 * 1024 * 1024

CB_QA, CB_KA, CB_VA, CB_CONV, CB_Z, CB_LG = 0, 4, 5, 6, 18, 22

MESH_AXES = ("x", "y", "c")


def _cparams(*sem):
    return pltpu.CompilerParams(dimension_semantics=sem, vmem_limit_bytes=VMEM_LIMIT)


def _dot(a, b, dims, precision=None):
    return lax.dot_general(a, b, (dims, ((), ())), preferred_element_type=f32, precision=precision)


NN = ((1,), (0,))
NT = ((1,), (1,))
TN = ((0,), (0,))


def _mat_spec(arr, kind, lead, br, bc, rb, cb):
    if kind == "plain":
        return pl.BlockSpec((br, bc), lambda i, j, k: (rb(i, j, k), cb(i, j, k)))
    if kind == "lead":
        return pl.BlockSpec((None, br, bc), lambda i, j, k: (lead, rb(i, j, k), cb(i, j, k)))
    if kind == "devcol":
        assert bc == arr.shape[-1]
        return pl.BlockSpec((None, None, br, bc), lambda i, j, k: (cb(i, j, k), lead, rb(i, j, k), 0))
    assert kind == "devrow" and br == arr.shape[-2]
    return pl.BlockSpec((None, None, br, bc), lambda i, j, k: (rb(i, j, k), lead, 0, cb(i, j, k)))


def _mm(name, a, b, mode, *, b_kind="plain", b_lead=0, o_kind="plain", epilogue=None, extras=(), params=(),
        out_dtypes=(f32,), tm=1024, tn=1024, tk=None):
    if tk is None:
        tk = 512 if mode == "tn" else 1024
    if b_kind in ("plain", "lead"):
        b_rows, b_cols = b.shape[-2:]
    elif b_kind == "devcol":
        b_rows, b_cols = b.shape[-2], N_DEV * b.shape[-1]
    else:
        b_rows, b_cols = N_DEV * b.shape[-2], b.shape[-1]
    if mode == "nn":
        (M, K), (K2, N) = a.shape, (b_rows, b_cols)
    elif mode == "nt":
        (M, K), (N, K2) = a.shape, (b_rows, b_cols)
    else:
        (K, M), (K2, N) = a.shape, (b_rows, b_cols)
    assert K == K2, (name, a.shape, b.shape, mode)
    tm, tn, tk = min(tm, M), min(tn, N), min(tk, K)
    cols_are_n = mode != "nt"
    if b_kind == "devcol":
        tn, tk = (b.shape[-1], tk) if cols_are_n else (tn, b.shape[-1])
    if b_kind == "devrow":
        tn, tk = (tn, b.shape[-2]) if cols_are_n else (b.shape[-2], tk)
    shard = N // N_DEV
    if o_kind == "devcol":
        tn = max(shard, tn // shard * shard)
    assert M % tm == 0 and N % tn == 0 and K % tk == 0, (name, M, N, K, tm, tn, tk)
    nk = K // tk
    dims = {"nn": NN, "nt": NT, "tn": TN}[mode]
    n_ex, n_out = len(extras) + len(params), len(out_dtypes)

    def body(*refs):
        a_ref, b_ref = refs[:2]
        ex = refs[2:2 + n_ex]
        outs = refs[2 + n_ex:2 + n_ex + n_out]
        acc = refs[-1]
        k = pl.program_id(2)

        @pl.when(k == 0)
        def _():
            acc[...] = jnp.zeros_like(acc)

        acc[...] += _dot(a_ref[...].astype(bf16), b_ref[...].astype(bf16), dims)

        @pl.when(k == nk - 1)
        def _():
            r = acc[...]
            res = epilogue(r, *[e[...] for e in ex]) if epilogue is not None else (r,)
            for o, v in zip(outs, res):
                if o_kind == "plain":
                    o[...] = v.astype(o.dtype)
                else:
                    for q in range(tn // shard):
                        o[q] = v[:, q * shard:(q + 1) * shard].astype(o.dtype)

    if mode == "tn":
        a_spec = pl.BlockSpec((tk, tm), lambda i, j, k: (k, i))
    else:
        a_spec = pl.BlockSpec((tm, tk), lambda i, j, k: (i, k))
    jb, kb = (lambda i, j, k: j), (lambda i, j, k: k)
    if mode == "nt":
        b_spec = _mat_spec(b, b_kind, b_lead, tn, tk, jb, kb)
    else:
        b_spec = _mat_spec(b, b_kind, b_lead, tk, tn, kb, jb)
    e_spec = pl.BlockSpec((tm, tn), lambda i, j, k: (i, j))
    if o_kind == "plain":
        o_spec, o_shape = e_spec, (M, N)
    else:
        o_spec, o_shape = pl.BlockSpec((tn // shard, tm, shard), lambda i, j, k: (j, i, 0)), (N_DEV, M, shard)
    res = pl.pallas_call(
        body, name=name,
        grid=(M // tm, N // tn, nk),
        in_specs=[a_spec, b_spec] + [e_spec] * len(extras)
        + [pl.BlockSpec(p.shape, lambda i, j, k: (0, 0)) for p in params],
        out_specs=[o_spec] * n_out,
        out_shape=[jax.ShapeDtypeStruct(o_shape, dt) for dt in out_dtypes],
        scratch_shapes=[pltpu.VMEM((tm, tn), f32)],
        compiler_params=_cparams("parallel", "parallel", "arbitrary"),
    )(a, b, *extras, *params)
    return res[0] if n_out == 1 else res


def _row_spec(tm, cb, width):
    assert (cb * LANE) % width == 0
    blk = (cb * LANE) // width
    return pl.BlockSpec((tm, width), lambda i: (i, blk))


def _whole_spec(p):
    nd = p.ndim
    return pl.BlockSpec(p.shape, lambda i: (0,) * nd)


def _tl_fwd(name, fn, rows, params, out_widths, out_dtypes, tm=256):
    T = rows[0][0].shape[0]
    tm = min(tm, T)
    nr, npar = len(rows), len(params)

    def body(*refs):
        vals = [r[...] for r in refs[:nr + npar]]
        outs = fn(*vals)
        for o, v in zip(refs[nr + npar:], outs):
            o[...] = v.astype(o.dtype)

    res = pl.pallas_call(
        body, name=name, grid=(T // tm,),
        in_specs=[_row_spec(tm, cb, w) for (_, cb, w) in rows] + [_whole_spec(p) for p in params],
        out_specs=[pl.BlockSpec((tm, w), lambda i: (i, 0)) for w in out_widths],
        out_shape=[jax.ShapeDtypeStruct((T, w), dt) for w, dt in zip(out_widths, out_dtypes)],
        compiler_params=_cparams("parallel"),
    )(*[r[0] for r in rows], *params)
    return res


def _tl_bwd(name, fn, rows, params, cot_rows, cot_fn=None, skip=(), tm=256):
    T = rows[0][0].shape[0]
    tm = min(tm, T)
    nr, npar, nc = len(rows), len(params), len(cot_rows)
    keep = [k for k in range(nr) if k not in skip]

    def body(*refs):
        vals = [r[...] for r in refs[:nr + npar]]
        cots = [r[...] for r in refs[nr + npar:nr + npar + nc]]
        outs = refs[nr + npar + nc:]
        cot = tuple(cot_fn(*cots)) if cot_fn is not None else tuple(cots)
        _, vjp = jax.vjp(fn, *vals)
        grads = vjp(cot)
        for o, k in zip(outs, keep):
            o[...] = grads[k].astype(o.dtype)
        i = pl.program_id(0)
        for o, g in zip(outs[len(keep):], grads[nr:]):
            @pl.when(i == 0)
            def _(o=o):
                o[...] = jnp.zeros_like(o)
            o[...] += g

    res = pl.pallas_call(
        body, name=name, grid=(T // tm,),
        in_specs=[_row_spec(tm, cb, w) for (_, cb, w) in rows] + [_whole_spec(p) for p in params]
        + [_row_spec(tm, cb, w) for (_, cb, w) in cot_rows],
        out_specs=[pl.BlockSpec((tm, rows[k][2]), lambda i: (i, 0)) for k in keep] + [_whole_spec(p) for p in params],
        out_shape=[jax.ShapeDtypeStruct((T, rows[k][2]), f32) for k in keep]
        + [jax.ShapeDtypeStruct(p.shape, f32) for p in params],
        compiler_params=_cparams("arbitrary"),
    )(*[r[0] for r in rows], *params, *[r[0] for r in cot_rows])
    return res[:len(keep)], res[len(keep):]


def _ln_res_fn(x, mix, g, b):
    pre = DN_ALPHA * x + mix
    mu = jnp.mean(pre, axis=-1, keepdims=True)
    var = jnp.mean(jnp.square(pre - mu), axis=-1, keepdims=True)
    return ((pre - mu) * lax.rsqrt(var + LN_EPS) * g + b,)


@jax.custom_jvp
def _expm1(x):
    small = jnp.abs(x) < 0.3
    xs = jnp.where(small, x, 0.0)
    poly = xs * (1.0 + xs * (1 / 2 + xs * (1 / 6 + xs * (1 / 24 + xs * (1 / 120 + xs * (
        1 / 720 + xs * (1 / 5040 + xs * (1 / 40320 + xs * (1 / 362880)))))))))
    return jnp.where(small, poly, jnp.exp(x) - 1.0)


@_expm1.defjvp
def _expm1_jvp(primals, tangents):
    (x,), (t,) = primals, tangents
    return _expm1(x), t * jnp.exp(x)


def _rglru_pre_fn(pre_r, pre_i, xc, b_a, b_x, lam):
    r = jax.nn.sigmoid(pre_r + b_a)
    i = jax.nn.sigmoid(pre_i + b_x)
    log_a = -LRU_C * r * jax.nn.softplus(-lam)
    a = jnp.exp(log_a)
    b = jnp.sqrt(-_expm1(2.0 * log_a)) * (i * xc)
    return a, b


def _rec_gate_fn(h, gate):
    return (h * jax.nn.gelu(gate),)


def _loss_head(y, t, tm=256):
    T, Dm = y.shape

    def body(y_ref, t_ref, dy_ref, loss_ref):
        e = y_ref[...] - t_ref[...]
        dy_ref[...] = e * (1.0 / Dm)

        @pl.when(pl.program_id(0) == 0)
        def _():
            loss_ref[...] = jnp.zeros_like(loss_ref)

        loss_ref[...] += 0.5 * jnp.sum(jnp.mean(e * e, axis=-1, keepdims=True), axis=0, keepdims=True)

    dy, loss = pl.pallas_call(
        body, name="loss_head", grid=(T // tm,),
        in_specs=[pl.BlockSpec((tm, Dm), lambda i: (i, 0))] * 2,
        out_specs=[pl.BlockSpec((tm, Dm), lambda i: (i, 0)), pl.BlockSpec((SUBLANE, LANE), lambda i: (0, 0))],
        out_shape=[jax.ShapeDtypeStruct((T, Dm), f32), jax.ShapeDtypeStruct((SUBLANE, LANE), f32)],
        compiler_params=_cparams("arbitrary"),
    )(y, t)
    return loss[0, 0], dy


def _conv_fwd(name, x, cb0, nblk, w, bias, tm=2048):
    T = x.shape[0]
    tm = min(tm, T)
    hb = tm // SUBLANE
    has_b = bias is not None

    def body(*refs):
        cur, prev, w_ref = refs[:3]
        b_ref = refs[3] if has_b else None
        o = refs[-1]
        i = pl.program_id(1)
        p = jnp.where(i > 0, prev[...], 0.0)
        xcat = jnp.concatenate([p, cur[...]], axis=0)
        acc = cur[...] * w_ref[3:4, :]
        for j in range(3):
            acc = acc + pltpu.roll(xcat, 3 - j, axis=0)[SUBLANE:] * w_ref[j:j + 1, :]
        if has_b:
            acc = acc + b_ref[...]
        o[...] = acc

    in_specs = [
        pl.BlockSpec((tm, LANE), lambda c, i: (i, cb0 + c)),
        pl.BlockSpec((SUBLANE, LANE), lambda c, i: (jnp.maximum(i * hb - 1, 0), cb0 + c)),
        pl.BlockSpec((4, LANE), lambda c, i: (0, c)),
    ]
    args = [x, x, w]
    if has_b:
        in_specs.append(pl.BlockSpec((1, LANE), lambda c, i: (0, c)))
        args.append(bias)
    return pl.pallas_call(
        body, name=name, grid=(nblk, T // tm),
        in_specs=in_specs,
        out_specs=pl.BlockSpec((tm, LANE), lambda c, i: (i, c)),
        out_shape=jax.ShapeDtypeStruct((T, nblk * LANE), f32),
        compiler_params=_cparams("parallel", "parallel"),
    )(*args)


def _conv_bwd(name, dy, x, cb0, nblk, w, into, into_cb, tm=2048):
    T = x.shape[0]
    tm = min(tm, T)
    hb = tm // SUBLANE
    nt = T // tm

    def body(dcur, dnext, xcur, xprev, w_ref, _, dx_ref, dw_ref, db_ref):
        i = pl.program_id(1)
        d = dcur[...]
        dn = jnp.where(i < nt - 1, dnext[...], 0.0)
        dcat = jnp.concatenate([d, dn], axis=0)
        acc = d * w_ref[3:4, :]
        for j in range(3):
            s = 3 - j
            acc = acc + pltpu.roll(dcat, tm + SUBLANE - s, axis=0)[:tm] * w_ref[j:j + 1, :]
        dx_ref[...] = acc

        p = jnp.where(i > 0, xprev[...], 0.0)
        xcat = jnp.concatenate([p, xcur[...]], axis=0)
        rows = [jnp.sum(d * pltpu.roll(xcat, 3 - j, axis=0)[SUBLANE:], axis=0, keepdims=True) for j in range(3)]
        rows.append(jnp.sum(d * xcur[...], axis=0, keepdims=True))
        rows.append(jnp.zeros((SUBLANE - 4, LANE), f32))

        @pl.when(i == 0)
        def _():
            dw_ref[...] = jnp.zeros_like(dw_ref)
            db_ref[...] = jnp.zeros_like(db_ref)

        dw_ref[...] += jnp.concatenate(rows, axis=0)
        db_ref[...] += jnp.broadcast_to(jnp.sum(d, axis=0, keepdims=True), (SUBLANE, LANE))

    nh = T // SUBLANE
    dx, dw, db = pl.pallas_call(
        body, name=name, grid=(nblk, nt),
        in_specs=[
            pl.BlockSpec((tm, LANE), lambda c, i: (i, c)),
            pl.BlockSpec((SUBLANE, LANE), lambda c, i: (jnp.minimum((i + 1) * hb, nh - 1), c)),
            pl.BlockSpec((tm, LANE), lambda c, i: (i, cb0 + c)),
            pl.BlockSpec((SUBLANE, LANE), lambda c, i: (jnp.maximum(i * hb - 1, 0), cb0 + c)),
            pl.BlockSpec((4, LANE), lambda c, i: (0, c)),
            pl.BlockSpec(memory_space=pl.ANY),
        ],
        out_specs=[
            pl.BlockSpec((tm, LANE), lambda c, i: (i, into_cb + c)),
            pl.BlockSpec((SUBLANE, LANE), lambda c, i: (0, c)),
            pl.BlockSpec((SUBLANE, LANE), lambda c, i: (0, c)),
        ],
        out_shape=[jax.ShapeDtypeStruct(into.shape, f32),
                   jax.ShapeDtypeStruct((SUBLANE, nblk * LANE), f32),
                   jax.ShapeDtypeStruct((SUBLANE, nblk * LANE), f32)],
        input_output_aliases={5: 0},
        compiler_params=_cparams("parallel", "arbitrary"),
    )(dy, dy, x, x, w, into)
    return dx, dw[:4], db[0]


@functools.partial(jax.custom_vjp, nondiff_argnums=(1,))
def _lroll(x, s):
    return pltpu.roll(x, s, axis=1)


def _lroll_fwd(x, s):
    return _lroll(x, s), None


def _lroll_bwd(s, _, g):
    return (_lroll(g, (LANE - s) % LANE),)


_lroll.defvjp(_lroll_fwd, _lroll_bwd)


def _rope_tables(T):
    half = A_HEAD_DIM // 2
    inv_freq = ROPE_THETA ** (-jnp.arange(half, dtype=f32) / half)
    ang = jnp.arange(T, dtype=f32)[:, None] * inv_freq[None, :]
    cos, sin = jnp.cos(ang), jnp.sin(ang)
    return jnp.tile(jnp.concatenate([cos, cos], axis=1), (1, 2)), jnp.tile(jnp.concatenate([-sin, sin], axis=1), (1, 2))


def _attn_block_fn(n, q, kp, kc, vp, vc, cq, sq, cp, sp, sinks):
    W = WINDOW
    lane = lax.broadcasted_iota(jnp.int32, (W, LANE), 1)
    lo_half = (lane % A_HEAD_DIM) < (A_HEAD_DIM // 2)
    lane8 = lax.broadcasted_iota(jnp.int32, sinks.shape, 1)

    def rope(x, c, s):
        return x * c + jnp.where(lo_half, _lroll(x, LANE - A_HEAD_DIM // 2), _lroll(x, A_HEAD_DIM // 2)) * s

    k2 = jnp.concatenate([rope(kp, cp, sp), rope(kc, cq, sq)], axis=0).astype(bf16)
    v2 = jnp.concatenate([vp, vc], axis=0).astype(bf16)
    qs = []
    for t in range(4):
        qt = rope(q[:, LANE * t:LANE * (t + 1)], cq, sq)
        g = t // 2
        for hh in range(2):
            qa = jnp.where((lane // A_HEAD_DIM) == hh, qt, 0.0)
            qs.append(_lroll(qa, A_HEAD_DIM) if hh != g else qa)
    s_all = _dot(jnp.concatenate(qs, axis=0).astype(bf16), k2, NT) * (A_HEAD_DIM ** -0.5)
    row = lax.broadcasted_iota(jnp.int32, (W, 2 * W), 0)
    col = lax.broadcasted_iota(jnp.int32, (W, 2 * W), 1)
    dist = row + W - col
    mask = (dist >= 0) & (dist < W) & ((col >= W) | (n > 0))
    ps = []
    for j in range(A_Q_HEADS):
        s = jnp.where(mask, s_all[W * j:W * (j + 1)], -jnp.inf)
        sink = jnp.sum(jnp.where(lane8 == j, sinks, 0.0), axis=1, keepdims=True)
        m = jnp.maximum(jnp.max(s, axis=-1, keepdims=True), sink)
        e = jnp.exp(s - m)
        ps.append((e / (jnp.sum(e, axis=-1, keepdims=True) + jnp.exp(sink - m))).astype(bf16))
    o = _dot(jnp.concatenate(ps, axis=0), v2, NN)
    outs = []
    for t in range(4):
        g = t // 2
        ot = jnp.zeros((W, LANE), f32)
        for hh in range(2):
            j = 2 * t + hh
            oj = jnp.where((lane // A_HEAD_DIM) == g, o[W * j:W * (j + 1)], 0.0)
            ot = ot + (_lroll(oj, A_HEAD_DIM) if hh != g else oj)
        outs.append(ot)
    return jnp.concatenate(outs, axis=1)


def _attn_specs():
    W = WINDOW
    prev = lambda n: jnp.maximum(n - 1, 0)
    return [
        pl.BlockSpec((W, 4 * LANE), lambda n: (n, CB_QA // 4)),
        pl.BlockSpec((W, LANE), lambda n: (prev(n), CB_KA)),
        pl.BlockSpec((W, LANE), lambda n: (n, CB_KA)),
        pl.BlockSpec((W, LANE), lambda n: (prev(n), CB_VA)),
        pl.BlockSpec((W, LANE), lambda n: (n, CB_VA)),
        pl.BlockSpec((W, LANE), lambda n: (n, 0)),
        pl.BlockSpec((W, LANE), lambda n: (n, 0)),
        pl.BlockSpec((W, LANE), lambda n: (prev(n), 0)),
        pl.BlockSpec((W, LANE), lambda n: (prev(n), 0)),
        pl.BlockSpec((1, A_Q_HEADS), lambda n: (0, 0)),
    ]


def _attn_fwd(name, proj, cos, sin, sinks):
    T = proj.shape[0]
    W = WINDOW

    def body(*refs):
        o = refs[-1]
        o[...] = _attn_block_fn(pl.program_id(0), *[r[...] for r in refs[:-1]])

    return pl.pallas_call(
        body, name=name, grid=(T // W,),
        in_specs=_attn_specs(),
        out_specs=pl.BlockSpec((W, 4 * LANE), lambda n: (n, 0)),
        out_shape=jax.ShapeDtypeStruct((T, 2 * 4 * LANE), f32),
        compiler_params=_cparams("parallel"),
    )(proj, proj, proj, proj, proj, cos, sin, cos, sin, sinks)


def _attn_bwd(name, proj, cos, sin, sinks, d_oab):
    T = proj.shape[0]
    W = WINDOW
    Q = 4 * LANE

    def body(*refs):
        ins = [r[...] for r in refs[:10]]
        do = refs[10][...]
        d_ref, ds_ref = refs[11:]
        n = pl.program_id(0)
        _, vjp = jax.vjp(functools.partial(_attn_block_fn, n), *ins)
        dq, dkp, dkc, dvp, dvc, _, _, _, _, dsk = vjp(do)

        @pl.when(n == 0)
        def _():
            d_ref[:, Q:] = jnp.zeros((T, 2 * LANE), f32)
            ds_ref[...] = jnp.zeros_like(ds_ref)

        cur = pl.ds(pl.multiple_of(n * W, W), W)
        d_ref[cur, :Q] = dq
        d_ref[cur, Q:Q + LANE] += dkc
        d_ref[cur, Q + LANE:] += dvc
        ds_ref[...] += dsk

        @pl.when(n > 0)
        def _():
            prv = pl.ds(pl.multiple_of((n - 1) * W, W), W)
            d_ref[prv, Q:Q + LANE] += dkp
            d_ref[prv, Q + LANE:] += dvp

    return pl.pallas_call(
        body, name=name, grid=(T // W,),
        in_specs=_attn_specs() + [pl.BlockSpec((W, Q), lambda n: (n, 0))],
        out_specs=[pl.BlockSpec((T, Q + 2 * LANE), lambda n: (0, 0)),
                   pl.BlockSpec((1, A_Q_HEADS), lambda n: (0, 0))],
        out_shape=[jax.ShapeDtypeStruct((T, HYB_PROJ_PAD), f32), jax.ShapeDtypeStruct((1, A_Q_HEADS), f32)],
        compiler_params=_cparams("arbitrary"),
    )(proj, proj, proj, proj, proj, cos, sin, cos, sin, sinks, d_oab)


def _bdot(spec, a, b, precision=None):
    return jnp.einsum(spec, a, b, preferred_element_type=f32, precision=precision)


@jax.custom_vjp
def _tri_inv(a):
    H, C, _ = a.shape
    B = 2 * SUBLANE
    nb = C // B
    r = lax.broadcasted_iota(jnp.int32, (C, C), 0)
    c = lax.broadcasted_iota(jnp.int32, (C, C), 1)
    a4 = jnp.where((r // B) == (c // B), a, 0.0).reshape(H, nb, B, C)
    t4 = jnp.broadcast_to(jnp.where(r == c, 1.0, 0.0).astype(f32), a.shape).reshape(H, nb, B, C)
    for j in range(B - 1):
        col = jnp.concatenate([a4[:, b:b + 1, :, B * b + j:B * b + j + 1] for b in range(nb)], axis=1)
        t4 = t4 - col * t4[:, :, j:j + 1, :]
    x = t4.reshape(H, C, C)
    hi = lax.Precision.HIGH
    while B < C:
        m = jnp.where(((r // (2 * B)) == (c // (2 * B))) & ((r // B) > (c // B)), a, 0.0)
        x = x - _bdot("hij,hjk->hik", x, _bdot("hij,hjk->hik", m, x, precision=hi), precision=hi)
        B *= 2
    return x


def _tri_inv_fwd(a):
    t = _tri_inv(a)
    return t, t


def _tri_inv_bwd(t, g):
    C = t.shape[-1]
    r = lax.broadcasted_iota(jnp.int32, (C, C), 0)
    c = lax.broadcasted_iota(jnp.int32, (C, C), 1)
    x = _bdot("hki,hkj->hij", t, g, precision=lax.Precision.HIGHEST)
    y = _bdot("hik,hjk->hij", x, t, precision=lax.Precision.HIGHEST)
    return (jnp.where(r > c, -y, 0.0),)


_tri_inv.defvjp(_tri_inv_fwd, _tri_inv_bwd)


@jax.custom_vjp
def _tri_inv_saved(a, t):
    return t


_tri_inv_saved.defvjp(lambda a, t: (t, t), lambda t, g: (_tri_inv_bwd(t, g)[0], jnp.zeros_like(t)))


def _silu(x):
    return x * jax.nn.sigmoid(x)


def _l2n(x):
    return x * lax.rsqrt(jnp.sum(x * x, axis=-1, keepdims=True) + NORM_EPS)


def _delta_chunk_fn(cq, ck, cv, z, lg, a_log, dt_bias, norm_w, S, t_saved=None, want_t=False):
    C = B_CHUNK
    lane = lax.broadcasted_iota(jnp.int32, (C, LANE), 1)
    pick = lambda l0: jnp.concatenate(
        [jnp.sum(jnp.where(lane == l0 + h, lg, 0.0), axis=1, keepdims=True)[None] for h in range(B_HEADS)], axis=0)
    bl, al = pick(0), pick(B_HEADS)
    q = _l2n(_silu(cq)) * (B_HEAD_DIM ** -0.5)
    k = _l2n(_silu(ck))
    v = _silu(cv)
    beta = jax.nn.sigmoid(bl)
    g = -jnp.exp(a_log) * jax.nn.softplus(al + dt_bias)
    r = lax.broadcasted_iota(jnp.int32, (C, C), 0)
    c = lax.broadcasted_iota(jnp.int32, (C, C), 1)
    eye = r == c
    g_row = jnp.sum(jnp.where(eye, g, 0.0), axis=1, keepdims=True)
    gc = jnp.sum(jnp.where(c <= r, g_row, 0.0), axis=2, keepdims=True)
    gc_row = jnp.sum(jnp.where(eye, gc, 0.0), axis=1, keepdims=True)
    decay_incl = jnp.exp(jnp.where(r >= c, gc - gc_row, -jnp.inf))
    decay_strict = jnp.where(r > c, decay_incl, 0.0)
    kb = k * beta
    vb = v * beta
    kbf = k.astype(bf16)
    a_mat = _bdot("hik,hjk->hij", kb.astype(bf16), kbf) * decay_strict
    t_f32 = _tri_inv(a_mat) if t_saved is None else _tri_inv_saved(a_mat, t_saved)
    t_mat = t_f32.astype(bf16)
    eg = jnp.exp(gc)
    u = _bdot("hij,hjv->hiv", t_mat, vb.astype(bf16))
    w = _bdot("hij,hjk->hik", t_mat, (kb * eg).astype(bf16))
    qk = _bdot("hik,hjk->hij", q.astype(bf16), kbf) * decay_incl
    g_last = jnp.sum(g, axis=1, keepdims=True)
    k_tail = k * jnp.exp(g_last - gc)
    Sb = S.astype(bf16)
    v_new = u - _bdot("hck,hkv->hcv", w.astype(bf16), Sb)
    o = _bdot("hck,hkv->hcv", (q * eg).astype(bf16), Sb) + _bdot("hij,hjv->hiv", qk.astype(bf16), v_new.astype(bf16))
    S_new = S * jnp.exp(g_last) + _bdot("hck,hcv->hkv", k_tail.astype(bf16), v_new.astype(bf16))
    ob = o * lax.rsqrt(jnp.mean(o * o, axis=-1, keepdims=True) + NORM_EPS) * norm_w
    return (ob * _silu(z), S_new) + ((t_f32,) if want_t else ())


def _delta_in_specs(rev, N):
    C = B_CHUNK
    ix = (lambda n: N - 1 - n) if rev else (lambda n: n)
    specs = [pl.BlockSpec((C, 3 * B_HEADS * LANE), lambda n: (ix(n), 0))]
    specs += [pl.BlockSpec((C, LANE), lambda n, h=h: (ix(n), CB_Z + h)) for h in range(B_HEADS)]
    specs += [
        pl.BlockSpec((C, LANE), lambda n: (ix(n), CB_LG)),
        pl.BlockSpec((B_HEADS, 1, 1), lambda n: (0, 0, 0)),
        pl.BlockSpec((B_HEADS, 1, 1), lambda n: (0, 0, 0)),
        pl.BlockSpec((1, LANE), lambda n: (0, 0)),
    ]
    return specs


def _delta_inputs(c_ref, z_refs, lg, al, dt, nw):
    H = B_HEADS
    part = lambda p: jnp.stack([c_ref[:, LANE * (p * H + h):LANE * (p * H + h + 1)] for h in range(H)])
    return (part(0), part(1), part(2), jnp.stack([z[...] for z in z_refs]), lg[...], al[...], dt[...], nw[...])


def _delta_fwd(name, c, proj, a_log, dt_bias, norm_w, o_ab):
    T = c.shape[0]
    C = B_CHUNK
    N = T // C
    Dh = B_HEAD_DIM
    H = B_HEADS

    def body(*refs):
        c_ref, z_refs, (lg, al, dt, nw) = refs[0], refs[1:1 + H], refs[1 + H:5 + H]
        o_ref, s_ref, t_ref, S = refs[6 + H:]

        @pl.when(pl.program_id(0) == 0)
        def _():
            S[...] = jnp.zeros_like(S)

        s0 = S[...]
        s_ref[...] = s0
        ob, s1, t = _delta_chunk_fn(*_delta_inputs(c_ref, z_refs, lg, al, dt, nw), s0, want_t=True)
        for h in range(H):
            o_ref[:, LANE * h:LANE * (h + 1)] = ob[h]
        t_ref[...] = t
        S[...] = s1

    return pl.pallas_call(
        body, name=name, grid=(N,),
        in_specs=_delta_in_specs(False, N) + [pl.BlockSpec(memory_space=pl.ANY)],
        out_specs=[pl.BlockSpec((C, H * LANE), lambda n: (n, 1)),
                   pl.BlockSpec((H, None, Dh, Dh), lambda n: (0, n, 0, 0)),
                   pl.BlockSpec((H, None, C, C), lambda n: (0, n, 0, 0))],
        out_shape=[jax.ShapeDtypeStruct(o_ab.shape, f32), jax.ShapeDtypeStruct((H, N, Dh, Dh), f32),
                   jax.ShapeDtypeStruct((H, N, C, C), f32)],
        input_output_aliases={5 + H: 0},
        scratch_shapes=[pltpu.VMEM((H, Dh, Dh), f32)],
        compiler_params=_cparams("arbitrary"),
    )(c, *([proj] * H), proj, a_log, dt_bias, norm_w, o_ab)


def _delta_bwd(name, c, proj, a_log, dt_bias, norm_w, s_saved, t_saved, d_oab, dproj):
    T = c.shape[0]
    C = B_CHUNK
    N = T // C
    Dh = B_HEAD_DIM
    H = B_HEADS

    def body(*refs):
        c_ref, z_refs, (lg, al, dt, nw) = refs[0], refs[1:1 + H], refs[1 + H:5 + H]
        s_ref, t_ref, do_ref = refs[5 + H:8 + H]
        dc, dtail, dal, ddt, dnw, dS = refs[9 + H:]

        @pl.when(pl.program_id(0) == 0)
        def _():
            dS[...] = jnp.zeros_like(dS)
            dal[...] = jnp.zeros_like(dal)
            ddt[...] = jnp.zeros_like(ddt)
            dnw[...] = jnp.zeros_like(dnw)

        _, vjp = jax.vjp(functools.partial(_delta_chunk_fn, t_saved=t_ref[...]),
                         *_delta_inputs(c_ref, z_refs, lg, al, dt, nw), s_ref[...])
        do = jnp.stack([do_ref[:, LANE * h:LANE * (h + 1)] for h in range(H)])
        g = vjp((do, dS[...]))
        for h in range(H):
            for p in range(3):
                dc[:, LANE * (p * H + h):LANE * (p * H + h + 1)] = g[p][h]
            dtail[:, LANE * h:LANE * (h + 1)] = g[3][h]
        dtail[:, LANE * H:LANE * (H + 1)] = g[4]
        dtail[:, LANE * (H + 1):] = jnp.zeros((C, LANE), f32)
        dal[...] += g[5]
        ddt[...] += g[6]
        dnw[...] += g[7]
        dS[...] = g[8]

    rn = lambda n: N - 1 - n
    return pl.pallas_call(
        body, name=name, grid=(N,),
        in_specs=_delta_in_specs(True, N) + [
            pl.BlockSpec((H, None, Dh, Dh), lambda n: (0, rn(n), 0, 0)),
            pl.BlockSpec((H, None, C, C), lambda n: (0, rn(n), 0, 0)),
            pl.BlockSpec((C, H * LANE), lambda n: (rn(n), 1)),
            pl.BlockSpec(memory_space=pl.ANY),
        ],
        out_specs=[
            pl.BlockSpec((C, 3 * H * LANE), lambda n: (rn(n), 0)),
            pl.BlockSpec((C, (H + 2) * LANE), lambda n: (rn(n), CB_Z // (H + 2))),
            pl.BlockSpec((H, 1, 1), lambda n: (0, 0, 0)),
            pl.BlockSpec((H, 1, 1), lambda n: (0, 0, 0)),
            pl.BlockSpec((1, LANE), lambda n: (0, 0)),
        ],
        out_shape=[jax.ShapeDtypeStruct((T, 3 * H * Dh), f32), jax.ShapeDtypeStruct(dproj.shape, f32),
                   jax.ShapeDtypeStruct((H, 1, 1), f32), jax.ShapeDtypeStruct((H, 1, 1), f32),
                   jax.ShapeDtypeStruct((1, LANE), f32)],
        input_output_aliases={8 + H: 1},
        scratch_shapes=[pltpu.VMEM((H, Dh, Dh), f32)],
        compiler_params=_cparams("arbitrary"),
    )(c, *([proj] * H), proj, a_log, dt_bias, norm_w, s_saved, t_saved, d_oab, dproj)


def _gate_matmuls(xc, wa_ref, wx_ref):
    bw = wa_ref.shape[-1]
    xb = xc.astype(bf16)
    blocks = [xb[:, bw * h:bw * (h + 1)] for h in range(LRU_BLOCKS)]
    return (jnp.concatenate([_dot(blocks[h], wa_ref[h], NN) for h in range(LRU_BLOCKS)], axis=1),
            jnp.concatenate([_dot(blocks[h], wx_ref[h], NN) for h in range(LRU_BLOCKS)], axis=1))


def _gates_fwd(name, xc, w_a, w_x, pars, tm=256):
    T, Wd = xc.shape
    tm = min(tm, T)

    def body(x_ref, wa_ref, wx_ref, ba, bx, lam, a_ref, b_ref):
        x = x_ref[...]
        pr, pi = _gate_matmuls(x, wa_ref, wx_ref)
        a_ref[...], b_ref[...] = _rglru_pre_fn(pr, pi, x, ba[...], bx[...], lam[...])

    row = pl.BlockSpec((tm, Wd), lambda i: (i, 0))
    return pl.pallas_call(
        body, name=name, grid=(T // tm,),
        in_specs=[row, _whole_spec(w_a), _whole_spec(w_x)] + [_whole_spec(p) for p in pars],
        out_specs=[row, row], out_shape=[jax.ShapeDtypeStruct((T, Wd), f32)] * 2,
        compiler_params=_cparams("parallel"),
    )(xc, w_a, w_x, *pars)


def _gates_bwd(name, xc, w_a, w_x, pars, lam_t, h_prev, tm=256):
    T, Wd = xc.shape
    tm = min(tm, T)
    bw = Wd // LRU_BLOCKS

    def body(x_ref, wa_ref, wx_ref, ba, bx, lam, lt_ref, hp_ref, dx_ref, dr_ref, di_ref, dba, dbx, dlam):
        x = x_ref[...]
        pr, pi = _gate_matmuls(x, wa_ref, wx_ref)
        _, vjp = jax.vjp(_rglru_pre_fn, pr, pi, x, ba[...], bx[...], lam[...])
        lt = lt_ref[...]
        dpr, dpi, dxc, g_ba, g_bx, g_lam = vjp((lt * hp_ref[...], lt))
        dprb, dpib = dpr.astype(bf16), dpi.astype(bf16)
        dx_ref[...] = dxc + jnp.concatenate(
            [_dot(dprb[:, bw * h:bw * (h + 1)], wa_ref[h], NT) + _dot(dpib[:, bw * h:bw * (h + 1)], wx_ref[h], NT)
             for h in range(LRU_BLOCKS)], axis=1)
        dr_ref[...] = dprb
        di_ref[...] = dpib

        @pl.when(pl.program_id(0) == 0)
        def _():
            dba[...] = jnp.zeros_like(dba)
            dbx[...] = jnp.zeros_like(dbx)
            dlam[...] = jnp.zeros_like(dlam)

        dba[...] += g_ba
        dbx[...] += g_bx
        dlam[...] += g_lam

    row = pl.BlockSpec((tm, Wd), lambda i: (i, 0))
    vec = pl.BlockSpec((1, Wd), lambda i: (0, 0))
    return pl.pallas_call(
        body, name=name, grid=(T // tm,),
        in_specs=[row, _whole_spec(w_a), _whole_spec(w_x)] + [_whole_spec(p) for p in pars] + [row, row],
        out_specs=[row, row, row, vec, vec, vec],
        out_shape=[jax.ShapeDtypeStruct((T, Wd), f32), jax.ShapeDtypeStruct((T, Wd), bf16),
                   jax.ShapeDtypeStruct((T, Wd), bf16)] + [jax.ShapeDtypeStruct((1, Wd), f32)] * 3,
        compiler_params=_cparams("arbitrary"),
    )(xc, w_a, w_x, *pars, lam_t, h_prev)


def _blockdiag_bwd_dw(name, xc, dpr, dpi, tk=512):
    T, Wd = xc.shape
    bw = Wd // LRU_BLOCKS
    tk = min(tk, T)

    def body(x_ref, dr, di, oa, ox):
        @pl.when(pl.program_id(1) == 0)
        def _():
            oa[...] = jnp.zeros_like(oa)
            ox[...] = jnp.zeros_like(ox)

        xb = x_ref[...].astype(bf16)
        oa[...] += _dot(xb, dr[...].astype(bf16), TN)
        ox[...] += _dot(xb, di[...].astype(bf16), TN)

    xs = pl.BlockSpec((tk, bw), lambda h, k: (k, h))
    ws = pl.BlockSpec((None, bw, bw), lambda h, k: (h, 0, 0))
    return pl.pallas_call(
        body, name=name, grid=(LRU_BLOCKS, T // tk), in_specs=[xs, xs, xs], out_specs=[ws, ws],
        out_shape=[jax.ShapeDtypeStruct((LRU_BLOCKS, bw, bw), f32)] * 2,
        compiler_params=_cparams("parallel", "arbitrary"),
    )(xc, dpr, dpi)


def _scan(name, a, proj, reverse, b=None, h=None, dhg=None, tt=512, cb=512):
    T, Wd = a.shape
    tt, cb = min(tt, T), min(cb, Wd)
    nt = T // tt
    ng = tt // SUBLANE

    def body(a_ref, g_ref, *rest):
        n_in = 2 if reverse else 1
        ins, outs, (carry, carry_a) = rest[:n_in], rest[n_in:-2], rest[-2:]

        @pl.when(pl.program_id(1) == 0)
        def _():
            carry[...] = jnp.zeros_like(carry)
            carry_a[...] = jnp.zeros_like(carry_a)

        row = lax.broadcasted_iota(jnp.int32, (SUBLANE, cb), 0)

        def step(gi, c):
            hp, ap = c
            g = (ng - 1 - gi) if reverse else gi
            rows = pl.ds(pl.multiple_of(g * SUBLANE, SUBLANE), SUBLANE)
            A = a_ref[rows, :]
            gate = g_ref[rows, :]
            a_first = jnp.broadcast_to(A[0:1, :], (SUBLANE, cb))
            if reverse:
                _, vjp = jax.vjp(_rec_gate_fn, ins[0][rows, :], gate)
                B, dgate = vjp((ins[1][rows, :],))
                outs[1][rows, :] = dgate
                A = jnp.where(row == SUBLANE - 1, ap, pltpu.roll(A, SUBLANE - 1, axis=0))
            else:
                B = ins[0][rows, :]
            for s in (1, 2, 4):
                sh = (SUBLANE - s) if reverse else s
                As = pltpu.roll(A, sh, axis=0)
                Bs = pltpu.roll(B, sh, axis=0)
                valid = (row < SUBLANE - s) if reverse else (row >= s)
                B = jnp.where(valid, A * Bs + B, B)
                A = jnp.where(valid, A * As, A)
            hcur = A * hp + B
            outs[0][rows, :] = hcur
            if not reverse:
                outs[1][rows, :] = jnp.where(row == 0, hp, pltpu.roll(hcur, 1, axis=0))
                outs[2][rows, :] = _rec_gate_fn(hcur, gate)[0]
            edge = hcur[0:1, :] if reverse else hcur[SUBLANE - 1:SUBLANE, :]
            return jnp.broadcast_to(edge, (SUBLANE, cb)), a_first

        carry[...], carry_a[...] = lax.fori_loop(0, ng, step, (carry[...], carry_a[...]))

    nc = Wd // cb
    tok = (lambda i: nt - 1 - i) if reverse else (lambda i: i)
    spec = pl.BlockSpec((tt, cb), lambda c, i: (tok(i), c))
    gate_half = pl.BlockSpec((tt, cb), lambda c, i: (tok(i), nc + c))
    if reverse:
        args, out_specs = (a, proj, h, dhg), [spec, gate_half]
        out_shape = [jax.ShapeDtypeStruct((T, Wd), f32), jax.ShapeDtypeStruct((T, 2 * Wd), f32)]
    else:
        args, out_specs = (a, proj, b), [spec] * 3
        out_shape = [jax.ShapeDtypeStruct((T, Wd), f32)] * 3
    return pl.pallas_call(
        body, name=name, grid=(nc, nt), in_specs=[spec, gate_half] + [spec] * (len(args) - 2), out_specs=out_specs,
        out_shape=out_shape,
        scratch_shapes=[pltpu.VMEM((SUBLANE, cb), f32), pltpu.VMEM((SUBLANE, cb), f32)],
        compiler_params=_cparams("parallel", "arbitrary"),
    )(*args)


def _relu2_epilogue(r):
    h = jnp.maximum(r, 0.0)
    return r, h * h


def _drelu2_epilogue(r, a):
    return (r * (2.0 * jnp.maximum(a.astype(f32), 0.0)),)


def _residual_cot(through, upper):
    return (through + DN_ALPHA * upper,)


def _merge_cols(name, g, tm=256):
    _, L, R, s = g.shape

    def body(g_ref, o_ref):
        for d in range(N_DEV):
            o_ref[:, s * d:s * (d + 1)] = g_ref[d].astype(bf16)
        o_ref[:, N_DEV * s:] = jnp.zeros((tm, HYB_PROJ_PAD - N_DEV * s), bf16)

    return pl.pallas_call(
        body, name=name, grid=(L, R // tm),
        in_specs=[pl.BlockSpec((N_DEV, None, tm, s), lambda l, i: (0, l, i, 0))],
        out_specs=pl.BlockSpec((None, tm, HYB_PROJ_PAD), lambda l, i: (l, i, 0)),
        out_shape=jax.ShapeDtypeStruct((L, R, HYB_PROJ_PAD), bf16),
        compiler_params=_cparams("parallel", "parallel"),
    )(g)


def _split_cols(name, dw, tm=256):
    R = dw.shape[0]
    s = HYB_PROJ // N_DEV

    def body(g_ref, o_ref):
        for d in range(N_DEV):
            o_ref[d] = g_ref[:, s * d:s * (d + 1)].astype(bf16)

    return pl.pallas_call(
        body, name=name, grid=(R // tm,),
        in_specs=[pl.BlockSpec((tm, HYB_PROJ_PAD), lambda i: (i, 0))],
        out_specs=pl.BlockSpec((N_DEV, tm, s), lambda i: (0, i, 0)),
        out_shape=jax.ShapeDtypeStruct((N_DEV, R, s), bf16),
        compiler_params=_cparams("parallel"),
    )(dw)


def _rows_to_dev(dw):
    nb, r, c = dw.shape
    t = dw.reshape(nb, N_DEV, r // N_DEV, c)
    return jnp.moveaxis(t, 1, 0).reshape(N_DEV, nb * (r // N_DEV), c).astype(bf16)


def _ln_epilogue(r, x, g, b):
    return r, _ln_res_fn(x, r, g, b)[0]


def _hybrid_fwd(tag, x, W, j, cos, sin, ln, before_out):
    proj = _mm(f"{tag}_proj", x, W["hyb_w_in"][j], "nn", b_kind="lead", b_lead=0)
    o_a = _attn_fwd(f"{tag}_attn", proj, cos, sin, W["hyb_sinks"][j][None, :])
    c = _conv_fwd(f"{tag}_conv", proj, CB_CONV, 12, W["hyb_conv_w"][j], None)
    o_ab, s_saved, t_saved = _delta_fwd(f"{tag}_delta", c, proj, W["hyb_a_log"][j].reshape(B_HEADS, 1, 1),
                                        W["hyb_dt_bias"][j].reshape(B_HEADS, 1, 1), W["hyb_norm_w"][j][None, :], o_a)
    before_out(o_ab)
    mix, x1 = _mm(f"{tag}_out", o_ab, W["hyb_w_out"][j], "nn", b_kind="lead", b_lead=0, epilogue=_ln_epilogue,
                  extras=(x,), params=ln, out_dtypes=(f32, f32), tm=512)
    return mix, x1, (proj, c, s_saved, t_saved, o_ab)


def _hybrid_bwd(tag, x, dmix, addend, W, j, cos, sin, saved, G, send_early):
    proj, c, s_saved, t_saved, o_ab = saved
    T = x.shape[0]
    d_oab = _mm(f"{tag}_dout", dmix, W["hyb_w_out"][j], "nt", b_kind="lead", b_lead=0)
    G["hyb_w_out"][j] = _mm(f"{tag}_dwout", o_ab, dmix, "tn", out_dtypes=(bf16,)).reshape(N_DEV, -1, D_MODEL)
    sinks = W["hyb_sinks"][j][None, :] + send_early({("hyb_w_out", j): G["hyb_w_out"][j]})
    dproj, dsinks = _attn_bwd(f"{tag}_dattn", proj, cos, sin, sinks, d_oab)
    a_log = W["hyb_a_log"][j].reshape(B_HEADS, 1, 1)
    dt_bias = W["hyb_dt_bias"][j].reshape(B_HEADS, 1, 1)
    dc, dproj, dal, ddt, dnw = _delta_bwd(f"{tag}_ddelta", c, proj, a_log, dt_bias, W["hyb_norm_w"][j][None, :],
                                          s_saved, t_saved, d_oab, dproj)
    dproj, dconv_w, _ = _conv_bwd(f"{tag}_dconv", dc, proj, CB_CONV, 12, W["hyb_conv_w"][j], dproj, CB_CONV)
    dx = _mm(f"{tag}_dx", dproj, W["hyb_w_in"][j], "nt", b_kind="lead", b_lead=0,
             **({} if addend is None else dict(epilogue=_residual_cot, extras=(addend,))))
    G["hyb_w_in"][j] = _split_cols(f"{tag}_dwin_split", _mm(f"{tag}_dwin", x, dproj, "tn", tn=1536))
    G["hyb_sinks"][j] = dsinks[0]
    G["hyb_conv_w"][j] = dconv_w
    G["hyb_a_log"][j] = dal.reshape(B_HEADS)
    G["hyb_dt_bias"][j] = ddt.reshape(B_HEADS)
    G["hyb_norm_w"][j] = dnw[0]
    return dx


def _rec_fwd(tag, x, W, j, ln, before_out):
    Wd = D_MODEL
    proj = _mm(f"{tag}_proj", x, W["rec_w_in"][j], "nn", b_kind="devcol", b_lead=0)
    xc = _conv_fwd(f"{tag}_conv", proj, 0, Wd // LANE, W["rec_conv_w"][j], W["rec_conv_b"][j][None, :])
    pars = [W["rec_b_a"][j][None, :], W["rec_b_x"][j][None, :], W["rec_lambda"][j][None, :]]
    a, b = _gates_fwd(f"{tag}_gates", xc, W["rec_w_a"][j][0], W["rec_w_x"][j][0], pars)
    h, h_prev, hg = _scan(f"{tag}_scan", a, proj, False, b=b)
    before_out(hg)
    mix, x1 = _mm(f"{tag}_out", hg, W["rec_w_out"][j], "nn", b_kind="lead", b_lead=0, epilogue=_ln_epilogue,
                  extras=(x,), params=ln, out_dtypes=(f32, f32), tm=512)
    return mix, x1, (proj, xc, a, h, h_prev, hg)


def _rec_bwd(tag, x, dmix, addend, W, j, saved, G, send_early):
    proj, xc, a, h, h_prev, hg = saved
    Wd = D_MODEL
    dhg = _mm(f"{tag}_dout", dmix, W["rec_w_out"][j], "nt", b_kind="lead", b_lead=0)
    G["rec_w_out"][j] = _mm(f"{tag}_dwout", hg, dmix, "tn", out_dtypes=(bf16,)).reshape(N_DEV, -1, D_MODEL)
    sent = send_early({("rec_w_out", j): G["rec_w_out"][j]})
    lam_t, dproj = _scan(f"{tag}_dscan", a, proj, True, h=h, dhg=dhg)
    pars = [W["rec_b_a"][j][None, :] + sent, W["rec_b_x"][j][None, :], W["rec_lambda"][j][None, :]]
    dxc, dpr, dpi, db_a, db_x, dlam = _gates_bwd(f"{tag}_dgates", xc, W["rec_w_a"][j][0], W["rec_w_x"][j][0], pars,
                                                 lam_t, h_prev)
    dwa, dwx = _blockdiag_bwd_dw(f"{tag}_dgates_dw", xc, dpr, dpi)
    G["rec_w_a"][j], G["rec_w_x"][j] = _rows_to_dev(dwa), _rows_to_dev(dwx)
    dproj, dconv_w, dconv_b = _conv_bwd(f"{tag}_dconv", dxc, proj, 0, Wd // LANE, W["rec_conv_w"][j], dproj, 0)
    dx = _mm(f"{tag}_dx", dproj, W["rec_w_in"][j], "nt", b_kind="devcol", b_lead=0,
             **({} if addend is None else dict(epilogue=_residual_cot, extras=(addend,))))
    G["rec_w_in"][j] = _mm(f"{tag}_dwin", x, dproj, "tn", o_kind="devcol", out_dtypes=(bf16,), tn=2048)
    G["rec_conv_w"][j] = dconv_w
    G["rec_conv_b"][j] = dconv_b
    G["rec_b_a"][j] = db_a[0]
    G["rec_b_x"][j] = db_x[0]
    G["rec_lambda"][j] = dlam[0]
    return dx


def _local_step(x, target, W, load_layer, grads_ready):
    T = x.shape[0]
    cos, sin = _rope_tables(T)
    saved = []
    for layer in range(DEPTH):
        j = layer // 2
        tag = f"L{layer}"
        load_layer(layer, 0, x)
        ln1 = (W["ln1_g"][layer][None, :], W["ln1_b"][layer][None, :])
        before_out = functools.partial(load_layer, layer, 1)
        if layer % 2 == 0:
            mix, x1, sv = _hybrid_fwd(tag, x, W, j, cos, sin, ln1, before_out)
        else:
            mix, x1, sv = _rec_fwd(tag, x, W, j, ln1, before_out)
        load_layer(layer, 2, x1)
        a, h2 = _mm(f"{tag}_mlp1", x1, W["mlp_w1"][layer], "nn", b_kind="devcol", b_lead=0, epilogue=_relu2_epilogue,
                    out_dtypes=(bf16, bf16), tm=2048)
        ln2 =(W["ln2_g"][layer][None, :], W["ln2_b"][layer][None, :])
        y, x2 = _mm(f"{tag}_mlp2", h2, W["mlp_w2"][layer], "nn", b_kind="devrow", b_lead=0, epilogue=_ln_epilogue,
                    extras=(x1,), params=ln2, out_dtypes=(f32, f32))
        saved.append((x, sv, mix, x1, a, h2, y))
        x = x2
    loss, dx = _loss_head(x, target)

    G = {k: [None] * (DEPTH if k.startswith(("ln", "mlp")) else DEPTH // 2) for k in (
        "hyb_w_in", "hyb_sinks", "hyb_conv_w", "hyb_a_log", "hyb_dt_bias", "hyb_norm_w", "hyb_w_out",
        "rec_w_in", "rec_conv_w", "rec_conv_b", "rec_w_a", "rec_b_a", "rec_w_x", "rec_b_x", "rec_lambda", "rec_w_out",
        "ln1_g", "ln1_b", "mlp_w1", "mlp_w2", "ln2_g", "ln2_b")}
    order = jnp.zeros((1, 1), f32)
    cot_rows, cot_fn = [(dx, 0, D_MODEL)], None
    for layer in reversed(range(DEPTH)):
        j = layer // 2
        tag = f"L{layer}"
        x0, sv, mix, x1, a, h2, y = saved[layer]
        ln2 = [W["ln2_g"][layer][None, :] + order, W["ln2_b"][layer][None, :]]
        (dy,), (dg2, db2) = _tl_bwd(f"{tag}_dln2", _ln_res_fn, [(x1, 0, D_MODEL), (y, 0, D_MODEL)], ln2,
                                    cot_rows, cot_fn=cot_fn, skip=(0,))
        G["ln2_g"][layer], G["ln2_b"][layer] = dg2[0], db2[0]
        da = _mm(f"{tag}_dmlp2", dy, W["mlp_w2"][layer], "nt", b_kind="devrow", b_lead=0, epilogue=_drelu2_epilogue,
                 extras=(a,), out_dtypes=(bf16,), tm=2048)
        G["mlp_w2"][layer] = _mm(f"{tag}_dw2", h2, dy, "tn", out_dtypes=(bf16,), tm=2048).reshape(N_DEV, -1, D_MODEL)
        dx1 = _mm(f"{tag}_dmlp1", da, W["mlp_w1"][layer], "nt", b_kind="devcol", b_lead=0, tm=2048)
        G["mlp_w1"][layer] = _mm(f"{tag}_dw1", x1, da, "tn", o_kind="devcol", out_dtypes=(bf16,), tn=2048)
        ln1 = [W["ln1_g"][layer][None, :], W["ln1_b"][layer][None, :]]
        (dmix,), (dg1, db1) = _tl_bwd(f"{tag}_dln1", _ln_res_fn, [(x0, 0, D_MODEL), (mix, 0, D_MODEL)], ln1,
                                      [(dx1, 0, D_MODEL), (dy, 0, D_MODEL)], cot_fn=_residual_cot, skip=(0,))
        G["ln1_g"][layer], G["ln1_b"][layer] = dg1[0], db1[0]
        dx0_a = dmix if layer == 0 else None
        early = functools.partial(grads_ready, f"l{layer}_early",
                                  {(k, layer): G[k][layer] for k in ("mlp_w1", "mlp_w2")})
        if layer % 2 == 0:
            dx = _hybrid_bwd(tag, x0, dmix, dx0_a, W, j, cos, sin, sv, G, early)
        else:
            dx = _rec_bwd(tag, x0, dmix, dx0_a, W, j, sv, G, early)
        order = grads_ready(f"l{layer}_late", {}, {(k, i): G[k][i] for k, i in _layer_weights(layer)[:-2]
                                                  if not k.endswith("w_out")})
        cot_rows, cot_fn = [(dx, 0, D_MODEL), (dmix, 0, D_MODEL)], _residual_cot
    big = {k for k, _ in BIG}
    return loss, dx, {k: jnp.stack(v) for k, v in G.items() if k not in big}


def _layer_weights(layer):
    j = layer // 2
    mixer = ["hyb_w_in", "hyb_w_out"] if layer % 2 == 0 else ["rec_w_in", "rec_w_out", "rec_w_a", "rec_w_x"]
    return [(k, j) for k in mixer] + [("mlp_w1", layer), ("mlp_w2", layer)]


def _my_coords():
    return lax.axis_index("x"), lax.axis_index("y"), lax.axis_index("c")


def _all_gather(name, arrays):
    na = len(arrays)

    def body(*refs):
        x_refs, out_refs = refs[:na], refs[na:2 * na]
        send_sems, recv_sems, local_sems = refs[2 * na:]
        x, y, c = _my_coords()
        me, sibling = (x, y, c), (x, y, 1 - c)
        chips = [(1 - x, y), (x, 1 - y), (1 - x, 1 - y)]

        def blk(a, px, py, pc):
            return out_refs[a].at[4 * px + 2 * py + pc]

        def copy(a, k, block, to, src=None):
            return pltpu.make_async_remote_copy(
                src_ref=blk(a, *block) if src is None else src, dst_ref=blk(a, *block),
                send_sem=send_sems.at[a, k], recv_sem=recv_sems.at[a, k],
                device_id=to, device_id_type=pl.DeviceIdType.MESH)

        mine = [pltpu.make_async_copy(x_refs[a], blk(a, *me), local_sems.at[a]) for a in range(na)]
        for cp in mine:
            cp.start()
        first = []
        for a in range(na):
            first.append(copy(a, 0, me, sibling, src=x_refs[a]))
            first += [copy(a, 1 + j, me, (*chip, c), src=x_refs[a]) for j, chip in enumerate(chips)]
        for cp in first:
            cp.start()
        passed = []
        for a in range(na):
            for j, chip in enumerate(chips):
                copy(a, 1 + j, (*chip, c), me).wait_recv()
                passed.append(copy(a, 4 + j, (*chip, c), sibling))
                passed[-1].start()
        for a in range(na):
            copy(a, 0, sibling, me).wait_recv()
            for j, chip in enumerate(chips):
                copy(a, 4 + j, (*chip, 1 - c), me).wait_recv()
        for cp in first + passed:
            cp.wait_send()
        for cp in mine:
            cp.wait()

    return pl.pallas_call(
        body, name=name,
        out_shape=[jax.ShapeDtypeStruct((N_DEV,) + a.shape, a.dtype) for a in arrays],
        in_specs=[pl.BlockSpec(memory_space=pl.ANY)] * na,
        out_specs=[pl.BlockSpec(memory_space=pl.ANY)] * na,
        scratch_shapes=[pltpu.SemaphoreType.DMA((na, 7)), pltpu.SemaphoreType.DMA((na, 7)),
                        pltpu.SemaphoreType.DMA((na,))],
    )(*arrays)


_HBM = pl.BlockSpec(memory_space=pltpu.HBM)
_SEM = pl.BlockSpec(memory_space=pltpu.SEMAPHORE)


def _flip(k, x, y, c):
    return ((1 - x) if k & 4 else x, (1 - y) if k & 2 else y, (1 - c) if k & 1 else c)


_PEERS = {"gather": (1, 2, 4, 6), "scatter": (1, 2, 3, 4, 5, 6, 7)}


def _push_copies(kind, x_refs, land_refs, send_sems, recv_sems, local_sems):
    x, y, c = _my_coords()
    me = 4 * x + 2 * y + c
    peers = _PEERS[kind]
    remote, local = [], []
    for a in range(len(x_refs)):
        local.append(pltpu.make_async_copy(x_refs[a] if kind == "gather" else x_refs[a].at[me], land_refs[a].at[me],
                                           local_sems.at[a]))
        for n, k in enumerate(peers):
            px, py, pc = _flip(k, x, y, c)
            remote.append(pltpu.make_async_remote_copy(
                src_ref=x_refs[a] if kind == "gather" else x_refs[a].at[4 * px + 2 * py + pc],
                dst_ref=land_refs[a].at[me],
                send_sem=send_sems.at[a * len(peers) + n], recv_sem=recv_sems.at[a * len(peers) + n],
                device_id=(px, py, pc), device_id_type=pl.DeviceIdType.MESH))
    return remote, local


def _pass_to_sibling(name, lands):
    na = len(lands)
    chips = (2, 4, 6)

    def body(*refs):
        out_refs, send_sems, recv_sems = refs[na:2 * na], refs[2 * na], refs[2 * na + 1]
        x, y, c = _my_coords()
        cps = []
        for a in range(na):
            for n, k in enumerate(chips):
                px, py, _ = _flip(k, x, y, c)
                cps.append(pltpu.make_async_remote_copy(
                    src_ref=out_refs[a].at[4 * px + 2 * py + c], dst_ref=out_refs[a].at[4 * px + 2 * py + c],
                    send_sem=send_sems.at[a * 3 + n], recv_sem=recv_sems.at[a * 3 + n],
                    device_id=(x, y, 1 - c), device_id_type=pl.DeviceIdType.MESH))
        for cp in cps:
            cp.start()
        for a in range(na):
            for n, k in enumerate(chips):
                px, py, _ = _flip(k, x, y, c)
                blk = out_refs[a].at[4 * px + 2 * py + (1 - c)]
                pltpu.make_async_remote_copy(src_ref=blk, dst_ref=blk, send_sem=send_sems.at[a * 3 + n],
                                             recv_sem=recv_sems.at[a * 3 + n], device_id=(x, y, 1 - c),
                                             device_id_type=pl.DeviceIdType.MESH).wait_recv()
        for cp in cps:
            cp.wait_send()

    return pl.pallas_call(
        body, name=name,
        out_shape=[jax.ShapeDtypeStruct(l.shape, l.dtype) for l in lands],
        in_specs=[pl.BlockSpec(memory_space=pl.ANY)] * na,
        out_specs=[pl.BlockSpec(memory_space=pl.ANY)] * na,
        input_output_aliases={a: a for a in range(na)},
        scratch_shapes=[pltpu.SemaphoreType.DMA((3 * na,)), pltpu.SemaphoreType.DMA((3 * na,))],
    )(*lands)


_SIDE_EFFECT = pltpu.CompilerParams(has_side_effects=pltpu.SideEffectType.DATAFLOW_SIDE_EFFECTING)


def _push_start(name, kind, srcs, lands):
    na = len(srcs)

    def body(*refs):
        remote, local = _push_copies(kind, refs[:na], refs[na:2 * na], *refs[2 * na:2 * na + 3])
        for cp in remote + local:
            cp.start()
        token = refs[-1]
        token[...] = jnp.zeros_like(token)

    arrays = list(srcs) + list(lands)
    n_remote = na * len(_PEERS[kind])
    res = pl.pallas_call(
        body, name=name,
        out_shape=(pltpu.SemaphoreType.DMA((n_remote,)), pltpu.SemaphoreType.DMA((n_remote,)),
                   pltpu.SemaphoreType.DMA((na,)), *[pltpu.HBM(t.shape, t.dtype) for t in arrays],
                   jax.ShapeDtypeStruct((SUBLANE, LANE), f32)),
        in_specs=[_HBM] * (2 * na),
        out_specs=(_SEM, _SEM, _SEM, *[_HBM] * (2 * na), pl.BlockSpec(memory_space=pltpu.VMEM)),
        input_output_aliases={i: 3 + i for i in range(2 * na)},
        compiler_params=_SIDE_EFFECT,
    )(*[pltpu.with_memory_space_constraint(t, pltpu.HBM) for t in arrays])
    return list(res[:3]), res[3:3 + na], res[3 + na:3 + 2 * na], res[-1][:1, :1]


def _push_wait(name, kind, sems, srcs, lands, after):
    na = len(srcs)

    def body(*refs):
        remote, local = _push_copies(kind, refs[:na], refs[na:2 * na], *refs[2 * na:2 * na + 3])
        for cp in remote:
            cp.wait_send()
            cp.wait_recv()
        for cp in local:
            cp.wait()

    arrays = list(srcs) + list(lands)
    res = pl.pallas_call(
        body, name=name,
        out_shape=tuple(pltpu.HBM(t.shape, t.dtype) for t in arrays),
        in_specs=[_HBM] * (2 * na) + [_SEM] * 3 + [pl.BlockSpec(memory_space=pl.ANY)],
        out_specs=tuple([_HBM] * (2 * na)),
        input_output_aliases={i: i for i in range(2 * na)},
        compiler_params=_SIDE_EFFECT,
    )(*arrays, *sems, after)
    return res[na:]


def _sum_blocks(name, land):
    _, R, n = land.shape
    tr = R

    def body(l_ref, o_ref):
        acc = l_ref[0].astype(f32)
        for s in range(1, N_DEV):
            acc = acc + l_ref[s].astype(f32)
        o_ref[...] = acc

    return pl.pallas_call(
        body, name=name, grid=(R // tr,),
        in_specs=[pl.BlockSpec((N_DEV, tr, n), lambda i: (0, i, 0))],
        out_specs=pl.BlockSpec((tr, n), lambda i: (i, 0)),
        out_shape=jax.ShapeDtypeStruct((R, n), f32),
        compiler_params=_cparams("parallel"),
    )(land)


def _adamw(name, w, g, m, v):
    shape = w.shape
    last = shape[-1]
    rows = math.prod(shape[:-1])
    tm = 256 if rows % 256 == 0 and rows > 256 else rows
    w2, g2, m2, v2 = (t.reshape(rows, last) for t in (w, g, m, v))

    def body(w_ref, g_ref, m_ref, v_ref, d_ref, mo_ref, vo_ref):
        gg = g_ref[...]
        mn = ADAM_B1 * m_ref[...] + (1.0 - ADAM_B1) * gg
        vn = ADAM_B2 * v_ref[...] + (1.0 - ADAM_B2) * jnp.square(gg)
        m_hat = mn / (1.0 - ADAM_B1 ** ADAM_STEP)
        v_hat = vn / (1.0 - ADAM_B2 ** ADAM_STEP)
        d_ref[...] = -ADAM_LR * (m_hat / (jnp.sqrt(v_hat) + ADAM_EPS) + ADAM_WD * w_ref[...])
        mo_ref[...] = mn
        vo_ref[...] = vn

    spec = pl.BlockSpec((tm, last), lambda i: (i, 0))
    d, mn, vn = pl.pallas_call(
        body, name=name, grid=(rows // tm,), in_specs=[spec] * 4, out_specs=[spec] * 3,
        out_shape=[jax.ShapeDtypeStruct((rows, last), f32)] * 3,
        compiler_params=_cparams("parallel"),
    )(w2, g2, m2, v2)
    return d.reshape(shape), mn.reshape(shape), vn.reshape(shape)


def _adamw_land(name, lands, w, m, v, tm=256):
    L = len(lands)
    _, R, C = lands[0].shape
    tm = min(tm, R)

    def body(*refs):
        l_refs, (w_ref, m_ref, v_ref, g_ref, d_ref, mo_ref, vo_ref) = refs[:L], refs[L:]
        for k in range(L):
            @pl.when(pl.program_id(0) == k)
            def _(k=k):
                gg = l_refs[k][0].astype(f32)
                for s in range(1, N_DEV):
                    gg = gg + l_refs[k][s].astype(f32)
                g_ref[...] = gg
                mn = ADAM_B1 * m_ref[...] + (1.0 - ADAM_B1) * gg
                vn = ADAM_B2 * v_ref[...] + (1.0 - ADAM_B2) * jnp.square(gg)
                m_hat = mn / (1.0 - ADAM_B1 ** ADAM_STEP)
                v_hat = vn / (1.0 - ADAM_B2 ** ADAM_STEP)
                d_ref[...] = -ADAM_LR * (m_hat / (jnp.sqrt(v_hat) + ADAM_EPS) + ADAM_WD * w_ref[...])
                mo_ref[...] = mn
                vo_ref[...] = vn

    land_specs = [pl.BlockSpec((N_DEV, tm, C), lambda l, i, k=k: (0, jnp.where(l == k, i, 0), 0)) for k in range(L)]
    spec = pl.BlockSpec((None, tm, C), lambda l, i: (l, i, 0))
    return pl.pallas_call(
        body, name=name, grid=(L, R // tm),
        in_specs=land_specs + [spec] * 3,
        out_specs=[spec] * 4,
        out_shape=[jax.ShapeDtypeStruct((L, R, C), f32)] * 4,
        compiler_params=_cparams("arbitrary", "arbitrary"),
    )(*lands, w, m, v)


BIG = [("hyb_w_in", 2), ("hyb_w_out", 1), ("rec_w_in", 2), ("rec_w_out", 1), ("rec_w_a", 2), ("rec_w_x", 2),
       ("mlp_w1", 2), ("mlp_w2", 1)]
SMALL = [("hyb_conv_w", 2), ("rec_conv_w", 2), ("rec_conv_b", 1), ("rec_b_a", 1), ("rec_b_x", 1), ("rec_lambda", 1)]
REPL = ["hyb_sinks", "hyb_a_log", "hyb_dt_bias", "hyb_norm_w", "ln1_g", "ln1_b", "ln2_g", "ln2_b"]
WEIGHTS = ["hyb_w_in", "hyb_sinks", "hyb_conv_w", "hyb_a_log", "hyb_dt_bias", "hyb_norm_w", "hyb_w_out", "rec_w_in",
           "rec_conv_w", "rec_conv_b", "rec_w_a", "rec_b_a", "rec_w_x", "rec_b_x", "rec_lambda", "rec_w_out",
           "ln1_g", "ln1_b", "mlp_w1", "mlp_w2", "ln2_g", "ln2_b"]


def _pack_rows(parts, dtype, row_mult):
    lead = parts[0].shape[:-1]
    flat = jnp.concatenate([p.astype(dtype) for p in parts], axis=-1)
    n = flat.shape[-1]
    unit = row_mult * LANE
    pad = (-n) % unit
    if pad:
        flat = jnp.concatenate([flat, jnp.zeros(lead + (pad,), dtype)], axis=-1)
    return flat.reshape(lead + ((n + pad) // LANE, LANE))


def _gather_full(gathered, shard_shapes, table):
    flat = gathered.reshape(N_DEV, -1)
    out, off = {}, 0
    for name, ax in table:
        shp = shard_shapes[name]
        n = math.prod(shp)
        arr = flat[:, off:off + n].reshape((N_DEV,) + shp)
        off += n
        arr = jnp.moveaxis(arr, 0, ax)
        out[name] = arr.reshape(shp[:ax] + (N_DEV * shp[ax],) + shp[ax + 1:])
    return out


def _matmul_layouts(tag, gw):
    out = {}
    bw = D_MODEL // LRU_BLOCKS
    for k, g in gw.items():
        L = g.shape[1]
        if k == "hyb_w_in":
            out[k] = _merge_cols(f"{tag}_w_in_merge", g)
        elif k in ("hyb_w_out", "rec_w_out"):
            out[k] = jnp.swapaxes(g, 0, 1).reshape(L, D_MODEL, D_MODEL)
        elif k in ("rec_w_a", "rec_w_x"):
            out[k] = jnp.moveaxis(g, 0, 2).reshape(L, LRU_BLOCKS, bw, bw)
        else:
            out[k] = g
    return out


def kernel(x, hyb_w_in, hyb_sinks, hyb_conv_w, hyb_a_log, hyb_dt_bias, hyb_norm_w, hyb_w_out, rec_w_in, rec_conv_w, rec_conv_b, rec_w_a, rec_b_a, rec_w_x, rec_b_x, rec_lambda, rec_w_out, ln1_g, ln1_b, mlp_w1, mlp_w2, ln2_g, ln2_b, loss_target, m_hyb_w_in, m_hyb_sinks, m_hyb_conv_w, m_hyb_a_log, m_hyb_dt_bias, m_hyb_norm_w, m_hyb_w_out, m_rec_w_in, m_rec_conv_w, m_rec_conv_b, m_rec_w_a, m_rec_b_a, m_rec_w_x, m_rec_b_x, m_rec_lambda, m_rec_w_out, m_ln1_g, m_ln1_b, m_mlp_w1, m_mlp_w2, m_ln2_g, m_ln2_b, v_hyb_w_in, v_hyb_sinks, v_hyb_conv_w, v_hyb_a_log, v_hyb_dt_bias, v_hyb_norm_w, v_hyb_w_out, v_rec_w_in, v_rec_conv_w, v_rec_conv_b, v_rec_w_a, v_rec_b_a, v_rec_w_x, v_rec_b_x, v_rec_lambda, v_rec_w_out, v_ln1_g, v_ln1_b, v_mlp_w1, v_mlp_w2, v_ln2_g, v_ln2_b):
    args = locals()
    w = {k: args[k] for k in WEIGHTS}
    m = {k: args["m_" + k] for k in WEIGHTS}
    v = {k: args["v_" + k] for k in WEIGHTS}
    shard_shapes = {k: tuple(t.shape) for k, t in w.items()}
    xi, yi, ci = _my_coords()
    me = 4 * xi + 2 * yi + ci

    in_flight = {}

    def install(tag, names, got):
        for (k, i), arr in zip(names, _matmul_layouts(tag, {k: g for (k, _), g in zip(names, got)}).values()):
            W[k][i] = arr

    def start_gather(tag, names):
        srcs = [w[k][i:i + 1].astype(bf16) for k, i in names]
        *pending, zero = _push_start(f"gather_{tag}_start", "gather", srcs,
                                     [lax.empty((N_DEV,) + s.shape, bf16) for s in srcs])
        in_flight[tag] = (names, pending)
        return zero

    def finish_gather(tag, after):
        names, pending = in_flight.pop(tag)
        half = _push_wait(f"gather_{tag}_wait", "gather", *pending, after)
        install(tag, names, _pass_to_sibling(f"gather_{tag}_pass", half))

    def started(k, zero):
        W[k] = W[k] + zero

    def mixer_w(layer):
        return _layer_weights(layer)[:-2]

    def mlp_w(layer):
        return _layer_weights(layer)[-2:]

    gathered0 = _all_gather("gather_first", [w[k][i:i + 1].astype(bf16) for k, i in mixer_w(0)]
                            + [_pack_rows([w[k].reshape(-1) for k, _ in SMALL], f32, SUBLANE)])
    W = _gather_full(gathered0[-1], shard_shapes, SMALL)
    W.update({k: w[k] for k in REPL})
    W.update({k: {} for k, _ in BIG})
    install("l0a", mixer_w(0), gathered0[:-1])
    started("hyb_sinks", start_gather("l0b", mlp_w(0)) + start_gather("l1a", mixer_w(1)))

    def load_layer(layer, stage, after):
        if stage == 0:
            if layer > 0:
                finish_gather(f"l{layer}a", after)
            if 0 < layer < DEPTH - 1:
                started("hyb_sinks" if layer % 2 == 0 else "rec_conv_b",
                        start_gather(f"l{layer + 1}a", mixer_w(layer + 1)))
        if stage == 2:
            finish_gather(f"l{layer}b", after)
            if layer < DEPTH - 1:
                started("ln2_g", start_gather(f"l{layer + 1}b", mlp_w(layer + 1)))

    grads_in_flight = {}

    def grads_ready(tag, a, b):
        g = {**a, **b}
        srcs = list(g.values())
        *pending, zero = _push_start(f"scatter_{tag}_start", "scatter", srcs, [lax.empty(s.shape, bf16) for s in srcs])
        grads_in_flight[tag] = (list(g.keys()), pending)
        return zero

    loss_local, grad_x, G = _local_step(x[0], loss_target[0], W, load_layer, grads_ready)
    loss = lax.psum(loss_local, MESH_AXES)

    landed = {}

    def land(tag, after):
        keys, pending = grads_in_flight[tag]
        landed.update(zip(keys, _push_wait(f"scatter_{tag}_wait", "scatter", *pending, after)))

    tags = list(grads_in_flight)
    for tag in tags[:-1]:
        land(tag, grad_x)
    rest = _pack_rows([G[k].reshape(-1) for k, _ in SMALL] + [G[k].reshape(-1) for k in REPL], f32, SUBLANE)
    g_rest = _sum_blocks("sum_rest", _all_gather("gather_rest", [rest])[0]).reshape(-1)

    grads, delta, new_m, new_v = {}, {}, {}, {}

    def adamw_big(k):
        shp = shard_shapes[k]
        s3 = (shp[0], math.prod(shp[1:-1]), shp[-1])
        lands = [landed[(k, i)].reshape((N_DEV,) + s3[1:]) for i in range(shp[0])]
        res = _adamw_land("adamw_" + k, lands, w[k].reshape(s3), m[k].reshape(s3), v[k].reshape(s3))
        grads[k], delta[k], new_m[k], new_v[k] = (r.reshape(shp) for r in res)

    late = {k for k, _ in grads_in_flight[tags[-1]][0]}
    for k in [k for k, _ in BIG if k not in late]:
        adamw_big(k)
        done = new_v[k]
    land(tags[-1], done)
    for k in [k for k, _ in BIG if k in late]:
        adamw_big(k)
    off = 0
    for k, ax in SMALL:
        full_shape = G[k].shape
        n = math.prod(full_shape)
        full = g_rest[off:off + n].reshape(full_shape)
        off += n
        s = shard_shapes[k][ax]
        grads[k] = lax.dynamic_slice_in_dim(full, me * s, s, axis=ax)
    for k in REPL:
        n = math.prod(shard_shapes[k])
        grads[k] = g_rest[off:off + n].reshape(shard_shapes[k])
        off += n

    for k in [k for k, _ in SMALL] + REPL:
        delta[k], new_m[k], new_v[k] = _adamw("adamw_" + k, w[k], grads[k], m[k], v[k])

    return (loss, grad_x[None], *[grads[k] for k in WEIGHTS], *[delta[k] for k in WEIGHTS],
            *[new_m[k] for k in WEIGHTS], *[new_v[k] for k in WEIGHTS])
```

```python
import functools
import math

import jax
import jax.numpy as jnp
from jax import lax
from jax.experimental import pallas as pl
from jax.experimental.pallas import tpu as pltpu

f32 = jnp.float32
bf16 = jnp.bfloat16

N_DEV = 8
D_MODEL = 1024
DEPTH = 4
A_HEAD_DIM = 64
A_Q_HEADS = 8
WINDOW = 128
ROPE_THETA = 10000.0
B_HEADS = 4
B_HEAD_DIM = 128
B_CHUNK = 64
LRU_BLOCKS = 4
LRU_C = 8.0
D_FF = 4 * D_MODEL
HYB_PROJ = 2824
HYB_PROJ_PAD = 3072
DN_ALPHA = (2 * DEPTH) ** 0.25
LN_EPS = 1e-5
NORM_EPS = 1e-6
ADAM_LR = 0.001
ADAM_B1 = 0.9
ADAM_B2 = 0.999
ADAM_EPS = 1e-08
ADAM_WD = 0.01
ADAM_STEP = 10

LANE = 128
SUBLANE = 8
VMEM_LIMIT = 48 * 1024 * 1024

CB_QA, CB_KA, CB_VA, CB_CONV, CB_Z, CB_LG = 0, 4, 5, 6, 18, 22

MESH_AXES = ("x", "y", "c")


def _cparams(*sem):
    return pltpu.CompilerParams(dimension_semantics=sem, vmem_limit_bytes=VMEM_LIMIT)


def _dot(a, b, dims, precision=None):
    return lax.dot_general(a, b, (dims, ((), ())), preferred_element_type=f32, precision=precision)


NN = ((1,), (0,))
NT = ((1,), (1,))
TN = ((0,), (0,))


def _mat_spec(arr, kind, lead, br, bc, rb, cb):
    if kind == "plain":
        return pl.BlockSpec((br, bc), lambda i, j, k: (rb(i, j, k), cb(i, j, k)))
    if kind == "lead":
        return pl.BlockSpec((None, br, bc), lambda i, j, k: (lead, rb(i, j, k), cb(i, j, k)))
    if kind == "devcol":
        assert bc == arr.shape[-1]
        return pl.BlockSpec((None, None, br, bc), lambda i, j, k: (cb(i, j, k), lead, rb(i, j, k), 0))
    assert kind == "devrow" and br == arr.shape[-2]
    return pl.BlockSpec((None, None, br, bc), lambda i, j, k: (rb(i, j, k), lead, 0, cb(i, j, k)))


def _mm(name, a, b, mode, *, b_kind="plain", b_lead=0, o_kind="plain", epilogue=None, extras=(), params=(),
        out_dtypes=(f32,), tm=1024, tn=1024, tk=None):
    if tk is None:
        tk = 512 if mode == "tn" else 1024
    if b_kind in ("plain", "lead"):
        b_rows, b_cols = b.shape[-2:]
    elif b_kind == "devcol":
        b_rows, b_cols = b.shape[-2], N_DEV * b.shape[-1]
    else:
        b_rows, b_cols = N_DEV * b.shape[-2], b.shape[-1]
    if mode == "nn":
        (M, K), (K2, N) = a.shape, (b_rows, b_cols)
    elif mode == "nt":
        (M, K), (N, K2) = a.shape, (b_rows, b_cols)
    else:
        (K, M), (K2, N) = a.shape, (b_rows, b_cols)
    assert K == K2, (name, a.shape, b.shape, mode)
    tm, tn, tk = min(tm, M), min(tn, N), min(tk, K)
    cols_are_n = mode != "nt"
    if b_kind == "devcol":
        tn, tk = (b.shape[-1], tk) if cols_are_n else (tn, b.shape[-1])
    if b_kind == "devrow":
        tn, tk = (tn, b.shape[-2]) if cols_are_n else (b.shape[-2], tk)
    shard = N // N_DEV
    if o_kind == "devcol":
        tn = max(shard, tn // shard * shard)
    assert M % tm == 0 and N % tn == 0 and K % tk == 0, (name, M, N, K, tm, tn, tk)
    nk = K // tk
    dims = {"nn": NN, "nt": NT, "tn": TN}[mode]
    n_ex, n_out = len(extras) + len(params), len(out_dtypes)

    def body(*refs):
        a_ref, b_ref = refs[:2]
        ex = refs[2:2 + n_ex]
        outs = refs[2 + n_ex:2 + n_ex + n_out]
        acc = refs[-1]
        k = pl.program_id(2)

        @pl.when(k == 0)
        def _():
            acc[...] = jnp.zeros_like(acc)

        acc[...] += _dot(a_ref[...].astype(bf16), b_ref[...].astype(bf16), dims)

        @pl.when(k == nk - 1)
        def _():
            r = acc[...]
            res = epilogue(r, *[e[...] for e in ex]) if epilogue is not None else (r,)
            for o, v in zip(outs, res):
                if o_kind == "plain":
                    o[...] = v.astype(o.dtype)
                else:
                    for q in range(tn // shard):
                        o[q] = v[:, q * shard:(q + 1) * shard].astype(o.dtype)

    if mode == "tn":
        a_spec = pl.BlockSpec((tk, tm), lambda i, j, k: (k, i))
    else:
        a_spec = pl.BlockSpec((tm, tk), lambda i, j, k: (i, k))
    jb, kb = (lambda i, j, k: j), (lambda i, j, k: k)
    if mode == "nt":
        b_spec = _mat_spec(b, b_kind, b_lead, tn, tk, jb, kb)
    else:
        b_spec = _mat_spec(b, b_kind, b_lead, tk, tn, kb, jb)
    e_spec = pl.BlockSpec((tm, tn), lambda i, j, k: (i, j))
    if o_kind == "plain":
        o_spec, o_shape = e_spec, (M, N)
    else:
        o_spec, o_shape = pl.BlockSpec((tn // shard, tm, shard), lambda i, j, k: (j, i, 0)), (N_DEV, M, shard)
    res = pl.pallas_call(
        body, name=name,
        grid=(M // tm, N // tn, nk),
        in_specs=[a_spec, b_spec] + [e_spec] * len(extras)
        + [pl.BlockSpec(p.shape, lambda i, j, k: (0, 0)) for p in params],
        out_specs=[o_spec] * n_out,
        out_shape=[jax.ShapeDtypeStruct(o_shape, dt) for dt in out_dtypes],
        scratch_shapes=[pltpu.VMEM((tm, tn), f32)],
        compiler_params=_cparams("parallel", "parallel", "arbitrary"),
    )(a, b, *extras, *params)
    return res[0] if n_out == 1 else res


def _row_spec(tm, cb, width):
    assert (cb * LANE) % width == 0
    blk = (cb * LANE) // width
    return pl.BlockSpec((tm, width), lambda i: (i, blk))


def _whole_spec(p):
    nd = p.ndim
    return pl.BlockSpec(p.shape, lambda i: (0,) * nd)


def _tl_fwd(name, fn, rows, params, out_widths, out_dtypes, tm=256):
    T = rows[0][0].shape[0]
    tm = min(tm, T)
    nr, npar = len(rows), len(params)

    def body(*refs):
        vals = [r[...] for r in refs[:nr + npar]]
        outs = fn(*vals)
        for o, v in zip(refs[nr + npar:], outs):
            o[...] = v.astype(o.dtype)

    res = pl.pallas_call(
        body, name=name, grid=(T // tm,),
        in_specs=[_row_spec(tm, cb, w) for (_, cb, w) in rows] + [_whole_spec(p) for p in params],
        out_specs=[pl.BlockSpec((tm, w), lambda i: (i, 0)) for w in out_widths],
        out_shape=[jax.ShapeDtypeStruct((T, w), dt) for w, dt in zip(out_widths, out_dtypes)],
        compiler_params=_cparams("parallel"),
    )(*[r[0] for r in rows], *params)
    return res


def _tl_bwd(name, fn, rows, params, cot_rows, cot_fn=None, skip=(), tm=256):
    T = rows[0][0].shape[0]
    tm = min(tm, T)
    nr, npar, nc = len(rows), len(params), len(cot_rows)
    keep = [k for k in range(nr) if k not in skip]

    def body(*refs):
        vals = [r[...] for r in refs[:nr + npar]]
        cots = [r[...] for r in refs[nr + npar:nr + npar + nc]]
        outs = refs[nr + npar + nc:]
        cot = tuple(cot_fn(*cots)) if cot_fn is not None else tuple(cots)
        _, vjp = jax.vjp(fn, *vals)
        grads = vjp(cot)
        for o, k in zip(outs, keep):
            o[...] = grads[k].astype(o.dtype)
        i = pl.program_id(0)
        for o, g in zip(outs[len(keep):], grads[nr:]):
            @pl.when(i == 0)
            def _(o=o):
                o[...] = jnp.zeros_like(o)
            o[...] += g

    res = pl.pallas_call(
        body, name=name, grid=(T // tm,),
        in_specs=[_row_spec(tm, cb, w) for (_, cb, w) in rows] + [_whole_spec(p) for p in params]
        + [_row_spec(tm, cb, w) for (_, cb, w) in cot_rows],
        out_specs=[pl.BlockSpec((tm, rows[k][2]), lambda i: (i, 0)) for k in keep] + [_whole_spec(p) for p in params],
        out_shape=[jax.ShapeDtypeStruct((T, rows[k][2]), f32) for k in keep]
        + [jax.ShapeDtypeStruct(p.shape, f32) for p in params],
        compiler_params=_cparams("arbitrary"),
    )(*[r[0] for r in rows], *params, *[r[0] for r in cot_rows])
    return res[:len(keep)], res[len(keep):]


def _ln_res_fn(x, mix, g, b):
    pre = DN_ALPHA * x + mix
    mu = jnp.mean(pre, axis=-1, keepdims=True)
    var = jnp.mean(jnp.square(pre - mu), axis=-1, keepdims=True)
    return ((pre - mu) * lax.rsqrt(var + LN_EPS) * g + b,)


@jax.custom_jvp
def _expm1(x):
    small = jnp.abs(x) < 0.3
    xs = jnp.where(small, x, 0.0)
    poly = xs * (1.0 + xs * (1 / 2 + xs * (1 / 6 + xs * (1 / 24 + xs * (1 / 120 + xs * (
        1 / 720 + xs * (1 / 5040 + xs * (1 / 40320 + xs * (1 / 362880)))))))))
    return jnp.where(small, poly, jnp.exp(x) - 1.0)


@_expm1.defjvp
def _expm1_jvp(primals, tangents):
    (x,), (t,) = primals, tangents
    return _expm1(x), t * jnp.exp(x)


def _rglru_pre_fn(pre_r, pre_i, xc, b_a, b_x, lam):
    r = jax.nn.sigmoid(pre_r + b_a)
    i = jax.nn.sigmoid(pre_i + b_x)
    log_a = -LRU_C * r * jax.nn.softplus(-lam)
    a = jnp.exp(log_a)
    b = jnp.sqrt(-_expm1(2.0 * log_a)) * (i * xc)
    return a, b


def _rec_gate_fn(h, gate):
    return (h * jax.nn.gelu(gate),)


def _loss_head(y, t, tm=256):
    T, Dm = y.shape

    def body(y_ref, t_ref, dy_ref, loss_ref):
        e = y_ref[...] - t_ref[...]
        dy_ref[...] = e * (1.0 / Dm)

        @pl.when(pl.program_id(0) == 0)
        def _():
            loss_ref[...] = jnp.zeros_like(loss_ref)

        loss_ref[...] += 0.5 * jnp.sum(jnp.mean(e * e, axis=-1, keepdims=True), axis=0, keepdims=True)

    dy, loss = pl.pallas_call(
        body, name="loss_head", grid=(T // tm,),
        in_specs=[pl.BlockSpec((tm, Dm), lambda i: (i, 0))] * 2,
        out_specs=[pl.BlockSpec((tm, Dm), lambda i: (i, 0)), pl.BlockSpec((SUBLANE, LANE), lambda i: (0, 0))],
        out_shape=[jax.ShapeDtypeStruct((T, Dm), f32), jax.ShapeDtypeStruct((SUBLANE, LANE), f32)],
        compiler_params=_cparams("arbitrary"),
    )(y, t)
    return loss[0, 0], dy


def _conv_fwd(name, x, cb0, nblk, w, bias, tm=2048):
    T = x.shape[0]
    tm = min(tm, T)
    hb = tm // SUBLANE
    has_b = bias is not None

    def body(*refs):
        cur, prev, w_ref = refs[:3]
        b_ref = refs[3] if has_b else None
        o = refs[-1]
        i = pl.program_id(1)
        p = jnp.where(i > 0, prev[...], 0.0)
        xcat = jnp.concatenate([p, cur[...]], axis=0)
        acc = cur[...] * w_ref[3:4, :]
        for j in range(3):
            acc = acc + pltpu.roll(xcat, 3 - j, axis=0)[SUBLANE:] * w_ref[j:j + 1, :]
        if has_b:
            acc = acc + b_ref[...]
        o[...] = acc

    in_specs = [
        pl.BlockSpec((tm, LANE), lambda c, i: (i, cb0 + c)),
        pl.BlockSpec((SUBLANE, LANE), lambda c, i: (jnp.maximum(i * hb - 1, 0), cb0 + c)),
        pl.BlockSpec((4, LANE), lambda c, i: (0, c)),
    ]
    args = [x, x, w]
    if has_b:
        in_specs.append(pl.BlockSpec((1, LANE), lambda c, i: (0, c)))
        args.append(bias)
    return pl.pallas_call(
        body, name=name, grid=(nblk, T // tm),
        in_specs=in_specs,
        out_specs=pl.BlockSpec((tm, LANE), lambda c, i: (i, c)),
        out_shape=jax.ShapeDtypeStruct((T, nblk * LANE), f32),
        compiler_params=_cparams("parallel", "parallel"),
    )(*args)


def _conv_bwd(name, dy, x, cb0, nblk, w, into, into_cb, tm=2048):
    T = x.shape[0]
    tm = min(tm, T)
    hb = tm // SUBLANE
    nt = T // tm

    def body(dcur, dnext, xcur, xprev, w_ref, _, dx_ref, dw_ref, db_ref):
        i = pl.program_id(1)
        d = dcur[...]
        dn = jnp.where(i < nt - 1, dnext[...], 0.0)
        dcat = jnp.concatenate([d, dn], axis=0)
        acc = d * w_ref[3:4, :]
        for j in range(3):
            s = 3 - j
            acc = acc + pltpu.roll(dcat, tm + SUBLANE - s, axis=0)[:tm] * w_ref[j:j + 1, :]
        dx_ref[...] = acc

        p = jnp.where(i > 0, xprev[...], 0.0)
        xcat = jnp.concatenate([p, xcur[...]], axis=0)
        rows = [jnp.sum(d * pltpu.roll(xcat, 3 - j, axis=0)[SUBLANE:], axis=0, keepdims=True) for j in range(3)]
        rows.append(jnp.sum(d * xcur[...], axis=0, keepdims=True))
        rows.append(jnp.zeros((SUBLANE - 4, LANE), f32))

        @pl.when(i == 0)
        def _():
            dw_ref[...] = jnp.zeros_like(dw_ref)
            db_ref[...] = jnp.zeros_like(db_ref)

        dw_ref[...] += jnp.concatenate(rows, axis=0)
        db_ref[...] += jnp.broadcast_to(jnp.sum(d, axis=0, keepdims=True), (SUBLANE, LANE))

    nh = T // SUBLANE
    dx, dw, db = pl.pallas_call(
        body, name=name, grid=(nblk, nt),
        in_specs=[
            pl.BlockSpec((tm, LANE), lambda c, i: (i, c)),
            pl.BlockSpec((SUBLANE, LANE), lambda c, i: (jnp.minimum((i + 1) * hb, nh - 1), c)),
            pl.BlockSpec((tm, LANE), lambda c, i: (i, cb0 + c)),
            pl.BlockSpec((SUBLANE, LANE), lambda c, i: (jnp.maximum(i * hb - 1, 0), cb0 + c)),
            pl.BlockSpec((4, LANE), lambda c, i: (0, c)),
            pl.BlockSpec(memory_space=pl.ANY),
        ],
        out_specs=[
            pl.BlockSpec((tm, LANE), lambda c, i: (i, into_cb + c)),
            pl.BlockSpec((SUBLANE, LANE), lambda c, i: (0, c)),
            pl.BlockSpec((SUBLANE, LANE), lambda c, i: (0, c)),
        ],
        out_shape=[jax.ShapeDtypeStruct(into.shape, f32),
                   jax.ShapeDtypeStruct((SUBLANE, nblk * LANE), f32),
                   jax.ShapeDtypeStruct((SUBLANE, nblk * LANE), f32)],
        input_output_aliases={5: 0},
        compiler_params=_cparams("parallel", "arbitrary"),
    )(dy, dy, x, x, w, into)
    return dx, dw[:4], db[0]


@functools.partial(jax.custom_vjp, nondiff_argnums=(1,))
def _lroll(x, s):
    return pltpu.roll(x, s, axis=1)


def _lroll_fwd(x, s):
    return _lroll(x, s), None


def _lroll_bwd(s, _, g):
    return (_lroll(g, (LANE - s) % LANE),)


_lroll.defvjp(_lroll_fwd, _lroll_bwd)


def _rope_tables(T):
    half = A_HEAD_DIM // 2
    inv_freq = ROPE_THETA ** (-jnp.arange(half, dtype=f32) / half)
    ang = jnp.arange(T, dtype=f32)[:, None] * inv_freq[None, :]
    cos, sin = jnp.cos(ang), jnp.sin(ang)
    return jnp.tile(jnp.concatenate([cos, cos], axis=1), (1, 2)), jnp.tile(jnp.concatenate([-sin, sin], axis=1), (1, 2))


def _attn_block_fn(n, q, kp, kc, vp, vc, cq, sq, cp, sp, sinks):
    W = WINDOW
    lane = lax.broadcasted_iota(jnp.int32, (W, LANE), 1)
    lo_half = (lane % A_HEAD_DIM) < (A_HEAD_DIM // 2)
    lane8 = lax.broadcasted_iota(jnp.int32, sinks.shape, 1)

    def rope(x, c, s):
        return x * c + jnp.where(lo_half, _lroll(x, LANE - A_HEAD_DIM // 2), _lroll(x, A_HEAD_DIM // 2)) * s

    k2 = jnp.concatenate([rope(kp, cp, sp), rope(kc, cq, sq)], axis=0).astype(bf16)
    v2 = jnp.concatenate([vp, vc], axis=0).astype(bf16)
    qs = []
    for t in range(4):
        qt = rope(q[:, LANE * t:LANE * (t + 1)], cq, sq)
        g = t // 2
        for hh in range(2):
            qa = jnp.where((lane // A_HEAD_DIM) == hh, qt, 0.0)
            qs.append(_lroll(qa, A_HEAD_DIM) if hh != g else qa)
    s_all = _dot(jnp.concatenate(qs, axis=0).astype(bf16), k2, NT) * (A_HEAD_DIM ** -0.5)
    row = lax.broadcasted_iota(jnp.int32, (W, 2 * W), 0)
    col = lax.broadcasted_iota(jnp.int32, (W, 2 * W), 1)
    dist = row + W - col
    mask = (dist >= 0) & (dist < W) & ((col >= W) | (n > 0))
    ps = []
    for j in range(A_Q_HEADS):
        s = jnp.where(mask, s_all[W * j:W * (j + 1)], -jnp.inf)
        sink = jnp.sum(jnp.where(lane8 == j, sinks, 0.0), axis=1, keepdims=True)
        m = jnp.maximum(jnp.max(s, axis=-1, keepdims=True), sink)
        e = jnp.exp(s - m)
        ps.append((e / (jnp.sum(e, axis=-1, keepdims=True) + jnp.exp(sink - m))).astype(bf16))
    o = _dot(jnp.concatenate(ps, axis=0), v2, NN)
    outs = []
    for t in range(4):
        g = t // 2
        ot = jnp.zeros((W, LANE), f32)
        for hh in range(2):
            j = 2 * t + hh
            oj = jnp.where((lane // A_HEAD_DIM) == g, o[W * j:W * (j + 1)], 0.0)
            ot = ot + (_lroll(oj, A_HEAD_DIM) if hh != g else oj)
        outs.append(ot)
    return jnp.concatenate(outs, axis=1)


def _attn_specs():
    W = WINDOW
    prev = lambda n: jnp.maximum(n - 1, 0)
    return [
        pl.BlockSpec((W, 4 * LANE), lambda n: (n, CB_QA // 4)),
        pl.BlockSpec((W, LANE), lambda n: (prev(n), CB_KA)),
        pl.BlockSpec((W, LANE), lambda n: (n, CB_KA)),
        pl.BlockSpec((W, LANE), lambda n: (prev(n), CB_VA)),
        pl.BlockSpec((W, LANE), lambda n: (n, CB_VA)),
        pl.BlockSpec((W, LANE), lambda n: (n, 0)),
        pl.BlockSpec((W, LANE), lambda n: (n, 0)),
        pl.BlockSpec((W, LANE), lambda n: (prev(n), 0)),
        pl.BlockSpec((W, LANE), lambda n: (prev(n), 0)),
        pl.BlockSpec((1, A_Q_HEADS), lambda n: (0, 0)),
    ]


def _attn_fwd(name, proj, cos, sin, sinks):
    T = proj.shape[0]
    W = WINDOW

    def body(*refs):
        o = refs[-1]
        o[...] = _attn_block_fn(pl.program_id(0), *[r[...] for r in refs[:-1]])

    return pl.pallas_call(
        body, name=name, grid=(T // W,),
        in_specs=_attn_specs(),
        out_specs=pl.BlockSpec((W, 4 * LANE), lambda n: (n, 0)),
        out_shape=jax.ShapeDtypeStruct((T, 2 * 4 * LANE), f32),
        compiler_params=_cparams("parallel"),
    )(proj, proj, proj, proj, proj, cos, sin, cos, sin, sinks)


def _attn_bwd(name, proj, cos, sin, sinks, d_oab):
    T = proj.shape[0]
    W = WINDOW
    Q = 4 * LANE

    def body(*refs):
        ins = [r[...] for r in refs[:10]]
        do = refs[10][...]
        d_ref, ds_ref = refs[11:]
        n = pl.program_id(0)
        _, vjp = jax.vjp(functools.partial(_attn_block_fn, n), *ins)
        dq, dkp, dkc, dvp, dvc, _, _, _, _, dsk = vjp(do)

        @pl.when(n == 0)
        def _():
            d_ref[:, Q:] = jnp.zeros((T, 2 * LANE), f32)
            ds_ref[...] = jnp.zeros_like(ds_ref)

        cur = pl.ds(pl.multiple_of(n * W, W), W)
        d_ref[cur, :Q] = dq
        d_ref[cur, Q:Q + LANE] += dkc
        d_ref[cur, Q + LANE:] += dvc
        ds_ref[...] += dsk

        @pl.when(n > 0)
        def _():
            prv = pl.ds(pl.multiple_of((n - 1) * W, W), W)
            d_ref[prv, Q:Q + LANE] += dkp
            d_ref[prv, Q + LANE:] += dvp

    return pl.pallas_call(
        body, name=name, grid=(T // W,),
        in_specs=_attn_specs() + [pl.BlockSpec((W, Q), lambda n: (n, 0))],
        out_specs=[pl.BlockSpec((T, Q + 2 * LANE), lambda n: (0, 0)),
                   pl.BlockSpec((1, A_Q_HEADS), lambda n: (0, 0))],
        out_shape=[jax.ShapeDtypeStruct((T, HYB_PROJ_PAD), f32), jax.ShapeDtypeStruct((1, A_Q_HEADS), f32)],
        compiler_params=_cparams("arbitrary"),
    )(proj, proj, proj, proj, proj, cos, sin, cos, sin, sinks, d_oab)


def _bdot(spec, a, b, precision=None):
    return jnp.einsum(spec, a, b, preferred_element_type=f32, precision=precision)


@jax.custom_vjp
def _tri_inv(a):
    H, C, _ = a.shape
    B = 2 * SUBLANE
    nb = C // B
    r = lax.broadcasted_iota(jnp.int32, (C, C), 0)
    c = lax.broadcasted_iota(jnp.int32, (C, C), 1)
    a4 = jnp.where((r // B) == (c // B), a, 0.0).reshape(H, nb, B, C)
    t4 = jnp.broadcast_to(jnp.where(r == c, 1.0, 0.0).astype(f32), a.shape).reshape(H, nb, B, C)
    for j in range(B - 1):
        col = jnp.concatenate([a4[:, b:b + 1, :, B * b + j:B * b + j + 1] for b in range(nb)], axis=1)
        t4 = t4 - col * t4[:, :, j:j + 1, :]
    x = t4.reshape(H, C, C)
    hi = lax.Precision.HIGH
    while B < C:
        m = jnp.where(((r // (2 * B)) == (c // (2 * B))) & ((r // B) > (c // B)), a, 0.0)
        x = x - _bdot("hij,hjk->hik", x, _bdot("hij,hjk->hik", m, x, precision=hi), precision=hi)
        B *= 2
    return x


def _tri_inv_fwd(a):
    t = _tri_inv(a)
    return t, t


def _tri_inv_bwd(t, g):
    C = t.shape[-1]
    r = lax.broadcasted_iota(jnp.int32, (C, C), 0)
    c = lax.broadcasted_iota(jnp.int32, (C, C), 1)
    x = _bdot("hki,hkj->hij", t, g, precision=lax.Precision.HIGHEST)
    y = _bdot("hik,hjk->hij", x, t, precision=lax.Precision.HIGHEST)
    return (jnp.where(r > c, -y, 0.0),)


_tri_inv.defvjp(_tri_inv_fwd, _tri_inv_bwd)


@jax.custom_vjp
def _tri_inv_saved(a, t):
    return t


_tri_inv_saved.defvjp(lambda a, t: (t, t), lambda t, g: (_tri_inv_bwd(t, g)[0], jnp.zeros_like(t)))


def _silu(x):
    return x * jax.nn.sigmoid(x)


def _l2n(x):
    return x * lax.rsqrt(jnp.sum(x * x, axis=-1, keepdims=True) + NORM_EPS)


def _delta_chunk_fn(cq, ck, cv, z, lg, a_log, dt_bias, norm_w, S, t_saved=None, want_t=False):
    C = B_CHUNK
    lane = lax.broadcasted_iota(jnp.int32, (C, LANE), 1)
    pick = lambda l0: jnp.concatenate(
        [jnp.sum(jnp.where(lane == l0 + h, lg, 0.0), axis=1, keepdims=True)[None] for h in range(B_HEADS)], axis=0)
    bl, al = pick(0), pick(B_HEADS)
    q = _l2n(_silu(cq)) * (B_HEAD_DIM ** -0.5)
    k = _l2n(_silu(ck))
    v = _silu(cv)
    beta = jax.nn.sigmoid(bl)
    g = -jnp.exp(a_log) * jax.nn.softplus(al + dt_bias)
    r = lax.broadcasted_iota(jnp.int32, (C, C), 0)
    c = lax.broadcasted_iota(jnp.int32, (C, C), 1)
    eye = r == c
    g_row = jnp.sum(jnp.where(eye, g, 0.0), axis=1, keepdims=True)
    gc = jnp.sum(jnp.where(c <= r, g_row, 0.0), axis=2, keepdims=True)
    gc_row = jnp.sum(jnp.where(eye, gc, 0.0), axis=1, keepdims=True)
    decay_incl = jnp.exp(jnp.where(r >= c, gc - gc_row, -jnp.inf))
    decay_strict = jnp.where(r > c, decay_incl, 0.0)
    kb = k * beta
    vb = v * beta
    kbf = k.astype(bf16)
    a_mat = _bdot("hik,hjk->hij", kb.astype(bf16), kbf) * decay_strict
    t_f32 = _tri_inv(a_mat) if t_saved is None else _tri_inv_saved(a_mat, t_saved)
    t_mat = t_f32.astype(bf16)
    eg = jnp.exp(gc)
    u = _bdot("hij,hjv->hiv", t_mat, vb.astype(bf16))
    w = _bdot("hij,hjk->hik", t_mat, (kb * eg).astype(bf16))
    qk = _bdot("hik,hjk->hij", q.astype(bf16), kbf) * decay_incl
    g_last = jnp.sum(g, axis=1, keepdims=True)
    k_tail = k * jnp.exp(g_last - gc)
    Sb = S.astype(bf16)
    v_new = u - _bdot("hck,hkv->hcv", w.astype(bf16), Sb)
    o = _bdot("hck,hkv->hcv", (q * eg).astype(bf16), Sb) + _bdot("hij,hjv->hiv", qk.astype(bf16), v_new.astype(bf16))
    S_new = S * jnp.exp(g_last) + _bdot("hck,hcv->hkv", k_tail.astype(bf16), v_new.astype(bf16))
    ob = o * lax.rsqrt(jnp.mean(o * o, axis=-1, keepdims=True) + NORM_EPS) * norm_w
    return (ob * _silu(z), S_new) + ((t_f32,) if want_t else ())


DELTA_CHUNKS_PER_STEP = 2


def _delta_in_specs(rev, N):
    C = DELTA_CHUNKS_PER_STEP * B_CHUNK
    ix = (lambda n: N - 1 - n) if rev else (lambda n: n)
    specs = [pl.BlockSpec((C, 3 * B_HEADS * LANE), lambda n: (ix(n), 0))]
    specs += [pl.BlockSpec((C, LANE), lambda n, h=h: (ix(n), CB_Z + h)) for h in range(B_HEADS)]
    specs += [
        pl.BlockSpec((C, LANE), lambda n: (ix(n), CB_LG)),
        pl.BlockSpec((B_HEADS, 1, 1), lambda n: (0, 0, 0)),
        pl.BlockSpec((B_HEADS, 1, 1), lambda n: (0, 0, 0)),
        pl.BlockSpec((1, LANE), lambda n: (0, 0)),
    ]
    return specs


def _delta_inputs(u, c_ref, z_refs, lg, al, dt, nw):
    H = B_HEADS
    rows = slice(u * B_CHUNK, (u + 1) * B_CHUNK)
    part = lambda p: jnp.stack([c_ref[rows, LANE * (p * H + h):LANE * (p * H + h + 1)] for h in range(H)])
    return (part(0), part(1), part(2), jnp.stack([z[rows, :] for z in z_refs]), lg[rows, :], al[...], dt[...], nw[...])


def _delta_fwd(name, c, proj, a_log, dt_bias, norm_w, o_ab):
    T = c.shape[0]
    C = B_CHUNK
    N = T // C
    Dh = B_HEAD_DIM
    H = B_HEADS

    def body(*refs):
        c_ref, z_refs, (lg, al, dt, nw) = refs[0], refs[1:1 + H], refs[1 + H:5 + H]
        o_ref, s_ref, t_ref, S = refs[6 + H:]

        @pl.when(pl.program_id(0) == 0)
        def _():
            S[...] = jnp.zeros_like(S)

        s = S[...]
        for u in range(U):
            s_ref[:, u] = s
            ob, s, t = _delta_chunk_fn(*_delta_inputs(u, c_ref, z_refs, lg, al, dt, nw), s, want_t=True)
            for h in range(H):
                o_ref[u * C:(u + 1) * C, LANE * h:LANE * (h + 1)] = ob[h]
            t_ref[:, u] = t
        S[...] = s

    U = DELTA_CHUNKS_PER_STEP
    return pl.pallas_call(
        body, name=name, grid=(N // U,),
        in_specs=_delta_in_specs(False, N // U) + [pl.BlockSpec(memory_space=pl.ANY)],
        out_specs=[pl.BlockSpec((U * C, H * LANE), lambda n: (n, 1)),
                   pl.BlockSpec((H, U, Dh, Dh), lambda n: (0, n, 0, 0)),
                   pl.BlockSpec((H, U, C, C), lambda n: (0, n, 0, 0))],
        out_shape=[jax.ShapeDtypeStruct(o_ab.shape, f32), jax.ShapeDtypeStruct((H, N, Dh, Dh), f32),
                   jax.ShapeDtypeStruct((H, N, C, C), f32)],
        input_output_aliases={5 + H: 0},
        scratch_shapes=[pltpu.VMEM((H, Dh, Dh), f32)],
        compiler_params=_cparams("arbitrary"),
    )(c, *([proj] * H), proj, a_log, dt_bias, norm_w, o_ab)


def _delta_bwd(name, c, proj, a_log, dt_bias, norm_w, s_saved, t_saved, d_oab, dproj):
    T = c.shape[0]
    C = B_CHUNK
    N = T // C
    Dh = B_HEAD_DIM
    H = B_HEADS

    def body(*refs):
        c_ref, z_refs, (lg, al, dt, nw) = refs[0], refs[1:1 + H], refs[1 + H:5 + H]
        s_ref, t_ref, do_ref = refs[5 + H:8 + H]
        dc, dtail, dal, ddt, dnw, dS = refs[9 + H:]

        @pl.when(pl.program_id(0) == 0)
        def _():
            dS[...] = jnp.zeros_like(dS)
            dal[...] = jnp.zeros_like(dal)
            ddt[...] = jnp.zeros_like(ddt)
            dnw[...] = jnp.zeros_like(dnw)

        ds = dS[...]
        for u in reversed(range(U)):
            rows = slice(u * C, (u + 1) * C)
            _, vjp = jax.vjp(functools.partial(_delta_chunk_fn, t_saved=t_ref[:, u]),
                             *_delta_inputs(u, c_ref, z_refs, lg, al, dt, nw), s_ref[:, u])
            do = jnp.stack([do_ref[rows, LANE * h:LANE * (h + 1)] for h in range(H)])
            g = vjp((do, ds))
            for h in range(H):
                for p in range(3):
                    dc[rows, LANE * (p * H + h):LANE * (p * H + h + 1)] = g[p][h]
                dtail[rows, LANE * h:LANE * (h + 1)] = g[3][h]
            dtail[rows, LANE * H:LANE * (H + 1)] = g[4]
            dtail[rows, LANE * (H + 1):] = jnp.zeros((C, LANE), f32)
            dal[...] += g[5]
            ddt[...] += g[6]
            dnw[...] += g[7]
            ds = g[8]
        dS[...] = ds

    U = DELTA_CHUNKS_PER_STEP
    NB = N // U
    rn = lambda n: NB - 1 - n
    return pl.pallas_call(
        body, name=name, grid=(NB,),
        in_specs=_delta_in_specs(True, NB) + [
            pl.BlockSpec((H, U, Dh, Dh), lambda n: (0, rn(n), 0, 0)),
            pl.BlockSpec((H, U, C, C), lambda n: (0, rn(n), 0, 0)),
            pl.BlockSpec((U * C, H * LANE), lambda n: (rn(n), 1)),
            pl.BlockSpec(memory_space=pl.ANY),
        ],
        out_specs=[
            pl.BlockSpec((U * C, 3 * H * LANE), lambda n: (rn(n), 0)),
            pl.BlockSpec((U * C, (H + 2) * LANE), lambda n: (rn(n), CB_Z // (H + 2))),
            pl.BlockSpec((H, 1, 1), lambda n: (0, 0, 0)),
            pl.BlockSpec((H, 1, 1), lambda n: (0, 0, 0)),
            pl.BlockSpec((1, LANE), lambda n: (0, 0)),
        ],
        out_shape=[jax.ShapeDtypeStruct((T, 3 * H * Dh), f32), jax.ShapeDtypeStruct(dproj.shape, f32),
                   jax.ShapeDtypeStruct((H, 1, 1), f32), jax.ShapeDtypeStruct((H, 1, 1), f32),
                   jax.ShapeDtypeStruct((1, LANE), f32)],
        input_output_aliases={8 + H: 1},
        scratch_shapes=[pltpu.VMEM((H, Dh, Dh), f32)],
        compiler_params=_cparams("arbitrary"),
    )(c, *([proj] * H), proj, a_log, dt_bias, norm_w, s_saved, t_saved, d_oab, dproj)


def _gate_matmuls(xc, wa_ref, wx_ref):
    bw = wa_ref.shape[-1]
    xb = xc.astype(bf16)
    blocks = [xb[:, bw * h:bw * (h + 1)] for h in range(LRU_BLOCKS)]
    return (jnp.concatenate([_dot(blocks[h], wa_ref[h], NN) for h in range(LRU_BLOCKS)], axis=1),
            jnp.concatenate([_dot(blocks[h], wx_ref[h], NN) for h in range(LRU_BLOCKS)], axis=1))


def _gates_fwd(name, xc, w_a, w_x, pars, tm=256):
    T, Wd = xc.shape
    tm = min(tm, T)

    def body(x_ref, wa_ref, wx_ref, ba, bx, lam, a_ref, b_ref):
        x = x_ref[...]
        pr, pi = _gate_matmuls(x, wa_ref, wx_ref)
        a_ref[...], b_ref[...] = _rglru_pre_fn(pr, pi, x, ba[...], bx[...], lam[...])

    row = pl.BlockSpec((tm, Wd), lambda i: (i, 0))
    return pl.pallas_call(
        body, name=name, grid=(T // tm,),
        in_specs=[row, _whole_spec(w_a), _whole_spec(w_x)] + [_whole_spec(p) for p in pars],
        out_specs=[row, row], out_shape=[jax.ShapeDtypeStruct((T, Wd), f32)] * 2,
        compiler_params=_cparams("parallel"),
    )(xc, w_a, w_x, *pars)


def _gates_bwd(name, xc, w_a, w_x, pars, lam_t, h_prev, tm=256):
    T, Wd = xc.shape
    tm = min(tm, T)
    bw = Wd // LRU_BLOCKS

    def body(x_ref, wa_ref, wx_ref, ba, bx, lam, lt_ref, hp_ref, dx_ref, dr_ref, di_ref, dba, dbx, dlam):
        x = x_ref[...]
        pr, pi = _gate_matmuls(x, wa_ref, wx_ref)
        _, vjp = jax.vjp(_rglru_pre_fn, pr, pi, x, ba[...], bx[...], lam[...])
        lt = lt_ref[...]
        dpr, dpi, dxc, g_ba, g_bx, g_lam = vjp((lt * hp_ref[...], lt))
        dprb, dpib = dpr.astype(bf16), dpi.astype(bf16)
        dx_ref[...] = dxc + jnp.concatenate(
            [_dot(dprb[:, bw * h:bw * (h + 1)], wa_ref[h], NT) + _dot(dpib[:, bw * h:bw * (h + 1)], wx_ref[h], NT)
             for h in range(LRU_BLOCKS)], axis=1)
        dr_ref[...] = dprb
        di_ref[...] = dpib

        @pl.when(pl.program_id(0) == 0)
        def _():
            dba[...] = jnp.zeros_like(dba)
            dbx[...] = jnp.zeros_like(dbx)
            dlam[...] = jnp.zeros_like(dlam)

        dba[...] += g_ba
        dbx[...] += g_bx
        dlam[...] += g_lam

    row = pl.BlockSpec((tm, Wd), lambda i: (i, 0))
    vec = pl.BlockSpec((1, Wd), lambda i: (0, 0))
    return pl.pallas_call(
        body, name=name, grid=(T // tm,),
        in_specs=[row, _whole_spec(w_a), _whole_spec(w_x)] + [_whole_spec(p) for p in pars] + [row, row],
        out_specs=[row, row, row, vec, vec, vec],
        out_shape=[jax.ShapeDtypeStruct((T, Wd), f32), jax.ShapeDtypeStruct((T, Wd), bf16),
                   jax.ShapeDtypeStruct((T, Wd), bf16)] + [jax.ShapeDtypeStruct((1, Wd), f32)] * 3,
        compiler_params=_cparams("arbitrary"),
    )(xc, w_a, w_x, *pars, lam_t, h_prev)


def _blockdiag_bwd_dw(name, xc, dpr, dpi, tk=512):
    T, Wd = xc.shape
    bw = Wd // LRU_BLOCKS
    tk = min(tk, T)

    def body(x_ref, dr, di, oa, ox):
        @pl.when(pl.program_id(1) == 0)
        def _():
            oa[...] = jnp.zeros_like(oa)
            ox[...] = jnp.zeros_like(ox)

        xb = x_ref[...].astype(bf16)
        oa[...] += _dot(xb, dr[...].astype(bf16), TN)
        ox[...] += _dot(xb, di[...].astype(bf16), TN)

    xs = pl.BlockSpec((tk, bw), lambda h, k: (k, h))
    ws = pl.BlockSpec((None, bw, bw), lambda h, k: (h, 0, 0))
    return pl.pallas_call(
        body, name=name, grid=(LRU_BLOCKS, T // tk), in_specs=[xs, xs, xs], out_specs=[ws, ws],
        out_shape=[jax.ShapeDtypeStruct((LRU_BLOCKS, bw, bw), f32)] * 2,
        compiler_params=_cparams("parallel", "arbitrary"),
    )(xc, dpr, dpi)


def _scan(name, a, proj, reverse, b=None, h=None, dhg=None, tt=512, cb=512):
    T, Wd = a.shape
    tt, cb = min(tt, T), min(cb, Wd)
    nt = T // tt
    ng = tt // SUBLANE

    def body(a_ref, g_ref, *rest):
        n_in = 2 if reverse else 1
        ins, outs, (carry, carry_a) = rest[:n_in], rest[n_in:-2], rest[-2:]

        @pl.when(pl.program_id(1) == 0)
        def _():
            carry[...] = jnp.zeros_like(carry)
            carry_a[...] = jnp.zeros_like(carry_a)

        row = lax.broadcasted_iota(jnp.int32, (SUBLANE, cb), 0)

        def step(gi, c):
            hp, ap = c
            g = (ng - 1 - gi) if reverse else gi
            rows = pl.ds(pl.multiple_of(g * SUBLANE, SUBLANE), SUBLANE)
            A = a_ref[rows, :]
            gate = g_ref[rows, :]
            a_first = jnp.broadcast_to(A[0:1, :], (SUBLANE, cb))
            if reverse:
                _, vjp = jax.vjp(_rec_gate_fn, ins[0][rows, :], gate)
                B, dgate = vjp((ins[1][rows, :],))
                outs[1][rows, :] = dgate
                A = jnp.where(row == SUBLANE - 1, ap, pltpu.roll(A, SUBLANE - 1, axis=0))
            else:
                B = ins[0][rows, :]
            for s in (1, 2, 4):
                sh = (SUBLANE - s) if reverse else s
                As = pltpu.roll(A, sh, axis=0)
                Bs = pltpu.roll(B, sh, axis=0)
                valid = (row < SUBLANE - s) if reverse else (row >= s)
                B = jnp.where(valid, A * Bs + B, B)
                A = jnp.where(valid, A * As, A)
            hcur = A * hp + B
            outs[0][rows, :] = hcur
            if not reverse:
                outs[1][rows, :] = jnp.where(row == 0, hp, pltpu.roll(hcur, 1, axis=0))
                outs[2][rows, :] = _rec_gate_fn(hcur, gate)[0]
            edge = hcur[0:1, :] if reverse else hcur[SUBLANE - 1:SUBLANE, :]
            return jnp.broadcast_to(edge, (SUBLANE, cb)), a_first

        carry[...], carry_a[...] = lax.fori_loop(0, ng, step, (carry[...], carry_a[...]))

    nc = Wd // cb
    tok = (lambda i: nt - 1 - i) if reverse else (lambda i: i)
    spec = pl.BlockSpec((tt, cb), lambda c, i: (tok(i), c))
    gate_half = pl.BlockSpec((tt, cb), lambda c, i: (tok(i), nc + c))
    if reverse:
        args, out_specs = (a, proj, h, dhg), [spec, gate_half]
        out_shape = [jax.ShapeDtypeStruct((T, Wd), f32), jax.ShapeDtypeStruct((T, 2 * Wd), f32)]
    else:
        args, out_specs = (a, proj, b), [spec] * 3
        out_shape = [jax.ShapeDtypeStruct((T, Wd), f32)] * 3
    return pl.pallas_call(
        body, name=name, grid=(nc, nt), in_specs=[spec, gate_half] + [spec] * (len(args) - 2), out_specs=out_specs,
        out_shape=out_shape,
        scratch_shapes=[pltpu.VMEM((SUBLANE, cb), f32), pltpu.VMEM((SUBLANE, cb), f32)],
        compiler_params=_cparams("parallel", "arbitrary"),
    )(*args)


def _relu2_epilogue(r):
    h = jnp.maximum(r, 0.0)
    return r, h * h


def _drelu2_epilogue(r, a):
    return (r * (2.0 * jnp.maximum(a.astype(f32), 0.0)),)


def _residual_cot(through, upper):
    return (through + DN_ALPHA * upper,)


def _merge_cols(name, g, tm=256):
    _, L, R, s = g.shape

    def body(g_ref, o_ref):
        for d in range(N_DEV):
            o_ref[:, s * d:s * (d + 1)] = g_ref[d].astype(bf16)
        o_ref[:, N_DEV * s:] = jnp.zeros((tm, HYB_PROJ_PAD - N_DEV * s), bf16)

    return pl.pallas_call(
        body, name=name, grid=(L, R // tm),
        in_specs=[pl.BlockSpec((N_DEV, None, tm, s), lambda l, i: (0, l, i, 0))],
        out_specs=pl.BlockSpec((None, tm, HYB_PROJ_PAD), lambda l, i: (l, i, 0)),
        out_shape=jax.ShapeDtypeStruct((L, R, HYB_PROJ_PAD), bf16),
        compiler_params=_cparams("parallel", "parallel"),
    )(g)


def _split_cols(name, dw, tm=256):
    R = dw.shape[0]
    s = HYB_PROJ // N_DEV

    def body(g_ref, o_ref):
        for d in range(N_DEV):
            o_ref[d] = g_ref[:, s * d:s * (d + 1)].astype(bf16)

    return pl.pallas_call(
        body, name=name, grid=(R // tm,),
        in_specs=[pl.BlockSpec((tm, HYB_PROJ_PAD), lambda i: (i, 0))],
        out_specs=pl.BlockSpec((N_DEV, tm, s), lambda i: (0, i, 0)),
        out_shape=jax.ShapeDtypeStruct((N_DEV, R, s), bf16),
        compiler_params=_cparams("parallel"),
    )(dw)


def _rows_to_dev(dw):
    nb, r, c = dw.shape
    t = dw.reshape(nb, N_DEV, r // N_DEV, c)
    return jnp.moveaxis(t, 1, 0).reshape(N_DEV, nb * (r // N_DEV), c).astype(bf16)


def _ln_epilogue(r, x, g, b):
    return r, _ln_res_fn(x, r, g, b)[0]


def _hybrid_fwd(tag, x, W, j, cos, sin, ln, before_out):
    proj = _mm(f"{tag}_proj", x, W["hyb_w_in"][j], "nn", b_kind="lead", b_lead=0)
    o_a = _attn_fwd(f"{tag}_attn", proj, cos, sin, W["hyb_sinks"][j][None, :])
    c = _conv_fwd(f"{tag}_conv", proj, CB_CONV, 12, W["hyb_conv_w"][j], None)
    o_ab, s_saved, t_saved = _delta_fwd(f"{tag}_delta", c, proj, W["hyb_a_log"][j].reshape(B_HEADS, 1, 1),
                                        W["hyb_dt_bias"][j].reshape(B_HEADS, 1, 1), W["hyb_norm_w"][j][None, :], o_a)
    before_out(o_ab)
    mix, x1 = _mm(f"{tag}_out", o_ab, W["hyb_w_out"][j], "nn", b_kind="lead", b_lead=0, epilogue=_ln_epilogue,
                  extras=(x,), params=ln, out_dtypes=(f32, f32), tm=512)
    return mix, x1, (proj, c, s_saved, t_saved, o_ab)


def _hybrid_bwd(tag, x, dmix, addend, W, j, cos, sin, saved, G, send_early):
    proj, c, s_saved, t_saved, o_ab = saved
    T = x.shape[0]
    d_oab = _mm(f"{tag}_dout", dmix, W["hyb_w_out"][j], "nt", b_kind="lead", b_lead=0)
    G["hyb_w_out"][j] = _mm(f"{tag}_dwout", o_ab, dmix, "tn", out_dtypes=(bf16,)).reshape(N_DEV, -1, D_MODEL)
    sinks = W["hyb_sinks"][j][None, :] + send_early({("hyb_w_out", j): G["hyb_w_out"][j]})
    dproj, dsinks = _attn_bwd(f"{tag}_dattn", proj, cos, sin, sinks, d_oab)
    a_log = W["hyb_a_log"][j].reshape(B_HEADS, 1, 1)
    dt_bias = W["hyb_dt_bias"][j].reshape(B_HEADS, 1, 1)
    dc, dproj, dal, ddt, dnw = _delta_bwd(f"{tag}_ddelta", c, proj, a_log, dt_bias, W["hyb_norm_w"][j][None, :],
                                          s_saved, t_saved, d_oab, dproj)
    dproj, dconv_w, _ = _conv_bwd(f"{tag}_dconv", dc, proj, CB_CONV, 12, W["hyb_conv_w"][j], dproj, CB_CONV)
    dx = _mm(f"{tag}_dx", dproj, W["hyb_w_in"][j], "nt", b_kind="lead", b_lead=0,
             **({} if addend is None else dict(epilogue=_residual_cot, extras=(addend,))))
    G["hyb_w_in"][j] = _split_cols(f"{tag}_dwin_split", _mm(f"{tag}_dwin", x, dproj, "tn", tn=1536))
    G["hyb_sinks"][j] = dsinks[0]
    G["hyb_conv_w"][j] = dconv_w
    G["hyb_a_log"][j] = dal.reshape(B_HEADS)
    G["hyb_dt_bias"][j] = ddt.reshape(B_HEADS)
    G["hyb_norm_w"][j] = dnw[0]
    return dx


def _rec_fwd(tag, x, W, j, ln, before_out):
    Wd = D_MODEL
    proj = _mm(f"{tag}_proj", x, W["rec_w_in"][j], "nn", b_kind="devcol", b_lead=0)
    xc = _conv_fwd(f"{tag}_conv", proj, 0, Wd // LANE, W["rec_conv_w"][j], W["rec_conv_b"][j][None, :])
    pars = [W["rec_b_a"][j][None, :], W["rec_b_x"][j][None, :], W["rec_lambda"][j][None, :]]
    a, b = _gates_fwd(f"{tag}_gates", xc, W["rec_w_a"][j][0], W["rec_w_x"][j][0], pars)
    h, h_prev, hg = _scan(f"{tag}_scan", a, proj, False, b=b)
    before_out(hg)
    mix, x1 = _mm(f"{tag}_out", hg, W["rec_w_out"][j], "nn", b_kind="lead", b_lead=0, epilogue=_ln_epilogue,
                  extras=(x,), params=ln, out_dtypes=(f32, f32), tm=512)
    return mix, x1, (proj, xc, a, h, h_prev, hg)


def _rec_bwd(tag, x, dmix, addend, W, j, saved, G, send_early):
    proj, xc, a, h, h_prev, hg = saved
    Wd = D_MODEL
    dhg = _mm(f"{tag}_dout", dmix, W["rec_w_out"][j], "nt", b_kind="lead", b_lead=0)
    G["rec_w_out"][j] = _mm(f"{tag}_dwout", hg, dmix, "tn", out_dtypes=(bf16,)).reshape(N_DEV, -1, D_MODEL)
    sent = send_early({("rec_w_out", j): G["rec_w_out"][j]})
    lam_t, dproj = _scan(f"{tag}_dscan", a, proj, True, h=h, dhg=dhg)
    pars = [W["rec_b_a"][j][None, :] + sent, W["rec_b_x"][j][None, :], W["rec_lambda"][j][None, :]]
    dxc, dpr, dpi, db_a, db_x, dlam = _gates_bwd(f"{tag}_dgates", xc, W["rec_w_a"][j][0], W["rec_w_x"][j][0], pars,
                                                 lam_t, h_prev)
    dwa, dwx = _blockdiag_bwd_dw(f"{tag}_dgates_dw", xc, dpr, dpi)
    G["rec_w_a"][j], G["rec_w_x"][j] = _rows_to_dev(dwa), _rows_to_dev(dwx)
    dproj, dconv_w, dconv_b = _conv_bwd(f"{tag}_dconv", dxc, proj, 0, Wd // LANE, W["rec_conv_w"][j], dproj, 0)
    dx = _mm(f"{tag}_dx", dproj, W["rec_w_in"][j], "nt", b_kind="devcol", b_lead=0,
             **({} if addend is None else dict(epilogue=_residual_cot, extras=(addend,))))
    G["rec_w_in"][j] = _mm(f"{tag}_dwin", x, dproj, "tn", o_kind="devcol", out_dtypes=(bf16,), tn=2048)
    G["rec_conv_w"][j] = dconv_w
    G["rec_conv_b"][j] = dconv_b
    G["rec_b_a"][j] = db_a[0]
    G["rec_b_x"][j] = db_x[0]
    G["rec_lambda"][j] = dlam[0]
    return dx


def _local_step(x, target, W, load_layer, grads_ready):
    T = x.shape[0]
    cos, sin = _rope_tables(T)
    saved = []
    for layer in range(DEPTH):
        j = layer // 2
        tag = f"L{layer}"
        load_layer(layer, 0, x)
        ln1 = (W["ln1_g"][layer][None, :], W["ln1_b"][layer][None, :])
        before_out = functools.partial(load_layer, layer, 1)
        if layer % 2 == 0:
            mix, x1, sv = _hybrid_fwd(tag, x, W, j, cos, sin, ln1, before_out)
        else:
            mix, x1, sv = _rec_fwd(tag, x, W, j, ln1, before_out)
        load_layer(layer, 2, x1)
        a, h2 = _mm(f"{tag}_mlp1", x1, W["mlp_w1"][layer], "nn", b_kind="devcol", b_lead=0, epilogue=_relu2_epilogue,
                    out_dtypes=(bf16, bf16), tm=2048)
        ln2 = (W["ln2_g"][layer][None, :], W["ln2_b"][layer][None, :])
        y, x2 = _mm(f"{tag}_mlp2", h2, W["mlp_w2"][layer], "nn", b_kind="lead", b_lead=0, epilogue=_ln_epilogue,
                    extras=(x1,), params=ln2, out_dtypes=(f32, f32))
        saved.append((x, sv, mix, x1, a, h2, y))
        x = x2
    loss, dx = _loss_head(x, target)

    G = {k: [None] * (DEPTH if k.startswith(("ln", "mlp")) else DEPTH // 2) for k in (
        "hyb_w_in", "hyb_sinks", "hyb_conv_w", "hyb_a_log", "hyb_dt_bias", "hyb_norm_w", "hyb_w_out",
        "rec_w_in", "rec_conv_w", "rec_conv_b", "rec_w_a", "rec_b_a", "rec_w_x", "rec_b_x", "rec_lambda", "rec_w_out",
        "ln1_g", "ln1_b", "mlp_w1", "mlp_w2", "ln2_g", "ln2_b")}
    order = jnp.zeros((1, 1), f32)
    cot_rows, cot_fn = [(dx, 0, D_MODEL)], None
    for layer in reversed(range(DEPTH)):
        j = layer // 2
        tag = f"L{layer}"
        x0, sv, mix, x1, a, h2, y = saved[layer]
        ln2 = [W["ln2_g"][layer][None, :] + order, W["ln2_b"][layer][None, :]]
        (dy,), (dg2, db2) = _tl_bwd(f"{tag}_dln2", _ln_res_fn, [(x1, 0, D_MODEL), (y, 0, D_MODEL)], ln2,
                                    cot_rows, cot_fn=cot_fn, skip=(0,))
        G["ln2_g"][layer], G["ln2_b"][layer] = dg2[0], db2[0]
        da = _mm(f"{tag}_dmlp2", dy, W["mlp_w2"][layer], "nt", b_kind="lead", b_lead=0, epilogue=_drelu2_epilogue,
                 extras=(a,), out_dtypes=(bf16,), tm=2048, tn=512)
        G["mlp_w2"][layer] = _mm(f"{tag}_dw2", h2, dy, "tn", out_dtypes=(bf16,), tm=2048).reshape(N_DEV, -1, D_MODEL)
        dx1 = _mm(f"{tag}_dmlp1", da, W["mlp_w1"][layer], "nt", b_kind="devcol", b_lead=0, tm=2048)
        G["mlp_w1"][layer] = _mm(f"{tag}_dw1", x1, da, "tn", o_kind="devcol", out_dtypes=(bf16,), tn=2048)
        ln1 = [W["ln1_g"][layer][None, :], W["ln1_b"][layer][None, :]]
        (dmix,), (dg1, db1) = _tl_bwd(f"{tag}_dln1", _ln_res_fn, [(x0, 0, D_MODEL), (mix, 0, D_MODEL)], ln1,
                                      [(dx1, 0, D_MODEL), (dy, 0, D_MODEL)], cot_fn=_residual_cot, skip=(0,))
        G["ln1_g"][layer], G["ln1_b"][layer] = dg1[0], db1[0]
        dx0_a = dmix if layer == 0 else None
        early = functools.partial(grads_ready, f"l{layer}_early",
                                  {(k, layer): G[k][layer] for k in ("mlp_w1", "mlp_w2")})
        if layer % 2 == 0:
            dx = _hybrid_bwd(tag, x0, dmix, dx0_a, W, j, cos, sin, sv, G, early)
        else:
            dx = _rec_bwd(tag, x0, dmix, dx0_a, W, j, sv, G, early)
        order = grads_ready(f"l{layer}_late", {}, {(k, i): G[k][i] for k, i in _layer_weights(layer)[:-2]
                                                  if not k.endswith("w_out")})
        cot_rows, cot_fn = [(dx, 0, D_MODEL), (dmix, 0, D_MODEL)], _residual_cot
    big = {k for k, _ in BIG}
    return loss, dx, {k: jnp.stack(v) for k, v in G.items() if k not in big}


def _layer_weights(layer):
    j = layer // 2
    mixer = ["hyb_w_in", "hyb_w_out"] if layer % 2 == 0 else ["rec_w_in", "rec_w_out", "rec_w_a", "rec_w_x"]
    return [(k, j) for k in mixer] + [("mlp_w1", layer), ("mlp_w2", layer)]


def _my_coords():
    return lax.axis_index("x"), lax.axis_index("y"), lax.axis_index("c")


def _all_gather(name, arrays):
    na = len(arrays)

    def body(*refs):
        x_refs, out_refs = refs[:na], refs[na:2 * na]
        send_sems, recv_sems, local_sems = refs[2 * na:]
        x, y, c = _my_coords()
        me, sibling = (x, y, c), (x, y, 1 - c)
        chips = [(1 - x, y), (x, 1 - y), (1 - x, 1 - y)]

        def blk(a, px, py, pc):
            return out_refs[a].at[4 * px + 2 * py + pc]

        def copy(a, k, block, to, src=None):
            return pltpu.make_async_remote_copy(
                src_ref=blk(a, *block) if src is None else src, dst_ref=blk(a, *block),
                send_sem=send_sems.at[a, k], recv_sem=recv_sems.at[a, k],
                device_id=to, device_id_type=pl.DeviceIdType.MESH)

        mine = [pltpu.make_async_copy(x_refs[a], blk(a, *me), local_sems.at[a]) for a in range(na)]
        for cp in mine:
            cp.start()
        first = []
        for a in range(na):
            first.append(copy(a, 0, me, sibling, src=x_refs[a]))
            first += [copy(a, 1 + j, me, (*chip, c), src=x_refs[a]) for j, chip in enumerate(chips)]
        for cp in first:
            cp.start()
        passed = []
        for a in range(na):
            for j, chip in enumerate(chips):
                copy(a, 1 + j, (*chip, c), me).wait_recv()
                passed.append(copy(a, 4 + j, (*chip, c), sibling))
                passed[-1].start()
        for a in range(na):
            copy(a, 0, sibling, me).wait_recv()
            for j, chip in enumerate(chips):
                copy(a, 4 + j, (*chip, 1 - c), me).wait_recv()
        for cp in first + passed:
            cp.wait_send()
        for cp in mine:
            cp.wait()

    return pl.pallas_call(
        body, name=name,
        out_shape=[jax.ShapeDtypeStruct((N_DEV,) + a.shape, a.dtype) for a in arrays],
        in_specs=[pl.BlockSpec(memory_space=pl.ANY)] * na,
        out_specs=[pl.BlockSpec(memory_space=pl.ANY)] * na,
        scratch_shapes=[pltpu.SemaphoreType.DMA((na, 7)), pltpu.SemaphoreType.DMA((na, 7)),
                        pltpu.SemaphoreType.DMA((na,))],
    )(*arrays)


_HBM = pl.BlockSpec(memory_space=pltpu.HBM)
_SEM = pl.BlockSpec(memory_space=pltpu.SEMAPHORE)


def _flip(k, x, y, c):
    return ((1 - x) if k & 4 else x, (1 - y) if k & 2 else y, (1 - c) if k & 1 else c)


_PEERS = {"gather": (1, 2, 4, 6), "scatter": (1, 2, 3, 4, 5, 6, 7)}


def _push_copies(kind, x_refs, land_refs, send_sems, recv_sems, local_sems):
    x, y, c = _my_coords()
    me = 4 * x + 2 * y + c
    peers = _PEERS[kind]
    remote, local = [], []
    for a in range(len(x_refs)):
        local.append(pltpu.make_async_copy(x_refs[a] if kind == "gather" else x_refs[a].at[me], land_refs[a].at[me],
                                           local_sems.at[a]))
        for n, k in enumerate(peers):
            px, py, pc = _flip(k, x, y, c)
            remote.append(pltpu.make_async_remote_copy(
                src_ref=x_refs[a] if kind == "gather" else x_refs[a].at[4 * px + 2 * py + pc],
                dst_ref=land_refs[a].at[me],
                send_sem=send_sems.at[a * len(peers) + n], recv_sem=recv_sems.at[a * len(peers) + n],
                device_id=(px, py, pc), device_id_type=pl.DeviceIdType.MESH))
    return remote, local


def _pass_to_sibling(name, lands):
    na = len(lands)
    chips = (2, 4, 6)

    def body(*refs):
        out_refs, send_sems, recv_sems = refs[na:2 * na], refs[2 * na], refs[2 * na + 1]
        x, y, c = _my_coords()
        cps = []
        for a in range(na):
            for n, k in enumerate(chips):
                px, py, _ = _flip(k, x, y, c)
                cps.append(pltpu.make_async_remote_copy(
                    src_ref=out_refs[a].at[4 * px + 2 * py + c], dst_ref=out_refs[a].at[4 * px + 2 * py + c],
                    send_sem=send_sems.at[a * 3 + n], recv_sem=recv_sems.at[a * 3 + n],
                    device_id=(x, y, 1 - c), device_id_type=pl.DeviceIdType.MESH))
        for cp in cps:
            cp.start()
        for a in range(na):
            for n, k in enumerate(chips):
                px, py, _ = _flip(k, x, y, c)
                blk = out_refs[a].at[4 * px + 2 * py + (1 - c)]
                pltpu.make_async_remote_copy(src_ref=blk, dst_ref=blk, send_sem=send_sems.at[a * 3 + n],
                                             recv_sem=recv_sems.at[a * 3 + n], device_id=(x, y, 1 - c),
                                             device_id_type=pl.DeviceIdType.MESH).wait_recv()
        for cp in cps:
            cp.wait_send()

    return pl.pallas_call(
        body, name=name,
        out_shape=[jax.ShapeDtypeStruct(l.shape, l.dtype) for l in lands],
        in_specs=[pl.BlockSpec(memory_space=pl.ANY)] * na,
        out_specs=[pl.BlockSpec(memory_space=pl.ANY)] * na,
        input_output_aliases={a: a for a in range(na)},
        scratch_shapes=[pltpu.SemaphoreType.DMA((3 * na,)), pltpu.SemaphoreType.DMA((3 * na,))],
    )(*lands)


_SIDE_EFFECT = pltpu.CompilerParams(has_side_effects=pltpu.SideEffectType.DATAFLOW_SIDE_EFFECTING)


def _push_start(name, kind, srcs, lands):
    na = len(srcs)

    def body(*refs):
        remote, local = _push_copies(kind, refs[:na], refs[na:2 * na], *refs[2 * na:2 * na + 3])
        for cp in remote + local:
            cp.start()
        token = refs[-1]
        token[...] = jnp.zeros_like(token)

    arrays = list(srcs) + list(lands)
    n_remote = na * len(_PEERS[kind])
    res = pl.pallas_call(
        body, name=name,
        out_shape=(pltpu.SemaphoreType.DMA((n_remote,)), pltpu.SemaphoreType.DMA((n_remote,)),
                   pltpu.SemaphoreType.DMA((na,)), *[pltpu.HBM(t.shape, t.dtype) for t in arrays],
                   jax.ShapeDtypeStruct((SUBLANE, LANE), f32)),
        in_specs=[_HBM] * (2 * na),
        out_specs=(_SEM, _SEM, _SEM, *[_HBM] * (2 * na), pl.BlockSpec(memory_space=pltpu.VMEM)),
        input_output_aliases={i: 3 + i for i in range(2 * na)},
        compiler_params=_SIDE_EFFECT,
    )(*[pltpu.with_memory_space_constraint(t, pltpu.HBM) for t in arrays])
    return list(res[:3]), res[3:3 + na], res[3 + na:3 + 2 * na], res[-1][:1, :1]


def _push_wait(name, kind, sems, srcs, lands, after):
    na = len(srcs)

    def body(*refs):
        remote, local = _push_copies(kind, refs[:na], refs[na:2 * na], *refs[2 * na:2 * na + 3])
        for cp in remote:
            cp.wait_send()
            cp.wait_recv()
        for cp in local:
            cp.wait()

    arrays = list(srcs) + list(lands)
    res = pl.pallas_call(
        body, name=name,
        out_shape=tuple(pltpu.HBM(t.shape, t.dtype) for t in arrays),
        in_specs=[_HBM] * (2 * na) + [_SEM] * 3 + [pl.BlockSpec(memory_space=pl.ANY)],
        out_specs=tuple([_HBM] * (2 * na)),
        input_output_aliases={i: i for i in range(2 * na)},
        compiler_params=_SIDE_EFFECT,
    )(*arrays, *sems, after)
    return res[na:]


def _sum_blocks(name, land):
    _, R, n = land.shape
    tr = R

    def body(l_ref, o_ref):
        acc = l_ref[0].astype(f32)
        for s in range(1, N_DEV):
            acc = acc + l_ref[s].astype(f32)
        o_ref[...] = acc

    return pl.pallas_call(
        body, name=name, grid=(R // tr,),
        in_specs=[pl.BlockSpec((N_DEV, tr, n), lambda i: (0, i, 0))],
        out_specs=pl.BlockSpec((tr, n), lambda i: (i, 0)),
        out_shape=jax.ShapeDtypeStruct((R, n), f32),
        compiler_params=_cparams("parallel"),
    )(land)


def _adamw(name, w, g, m, v):
    shape = w.shape
    last = shape[-1]
    rows = math.prod(shape[:-1])
    tm = 256 if rows % 256 == 0 and rows > 256 else rows
    w2, g2, m2, v2 = (t.reshape(rows, last) for t in (w, g, m, v))

    def body(w_ref, g_ref, m_ref, v_ref, d_ref, mo_ref, vo_ref):
        gg = g_ref[...]
        mn = ADAM_B1 * m_ref[...] + (1.0 - ADAM_B1) * gg
        vn = ADAM_B2 * v_ref[...] + (1.0 - ADAM_B2) * jnp.square(gg)
        m_hat = mn / (1.0 - ADAM_B1 ** ADAM_STEP)
        v_hat = vn / (1.0 - ADAM_B2 ** ADAM_STEP)
        d_ref[...] = -ADAM_LR * (m_hat / (jnp.sqrt(v_hat) + ADAM_EPS) + ADAM_WD * w_ref[...])
        mo_ref[...] = mn
        vo_ref[...] = vn

    spec = pl.BlockSpec((tm, last), lambda i: (i, 0))
    d, mn, vn = pl.pallas_call(
        body, name=name, grid=(rows // tm,), in_specs=[spec] * 4, out_specs=[spec] * 3,
        out_shape=[jax.ShapeDtypeStruct((rows, last), f32)] * 3,
        compiler_params=_cparams("parallel"),
    )(w2, g2, m2, v2)
    return d.reshape(shape), mn.reshape(shape), vn.reshape(shape)


def _adamw_land(name, lands, w, m, v, tm=256):
    L = len(lands)
    _, R, C = lands[0].shape
    tm = min(tm, R)

    def body(*refs):
        l_refs, (w_ref, m_ref, v_ref, g_ref, d_ref, mo_ref, vo_ref) = refs[:L], refs[L:]
        for k in range(L):
            @pl.when(pl.program_id(0) == k)
            def _(k=k):
                gg = l_refs[k][0].astype(f32)
                for s in range(1, N_DEV):
                    gg = gg + l_refs[k][s].astype(f32)
                g_ref[...] = gg
                mn = ADAM_B1 * m_ref[...] + (1.0 - ADAM_B1) * gg
                vn = ADAM_B2 * v_ref[...] + (1.0 - ADAM_B2) * jnp.square(gg)
                m_hat = mn / (1.0 - ADAM_B1 ** ADAM_STEP)
                v_hat = vn / (1.0 - ADAM_B2 ** ADAM_STEP)
                d_ref[...] = -ADAM_LR * (m_hat / (jnp.sqrt(v_hat) + ADAM_EPS) + ADAM_WD * w_ref[...])
                mo_ref[...] = mn
                vo_ref[...] = vn

    land_specs = [pl.BlockSpec((N_DEV, tm, C), lambda l, i, k=k: (0, jnp.where(l == k, i, 0), 0)) for k in range(L)]
    spec = pl.BlockSpec((None, tm, C), lambda l, i: (l, i, 0))
    return pl.pallas_call(
        body, name=name, grid=(L, R // tm),
        in_specs=land_specs + [spec] * 3,
        out_specs=[spec] * 4,
        out_shape=[jax.ShapeDtypeStruct((L, R, C), f32)] * 4,
        compiler_params=_cparams("arbitrary", "arbitrary"),
    )(*lands, w, m, v)


BIG = [("hyb_w_in", 2), ("hyb_w_out", 1), ("rec_w_in", 2), ("rec_w_out", 1), ("rec_w_a", 2), ("rec_w_x", 2),
       ("mlp_w1", 2), ("mlp_w2", 1)]
SMALL = [("hyb_conv_w", 2), ("rec_conv_w", 2), ("rec_conv_b", 1), ("rec_b_a", 1), ("rec_b_x", 1), ("rec_lambda", 1)]
REPL = ["hyb_sinks", "hyb_a_log", "hyb_dt_bias", "hyb_norm_w", "ln1_g", "ln1_b", "ln2_g", "ln2_b"]
WEIGHTS = ["hyb_w_in", "hyb_sinks", "hyb_conv_w", "hyb_a_log", "hyb_dt_bias", "hyb_norm_w", "hyb_w_out", "rec_w_in",
           "rec_conv_w", "rec_conv_b", "rec_w_a", "rec_b_a", "rec_w_x", "rec_b_x", "rec_lambda", "rec_w_out",
           "ln1_g", "ln1_b", "mlp_w1", "mlp_w2", "ln2_g", "ln2_b"]


def _pack_rows(parts, dtype, row_mult):
    lead = parts[0].shape[:-1]
    flat = jnp.concatenate([p.astype(dtype) for p in parts], axis=-1)
    n = flat.shape[-1]
    unit = row_mult * LANE
    pad = (-n) % unit
    if pad:
        flat = jnp.concatenate([flat, jnp.zeros(lead + (pad,), dtype)], axis=-1)
    return flat.reshape(lead + ((n + pad) // LANE, LANE))


def _gather_full(gathered, shard_shapes, table):
    flat = gathered.reshape(N_DEV, -1)
    out, off = {}, 0
    for name, ax in table:
        shp = shard_shapes[name]
        n = math.prod(shp)
        arr = flat[:, off:off + n].reshape((N_DEV,) + shp)
        off += n
        arr = jnp.moveaxis(arr, 0, ax)
        out[name] = arr.reshape(shp[:ax] + (N_DEV * shp[ax],) + shp[ax + 1:])
    return out


def _matmul_layouts(tag, gw):
    out = {}
    bw = D_MODEL // LRU_BLOCKS
    for k, g in gw.items():
        L = g.shape[1]
        if k == "hyb_w_in":
            out[k] = _merge_cols(f"{tag}_w_in_merge", g)
        elif k in ("hyb_w_out", "rec_w_out", "mlp_w2"):
            out[k] = jnp.swapaxes(g, 0, 1).reshape(L, N_DEV * g.shape[2], g.shape[3])
        elif k in ("rec_w_a", "rec_w_x"):
            out[k] = jnp.moveaxis(g, 0, 2).reshape(L, LRU_BLOCKS, bw, bw)
        else:
            out[k] = g
    return out


def kernel(x, hyb_w_in, hyb_sinks, hyb_conv_w, hyb_a_log, hyb_dt_bias, hyb_norm_w, hyb_w_out, rec_w_in, rec_conv_w, rec_conv_b, rec_w_a, rec_b_a, rec_w_x, rec_b_x, rec_lambda, rec_w_out, ln1_g, ln1_b, mlp_w1, mlp_w2, ln2_g, ln2_b, loss_target, m_hyb_w_in, m_hyb_sinks, m_hyb_conv_w, m_hyb_a_log, m_hyb_dt_bias, m_hyb_norm_w, m_hyb_w_out, m_rec_w_in, m_rec_conv_w, m_rec_conv_b, m_rec_w_a, m_rec_b_a, m_rec_w_x, m_rec_b_x, m_rec_lambda, m_rec_w_out, m_ln1_g, m_ln1_b, m_mlp_w1, m_mlp_w2, m_ln2_g, m_ln2_b, v_hyb_w_in, v_hyb_sinks, v_hyb_conv_w, v_hyb_a_log, v_hyb_dt_bias, v_hyb_norm_w, v_hyb_w_out, v_rec_w_in, v_rec_conv_w, v_rec_conv_b, v_rec_w_a, v_rec_b_a, v_rec_w_x, v_rec_b_x, v_rec_lambda, v_rec_w_out, v_ln1_g, v_ln1_b, v_mlp_w1, v_mlp_w2, v_ln2_g, v_ln2_b):
    args = locals()
    w = {k: args[k] for k in WEIGHTS}
    m = {k: args["m_" + k] for k in WEIGHTS}
    v = {k: args["v_" + k] for k in WEIGHTS}
    shard_shapes = {k: tuple(t.shape) for k, t in w.items()}
    xi, yi, ci = _my_coords()
    me = 4 * xi + 2 * yi + ci

    in_flight = {}

    def install(tag, names, got):
        for (k, i), arr in zip(names, _matmul_layouts(tag, {k: g for (k, _), g in zip(names, got)}).values()):
            W[k][i] = arr

    def start_gather(tag, names):
        srcs = [w[k][i:i + 1].astype(bf16) for k, i in names]
        *pending, zero = _push_start(f"gather_{tag}_start", "gather", srcs,
                                     [lax.empty((N_DEV,) + s.shape, bf16) for s in srcs])
        in_flight[tag] = (names, pending)
        return zero

    def finish_gather(tag, after):
        names, pending = in_flight.pop(tag)
        half = _push_wait(f"gather_{tag}_wait", "gather", *pending, after)
        install(tag, names, _pass_to_sibling(f"gather_{tag}_pass", half))

    def started(k, zero):
        W[k] = W[k] + zero

    def mixer_w(layer):
        return _layer_weights(layer)[:-2]

    def mlp_w(layer):
        return _layer_weights(layer)[-2:]

    gathered0 = _all_gather("gather_first", [w[k][i:i + 1].astype(bf16) for k, i in mixer_w(0)]
                            + [_pack_rows([w[k].reshape(-1) for k, _ in SMALL], f32, SUBLANE)])
    W = _gather_full(gathered0[-1], shard_shapes, SMALL)
    W.update({k: w[k] for k in REPL})
    W.update({k: {} for k, _ in BIG})
    install("l0a", mixer_w(0), gathered0[:-1])
    started("hyb_sinks", start_gather("l0b", mlp_w(0)) + start_gather("l1a", mixer_w(1)))

    def load_layer(layer, stage, after):
        if stage == 0:
            if layer > 0:
                finish_gather(f"l{layer}a", after)
            if 0 < layer < DEPTH - 1:
                started("hyb_sinks" if layer % 2 == 0 else "rec_conv_b",
                        start_gather(f"l{layer + 1}a", mixer_w(layer + 1)))
        if stage == 2:
            finish_gather(f"l{layer}b", after)
            if layer < DEPTH - 1:
                started("ln2_g", start_gather(f"l{layer + 1}b", mlp_w(layer + 1)))

    grads_in_flight = {}

    def grads_ready(tag, a, b):
        g = {**a, **b}
        srcs = list(g.values())
        *pending, zero = _push_start(f"scatter_{tag}_start", "scatter", srcs, [lax.empty(s.shape, bf16) for s in srcs])
        grads_in_flight[tag] = (list(g.keys()), pending)
        return zero

    loss_local, grad_x, G = _local_step(x[0], loss_target[0], W, load_layer, grads_ready)
    loss = lax.psum(loss_local, MESH_AXES)

    landed = {}

    def land(tag, after):
        keys, pending = grads_in_flight[tag]
        landed.update(zip(keys, _push_wait(f"scatter_{tag}_wait", "scatter", *pending, after)))

    tags = list(grads_in_flight)
    for tag in tags[:-1]:
        land(tag, grad_x)
    rest = _pack_rows([G[k].reshape(-1) for k, _ in SMALL] + [G[k].reshape(-1) for k in REPL], f32, SUBLANE)
    g_rest = _sum_blocks("sum_rest", _all_gather("gather_rest", [rest])[0]).reshape(-1)

    grads, delta, new_m, new_v = {}, {}, {}, {}

    def adamw_big(k):
        shp = shard_shapes[k]
        s3 = (shp[0], math.prod(shp[1:-1]), shp[-1])
        lands = [landed[(k, i)].reshape((N_DEV,) + s3[1:]) for i in range(shp[0])]
        res = _adamw_land("adamw_" + k, lands, w[k].reshape(s3), m[k].reshape(s3), v[k].reshape(s3))
        grads[k], delta[k], new_m[k], new_v[k] = (r.reshape(shp) for r in res)

    late = {k for k, _ in grads_in_flight[tags[-1]][0]}
    for k in [k for k, _ in BIG if k not in late]:
        adamw_big(k)
        done = new_v[k]
    land(tags[-1], done)
    for k in [k for k, _ in BIG if k in late]:
        adamw_big(k)
    off = 0
    for k, ax in SMALL:
        full_shape = G[k].shape
        n = math.prod(full_shape)
        full = g_rest[off:off + n].reshape(full_shape)
        off += n
        s = shard_shapes[k][ax]
        grads[k] = lax.dynamic_slice_in_dim(full, me * s, s, axis=ax)
    for k in REPL:
        n = math.prod(shard_shapes[k])
        grads[k] = g_rest[off:off + n].reshape(shard_shapes[k])
        off += n

    for k in [k for k, _ in SMALL] + REPL:
        delta[k], new_m[k], new_v[k] = _adamw("adamw_" + k, w[k], grads[k], m[k], v[k])

    return (loss, grad_x[None], *[grads[k] for k in WEIGHTS], *[delta[k] for k in WEIGHTS],
            *[new_m[k] for k in WEIGHTS], *[new_v[k] for k in WEIGHTS])
```

```python
import functools
import math

import jax
import jax.numpy as jnp
from jax import lax
from jax.experimental import pallas as pl
from jax.experimental.pallas import tpu as pltpu

f32 = jnp.float32
bf16 = jnp.bfloat16

N_DEV = 8
D_MODEL = 1024
DEPTH = 4
A_HEAD_DIM = 64
A_Q_HEADS = 8
WINDOW = 128
ROPE_THETA = 10000.0
B_HEADS = 4
B_HEAD_DIM = 128
B_CHUNK = 64
LRU_BLOCKS = 4
LRU_C = 8.0
D_FF = 4 * D_MODEL
HYB_PROJ = 2824
HYB_PROJ_PAD = 3072
DN_ALPHA = (2 * DEPTH) ** 0.25
LN_EPS = 1e-5
NORM_EPS = 1e-6
ADAM_LR = 0.001
ADAM_B1 = 0.9
ADAM_B2 = 0.999
ADAM_EPS = 1e-08
ADAM_WD = 0.01
ADAM_STEP = 10

LANE = 128
SUBLANE = 8
VMEM_LIMIT = 48 * 1024 * 1024

CB_QA, CB_KA, CB_VA, CB_CONV, CB_Z, CB_LG = 0, 4, 5, 6, 18, 22

MESH_AXES = ("x", "y", "c")


def _cparams(*sem):
    return pltpu.CompilerParams(dimension_semantics=sem, vmem_limit_bytes=VMEM_LIMIT)


def _dot(a, b, dims, precision=None):
    return lax.dot_general(a, b, (dims, ((), ())), preferred_element_type=f32, precision=precision)


NN = ((1,), (0,))
NT = ((1,), (1,))
TN = ((0,), (0,))


def _mat_spec(arr, kind, lead, br, bc, rb, cb):
    if kind == "plain":
        return pl.BlockSpec((br, bc), lambda i, j, k: (rb(i, j, k), cb(i, j, k)))
    if kind == "lead":
        return pl.BlockSpec((None, br, bc), lambda i, j, k: (lead, rb(i, j, k), cb(i, j, k)))
    if kind == "devcol":
        assert bc == arr.shape[-1]
        return pl.BlockSpec((None, None, br, bc), lambda i, j, k: (cb(i, j, k), lead, rb(i, j, k), 0))
    assert kind == "devrow" and br == arr.shape[-2]
    return pl.BlockSpec((None, None, br, bc), lambda i, j, k: (rb(i, j, k), lead, 0, cb(i, j, k)))


def _mm(name, a, b, mode, *, b_kind="plain", b_lead=0, o_kind="plain", epilogue=None, extras=(), params=(),
        out_dtypes=(f32,), tm=1024, tn=1024, tk=None):
    if tk is None:
        tk = 512 if mode == "tn" else 1024
    if b_kind in ("plain", "lead"):
        b_rows, b_cols = b.shape[-2:]
    elif b_kind == "devcol":
        b_rows, b_cols = b.shape[-2], N_DEV * b.shape[-1]
    else:
        b_rows, b_cols = N_DEV * b.shape[-2], b.shape[-1]
    if mode == "nn":
        (M, K), (K2, N) = a.shape, (b_rows, b_cols)
    elif mode == "nt":
        (M, K), (N, K2) = a.shape, (b_rows, b_cols)
    else:
        (K, M), (K2, N) = a.shape, (b_rows, b_cols)
    assert K == K2, (name, a.shape, b.shape, mode)
    tm, tn, tk = min(tm, M), min(tn, N), min(tk, K)
    cols_are_n = mode != "nt"
    if b_kind == "devcol":
        tn, tk = (b.shape[-1], tk) if cols_are_n else (tn, b.shape[-1])
    if b_kind == "devrow":
        tn, tk = (tn, b.shape[-2]) if cols_are_n else (b.shape[-2], tk)
    shard = N // N_DEV
    if o_kind == "devcol":
        tn = max(shard, tn // shard * shard)
    assert M % tm == 0 and N % tn == 0 and K % tk == 0, (name, M, N, K, tm, tn, tk)
    nk = K // tk
    dims = {"nn": NN, "nt": NT, "tn": TN}[mode]
    n_ex, n_out = len(extras) + len(params), len(out_dtypes)

    def body(*refs):
        a_ref, b_ref = refs[:2]
        ex = refs[2:2 + n_ex]
        outs = refs[2 + n_ex:2 + n_ex + n_out]
        acc = refs[-1]
        k = pl.program_id(2)

        @pl.when(k == 0)
        def _():
            acc[...] = jnp.zeros_like(acc)

        acc[...] += _dot(a_ref[...].astype(bf16), b_ref[...].astype(bf16), dims)

        @pl.when(k == nk - 1)
        def _():
            r = acc[...]
            res = epilogue(r, *[e[...] for e in ex]) if epilogue is not None else (r,)
            for o, v in zip(outs, res):
                if o_kind == "plain":
                    o[...] = v.astype(o.dtype)
                else:
                    for q in range(tn // shard):
                        o[q] = v[:, q * shard:(q + 1) * shard].astype(o.dtype)

    if mode == "tn":
        a_spec = pl.BlockSpec((tk, tm), lambda i, j, k: (k, i))
    else:
        a_spec = pl.BlockSpec((tm, tk), lambda i, j, k: (i, k))
    jb, kb = (lambda i, j, k: j), (lambda i, j, k: k)
    if mode == "nt":
        b_spec = _mat_spec(b, b_kind, b_lead, tn, tk, jb, kb)
    else:
        b_spec = _mat_spec(b, b_kind, b_lead, tk, tn, kb, jb)
    e_spec = pl.BlockSpec((tm, tn), lambda i, j, k: (i, j))
    if o_kind == "plain":
        o_spec, o_shape = e_spec, (M, N)
    else:
        o_spec, o_shape = pl.BlockSpec((tn // shard, tm, shard), lambda i, j, k: (j, i, 0)), (N_DEV, M, shard)
    res = pl.pallas_call(
        body, name=name,
        grid=(M // tm, N // tn, nk),
        in_specs=[a_spec, b_spec] + [e_spec] * len(extras)
        + [pl.BlockSpec(p.shape, lambda i, j, k: (0, 0)) for p in params],
        out_specs=[o_spec] * n_out,
        out_shape=[jax.ShapeDtypeStruct(o_shape, dt) for dt in out_dtypes],
        scratch_shapes=[pltpu.VMEM((tm, tn), f32)],
        compiler_params=_cparams("parallel", "parallel", "arbitrary"),
    )(a, b, *extras, *params)
    return res[0] if n_out == 1 else res


def _row_spec(tm, cb, width):
    assert (cb * LANE) % width == 0
    blk = (cb * LANE) // width
    return pl.BlockSpec((tm, width), lambda i: (i, blk))


def _whole_spec(p):
    nd = p.ndim
    return pl.BlockSpec(p.shape, lambda i: (0,) * nd)


def _tl_fwd(name, fn, rows, params, out_widths, out_dtypes, tm=256):
    T = rows[0][0].shape[0]
    tm = min(tm, T)
    nr, npar = len(rows), len(params)

    def body(*refs):
        vals = [r[...] for r in refs[:nr + npar]]
        outs = fn(*vals)
        for o, v in zip(refs[nr + npar:], outs):
            o[...] = v.astype(o.dtype)

    res = pl.pallas_call(
        body, name=name, grid=(T // tm,),
        in_specs=[_row_spec(tm, cb, w) for (_, cb, w) in rows] + [_whole_spec(p) for p in params],
        out_specs=[pl.BlockSpec((tm, w), lambda i: (i, 0)) for w in out_widths],
        out_shape=[jax.ShapeDtypeStruct((T, w), dt) for w, dt in zip(out_widths, out_dtypes)],
        compiler_params=_cparams("parallel"),
    )(*[r[0] for r in rows], *params)
    return res


def _tl_bwd(name, fn, rows, params, cot_rows, cot_fn=None, skip=(), tm=256):
    T = rows[0][0].shape[0]
    tm = min(tm, T)
    nr, npar, nc = len(rows), len(params), len(cot_rows)
    keep = [k for k in range(nr) if k not in skip]

    def body(*refs):
        vals = [r[...] for r in refs[:nr + npar]]
        cots = [r[...] for r in refs[nr + npar:nr + npar + nc]]
        outs = refs[nr + npar + nc:]
        cot = tuple(cot_fn(*cots)) if cot_fn is not None else tuple(cots)
        _, vjp = jax.vjp(fn, *vals)
        grads = vjp(cot)
        for o, k in zip(outs, keep):
            o[...] = grads[k].astype(o.dtype)
        i = pl.program_id(0)
        for o, g in zip(outs[len(keep):], grads[nr:]):
            @pl.when(i == 0)
            def _(o=o):
                o[...] = jnp.zeros_like(o)
            o[...] += g

    res = pl.pallas_call(
        body, name=name, grid=(T // tm,),
        in_specs=[_row_spec(tm, cb, w) for (_, cb, w) in rows] + [_whole_spec(p) for p in params]
        + [_row_spec(tm, cb, w) for (_, cb, w) in cot_rows],
        out_specs=[pl.BlockSpec((tm, rows[k][2]), lambda i: (i, 0)) for k in keep] + [_whole_spec(p) for p in params],
        out_shape=[jax.ShapeDtypeStruct((T, rows[k][2]), f32) for k in keep]
        + [jax.ShapeDtypeStruct(p.shape, f32) for p in params],
        compiler_params=_cparams("arbitrary"),
    )(*[r[0] for r in rows], *params, *[r[0] for r in cot_rows])
    return res[:len(keep)], res[len(keep):]


def _ln_res_fn(x, mix, g, b):
    pre = DN_ALPHA * x + mix
    mu = jnp.mean(pre, axis=-1, keepdims=True)
    var = jnp.mean(jnp.square(pre - mu), axis=-1, keepdims=True)
    return ((pre - mu) * lax.rsqrt(var + LN_EPS) * g + b,)


@jax.custom_jvp
def _expm1(x):
    small = jnp.abs(x) < 0.3
    xs = jnp.where(small, x, 0.0)
    poly = xs * (1.0 + xs * (1 / 2 + xs * (1 / 6 + xs * (1 / 24 + xs * (1 / 120 + xs * (
        1 / 720 + xs * (1 / 5040 + xs * (1 / 40320 + xs * (1 / 362880)))))))))
    return jnp.where(small, poly, jnp.exp(x) - 1.0)


@_expm1.defjvp
def _expm1_jvp(primals, tangents):
    (x,), (t,) = primals, tangents
    return _expm1(x), t * jnp.exp(x)


def _rglru_pre_fn(pre_r, pre_i, xc, b_a, b_x, lam):
    r = jax.nn.sigmoid(pre_r + b_a)
    i = jax.nn.sigmoid(pre_i + b_x)
    log_a = -LRU_C * r * jax.nn.softplus(-lam)
    a = jnp.exp(log_a)
    b = jnp.sqrt(-_expm1(2.0 * log_a)) * (i * xc)
    return a, b


def _rec_gate_fn(h, gate):
    return (h * jax.nn.gelu(gate),)


def _loss_head(y, t, tm=256):
    T, Dm = y.shape

    def body(y_ref, t_ref, dy_ref, loss_ref):
        e = y_ref[...] - t_ref[...]
        dy_ref[...] = e * (1.0 / Dm)

        @pl.when(pl.program_id(0) == 0)
        def _():
            loss_ref[...] = jnp.zeros_like(loss_ref)

        loss_ref[...] += 0.5 * jnp.sum(jnp.mean(e * e, axis=-1, keepdims=True), axis=0, keepdims=True)

    dy, loss = pl.pallas_call(
        body, name="loss_head", grid=(T // tm,),
        in_specs=[pl.BlockSpec((tm, Dm), lambda i: (i, 0))] * 2,
        out_specs=[pl.BlockSpec((tm, Dm), lambda i: (i, 0)), pl.BlockSpec((SUBLANE, LANE), lambda i: (0, 0))],
        out_shape=[jax.ShapeDtypeStruct((T, Dm), f32), jax.ShapeDtypeStruct((SUBLANE, LANE), f32)],
        compiler_params=_cparams("arbitrary"),
    )(y, t)
    return loss[0, 0], dy


def _conv_fwd(name, x, cb0, nblk, w, bias, tm=2048):
    T = x.shape[0]
    tm = min(tm, T)
    hb = tm // SUBLANE
    has_b = bias is not None

    def body(*refs):
        cur, prev, w_ref = refs[:3]
        b_ref = refs[3] if has_b else None
        o = refs[-1]
        i = pl.program_id(1)
        p = jnp.where(i > 0, prev[...], 0.0)
        xcat = jnp.concatenate([p, cur[...]], axis=0)
        acc = cur[...] * w_ref[3:4, :]
        for j in range(3):
            acc = acc + pltpu.roll(xcat, 3 - j, axis=0)[SUBLANE:] * w_ref[j:j + 1, :]
        if has_b:
            acc = acc + b_ref[...]
        o[...] = acc

    in_specs = [
        pl.BlockSpec((tm, LANE), lambda c, i: (i, cb0 + c)),
        pl.BlockSpec((SUBLANE, LANE), lambda c, i: (jnp.maximum(i * hb - 1, 0), cb0 + c)),
        pl.BlockSpec((4, LANE), lambda c, i: (0, c)),
    ]
    args = [x, x, w]
    if has_b:
        in_specs.append(pl.BlockSpec((1, LANE), lambda c, i: (0, c)))
        args.append(bias)
    return pl.pallas_call(
        body, name=name, grid=(nblk, T // tm),
        in_specs=in_specs,
        out_specs=pl.BlockSpec((tm, LANE), lambda c, i: (i, c)),
        out_shape=jax.ShapeDtypeStruct((T, nblk * LANE), f32),
        compiler_params=_cparams("parallel", "parallel"),
    )(*args)


def _conv_bwd(name, dy, x, cb0, nblk, w, into, into_cb, tm=2048):
    T = x.shape[0]
    tm = min(tm, T)
    hb = tm // SUBLANE
    nt = T // tm

    def body(dcur, dnext, xcur, xprev, w_ref, _, dx_ref, dw_ref, db_ref):
        i = pl.program_id(1)
        d = dcur[...]
        dn = jnp.where(i < nt - 1, dnext[...], 0.0)
        dcat = jnp.concatenate([d, dn], axis=0)
        acc = d * w_ref[3:4, :]
        for j in range(3):
            s = 3 - j
            acc = acc + pltpu.roll(dcat, tm + SUBLANE - s, axis=0)[:tm] * w_ref[j:j + 1, :]
        dx_ref[...] = acc

        p = jnp.where(i > 0, xprev[...], 0.0)
        xcat = jnp.concatenate([p, xcur[...]], axis=0)
        rows = [jnp.sum(d * pltpu.roll(xcat, 3 - j, axis=0)[SUBLANE:], axis=0, keepdims=True) for j in range(3)]
        rows.append(jnp.sum(d * xcur[...], axis=0, keepdims=True))
        rows.append(jnp.zeros((SUBLANE - 4, LANE), f32))

        @pl.when(i == 0)
        def _():
            dw_ref[...] = jnp.zeros_like(dw_ref)
            db_ref[...] = jnp.zeros_like(db_ref)

        dw_ref[...] += jnp.concatenate(rows, axis=0)
        db_ref[...] += jnp.broadcast_to(jnp.sum(d, axis=0, keepdims=True), (SUBLANE, LANE))

    nh = T // SUBLANE
    dx, dw, db = pl.pallas_call(
        body, name=name, grid=(nblk, nt),
        in_specs=[
            pl.BlockSpec((tm, LANE), lambda c, i: (i, c)),
            pl.BlockSpec((SUBLANE, LANE), lambda c, i: (jnp.minimum((i + 1) * hb, nh - 1), c)),
            pl.BlockSpec((tm, LANE), lambda c, i: (i, cb0 + c)),
            pl.BlockSpec((SUBLANE, LANE), lambda c, i: (jnp.maximum(i * hb - 1, 0), cb0 + c)),
            pl.BlockSpec((4, LANE), lambda c, i: (0, c)),
            pl.BlockSpec(memory_space=pl.ANY),
        ],
        out_specs=[
            pl.BlockSpec((tm, LANE), lambda c, i: (i, into_cb + c)),
            pl.BlockSpec((SUBLANE, LANE), lambda c, i: (0, c)),
            pl.BlockSpec((SUBLANE, LANE), lambda c, i: (0, c)),
        ],
        out_shape=[jax.ShapeDtypeStruct(into.shape, f32),
                   jax.ShapeDtypeStruct((SUBLANE, nblk * LANE), f32),
                   jax.ShapeDtypeStruct((SUBLANE, nblk * LANE), f32)],
        input_output_aliases={5: 0},
        compiler_params=_cparams("parallel", "arbitrary"),
    )(dy, dy, x, x, w, into)
    return dx, dw[:4], db[0]


@functools.partial(jax.custom_vjp, nondiff_argnums=(1,))
def _lroll(x, s):
    return pltpu.roll(x, s, axis=1)


def _lroll_fwd(x, s):
    return _lroll(x, s), None


def _lroll_bwd(s, _, g):
    return (_lroll(g, (LANE - s) % LANE),)


_lroll.defvjp(_lroll_fwd, _lroll_bwd)


def _rope_tables(T):
    half = A_HEAD_DIM // 2
    inv_freq = ROPE_THETA ** (-jnp.arange(half, dtype=f32) / half)
    ang = jnp.arange(T, dtype=f32)[:, None] * inv_freq[None, :]
    cos, sin = jnp.cos(ang), jnp.sin(ang)
    return jnp.tile(jnp.concatenate([cos, cos], axis=1), (1, 2)), jnp.tile(jnp.concatenate([-sin, sin], axis=1), (1, 2))


def _attn_block_fn(n, q, kp, kc, vp, vc, cq, sq, cp, sp, sinks):
    W = WINDOW
    lane = lax.broadcasted_iota(jnp.int32, (W, LANE), 1)
    lo_half = (lane % A_HEAD_DIM) < (A_HEAD_DIM // 2)
    lane8 = lax.broadcasted_iota(jnp.int32, sinks.shape, 1)

    def rope(x, c, s):
        return x * c + jnp.where(lo_half, _lroll(x, LANE - A_HEAD_DIM // 2), _lroll(x, A_HEAD_DIM // 2)) * s

    k2 = jnp.concatenate([rope(kp, cp, sp), rope(kc, cq, sq)], axis=0).astype(bf16)
    v2 = jnp.concatenate([vp, vc], axis=0).astype(bf16)
    qs = []
    for t in range(4):
        qt = rope(q[:, LANE * t:LANE * (t + 1)], cq, sq)
        g = t // 2
        for hh in range(2):
            qa = jnp.where((lane // A_HEAD_DIM) == hh, qt, 0.0)
            qs.append(_lroll(qa, A_HEAD_DIM) if hh != g else qa)
    s_all = _dot(jnp.concatenate(qs, axis=0).astype(bf16), k2, NT) * (A_HEAD_DIM ** -0.5)
    row = lax.broadcasted_iota(jnp.int32, (W, 2 * W), 0)
    col = lax.broadcasted_iota(jnp.int32, (W, 2 * W), 1)
    dist = row + W - col
    mask = (dist >= 0) & (dist < W) & ((col >= W) | (n > 0))
    ps = []
    for j in range(A_Q_HEADS):
        s = jnp.where(mask, s_all[W * j:W * (j + 1)], -jnp.inf)
        sink = jnp.sum(jnp.where(lane8 == j, sinks, 0.0), axis=1, keepdims=True)
        m = jnp.maximum(jnp.max(s, axis=-1, keepdims=True), sink)
        e = jnp.exp(s - m)
        ps.append((e / (jnp.sum(e, axis=-1, keepdims=True) + jnp.exp(sink - m))).astype(bf16))
    o = _dot(jnp.concatenate(ps, axis=0), v2, NN)
    outs = []
    for t in range(4):
        g = t // 2
        ot = jnp.zeros((W, LANE), f32)
        for hh in range(2):
            j = 2 * t + hh
            oj = jnp.where((lane // A_HEAD_DIM) == g, o[W * j:W * (j + 1)], 0.0)
            ot = ot + (_lroll(oj, A_HEAD_DIM) if hh != g else oj)
        outs.append(ot)
    return jnp.concatenate(outs, axis=1)


def _attn_specs():
    W = WINDOW
    prev = lambda n: jnp.maximum(n - 1, 0)
    return [
        pl.BlockSpec((W, 4 * LANE), lambda n: (n, CB_QA // 4)),
        pl.BlockSpec((W, LANE), lambda n: (prev(n), CB_KA)),
        pl.BlockSpec((W, LANE), lambda n: (n, CB_KA)),
        pl.BlockSpec((W, LANE), lambda n: (prev(n), CB_VA)),
        pl.BlockSpec((W, LANE), lambda n: (n, CB_VA)),
        pl.BlockSpec((W, LANE), lambda n: (n, 0)),
        pl.BlockSpec((W, LANE), lambda n: (n, 0)),
        pl.BlockSpec((W, LANE), lambda n: (prev(n), 0)),
        pl.BlockSpec((W, LANE), lambda n: (prev(n), 0)),
        pl.BlockSpec((1, A_Q_HEADS), lambda n: (0, 0)),
    ]


def _attn_fwd(name, proj, cos, sin, sinks):
    T = proj.shape[0]
    W = WINDOW

    def body(*refs):
        o = refs[-1]
        o[...] = _attn_block_fn(pl.program_id(0), *[r[...] for r in refs[:-1]])

    return pl.pallas_call(
        body, name=name, grid=(T // W,),
        in_specs=_attn_specs(),
        out_specs=pl.BlockSpec((W, 4 * LANE), lambda n: (n, 0)),
        out_shape=jax.ShapeDtypeStruct((T, 2 * 4 * LANE), f32),
        compiler_params=_cparams("parallel"),
    )(proj, proj, proj, proj, proj, cos, sin, cos, sin, sinks)


def _attn_bwd(name, proj, cos, sin, sinks, d_oab):
    T = proj.shape[0]
    W = WINDOW
    Q = 4 * LANE

    def body(*refs):
        ins = [r[...] for r in refs[:10]]
        do = refs[10][...]
        d_ref, ds_ref = refs[11:]
        n = pl.program_id(0)
        _, vjp = jax.vjp(functools.partial(_attn_block_fn, n), *ins)
        dq, dkp, dkc, dvp, dvc, _, _, _, _, dsk = vjp(do)

        @pl.when(n == 0)
        def _():
            d_ref[:, Q:] = jnp.zeros((T, 2 * LANE), f32)
            ds_ref[...] = jnp.zeros_like(ds_ref)

        cur = pl.ds(pl.multiple_of(n * W, W), W)
        d_ref[cur, :Q] = dq
        d_ref[cur, Q:Q + LANE] += dkc
        d_ref[cur, Q + LANE:] += dvc
        ds_ref[...] += dsk

        @pl.when(n > 0)
        def _():
            prv = pl.ds(pl.multiple_of((n - 1) * W, W), W)
            d_ref[prv, Q:Q + LANE] += dkp
            d_ref[prv, Q + LANE:] += dvp

    return pl.pallas_call(
        body, name=name, grid=(T // W,),
        in_specs=_attn_specs() + [pl.BlockSpec((W, Q), lambda n: (n, 0))],
        out_specs=[pl.BlockSpec((T, Q + 2 * LANE), lambda n: (0, 0)),
                   pl.BlockSpec((1, A_Q_HEADS), lambda n: (0, 0))],
        out_shape=[jax.ShapeDtypeStruct((T, HYB_PROJ_PAD), f32), jax.ShapeDtypeStruct((1, A_Q_HEADS), f32)],
        compiler_params=_cparams("arbitrary"),
    )(proj, proj, proj, proj, proj, cos, sin, cos, sin, sinks, d_oab)


def _bdot(spec, a, b, precision=None):
    return jnp.einsum(spec, a, b, preferred_element_type=f32, precision=precision)


@jax.custom_vjp
def _tri_inv(a):
    H, C, _ = a.shape
    B = 2 * SUBLANE
    nb = C // B
    r = lax.broadcasted_iota(jnp.int32, (C, C), 0)
    c = lax.broadcasted_iota(jnp.int32, (C, C), 1)
    a4 = jnp.where((r // B) == (c // B), a, 0.0).reshape(H, nb, B, C)
    t4 = jnp.broadcast_to(jnp.where(r == c, 1.0, 0.0).astype(f32), a.shape).reshape(H, nb, B, C)
    for j in range(B - 1):
        col = jnp.concatenate([a4[:, b:b + 1, :, B * b + j:B * b + j + 1] for b in range(nb)], axis=1)
        t4 = t4 - col * t4[:, :, j:j + 1, :]
    x = t4.reshape(H, C, C)
    hi = lax.Precision.HIGH
    while B < C:
        m = jnp.where(((r // (2 * B)) == (c // (2 * B))) & ((r // B) > (c // B)), a, 0.0)
        x = x - _bdot("hij,hjk->hik", x, _bdot("hij,hjk->hik", m, x, precision=hi), precision=hi)
        B *= 2
    return x


def _tri_inv_fwd(a):
    t = _tri_inv(a)
    return t, t


def _tri_inv_bwd(t, g):
    C = t.shape[-1]
    r = lax.broadcasted_iota(jnp.int32, (C, C), 0)
    c = lax.broadcasted_iota(jnp.int32, (C, C), 1)
    x = _bdot("hki,hkj->hij", t, g, precision=lax.Precision.HIGHEST)
    y = _bdot("hik,hjk->hij", x, t, precision=lax.Precision.HIGHEST)
    return (jnp.where(r > c, -y, 0.0),)


_tri_inv.defvjp(_tri_inv_fwd, _tri_inv_bwd)


@jax.custom_vjp
def _tri_inv_saved(a, t):
    return t


_tri_inv_saved.defvjp(lambda a, t: (t, t), lambda t, g: (_tri_inv_bwd(t, g)[0], jnp.zeros_like(t)))


def _silu(x):
    return x * jax.nn.sigmoid(x)


def _l2n(x):
    return x * lax.rsqrt(jnp.sum(x * x, axis=-1, keepdims=True) + NORM_EPS)


def _delta_chunk_fn(cq, ck, cv, z, lg, a_log, dt_bias, norm_w, S, t_saved=None, want_t=False):
    C = B_CHUNK
    lane = lax.broadcasted_iota(jnp.int32, (C, LANE), 1)
    pick = lambda l0: jnp.concatenate(
        [jnp.sum(jnp.where(lane == l0 + h, lg, 0.0), axis=1, keepdims=True)[None] for h in range(B_HEADS)], axis=0)
    bl, al = pick(0), pick(B_HEADS)
    q = _l2n(_silu(cq)) * (B_HEAD_DIM ** -0.5)
    k = _l2n(_silu(ck))
    v = _silu(cv)
    beta = jax.nn.sigmoid(bl)
    g = -jnp.exp(a_log) * jax.nn.softplus(al + dt_bias)
    r = lax.broadcasted_iota(jnp.int32, (C, C), 0)
    c = lax.broadcasted_iota(jnp.int32, (C, C), 1)
    eye = r == c
    g_row = jnp.sum(jnp.where(eye, g, 0.0), axis=1, keepdims=True)
    gc = jnp.sum(jnp.where(c <= r, g_row, 0.0), axis=2, keepdims=True)
    gc_row = jnp.sum(jnp.where(eye, gc, 0.0), axis=1, keepdims=True)
    decay_incl = jnp.exp(jnp.where(r >= c, gc - gc_row, -jnp.inf))
    decay_strict = jnp.where(r > c, decay_incl, 0.0)
    kb = k * beta
    vb = v * beta
    kbf = k.astype(bf16)
    a_mat = _bdot("hik,hjk->hij", kb.astype(bf16), kbf) * decay_strict
    t_f32 = _tri_inv(a_mat) if t_saved is None else _tri_inv_saved(a_mat, t_saved)
    t_mat = t_f32.astype(bf16)
    eg = jnp.exp(gc)
    u = _bdot("hij,hjv->hiv", t_mat, vb.astype(bf16))
    w = _bdot("hij,hjk->hik", t_mat, (kb * eg).astype(bf16))
    qk = _bdot("hik,hjk->hij", q.astype(bf16), kbf) * decay_incl
    g_last = jnp.sum(g, axis=1, keepdims=True)
    k_tail = k * jnp.exp(g_last - gc)
    Sb = S.astype(bf16)
    v_new = u - _bdot("hck,hkv->hcv", w.astype(bf16), Sb)
    o = _bdot("hck,hkv->hcv", (q * eg).astype(bf16), Sb) + _bdot("hij,hjv->hiv", qk.astype(bf16), v_new.astype(bf16))
    S_new = S * jnp.exp(g_last) + _bdot("hck,hcv->hkv", k_tail.astype(bf16), v_new.astype(bf16))
    ob = o * lax.rsqrt(jnp.mean(o * o, axis=-1, keepdims=True) + NORM_EPS) * norm_w
    return (ob * _silu(z), S_new) + ((t_f32,) if want_t else ())


DELTA_CHUNKS_PER_STEP = 4


def _delta_in_specs(rev, N):
    C = DELTA_CHUNKS_PER_STEP * B_CHUNK
    ix = (lambda n: N - 1 - n) if rev else (lambda n: n)
    specs = [pl.BlockSpec((C, 3 * B_HEADS * LANE), lambda n: (ix(n), 0))]
    specs += [pl.BlockSpec((C, LANE), lambda n, h=h: (ix(n), CB_Z + h)) for h in range(B_HEADS)]
    specs += [
        pl.BlockSpec((C, LANE), lambda n: (ix(n), CB_LG)),
        pl.BlockSpec((B_HEADS, 1, 1), lambda n: (0, 0, 0)),
        pl.BlockSpec((B_HEADS, 1, 1), lambda n: (0, 0, 0)),
        pl.BlockSpec((1, LANE), lambda n: (0, 0)),
    ]
    return specs


def _delta_inputs(u, c_ref, z_refs, lg, al, dt, nw):
    H = B_HEADS
    rows = slice(u * B_CHUNK, (u + 1) * B_CHUNK)
    part = lambda p: jnp.stack([c_ref[rows, LANE * (p * H + h):LANE * (p * H + h + 1)] for h in range(H)])
    return (part(0), part(1), part(2), jnp.stack([z[rows, :] for z in z_refs]), lg[rows, :], al[...], dt[...], nw[...])


def _delta_fwd(name, c, proj, a_log, dt_bias, norm_w, o_ab):
    T = c.shape[0]
    C = B_CHUNK
    N = T // C
    Dh = B_HEAD_DIM
    H = B_HEADS

    def body(*refs):
        c_ref, z_refs, (lg, al, dt, nw) = refs[0], refs[1:1 + H], refs[1 + H:5 + H]
        o_ref, s_ref, t_ref, S = refs[6 + H:]

        @pl.when(pl.program_id(0) == 0)
        def _():
            S[...] = jnp.zeros_like(S)

        s = S[...]
        for u in range(U):
            s_ref[:, u] = s
            ob, s, t = _delta_chunk_fn(*_delta_inputs(u, c_ref, z_refs, lg, al, dt, nw), s, want_t=True)
            for h in range(H):
                o_ref[u * C:(u + 1) * C, LANE * h:LANE * (h + 1)] = ob[h]
            t_ref[:, u] = t
        S[...] = s

    U = DELTA_CHUNKS_PER_STEP
    return pl.pallas_call(
        body, name=name, grid=(N // U,),
        in_specs=_delta_in_specs(False, N // U) + [pl.BlockSpec(memory_space=pl.ANY)],
        out_specs=[pl.BlockSpec((U * C, H * LANE), lambda n: (n, 1)),
                   pl.BlockSpec((H, U, Dh, Dh), lambda n: (0, n, 0, 0)),
                   pl.BlockSpec((H, U, C, C), lambda n: (0, n, 0, 0))],
        out_shape=[jax.ShapeDtypeStruct(o_ab.shape, f32), jax.ShapeDtypeStruct((H, N, Dh, Dh), f32),
                   jax.ShapeDtypeStruct((H, N, C, C), f32)],
        input_output_aliases={5 + H: 0},
        scratch_shapes=[pltpu.VMEM((H, Dh, Dh), f32)],
        compiler_params=_cparams("arbitrary"),
    )(c, *([proj] * H), proj, a_log, dt_bias, norm_w, o_ab)


def _delta_bwd(name, c, proj, a_log, dt_bias, norm_w, s_saved, t_saved, d_oab, dproj):
    T = c.shape[0]
    C = B_CHUNK
    N = T // C
    Dh = B_HEAD_DIM
    H = B_HEADS

    def body(*refs):
        c_ref, z_refs, (lg, al, dt, nw) = refs[0], refs[1:1 + H], refs[1 + H:5 + H]
        s_ref, t_ref, do_ref = refs[5 + H:8 + H]
        dc, dtail, dal, ddt, dnw, dS = refs[9 + H:]

        @pl.when(pl.program_id(0) == 0)
        def _():
            dS[...] = jnp.zeros_like(dS)
            dal[...] = jnp.zeros_like(dal)
            ddt[...] = jnp.zeros_like(ddt)
            dnw[...] = jnp.zeros_like(dnw)

        ds = dS[...]
        for u in reversed(range(U)):
            rows = slice(u * C, (u + 1) * C)
            _, vjp = jax.vjp(functools.partial(_delta_chunk_fn, t_saved=t_ref[:, u]),
                             *_delta_inputs(u, c_ref, z_refs, lg, al, dt, nw), s_ref[:, u])
            do = jnp.stack([do_ref[rows, LANE * h:LANE * (h + 1)] for h in range(H)])
            g = vjp((do, ds))
            for h in range(H):
                for p in range(3):
                    dc[rows, LANE * (p * H + h):LANE * (p * H + h + 1)] = g[p][h]
                dtail[rows, LANE * h:LANE * (h + 1)] = g[3][h]
            dtail[rows, LANE * H:LANE * (H + 1)] = g[4]
            dtail[rows, LANE * (H + 1):] = jnp.zeros((C, LANE), f32)
            dal[...] += g[5]
            ddt[...] += g[6]
            dnw[...] += g[7]
            ds = g[8]
        dS[...] = ds

    U = DELTA_CHUNKS_PER_STEP
    NB = N // U
    rn = lambda n: NB - 1 - n
    return pl.pallas_call(
        body, name=name, grid=(NB,),
        in_specs=_delta_in_specs(True, NB) + [
            pl.BlockSpec((H, U, Dh, Dh), lambda n: (0, rn(n), 0, 0)),
            pl.BlockSpec((H, U, C, C), lambda n: (0, rn(n), 0, 0)),
            pl.BlockSpec((U * C, H * LANE), lambda n: (rn(n), 1)),
            pl.BlockSpec(memory_space=pl.ANY),
        ],
        out_specs=[
            pl.BlockSpec((U * C, 3 * H * LANE), lambda n: (rn(n), 0)),
            pl.BlockSpec((U * C, (H + 2) * LANE), lambda n: (rn(n), CB_Z // (H + 2))),
            pl.BlockSpec((H, 1, 1), lambda n: (0, 0, 0)),
            pl.BlockSpec((H, 1, 1), lambda n: (0, 0, 0)),
            pl.BlockSpec((1, LANE), lambda n: (0, 0)),
        ],
        out_shape=[jax.ShapeDtypeStruct((T, 3 * H * Dh), f32), jax.ShapeDtypeStruct(dproj.shape, f32),
                   jax.ShapeDtypeStruct((H, 1, 1), f32), jax.ShapeDtypeStruct((H, 1, 1), f32),
                   jax.ShapeDtypeStruct((1, LANE), f32)],
        input_output_aliases={8 + H: 1},
        scratch_shapes=[pltpu.VMEM((H, Dh, Dh), f32)],
        compiler_params=_cparams("arbitrary"),
    )(c, *([proj] * H), proj, a_log, dt_bias, norm_w, s_saved, t_saved, d_oab, dproj)


def _gate_matmuls(xc, wa_ref, wx_ref):
    bw = wa_ref.shape[-1]
    xb = xc.astype(bf16)
    blocks = [xb[:, bw * h:bw * (h + 1)] for h in range(LRU_BLOCKS)]
    return (jnp.concatenate([_dot(blocks[h], wa_ref[h], NN) for h in range(LRU_BLOCKS)], axis=1),
            jnp.concatenate([_dot(blocks[h], wx_ref[h], NN) for h in range(LRU_BLOCKS)], axis=1))


def _gates_fwd(name, xc, w_a, w_x, pars, tm=256):
    T, Wd = xc.shape
    tm = min(tm, T)

    def body(x_ref, wa_ref, wx_ref, ba, bx, lam, a_ref, b_ref):
        x = x_ref[...]
        pr, pi = _gate_matmuls(x, wa_ref, wx_ref)
        a_ref[...], b_ref[...] = _rglru_pre_fn(pr, pi, x, ba[...], bx[...], lam[...])

    row = pl.BlockSpec((tm, Wd), lambda i: (i, 0))
    return pl.pallas_call(
        body, name=name, grid=(T // tm,),
        in_specs=[row, _whole_spec(w_a), _whole_spec(w_x)] + [_whole_spec(p) for p in pars],
        out_specs=[row, row], out_shape=[jax.ShapeDtypeStruct((T, Wd), f32)] * 2,
        compiler_params=_cparams("parallel"),
    )(xc, w_a, w_x, *pars)


def _gates_bwd(name, xc, w_a, w_x, pars, lam_t, h_prev, tm=256):
    T, Wd = xc.shape
    tm = min(tm, T)
    bw = Wd // LRU_BLOCKS

    def body(x_ref, wa_ref, wx_ref, ba, bx, lam, lt_ref, hp_ref, dx_ref, dr_ref, di_ref, dba, dbx, dlam):
        x = x_ref[...]
        pr, pi = _gate_matmuls(x, wa_ref, wx_ref)
        _, vjp = jax.vjp(_rglru_pre_fn, pr, pi, x, ba[...], bx[...], lam[...])
        lt = lt_ref[...]
        dpr, dpi, dxc, g_ba, g_bx, g_lam = vjp((lt * hp_ref[...], lt))
        dprb, dpib = dpr.astype(bf16), dpi.astype(bf16)
        dx_ref[...] = dxc + jnp.concatenate(
            [_dot(dprb[:, bw * h:bw * (h + 1)], wa_ref[h], NT) + _dot(dpib[:, bw * h:bw * (h + 1)], wx_ref[h], NT)
             for h in range(LRU_BLOCKS)], axis=1)
        dr_ref[...] = dprb
        di_ref[...] = dpib

        @pl.when(pl.program_id(0) == 0)
        def _():
            dba[...] = jnp.zeros_like(dba)
            dbx[...] = jnp.zeros_like(dbx)
            dlam[...] = jnp.zeros_like(dlam)

        dba[...] += g_ba
        dbx[...] += g_bx
        dlam[...] += g_lam

    row = pl.BlockSpec((tm, Wd), lambda i: (i, 0))
    vec = pl.BlockSpec((1, Wd), lambda i: (0, 0))
    return pl.pallas_call(
        body, name=name, grid=(T // tm,),
        in_specs=[row, _whole_spec(w_a), _whole_spec(w_x)] + [_whole_spec(p) for p in pars] + [row, row],
        out_specs=[row, row, row, vec, vec, vec],
        out_shape=[jax.ShapeDtypeStruct((T, Wd), f32), jax.ShapeDtypeStruct((T, Wd), bf16),
                   jax.ShapeDtypeStruct((T, Wd), bf16)] + [jax.ShapeDtypeStruct((1, Wd), f32)] * 3,
        compiler_params=_cparams("arbitrary"),
    )(xc, w_a, w_x, *pars, lam_t, h_prev)


def _blockdiag_bwd_dw(name, xc, dpr, dpi, tk=512):
    T, Wd = xc.shape
    bw = Wd // LRU_BLOCKS
    tk = min(tk, T)

    def body(x_ref, dr, di, oa, ox):
        @pl.when(pl.program_id(1) == 0)
        def _():
            oa[...] = jnp.zeros_like(oa)
            ox[...] = jnp.zeros_like(ox)

        xb = x_ref[...].astype(bf16)
        oa[...] += _dot(xb, dr[...].astype(bf16), TN)
        ox[...] += _dot(xb, di[...].astype(bf16), TN)

    xs = pl.BlockSpec((tk, bw), lambda h, k: (k, h))
    ws = pl.BlockSpec((None, bw, bw), lambda h, k: (h, 0, 0))
    return pl.pallas_call(
        body, name=name, grid=(LRU_BLOCKS, T // tk), in_specs=[xs, xs, xs], out_specs=[ws, ws],
        out_shape=[jax.ShapeDtypeStruct((LRU_BLOCKS, bw, bw), f32)] * 2,
        compiler_params=_cparams("parallel", "arbitrary"),
    )(xc, dpr, dpi)


def _scan(name, a, proj, reverse, b=None, h=None, dhg=None, tt=512, cb=512):
    T, Wd = a.shape
    tt, cb = min(tt, T), min(cb, Wd)
    nt = T // tt
    ng = tt // SUBLANE

    def body(a_ref, g_ref, *rest):
        n_in = 2 if reverse else 1
        ins, outs, (carry, carry_a) = rest[:n_in], rest[n_in:-2], rest[-2:]

        @pl.when(pl.program_id(1) == 0)
        def _():
            carry[...] = jnp.zeros_like(carry)
            carry_a[...] = jnp.zeros_like(carry_a)

        row = lax.broadcasted_iota(jnp.int32, (SUBLANE, cb), 0)

        def step(gi, c):
            hp, ap = c
            g = (ng - 1 - gi) if reverse else gi
            rows = pl.ds(pl.multiple_of(g * SUBLANE, SUBLANE), SUBLANE)
            A = a_ref[rows, :]
            gate = g_ref[rows, :]
            a_first = jnp.broadcast_to(A[0:1, :], (SUBLANE, cb))
            if reverse:
                _, vjp = jax.vjp(_rec_gate_fn, ins[0][rows, :], gate)
                B, dgate = vjp((ins[1][rows, :],))
                outs[1][rows, :] = dgate
                A = jnp.where(row == SUBLANE - 1, ap, pltpu.roll(A, SUBLANE - 1, axis=0))
            else:
                B = ins[0][rows, :]
            for s in (1, 2, 4):
                sh = (SUBLANE - s) if reverse else s
                As = pltpu.roll(A, sh, axis=0)
                Bs = pltpu.roll(B, sh, axis=0)
                valid = (row < SUBLANE - s) if reverse else (row >= s)
                B = jnp.where(valid, A * Bs + B, B)
                A = jnp.where(valid, A * As, A)
            hcur = A * hp + B
            outs[0][rows, :] = hcur
            if not reverse:
                outs[1][rows, :] = jnp.where(row == 0, hp, pltpu.roll(hcur, 1, axis=0))
                outs[2][rows, :] = _rec_gate_fn(hcur, gate)[0]
            edge = hcur[0:1, :] if reverse else hcur[SUBLANE - 1:SUBLANE, :]
            return jnp.broadcast_to(edge, (SUBLANE, cb)), a_first

        carry[...], carry_a[...] = lax.fori_loop(0, ng, step, (carry[...], carry_a[...]))

    nc = Wd // cb
    tok = (lambda i: nt - 1 - i) if reverse else (lambda i: i)
    spec = pl.BlockSpec((tt, cb), lambda c, i: (tok(i), c))
    gate_half = pl.BlockSpec((tt, cb), lambda c, i: (tok(i), nc + c))
    if reverse:
        args, out_specs = (a, proj, h, dhg), [spec, gate_half]
        out_shape = [jax.ShapeDtypeStruct((T, Wd), f32), jax.ShapeDtypeStruct((T, 2 * Wd), f32)]
    else:
        args, out_specs = (a, proj, b), [spec] * 3
        out_shape = [jax.ShapeDtypeStruct((T, Wd), f32)] * 3
    return pl.pallas_call(
        body, name=name, grid=(nc, nt), in_specs=[spec, gate_half] + [spec] * (len(args) - 2), out_specs=out_specs,
        out_shape=out_shape,
        scratch_shapes=[pltpu.VMEM((SUBLANE, cb), f32), pltpu.VMEM((SUBLANE, cb), f32)],
        compiler_params=_cparams("parallel", "arbitrary"),
    )(*args)


def _relu2_epilogue(r):
    h = jnp.maximum(r, 0.0)
    return r, h * h


def _drelu2_epilogue(r, a):
    return (r * (2.0 * jnp.maximum(a.astype(f32), 0.0)),)


def _residual_cot(through, upper):
    return (through + DN_ALPHA * upper,)


def _merge_cols(name, g, tm=256):
    _, L, R, s = g.shape

    def body(g_ref, o_ref):
        for d in range(N_DEV):
            o_ref[:, s * d:s * (d + 1)] = g_ref[d].astype(bf16)
        o_ref[:, N_DEV * s:] = jnp.zeros((tm, HYB_PROJ_PAD - N_DEV * s), bf16)

    return pl.pallas_call(
        body, name=name, grid=(L, R // tm),
        in_specs=[pl.BlockSpec((N_DEV, None, tm, s), lambda l, i: (0, l, i, 0))],
        out_specs=pl.BlockSpec((None, tm, HYB_PROJ_PAD), lambda l, i: (l, i, 0)),
        out_shape=jax.ShapeDtypeStruct((L, R, HYB_PROJ_PAD), bf16),
        compiler_params=_cparams("parallel", "parallel"),
    )(g)


def _split_cols(name, dw, tm=256):
    R = dw.shape[0]
    s = HYB_PROJ // N_DEV

    def body(g_ref, o_ref):
        for d in range(N_DEV):
            o_ref[d] = g_ref[:, s * d:s * (d + 1)].astype(bf16)

    return pl.pallas_call(
        body, name=name, grid=(R // tm,),
        in_specs=[pl.BlockSpec((tm, HYB_PROJ_PAD), lambda i: (i, 0))],
        out_specs=pl.BlockSpec((N_DEV, tm, s), lambda i: (0, i, 0)),
        out_shape=jax.ShapeDtypeStruct((N_DEV, R, s), bf16),
        compiler_params=_cparams("parallel"),
    )(dw)


def _rows_to_dev(dw):
    nb, r, c = dw.shape
    t = dw.reshape(nb, N_DEV, r // N_DEV, c)
    return jnp.moveaxis(t, 1, 0).reshape(N_DEV, nb * (r // N_DEV), c).astype(bf16)


def _ln_epilogue(r, x, g, b):
    return r, _ln_res_fn(x, r, g, b)[0]


def _hybrid_fwd(tag, x, W, j, cos, sin, ln, before_out):
    proj = _mm(f"{tag}_proj", x, W["hyb_w_in"][j], "nn", b_kind="lead", b_lead=0)
    o_a = _attn_fwd(f"{tag}_attn", proj, cos, sin, W["hyb_sinks"][j][None, :])
    c = _conv_fwd(f"{tag}_conv", proj, CB_CONV, 12, W["hyb_conv_w"][j], None)
    o_ab, s_saved, t_saved = _delta_fwd(f"{tag}_delta", c, proj, W["hyb_a_log"][j].reshape(B_HEADS, 1, 1),
                                        W["hyb_dt_bias"][j].reshape(B_HEADS, 1, 1), W["hyb_norm_w"][j][None, :], o_a)
    before_out(o_ab)
    mix, x1 = _mm(f"{tag}_out", o_ab, W["hyb_w_out"][j], "nn", b_kind="lead", b_lead=0, epilogue=_ln_epilogue,
                  extras=(x,), params=ln, out_dtypes=(f32, f32), tm=512)
    return mix, x1, (proj, c, s_saved, t_saved, o_ab)


def _hybrid_bwd(tag, x, dmix, addend, W, j, cos, sin, saved, G, send_early):
    proj, c, s_saved, t_saved, o_ab = saved
    T = x.shape[0]
    d_oab = _mm(f"{tag}_dout", dmix, W["hyb_w_out"][j], "nt", b_kind="lead", b_lead=0)
    G["hyb_w_out"][j] = _mm(f"{tag}_dwout", o_ab, dmix, "tn", out_dtypes=(bf16,)).reshape(N_DEV, -1, D_MODEL)
    sinks = W["hyb_sinks"][j][None, :] + send_early({("hyb_w_out", j): G["hyb_w_out"][j]})
    dproj, dsinks = _attn_bwd(f"{tag}_dattn", proj, cos, sin, sinks, d_oab)
    a_log = W["hyb_a_log"][j].reshape(B_HEADS, 1, 1)
    dt_bias = W["hyb_dt_bias"][j].reshape(B_HEADS, 1, 1)
    dc, dproj, dal, ddt, dnw = _delta_bwd(f"{tag}_ddelta", c, proj, a_log, dt_bias, W["hyb_norm_w"][j][None, :],
                                          s_saved, t_saved, d_oab, dproj)
    dproj, dconv_w, _ = _conv_bwd(f"{tag}_dconv", dc, proj, CB_CONV, 12, W["hyb_conv_w"][j], dproj, CB_CONV)
    dx = _mm(f"{tag}_dx", dproj, W["hyb_w_in"][j], "nt", b_kind="lead", b_lead=0,
             **({} if addend is None else dict(epilogue=_residual_cot, extras=(addend,))))
    G["hyb_w_in"][j] = _split_cols(f"{tag}_dwin_split", _mm(f"{tag}_dwin", x, dproj, "tn", tn=1536))
    G["hyb_sinks"][j] = dsinks[0]
    G["hyb_conv_w"][j] = dconv_w
    G["hyb_a_log"][j] = dal.reshape(B_HEADS)
    G["hyb_dt_bias"][j] = ddt.reshape(B_HEADS)
    G["hyb_norm_w"][j] = dnw[0]
    return dx


def _rec_fwd(tag, x, W, j, ln, before_out):
    Wd = D_MODEL
    proj = _mm(f"{tag}_proj", x, W["rec_w_in"][j], "nn", b_kind="devcol", b_lead=0)
    xc = _conv_fwd(f"{tag}_conv", proj, 0, Wd // LANE, W["rec_conv_w"][j], W["rec_conv_b"][j][None, :])
    pars = [W["rec_b_a"][j][None, :], W["rec_b_x"][j][None, :], W["rec_lambda"][j][None, :]]
    a, b = _gates_fwd(f"{tag}_gates", xc, W["rec_w_a"][j][0], W["rec_w_x"][j][0], pars)
    h, h_prev, hg = _scan(f"{tag}_scan", a, proj, False, b=b)
    before_out(hg)
    mix, x1 = _mm(f"{tag}_out", hg, W["rec_w_out"][j], "nn", b_kind="lead", b_lead=0, epilogue=_ln_epilogue,
                  extras=(x,), params=ln, out_dtypes=(f32, f32), tm=512)
    return mix, x1, (proj, xc, a, h, h_prev, hg)


def _rec_bwd(tag, x, dmix, addend, W, j, saved, G, send_early):
    proj, xc, a, h, h_prev, hg = saved
    Wd = D_MODEL
    dhg = _mm(f"{tag}_dout", dmix, W["rec_w_out"][j], "nt", b_kind="lead", b_lead=0)
    G["rec_w_out"][j] = _mm(f"{tag}_dwout", hg, dmix, "tn", out_dtypes=(bf16,)).reshape(N_DEV, -1, D_MODEL)
    sent = send_early({("rec_w_out", j): G["rec_w_out"][j]})
    lam_t, dproj = _scan(f"{tag}_dscan", a, proj, True, h=h, dhg=dhg)
    pars = [W["rec_b_a"][j][None, :] + sent, W["rec_b_x"][j][None, :], W["rec_lambda"][j][None, :]]
    dxc, dpr, dpi, db_a, db_x, dlam = _gates_bwd(f"{tag}_dgates", xc, W["rec_w_a"][j][0], W["rec_w_x"][j][0], pars,
                                                 lam_t, h_prev)
    dwa, dwx = _blockdiag_bwd_dw(f"{tag}_dgates_dw", xc, dpr, dpi)
    G["rec_w_a"][j], G["rec_w_x"][j] = _rows_to_dev(dwa), _rows_to_dev(dwx)
    dproj, dconv_w, dconv_b = _conv_bwd(f"{tag}_dconv", dxc, proj, 0, Wd // LANE, W["rec_conv_w"][j], dproj, 0)
    dx = _mm(f"{tag}_dx", dproj, W["rec_w_in"][j], "nt", b_kind="devcol", b_lead=0,
             **({} if addend is None else dict(epilogue=_residual_cot, extras=(addend,))))
    G["rec_w_in"][j] = _mm(f"{tag}_dwin", x, dproj, "tn", o_kind="devcol", out_dtypes=(bf16,), tn=2048)
    G["rec_conv_w"][j] = dconv_w
    G["rec_conv_b"][j] = dconv_b
    G["rec_b_a"][j] = db_a[0]
    G["rec_b_x"][j] = db_x[0]
    G["rec_lambda"][j] = dlam[0]
    return dx


def _local_step(x, target, W, load_layer, grads_ready):
    T = x.shape[0]
    cos, sin = _rope_tables(T)
    saved = []
    for layer in range(DEPTH):
        j = layer // 2
        tag = f"L{layer}"
        load_layer(layer, 0, x)
        ln1 = (W["ln1_g"][layer][None, :], W["ln1_b"][layer][None, :])
        before_out = functools.partial(load_layer, layer, 1)
        if layer % 2 == 0:
            mix, x1, sv = _hybrid_fwd(tag, x, W, j, cos, sin, ln1, before_out)
        else:
            mix, x1, sv = _rec_fwd(tag, x, W, j, ln1, before_out)
        load_layer(layer, 2, x1)
        a, h2 = _mm(f"{tag}_mlp1", x1, W["mlp_w1"][layer], "nn", b_kind="devcol", b_lead=0, epilogue=_relu2_epilogue,
                    out_dtypes=(bf16, bf16), tm=2048)
        ln2 = (W["ln2_g"][layer][None, :], W["ln2_b"][layer][None, :])
        y, x2 = _mm(f"{tag}_mlp2", h2, W["mlp_w2"][layer], "nn", b_kind="lead", b_lead=0, epilogue=_ln_epilogue,
                    extras=(x1,), params=ln2, out_dtypes=(f32, f32))
        saved.append((x, sv, mix, x1, a, h2, y))
        x = x2
    loss, dx = _loss_head(x, target)

    G = {k: [None] * (DEPTH if k.startswith(("ln", "mlp")) else DEPTH // 2) for k in (
        "hyb_w_in", "hyb_sinks", "hyb_conv_w", "hyb_a_log", "hyb_dt_bias", "hyb_norm_w", "hyb_w_out",
        "rec_w_in", "rec_conv_w", "rec_conv_b", "rec_w_a", "rec_b_a", "rec_w_x", "rec_b_x", "rec_lambda", "rec_w_out",
        "ln1_g", "ln1_b", "mlp_w1", "mlp_w2", "ln2_g", "ln2_b")}
    order = jnp.zeros((1, 1), f32)
    cot_rows, cot_fn = [(dx, 0, D_MODEL)], None
    for layer in reversed(range(DEPTH)):
        j = layer // 2
        tag = f"L{layer}"
        x0, sv, mix, x1, a, h2, y = saved[layer]
        ln2 = [W["ln2_g"][layer][None, :] + order, W["ln2_b"][layer][None, :]]
        (dy,), (dg2, db2) = _tl_bwd(f"{tag}_dln2", _ln_res_fn, [(x1, 0, D_MODEL), (y, 0, D_MODEL)], ln2,
                                    cot_rows, cot_fn=cot_fn, skip=(0,))
        G["ln2_g"][layer], G["ln2_b"][layer] = dg2[0], db2[0]
        da = _mm(f"{tag}_dmlp2", dy, W["mlp_w2"][layer], "nt", b_kind="lead", b_lead=0, epilogue=_drelu2_epilogue,
                 extras=(a,), out_dtypes=(bf16,), tm=2048, tn=512)
        G["mlp_w2"][layer] = _mm(f"{tag}_dw2", h2, dy, "tn", out_dtypes=(bf16,), tm=2048).reshape(N_DEV, -1, D_MODEL)
        dx1 = _mm(f"{tag}_dmlp1", da, W["mlp_w1"][layer], "nt", b_kind="devcol", b_lead=0, tm=2048)
        G["mlp_w1"][layer] = _mm(f"{tag}_dw1", x1, da, "tn", o_kind="devcol", out_dtypes=(bf16,), tn=2048)
        ln1 = [W["ln1_g"][layer][None, :], W["ln1_b"][layer][None, :]]
        (dmix,), (dg1, db1) = _tl_bwd(f"{tag}_dln1", _ln_res_fn, [(x0, 0, D_MODEL), (mix, 0, D_MODEL)], ln1,
                                      [(dx1, 0, D_MODEL), (dy, 0, D_MODEL)], cot_fn=_residual_cot, skip=(0,))
        G["ln1_g"][layer], G["ln1_b"][layer] = dg1[0], db1[0]
        dx0_a = dmix if layer == 0 else None
        early = functools.partial(grads_ready, f"l{layer}_early",
                                  {(k, layer): G[k][layer] for k in ("mlp_w1", "mlp_w2")})
        if layer % 2 == 0:
            dx = _hybrid_bwd(tag, x0, dmix, dx0_a, W, j, cos, sin, sv, G, early)
        else:
            dx = _rec_bwd(tag, x0, dmix, dx0_a, W, j, sv, G, early)
        order = grads_ready(f"l{layer}_late", {}, {(k, i): G[k][i] for k, i in _layer_weights(layer)[:-2]
                                                  if not k.endswith("w_out")})
        cot_rows, cot_fn = [(dx, 0, D_MODEL), (dmix, 0, D_MODEL)], _residual_cot
    big = {k for k, _ in BIG}
    return loss, dx, {k: jnp.stack(v) for k, v in G.items() if k not in big}


def _layer_weights(layer):
    j = layer // 2
    mixer = ["hyb_w_in", "hyb_w_out"] if layer % 2 == 0 else ["rec_w_in", "rec_w_out", "rec_w_a", "rec_w_x"]
    return [(k, j) for k in mixer] + [("mlp_w1", layer), ("mlp_w2", layer)]


def _my_coords():
    return lax.axis_index("x"), lax.axis_index("y"), lax.axis_index("c")


def _all_gather(name, arrays):
    na = len(arrays)

    def body(*refs):
        x_refs, out_refs = refs[:na], refs[na:2 * na]
        send_sems, recv_sems, local_sems = refs[2 * na:]
        x, y, c = _my_coords()
        me, sibling = (x, y, c), (x, y, 1 - c)
        chips = [(1 - x, y), (x, 1 - y), (1 - x, 1 - y)]

        def blk(a, px, py, pc):
            return out_refs[a].at[4 * px + 2 * py + pc]

        def copy(a, k, block, to, src=None):
            return pltpu.make_async_remote_copy(
                src_ref=blk(a, *block) if src is None else src, dst_ref=blk(a, *block),
                send_sem=send_sems.at[a, k], recv_sem=recv_sems.at[a, k],
                device_id=to, device_id_type=pl.DeviceIdType.MESH)

        mine = [pltpu.make_async_copy(x_refs[a], blk(a, *me), local_sems.at[a]) for a in range(na)]
        for cp in mine:
            cp.start()
        first = []
        for a in range(na):
            first.append(copy(a, 0, me, sibling, src=x_refs[a]))
            first += [copy(a, 1 + j, me, (*chip, c), src=x_refs[a]) for j, chip in enumerate(chips)]
        for cp in first:
            cp.start()
        passed = []
        for a in range(na):
            for j, chip in enumerate(chips):
                copy(a, 1 + j, (*chip, c), me).wait_recv()
                passed.append(copy(a, 4 + j, (*chip, c), sibling))
                passed[-1].start()
        for a in range(na):
            copy(a, 0, sibling, me).wait_recv()
            for j, chip in enumerate(chips):
                copy(a, 4 + j, (*chip, 1 - c), me).wait_recv()
        for cp in first + passed:
            cp.wait_send()
        for cp in mine:
            cp.wait()

    return pl.pallas_call(
        body, name=name,
        out_shape=[jax.ShapeDtypeStruct((N_DEV,) + a.shape, a.dtype) for a in arrays],
        in_specs=[pl.BlockSpec(memory_space=pl.ANY)] * na,
        out_specs=[pl.BlockSpec(memory_space=pl.ANY)] * na,
        scratch_shapes=[pltpu.SemaphoreType.DMA((na, 7)), pltpu.SemaphoreType.DMA((na, 7)),
                        pltpu.SemaphoreType.DMA((na,))],
    )(*arrays)


_HBM = pl.BlockSpec(memory_space=pltpu.HBM)
_SEM = pl.BlockSpec(memory_space=pltpu.SEMAPHORE)


def _flip(k, x, y, c):
    return ((1 - x) if k & 4 else x, (1 - y) if k & 2 else y, (1 - c) if k & 1 else c)


_PEERS = {"gather": (1, 2, 4, 6), "scatter": (1, 2, 3, 4, 5, 6, 7)}


def _push_copies(kind, x_refs, land_refs, send_sems, recv_sems, local_sems):
    x, y, c = _my_coords()
    me = 4 * x + 2 * y + c
    peers = _PEERS[kind]
    remote, local = [], []
    for a in range(len(x_refs)):
        local.append(pltpu.make_async_copy(x_refs[a] if kind == "gather" else x_refs[a].at[me], land_refs[a].at[me],
                                           local_sems.at[a]))
        for n, k in enumerate(peers):
            px, py, pc = _flip(k, x, y, c)
            remote.append(pltpu.make_async_remote_copy(
                src_ref=x_refs[a] if kind == "gather" else x_refs[a].at[4 * px + 2 * py + pc],
                dst_ref=land_refs[a].at[me],
                send_sem=send_sems.at[a * len(peers) + n], recv_sem=recv_sems.at[a * len(peers) + n],
                device_id=(px, py, pc), device_id_type=pl.DeviceIdType.MESH))
    return remote, local


def _pass_to_sibling(name, lands):
    na = len(lands)
    chips = (2, 4, 6)

    def body(*refs):
        out_refs, send_sems, recv_sems = refs[na:2 * na], refs[2 * na], refs[2 * na + 1]
        x, y, c = _my_coords()
        cps = []
        for a in range(na):
            for n, k in enumerate(chips):
                px, py, _ = _flip(k, x, y, c)
                cps.append(pltpu.make_async_remote_copy(
                    src_ref=out_refs[a].at[4 * px + 2 * py + c], dst_ref=out_refs[a].at[4 * px + 2 * py + c],
                    send_sem=send_sems.at[a * 3 + n], recv_sem=recv_sems.at[a * 3 + n],
                    device_id=(x, y, 1 - c), device_id_type=pl.DeviceIdType.MESH))
        for cp in cps:
            cp.start()
        for a in range(na):
            for n, k in enumerate(chips):
                px, py, _ = _flip(k, x, y, c)
                blk = out_refs[a].at[4 * px + 2 * py + (1 - c)]
                pltpu.make_async_remote_copy(src_ref=blk, dst_ref=blk, send_sem=send_sems.at[a * 3 + n],
                                             recv_sem=recv_sems.at[a * 3 + n], device_id=(x, y, 1 - c),
                                             device_id_type=pl.DeviceIdType.MESH).wait_recv()
        for cp in cps:
            cp.wait_send()

    return pl.pallas_call(
        body, name=name,
        out_shape=[jax.ShapeDtypeStruct(l.shape, l.dtype) for l in lands],
        in_specs=[pl.BlockSpec(memory_space=pl.ANY)] * na,
        out_specs=[pl.BlockSpec(memory_space=pl.ANY)] * na,
        input_output_aliases={a: a for a in range(na)},
        scratch_shapes=[pltpu.SemaphoreType.DMA((3 * na,)), pltpu.SemaphoreType.DMA((3 * na,))],
    )(*lands)


_SIDE_EFFECT = pltpu.CompilerParams(has_side_effects=pltpu.SideEffectType.DATAFLOW_SIDE_EFFECTING)


def _push_start(name, kind, srcs, lands):
    na = len(srcs)

    def body(*refs):
        remote, local = _push_copies(kind, refs[:na], refs[na:2 * na], *refs[2 * na:2 * na + 3])
        for cp in remote + local:
            cp.start()
        token = refs[-1]
        token[...] = jnp.zeros_like(token)

    arrays = list(srcs) + list(lands)
    n_remote = na * len(_PEERS[kind])
    res = pl.pallas_call(
        body, name=name,
        out_shape=(pltpu.SemaphoreType.DMA((n_remote,)), pltpu.SemaphoreType.DMA((n_remote,)),
                   pltpu.SemaphoreType.DMA((na,)), *[pltpu.HBM(t.shape, t.dtype) for t in arrays],
                   jax.ShapeDtypeStruct((SUBLANE, LANE), f32)),
        in_specs=[_HBM] * (2 * na),
        out_specs=(_SEM, _SEM, _SEM, *[_HBM] * (2 * na), pl.BlockSpec(memory_space=pltpu.VMEM)),
        input_output_aliases={i: 3 + i for i in range(2 * na)},
        compiler_params=_SIDE_EFFECT,
    )(*[pltpu.with_memory_space_constraint(t, pltpu.HBM) for t in arrays])
    return list(res[:3]), res[3:3 + na], res[3 + na:3 + 2 * na], res[-1][:1, :1]


def _push_wait(name, kind, sems, srcs, lands, after):
    na = len(srcs)

    def body(*refs):
        remote, local = _push_copies(kind, refs[:na], refs[na:2 * na], *refs[2 * na:2 * na + 3])
        for cp in remote:
            cp.wait_send()
            cp.wait_recv()
        for cp in local:
            cp.wait()

    arrays = list(srcs) + list(lands)
    res = pl.pallas_call(
        body, name=name,
        out_shape=tuple(pltpu.HBM(t.shape, t.dtype) for t in arrays),
        in_specs=[_HBM] * (2 * na) + [_SEM] * 3 + [pl.BlockSpec(memory_space=pl.ANY)],
        out_specs=tuple([_HBM] * (2 * na)),
        input_output_aliases={i: i for i in range(2 * na)},
        compiler_params=_SIDE_EFFECT,
    )(*arrays, *sems, after)
    return res[na:]


def _sum_blocks(name, land):
    _, R, n = land.shape
    tr = R

    def body(l_ref, o_ref):
        acc = l_ref[0].astype(f32)
        for s in range(1, N_DEV):
            acc = acc + l_ref[s].astype(f32)
        o_ref[...] = acc

    return pl.pallas_call(
        body, name=name, grid=(R // tr,),
        in_specs=[pl.BlockSpec((N_DEV, tr, n), lambda i: (0, i, 0))],
        out_specs=pl.BlockSpec((tr, n), lambda i: (i, 0)),
        out_shape=jax.ShapeDtypeStruct((R, n), f32),
        compiler_params=_cparams("parallel"),
    )(land)


def _adamw(name, w, g, m, v):
    shape = w.shape
    last = shape[-1]
    rows = math.prod(shape[:-1])
    tm = 256 if rows % 256 == 0 and rows > 256 else rows
    w2, g2, m2, v2 = (t.reshape(rows, last) for t in (w, g, m, v))

    def body(w_ref, g_ref, m_ref, v_ref, d_ref, mo_ref, vo_ref):
        gg = g_ref[...]
        mn = ADAM_B1 * m_ref[...] + (1.0 - ADAM_B1) * gg
        vn = ADAM_B2 * v_ref[...] + (1.0 - ADAM_B2) * jnp.square(gg)
        m_hat = mn / (1.0 - ADAM_B1 ** ADAM_STEP)
        v_hat = vn / (1.0 - ADAM_B2 ** ADAM_STEP)
        d_ref[...] = -ADAM_LR * (m_hat / (jnp.sqrt(v_hat) + ADAM_EPS) + ADAM_WD * w_ref[...])
        mo_ref[...] = mn
        vo_ref[...] = vn

    spec = pl.BlockSpec((tm, last), lambda i: (i, 0))
    d, mn, vn = pl.pallas_call(
        body, name=name, grid=(rows // tm,), in_specs=[spec] * 4, out_specs=[spec] * 3,
        out_shape=[jax.ShapeDtypeStruct((rows, last), f32)] * 3,
        compiler_params=_cparams("parallel"),
    )(w2, g2, m2, v2)
    return d.reshape(shape), mn.reshape(shape), vn.reshape(shape)


def _adamw_land(name, lands, w, m, v, tm=256):
    L = len(lands)
    _, R, C = lands[0].shape
    tm = min(tm, R)

    def body(*refs):
        l_refs, (w_ref, m_ref, v_ref, g_ref, d_ref, mo_ref, vo_ref) = refs[:L], refs[L:]
        for k in range(L):
            @pl.when(pl.program_id(0) == k)
            def _(k=k):
                gg = l_refs[k][0].astype(f32)
                for s in range(1, N_DEV):
                    gg = gg + l_refs[k][s].astype(f32)
                g_ref[...] = gg
                mn = ADAM_B1 * m_ref[...] + (1.0 - ADAM_B1) * gg
                vn = ADAM_B2 * v_ref[...] + (1.0 - ADAM_B2) * jnp.square(gg)
                m_hat = mn / (1.0 - ADAM_B1 ** ADAM_STEP)
                v_hat = vn / (1.0 - ADAM_B2 ** ADAM_STEP)
                d_ref[...] = -ADAM_LR * (m_hat / (jnp.sqrt(v_hat) + ADAM_EPS) + ADAM_WD * w_ref[...])
                mo_ref[...] = mn
                vo_ref[...] = vn

    land_specs = [pl.BlockSpec((N_DEV, tm, C), lambda l, i, k=k: (0, jnp.where(l == k, i, 0), 0)) for k in range(L)]
    spec = pl.BlockSpec((None, tm, C), lambda l, i: (l, i, 0))
    return pl.pallas_call(
        body, name=name, grid=(L, R // tm),
        in_specs=land_specs + [spec] * 3,
        out_specs=[spec] * 4,
        out_shape=[jax.ShapeDtypeStruct((L, R, C), f32)] * 4,
        compiler_params=_cparams("arbitrary", "arbitrary"),
    )(*lands, w, m, v)


BIG = [("hyb_w_in", 2), ("hyb_w_out", 1), ("rec_w_in", 2), ("rec_w_out", 1), ("rec_w_a", 2), ("rec_w_x", 2),
       ("mlp_w1", 2), ("mlp_w2", 1)]
SMALL = [("hyb_conv_w", 2), ("rec_conv_w", 2), ("rec_conv_b", 1), ("rec_b_a", 1), ("rec_b_x", 1), ("rec_lambda", 1)]
REPL = ["hyb_sinks", "hyb_a_log", "hyb_dt_bias", "hyb_norm_w", "ln1_g", "ln1_b", "ln2_g", "ln2_b"]
WEIGHTS = ["hyb_w_in", "hyb_sinks", "hyb_conv_w", "hyb_a_log", "hyb_dt_bias", "hyb_norm_w", "hyb_w_out", "rec_w_in",
           "rec_conv_w", "rec_conv_b", "rec_w_a", "rec_b_a", "rec_w_x", "rec_b_x", "rec_lambda", "rec_w_out",
           "ln1_g", "ln1_b", "mlp_w1", "mlp_w2", "ln2_g", "ln2_b"]


def _pack_rows(parts, dtype, row_mult):
    lead = parts[0].shape[:-1]
    flat = jnp.concatenate([p.astype(dtype) for p in parts], axis=-1)
    n = flat.shape[-1]
    unit = row_mult * LANE
    pad = (-n) % unit
    if pad:
        flat = jnp.concatenate([flat, jnp.zeros(lead + (pad,), dtype)], axis=-1)
    return flat.reshape(lead + ((n + pad) // LANE, LANE))


def _gather_full(gathered, shard_shapes, table):
    flat = gathered.reshape(N_DEV, -1)
    out, off = {}, 0
    for name, ax in table:
        shp = shard_shapes[name]
        n = math.prod(shp)
        arr = flat[:, off:off + n].reshape((N_DEV,) + shp)
        off += n
        arr = jnp.moveaxis(arr, 0, ax)
        out[name] = arr.reshape(shp[:ax] + (N_DEV * shp[ax],) + shp[ax + 1:])
    return out


def _matmul_layouts(tag, gw):
    out = {}
    bw = D_MODEL // LRU_BLOCKS
    for k, g in gw.items():
        L = g.shape[1]
        if k == "hyb_w_in":
            out[k] = _merge_cols(f"{tag}_w_in_merge", g)
        elif k in ("hyb_w_out", "rec_w_out", "mlp_w2"):
            out[k] = jnp.swapaxes(g, 0, 1).reshape(L, N_DEV * g.shape[2], g.shape[3])
        elif k in ("rec_w_a", "rec_w_x"):
            out[k] = jnp.moveaxis(g, 0, 2).reshape(L, LRU_BLOCKS, bw, bw)
        else:
            out[k] = g
    return out


def kernel(x, hyb_w_in, hyb_sinks, hyb_conv_w, hyb_a_log, hyb_dt_bias, hyb_norm_w, hyb_w_out, rec_w_in, rec_conv_w, rec_conv_b, rec_w_a, rec_b_a, rec_w_x, rec_b_x, rec_lambda, rec_w_out, ln1_g, ln1_b, mlp_w1, mlp_w2, ln2_g, ln2_b, loss_target, m_hyb_w_in, m_hyb_sinks, m_hyb_conv_w, m_hyb_a_log, m_hyb_dt_bias, m_hyb_norm_w, m_hyb_w_out, m_rec_w_in, m_rec_conv_w, m_rec_conv_b, m_rec_w_a, m_rec_b_a, m_rec_w_x, m_rec_b_x, m_rec_lambda, m_rec_w_out, m_ln1_g, m_ln1_b, m_mlp_w1, m_mlp_w2, m_ln2_g, m_ln2_b, v_hyb_w_in, v_hyb_sinks, v_hyb_conv_w, v_hyb_a_log, v_hyb_dt_bias, v_hyb_norm_w, v_hyb_w_out, v_rec_w_in, v_rec_conv_w, v_rec_conv_b, v_rec_w_a, v_rec_b_a, v_rec_w_x, v_rec_b_x, v_rec_lambda, v_rec_w_out, v_ln1_g, v_ln1_b, v_mlp_w1, v_mlp_w2, v_ln2_g, v_ln2_b):
    args = locals()
    w = {k: args[k] for k in WEIGHTS}
    m = {k: args["m_" + k] for k in WEIGHTS}
    v = {k: args["v_" + k] for k in WEIGHTS}
    shard_shapes = {k: tuple(t.shape) for k, t in w.items()}
    xi, yi, ci = _my_coords()
    me = 4 * xi + 2 * yi + ci

    in_flight = {}

    def install(tag, names, got):
        for (k, i), arr in zip(names, _matmul_layouts(tag, {k: g for (k, _), g in zip(names, got)}).values()):
            W[k][i] = arr

    def start_gather(tag, names):
        srcs = [w[k][i:i + 1].astype(bf16) for k, i in names]
        *pending, zero = _push_start(f"gather_{tag}_start", "gather", srcs,
                                     [lax.empty((N_DEV,) + s.shape, bf16) for s in srcs])
        in_flight[tag] = (names, pending)
        return zero

    def finish_gather(tag, after):
        names, pending = in_flight.pop(tag)
        half = _push_wait(f"gather_{tag}_wait", "gather", *pending, after)
        install(tag, names, _pass_to_sibling(f"gather_{tag}_pass", half))

    def started(k, zero):
        W[k] = W[k] + zero

    def mixer_w(layer):
        return _layer_weights(layer)[:-2]

    def mlp_w(layer):
        return _layer_weights(layer)[-2:]

    gathered0 = _all_gather("gather_first", [w[k][i:i + 1].astype(bf16) for k, i in mixer_w(0)]
                            + [_pack_rows([w[k].reshape(-1) for k, _ in SMALL], f32, SUBLANE)])
    W = _gather_full(gathered0[-1], shard_shapes, SMALL)
    W.update({k: w[k] for k in REPL})
    W.update({k: {} for k, _ in BIG})
    install("l0a", mixer_w(0), gathered0[:-1])
    started("hyb_sinks", start_gather("l0b", mlp_w(0)) + start_gather("l1a", mixer_w(1)))

    def load_layer(layer, stage, after):
        if stage == 0:
            if layer > 0:
                finish_gather(f"l{layer}a", after)
            if 0 < layer < DEPTH - 1:
                started("hyb_sinks" if layer % 2 == 0 else "rec_conv_b",
                        start_gather(f"l{layer + 1}a", mixer_w(layer + 1)))
        if stage == 2:
            finish_gather(f"l{layer}b", after)
            if layer < DEPTH - 1:
                started("ln2_g", start_gather(f"l{layer + 1}b", mlp_w(layer + 1)))

    grads_in_flight = {}

    def grads_ready(tag, a, b):
        g = {**a, **b}
        srcs = list(g.values())
        *pending, zero = _push_start(f"scatter_{tag}_start", "scatter", srcs, [lax.empty(s.shape, bf16) for s in srcs])
        grads_in_flight[tag] = (list(g.keys()), pending)
        return zero

    loss_local, grad_x, G = _local_step(x[0], loss_target[0], W, load_layer, grads_ready)
    loss = lax.psum(loss_local, MESH_AXES)

    landed = {}

    def land(tag, after):
        keys, pending = grads_in_flight[tag]
        landed.update(zip(keys, _push_wait(f"scatter_{tag}_wait", "scatter", *pending, after)))

    tags = list(grads_in_flight)
    for tag in tags[:-1]:
        land(tag, grad_x)
    rest = _pack_rows([G[k].reshape(-1) for k, _ in SMALL] + [G[k].reshape(-1) for k in REPL], f32, SUBLANE)
    g_rest = _sum_blocks("sum_rest", _all_gather("gather_rest", [rest])[0]).reshape(-1)

    grads, delta, new_m, new_v = {}, {}, {}, {}

    def adamw_big(k):
        shp = shard_shapes[k]
        s3 = (shp[0], math.prod(shp[1:-1]), shp[-1])
        lands = [landed[(k, i)].reshape((N_DEV,) + s3[1:]) for i in range(shp[0])]
        res = _adamw_land("adamw_" + k, lands, w[k].reshape(s3), m[k].reshape(s3), v[k].reshape(s3))
        grads[k], delta[k], new_m[k], new_v[k] = (r.reshape(shp) for r in res)

    late = {k for k, _ in grads_in_flight[tags[-1]][0]}
    for k in [k for k, _ in BIG if k not in late]:
        adamw_big(k)
        done = new_v[k]
    land(tags[-1], done)
    for k in [k for k, _ in BIG if k in late]:
        adamw_big(k)
    off = 0
    for k, ax in SMALL:
        full_shape = G[k].shape
        n = math.prod(full_shape)
        full = g_rest[off:off + n].reshape(full_shape)
        off += n
        s = shard_shapes[k][ax]
        grads[k] = lax.dynamic_slice_in_dim(full, me * s, s, axis=ax)
    for k in REPL:
        n = math.prod(shard_shapes[k])
        grads[k] = g_rest[off:off + n].reshape(shard_shapes[k])
        off += n

    for k in [k for k, _ in SMALL] + REPL:
        delta[k], new_m[k], new_v[k] = _adamw("adamw_" + k, w[k], grads[k], m[k], v[k])

    return (loss, grad_x[None], *[grads[k] for k in WEIGHTS], *[delta[k] for k in WEIGHTS],
            *[new_m[k] for k in WEIGHTS], *[new_v[k] for k in WEIGHTS])
```

```python
import functools
import math

import jax
import jax.numpy as jnp
from jax import lax
from jax.experimental import pallas as pl
from jax.experimental.pallas import tpu as pltpu

f32 = jnp.float32
bf16 = jnp.bfloat16

N_DEV = 8
D_MODEL = 1024
DEPTH = 4
A_HEAD_DIM = 64
A_Q_HEADS = 8
WINDOW = 128
ROPE_THETA = 10000.0
B_HEADS = 4
B_HEAD_DIM = 128
B_CHUNK = 64
LRU_BLOCKS = 4
LRU_C = 8.0
D_FF = 4 * D_MODEL
HYB_PROJ = 2824
HYB_PROJ_PAD = 3072
DN_ALPHA = (2 * DEPTH) ** 0.25
LN_EPS = 1e-5
NORM_EPS = 1e-6
ADAM_LR = 0.001
ADAM_B1 = 0.9
ADAM_B2 = 0.999
ADAM_EPS = 1e-08
ADAM_WD = 0.01
ADAM_STEP = 10

LANE = 128
SUBLANE = 8
VMEM_LIMIT = 48 * 1024 * 1024

CB_QA, CB_KA, CB_VA, CB_CONV, CB_Z, CB_LG = 0, 4, 5, 6, 18, 22

MESH_AXES = ("x", "y", "c")


def _cparams(*sem):
    return pltpu.CompilerParams(dimension_semantics=sem, vmem_limit_bytes=VMEM_LIMIT)


def _dot(a, b, dims, precision=None):
    return lax.dot_general(a, b, (dims, ((), ())), preferred_element_type=f32, precision=precision)


NN = ((1,), (0,))
NT = ((1,), (1,))
TN = ((0,), (0,))


def _mat_spec(arr, kind, lead, br, bc, rb, cb):
    if kind == "plain":
        return pl.BlockSpec((br, bc), lambda i, j, k: (rb(i, j, k), cb(i, j, k)))
    if kind == "lead":
        return pl.BlockSpec((None, br, bc), lambda i, j, k: (lead, rb(i, j, k), cb(i, j, k)))
    if kind == "devcol":
        assert bc == arr.shape[-1]
        return pl.BlockSpec((None, None, br, bc), lambda i, j, k: (cb(i, j, k), lead, rb(i, j, k), 0))
    assert kind == "devrow" and br == arr.shape[-2]
    return pl.BlockSpec((None, None, br, bc), lambda i, j, k: (rb(i, j, k), lead, 0, cb(i, j, k)))


def _mm(name, a, b, mode, *, b_kind="plain", b_lead=0, o_kind="plain", epilogue=None, extras=(), params=(),
        out_dtypes=(f32,), tm=1024, tn=1024, tk=None):
    if tk is None:
        tk = 512 if mode == "tn" else 1024
    if b_kind in ("plain", "lead"):
        b_rows, b_cols = b.shape[-2:]
    elif b_kind == "devcol":
        b_rows, b_cols = b.shape[-2], N_DEV * b.shape[-1]
    else:
        b_rows, b_cols = N_DEV * b.shape[-2], b.shape[-1]
    if mode == "nn":
        (M, K), (K2, N) = a.shape, (b_rows, b_cols)
    elif mode == "nt":
        (M, K), (N, K2) = a.shape, (b_rows, b_cols)
    else:
        (K, M), (K2, N) = a.shape, (b_rows, b_cols)
    assert K == K2, (name, a.shape, b.shape, mode)
    tm, tn, tk = min(tm, M), min(tn, N), min(tk, K)
    cols_are_n = mode != "nt"
    if b_kind == "devcol":
        tn, tk = (b.shape[-1], tk) if cols_are_n else (tn, b.shape[-1])
    if b_kind == "devrow":
        tn, tk = (tn, b.shape[-2]) if cols_are_n else (b.shape[-2], tk)
    shard = N // N_DEV
    if o_kind == "devcol":
        tn = max(shard, tn // shard * shard)
    assert M % tm == 0 and N % tn == 0 and K % tk == 0, (name, M, N, K, tm, tn, tk)
    nk = K // tk
    dims = {"nn": NN, "nt": NT, "tn": TN}[mode]
    n_ex, n_out = len(extras) + len(params), len(out_dtypes)

    def body(*refs):
        a_ref, b_ref = refs[:2]
        ex = refs[2:2 + n_ex]
        outs = refs[2 + n_ex:2 + n_ex + n_out]
        acc = refs[-1]
        k = pl.program_id(2)

        @pl.when(k == 0)
        def _():
            acc[...] = jnp.zeros_like(acc)

        acc[...] += _dot(a_ref[...].astype(bf16), b_ref[...].astype(bf16), dims)

        @pl.when(k == nk - 1)
        def _():
            r = acc[...]
            res = epilogue(r, *[e[...] for e in ex]) if epilogue is not None else (r,)
            for o, v in zip(outs, res):
                if o_kind == "plain":
                    o[...] = v.astype(o.dtype)
                else:
                    for q in range(tn // shard):
                        o[q] = v[:, q * shard:(q + 1) * shard].astype(o.dtype)

    if mode == "tn":
        a_spec = pl.BlockSpec((tk, tm), lambda i, j, k: (k, i))
    else:
        a_spec = pl.BlockSpec((tm, tk), lambda i, j, k: (i, k))
    jb, kb = (lambda i, j, k: j), (lambda i, j, k: k)
    if mode == "nt":
        b_spec = _mat_spec(b, b_kind, b_lead, tn, tk, jb, kb)
    else:
        b_spec = _mat_spec(b, b_kind, b_lead, tk, tn, kb, jb)
    e_spec = pl.BlockSpec((tm, tn), lambda i, j, k: (i, j))
    if o_kind == "plain":
        o_spec, o_shape = e_spec, (M, N)
    else:
        o_spec, o_shape = pl.BlockSpec((tn // shard, tm, shard), lambda i, j, k: (j, i, 0)), (N_DEV, M, shard)
    res = pl.pallas_call(
        body, name=name,
        grid=(M // tm, N // tn, nk),
        in_specs=[a_spec, b_spec] + [e_spec] * len(extras)
        + [pl.BlockSpec(p.shape, lambda i, j, k: (0, 0)) for p in params],
        out_specs=[o_spec] * n_out,
        out_shape=[jax.ShapeDtypeStruct(o_shape, dt) for dt in out_dtypes],
        scratch_shapes=[pltpu.VMEM((tm, tn), f32)],
        compiler_params=_cparams("parallel", "parallel", "arbitrary"),
    )(a, b, *extras, *params)
    return res[0] if n_out == 1 else res


def _row_spec(tm, cb, width):
    assert (cb * LANE) % width == 0
    blk = (cb * LANE) // width
    return pl.BlockSpec((tm, width), lambda i: (i, blk))


def _whole_spec(p):
    nd = p.ndim
    return pl.BlockSpec(p.shape, lambda i: (0,) * nd)


def _tl_fwd(name, fn, rows, params, out_widths, out_dtypes, tm=256):
    T = rows[0][0].shape[0]
    tm = min(tm, T)
    nr, npar = len(rows), len(params)

    def body(*refs):
        vals = [r[...] for r in refs[:nr + npar]]
        outs = fn(*vals)
        for o, v in zip(refs[nr + npar:], outs):
            o[...] = v.astype(o.dtype)

    res = pl.pallas_call(
        body, name=name, grid=(T // tm,),
        in_specs=[_row_spec(tm, cb, w) for (_, cb, w) in rows] + [_whole_spec(p) for p in params],
        out_specs=[pl.BlockSpec((tm, w), lambda i: (i, 0)) for w in out_widths],
        out_shape=[jax.ShapeDtypeStruct((T, w), dt) for w, dt in zip(out_widths, out_dtypes)],
        compiler_params=_cparams("parallel"),
    )(*[r[0] for r in rows], *params)
    return res


def _tl_bwd(name, fn, rows, params, cot_rows, cot_fn=None, skip=(), bf16_copy=False, tm=256):
    T = rows[0][0].shape[0]
    tm = min(tm, T)
    nr, npar, nc = len(rows), len(params), len(cot_rows)
    keep = [k for k in range(nr) if k not in skip]
    n_rows = len(keep) + int(bf16_copy)
    row_dtypes = [(rows[k][2], f32) for k in keep] + ([(rows[keep[0]][2], bf16)] if bf16_copy else [])

    def body(*refs):
        vals = [r[...] for r in refs[:nr + npar]]
        cots = [r[...] for r in refs[nr + npar:nr + npar + nc]]
        outs = refs[nr + npar + nc:]
        cot = tuple(cot_fn(*cots)) if cot_fn is not None else tuple(cots)
        _, vjp = jax.vjp(fn, *vals)
        grads = vjp(cot)
        for o, k in zip(outs, keep):
            o[...] = grads[k].astype(o.dtype)
        if bf16_copy:
            outs[len(keep)][...] = grads[keep[0]].astype(bf16)
        i = pl.program_id(0)
        for o, g in zip(outs[n_rows:], grads[nr:]):
            @pl.when(i == 0)
            def _(o=o):
                o[...] = jnp.zeros_like(o)
            o[...] += g

    res = pl.pallas_call(
        body, name=name, grid=(T // tm,),
        in_specs=[_row_spec(tm, cb, w) for (_, cb, w) in rows] + [_whole_spec(p) for p in params]
        + [_row_spec(tm, cb, w) for (_, cb, w) in cot_rows],
        out_specs=[pl.BlockSpec((tm, w), lambda i: (i, 0)) for w, _ in row_dtypes] + [_whole_spec(p) for p in params],
        out_shape=[jax.ShapeDtypeStruct((T, w), dt) for w, dt in row_dtypes]
        + [jax.ShapeDtypeStruct(p.shape, f32) for p in params],
        compiler_params=_cparams("arbitrary"),
    )(*[r[0] for r in rows], *params, *[r[0] for r in cot_rows])
    return res[:n_rows], res[n_rows:]


def _ln_res_fn(x, mix, g, b):
    pre = DN_ALPHA * x + mix
    mu = jnp.mean(pre, axis=-1, keepdims=True)
    var = jnp.mean(jnp.square(pre - mu), axis=-1, keepdims=True)
    return ((pre - mu) * lax.rsqrt(var + LN_EPS) * g + b,)


@jax.custom_jvp
def _expm1(x):
    small = jnp.abs(x) < 0.3
    xs = jnp.where(small, x, 0.0)
    poly = xs * (1.0 + xs * (1 / 2 + xs * (1 / 6 + xs * (1 / 24 + xs * (1 / 120 + xs * (
        1 / 720 + xs * (1 / 5040 + xs * (1 / 40320 + xs * (1 / 362880)))))))))
    return jnp.where(small, poly, jnp.exp(x) - 1.0)


@_expm1.defjvp
def _expm1_jvp(primals, tangents):
    (x,), (t,) = primals, tangents
    return _expm1(x), t * jnp.exp(x)


def _rglru_pre_fn(pre_r, pre_i, xc, b_a, b_x, lam):
    r = jax.nn.sigmoid(pre_r + b_a)
    i = jax.nn.sigmoid(pre_i + b_x)
    log_a = -LRU_C * r * jax.nn.softplus(-lam)
    a = jnp.exp(log_a)
    b = jnp.sqrt(-_expm1(2.0 * log_a)) * (i * xc)
    return a, b


def _rec_gate_fn(h, gate):
    return (h * jax.nn.gelu(gate),)


def _loss_head(y, t, tm=256):
    T, Dm = y.shape

    def body(y_ref, t_ref, dy_ref, loss_ref):
        e = y_ref[...] - t_ref[...]
        dy_ref[...] = e * (1.0 / Dm)

        @pl.when(pl.program_id(0) == 0)
        def _():
            loss_ref[...] = jnp.zeros_like(loss_ref)

        loss_ref[...] += 0.5 * jnp.sum(jnp.mean(e * e, axis=-1, keepdims=True), axis=0, keepdims=True)

    dy, loss = pl.pallas_call(
        body, name="loss_head", grid=(T // tm,),
        in_specs=[pl.BlockSpec((tm, Dm), lambda i: (i, 0))] * 2,
        out_specs=[pl.BlockSpec((tm, Dm), lambda i: (i, 0)), pl.BlockSpec((SUBLANE, LANE), lambda i: (0, 0))],
        out_shape=[jax.ShapeDtypeStruct((T, Dm), f32), jax.ShapeDtypeStruct((SUBLANE, LANE), f32)],
        compiler_params=_cparams("arbitrary"),
    )(y, t)
    return loss[0, 0], dy


def _conv_fwd(name, x, cb0, nblk, w, bias, tm=2048):
    T = x.shape[0]
    tm = min(tm, T)
    hb = tm // SUBLANE
    has_b = bias is not None

    def body(*refs):
        cur, prev, w_ref = refs[:3]
        b_ref = refs[3] if has_b else None
        o = refs[-1]
        i = pl.program_id(1)
        p = jnp.where(i > 0, prev[...], 0.0)
        xcat = jnp.concatenate([p, cur[...]], axis=0)
        acc = cur[...] * w_ref[3:4, :]
        for j in range(3):
            acc = acc + pltpu.roll(xcat, 3 - j, axis=0)[SUBLANE:] * w_ref[j:j + 1, :]
        if has_b:
            acc = acc + b_ref[...]
        o[...] = acc

    in_specs = [
        pl.BlockSpec((tm, LANE), lambda c, i: (i, cb0 + c)),
        pl.BlockSpec((SUBLANE, LANE), lambda c, i: (jnp.maximum(i * hb - 1, 0), cb0 + c)),
        pl.BlockSpec((4, LANE), lambda c, i: (0, c)),
    ]
    args = [x, x, w]
    if has_b:
        in_specs.append(pl.BlockSpec((1, LANE), lambda c, i: (0, c)))
        args.append(bias)
    return pl.pallas_call(
        body, name=name, grid=(nblk, T // tm),
        in_specs=in_specs,
        out_specs=pl.BlockSpec((tm, LANE), lambda c, i: (i, c)),
        out_shape=jax.ShapeDtypeStruct((T, nblk * LANE), f32),
        compiler_params=_cparams("parallel", "parallel"),
    )(*args)


def _conv_bwd(name, dy, x, cb0, nblk, w, into, into_cb, tm=2048):
    T = x.shape[0]
    tm = min(tm, T)
    hb = tm // SUBLANE
    nt = T // tm

    def body(dcur, dnext, xcur, xprev, w_ref, _, dx_ref, dw_ref, db_ref):
        i = pl.program_id(1)
        d = dcur[...]
        dn = jnp.where(i < nt - 1, dnext[...], 0.0)
        dcat = jnp.concatenate([d, dn], axis=0)
        acc = d * w_ref[3:4, :]
        for j in range(3):
            s = 3 - j
            acc = acc + pltpu.roll(dcat, tm + SUBLANE - s, axis=0)[:tm] * w_ref[j:j + 1, :]
        dx_ref[...] = acc

        p = jnp.where(i > 0, xprev[...], 0.0)
        xcat = jnp.concatenate([p, xcur[...]], axis=0)
        rows = [jnp.sum(d * pltpu.roll(xcat, 3 - j, axis=0)[SUBLANE:], axis=0, keepdims=True) for j in range(3)]
        rows.append(jnp.sum(d * xcur[...], axis=0, keepdims=True))
        rows.append(jnp.zeros((SUBLANE - 4, LANE), f32))

        @pl.when(i == 0)
        def _():
            dw_ref[...] = jnp.zeros_like(dw_ref)
            db_ref[...] = jnp.zeros_like(db_ref)

        dw_ref[...] += jnp.concatenate(rows, axis=0)
        db_ref[...] += jnp.broadcast_to(jnp.sum(d, axis=0, keepdims=True), (SUBLANE, LANE))

    nh = T // SUBLANE
    dx, dw, db = pl.pallas_call(
        body, name=name, grid=(nblk, nt),
        in_specs=[
            pl.BlockSpec((tm, LANE), lambda c, i: (i, c)),
            pl.BlockSpec((SUBLANE, LANE), lambda c, i: (jnp.minimum((i + 1) * hb, nh - 1), c)),
            pl.BlockSpec((tm, LANE), lambda c, i: (i, cb0 + c)),
            pl.BlockSpec((SUBLANE, LANE), lambda c, i: (jnp.maximum(i * hb - 1, 0), cb0 + c)),
            pl.BlockSpec((4, LANE), lambda c, i: (0, c)),
            pl.BlockSpec(memory_space=pl.ANY),
        ],
        out_specs=[
            pl.BlockSpec((tm, LANE), lambda c, i: (i, into_cb + c)),
            pl.BlockSpec((SUBLANE, LANE), lambda c, i: (0, c)),
            pl.BlockSpec((SUBLANE, LANE), lambda c, i: (0, c)),
        ],
        out_shape=[jax.ShapeDtypeStruct(into.shape, f32),
                   jax.ShapeDtypeStruct((SUBLANE, nblk * LANE), f32),
                   jax.ShapeDtypeStruct((SUBLANE, nblk * LANE), f32)],
        input_output_aliases={5: 0},
        compiler_params=_cparams("parallel", "arbitrary"),
    )(dy, dy, x, x, w, into)
    return dx, dw[:4], db[0]


@functools.partial(jax.custom_vjp, nondiff_argnums=(1,))
def _lroll(x, s):
    return pltpu.roll(x, s, axis=1)


def _lroll_fwd(x, s):
    return _lroll(x, s), None


def _lroll_bwd(s, _, g):
    return (_lroll(g, (LANE - s) % LANE),)


_lroll.defvjp(_lroll_fwd, _lroll_bwd)


def _rope_tables(T):
    half = A_HEAD_DIM // 2
    inv_freq = ROPE_THETA ** (-jnp.arange(half, dtype=f32) / half)
    ang = jnp.arange(T, dtype=f32)[:, None] * inv_freq[None, :]
    cos, sin = jnp.cos(ang), jnp.sin(ang)
    return jnp.tile(jnp.concatenate([cos, cos], axis=1), (1, 2)), jnp.tile(jnp.concatenate([-sin, sin], axis=1), (1, 2))


def _attn_block_fn(n, q, kp, kc, vp, vc, cq, sq, cp, sp, sinks):
    W = WINDOW
    lane = lax.broadcasted_iota(jnp.int32, (W, LANE), 1)
    lo_half = (lane % A_HEAD_DIM) < (A_HEAD_DIM // 2)
    lane8 = lax.broadcasted_iota(jnp.int32, sinks.shape, 1)

    def rope(x, c, s):
        return x * c + jnp.where(lo_half, _lroll(x, LANE - A_HEAD_DIM // 2), _lroll(x, A_HEAD_DIM // 2)) * s

    k2 = jnp.concatenate([rope(kp, cp, sp), rope(kc, cq, sq)], axis=0).astype(bf16)
    v2 = jnp.concatenate([vp, vc], axis=0).astype(bf16)
    qs = []
    for t in range(4):
        qt = rope(q[:, LANE * t:LANE * (t + 1)], cq, sq)
        g = t // 2
        for hh in range(2):
            qa = jnp.where((lane // A_HEAD_DIM) == hh, qt, 0.0)
            qs.append(_lroll(qa, A_HEAD_DIM) if hh != g else qa)
    s_all = _dot(jnp.concatenate(qs, axis=0).astype(bf16), k2, NT) * (A_HEAD_DIM ** -0.5)
    row = lax.broadcasted_iota(jnp.int32, (W, 2 * W), 0)
    col = lax.broadcasted_iota(jnp.int32, (W, 2 * W), 1)
    dist = row + W - col
    mask = (dist >= 0) & (dist < W) & ((col >= W) | (n > 0))
    ps = []
    for j in range(A_Q_HEADS):
        s = jnp.where(mask, s_all[W * j:W * (j + 1)], -jnp.inf)
        sink = jnp.sum(jnp.where(lane8 == j, sinks, 0.0), axis=1, keepdims=True)
        m = jnp.maximum(jnp.max(s, axis=-1, keepdims=True), sink)
        e = jnp.exp(s - m)
        ps.append((e / (jnp.sum(e, axis=-1, keepdims=True) + jnp.exp(sink - m))).astype(bf16))
    o = _dot(jnp.concatenate(ps, axis=0), v2, NN)
    outs = []
    for t in range(4):
        g = t // 2
        ot = jnp.zeros((W, LANE), f32)
        for hh in range(2):
            j = 2 * t + hh
            oj = jnp.where((lane // A_HEAD_DIM) == g, o[W * j:W * (j + 1)], 0.0)
            ot = ot + (_lroll(oj, A_HEAD_DIM) if hh != g else oj)
        outs.append(ot)
    return jnp.concatenate(outs, axis=1)


def _attn_specs():
    W = WINDOW
    prev = lambda n: jnp.maximum(n - 1, 0)
    return [
        pl.BlockSpec((W, 4 * LANE), lambda n: (n, CB_QA // 4)),
        pl.BlockSpec((W, LANE), lambda n: (prev(n), CB_KA)),
        pl.BlockSpec((W, LANE), lambda n: (n, CB_KA)),
        pl.BlockSpec((W, LANE), lambda n: (prev(n), CB_VA)),
        pl.BlockSpec((W, LANE), lambda n: (n, CB_VA)),
        pl.BlockSpec((W, LANE), lambda n: (n, 0)),
        pl.BlockSpec((W, LANE), lambda n: (n, 0)),
        pl.BlockSpec((W, LANE), lambda n: (prev(n), 0)),
        pl.BlockSpec((W, LANE), lambda n: (prev(n), 0)),
        pl.BlockSpec((1, A_Q_HEADS), lambda n: (0, 0)),
    ]


def _attn_fwd(name, proj, cos, sin, sinks):
    T = proj.shape[0]
    W = WINDOW

    def body(*refs):
        o = refs[-1]
        o[...] = _attn_block_fn(pl.program_id(0), *[r[...] for r in refs[:-1]])

    return pl.pallas_call(
        body, name=name, grid=(T // W,),
        in_specs=_attn_specs(),
        out_specs=pl.BlockSpec((W, 4 * LANE), lambda n: (n, 0)),
        out_shape=jax.ShapeDtypeStruct((T, 2 * 4 * LANE), f32),
        compiler_params=_cparams("parallel"),
    )(proj, proj, proj, proj, proj, cos, sin, cos, sin, sinks)


def _attn_bwd(name, proj, cos, sin, sinks, d_oab):
    T = proj.shape[0]
    W = WINDOW
    Q = 4 * LANE

    def body(*refs):
        ins = [r[...] for r in refs[:10]]
        do = refs[10][...]
        d_ref, ds_ref = refs[11:]
        n = pl.program_id(0)
        _, vjp = jax.vjp(functools.partial(_attn_block_fn, n), *ins)
        dq, dkp, dkc, dvp, dvc, _, _, _, _, dsk = vjp(do)

        @pl.when(n == 0)
        def _():
            d_ref[:, Q:] = jnp.zeros((T, 2 * LANE), f32)
            ds_ref[...] = jnp.zeros_like(ds_ref)

        cur = pl.ds(pl.multiple_of(n * W, W), W)
        d_ref[cur, :Q] = dq
        d_ref[cur, Q:Q + LANE] += dkc
        d_ref[cur, Q + LANE:] += dvc
        ds_ref[...] += dsk

        @pl.when(n > 0)
        def _():
            prv = pl.ds(pl.multiple_of((n - 1) * W, W), W)
            d_ref[prv, Q:Q + LANE] += dkp
            d_ref[prv, Q + LANE:] += dvp

    return pl.pallas_call(
        body, name=name, grid=(T // W,),
        in_specs=_attn_specs() + [pl.BlockSpec((W, Q), lambda n: (n, 0))],
        out_specs=[pl.BlockSpec((T, Q + 2 * LANE), lambda n: (0, 0)),
                   pl.BlockSpec((1, A_Q_HEADS), lambda n: (0, 0))],
        out_shape=[jax.ShapeDtypeStruct((T, HYB_PROJ_PAD), f32), jax.ShapeDtypeStruct((1, A_Q_HEADS), f32)],
        compiler_params=_cparams("arbitrary"),
    )(proj, proj, proj, proj, proj, cos, sin, cos, sin, sinks, d_oab)


def _bdot(spec, a, b, precision=None):
    return jnp.einsum(spec, a, b, preferred_element_type=f32, precision=precision)


@jax.custom_vjp
def _tri_inv(a):
    H, C, _ = a.shape
    B = 2 * SUBLANE
    nb = C // B
    r = lax.broadcasted_iota(jnp.int32, (C, C), 0)
    c = lax.broadcasted_iota(jnp.int32, (C, C), 1)
    a4 = jnp.where((r // B) == (c // B), a, 0.0).reshape(H, nb, B, C)
    t4 = jnp.broadcast_to(jnp.where(r == c, 1.0, 0.0).astype(f32), a.shape).reshape(H, nb, B, C)
    for j in range(B - 1):
        col = jnp.concatenate([a4[:, b:b + 1, :, B * b + j:B * b + j + 1] for b in range(nb)], axis=1)
        t4 = t4 - col * t4[:, :, j:j + 1, :]
    x = t4.reshape(H, C, C)
    hi = lax.Precision.HIGH
    while B < C:
        m = jnp.where(((r // (2 * B)) == (c // (2 * B))) & ((r // B) > (c // B)), a, 0.0)
        x = x - _bdot("hij,hjk->hik", x, _bdot("hij,hjk->hik", m, x, precision=hi), precision=hi)
        B *= 2
    return x


def _tri_inv_fwd(a):
    t = _tri_inv(a)
    return t, t


def _tri_inv_bwd(t, g):
    C = t.shape[-1]
    r = lax.broadcasted_iota(jnp.int32, (C, C), 0)
    c = lax.broadcasted_iota(jnp.int32, (C, C), 1)
    x = _bdot("hki,hkj->hij", t, g, precision=lax.Precision.HIGHEST)
    y = _bdot("hik,hjk->hij", x, t, precision=lax.Precision.HIGHEST)
    return (jnp.where(r > c, -y, 0.0),)


_tri_inv.defvjp(_tri_inv_fwd, _tri_inv_bwd)


@jax.custom_vjp
def _tri_inv_saved(a, t):
    return t


_tri_inv_saved.defvjp(lambda a, t: (t, t), lambda t, g: (_tri_inv_bwd(t, g)[0], jnp.zeros_like(t)))


def _silu(x):
    return x * jax.nn.sigmoid(x)


def _l2n(x):
    return x * lax.rsqrt(jnp.sum(x * x, axis=-1, keepdims=True) + NORM_EPS)


def _delta_chunk_fn(cq, ck, cv, z, lg, a_log, dt_bias, norm_w, S, t_saved=None, want_t=False):
    C = B_CHUNK
    lane = lax.broadcasted_iota(jnp.int32, (C, LANE), 1)
    pick = lambda l0: jnp.concatenate(
        [jnp.sum(jnp.where(lane == l0 + h, lg, 0.0), axis=1, keepdims=True)[None] for h in range(B_HEADS)], axis=0)
    bl, al = pick(0), pick(B_HEADS)
    q = _l2n(_silu(cq)) * (B_HEAD_DIM ** -0.5)
    k = _l2n(_silu(ck))
    v = _silu(cv)
    beta = jax.nn.sigmoid(bl)
    g = -jnp.exp(a_log) * jax.nn.softplus(al + dt_bias)
    r = lax.broadcasted_iota(jnp.int32, (C, C), 0)
    c = lax.broadcasted_iota(jnp.int32, (C, C), 1)
    eye = r == c
    g_row = jnp.sum(jnp.where(eye, g, 0.0), axis=1, keepdims=True)
    gc = jnp.sum(jnp.where(c <= r, g_row, 0.0), axis=2, keepdims=True)
    gc_row = jnp.sum(jnp.where(eye, gc, 0.0), axis=1, keepdims=True)
    decay_incl = jnp.exp(jnp.where(r >= c, gc - gc_row, -jnp.inf))
    decay_strict = jnp.where(r > c, decay_incl, 0.0)
    kb = k * beta
    vb = v * beta
    kbf = k.astype(bf16)
    a_mat = _bdot("hik,hjk->hij", kb.astype(bf16), kbf) * decay_strict
    t_f32 = _tri_inv(a_mat) if t_saved is None else _tri_inv_saved(a_mat, t_saved)
    t_mat = t_f32.astype(bf16)
    eg = jnp.exp(gc)
    u = _bdot("hij,hjv->hiv", t_mat, vb.astype(bf16))
    w = _bdot("hij,hjk->hik", t_mat, (kb * eg).astype(bf16))
    qk = _bdot("hik,hjk->hij", q.astype(bf16), kbf) * decay_incl
    g_last = jnp.sum(g, axis=1, keepdims=True)
    k_tail = k * jnp.exp(g_last - gc)
    Sb = S.astype(bf16)
    v_new = u - _bdot("hck,hkv->hcv", w.astype(bf16), Sb)
    o = _bdot("hck,hkv->hcv", (q * eg).astype(bf16), Sb) + _bdot("hij,hjv->hiv", qk.astype(bf16), v_new.astype(bf16))
    S_new = S * jnp.exp(g_last) + _bdot("hck,hcv->hkv", k_tail.astype(bf16), v_new.astype(bf16))
    ob = o * lax.rsqrt(jnp.mean(o * o, axis=-1, keepdims=True) + NORM_EPS) * norm_w
    return (ob * _silu(z), S_new) + ((t_f32,) if want_t else ())


DELTA_CHUNKS_PER_STEP = 4


def _delta_in_specs(rev, N):
    C = DELTA_CHUNKS_PER_STEP * B_CHUNK
    ix = (lambda n: N - 1 - n) if rev else (lambda n: n)
    specs = [pl.BlockSpec((C, 3 * B_HEADS * LANE), lambda n: (ix(n), 0))]
    specs += [pl.BlockSpec((C, LANE), lambda n, h=h: (ix(n), CB_Z + h)) for h in range(B_HEADS)]
    specs += [
        pl.BlockSpec((C, LANE), lambda n: (ix(n), CB_LG)),
        pl.BlockSpec((B_HEADS, 1, 1), lambda n: (0, 0, 0)),
        pl.BlockSpec((B_HEADS, 1, 1), lambda n: (0, 0, 0)),
        pl.BlockSpec((1, LANE), lambda n: (0, 0)),
    ]
    return specs


def _delta_inputs(u, c_ref, z_refs, lg, al, dt, nw):
    H = B_HEADS
    rows = slice(u * B_CHUNK, (u + 1) * B_CHUNK)
    part = lambda p: jnp.stack([c_ref[rows, LANE * (p * H + h):LANE * (p * H + h + 1)] for h in range(H)])
    return (part(0), part(1), part(2), jnp.stack([z[rows, :] for z in z_refs]), lg[rows, :], al[...], dt[...], nw[...])


def _delta_fwd(name, c, proj, a_log, dt_bias, norm_w, o_ab):
    T = c.shape[0]
    C = B_CHUNK
    N = T // C
    Dh = B_HEAD_DIM
    H = B_HEADS

    def body(*refs):
        c_ref, z_refs, (lg, al, dt, nw) = refs[0], refs[1:1 + H], refs[1 + H:5 + H]
        o_ref, s_ref, t_ref, S = refs[6 + H:]

        @pl.when(pl.program_id(0) == 0)
        def _():
            S[...] = jnp.zeros_like(S)

        s = S[...]
        for u in range(U):
            s_ref[:, u] = s
            ob, s, t = _delta_chunk_fn(*_delta_inputs(u, c_ref, z_refs, lg, al, dt, nw), s, want_t=True)
            for h in range(H):
                o_ref[u * C:(u + 1) * C, LANE * h:LANE * (h + 1)] = ob[h]
            t_ref[:, u] = t
        S[...] = s

    U = DELTA_CHUNKS_PER_STEP
    return pl.pallas_call(
        body, name=name, grid=(N // U,),
        in_specs=_delta_in_specs(False, N // U) + [pl.BlockSpec(memory_space=pl.ANY)],
        out_specs=[pl.BlockSpec((U * C, H * LANE), lambda n: (n, 1)),
                   pl.BlockSpec((H, U, Dh, Dh), lambda n: (0, n, 0, 0)),
                   pl.BlockSpec((H, U, C, C), lambda n: (0, n, 0, 0))],
        out_shape=[jax.ShapeDtypeStruct(o_ab.shape, f32), jax.ShapeDtypeStruct((H, N, Dh, Dh), f32),
                   jax.ShapeDtypeStruct((H, N, C, C), f32)],
        input_output_aliases={5 + H: 0},
        scratch_shapes=[pltpu.VMEM((H, Dh, Dh), f32)],
        compiler_params=_cparams("arbitrary"),
    )(c, *([proj] * H), proj, a_log, dt_bias, norm_w, o_ab)


def _delta_bwd(name, c, proj, a_log, dt_bias, norm_w, s_saved, t_saved, d_oab, dproj):
    T = c.shape[0]
    C = B_CHUNK
    N = T // C
    Dh = B_HEAD_DIM
    H = B_HEADS

    def body(*refs):
        c_ref, z_refs, (lg, al, dt, nw) = refs[0], refs[1:1 + H], refs[1 + H:5 + H]
        s_ref, t_ref, do_ref = refs[5 + H:8 + H]
        dc, dtail, dal, ddt, dnw, dS = refs[9 + H:]

        @pl.when(pl.program_id(0) == 0)
        def _():
            dS[...] = jnp.zeros_like(dS)
            dal[...] = jnp.zeros_like(dal)
            ddt[...] = jnp.zeros_like(ddt)
            dnw[...] = jnp.zeros_like(dnw)

        ds = dS[...]
        for u in reversed(range(U)):
            rows = slice(u * C, (u + 1) * C)
            _, vjp = jax.vjp(functools.partial(_delta_chunk_fn, t_saved=t_ref[:, u]),
                             *_delta_inputs(u, c_ref, z_refs, lg, al, dt, nw), s_ref[:, u])
            do = jnp.stack([do_ref[rows, LANE * h:LANE * (h + 1)] for h in range(H)])
            g = vjp((do, ds))
            for h in range(H):
                for p in range(3):
                    dc[rows, LANE * (p * H + h):LANE * (p * H + h + 1)] = g[p][h]
                dtail[rows, LANE * h:LANE * (h + 1)] = g[3][h]
            dtail[rows, LANE * H:LANE * (H + 1)] = g[4]
            dtail[rows, LANE * (H + 1):] = jnp.zeros((C, LANE), f32)
            dal[...] += g[5]
            ddt[...] += g[6]
            dnw[...] += g[7]
            ds = g[8]
        dS[...] = ds

    U = DELTA_CHUNKS_PER_STEP
    NB = N // U
    rn = lambda n: NB - 1 - n
    return pl.pallas_call(
        body, name=name, grid=(NB,),
        in_specs=_delta_in_specs(True, NB) + [
            pl.BlockSpec((H, U, Dh, Dh), lambda n: (0, rn(n), 0, 0)),
            pl.BlockSpec((H, U, C, C), lambda n: (0, rn(n), 0, 0)),
            pl.BlockSpec((U * C, H * LANE), lambda n: (rn(n), 1)),
            pl.BlockSpec(memory_space=pl.ANY),
        ],
        out_specs=[
            pl.BlockSpec((U * C, 3 * H * LANE), lambda n: (rn(n), 0)),
            pl.BlockSpec((U * C, (H + 2) * LANE), lambda n: (rn(n), CB_Z // (H + 2))),
            pl.BlockSpec((H, 1, 1), lambda n: (0, 0, 0)),
            pl.BlockSpec((H, 1, 1), lambda n: (0, 0, 0)),
            pl.BlockSpec((1, LANE), lambda n: (0, 0)),
        ],
        out_shape=[jax.ShapeDtypeStruct((T, 3 * H * Dh), f32), jax.ShapeDtypeStruct(dproj.shape, f32),
                   jax.ShapeDtypeStruct((H, 1, 1), f32), jax.ShapeDtypeStruct((H, 1, 1), f32),
                   jax.ShapeDtypeStruct((1, LANE), f32)],
        input_output_aliases={8 + H: 1},
        scratch_shapes=[pltpu.VMEM((H, Dh, Dh), f32)],
        compiler_params=_cparams("arbitrary"),
    )(c, *([proj] * H), proj, a_log, dt_bias, norm_w, s_saved, t_saved, d_oab, dproj)


def _gate_matmuls(xc, wa_ref, wx_ref):
    bw = wa_ref.shape[-1]
    xb = xc.astype(bf16)
    blocks = [xb[:, bw * h:bw * (h + 1)] for h in range(LRU_BLOCKS)]
    return (jnp.concatenate([_dot(blocks[h], wa_ref[h], NN) for h in range(LRU_BLOCKS)], axis=1),
            jnp.concatenate([_dot(blocks[h], wx_ref[h], NN) for h in range(LRU_BLOCKS)], axis=1))


def _gates_fwd(name, xc, w_a, w_x, pars, tm=256):
    T, Wd = xc.shape
    tm = min(tm, T)

    def body(x_ref, wa_ref, wx_ref, ba, bx, lam, a_ref, b_ref):
        x = x_ref[...]
        pr, pi = _gate_matmuls(x, wa_ref, wx_ref)
        a_ref[...], b_ref[...] = _rglru_pre_fn(pr, pi, x, ba[...], bx[...], lam[...])

    row = pl.BlockSpec((tm, Wd), lambda i: (i, 0))
    return pl.pallas_call(
        body, name=name, grid=(T // tm,),
        in_specs=[row, _whole_spec(w_a), _whole_spec(w_x)] + [_whole_spec(p) for p in pars],
        out_specs=[row, row], out_shape=[jax.ShapeDtypeStruct((T, Wd), f32)] * 2,
        compiler_params=_cparams("parallel"),
    )(xc, w_a, w_x, *pars)


def _gates_bwd(name, xc, w_a, w_x, pars, lam_t, h_prev, tm=256):
    T, Wd = xc.shape
    tm = min(tm, T)
    bw = Wd // LRU_BLOCKS

    def body(x_ref, wa_ref, wx_ref, ba, bx, lam, lt_ref, hp_ref, dx_ref, dr_ref, di_ref, dba, dbx, dlam):
        x = x_ref[...]
        pr, pi = _gate_matmuls(x, wa_ref, wx_ref)
        _, vjp = jax.vjp(_rglru_pre_fn, pr, pi, x, ba[...], bx[...], lam[...])
        lt = lt_ref[...]
        dpr, dpi, dxc, g_ba, g_bx, g_lam = vjp((lt * hp_ref[...], lt))
        dprb, dpib = dpr.astype(bf16), dpi.astype(bf16)
        dx_ref[...] = dxc + jnp.concatenate(
            [_dot(dprb[:, bw * h:bw * (h + 1)], wa_ref[h], NT) + _dot(dpib[:, bw * h:bw * (h + 1)], wx_ref[h], NT)
             for h in range(LRU_BLOCKS)], axis=1)
        dr_ref[...] = dprb
        di_ref[...] = dpib

        @pl.when(pl.program_id(0) == 0)
        def _():
            dba[...] = jnp.zeros_like(dba)
            dbx[...] = jnp.zeros_like(dbx)
            dlam[...] = jnp.zeros_like(dlam)

        dba[...] += g_ba
        dbx[...] += g_bx
        dlam[...] += g_lam

    row = pl.BlockSpec((tm, Wd), lambda i: (i, 0))
    vec = pl.BlockSpec((1, Wd), lambda i: (0, 0))
    return pl.pallas_call(
        body, name=name, grid=(T // tm,),
        in_specs=[row, _whole_spec(w_a), _whole_spec(w_x)] + [_whole_spec(p) for p in pars] + [row, row],
        out_specs=[row, row, row, vec, vec, vec],
        out_shape=[jax.ShapeDtypeStruct((T, Wd), f32), jax.ShapeDtypeStruct((T, Wd), bf16),
                   jax.ShapeDtypeStruct((T, Wd), bf16)] + [jax.ShapeDtypeStruct((1, Wd), f32)] * 3,
        compiler_params=_cparams("arbitrary"),
    )(xc, w_a, w_x, *pars, lam_t, h_prev)


def _blockdiag_bwd_dw(name, xc, dpr, dpi, tk=512):
    T, Wd = xc.shape
    bw = Wd // LRU_BLOCKS
    tk = min(tk, T)

    def body(x_ref, dr, di, oa, ox):
        @pl.when(pl.program_id(1) == 0)
        def _():
            oa[...] = jnp.zeros_like(oa)
            ox[...] = jnp.zeros_like(ox)

        xb = x_ref[...].astype(bf16)
        oa[...] += _dot(xb, dr[...].astype(bf16), TN)
        ox[...] += _dot(xb, di[...].astype(bf16), TN)

    xs = pl.BlockSpec((tk, bw), lambda h, k: (k, h))
    ws = pl.BlockSpec((None, bw, bw), lambda h, k: (h, 0, 0))
    return pl.pallas_call(
        body, name=name, grid=(LRU_BLOCKS, T // tk), in_specs=[xs, xs, xs], out_specs=[ws, ws],
        out_shape=[jax.ShapeDtypeStruct((LRU_BLOCKS, bw, bw), f32)] * 2,
        compiler_params=_cparams("parallel", "arbitrary"),
    )(xc, dpr, dpi)


def _scan(name, a, proj, reverse, b=None, h=None, dhg=None, tt=512, cb=512):
    T, Wd = a.shape
    tt, cb = min(tt, T), min(cb, Wd)
    nt = T // tt
    ng = tt // SUBLANE

    def body(a_ref, g_ref, *rest):
        n_in = 2 if reverse else 1
        ins, outs, (carry, carry_a) = rest[:n_in], rest[n_in:-2], rest[-2:]

        @pl.when(pl.program_id(1) == 0)
        def _():
            carry[...] = jnp.zeros_like(carry)
            carry_a[...] = jnp.zeros_like(carry_a)

        row = lax.broadcasted_iota(jnp.int32, (SUBLANE, cb), 0)

        def step(gi, c):
            hp, ap = c
            g = (ng - 1 - gi) if reverse else gi
            rows = pl.ds(pl.multiple_of(g * SUBLANE, SUBLANE), SUBLANE)
            A = a_ref[rows, :]
            gate = g_ref[rows, :]
            a_first = jnp.broadcast_to(A[0:1, :], (SUBLANE, cb))
            if reverse:
                _, vjp = jax.vjp(_rec_gate_fn, ins[0][rows, :], gate)
                B, dgate = vjp((ins[1][rows, :],))
                outs[1][rows, :] = dgate
                A = jnp.where(row == SUBLANE - 1, ap, pltpu.roll(A, SUBLANE - 1, axis=0))
            else:
                B = ins[0][rows, :]
            for s in (1, 2, 4):
                sh = (SUBLANE - s) if reverse else s
                As = pltpu.roll(A, sh, axis=0)
                Bs = pltpu.roll(B, sh, axis=0)
                valid = (row < SUBLANE - s) if reverse else (row >= s)
                B = jnp.where(valid, A * Bs + B, B)
                A = jnp.where(valid, A * As, A)
            hcur = A * hp + B
            outs[0][rows, :] = hcur
            if not reverse:
                outs[1][rows, :] = jnp.where(row == 0, hp, pltpu.roll(hcur, 1, axis=0))
                outs[2][rows, :] = _rec_gate_fn(hcur, gate)[0]
            edge = hcur[0:1, :] if reverse else hcur[SUBLANE - 1:SUBLANE, :]
            return jnp.broadcast_to(edge, (SUBLANE, cb)), a_first

        carry[...], carry_a[...] = lax.fori_loop(0, ng, step, (carry[...], carry_a[...]))

    nc = Wd // cb
    tok = (lambda i: nt - 1 - i) if reverse else (lambda i: i)
    spec = pl.BlockSpec((tt, cb), lambda c, i: (tok(i), c))
    gate_half = pl.BlockSpec((tt, cb), lambda c, i: (tok(i), nc + c))
    if reverse:
        args, out_specs = (a, proj, h, dhg), [spec, gate_half]
        out_shape = [jax.ShapeDtypeStruct((T, Wd), f32), jax.ShapeDtypeStruct((T, 2 * Wd), f32)]
    else:
        args, out_specs = (a, proj, b), [spec] * 3
        out_shape = [jax.ShapeDtypeStruct((T, Wd), f32)] * 3
    return pl.pallas_call(
        body, name=name, grid=(nc, nt), in_specs=[spec, gate_half] + [spec] * (len(args) - 2), out_specs=out_specs,
        out_shape=out_shape,
        scratch_shapes=[pltpu.VMEM((SUBLANE, cb), f32), pltpu.VMEM((SUBLANE, cb), f32)],
        compiler_params=_cparams("parallel", "arbitrary"),
    )(*args)


def _relu2_epilogue(r):
    h = jnp.maximum(r, 0.0)
    return r, h * h


def _drelu2_epilogue(r, a):
    return (r * (2.0 * jnp.maximum(a.astype(f32), 0.0)),)


def _residual_cot(through, upper):
    return (through + DN_ALPHA * upper,)


def _merge_cols(name, g, tm=256):
    _, L, R, s = g.shape

    def body(g_ref, o_ref):
        for d in range(N_DEV):
            o_ref[:, s * d:s * (d + 1)] = g_ref[d].astype(bf16)
        o_ref[:, N_DEV * s:] = jnp.zeros((tm, HYB_PROJ_PAD - N_DEV * s), bf16)

    return pl.pallas_call(
        body, name=name, grid=(L, R // tm),
        in_specs=[pl.BlockSpec((N_DEV, None, tm, s), lambda l, i: (0, l, i, 0))],
        out_specs=pl.BlockSpec((None, tm, HYB_PROJ_PAD), lambda l, i: (l, i, 0)),
        out_shape=jax.ShapeDtypeStruct((L, R, HYB_PROJ_PAD), bf16),
        compiler_params=_cparams("parallel", "parallel"),
    )(g)


def _split_cols(name, dw, tm=256):
    R = dw.shape[0]
    s = HYB_PROJ // N_DEV

    def body(g_ref, o_ref):
        for d in range(N_DEV):
            o_ref[d] = g_ref[:, s * d:s * (d + 1)].astype(bf16)

    return pl.pallas_call(
        body, name=name, grid=(R // tm,),
        in_specs=[pl.BlockSpec((tm, HYB_PROJ_PAD), lambda i: (i, 0))],
        out_specs=pl.BlockSpec((N_DEV, tm, s), lambda i: (0, i, 0)),
        out_shape=jax.ShapeDtypeStruct((N_DEV, R, s), bf16),
        compiler_params=_cparams("parallel"),
    )(dw)


def _rows_to_dev(dw):
    nb, r, c = dw.shape
    t = dw.reshape(nb, N_DEV, r // N_DEV, c)
    return jnp.moveaxis(t, 1, 0).reshape(N_DEV, nb * (r // N_DEV), c).astype(bf16)


def _ln_epilogue(r, x, g, b):
    y = _ln_res_fn(x, r, g, b)[0]
    return r, y, y


def _hybrid_fwd(tag, x, xb, W, j, cos, sin, ln, before_out):
    proj = _mm(f"{tag}_proj", xb, W["hyb_w_in"][j], "nn", b_kind="lead", b_lead=0)
    o_a = _attn_fwd(f"{tag}_attn", proj, cos, sin, W["hyb_sinks"][j][None, :])
    c = _conv_fwd(f"{tag}_conv", proj, CB_CONV, 12, W["hyb_conv_w"][j], None)
    o_ab, s_saved, t_saved = _delta_fwd(f"{tag}_delta", c, proj, W["hyb_a_log"][j].reshape(B_HEADS, 1, 1),
                                        W["hyb_dt_bias"][j].reshape(B_HEADS, 1, 1), W["hyb_norm_w"][j][None, :], o_a)
    before_out(o_ab)
    mix, x1, x1b = _mm(f"{tag}_out", o_ab, W["hyb_w_out"][j], "nn", b_kind="lead", b_lead=0, epilogue=_ln_epilogue,
                       extras=(x,), params=ln, out_dtypes=(f32, f32, bf16), tm=512)
    return mix, x1, x1b, (proj, c, s_saved, t_saved, o_ab)


def _hybrid_bwd(tag, x, dmix, addend, W, j, cos, sin, saved, G, send_early):
    proj, c, s_saved, t_saved, o_ab = saved
    T = x.shape[0]
    d_oab = _mm(f"{tag}_dout", dmix, W["hyb_w_out"][j], "nt", b_kind="lead", b_lead=0)
    G["hyb_w_out"][j] = _mm(f"{tag}_dwout", o_ab, dmix, "tn", out_dtypes=(bf16,)).reshape(N_DEV, -1, D_MODEL)
    sinks = W["hyb_sinks"][j][None, :] + send_early({("hyb_w_out", j): G["hyb_w_out"][j]})
    dproj, dsinks = _attn_bwd(f"{tag}_dattn", proj, cos, sin, sinks, d_oab)
    a_log = W["hyb_a_log"][j].reshape(B_HEADS, 1, 1)
    dt_bias = W["hyb_dt_bias"][j].reshape(B_HEADS, 1, 1)
    dc, dproj, dal, ddt, dnw = _delta_bwd(f"{tag}_ddelta", c, proj, a_log, dt_bias, W["hyb_norm_w"][j][None, :],
                                          s_saved, t_saved, d_oab, dproj)
    dproj, dconv_w, _ = _conv_bwd(f"{tag}_dconv", dc, proj, CB_CONV, 12, W["hyb_conv_w"][j], dproj, CB_CONV)
    dx = _mm(f"{tag}_dx", dproj, W["hyb_w_in"][j], "nt", b_kind="lead", b_lead=0,
             **({} if addend is None else dict(epilogue=_residual_cot, extras=(addend,))))
    G["hyb_w_in"][j] = _split_cols(f"{tag}_dwin_split", _mm(f"{tag}_dwin", x, dproj, "tn", tn=1536))
    G["hyb_sinks"][j] = dsinks[0]
    G["hyb_conv_w"][j] = dconv_w
    G["hyb_a_log"][j] = dal.reshape(B_HEADS)
    G["hyb_dt_bias"][j] = ddt.reshape(B_HEADS)
    G["hyb_norm_w"][j] = dnw[0]
    return dx


def _rec_fwd(tag, x, xb, W, j, ln, before_out):
    Wd = D_MODEL
    proj = _mm(f"{tag}_proj", xb, W["rec_w_in"][j], "nn", b_kind="devcol", b_lead=0)
    xc = _conv_fwd(f"{tag}_conv", proj, 0, Wd // LANE, W["rec_conv_w"][j], W["rec_conv_b"][j][None, :])
    pars = [W["rec_b_a"][j][None, :], W["rec_b_x"][j][None, :], W["rec_lambda"][j][None, :]]
    a, b = _gates_fwd(f"{tag}_gates", xc, W["rec_w_a"][j][0], W["rec_w_x"][j][0], pars)
    h, h_prev, hg = _scan(f"{tag}_scan", a, proj, False, b=b)
    before_out(hg)
    mix, x1, x1b = _mm(f"{tag}_out", hg, W["rec_w_out"][j], "nn", b_kind="lead", b_lead=0, epilogue=_ln_epilogue,
                       extras=(x,), params=ln, out_dtypes=(f32, f32, bf16), tm=512)
    return mix, x1, x1b, (proj, xc, a, h, h_prev, hg)


def _rec_bwd(tag, x, dmix, addend, W, j, saved, G, send_early):
    proj, xc, a, h, h_prev, hg = saved
    Wd = D_MODEL
    dhg = _mm(f"{tag}_dout", dmix, W["rec_w_out"][j], "nt", b_kind="lead", b_lead=0)
    G["rec_w_out"][j] = _mm(f"{tag}_dwout", hg, dmix, "tn", out_dtypes=(bf16,)).reshape(N_DEV, -1, D_MODEL)
    sent = send_early({("rec_w_out", j): G["rec_w_out"][j]})
    lam_t, dproj = _scan(f"{tag}_dscan", a, proj, True, h=h, dhg=dhg)
    pars = [W["rec_b_a"][j][None, :] + sent, W["rec_b_x"][j][None, :], W["rec_lambda"][j][None, :]]
    dxc, dpr, dpi, db_a, db_x, dlam = _gates_bwd(f"{tag}_dgates", xc, W["rec_w_a"][j][0], W["rec_w_x"][j][0], pars,
                                                 lam_t, h_prev)
    dwa, dwx = _blockdiag_bwd_dw(f"{tag}_dgates_dw", xc, dpr, dpi)
    G["rec_w_a"][j], G["rec_w_x"][j] = _rows_to_dev(dwa), _rows_to_dev(dwx)
    dproj, dconv_w, dconv_b = _conv_bwd(f"{tag}_dconv", dxc, proj, 0, Wd // LANE, W["rec_conv_w"][j], dproj, 0)
    dx = _mm(f"{tag}_dx", dproj, W["rec_w_in"][j], "nt", b_kind="devcol", b_lead=0,
             **({} if addend is None else dict(epilogue=_residual_cot, extras=(addend,))))
    G["rec_w_in"][j] = _mm(f"{tag}_dwin", x, dproj, "tn", o_kind="devcol", out_dtypes=(bf16,), tn=2048)
    G["rec_conv_w"][j] = dconv_w
    G["rec_conv_b"][j] = dconv_b
    G["rec_b_a"][j] = db_a[0]
    G["rec_b_x"][j] = db_x[0]
    G["rec_lambda"][j] = dlam[0]
    return dx


def _local_step(x, target, W, load_layer, grads_ready):
    T = x.shape[0]
    cos, sin = _rope_tables(T)
    saved = []
    xb = x
    for layer in range(DEPTH):
        j = layer // 2
        tag = f"L{layer}"
        load_layer(layer, 0, x)
        ln1 = (W["ln1_g"][layer][None, :], W["ln1_b"][layer][None, :])
        before_out = functools.partial(load_layer, layer, 1)
        if layer % 2 == 0:
            mix, x1, x1b, sv = _hybrid_fwd(tag, x, xb, W, j, cos, sin, ln1, before_out)
        else:
            mix, x1, x1b, sv = _rec_fwd(tag, x, xb, W, j, ln1, before_out)
        load_layer(layer, 2, x1)
        a, h2 = _mm(f"{tag}_mlp1", x1b, W["mlp_w1"][layer], "nn", b_kind="devcol", b_lead=0, epilogue=_relu2_epilogue,
                    out_dtypes=(bf16, bf16), tm=2048)
        ln2 = (W["ln2_g"][layer][None, :], W["ln2_b"][layer][None, :])
        y, x2, x2b = _mm(f"{tag}_mlp2", h2, W["mlp_w2"][layer], "nn", b_kind="lead", b_lead=0, epilogue=_ln_epilogue,
                         extras=(x1,), params=ln2, out_dtypes=(f32, f32, bf16))
        saved.append((x, xb, sv, mix, x1, x1b, a, h2, y))
        x, xb = x2, x2b
    loss, dx = _loss_head(x, target)

    G = {k: [None] * (DEPTH if k.startswith(("ln", "mlp")) else DEPTH // 2) for k in (
        "hyb_w_in", "hyb_sinks", "hyb_conv_w", "hyb_a_log", "hyb_dt_bias", "hyb_norm_w", "hyb_w_out",
        "rec_w_in", "rec_conv_w", "rec_conv_b", "rec_w_a", "rec_b_a", "rec_w_x", "rec_b_x", "rec_lambda", "rec_w_out",
        "ln1_g", "ln1_b", "mlp_w1", "mlp_w2", "ln2_g", "ln2_b")}
    order = jnp.zeros((1, 1), f32)
    cot_rows, cot_fn = [(dx, 0, D_MODEL)], None
    for layer in reversed(range(DEPTH)):
        j = layer // 2
        tag = f"L{layer}"
        x0, x0b, sv, mix, x1, x1b, a, h2, y = saved[layer]
        ln2 = [W["ln2_g"][layer][None, :] + order, W["ln2_b"][layer][None, :]]
        (dy, dyb), (dg2, db2) = _tl_bwd(f"{tag}_dln2", _ln_res_fn, [(x1, 0, D_MODEL), (y, 0, D_MODEL)], ln2,
                                        cot_rows, cot_fn=cot_fn, skip=(0,), bf16_copy=True)
        G["ln2_g"][layer], G["ln2_b"][layer] = dg2[0], db2[0]
        da = _mm(f"{tag}_dmlp2", dyb, W["mlp_w2"][layer], "nt", b_kind="lead", b_lead=0, epilogue=_drelu2_epilogue,
                 extras=(a,), out_dtypes=(bf16,), tm=2048, tn=512)
        G["mlp_w2"][layer] = _mm(f"{tag}_dw2", h2, dyb, "tn", out_dtypes=(bf16,), tm=2048).reshape(N_DEV, -1, D_MODEL)
        dx1 = _mm(f"{tag}_dmlp1", da, W["mlp_w1"][layer], "nt", b_kind="devcol", b_lead=0, tm=2048)
        G["mlp_w1"][layer] = _mm(f"{tag}_dw1", x1b, da, "tn", o_kind="devcol", out_dtypes=(bf16,), tn=2048)
        ln1 = [W["ln1_g"][layer][None, :], W["ln1_b"][layer][None, :]]
        (dmix, dmixb), (dg1, db1) = _tl_bwd(f"{tag}_dln1", _ln_res_fn, [(x0, 0, D_MODEL), (mix, 0, D_MODEL)], ln1,
                                            [(dx1, 0, D_MODEL), (dy, 0, D_MODEL)], cot_fn=_residual_cot, skip=(0,),
                                            bf16_copy=True)
        G["ln1_g"][layer], G["ln1_b"][layer] = dg1[0], db1[0]
        dx0_a = dmix if layer == 0 else None
        early = functools.partial(grads_ready, f"l{layer}_early",
                                  {(k, layer): G[k][layer] for k in ("mlp_w1", "mlp_w2")})
        if layer % 2 == 0:
            dx = _hybrid_bwd(tag, x0b, dmixb, dx0_a, W, j, cos, sin, sv, G, early)
        else:
            dx = _rec_bwd(tag, x0b, dmixb, dx0_a, W, j, sv, G, early)
        order = grads_ready(f"l{layer}_late", {}, {(k, i): G[k][i] for k, i in _layer_weights(layer)[:-2]
                                                  if not k.endswith("w_out")})
        cot_rows, cot_fn = [(dx, 0, D_MODEL), (dmix, 0, D_MODEL)], _residual_cot
    big = {k for k, _ in BIG}
    return loss, dx, {k: jnp.stack(v) for k, v in G.items() if k not in big}


def _layer_weights(layer):
    j = layer // 2
    mixer = ["hyb_w_in", "hyb_w_out"] if layer % 2 == 0 else ["rec_w_in", "rec_w_out", "rec_w_a", "rec_w_x"]
    return [(k, j) for k in mixer] + [("mlp_w1", layer), ("mlp_w2", layer)]


def _my_coords():
    return lax.axis_index("x"), lax.axis_index("y"), lax.axis_index("c")


def _all_gather(name, arrays):
    na = len(arrays)

    def body(*refs):
        x_refs, out_refs = refs[:na], refs[na:2 * na]
        send_sems, recv_sems, local_sems = refs[2 * na:]
        x, y, c = _my_coords()
        me, sibling = (x, y, c), (x, y, 1 - c)
        chips = [(1 - x, y), (x, 1 - y), (1 - x, 1 - y)]

        def blk(a, px, py, pc):
            return out_refs[a].at[4 * px + 2 * py + pc]

        def copy(a, k, block, to, src=None):
            return pltpu.make_async_remote_copy(
                src_ref=blk(a, *block) if src is None else src, dst_ref=blk(a, *block),
                send_sem=send_sems.at[a, k], recv_sem=recv_sems.at[a, k],
                device_id=to, device_id_type=pl.DeviceIdType.MESH)

        mine = [pltpu.make_async_copy(x_refs[a], blk(a, *me), local_sems.at[a]) for a in range(na)]
        for cp in mine:
            cp.start()
        first = []
        for a in range(na):
            first.append(copy(a, 0, me, sibling, src=x_refs[a]))
            first += [copy(a, 1 + j, me, (*chip, c), src=x_refs[a]) for j, chip in enumerate(chips)]
        for cp in first:
            cp.start()
        passed = []
        for a in range(na):
            for j, chip in enumerate(chips):
                copy(a, 1 + j, (*chip, c), me).wait_recv()
                passed.append(copy(a, 4 + j, (*chip, c), sibling))
                passed[-1].start()
        for a in range(na):
            copy(a, 0, sibling, me).wait_recv()
            for j, chip in enumerate(chips):
                copy(a, 4 + j, (*chip, 1 - c), me).wait_recv()
        for cp in first + passed:
            cp.wait_send()
        for cp in mine:
            cp.wait()

    return pl.pallas_call(
        body, name=name,
        out_shape=[jax.ShapeDtypeStruct((N_DEV,) + a.shape, a.dtype) for a in arrays],
        in_specs=[pl.BlockSpec(memory_space=pl.ANY)] * na,
        out_specs=[pl.BlockSpec(memory_space=pl.ANY)] * na,
        scratch_shapes=[pltpu.SemaphoreType.DMA((na, 7)), pltpu.SemaphoreType.DMA((na, 7)),
                        pltpu.SemaphoreType.DMA((na,))],
    )(*arrays)


_HBM = pl.BlockSpec(memory_space=pltpu.HBM)
_SEM = pl.BlockSpec(memory_space=pltpu.SEMAPHORE)


def _flip(k, x, y, c):
    return ((1 - x) if k & 4 else x, (1 - y) if k & 2 else y, (1 - c) if k & 1 else c)


_PEERS = {"gather": (1, 2, 4, 6), "scatter": (1, 2, 3, 4, 5, 6, 7)}


def _push_copies(kind, x_refs, land_refs, send_sems, recv_sems, local_sems):
    x, y, c = _my_coords()
    me = 4 * x + 2 * y + c
    peers = _PEERS[kind]
    remote, local = [], []
    for a in range(len(x_refs)):
        local.append(pltpu.make_async_copy(x_refs[a] if kind == "gather" else x_refs[a].at[me], land_refs[a].at[me],
                                           local_sems.at[a]))
        for n, k in enumerate(peers):
            px, py, pc = _flip(k, x, y, c)
            remote.append(pltpu.make_async_remote_copy(
                src_ref=x_refs[a] if kind == "gather" else x_refs[a].at[4 * px + 2 * py + pc],
                dst_ref=land_refs[a].at[me],
                send_sem=send_sems.at[a * len(peers) + n], recv_sem=recv_sems.at[a * len(peers) + n],
                device_id=(px, py, pc), device_id_type=pl.DeviceIdType.MESH))
    return remote, local


def _pass_to_sibling(name, lands):
    na = len(lands)
    chips = (2, 4, 6)

    def body(*refs):
        out_refs, send_sems, recv_sems = refs[na:2 * na], refs[2 * na], refs[2 * na + 1]
        x, y, c = _my_coords()
        cps = []
        for a in range(na):
            for n, k in enumerate(chips):
                px, py, _ = _flip(k, x, y, c)
                cps.append(pltpu.make_async_remote_copy(
                    src_ref=out_refs[a].at[4 * px + 2 * py + c], dst_ref=out_refs[a].at[4 * px + 2 * py + c],
                    send_sem=send_sems.at[a * 3 + n], recv_sem=recv_sems.at[a * 3 + n],
                    device_id=(x, y, 1 - c), device_id_type=pl.DeviceIdType.MESH))
        for cp in cps:
            cp.start()
        for a in range(na):
            for n, k in enumerate(chips):
                px, py, _ = _flip(k, x, y, c)
                blk = out_refs[a].at[4 * px + 2 * py + (1 - c)]
                pltpu.make_async_remote_copy(src_ref=blk, dst_ref=blk, send_sem=send_sems.at[a * 3 + n],
                                             recv_sem=recv_sems.at[a * 3 + n], device_id=(x, y, 1 - c),
                                             device_id_type=pl.DeviceIdType.MESH).wait_recv()
        for cp in cps:
            cp.wait_send()

    return pl.pallas_call(
        body, name=name,
        out_shape=[jax.ShapeDtypeStruct(l.shape, l.dtype) for l in lands],
        in_specs=[pl.BlockSpec(memory_space=pl.ANY)] * na,
        out_specs=[pl.BlockSpec(memory_space=pl.ANY)] * na,
        input_output_aliases={a: a for a in range(na)},
        scratch_shapes=[pltpu.SemaphoreType.DMA((3 * na,)), pltpu.SemaphoreType.DMA((3 * na,))],
    )(*lands)


_SIDE_EFFECT = pltpu.CompilerParams(has_side_effects=pltpu.SideEffectType.DATAFLOW_SIDE_EFFECTING)


def _push_start(name, kind, srcs, lands):
    na = len(srcs)

    def body(*refs):
        remote, local = _push_copies(kind, refs[:na], refs[na:2 * na], *refs[2 * na:2 * na + 3])
        for cp in remote + local:
            cp.start()
        token = refs[-1]
        token[...] = jnp.zeros_like(token)

    arrays = list(srcs) + list(lands)
    n_remote = na * len(_PEERS[kind])
    res = pl.pallas_call(
        body, name=name,
        out_shape=(pltpu.SemaphoreType.DMA((n_remote,)), pltpu.SemaphoreType.DMA((n_remote,)),
                   pltpu.SemaphoreType.DMA((na,)), *[pltpu.HBM(t.shape, t.dtype) for t in arrays],
                   jax.ShapeDtypeStruct((SUBLANE, LANE), f32)),
        in_specs=[_HBM] * (2 * na),
        out_specs=(_SEM, _SEM, _SEM, *[_HBM] * (2 * na), pl.BlockSpec(memory_space=pltpu.VMEM)),
        input_output_aliases={i: 3 + i for i in range(2 * na)},
        compiler_params=_SIDE_EFFECT,
    )(*[pltpu.with_memory_space_constraint(t, pltpu.HBM) for t in arrays])
    return list(res[:3]), res[3:3 + na], res[3 + na:3 + 2 * na], res[-1][:1, :1]


def _push_wait(name, kind, sems, srcs, lands, after):
    na = len(srcs)

    def body(*refs):
        remote, local = _push_copies(kind, refs[:na], refs[na:2 * na], *refs[2 * na:2 * na + 3])
        for cp in remote:
            cp.wait_send()
            cp.wait_recv()
        for cp in local:
            cp.wait()

    arrays = list(srcs) + list(lands)
    res = pl.pallas_call(
        body, name=name,
        out_shape=tuple(pltpu.HBM(t.shape, t.dtype) for t in arrays),
        in_specs=[_HBM] * (2 * na) + [_SEM] * 3 + [pl.BlockSpec(memory_space=pl.ANY)],
        out_specs=tuple([_HBM] * (2 * na)),
        input_output_aliases={i: i for i in range(2 * na)},
        compiler_params=_SIDE_EFFECT,
    )(*arrays, *sems, after)
    return res[na:]


def _sum_blocks(name, land):
    _, R, n = land.shape
    tr = R

    def body(l_ref, o_ref):
        acc = l_ref[0].astype(f32)
        for s in range(1, N_DEV):
            acc = acc + l_ref[s].astype(f32)
        o_ref[...] = acc

    return pl.pallas_call(
        body, name=name, grid=(R // tr,),
        in_specs=[pl.BlockSpec((N_DEV, tr, n), lambda i: (0, i, 0))],
        out_specs=pl.BlockSpec((tr, n), lambda i: (i, 0)),
        out_shape=jax.ShapeDtypeStruct((R, n), f32),
        compiler_params=_cparams("parallel"),
    )(land)


def _adamw(name, w, g, m, v):
    shape = w.shape
    last = shape[-1]
    rows = math.prod(shape[:-1])
    tm = 256 if rows % 256 == 0 and rows > 256 else rows
    w2, g2, m2, v2 = (t.reshape(rows, last) for t in (w, g, m, v))

    def body(w_ref, g_ref, m_ref, v_ref, d_ref, mo_ref, vo_ref):
        gg = g_ref[...]
        mn = ADAM_B1 * m_ref[...] + (1.0 - ADAM_B1) * gg
        vn = ADAM_B2 * v_ref[...] + (1.0 - ADAM_B2) * jnp.square(gg)
        m_hat = mn / (1.0 - ADAM_B1 ** ADAM_STEP)
        v_hat = vn / (1.0 - ADAM_B2 ** ADAM_STEP)
        d_ref[...] = -ADAM_LR * (m_hat / (jnp.sqrt(v_hat) + ADAM_EPS) + ADAM_WD * w_ref[...])
        mo_ref[...] = mn
        vo_ref[...] = vn

    spec = pl.BlockSpec((tm, last), lambda i: (i, 0))
    d, mn, vn = pl.pallas_call(
        body, name=name, grid=(rows // tm,), in_specs=[spec] * 4, out_specs=[spec] * 3,
        out_shape=[jax.ShapeDtypeStruct((rows, last), f32)] * 3,
        compiler_params=_cparams("parallel"),
    )(w2, g2, m2, v2)
    return d.reshape(shape), mn.reshape(shape), vn.reshape(shape)


def _adamw_land(name, lands, w, m, v, tm=256):
    L = len(lands)
    _, R, C = lands[0].shape
    tm = min(tm, R)

    def body(*refs):
        l_refs, (w_ref, m_ref, v_ref, g_ref, d_ref, mo_ref, vo_ref) = refs[:L], refs[L:]
        for k in range(L):
            @pl.when(pl.program_id(0) == k)
            def _(k=k):
                gg = l_refs[k][0].astype(f32)
                for s in range(1, N_DEV):
                    gg = gg + l_refs[k][s].astype(f32)
                g_ref[...] = gg
                mn = ADAM_B1 * m_ref[...] + (1.0 - ADAM_B1) * gg
                vn = ADAM_B2 * v_ref[...] + (1.0 - ADAM_B2) * jnp.square(gg)
                m_hat = mn / (1.0 - ADAM_B1 ** ADAM_STEP)
                v_hat = vn / (1.0 - ADAM_B2 ** ADAM_STEP)
                d_ref[...] = -ADAM_LR * (m_hat / (jnp.sqrt(v_hat) + ADAM_EPS) + ADAM_WD * w_ref[...])
                mo_ref[...] = mn
                vo_ref[...] = vn

    land_specs = [pl.BlockSpec((N_DEV, tm, C), lambda l, i, k=k: (0, jnp.where(l == k, i, 0), 0)) for k in range(L)]
    spec = pl.BlockSpec((None, tm, C), lambda l, i: (l, i, 0))
    return pl.pallas_call(
        body, name=name, grid=(L, R // tm),
        in_specs=land_specs + [spec] * 3,
        out_specs=[spec] * 4,
        out_shape=[jax.ShapeDtypeStruct((L, R, C), f32)] * 4,
        compiler_params=_cparams("arbitrary", "arbitrary"),
    )(*lands, w, m, v)


BIG = [("hyb_w_in", 2), ("hyb_w_out", 1), ("rec_w_in", 2), ("rec_w_out", 1), ("rec_w_a", 2), ("rec_w_x", 2),
       ("mlp_w1", 2), ("mlp_w2", 1)]
SMALL = [("hyb_conv_w", 2), ("rec_conv_w", 2), ("rec_conv_b", 1), ("rec_b_a", 1), ("rec_b_x", 1), ("rec_lambda", 1)]
REPL = ["hyb_sinks", "hyb_a_log", "hyb_dt_bias", "hyb_norm_w", "ln1_g", "ln1_b", "ln2_g", "ln2_b"]
WEIGHTS = ["hyb_w_in", "hyb_sinks", "hyb_conv_w", "hyb_a_log", "hyb_dt_bias", "hyb_norm_w", "hyb_w_out", "rec_w_in",
           "rec_conv_w", "rec_conv_b", "rec_w_a", "rec_b_a", "rec_w_x", "rec_b_x", "rec_lambda", "rec_w_out",
           "ln1_g", "ln1_b", "mlp_w1", "mlp_w2", "ln2_g", "ln2_b"]


def _pack_rows(parts, dtype, row_mult):
    lead = parts[0].shape[:-1]
    flat = jnp.concatenate([p.astype(dtype) for p in parts], axis=-1)
    n = flat.shape[-1]
    unit = row_mult * LANE
    pad = (-n) % unit
    if pad:
        flat = jnp.concatenate([flat, jnp.zeros(lead + (pad,), dtype)], axis=-1)
    return flat.reshape(lead + ((n + pad) // LANE, LANE))


def _gather_full(gathered, shard_shapes, table):
    flat = gathered.reshape(N_DEV, -1)
    out, off = {}, 0
    for name, ax in table:
        shp = shard_shapes[name]
        n = math.prod(shp)
        arr = flat[:, off:off + n].reshape((N_DEV,) + shp)
        off += n
        arr = jnp.moveaxis(arr, 0, ax)
        out[name] = arr.reshape(shp[:ax] + (N_DEV * shp[ax],) + shp[ax + 1:])
    return out


def _matmul_layouts(tag, gw):
    out = {}
    bw = D_MODEL // LRU_BLOCKS
    for k, g in gw.items():
        L = g.shape[1]
        if k == "hyb_w_in":
            out[k] = _merge_cols(f"{tag}_w_in_merge", g)
        elif k in ("hyb_w_out", "rec_w_out", "mlp_w2"):
            out[k] = jnp.swapaxes(g, 0, 1).reshape(L, N_DEV * g.shape[2], g.shape[3])
        elif k in ("rec_w_a", "rec_w_x"):
            out[k] = jnp.moveaxis(g, 0, 2).reshape(L, LRU_BLOCKS, bw, bw)
        else:
            out[k] = g
    return out


def kernel(x, hyb_w_in, hyb_sinks, hyb_conv_w, hyb_a_log, hyb_dt_bias, hyb_norm_w, hyb_w_out, rec_w_in, rec_conv_w, rec_conv_b, rec_w_a, rec_b_a, rec_w_x, rec_b_x, rec_lambda, rec_w_out, ln1_g, ln1_b, mlp_w1, mlp_w2, ln2_g, ln2_b, loss_target, m_hyb_w_in, m_hyb_sinks, m_hyb_conv_w, m_hyb_a_log, m_hyb_dt_bias, m_hyb_norm_w, m_hyb_w_out, m_rec_w_in, m_rec_conv_w, m_rec_conv_b, m_rec_w_a, m_rec_b_a, m_rec_w_x, m_rec_b_x, m_rec_lambda, m_rec_w_out, m_ln1_g, m_ln1_b, m_mlp_w1, m_mlp_w2, m_ln2_g, m_ln2_b, v_hyb_w_in, v_hyb_sinks, v_hyb_conv_w, v_hyb_a_log, v_hyb_dt_bias, v_hyb_norm_w, v_hyb_w_out, v_rec_w_in, v_rec_conv_w, v_rec_conv_b, v_rec_w_a, v_rec_b_a, v_rec_w_x, v_rec_b_x, v_rec_lambda, v_rec_w_out, v_ln1_g, v_ln1_b, v_mlp_w1, v_mlp_w2, v_ln2_g, v_ln2_b):
    args = locals()
    w = {k: args[k] for k in WEIGHTS}
    m = {k: args["m_" + k] for k in WEIGHTS}
    v = {k: args["v_" + k] for k in WEIGHTS}
    shard_shapes = {k: tuple(t.shape) for k, t in w.items()}
    xi, yi, ci = _my_coords()
    me = 4 * xi + 2 * yi + ci

    in_flight = {}

    def install(tag, names, got):
        for (k, i), arr in zip(names, _matmul_layouts(tag, {k: g for (k, _), g in zip(names, got)}).values()):
            W[k][i] = arr

    def start_gather(tag, names):
        srcs = [w[k][i:i + 1].astype(bf16) for k, i in names]
        *pending, zero = _push_start(f"gather_{tag}_start", "gather", srcs,
                                     [lax.empty((N_DEV,) + s.shape, bf16) for s in srcs])
        in_flight[tag] = (names, pending)
        return zero

    def finish_gather(tag, after):
        names, pending = in_flight.pop(tag)
        half = _push_wait(f"gather_{tag}_wait", "gather", *pending, after)
        install(tag, names, _pass_to_sibling(f"gather_{tag}_pass", half))

    def started(k, zero):
        W[k] = W[k] + zero

    def mixer_w(layer):
        return _layer_weights(layer)[:-2]

    def mlp_w(layer):
        return _layer_weights(layer)[-2:]

    gathered0 = _all_gather("gather_first", [w[k][i:i + 1].astype(bf16) for k, i in mixer_w(0)]
                            + [_pack_rows([w[k].reshape(-1) for k, _ in SMALL], f32, SUBLANE)])
    W = _gather_full(gathered0[-1], shard_shapes, SMALL)
    W.update({k: w[k] for k in REPL})
    W.update({k: {} for k, _ in BIG})
    install("l0a", mixer_w(0), gathered0[:-1])
    started("hyb_sinks", start_gather("l0b", mlp_w(0)) + start_gather("l1a", mixer_w(1)))

    def load_layer(layer, stage, after):
        if stage == 0:
            if layer > 0:
                finish_gather(f"l{layer}a", after)
            if 0 < layer < DEPTH - 1:
                started("hyb_sinks" if layer % 2 == 0 else "rec_conv_b",
                        start_gather(f"l{layer + 1}a", mixer_w(layer + 1)))
        if stage == 2:
            finish_gather(f"l{layer}b", after)
            if layer < DEPTH - 1:
                started("ln2_g", start_gather(f"l{layer + 1}b", mlp_w(layer + 1)))

    grads_in_flight = {}

    def grads_ready(tag, a, b):
        g = {**a, **b}
        srcs = list(g.values())
        *pending, zero = _push_start(f"scatter_{tag}_start", "scatter", srcs, [lax.empty(s.shape, bf16) for s in srcs])
        grads_in_flight[tag] = (list(g.keys()), pending)
        return zero

    loss_local, grad_x, G = _local_step(x[0], loss_target[0], W, load_layer, grads_ready)
    loss = lax.psum(loss_local, MESH_AXES)

    landed = {}

    def land(tag, after):
        keys, pending = grads_in_flight[tag]
        landed.update(zip(keys, _push_wait(f"scatter_{tag}_wait", "scatter", *pending, after)))

    tags = list(grads_in_flight)
    for tag in tags[:-1]:
        land(tag, grad_x)
    rest = _pack_rows([G[k].reshape(-1) for k, _ in SMALL] + [G[k].reshape(-1) for k in REPL], f32, SUBLANE)
    g_rest = _sum_blocks("sum_rest", _all_gather("gather_rest", [rest])[0]).reshape(-1)

    grads, delta, new_m, new_v = {}, {}, {}, {}

    def adamw_big(k):
        shp = shard_shapes[k]
        s3 = (shp[0], math.prod(shp[1:-1]), shp[-1])
        lands = [landed[(k, i)].reshape((N_DEV,) + s3[1:]) for i in range(shp[0])]
        res = _adamw_land("adamw_" + k, lands, w[k].reshape(s3), m[k].reshape(s3), v[k].reshape(s3))
        grads[k], delta[k], new_m[k], new_v[k] = (r.reshape(shp) for r in res)

    late = {k for k, _ in grads_in_flight[tags[-1]][0]}
    for k in [k for k, _ in BIG if k not in late]:
        adamw_big(k)
        done = new_v[k]
    land(tags[-1], done)
    for k in [k for k, _ in BIG if k in late]:
        adamw_big(k)
    off = 0
    for k, ax in SMALL:
        full_shape = G[k].shape
        n = math.prod(full_shape)
        full = g_rest[off:off + n].reshape(full_shape)
        off += n
        s = shard_shapes[k][ax]
        grads[k] = lax.dynamic_slice_in_dim(full, me * s, s, axis=ax)
    for k in REPL:
        n = math.prod(shard_shapes[k])
        grads[k] = g_rest[off:off + n].reshape(shard_shapes[k])
        off += n

    for k in [k for k, _ in SMALL] + REPL:
        delta[k], new_m[k], new_v[k] = _adamw("adamw_" + k, w[k], grads[k], m[k], v[k])

    return (loss, grad_x[None], *[grads[k] for k in WEIGHTS], *[delta[k] for k in WEIGHTS],
            *[new_m[k] for k in WEIGHTS], *[new_v[k] for k in WEIGHTS])
```

```python
import functools
import math

import jax
import jax.numpy as jnp
from jax import lax
from jax.experimental import pallas as pl
from jax.experimental.pallas import tpu as pltpu

f32 = jnp.float32
bf16 = jnp.bfloat16

N_DEV = 8
D_MODEL = 1024
DEPTH = 4
A_HEAD_DIM = 64
A_Q_HEADS = 8
WINDOW = 128
ROPE_THETA = 10000.0
B_HEADS = 4
B_HEAD_DIM = 128
B_CHUNK = 64
LRU_BLOCKS = 4
LRU_C = 8.0
D_FF = 4 * D_MODEL
HYB_PROJ = 2824
HYB_PROJ_PAD = 3072
DN_ALPHA = (2 * DEPTH) ** 0.25
LN_EPS = 1e-5
NORM_EPS = 1e-6
ADAM_LR = 0.001
ADAM_B1 = 0.9
ADAM_B2 = 0.999
ADAM_EPS = 1e-08
ADAM_WD = 0.01
ADAM_STEP = 10

LANE = 128
SUBLANE = 8
VMEM_LIMIT = 48 * 1024 * 1024

CB_QA, CB_KA, CB_VA, CB_CONV, CB_Z, CB_LG = 0, 4, 5, 6, 18, 22

MESH_AXES = ("x", "y", "c")


def _cparams(*sem):
    return pltpu.CompilerParams(dimension_semantics=sem, vmem_limit_bytes=VMEM_LIMIT)


def _dot(a, b, dims, precision=None):
    return lax.dot_general(a, b, (dims, ((), ())), preferred_element_type=f32, precision=precision)


NN = ((1,), (0,))
NT = ((1,), (1,))
TN = ((0,), (0,))


def _mat_spec(arr, kind, lead, br, bc, rb, cb):
    if kind == "plain":
        return pl.BlockSpec((br, bc), lambda i, j, k: (rb(i, j, k), cb(i, j, k)))
    if kind == "lead":
        return pl.BlockSpec((None, br, bc), lambda i, j, k: (lead, rb(i, j, k), cb(i, j, k)))
    assert kind == "devcol" and bc == arr.shape[-1]
    return pl.BlockSpec((None, None, br, bc), lambda i, j, k: (cb(i, j, k), lead, rb(i, j, k), 0))


def _mm(name, a, b, mode, *, b_kind="plain", b_lead=0, o_kind="plain", epilogue=None, extras=(), params=(),
        out_dtypes=(f32,), tm=1024, tn=1024, tk=None):
    if tk is None:
        tk = 512 if mode == "tn" else 1024
    if b_kind in ("plain", "lead"):
        b_rows, b_cols = b.shape[-2:]
    else:
        b_rows, b_cols = b.shape[-2], N_DEV * b.shape[-1]
    if mode == "nn":
        (M, K), (K2, N) = a.shape, (b_rows, b_cols)
    elif mode == "nt":
        (M, K), (N, K2) = a.shape, (b_rows, b_cols)
    else:
        (K, M), (K2, N) = a.shape, (b_rows, b_cols)
    assert K == K2, (name, a.shape, b.shape, mode)
    tm, tn, tk = min(tm, M), min(tn, N), min(tk, K)
    cols_are_n = mode != "nt"
    if b_kind == "devcol":
        tn, tk = (b.shape[-1], tk) if cols_are_n else (tn, b.shape[-1])
    shard = N // N_DEV
    if o_kind == "devcol":
        tn = max(shard, tn // shard * shard)
    assert M % tm == 0 and N % tn == 0 and K % tk == 0, (name, M, N, K, tm, tn, tk)
    nk = K // tk
    dims = {"nn": NN, "nt": NT, "tn": TN}[mode]
    n_ex, n_out = len(extras) + len(params), len(out_dtypes)

    def body(*refs):
        a_ref, b_ref = refs[:2]
        ex = refs[2:2 + n_ex]
        outs = refs[2 + n_ex:2 + n_ex + n_out]
        acc = refs[-1]
        k = pl.program_id(2)

        @pl.when(k == 0)
        def _():
            acc[...] = jnp.zeros_like(acc)

        acc[...] += _dot(a_ref[...].astype(bf16), b_ref[...].astype(bf16), dims)

        @pl.when(k == nk - 1)
        def _():
            r = acc[...]
            res = epilogue(r, *[e[...] for e in ex]) if epilogue is not None else (r,)
            for o, v in zip(outs, res):
                if o_kind == "plain":
                    o[...] = v.astype(o.dtype)
                else:
                    for q in range(tn // shard):
                        o[q] = v[:, q * shard:(q + 1) * shard].astype(o.dtype)

    if mode == "tn":
        a_spec = pl.BlockSpec((tk, tm), lambda i, j, k: (k, i))
    else:
        a_spec = pl.BlockSpec((tm, tk), lambda i, j, k: (i, k))
    jb, kb = (lambda i, j, k: j), (lambda i, j, k: k)
    if mode == "nt":
        b_spec = _mat_spec(b, b_kind, b_lead, tn, tk, jb, kb)
    else:
        b_spec = _mat_spec(b, b_kind, b_lead, tk, tn, kb, jb)
    e_spec = pl.BlockSpec((tm, tn), lambda i, j, k: (i, j))
    if o_kind == "plain":
        o_spec, o_shape = e_spec, (M, N)
    else:
        o_spec, o_shape = pl.BlockSpec((tn // shard, tm, shard), lambda i, j, k: (j, i, 0)), (N_DEV, M, shard)
    res = pl.pallas_call(
        body, name=name,
        grid=(M // tm, N // tn, nk),
        in_specs=[a_spec, b_spec] + [e_spec] * len(extras)
        + [pl.BlockSpec(p.shape, lambda i, j, k: (0, 0)) for p in params],
        out_specs=[o_spec] * n_out,
        out_shape=[jax.ShapeDtypeStruct(o_shape, dt) for dt in out_dtypes],
        scratch_shapes=[pltpu.VMEM((tm, tn), f32)],
        compiler_params=_cparams("parallel", "parallel", "arbitrary"),
    )(a, b, *extras, *params)
    return res[0] if n_out == 1 else res


def _row_spec(tm, cb, width):
    assert (cb * LANE) % width == 0
    blk = (cb * LANE) // width
    return pl.BlockSpec((tm, width), lambda i: (i, blk))


def _whole_spec(p):
    nd = p.ndim
    return pl.BlockSpec(p.shape, lambda i: (0,) * nd)


def _tl_bwd(name, fn, rows, params, cot_rows, cot_fn=None, skip=(), bf16_copy=False, tm=512):
    T = rows[0][0].shape[0]
    tm = min(tm, T)
    nr, npar, nc = len(rows), len(params), len(cot_rows)
    keep = [k for k in range(nr) if k not in skip]
    n_rows = len(keep) + int(bf16_copy)
    row_dtypes = [(rows[k][2], f32) for k in keep] + ([(rows[keep[0]][2], bf16)] if bf16_copy else [])

    def body(*refs):
        vals = [r[...] for r in refs[:nr + npar]]
        cots = [r[...] for r in refs[nr + npar:nr + npar + nc]]
        outs = refs[nr + npar + nc:]
        cot = tuple(cot_fn(*cots)) if cot_fn is not None else tuple(cots)
        _, vjp = jax.vjp(fn, *vals)
        grads = vjp(cot)
        for o, k in zip(outs, keep):
            o[...] = grads[k].astype(o.dtype)
        if bf16_copy:
            outs[len(keep)][...] = grads[keep[0]].astype(bf16)
        i = pl.program_id(0)
        for o, g in zip(outs[n_rows:], grads[nr:]):
            @pl.when(i == 0)
            def _(o=o):
                o[...] = jnp.zeros_like(o)
            o[...] += g

    res = pl.pallas_call(
        body, name=name, grid=(T // tm,),
        in_specs=[_row_spec(tm, cb, w) for (_, cb, w) in rows] + [_whole_spec(p) for p in params]
        + [_row_spec(tm, cb, w) for (_, cb, w) in cot_rows],
        out_specs=[pl.BlockSpec((tm, w), lambda i: (i, 0)) for w, _ in row_dtypes] + [_whole_spec(p) for p in params],
        out_shape=[jax.ShapeDtypeStruct((T, w), dt) for w, dt in row_dtypes]
        + [jax.ShapeDtypeStruct(p.shape, f32) for p in params],
        compiler_params=_cparams("arbitrary"),
    )(*[r[0] for r in rows], *params, *[r[0] for r in cot_rows])
    return res[:n_rows], res[n_rows:]


def _ln_res_fn(x, mix, g, b):
    pre = DN_ALPHA * x + mix
    mu = jnp.mean(pre, axis=-1, keepdims=True)
    var = jnp.mean(jnp.square(pre - mu), axis=-1, keepdims=True)
    return ((pre - mu) * lax.rsqrt(var + LN_EPS) * g + b,)


@jax.custom_jvp
def _expm1(x):
    small = jnp.abs(x) < 0.3
    xs = jnp.where(small, x, 0.0)
    poly = xs * (1.0 + xs * (1 / 2 + xs * (1 / 6 + xs * (1 / 24 + xs * (1 / 120 + xs * (
        1 / 720 + xs * (1 / 5040 + xs * (1 / 40320 + xs * (1 / 362880)))))))))
    return jnp.where(small, poly, jnp.exp(x) - 1.0)


@_expm1.defjvp
def _expm1_jvp(primals, tangents):
    (x,), (t,) = primals, tangents
    return _expm1(x), t * jnp.exp(x)


def _rglru_pre_fn(pre_r, pre_i, xc, b_a, b_x, lam):
    r = jax.nn.sigmoid(pre_r + b_a)
    i = jax.nn.sigmoid(pre_i + b_x)
    log_a = -LRU_C * r * jax.nn.softplus(-lam)
    a = jnp.exp(log_a)
    b = jnp.sqrt(-_expm1(2.0 * log_a)) * (i * xc)
    return a, b


def _rec_gate_fn(h, gate):
    return (h * jax.nn.gelu(gate),)


def _loss_head(y, t, tm=512):
    T, Dm = y.shape

    def body(y_ref, t_ref, dy_ref, loss_ref):
        e = y_ref[...] - t_ref[...]
        dy_ref[...] = e * (1.0 / Dm)

        @pl.when(pl.program_id(0) == 0)
        def _():
            loss_ref[...] = jnp.zeros_like(loss_ref)

        loss_ref[...] += 0.5 * jnp.sum(jnp.mean(e * e, axis=-1, keepdims=True), axis=0, keepdims=True)

    dy, loss = pl.pallas_call(
        body, name="loss_head", grid=(T // tm,),
        in_specs=[pl.BlockSpec((tm, Dm), lambda i: (i, 0))] * 2,
        out_specs=[pl.BlockSpec((tm, Dm), lambda i: (i, 0)), pl.BlockSpec((SUBLANE, LANE), lambda i: (0, 0))],
        out_shape=[jax.ShapeDtypeStruct((T, Dm), f32), jax.ShapeDtypeStruct((SUBLANE, LANE), f32)],
        compiler_params=_cparams("arbitrary"),
    )(y, t)
    return loss[0, 0], dy


def _conv_fwd(name, x, cb0, nblk, w, bias, tm=2048):
    T = x.shape[0]
    tm = min(tm, T)
    hb = tm // SUBLANE
    has_b = bias is not None

    def body(*refs):
        cur, prev, w_ref = refs[:3]
        b_ref = refs[3] if has_b else None
        o = refs[-1]
        i = pl.program_id(1)
        p = jnp.where(i > 0, prev[...], 0.0)
        xcat = jnp.concatenate([p, cur[...]], axis=0)
        acc = cur[...] * w_ref[3:4, :]
        for j in range(3):
            acc = acc + pltpu.roll(xcat, 3 - j, axis=0)[SUBLANE:] * w_ref[j:j + 1, :]
        if has_b:
            acc = acc + b_ref[...]
        o[...] = acc

    in_specs = [
        pl.BlockSpec((tm, LANE), lambda c, i: (i, cb0 + c)),
        pl.BlockSpec((SUBLANE, LANE), lambda c, i: (jnp.maximum(i * hb - 1, 0), cb0 + c)),
        pl.BlockSpec((4, LANE), lambda c, i: (0, c)),
    ]
    args = [x, x, w]
    if has_b:
        in_specs.append(pl.BlockSpec((1, LANE), lambda c, i: (0, c)))
        args.append(bias)
    return pl.pallas_call(
        body, name=name, grid=(nblk, T // tm),
        in_specs=in_specs,
        out_specs=pl.BlockSpec((tm, LANE), lambda c, i: (i, c)),
        out_shape=jax.ShapeDtypeStruct((T, nblk * LANE), f32),
        compiler_params=_cparams("parallel", "parallel"),
    )(*args)


def _conv_bwd(name, dy, x, cb0, nblk, w, into, into_cb, tm=2048):
    T = x.shape[0]
    tm = min(tm, T)
    hb = tm // SUBLANE
    nt = T // tm

    def body(dcur, dnext, xcur, xprev, w_ref, _, dx_ref, dw_ref, db_ref):
        i = pl.program_id(1)
        d = dcur[...]
        dn = jnp.where(i < nt - 1, dnext[...], 0.0)
        dcat = jnp.concatenate([d, dn], axis=0)
        acc = d * w_ref[3:4, :]
        for j in range(3):
            s = 3 - j
            acc = acc + pltpu.roll(dcat, tm + SUBLANE - s, axis=0)[:tm] * w_ref[j:j + 1, :]
        dx_ref[...] = acc

        p = jnp.where(i > 0, xprev[...], 0.0)
        xcat = jnp.concatenate([p, xcur[...]], axis=0)
        rows = [jnp.sum(d * pltpu.roll(xcat, 3 - j, axis=0)[SUBLANE:], axis=0, keepdims=True) for j in range(3)]
        rows.append(jnp.sum(d * xcur[...], axis=0, keepdims=True))
        rows.append(jnp.zeros((SUBLANE - 4, LANE), f32))

        @pl.when(i == 0)
        def _():
            dw_ref[...] = jnp.zeros_like(dw_ref)
            db_ref[...] = jnp.zeros_like(db_ref)

        dw_ref[...] += jnp.concatenate(rows, axis=0)
        db_ref[...] += jnp.broadcast_to(jnp.sum(d, axis=0, keepdims=True), (SUBLANE, LANE))

    nh = T // SUBLANE
    dx, dw, db = pl.pallas_call(
        body, name=name, grid=(nblk, nt),
        in_specs=[
            pl.BlockSpec((tm, LANE), lambda c, i: (i, c)),
            pl.BlockSpec((SUBLANE, LANE), lambda c, i: (jnp.minimum((i + 1) * hb, nh - 1), c)),
            pl.BlockSpec((tm, LANE), lambda c, i: (i, cb0 + c)),
            pl.BlockSpec((SUBLANE, LANE), lambda c, i: (jnp.maximum(i * hb - 1, 0), cb0 + c)),
            pl.BlockSpec((4, LANE), lambda c, i: (0, c)),
            pl.BlockSpec(memory_space=pl.ANY),
        ],
        out_specs=[
            pl.BlockSpec((tm, LANE), lambda c, i: (i, into_cb + c)),
            pl.BlockSpec((SUBLANE, LANE), lambda c, i: (0, c)),
            pl.BlockSpec((SUBLANE, LANE), lambda c, i: (0, c)),
        ],
        out_shape=[jax.ShapeDtypeStruct(into.shape, f32),
                   jax.ShapeDtypeStruct((SUBLANE, nblk * LANE), f32),
                   jax.ShapeDtypeStruct((SUBLANE, nblk * LANE), f32)],
        input_output_aliases={5: 0},
        compiler_params=_cparams("parallel", "arbitrary"),
    )(dy, dy, x, x, w, into)
    return dx, dw[:4], db[0]


@functools.partial(jax.custom_vjp, nondiff_argnums=(1,))
def _lroll(x, s):
    return pltpu.roll(x, s, axis=1)


def _lroll_fwd(x, s):
    return _lroll(x, s), None


def _lroll_bwd(s, _, g):
    return (_lroll(g, (LANE - s) % LANE),)


_lroll.defvjp(_lroll_fwd, _lroll_bwd)


def _rope_tables(T):
    half = A_HEAD_DIM // 2
    inv_freq = ROPE_THETA ** (-jnp.arange(half, dtype=f32) / half)
    ang = jnp.arange(T, dtype=f32)[:, None] * inv_freq[None, :]
    cos, sin = jnp.cos(ang), jnp.sin(ang)
    return jnp.tile(jnp.concatenate([cos, cos], axis=1), (1, 2)), jnp.tile(jnp.concatenate([-sin, sin], axis=1), (1, 2))


def _attn_block_fn(n, q, kp, kc, vp, vc, cq, sq, cp, sp, sinks):
    W = WINDOW
    lane = lax.broadcasted_iota(jnp.int32, (W, LANE), 1)
    lo_half = (lane % A_HEAD_DIM) < (A_HEAD_DIM // 2)
    lane8 = lax.broadcasted_iota(jnp.int32, sinks.shape, 1)

    def rope(x, c, s):
        return x * c + jnp.where(lo_half, _lroll(x, LANE - A_HEAD_DIM // 2), _lroll(x, A_HEAD_DIM // 2)) * s

    k2 = jnp.concatenate([rope(kp, cp, sp), rope(kc, cq, sq)], axis=0).astype(bf16)
    v2 = jnp.concatenate([vp, vc], axis=0).astype(bf16)
    qs = []
    for t in range(4):
        qt = rope(q[:, LANE * t:LANE * (t + 1)], cq, sq)
        g = t // 2
        for hh in range(2):
            qa = jnp.where((lane // A_HEAD_DIM) == hh, qt, 0.0)
            qs.append(_lroll(qa, A_HEAD_DIM) if hh != g else qa)
    s_all = _dot(jnp.concatenate(qs, axis=0).astype(bf16), k2, NT) * (A_HEAD_DIM ** -0.5)
    row = lax.broadcasted_iota(jnp.int32, (W, 2 * W), 0)
    col = lax.broadcasted_iota(jnp.int32, (W, 2 * W), 1)
    dist = row + W - col
    mask = (dist >= 0) & (dist < W) & ((col >= W) | (n > 0))
    ps = []
    for j in range(A_Q_HEADS):
        s = jnp.where(mask, s_all[W * j:W * (j + 1)], -jnp.inf)
        sink = jnp.sum(jnp.where(lane8 == j, sinks, 0.0), axis=1, keepdims=True)
        m = jnp.maximum(jnp.max(s, axis=-1, keepdims=True), sink)
        e = jnp.exp(s - m)
        ps.append((e / (jnp.sum(e, axis=-1, keepdims=True) + jnp.exp(sink - m))).astype(bf16))
    o = _dot(jnp.concatenate(ps, axis=0), v2, NN)
    outs = []
    for t in range(4):
        g = t // 2
        ot = jnp.zeros((W, LANE), f32)
        for hh in range(2):
            j = 2 * t + hh
            oj = jnp.where((lane // A_HEAD_DIM) == g, o[W * j:W * (j + 1)], 0.0)
            ot = ot + (_lroll(oj, A_HEAD_DIM) if hh != g else oj)
        outs.append(ot)
    return jnp.concatenate(outs, axis=1)


def _attn_specs():
    W = WINDOW
    prev = lambda n: jnp.maximum(n - 1, 0)
    return [
        pl.BlockSpec((W, 4 * LANE), lambda n: (n, CB_QA // 4)),
        pl.BlockSpec((W, LANE), lambda n: (prev(n), CB_KA)),
        pl.BlockSpec((W, LANE), lambda n: (n, CB_KA)),
        pl.BlockSpec((W, LANE), lambda n: (prev(n), CB_VA)),
        pl.BlockSpec((W, LANE), lambda n: (n, CB_VA)),
        pl.BlockSpec((W, LANE), lambda n: (n, 0)),
        pl.BlockSpec((W, LANE), lambda n: (n, 0)),
        pl.BlockSpec((W, LANE), lambda n: (prev(n), 0)),
        pl.BlockSpec((W, LANE), lambda n: (prev(n), 0)),
        pl.BlockSpec((1, A_Q_HEADS), lambda n: (0, 0)),
    ]


def _attn_fwd(name, proj, cos, sin, sinks):
    T = proj.shape[0]
    W = WINDOW

    def body(*refs):
        o = refs[-1]
        o[...] = _attn_block_fn(pl.program_id(0), *[r[...] for r in refs[:-1]])

    return pl.pallas_call(
        body, name=name, grid=(T // W,),
        in_specs=_attn_specs(),
        out_specs=pl.BlockSpec((W, 4 * LANE), lambda n: (n, 0)),
        out_shape=jax.ShapeDtypeStruct((T, 2 * 4 * LANE), f32),
        compiler_params=_cparams("parallel"),
    )(proj, proj, proj, proj, proj, cos, sin, cos, sin, sinks)


def _attn_bwd(name, proj, cos, sin, sinks, d_oab):
    T = proj.shape[0]
    W = WINDOW
    Q = 4 * LANE

    def body(*refs):
        ins = [r[...] for r in refs[:10]]
        do = refs[10][...]
        d_ref, ds_ref = refs[11:]
        n = pl.program_id(0)
        _, vjp = jax.vjp(functools.partial(_attn_block_fn, n), *ins)
        dq, dkp, dkc, dvp, dvc, _, _, _, _, dsk = vjp(do)

        @pl.when(n == 0)
        def _():
            d_ref[:, Q:] = jnp.zeros((T, 2 * LANE), f32)
            ds_ref[...] = jnp.zeros_like(ds_ref)

        cur = pl.ds(pl.multiple_of(n * W, W), W)
        d_ref[cur, :Q] = dq
        d_ref[cur, Q:Q + LANE] += dkc
        d_ref[cur, Q + LANE:] += dvc
        ds_ref[...] += dsk

        @pl.when(n > 0)
        def _():
            prv = pl.ds(pl.multiple_of((n - 1) * W, W), W)
            d_ref[prv, Q:Q + LANE] += dkp
            d_ref[prv, Q + LANE:] += dvp

    return pl.pallas_call(
        body, name=name, grid=(T // W,),
        in_specs=_attn_specs() + [pl.BlockSpec((W, Q), lambda n: (n, 0))],
        out_specs=[pl.BlockSpec((T, Q + 2 * LANE), lambda n: (0, 0)),
                   pl.BlockSpec((1, A_Q_HEADS), lambda n: (0, 0))],
        out_shape=[jax.ShapeDtypeStruct((T, HYB_PROJ_PAD), f32), jax.ShapeDtypeStruct((1, A_Q_HEADS), f32)],
        compiler_params=_cparams("arbitrary"),
    )(proj, proj, proj, proj, proj, cos, sin, cos, sin, sinks, d_oab)


def _bdot(spec, a, b, precision=None):
    return jnp.einsum(spec, a, b, preferred_element_type=f32, precision=precision)


@jax.custom_vjp
def _tri_inv(a):
    H, C, _ = a.shape
    B = 2 * SUBLANE
    nb = C // B
    r = lax.broadcasted_iota(jnp.int32, (C, C), 0)
    c = lax.broadcasted_iota(jnp.int32, (C, C), 1)
    a4 = jnp.where((r // B) == (c // B), a, 0.0).reshape(H, nb, B, C)
    t4 = jnp.broadcast_to(jnp.where(r == c, 1.0, 0.0).astype(f32), a.shape).reshape(H, nb, B, C)
    for j in range(B - 1):
        col = jnp.concatenate([a4[:, b:b + 1, :, B * b + j:B * b + j + 1] for b in range(nb)], axis=1)
        t4 = t4 - col * t4[:, :, j:j + 1, :]
    x = t4.reshape(H, C, C)
    hi = lax.Precision.HIGH
    while B < C:
        m = jnp.where(((r // (2 * B)) == (c // (2 * B))) & ((r // B) > (c // B)), a, 0.0)
        x = x - _bdot("hij,hjk->hik", x, _bdot("hij,hjk->hik", m, x, precision=hi), precision=hi)
        B *= 2
    return x


def _tri_inv_fwd(a):
    t = _tri_inv(a)
    return t, t


def _tri_inv_bwd(t, g):
    C = t.shape[-1]
    r = lax.broadcasted_iota(jnp.int32, (C, C), 0)
    c = lax.broadcasted_iota(jnp.int32, (C, C), 1)
    x = _bdot("hki,hkj->hij", t, g, precision=lax.Precision.HIGH)
    y = _bdot("hik,hjk->hij", x, t, precision=lax.Precision.HIGH)
    return (jnp.where(r > c, -y, 0.0),)


_tri_inv.defvjp(_tri_inv_fwd, _tri_inv_bwd)


@jax.custom_vjp
def _tri_inv_saved(a, t):
    return t


_tri_inv_saved.defvjp(lambda a, t: (t, t), lambda t, g: (_tri_inv_bwd(t, g)[0], jnp.zeros_like(t)))


def _silu(x):
    return x * jax.nn.sigmoid(x)


def _l2n(x):
    return x * lax.rsqrt(jnp.sum(x * x, axis=-1, keepdims=True) + NORM_EPS)


def _delta_chunk_fn(cq, ck, cv, z, lg, a_log, dt_bias, norm_w, S, t_saved=None, want_t=False):
    C = B_CHUNK
    lane = lax.broadcasted_iota(jnp.int32, (C, LANE), 1)
    pick = lambda l0: jnp.concatenate(
        [jnp.sum(jnp.where(lane == l0 + h, lg, 0.0), axis=1, keepdims=True)[None] for h in range(B_HEADS)], axis=0)
    bl, al = pick(0), pick(B_HEADS)
    q = _l2n(_silu(cq)) * (B_HEAD_DIM ** -0.5)
    k = _l2n(_silu(ck))
    v = _silu(cv)
    beta = jax.nn.sigmoid(bl)
    g = -jnp.exp(a_log) * jax.nn.softplus(al + dt_bias)
    r = lax.broadcasted_iota(jnp.int32, (C, C), 0)
    c = lax.broadcasted_iota(jnp.int32, (C, C), 1)
    eye = r == c
    g_row = jnp.sum(jnp.where(eye, g, 0.0), axis=1, keepdims=True)
    gc = jnp.sum(jnp.where(c <= r, g_row, 0.0), axis=2, keepdims=True)
    gc_row = jnp.sum(jnp.where(eye, gc, 0.0), axis=1, keepdims=True)
    decay_incl = jnp.exp(jnp.where(r >= c, gc - gc_row, -jnp.inf))
    decay_strict = jnp.where(r > c, decay_incl, 0.0)
    kb = k * beta
    vb = v * beta
    kbf = k.astype(bf16)
    a_mat = _bdot("hik,hjk->hij", kb.astype(bf16), kbf) * decay_strict
    t_f32 = _tri_inv(a_mat) if t_saved is None else _tri_inv_saved(a_mat, t_saved)
    t_mat = t_f32.astype(bf16)
    eg = jnp.exp(gc)
    u = _bdot("hij,hjv->hiv", t_mat, vb.astype(bf16))
    w = _bdot("hij,hjk->hik", t_mat, (kb * eg).astype(bf16))
    qk = _bdot("hik,hjk->hij", q.astype(bf16), kbf) * decay_incl
    g_last = jnp.sum(g, axis=1, keepdims=True)
    k_tail = k * jnp.exp(g_last - gc)
    Sb = S.astype(bf16)
    v_new = u - _bdot("hck,hkv->hcv", w.astype(bf16), Sb)
    o = _bdot("hck,hkv->hcv", (q * eg).astype(bf16), Sb) + _bdot("hij,hjv->hiv", qk.astype(bf16), v_new.astype(bf16))
    S_new = S * jnp.exp(g_last) + _bdot("hck,hcv->hkv", k_tail.astype(bf16), v_new.astype(bf16))
    ob = o * lax.rsqrt(jnp.mean(o * o, axis=-1, keepdims=True) + NORM_EPS) * norm_w
    return (ob * _silu(z), S_new) + ((t_f32,) if want_t else ())


DELTA_CHUNKS_PER_STEP = 4


def _delta_in_specs(rev, N):
    C = DELTA_CHUNKS_PER_STEP * B_CHUNK
    ix = (lambda n: N - 1 - n) if rev else (lambda n: n)
    specs = [pl.BlockSpec((C, 3 * B_HEADS * LANE), lambda n: (ix(n), 0))]
    specs += [pl.BlockSpec((C, LANE), lambda n, h=h: (ix(n), CB_Z + h)) for h in range(B_HEADS)]
    specs += [
        pl.BlockSpec((C, LANE), lambda n: (ix(n), CB_LG)),
        pl.BlockSpec((B_HEADS, 1, 1), lambda n: (0, 0, 0)),
        pl.BlockSpec((B_HEADS, 1, 1), lambda n: (0, 0, 0)),
        pl.BlockSpec((1, LANE), lambda n: (0, 0)),
    ]
    return specs


def _delta_inputs(u, c_ref, z_refs, lg, al, dt, nw):
    H = B_HEADS
    rows = slice(u * B_CHUNK, (u + 1) * B_CHUNK)
    part = lambda p: jnp.stack([c_ref[rows, LANE * (p * H + h):LANE * (p * H + h + 1)] for h in range(H)])
    return (part(0), part(1), part(2), jnp.stack([z[rows, :] for z in z_refs]), lg[rows, :], al[...], dt[...], nw[...])


def _delta_fwd(name, c, proj, a_log, dt_bias, norm_w, o_ab):
    T = c.shape[0]
    C = B_CHUNK
    N = T // C
    Dh = B_HEAD_DIM
    H = B_HEADS

    def body(*refs):
        c_ref, z_refs, (lg, al, dt, nw) = refs[0], refs[1:1 + H], refs[1 + H:5 + H]
        o_ref, s_ref, t_ref, S = refs[6 + H:]

        @pl.when(pl.program_id(0) == 0)
        def _():
            S[...] = jnp.zeros_like(S)

        s = S[...]
        for u in range(U):
            s_ref[:, u] = s
            ob, s, t = _delta_chunk_fn(*_delta_inputs(u, c_ref, z_refs, lg, al, dt, nw), s, want_t=True)
            for h in range(H):
                o_ref[u * C:(u + 1) * C, LANE * h:LANE * (h + 1)] = ob[h]
            t_ref[:, u] = t
        S[...] = s

    U = DELTA_CHUNKS_PER_STEP
    return pl.pallas_call(
        body, name=name, grid=(N // U,),
        in_specs=_delta_in_specs(False, N // U) + [pl.BlockSpec(memory_space=pl.ANY)],
        out_specs=[pl.BlockSpec((U * C, H * LANE), lambda n: (n, 1)),
                   pl.BlockSpec((H, U, Dh, Dh), lambda n: (0, n, 0, 0)),
                   pl.BlockSpec((H, U, C, C), lambda n: (0, n, 0, 0))],
        out_shape=[jax.ShapeDtypeStruct(o_ab.shape, f32), jax.ShapeDtypeStruct((H, N, Dh, Dh), f32),
                   jax.ShapeDtypeStruct((H, N, C, C), f32)],
        input_output_aliases={5 + H: 0},
        scratch_shapes=[pltpu.VMEM((H, Dh, Dh), f32)],
        compiler_params=_cparams("arbitrary"),
    )(c, *([proj] * H), proj, a_log, dt_bias, norm_w, o_ab)


def _delta_bwd(name, c, proj, a_log, dt_bias, norm_w, s_saved, t_saved, d_oab, dproj):
    T = c.shape[0]
    C = B_CHUNK
    N = T // C
    Dh = B_HEAD_DIM
    H = B_HEADS

    def body(*refs):
        c_ref, z_refs, (lg, al, dt, nw) = refs[0], refs[1:1 + H], refs[1 + H:5 + H]
        s_ref, t_ref, do_ref = refs[5 + H:8 + H]
        dc, dtail, dal, ddt, dnw, dS = refs[9 + H:]

        @pl.when(pl.program_id(0) == 0)
        def _():
            dS[...] = jnp.zeros_like(dS)
            dal[...] = jnp.zeros_like(dal)
            ddt[...] = jnp.zeros_like(ddt)
            dnw[...] = jnp.zeros_like(dnw)

        ds = dS[...]
        for u in reversed(range(U)):
            rows = slice(u * C, (u + 1) * C)
            _, vjp = jax.vjp(functools.partial(_delta_chunk_fn, t_saved=t_ref[:, u]),
                             *_delta_inputs(u, c_ref, z_refs, lg, al, dt, nw), s_ref[:, u])
            do = jnp.stack([do_ref[rows, LANE * h:LANE * (h + 1)] for h in range(H)])
            g = vjp((do, ds))
            for h in range(H):
                for p in range(3):
                    dc[rows, LANE * (p * H + h):LANE * (p * H + h + 1)] = g[p][h]
                dtail[rows, LANE * h:LANE * (h + 1)] = g[3][h]
            dtail[rows, LANE * H:LANE * (H + 1)] = g[4]
            dtail[rows, LANE * (H + 1):] = jnp.zeros((C, LANE), f32)
            dal[...] += g[5]
            ddt[...] += g[6]
            dnw[...] += g[7]
            ds = g[8]
        dS[...] = ds

    U = DELTA_CHUNKS_PER_STEP
    NB = N // U
    rn = lambda n: NB - 1 - n
    return pl.pallas_call(
        body, name=name, grid=(NB,),
        in_specs=_delta_in_specs(True, NB) + [
            pl.BlockSpec((H, U, Dh, Dh), lambda n: (0, rn(n), 0, 0)),
            pl.BlockSpec((H, U, C, C), lambda n: (0, rn(n), 0, 0)),
            pl.BlockSpec((U * C, H * LANE), lambda n: (rn(n), 1)),
            pl.BlockSpec(memory_space=pl.ANY),
        ],
        out_specs=[
            pl.BlockSpec((U * C, 3 * H * LANE), lambda n: (rn(n), 0)),
            pl.BlockSpec((U * C, (H + 2) * LANE), lambda n: (rn(n), CB_Z // (H + 2))),
            pl.BlockSpec((H, 1, 1), lambda n: (0, 0, 0)),
            pl.BlockSpec((H, 1, 1), lambda n: (0, 0, 0)),
            pl.BlockSpec((1, LANE), lambda n: (0, 0)),
        ],
        out_shape=[jax.ShapeDtypeStruct((T, 3 * H * Dh), f32), jax.ShapeDtypeStruct(dproj.shape, f32),
                   jax.ShapeDtypeStruct((H, 1, 1), f32), jax.ShapeDtypeStruct((H, 1, 1), f32),
                   jax.ShapeDtypeStruct((1, LANE), f32)],
        input_output_aliases={8 + H: 1},
        scratch_shapes=[pltpu.VMEM((H, Dh, Dh), f32)],
        compiler_params=_cparams("arbitrary"),
    )(c, *([proj] * H), proj, a_log, dt_bias, norm_w, s_saved, t_saved, d_oab, dproj)


def _gate_matmuls(xc, wa_ref, wx_ref):
    bw = wa_ref.shape[-1]
    xb = xc.astype(bf16)
    blocks = [xb[:, bw * h:bw * (h + 1)] for h in range(LRU_BLOCKS)]
    return (jnp.concatenate([_dot(blocks[h], wa_ref[h], NN) for h in range(LRU_BLOCKS)], axis=1),
            jnp.concatenate([_dot(blocks[h], wx_ref[h], NN) for h in range(LRU_BLOCKS)], axis=1))


def _gates_fwd(name, xc, w_a, w_x, pars, tm=512):
    T, Wd = xc.shape
    tm = min(tm, T)

    def body(x_ref, wa_ref, wx_ref, ba, bx, lam, a_ref, b_ref):
        x = x_ref[...]
        pr, pi = _gate_matmuls(x, wa_ref, wx_ref)
        a_ref[...], b_ref[...] = _rglru_pre_fn(pr, pi, x, ba[...], bx[...], lam[...])

    row = pl.BlockSpec((tm, Wd), lambda i: (i, 0))
    return pl.pallas_call(
        body, name=name, grid=(T // tm,),
        in_specs=[row, _whole_spec(w_a), _whole_spec(w_x)] + [_whole_spec(p) for p in pars],
        out_specs=[row, row], out_shape=[jax.ShapeDtypeStruct((T, Wd), f32)] * 2,
        compiler_params=_cparams("parallel"),
    )(xc, w_a, w_x, *pars)


def _gates_bwd(name, xc, w_a, w_x, pars, lam_t, h_prev, tm=512):
    T, Wd = xc.shape
    tm = min(tm, T)
    bw = Wd // LRU_BLOCKS

    def body(x_ref, wa_ref, wx_ref, ba, bx, lam, lt_ref, hp_ref, dx_ref, dr_ref, di_ref, dba, dbx, dlam):
        x = x_ref[...]
        pr, pi = _gate_matmuls(x, wa_ref, wx_ref)
        _, vjp = jax.vjp(_rglru_pre_fn, pr, pi, x, ba[...], bx[...], lam[...])
        lt = lt_ref[...]
        dpr, dpi, dxc, g_ba, g_bx, g_lam = vjp((lt * hp_ref[...], lt))
        dprb, dpib = dpr.astype(bf16), dpi.astype(bf16)
        dx_ref[...] = dxc + jnp.concatenate(
            [_dot(dprb[:, bw * h:bw * (h + 1)], wa_ref[h], NT) + _dot(dpib[:, bw * h:bw * (h + 1)], wx_ref[h], NT)
             for h in range(LRU_BLOCKS)], axis=1)
        dr_ref[...] = dprb
        di_ref[...] = dpib

        @pl.when(pl.program_id(0) == 0)
        def _():
            dba[...] = jnp.zeros_like(dba)
            dbx[...] = jnp.zeros_like(dbx)
            dlam[...] = jnp.zeros_like(dlam)

        dba[...] += g_ba
        dbx[...] += g_bx
        dlam[...] += g_lam

    row = pl.BlockSpec((tm, Wd), lambda i: (i, 0))
    vec = pl.BlockSpec((1, Wd), lambda i: (0, 0))
    return pl.pallas_call(
        body, name=name, grid=(T // tm,),
        in_specs=[row, _whole_spec(w_a), _whole_spec(w_x)] + [_whole_spec(p) for p in pars] + [row, row],
        out_specs=[row, row, row, vec, vec, vec],
        out_shape=[jax.ShapeDtypeStruct((T, Wd), f32), jax.ShapeDtypeStruct((T, Wd), bf16),
                   jax.ShapeDtypeStruct((T, Wd), bf16)] + [jax.ShapeDtypeStruct((1, Wd), f32)] * 3,
        compiler_params=_cparams("arbitrary"),
    )(xc, w_a, w_x, *pars, lam_t, h_prev)


def _blockdiag_bwd_dw(name, xc, dpr, dpi, tk=512):
    T, Wd = xc.shape
    bw = Wd // LRU_BLOCKS
    tk = min(tk, T)

    def body(x_ref, dr, di, oa, ox):
        @pl.when(pl.program_id(1) == 0)
        def _():
            oa[...] = jnp.zeros_like(oa)
            ox[...] = jnp.zeros_like(ox)

        xb = x_ref[...].astype(bf16)
        oa[...] += _dot(xb, dr[...].astype(bf16), TN)
        ox[...] += _dot(xb, di[...].astype(bf16), TN)

    xs = pl.BlockSpec((tk, bw), lambda h, k: (k, h))
    ws = pl.BlockSpec((None, bw, bw), lambda h, k: (h, 0, 0))
    return pl.pallas_call(
        body, name=name, grid=(LRU_BLOCKS, T // tk), in_specs=[xs, xs, xs], out_specs=[ws, ws],
        out_shape=[jax.ShapeDtypeStruct((LRU_BLOCKS, bw, bw), f32)] * 2,
        compiler_params=_cparams("parallel", "arbitrary"),
    )(xc, dpr, dpi)


def _scan(name, a, proj, reverse, b=None, h=None, dhg=None, tt=512, cb=512):
    T, Wd = a.shape
    tt, cb = min(tt, T), min(cb, Wd)
    nt = T // tt
    ng = tt // SUBLANE

    def body(a_ref, g_ref, *rest):
        n_in = 2 if reverse else 1
        ins, outs, (carry, carry_a) = rest[:n_in], rest[n_in:-2], rest[-2:]

        @pl.when(pl.program_id(1) == 0)
        def _():
            carry[...] = jnp.zeros_like(carry)
            carry_a[...] = jnp.zeros_like(carry_a)

        row = lax.broadcasted_iota(jnp.int32, (SUBLANE, cb), 0)

        def step(gi, c):
            hp, ap = c
            g = (ng - 1 - gi) if reverse else gi
            rows = pl.ds(pl.multiple_of(g * SUBLANE, SUBLANE), SUBLANE)
            A = a_ref[rows, :]
            gate = g_ref[rows, :]
            a_first = jnp.broadcast_to(A[0:1, :], (SUBLANE, cb))
            if reverse:
                _, vjp = jax.vjp(_rec_gate_fn, ins[0][rows, :], gate)
                B, dgate = vjp((ins[1][rows, :],))
                outs[1][rows, :] = dgate
                A = jnp.where(row == SUBLANE - 1, ap, pltpu.roll(A, SUBLANE - 1, axis=0))
            else:
                B = ins[0][rows, :]
            for s in (1, 2, 4):
                sh = (SUBLANE - s) if reverse else s
                As = pltpu.roll(A, sh, axis=0)
                Bs = pltpu.roll(B, sh, axis=0)
                valid = (row < SUBLANE - s) if reverse else (row >= s)
                B = jnp.where(valid, A * Bs + B, B)
                A = jnp.where(valid, A * As, A)
            hcur = A * hp + B
            outs[0][rows, :] = hcur
            if not reverse:
                outs[1][rows, :] = jnp.where(row == 0, hp, pltpu.roll(hcur, 1, axis=0))
                outs[2][rows, :] = _rec_gate_fn(hcur, gate)[0]
            edge = hcur[0:1, :] if reverse else hcur[SUBLANE - 1:SUBLANE, :]
            return jnp.broadcast_to(edge, (SUBLANE, cb)), a_first

        carry[...], carry_a[...] = lax.fori_loop(0, ng, step, (carry[...], carry_a[...]))

    nc = Wd // cb
    tok = (lambda i: nt - 1 - i) if reverse else (lambda i: i)
    spec = pl.BlockSpec((tt, cb), lambda c, i: (tok(i), c))
    gate_half = pl.BlockSpec((tt, cb), lambda c, i: (tok(i), nc + c))
    if reverse:
        args, out_specs = (a, proj, h, dhg), [spec, gate_half]
        out_shape = [jax.ShapeDtypeStruct((T, Wd), f32), jax.ShapeDtypeStruct((T, 2 * Wd), f32)]
    else:
        args, out_specs = (a, proj, b), [spec] * 3
        out_shape = [jax.ShapeDtypeStruct((T, Wd), f32)] * 3
    return pl.pallas_call(
        body, name=name, grid=(nc, nt), in_specs=[spec, gate_half] + [spec] * (len(args) - 2), out_specs=out_specs,
        out_shape=out_shape,
        scratch_shapes=[pltpu.VMEM((SUBLANE, cb), f32), pltpu.VMEM((SUBLANE, cb), f32)],
        compiler_params=_cparams("parallel", "arbitrary"),
    )(*args)


def _relu2_epilogue(r):
    h = jnp.maximum(r, 0.0)
    return r, h * h


def _drelu2_epilogue(r, a):
    return (r * (2.0 * jnp.maximum(a.astype(f32), 0.0)),)


def _residual_cot(through, upper):
    return (through + DN_ALPHA * upper,)


def _merge_cols(name, g, tm=256):
    _, L, R, s = g.shape

    def body(g_ref, o_ref):
        for d in range(N_DEV):
            o_ref[:, s * d:s * (d + 1)] = g_ref[d].astype(bf16)
        o_ref[:, N_DEV * s:] = jnp.zeros((tm, HYB_PROJ_PAD - N_DEV * s), bf16)

    return pl.pallas_call(
        body, name=name, grid=(L, R // tm),
        in_specs=[pl.BlockSpec((N_DEV, None, tm, s), lambda l, i: (0, l, i, 0))],
        out_specs=pl.BlockSpec((None, tm, HYB_PROJ_PAD), lambda l, i: (l, i, 0)),
        out_shape=jax.ShapeDtypeStruct((L, R, HYB_PROJ_PAD), bf16),
        compiler_params=_cparams("parallel", "parallel"),
    )(g)


def _split_cols(name, dw, tm=256):
    R = dw.shape[0]
    s = HYB_PROJ // N_DEV

    def body(g_ref, o_ref):
        for d in range(N_DEV):
            o_ref[d] = g_ref[:, s * d:s * (d + 1)].astype(bf16)

    return pl.pallas_call(
        body, name=name, grid=(R // tm,),
        in_specs=[pl.BlockSpec((tm, HYB_PROJ_PAD), lambda i: (i, 0))],
        out_specs=pl.BlockSpec((N_DEV, tm, s), lambda i: (0, i, 0)),
        out_shape=jax.ShapeDtypeStruct((N_DEV, R, s), bf16),
        compiler_params=_cparams("parallel"),
    )(dw)


def _rows_to_dev(dw):
    nb, r, c = dw.shape
    t = dw.reshape(nb, N_DEV, r // N_DEV, c)
    return jnp.moveaxis(t, 1, 0).reshape(N_DEV, nb * (r // N_DEV), c).astype(bf16)


def _ln_epilogue(r, x, g, b):
    y = _ln_res_fn(x, r, g, b)[0]
    return r, y, y


def _hybrid_fwd(tag, x, xb, W, j, cos, sin, ln):
    proj = _mm(f"{tag}_proj", xb, W["hyb_w_in"][j], "nn", b_kind="lead", b_lead=0)
    o_a = _attn_fwd(f"{tag}_attn", proj, cos, sin, W["hyb_sinks"][j][None, :])
    c = _conv_fwd(f"{tag}_conv", proj, CB_CONV, 12, W["hyb_conv_w"][j], None)
    o_ab, s_saved, t_saved = _delta_fwd(f"{tag}_delta", c, proj, W["hyb_a_log"][j].reshape(B_HEADS, 1, 1),
                                        W["hyb_dt_bias"][j].reshape(B_HEADS, 1, 1), W["hyb_norm_w"][j][None, :], o_a)
    mix, x1, x1b = _mm(f"{tag}_out", o_ab, W["hyb_w_out"][j], "nn", b_kind="lead", b_lead=0, epilogue=_ln_epilogue,
                       extras=(x,), params=ln, out_dtypes=(f32, f32, bf16), tm=512)
    return mix, x1, x1b, (proj, c, s_saved, t_saved, o_ab)


def _hybrid_bwd(tag, x, dmix, addend, W, j, cos, sin, saved, G, send_early):
    proj, c, s_saved, t_saved, o_ab = saved
    T = x.shape[0]
    d_oab = _mm(f"{tag}_dout", dmix, W["hyb_w_out"][j], "nt", b_kind="lead", b_lead=0)
    G["hyb_w_out"][j] = _mm(f"{tag}_dwout", o_ab, dmix, "tn", out_dtypes=(bf16,)).reshape(N_DEV, -1, D_MODEL)
    sinks = W["hyb_sinks"][j][None, :] + send_early({("hyb_w_out", j): G["hyb_w_out"][j]})
    dproj, dsinks = _attn_bwd(f"{tag}_dattn", proj, cos, sin, sinks, d_oab)
    a_log = W["hyb_a_log"][j].reshape(B_HEADS, 1, 1)
    dt_bias = W["hyb_dt_bias"][j].reshape(B_HEADS, 1, 1)
    dc, dproj, dal, ddt, dnw = _delta_bwd(f"{tag}_ddelta", c, proj, a_log, dt_bias, W["hyb_norm_w"][j][None, :],
                                          s_saved, t_saved, d_oab, dproj)
    dproj, dconv_w, _ = _conv_bwd(f"{tag}_dconv", dc, proj, CB_CONV, 12, W["hyb_conv_w"][j], dproj, CB_CONV)
    dx = _mm(f"{tag}_dx", dproj, W["hyb_w_in"][j], "nt", b_kind="lead", b_lead=0,
             **({} if addend is None else dict(epilogue=_residual_cot, extras=(addend,))))
    G["hyb_w_in"][j] = _split_cols(f"{tag}_dwin_split", _mm(f"{tag}_dwin", x, dproj, "tn", tn=1536))
    G["hyb_sinks"][j] = dsinks[0]
    G["hyb_conv_w"][j] = dconv_w
    G["hyb_a_log"][j] = dal.reshape(B_HEADS)
    G["hyb_dt_bias"][j] = ddt.reshape(B_HEADS)
    G["hyb_norm_w"][j] = dnw[0]
    return dx


def _rec_fwd(tag, x, xb, W, j, ln):
    Wd = D_MODEL
    proj = _mm(f"{tag}_proj", xb, W["rec_w_in"][j], "nn", b_kind="devcol", b_lead=0)
    xc = _conv_fwd(f"{tag}_conv", proj, 0, Wd // LANE, W["rec_conv_w"][j], W["rec_conv_b"][j][None, :])
    pars = [W["rec_b_a"][j][None, :], W["rec_b_x"][j][None, :], W["rec_lambda"][j][None, :]]
    a, b = _gates_fwd(f"{tag}_gates", xc, W["rec_w_a"][j][0], W["rec_w_x"][j][0], pars)
    h, h_prev, hg = _scan(f"{tag}_scan", a, proj, False, b=b)
    mix, x1, x1b = _mm(f"{tag}_out", hg, W["rec_w_out"][j], "nn", b_kind="lead", b_lead=0, epilogue=_ln_epilogue,
                       extras=(x,), params=ln, out_dtypes=(f32, f32, bf16), tm=512)
    return mix, x1, x1b, (proj, xc, a, h, h_prev, hg)


def _rec_bwd(tag, x, dmix, addend, W, j, saved, G, send_early):
    proj, xc, a, h, h_prev, hg = saved
    Wd = D_MODEL
    dhg = _mm(f"{tag}_dout", dmix, W["rec_w_out"][j], "nt", b_kind="lead", b_lead=0)
    G["rec_w_out"][j] = _mm(f"{tag}_dwout", hg, dmix, "tn", out_dtypes=(bf16,)).reshape(N_DEV, -1, D_MODEL)
    sent = send_early({("rec_w_out", j): G["rec_w_out"][j]})
    lam_t, dproj = _scan(f"{tag}_dscan", a, proj, True, h=h, dhg=dhg)
    pars = [W["rec_b_a"][j][None, :] + sent, W["rec_b_x"][j][None, :], W["rec_lambda"][j][None, :]]
    dxc, dpr, dpi, db_a, db_x, dlam = _gates_bwd(f"{tag}_dgates", xc, W["rec_w_a"][j][0], W["rec_w_x"][j][0], pars,
                                                 lam_t, h_prev)
    dwa, dwx = _blockdiag_bwd_dw(f"{tag}_dgates_dw", xc, dpr, dpi)
    G["rec_w_a"][j], G["rec_w_x"][j] = _rows_to_dev(dwa), _rows_to_dev(dwx)
    dproj, dconv_w, dconv_b = _conv_bwd(f"{tag}_dconv", dxc, proj, 0, Wd // LANE, W["rec_conv_w"][j], dproj, 0)
    dx = _mm(f"{tag}_dx", dproj, W["rec_w_in"][j], "nt", b_kind="devcol", b_lead=0,
             **({} if addend is None else dict(epilogue=_residual_cot, extras=(addend,))))
    G["rec_w_in"][j] = _mm(f"{tag}_dwin", x, dproj, "tn", o_kind="devcol", out_dtypes=(bf16,), tn=2048)
    G["rec_conv_w"][j] = dconv_w
    G["rec_conv_b"][j] = dconv_b
    G["rec_b_a"][j] = db_a[0]
    G["rec_b_x"][j] = db_x[0]
    G["rec_lambda"][j] = dlam[0]
    return dx


def _local_step(x, target, W, load_layer, grads_ready):
    T = x.shape[0]
    cos, sin = _rope_tables(T)
    saved = []
    xb = x
    for layer in range(DEPTH):
        j = layer // 2
        tag = f"L{layer}"
        load_layer(layer, "mixer", x)
        ln1 = (W["ln1_g"][layer][None, :], W["ln1_b"][layer][None, :])
        if layer % 2 == 0:
            mix, x1, x1b, sv = _hybrid_fwd(tag, x, xb, W, j, cos, sin, ln1)
        else:
            mix, x1, x1b, sv = _rec_fwd(tag, x, xb, W, j, ln1)
        load_layer(layer, "mlp", x1)
        a, h2 = _mm(f"{tag}_mlp1", x1b, W["mlp_w1"][layer], "nn", b_kind="devcol", b_lead=0, epilogue=_relu2_epilogue,
                    out_dtypes=(bf16, bf16), tm=2048)
        ln2 = (W["ln2_g"][layer][None, :], W["ln2_b"][layer][None, :])
        y, x2, x2b = _mm(f"{tag}_mlp2", h2, W["mlp_w2"][layer], "nn", b_kind="lead", b_lead=0, epilogue=_ln_epilogue,
                         extras=(x1,), params=ln2, out_dtypes=(f32, f32, bf16))
        saved.append((x, xb, sv, mix, x1, x1b, a, h2, y))
        x, xb = x2, x2b
    loss, dx = _loss_head(x, target)

    G = {k: [None] * (DEPTH if k.startswith(("ln", "mlp")) else DEPTH // 2) for k in (
        "hyb_w_in", "hyb_sinks", "hyb_conv_w", "hyb_a_log", "hyb_dt_bias", "hyb_norm_w", "hyb_w_out",
        "rec_w_in", "rec_conv_w", "rec_conv_b", "rec_w_a", "rec_b_a", "rec_w_x", "rec_b_x", "rec_lambda", "rec_w_out",
        "ln1_g", "ln1_b", "mlp_w1", "mlp_w2", "ln2_g", "ln2_b")}
    order = jnp.zeros((1, 1), f32)
    cot_rows, cot_fn = [(dx, 0, D_MODEL)], None
    for layer in reversed(range(DEPTH)):
        j = layer // 2
        tag = f"L{layer}"
        x0, x0b, sv, mix, x1, x1b, a, h2, y = saved[layer]
        ln2 = [W["ln2_g"][layer][None, :] + order, W["ln2_b"][layer][None, :]]
        (dy, dyb), (dg2, db2) = _tl_bwd(f"{tag}_dln2", _ln_res_fn, [(x1, 0, D_MODEL), (y, 0, D_MODEL)], ln2,
                                        cot_rows, cot_fn=cot_fn, skip=(0,), bf16_copy=True)
        G["ln2_g"][layer], G["ln2_b"][layer] = dg2[0], db2[0]
        da = _mm(f"{tag}_dmlp2", dyb, W["mlp_w2"][layer], "nt", b_kind="lead", b_lead=0, epilogue=_drelu2_epilogue,
                 extras=(a,), out_dtypes=(bf16,), tm=2048, tn=512)
        G["mlp_w2"][layer] = _mm(f"{tag}_dw2", h2, dyb, "tn", out_dtypes=(bf16,), tm=2048).reshape(N_DEV, -1, D_MODEL)
        dx1 = _mm(f"{tag}_dmlp1", da, W["mlp_w1"][layer], "nt", b_kind="devcol", b_lead=0, tm=2048)
        G["mlp_w1"][layer] = _mm(f"{tag}_dw1", x1b, da, "tn", o_kind="devcol", out_dtypes=(bf16,), tn=2048)
        ln1 = [W["ln1_g"][layer][None, :], W["ln1_b"][layer][None, :]]
        (dmix, dmixb), (dg1, db1) = _tl_bwd(f"{tag}_dln1", _ln_res_fn, [(x0, 0, D_MODEL), (mix, 0, D_MODEL)], ln1,
                                            [(dx1, 0, D_MODEL), (dy, 0, D_MODEL)], cot_fn=_residual_cot, skip=(0,),
                                            bf16_copy=True)
        G["ln1_g"][layer], G["ln1_b"][layer] = dg1[0], db1[0]
        dx0_a = dmix if layer == 0 else None
        early = functools.partial(grads_ready, f"l{layer}_early",
                                  {(k, layer): G[k][layer] for k in ("mlp_w1", "mlp_w2")})
        if layer % 2 == 0:
            dx = _hybrid_bwd(tag, x0b, dmixb, dx0_a, W, j, cos, sin, sv, G, early)
        else:
            dx = _rec_bwd(tag, x0b, dmixb, dx0_a, W, j, sv, G, early)
        order = grads_ready(f"l{layer}_late", {}, {(k, i): G[k][i] for k, i in _layer_weights(layer)[:-2]
                                                  if not k.endswith("w_out")})
        cot_rows, cot_fn = [(dx, 0, D_MODEL), (dmix, 0, D_MODEL)], _residual_cot
    big = {k for k, _ in BIG}
    return loss, dx, {k: jnp.stack(v) for k, v in G.items() if k not in big}


def _layer_weights(layer):
    j = layer // 2
    mixer = ["hyb_w_in", "hyb_w_out"] if layer % 2 == 0 else ["rec_w_in", "rec_w_out", "rec_w_a", "rec_w_x"]
    return [(k, j) for k in mixer] + [("mlp_w1", layer), ("mlp_w2", layer)]


def _my_coords():
    return lax.axis_index("x"), lax.axis_index("y"), lax.axis_index("c")


def _all_gather(name, arrays):
    na = len(arrays)

    def body(*refs):
        x_refs, out_refs = refs[:na], refs[na:2 * na]
        send_sems, recv_sems, local_sems = refs[2 * na:]
        x, y, c = _my_coords()
        me, sibling = (x, y, c), (x, y, 1 - c)
        chips = [(1 - x, y), (x, 1 - y), (1 - x, 1 - y)]

        def blk(a, px, py, pc):
            return out_refs[a].at[4 * px + 2 * py + pc]

        def copy(a, k, block, to, src=None):
            return pltpu.make_async_remote_copy(
                src_ref=blk(a, *block) if src is None else src, dst_ref=blk(a, *block),
                send_sem=send_sems.at[a, k], recv_sem=recv_sems.at[a, k],
                device_id=to, device_id_type=pl.DeviceIdType.MESH)

        mine = [pltpu.make_async_copy(x_refs[a], blk(a, *me), local_sems.at[a]) for a in range(na)]
        for cp in mine:
            cp.start()
        first = []
        for a in range(na):
            first.append(copy(a, 0, me, sibling, src=x_refs[a]))
            first += [copy(a, 1 + j, me, (*chip, c), src=x_refs[a]) for j, chip in enumerate(chips)]
        for cp in first:
            cp.start()
        passed = []
        for a in range(na):
            for j, chip in enumerate(chips):
                copy(a, 1 + j, (*chip, c), me).wait_recv()
                passed.append(copy(a, 4 + j, (*chip, c), sibling))
                passed[-1].start()
        for a in range(na):
            copy(a, 0, sibling, me).wait_recv()
            for j, chip in enumerate(chips):
                copy(a, 4 + j, (*chip, 1 - c), me).wait_recv()
        for cp in first + passed:
            cp.wait_send()
        for cp in mine:
            cp.wait()

    return pl.pallas_call(
        body, name=name,
        out_shape=[jax.ShapeDtypeStruct((N_DEV,) + a.shape, a.dtype) for a in arrays],
        in_specs=[pl.BlockSpec(memory_space=pl.ANY)] * na,
        out_specs=[pl.BlockSpec(memory_space=pl.ANY)] * na,
        scratch_shapes=[pltpu.SemaphoreType.DMA((na, 7)), pltpu.SemaphoreType.DMA((na, 7)),
                        pltpu.SemaphoreType.DMA((na,))],
    )(*arrays)


_HBM = pl.BlockSpec(memory_space=pltpu.HBM)
_SEM = pl.BlockSpec(memory_space=pltpu.SEMAPHORE)


def _flip(k, x, y, c):
    return ((1 - x) if k & 4 else x, (1 - y) if k & 2 else y, (1 - c) if k & 1 else c)


_PEERS = {"gather": (1, 2, 4, 6), "scatter": (1, 2, 3, 4, 5, 6, 7)}


def _push_copies(kind, x_refs, land_refs, send_sems, recv_sems, local_sems):
    x, y, c = _my_coords()
    me = 4 * x + 2 * y + c
    peers = _PEERS[kind]
    remote, local = [], []
    for a in range(len(x_refs)):
        local.append(pltpu.make_async_copy(x_refs[a] if kind == "gather" else x_refs[a].at[me], land_refs[a].at[me],
                                           local_sems.at[a]))
        for n, k in enumerate(peers):
            px, py, pc = _flip(k, x, y, c)
            remote.append(pltpu.make_async_remote_copy(
                src_ref=x_refs[a] if kind == "gather" else x_refs[a].at[4 * px + 2 * py + pc],
                dst_ref=land_refs[a].at[me],
                send_sem=send_sems.at[a * len(peers) + n], recv_sem=recv_sems.at[a * len(peers) + n],
                device_id=(px, py, pc), device_id_type=pl.DeviceIdType.MESH))
    return remote, local


def _pass_to_sibling(name, lands):
    na = len(lands)
    chips = (2, 4, 6)

    def body(*refs):
        out_refs, send_sems, recv_sems = refs[na:2 * na], refs[2 * na], refs[2 * na + 1]
        x, y, c = _my_coords()
        cps = []
        for a in range(na):
            for n, k in enumerate(chips):
                px, py, _ = _flip(k, x, y, c)
                cps.append(pltpu.make_async_remote_copy(
                    src_ref=out_refs[a].at[4 * px + 2 * py + c], dst_ref=out_refs[a].at[4 * px + 2 * py + c],
                    send_sem=send_sems.at[a * 3 + n], recv_sem=recv_sems.at[a * 3 + n],
                    device_id=(x, y, 1 - c), device_id_type=pl.DeviceIdType.MESH))
        for cp in cps:
            cp.start()
        for a in range(na):
            for n, k in enumerate(chips):
                px, py, _ = _flip(k, x, y, c)
                blk = out_refs[a].at[4 * px + 2 * py + (1 - c)]
                pltpu.make_async_remote_copy(src_ref=blk, dst_ref=blk, send_sem=send_sems.at[a * 3 + n],
                                             recv_sem=recv_sems.at[a * 3 + n], device_id=(x, y, 1 - c),
                                             device_id_type=pl.DeviceIdType.MESH).wait_recv()
        for cp in cps:
            cp.wait_send()

    return pl.pallas_call(
        body, name=name,
        out_shape=[jax.ShapeDtypeStruct(l.shape, l.dtype) for l in lands],
        in_specs=[pl.BlockSpec(memory_space=pl.ANY)] * na,
        out_specs=[pl.BlockSpec(memory_space=pl.ANY)] * na,
        input_output_aliases={a: a for a in range(na)},
        scratch_shapes=[pltpu.SemaphoreType.DMA((3 * na,)), pltpu.SemaphoreType.DMA((3 * na,))],
    )(*lands)


_SIDE_EFFECT = pltpu.CompilerParams(has_side_effects=pltpu.SideEffectType.DATAFLOW_SIDE_EFFECTING)


def _push_start(name, kind, srcs, lands):
    na = len(srcs)

    def body(*refs):
        remote, local = _push_copies(kind, refs[:na], refs[na:2 * na], *refs[2 * na:2 * na + 3])
        for cp in remote + local:
            cp.start()
        token = refs[-1]
        token[...] = jnp.zeros_like(token)

    arrays = list(srcs) + list(lands)
    n_remote = na * len(_PEERS[kind])
    res = pl.pallas_call(
        body, name=name,
        out_shape=(pltpu.SemaphoreType.DMA((n_remote,)), pltpu.SemaphoreType.DMA((n_remote,)),
                   pltpu.SemaphoreType.DMA((na,)), *[pltpu.HBM(t.shape, t.dtype) for t in arrays],
                   jax.ShapeDtypeStruct((SUBLANE, LANE), f32)),
        in_specs=[_HBM] * (2 * na),
        out_specs=(_SEM, _SEM, _SEM, *[_HBM] * (2 * na), pl.BlockSpec(memory_space=pltpu.VMEM)),
        input_output_aliases={i: 3 + i for i in range(2 * na)},
        compiler_params=_SIDE_EFFECT,
    )(*[pltpu.with_memory_space_constraint(t, pltpu.HBM) for t in arrays])
    return list(res[:3]), res[3:3 + na], res[3 + na:3 + 2 * na], res[-1][:1, :1]


def _push_wait(name, kind, sems, srcs, lands, after):
    na = len(srcs)

    def body(*refs):
        remote, local = _push_copies(kind, refs[:na], refs[na:2 * na], *refs[2 * na:2 * na + 3])
        for cp in remote:
            cp.wait_send()
            cp.wait_recv()
        for cp in local:
            cp.wait()

    arrays = list(srcs) + list(lands)
    res = pl.pallas_call(
        body, name=name,
        out_shape=tuple(pltpu.HBM(t.shape, t.dtype) for t in arrays),
        in_specs=[_HBM] * (2 * na) + [_SEM] * 3 + [pl.BlockSpec(memory_space=pl.ANY)],
        out_specs=tuple([_HBM] * (2 * na)),
        input_output_aliases={i: i for i in range(2 * na)},
        compiler_params=_SIDE_EFFECT,
    )(*arrays, *sems, after)
    return res[na:]


def _sum_blocks(name, land):
    _, R, n = land.shape
    tr = R

    def body(l_ref, o_ref):
        acc = l_ref[0].astype(f32)
        for s in range(1, N_DEV):
            acc = acc + l_ref[s].astype(f32)
        o_ref[...] = acc

    return pl.pallas_call(
        body, name=name, grid=(R // tr,),
        in_specs=[pl.BlockSpec((N_DEV, tr, n), lambda i: (0, i, 0))],
        out_specs=pl.BlockSpec((tr, n), lambda i: (i, 0)),
        out_shape=jax.ShapeDtypeStruct((R, n), f32),
        compiler_params=_cparams("parallel"),
    )(land)


def _adamw(name, w, g, m, v):
    shape = w.shape
    last = shape[-1]
    rows = math.prod(shape[:-1])
    tm = 256 if rows % 256 == 0 and rows > 256 else rows
    w2, g2, m2, v2 = (t.reshape(rows, last) for t in (w, g, m, v))

    def body(w_ref, g_ref, m_ref, v_ref, d_ref, mo_ref, vo_ref):
        gg = g_ref[...]
        mn = ADAM_B1 * m_ref[...] + (1.0 - ADAM_B1) * gg
        vn = ADAM_B2 * v_ref[...] + (1.0 - ADAM_B2) * jnp.square(gg)
        m_hat = mn / (1.0 - ADAM_B1 ** ADAM_STEP)
        v_hat = vn / (1.0 - ADAM_B2 ** ADAM_STEP)
        d_ref[...] = -ADAM_LR * (m_hat / (jnp.sqrt(v_hat) + ADAM_EPS) + ADAM_WD * w_ref[...])
        mo_ref[...] = mn
        vo_ref[...] = vn

    spec = pl.BlockSpec((tm, last), lambda i: (i, 0))
    d, mn, vn = pl.pallas_call(
        body, name=name, grid=(rows // tm,), in_specs=[spec] * 4, out_specs=[spec] * 3,
        out_shape=[jax.ShapeDtypeStruct((rows, last), f32)] * 3,
        compiler_params=_cparams("parallel"),
    )(w2, g2, m2, v2)
    return d.reshape(shape), mn.reshape(shape), vn.reshape(shape)


def _adamw_land(name, lands, w, m, v, tm=256):
    L = len(lands)
    _, R, C = lands[0].shape
    tm = min(tm, R)

    def body(*refs):
        l_refs, (w_ref, m_ref, v_ref, g_ref, d_ref, mo_ref, vo_ref) = refs[:L], refs[L:]
        for k in range(L):
            @pl.when(pl.program_id(0) == k)
            def _(k=k):
                gg = l_refs[k][0].astype(f32)
                for s in range(1, N_DEV):
                    gg = gg + l_refs[k][s].astype(f32)
                g_ref[...] = gg
                mn = ADAM_B1 * m_ref[...] + (1.0 - ADAM_B1) * gg
                vn = ADAM_B2 * v_ref[...] + (1.0 - ADAM_B2) * jnp.square(gg)
                m_hat = mn / (1.0 - ADAM_B1 ** ADAM_STEP)
                v_hat = vn / (1.0 - ADAM_B2 ** ADAM_STEP)
                d_ref[...] = -ADAM_LR * (m_hat / (jnp.sqrt(v_hat) + ADAM_EPS) + ADAM_WD * w_ref[...])
                mo_ref[...] = mn
                vo_ref[...] = vn

    land_specs = [pl.BlockSpec((N_DEV, tm, C), lambda l, i, k=k: (0, jnp.where(l == k, i, 0), 0)) for k in range(L)]
    spec = pl.BlockSpec((None, tm, C), lambda l, i: (l, i, 0))
    return pl.pallas_call(
        body, name=name, grid=(L, R // tm),
        in_specs=land_specs + [spec] * 3,
        out_specs=[spec] * 4,
        out_shape=[jax.ShapeDtypeStruct((L, R, C), f32)] * 4,
        compiler_params=_cparams("arbitrary", "arbitrary"),
    )(*lands, w, m, v)


BIG = [("hyb_w_in", 2), ("hyb_w_out", 1), ("rec_w_in", 2), ("rec_w_out", 1), ("rec_w_a", 2), ("rec_w_x", 2),
       ("mlp_w1", 2), ("mlp_w2", 1)]
SMALL = [("hyb_conv_w", 2), ("rec_conv_w", 2), ("rec_conv_b", 1), ("rec_b_a", 1), ("rec_b_x", 1), ("rec_lambda", 1)]
REPL = ["hyb_sinks", "hyb_a_log", "hyb_dt_bias", "hyb_norm_w", "ln1_g", "ln1_b", "ln2_g", "ln2_b"]
WEIGHTS = ["hyb_w_in", "hyb_sinks", "hyb_conv_w", "hyb_a_log", "hyb_dt_bias", "hyb_norm_w", "hyb_w_out", "rec_w_in",
           "rec_conv_w", "rec_conv_b", "rec_w_a", "rec_b_a", "rec_w_x", "rec_b_x", "rec_lambda", "rec_w_out",
           "ln1_g", "ln1_b", "mlp_w1", "mlp_w2", "ln2_g", "ln2_b"]


def _pack_rows(parts, dtype, row_mult):
    lead = parts[0].shape[:-1]
    flat = jnp.concatenate([p.astype(dtype) for p in parts], axis=-1)
    n = flat.shape[-1]
    unit = row_mult * LANE
    pad = (-n) % unit
    if pad:
        flat = jnp.concatenate([flat, jnp.zeros(lead + (pad,), dtype)], axis=-1)
    return flat.reshape(lead + ((n + pad) // LANE, LANE))


def _gather_full(gathered, shard_shapes, table):
    flat = gathered.reshape(N_DEV, -1)
    out, off = {}, 0
    for name, ax in table:
        shp = shard_shapes[name]
        n = math.prod(shp)
        arr = flat[:, off:off + n].reshape((N_DEV,) + shp)
        off += n
        arr = jnp.moveaxis(arr, 0, ax)
        out[name] = arr.reshape(shp[:ax] + (N_DEV * shp[ax],) + shp[ax + 1:])
    return out


def _matmul_layouts(tag, gw):
    out = {}
    bw = D_MODEL // LRU_BLOCKS
    for k, g in gw.items():
        L = g.shape[1]
        if k == "hyb_w_in":
            out[k] = _merge_cols(f"{tag}_w_in_merge", g)
        elif k in ("hyb_w_out", "rec_w_out", "mlp_w2"):
            out[k] = jnp.swapaxes(g, 0, 1).reshape(L, N_DEV * g.shape[2], g.shape[3])
        elif k in ("rec_w_a", "rec_w_x"):
            out[k] = jnp.moveaxis(g, 0, 2).reshape(L, LRU_BLOCKS, bw, bw)
        else:
            out[k] = g
    return out


def kernel(x, hyb_w_in, hyb_sinks, hyb_conv_w, hyb_a_log, hyb_dt_bias, hyb_norm_w, hyb_w_out, rec_w_in, rec_conv_w, rec_conv_b, rec_w_a, rec_b_a, rec_w_x, rec_b_x, rec_lambda, rec_w_out, ln1_g, ln1_b, mlp_w1, mlp_w2, ln2_g, ln2_b, loss_target, m_hyb_w_in, m_hyb_sinks, m_hyb_conv_w, m_hyb_a_log, m_hyb_dt_bias, m_hyb_norm_w, m_hyb_w_out, m_rec_w_in, m_rec_conv_w, m_rec_conv_b, m_rec_w_a, m_rec_b_a, m_rec_w_x, m_rec_b_x, m_rec_lambda, m_rec_w_out, m_ln1_g, m_ln1_b, m_mlp_w1, m_mlp_w2, m_ln2_g, m_ln2_b, v_hyb_w_in, v_hyb_sinks, v_hyb_conv_w, v_hyb_a_log, v_hyb_dt_bias, v_hyb_norm_w, v_hyb_w_out, v_rec_w_in, v_rec_conv_w, v_rec_conv_b, v_rec_w_a, v_rec_b_a, v_rec_w_x, v_rec_b_x, v_rec_lambda, v_rec_w_out, v_ln1_g, v_ln1_b, v_mlp_w1, v_mlp_w2, v_ln2_g, v_ln2_b):
    args = locals()
    w = {k: args[k] for k in WEIGHTS}
    m = {k: args["m_" + k] for k in WEIGHTS}
    v = {k: args["v_" + k] for k in WEIGHTS}
    shard_shapes = {k: tuple(t.shape) for k, t in w.items()}
    xi, yi, ci = _my_coords()
    me = 4 * xi + 2 * yi + ci

    in_flight = {}

    def install(tag, names, got):
        for (k, i), arr in zip(names, _matmul_layouts(tag, {k: g for (k, _), g in zip(names, got)}).values()):
            W[k][i] = arr

    def start_gather(tag, names):
        srcs = [w[k][i:i + 1].astype(bf16) for k, i in names]
        *pending, zero = _push_start(f"gather_{tag}_start", "gather", srcs,
                                     [lax.empty((N_DEV,) + s.shape, bf16) for s in srcs])
        in_flight[tag] = (names, pending)
        return zero

    def finish_gather(tag, after):
        names, pending = in_flight.pop(tag)
        half = _push_wait(f"gather_{tag}_wait", "gather", *pending, after)
        install(tag, names, _pass_to_sibling(f"gather_{tag}_pass", half))

    def started(k, zero):
        W[k] = W[k] + zero

    def mixer_w(layer):
        return _layer_weights(layer)[:-2]

    def mlp_w(layer):
        return _layer_weights(layer)[-2:]

    gathered0 = _all_gather("gather_first", [w[k][i:i + 1].astype(bf16) for k, i in mixer_w(0)]
                            + [_pack_rows([w[k].reshape(-1) for k, _ in SMALL], f32, SUBLANE)])
    W = _gather_full(gathered0[-1], shard_shapes, SMALL)
    W.update({k: w[k] for k in REPL})
    W.update({k: {} for k, _ in BIG})
    install("l0a", mixer_w(0), gathered0[:-1])
    started("hyb_sinks", start_gather("l0b", mlp_w(0)) + start_gather("l1a", mixer_w(1)))

    def load_layer(layer, part, after):
        if part == "mixer":
            if layer > 0:
                finish_gather(f"l{layer}a", after)
            if 0 < layer < DEPTH - 1:
                started("hyb_sinks" if layer % 2 == 0 else "rec_conv_b",
                        start_gather(f"l{layer + 1}a", mixer_w(layer + 1)))
        else:
            finish_gather(f"l{layer}b", after)
            if layer < DEPTH - 1:
                started("ln2_g", start_gather(f"l{layer + 1}b", mlp_w(layer + 1)))

    grads_in_flight = {}

    def grads_ready(tag, a, b):
        g = {**a, **b}
        srcs = list(g.values())
        *pending, zero = _push_start(f"scatter_{tag}_start", "scatter", srcs, [lax.empty(s.shape, bf16) for s in srcs])
        grads_in_flight[tag] = (list(g.keys()), pending)
        return zero

    loss_local, grad_x, G = _local_step(x[0], loss_target[0], W, load_layer, grads_ready)
    loss = lax.psum(loss_local, MESH_AXES)

    landed = {}

    def land(tag, after):
        keys, pending = grads_in_flight[tag]
        landed.update(zip(keys, _push_wait(f"scatter_{tag}_wait", "scatter", *pending, after)))

    tags = list(grads_in_flight)
    for tag in tags[:-1]:
        land(tag, grad_x)
    rest = _pack_rows([G[k].reshape(-1) for k, _ in SMALL] + [G[k].reshape(-1) for k in REPL], f32, SUBLANE)
    g_rest = _sum_blocks("sum_rest", _all_gather("gather_rest", [rest])[0]).reshape(-1)

    grads, delta, new_m, new_v = {}, {}, {}, {}

    def adamw_big(k):
        shp = shard_shapes[k]
        s3 = (shp[0], math.prod(shp[1:-1]), shp[-1])
        lands = [landed[(k, i)].reshape((N_DEV,) + s3[1:]) for i in range(shp[0])]
        res = _adamw_land("adamw_" + k, lands, w[k].reshape(s3), m[k].reshape(s3), v[k].reshape(s3))
        grads[k], delta[k], new_m[k], new_v[k] = (r.reshape(shp) for r in res)

    late = {k for k, _ in grads_in_flight[tags[-1]][0]}
    for k in [k for k, _ in BIG if k not in late]:
        adamw_big(k)
        done = new_v[k]
    land(tags[-1], done)
    for k in [k for k, _ in BIG if k in late]:
        adamw_big(k)
    off = 0
    for k, ax in SMALL:
        full_shape = G[k].shape
        n = math.prod(full_shape)
        full = g_rest[off:off + n].reshape(full_shape)
        off += n
        s = shard_shapes[k][ax]
        grads[k] = lax.dynamic_slice_in_dim(full, me * s, s, axis=ax)
    for k in REPL:
        n = math.prod(shard_shapes[k])
        grads[k] = g_rest[off:off + n].reshape(shard_shapes[k])
        off += n

    for k in [k for k, _ in SMALL] + REPL:
        delta[k], new_m[k], new_v[k] = _adamw("adamw_" + k, w[k], grads[k], m[k], v[k])

    return (loss, grad_x[None], *[grads[k] for k in WEIGHTS], *[delta[k] for k in WEIGHTS],
            *[new_m[k] for k in WEIGHTS], *[new_v[k] for k in WEIGHTS])
```

```python
import functools
import math

import jax
import jax.numpy as jnp
from jax import lax
from jax.experimental import pallas as pl
from jax.experimental.pallas import tpu as pltpu

f32 = jnp.float32
bf16 = jnp.bfloat16

N_DEV = 8
D_MODEL = 1024
DEPTH = 4
A_HEAD_DIM = 64
A_Q_HEADS = 8
WINDOW = 128
ROPE_THETA = 10000.0
B_HEADS = 4
B_HEAD_DIM = 128
B_CHUNK = 64
LRU_BLOCKS = 4
LRU_C = 8.0
D_FF = 4 * D_MODEL
HYB_PROJ = 2824
HYB_PROJ_PAD = 3072
DN_ALPHA = (2 * DEPTH) ** 0.25
LN_EPS = 1e-5
NORM_EPS = 1e-6
ADAM_LR = 0.001
ADAM_B1 = 0.9
ADAM_B2 = 0.999
ADAM_EPS = 1e-08
ADAM_WD = 0.01
ADAM_STEP = 10

LANE = 128
SUBLANE = 8
VMEM_LIMIT = 48 * 1024 * 1024

CB_QA, CB_KA, CB_VA, CB_CONV, CB_Z, CB_LG = 0, 4, 5, 6, 18, 22

MESH_AXES = ("x", "y", "c")


def _cparams(*sem):
    return pltpu.CompilerParams(dimension_semantics=sem, vmem_limit_bytes=VMEM_LIMIT)


def _dot(a, b, dims, precision=None):
    return lax.dot_general(a, b, (dims, ((), ())), preferred_element_type=f32, precision=precision)


NN = ((1,), (0,))
NT = ((1,), (1,))
TN = ((0,), (0,))


def _mat_spec(arr, kind, lead, br, bc, rb, cb):
    if kind == "plain":
        return pl.BlockSpec((br, bc), lambda i, j, k: (rb(i, j, k), cb(i, j, k)))
    if kind == "lead":
        return pl.BlockSpec((None, br, bc), lambda i, j, k: (lead, rb(i, j, k), cb(i, j, k)))
    assert kind == "devcol" and bc == arr.shape[-1]
    return pl.BlockSpec((None, None, br, bc), lambda i, j, k: (cb(i, j, k), lead, rb(i, j, k), 0))


def _mm(name, a, b, mode, *, b_kind="plain", b_lead=0, o_kind="plain", epilogue=None, extras=(), params=(),
        out_dtypes=(f32,), tm=1024, tn=1024, tk=None):
    if tk is None:
        tk = 512 if mode == "tn" else 1024
    if b_kind in ("plain", "lead"):
        b_rows, b_cols = b.shape[-2:]
    else:
        b_rows, b_cols = b.shape[-2], N_DEV * b.shape[-1]
    if mode == "nn":
        (M, K), (K2, N) = a.shape, (b_rows, b_cols)
    elif mode == "nt":
        (M, K), (N, K2) = a.shape, (b_rows, b_cols)
    else:
        (K, M), (K2, N) = a.shape, (b_rows, b_cols)
    assert K == K2, (name, a.shape, b.shape, mode)
    tm, tn, tk = min(tm, M), min(tn, N), min(tk, K)
    cols_are_n = mode != "nt"
    if b_kind == "devcol":
        tn, tk = (b.shape[-1], tk) if cols_are_n else (tn, b.shape[-1])
    shard = N // N_DEV
    if o_kind == "devcol":
        tn = max(shard, tn // shard * shard)
    assert M % tm == 0 and N % tn == 0 and K % tk == 0, (name, M, N, K, tm, tn, tk)
    nk = K // tk
    dims = {"nn": NN, "nt": NT, "tn": TN}[mode]
    n_ex, n_out = len(extras) + len(params), len(out_dtypes)

    def body(*refs):
        a_ref, b_ref = refs[:2]
        ex = refs[2:2 + n_ex]
        outs = refs[2 + n_ex:2 + n_ex + n_out]
        acc = refs[-1]
        k = pl.program_id(2)

        @pl.when(k == 0)
        def _():
            acc[...] = jnp.zeros_like(acc)

        acc[...] += _dot(a_ref[...].astype(bf16), b_ref[...].astype(bf16), dims)

        @pl.when(k == nk - 1)
        def _():
            r = acc[...]
            res = epilogue(r, *[e[...] for e in ex]) if epilogue is not None else (r,)
            for o, v in zip(outs, res):
                if o_kind == "plain":
                    o[...] = v.astype(o.dtype)
                else:
                    for q in range(tn // shard):
                        o[q] = v[:, q * shard:(q + 1) * shard].astype(o.dtype)

    if mode == "tn":
        a_spec = pl.BlockSpec((tk, tm), lambda i, j, k: (k, i))
    else:
        a_spec = pl.BlockSpec((tm, tk), lambda i, j, k: (i, k))
    jb, kb = (lambda i, j, k: j), (lambda i, j, k: k)
    if mode == "nt":
        b_spec = _mat_spec(b, b_kind, b_lead, tn, tk, jb, kb)
    else:
        b_spec = _mat_spec(b, b_kind, b_lead, tk, tn, kb, jb)
    e_spec = pl.BlockSpec((tm, tn), lambda i, j, k: (i, j))
    if o_kind == "plain":
        o_spec, o_shape = e_spec, (M, N)
    else:
        o_spec, o_shape = pl.BlockSpec((tn // shard, tm, shard), lambda i, j, k: (j, i, 0)), (N_DEV, M, shard)
    res = pl.pallas_call(
        body, name=name,
        grid=(M // tm, N // tn, nk),
        in_specs=[a_spec, b_spec] + [e_spec] * len(extras)
        + [pl.BlockSpec(p.shape, lambda i, j, k: (0, 0)) for p in params],
        out_specs=[o_spec] * n_out,
        out_shape=[jax.ShapeDtypeStruct(o_shape, dt) for dt in out_dtypes],
        scratch_shapes=[pltpu.VMEM((tm, tn), f32)],
        compiler_params=_cparams("parallel", "parallel", "arbitrary"),
    )(a, b, *extras, *params)
    return res[0] if n_out == 1 else res


def _row_spec(tm, cb, width):
    assert (cb * LANE) % width == 0
    blk = (cb * LANE) // width
    return pl.BlockSpec((tm, width), lambda i: (i, blk))


def _whole_spec(p):
    nd = p.ndim
    return pl.BlockSpec(p.shape, lambda i: (0,) * nd)


def _tl_bwd(name, fn, rows, params, cot_rows, cot_fn=None, skip=(), bf16_copy=False, tm=512):
    T = rows[0][0].shape[0]
    tm = min(tm, T)
    nr, npar, nc = len(rows), len(params), len(cot_rows)
    keep = [k for k in range(nr) if k not in skip]
    n_rows = len(keep) + int(bf16_copy)
    row_dtypes = [(rows[k][2], f32) for k in keep] + ([(rows[keep[0]][2], bf16)] if bf16_copy else [])

    def body(*refs):
        vals = [r[...] for r in refs[:nr + npar]]
        cots = [r[...] for r in refs[nr + npar:nr + npar + nc]]
        outs = refs[nr + npar + nc:]
        cot = tuple(cot_fn(*cots)) if cot_fn is not None else tuple(cots)
        _, vjp = jax.vjp(fn, *vals)
        grads = vjp(cot)
        for o, k in zip(outs, keep):
            o[...] = grads[k].astype(o.dtype)
        if bf16_copy:
            outs[len(keep)][...] = grads[keep[0]].astype(bf16)
        i = pl.program_id(0)
        for o, g in zip(outs[n_rows:], grads[nr:]):
            @pl.when(i == 0)
            def _(o=o):
                o[...] = jnp.zeros_like(o)
            o[...] += g

    res = pl.pallas_call(
        body, name=name, grid=(T // tm,),
        in_specs=[_row_spec(tm, cb, w) for (_, cb, w) in rows] + [_whole_spec(p) for p in params]
        + [_row_spec(tm, cb, w) for (_, cb, w) in cot_rows],
        out_specs=[pl.BlockSpec((tm, w), lambda i: (i, 0)) for w, _ in row_dtypes] + [_whole_spec(p) for p in params],
        out_shape=[jax.ShapeDtypeStruct((T, w), dt) for w, dt in row_dtypes]
        + [jax.ShapeDtypeStruct(p.shape, f32) for p in params],
        compiler_params=_cparams("arbitrary"),
    )(*[r[0] for r in rows], *params, *[r[0] for r in cot_rows])
    return res[:n_rows], res[n_rows:]


def _ln_res_fn(x, mix, g, b):
    pre = DN_ALPHA * x + mix
    mu = jnp.mean(pre, axis=-1, keepdims=True)
    var = jnp.mean(jnp.square(pre - mu), axis=-1, keepdims=True)
    return ((pre - mu) * lax.rsqrt(var + LN_EPS) * g + b,)


@jax.custom_jvp
def _expm1(x):
    small = jnp.abs(x) < 0.3
    xs = jnp.where(small, x, 0.0)
    poly = xs * (1.0 + xs * (1 / 2 + xs * (1 / 6 + xs * (1 / 24 + xs * (1 / 120 + xs * (
        1 / 720 + xs * (1 / 5040 + xs * (1 / 40320 + xs * (1 / 362880)))))))))
    return jnp.where(small, poly, jnp.exp(x) - 1.0)


@_expm1.defjvp
def _expm1_jvp(primals, tangents):
    (x,), (t,) = primals, tangents
    return _expm1(x), t * jnp.exp(x)


def _rglru_pre_fn(pre_r, pre_i, xc, b_a, b_x, lam):
    r = jax.nn.sigmoid(pre_r + b_a)
    i = jax.nn.sigmoid(pre_i + b_x)
    log_a = -LRU_C * r * jax.nn.softplus(-lam)
    a = jnp.exp(log_a)
    b = jnp.sqrt(-_expm1(2.0 * log_a)) * (i * xc)
    return a, b


def _rec_gate_fn(h, gate):
    return (h * jax.nn.gelu(gate),)


def _loss_head(y, t, tm=512):
    T, Dm = y.shape
    tm = min(tm, T)

    def body(y_ref, t_ref, dy_ref, loss_ref):
        e = y_ref[...] - t_ref[...]
        dy_ref[...] = e * (1.0 / Dm)

        @pl.when(pl.program_id(0) == 0)
        def _():
            loss_ref[...] = jnp.zeros_like(loss_ref)

        loss_ref[...] += 0.5 * jnp.sum(jnp.mean(e * e, axis=-1, keepdims=True), axis=0, keepdims=True)

    dy, loss = pl.pallas_call(
        body, name="loss_head", grid=(T // tm,),
        in_specs=[pl.BlockSpec((tm, Dm), lambda i: (i, 0))] * 2,
        out_specs=[pl.BlockSpec((tm, Dm), lambda i: (i, 0)), pl.BlockSpec((SUBLANE, LANE), lambda i: (0, 0))],
        out_shape=[jax.ShapeDtypeStruct((T, Dm), f32), jax.ShapeDtypeStruct((SUBLANE, LANE), f32)],
        compiler_params=_cparams("arbitrary"),
    )(y, t)
    return loss[0, 0], dy


def _conv_fwd(name, x, cb0, nblk, w, bias, tm=2048):
    T = x.shape[0]
    tm = min(tm, T)
    hb = tm // SUBLANE
    has_b = bias is not None

    def body(*refs):
        cur, prev, w_ref = refs[:3]
        b_ref = refs[3] if has_b else None
        o = refs[-1]
        i = pl.program_id(1)
        p = jnp.where(i > 0, prev[...], 0.0)
        xcat = jnp.concatenate([p, cur[...]], axis=0)
        acc = cur[...] * w_ref[3:4, :]
        for j in range(3):
            acc = acc + pltpu.roll(xcat, 3 - j, axis=0)[SUBLANE:] * w_ref[j:j + 1, :]
        if has_b:
            acc = acc + b_ref[...]
        o[...] = acc

    in_specs = [
        pl.BlockSpec((tm, LANE), lambda c, i: (i, cb0 + c)),
        pl.BlockSpec((SUBLANE, LANE), lambda c, i: (jnp.maximum(i * hb - 1, 0), cb0 + c)),
        pl.BlockSpec((4, LANE), lambda c, i: (0, c)),
    ]
    args = [x, x, w]
    if has_b:
        in_specs.append(pl.BlockSpec((1, LANE), lambda c, i: (0, c)))
        args.append(bias)
    return pl.pallas_call(
        body, name=name, grid=(nblk, T // tm),
        in_specs=in_specs,
        out_specs=pl.BlockSpec((tm, LANE), lambda c, i: (i, c)),
        out_shape=jax.ShapeDtypeStruct((T, nblk * LANE), f32),
        compiler_params=_cparams("parallel", "parallel"),
    )(*args)


def _conv_bwd(name, dy, x, cb0, nblk, w, into, into_cb, tm=2048):
    T = x.shape[0]
    tm = min(tm, T)
    hb = tm // SUBLANE
    nt = T // tm

    def body(dcur, dnext, xcur, xprev, w_ref, _, dx_ref, dw_ref, db_ref):
        i = pl.program_id(1)
        d = dcur[...]
        dn = jnp.where(i < nt - 1, dnext[...], 0.0)
        dcat = jnp.concatenate([d, dn], axis=0)
        acc = d * w_ref[3:4, :]
        for j in range(3):
            s = 3 - j
            acc = acc + pltpu.roll(dcat, tm + SUBLANE - s, axis=0)[:tm] * w_ref[j:j + 1, :]
        dx_ref[...] = acc.astype(dx_ref.dtype)

        p = jnp.where(i > 0, xprev[...], 0.0)
        xcat = jnp.concatenate([p, xcur[...]], axis=0)
        rows = [jnp.sum(d * pltpu.roll(xcat, 3 - j, axis=0)[SUBLANE:], axis=0, keepdims=True) for j in range(3)]
        rows.append(jnp.sum(d * xcur[...], axis=0, keepdims=True))
        rows.append(jnp.zeros((SUBLANE - 4, LANE), f32))

        @pl.when(i == 0)
        def _():
            dw_ref[...] = jnp.zeros_like(dw_ref)
            db_ref[...] = jnp.zeros_like(db_ref)

        dw_ref[...] += jnp.concatenate(rows, axis=0)
        db_ref[...] += jnp.broadcast_to(jnp.sum(d, axis=0, keepdims=True), (SUBLANE, LANE))

    nh = T // SUBLANE
    dx, dw, db = pl.pallas_call(
        body, name=name, grid=(nblk, nt),
        in_specs=[
            pl.BlockSpec((tm, LANE), lambda c, i: (i, c)),
            pl.BlockSpec((SUBLANE, LANE), lambda c, i: (jnp.minimum((i + 1) * hb, nh - 1), c)),
            pl.BlockSpec((tm, LANE), lambda c, i: (i, cb0 + c)),
            pl.BlockSpec((SUBLANE, LANE), lambda c, i: (jnp.maximum(i * hb - 1, 0), cb0 + c)),
            pl.BlockSpec((4, LANE), lambda c, i: (0, c)),
            pl.BlockSpec(memory_space=pl.ANY),
        ],
        out_specs=[
            pl.BlockSpec((tm, LANE), lambda c, i: (i, into_cb + c)),
            pl.BlockSpec((SUBLANE, LANE), lambda c, i: (0, c)),
            pl.BlockSpec((SUBLANE, LANE), lambda c, i: (0, c)),
        ],
        out_shape=[jax.ShapeDtypeStruct(into.shape, into.dtype),
                   jax.ShapeDtypeStruct((SUBLANE, nblk * LANE), f32),
                   jax.ShapeDtypeStruct((SUBLANE, nblk * LANE), f32)],
        input_output_aliases={5: 0},
        compiler_params=_cparams("parallel", "arbitrary"),
    )(dy, dy, x, x, w, into)
    return dx, dw[:4], db[0]


@functools.partial(jax.custom_vjp, nondiff_argnums=(1,))
def _lroll(x, s):
    return pltpu.roll(x, s, axis=1)


def _lroll_fwd(x, s):
    return _lroll(x, s), None


def _lroll_bwd(s, _, g):
    return (_lroll(g, (LANE - s) % LANE),)


_lroll.defvjp(_lroll_fwd, _lroll_bwd)


def _rope_tables(T):
    half = A_HEAD_DIM // 2
    inv_freq = ROPE_THETA ** (-jnp.arange(half, dtype=f32) / half)
    ang = jnp.arange(T, dtype=f32)[:, None] * inv_freq[None, :]
    cos, sin = jnp.cos(ang), jnp.sin(ang)
    return jnp.tile(jnp.concatenate([cos, cos], axis=1), (1, 2)), jnp.tile(jnp.concatenate([-sin, sin], axis=1), (1, 2))


def _attn_block_fn(n, q, kp, kc, vp, vc, cq, sq, cp, sp, sinks):
    W = WINDOW
    lane = lax.broadcasted_iota(jnp.int32, (W, LANE), 1)
    lo_half = (lane % A_HEAD_DIM) < (A_HEAD_DIM // 2)
    lane8 = lax.broadcasted_iota(jnp.int32, sinks.shape, 1)

    def rope(x, c, s):
        return x * c + jnp.where(lo_half, _lroll(x, LANE - A_HEAD_DIM // 2), _lroll(x, A_HEAD_DIM // 2)) * s

    k2 = jnp.concatenate([rope(kp, cp, sp), rope(kc, cq, sq)], axis=0).astype(bf16)
    v2 = jnp.concatenate([vp, vc], axis=0).astype(bf16)
    qs = []
    for t in range(4):
        qt = rope(q[:, LANE * t:LANE * (t + 1)], cq, sq)
        g = t // 2
        for hh in range(2):
            qa = jnp.where((lane // A_HEAD_DIM) == hh, qt, 0.0)
            qs.append(_lroll(qa, A_HEAD_DIM) if hh != g else qa)
    s_all = _dot(jnp.concatenate(qs, axis=0).astype(bf16), k2, NT) * (A_HEAD_DIM ** -0.5)
    row = lax.broadcasted_iota(jnp.int32, (W, 2 * W), 0)
    col = lax.broadcasted_iota(jnp.int32, (W, 2 * W), 1)
    dist = row + W - col
    mask = (dist >= 0) & (dist < W) & ((col >= W) | (n > 0))
    ps = []
    for j in range(A_Q_HEADS):
        s = jnp.where(mask, s_all[W * j:W * (j + 1)], -jnp.inf)
        sink = jnp.sum(jnp.where(lane8 == j, sinks, 0.0), axis=1, keepdims=True)
        m = jnp.maximum(jnp.max(s, axis=-1, keepdims=True), sink)
        e = jnp.exp(s - m)
        ps.append((e / (jnp.sum(e, axis=-1, keepdims=True) + jnp.exp(sink - m))).astype(bf16))
    o = _dot(jnp.concatenate(ps, axis=0), v2, NN)
    outs = []
    for t in range(4):
        g = t // 2
        ot = jnp.zeros((W, LANE), f32)
        for hh in range(2):
            j = 2 * t + hh
            oj = jnp.where((lane // A_HEAD_DIM) == g, o[W * j:W * (j + 1)], 0.0)
            ot = ot + (_lroll(oj, A_HEAD_DIM) if hh != g else oj)
        outs.append(ot)
    return jnp.concatenate(outs, axis=1)


def _attn_specs():
    W = WINDOW
    prev = lambda n: jnp.maximum(n - 1, 0)
    return [
        pl.BlockSpec((W, 4 * LANE), lambda n: (n, CB_QA // 4)),
        pl.BlockSpec((W, LANE), lambda n: (prev(n), CB_KA)),
        pl.BlockSpec((W, LANE), lambda n: (n, CB_KA)),
        pl.BlockSpec((W, LANE), lambda n: (prev(n), CB_VA)),
        pl.BlockSpec((W, LANE), lambda n: (n, CB_VA)),
        pl.BlockSpec((W, LANE), lambda n: (n, 0)),
        pl.BlockSpec((W, LANE), lambda n: (n, 0)),
        pl.BlockSpec((W, LANE), lambda n: (prev(n), 0)),
        pl.BlockSpec((W, LANE), lambda n: (prev(n), 0)),
        pl.BlockSpec((1, A_Q_HEADS), lambda n: (0, 0)),
    ]


def _attn_fwd(name, proj, cos, sin, sinks):
    T = proj.shape[0]
    W = WINDOW

    def body(*refs):
        o = refs[-1]
        o[...] = _attn_block_fn(pl.program_id(0), *[r[...] for r in refs[:-1]]).astype(o.dtype)

    return pl.pallas_call(
        body, name=name, grid=(T // W,),
        in_specs=_attn_specs(),
        out_specs=pl.BlockSpec((W, 4 * LANE), lambda n: (n, 0)),
        out_shape=jax.ShapeDtypeStruct((T, 2 * 4 * LANE), bf16),
        compiler_params=_cparams("parallel"),
    )(proj, proj, proj, proj, proj, cos, sin, cos, sin, sinks)


def _attn_bwd(name, proj, cos, sin, sinks, d_oab):
    T = proj.shape[0]
    W = WINDOW
    Q = 4 * LANE
    nb = T // W

    def body(*refs):
        ins = [r[...] for r in refs[:10]]
        do = refs[10][...]
        out_ref, ds_ref, d_ref = refs[11:]
        n = pl.program_id(0)
        _, vjp = jax.vjp(functools.partial(_attn_block_fn, n), *ins)
        dq, dkp, dkc, dvp, dvc, _, _, _, _, dsk = vjp(do)

        @pl.when(n == 0)
        def _():
            d_ref[:, Q:] = jnp.zeros((T, 2 * LANE), f32)
            ds_ref[...] = jnp.zeros_like(ds_ref)

        cur = pl.ds(pl.multiple_of(n * W, W), W)
        d_ref[cur, :Q] = dq
        d_ref[cur, Q:Q + LANE] += dkc
        d_ref[cur, Q + LANE:] += dvc
        ds_ref[...] += dsk

        @pl.when(n > 0)
        def _():
            prv = pl.ds(pl.multiple_of((n - 1) * W, W), W)
            d_ref[prv, Q:Q + LANE] += dkp
            d_ref[prv, Q + LANE:] += dvp

        @pl.when(n == nb - 1)
        def _():
            out_ref[...] = d_ref[...].astype(out_ref.dtype)

    return pl.pallas_call(
        body, name=name, grid=(nb,),
        in_specs=_attn_specs() + [pl.BlockSpec((W, Q), lambda n: (n, 0))],
        out_specs=[pl.BlockSpec((T, Q + 2 * LANE), lambda n: (0, 0)),
                   pl.BlockSpec((1, A_Q_HEADS), lambda n: (0, 0))],
        out_shape=[jax.ShapeDtypeStruct((T, HYB_PROJ_PAD), bf16), jax.ShapeDtypeStruct((1, A_Q_HEADS), f32)],
        scratch_shapes=[pltpu.VMEM((T, Q + 2 * LANE), f32)],
        compiler_params=_cparams("arbitrary"),
    )(proj, proj, proj, proj, proj, cos, sin, cos, sin, sinks, d_oab)


def _bdot(spec, a, b, precision=None):
    return jnp.einsum(spec, a, b, preferred_element_type=f32, precision=precision)


@jax.custom_vjp
def _tri_inv(a):
    H, C, _ = a.shape
    B = 2 * SUBLANE
    nb = C // B
    r = lax.broadcasted_iota(jnp.int32, (C, C), 0)
    c = lax.broadcasted_iota(jnp.int32, (C, C), 1)
    a4 = jnp.where((r // B) == (c // B), a, 0.0).reshape(H, nb, B, C)
    t4 = jnp.broadcast_to(jnp.where(r == c, 1.0, 0.0).astype(f32), a.shape).reshape(H, nb, B, C)
    for j in range(B - 1):
        col = jnp.concatenate([a4[:, b:b + 1, :, B * b + j:B * b + j + 1] for b in range(nb)], axis=1)
        t4 = t4 - col * t4[:, :, j:j + 1, :]
    x = t4.reshape(H, C, C)
    hi = lax.Precision.HIGH
    while B < C:
        m = jnp.where(((r // (2 * B)) == (c // (2 * B))) & ((r // B) > (c // B)), a, 0.0)
        x = x - _bdot("hij,hjk->hik", x, _bdot("hij,hjk->hik", m, x, precision=hi), precision=hi)
        B *= 2
    return x


def _tri_inv_fwd(a):
    t = _tri_inv(a)
    return t, t


def _tri_inv_bwd(t, g):
    C = t.shape[-1]
    r = lax.broadcasted_iota(jnp.int32, (C, C), 0)
    c = lax.broadcasted_iota(jnp.int32, (C, C), 1)
    x = _bdot("hki,hkj->hij", t, g, precision=lax.Precision.HIGHEST)
    y = _bdot("hik,hjk->hij", x, t, precision=lax.Precision.HIGHEST)
    return (jnp.where(r > c, -y, 0.0),)


_tri_inv.defvjp(_tri_inv_fwd, _tri_inv_bwd)


@jax.custom_vjp
def _tri_inv_saved(a, t):
    return t


_tri_inv_saved.defvjp(lambda a, t: (t, t), lambda t, g: (_tri_inv_bwd(t, g)[0], jnp.zeros_like(t)))


def _silu(x):
    return x * jax.nn.sigmoid(x)


def _l2n(x):
    return x * lax.rsqrt(jnp.sum(x * x, axis=-1, keepdims=True) + NORM_EPS)


def _delta_chunk_fn(cq, ck, cv, z, lg, a_log, dt_bias, norm_w, S, t_saved=None, want_t=False):
    C = B_CHUNK
    lane = lax.broadcasted_iota(jnp.int32, (C, LANE), 1)
    pick = lambda l0: jnp.concatenate(
        [jnp.sum(jnp.where(lane == l0 + h, lg, 0.0), axis=1, keepdims=True)[None] for h in range(B_HEADS)], axis=0)
    bl, al = pick(0), pick(B_HEADS)
    q = _l2n(_silu(cq)) * (B_HEAD_DIM ** -0.5)
    k = _l2n(_silu(ck))
    v = _silu(cv)
    beta = jax.nn.sigmoid(bl)
    g = -jnp.exp(a_log) * jax.nn.softplus(al + dt_bias)
    r = lax.broadcasted_iota(jnp.int32, (C, C), 0)
    c = lax.broadcasted_iota(jnp.int32, (C, C), 1)
    eye = r == c
    g_row = jnp.sum(jnp.where(eye, g, 0.0), axis=1, keepdims=True)
    gc = jnp.sum(jnp.where(c <= r, g_row, 0.0), axis=2, keepdims=True)
    gc_row = jnp.sum(jnp.where(eye, gc, 0.0), axis=1, keepdims=True)
    decay_incl = jnp.exp(jnp.where(r >= c, gc - gc_row, -jnp.inf))
    decay_strict = jnp.where(r > c, decay_incl, 0.0)
    kb = k * beta
    vb = v * beta
    kbf = k.astype(bf16)
    a_mat = _bdot("hik,hjk->hij", kb.astype(bf16), kbf) * decay_strict
    t_f32 = _tri_inv(a_mat) if t_saved is None else _tri_inv_saved(a_mat, t_saved)
    t_mat = t_f32.astype(bf16)
    eg = jnp.exp(gc)
    u = _bdot("hij,hjv->hiv", t_mat, vb.astype(bf16))
    w = _bdot("hij,hjk->hik", t_mat, (kb * eg).astype(bf16))
    qk = _bdot("hik,hjk->hij", q.astype(bf16), kbf) * decay_incl
    g_last = jnp.sum(g, axis=1, keepdims=True)
    k_tail = k * jnp.exp(g_last - gc)
    Sb = S.astype(bf16)
    v_new = u - _bdot("hck,hkv->hcv", w.astype(bf16), Sb)
    o = _bdot("hck,hkv->hcv", (q * eg).astype(bf16), Sb) + _bdot("hij,hjv->hiv", qk.astype(bf16), v_new.astype(bf16))
    S_new = S * jnp.exp(g_last) + _bdot("hck,hcv->hkv", k_tail.astype(bf16), v_new.astype(bf16))
    ob = o * lax.rsqrt(jnp.mean(o * o, axis=-1, keepdims=True) + NORM_EPS) * norm_w
    return (ob * _silu(z), S_new) + ((t_f32,) if want_t else ())


DELTA_CHUNKS_PER_STEP = 4


def _delta_in_specs(rev, N):
    C = DELTA_CHUNKS_PER_STEP * B_CHUNK
    ix = (lambda n: N - 1 - n) if rev else (lambda n: n)
    specs = [pl.BlockSpec((C, 3 * B_HEADS * LANE), lambda n: (ix(n), 0))]
    specs += [pl.BlockSpec((C, LANE), lambda n, h=h: (ix(n), CB_Z + h)) for h in range(B_HEADS)]
    specs += [
        pl.BlockSpec((C, LANE), lambda n: (ix(n), CB_LG)),
        pl.BlockSpec((B_HEADS, 1, 1), lambda n: (0, 0, 0)),
        pl.BlockSpec((B_HEADS, 1, 1), lambda n: (0, 0, 0)),
        pl.BlockSpec((1, LANE), lambda n: (0, 0)),
    ]
    return specs


def _delta_inputs(u, c_ref, z_refs, lg, al, dt, nw):
    H = B_HEADS
    rows = slice(u * B_CHUNK, (u + 1) * B_CHUNK)
    part = lambda p: jnp.stack([c_ref[rows, LANE * (p * H + h):LANE * (p * H + h + 1)] for h in range(H)])
    return (part(0), part(1), part(2), jnp.stack([z[rows, :] for z in z_refs]), lg[rows, :], al[...], dt[...], nw[...])


def _delta_fwd(name, c, proj, a_log, dt_bias, norm_w, o_ab):
    T = c.shape[0]
    C = B_CHUNK
    N = T // C
    Dh = B_HEAD_DIM
    H = B_HEADS

    def body(*refs):
        c_ref, z_refs, (lg, al, dt, nw) = refs[0], refs[1:1 + H], refs[1 + H:5 + H]
        o_ref, s_ref, t_ref, S = refs[6 + H:]

        @pl.when(pl.program_id(0) == 0)
        def _():
            S[...] = jnp.zeros_like(S)

        s = S[...]
        for u in range(U):
            s_ref[:, u] = s
            ob, s, t = _delta_chunk_fn(*_delta_inputs(u, c_ref, z_refs, lg, al, dt, nw), s, want_t=True)
            for h in range(H):
                o_ref[u * C:(u + 1) * C, LANE * h:LANE * (h + 1)] = ob[h].astype(o_ref.dtype)
            t_ref[:, u] = t
        S[...] = s

    U = DELTA_CHUNKS_PER_STEP
    return pl.pallas_call(
        body, name=name, grid=(N // U,),
        in_specs=_delta_in_specs(False, N // U) + [pl.BlockSpec(memory_space=pl.ANY)],
        out_specs=[pl.BlockSpec((U * C, H * LANE), lambda n: (n, 1)),
                   pl.BlockSpec((H, U, Dh, Dh), lambda n: (0, n, 0, 0)),
                   pl.BlockSpec((H, U, C, C), lambda n: (0, n, 0, 0))],
        out_shape=[jax.ShapeDtypeStruct(o_ab.shape, o_ab.dtype), jax.ShapeDtypeStruct((H, N, Dh, Dh), f32),
                   jax.ShapeDtypeStruct((H, N, C, C), f32)],
        input_output_aliases={5 + H: 0},
        scratch_shapes=[pltpu.VMEM((H, Dh, Dh), f32)],
        compiler_params=_cparams("arbitrary"),
    )(c, *([proj] * H), proj, a_log, dt_bias, norm_w, o_ab)


def _delta_bwd(name, c, proj, a_log, dt_bias, norm_w, s_saved, t_saved, d_oab, dproj):
    T = c.shape[0]
    C = B_CHUNK
    N = T // C
    Dh = B_HEAD_DIM
    H = B_HEADS

    def body(*refs):
        c_ref, z_refs, (lg, al, dt, nw) = refs[0], refs[1:1 + H], refs[1 + H:5 + H]
        s_ref, t_ref, do_ref = refs[5 + H:8 + H]
        dc, dtail, dal, ddt, dnw, dS = refs[9 + H:]

        @pl.when(pl.program_id(0) == 0)
        def _():
            dS[...] = jnp.zeros_like(dS)
            dal[...] = jnp.zeros_like(dal)
            ddt[...] = jnp.zeros_like(ddt)
            dnw[...] = jnp.zeros_like(dnw)

        ds = dS[...]
        for u in reversed(range(U)):
            rows = slice(u * C, (u + 1) * C)
            _, vjp = jax.vjp(functools.partial(_delta_chunk_fn, t_saved=t_ref[:, u]),
                             *_delta_inputs(u, c_ref, z_refs, lg, al, dt, nw), s_ref[:, u])
            do = jnp.stack([do_ref[rows, LANE * h:LANE * (h + 1)] for h in range(H)])
            g = vjp((do, ds))
            for h in range(H):
                for p in range(3):
                    dc[rows, LANE * (p * H + h):LANE * (p * H + h + 1)] = g[p][h]
                dtail[rows, LANE * h:LANE * (h + 1)] = g[3][h].astype(dtail.dtype)
            dtail[rows, LANE * H:LANE * (H + 1)] = g[4].astype(dtail.dtype)
            dtail[rows, LANE * (H + 1):] = jnp.zeros((C, LANE), dtail.dtype)
            dal[...] += g[5]
            ddt[...] += g[6]
            dnw[...] += g[7]
            ds = g[8]
        dS[...] = ds

    U = DELTA_CHUNKS_PER_STEP
    NB = N // U
    rn = lambda n: NB - 1 - n
    return pl.pallas_call(
        body, name=name, grid=(NB,),
        in_specs=_delta_in_specs(True, NB) + [
            pl.BlockSpec((H, U, Dh, Dh), lambda n: (0, rn(n), 0, 0)),
            pl.BlockSpec((H, U, C, C), lambda n: (0, rn(n), 0, 0)),
            pl.BlockSpec((U * C, H * LANE), lambda n: (rn(n), 1)),
            pl.BlockSpec(memory_space=pl.ANY),
        ],
        out_specs=[
            pl.BlockSpec((U * C, 3 * H * LANE), lambda n: (rn(n), 0)),
            pl.BlockSpec((U * C, (H + 2) * LANE), lambda n: (rn(n), CB_Z // (H + 2))),
            pl.BlockSpec((H, 1, 1), lambda n: (0, 0, 0)),
            pl.BlockSpec((H, 1, 1), lambda n: (0, 0, 0)),
            pl.BlockSpec((1, LANE), lambda n: (0, 0)),
        ],
        out_shape=[jax.ShapeDtypeStruct((T, 3 * H * Dh), f32), jax.ShapeDtypeStruct(dproj.shape, dproj.dtype),
                   jax.ShapeDtypeStruct((H, 1, 1), f32), jax.ShapeDtypeStruct((H, 1, 1), f32),
                   jax.ShapeDtypeStruct((1, LANE), f32)],
        input_output_aliases={8 + H: 1},
        scratch_shapes=[pltpu.VMEM((H, Dh, Dh), f32)],
        compiler_params=_cparams("arbitrary"),
    )(c, *([proj] * H), proj, a_log, dt_bias, norm_w, s_saved, t_saved, d_oab, dproj)


def _gate_matmuls(xc, wa_ref, wx_ref):
    bw = wa_ref.shape[-1]
    xb = xc.astype(bf16)
    blocks = [xb[:, bw * h:bw * (h + 1)] for h in range(LRU_BLOCKS)]
    return (jnp.concatenate([_dot(blocks[h], wa_ref[h], NN) for h in range(LRU_BLOCKS)], axis=1),
            jnp.concatenate([_dot(blocks[h], wx_ref[h], NN) for h in range(LRU_BLOCKS)], axis=1))


def _gates_fwd(name, xc, w_a, w_x, pars, tm=512):
    T, Wd = xc.shape
    tm = min(tm, T)

    def body(x_ref, wa_ref, wx_ref, ba, bx, lam, a_ref, b_ref):
        x = x_ref[...]
        pr, pi = _gate_matmuls(x, wa_ref, wx_ref)
        a_ref[...], b_ref[...] = _rglru_pre_fn(pr, pi, x, ba[...], bx[...], lam[...])

    row = pl.BlockSpec((tm, Wd), lambda i: (i, 0))
    return pl.pallas_call(
        body, name=name, grid=(T // tm,),
        in_specs=[row, _whole_spec(w_a), _whole_spec(w_x)] + [_whole_spec(p) for p in pars],
        out_specs=[row, row], out_shape=[jax.ShapeDtypeStruct((T, Wd), f32)] * 2,
        compiler_params=_cparams("parallel"),
    )(xc, w_a, w_x, *pars)


def _gates_bwd(name, xc, w_a, w_x, pars, lam_t, h_prev, tm=512):
    T, Wd = xc.shape
    tm = min(tm, T)
    bw = Wd // LRU_BLOCKS

    def body(x_ref, wa_ref, wx_ref, ba, bx, lam, lt_ref, hp_ref, dx_ref, dr_ref, di_ref, dba, dbx, dlam):
        x = x_ref[...]
        pr, pi = _gate_matmuls(x, wa_ref, wx_ref)
        _, vjp = jax.vjp(_rglru_pre_fn, pr, pi, x, ba[...], bx[...], lam[...])
        lt = lt_ref[...]
        dpr, dpi, dxc, g_ba, g_bx, g_lam = vjp((lt * hp_ref[...], lt))
        dprb, dpib = dpr.astype(bf16), dpi.astype(bf16)
        dx_ref[...] = dxc + jnp.concatenate(
            [_dot(dprb[:, bw * h:bw * (h + 1)], wa_ref[h], NT) + _dot(dpib[:, bw * h:bw * (h + 1)], wx_ref[h], NT)
             for h in range(LRU_BLOCKS)], axis=1)
        dr_ref[...] = dprb
        di_ref[...] = dpib

        @pl.when(pl.program_id(0) == 0)
        def _():
            dba[...] = jnp.zeros_like(dba)
            dbx[...] = jnp.zeros_like(dbx)
            dlam[...] = jnp.zeros_like(dlam)

        dba[...] += g_ba
        dbx[...] += g_bx
        dlam[...] += g_lam

    row = pl.BlockSpec((tm, Wd), lambda i: (i, 0))
    vec = pl.BlockSpec((1, Wd), lambda i: (0, 0))
    return pl.pallas_call(
        body, name=name, grid=(T // tm,),
        in_specs=[row, _whole_spec(w_a), _whole_spec(w_x)] + [_whole_spec(p) for p in pars] + [row, row],
        out_specs=[row, row, row, vec, vec, vec],
        out_shape=[jax.ShapeDtypeStruct((T, Wd), f32), jax.ShapeDtypeStruct((T, Wd), bf16),
                   jax.ShapeDtypeStruct((T, Wd), bf16)] + [jax.ShapeDtypeStruct((1, Wd), f32)] * 3,
        compiler_params=_cparams("arbitrary"),
    )(xc, w_a, w_x, *pars, lam_t, h_prev)


def _blockdiag_bwd_dw(name, xc, dpr, dpi, tk=512):
    T, Wd = xc.shape
    bw = Wd // LRU_BLOCKS
    tk = min(tk, T)

    def body(x_ref, dr, di, oa, ox):
        @pl.when(pl.program_id(1) == 0)
        def _():
            oa[...] = jnp.zeros_like(oa)
            ox[...] = jnp.zeros_like(ox)

        xb = x_ref[...].astype(bf16)
        oa[...] += _dot(xb, dr[...].astype(bf16), TN)
        ox[...] += _dot(xb, di[...].astype(bf16), TN)

    xs = pl.BlockSpec((tk, bw), lambda h, k: (k, h))
    ws = pl.BlockSpec((None, bw, bw), lambda h, k: (h, 0, 0))
    return pl.pallas_call(
        body, name=name, grid=(LRU_BLOCKS, T // tk), in_specs=[xs, xs, xs], out_specs=[ws, ws],
        out_shape=[jax.ShapeDtypeStruct((LRU_BLOCKS, bw, bw), f32)] * 2,
        compiler_params=_cparams("parallel", "arbitrary"),
    )(xc, dpr, dpi)


def _scan(name, a, proj, reverse, b=None, h=None, dhg=None, tt=512, cb=512):
    T, Wd = a.shape
    tt, cb = min(tt, T), min(cb, Wd)
    nt = T // tt
    ng = tt // SUBLANE

    def body(a_ref, g_ref, *rest):
        n_in = 2 if reverse else 1
        ins, outs, (carry, carry_a) = rest[:n_in], rest[n_in:-2], rest[-2:]

        @pl.when(pl.program_id(1) == 0)
        def _():
            carry[...] = jnp.zeros_like(carry)
            carry_a[...] = jnp.zeros_like(carry_a)

        row = lax.broadcasted_iota(jnp.int32, (SUBLANE, cb), 0)

        def step(gi, c):
            hp, ap = c
            g = (ng - 1 - gi) if reverse else gi
            rows = pl.ds(pl.multiple_of(g * SUBLANE, SUBLANE), SUBLANE)
            A = a_ref[rows, :]
            gate = g_ref[rows, :]
            a_first = jnp.broadcast_to(A[0:1, :], (SUBLANE, cb))
            if reverse:
                _, vjp = jax.vjp(_rec_gate_fn, ins[0][rows, :], gate)
                B, dgate = vjp((ins[1][rows, :],))
                outs[1][rows, :] = dgate
                A = jnp.where(row == SUBLANE - 1, ap, pltpu.roll(A, SUBLANE - 1, axis=0))
            else:
                B = ins[0][rows, :]
            for s in (1, 2, 4):
                sh = (SUBLANE - s) if reverse else s
                As = pltpu.roll(A, sh, axis=0)
                Bs = pltpu.roll(B, sh, axis=0)
                valid = (row < SUBLANE - s) if reverse else (row >= s)
                B = jnp.where(valid, A * Bs + B, B)
                A = jnp.where(valid, A * As, A)
            hcur = A * hp + B
            outs[0][rows, :] = hcur
            if not reverse:
                outs[1][rows, :] = jnp.where(row == 0, hp, pltpu.roll(hcur, 1, axis=0))
                outs[2][rows, :] = _rec_gate_fn(hcur, gate)[0]
            edge = hcur[0:1, :] if reverse else hcur[SUBLANE - 1:SUBLANE, :]
            return jnp.broadcast_to(edge, (SUBLANE, cb)), a_first

        carry[...], carry_a[...] = lax.fori_loop(0, ng, step, (carry[...], carry_a[...]))

    nc = Wd // cb
    tok = (lambda i: nt - 1 - i) if reverse else (lambda i: i)
    spec = pl.BlockSpec((tt, cb), lambda c, i: (tok(i), c))
    gate_half = pl.BlockSpec((tt, cb), lambda c, i: (tok(i), nc + c))
    if reverse:
        args, out_specs = (a, proj, h, dhg), [spec, gate_half]
        out_shape = [jax.ShapeDtypeStruct((T, Wd), f32), jax.ShapeDtypeStruct((T, 2 * Wd), f32)]
    else:
        args, out_specs = (a, proj, b), [spec] * 3
        out_shape = [jax.ShapeDtypeStruct((T, Wd), f32)] * 3
    return pl.pallas_call(
        body, name=name, grid=(nc, nt), in_specs=[spec, gate_half] + [spec] * (len(args) - 2), out_specs=out_specs,
        out_shape=out_shape,
        scratch_shapes=[pltpu.VMEM((SUBLANE, cb), f32), pltpu.VMEM((SUBLANE, cb), f32)],
        compiler_params=_cparams("parallel", "arbitrary"),
    )(*args)


def _relu2_epilogue(r):
    h = jnp.maximum(r, 0.0)
    return r, h * h


def _drelu2_epilogue(r, a):
    return (r * (2.0 * jnp.maximum(a.astype(f32), 0.0)),)


def _residual_cot(through, upper):
    return (through + DN_ALPHA * upper,)


def _merge_cols(name, g, tm=256):
    _, L, R, s = g.shape

    def body(g_ref, o_ref):
        for d in range(N_DEV):
            o_ref[:, s * d:s * (d + 1)] = g_ref[d].astype(bf16)
        o_ref[:, N_DEV * s:] = jnp.zeros((tm, HYB_PROJ_PAD - N_DEV * s), bf16)

    return pl.pallas_call(
        body, name=name, grid=(L, R // tm),
        in_specs=[pl.BlockSpec((N_DEV, None, tm, s), lambda l, i: (0, l, i, 0))],
        out_specs=pl.BlockSpec((None, tm, HYB_PROJ_PAD), lambda l, i: (l, i, 0)),
        out_shape=jax.ShapeDtypeStruct((L, R, HYB_PROJ_PAD), bf16),
        compiler_params=_cparams("parallel", "parallel"),
    )(g)


def _split_cols(name, dw, tm=256):
    R = dw.shape[0]
    s = HYB_PROJ // N_DEV

    def body(g_ref, o_ref):
        for d in range(N_DEV):
            o_ref[d] = g_ref[:, s * d:s * (d + 1)].astype(bf16)

    return pl.pallas_call(
        body, name=name, grid=(R // tm,),
        in_specs=[pl.BlockSpec((tm, HYB_PROJ_PAD), lambda i: (i, 0))],
        out_specs=pl.BlockSpec((N_DEV, tm, s), lambda i: (0, i, 0)),
        out_shape=jax.ShapeDtypeStruct((N_DEV, R, s), bf16),
        compiler_params=_cparams("parallel"),
    )(dw)


def _rows_to_dev(dw):
    nb, r, c = dw.shape
    t = dw.reshape(nb, N_DEV, r // N_DEV, c)
    return jnp.moveaxis(t, 1, 0).reshape(N_DEV, nb * (r // N_DEV), c).astype(bf16)


def _ln_epilogue(r, x, g, b):
    y = _ln_res_fn(x, r, g, b)[0]
    return r, y, y


def _hybrid_fwd(tag, x, xb, W, j, cos, sin, ln):
    proj = _mm(f"{tag}_proj", xb, W["hyb_w_in"][j], "nn", b_kind="lead", b_lead=0)
    o_a = _attn_fwd(f"{tag}_attn", proj, cos, sin, W["hyb_sinks"][j][None, :])
    c = _conv_fwd(f"{tag}_conv", proj, CB_CONV, 12, W["hyb_conv_w"][j], None)
    o_ab, s_saved, t_saved = _delta_fwd(f"{tag}_delta", c, proj, W["hyb_a_log"][j].reshape(B_HEADS, 1, 1),
                                        W["hyb_dt_bias"][j].reshape(B_HEADS, 1, 1), W["hyb_norm_w"][j][None, :], o_a)
    mix, x1, x1b = _mm(f"{tag}_out", o_ab, W["hyb_w_out"][j], "nn", b_kind="lead", b_lead=0, epilogue=_ln_epilogue,
                       extras=(x,), params=ln, out_dtypes=(f32, f32, bf16), tm=512)
    return mix, x1, x1b, (proj, c, s_saved, t_saved, o_ab)


def _hybrid_bwd(tag, x, dmix, addend, W, j, cos, sin, saved, G, send_early):
    proj, c, s_saved, t_saved, o_ab = saved
    T = x.shape[0]
    d_oab = _mm(f"{tag}_dout", dmix, W["hyb_w_out"][j], "nt", b_kind="lead", b_lead=0)
    G["hyb_w_out"][j] = _mm(f"{tag}_dwout", o_ab, dmix, "tn", out_dtypes=(bf16,)).reshape(N_DEV, -1, D_MODEL)
    sinks = W["hyb_sinks"][j][None, :] + send_early({("hyb_w_out", j): G["hyb_w_out"][j]})
    dproj, dsinks = _attn_bwd(f"{tag}_dattn", proj, cos, sin, sinks, d_oab)
    a_log = W["hyb_a_log"][j].reshape(B_HEADS, 1, 1)
    dt_bias = W["hyb_dt_bias"][j].reshape(B_HEADS, 1, 1)
    dc, dproj, dal, ddt, dnw = _delta_bwd(f"{tag}_ddelta", c, proj, a_log, dt_bias, W["hyb_norm_w"][j][None, :],
                                          s_saved, t_saved, d_oab, dproj)
    dproj, dconv_w, _ = _conv_bwd(f"{tag}_dconv", dc, proj, CB_CONV, 12, W["hyb_conv_w"][j], dproj, CB_CONV)
    dx = _mm(f"{tag}_dx", dproj, W["hyb_w_in"][j], "nt", b_kind="lead", b_lead=0,
             **({} if addend is None else dict(epilogue=_residual_cot, extras=(addend,))))
    G["hyb_w_in"][j] = _split_cols(f"{tag}_dwin_split", _mm(f"{tag}_dwin", x, dproj, "tn", tn=1536))
    G["hyb_sinks"][j] = dsinks[0]
    G["hyb_conv_w"][j] = dconv_w
    G["hyb_a_log"][j] = dal.reshape(B_HEADS)
    G["hyb_dt_bias"][j] = ddt.reshape(B_HEADS)
    G["hyb_norm_w"][j] = dnw[0]
    return dx


def _rec_fwd(tag, x, xb, W, j, ln):
    Wd = D_MODEL
    proj = _mm(f"{tag}_proj", xb, W["rec_w_in"][j], "nn", b_kind="devcol", b_lead=0)
    xc = _conv_fwd(f"{tag}_conv", proj, 0, Wd // LANE, W["rec_conv_w"][j], W["rec_conv_b"][j][None, :])
    pars = [W["rec_b_a"][j][None, :], W["rec_b_x"][j][None, :], W["rec_lambda"][j][None, :]]
    a, b = _gates_fwd(f"{tag}_gates", xc, W["rec_w_a"][j][0], W["rec_w_x"][j][0], pars)
    h, h_prev, hg = _scan(f"{tag}_scan", a, proj, False, b=b)
    mix, x1, x1b = _mm(f"{tag}_out", hg, W["rec_w_out"][j], "nn", b_kind="lead", b_lead=0, epilogue=_ln_epilogue,
                       extras=(x,), params=ln, out_dtypes=(f32, f32, bf16), tm=512)
    return mix, x1, x1b, (proj, xc, a, h, h_prev, hg)


def _rec_bwd(tag, x, dmix, addend, W, j, saved, G, send_early):
    proj, xc, a, h, h_prev, hg = saved
    Wd = D_MODEL
    dhg = _mm(f"{tag}_dout", dmix, W["rec_w_out"][j], "nt", b_kind="lead", b_lead=0)
    G["rec_w_out"][j] = _mm(f"{tag}_dwout", hg, dmix, "tn", out_dtypes=(bf16,)).reshape(N_DEV, -1, D_MODEL)
    sent = send_early({("rec_w_out", j): G["rec_w_out"][j]})
    lam_t, dproj = _scan(f"{tag}_dscan", a, proj, True, h=h, dhg=dhg)
    pars = [W["rec_b_a"][j][None, :] + sent, W["rec_b_x"][j][None, :], W["rec_lambda"][j][None, :]]
    dxc, dpr, dpi, db_a, db_x, dlam = _gates_bwd(f"{tag}_dgates", xc, W["rec_w_a"][j][0], W["rec_w_x"][j][0], pars,
                                                 lam_t, h_prev)
    dwa, dwx = _blockdiag_bwd_dw(f"{tag}_dgates_dw", xc, dpr, dpi)
    G["rec_w_a"][j], G["rec_w_x"][j] = _rows_to_dev(dwa), _rows_to_dev(dwx)
    dproj, dconv_w, dconv_b = _conv_bwd(f"{tag}_dconv", dxc, proj, 0, Wd // LANE, W["rec_conv_w"][j], dproj, 0)
    dx = _mm(f"{tag}_dx", dproj, W["rec_w_in"][j], "nt", b_kind="devcol", b_lead=0,
             **({} if addend is None else dict(epilogue=_residual_cot, extras=(addend,))))
    G["rec_w_in"][j] = _mm(f"{tag}_dwin", x, dproj, "tn", o_kind="devcol", out_dtypes=(bf16,), tn=2048)
    G["rec_conv_w"][j] = dconv_w
    G["rec_conv_b"][j] = dconv_b
    G["rec_b_a"][j] = db_a[0]
    G["rec_b_x"][j] = db_x[0]
    G["rec_lambda"][j] = dlam[0]
    return dx


def _local_step(x, target, W, load_layer, grads_ready):
    T = x.shape[0]
    cos, sin = _rope_tables(T)
    saved = []
    xb = x
    for layer in range(DEPTH):
        j = layer // 2
        tag = f"L{layer}"
        load_layer(layer, "mixer", x)
        ln1 = (W["ln1_g"][layer][None, :], W["ln1_b"][layer][None, :])
        if layer % 2 == 0:
            mix, x1, x1b, sv = _hybrid_fwd(tag, x, xb, W, j, cos, sin, ln1)
        else:
            mix, x1, x1b, sv = _rec_fwd(tag, x, xb, W, j, ln1)
        load_layer(layer, "mlp", x1)
        a, h2 = _mm(f"{tag}_mlp1", x1b, W["mlp_w1"][layer], "nn", b_kind="devcol", b_lead=0, epilogue=_relu2_epilogue,
                    out_dtypes=(bf16, bf16), tm=2048)
        ln2 = (W["ln2_g"][layer][None, :], W["ln2_b"][layer][None, :])
        y, x2, x2b = _mm(f"{tag}_mlp2", h2, W["mlp_w2"][layer], "nn", b_kind="lead", b_lead=0, epilogue=_ln_epilogue,
                         extras=(x1,), params=ln2, out_dtypes=(f32, f32, bf16))
        saved.append((x, xb, sv, mix, x1, x1b, a, h2, y))
        x, xb = x2, x2b
    loss, dx = _loss_head(x, target)

    G = {k: [None] * (DEPTH if k.startswith(("ln", "mlp")) else DEPTH // 2) for k in (
        "hyb_w_in", "hyb_sinks", "hyb_conv_w", "hyb_a_log", "hyb_dt_bias", "hyb_norm_w", "hyb_w_out",
        "rec_w_in", "rec_conv_w", "rec_conv_b", "rec_w_a", "rec_b_a", "rec_w_x", "rec_b_x", "rec_lambda", "rec_w_out",
        "ln1_g", "ln1_b", "mlp_w1", "mlp_w2", "ln2_g", "ln2_b")}
    order = jnp.zeros((1, 1), f32)
    cot_rows, cot_fn = [(dx, 0, D_MODEL)], None
    for layer in reversed(range(DEPTH)):
        j = layer // 2
        tag = f"L{layer}"
        x0, x0b, sv, mix, x1, x1b, a, h2, y = saved[layer]
        ln2 = [W["ln2_g"][layer][None, :] + order, W["ln2_b"][layer][None, :]]
        (dy, dyb), (dg2, db2) = _tl_bwd(f"{tag}_dln2", _ln_res_fn, [(x1, 0, D_MODEL), (y, 0, D_MODEL)], ln2,
                                        cot_rows, cot_fn=cot_fn, skip=(0,), bf16_copy=True)
        G["ln2_g"][layer], G["ln2_b"][layer] = dg2[0], db2[0]
        da = _mm(f"{tag}_dmlp2", dyb, W["mlp_w2"][layer], "nt", b_kind="lead", b_lead=0, epilogue=_drelu2_epilogue,
                 extras=(a,), out_dtypes=(bf16,), tm=2048, tn=512)
        G["mlp_w2"][layer] = _mm(f"{tag}_dw2", h2, dyb, "tn", out_dtypes=(bf16,), tm=2048).reshape(N_DEV, -1, D_MODEL)
        dx1 = _mm(f"{tag}_dmlp1", da, W["mlp_w1"][layer], "nt", b_kind="devcol", b_lead=0, tm=2048)
        G["mlp_w1"][layer] = _mm(f"{tag}_dw1", x1b, da, "tn", o_kind="devcol", out_dtypes=(bf16,), tn=2048)
        ln1 = [W["ln1_g"][layer][None, :], W["ln1_b"][layer][None, :]]
        (dmix, dmixb), (dg1, db1) = _tl_bwd(f"{tag}_dln1", _ln_res_fn, [(x0, 0, D_MODEL), (mix, 0, D_MODEL)], ln1,
                                            [(dx1, 0, D_MODEL), (dy, 0, D_MODEL)], cot_fn=_residual_cot, skip=(0,),
                                            bf16_copy=True)
        G["ln1_g"][layer], G["ln1_b"][layer] = dg1[0], db1[0]
        dx0_a = dmix if layer == 0 else None
        early = functools.partial(grads_ready, f"l{layer}_early",
                                  {(k, layer): G[k][layer] for k in ("mlp_w1", "mlp_w2")})
        if layer % 2 == 0:
            dx = _hybrid_bwd(tag, x0b, dmixb, dx0_a, W, j, cos, sin, sv, G, early)
        else:
            dx = _rec_bwd(tag, x0b, dmixb, dx0_a, W, j, sv, G, early)
        order = grads_ready(f"l{layer}_late", {}, {(k, i): G[k][i] for k, i in _layer_weights(layer)[:-2]
                                                  if not k.endswith("w_out")})
        cot_rows, cot_fn = [(dx, 0, D_MODEL), (dmix, 0, D_MODEL)], _residual_cot
    big = {k for k, _ in BIG}
    return loss, dx, {k: jnp.stack(v) for k, v in G.items() if k not in big}


def _layer_weights(layer):
    j = layer // 2
    mixer = ["hyb_w_in", "hyb_w_out"] if layer % 2 == 0 else ["rec_w_in", "rec_w_out", "rec_w_a", "rec_w_x"]
    return [(k, j) for k in mixer] + [("mlp_w1", layer), ("mlp_w2", layer)]


def _my_coords():
    return lax.axis_index("x"), lax.axis_index("y"), lax.axis_index("c")


def _all_gather(name, arrays):
    na = len(arrays)

    def body(*refs):
        x_refs, out_refs = refs[:na], refs[na:2 * na]
        send_sems, recv_sems, local_sems = refs[2 * na:]
        x, y, c = _my_coords()
        me, sibling = (x, y, c), (x, y, 1 - c)
        chips = [(1 - x, y), (x, 1 - y), (1 - x, 1 - y)]

        def blk(a, px, py, pc):
            return out_refs[a].at[4 * px + 2 * py + pc]

        def copy(a, k, block, to, src=None):
            return pltpu.make_async_remote_copy(
                src_ref=blk(a, *block) if src is None else src, dst_ref=blk(a, *block),
                send_sem=send_sems.at[a, k], recv_sem=recv_sems.at[a, k],
                device_id=to, device_id_type=pl.DeviceIdType.MESH)

        mine = [pltpu.make_async_copy(x_refs[a], blk(a, *me), local_sems.at[a]) for a in range(na)]
        for cp in mine:
            cp.start()
        first = []
        for a in range(na):
            first.append(copy(a, 0, me, sibling, src=x_refs[a]))
            first += [copy(a, 1 + j, me, (*chip, c), src=x_refs[a]) for j, chip in enumerate(chips)]
        for cp in first:
            cp.start()
        passed = []
        for a in range(na):
            for j, chip in enumerate(chips):
                copy(a, 1 + j, (*chip, c), me).wait_recv()
                passed.append(copy(a, 4 + j, (*chip, c), sibling))
                passed[-1].start()
        for a in range(na):
            copy(a, 0, sibling, me).wait_recv()
            for j, chip in enumerate(chips):
                copy(a, 4 + j, (*chip, 1 - c), me).wait_recv()
        for cp in first + passed:
            cp.wait_send()
        for cp in mine:
            cp.wait()

    return pl.pallas_call(
        body, name=name,
        out_shape=[jax.ShapeDtypeStruct((N_DEV,) + a.shape, a.dtype) for a in arrays],
        in_specs=[pl.BlockSpec(memory_space=pl.ANY)] * na,
        out_specs=[pl.BlockSpec(memory_space=pl.ANY)] * na,
        scratch_shapes=[pltpu.SemaphoreType.DMA((na, 7)), pltpu.SemaphoreType.DMA((na, 7)),
                        pltpu.SemaphoreType.DMA((na,))],
    )(*arrays)


_HBM = pl.BlockSpec(memory_space=pltpu.HBM)
_SEM = pl.BlockSpec(memory_space=pltpu.SEMAPHORE)


def _flip(k, x, y, c):
    return ((1 - x) if k & 4 else x, (1 - y) if k & 2 else y, (1 - c) if k & 1 else c)


_PEERS = {"gather": (1, 2, 4, 6), "scatter": (1, 2, 3, 4, 5, 6, 7)}


def _push_copies(kind, x_refs, land_refs, send_sems, recv_sems, local_sems):
    x, y, c = _my_coords()
    me = 4 * x + 2 * y + c
    peers = _PEERS[kind]
    remote, local = [], []
    for a in range(len(x_refs)):
        local.append(pltpu.make_async_copy(x_refs[a] if kind == "gather" else x_refs[a].at[me], land_refs[a].at[me],
                                           local_sems.at[a]))
        for n, k in enumerate(peers):
            px, py, pc = _flip(k, x, y, c)
            remote.append(pltpu.make_async_remote_copy(
                src_ref=x_refs[a] if kind == "gather" else x_refs[a].at[4 * px + 2 * py + pc],
                dst_ref=land_refs[a].at[me],
                send_sem=send_sems.at[a * len(peers) + n], recv_sem=recv_sems.at[a * len(peers) + n],
                device_id=(px, py, pc), device_id_type=pl.DeviceIdType.MESH))
    return remote, local


def _pass_to_sibling(name, lands):
    na = len(lands)
    chips = (2, 4, 6)

    def body(*refs):
        out_refs, send_sems, recv_sems = refs[na:2 * na], refs[2 * na], refs[2 * na + 1]
        x, y, c = _my_coords()
        cps = []
        for a in range(na):
            for n, k in enumerate(chips):
                px, py, _ = _flip(k, x, y, c)
                cps.append(pltpu.make_async_remote_copy(
                    src_ref=out_refs[a].at[4 * px + 2 * py + c], dst_ref=out_refs[a].at[4 * px + 2 * py + c],
                    send_sem=send_sems.at[a * 3 + n], recv_sem=recv_sems.at[a * 3 + n],
                    device_id=(x, y, 1 - c), device_id_type=pl.DeviceIdType.MESH))
        for cp in cps:
            cp.start()
        for a in range(na):
            for n, k in enumerate(chips):
                px, py, _ = _flip(k, x, y, c)
                blk = out_refs[a].at[4 * px + 2 * py + (1 - c)]
                pltpu.make_async_remote_copy(src_ref=blk, dst_ref=blk, send_sem=send_sems.at[a * 3 + n],
                                             recv_sem=recv_sems.at[a * 3 + n], device_id=(x, y, 1 - c),
                                             device_id_type=pl.DeviceIdType.MESH).wait_recv()
        for cp in cps:
            cp.wait_send()

    return pl.pallas_call(
        body, name=name,
        out_shape=[jax.ShapeDtypeStruct(l.shape, l.dtype) for l in lands],
        in_specs=[pl.BlockSpec(memory_space=pl.ANY)] * na,
        out_specs=[pl.BlockSpec(memory_space=pl.ANY)] * na,
        input_output_aliases={a: a for a in range(na)},
        scratch_shapes=[pltpu.SemaphoreType.DMA((3 * na,)), pltpu.SemaphoreType.DMA((3 * na,))],
    )(*lands)


_SIDE_EFFECT = pltpu.CompilerParams(has_side_effects=pltpu.SideEffectType.DATAFLOW_SIDE_EFFECTING)


def _push_start(name, kind, srcs, lands):
    na = len(srcs)

    def body(*refs):
        remote, local = _push_copies(kind, refs[:na], refs[na:2 * na], *refs[2 * na:2 * na + 3])
        for cp in remote + local:
            cp.start()
        token = refs[-1]
        token[...] = jnp.zeros_like(token)

    arrays = list(srcs) + list(lands)
    n_remote = na * len(_PEERS[kind])
    res = pl.pallas_call(
        body, name=name,
        out_shape=(pltpu.SemaphoreType.DMA((n_remote,)), pltpu.SemaphoreType.DMA((n_remote,)),
                   pltpu.SemaphoreType.DMA((na,)), *[pltpu.HBM(t.shape, t.dtype) for t in arrays],
                   jax.ShapeDtypeStruct((SUBLANE, LANE), f32)),
        in_specs=[_HBM] * (2 * na),
        out_specs=(_SEM, _SEM, _SEM, *[_HBM] * (2 * na), pl.BlockSpec(memory_space=pltpu.VMEM)),
        input_output_aliases={i: 3 + i for i in range(2 * na)},
        compiler_params=_SIDE_EFFECT,
    )(*[pltpu.with_memory_space_constraint(t, pltpu.HBM) for t in arrays])
    return list(res[:3]), res[3:3 + na], res[3 + na:3 + 2 * na], res[-1][:1, :1]


def _push_wait(name, kind, sems, srcs, lands, after):
    na = len(srcs)

    def body(*refs):
        remote, local = _push_copies(kind, refs[:na], refs[na:2 * na], *refs[2 * na:2 * na + 3])
        for cp in remote:
            cp.wait_send()
            cp.wait_recv()
        for cp in local:
            cp.wait()

    arrays = list(srcs) + list(lands)
    res = pl.pallas_call(
        body, name=name,
        out_shape=tuple(pltpu.HBM(t.shape, t.dtype) for t in arrays),
        in_specs=[_HBM] * (2 * na) + [_SEM] * 3 + [pl.BlockSpec(memory_space=pl.ANY)],
        out_specs=tuple([_HBM] * (2 * na)),
        input_output_aliases={i: i for i in range(2 * na)},
        compiler_params=_SIDE_EFFECT,
    )(*arrays, *sems, after)
    return res[na:]


def _sum_blocks(name, land):
    _, R, n = land.shape
    tr = R

    def body(l_ref, o_ref):
        acc = l_ref[0].astype(f32)
        for s in range(1, N_DEV):
            acc = acc + l_ref[s].astype(f32)
        o_ref[...] = acc

    return pl.pallas_call(
        body, name=name, grid=(R // tr,),
        in_specs=[pl.BlockSpec((N_DEV, tr, n), lambda i: (0, i, 0))],
        out_specs=pl.BlockSpec((tr, n), lambda i: (i, 0)),
        out_shape=jax.ShapeDtypeStruct((R, n), f32),
        compiler_params=_cparams("parallel"),
    )(land)


def _adamw(name, w, g, m, v):
    shape = w.shape
    last = shape[-1]
    rows = math.prod(shape[:-1])
    tm = 256 if rows % 256 == 0 and rows > 256 else rows
    w2, g2, m2, v2 = (t.reshape(rows, last) for t in (w, g, m, v))

    def body(w_ref, g_ref, m_ref, v_ref, d_ref, mo_ref, vo_ref):
        gg = g_ref[...]
        mn = ADAM_B1 * m_ref[...] + (1.0 - ADAM_B1) * gg
        vn = ADAM_B2 * v_ref[...] + (1.0 - ADAM_B2) * jnp.square(gg)
        m_hat = mn / (1.0 - ADAM_B1 ** ADAM_STEP)
        v_hat = vn / (1.0 - ADAM_B2 ** ADAM_STEP)
        d_ref[...] = -ADAM_LR * (m_hat / (jnp.sqrt(v_hat) + ADAM_EPS) + ADAM_WD * w_ref[...])
        mo_ref[...] = mn
        vo_ref[...] = vn

    spec = pl.BlockSpec((tm, last), lambda i: (i, 0))
    d, mn, vn = pl.pallas_call(
        body, name=name, grid=(rows // tm,), in_specs=[spec] * 4, out_specs=[spec] * 3,
        out_shape=[jax.ShapeDtypeStruct((rows, last), f32)] * 3,
        compiler_params=_cparams("parallel"),
    )(w2, g2, m2, v2)
    return d.reshape(shape), mn.reshape(shape), vn.reshape(shape)


def _adamw_land(name, lands, w, m, v, tm=256):
    L = len(lands)
    _, R, C = lands[0].shape
    tm = min(tm, R)

    def body(*refs):
        l_refs, (w_ref, m_ref, v_ref, g_ref, d_ref, mo_ref, vo_ref) = refs[:L], refs[L:]
        for k in range(L):
            @pl.when(pl.program_id(0) == k)
            def _(k=k):
                gg = l_refs[k][0].astype(f32)
                for s in range(1, N_DEV):
                    gg = gg + l_refs[k][s].astype(f32)
                g_ref[...] = gg
                mn = ADAM_B1 * m_ref[...] + (1.0 - ADAM_B1) * gg
                vn = ADAM_B2 * v_ref[...] + (1.0 - ADAM_B2) * jnp.square(gg)
                m_hat = mn / (1.0 - ADAM_B1 ** ADAM_STEP)
                v_hat = vn / (1.0 - ADAM_B2 ** ADAM_STEP)
                d_ref[...] = -ADAM_LR * (m_hat / (jnp.sqrt(v_hat) + ADAM_EPS) + ADAM_WD * w_ref[...])
                mo_ref[...] = mn
                vo_ref[...] = vn

    land_specs = [pl.BlockSpec((N_DEV, tm, C), lambda l, i, k=k: (0, jnp.where(l == k, i, 0), 0)) for k in range(L)]
    spec = pl.BlockSpec((None, tm, C), lambda l, i: (l, i, 0))
    return pl.pallas_call(
        body, name=name, grid=(L, R // tm),
        in_specs=land_specs + [spec] * 3,
        out_specs=[spec] * 4,
        out_shape=[jax.ShapeDtypeStruct((L, R, C), f32)] * 4,
        compiler_params=_cparams("arbitrary", "arbitrary"),
    )(*lands, w, m, v)


BIG = [("hyb_w_in", 2), ("hyb_w_out", 1), ("rec_w_in", 2), ("rec_w_out", 1), ("rec_w_a", 2), ("rec_w_x", 2),
       ("mlp_w1", 2), ("mlp_w2", 1)]
SMALL = [("hyb_conv_w", 2), ("rec_conv_w", 2), ("rec_conv_b", 1), ("rec_b_a", 1), ("rec_b_x", 1), ("rec_lambda", 1)]
REPL = ["hyb_sinks", "hyb_a_log", "hyb_dt_bias", "hyb_norm_w", "ln1_g", "ln1_b", "ln2_g", "ln2_b"]
WEIGHTS = ["hyb_w_in", "hyb_sinks", "hyb_conv_w", "hyb_a_log", "hyb_dt_bias", "hyb_norm_w", "hyb_w_out", "rec_w_in",
           "rec_conv_w", "rec_conv_b", "rec_w_a", "rec_b_a", "rec_w_x", "rec_b_x", "rec_lambda", "rec_w_out",
           "ln1_g", "ln1_b", "mlp_w1", "mlp_w2", "ln2_g", "ln2_b"]


def _pack_rows(parts, dtype, row_mult):
    lead = parts[0].shape[:-1]
    flat = jnp.concatenate([p.astype(dtype) for p in parts], axis=-1)
    n = flat.shape[-1]
    unit = row_mult * LANE
    pad = (-n) % unit
    if pad:
        flat = jnp.concatenate([flat, jnp.zeros(lead + (pad,), dtype)], axis=-1)
    return flat.reshape(lead + ((n + pad) // LANE, LANE))


def _gather_full(gathered, shard_shapes, table):
    flat = gathered.reshape(N_DEV, -1)
    out, off = {}, 0
    for name, ax in table:
        shp = shard_shapes[name]
        n = math.prod(shp)
        arr = flat[:, off:off + n].reshape((N_DEV,) + shp)
        off += n
        arr = jnp.moveaxis(arr, 0, ax)
        out[name] = arr.reshape(shp[:ax] + (N_DEV * shp[ax],) + shp[ax + 1:])
    return out


def _matmul_layouts(tag, gw):
    out = {}
    bw = D_MODEL // LRU_BLOCKS
    for k, g in gw.items():
        L = g.shape[1]
        if k == "hyb_w_in":
            out[k] = _merge_cols(f"{tag}_w_in_merge", g)
        elif k in ("hyb_w_out", "rec_w_out", "mlp_w2"):
            out[k] = jnp.swapaxes(g, 0, 1).reshape(L, N_DEV * g.shape[2], g.shape[3])
        elif k in ("rec_w_a", "rec_w_x"):
            out[k] = jnp.moveaxis(g, 0, 2).reshape(L, LRU_BLOCKS, bw, bw)
        else:
            out[k] = g
    return out


def kernel(x, hyb_w_in, hyb_sinks, hyb_conv_w, hyb_a_log, hyb_dt_bias, hyb_norm_w, hyb_w_out, rec_w_in, rec_conv_w, rec_conv_b, rec_w_a, rec_b_a, rec_w_x, rec_b_x, rec_lambda, rec_w_out, ln1_g, ln1_b, mlp_w1, mlp_w2, ln2_g, ln2_b, loss_target, m_hyb_w_in, m_hyb_sinks, m_hyb_conv_w, m_hyb_a_log, m_hyb_dt_bias, m_hyb_norm_w, m_hyb_w_out, m_rec_w_in, m_rec_conv_w, m_rec_conv_b, m_rec_w_a, m_rec_b_a, m_rec_w_x, m_rec_b_x, m_rec_lambda, m_rec_w_out, m_ln1_g, m_ln1_b, m_mlp_w1, m_mlp_w2, m_ln2_g, m_ln2_b, v_hyb_w_in, v_hyb_sinks, v_hyb_conv_w, v_hyb_a_log, v_hyb_dt_bias, v_hyb_norm_w, v_hyb_w_out, v_rec_w_in, v_rec_conv_w, v_rec_conv_b, v_rec_w_a, v_rec_b_a, v_rec_w_x, v_rec_b_x, v_rec_lambda, v_rec_w_out, v_ln1_g, v_ln1_b, v_mlp_w1, v_mlp_w2, v_ln2_g, v_ln2_b):
    args = locals()
    w = {k: args[k] for k in WEIGHTS}
    m = {k: args["m_" + k] for k in WEIGHTS}
    v = {k: args["v_" + k] for k in WEIGHTS}
    shard_shapes = {k: tuple(t.shape) for k, t in w.items()}
    xi, yi, ci = _my_coords()
    me = 4 * xi + 2 * yi + ci

    in_flight = {}

    def install(tag, names, got):
        for (k, i), arr in zip(names, _matmul_layouts(tag, {k: g for (k, _), g in zip(names, got)}).values()):
            W[k][i] = arr

    def start_gather(tag, names):
        srcs = [w[k][i:i + 1].astype(bf16) for k, i in names]
        *pending, zero = _push_start(f"gather_{tag}_start", "gather", srcs,
                                     [lax.empty((N_DEV,) + s.shape, bf16) for s in srcs])
        in_flight[tag] = (names, pending)
        return zero

    def finish_gather(tag, after):
        names, pending = in_flight.pop(tag)
        half = _push_wait(f"gather_{tag}_wait", "gather", *pending, after)
        install(tag, names, _pass_to_sibling(f"gather_{tag}_pass", half))

    def started(k, zero):
        W[k] = W[k] + zero

    def mixer_w(layer):
        return _layer_weights(layer)[:-2]

    def mlp_w(layer):
        return _layer_weights(layer)[-2:]

    gathered0 = _all_gather("gather_first", [w[k][i:i + 1].astype(bf16) for k, i in mixer_w(0)]
                            + [_pack_rows([w[k].reshape(-1) for k, _ in SMALL], f32, SUBLANE)])
    W = _gather_full(gathered0[-1], shard_shapes, SMALL)
    W.update({k: w[k] for k in REPL})
    W.update({k: {} for k, _ in BIG})
    install("l0a", mixer_w(0), gathered0[:-1])
    started("hyb_sinks", start_gather("l0b", mlp_w(0)) + start_gather("l1a", mixer_w(1)))

    def load_layer(layer, part, after):
        if part == "mixer":
            if layer > 0:
                finish_gather(f"l{layer}a", after)
            if 0 < layer < DEPTH - 1:
                started("hyb_sinks" if layer % 2 == 0 else "rec_conv_b",
                        start_gather(f"l{layer + 1}a", mixer_w(layer + 1)))
        else:
            finish_gather(f"l{layer}b", after)
            if layer < DEPTH - 1:
                started("ln2_g", start_gather(f"l{layer + 1}b", mlp_w(layer + 1)))

    grads_in_flight = {}

    def grads_ready(tag, a, b):
        g = {**a, **b}
        srcs = list(g.values())
        *pending, zero = _push_start(f"scatter_{tag}_start", "scatter", srcs, [lax.empty(s.shape, bf16) for s in srcs])
        grads_in_flight[tag] = (list(g.keys()), pending)
        return zero

    loss_local, grad_x, G = _local_step(x[0], loss_target[0], W, load_layer, grads_ready)
    loss = lax.psum(loss_local, MESH_AXES)

    landed = {}

    def land(tag, after):
        keys, pending = grads_in_flight[tag]
        landed.update(zip(keys, _push_wait(f"scatter_{tag}_wait", "scatter", *pending, after)))

    tags = list(grads_in_flight)
    for tag in tags[:-1]:
        land(tag, grad_x)
    rest = _pack_rows([G[k].reshape(-1) for k, _ in SMALL] + [G[k].reshape(-1) for k in REPL], f32, SUBLANE)
    g_rest = _sum_blocks("sum_rest", _all_gather("gather_rest", [rest])[0]).reshape(-1)

    grads, delta, new_m, new_v = {}, {}, {}, {}

    def adamw_big(k):
        shp = shard_shapes[k]
        s3 = (shp[0], math.prod(shp[1:-1]), shp[-1])
        lands = [landed[(k, i)].reshape((N_DEV,) + s3[1:]) for i in range(shp[0])]
        res = _adamw_land("adamw_" + k, lands, w[k].reshape(s3), m[k].reshape(s3), v[k].reshape(s3))
        grads[k], delta[k], new_m[k], new_v[k] = (r.reshape(shp) for r in res)

    late = {k for k, _ in grads_in_flight[tags[-1]][0]}
    for k in [k for k, _ in BIG if k not in late]:
        adamw_big(k)
        done = new_v[k]
    land(tags[-1], done)
    for k in [k for k, _ in BIG if k in late]:
        adamw_big(k)
    off = 0
    for k, ax in SMALL:
        full_shape = G[k].shape
        n = math.prod(full_shape)
        full = g_rest[off:off + n].reshape(full_shape)
        off += n
        s = shard_shapes[k][ax]
        grads[k] = lax.dynamic_slice_in_dim(full, me * s, s, axis=ax)
    for k in REPL:
        n = math.prod(shard_shapes[k])
        grads[k] = g_rest[off:off + n].reshape(shard_shapes[k])
        off += n

    for k in [k for k, _ in SMALL] + REPL:
        delta[k], new_m[k], new_v[k] = _adamw("adamw_" + k, w[k], grads[k], m[k], v[k])

    return (loss, grad_x[None], *[grads[k] for k in WEIGHTS], *[delta[k] for k in WEIGHTS],
            *[new_m[k] for k in WEIGHTS], *[new_v[k] for k in WEIGHTS])
```

```python
import functools
import math

import jax
import jax.numpy as jnp
from jax import lax
from jax.experimental import pallas as pl
from jax.experimental.pallas import tpu as pltpu

f32 = jnp.float32
bf16 = jnp.bfloat16

N_DEV = 8
D_MODEL = 1024
DEPTH = 4
A_HEAD_DIM = 64
A_Q_HEADS = 8
WINDOW = 128
ROPE_THETA = 10000.0
B_HEADS = 4
B_HEAD_DIM = 128
B_CHUNK = 64
LRU_BLOCKS = 4
LRU_C = 8.0
D_FF = 4 * D_MODEL
HYB_PROJ = 2824
HYB_PROJ_PAD = 3072
DN_ALPHA = (2 * DEPTH) ** 0.25
LN_EPS = 1e-5
NORM_EPS = 1e-6
ADAM_LR = 0.001
ADAM_B1 = 0.9
ADAM_B2 = 0.999
ADAM_EPS = 1e-08
ADAM_WD = 0.01
ADAM_STEP = 10

LANE = 128
SUBLANE = 8
VMEM_LIMIT = 48 * 1024 * 1024

CB_QA, CB_KA, CB_VA, CB_CONV, CB_Z, CB_LG = 0, 4, 5, 6, 18, 22

MESH_AXES = ("x", "y", "c")


def _cparams(*sem):
    return pltpu.CompilerParams(dimension_semantics=sem, vmem_limit_bytes=VMEM_LIMIT)


def _dot(a, b, dims, precision=None):
    return lax.dot_general(a, b, (dims, ((), ())), preferred_element_type=f32, precision=precision)


NN = ((1,), (0,))
NT = ((1,), (1,))
TN = ((0,), (0,))


def _mat_spec(arr, kind, lead, br, bc, rb, cb):
    if kind == "plain":
        return pl.BlockSpec((br, bc), lambda i, j, k: (rb(i, j, k), cb(i, j, k)))
    if kind == "lead":
        return pl.BlockSpec((None, br, bc), lambda i, j, k: (lead, rb(i, j, k), cb(i, j, k)))
    assert kind == "devcol" and bc == arr.shape[-1]
    return pl.BlockSpec((None, None, br, bc), lambda i, j, k: (cb(i, j, k), lead, rb(i, j, k), 0))


def _mm(name, a, b, mode, *, b_kind="plain", b_lead=0, o_kind="plain", epilogue=None, extras=(), params=(),
        out_dtypes=(f32,), tm=1024, tn=1024, tk=None):
    if tk is None:
        tk = 512 if mode == "tn" else 1024
    if b_kind in ("plain", "lead"):
        b_rows, b_cols = b.shape[-2:]
    else:
        b_rows, b_cols = b.shape[-2], N_DEV * b.shape[-1]
    if mode == "nn":
        (M, K), (K2, N) = a.shape, (b_rows, b_cols)
    elif mode == "nt":
        (M, K), (N, K2) = a.shape, (b_rows, b_cols)
    else:
        (K, M), (K2, N) = a.shape, (b_rows, b_cols)
    assert K == K2, (name, a.shape, b.shape, mode)
    tm, tn, tk = min(tm, M), min(tn, N), min(tk, K)
    cols_are_n = mode != "nt"
    if b_kind == "devcol":
        tn, tk = (b.shape[-1], tk) if cols_are_n else (tn, b.shape[-1])
    shard = N // N_DEV
    if o_kind == "devcol":
        tn = max(shard, tn // shard * shard)
    assert M % tm == 0 and N % tn == 0 and K % tk == 0, (name, M, N, K, tm, tn, tk)
    nk = K // tk
    dims = {"nn": NN, "nt": NT, "tn": TN}[mode]
    n_ex, n_out = len(extras) + len(params), len(out_dtypes)

    def body(*refs):
        a_ref, b_ref = refs[:2]
        ex = refs[2:2 + n_ex]
        outs = refs[2 + n_ex:2 + n_ex + n_out]
        acc = refs[-1]
        k = pl.program_id(2)

        @pl.when(k == 0)
        def _():
            acc[...] = jnp.zeros_like(acc)

        acc[...] += _dot(a_ref[...].astype(bf16), b_ref[...].astype(bf16), dims)

        @pl.when(k == nk - 1)
        def _():
            r = acc[...]
            res = epilogue(r, *[e[...] for e in ex]) if epilogue is not None else (r,)
            for o, v in zip(outs, res):
                if o_kind == "plain":
                    o[...] = v.astype(o.dtype)
                else:
                    for q in range(tn // shard):
                        o[q] = v[:, q * shard:(q + 1) * shard].astype(o.dtype)

    if mode == "tn":
        a_spec = pl.BlockSpec((tk, tm), lambda i, j, k: (k, i))
    else:
        a_spec = pl.BlockSpec((tm, tk), lambda i, j, k: (i, k))
    jb, kb = (lambda i, j, k: j), (lambda i, j, k: k)
    if mode == "nt":
        b_spec = _mat_spec(b, b_kind, b_lead, tn, tk, jb, kb)
    else:
        b_spec = _mat_spec(b, b_kind, b_lead, tk, tn, kb, jb)
    e_spec = pl.BlockSpec((tm, tn), lambda i, j, k: (i, j))
    if o_kind == "plain":
        o_spec, o_shape = e_spec, (M, N)
    else:
        o_spec, o_shape = pl.BlockSpec((tn // shard, tm, shard), lambda i, j, k: (j, i, 0)), (N_DEV, M, shard)
    res = pl.pallas_call(
        body, name=name,
        grid=(M // tm, N // tn, nk),
        in_specs=[a_spec, b_spec] + [e_spec] * len(extras)
        + [pl.BlockSpec(p.shape, lambda i, j, k: (0, 0)) for p in params],
        out_specs=[o_spec] * n_out,
        out_shape=[jax.ShapeDtypeStruct(o_shape, dt) for dt in out_dtypes],
        scratch_shapes=[pltpu.VMEM((tm, tn), f32)],
        compiler_params=_cparams("parallel", "parallel", "arbitrary"),
    )(a, b, *extras, *params)
    return res[0] if n_out == 1 else res


def _row_spec(tm, cb, width):
    assert (cb * LANE) % width == 0
    blk = (cb * LANE) // width
    return pl.BlockSpec((tm, width), lambda i: (i, blk))


def _whole_spec(p):
    nd = p.ndim
    return pl.BlockSpec(p.shape, lambda i: (0,) * nd)


def _tl_bwd(name, fn, rows, params, cot_rows, cot_fn=None, skip=(), bf16_copy=False, tm=512):
    T = rows[0][0].shape[0]
    tm = min(tm, T)
    nr, npar, nc = len(rows), len(params), len(cot_rows)
    keep = [k for k in range(nr) if k not in skip]
    n_rows = len(keep) + int(bf16_copy)
    row_dtypes = [(rows[k][2], f32) for k in keep] + ([(rows[keep[0]][2], bf16)] if bf16_copy else [])

    def body(*refs):
        vals = [r[...] for r in refs[:nr + npar]]
        cots = [r[...] for r in refs[nr + npar:nr + npar + nc]]
        outs = refs[nr + npar + nc:]
        cot = tuple(cot_fn(*cots)) if cot_fn is not None else tuple(cots)
        _, vjp = jax.vjp(fn, *vals)
        grads = vjp(cot)
        for o, k in zip(outs, keep):
            o[...] = grads[k].astype(o.dtype)
        if bf16_copy:
            outs[len(keep)][...] = grads[keep[0]].astype(bf16)
        i = pl.program_id(0)
        for o, g in zip(outs[n_rows:], grads[nr:]):
            @pl.when(i == 0)
            def _(o=o):
                o[...] = jnp.zeros_like(o)
            o[...] += g

    res = pl.pallas_call(
        body, name=name, grid=(T // tm,),
        in_specs=[_row_spec(tm, cb, w) for (_, cb, w) in rows] + [_whole_spec(p) for p in params]
        + [_row_spec(tm, cb, w) for (_, cb, w) in cot_rows],
        out_specs=[pl.BlockSpec((tm, w), lambda i: (i, 0)) for w, _ in row_dtypes] + [_whole_spec(p) for p in params],
        out_shape=[jax.ShapeDtypeStruct((T, w), dt) for w, dt in row_dtypes]
        + [jax.ShapeDtypeStruct(p.shape, f32) for p in params],
        compiler_params=_cparams("arbitrary"),
    )(*[r[0] for r in rows], *params, *[r[0] for r in cot_rows])
    return res[:n_rows], res[n_rows:]


def _ln_res_fn(x, mix, g, b):
    pre = DN_ALPHA * x + mix
    mu = jnp.mean(pre, axis=-1, keepdims=True)
    var = jnp.mean(jnp.square(pre - mu), axis=-1, keepdims=True)
    return ((pre - mu) * lax.rsqrt(var + LN_EPS) * g + b,)


@jax.custom_jvp
def _expm1(x):
    small = jnp.abs(x) < 0.3
    xs = jnp.where(small, x, 0.0)
    poly = xs * (1.0 + xs * (1 / 2 + xs * (1 / 6 + xs * (1 / 24 + xs * (1 / 120 + xs * (
        1 / 720 + xs * (1 / 5040 + xs * (1 / 40320 + xs * (1 / 362880)))))))))
    return jnp.where(small, poly, jnp.exp(x) - 1.0)


@_expm1.defjvp
def _expm1_jvp(primals, tangents):
    (x,), (t,) = primals, tangents
    return _expm1(x), t * jnp.exp(x)


def _rglru_pre_fn(pre_r, pre_i, xc, b_a, b_x, lam):
    r = jax.nn.sigmoid(pre_r + b_a)
    i = jax.nn.sigmoid(pre_i + b_x)
    log_a = -LRU_C * r * jax.nn.softplus(-lam)
    a = jnp.exp(log_a)
    b = jnp.sqrt(-_expm1(2.0 * log_a)) * (i * xc)
    return a, b


def _rec_gate_fn(h, gate):
    return (h * jax.nn.gelu(gate),)


def _loss_head(y, t, tm=512):
    T, Dm = y.shape
    tm = min(tm, T)

    def body(y_ref, t_ref, dy_ref, loss_ref):
        e = y_ref[...] - t_ref[...]
        dy_ref[...] = e * (1.0 / Dm)

        @pl.when(pl.program_id(0) == 0)
        def _():
            loss_ref[...] = jnp.zeros_like(loss_ref)

        loss_ref[...] += 0.5 * jnp.sum(jnp.mean(e * e, axis=-1, keepdims=True), axis=0, keepdims=True)

    dy, loss = pl.pallas_call(
        body, name="loss_head", grid=(T // tm,),
        in_specs=[pl.BlockSpec((tm, Dm), lambda i: (i, 0))] * 2,
        out_specs=[pl.BlockSpec((tm, Dm), lambda i: (i, 0)), pl.BlockSpec((SUBLANE, LANE), lambda i: (0, 0))],
        out_shape=[jax.ShapeDtypeStruct((T, Dm), f32), jax.ShapeDtypeStruct((SUBLANE, LANE), f32)],
        compiler_params=_cparams("arbitrary"),
    )(y, t)
    return loss[0, 0], dy


def _conv_fwd(name, x, cb0, nblk, w, bias, tm=2048):
    T = x.shape[0]
    tm = min(tm, T)
    hb = tm // SUBLANE
    has_b = bias is not None

    def body(*refs):
        cur, prev, w_ref = refs[:3]
        b_ref = refs[3] if has_b else None
        o = refs[-1]
        i = pl.program_id(1)
        p = jnp.where(i > 0, prev[...], 0.0)
        xcat = jnp.concatenate([p, cur[...]], axis=0)
        acc = cur[...] * w_ref[3:4, :]
        for j in range(3):
            acc = acc + pltpu.roll(xcat, 3 - j, axis=0)[SUBLANE:] * w_ref[j:j + 1, :]
        if has_b:
            acc = acc + b_ref[...]
        o[...] = acc

    in_specs = [
        pl.BlockSpec((tm, LANE), lambda c, i: (i, cb0 + c)),
        pl.BlockSpec((SUBLANE, LANE), lambda c, i: (jnp.maximum(i * hb - 1, 0), cb0 + c)),
        pl.BlockSpec((4, LANE), lambda c, i: (0, c)),
    ]
    args = [x, x, w]
    if has_b:
        in_specs.append(pl.BlockSpec((1, LANE), lambda c, i: (0, c)))
        args.append(bias)
    return pl.pallas_call(
        body, name=name, grid=(nblk, T // tm),
        in_specs=in_specs,
        out_specs=pl.BlockSpec((tm, LANE), lambda c, i: (i, c)),
        out_shape=jax.ShapeDtypeStruct((T, nblk * LANE), f32),
        compiler_params=_cparams("parallel", "parallel"),
    )(*args)


def _conv_bwd(name, dy, x, cb0, nblk, w, into, into_cb, tm=2048):
    T = x.shape[0]
    tm = min(tm, T)
    hb = tm // SUBLANE
    nt = T // tm

    def body(dcur, dnext, xcur, xprev, w_ref, _, dx_ref, dw_ref, db_ref):
        i = pl.program_id(1)
        d = dcur[...]
        dn = jnp.where(i < nt - 1, dnext[...], 0.0)
        dcat = jnp.concatenate([d, dn], axis=0)
        acc = d * w_ref[3:4, :]
        for j in range(3):
            s = 3 - j
            acc = acc + pltpu.roll(dcat, tm + SUBLANE - s, axis=0)[:tm] * w_ref[j:j + 1, :]
        dx_ref[...] = acc.astype(dx_ref.dtype)

        p = jnp.where(i > 0, xprev[...], 0.0)
        xcat = jnp.concatenate([p, xcur[...]], axis=0)
        rows = [jnp.sum(d * pltpu.roll(xcat, 3 - j, axis=0)[SUBLANE:], axis=0, keepdims=True) for j in range(3)]
        rows.append(jnp.sum(d * xcur[...], axis=0, keepdims=True))
        rows.append(jnp.zeros((SUBLANE - 4, LANE), f32))

        @pl.when(i == 0)
        def _():
            dw_ref[...] = jnp.zeros_like(dw_ref)
            db_ref[...] = jnp.zeros_like(db_ref)

        dw_ref[...] += jnp.concatenate(rows, axis=0)
        db_ref[...] += jnp.broadcast_to(jnp.sum(d, axis=0, keepdims=True), (SUBLANE, LANE))

    nh = T // SUBLANE
    dx, dw, db = pl.pallas_call(
        body, name=name, grid=(nblk, nt),
        in_specs=[
            pl.BlockSpec((tm, LANE), lambda c, i: (i, c)),
            pl.BlockSpec((SUBLANE, LANE), lambda c, i: (jnp.minimum((i + 1) * hb, nh - 1), c)),
            pl.BlockSpec((tm, LANE), lambda c, i: (i, cb0 + c)),
            pl.BlockSpec((SUBLANE, LANE), lambda c, i: (jnp.maximum(i * hb - 1, 0), cb0 + c)),
            pl.BlockSpec((4, LANE), lambda c, i: (0, c)),
            pl.BlockSpec(memory_space=pl.ANY),
        ],
        out_specs=[
            pl.BlockSpec((tm, LANE), lambda c, i: (i, into_cb + c)),
            pl.BlockSpec((SUBLANE, LANE), lambda c, i: (0, c)),
            pl.BlockSpec((SUBLANE, LANE), lambda c, i: (0, c)),
        ],
        out_shape=[jax.ShapeDtypeStruct(into.shape, into.dtype),
                   jax.ShapeDtypeStruct((SUBLANE, nblk * LANE), f32),
                   jax.ShapeDtypeStruct((SUBLANE, nblk * LANE), f32)],
        input_output_aliases={5: 0},
        compiler_params=_cparams("parallel", "arbitrary"),
    )(dy, dy, x, x, w, into)
    return dx, dw[:4], db[0]


@functools.partial(jax.custom_vjp, nondiff_argnums=(1,))
def _lroll(x, s):
    return pltpu.roll(x, s, axis=1)


def _lroll_fwd(x, s):
    return _lroll(x, s), None


def _lroll_bwd(s, _, g):
    return (_lroll(g, (LANE - s) % LANE),)


_lroll.defvjp(_lroll_fwd, _lroll_bwd)


def _rope_tables(T):
    half = A_HEAD_DIM // 2
    inv_freq = ROPE_THETA ** (-jnp.arange(half, dtype=f32) / half)
    ang = jnp.arange(T, dtype=f32)[:, None] * inv_freq[None, :]
    cos, sin = jnp.cos(ang), jnp.sin(ang)
    return jnp.tile(jnp.concatenate([cos, cos], axis=1), (1, 2)), jnp.tile(jnp.concatenate([-sin, sin], axis=1), (1, 2))


def _attn_block_fn(n, q, kp, kc, vp, vc, cq, sq, cp, sp, sinks):
    W = WINDOW
    lane = lax.broadcasted_iota(jnp.int32, (W, LANE), 1)
    lo_half = (lane % A_HEAD_DIM) < (A_HEAD_DIM // 2)
    lane8 = lax.broadcasted_iota(jnp.int32, sinks.shape, 1)

    def rope(x, c, s):
        return x * c + jnp.where(lo_half, _lroll(x, LANE - A_HEAD_DIM // 2), _lroll(x, A_HEAD_DIM // 2)) * s

    k2 = jnp.concatenate([rope(kp, cp, sp), rope(kc, cq, sq)], axis=0).astype(bf16)
    v2 = jnp.concatenate([vp, vc], axis=0).astype(bf16)
    qs = []
    for t in range(4):
        qt = rope(q[:, LANE * t:LANE * (t + 1)], cq, sq)
        g = t // 2
        for hh in range(2):
            qa = jnp.where((lane // A_HEAD_DIM) == hh, qt, 0.0)
            qs.append(_lroll(qa, A_HEAD_DIM) if hh != g else qa)
    s_all = _dot(jnp.concatenate(qs, axis=0).astype(bf16), k2, NT) * (A_HEAD_DIM ** -0.5)
    row = lax.broadcasted_iota(jnp.int32, (W, 2 * W), 0)
    col = lax.broadcasted_iota(jnp.int32, (W, 2 * W), 1)
    dist = row + W - col
    mask = (dist >= 0) & (dist < W) & ((col >= W) | (n > 0))
    ps = []
    for j in range(A_Q_HEADS):
        s = jnp.where(mask, s_all[W * j:W * (j + 1)], -jnp.inf)
        sink = jnp.sum(jnp.where(lane8 == j, sinks, 0.0), axis=1, keepdims=True)
        m = jnp.maximum(jnp.max(s, axis=-1, keepdims=True), sink)
        e = jnp.exp(s - m)
        ps.append((e / (jnp.sum(e, axis=-1, keepdims=True) + jnp.exp(sink - m))).astype(bf16))
    o = _dot(jnp.concatenate(ps, axis=0), v2, NN)
    outs = []
    for t in range(4):
        g = t // 2
        ot = jnp.zeros((W, LANE), f32)
        for hh in range(2):
            j = 2 * t + hh
            oj = jnp.where((lane // A_HEAD_DIM) == g, o[W * j:W * (j + 1)], 0.0)
            ot = ot + (_lroll(oj, A_HEAD_DIM) if hh != g else oj)
        outs.append(ot)
    return jnp.concatenate(outs, axis=1)


def _attn_specs():
    W = WINDOW
    prev = lambda n: jnp.maximum(n - 1, 0)
    return [
        pl.BlockSpec((W, 4 * LANE), lambda n: (n, CB_QA // 4)),
        pl.BlockSpec((W, LANE), lambda n: (prev(n), CB_KA)),
        pl.BlockSpec((W, LANE), lambda n: (n, CB_KA)),
        pl.BlockSpec((W, LANE), lambda n: (prev(n), CB_VA)),
        pl.BlockSpec((W, LANE), lambda n: (n, CB_VA)),
        pl.BlockSpec((W, LANE), lambda n: (n, 0)),
        pl.BlockSpec((W, LANE), lambda n: (n, 0)),
        pl.BlockSpec((W, LANE), lambda n: (prev(n), 0)),
        pl.BlockSpec((W, LANE), lambda n: (prev(n), 0)),
        pl.BlockSpec((1, A_Q_HEADS), lambda n: (0, 0)),
    ]


def _attn_fwd(name, proj, cos, sin, sinks):
    T = proj.shape[0]
    W = WINDOW

    def body(*refs):
        o = refs[-1]
        o[...] = _attn_block_fn(pl.program_id(0), *[r[...] for r in refs[:-1]]).astype(o.dtype)

    return pl.pallas_call(
        body, name=name, grid=(T // W,),
        in_specs=_attn_specs(),
        out_specs=pl.BlockSpec((W, 4 * LANE), lambda n: (n, 0)),
        out_shape=jax.ShapeDtypeStruct((T, 2 * 4 * LANE), bf16),
        compiler_params=_cparams("parallel"),
    )(proj, proj, proj, proj, proj, cos, sin, cos, sin, sinks)


def _attn_bwd(name, proj, cos, sin, sinks, d_oab):
    T = proj.shape[0]
    W = WINDOW
    Q = 4 * LANE
    nb = T // W

    def body(*refs):
        ins = [r[...] for r in refs[:10]]
        do = refs[10][...]
        out_ref, ds_ref, d_ref = refs[11:]
        n = pl.program_id(0)
        _, vjp = jax.vjp(functools.partial(_attn_block_fn, n), *ins)
        dq, dkp, dkc, dvp, dvc, _, _, _, _, dsk = vjp(do)

        @pl.when(n == 0)
        def _():
            d_ref[:, Q:] = jnp.zeros((T, 2 * LANE), f32)
            ds_ref[...] = jnp.zeros_like(ds_ref)

        cur = pl.ds(pl.multiple_of(n * W, W), W)
        d_ref[cur, :Q] = dq
        d_ref[cur, Q:Q + LANE] += dkc
        d_ref[cur, Q + LANE:] += dvc
        ds_ref[...] += dsk

        @pl.when(n > 0)
        def _():
            prv = pl.ds(pl.multiple_of((n - 1) * W, W), W)
            d_ref[prv, Q:Q + LANE] += dkp
            d_ref[prv, Q + LANE:] += dvp

        @pl.when(n == nb - 1)
        def _():
            out_ref[...] = d_ref[...].astype(out_ref.dtype)

    return pl.pallas_call(
        body, name=name, grid=(nb,),
        in_specs=_attn_specs() + [pl.BlockSpec((W, Q), lambda n: (n, 0))],
        out_specs=[pl.BlockSpec((T, Q + 2 * LANE), lambda n: (0, 0)),
                   pl.BlockSpec((1, A_Q_HEADS), lambda n: (0, 0))],
        out_shape=[jax.ShapeDtypeStruct((T, HYB_PROJ_PAD), bf16), jax.ShapeDtypeStruct((1, A_Q_HEADS), f32)],
        scratch_shapes=[pltpu.VMEM((T, Q + 2 * LANE), f32)],
        compiler_params=_cparams("arbitrary"),
    )(proj, proj, proj, proj, proj, cos, sin, cos, sin, sinks, d_oab)


def _bdot(spec, a, b, precision=None):
    return jnp.einsum(spec, a, b, preferred_element_type=f32, precision=precision)


@jax.custom_vjp
def _tri_inv(a):
    H, C, _ = a.shape
    B = 2 * SUBLANE
    nb = C // B
    r = lax.broadcasted_iota(jnp.int32, (C, C), 0)
    c = lax.broadcasted_iota(jnp.int32, (C, C), 1)
    a4 = jnp.where((r // B) == (c // B), a, 0.0).reshape(H, nb, B, C)
    t4 = jnp.broadcast_to(jnp.where(r == c, 1.0, 0.0).astype(f32), a.shape).reshape(H, nb, B, C)
    for j in range(B - 1):
        col = jnp.concatenate([a4[:, b:b + 1, :, B * b + j:B * b + j + 1] for b in range(nb)], axis=1)
        t4 = t4 - col * t4[:, :, j:j + 1, :]
    x = t4.reshape(H, C, C)
    hi = lax.Precision.HIGH
    while B < C:
        m = jnp.where(((r // (2 * B)) == (c // (2 * B))) & ((r // B) > (c // B)), a, 0.0)
        x = x - _bdot("hij,hjk->hik", x, _bdot("hij,hjk->hik", m, x, precision=hi), precision=hi)
        B *= 2
    return x


def _tri_inv_fwd(a):
    t = _tri_inv(a)
    return t, t


def _tri_inv_bwd(t, g):
    C = t.shape[-1]
    r = lax.broadcasted_iota(jnp.int32, (C, C), 0)
    c = lax.broadcasted_iota(jnp.int32, (C, C), 1)
    x = _bdot("hki,hkj->hij", t, g, precision=lax.Precision.HIGHEST)
    y = _bdot("hik,hjk->hij", x, t, precision=lax.Precision.HIGHEST)
    return (jnp.where(r > c, -y, 0.0),)


_tri_inv.defvjp(_tri_inv_fwd, _tri_inv_bwd)


@jax.custom_vjp
def _tri_inv_saved(a, t):
    return t


_tri_inv_saved.defvjp(lambda a, t: (t, t), lambda t, g: (_tri_inv_bwd(t, g)[0], jnp.zeros_like(t)))


def _silu(x):
    return x * jax.nn.sigmoid(x)


def _l2n(x):
    return x * lax.rsqrt(jnp.sum(x * x, axis=-1, keepdims=True) + NORM_EPS)


def _delta_chunk_fn(cq, ck, cv, z, lg, a_log, dt_bias, norm_w, S, t_saved=None, want_t=False):
    C = B_CHUNK
    lane = lax.broadcasted_iota(jnp.int32, (C, LANE), 1)
    pick = lambda l0: jnp.concatenate(
        [jnp.sum(jnp.where(lane == l0 + h, lg, 0.0), axis=1, keepdims=True)[None] for h in range(B_HEADS)], axis=0)
    bl, al = pick(0), pick(B_HEADS)
    q = _l2n(_silu(cq)) * (B_HEAD_DIM ** -0.5)
    k = _l2n(_silu(ck))
    v = _silu(cv)
    beta = jax.nn.sigmoid(bl)
    g = -jnp.exp(a_log) * jax.nn.softplus(al + dt_bias)
    r = lax.broadcasted_iota(jnp.int32, (C, C), 0)
    c = lax.broadcasted_iota(jnp.int32, (C, C), 1)
    eye = r == c
    g_row = jnp.sum(jnp.where(eye, g, 0.0), axis=1, keepdims=True)
    gc = jnp.sum(jnp.where(c <= r, g_row, 0.0), axis=2, keepdims=True)
    gc_row = jnp.sum(jnp.where(eye, gc, 0.0), axis=1, keepdims=True)
    decay_incl = jnp.exp(jnp.where(r >= c, gc - gc_row, -jnp.inf))
    decay_strict = jnp.where(r > c, decay_incl, 0.0)
    kb = k * beta
    vb = v * beta
    kbf = k.astype(bf16)
    a_mat = _bdot("hik,hjk->hij", kb.astype(bf16), kbf) * decay_strict
    t_f32 = _tri_inv(a_mat) if t_saved is None else _tri_inv_saved(a_mat, t_saved)
    t_mat = t_f32.astype(bf16)
    eg = jnp.exp(gc)
    u = _bdot("hij,hjv->hiv", t_mat, vb.astype(bf16))
    w = _bdot("hij,hjk->hik", t_mat, (kb * eg).astype(bf16))
    qk = _bdot("hik,hjk->hij", q.astype(bf16), kbf) * decay_incl
    g_last = jnp.sum(g, axis=1, keepdims=True)
    k_tail = k * jnp.exp(g_last - gc)
    Sb = S.astype(bf16)
    v_new = u - _bdot("hck,hkv->hcv", w.astype(bf16), Sb)
    o = _bdot("hck,hkv->hcv", (q * eg).astype(bf16), Sb) + _bdot("hij,hjv->hiv", qk.astype(bf16), v_new.astype(bf16))
    S_new = S * jnp.exp(g_last) + _bdot("hck,hcv->hkv", k_tail.astype(bf16), v_new.astype(bf16))
    ob = o * lax.rsqrt(jnp.mean(o * o, axis=-1, keepdims=True) + NORM_EPS) * norm_w
    return (ob * _silu(z), S_new) + ((t_f32,) if want_t else ())


DELTA_CHUNKS_PER_STEP = 8


def _delta_in_specs(rev, N):
    C = DELTA_CHUNKS_PER_STEP * B_CHUNK
    ix = (lambda n: N - 1 - n) if rev else (lambda n: n)
    specs = [pl.BlockSpec((C, 3 * B_HEADS * LANE), lambda n: (ix(n), 0))]
    specs += [pl.BlockSpec((C, LANE), lambda n, h=h: (ix(n), CB_Z + h)) for h in range(B_HEADS)]
    specs += [
        pl.BlockSpec((C, LANE), lambda n: (ix(n), CB_LG)),
        pl.BlockSpec((B_HEADS, 1, 1), lambda n: (0, 0, 0)),
        pl.BlockSpec((B_HEADS, 1, 1), lambda n: (0, 0, 0)),
        pl.BlockSpec((1, LANE), lambda n: (0, 0)),
    ]
    return specs


def _delta_inputs(u, c_ref, z_refs, lg, al, dt, nw):
    H = B_HEADS
    rows = slice(u * B_CHUNK, (u + 1) * B_CHUNK)
    part = lambda p: jnp.stack([c_ref[rows, LANE * (p * H + h):LANE * (p * H + h + 1)] for h in range(H)])
    return (part(0), part(1), part(2), jnp.stack([z[rows, :] for z in z_refs]), lg[rows, :], al[...], dt[...], nw[...])


def _delta_fwd(name, c, proj, a_log, dt_bias, norm_w, o_ab):
    T = c.shape[0]
    C = B_CHUNK
    N = T // C
    Dh = B_HEAD_DIM
    H = B_HEADS

    def body(*refs):
        c_ref, z_refs, (lg, al, dt, nw) = refs[0], refs[1:1 + H], refs[1 + H:5 + H]
        o_ref, s_ref, t_ref, S = refs[6 + H:]

        @pl.when(pl.program_id(0) == 0)
        def _():
            S[...] = jnp.zeros_like(S)

        s = S[...]
        for u in range(U):
            s_ref[:, u] = s
            ob, s, t = _delta_chunk_fn(*_delta_inputs(u, c_ref, z_refs, lg, al, dt, nw), s, want_t=True)
            for h in range(H):
                o_ref[u * C:(u + 1) * C, LANE * h:LANE * (h + 1)] = ob[h].astype(o_ref.dtype)
            t_ref[:, u] = t
        S[...] = s

    U = DELTA_CHUNKS_PER_STEP
    return pl.pallas_call(
        body, name=name, grid=(N // U,),
        in_specs=_delta_in_specs(False, N // U) + [pl.BlockSpec(memory_space=pl.ANY)],
        out_specs=[pl.BlockSpec((U * C, H * LANE), lambda n: (n, 1)),
                   pl.BlockSpec((H, U, Dh, Dh), lambda n: (0, n, 0, 0)),
                   pl.BlockSpec((H, U, C, C), lambda n: (0, n, 0, 0))],
        out_shape=[jax.ShapeDtypeStruct(o_ab.shape, o_ab.dtype), jax.ShapeDtypeStruct((H, N, Dh, Dh), f32),
                   jax.ShapeDtypeStruct((H, N, C, C), f32)],
        input_output_aliases={5 + H: 0},
        scratch_shapes=[pltpu.VMEM((H, Dh, Dh), f32)],
        compiler_params=_cparams("arbitrary"),
    )(c, *([proj] * H), proj, a_log, dt_bias, norm_w, o_ab)


def _delta_bwd(name, c, proj, a_log, dt_bias, norm_w, s_saved, t_saved, d_oab, dproj):
    T = c.shape[0]
    C = B_CHUNK
    N = T // C
    Dh = B_HEAD_DIM
    H = B_HEADS

    def body(*refs):
        c_ref, z_refs, (lg, al, dt, nw) = refs[0], refs[1:1 + H], refs[1 + H:5 + H]
        s_ref, t_ref, do_ref = refs[5 + H:8 + H]
        dc, dtail, dal, ddt, dnw, dS = refs[9 + H:]

        @pl.when(pl.program_id(0) == 0)
        def _():
            dS[...] = jnp.zeros_like(dS)
            dal[...] = jnp.zeros_like(dal)
            ddt[...] = jnp.zeros_like(ddt)
            dnw[...] = jnp.zeros_like(dnw)

        ds = dS[...]
        for u in reversed(range(U)):
            rows = slice(u * C, (u + 1) * C)
            _, vjp = jax.vjp(functools.partial(_delta_chunk_fn, t_saved=t_ref[:, u]),
                             *_delta_inputs(u, c_ref, z_refs, lg, al, dt, nw), s_ref[:, u])
            do = jnp.stack([do_ref[rows, LANE * h:LANE * (h + 1)] for h in range(H)])
            g = vjp((do, ds))
            for h in range(H):
                for p in range(3):
                    dc[rows, LANE * (p * H + h):LANE * (p * H + h + 1)] = g[p][h]
                dtail[rows, LANE * h:LANE * (h + 1)] = g[3][h].astype(dtail.dtype)
            dtail[rows, LANE * H:LANE * (H + 1)] = g[4].astype(dtail.dtype)
            dtail[rows, LANE * (H + 1):] = jnp.zeros((C, LANE), dtail.dtype)
            dal[...] += g[5]
            ddt[...] += g[6]
            dnw[...] += g[7]
            ds = g[8]
        dS[...] = ds

    U = DELTA_CHUNKS_PER_STEP
    NB = N // U
    rn = lambda n: NB - 1 - n
    return pl.pallas_call(
        body, name=name, grid=(NB,),
        in_specs=_delta_in_specs(True, NB) + [
            pl.BlockSpec((H, U, Dh, Dh), lambda n: (0, rn(n), 0, 0)),
            pl.BlockSpec((H, U, C, C), lambda n: (0, rn(n), 0, 0)),
            pl.BlockSpec((U * C, H * LANE), lambda n: (rn(n), 1)),
            pl.BlockSpec(memory_space=pl.ANY),
        ],
        out_specs=[
            pl.BlockSpec((U * C, 3 * H * LANE), lambda n: (rn(n), 0)),
            pl.BlockSpec((U * C, (H + 2) * LANE), lambda n: (rn(n), CB_Z // (H + 2))),
            pl.BlockSpec((H, 1, 1), lambda n: (0, 0, 0)),
            pl.BlockSpec((H, 1, 1), lambda n: (0, 0, 0)),
            pl.BlockSpec((1, LANE), lambda n: (0, 0)),
        ],
        out_shape=[jax.ShapeDtypeStruct((T, 3 * H * Dh), f32), jax.ShapeDtypeStruct(dproj.shape, dproj.dtype),
                   jax.ShapeDtypeStruct((H, 1, 1), f32), jax.ShapeDtypeStruct((H, 1, 1), f32),
                   jax.ShapeDtypeStruct((1, LANE), f32)],
        input_output_aliases={8 + H: 1},
        scratch_shapes=[pltpu.VMEM((H, Dh, Dh), f32)],
        compiler_params=_cparams("arbitrary"),
    )(c, *([proj] * H), proj, a_log, dt_bias, norm_w, s_saved, t_saved, d_oab, dproj)


def _gate_matmuls(xc, wa_ref, wx_ref):
    bw = wa_ref.shape[-1]
    xb = xc.astype(bf16)
    blocks = [xb[:, bw * h:bw * (h + 1)] for h in range(LRU_BLOCKS)]
    return (jnp.concatenate([_dot(blocks[h], wa_ref[h], NN) for h in range(LRU_BLOCKS)], axis=1),
            jnp.concatenate([_dot(blocks[h], wx_ref[h], NN) for h in range(LRU_BLOCKS)], axis=1))


def _gates_fwd(name, xc, w_a, w_x, pars, tm=512):
    T, Wd = xc.shape
    tm = min(tm, T)

    def body(x_ref, wa_ref, wx_ref, ba, bx, lam, a_ref, b_ref):
        x = x_ref[...]
        pr, pi = _gate_matmuls(x, wa_ref, wx_ref)
        a_ref[...], b_ref[...] = _rglru_pre_fn(pr, pi, x, ba[...], bx[...], lam[...])

    row = pl.BlockSpec((tm, Wd), lambda i: (i, 0))
    return pl.pallas_call(
        body, name=name, grid=(T // tm,),
        in_specs=[row, _whole_spec(w_a), _whole_spec(w_x)] + [_whole_spec(p) for p in pars],
        out_specs=[row, row], out_shape=[jax.ShapeDtypeStruct((T, Wd), f32)] * 2,
        compiler_params=_cparams("parallel"),
    )(xc, w_a, w_x, *pars)


def _gates_bwd(name, xc, w_a, w_x, pars, lam_t, h_prev, tm=512):
    T, Wd = xc.shape
    tm = min(tm, T)
    bw = Wd // LRU_BLOCKS

    def body(x_ref, wa_ref, wx_ref, ba, bx, lam, lt_ref, hp_ref, dx_ref, dr_ref, di_ref, dba, dbx, dlam):
        x = x_ref[...]
        pr, pi = _gate_matmuls(x, wa_ref, wx_ref)
        _, vjp = jax.vjp(_rglru_pre_fn, pr, pi, x, ba[...], bx[...], lam[...])
        lt = lt_ref[...]
        dpr, dpi, dxc, g_ba, g_bx, g_lam = vjp((lt * hp_ref[...], lt))
        dprb, dpib = dpr.astype(bf16), dpi.astype(bf16)
        dx_ref[...] = dxc + jnp.concatenate(
            [_dot(dprb[:, bw * h:bw * (h + 1)], wa_ref[h], NT) + _dot(dpib[:, bw * h:bw * (h + 1)], wx_ref[h], NT)
             for h in range(LRU_BLOCKS)], axis=1)
        dr_ref[...] = dprb
        di_ref[...] = dpib

        @pl.when(pl.program_id(0) == 0)
        def _():
            dba[...] = jnp.zeros_like(dba)
            dbx[...] = jnp.zeros_like(dbx)
            dlam[...] = jnp.zeros_like(dlam)

        dba[...] += g_ba
        dbx[...] += g_bx
        dlam[...] += g_lam

    row = pl.BlockSpec((tm, Wd), lambda i: (i, 0))
    vec = pl.BlockSpec((1, Wd), lambda i: (0, 0))
    return pl.pallas_call(
        body, name=name, grid=(T // tm,),
        in_specs=[row, _whole_spec(w_a), _whole_spec(w_x)] + [_whole_spec(p) for p in pars] + [row, row],
        out_specs=[row, row, row, vec, vec, vec],
        out_shape=[jax.ShapeDtypeStruct((T, Wd), f32), jax.ShapeDtypeStruct((T, Wd), bf16),
                   jax.ShapeDtypeStruct((T, Wd), bf16)] + [jax.ShapeDtypeStruct((1, Wd), f32)] * 3,
        compiler_params=_cparams("arbitrary"),
    )(xc, w_a, w_x, *pars, lam_t, h_prev)


def _blockdiag_bwd_dw(name, xc, dpr, dpi, tk=512):
    T, Wd = xc.shape
    bw = Wd // LRU_BLOCKS
    tk = min(tk, T)

    def body(x_ref, dr, di, oa, ox):
        @pl.when(pl.program_id(1) == 0)
        def _():
            oa[...] = jnp.zeros_like(oa)
            ox[...] = jnp.zeros_like(ox)

        xb = x_ref[...].astype(bf16)
        oa[...] += _dot(xb, dr[...].astype(bf16), TN)
        ox[...] += _dot(xb, di[...].astype(bf16), TN)

    xs = pl.BlockSpec((tk, bw), lambda h, k: (k, h))
    ws = pl.BlockSpec((None, bw, bw), lambda h, k: (h, 0, 0))
    return pl.pallas_call(
        body, name=name, grid=(LRU_BLOCKS, T // tk), in_specs=[xs, xs, xs], out_specs=[ws, ws],
        out_shape=[jax.ShapeDtypeStruct((LRU_BLOCKS, bw, bw), f32)] * 2,
        compiler_params=_cparams("parallel", "arbitrary"),
    )(xc, dpr, dpi)


def _scan(name, a, proj, reverse, b=None, h=None, dhg=None, tt=512, cb=512):
    T, Wd = a.shape
    tt, cb = min(tt, T), min(cb, Wd)
    nt = T // tt
    ng = tt // SUBLANE

    def body(a_ref, g_ref, *rest):
        n_in = 2 if reverse else 1
        ins, outs, (carry, carry_a) = rest[:n_in], rest[n_in:-2], rest[-2:]

        @pl.when(pl.program_id(1) == 0)
        def _():
            carry[...] = jnp.zeros_like(carry)
            carry_a[...] = jnp.zeros_like(carry_a)

        row = lax.broadcasted_iota(jnp.int32, (SUBLANE, cb), 0)

        def step(gi, c):
            hp, ap = c
            g = (ng - 1 - gi) if reverse else gi
            rows = pl.ds(pl.multiple_of(g * SUBLANE, SUBLANE), SUBLANE)
            A = a_ref[rows, :]
            gate = g_ref[rows, :]
            a_first = jnp.broadcast_to(A[0:1, :], (SUBLANE, cb))
            if reverse:
                _, vjp = jax.vjp(_rec_gate_fn, ins[0][rows, :], gate)
                B, dgate = vjp((ins[1][rows, :],))
                outs[1][rows, :] = dgate
                A = jnp.where(row == SUBLANE - 1, ap, pltpu.roll(A, SUBLANE - 1, axis=0))
            else:
                B = ins[0][rows, :]
            for s in (1, 2, 4):
                sh = (SUBLANE - s) if reverse else s
                As = pltpu.roll(A, sh, axis=0)
                Bs = pltpu.roll(B, sh, axis=0)
                valid = (row < SUBLANE - s) if reverse else (row >= s)
                B = jnp.where(valid, A * Bs + B, B)
                A = jnp.where(valid, A * As, A)
            hcur = A * hp + B
            outs[0][rows, :] = hcur
            if not reverse:
                outs[1][rows, :] = jnp.where(row == 0, hp, pltpu.roll(hcur, 1, axis=0))
                outs[2][rows, :] = _rec_gate_fn(hcur, gate)[0]
            edge = hcur[0:1, :] if reverse else hcur[SUBLANE - 1:SUBLANE, :]
            return jnp.broadcast_to(edge, (SUBLANE, cb)), a_first

        carry[...], carry_a[...] = lax.fori_loop(0, ng, step, (carry[...], carry_a[...]))

    nc = Wd // cb
    tok = (lambda i: nt - 1 - i) if reverse else (lambda i: i)
    spec = pl.BlockSpec((tt, cb), lambda c, i: (tok(i), c))
    gate_half = pl.BlockSpec((tt, cb), lambda c, i: (tok(i), nc + c))
    if reverse:
        args, out_specs = (a, proj, h, dhg), [spec, gate_half]
        out_shape = [jax.ShapeDtypeStruct((T, Wd), f32), jax.ShapeDtypeStruct((T, 2 * Wd), f32)]
    else:
        args, out_specs = (a, proj, b), [spec] * 3
        out_shape = [jax.ShapeDtypeStruct((T, Wd), f32)] * 3
    return pl.pallas_call(
        body, name=name, grid=(nc, nt), in_specs=[spec, gate_half] + [spec] * (len(args) - 2), out_specs=out_specs,
        out_shape=out_shape,
        scratch_shapes=[pltpu.VMEM((SUBLANE, cb), f32), pltpu.VMEM((SUBLANE, cb), f32)],
        compiler_params=_cparams("parallel", "arbitrary"),
    )(*args)


def _relu2_epilogue(r):
    h = jnp.maximum(r, 0.0)
    return r, h * h


def _drelu2_epilogue(r, a):
    return (r * (2.0 * jnp.maximum(a.astype(f32), 0.0)),)


def _residual_cot(through, upper):
    return (through + DN_ALPHA * upper,)


def _merge_cols(name, g, tm=256):
    _, L, R, s = g.shape

    def body(g_ref, o_ref):
        for d in range(N_DEV):
            o_ref[:, s * d:s * (d + 1)] = g_ref[d].astype(bf16)
        o_ref[:, N_DEV * s:] = jnp.zeros((tm, HYB_PROJ_PAD - N_DEV * s), bf16)

    return pl.pallas_call(
        body, name=name, grid=(L, R // tm),
        in_specs=[pl.BlockSpec((N_DEV, None, tm, s), lambda l, i: (0, l, i, 0))],
        out_specs=pl.BlockSpec((None, tm, HYB_PROJ_PAD), lambda l, i: (l, i, 0)),
        out_shape=jax.ShapeDtypeStruct((L, R, HYB_PROJ_PAD), bf16),
        compiler_params=_cparams("parallel", "parallel"),
    )(g)


def _split_cols(name, dw, tm=256):
    R = dw.shape[0]
    s = HYB_PROJ // N_DEV

    def body(g_ref, o_ref):
        for d in range(N_DEV):
            o_ref[d] = g_ref[:, s * d:s * (d + 1)].astype(bf16)

    return pl.pallas_call(
        body, name=name, grid=(R // tm,),
        in_specs=[pl.BlockSpec((tm, HYB_PROJ_PAD), lambda i: (i, 0))],
        out_specs=pl.BlockSpec((N_DEV, tm, s), lambda i: (0, i, 0)),
        out_shape=jax.ShapeDtypeStruct((N_DEV, R, s), bf16),
        compiler_params=_cparams("parallel"),
    )(dw)


def _rows_to_dev(dw):
    nb, r, c = dw.shape
    t = dw.reshape(nb, N_DEV, r // N_DEV, c)
    return jnp.moveaxis(t, 1, 0).reshape(N_DEV, nb * (r // N_DEV), c).astype(bf16)


def _ln_epilogue(r, x, g, b):
    y = _ln_res_fn(x, r, g, b)[0]
    return r, y, y


def _hybrid_fwd(tag, x, xb, W, j, cos, sin, ln):
    proj = _mm(f"{tag}_proj", xb, W["hyb_w_in"][j], "nn", b_kind="lead", b_lead=0)
    o_a = _attn_fwd(f"{tag}_attn", proj, cos, sin, W["hyb_sinks"][j][None, :])
    c = _conv_fwd(f"{tag}_conv", proj, CB_CONV, 12, W["hyb_conv_w"][j], None)
    o_ab, s_saved, t_saved = _delta_fwd(f"{tag}_delta", c, proj, W["hyb_a_log"][j].reshape(B_HEADS, 1, 1),
                                        W["hyb_dt_bias"][j].reshape(B_HEADS, 1, 1), W["hyb_norm_w"][j][None, :], o_a)
    mix, x1, x1b = _mm(f"{tag}_out", o_ab, W["hyb_w_out"][j], "nn", b_kind="lead", b_lead=0, epilogue=_ln_epilogue,
                       extras=(x,), params=ln, out_dtypes=(f32, f32, bf16), tm=512)
    return mix, x1, x1b, (proj, c, s_saved, t_saved, o_ab)


def _hybrid_bwd(tag, x, dmix, addend, W, j, cos, sin, saved, G, send_early):
    proj, c, s_saved, t_saved, o_ab = saved
    T = x.shape[0]
    d_oab = _mm(f"{tag}_dout", dmix, W["hyb_w_out"][j], "nt", b_kind="lead", b_lead=0)
    G["hyb_w_out"][j] = _mm(f"{tag}_dwout", o_ab, dmix, "tn", out_dtypes=(bf16,)).reshape(N_DEV, -1, D_MODEL)
    sinks = W["hyb_sinks"][j][None, :] + send_early({("hyb_w_out", j): G["hyb_w_out"][j]})
    dproj, dsinks = _attn_bwd(f"{tag}_dattn", proj, cos, sin, sinks, d_oab)
    a_log = W["hyb_a_log"][j].reshape(B_HEADS, 1, 1)
    dt_bias = W["hyb_dt_bias"][j].reshape(B_HEADS, 1, 1)
    dc, dproj, dal, ddt, dnw = _delta_bwd(f"{tag}_ddelta", c, proj, a_log, dt_bias, W["hyb_norm_w"][j][None, :],
                                          s_saved, t_saved, d_oab, dproj)
    dproj, dconv_w, _ = _conv_bwd(f"{tag}_dconv", dc, proj, CB_CONV, 12, W["hyb_conv_w"][j], dproj, CB_CONV)
    dx = _mm(f"{tag}_dx", dproj, W["hyb_w_in"][j], "nt", b_kind="lead", b_lead=0,
             **({} if addend is None else dict(epilogue=_residual_cot, extras=(addend,))))
    G["hyb_w_in"][j] = _split_cols(f"{tag}_dwin_split", _mm(f"{tag}_dwin", x, dproj, "tn", tn=1536))
    G["hyb_sinks"][j] = dsinks[0]
    G["hyb_conv_w"][j] = dconv_w
    G["hyb_a_log"][j] = dal.reshape(B_HEADS)
    G["hyb_dt_bias"][j] = ddt.reshape(B_HEADS)
    G["hyb_norm_w"][j] = dnw[0]
    return dx


def _rec_fwd(tag, x, xb, W, j, ln):
    Wd = D_MODEL
    proj = _mm(f"{tag}_proj", xb, W["rec_w_in"][j], "nn", b_kind="devcol", b_lead=0)
    xc = _conv_fwd(f"{tag}_conv", proj, 0, Wd // LANE, W["rec_conv_w"][j], W["rec_conv_b"][j][None, :])
    pars = [W["rec_b_a"][j][None, :], W["rec_b_x"][j][None, :], W["rec_lambda"][j][None, :]]
    a, b = _gates_fwd(f"{tag}_gates", xc, W["rec_w_a"][j][0], W["rec_w_x"][j][0], pars)
    h, h_prev, hg = _scan(f"{tag}_scan", a, proj, False, b=b)
    mix, x1, x1b = _mm(f"{tag}_out", hg, W["rec_w_out"][j], "nn", b_kind="lead", b_lead=0, epilogue=_ln_epilogue,
                       extras=(x,), params=ln, out_dtypes=(f32, f32, bf16), tm=512)
    return mix, x1, x1b, (proj, xc, a, h, h_prev, hg)


def _rec_bwd(tag, x, dmix, addend, W, j, saved, G, send_early):
    proj, xc, a, h, h_prev, hg = saved
    Wd = D_MODEL
    dhg = _mm(f"{tag}_dout", dmix, W["rec_w_out"][j], "nt", b_kind="lead", b_lead=0)
    G["rec_w_out"][j] = _mm(f"{tag}_dwout", hg, dmix, "tn", out_dtypes=(bf16,)).reshape(N_DEV, -1, D_MODEL)
    sent = send_early({("rec_w_out", j): G["rec_w_out"][j]})
    lam_t, dproj = _scan(f"{tag}_dscan", a, proj, True, h=h, dhg=dhg)
    pars = [W["rec_b_a"][j][None, :] + sent, W["rec_b_x"][j][None, :], W["rec_lambda"][j][None, :]]
    dxc, dpr, dpi, db_a, db_x, dlam = _gates_bwd(f"{tag}_dgates", xc, W["rec_w_a"][j][0], W["rec_w_x"][j][0], pars,
                                                 lam_t, h_prev)
    dwa, dwx = _blockdiag_bwd_dw(f"{tag}_dgates_dw", xc, dpr, dpi)
    G["rec_w_a"][j], G["rec_w_x"][j] = _rows_to_dev(dwa), _rows_to_dev(dwx)
    dproj, dconv_w, dconv_b = _conv_bwd(f"{tag}_dconv", dxc, proj, 0, Wd // LANE, W["rec_conv_w"][j], dproj, 0)
    dx = _mm(f"{tag}_dx", dproj, W["rec_w_in"][j], "nt", b_kind="devcol", b_lead=0,
             **({} if addend is None else dict(epilogue=_residual_cot, extras=(addend,))))
    G["rec_w_in"][j] = _mm(f"{tag}_dwin", x, dproj, "tn", o_kind="devcol", out_dtypes=(bf16,), tn=2048)
    G["rec_conv_w"][j] = dconv_w
    G["rec_conv_b"][j] = dconv_b
    G["rec_b_a"][j] = db_a[0]
    G["rec_b_x"][j] = db_x[0]
    G["rec_lambda"][j] = dlam[0]
    return dx


def _local_step(x, target, W, load_layer, grads_ready):
    T = x.shape[0]
    cos, sin = _rope_tables(T)
    saved = []
    xb = x
    for layer in range(DEPTH):
        j = layer // 2
        tag = f"L{layer}"
        load_layer(layer, "mixer", x)
        ln1 = (W["ln1_g"][layer][None, :], W["ln1_b"][layer][None, :])
        if layer % 2 == 0:
            mix, x1, x1b, sv = _hybrid_fwd(tag, x, xb, W, j, cos, sin, ln1)
        else:
            mix, x1, x1b, sv = _rec_fwd(tag, x, xb, W, j, ln1)
        load_layer(layer, "mlp", x1)
        a, h2 = _mm(f"{tag}_mlp1", x1b, W["mlp_w1"][layer], "nn", b_kind="devcol", b_lead=0, epilogue=_relu2_epilogue,
                    out_dtypes=(bf16, bf16), tm=2048)
        ln2 = (W["ln2_g"][layer][None, :], W["ln2_b"][layer][None, :])
        y, x2, x2b = _mm(f"{tag}_mlp2", h2, W["mlp_w2"][layer], "nn", b_kind="lead", b_lead=0, epilogue=_ln_epilogue,
                         extras=(x1,), params=ln2, out_dtypes=(f32, f32, bf16))
        saved.append((x, xb, sv, mix, x1, x1b, a, h2, y))
        x, xb = x2, x2b
    loss, dx = _loss_head(x, target)

    G = {k: [None] * (DEPTH if k.startswith(("ln", "mlp")) else DEPTH // 2) for k in (
        "hyb_w_in", "hyb_sinks", "hyb_conv_w", "hyb_a_log", "hyb_dt_bias", "hyb_norm_w", "hyb_w_out",
        "rec_w_in", "rec_conv_w", "rec_conv_b", "rec_w_a", "rec_b_a", "rec_w_x", "rec_b_x", "rec_lambda", "rec_w_out",
        "ln1_g", "ln1_b", "mlp_w1", "mlp_w2", "ln2_g", "ln2_b")}
    held = {}
    cot_rows, cot_fn = [(dx, 0, D_MODEL)], None
    for layer in reversed(range(DEPTH)):
        j = layer // 2
        tag = f"L{layer}"
        x0, x0b, sv, mix, x1, x1b, a, h2, y = saved[layer]
        ln2 = [W["ln2_g"][layer][None, :], W["ln2_b"][layer][None, :]]
        (dy, dyb), (dg2, db2) = _tl_bwd(f"{tag}_dln2", _ln_res_fn, [(x1, 0, D_MODEL), (y, 0, D_MODEL)], ln2,
                                        cot_rows, cot_fn=cot_fn, skip=(0,), bf16_copy=True)
        G["ln2_g"][layer], G["ln2_b"][layer] = dg2[0], db2[0]
        da = _mm(f"{tag}_dmlp2", dyb, W["mlp_w2"][layer], "nt", b_kind="lead", b_lead=0, epilogue=_drelu2_epilogue,
                 extras=(a,), out_dtypes=(bf16,), tm=2048, tn=512)
        G["mlp_w2"][layer] = _mm(f"{tag}_dw2", h2, dyb, "tn", out_dtypes=(bf16,), tm=2048).reshape(N_DEV, -1, D_MODEL)
        dx1 = _mm(f"{tag}_dmlp1", da, W["mlp_w1"][layer], "nt", b_kind="devcol", b_lead=0, tm=2048)
        G["mlp_w1"][layer] = _mm(f"{tag}_dw1", x1b, da, "tn", o_kind="devcol", out_dtypes=(bf16,), tn=2048)
        ln1 = [W["ln1_g"][layer][None, :], W["ln1_b"][layer][None, :]]
        (dmix, dmixb), (dg1, db1) = _tl_bwd(f"{tag}_dln1", _ln_res_fn, [(x0, 0, D_MODEL), (mix, 0, D_MODEL)], ln1,
                                            [(dx1, 0, D_MODEL), (dy, 0, D_MODEL)], cot_fn=_residual_cot, skip=(0,),
                                            bf16_copy=True)
        G["ln1_g"][layer], G["ln1_b"][layer] = dg1[0], db1[0]
        dx0_a = dmix if layer == 0 else None
        held.update({(k, layer): G[k][layer] for k in ("mlp_w1", "mlp_w2")})
        early = functools.partial(grads_ready, f"l{layer}_early", held)
        if layer % 2 == 0:
            dx = _hybrid_bwd(tag, x0b, dmixb, dx0_a, W, j, cos, sin, sv, G, early)
        else:
            dx = _rec_bwd(tag, x0b, dmixb, dx0_a, W, j, sv, G, early)
        held = {(k, i): G[k][i] for k, i in _layer_weights(layer)[:-2] if not k.endswith("w_out")}
        cot_rows, cot_fn = [(dx, 0, D_MODEL), (dmix, 0, D_MODEL)], _residual_cot
    grads_ready("l0_late", held, {})
    big = {k for k, _ in BIG}
    return loss, dx, {k: jnp.stack(v) for k, v in G.items() if k not in big}


def _layer_weights(layer):
    j = layer // 2
    mixer = ["hyb_w_in", "hyb_w_out"] if layer % 2 == 0 else ["rec_w_in", "rec_w_out", "rec_w_a", "rec_w_x"]
    return [(k, j) for k in mixer] + [("mlp_w1", layer), ("mlp_w2", layer)]


def _my_coords():
    return lax.axis_index("x"), lax.axis_index("y"), lax.axis_index("c")


def _all_gather(name, arrays):
    na = len(arrays)

    def body(*refs):
        x_refs, out_refs = refs[:na], refs[na:2 * na]
        send_sems, recv_sems, local_sems = refs[2 * na:]
        x, y, c = _my_coords()
        me, sibling = (x, y, c), (x, y, 1 - c)
        chips = [(1 - x, y), (x, 1 - y), (1 - x, 1 - y)]

        def blk(a, px, py, pc):
            return out_refs[a].at[4 * px + 2 * py + pc]

        def copy(a, k, block, to, src=None):
            return pltpu.make_async_remote_copy(
                src_ref=blk(a, *block) if src is None else src, dst_ref=blk(a, *block),
                send_sem=send_sems.at[a, k], recv_sem=recv_sems.at[a, k],
                device_id=to, device_id_type=pl.DeviceIdType.MESH)

        mine = [pltpu.make_async_copy(x_refs[a], blk(a, *me), local_sems.at[a]) for a in range(na)]
        for cp in mine:
            cp.start()
        first = []
        for a in range(na):
            first.append(copy(a, 0, me, sibling, src=x_refs[a]))
            first += [copy(a, 1 + j, me, (*chip, c), src=x_refs[a]) for j, chip in enumerate(chips)]
        for cp in first:
            cp.start()
        passed = []
        for a in range(na):
            for j, chip in enumerate(chips):
                copy(a, 1 + j, (*chip, c), me).wait_recv()
                passed.append(copy(a, 4 + j, (*chip, c), sibling))
                passed[-1].start()
        for a in range(na):
            copy(a, 0, sibling, me).wait_recv()
            for j, chip in enumerate(chips):
                copy(a, 4 + j, (*chip, 1 - c), me).wait_recv()
        for cp in first + passed:
            cp.wait_send()
        for cp in mine:
            cp.wait()

    return pl.pallas_call(
        body, name=name,
        out_shape=[jax.ShapeDtypeStruct((N_DEV,) + a.shape, a.dtype) for a in arrays],
        in_specs=[pl.BlockSpec(memory_space=pl.ANY)] * na,
        out_specs=[pl.BlockSpec(memory_space=pl.ANY)] * na,
        scratch_shapes=[pltpu.SemaphoreType.DMA((na, 7)), pltpu.SemaphoreType.DMA((na, 7)),
                        pltpu.SemaphoreType.DMA((na,))],
    )(*arrays)


_HBM = pl.BlockSpec(memory_space=pltpu.HBM)
_SEM = pl.BlockSpec(memory_space=pltpu.SEMAPHORE)


def _flip(k, x, y, c):
    return ((1 - x) if k & 4 else x, (1 - y) if k & 2 else y, (1 - c) if k & 1 else c)


_PEERS = {"gather": (1, 2, 4, 6), "scatter": (1, 2, 3, 4, 5, 6, 7)}


def _push_copies(kind, x_refs, land_refs, send_sems, recv_sems, local_sems):
    x, y, c = _my_coords()
    me = 4 * x + 2 * y + c
    peers = _PEERS[kind]
    remote, local = [], []
    for a in range(len(x_refs)):
        local.append(pltpu.make_async_copy(x_refs[a] if kind == "gather" else x_refs[a].at[me], land_refs[a].at[me],
                                           local_sems.at[a]))
        for n, k in enumerate(peers):
            px, py, pc = _flip(k, x, y, c)
            remote.append(pltpu.make_async_remote_copy(
                src_ref=x_refs[a] if kind == "gather" else x_refs[a].at[4 * px + 2 * py + pc],
                dst_ref=land_refs[a].at[me],
                send_sem=send_sems.at[a * len(peers) + n], recv_sem=recv_sems.at[a * len(peers) + n],
                device_id=(px, py, pc), device_id_type=pl.DeviceIdType.MESH))
    return remote, local


def _pass_to_sibling(name, lands):
    na = len(lands)
    chips = (2, 4, 6)

    def body(*refs):
        out_refs, send_sems, recv_sems = refs[na:2 * na], refs[2 * na], refs[2 * na + 1]
        x, y, c = _my_coords()
        cps = []
        for a in range(na):
            for n, k in enumerate(chips):
                px, py, _ = _flip(k, x, y, c)
                cps.append(pltpu.make_async_remote_copy(
                    src_ref=out_refs[a].at[4 * px + 2 * py + c], dst_ref=out_refs[a].at[4 * px + 2 * py + c],
                    send_sem=send_sems.at[a * 3 + n], recv_sem=recv_sems.at[a * 3 + n],
                    device_id=(x, y, 1 - c), device_id_type=pl.DeviceIdType.MESH))
        for cp in cps:
            cp.start()
        for a in range(na):
            for n, k in enumerate(chips):
                px, py, _ = _flip(k, x, y, c)
                blk = out_refs[a].at[4 * px + 2 * py + (1 - c)]
                pltpu.make_async_remote_copy(src_ref=blk, dst_ref=blk, send_sem=send_sems.at[a * 3 + n],
                                             recv_sem=recv_sems.at[a * 3 + n], device_id=(x, y, 1 - c),
                                             device_id_type=pl.DeviceIdType.MESH).wait_recv()
        for cp in cps:
            cp.wait_send()

    return pl.pallas_call(
        body, name=name,
        out_shape=[jax.ShapeDtypeStruct(l.shape, l.dtype) for l in lands],
        in_specs=[pl.BlockSpec(memory_space=pl.ANY)] * na,
        out_specs=[pl.BlockSpec(memory_space=pl.ANY)] * na,
        input_output_aliases={a: a for a in range(na)},
        scratch_shapes=[pltpu.SemaphoreType.DMA((3 * na,)), pltpu.SemaphoreType.DMA((3 * na,))],
    )(*lands)


_SIDE_EFFECT = pltpu.CompilerParams(has_side_effects=pltpu.SideEffectType.DATAFLOW_SIDE_EFFECTING)


def _push_start(name, kind, srcs, lands):
    na = len(srcs)

    def body(*refs):
        remote, local = _push_copies(kind, refs[:na], refs[na:2 * na], *refs[2 * na:2 * na + 3])
        for cp in remote + local:
            cp.start()
        token = refs[-1]
        token[...] = jnp.zeros_like(token)

    arrays = list(srcs) + list(lands)
    n_remote = na * len(_PEERS[kind])
    res = pl.pallas_call(
        body, name=name,
        out_shape=(pltpu.SemaphoreType.DMA((n_remote,)), pltpu.SemaphoreType.DMA((n_remote,)),
                   pltpu.SemaphoreType.DMA((na,)), *[pltpu.HBM(t.shape, t.dtype) for t in arrays],
                   jax.ShapeDtypeStruct((SUBLANE, LANE), f32)),
        in_specs=[_HBM] * (2 * na),
        out_specs=(_SEM, _SEM, _SEM, *[_HBM] * (2 * na), pl.BlockSpec(memory_space=pltpu.VMEM)),
        input_output_aliases={i: 3 + i for i in range(2 * na)},
        compiler_params=_SIDE_EFFECT,
    )(*[pltpu.with_memory_space_constraint(t, pltpu.HBM) for t in arrays])
    return list(res[:3]), res[3:3 + na], res[3 + na:3 + 2 * na], res[-1][:1, :1]


def _push_wait(name, kind, sems, srcs, lands, after):
    na = len(srcs)

    def body(*refs):
        remote, local = _push_copies(kind, refs[:na], refs[na:2 * na], *refs[2 * na:2 * na + 3])
        for cp in remote:
            cp.wait_send()
            cp.wait_recv()
        for cp in local:
            cp.wait()

    arrays = list(srcs) + list(lands)
    res = pl.pallas_call(
        body, name=name,
        out_shape=tuple(pltpu.HBM(t.shape, t.dtype) for t in arrays),
        in_specs=[_HBM] * (2 * na) + [_SEM] * 3 + [pl.BlockSpec(memory_space=pl.ANY)],
        out_specs=tuple([_HBM] * (2 * na)),
        input_output_aliases={i: i for i in range(2 * na)},
        compiler_params=_SIDE_EFFECT,
    )(*arrays, *sems, after)
    return res[na:]


def _sum_blocks(name, land):
    _, R, n = land.shape
    tr = R

    def body(l_ref, o_ref):
        acc = l_ref[0].astype(f32)
        for s in range(1, N_DEV):
            acc = acc + l_ref[s].astype(f32)
        o_ref[...] = acc

    return pl.pallas_call(
        body, name=name, grid=(R // tr,),
        in_specs=[pl.BlockSpec((N_DEV, tr, n), lambda i: (0, i, 0))],
        out_specs=pl.BlockSpec((tr, n), lambda i: (i, 0)),
        out_shape=jax.ShapeDtypeStruct((R, n), f32),
        compiler_params=_cparams("parallel"),
    )(land)


def _adamw(name, w, g, m, v):
    shape = w.shape
    last = shape[-1]
    rows = math.prod(shape[:-1])
    tm = 256 if rows % 256 == 0 and rows > 256 else rows
    w2, g2, m2, v2 = (t.reshape(rows, last) for t in (w, g, m, v))

    def body(w_ref, g_ref, m_ref, v_ref, d_ref, mo_ref, vo_ref):
        gg = g_ref[...]
        mn = ADAM_B1 * m_ref[...] + (1.0 - ADAM_B1) * gg
        vn = ADAM_B2 * v_ref[...] + (1.0 - ADAM_B2) * jnp.square(gg)
        m_hat = mn / (1.0 - ADAM_B1 ** ADAM_STEP)
        v_hat = vn / (1.0 - ADAM_B2 ** ADAM_STEP)
        d_ref[...] = -ADAM_LR * (m_hat / (jnp.sqrt(v_hat) + ADAM_EPS) + ADAM_WD * w_ref[...])
        mo_ref[...] = mn
        vo_ref[...] = vn

    spec = pl.BlockSpec((tm, last), lambda i: (i, 0))
    d, mn, vn = pl.pallas_call(
        body, name=name, grid=(rows // tm,), in_specs=[spec] * 4, out_specs=[spec] * 3,
        out_shape=[jax.ShapeDtypeStruct((rows, last), f32)] * 3,
        compiler_params=_cparams("parallel"),
    )(w2, g2, m2, v2)
    return d.reshape(shape), mn.reshape(shape), vn.reshape(shape)


def _adamw_land(name, lands, w, m, v, tm=256):
    L = len(lands)
    _, R, C = lands[0].shape
    tm = min(tm, R)

    def body(*refs):
        l_refs, (w_ref, m_ref, v_ref, g_ref, d_ref, mo_ref, vo_ref) = refs[:L], refs[L:]
        for k in range(L):
            @pl.when(pl.program_id(0) == k)
            def _(k=k):
                gg = l_refs[k][0].astype(f32)
                for s in range(1, N_DEV):
                    gg = gg + l_refs[k][s].astype(f32)
                g_ref[...] = gg
                mn = ADAM_B1 * m_ref[...] + (1.0 - ADAM_B1) * gg
                vn = ADAM_B2 * v_ref[...] + (1.0 - ADAM_B2) * jnp.square(gg)
                m_hat = mn / (1.0 - ADAM_B1 ** ADAM_STEP)
                v_hat = vn / (1.0 - ADAM_B2 ** ADAM_STEP)
                d_ref[...] = -ADAM_LR * (m_hat / (jnp.sqrt(v_hat) + ADAM_EPS) + ADAM_WD * w_ref[...])
                mo_ref[...] = mn
                vo_ref[...] = vn

    land_specs = [pl.BlockSpec((N_DEV, tm, C), lambda l, i, k=k: (0, jnp.where(l == k, i, 0), 0)) for k in range(L)]
    spec = pl.BlockSpec((None, tm, C), lambda l, i: (l, i, 0))
    return pl.pallas_call(
        body, name=name, grid=(L, R // tm),
        in_specs=land_specs + [spec] * 3,
        out_specs=[spec] * 4,
        out_shape=[jax.ShapeDtypeStruct((L, R, C), f32)] * 4,
        compiler_params=_cparams("arbitrary", "arbitrary"),
    )(*lands, w, m, v)


BIG = [("hyb_w_in", 2), ("hyb_w_out", 1), ("rec_w_in", 2), ("rec_w_out", 1), ("rec_w_a", 2), ("rec_w_x", 2),
       ("mlp_w1", 2), ("mlp_w2", 1)]
SMALL = [("hyb_conv_w", 2), ("rec_conv_w", 2), ("rec_conv_b", 1), ("rec_b_a", 1), ("rec_b_x", 1), ("rec_lambda", 1)]
REPL = ["hyb_sinks", "hyb_a_log", "hyb_dt_bias", "hyb_norm_w", "ln1_g", "ln1_b", "ln2_g", "ln2_b"]
WEIGHTS = ["hyb_w_in", "hyb_sinks", "hyb_conv_w", "hyb_a_log", "hyb_dt_bias", "hyb_norm_w", "hyb_w_out", "rec_w_in",
           "rec_conv_w", "rec_conv_b", "rec_w_a", "rec_b_a", "rec_w_x", "rec_b_x", "rec_lambda", "rec_w_out",
           "ln1_g", "ln1_b", "mlp_w1", "mlp_w2", "ln2_g", "ln2_b"]


def _pack_rows(parts, dtype, row_mult):
    lead = parts[0].shape[:-1]
    flat = jnp.concatenate([p.astype(dtype) for p in parts], axis=-1)
    n = flat.shape[-1]
    unit = row_mult * LANE
    pad = (-n) % unit
    if pad:
        flat = jnp.concatenate([flat, jnp.zeros(lead + (pad,), dtype)], axis=-1)
    return flat.reshape(lead + ((n + pad) // LANE, LANE))


def _gather_full(gathered, shard_shapes, table):
    flat = gathered.reshape(N_DEV, -1)
    out, off = {}, 0
    for name, ax in table:
        shp = shard_shapes[name]
        n = math.prod(shp)
        arr = flat[:, off:off + n].reshape((N_DEV,) + shp)
        off += n
        arr = jnp.moveaxis(arr, 0, ax)
        out[name] = arr.reshape(shp[:ax] + (N_DEV * shp[ax],) + shp[ax + 1:])
    return out


def _matmul_layouts(tag, gw):
    out = {}
    bw = D_MODEL // LRU_BLOCKS
    for k, g in gw.items():
        L = g.shape[1]
        if k == "hyb_w_in":
            out[k] = _merge_cols(f"{tag}_w_in_merge", g)
        elif k in ("hyb_w_out", "rec_w_out", "mlp_w2"):
            out[k] = jnp.swapaxes(g, 0, 1).reshape(L, N_DEV * g.shape[2], g.shape[3])
        elif k in ("rec_w_a", "rec_w_x"):
            out[k] = jnp.moveaxis(g, 0, 2).reshape(L, LRU_BLOCKS, bw, bw)
        else:
            out[k] = g
    return out


def kernel(x, hyb_w_in, hyb_sinks, hyb_conv_w, hyb_a_log, hyb_dt_bias, hyb_norm_w, hyb_w_out, rec_w_in, rec_conv_w, rec_conv_b, rec_w_a, rec_b_a, rec_w_x, rec_b_x, rec_lambda, rec_w_out, ln1_g, ln1_b, mlp_w1, mlp_w2, ln2_g, ln2_b, loss_target, m_hyb_w_in, m_hyb_sinks, m_hyb_conv_w, m_hyb_a_log, m_hyb_dt_bias, m_hyb_norm_w, m_hyb_w_out, m_rec_w_in, m_rec_conv_w, m_rec_conv_b, m_rec_w_a, m_rec_b_a, m_rec_w_x, m_rec_b_x, m_rec_lambda, m_rec_w_out, m_ln1_g, m_ln1_b, m_mlp_w1, m_mlp_w2, m_ln2_g, m_ln2_b, v_hyb_w_in, v_hyb_sinks, v_hyb_conv_w, v_hyb_a_log, v_hyb_dt_bias, v_hyb_norm_w, v_hyb_w_out, v_rec_w_in, v_rec_conv_w, v_rec_conv_b, v_rec_w_a, v_rec_b_a, v_rec_w_x, v_rec_b_x, v_rec_lambda, v_rec_w_out, v_ln1_g, v_ln1_b, v_mlp_w1, v_mlp_w2, v_ln2_g, v_ln2_b):
    args = locals()
    w = {k: args[k] for k in WEIGHTS}
    m = {k: args["m_" + k] for k in WEIGHTS}
    v = {k: args["v_" + k] for k in WEIGHTS}
    shard_shapes = {k: tuple(t.shape) for k, t in w.items()}
    xi, yi, ci = _my_coords()
    me = 4 * xi + 2 * yi + ci

    in_flight = {}

    def install(tag, names, got):
        for (k, i), arr in zip(names, _matmul_layouts(tag, {k: g for (k, _), g in zip(names, got)}).values()):
            W[k][i] = arr

    def start_gather(tag, names):
        srcs = [w[k][i:i + 1].astype(bf16) for k, i in names]
        *pending, zero = _push_start(f"gather_{tag}_start", "gather", srcs,
                                     [lax.empty((N_DEV,) + s.shape, bf16) for s in srcs])
        in_flight[tag] = (names, pending)
        return zero

    def finish_gather(tag, after):
        names, pending = in_flight.pop(tag)
        half = _push_wait(f"gather_{tag}_wait", "gather", *pending, after)
        install(tag, names, _pass_to_sibling(f"gather_{tag}_pass", half))

    def started(k, zero):
        W[k] = W[k] + zero

    def mixer_w(layer):
        return _layer_weights(layer)[:-2]

    def mlp_w(layer):
        return _layer_weights(layer)[-2:]

    gathered0 = _all_gather("gather_first", [w[k][i:i + 1].astype(bf16) for k, i in mixer_w(0)]
                            + [_pack_rows([w[k].reshape(-1) for k, _ in SMALL], f32, SUBLANE)])
    W = _gather_full(gathered0[-1], shard_shapes, SMALL)
    W.update({k: w[k] for k in REPL})
    W.update({k: {} for k, _ in BIG})
    install("l0a", mixer_w(0), gathered0[:-1])
    started("hyb_sinks", start_gather("l0b", mlp_w(0)) + start_gather("l1a", mixer_w(1)))

    def load_layer(layer, part, after):
        if part == "mixer":
            if layer > 0:
                finish_gather(f"l{layer}a", after)
            if 0 < layer < DEPTH - 1:
                started("hyb_sinks" if layer % 2 == 0 else "rec_conv_b",
                        start_gather(f"l{layer + 1}a", mixer_w(layer + 1)))
        else:
            finish_gather(f"l{layer}b", after)
            if layer < DEPTH - 1:
                started("ln2_g", start_gather(f"l{layer + 1}b", mlp_w(layer + 1)))

    grads_in_flight = {}

    def grads_ready(tag, a, b):
        g = {**a, **b}
        srcs = list(g.values())
        *pending, zero = _push_start(f"scatter_{tag}_start", "scatter", srcs, [lax.empty(s.shape, bf16) for s in srcs])
        grads_in_flight[tag] = (list(g.keys()), pending)
        return zero

    loss_local, grad_x, G = _local_step(x[0], loss_target[0], W, load_layer, grads_ready)
    loss = lax.psum(loss_local, MESH_AXES)

    landed = {}

    def land(tag, after):
        keys, pending = grads_in_flight[tag]
        landed.update(zip(keys, _push_wait(f"scatter_{tag}_wait", "scatter", *pending, after)))

    tags = list(grads_in_flight)
    for tag in tags[:-1]:
        land(tag, grad_x)
    rest = _pack_rows([G[k].reshape(-1) for k, _ in SMALL] + [G[k].reshape(-1) for k in REPL], f32, SUBLANE)
    g_rest = _sum_blocks("sum_rest", _all_gather("gather_rest", [rest])[0]).reshape(-1)

    grads, delta, new_m, new_v = {}, {}, {}, {}

    def adamw_big(k):
        shp = shard_shapes[k]
        s3 = (shp[0], math.prod(shp[1:-1]), shp[-1])
        lands = [landed[(k, i)].reshape((N_DEV,) + s3[1:]) for i in range(shp[0])]
        res = _adamw_land("adamw_" + k, lands, w[k].reshape(s3), m[k].reshape(s3), v[k].reshape(s3))
        grads[k], delta[k], new_m[k], new_v[k] = (r.reshape(shp) for r in res)

    late = {k for k, _ in grads_in_flight[tags[-1]][0]}
    for k in [k for k, _ in BIG if k not in late]:
        adamw_big(k)
        done = new_v[k]
    land(tags[-1], done)
    for k in [k for k, _ in BIG if k in late]:
        adamw_big(k)
    off = 0
    for k, ax in SMALL:
        full_shape = G[k].shape
        n = math.prod(full_shape)
        full = g_rest[off:off + n].reshape(full_shape)
        off += n
        s = shard_shapes[k][ax]
        grads[k] = lax.dynamic_slice_in_dim(full, me * s, s, axis=ax)
    for k in REPL:
        n = math.prod(shard_shapes[k])
        grads[k] = g_rest[off:off + n].reshape(shard_shapes[k])
        off += n

    for k in [k for k, _ in SMALL] + REPL:
        delta[k], new_m[k], new_v[k] = _adamw("adamw_" + k, w[k], grads[k], m[k], v[k])

    return (loss, grad_x[None], *[grads[k] for k in WEIGHTS], *[delta[k] for k in WEIGHTS],
            *[new_m[k] for k in WEIGHTS], *[new_v[k] for k in WEIGHTS])
```

```python
import functools
import math

import jax
import jax.numpy as jnp
from jax import lax
from jax.experimental import pallas as pl
from jax.experimental.pallas import tpu as pltpu

f32 = jnp.float32
bf16 = jnp.bfloat16

N_DEV = 8
D_MODEL = 1024
DEPTH = 4
A_HEAD_DIM = 64
A_Q_HEADS = 8
WINDOW = 128
ROPE_THETA = 10000.0
B_HEADS = 4
B_HEAD_DIM = 128
B_CHUNK = 64
LRU_BLOCKS = 4
LRU_C = 8.0
D_FF = 4 * D_MODEL
HYB_PROJ = 2824
HYB_PROJ_PAD = 3072
DN_ALPHA = (2 * DEPTH) ** 0.25
LN_EPS = 1e-5
NORM_EPS = 1e-6
ADAM_LR = 0.001
ADAM_B1 = 0.9
ADAM_B2 = 0.999
ADAM_EPS = 1e-08
ADAM_WD = 0.01
ADAM_STEP = 10

LANE = 128
SUBLANE = 8
VMEM_LIMIT = 48 * 1024 * 1024

CB_QA, CB_KA, CB_VA, CB_CONV, CB_Z, CB_LG = 0, 4, 5, 6, 18, 22

MESH_AXES = ("x", "y", "c")


def _cparams(*sem):
    return pltpu.CompilerParams(dimension_semantics=sem, vmem_limit_bytes=VMEM_LIMIT)


def _dot(a, b, dims, precision=None):
    return lax.dot_general(a, b, (dims, ((), ())), preferred_element_type=f32, precision=precision)


NN = ((1,), (0,))
NT = ((1,), (1,))
TN = ((0,), (0,))


def _mat_spec(arr, kind, lead, br, bc, rb, cb):
    if kind == "plain":
        return pl.BlockSpec((br, bc), lambda i, j, k: (rb(i, j, k), cb(i, j, k)))
    if kind == "lead":
        return pl.BlockSpec((None, br, bc), lambda i, j, k: (lead, rb(i, j, k), cb(i, j, k)))
    assert kind == "devcol" and bc == arr.shape[-1]
    return pl.BlockSpec((None, None, br, bc), lambda i, j, k: (cb(i, j, k), lead, rb(i, j, k), 0))


def _mm(name, a, b, mode, *, b_kind="plain", b_lead=0, o_kind="plain", epilogue=None, extras=(), params=(),
        out_dtypes=(f32,), tm=1024, tn=1024, tk=None):
    if tk is None:
        tk = 512 if mode == "tn" else 1024
    if b_kind in ("plain", "lead"):
        b_rows, b_cols = b.shape[-2:]
    else:
        b_rows, b_cols = b.shape[-2], N_DEV * b.shape[-1]
    if mode == "nn":
        (M, K), (K2, N) = a.shape, (b_rows, b_cols)
    elif mode == "nt":
        (M, K), (N, K2) = a.shape, (b_rows, b_cols)
    else:
        (K, M), (K2, N) = a.shape, (b_rows, b_cols)
    assert K == K2, (name, a.shape, b.shape, mode)
    tm, tn, tk = min(tm, M), min(tn, N), min(tk, K)
    cols_are_n = mode != "nt"
    if b_kind == "devcol":
        tn, tk = (b.shape[-1], tk) if cols_are_n else (tn, b.shape[-1])
    shard = N // N_DEV
    if o_kind == "devcol":
        tn = max(shard, tn // shard * shard)
    assert M % tm == 0 and N % tn == 0 and K % tk == 0, (name, M, N, K, tm, tn, tk)
    nk = K // tk
    dims = {"nn": NN, "nt": NT, "tn": TN}[mode]
    n_ex, n_out = len(extras) + len(params), len(out_dtypes)

    def body(*refs):
        a_ref, b_ref = refs[:2]
        ex = refs[2:2 + n_ex]
        outs = refs[2 + n_ex:2 + n_ex + n_out]
        acc = refs[-1]
        k = pl.program_id(2)

        @pl.when(k == 0)
        def _():
            acc[...] = jnp.zeros_like(acc)

        acc[...] += _dot(a_ref[...].astype(bf16), b_ref[...].astype(bf16), dims)

        @pl.when(k == nk - 1)
        def _():
            r = acc[...]
            res = epilogue(r, *[e[...] for e in ex]) if epilogue is not None else (r,)
            for o, v in zip(outs, res):
                if o_kind == "plain":
                    o[...] = v.astype(o.dtype)
                else:
                    for q in range(tn // shard):
                        o[q] = v[:, q * shard:(q + 1) * shard].astype(o.dtype)

    if mode == "tn":
        a_spec = pl.BlockSpec((tk, tm), lambda i, j, k: (k, i))
    else:
        a_spec = pl.BlockSpec((tm, tk), lambda i, j, k: (i, k))
    jb, kb = (lambda i, j, k: j), (lambda i, j, k: k)
    if mode == "nt":
        b_spec = _mat_spec(b, b_kind, b_lead, tn, tk, jb, kb)
    else:
        b_spec = _mat_spec(b, b_kind, b_lead, tk, tn, kb, jb)
    e_spec = pl.BlockSpec((tm, tn), lambda i, j, k: (i, j))
    if o_kind == "plain":
        o_spec, o_shape = e_spec, (M, N)
    else:
        o_spec, o_shape = pl.BlockSpec((tn // shard, tm, shard), lambda i, j, k: (j, i, 0)), (N_DEV, M, shard)
    res = pl.pallas_call(
        body, name=name,
        grid=(M // tm, N // tn, nk),
        in_specs=[a_spec, b_spec] + [e_spec] * len(extras)
        + [pl.BlockSpec(p.shape, lambda i, j, k: (0, 0)) for p in params],
        out_specs=[o_spec] * n_out,
        out_shape=[jax.ShapeDtypeStruct(o_shape, dt) for dt in out_dtypes],
        scratch_shapes=[pltpu.VMEM((tm, tn), f32)],
        compiler_params=_cparams("parallel", "parallel", "arbitrary"),
    )(a, b, *extras, *params)
    return res[0] if n_out == 1 else res


def _row_spec(tm, cb, width):
    assert (cb * LANE) % width == 0
    blk = (cb * LANE) // width
    return pl.BlockSpec((tm, width), lambda i: (i, blk))


def _whole_spec(p):
    nd = p.ndim
    return pl.BlockSpec(p.shape, lambda i: (0,) * nd)


def _tl_bwd(name, fn, rows, params, cot_rows, cot_fn=None, skip=(), bf16_copy=False, tm=512):
    T = rows[0][0].shape[0]
    tm = min(tm, T)
    nr, npar, nc = len(rows), len(params), len(cot_rows)
    keep = [k for k in range(nr) if k not in skip]
    n_rows = len(keep) + int(bf16_copy)
    row_dtypes = [(rows[k][2], f32) for k in keep] + ([(rows[keep[0]][2], bf16)] if bf16_copy else [])

    def body(*refs):
        vals = [r[...] for r in refs[:nr + npar]]
        cots = [r[...] for r in refs[nr + npar:nr + npar + nc]]
        outs = refs[nr + npar + nc:]
        cot = tuple(cot_fn(*cots)) if cot_fn is not None else tuple(cots)
        _, vjp = jax.vjp(fn, *vals)
        grads = vjp(cot)
        for o, k in zip(outs, keep):
            o[...] = grads[k].astype(o.dtype)
        if bf16_copy:
            outs[len(keep)][...] = grads[keep[0]].astype(bf16)
        i = pl.program_id(0)
        for o, g in zip(outs[n_rows:], grads[nr:]):
            @pl.when(i == 0)
            def _(o=o):
                o[...] = jnp.zeros_like(o)
            o[...] += g

    res = pl.pallas_call(
        body, name=name, grid=(T // tm,),
        in_specs=[_row_spec(tm, cb, w) for (_, cb, w) in rows] + [_whole_spec(p) for p in params]
        + [_row_spec(tm, cb, w) for (_, cb, w) in cot_rows],
        out_specs=[pl.BlockSpec((tm, w), lambda i: (i, 0)) for w, _ in row_dtypes] + [_whole_spec(p) for p in params],
        out_shape=[jax.ShapeDtypeStruct((T, w), dt) for w, dt in row_dtypes]
        + [jax.ShapeDtypeStruct(p.shape, f32) for p in params],
        compiler_params=_cparams("arbitrary"),
    )(*[r[0] for r in rows], *params, *[r[0] for r in cot_rows])
    return res[:n_rows], res[n_rows:]


def _ln_res_fn(x, mix, g, b):
    pre = DN_ALPHA * x + mix
    mu = jnp.mean(pre, axis=-1, keepdims=True)
    var = jnp.mean(jnp.square(pre - mu), axis=-1, keepdims=True)
    return ((pre - mu) * lax.rsqrt(var + LN_EPS) * g + b,)


@jax.custom_jvp
def _expm1(x):
    small = jnp.abs(x) < 0.3
    xs = jnp.where(small, x, 0.0)
    poly = xs * (1.0 + xs * (1 / 2 + xs * (1 / 6 + xs * (1 / 24 + xs * (1 / 120 + xs * (
        1 / 720 + xs * (1 / 5040 + xs * (1 / 40320 + xs * (1 / 362880)))))))))
    return jnp.where(small, poly, jnp.exp(x) - 1.0)


@_expm1.defjvp
def _expm1_jvp(primals, tangents):
    (x,), (t,) = primals, tangents
    return _expm1(x), t * jnp.exp(x)


def _rglru_pre_fn(pre_r, pre_i, xc, b_a, b_x, lam):
    r = jax.nn.sigmoid(pre_r + b_a)
    i = jax.nn.sigmoid(pre_i + b_x)
    log_a = -LRU_C * r * jax.nn.softplus(-lam)
    a = jnp.exp(log_a)
    b = jnp.sqrt(-_expm1(2.0 * log_a)) * (i * xc)
    return a, b


def _rec_gate_fn(h, gate):
    return (h * jax.nn.gelu(gate),)


def _loss_head(y, t, tm=512):
    T, Dm = y.shape
    tm = min(tm, T)

    def body(y_ref, t_ref, dy_ref, loss_ref):
        e = y_ref[...] - t_ref[...]
        dy_ref[...] = e * (1.0 / Dm)

        @pl.when(pl.program_id(0) == 0)
        def _():
            loss_ref[...] = jnp.zeros_like(loss_ref)

        loss_ref[...] += 0.5 * jnp.sum(jnp.mean(e * e, axis=-1, keepdims=True), axis=0, keepdims=True)

    dy, loss = pl.pallas_call(
        body, name="loss_head", grid=(T // tm,),
        in_specs=[pl.BlockSpec((tm, Dm), lambda i: (i, 0))] * 2,
        out_specs=[pl.BlockSpec((tm, Dm), lambda i: (i, 0)), pl.BlockSpec((SUBLANE, LANE), lambda i: (0, 0))],
        out_shape=[jax.ShapeDtypeStruct((T, Dm), f32), jax.ShapeDtypeStruct((SUBLANE, LANE), f32)],
        compiler_params=_cparams("arbitrary"),
    )(y, t)
    return loss[0, 0], dy


def _conv_fwd(name, x, cb0, nblk, w, bias, tm=2048):
    T = x.shape[0]
    tm = min(tm, T)
    hb = tm // SUBLANE
    has_b = bias is not None

    def body(*refs):
        cur, prev, w_ref = refs[:3]
        b_ref = refs[3] if has_b else None
        o = refs[-1]
        i = pl.program_id(1)
        p = jnp.where(i > 0, prev[...], 0.0)
        xcat = jnp.concatenate([p, cur[...]], axis=0)
        acc = cur[...] * w_ref[3:4, :]
        for j in range(3):
            acc = acc + pltpu.roll(xcat, 3 - j, axis=0)[SUBLANE:] * w_ref[j:j + 1, :]
        if has_b:
            acc = acc + b_ref[...]
        o[...] = acc

    in_specs = [
        pl.BlockSpec((tm, LANE), lambda c, i: (i, cb0 + c)),
        pl.BlockSpec((SUBLANE, LANE), lambda c, i: (jnp.maximum(i * hb - 1, 0), cb0 + c)),
        pl.BlockSpec((4, LANE), lambda c, i: (0, c)),
    ]
    args = [x, x, w]
    if has_b:
        in_specs.append(pl.BlockSpec((1, LANE), lambda c, i: (0, c)))
        args.append(bias)
    return pl.pallas_call(
        body, name=name, grid=(nblk, T // tm),
        in_specs=in_specs,
        out_specs=pl.BlockSpec((tm, LANE), lambda c, i: (i, c)),
        out_shape=jax.ShapeDtypeStruct((T, nblk * LANE), f32),
        compiler_params=_cparams("parallel", "parallel"),
    )(*args)


def _conv_bwd(name, dy, x, cb0, nblk, w, into, into_cb, tm=2048):
    T = x.shape[0]
    tm = min(tm, T)
    hb = tm // SUBLANE
    nt = T // tm

    def body(dcur, dnext, xcur, xprev, w_ref, _, dx_ref, dw_ref, db_ref):
        i = pl.program_id(1)
        d = dcur[...]
        dn = jnp.where(i < nt - 1, dnext[...], 0.0)
        dcat = jnp.concatenate([d, dn], axis=0)
        acc = d * w_ref[3:4, :]
        for j in range(3):
            s = 3 - j
            acc = acc + pltpu.roll(dcat, tm + SUBLANE - s, axis=0)[:tm] * w_ref[j:j + 1, :]
        dx_ref[...] = acc.astype(dx_ref.dtype)

        p = jnp.where(i > 0, xprev[...], 0.0)
        xcat = jnp.concatenate([p, xcur[...]], axis=0)
        rows = [jnp.sum(d * pltpu.roll(xcat, 3 - j, axis=0)[SUBLANE:], axis=0, keepdims=True) for j in range(3)]
        rows.append(jnp.sum(d * xcur[...], axis=0, keepdims=True))
        rows.append(jnp.zeros((SUBLANE - 4, LANE), f32))

        @pl.when(i == 0)
        def _():
            dw_ref[...] = jnp.zeros_like(dw_ref)
            db_ref[...] = jnp.zeros_like(db_ref)

        dw_ref[...] += jnp.concatenate(rows, axis=0)
        db_ref[...] += jnp.broadcast_to(jnp.sum(d, axis=0, keepdims=True), (SUBLANE, LANE))

    nh = T // SUBLANE
    dx, dw, db = pl.pallas_call(
        body, name=name, grid=(nblk, nt),
        in_specs=[
            pl.BlockSpec((tm, LANE), lambda c, i: (i, c)),
            pl.BlockSpec((SUBLANE, LANE), lambda c, i: (jnp.minimum((i + 1) * hb, nh - 1), c)),
            pl.BlockSpec((tm, LANE), lambda c, i: (i, cb0 + c)),
            pl.BlockSpec((SUBLANE, LANE), lambda c, i: (jnp.maximum(i * hb - 1, 0), cb0 + c)),
            pl.BlockSpec((4, LANE), lambda c, i: (0, c)),
            pl.BlockSpec(memory_space=pl.ANY),
        ],
        out_specs=[
            pl.BlockSpec((tm, LANE), lambda c, i: (i, into_cb + c)),
            pl.BlockSpec((SUBLANE, LANE), lambda c, i: (0, c)),
            pl.BlockSpec((SUBLANE, LANE), lambda c, i: (0, c)),
        ],
        out_shape=[jax.ShapeDtypeStruct(into.shape, into.dtype),
                   jax.ShapeDtypeStruct((SUBLANE, nblk * LANE), f32),
                   jax.ShapeDtypeStruct((SUBLANE, nblk * LANE), f32)],
        input_output_aliases={5: 0},
        compiler_params=_cparams("parallel", "arbitrary"),
    )(dy, dy, x, x, w, into)
    return dx, dw[:4], db[0]


@functools.partial(jax.custom_vjp, nondiff_argnums=(1,))
def _lroll(x, s):
    return pltpu.roll(x, s, axis=1)


def _lroll_fwd(x, s):
    return _lroll(x, s), None


def _lroll_bwd(s, _, g):
    return (_lroll(g, (LANE - s) % LANE),)


_lroll.defvjp(_lroll_fwd, _lroll_bwd)


def _rope_tables(T):
    half = A_HEAD_DIM // 2
    inv_freq = ROPE_THETA ** (-jnp.arange(half, dtype=f32) / half)
    ang = jnp.arange(T, dtype=f32)[:, None] * inv_freq[None, :]
    cos, sin = jnp.cos(ang), jnp.sin(ang)
    return jnp.tile(jnp.concatenate([cos, cos], axis=1), (1, 2)), jnp.tile(jnp.concatenate([-sin, sin], axis=1), (1, 2))


def _attn_block_fn(n, q, kp, kc, vp, vc, cq, sq, cp, sp, sinks):
    W = WINDOW
    lane = lax.broadcasted_iota(jnp.int32, (W, LANE), 1)
    lo_half = (lane % A_HEAD_DIM) < (A_HEAD_DIM // 2)
    lane8 = lax.broadcasted_iota(jnp.int32, sinks.shape, 1)

    def rope(x, c, s):
        return x * c + jnp.where(lo_half, _lroll(x, LANE - A_HEAD_DIM // 2), _lroll(x, A_HEAD_DIM // 2)) * s

    k2 = jnp.concatenate([rope(kp, cp, sp), rope(kc, cq, sq)], axis=0).astype(bf16)
    v2 = jnp.concatenate([vp, vc], axis=0).astype(bf16)
    qs = []
    for t in range(4):
        qt = rope(q[:, LANE * t:LANE * (t + 1)], cq, sq)
        g = t // 2
        for hh in range(2):
            qa = jnp.where((lane // A_HEAD_DIM) == hh, qt, 0.0)
            qs.append(_lroll(qa, A_HEAD_DIM) if hh != g else qa)
    s_all = _dot(jnp.concatenate(qs, axis=0).astype(bf16), k2, NT) * (A_HEAD_DIM ** -0.5)
    row = lax.broadcasted_iota(jnp.int32, (W, 2 * W), 0)
    col = lax.broadcasted_iota(jnp.int32, (W, 2 * W), 1)
    dist = row + W - col
    mask = (dist >= 0) & (dist < W) & ((col >= W) | (n > 0))
    ps = []
    for j in range(A_Q_HEADS):
        s = jnp.where(mask, s_all[W * j:W * (j + 1)], -jnp.inf)
        sink = jnp.sum(jnp.where(lane8 == j, sinks, 0.0), axis=1, keepdims=True)
        m = jnp.maximum(jnp.max(s, axis=-1, keepdims=True), sink)
        e = jnp.exp(s - m)
        ps.append((e / (jnp.sum(e, axis=-1, keepdims=True) + jnp.exp(sink - m))).astype(bf16))
    o = _dot(jnp.concatenate(ps, axis=0), v2, NN)
    outs = []
    for t in range(4):
        g = t // 2
        ot = jnp.zeros((W, LANE), f32)
        for hh in range(2):
            j = 2 * t + hh
            oj = jnp.where((lane // A_HEAD_DIM) == g, o[W * j:W * (j + 1)], 0.0)
            ot = ot + (_lroll(oj, A_HEAD_DIM) if hh != g else oj)
        outs.append(ot)
    return jnp.concatenate(outs, axis=1)


def _attn_specs():
    W = WINDOW
    prev = lambda n: jnp.maximum(n - 1, 0)
    return [
        pl.BlockSpec((W, 4 * LANE), lambda n: (n, CB_QA // 4)),
        pl.BlockSpec((W, LANE), lambda n: (prev(n), CB_KA)),
        pl.BlockSpec((W, LANE), lambda n: (n, CB_KA)),
        pl.BlockSpec((W, LANE), lambda n: (prev(n), CB_VA)),
        pl.BlockSpec((W, LANE), lambda n: (n, CB_VA)),
        pl.BlockSpec((W, LANE), lambda n: (n, 0)),
        pl.BlockSpec((W, LANE), lambda n: (n, 0)),
        pl.BlockSpec((W, LANE), lambda n: (prev(n), 0)),
        pl.BlockSpec((W, LANE), lambda n: (prev(n), 0)),
        pl.BlockSpec((1, A_Q_HEADS), lambda n: (0, 0)),
    ]


def _attn_fwd(name, proj, cos, sin, sinks):
    T = proj.shape[0]
    W = WINDOW

    def body(*refs):
        o = refs[-1]
        o[...] = _attn_block_fn(pl.program_id(0), *[r[...] for r in refs[:-1]]).astype(o.dtype)

    return pl.pallas_call(
        body, name=name, grid=(T // W,),
        in_specs=_attn_specs(),
        out_specs=pl.BlockSpec((W, 4 * LANE), lambda n: (n, 0)),
        out_shape=jax.ShapeDtypeStruct((T, 2 * 4 * LANE), bf16),
        compiler_params=_cparams("parallel"),
    )(proj, proj, proj, proj, proj, cos, sin, cos, sin, sinks)


def _attn_bwd(name, proj, cos, sin, sinks, d_oab):
    T = proj.shape[0]
    W = WINDOW
    Q = 4 * LANE
    nb = T // W

    def body(*refs):
        ins = [r[...] for r in refs[:10]]
        do = refs[10][...]
        out_ref, ds_ref, d_ref = refs[11:]
        n = pl.program_id(0)
        _, vjp = jax.vjp(functools.partial(_attn_block_fn, n), *ins)
        dq, dkp, dkc, dvp, dvc, _, _, _, _, dsk = vjp(do)

        @pl.when(n == 0)
        def _():
            d_ref[:, Q:] = jnp.zeros((T, 2 * LANE), f32)
            ds_ref[...] = jnp.zeros_like(ds_ref)

        cur = pl.ds(pl.multiple_of(n * W, W), W)
        d_ref[cur, :Q] = dq
        d_ref[cur, Q:Q + LANE] += dkc
        d_ref[cur, Q + LANE:] += dvc
        ds_ref[...] += dsk

        @pl.when(n > 0)
        def _():
            prv = pl.ds(pl.multiple_of((n - 1) * W, W), W)
            d_ref[prv, Q:Q + LANE] += dkp
            d_ref[prv, Q + LANE:] += dvp

        @pl.when(n == nb - 1)
        def _():
            out_ref[...] = d_ref[...].astype(out_ref.dtype)

    return pl.pallas_call(
        body, name=name, grid=(nb,),
        in_specs=_attn_specs() + [pl.BlockSpec((W, Q), lambda n: (n, 0))],
        out_specs=[pl.BlockSpec((T, Q + 2 * LANE), lambda n: (0, 0)),
                   pl.BlockSpec((1, A_Q_HEADS), lambda n: (0, 0))],
        out_shape=[jax.ShapeDtypeStruct((T, HYB_PROJ_PAD), bf16), jax.ShapeDtypeStruct((1, A_Q_HEADS), f32)],
        scratch_shapes=[pltpu.VMEM((T, Q + 2 * LANE), f32)],
        compiler_params=_cparams("arbitrary"),
    )(proj, proj, proj, proj, proj, cos, sin, cos, sin, sinks, d_oab)


def _bdot(spec, a, b, precision=None):
    return jnp.einsum(spec, a, b, preferred_element_type=f32, precision=precision)


@jax.custom_vjp
def _tri_inv(a):
    H, C, _ = a.shape
    B = 2 * SUBLANE
    nb = C // B
    r = lax.broadcasted_iota(jnp.int32, (C, C), 0)
    c = lax.broadcasted_iota(jnp.int32, (C, C), 1)
    a4 = jnp.where((r // B) == (c // B), a, 0.0).reshape(H, nb, B, C)
    t4 = jnp.broadcast_to(jnp.where(r == c, 1.0, 0.0).astype(f32), a.shape).reshape(H, nb, B, C)
    for j in range(B - 1):
        col = jnp.concatenate([a4[:, b:b + 1, :, B * b + j:B * b + j + 1] for b in range(nb)], axis=1)
        t4 = t4 - col * t4[:, :, j:j + 1, :]
    x = t4.reshape(H, C, C)
    hi = lax.Precision.HIGH
    while B < C:
        m = jnp.where(((r // (2 * B)) == (c // (2 * B))) & ((r // B) > (c // B)), a, 0.0)
        x = x - _bdot("hij,hjk->hik", x, _bdot("hij,hjk->hik", m, x, precision=hi), precision=hi)
        B *= 2
    return x


def _tri_inv_fwd(a):
    t = _tri_inv(a)
    return t, t


def _tri_inv_bwd(t, g):
    C = t.shape[-1]
    r = lax.broadcasted_iota(jnp.int32, (C, C), 0)
    c = lax.broadcasted_iota(jnp.int32, (C, C), 1)
    x = _bdot("hki,hkj->hij", t, g, precision=lax.Precision.HIGHEST)
    y = _bdot("hik,hjk->hij", x, t, precision=lax.Precision.HIGHEST)
    return (jnp.where(r > c, -y, 0.0),)


_tri_inv.defvjp(_tri_inv_fwd, _tri_inv_bwd)


@jax.custom_vjp
def _tri_inv_saved(a, t):
    return t


_tri_inv_saved.defvjp(lambda a, t: (t, t), lambda t, g: (_tri_inv_bwd(t, g)[0], jnp.zeros_like(t)))


def _silu(x):
    return x * jax.nn.sigmoid(x)


def _l2n(x):
    return x * lax.rsqrt(jnp.sum(x * x, axis=-1, keepdims=True) + NORM_EPS)


def _delta_chunk_fn(cq, ck, cv, z, lg, a_log, dt_bias, norm_w, S, t_saved=None, want_t=False):
    C = B_CHUNK
    lane = lax.broadcasted_iota(jnp.int32, (C, LANE), 1)
    pick = lambda l0: jnp.concatenate(
        [jnp.sum(jnp.where(lane == l0 + h, lg, 0.0), axis=1, keepdims=True)[None] for h in range(B_HEADS)], axis=0)
    bl, al = pick(0), pick(B_HEADS)
    q = _l2n(_silu(cq)) * (B_HEAD_DIM ** -0.5)
    k = _l2n(_silu(ck))
    v = _silu(cv)
    beta = jax.nn.sigmoid(bl)
    g = -jnp.exp(a_log) * jax.nn.softplus(al + dt_bias)
    r = lax.broadcasted_iota(jnp.int32, (C, C), 0)
    c = lax.broadcasted_iota(jnp.int32, (C, C), 1)
    eye = r == c
    g_row = jnp.sum(jnp.where(eye, g, 0.0), axis=1, keepdims=True)
    gc = jnp.sum(jnp.where(c <= r, g_row, 0.0), axis=2, keepdims=True)
    gc_row = jnp.sum(jnp.where(eye, gc, 0.0), axis=1, keepdims=True)
    decay_incl = jnp.exp(jnp.where(r >= c, gc - gc_row, -jnp.inf))
    decay_strict = jnp.where(r > c, decay_incl, 0.0)
    kb = k * beta
    vb = v * beta
    kbf = k.astype(bf16)
    a_mat = _bdot("hik,hjk->hij", kb.astype(bf16), kbf) * decay_strict
    t_f32 = _tri_inv(a_mat) if t_saved is None else _tri_inv_saved(a_mat, t_saved)
    t_mat = t_f32.astype(bf16)
    eg = jnp.exp(gc)
    u = _bdot("hij,hjv->hiv", t_mat, vb.astype(bf16))
    w = _bdot("hij,hjk->hik", t_mat, (kb * eg).astype(bf16))
    qk = _bdot("hik,hjk->hij", q.astype(bf16), kbf) * decay_incl
    g_last = jnp.sum(g, axis=1, keepdims=True)
    k_tail = k * jnp.exp(g_last - gc)
    Sb = S.astype(bf16)
    v_new = u - _bdot("hck,hkv->hcv", w.astype(bf16), Sb)
    o = _bdot("hck,hkv->hcv", (q * eg).astype(bf16), Sb) + _bdot("hij,hjv->hiv", qk.astype(bf16), v_new.astype(bf16))
    S_new = S * jnp.exp(g_last) + _bdot("hck,hcv->hkv", k_tail.astype(bf16), v_new.astype(bf16))
    ob = o * lax.rsqrt(jnp.mean(o * o, axis=-1, keepdims=True) + NORM_EPS) * norm_w
    return (ob * _silu(z), S_new) + ((t_f32,) if want_t else ())


DELTA_CHUNKS_PER_STEP = 8


def _delta_in_specs(rev, N):
    C = DELTA_CHUNKS_PER_STEP * B_CHUNK
    ix = (lambda n: N - 1 - n) if rev else (lambda n: n)
    specs = [pl.BlockSpec((C, 3 * B_HEADS * LANE), lambda n: (ix(n), 0))]
    specs += [pl.BlockSpec((C, LANE), lambda n, h=h: (ix(n), CB_Z + h)) for h in range(B_HEADS)]
    specs += [
        pl.BlockSpec((C, LANE), lambda n: (ix(n), CB_LG)),
        pl.BlockSpec((B_HEADS, 1, 1), lambda n: (0, 0, 0)),
        pl.BlockSpec((B_HEADS, 1, 1), lambda n: (0, 0, 0)),
        pl.BlockSpec((1, LANE), lambda n: (0, 0)),
    ]
    return specs


def _delta_inputs(u, c_ref, z_refs, lg, al, dt, nw):
    H = B_HEADS
    rows = slice(u * B_CHUNK, (u + 1) * B_CHUNK)
    part = lambda p: jnp.stack([c_ref[rows, LANE * (p * H + h):LANE * (p * H + h + 1)] for h in range(H)])
    return (part(0), part(1), part(2), jnp.stack([z[rows, :] for z in z_refs]), lg[rows, :], al[...], dt[...], nw[...])


def _delta_fwd(name, c, proj, a_log, dt_bias, norm_w, o_ab):
    T = c.shape[0]
    C = B_CHUNK
    N = T // C
    Dh = B_HEAD_DIM
    H = B_HEADS

    def body(*refs):
        c_ref, z_refs, (lg, al, dt, nw) = refs[0], refs[1:1 + H], refs[1 + H:5 + H]
        o_ref, s_ref, t_ref, S = refs[6 + H:]

        @pl.when(pl.program_id(0) == 0)
        def _():
            S[...] = jnp.zeros_like(S)

        s = S[...]
        for u in range(U):
            s_ref[:, u] = s
            ob, s, t = _delta_chunk_fn(*_delta_inputs(u, c_ref, z_refs, lg, al, dt, nw), s, want_t=True)
            for h in range(H):
                o_ref[u * C:(u + 1) * C, LANE * h:LANE * (h + 1)] = ob[h].astype(o_ref.dtype)
            t_ref[:, u] = t
        S[...] = s

    U = DELTA_CHUNKS_PER_STEP
    return pl.pallas_call(
        body, name=name, grid=(N // U,),
        in_specs=_delta_in_specs(False, N // U) + [pl.BlockSpec(memory_space=pl.ANY)],
        out_specs=[pl.BlockSpec((U * C, H * LANE), lambda n: (n, 1)),
                   pl.BlockSpec((H, U, Dh, Dh), lambda n: (0, n, 0, 0)),
                   pl.BlockSpec((H, U, C, C), lambda n: (0, n, 0, 0))],
        out_shape=[jax.ShapeDtypeStruct(o_ab.shape, o_ab.dtype), jax.ShapeDtypeStruct((H, N, Dh, Dh), f32),
                   jax.ShapeDtypeStruct((H, N, C, C), f32)],
        input_output_aliases={5 + H: 0},
        scratch_shapes=[pltpu.VMEM((H, Dh, Dh), f32)],
        compiler_params=_cparams("arbitrary"),
    )(c, *([proj] * H), proj, a_log, dt_bias, norm_w, o_ab)


def _delta_bwd(name, c, proj, a_log, dt_bias, norm_w, s_saved, t_saved, d_oab, dproj):
    T = c.shape[0]
    C = B_CHUNK
    N = T // C
    Dh = B_HEAD_DIM
    H = B_HEADS

    def body(*refs):
        c_ref, z_refs, (lg, al, dt, nw) = refs[0], refs[1:1 + H], refs[1 + H:5 + H]
        s_ref, t_ref, do_ref = refs[5 + H:8 + H]
        dc, dtail, dal, ddt, dnw, dS = refs[9 + H:]

        @pl.when(pl.program_id(0) == 0)
        def _():
            dS[...] = jnp.zeros_like(dS)
            dal[...] = jnp.zeros_like(dal)
            ddt[...] = jnp.zeros_like(ddt)
            dnw[...] = jnp.zeros_like(dnw)

        ds = dS[...]
        for u in reversed(range(U)):
            rows = slice(u * C, (u + 1) * C)
            _, vjp = jax.vjp(functools.partial(_delta_chunk_fn, t_saved=t_ref[:, u]),
                             *_delta_inputs(u, c_ref, z_refs, lg, al, dt, nw), s_ref[:, u])
            do = jnp.stack([do_ref[rows, LANE * h:LANE * (h + 1)] for h in range(H)])
            g = vjp((do, ds))
            for h in range(H):
                for p in range(3):
                    dc[rows, LANE * (p * H + h):LANE * (p * H + h + 1)] = g[p][h]
                dtail[rows, LANE * h:LANE * (h + 1)] = g[3][h].astype(dtail.dtype)
            dtail[rows, LANE * H:LANE * (H + 1)] = g[4].astype(dtail.dtype)
            dtail[rows, LANE * (H + 1):] = jnp.zeros((C, LANE), dtail.dtype)
            dal[...] += g[5]
            ddt[...] += g[6]
            dnw[...] += g[7]
            ds = g[8]
        dS[...] = ds

    U = DELTA_CHUNKS_PER_STEP
    NB = N // U
    rn = lambda n: NB - 1 - n
    return pl.pallas_call(
        body, name=name, grid=(NB,),
        in_specs=_delta_in_specs(True, NB) + [
            pl.BlockSpec((H, U, Dh, Dh), lambda n: (0, rn(n), 0, 0)),
            pl.BlockSpec((H, U, C, C), lambda n: (0, rn(n), 0, 0)),
            pl.BlockSpec((U * C, H * LANE), lambda n: (rn(n), 1)),
            pl.BlockSpec(memory_space=pl.ANY),
        ],
        out_specs=[
            pl.BlockSpec((U * C, 3 * H * LANE), lambda n: (rn(n), 0)),
            pl.BlockSpec((U * C, (H + 2) * LANE), lambda n: (rn(n), CB_Z // (H + 2))),
            pl.BlockSpec((H, 1, 1), lambda n: (0, 0, 0)),
            pl.BlockSpec((H, 1, 1), lambda n: (0, 0, 0)),
            pl.BlockSpec((1, LANE), lambda n: (0, 0)),
        ],
        out_shape=[jax.ShapeDtypeStruct((T, 3 * H * Dh), f32), jax.ShapeDtypeStruct(dproj.shape, dproj.dtype),
                   jax.ShapeDtypeStruct((H, 1, 1), f32), jax.ShapeDtypeStruct((H, 1, 1), f32),
                   jax.ShapeDtypeStruct((1, LANE), f32)],
        input_output_aliases={8 + H: 1},
        scratch_shapes=[pltpu.VMEM((H, Dh, Dh), f32)],
        compiler_params=_cparams("arbitrary"),
    )(c, *([proj] * H), proj, a_log, dt_bias, norm_w, s_saved, t_saved, d_oab, dproj)


def _gate_matmuls(xc, wa_ref, wx_ref):
    bw = wa_ref.shape[-1]
    xb = xc.astype(bf16)
    blocks = [xb[:, bw * h:bw * (h + 1)] for h in range(LRU_BLOCKS)]
    return (jnp.concatenate([_dot(blocks[h], wa_ref[h], NN) for h in range(LRU_BLOCKS)], axis=1),
            jnp.concatenate([_dot(blocks[h], wx_ref[h], NN) for h in range(LRU_BLOCKS)], axis=1))


def _gates_fwd(name, xc, w_a, w_x, pars, tm=512):
    T, Wd = xc.shape
    tm = min(tm, T)

    def body(x_ref, wa_ref, wx_ref, ba, bx, lam, a_ref, b_ref):
        x = x_ref[...]
        pr, pi = _gate_matmuls(x, wa_ref, wx_ref)
        a_ref[...], b_ref[...] = _rglru_pre_fn(pr, pi, x, ba[...], bx[...], lam[...])

    row = pl.BlockSpec((tm, Wd), lambda i: (i, 0))
    return pl.pallas_call(
        body, name=name, grid=(T // tm,),
        in_specs=[row, _whole_spec(w_a), _whole_spec(w_x)] + [_whole_spec(p) for p in pars],
        out_specs=[row, row], out_shape=[jax.ShapeDtypeStruct((T, Wd), f32)] * 2,
        compiler_params=_cparams("parallel"),
    )(xc, w_a, w_x, *pars)


def _gates_bwd(name, xc, w_a, w_x, pars, lam_t, h_prev, tm=512):
    T, Wd = xc.shape
    tm = min(tm, T)
    bw = Wd // LRU_BLOCKS

    def body(x_ref, wa_ref, wx_ref, ba, bx, lam, lt_ref, hp_ref, dx_ref, dr_ref, di_ref, dba, dbx, dlam):
        x = x_ref[...]
        pr, pi = _gate_matmuls(x, wa_ref, wx_ref)
        _, vjp = jax.vjp(_rglru_pre_fn, pr, pi, x, ba[...], bx[...], lam[...])
        lt = lt_ref[...]
        dpr, dpi, dxc, g_ba, g_bx, g_lam = vjp((lt * hp_ref[...], lt))
        dprb, dpib = dpr.astype(bf16), dpi.astype(bf16)
        dx_ref[...] = dxc + jnp.concatenate(
            [_dot(dprb[:, bw * h:bw * (h + 1)], wa_ref[h], NT) + _dot(dpib[:, bw * h:bw * (h + 1)], wx_ref[h], NT)
             for h in range(LRU_BLOCKS)], axis=1)
        dr_ref[...] = dprb
        di_ref[...] = dpib

        @pl.when(pl.program_id(0) == 0)
        def _():
            dba[...] = jnp.zeros_like(dba)
            dbx[...] = jnp.zeros_like(dbx)
            dlam[...] = jnp.zeros_like(dlam)

        dba[...] += g_ba
        dbx[...] += g_bx
        dlam[...] += g_lam

    row = pl.BlockSpec((tm, Wd), lambda i: (i, 0))
    vec = pl.BlockSpec((1, Wd), lambda i: (0, 0))
    return pl.pallas_call(
        body, name=name, grid=(T // tm,),
        in_specs=[row, _whole_spec(w_a), _whole_spec(w_x)] + [_whole_spec(p) for p in pars] + [row, row],
        out_specs=[row, row, row, vec, vec, vec],
        out_shape=[jax.ShapeDtypeStruct((T, Wd), f32), jax.ShapeDtypeStruct((T, Wd), bf16),
                   jax.ShapeDtypeStruct((T, Wd), bf16)] + [jax.ShapeDtypeStruct((1, Wd), f32)] * 3,
        compiler_params=_cparams("arbitrary"),
    )(xc, w_a, w_x, *pars, lam_t, h_prev)


def _blockdiag_bwd_dw(name, xc, dpr, dpi, tk=512):
    T, Wd = xc.shape
    bw = Wd // LRU_BLOCKS
    tk = min(tk, T)

    def body(x_ref, dr, di, oa, ox):
        @pl.when(pl.program_id(1) == 0)
        def _():
            oa[...] = jnp.zeros_like(oa)
            ox[...] = jnp.zeros_like(ox)

        xb = x_ref[...].astype(bf16)
        oa[...] += _dot(xb, dr[...].astype(bf16), TN)
        ox[...] += _dot(xb, di[...].astype(bf16), TN)

    xs = pl.BlockSpec((tk, bw), lambda h, k: (k, h))
    ws = pl.BlockSpec((None, bw, bw), lambda h, k: (h, 0, 0))
    return pl.pallas_call(
        body, name=name, grid=(LRU_BLOCKS, T // tk), in_specs=[xs, xs, xs], out_specs=[ws, ws],
        out_shape=[jax.ShapeDtypeStruct((LRU_BLOCKS, bw, bw), f32)] * 2,
        compiler_params=_cparams("parallel", "arbitrary"),
    )(xc, dpr, dpi)


def _scan(name, a, proj, reverse, b=None, h=None, dhg=None, tt=512, cb=512):
    T, Wd = a.shape
    tt, cb = min(tt, T), min(cb, Wd)
    nt = T // tt
    ng = tt // SUBLANE

    def body(a_ref, g_ref, *rest):
        n_in = 2 if reverse else 1
        ins, outs, (carry, carry_a) = rest[:n_in], rest[n_in:-2], rest[-2:]

        @pl.when(pl.program_id(1) == 0)
        def _():
            carry[...] = jnp.zeros_like(carry)
            carry_a[...] = jnp.zeros_like(carry_a)

        row = lax.broadcasted_iota(jnp.int32, (SUBLANE, cb), 0)

        def step(gi, c):
            hp, ap = c
            g = (ng - 1 - gi) if reverse else gi
            rows = pl.ds(pl.multiple_of(g * SUBLANE, SUBLANE), SUBLANE)
            A = a_ref[rows, :]
            gate = g_ref[rows, :]
            a_first = jnp.broadcast_to(A[0:1, :], (SUBLANE, cb))
            if reverse:
                _, vjp = jax.vjp(_rec_gate_fn, ins[0][rows, :], gate)
                B, dgate = vjp((ins[1][rows, :],))
                outs[1][rows, :] = dgate
                A = jnp.where(row == SUBLANE - 1, ap, pltpu.roll(A, SUBLANE - 1, axis=0))
            else:
                B = ins[0][rows, :]
            for s in (1, 2, 4):
                sh = (SUBLANE - s) if reverse else s
                As = pltpu.roll(A, sh, axis=0)
                Bs = pltpu.roll(B, sh, axis=0)
                valid = (row < SUBLANE - s) if reverse else (row >= s)
                B = jnp.where(valid, A * Bs + B, B)
                A = jnp.where(valid, A * As, A)
            hcur = A * hp + B
            outs[0][rows, :] = hcur
            if not reverse:
                outs[1][rows, :] = jnp.where(row == 0, hp, pltpu.roll(hcur, 1, axis=0))
                outs[2][rows, :] = _rec_gate_fn(hcur, gate)[0]
            edge = hcur[0:1, :] if reverse else hcur[SUBLANE - 1:SUBLANE, :]
            return jnp.broadcast_to(edge, (SUBLANE, cb)), a_first

        carry[...], carry_a[...] = lax.fori_loop(0, ng, step, (carry[...], carry_a[...]))

    nc = Wd // cb
    tok = (lambda i: nt - 1 - i) if reverse else (lambda i: i)
    spec = pl.BlockSpec((tt, cb), lambda c, i: (tok(i), c))
    gate_half = pl.BlockSpec((tt, cb), lambda c, i: (tok(i), nc + c))
    if reverse:
        args, out_specs = (a, proj, h, dhg), [spec, gate_half]
        out_shape = [jax.ShapeDtypeStruct((T, Wd), f32), jax.ShapeDtypeStruct((T, 2 * Wd), f32)]
    else:
        args, out_specs = (a, proj, b), [spec] * 3
        out_shape = [jax.ShapeDtypeStruct((T, Wd), f32)] * 3
    return pl.pallas_call(
        body, name=name, grid=(nc, nt), in_specs=[spec, gate_half] + [spec] * (len(args) - 2), out_specs=out_specs,
        out_shape=out_shape,
        scratch_shapes=[pltpu.VMEM((SUBLANE, cb), f32), pltpu.VMEM((SUBLANE, cb), f32)],
        compiler_params=_cparams("parallel", "arbitrary"),
    )(*args)


def _relu2_epilogue(r):
    h = jnp.maximum(r, 0.0)
    return r, h * h


def _drelu2_epilogue(r, a):
    return (r * (2.0 * jnp.maximum(a.astype(f32), 0.0)),)


def _residual_cot(through, upper):
    return (through + DN_ALPHA * upper,)


def _merge_cols(name, g, tm=256):
    _, L, R, s = g.shape

    def body(g_ref, o_ref):
        for d in range(N_DEV):
            o_ref[:, s * d:s * (d + 1)] = g_ref[d].astype(bf16)
        o_ref[:, N_DEV * s:] = jnp.zeros((tm, HYB_PROJ_PAD - N_DEV * s), bf16)

    return pl.pallas_call(
        body, name=name, grid=(L, R // tm),
        in_specs=[pl.BlockSpec((N_DEV, None, tm, s), lambda l, i: (0, l, i, 0))],
        out_specs=pl.BlockSpec((None, tm, HYB_PROJ_PAD), lambda l, i: (l, i, 0)),
        out_shape=jax.ShapeDtypeStruct((L, R, HYB_PROJ_PAD), bf16),
        compiler_params=_cparams("parallel", "parallel"),
    )(g)


def _split_cols(name, dw, tm=256):
    R = dw.shape[0]
    s = HYB_PROJ // N_DEV

    def body(g_ref, o_ref):
        for d in range(N_DEV):
            o_ref[d] = g_ref[:, s * d:s * (d + 1)].astype(bf16)

    return pl.pallas_call(
        body, name=name, grid=(R // tm,),
        in_specs=[pl.BlockSpec((tm, HYB_PROJ_PAD), lambda i: (i, 0))],
        out_specs=pl.BlockSpec((N_DEV, tm, s), lambda i: (0, i, 0)),
        out_shape=jax.ShapeDtypeStruct((N_DEV, R, s), bf16),
        compiler_params=_cparams("parallel"),
    )(dw)


def _rows_to_dev(dw):
    nb, r, c = dw.shape
    t = dw.reshape(nb, N_DEV, r // N_DEV, c)
    return jnp.moveaxis(t, 1, 0).reshape(N_DEV, nb * (r // N_DEV), c).astype(bf16)


def _ln_epilogue(r, x, g, b):
    y = _ln_res_fn(x, r, g, b)[0]
    return r, y, y


def _hybrid_fwd(tag, x, xb, W, j, cos, sin, ln):
    proj = _mm(f"{tag}_proj", xb, W["hyb_w_in"][j], "nn", b_kind="lead", b_lead=0)
    o_a = _attn_fwd(f"{tag}_attn", proj, cos, sin, W["hyb_sinks"][j][None, :])
    c = _conv_fwd(f"{tag}_conv", proj, CB_CONV, 12, W["hyb_conv_w"][j], None)
    o_ab, s_saved, t_saved = _delta_fwd(f"{tag}_delta", c, proj, W["hyb_a_log"][j].reshape(B_HEADS, 1, 1),
                                        W["hyb_dt_bias"][j].reshape(B_HEADS, 1, 1), W["hyb_norm_w"][j][None, :], o_a)
    mix, x1, x1b = _mm(f"{tag}_out", o_ab, W["hyb_w_out"][j], "nn", b_kind="lead", b_lead=0, epilogue=_ln_epilogue,
                       extras=(x,), params=ln, out_dtypes=(f32, f32, bf16), tm=512)
    return mix, x1, x1b, (proj, c, s_saved, t_saved, o_ab)


def _hybrid_bwd(tag, x, dmix, addend, W, j, cos, sin, saved, G, send_early):
    proj, c, s_saved, t_saved, o_ab = saved
    T = x.shape[0]
    d_oab = _mm(f"{tag}_dout", dmix, W["hyb_w_out"][j], "nt", b_kind="lead", b_lead=0)
    G["hyb_w_out"][j] = _mm(f"{tag}_dwout", o_ab, dmix, "tn", out_dtypes=(bf16,)).reshape(N_DEV, -1, D_MODEL)
    sinks = W["hyb_sinks"][j][None, :] + send_early({("hyb_w_out", j): G["hyb_w_out"][j]})
    dproj, dsinks = _attn_bwd(f"{tag}_dattn", proj, cos, sin, sinks, d_oab)
    a_log = W["hyb_a_log"][j].reshape(B_HEADS, 1, 1)
    dt_bias = W["hyb_dt_bias"][j].reshape(B_HEADS, 1, 1)
    dc, dproj, dal, ddt, dnw = _delta_bwd(f"{tag}_ddelta", c, proj, a_log, dt_bias, W["hyb_norm_w"][j][None, :],
                                          s_saved, t_saved, d_oab, dproj)
    dproj, dconv_w, _ = _conv_bwd(f"{tag}_dconv", dc, proj, CB_CONV, 12, W["hyb_conv_w"][j], dproj, CB_CONV)
    dx = _mm(f"{tag}_dx", dproj, W["hyb_w_in"][j], "nt", b_kind="lead", b_lead=0,
             **({} if addend is None else dict(epilogue=_residual_cot, extras=(addend,))))
    G["hyb_w_in"][j] = _split_cols(f"{tag}_dwin_split", _mm(f"{tag}_dwin", x, dproj, "tn", tn=1536))
    G["hyb_sinks"][j] = dsinks[0]
    G["hyb_conv_w"][j] = dconv_w
    G["hyb_a_log"][j] = dal.reshape(B_HEADS)
    G["hyb_dt_bias"][j] = ddt.reshape(B_HEADS)
    G["hyb_norm_w"][j] = dnw[0]
    return dx


def _rec_fwd(tag, x, xb, W, j, ln):
    Wd = D_MODEL
    proj = _mm(f"{tag}_proj", xb, W["rec_w_in"][j], "nn", b_kind="devcol", b_lead=0)
    xc = _conv_fwd(f"{tag}_conv", proj, 0, Wd // LANE, W["rec_conv_w"][j], W["rec_conv_b"][j][None, :])
    pars = [W["rec_b_a"][j][None, :], W["rec_b_x"][j][None, :], W["rec_lambda"][j][None, :]]
    a, b = _gates_fwd(f"{tag}_gates", xc, W["rec_w_a"][j][0], W["rec_w_x"][j][0], pars)
    h, h_prev, hg = _scan(f"{tag}_scan", a, proj, False, b=b)
    mix, x1, x1b = _mm(f"{tag}_out", hg, W["rec_w_out"][j], "nn", b_kind="lead", b_lead=0, epilogue=_ln_epilogue,
                       extras=(x,), params=ln, out_dtypes=(f32, f32, bf16), tm=512)
    return mix, x1, x1b, (proj, xc, a, h, h_prev, hg)


def _rec_bwd(tag, x, dmix, addend, W, j, saved, G, send_early):
    proj, xc, a, h, h_prev, hg = saved
    Wd = D_MODEL
    dhg = _mm(f"{tag}_dout", dmix, W["rec_w_out"][j], "nt", b_kind="lead", b_lead=0)
    G["rec_w_out"][j] = _mm(f"{tag}_dwout", hg, dmix, "tn", out_dtypes=(bf16,)).reshape(N_DEV, -1, D_MODEL)
    sent = send_early({("rec_w_out", j): G["rec_w_out"][j]})
    lam_t, dproj = _scan(f"{tag}_dscan", a, proj, True, h=h, dhg=dhg)
    pars = [W["rec_b_a"][j][None, :] + sent, W["rec_b_x"][j][None, :], W["rec_lambda"][j][None, :]]
    dxc, dpr, dpi, db_a, db_x, dlam = _gates_bwd(f"{tag}_dgates", xc, W["rec_w_a"][j][0], W["rec_w_x"][j][0], pars,
                                                 lam_t, h_prev)
    dwa, dwx = _blockdiag_bwd_dw(f"{tag}_dgates_dw", xc, dpr, dpi)
    G["rec_w_a"][j], G["rec_w_x"][j] = _rows_to_dev(dwa), _rows_to_dev(dwx)
    dproj, dconv_w, dconv_b = _conv_bwd(f"{tag}_dconv", dxc, proj, 0, Wd // LANE, W["rec_conv_w"][j], dproj, 0)
    dx = _mm(f"{tag}_dx", dproj, W["rec_w_in"][j], "nt", b_kind="devcol", b_lead=0,
             **({} if addend is None else dict(epilogue=_residual_cot, extras=(addend,))))
    G["rec_w_in"][j] = _mm(f"{tag}_dwin", x, dproj, "tn", o_kind="devcol", out_dtypes=(bf16,), tn=2048)
    G["rec_conv_w"][j] = dconv_w
    G["rec_conv_b"][j] = dconv_b
    G["rec_b_a"][j] = db_a[0]
    G["rec_b_x"][j] = db_x[0]
    G["rec_lambda"][j] = dlam[0]
    return dx


def _local_step(x, target, W, load_layer, grads_ready):
    T = x.shape[0]
    cos, sin = _rope_tables(T)
    saved = []
    xb = x
    for layer in range(DEPTH):
        j = layer // 2
        tag = f"L{layer}"
        load_layer(layer, "mixer", x)
        ln1 = (W["ln1_g"][layer][None, :], W["ln1_b"][layer][None, :])
        if layer % 2 == 0:
            mix, x1, x1b, sv = _hybrid_fwd(tag, x, xb, W, j, cos, sin, ln1)
        else:
            mix, x1, x1b, sv = _rec_fwd(tag, x, xb, W, j, ln1)
        load_layer(layer, "mlp", x1)
        a, h2 = _mm(f"{tag}_mlp1", x1b, W["mlp_w1"][layer], "nn", b_kind="devcol", b_lead=0, epilogue=_relu2_epilogue,
                    out_dtypes=(bf16, bf16), tm=2048)
        ln2 = (W["ln2_g"][layer][None, :], W["ln2_b"][layer][None, :])
        y, x2, x2b = _mm(f"{tag}_mlp2", h2, W["mlp_w2"][layer], "nn", b_kind="lead", b_lead=0, epilogue=_ln_epilogue,
                         extras=(x1,), params=ln2, out_dtypes=(f32, f32, bf16))
        saved.append((x, xb, sv, mix, x1, x1b, a, h2, y))
        x, xb = x2, x2b
    loss, dx = _loss_head(x, target)

    G = {k: [None] * (DEPTH if k.startswith(("ln", "mlp")) else DEPTH // 2) for k in (
        "hyb_w_in", "hyb_sinks", "hyb_conv_w", "hyb_a_log", "hyb_dt_bias", "hyb_norm_w", "hyb_w_out",
        "rec_w_in", "rec_conv_w", "rec_conv_b", "rec_w_a", "rec_b_a", "rec_w_x", "rec_b_x", "rec_lambda", "rec_w_out",
        "ln1_g", "ln1_b", "mlp_w1", "mlp_w2", "ln2_g", "ln2_b")}
    held = {}
    cot_rows, cot_fn = [(dx, 0, D_MODEL)], None
    for layer in reversed(range(DEPTH)):
        j = layer // 2
        tag = f"L{layer}"
        x0, x0b, sv, mix, x1, x1b, a, h2, y = saved[layer]
        ln2 = [W["ln2_g"][layer][None, :], W["ln2_b"][layer][None, :]]
        (dy, dyb), (dg2, db2) = _tl_bwd(f"{tag}_dln2", _ln_res_fn, [(x1, 0, D_MODEL), (y, 0, D_MODEL)], ln2,
                                        cot_rows, cot_fn=cot_fn, skip=(0,), bf16_copy=True)
        G["ln2_g"][layer], G["ln2_b"][layer] = dg2[0], db2[0]
        da = _mm(f"{tag}_dmlp2", dyb, W["mlp_w2"][layer], "nt", b_kind="lead", b_lead=0, epilogue=_drelu2_epilogue,
                 extras=(a,), out_dtypes=(bf16,), tm=2048, tn=512)
        G["mlp_w2"][layer] = _mm(f"{tag}_dw2", h2, dyb, "tn", out_dtypes=(bf16,), tm=2048).reshape(N_DEV, -1, D_MODEL)
        dx1 = _mm(f"{tag}_dmlp1", da, W["mlp_w1"][layer], "nt", b_kind="devcol", b_lead=0, tm=2048)
        G["mlp_w1"][layer] = _mm(f"{tag}_dw1", x1b, da, "tn", o_kind="devcol", out_dtypes=(bf16,), tn=2048)
        ln1 = [W["ln1_g"][layer][None, :], W["ln1_b"][layer][None, :]]
        (dmix, dmixb), (dg1, db1) = _tl_bwd(f"{tag}_dln1", _ln_res_fn, [(x0, 0, D_MODEL), (mix, 0, D_MODEL)], ln1,
                                            [(dx1, 0, D_MODEL), (dy, 0, D_MODEL)], cot_fn=_residual_cot, skip=(0,),
                                            bf16_copy=True)
        G["ln1_g"][layer], G["ln1_b"][layer] = dg1[0], db1[0]
        dx0_a = dmix if layer == 0 else None
        held.update({(k, layer): G[k][layer] for k in ("mlp_w1", "mlp_w2")})
        early = functools.partial(grads_ready, f"l{layer}_early", held)
        if layer % 2 == 0:
            dx = _hybrid_bwd(tag, x0b, dmixb, dx0_a, W, j, cos, sin, sv, G, early)
        else:
            dx = _rec_bwd(tag, x0b, dmixb, dx0_a, W, j, sv, G, early)
        held = {(k, i): G[k][i] for k, i in _layer_weights(layer)[:-2] if not k.endswith("w_out")}
        cot_rows, cot_fn = [(dx, 0, D_MODEL), (dmix, 0, D_MODEL)], _residual_cot
    grads_ready("l0_late", held, {})
    big = {k for k, _ in BIG}
    return loss, dx, {k: jnp.stack(v) for k, v in G.items() if k not in big}


def _layer_weights(layer):
    j = layer // 2
    mixer = ["hyb_w_in", "hyb_w_out"] if layer % 2 == 0 else ["rec_w_in", "rec_w_out", "rec_w_a", "rec_w_x"]
    return [(k, j) for k in mixer] + [("mlp_w1", layer), ("mlp_w2", layer)]


def _my_coords():
    return lax.axis_index("x"), lax.axis_index("y"), lax.axis_index("c")


def _all_gather(name, arrays):
    na = len(arrays)

    def body(*refs):
        x_refs, out_refs = refs[:na], refs[na:2 * na]
        send_sems, recv_sems, local_sems = refs[2 * na:]
        x, y, c = _my_coords()
        me, sibling = (x, y, c), (x, y, 1 - c)
        chips = [(1 - x, y), (x, 1 - y), (1 - x, 1 - y)]

        def blk(a, px, py, pc):
            return out_refs[a].at[4 * px + 2 * py + pc]

        def copy(a, k, block, to, src=None):
            return pltpu.make_async_remote_copy(
                src_ref=blk(a, *block) if src is None else src, dst_ref=blk(a, *block),
                send_sem=send_sems.at[a, k], recv_sem=recv_sems.at[a, k],
                device_id=to, device_id_type=pl.DeviceIdType.MESH)

        mine = [pltpu.make_async_copy(x_refs[a], blk(a, *me), local_sems.at[a]) for a in range(na)]
        for cp in mine:
            cp.start()
        first = []
        for a in range(na):
            first.append(copy(a, 0, me, sibling, src=x_refs[a]))
            first += [copy(a, 1 + j, me, (*chip, c), src=x_refs[a]) for j, chip in enumerate(chips)]
        for cp in first:
            cp.start()
        passed = []
        for a in range(na):
            for j, chip in enumerate(chips):
                copy(a, 1 + j, (*chip, c), me).wait_recv()
                passed.append(copy(a, 4 + j, (*chip, c), sibling))
                passed[-1].start()
        for a in range(na):
            copy(a, 0, sibling, me).wait_recv()
            for j, chip in enumerate(chips):
                copy(a, 4 + j, (*chip, 1 - c), me).wait_recv()
        for cp in first + passed:
            cp.wait_send()
        for cp in mine:
            cp.wait()

    return pl.pallas_call(
        body, name=name,
        out_shape=[jax.ShapeDtypeStruct((N_DEV,) + a.shape, a.dtype) for a in arrays],
        in_specs=[pl.BlockSpec(memory_space=pl.ANY)] * na,
        out_specs=[pl.BlockSpec(memory_space=pl.ANY)] * na,
        scratch_shapes=[pltpu.SemaphoreType.DMA((na, 7)), pltpu.SemaphoreType.DMA((na, 7)),
                        pltpu.SemaphoreType.DMA((na,))],
    )(*arrays)


_HBM = pl.BlockSpec(memory_space=pltpu.HBM)
_SEM = pl.BlockSpec(memory_space=pltpu.SEMAPHORE)


def _flip(k, x, y, c):
    return ((1 - x) if k & 4 else x, (1 - y) if k & 2 else y, (1 - c) if k & 1 else c)


_PEERS = {"gather": (1, 2, 4, 6), "scatter": (1, 2, 3, 4, 5, 6, 7)}


def _push_copies(kind, x_refs, land_refs, send_sems, recv_sems, local_sems):
    x, y, c = _my_coords()
    me = 4 * x + 2 * y + c
    peers = _PEERS[kind]
    remote, local = [], []
    for a in range(len(x_refs)):
        local.append(pltpu.make_async_copy(x_refs[a] if kind == "gather" else x_refs[a].at[me], land_refs[a].at[me],
                                           local_sems.at[a]))
        for n, k in enumerate(peers):
            px, py, pc = _flip(k, x, y, c)
            remote.append(pltpu.make_async_remote_copy(
                src_ref=x_refs[a] if kind == "gather" else x_refs[a].at[4 * px + 2 * py + pc],
                dst_ref=land_refs[a].at[me],
                send_sem=send_sems.at[a * len(peers) + n], recv_sem=recv_sems.at[a * len(peers) + n],
                device_id=(px, py, pc), device_id_type=pl.DeviceIdType.MESH))
    return remote, local


def _pass_to_sibling(name, lands):
    na = len(lands)
    chips = (2, 4, 6)

    def body(*refs):
        out_refs, send_sems, recv_sems = refs[na:2 * na], refs[2 * na], refs[2 * na + 1]
        x, y, c = _my_coords()
        cps = []
        for a in range(na):
            for n, k in enumerate(chips):
                px, py, _ = _flip(k, x, y, c)
                cps.append(pltpu.make_async_remote_copy(
                    src_ref=out_refs[a].at[4 * px + 2 * py + c], dst_ref=out_refs[a].at[4 * px + 2 * py + c],
                    send_sem=send_sems.at[a * 3 + n], recv_sem=recv_sems.at[a * 3 + n],
                    device_id=(x, y, 1 - c), device_id_type=pl.DeviceIdType.MESH))
        for cp in cps:
            cp.start()
        for a in range(na):
            for n, k in enumerate(chips):
                px, py, _ = _flip(k, x, y, c)
                blk = out_refs[a].at[4 * px + 2 * py + (1 - c)]
                pltpu.make_async_remote_copy(src_ref=blk, dst_ref=blk, send_sem=send_sems.at[a * 3 + n],
                                             recv_sem=recv_sems.at[a * 3 + n], device_id=(x, y, 1 - c),
                                             device_id_type=pl.DeviceIdType.MESH).wait_recv()
        for cp in cps:
            cp.wait_send()

    return pl.pallas_call(
        body, name=name,
        out_shape=[jax.ShapeDtypeStruct(l.shape, l.dtype) for l in lands],
        in_specs=[pl.BlockSpec(memory_space=pl.ANY)] * na,
        out_specs=[pl.BlockSpec(memory_space=pl.ANY)] * na,
        input_output_aliases={a: a for a in range(na)},
        scratch_shapes=[pltpu.SemaphoreType.DMA((3 * na,)), pltpu.SemaphoreType.DMA((3 * na,))],
    )(*lands)


_SIDE_EFFECT = pltpu.CompilerParams(has_side_effects=pltpu.SideEffectType.DATAFLOW_SIDE_EFFECTING)


def _push_start(name, kind, srcs, lands):
    na = len(srcs)

    def body(*refs):
        remote, local = _push_copies(kind, refs[:na], refs[na:2 * na], *refs[2 * na:2 * na + 3])
        for cp in remote + local:
            cp.start()
        token = refs[-1]
        token[...] = jnp.zeros_like(token)

    arrays = list(srcs) + list(lands)
    n_remote = na * len(_PEERS[kind])
    res = pl.pallas_call(
        body, name=name,
        out_shape=(pltpu.SemaphoreType.DMA((n_remote,)), pltpu.SemaphoreType.DMA((n_remote,)),
                   pltpu.SemaphoreType.DMA((na,)), *[pltpu.HBM(t.shape, t.dtype) for t in arrays],
                   jax.ShapeDtypeStruct((SUBLANE, LANE), f32)),
        in_specs=[_HBM] * (2 * na),
        out_specs=(_SEM, _SEM, _SEM, *[_HBM] * (2 * na), pl.BlockSpec(memory_space=pltpu.VMEM)),
        input_output_aliases={i: 3 + i for i in range(2 * na)},
        compiler_params=_SIDE_EFFECT,
    )(*[pltpu.with_memory_space_constraint(t, pltpu.HBM) for t in arrays])
    return list(res[:3]), res[3:3 + na], res[3 + na:3 + 2 * na], res[-1][:1, :1]


def _push_wait(name, kind, sems, srcs, lands, after):
    na = len(srcs)

    def body(*refs):
        remote, local = _push_copies(kind, refs[:na], refs[na:2 * na], *refs[2 * na:2 * na + 3])
        for cp in remote:
            cp.wait_send()
            cp.wait_recv()
        for cp in local:
            cp.wait()

    arrays = list(srcs) + list(lands)
    res = pl.pallas_call(
        body, name=name,
        out_shape=tuple(pltpu.HBM(t.shape, t.dtype) for t in arrays),
        in_specs=[_HBM] * (2 * na) + [_SEM] * 3 + [pl.BlockSpec(memory_space=pl.ANY)],
        out_specs=tuple([_HBM] * (2 * na)),
        input_output_aliases={i: i for i in range(2 * na)},
        compiler_params=_SIDE_EFFECT,
    )(*arrays, *sems, after)
    return res[na:]


def _sum_blocks(name, land):
    _, R, n = land.shape
    tr = R

    def body(l_ref, o_ref):
        acc = l_ref[0].astype(f32)
        for s in range(1, N_DEV):
            acc = acc + l_ref[s].astype(f32)
        o_ref[...] = acc

    return pl.pallas_call(
        body, name=name, grid=(R // tr,),
        in_specs=[pl.BlockSpec((N_DEV, tr, n), lambda i: (0, i, 0))],
        out_specs=pl.BlockSpec((tr, n), lambda i: (i, 0)),
        out_shape=jax.ShapeDtypeStruct((R, n), f32),
        compiler_params=_cparams("parallel"),
    )(land)


def _adamw(name, w, g, m, v):
    shape = w.shape
    last = shape[-1]
    rows = math.prod(shape[:-1])
    tm = 256 if rows % 256 == 0 and rows > 256 else rows
    w2, g2, m2, v2 = (t.reshape(rows, last) for t in (w, g, m, v))

    def body(w_ref, g_ref, m_ref, v_ref, d_ref, mo_ref, vo_ref):
        gg = g_ref[...]
        mn = ADAM_B1 * m_ref[...] + (1.0 - ADAM_B1) * gg
        vn = ADAM_B2 * v_ref[...] + (1.0 - ADAM_B2) * jnp.square(gg)
        m_hat = mn / (1.0 - ADAM_B1 ** ADAM_STEP)
        v_hat = vn / (1.0 - ADAM_B2 ** ADAM_STEP)
        d_ref[...] = -ADAM_LR * (m_hat / (jnp.sqrt(v_hat) + ADAM_EPS) + ADAM_WD * w_ref[...])
        mo_ref[...] = mn
        vo_ref[...] = vn

    spec = pl.BlockSpec((tm, last), lambda i: (i, 0))
    d, mn, vn = pl.pallas_call(
        body, name=name, grid=(rows // tm,), in_specs=[spec] * 4, out_specs=[spec] * 3,
        out_shape=[jax.ShapeDtypeStruct((rows, last), f32)] * 3,
        compiler_params=_cparams("parallel"),
    )(w2, g2, m2, v2)
    return d.reshape(shape), mn.reshape(shape), vn.reshape(shape)


def _adamw_land(name, lands, w, m, v, tm=256):
    L = len(lands)
    _, R, C = lands[0].shape
    tm = min(tm, R)

    def body(*refs):
        l_refs, (w_ref, m_ref, v_ref, g_ref, d_ref, mo_ref, vo_ref) = refs[:L], refs[L:]
        for k in range(L):
            @pl.when(pl.program_id(0) == k)
            def _(k=k):
                gg = l_refs[k][0].astype(f32)
                for s in range(1, N_DEV):
                    gg = gg + l_refs[k][s].astype(f32)
                g_ref[...] = gg
                mn = ADAM_B1 * m_ref[...] + (1.0 - ADAM_B1) * gg
                vn = ADAM_B2 * v_ref[...] + (1.0 - ADAM_B2) * jnp.square(gg)
                m_hat = mn / (1.0 - ADAM_B1 ** ADAM_STEP)
                v_hat = vn / (1.0 - ADAM_B2 ** ADAM_STEP)
                d_ref[...] = -ADAM_LR * (m_hat / (jnp.sqrt(v_hat) + ADAM_EPS) + ADAM_WD * w_ref[...])
                mo_ref[...] = mn
                vo_ref[...] = vn

    land_specs = [pl.BlockSpec((N_DEV, tm, C), lambda l, i, k=k: (0, jnp.where(l == k, i, 0), 0)) for k in range(L)]
    spec = pl.BlockSpec((None, tm, C), lambda l, i: (l, i, 0))
    return pl.pallas_call(
        body, name=name, grid=(L, R // tm),
        in_specs=land_specs + [spec] * 3,
        out_specs=[spec] * 4,
        out_shape=[jax.ShapeDtypeStruct((L, R, C), f32)] * 4,
        compiler_params=_cparams("arbitrary", "arbitrary"),
    )(*lands, w, m, v)


BIG = [("hyb_w_in", 2), ("hyb_w_out", 1), ("rec_w_in", 2), ("rec_w_out", 1), ("rec_w_a", 2), ("rec_w_x", 2),
       ("mlp_w1", 2), ("mlp_w2", 1)]
SMALL = [("hyb_conv_w", 2), ("rec_conv_w", 2), ("rec_conv_b", 1), ("rec_b_a", 1), ("rec_b_x", 1), ("rec_lambda", 1)]
REPL = ["hyb_sinks", "hyb_a_log", "hyb_dt_bias", "hyb_norm_w", "ln1_g", "ln1_b", "ln2_g", "ln2_b"]
WEIGHTS = ["hyb_w_in", "hyb_sinks", "hyb_conv_w", "hyb_a_log", "hyb_dt_bias", "hyb_norm_w", "hyb_w_out", "rec_w_in",
           "rec_conv_w", "rec_conv_b", "rec_w_a", "rec_b_a", "rec_w_x", "rec_b_x", "rec_lambda", "rec_w_out",
           "ln1_g", "ln1_b", "mlp_w1", "mlp_w2", "ln2_g", "ln2_b"]


def _pack_rows(parts, dtype, row_mult):
    lead = parts[0].shape[:-1]
    flat = jnp.concatenate([p.astype(dtype) for p in parts], axis=-1)
    n = flat.shape[-1]
    unit = row_mult * LANE
    pad = (-n) % unit
    if pad:
        flat = jnp.concatenate([flat, jnp.zeros(lead + (pad,), dtype)], axis=-1)
    return flat.reshape(lead + ((n + pad) // LANE, LANE))


def _gather_full(gathered, shard_shapes, table):
    flat = gathered.reshape(N_DEV, -1)
    out, off = {}, 0
    for name, ax in table:
        shp = shard_shapes[name]
        n = math.prod(shp)
        arr = flat[:, off:off + n].reshape((N_DEV,) + shp)
        off += n
        arr = jnp.moveaxis(arr, 0, ax)
        out[name] = arr.reshape(shp[:ax] + (N_DEV * shp[ax],) + shp[ax + 1:])
    return out


def _matmul_layouts(tag, gw):
    out = {}
    bw = D_MODEL // LRU_BLOCKS
    for k, g in gw.items():
        L = g.shape[1]
        if k == "hyb_w_in":
            out[k] = _merge_cols(f"{tag}_w_in_merge", g)
        elif k in ("hyb_w_out", "rec_w_out", "mlp_w2"):
            out[k] = jnp.swapaxes(g, 0, 1).reshape(L, N_DEV * g.shape[2], g.shape[3])
        elif k in ("rec_w_a", "rec_w_x"):
            out[k] = jnp.moveaxis(g, 0, 2).reshape(L, LRU_BLOCKS, bw, bw)
        else:
            out[k] = g
    return out


def kernel(x, hyb_w_in, hyb_sinks, hyb_conv_w, hyb_a_log, hyb_dt_bias, hyb_norm_w, hyb_w_out, rec_w_in, rec_conv_w, rec_conv_b, rec_w_a, rec_b_a, rec_w_x, rec_b_x, rec_lambda, rec_w_out, ln1_g, ln1_b, mlp_w1, mlp_w2, ln2_g, ln2_b, loss_target, m_hyb_w_in, m_hyb_sinks, m_hyb_conv_w, m_hyb_a_log, m_hyb_dt_bias, m_hyb_norm_w, m_hyb_w_out, m_rec_w_in, m_rec_conv_w, m_rec_conv_b, m_rec_w_a, m_rec_b_a, m_rec_w_x, m_rec_b_x, m_rec_lambda, m_rec_w_out, m_ln1_g, m_ln1_b, m_mlp_w1, m_mlp_w2, m_ln2_g, m_ln2_b, v_hyb_w_in, v_hyb_sinks, v_hyb_conv_w, v_hyb_a_log, v_hyb_dt_bias, v_hyb_norm_w, v_hyb_w_out, v_rec_w_in, v_rec_conv_w, v_rec_conv_b, v_rec_w_a, v_rec_b_a, v_rec_w_x, v_rec_b_x, v_rec_lambda, v_rec_w_out, v_ln1_g, v_ln1_b, v_mlp_w1, v_mlp_w2, v_ln2_g, v_ln2_b):
    args = locals()
    w = {k: args[k] for k in WEIGHTS}
    m = {k: args["m_" + k] for k in WEIGHTS}
    v = {k: args["v_" + k] for k in WEIGHTS}
    shard_shapes = {k: tuple(t.shape) for k, t in w.items()}
    xi, yi, ci = _my_coords()
    me = 4 * xi + 2 * yi + ci

    in_flight = {}

    def install(tag, names, got):
        for (k, i), arr in zip(names, _matmul_layouts(tag, {k: g for (k, _), g in zip(names, got)}).values()):
            W[k][i] = arr

    def start_gather(tag, names):
        srcs = [w[k][i:i + 1].astype(bf16) for k, i in names]
        *pending, zero = _push_start(f"gather_{tag}_start", "gather", srcs,
                                     [lax.empty((N_DEV,) + s.shape, bf16) for s in srcs])
        in_flight[tag] = (names, pending)
        return zero

    def finish_gather(tag, after):
        names, pending = in_flight.pop(tag)
        half = _push_wait(f"gather_{tag}_wait", "gather", *pending, after)
        install(tag, names, _pass_to_sibling(f"gather_{tag}_pass", half))

    def started(k, zero):
        W[k] = W[k] + zero

    def mixer_w(layer):
        return _layer_weights(layer)[:-2]

    def mlp_w(layer):
        return _layer_weights(layer)[-2:]

    gathered0 = _all_gather("gather_first", [w[k][i:i + 1].astype(bf16) for k, i in mixer_w(0)]
                            + [_pack_rows([w[k].reshape(-1) for k, _ in SMALL], f32, SUBLANE)])
    W = _gather_full(gathered0[-1], shard_shapes, SMALL)
    W.update({k: w[k] for k in REPL})
    W.update({k: {} for k, _ in BIG})
    install("l0a", mixer_w(0), gathered0[:-1])
    started("hyb_sinks", start_gather("l0b", mlp_w(0)) + start_gather("l1a", mixer_w(1)))

    def load_layer(layer, part, after):
        if part == "mixer":
            if layer == 1:
                finish_gather("l1a", after)
            if layer >= 2:
                finish_gather(f"l{layer}", after)
            if 1 <= layer < DEPTH - 1:
                started("hyb_sinks" if layer % 2 == 0 else "rec_conv_b",
                        start_gather(f"l{layer + 1}", _layer_weights(layer + 1)))
        elif layer == 0:
            finish_gather("l0b", after)
            started("ln2_g", start_gather("l1b", mlp_w(1)))
        elif layer == 1:
            finish_gather("l1b", after)

    grads_in_flight = {}

    def grads_ready(tag, a, b):
        g = {**a, **b}
        srcs = list(g.values())
        *pending, zero = _push_start(f"scatter_{tag}_start", "scatter", srcs, [lax.empty(s.shape, bf16) for s in srcs])
        grads_in_flight[tag] = (list(g.keys()), pending)
        return zero

    loss_local, grad_x, G = _local_step(x[0], loss_target[0], W, load_layer, grads_ready)
    loss = lax.psum(loss_local, MESH_AXES)

    landed = {}

    def land(tag, after):
        keys, pending = grads_in_flight[tag]
        landed.update(zip(keys, _push_wait(f"scatter_{tag}_wait", "scatter", *pending, after)))

    tags = list(grads_in_flight)
    for tag in tags[:-1]:
        land(tag, grad_x)
    rest = _pack_rows([G[k].reshape(-1) for k, _ in SMALL] + [G[k].reshape(-1) for k in REPL], f32, SUBLANE)
    g_rest = _sum_blocks("sum_rest", _all_gather("gather_rest", [rest])[0]).reshape(-1)

    grads, delta, new_m, new_v = {}, {}, {}, {}

    def adamw_big(k):
        shp = shard_shapes[k]
        s3 = (shp[0], math.prod(shp[1:-1]), shp[-1])
        lands = [landed[(k, i)].reshape((N_DEV,) + s3[1:]) for i in range(shp[0])]
        res = _adamw_land("adamw_" + k, lands, w[k].reshape(s3), m[k].reshape(s3), v[k].reshape(s3))
        grads[k], delta[k], new_m[k], new_v[k] = (r.reshape(shp) for r in res)

    late = {k for k, _ in grads_in_flight[tags[-1]][0]}
    for k in [k for k, _ in BIG if k not in late]:
        adamw_big(k)
        done = new_v[k]
    land(tags[-1], done)
    for k in [k for k, _ in BIG if k in late]:
        adamw_big(k)
    off = 0
    for k, ax in SMALL:
        full_shape = G[k].shape
        n = math.prod(full_shape)
        full = g_rest[off:off + n].reshape(full_shape)
        off += n
        s = shard_shapes[k][ax]
        grads[k] = lax.dynamic_slice_in_dim(full, me * s, s, axis=ax)
    for k in REPL:
        n = math.prod(shard_shapes[k])
        grads[k] = g_rest[off:off + n].reshape(shard_shapes[k])
        off += n

    for k in [k for k, _ in SMALL] + REPL:
        delta[k], new_m[k], new_v[k] = _adamw("adamw_" + k, w[k], grads[k], m[k], v[k])

    return (loss, grad_x[None], *[grads[k] for k in WEIGHTS], *[delta[k] for k in WEIGHTS],
            *[new_m[k] for k in WEIGHTS], *[new_v[k] for k in WEIGHTS])
```

```python
import functools
import math

import jax
import jax.numpy as jnp
from jax import lax
from jax.experimental import pallas as pl
from jax.experimental.pallas import tpu as pltpu

f32 = jnp.float32
bf16 = jnp.bfloat16

N_DEV = 8
D_MODEL = 1024
DEPTH = 4
A_HEAD_DIM = 64
A_Q_HEADS = 8
WINDOW = 128
ROPE_THETA = 10000.0
B_HEADS = 4
B_HEAD_DIM = 128
B_CHUNK = 64
LRU_BLOCKS = 4
LRU_C = 8.0
D_FF = 4 * D_MODEL
HYB_PROJ = 2824
HYB_PROJ_PAD = 3072
DN_ALPHA = (2 * DEPTH) ** 0.25
LN_EPS = 1e-5
NORM_EPS = 1e-6
ADAM_LR = 0.001
ADAM_B1 = 0.9
ADAM_B2 = 0.999
ADAM_EPS = 1e-08
ADAM_WD = 0.01
ADAM_STEP = 10

LANE = 128
SUBLANE = 8
VMEM_LIMIT = 48 * 1024 * 1024

CB_QA, CB_KA, CB_VA, CB_CONV, CB_Z, CB_LG = 0, 4, 5, 6, 18, 22

MESH_AXES = ("x", "y", "c")


def _cparams(*sem):
    return pltpu.CompilerParams(dimension_semantics=sem, vmem_limit_bytes=VMEM_LIMIT)


def _dot(a, b, dims, precision=None):
    return lax.dot_general(a, b, (dims, ((), ())), preferred_element_type=f32, precision=precision)


NN = ((1,), (0,))
NT = ((1,), (1,))
TN = ((0,), (0,))


def _mat_spec(arr, kind, lead, br, bc, rb, cb):
    if kind == "plain":
        return pl.BlockSpec((br, bc), lambda i, j, k: (rb(i, j, k), cb(i, j, k)))
    if kind == "lead":
        return pl.BlockSpec((None, br, bc), lambda i, j, k: (lead, rb(i, j, k), cb(i, j, k)))
    assert kind == "devcol" and bc == arr.shape[-1]
    return pl.BlockSpec((None, None, br, bc), lambda i, j, k: (cb(i, j, k), lead, rb(i, j, k), 0))


def _mm(name, a, b, mode, *, b_kind="plain", b_lead=0, o_kind="plain", epilogue=None, extras=(), params=(),
        out_dtypes=(f32,), tm=1024, tn=1024, tk=None):
    if tk is None:
        tk = 512 if mode == "tn" else 1024
    if b_kind in ("plain", "lead"):
        b_rows, b_cols = b.shape[-2:]
    else:
        b_rows, b_cols = b.shape[-2], N_DEV * b.shape[-1]
    if mode == "nn":
        (M, K), (K2, N) = a.shape, (b_rows, b_cols)
    elif mode == "nt":
        (M, K), (N, K2) = a.shape, (b_rows, b_cols)
    else:
        (K, M), (K2, N) = a.shape, (b_rows, b_cols)
    assert K == K2, (name, a.shape, b.shape, mode)
    tm, tn, tk = min(tm, M), min(tn, N), min(tk, K)
    cols_are_n = mode != "nt"
    if b_kind == "devcol":
        tn, tk = (b.shape[-1], tk) if cols_are_n else (tn, b.shape[-1])
    shard = N // N_DEV
    if o_kind == "devcol":
        tn = max(shard, tn // shard * shard)
    assert M % tm == 0 and N % tn == 0 and K % tk == 0, (name, M, N, K, tm, tn, tk)
    nk = K // tk
    dims = {"nn": NN, "nt": NT, "tn": TN}[mode]
    n_ex, n_out = len(extras) + len(params), len(out_dtypes)

    def body(*refs):
        a_ref, b_ref = refs[:2]
        ex = refs[2:2 + n_ex]
        outs = refs[2 + n_ex:2 + n_ex + n_out]
        acc = refs[-1]
        k = pl.program_id(2)

        @pl.when(k == 0)
        def _():
            acc[...] = jnp.zeros_like(acc)

        acc[...] += _dot(a_ref[...].astype(bf16), b_ref[...].astype(bf16), dims)

        @pl.when(k == nk - 1)
        def _():
            r = acc[...]
            res = epilogue(r, *[e[...] for e in ex]) if epilogue is not None else (r,)
            for o, v in zip(outs, res):
                if o_kind == "plain":
                    o[...] = v.astype(o.dtype)
                else:
                    for q in range(tn // shard):
                        o[q] = v[:, q * shard:(q + 1) * shard].astype(o.dtype)

    if mode == "tn":
        a_spec = pl.BlockSpec((tk, tm), lambda i, j, k: (k, i))
    else:
        a_spec = pl.BlockSpec((tm, tk), lambda i, j, k: (i, k))
    jb, kb = (lambda i, j, k: j), (lambda i, j, k: k)
    if mode == "nt":
        b_spec = _mat_spec(b, b_kind, b_lead, tn, tk, jb, kb)
    else:
        b_spec = _mat_spec(b, b_kind, b_lead, tk, tn, kb, jb)
    e_spec = pl.BlockSpec((tm, tn), lambda i, j, k: (i, j))
    if o_kind == "plain":
        o_spec, o_shape = e_spec, (M, N)
    else:
        o_spec, o_shape = pl.BlockSpec((tn // shard, tm, shard), lambda i, j, k: (j, i, 0)), (N_DEV, M, shard)
    res = pl.pallas_call(
        body, name=name,
        grid=(M // tm, N // tn, nk),
        in_specs=[a_spec, b_spec] + [e_spec] * len(extras)
        + [pl.BlockSpec(p.shape, lambda i, j, k: (0, 0)) for p in params],
        out_specs=[o_spec] * n_out,
        out_shape=[jax.ShapeDtypeStruct(o_shape, dt) for dt in out_dtypes],
        scratch_shapes=[pltpu.VMEM((tm, tn), f32)],
        compiler_params=_cparams("parallel", "parallel", "arbitrary"),
    )(a, b, *extras, *params)
    return res[0] if n_out == 1 else res


def _row_spec(tm, cb, width):
    assert (cb * LANE) % width == 0
    blk = (cb * LANE) // width
    return pl.BlockSpec((tm, width), lambda i: (i, blk))


def _whole_spec(p):
    nd = p.ndim
    return pl.BlockSpec(p.shape, lambda i: (0,) * nd)


def _tl_bwd(name, fn, rows, params, cot_rows, cot_fn=None, skip=(), bf16_copy=False, tm=512):
    T = rows[0][0].shape[0]
    tm = min(tm, T)
    nr, npar, nc = len(rows), len(params), len(cot_rows)
    keep = [k for k in range(nr) if k not in skip]
    n_rows = len(keep) + int(bf16_copy)
    row_dtypes = [(rows[k][2], f32) for k in keep] + ([(rows[keep[0]][2], bf16)] if bf16_copy else [])

    def body(*refs):
        vals = [r[...] for r in refs[:nr + npar]]
        cots = [r[...] for r in refs[nr + npar:nr + npar + nc]]
        outs = refs[nr + npar + nc:]
        cot = tuple(cot_fn(*cots)) if cot_fn is not None else tuple(cots)
        _, vjp = jax.vjp(fn, *vals)
        grads = vjp(cot)
        for o, k in zip(outs, keep):
            o[...] = grads[k].astype(o.dtype)
        if bf16_copy:
            outs[len(keep)][...] = grads[keep[0]].astype(bf16)
        i = pl.program_id(0)
        for o, g in zip(outs[n_rows:], grads[nr:]):
            @pl.when(i == 0)
            def _(o=o):
                o[...] = jnp.zeros_like(o)
            o[...] += g

    res = pl.pallas_call(
        body, name=name, grid=(T // tm,),
        in_specs=[_row_spec(tm, cb, w) for (_, cb, w) in rows] + [_whole_spec(p) for p in params]
        + [_row_spec(tm, cb, w) for (_, cb, w) in cot_rows],
        out_specs=[pl.BlockSpec((tm, w), lambda i: (i, 0)) for w, _ in row_dtypes] + [_whole_spec(p) for p in params],
        out_shape=[jax.ShapeDtypeStruct((T, w), dt) for w, dt in row_dtypes]
        + [jax.ShapeDtypeStruct(p.shape, f32) for p in params],
        compiler_params=_cparams("arbitrary"),
    )(*[r[0] for r in rows], *params, *[r[0] for r in cot_rows])
    return res[:n_rows], res[n_rows:]


def _ln_res_fn(x, mix, g, b):
    pre = DN_ALPHA * x + mix
    mu = jnp.mean(pre, axis=-1, keepdims=True)
    var = jnp.mean(jnp.square(pre - mu), axis=-1, keepdims=True)
    return ((pre - mu) * lax.rsqrt(var + LN_EPS) * g + b,)


@jax.custom_jvp
def _expm1(x):
    small = jnp.abs(x) < 0.3
    xs = jnp.where(small, x, 0.0)
    poly = xs * (1.0 + xs * (1 / 2 + xs * (1 / 6 + xs * (1 / 24 + xs * (1 / 120 + xs * (
        1 / 720 + xs * (1 / 5040 + xs * (1 / 40320 + xs * (1 / 362880)))))))))
    return jnp.where(small, poly, jnp.exp(x) - 1.0)


@_expm1.defjvp
def _expm1_jvp(primals, tangents):
    (x,), (t,) = primals, tangents
    return _expm1(x), t * jnp.exp(x)


def _rglru_pre_fn(pre_r, pre_i, xc, b_a, b_x, lam):
    r = jax.nn.sigmoid(pre_r + b_a)
    i = jax.nn.sigmoid(pre_i + b_x)
    log_a = -LRU_C * r * jax.nn.softplus(-lam)
    a = jnp.exp(log_a)
    b = jnp.sqrt(-_expm1(2.0 * log_a)) * (i * xc)
    return a, b


def _rec_gate_fn(h, gate):
    return (h * jax.nn.gelu(gate),)


def _loss_head(y, t, tm=512):
    T, Dm = y.shape
    tm = min(tm, T)

    def body(y_ref, t_ref, dy_ref, loss_ref):
        e = y_ref[...] - t_ref[...]
        dy_ref[...] = e * (1.0 / Dm)

        @pl.when(pl.program_id(0) == 0)
        def _():
            loss_ref[...] = jnp.zeros_like(loss_ref)

        loss_ref[...] += 0.5 * jnp.sum(jnp.mean(e * e, axis=-1, keepdims=True), axis=0, keepdims=True)

    dy, loss = pl.pallas_call(
        body, name="loss_head", grid=(T // tm,),
        in_specs=[pl.BlockSpec((tm, Dm), lambda i: (i, 0))] * 2,
        out_specs=[pl.BlockSpec((tm, Dm), lambda i: (i, 0)), pl.BlockSpec((SUBLANE, LANE), lambda i: (0, 0))],
        out_shape=[jax.ShapeDtypeStruct((T, Dm), f32), jax.ShapeDtypeStruct((SUBLANE, LANE), f32)],
        compiler_params=_cparams("arbitrary"),
    )(y, t)
    return loss[0, 0], dy


def _conv_fwd(name, x, cb0, nblk, w, bias, tm=2048):
    T = x.shape[0]
    tm = min(tm, T)
    hb = tm // SUBLANE
    has_b = bias is not None

    def body(*refs):
        cur, prev, w_ref = refs[:3]
        b_ref = refs[3] if has_b else None
        o = refs[-1]
        i = pl.program_id(1)
        p = jnp.where(i > 0, prev[...], 0.0)
        xcat = jnp.concatenate([p, cur[...]], axis=0)
        acc = cur[...] * w_ref[3:4, :]
        for j in range(3):
            acc = acc + pltpu.roll(xcat, 3 - j, axis=0)[SUBLANE:] * w_ref[j:j + 1, :]
        if has_b:
            acc = acc + b_ref[...]
        o[...] = acc

    in_specs = [
        pl.BlockSpec((tm, LANE), lambda c, i: (i, cb0 + c)),
        pl.BlockSpec((SUBLANE, LANE), lambda c, i: (jnp.maximum(i * hb - 1, 0), cb0 + c)),
        pl.BlockSpec((4, LANE), lambda c, i: (0, c)),
    ]
    args = [x, x, w]
    if has_b:
        in_specs.append(pl.BlockSpec((1, LANE), lambda c, i: (0, c)))
        args.append(bias)
    return pl.pallas_call(
        body, name=name, grid=(nblk, T // tm),
        in_specs=in_specs,
        out_specs=pl.BlockSpec((tm, LANE), lambda c, i: (i, c)),
        out_shape=jax.ShapeDtypeStruct((T, nblk * LANE), f32),
        compiler_params=_cparams("parallel", "parallel"),
    )(*args)


def _conv_bwd(name, dy, x, cb0, nblk, w, into, into_cb, tm=2048):
    T = x.shape[0]
    tm = min(tm, T)
    hb = tm // SUBLANE
    nt = T // tm

    def body(dcur, dnext, xcur, xprev, w_ref, _, dx_ref, dw_ref, db_ref):
        i = pl.program_id(1)
        d = dcur[...]
        dn = jnp.where(i < nt - 1, dnext[...], 0.0)
        dcat = jnp.concatenate([d, dn], axis=0)
        acc = d * w_ref[3:4, :]
        for j in range(3):
            s = 3 - j
            acc = acc + pltpu.roll(dcat, tm + SUBLANE - s, axis=0)[:tm] * w_ref[j:j + 1, :]
        dx_ref[...] = acc.astype(dx_ref.dtype)

        p = jnp.where(i > 0, xprev[...], 0.0)
        xcat = jnp.concatenate([p, xcur[...]], axis=0)
        rows = [jnp.sum(d * pltpu.roll(xcat, 3 - j, axis=0)[SUBLANE:], axis=0, keepdims=True) for j in range(3)]
        rows.append(jnp.sum(d * xcur[...], axis=0, keepdims=True))
        rows.append(jnp.zeros((SUBLANE - 4, LANE), f32))

        @pl.when(i == 0)
        def _():
            dw_ref[...] = jnp.zeros_like(dw_ref)
            db_ref[...] = jnp.zeros_like(db_ref)

        dw_ref[...] += jnp.concatenate(rows, axis=0)
        db_ref[...] += jnp.broadcast_to(jnp.sum(d, axis=0, keepdims=True), (SUBLANE, LANE))

    nh = T // SUBLANE
    dx, dw, db = pl.pallas_call(
        body, name=name, grid=(nblk, nt),
        in_specs=[
            pl.BlockSpec((tm, LANE), lambda c, i: (i, c)),
            pl.BlockSpec((SUBLANE, LANE), lambda c, i: (jnp.minimum((i + 1) * hb, nh - 1), c)),
            pl.BlockSpec((tm, LANE), lambda c, i: (i, cb0 + c)),
            pl.BlockSpec((SUBLANE, LANE), lambda c, i: (jnp.maximum(i * hb - 1, 0), cb0 + c)),
            pl.BlockSpec((4, LANE), lambda c, i: (0, c)),
            pl.BlockSpec(memory_space=pl.ANY),
        ],
        out_specs=[
            pl.BlockSpec((tm, LANE), lambda c, i: (i, into_cb + c)),
            pl.BlockSpec((SUBLANE, LANE), lambda c, i: (0, c)),
            pl.BlockSpec((SUBLANE, LANE), lambda c, i: (0, c)),
        ],
        out_shape=[jax.ShapeDtypeStruct(into.shape, into.dtype),
                   jax.ShapeDtypeStruct((SUBLANE, nblk * LANE), f32),
                   jax.ShapeDtypeStruct((SUBLANE, nblk * LANE), f32)],
        input_output_aliases={5: 0},
        compiler_params=_cparams("parallel", "arbitrary"),
    )(dy, dy, x, x, w, into)
    return dx, dw[:4], db[0]


@functools.partial(jax.custom_vjp, nondiff_argnums=(1,))
def _lroll(x, s):
    return pltpu.roll(x, s, axis=1)


def _lroll_fwd(x, s):
    return _lroll(x, s), None


def _lroll_bwd(s, _, g):
    return (_lroll(g, (LANE - s) % LANE),)


_lroll.defvjp(_lroll_fwd, _lroll_bwd)


def _rope_tables(T):
    half = A_HEAD_DIM // 2
    inv_freq = ROPE_THETA ** (-jnp.arange(half, dtype=f32) / half)
    ang = jnp.arange(T, dtype=f32)[:, None] * inv_freq[None, :]
    cos, sin = jnp.cos(ang), jnp.sin(ang)
    return jnp.tile(jnp.concatenate([cos, cos], axis=1), (1, 2)), jnp.tile(jnp.concatenate([-sin, sin], axis=1), (1, 2))


def _attn_block_fn(n, q, kp, kc, vp, vc, cq, sq, cp, sp, sinks):
    W = WINDOW
    lane = lax.broadcasted_iota(jnp.int32, (W, LANE), 1)
    lo_half = (lane % A_HEAD_DIM) < (A_HEAD_DIM // 2)
    lane8 = lax.broadcasted_iota(jnp.int32, sinks.shape, 1)

    def rope(x, c, s):
        return x * c + jnp.where(lo_half, _lroll(x, LANE - A_HEAD_DIM // 2), _lroll(x, A_HEAD_DIM // 2)) * s

    k2 = jnp.concatenate([rope(kp, cp, sp), rope(kc, cq, sq)], axis=0).astype(bf16)
    v2 = jnp.concatenate([vp, vc], axis=0).astype(bf16)
    qs = []
    for t in range(4):
        qt = rope(q[:, LANE * t:LANE * (t + 1)], cq, sq)
        g = t // 2
        for hh in range(2):
            qa = jnp.where((lane // A_HEAD_DIM) == hh, qt, 0.0)
            qs.append(_lroll(qa, A_HEAD_DIM) if hh != g else qa)
    s_all = _dot(jnp.concatenate(qs, axis=0).astype(bf16), k2, NT) * (A_HEAD_DIM ** -0.5)
    row = lax.broadcasted_iota(jnp.int32, (W, 2 * W), 0)
    col = lax.broadcasted_iota(jnp.int32, (W, 2 * W), 1)
    dist = row + W - col
    mask = (dist >= 0) & (dist < W) & ((col >= W) | (n > 0))
    ps = []
    for j in range(A_Q_HEADS):
        s = jnp.where(mask, s_all[W * j:W * (j + 1)], -jnp.inf)
        sink = jnp.sum(jnp.where(lane8 == j, sinks, 0.0), axis=1, keepdims=True)
        m = jnp.maximum(jnp.max(s, axis=-1, keepdims=True), sink)
        e = jnp.exp(s - m)
        ps.append((e / (jnp.sum(e, axis=-1, keepdims=True) + jnp.exp(sink - m))).astype(bf16))
    o = _dot(jnp.concatenate(ps, axis=0), v2, NN)
    outs = []
    for t in range(4):
        g = t // 2
        ot = jnp.zeros((W, LANE), f32)
        for hh in range(2):
            j = 2 * t + hh
            oj = jnp.where((lane // A_HEAD_DIM) == g, o[W * j:W * (j + 1)], 0.0)
            ot = ot + (_lroll(oj, A_HEAD_DIM) if hh != g else oj)
        outs.append(ot)
    return jnp.concatenate(outs, axis=1)


def _attn_specs():
    W = WINDOW
    prev = lambda n: jnp.maximum(n - 1, 0)
    return [
        pl.BlockSpec((W, 4 * LANE), lambda n: (n, CB_QA // 4)),
        pl.BlockSpec((W, LANE), lambda n: (prev(n), CB_KA)),
        pl.BlockSpec((W, LANE), lambda n: (n, CB_KA)),
        pl.BlockSpec((W, LANE), lambda n: (prev(n), CB_VA)),
        pl.BlockSpec((W, LANE), lambda n: (n, CB_VA)),
        pl.BlockSpec((W, LANE), lambda n: (n, 0)),
        pl.BlockSpec((W, LANE), lambda n: (n, 0)),
        pl.BlockSpec((W, LANE), lambda n: (prev(n), 0)),
        pl.BlockSpec((W, LANE), lambda n: (prev(n), 0)),
        pl.BlockSpec((1, A_Q_HEADS), lambda n: (0, 0)),
    ]


def _attn_fwd(name, proj, cos, sin, sinks):
    T = proj.shape[0]
    W = WINDOW

    def body(*refs):
        o = refs[-1]
        o[...] = _attn_block_fn(pl.program_id(0), *[r[...] for r in refs[:-1]]).astype(o.dtype)

    return pl.pallas_call(
        body, name=name, grid=(T // W,),
        in_specs=_attn_specs(),
        out_specs=pl.BlockSpec((W, 4 * LANE), lambda n: (n, 0)),
        out_shape=jax.ShapeDtypeStruct((T, 2 * 4 * LANE), bf16),
        compiler_params=_cparams("parallel"),
    )(proj, proj, proj, proj, proj, cos, sin, cos, sin, sinks)


def _attn_bwd(name, proj, cos, sin, sinks, d_oab):
    T = proj.shape[0]
    W = WINDOW
    Q = 4 * LANE
    nb = T // W

    def body(*refs):
        ins = [r[...] for r in refs[:10]]
        do = refs[10][...]
        out_ref, ds_ref, d_ref = refs[11:]
        n = pl.program_id(0)
        _, vjp = jax.vjp(functools.partial(_attn_block_fn, n), *ins)
        dq, dkp, dkc, dvp, dvc, _, _, _, _, dsk = vjp(do)

        @pl.when(n == 0)
        def _():
            d_ref[:, Q:] = jnp.zeros((T, 2 * LANE), f32)
            ds_ref[...] = jnp.zeros_like(ds_ref)

        cur = pl.ds(pl.multiple_of(n * W, W), W)
        d_ref[cur, :Q] = dq
        d_ref[cur, Q:Q + LANE] += dkc
        d_ref[cur, Q + LANE:] += dvc
        ds_ref[...] += dsk

        @pl.when(n > 0)
        def _():
            prv = pl.ds(pl.multiple_of((n - 1) * W, W), W)
            d_ref[prv, Q:Q + LANE] += dkp
            d_ref[prv, Q + LANE:] += dvp

        @pl.when(n == nb - 1)
        def _():
            out_ref[...] = d_ref[...].astype(out_ref.dtype)

    return pl.pallas_call(
        body, name=name, grid=(nb,),
        in_specs=_attn_specs() + [pl.BlockSpec((W, Q), lambda n: (n, 0))],
        out_specs=[pl.BlockSpec((T, Q + 2 * LANE), lambda n: (0, 0)),
                   pl.BlockSpec((1, A_Q_HEADS), lambda n: (0, 0))],
        out_shape=[jax.ShapeDtypeStruct((T, HYB_PROJ_PAD), bf16), jax.ShapeDtypeStruct((1, A_Q_HEADS), f32)],
        scratch_shapes=[pltpu.VMEM((T, Q + 2 * LANE), f32)],
        compiler_params=_cparams("arbitrary"),
    )(proj, proj, proj, proj, proj, cos, sin, cos, sin, sinks, d_oab)


def _bdot(spec, a, b, precision=None):
    return jnp.einsum(spec, a, b, preferred_element_type=f32, precision=precision)


@jax.custom_vjp
def _tri_inv(a):
    H, C, _ = a.shape
    B = 2 * SUBLANE
    nb = C // B
    r = lax.broadcasted_iota(jnp.int32, (C, C), 0)
    c = lax.broadcasted_iota(jnp.int32, (C, C), 1)
    a4 = jnp.where((r // B) == (c // B), a, 0.0).reshape(H, nb, B, C)
    t4 = jnp.broadcast_to(jnp.where(r == c, 1.0, 0.0).astype(f32), a.shape).reshape(H, nb, B, C)
    for j in range(B - 1):
        col = jnp.concatenate([a4[:, b:b + 1, :, B * b + j:B * b + j + 1] for b in range(nb)], axis=1)
        t4 = t4 - col * t4[:, :, j:j + 1, :]
    x = t4.reshape(H, C, C)
    hi = lax.Precision.HIGH
    while B < C:
        m = jnp.where(((r // (2 * B)) == (c // (2 * B))) & ((r // B) > (c // B)), a, 0.0)
        x = x - _bdot("hij,hjk->hik", x, _bdot("hij,hjk->hik", m, x, precision=hi), precision=hi)
        B *= 2
    return x


def _tri_inv_fwd(a):
    t = _tri_inv(a)
    return t, t


def _tri_inv_bwd(t, g):
    C = t.shape[-1]
    r = lax.broadcasted_iota(jnp.int32, (C, C), 0)
    c = lax.broadcasted_iota(jnp.int32, (C, C), 1)
    x = _bdot("hki,hkj->hij", t, g, precision=lax.Precision.HIGHEST)
    y = _bdot("hik,hjk->hij", x, t, precision=lax.Precision.HIGHEST)
    return (jnp.where(r > c, -y, 0.0),)


_tri_inv.defvjp(_tri_inv_fwd, _tri_inv_bwd)


@jax.custom_vjp
def _tri_inv_saved(a, t):
    return t


_tri_inv_saved.defvjp(lambda a, t: (t, t), lambda t, g: (_tri_inv_bwd(t, g)[0], jnp.zeros_like(t)))


def _silu(x):
    return x * jax.nn.sigmoid(x)


def _l2n(x):
    return x * lax.rsqrt(jnp.sum(x * x, axis=-1, keepdims=True) + NORM_EPS)


def _delta_chunk_fn(cq, ck, cv, z, lg, a_log, dt_bias, norm_w, S, t_saved=None, want_t=False):
    C = B_CHUNK
    lane = lax.broadcasted_iota(jnp.int32, (C, LANE), 1)
    pick = lambda l0: jnp.concatenate(
        [jnp.sum(jnp.where(lane == l0 + h, lg, 0.0), axis=1, keepdims=True)[None] for h in range(B_HEADS)], axis=0)
    bl, al = pick(0), pick(B_HEADS)
    q = _l2n(_silu(cq)) * (B_HEAD_DIM ** -0.5)
    k = _l2n(_silu(ck))
    v = _silu(cv)
    beta = jax.nn.sigmoid(bl)
    g = -jnp.exp(a_log) * jax.nn.softplus(al + dt_bias)
    r = lax.broadcasted_iota(jnp.int32, (C, C), 0)
    c = lax.broadcasted_iota(jnp.int32, (C, C), 1)
    eye = r == c
    g_row = jnp.sum(jnp.where(eye, g, 0.0), axis=1, keepdims=True)
    gc = jnp.sum(jnp.where(c <= r, g_row, 0.0), axis=2, keepdims=True)
    gc_row = jnp.sum(jnp.where(eye, gc, 0.0), axis=1, keepdims=True)
    decay_incl = jnp.exp(jnp.where(r >= c, gc - gc_row, -jnp.inf))
    decay_strict = jnp.where(r > c, decay_incl, 0.0)
    kb = k * beta
    vb = v * beta
    kbf = k.astype(bf16)
    a_mat = _bdot("hik,hjk->hij", kb.astype(bf16), kbf) * decay_strict
    t_f32 = _tri_inv(a_mat) if t_saved is None else _tri_inv_saved(a_mat, t_saved)
    t_mat = t_f32.astype(bf16)
    eg = jnp.exp(gc)
    u = _bdot("hij,hjv->hiv", t_mat, vb.astype(bf16))
    w = _bdot("hij,hjk->hik", t_mat, (kb * eg).astype(bf16))
    qk = _bdot("hik,hjk->hij", q.astype(bf16), kbf) * decay_incl
    g_last = jnp.sum(g, axis=1, keepdims=True)
    k_tail = k * jnp.exp(g_last - gc)
    Sb = S.astype(bf16)
    v_new = u - _bdot("hck,hkv->hcv", w.astype(bf16), Sb)
    o = _bdot("hck,hkv->hcv", (q * eg).astype(bf16), Sb) + _bdot("hij,hjv->hiv", qk.astype(bf16), v_new.astype(bf16))
    S_new = S * jnp.exp(g_last) + _bdot("hck,hcv->hkv", k_tail.astype(bf16), v_new.astype(bf16))
    ob = o * lax.rsqrt(jnp.mean(o * o, axis=-1, keepdims=True) + NORM_EPS) * norm_w
    return (ob * _silu(z), S_new) + ((t_f32,) if want_t else ())


DELTA_CHUNKS_PER_STEP = 8


def _delta_in_specs(rev, N):
    C = DELTA_CHUNKS_PER_STEP * B_CHUNK
    ix = (lambda n: N - 1 - n) if rev else (lambda n: n)
    specs = [pl.BlockSpec((C, 3 * B_HEADS * LANE), lambda n: (ix(n), 0))]
    specs += [pl.BlockSpec((C, LANE), lambda n, h=h: (ix(n), CB_Z + h)) for h in range(B_HEADS)]
    specs += [
        pl.BlockSpec((C, LANE), lambda n: (ix(n), CB_LG)),
        pl.BlockSpec((B_HEADS, 1, 1), lambda n: (0, 0, 0)),
        pl.BlockSpec((B_HEADS, 1, 1), lambda n: (0, 0, 0)),
        pl.BlockSpec((1, LANE), lambda n: (0, 0)),
    ]
    return specs


def _delta_inputs(u, c_ref, z_refs, lg, al, dt, nw):
    H = B_HEADS
    rows = slice(u * B_CHUNK, (u + 1) * B_CHUNK)
    part = lambda p: jnp.stack([c_ref[rows, LANE * (p * H + h):LANE * (p * H + h + 1)] for h in range(H)])
    return (part(0), part(1), part(2), jnp.stack([z[rows, :] for z in z_refs]), lg[rows, :], al[...], dt[...], nw[...])


def _delta_fwd(name, c, proj, a_log, dt_bias, norm_w, o_ab):
    T = c.shape[0]
    C = B_CHUNK
    N = T // C
    Dh = B_HEAD_DIM
    H = B_HEADS

    def body(*refs):
        c_ref, z_refs, (lg, al, dt, nw) = refs[0], refs[1:1 + H], refs[1 + H:5 + H]
        o_ref, s_ref, t_ref, S = refs[6 + H:]

        @pl.when(pl.program_id(0) == 0)
        def _():
            S[...] = jnp.zeros_like(S)

        s = S[...]
        for u in range(U):
            s_ref[:, u] = s
            ob, s, t = _delta_chunk_fn(*_delta_inputs(u, c_ref, z_refs, lg, al, dt, nw), s, want_t=True)
            for h in range(H):
                o_ref[u * C:(u + 1) * C, LANE * h:LANE * (h + 1)] = ob[h].astype(o_ref.dtype)
            t_ref[:, u] = t
        S[...] = s

    U = DELTA_CHUNKS_PER_STEP
    return pl.pallas_call(
        body, name=name, grid=(N // U,),
        in_specs=_delta_in_specs(False, N // U) + [pl.BlockSpec(memory_space=pl.ANY)],
        out_specs=[pl.BlockSpec((U * C, H * LANE), lambda n: (n, 1)),
                   pl.BlockSpec((H, U, Dh, Dh), lambda n: (0, n, 0, 0)),
                   pl.BlockSpec((H, U, C, C), lambda n: (0, n, 0, 0))],
        out_shape=[jax.ShapeDtypeStruct(o_ab.shape, o_ab.dtype), jax.ShapeDtypeStruct((H, N, Dh, Dh), f32),
                   jax.ShapeDtypeStruct((H, N, C, C), f32)],
        input_output_aliases={5 + H: 0},
        scratch_shapes=[pltpu.VMEM((H, Dh, Dh), f32)],
        compiler_params=_cparams("arbitrary"),
    )(c, *([proj] * H), proj, a_log, dt_bias, norm_w, o_ab)


def _delta_bwd(name, c, proj, a_log, dt_bias, norm_w, s_saved, t_saved, d_oab, dproj):
    T = c.shape[0]
    C = B_CHUNK
    N = T // C
    Dh = B_HEAD_DIM
    H = B_HEADS

    def body(*refs):
        c_ref, z_refs, (lg, al, dt, nw) = refs[0], refs[1:1 + H], refs[1 + H:5 + H]
        s_ref, t_ref, do_ref = refs[5 + H:8 + H]
        dc, dtail, dal, ddt, dnw, dS = refs[9 + H:]

        @pl.when(pl.program_id(0) == 0)
        def _():
            dS[...] = jnp.zeros_like(dS)
            dal[...] = jnp.zeros_like(dal)
            ddt[...] = jnp.zeros_like(ddt)
            dnw[...] = jnp.zeros_like(dnw)

        ds = dS[...]
        for u in reversed(range(U)):
            rows = slice(u * C, (u + 1) * C)
            _, vjp = jax.vjp(functools.partial(_delta_chunk_fn, t_saved=t_ref[:, u]),
                             *_delta_inputs(u, c_ref, z_refs, lg, al, dt, nw), s_ref[:, u])
            do = jnp.stack([do_ref[rows, LANE * h:LANE * (h + 1)] for h in range(H)])
            g = vjp((do, ds))
            for h in range(H):
                for p in range(3):
                    dc[rows, LANE * (p * H + h):LANE * (p * H + h + 1)] = g[p][h]
                dtail[rows, LANE * h:LANE * (h + 1)] = g[3][h].astype(dtail.dtype)
            dtail[rows, LANE * H:LANE * (H + 1)] = g[4].astype(dtail.dtype)
            dtail[rows, LANE * (H + 1):] = jnp.zeros((C, LANE), dtail.dtype)
            dal[...] += g[5]
            ddt[...] += g[6]
            dnw[...] += g[7]
            ds = g[8]
        dS[...] = ds

    U = DELTA_CHUNKS_PER_STEP
    NB = N // U
    rn = lambda n: NB - 1 - n
    return pl.pallas_call(
        body, name=name, grid=(NB,),
        in_specs=_delta_in_specs(True, NB) + [
            pl.BlockSpec((H, U, Dh, Dh), lambda n: (0, rn(n), 0, 0)),
            pl.BlockSpec((H, U, C, C), lambda n: (0, rn(n), 0, 0)),
            pl.BlockSpec((U * C, H * LANE), lambda n: (rn(n), 1)),
            pl.BlockSpec(memory_space=pl.ANY),
        ],
        out_specs=[
            pl.BlockSpec((U * C, 3 * H * LANE), lambda n: (rn(n), 0)),
            pl.BlockSpec((U * C, (H + 2) * LANE), lambda n: (rn(n), CB_Z // (H + 2))),
            pl.BlockSpec((H, 1, 1), lambda n: (0, 0, 0)),
            pl.BlockSpec((H, 1, 1), lambda n: (0, 0, 0)),
            pl.BlockSpec((1, LANE), lambda n: (0, 0)),
        ],
        out_shape=[jax.ShapeDtypeStruct((T, 3 * H * Dh), f32), jax.ShapeDtypeStruct(dproj.shape, dproj.dtype),
                   jax.ShapeDtypeStruct((H, 1, 1), f32), jax.ShapeDtypeStruct((H, 1, 1), f32),
                   jax.ShapeDtypeStruct((1, LANE), f32)],
        input_output_aliases={8 + H: 1},
        scratch_shapes=[pltpu.VMEM((H, Dh, Dh), f32)],
        compiler_params=_cparams("arbitrary"),
    )(c, *([proj] * H), proj, a_log, dt_bias, norm_w, s_saved, t_saved, d_oab, dproj)


def _gate_matmuls(xc, wa_ref, wx_ref):
    bw = wa_ref.shape[-1]
    xb = xc.astype(bf16)
    blocks = [xb[:, bw * h:bw * (h + 1)] for h in range(LRU_BLOCKS)]
    return (jnp.concatenate([_dot(blocks[h], wa_ref[h], NN) for h in range(LRU_BLOCKS)], axis=1),
            jnp.concatenate([_dot(blocks[h], wx_ref[h], NN) for h in range(LRU_BLOCKS)], axis=1))


def _gates_fwd(name, xc, w_a, w_x, pars, tm=512):
    T, Wd = xc.shape
    tm = min(tm, T)

    def body(x_ref, wa_ref, wx_ref, ba, bx, lam, a_ref, b_ref):
        x = x_ref[...]
        pr, pi = _gate_matmuls(x, wa_ref, wx_ref)
        a_ref[...], b_ref[...] = _rglru_pre_fn(pr, pi, x, ba[...], bx[...], lam[...])

    row = pl.BlockSpec((tm, Wd), lambda i: (i, 0))
    return pl.pallas_call(
        body, name=name, grid=(T // tm,),
        in_specs=[row, _whole_spec(w_a), _whole_spec(w_x)] + [_whole_spec(p) for p in pars],
        out_specs=[row, row], out_shape=[jax.ShapeDtypeStruct((T, Wd), f32)] * 2,
        compiler_params=_cparams("parallel"),
    )(xc, w_a, w_x, *pars)


def _gates_bwd(name, xc, w_a, w_x, pars, lam_t, h_prev, tm=512):
    T, Wd = xc.shape
    tm = min(tm, T)
    bw = Wd // LRU_BLOCKS

    def body(x_ref, wa_ref, wx_ref, ba, bx, lam, lt_ref, hp_ref, dx_ref, dr_ref, di_ref, dba, dbx, dlam):
        x = x_ref[...]
        pr, pi = _gate_matmuls(x, wa_ref, wx_ref)
        _, vjp = jax.vjp(_rglru_pre_fn, pr, pi, x, ba[...], bx[...], lam[...])
        lt = lt_ref[...]
        dpr, dpi, dxc, g_ba, g_bx, g_lam = vjp((lt * hp_ref[...], lt))
        dprb, dpib = dpr.astype(bf16), dpi.astype(bf16)
        dx_ref[...] = dxc + jnp.concatenate(
            [_dot(dprb[:, bw * h:bw * (h + 1)], wa_ref[h], NT) + _dot(dpib[:, bw * h:bw * (h + 1)], wx_ref[h], NT)
             for h in range(LRU_BLOCKS)], axis=1)
        dr_ref[...] = dprb
        di_ref[...] = dpib

        @pl.when(pl.program_id(0) == 0)
        def _():
            dba[...] = jnp.zeros_like(dba)
            dbx[...] = jnp.zeros_like(dbx)
            dlam[...] = jnp.zeros_like(dlam)

        dba[...] += g_ba
        dbx[...] += g_bx
        dlam[...] += g_lam

    row = pl.BlockSpec((tm, Wd), lambda i: (i, 0))
    vec = pl.BlockSpec((1, Wd), lambda i: (0, 0))
    return pl.pallas_call(
        body, name=name, grid=(T // tm,),
        in_specs=[row, _whole_spec(w_a), _whole_spec(w_x)] + [_whole_spec(p) for p in pars] + [row, row],
        out_specs=[row, row, row, vec, vec, vec],
        out_shape=[jax.ShapeDtypeStruct((T, Wd), f32), jax.ShapeDtypeStruct((T, Wd), bf16),
                   jax.ShapeDtypeStruct((T, Wd), bf16)] + [jax.ShapeDtypeStruct((1, Wd), f32)] * 3,
        compiler_params=_cparams("arbitrary"),
    )(xc, w_a, w_x, *pars, lam_t, h_prev)


def _blockdiag_bwd_dw(name, xc, dpr, dpi, tk=512):
    T, Wd = xc.shape
    bw = Wd // LRU_BLOCKS
    tk = min(tk, T)

    def body(x_ref, dr, di, oa, ox):
        @pl.when(pl.program_id(1) == 0)
        def _():
            oa[...] = jnp.zeros_like(oa)
            ox[...] = jnp.zeros_like(ox)

        xb = x_ref[...].astype(bf16)
        oa[...] += _dot(xb, dr[...].astype(bf16), TN)
        ox[...] += _dot(xb, di[...].astype(bf16), TN)

    xs = pl.BlockSpec((tk, bw), lambda h, k: (k, h))
    ws = pl.BlockSpec((None, bw, bw), lambda h, k: (h, 0, 0))
    return pl.pallas_call(
        body, name=name, grid=(LRU_BLOCKS, T // tk), in_specs=[xs, xs, xs], out_specs=[ws, ws],
        out_shape=[jax.ShapeDtypeStruct((LRU_BLOCKS, bw, bw), f32)] * 2,
        compiler_params=_cparams("parallel", "arbitrary"),
    )(xc, dpr, dpi)


def _scan(name, a, proj, reverse, b=None, h=None, dhg=None, tt=512, cb=512):
    T, Wd = a.shape
    tt, cb = min(tt, T), min(cb, Wd)
    nt = T // tt
    ng = tt // SUBLANE

    def body(a_ref, g_ref, *rest):
        n_in = 2 if reverse else 1
        ins, outs, (carry, carry_a) = rest[:n_in], rest[n_in:-2], rest[-2:]

        @pl.when(pl.program_id(1) == 0)
        def _():
            carry[...] = jnp.zeros_like(carry)
            carry_a[...] = jnp.zeros_like(carry_a)

        row = lax.broadcasted_iota(jnp.int32, (SUBLANE, cb), 0)

        def group(g, c):
            hp, ap = c
            rows = pl.ds(pl.multiple_of(g * SUBLANE, SUBLANE), SUBLANE)
            A = a_ref[rows, :]
            gate = g_ref[rows, :]
            a_first = jnp.broadcast_to(A[0:1, :], (SUBLANE, cb))
            if reverse:
                _, vjp = jax.vjp(_rec_gate_fn, ins[0][rows, :], gate)
                B, narrow = vjp((ins[1][rows, :],))
                A = jnp.where(row == SUBLANE - 1, ap, pltpu.roll(A, SUBLANE - 1, axis=0))
            else:
                B = ins[0][rows, :]
            for s in (1, 2, 4):
                sh = (SUBLANE - s) if reverse else s
                As = pltpu.roll(A, sh, axis=0)
                Bs = pltpu.roll(B, sh, axis=0)
                valid = (row < SUBLANE - s) if reverse else (row >= s)
                B = jnp.where(valid, A * Bs + B, B)
                A = jnp.where(valid, A * As, A)
            hcur = A * hp + B
            outs[0][rows, :] = hcur
            if not reverse:
                outs[1][rows, :] = jnp.where(row == 0, hp, pltpu.roll(hcur, 1, axis=0))
                narrow = _rec_gate_fn(hcur, gate)[0]
            edge = hcur[0:1, :] if reverse else hcur[SUBLANE - 1:SUBLANE, :]
            return (jnp.broadcast_to(edge, (SUBLANE, cb)), a_first), narrow

        def pair(pi, c):
            p = (ng // 2 - 1 - pi) if reverse else pi
            c, first = group(2 * p + (1 if reverse else 0), c)
            c, second = group(2 * p + (0 if reverse else 1), c)
            lo, hi = (second, first) if reverse else (first, second)
            rows = pl.ds(pl.multiple_of(p * 2 * SUBLANE, 2 * SUBLANE), 2 * SUBLANE)
            outs[-1][rows, :] = jnp.concatenate([lo, hi], axis=0).astype(bf16)
            return c

        carry[...], carry_a[...] = lax.fori_loop(0, ng // 2, pair, (carry[...], carry_a[...]))

    nc = Wd // cb
    tok = (lambda i: nt - 1 - i) if reverse else (lambda i: i)
    spec = pl.BlockSpec((tt, cb), lambda c, i: (tok(i), c))
    gate_half = pl.BlockSpec((tt, cb), lambda c, i: (tok(i), nc + c))
    if reverse:
        args, out_specs = (a, proj, h, dhg), [spec, gate_half]
        out_shape = [jax.ShapeDtypeStruct((T, Wd), f32), jax.ShapeDtypeStruct((T, 2 * Wd), bf16)]
    else:
        args, out_specs = (a, proj, b), [spec] * 3
        out_shape = [jax.ShapeDtypeStruct((T, Wd), f32)] * 2 + [jax.ShapeDtypeStruct((T, Wd), bf16)]
    return pl.pallas_call(
        body, name=name, grid=(nc, nt), in_specs=[spec, gate_half] + [spec] * (len(args) - 2), out_specs=out_specs,
        out_shape=out_shape,
        scratch_shapes=[pltpu.VMEM((SUBLANE, cb), f32), pltpu.VMEM((SUBLANE, cb), f32)],
        compiler_params=_cparams("parallel", "arbitrary"),
    )(*args)


def _relu2_epilogue(r):
    h = jnp.maximum(r, 0.0)
    return r, h * h


def _drelu2_epilogue(r, a):
    return (r * (2.0 * jnp.maximum(a.astype(f32), 0.0)),)


def _residual_cot(through, upper):
    return (through + DN_ALPHA * upper,)


def _merge_cols(name, g, tm=256):
    _, L, R, s = g.shape

    def body(g_ref, o_ref):
        for d in range(N_DEV):
            o_ref[:, s * d:s * (d + 1)] = g_ref[d].astype(bf16)
        o_ref[:, N_DEV * s:] = jnp.zeros((tm, HYB_PROJ_PAD - N_DEV * s), bf16)

    return pl.pallas_call(
        body, name=name, grid=(L, R // tm),
        in_specs=[pl.BlockSpec((N_DEV, None, tm, s), lambda l, i: (0, l, i, 0))],
        out_specs=pl.BlockSpec((None, tm, HYB_PROJ_PAD), lambda l, i: (l, i, 0)),
        out_shape=jax.ShapeDtypeStruct((L, R, HYB_PROJ_PAD), bf16),
        compiler_params=_cparams("parallel", "parallel"),
    )(g)


def _split_cols(name, dw, tm=256):
    R = dw.shape[0]
    s = HYB_PROJ // N_DEV

    def body(g_ref, o_ref):
        for d in range(N_DEV):
            o_ref[d] = g_ref[:, s * d:s * (d + 1)].astype(bf16)

    return pl.pallas_call(
        body, name=name, grid=(R // tm,),
        in_specs=[pl.BlockSpec((tm, HYB_PROJ_PAD), lambda i: (i, 0))],
        out_specs=pl.BlockSpec((N_DEV, tm, s), lambda i: (0, i, 0)),
        out_shape=jax.ShapeDtypeStruct((N_DEV, R, s), bf16),
        compiler_params=_cparams("parallel"),
    )(dw)


def _rows_to_dev(dw):
    nb, r, c = dw.shape
    t = dw.reshape(nb, N_DEV, r // N_DEV, c)
    return jnp.moveaxis(t, 1, 0).reshape(N_DEV, nb * (r // N_DEV), c).astype(bf16)


def _ln_epilogue(r, x, g, b):
    y = _ln_res_fn(x, r, g, b)[0]
    return r, y, y


def _hybrid_fwd(tag, x, xb, W, j, cos, sin, ln):
    proj = _mm(f"{tag}_proj", xb, W["hyb_w_in"][j], "nn", b_kind="lead", b_lead=0)
    o_a = _attn_fwd(f"{tag}_attn", proj, cos, sin, W["hyb_sinks"][j][None, :])
    c = _conv_fwd(f"{tag}_conv", proj, CB_CONV, 12, W["hyb_conv_w"][j], None)
    o_ab, s_saved, t_saved = _delta_fwd(f"{tag}_delta", c, proj, W["hyb_a_log"][j].reshape(B_HEADS, 1, 1),
                                        W["hyb_dt_bias"][j].reshape(B_HEADS, 1, 1), W["hyb_norm_w"][j][None, :], o_a)
    mix, x1, x1b = _mm(f"{tag}_out", o_ab, W["hyb_w_out"][j], "nn", b_kind="lead", b_lead=0, epilogue=_ln_epilogue,
                       extras=(x,), params=ln, out_dtypes=(f32, f32, bf16), tm=512)
    return mix, x1, x1b, (proj, c, s_saved, t_saved, o_ab)


def _hybrid_bwd(tag, x, dmix, addend, W, j, cos, sin, saved, G, send_early):
    proj, c, s_saved, t_saved, o_ab = saved
    T = x.shape[0]
    d_oab = _mm(f"{tag}_dout", dmix, W["hyb_w_out"][j], "nt", b_kind="lead", b_lead=0)
    G["hyb_w_out"][j] = _mm(f"{tag}_dwout", o_ab, dmix, "tn", out_dtypes=(bf16,)).reshape(N_DEV, -1, D_MODEL)
    sinks = W["hyb_sinks"][j][None, :] + send_early({("hyb_w_out", j): G["hyb_w_out"][j]})
    dproj, dsinks = _attn_bwd(f"{tag}_dattn", proj, cos, sin, sinks, d_oab)
    a_log = W["hyb_a_log"][j].reshape(B_HEADS, 1, 1)
    dt_bias = W["hyb_dt_bias"][j].reshape(B_HEADS, 1, 1)
    dc, dproj, dal, ddt, dnw = _delta_bwd(f"{tag}_ddelta", c, proj, a_log, dt_bias, W["hyb_norm_w"][j][None, :],
                                          s_saved, t_saved, d_oab, dproj)
    dproj, dconv_w, _ = _conv_bwd(f"{tag}_dconv", dc, proj, CB_CONV, 12, W["hyb_conv_w"][j], dproj, CB_CONV)
    dx = _mm(f"{tag}_dx", dproj, W["hyb_w_in"][j], "nt", b_kind="lead", b_lead=0,
             **({} if addend is None else dict(epilogue=_residual_cot, extras=(addend,))))
    G["hyb_w_in"][j] = _split_cols(f"{tag}_dwin_split", _mm(f"{tag}_dwin", x, dproj, "tn", tn=1536))
    G["hyb_sinks"][j] = dsinks[0]
    G["hyb_conv_w"][j] = dconv_w
    G["hyb_a_log"][j] = dal.reshape(B_HEADS)
    G["hyb_dt_bias"][j] = ddt.reshape(B_HEADS)
    G["hyb_norm_w"][j] = dnw[0]
    return dx


def _rec_fwd(tag, x, xb, W, j, ln):
    Wd = D_MODEL
    proj = _mm(f"{tag}_proj", xb, W["rec_w_in"][j], "nn", b_kind="devcol", b_lead=0)
    xc = _conv_fwd(f"{tag}_conv", proj, 0, Wd // LANE, W["rec_conv_w"][j], W["rec_conv_b"][j][None, :])
    pars = [W["rec_b_a"][j][None, :], W["rec_b_x"][j][None, :], W["rec_lambda"][j][None, :]]
    a, b = _gates_fwd(f"{tag}_gates", xc, W["rec_w_a"][j][0], W["rec_w_x"][j][0], pars)
    h, h_prev, hg = _scan(f"{tag}_scan", a, proj, False, b=b)
    mix, x1, x1b = _mm(f"{tag}_out", hg, W["rec_w_out"][j], "nn", b_kind="lead", b_lead=0, epilogue=_ln_epilogue,
                       extras=(x,), params=ln, out_dtypes=(f32, f32, bf16), tm=512)
    return mix, x1, x1b, (proj, xc, a, h, h_prev, hg)


def _rec_bwd(tag, x, dmix, addend, W, j, saved, G, send_early):
    proj, xc, a, h, h_prev, hg = saved
    Wd = D_MODEL
    dhg = _mm(f"{tag}_dout", dmix, W["rec_w_out"][j], "nt", b_kind="lead", b_lead=0)
    G["rec_w_out"][j] = _mm(f"{tag}_dwout", hg, dmix, "tn", out_dtypes=(bf16,)).reshape(N_DEV, -1, D_MODEL)
    sent = send_early({("rec_w_out", j): G["rec_w_out"][j]})
    lam_t, dproj = _scan(f"{tag}_dscan", a, proj, True, h=h, dhg=dhg)
    pars = [W["rec_b_a"][j][None, :] + sent, W["rec_b_x"][j][None, :], W["rec_lambda"][j][None, :]]
    dxc, dpr, dpi, db_a, db_x, dlam = _gates_bwd(f"{tag}_dgates", xc, W["rec_w_a"][j][0], W["rec_w_x"][j][0], pars,
                                                 lam_t, h_prev)
    dwa, dwx = _blockdiag_bwd_dw(f"{tag}_dgates_dw", xc, dpr, dpi)
    G["rec_w_a"][j], G["rec_w_x"][j] = _rows_to_dev(dwa), _rows_to_dev(dwx)
    dproj, dconv_w, dconv_b = _conv_bwd(f"{tag}_dconv", dxc, proj, 0, Wd // LANE, W["rec_conv_w"][j], dproj, 0)
    dx = _mm(f"{tag}_dx", dproj, W["rec_w_in"][j], "nt", b_kind="devcol", b_lead=0,
             **({} if addend is None else dict(epilogue=_residual_cot, extras=(addend,))))
    G["rec_w_in"][j] = _mm(f"{tag}_dwin", x, dproj, "tn", o_kind="devcol", out_dtypes=(bf16,), tn=2048)
    G["rec_conv_w"][j] = dconv_w
    G["rec_conv_b"][j] = dconv_b
    G["rec_b_a"][j] = db_a[0]
    G["rec_b_x"][j] = db_x[0]
    G["rec_lambda"][j] = dlam[0]
    return dx


def _local_step(x, target, W, load_layer, grads_ready):
    T = x.shape[0]
    cos, sin = _rope_tables(T)
    saved = []
    xb = x
    for layer in range(DEPTH):
        j = layer // 2
        tag = f"L{layer}"
        load_layer(layer, "mixer", x)
        ln1 = (W["ln1_g"][layer][None, :], W["ln1_b"][layer][None, :])
        if layer % 2 == 0:
            mix, x1, x1b, sv = _hybrid_fwd(tag, x, xb, W, j, cos, sin, ln1)
        else:
            mix, x1, x1b, sv = _rec_fwd(tag, x, xb, W, j, ln1)
        load_layer(layer, "mlp", x1)
        a, h2 = _mm(f"{tag}_mlp1", x1b, W["mlp_w1"][layer], "nn", b_kind="devcol", b_lead=0, epilogue=_relu2_epilogue,
                    out_dtypes=(bf16, bf16), tm=2048)
        ln2 = (W["ln2_g"][layer][None, :], W["ln2_b"][layer][None, :])
        y, x2, x2b = _mm(f"{tag}_mlp2", h2, W["mlp_w2"][layer], "nn", b_kind="lead", b_lead=0, epilogue=_ln_epilogue,
                         extras=(x1,), params=ln2, out_dtypes=(f32, f32, bf16))
        saved.append((x, xb, sv, mix, x1, x1b, a, h2, y))
        x, xb = x2, x2b
    loss, dx = _loss_head(x, target)

    G = {k: [None] * (DEPTH if k.startswith(("ln", "mlp")) else DEPTH // 2) for k in (
        "hyb_w_in", "hyb_sinks", "hyb_conv_w", "hyb_a_log", "hyb_dt_bias", "hyb_norm_w", "hyb_w_out",
        "rec_w_in", "rec_conv_w", "rec_conv_b", "rec_w_a", "rec_b_a", "rec_w_x", "rec_b_x", "rec_lambda", "rec_w_out",
        "ln1_g", "ln1_b", "mlp_w1", "mlp_w2", "ln2_g", "ln2_b")}
    held = {}
    cot_rows, cot_fn = [(dx, 0, D_MODEL)], None
    for layer in reversed(range(DEPTH)):
        j = layer // 2
        tag = f"L{layer}"
        x0, x0b, sv, mix, x1, x1b, a, h2, y = saved[layer]
        ln2 = [W["ln2_g"][layer][None, :], W["ln2_b"][layer][None, :]]
        (dy, dyb), (dg2, db2) = _tl_bwd(f"{tag}_dln2", _ln_res_fn, [(x1, 0, D_MODEL), (y, 0, D_MODEL)], ln2,
                                        cot_rows, cot_fn=cot_fn, skip=(0,), bf16_copy=True)
        G["ln2_g"][layer], G["ln2_b"][layer] = dg2[0], db2[0]
        da = _mm(f"{tag}_dmlp2", dyb, W["mlp_w2"][layer], "nt", b_kind="lead", b_lead=0, epilogue=_drelu2_epilogue,
                 extras=(a,), out_dtypes=(bf16,), tm=2048, tn=512)
        G["mlp_w2"][layer] = _mm(f"{tag}_dw2", h2, dyb, "tn", out_dtypes=(bf16,), tm=2048).reshape(N_DEV, -1, D_MODEL)
        dx1 = _mm(f"{tag}_dmlp1", da, W["mlp_w1"][layer], "nt", b_kind="devcol", b_lead=0, tm=2048)
        G["mlp_w1"][layer] = _mm(f"{tag}_dw1", x1b, da, "tn", o_kind="devcol", out_dtypes=(bf16,), tn=2048)
        ln1 = [W["ln1_g"][layer][None, :], W["ln1_b"][layer][None, :]]
        (dmix, dmixb), (dg1, db1) = _tl_bwd(f"{tag}_dln1", _ln_res_fn, [(x0, 0, D_MODEL), (mix, 0, D_MODEL)], ln1,
                                            [(dx1, 0, D_MODEL), (dy, 0, D_MODEL)], cot_fn=_residual_cot, skip=(0,),
                                            bf16_copy=True)
        G["ln1_g"][layer], G["ln1_b"][layer] = dg1[0], db1[0]
        dx0_a = dmix if layer == 0 else None
        held.update({(k, layer): G[k][layer] for k in ("mlp_w1", "mlp_w2")})
        early = functools.partial(grads_ready, f"l{layer}_early", held)
        if layer % 2 == 0:
            dx = _hybrid_bwd(tag, x0b, dmixb, dx0_a, W, j, cos, sin, sv, G, early)
        else:
            dx = _rec_bwd(tag, x0b, dmixb, dx0_a, W, j, sv, G, early)
        held = {(k, i): G[k][i] for k, i in _layer_weights(layer)[:-2] if not k.endswith("w_out")}
        cot_rows, cot_fn = [(dx, 0, D_MODEL), (dmix, 0, D_MODEL)], _residual_cot
    grads_ready("l0_late", held, {})
    big = {k for k, _ in BIG}
    return loss, dx, {k: jnp.stack(v) for k, v in G.items() if k not in big}


def _layer_weights(layer):
    j = layer // 2
    mixer = ["hyb_w_in", "hyb_w_out"] if layer % 2 == 0 else ["rec_w_in", "rec_w_out", "rec_w_a", "rec_w_x"]
    return [(k, j) for k in mixer] + [("mlp_w1", layer), ("mlp_w2", layer)]


def _my_coords():
    return lax.axis_index("x"), lax.axis_index("y"), lax.axis_index("c")


def _all_gather(name, arrays):
    na = len(arrays)

    def body(*refs):
        x_refs, out_refs = refs[:na], refs[na:2 * na]
        send_sems, recv_sems, local_sems = refs[2 * na:]
        x, y, c = _my_coords()
        me, sibling = (x, y, c), (x, y, 1 - c)
        chips = [(1 - x, y), (x, 1 - y), (1 - x, 1 - y)]

        def blk(a, px, py, pc):
            return out_refs[a].at[4 * px + 2 * py + pc]

        def copy(a, k, block, to, src=None):
            return pltpu.make_async_remote_copy(
                src_ref=blk(a, *block) if src is None else src, dst_ref=blk(a, *block),
                send_sem=send_sems.at[a, k], recv_sem=recv_sems.at[a, k],
                device_id=to, device_id_type=pl.DeviceIdType.MESH)

        mine = [pltpu.make_async_copy(x_refs[a], blk(a, *me), local_sems.at[a]) for a in range(na)]
        for cp in mine:
            cp.start()
        first = []
        for a in range(na):
            first.append(copy(a, 0, me, sibling, src=x_refs[a]))
            first += [copy(a, 1 + j, me, (*chip, c), src=x_refs[a]) for j, chip in enumerate(chips)]
        for cp in first:
            cp.start()
        passed = []
        for a in range(na):
            for j, chip in enumerate(chips):
                copy(a, 1 + j, (*chip, c), me).wait_recv()
                passed.append(copy(a, 4 + j, (*chip, c), sibling))
                passed[-1].start()
        for a in range(na):
            copy(a, 0, sibling, me).wait_recv()
            for j, chip in enumerate(chips):
                copy(a, 4 + j, (*chip, 1 - c), me).wait_recv()
        for cp in first + passed:
            cp.wait_send()
        for cp in mine:
            cp.wait()

    return pl.pallas_call(
        body, name=name,
        out_shape=[jax.ShapeDtypeStruct((N_DEV,) + a.shape, a.dtype) for a in arrays],
        in_specs=[pl.BlockSpec(memory_space=pl.ANY)] * na,
        out_specs=[pl.BlockSpec(memory_space=pl.ANY)] * na,
        scratch_shapes=[pltpu.SemaphoreType.DMA((na, 7)), pltpu.SemaphoreType.DMA((na, 7)),
                        pltpu.SemaphoreType.DMA((na,))],
    )(*arrays)


_HBM = pl.BlockSpec(memory_space=pltpu.HBM)
_SEM = pl.BlockSpec(memory_space=pltpu.SEMAPHORE)


def _flip(k, x, y, c):
    return ((1 - x) if k & 4 else x, (1 - y) if k & 2 else y, (1 - c) if k & 1 else c)


_PEERS = {"gather": (1, 2, 4, 6), "scatter": (1, 2, 3, 4, 5, 6, 7)}


def _push_copies(kind, x_refs, land_refs, send_sems, recv_sems, local_sems):
    x, y, c = _my_coords()
    me = 4 * x + 2 * y + c
    peers = _PEERS[kind]
    remote, local = [], []
    for a in range(len(x_refs)):
        local.append(pltpu.make_async_copy(x_refs[a] if kind == "gather" else x_refs[a].at[me], land_refs[a].at[me],
                                           local_sems.at[a]))
        for n, k in enumerate(peers):
            px, py, pc = _flip(k, x, y, c)
            remote.append(pltpu.make_async_remote_copy(
                src_ref=x_refs[a] if kind == "gather" else x_refs[a].at[4 * px + 2 * py + pc],
                dst_ref=land_refs[a].at[me],
                send_sem=send_sems.at[a * len(peers) + n], recv_sem=recv_sems.at[a * len(peers) + n],
                device_id=(px, py, pc), device_id_type=pl.DeviceIdType.MESH))
    return remote, local


def _pass_to_sibling(name, lands):
    na = len(lands)
    chips = (2, 4, 6)

    def body(*refs):
        out_refs, send_sems, recv_sems = refs[na:2 * na], refs[2 * na], refs[2 * na + 1]
        x, y, c = _my_coords()
        cps = []
        for a in range(na):
            for n, k in enumerate(chips):
                px, py, _ = _flip(k, x, y, c)
                cps.append(pltpu.make_async_remote_copy(
                    src_ref=out_refs[a].at[4 * px + 2 * py + c], dst_ref=out_refs[a].at[4 * px + 2 * py + c],
                    send_sem=send_sems.at[a * 3 + n], recv_sem=recv_sems.at[a * 3 + n],
                    device_id=(x, y, 1 - c), device_id_type=pl.DeviceIdType.MESH))
        for cp in cps:
            cp.start()
        for a in range(na):
            for n, k in enumerate(chips):
                px, py, _ = _flip(k, x, y, c)
                blk = out_refs[a].at[4 * px + 2 * py + (1 - c)]
                pltpu.make_async_remote_copy(src_ref=blk, dst_ref=blk, send_sem=send_sems.at[a * 3 + n],
                                             recv_sem=recv_sems.at[a * 3 + n], device_id=(x, y, 1 - c),
                                             device_id_type=pl.DeviceIdType.MESH).wait_recv()
        for cp in cps:
            cp.wait_send()

    return pl.pallas_call(
        body, name=name,
        out_shape=[jax.ShapeDtypeStruct(l.shape, l.dtype) for l in lands],
        in_specs=[pl.BlockSpec(memory_space=pl.ANY)] * na,
        out_specs=[pl.BlockSpec(memory_space=pl.ANY)] * na,
        input_output_aliases={a: a for a in range(na)},
        scratch_shapes=[pltpu.SemaphoreType.DMA((3 * na,)), pltpu.SemaphoreType.DMA((3 * na,))],
    )(*lands)


_SIDE_EFFECT = pltpu.CompilerParams(has_side_effects=pltpu.SideEffectType.DATAFLOW_SIDE_EFFECTING)


def _push_start(name, kind, srcs, lands):
    na = len(srcs)

    def body(*refs):
        remote, local = _push_copies(kind, refs[:na], refs[na:2 * na], *refs[2 * na:2 * na + 3])
        for cp in remote + local:
            cp.start()
        token = refs[-1]
        token[...] = jnp.zeros_like(token)

    arrays = list(srcs) + list(lands)
    n_remote = na * len(_PEERS[kind])
    res = pl.pallas_call(
        body, name=name,
        out_shape=(pltpu.SemaphoreType.DMA((n_remote,)), pltpu.SemaphoreType.DMA((n_remote,)),
                   pltpu.SemaphoreType.DMA((na,)), *[pltpu.HBM(t.shape, t.dtype) for t in arrays],
                   jax.ShapeDtypeStruct((SUBLANE, LANE), f32)),
        in_specs=[_HBM] * (2 * na),
        out_specs=(_SEM, _SEM, _SEM, *[_HBM] * (2 * na), pl.BlockSpec(memory_space=pltpu.VMEM)),
        input_output_aliases={i: 3 + i for i in range(2 * na)},
        compiler_params=_SIDE_EFFECT,
    )(*[pltpu.with_memory_space_constraint(t, pltpu.HBM) for t in arrays])
    return list(res[:3]), res[3:3 + na], res[3 + na:3 + 2 * na], res[-1][:1, :1]


def _push_wait(name, kind, sems, srcs, lands, after):
    na = len(srcs)

    def body(*refs):
        remote, local = _push_copies(kind, refs[:na], refs[na:2 * na], *refs[2 * na:2 * na + 3])
        for cp in remote:
            cp.wait_send()
            cp.wait_recv()
        for cp in local:
            cp.wait()

    arrays = list(srcs) + list(lands)
    res = pl.pallas_call(
        body, name=name,
        out_shape=tuple(pltpu.HBM(t.shape, t.dtype) for t in arrays),
        in_specs=[_HBM] * (2 * na) + [_SEM] * 3 + [pl.BlockSpec(memory_space=pl.ANY)],
        out_specs=tuple([_HBM] * (2 * na)),
        input_output_aliases={i: i for i in range(2 * na)},
        compiler_params=_SIDE_EFFECT,
    )(*arrays, *sems, after)
    return res[na:]


def _sum_blocks(name, land):
    _, R, n = land.shape
    tr = R

    def body(l_ref, o_ref):
        acc = l_ref[0].astype(f32)
        for s in range(1, N_DEV):
            acc = acc + l_ref[s].astype(f32)
        o_ref[...] = acc

    return pl.pallas_call(
        body, name=name, grid=(R // tr,),
        in_specs=[pl.BlockSpec((N_DEV, tr, n), lambda i: (0, i, 0))],
        out_specs=pl.BlockSpec((tr, n), lambda i: (i, 0)),
        out_shape=jax.ShapeDtypeStruct((R, n), f32),
        compiler_params=_cparams("parallel"),
    )(land)


def _adamw(name, w, g, m, v):
    shape = w.shape
    last = shape[-1]
    rows = math.prod(shape[:-1])
    tm = 256 if rows % 256 == 0 and rows > 256 else rows
    w2, g2, m2, v2 = (t.reshape(rows, last) for t in (w, g, m, v))

    def body(w_ref, g_ref, m_ref, v_ref, d_ref, mo_ref, vo_ref):
        gg = g_ref[...]
        mn = ADAM_B1 * m_ref[...] + (1.0 - ADAM_B1) * gg
        vn = ADAM_B2 * v_ref[...] + (1.0 - ADAM_B2) * jnp.square(gg)
        m_hat = mn / (1.0 - ADAM_B1 ** ADAM_STEP)
        v_hat = vn / (1.0 - ADAM_B2 ** ADAM_STEP)
        d_ref[...] = -ADAM_LR * (m_hat / (jnp.sqrt(v_hat) + ADAM_EPS) + ADAM_WD * w_ref[...])
        mo_ref[...] = mn
        vo_ref[...] = vn

    spec = pl.BlockSpec((tm, last), lambda i: (i, 0))
    d, mn, vn = pl.pallas_call(
        body, name=name, grid=(rows // tm,), in_specs=[spec] * 4, out_specs=[spec] * 3,
        out_shape=[jax.ShapeDtypeStruct((rows, last), f32)] * 3,
        compiler_params=_cparams("parallel"),
    )(w2, g2, m2, v2)
    return d.reshape(shape), mn.reshape(shape), vn.reshape(shape)


def _adamw_land(name, lands, w, m, v, tm=256):
    L = len(lands)
    _, R, C = lands[0].shape
    tm = min(tm, R)

    def body(*refs):
        l_refs, (w_ref, m_ref, v_ref, g_ref, d_ref, mo_ref, vo_ref) = refs[:L], refs[L:]
        for k in range(L):
            @pl.when(pl.program_id(0) == k)
            def _(k=k):
                gg = l_refs[k][0].astype(f32)
                for s in range(1, N_DEV):
                    gg = gg + l_refs[k][s].astype(f32)
                g_ref[...] = gg
                mn = ADAM_B1 * m_ref[...] + (1.0 - ADAM_B1) * gg
                vn = ADAM_B2 * v_ref[...] + (1.0 - ADAM_B2) * jnp.square(gg)
                m_hat = mn / (1.0 - ADAM_B1 ** ADAM_STEP)
                v_hat = vn / (1.0 - ADAM_B2 ** ADAM_STEP)
                d_ref[...] = -ADAM_LR * (m_hat / (jnp.sqrt(v_hat) + ADAM_EPS) + ADAM_WD * w_ref[...])
                mo_ref[...] = mn
                vo_ref[...] = vn

    land_specs = [pl.BlockSpec((N_DEV, tm, C), lambda l, i, k=k: (0, jnp.where(l == k, i, 0), 0)) for k in range(L)]
    spec = pl.BlockSpec((None, tm, C), lambda l, i: (l, i, 0))
    return pl.pallas_call(
        body, name=name, grid=(L, R // tm),
        in_specs=land_specs + [spec] * 3,
        out_specs=[spec] * 4,
        out_shape=[jax.ShapeDtypeStruct((L, R, C), f32)] * 4,
        compiler_params=_cparams("arbitrary", "arbitrary"),
    )(*lands, w, m, v)


BIG = [("hyb_w_in", 2), ("hyb_w_out", 1), ("rec_w_in", 2), ("rec_w_out", 1), ("rec_w_a", 2), ("rec_w_x", 2),
       ("mlp_w1", 2), ("mlp_w2", 1)]
SMALL = [("hyb_conv_w", 2), ("rec_conv_w", 2), ("rec_conv_b", 1), ("rec_b_a", 1), ("rec_b_x", 1), ("rec_lambda", 1)]
REPL = ["hyb_sinks", "hyb_a_log", "hyb_dt_bias", "hyb_norm_w", "ln1_g", "ln1_b", "ln2_g", "ln2_b"]
WEIGHTS = ["hyb_w_in", "hyb_sinks", "hyb_conv_w", "hyb_a_log", "hyb_dt_bias", "hyb_norm_w", "hyb_w_out", "rec_w_in",
           "rec_conv_w", "rec_conv_b", "rec_w_a", "rec_b_a", "rec_w_x", "rec_b_x", "rec_lambda", "rec_w_out",
           "ln1_g", "ln1_b", "mlp_w1", "mlp_w2", "ln2_g", "ln2_b"]


def _pack_rows(parts, dtype, row_mult):
    lead = parts[0].shape[:-1]
    flat = jnp.concatenate([p.astype(dtype) for p in parts], axis=-1)
    n = flat.shape[-1]
    unit = row_mult * LANE
    pad = (-n) % unit
    if pad:
        flat = jnp.concatenate([flat, jnp.zeros(lead + (pad,), dtype)], axis=-1)
    return flat.reshape(lead + ((n + pad) // LANE, LANE))


def _gather_full(gathered, shard_shapes, table):
    flat = gathered.reshape(N_DEV, -1)
    out, off = {}, 0
    for name, ax in table:
        shp = shard_shapes[name]
        n = math.prod(shp)
        arr = flat[:, off:off + n].reshape((N_DEV,) + shp)
        off += n
        arr = jnp.moveaxis(arr, 0, ax)
        out[name] = arr.reshape(shp[:ax] + (N_DEV * shp[ax],) + shp[ax + 1:])
    return out


def _matmul_layouts(tag, gw):
    out = {}
    bw = D_MODEL // LRU_BLOCKS
    for k, g in gw.items():
        L = g.shape[1]
        if k == "hyb_w_in":
            out[k] = _merge_cols(f"{tag}_w_in_merge", g)
        elif k in ("hyb_w_out", "rec_w_out", "mlp_w2"):
            out[k] = jnp.swapaxes(g, 0, 1).reshape(L, N_DEV * g.shape[2], g.shape[3])
        elif k in ("rec_w_a", "rec_w_x"):
            out[k] = jnp.moveaxis(g, 0, 2).reshape(L, LRU_BLOCKS, bw, bw)
        else:
            out[k] = g
    return out


def kernel(x, hyb_w_in, hyb_sinks, hyb_conv_w, hyb_a_log, hyb_dt_bias, hyb_norm_w, hyb_w_out, rec_w_in, rec_conv_w, rec_conv_b, rec_w_a, rec_b_a, rec_w_x, rec_b_x, rec_lambda, rec_w_out, ln1_g, ln1_b, mlp_w1, mlp_w2, ln2_g, ln2_b, loss_target, m_hyb_w_in, m_hyb_sinks, m_hyb_conv_w, m_hyb_a_log, m_hyb_dt_bias, m_hyb_norm_w, m_hyb_w_out, m_rec_w_in, m_rec_conv_w, m_rec_conv_b, m_rec_w_a, m_rec_b_a, m_rec_w_x, m_rec_b_x, m_rec_lambda, m_rec_w_out, m_ln1_g, m_ln1_b, m_mlp_w1, m_mlp_w2, m_ln2_g, m_ln2_b, v_hyb_w_in, v_hyb_sinks, v_hyb_conv_w, v_hyb_a_log, v_hyb_dt_bias, v_hyb_norm_w, v_hyb_w_out, v_rec_w_in, v_rec_conv_w, v_rec_conv_b, v_rec_w_a, v_rec_b_a, v_rec_w_x, v_rec_b_x, v_rec_lambda, v_rec_w_out, v_ln1_g, v_ln1_b, v_mlp_w1, v_mlp_w2, v_ln2_g, v_ln2_b):
    args = locals()
    w = {k: args[k] for k in WEIGHTS}
    m = {k: args["m_" + k] for k in WEIGHTS}
    v = {k: args["v_" + k] for k in WEIGHTS}
    shard_shapes = {k: tuple(t.shape) for k, t in w.items()}
    xi, yi, ci = _my_coords()
    me = 4 * xi + 2 * yi + ci

    in_flight = {}

    def install(tag, names, got):
        for (k, i), arr in zip(names, _matmul_layouts(tag, {k: g for (k, _), g in zip(names, got)}).values()):
            W[k][i] = arr

    def start_gather(tag, names):
        srcs = [w[k][i:i + 1].astype(bf16) for k, i in names]
        *pending, zero = _push_start(f"gather_{tag}_start", "gather", srcs,
                                     [lax.empty((N_DEV,) + s.shape, bf16) for s in srcs])
        in_flight[tag] = (names, pending)
        return zero

    def finish_gather(tag, after):
        names, pending = in_flight.pop(tag)
        half = _push_wait(f"gather_{tag}_wait", "gather", *pending, after)
        install(tag, names, _pass_to_sibling(f"gather_{tag}_pass", half))

    def started(k, zero):
        W[k] = W[k] + zero

    def mixer_w(layer):
        return _layer_weights(layer)[:-2]

    def mlp_w(layer):
        return _layer_weights(layer)[-2:]

    gathered0 = _all_gather("gather_first", [w[k][i:i + 1].astype(bf16) for k, i in mixer_w(0)]
                            + [_pack_rows([w[k].reshape(-1) for k, _ in SMALL], f32, SUBLANE)])
    W = _gather_full(gathered0[-1], shard_shapes, SMALL)
    W.update({k: w[k] for k in REPL})
    W.update({k: {} for k, _ in BIG})
    install("l0a", mixer_w(0), gathered0[:-1])
    started("hyb_sinks", start_gather("l0b", mlp_w(0)) + start_gather("l1a", mixer_w(1)))

    def load_layer(layer, part, after):
        if part == "mixer":
            if layer == 1:
                finish_gather("l1a", after)
            if layer >= 2:
                finish_gather(f"l{layer}", after)
            if 1 <= layer < DEPTH - 1:
                started("hyb_sinks" if layer % 2 == 0 else "rec_conv_b",
                        start_gather(f"l{layer + 1}", _layer_weights(layer + 1)))
        elif layer == 0:
            finish_gather("l0b", after)
            started("ln2_g", start_gather("l1b", mlp_w(1)))
        elif layer == 1:
            finish_gather("l1b", after)

    grads_in_flight = {}

    def grads_ready(tag, a, b):
        g = {**a, **b}
        srcs = list(g.values())
        *pending, zero = _push_start(f"scatter_{tag}_start", "scatter", srcs, [lax.empty(s.shape, bf16) for s in srcs])
        grads_in_flight[tag] = (list(g.keys()), pending)
        return zero

    loss_local, grad_x, G = _local_step(x[0], loss_target[0], W, load_layer, grads_ready)
    loss = lax.psum(loss_local, MESH_AXES)

    landed = {}

    def land(tag, after):
        keys, pending = grads_in_flight[tag]
        landed.update(zip(keys, _push_wait(f"scatter_{tag}_wait", "scatter", *pending, after)))

    tags = list(grads_in_flight)
    for tag in tags[:-1]:
        land(tag, grad_x)
    rest = _pack_rows([G[k].reshape(-1) for k, _ in SMALL] + [G[k].reshape(-1) for k in REPL], f32, SUBLANE)
    g_rest = _sum_blocks("sum_rest", _all_gather("gather_rest", [rest])[0]).reshape(-1)

    grads, delta, new_m, new_v = {}, {}, {}, {}

    def adamw_big(k):
        shp = shard_shapes[k]
        s3 = (shp[0], math.prod(shp[1:-1]), shp[-1])
        lands = [landed[(k, i)].reshape((N_DEV,) + s3[1:]) for i in range(shp[0])]
        res = _adamw_land("adamw_" + k, lands, w[k].reshape(s3), m[k].reshape(s3), v[k].reshape(s3))
        grads[k], delta[k], new_m[k], new_v[k] = (r.reshape(shp) for r in res)

    late = {k for k, _ in grads_in_flight[tags[-1]][0]}
    for k in [k for k, _ in BIG if k not in late]:
        adamw_big(k)
        done = new_v[k]
    land(tags[-1], done)
    for k in [k for k, _ in BIG if k in late]:
        adamw_big(k)
    off = 0
    for k, ax in SMALL:
        full_shape = G[k].shape
        n = math.prod(full_shape)
        full = g_rest[off:off + n].reshape(full_shape)
        off += n
        s = shard_shapes[k][ax]
        grads[k] = lax.dynamic_slice_in_dim(full, me * s, s, axis=ax)
    for k in REPL:
        n = math.prod(shard_shapes[k])
        grads[k] = g_rest[off:off + n].reshape(shard_shapes[k])
        off += n

    for k in [k for k, _ in SMALL] + REPL:
        delta[k], new_m[k], new_v[k] = _adamw("adamw_" + k, w[k], grads[k], m[k], v[k])

    return (loss, grad_x[None], *[grads[k] for k in WEIGHTS], *[delta[k] for k in WEIGHTS],
            *[new_m[k] for k in WEIGHTS], *[new_v[k] for k in WEIGHTS])
```

```python
import functools
import math

import jax
import jax.numpy as jnp
from jax import lax
from jax.experimental import pallas as pl
from jax.experimental.pallas import tpu as pltpu

f32 = jnp.float32
bf16 = jnp.bfloat16

N_DEV = 8
D_MODEL = 1024
DEPTH = 4
A_HEAD_DIM = 64
A_Q_HEADS = 8
WINDOW = 128
ROPE_THETA = 10000.0
B_HEADS = 4
B_HEAD_DIM = 128
B_CHUNK = 64
LRU_BLOCKS = 4
LRU_C = 8.0
D_FF = 4 * D_MODEL
HYB_PROJ = 2824
HYB_PROJ_PAD = 3072
DN_ALPHA = (2 * DEPTH) ** 0.25
LN_EPS = 1e-5
NORM_EPS = 1e-6
ADAM_LR = 0.001
ADAM_B1 = 0.9
ADAM_B2 = 0.999
ADAM_EPS = 1e-08
ADAM_WD = 0.01
ADAM_STEP = 10

LANE = 128
SUBLANE = 8
VMEM_LIMIT = 48 * 1024 * 1024

CB_QA, CB_KA, CB_VA, CB_CONV, CB_Z, CB_LG = 0, 4, 5, 6, 18, 22

MESH_AXES = ("x", "y", "c")


def _cparams(*sem):
    return pltpu.CompilerParams(dimension_semantics=sem, vmem_limit_bytes=VMEM_LIMIT)


def _dot(a, b, dims, precision=None):
    return lax.dot_general(a, b, (dims, ((), ())), preferred_element_type=f32, precision=precision)


NN = ((1,), (0,))
NT = ((1,), (1,))
TN = ((0,), (0,))


def _mat_spec(arr, kind, lead, br, bc, rb, cb):
    if kind == "plain":
        return pl.BlockSpec((br, bc), lambda i, j, k: (rb(i, j, k), cb(i, j, k)))
    if kind == "lead":
        return pl.BlockSpec((None, br, bc), lambda i, j, k: (lead, rb(i, j, k), cb(i, j, k)))
    assert kind == "devcol" and bc == arr.shape[-1]
    return pl.BlockSpec((None, None, br, bc), lambda i, j, k: (cb(i, j, k), lead, rb(i, j, k), 0))


def _mm(name, a, b, mode, *, b_kind="plain", b_lead=0, o_kind="plain", epilogue=None, extras=(), params=(),
        out_dtypes=(f32,), tm=1024, tn=1024, tk=None):
    if tk is None:
        tk = 512 if mode == "tn" else 1024
    if b_kind in ("plain", "lead"):
        b_rows, b_cols = b.shape[-2:]
    else:
        b_rows, b_cols = b.shape[-2], N_DEV * b.shape[-1]
    if mode == "nn":
        (M, K), (K2, N) = a.shape, (b_rows, b_cols)
    elif mode == "nt":
        (M, K), (N, K2) = a.shape, (b_rows, b_cols)
    else:
        (K, M), (K2, N) = a.shape, (b_rows, b_cols)
    assert K == K2, (name, a.shape, b.shape, mode)
    tm, tn, tk = min(tm, M), min(tn, N), min(tk, K)
    cols_are_n = mode != "nt"
    if b_kind == "devcol":
        tn, tk = (b.shape[-1], tk) if cols_are_n else (tn, b.shape[-1])
    shard = N // N_DEV
    if o_kind == "devcol":
        tn = max(shard, tn // shard * shard)
    assert M % tm == 0 and N % tn == 0 and K % tk == 0, (name, M, N, K, tm, tn, tk)
    nk = K // tk
    dims = {"nn": NN, "nt": NT, "tn": TN}[mode]
    n_ex, n_out = len(extras) + len(params), len(out_dtypes)

    def body(*refs):
        a_ref, b_ref = refs[:2]
        ex = refs[2:2 + n_ex]
        outs = refs[2 + n_ex:2 + n_ex + n_out]
        acc = refs[-1]
        k = pl.program_id(2)

        @pl.when(k == 0)
        def _():
            acc[...] = jnp.zeros_like(acc)

        acc[...] += _dot(a_ref[...].astype(bf16), b_ref[...].astype(bf16), dims)

        @pl.when(k == nk - 1)
        def _():
            r = acc[...]
            res = epilogue(r, *[e[...] for e in ex]) if epilogue is not None else (r,)
            for o, v in zip(outs, res):
                if o_kind == "plain":
                    o[...] = v.astype(o.dtype)
                else:
                    for q in range(tn // shard):
                        o[q] = v[:, q * shard:(q + 1) * shard].astype(o.dtype)

    if mode == "tn":
        a_spec = pl.BlockSpec((tk, tm), lambda i, j, k: (k, i))
    else:
        a_spec = pl.BlockSpec((tm, tk), lambda i, j, k: (i, k))
    jb, kb = (lambda i, j, k: j), (lambda i, j, k: k)
    if mode == "nt":
        b_spec = _mat_spec(b, b_kind, b_lead, tn, tk, jb, kb)
    else:
        b_spec = _mat_spec(b, b_kind, b_lead, tk, tn, kb, jb)
    e_spec = pl.BlockSpec((tm, tn), lambda i, j, k: (i, j))
    if o_kind == "plain":
        o_spec, o_shape = e_spec, (M, N)
    else:
        o_spec, o_shape = pl.BlockSpec((tn // shard, tm, shard), lambda i, j, k: (j, i, 0)), (N_DEV, M, shard)
    res = pl.pallas_call(
        body, name=name,
        grid=(M // tm, N // tn, nk),
        in_specs=[a_spec, b_spec] + [e_spec] * len(extras)
        + [pl.BlockSpec(p.shape, lambda i, j, k: (0, 0)) for p in params],
        out_specs=[o_spec] * n_out,
        out_shape=[jax.ShapeDtypeStruct(o_shape, dt) for dt in out_dtypes],
        scratch_shapes=[pltpu.VMEM((tm, tn), f32)],
        compiler_params=_cparams("parallel", "parallel", "arbitrary"),
    )(a, b, *extras, *params)
    return res[0] if n_out == 1 else res


def _row_spec(tm, cb, width):
    assert (cb * LANE) % width == 0
    blk = (cb * LANE) // width
    return pl.BlockSpec((tm, width), lambda i: (i, blk))


def _whole_spec(p):
    nd = p.ndim
    return pl.BlockSpec(p.shape, lambda i: (0,) * nd)


def _tl_bwd(name, fn, rows, params, cot_rows, cot_fn=None, skip=(), bf16_copy=False, tm=512):
    T = rows[0][0].shape[0]
    tm = min(tm, T)
    nr, npar, nc = len(rows), len(params), len(cot_rows)
    keep = [k for k in range(nr) if k not in skip]
    n_rows = len(keep) + int(bf16_copy)
    row_dtypes = [(rows[k][2], f32) for k in keep] + ([(rows[keep[0]][2], bf16)] if bf16_copy else [])

    def body(*refs):
        vals = [r[...] for r in refs[:nr + npar]]
        cots = [r[...] for r in refs[nr + npar:nr + npar + nc]]
        outs = refs[nr + npar + nc:]
        cot = tuple(cot_fn(*cots)) if cot_fn is not None else tuple(cots)
        _, vjp = jax.vjp(fn, *vals)
        grads = vjp(cot)
        for o, k in zip(outs, keep):
            o[...] = grads[k].astype(o.dtype)
        if bf16_copy:
            outs[len(keep)][...] = grads[keep[0]].astype(bf16)
        i = pl.program_id(0)
        for o, g in zip(outs[n_rows:], grads[nr:]):
            @pl.when(i == 0)
            def _(o=o):
                o[...] = jnp.zeros_like(o)
            o[...] += g

    res = pl.pallas_call(
        body, name=name, grid=(T // tm,),
        in_specs=[_row_spec(tm, cb, w) for (_, cb, w) in rows] + [_whole_spec(p) for p in params]
        + [_row_spec(tm, cb, w) for (_, cb, w) in cot_rows],
        out_specs=[pl.BlockSpec((tm, w), lambda i: (i, 0)) for w, _ in row_dtypes] + [_whole_spec(p) for p in params],
        out_shape=[jax.ShapeDtypeStruct((T, w), dt) for w, dt in row_dtypes]
        + [jax.ShapeDtypeStruct(p.shape, f32) for p in params],
        compiler_params=_cparams("arbitrary"),
    )(*[r[0] for r in rows], *params, *[r[0] for r in cot_rows])
    return res[:n_rows], res[n_rows:]


def _ln_res_fn(x, mix, g, b):
    pre = DN_ALPHA * x + mix
    mu = jnp.mean(pre, axis=-1, keepdims=True)
    var = jnp.mean(jnp.square(pre - mu), axis=-1, keepdims=True)
    return ((pre - mu) * lax.rsqrt(var + LN_EPS) * g + b,)


@jax.custom_jvp
def _expm1(x):
    small = jnp.abs(x) < 0.3
    xs = jnp.where(small, x, 0.0)
    poly = xs * (1.0 + xs * (1 / 2 + xs * (1 / 6 + xs * (1 / 24 + xs * (1 / 120 + xs * (
        1 / 720 + xs * (1 / 5040 + xs * (1 / 40320 + xs * (1 / 362880)))))))))
    return jnp.where(small, poly, jnp.exp(x) - 1.0)


@_expm1.defjvp
def _expm1_jvp(primals, tangents):
    (x,), (t,) = primals, tangents
    return _expm1(x), t * jnp.exp(x)


def _rglru_pre_fn(pre_r, pre_i, xc, b_a, b_x, lam):
    r = jax.nn.sigmoid(pre_r + b_a)
    i = jax.nn.sigmoid(pre_i + b_x)
    log_a = -LRU_C * r * jax.nn.softplus(-lam)
    a = jnp.exp(log_a)
    b = jnp.sqrt(-_expm1(2.0 * log_a)) * (i * xc)
    return a, b


def _rec_gate_fn(h, gate):
    return (h * jax.nn.gelu(gate),)


def _loss_head(y, t, tm=512):
    T, Dm = y.shape
    tm = min(tm, T)

    def body(y_ref, t_ref, dy_ref, loss_ref):
        e = y_ref[...] - t_ref[...]
        dy_ref[...] = e * (1.0 / Dm)

        @pl.when(pl.program_id(0) == 0)
        def _():
            loss_ref[...] = jnp.zeros_like(loss_ref)

        loss_ref[...] += 0.5 * jnp.sum(jnp.mean(e * e, axis=-1, keepdims=True), axis=0, keepdims=True)

    dy, loss = pl.pallas_call(
        body, name="loss_head", grid=(T // tm,),
        in_specs=[pl.BlockSpec((tm, Dm), lambda i: (i, 0))] * 2,
        out_specs=[pl.BlockSpec((tm, Dm), lambda i: (i, 0)), pl.BlockSpec((SUBLANE, LANE), lambda i: (0, 0))],
        out_shape=[jax.ShapeDtypeStruct((T, Dm), f32), jax.ShapeDtypeStruct((SUBLANE, LANE), f32)],
        compiler_params=_cparams("arbitrary"),
    )(y, t)
    return loss[0, 0], dy


def _conv_fwd(name, x, cb0, nblk, w, bias, tm=2048):
    T = x.shape[0]
    tm = min(tm, T)
    hb = tm // SUBLANE
    has_b = bias is not None

    def body(*refs):
        cur, prev, w_ref = refs[:3]
        b_ref = refs[3] if has_b else None
        o = refs[-1]
        i = pl.program_id(1)
        p = jnp.where(i > 0, prev[...], 0.0)
        xcat = jnp.concatenate([p, cur[...]], axis=0)
        acc = cur[...] * w_ref[3:4, :]
        for j in range(3):
            acc = acc + pltpu.roll(xcat, 3 - j, axis=0)[SUBLANE:] * w_ref[j:j + 1, :]
        if has_b:
            acc = acc + b_ref[...]
        o[...] = acc

    in_specs = [
        pl.BlockSpec((tm, LANE), lambda c, i: (i, cb0 + c)),
        pl.BlockSpec((SUBLANE, LANE), lambda c, i: (jnp.maximum(i * hb - 1, 0), cb0 + c)),
        pl.BlockSpec((4, LANE), lambda c, i: (0, c)),
    ]
    args = [x, x, w]
    if has_b:
        in_specs.append(pl.BlockSpec((1, LANE), lambda c, i: (0, c)))
        args.append(bias)
    return pl.pallas_call(
        body, name=name, grid=(nblk, T // tm),
        in_specs=in_specs,
        out_specs=pl.BlockSpec((tm, LANE), lambda c, i: (i, c)),
        out_shape=jax.ShapeDtypeStruct((T, nblk * LANE), f32),
        compiler_params=_cparams("parallel", "parallel"),
    )(*args)


def _conv_bwd(name, dy, x, cb0, nblk, w, into, into_cb, tm=2048):
    T = x.shape[0]
    tm = min(tm, T)
    hb = tm // SUBLANE
    nt = T // tm

    def body(dcur, dnext, xcur, xprev, w_ref, _, dx_ref, dw_ref, db_ref):
        i = pl.program_id(1)
        d = dcur[...]
        dn = jnp.where(i < nt - 1, dnext[...], 0.0)
        dcat = jnp.concatenate([d, dn], axis=0)
        acc = d * w_ref[3:4, :]
        for j in range(3):
            s = 3 - j
            acc = acc + pltpu.roll(dcat, tm + SUBLANE - s, axis=0)[:tm] * w_ref[j:j + 1, :]
        dx_ref[...] = acc.astype(dx_ref.dtype)

        p = jnp.where(i > 0, xprev[...], 0.0)
        xcat = jnp.concatenate([p, xcur[...]], axis=0)
        rows = [jnp.sum(d * pltpu.roll(xcat, 3 - j, axis=0)[SUBLANE:], axis=0, keepdims=True) for j in range(3)]
        rows.append(jnp.sum(d * xcur[...], axis=0, keepdims=True))
        rows.append(jnp.zeros((SUBLANE - 4, LANE), f32))

        @pl.when(i == 0)
        def _():
            dw_ref[...] = jnp.zeros_like(dw_ref)
            db_ref[...] = jnp.zeros_like(db_ref)

        dw_ref[...] += jnp.concatenate(rows, axis=0)
        db_ref[...] += jnp.broadcast_to(jnp.sum(d, axis=0, keepdims=True), (SUBLANE, LANE))

    nh = T // SUBLANE
    dx, dw, db = pl.pallas_call(
        body, name=name, grid=(nblk, nt),
        in_specs=[
            pl.BlockSpec((tm, LANE), lambda c, i: (i, c)),
            pl.BlockSpec((SUBLANE, LANE), lambda c, i: (jnp.minimum((i + 1) * hb, nh - 1), c)),
            pl.BlockSpec((tm, LANE), lambda c, i: (i, cb0 + c)),
            pl.BlockSpec((SUBLANE, LANE), lambda c, i: (jnp.maximum(i * hb - 1, 0), cb0 + c)),
            pl.BlockSpec((4, LANE), lambda c, i: (0, c)),
            pl.BlockSpec(memory_space=pl.ANY),
        ],
        out_specs=[
            pl.BlockSpec((tm, LANE), lambda c, i: (i, into_cb + c)),
            pl.BlockSpec((SUBLANE, LANE), lambda c, i: (0, c)),
            pl.BlockSpec((SUBLANE, LANE), lambda c, i: (0, c)),
        ],
        out_shape=[jax.ShapeDtypeStruct(into.shape, into.dtype),
                   jax.ShapeDtypeStruct((SUBLANE, nblk * LANE), f32),
                   jax.ShapeDtypeStruct((SUBLANE, nblk * LANE), f32)],
        input_output_aliases={5: 0},
        compiler_params=_cparams("parallel", "arbitrary"),
    )(dy, dy, x, x, w, into)
    return dx, dw[:4], db[0]


@functools.partial(jax.custom_vjp, nondiff_argnums=(1,))
def _lroll(x, s):
    return pltpu.roll(x, s, axis=1)


def _lroll_fwd(x, s):
    return _lroll(x, s), None


def _lroll_bwd(s, _, g):
    return (_lroll(g, (LANE - s) % LANE),)


_lroll.defvjp(_lroll_fwd, _lroll_bwd)


def _rope_tables(T):
    half = A_HEAD_DIM // 2
    inv_freq = ROPE_THETA ** (-jnp.arange(half, dtype=f32) / half)
    ang = jnp.arange(T, dtype=f32)[:, None] * inv_freq[None, :]
    cos, sin = jnp.cos(ang), jnp.sin(ang)
    return jnp.tile(jnp.concatenate([cos, cos], axis=1), (1, 2)), jnp.tile(jnp.concatenate([-sin, sin], axis=1), (1, 2))


def _attn_block_fn(n, q, kp, kc, vp, vc, cq, sq, cp, sp, sinks):
    W = WINDOW
    lane = lax.broadcasted_iota(jnp.int32, (W, LANE), 1)
    lo_half = (lane % A_HEAD_DIM) < (A_HEAD_DIM // 2)
    lane8 = lax.broadcasted_iota(jnp.int32, sinks.shape, 1)

    def rope(x, c, s):
        return x * c + jnp.where(lo_half, _lroll(x, LANE - A_HEAD_DIM // 2), _lroll(x, A_HEAD_DIM // 2)) * s

    k2 = jnp.concatenate([rope(kp, cp, sp), rope(kc, cq, sq)], axis=0).astype(bf16)
    v2 = jnp.concatenate([vp, vc], axis=0).astype(bf16)
    qs = []
    for t in range(4):
        qt = rope(q[:, LANE * t:LANE * (t + 1)], cq, sq)
        g = t // 2
        for hh in range(2):
            qa = jnp.where((lane // A_HEAD_DIM) == hh, qt, 0.0)
            qs.append(_lroll(qa, A_HEAD_DIM) if hh != g else qa)
    s_all = _dot(jnp.concatenate(qs, axis=0).astype(bf16), k2, NT) * (A_HEAD_DIM ** -0.5)
    row = lax.broadcasted_iota(jnp.int32, (W, 2 * W), 0)
    col = lax.broadcasted_iota(jnp.int32, (W, 2 * W), 1)
    dist = row + W - col
    mask = (dist >= 0) & (dist < W) & ((col >= W) | (n > 0))
    ps = []
    for j in range(A_Q_HEADS):
        s = jnp.where(mask, s_all[W * j:W * (j + 1)], -jnp.inf)
        sink = jnp.sum(jnp.where(lane8 == j, sinks, 0.0), axis=1, keepdims=True)
        m = jnp.maximum(jnp.max(s, axis=-1, keepdims=True), sink)
        e = jnp.exp(s - m)
        ps.append((e / (jnp.sum(e, axis=-1, keepdims=True) + jnp.exp(sink - m))).astype(bf16))
    o = _dot(jnp.concatenate(ps, axis=0), v2, NN)
    outs = []
    for t in range(4):
        g = t // 2
        ot = jnp.zeros((W, LANE), f32)
        for hh in range(2):
            j = 2 * t + hh
            oj = jnp.where((lane // A_HEAD_DIM) == g, o[W * j:W * (j + 1)], 0.0)
            ot = ot + (_lroll(oj, A_HEAD_DIM) if hh != g else oj)
        outs.append(ot)
    return jnp.concatenate(outs, axis=1)


def _attn_specs():
    W = WINDOW
    prev = lambda n: jnp.maximum(n - 1, 0)
    return [
        pl.BlockSpec((W, 4 * LANE), lambda n: (n, CB_QA // 4)),
        pl.BlockSpec((W, LANE), lambda n: (prev(n), CB_KA)),
        pl.BlockSpec((W, LANE), lambda n: (n, CB_KA)),
        pl.BlockSpec((W, LANE), lambda n: (prev(n), CB_VA)),
        pl.BlockSpec((W, LANE), lambda n: (n, CB_VA)),
        pl.BlockSpec((W, LANE), lambda n: (n, 0)),
        pl.BlockSpec((W, LANE), lambda n: (n, 0)),
        pl.BlockSpec((W, LANE), lambda n: (prev(n), 0)),
        pl.BlockSpec((W, LANE), lambda n: (prev(n), 0)),
        pl.BlockSpec((1, A_Q_HEADS), lambda n: (0, 0)),
    ]


def _attn_fwd(name, proj, cos, sin, sinks):
    T = proj.shape[0]
    W = WINDOW

    def body(*refs):
        o = refs[-1]
        o[...] = _attn_block_fn(pl.program_id(0), *[r[...] for r in refs[:-1]]).astype(o.dtype)

    return pl.pallas_call(
        body, name=name, grid=(T // W,),
        in_specs=_attn_specs(),
        out_specs=pl.BlockSpec((W, 4 * LANE), lambda n: (n, 0)),
        out_shape=jax.ShapeDtypeStruct((T, 2 * 4 * LANE), bf16),
        compiler_params=_cparams("parallel"),
    )(proj, proj, proj, proj, proj, cos, sin, cos, sin, sinks)


def _attn_bwd(name, proj, cos, sin, sinks, d_oab):
    T = proj.shape[0]
    W = WINDOW
    Q = 4 * LANE
    nb = T // W

    def body(*refs):
        ins = [r[...] for r in refs[:10]]
        do = refs[10][...]
        out_ref, ds_ref, d_ref = refs[11:]
        n = pl.program_id(0)
        _, vjp = jax.vjp(functools.partial(_attn_block_fn, n), *ins)
        dq, dkp, dkc, dvp, dvc, _, _, _, _, dsk = vjp(do)

        @pl.when(n == 0)
        def _():
            d_ref[:, Q:] = jnp.zeros((T, 2 * LANE), f32)
            ds_ref[...] = jnp.zeros_like(ds_ref)

        cur = pl.ds(pl.multiple_of(n * W, W), W)
        d_ref[cur, :Q] = dq
        d_ref[cur, Q:Q + LANE] += dkc
        d_ref[cur, Q + LANE:] += dvc
        ds_ref[...] += dsk

        @pl.when(n > 0)
        def _():
            prv = pl.ds(pl.multiple_of((n - 1) * W, W), W)
            d_ref[prv, Q:Q + LANE] += dkp
            d_ref[prv, Q + LANE:] += dvp

        @pl.when(n == nb - 1)
        def _():
            out_ref[...] = d_ref[...].astype(out_ref.dtype)

    return pl.pallas_call(
        body, name=name, grid=(nb,),
        in_specs=_attn_specs() + [pl.BlockSpec((W, Q), lambda n: (n, 0))],
        out_specs=[pl.BlockSpec((T, Q + 2 * LANE), lambda n: (0, 0)),
                   pl.BlockSpec((1, A_Q_HEADS), lambda n: (0, 0))],
        out_shape=[jax.ShapeDtypeStruct((T, HYB_PROJ_PAD), bf16), jax.ShapeDtypeStruct((1, A_Q_HEADS), f32)],
        scratch_shapes=[pltpu.VMEM((T, Q + 2 * LANE), f32)],
        compiler_params=_cparams("arbitrary"),
    )(proj, proj, proj, proj, proj, cos, sin, cos, sin, sinks, d_oab)


def _bdot(spec, a, b, precision=None):
    return jnp.einsum(spec, a, b, preferred_element_type=f32, precision=precision)


@jax.custom_vjp
def _tri_inv(a):
    H, C, _ = a.shape
    B = 2 * SUBLANE
    nb = C // B
    r = lax.broadcasted_iota(jnp.int32, (C, C), 0)
    c = lax.broadcasted_iota(jnp.int32, (C, C), 1)
    a4 = jnp.where((r // B) == (c // B), a, 0.0).reshape(H, nb, B, C)
    t4 = jnp.broadcast_to(jnp.where(r == c, 1.0, 0.0).astype(f32), a.shape).reshape(H, nb, B, C)
    for j in range(B - 1):
        col = jnp.concatenate([a4[:, b:b + 1, :, B * b + j:B * b + j + 1] for b in range(nb)], axis=1)
        t4 = t4 - col * t4[:, :, j:j + 1, :]
    x = t4.reshape(H, C, C)
    hi = lax.Precision.HIGH
    while B < C:
        m = jnp.where(((r // (2 * B)) == (c // (2 * B))) & ((r // B) > (c // B)), a, 0.0)
        x = x - _bdot("hij,hjk->hik", x, _bdot("hij,hjk->hik", m, x, precision=hi), precision=hi)
        B *= 2
    return x


def _tri_inv_fwd(a):
    t = _tri_inv(a)
    return t, t


def _tri_inv_bwd(t, g):
    C = t.shape[-1]
    r = lax.broadcasted_iota(jnp.int32, (C, C), 0)
    c = lax.broadcasted_iota(jnp.int32, (C, C), 1)
    x = _bdot("hki,hkj->hij", t, g, precision=lax.Precision.HIGHEST)
    y = _bdot("hik,hjk->hij", x, t, precision=lax.Precision.HIGHEST)
    return (jnp.where(r > c, -y, 0.0),)


_tri_inv.defvjp(_tri_inv_fwd, _tri_inv_bwd)


@jax.custom_vjp
def _tri_inv_saved(a, t):
    return t


_tri_inv_saved.defvjp(lambda a, t: (t, t), lambda t, g: (_tri_inv_bwd(t, g)[0], jnp.zeros_like(t)))


def _silu(x):
    return x * jax.nn.sigmoid(x)


def _l2n(x):
    return x * lax.rsqrt(jnp.sum(x * x, axis=-1, keepdims=True) + NORM_EPS)


def _delta_chunk_fn(cq, ck, cv, z, lg, a_log, dt_bias, norm_w, S, t_saved=None, want_t=False):
    C = B_CHUNK
    lane = lax.broadcasted_iota(jnp.int32, (C, LANE), 1)
    pick = lambda l0: jnp.concatenate(
        [jnp.sum(jnp.where(lane == l0 + h, lg, 0.0), axis=1, keepdims=True)[None] for h in range(B_HEADS)], axis=0)
    bl, al = pick(0), pick(B_HEADS)
    q = _l2n(_silu(cq)) * (B_HEAD_DIM ** -0.5)
    k = _l2n(_silu(ck))
    v = _silu(cv)
    beta = jax.nn.sigmoid(bl)
    g = -jnp.exp(a_log) * jax.nn.softplus(al + dt_bias)
    r = lax.broadcasted_iota(jnp.int32, (C, C), 0)
    c = lax.broadcasted_iota(jnp.int32, (C, C), 1)
    eye = r == c
    g_row = jnp.sum(jnp.where(eye, g, 0.0), axis=1, keepdims=True)
    gc = jnp.sum(jnp.where(c <= r, g_row, 0.0), axis=2, keepdims=True)
    gc_row = jnp.sum(jnp.where(eye, gc, 0.0), axis=1, keepdims=True)
    decay_incl = jnp.exp(jnp.where(r >= c, gc - gc_row, -jnp.inf))
    decay_strict = jnp.where(r > c, decay_incl, 0.0)
    kb = k * beta
    vb = v * beta
    kbf = k.astype(bf16)
    a_mat = _bdot("hik,hjk->hij", kb.astype(bf16), kbf) * decay_strict
    t_f32 = _tri_inv(a_mat) if t_saved is None else _tri_inv_saved(a_mat, t_saved)
    t_mat = t_f32.astype(bf16)
    eg = jnp.exp(gc)
    u = _bdot("hij,hjv->hiv", t_mat, vb.astype(bf16))
    w = _bdot("hij,hjk->hik", t_mat, (kb * eg).astype(bf16))
    qk = _bdot("hik,hjk->hij", q.astype(bf16), kbf) * decay_incl
    g_last = jnp.sum(g, axis=1, keepdims=True)
    k_tail = k * jnp.exp(g_last - gc)
    Sb = S.astype(bf16)
    v_new = u - _bdot("hck,hkv->hcv", w.astype(bf16), Sb)
    o = _bdot("hck,hkv->hcv", (q * eg).astype(bf16), Sb) + _bdot("hij,hjv->hiv", qk.astype(bf16), v_new.astype(bf16))
    S_new = S * jnp.exp(g_last) + _bdot("hck,hcv->hkv", k_tail.astype(bf16), v_new.astype(bf16))
    ob = o * lax.rsqrt(jnp.mean(o * o, axis=-1, keepdims=True) + NORM_EPS) * norm_w
    return (ob * _silu(z), S_new) + ((t_f32,) if want_t else ())


DELTA_CHUNKS_PER_STEP = 8


def _delta_in_specs(rev, N):
    C = DELTA_CHUNKS_PER_STEP * B_CHUNK
    ix = (lambda n: N - 1 - n) if rev else (lambda n: n)
    specs = [pl.BlockSpec((C, 3 * B_HEADS * LANE), lambda n: (ix(n), 0))]
    specs += [pl.BlockSpec((C, LANE), lambda n, h=h: (ix(n), CB_Z + h)) for h in range(B_HEADS)]
    specs += [
        pl.BlockSpec((C, LANE), lambda n: (ix(n), CB_LG)),
        pl.BlockSpec((B_HEADS, 1, 1), lambda n: (0, 0, 0)),
        pl.BlockSpec((B_HEADS, 1, 1), lambda n: (0, 0, 0)),
        pl.BlockSpec((1, LANE), lambda n: (0, 0)),
    ]
    return specs


def _delta_inputs(u, c_ref, z_refs, lg, al, dt, nw):
    H = B_HEADS
    rows = slice(u * B_CHUNK, (u + 1) * B_CHUNK)
    part = lambda p: jnp.stack([c_ref[rows, LANE * (p * H + h):LANE * (p * H + h + 1)] for h in range(H)])
    return (part(0), part(1), part(2), jnp.stack([z[rows, :] for z in z_refs]), lg[rows, :], al[...], dt[...], nw[...])


def _delta_fwd(name, c, proj, a_log, dt_bias, norm_w, o_ab):
    T = c.shape[0]
    C = B_CHUNK
    N = T // C
    Dh = B_HEAD_DIM
    H = B_HEADS

    def body(*refs):
        c_ref, z_refs, (lg, al, dt, nw) = refs[0], refs[1:1 + H], refs[1 + H:5 + H]
        o_ref, s_ref, t_ref, S = refs[6 + H:]

        @pl.when(pl.program_id(0) == 0)
        def _():
            S[...] = jnp.zeros_like(S)

        s = S[...]
        for u in range(U):
            s_ref[:, u] = s
            ob, s, t = _delta_chunk_fn(*_delta_inputs(u, c_ref, z_refs, lg, al, dt, nw), s, want_t=True)
            for h in range(H):
                o_ref[u * C:(u + 1) * C, LANE * h:LANE * (h + 1)] = ob[h].astype(o_ref.dtype)
            t_ref[:, u] = t
        S[...] = s

    U = DELTA_CHUNKS_PER_STEP
    return pl.pallas_call(
        body, name=name, grid=(N // U,),
        in_specs=_delta_in_specs(False, N // U) + [pl.BlockSpec(memory_space=pl.ANY)],
        out_specs=[pl.BlockSpec((U * C, H * LANE), lambda n: (n, 1)),
                   pl.BlockSpec((H, U, Dh, Dh), lambda n: (0, n, 0, 0)),
                   pl.BlockSpec((H, U, C, C), lambda n: (0, n, 0, 0))],
        out_shape=[jax.ShapeDtypeStruct(o_ab.shape, o_ab.dtype), jax.ShapeDtypeStruct((H, N, Dh, Dh), f32),
                   jax.ShapeDtypeStruct((H, N, C, C), f32)],
        input_output_aliases={5 + H: 0},
        scratch_shapes=[pltpu.VMEM((H, Dh, Dh), f32)],
        compiler_params=_cparams("arbitrary"),
    )(c, *([proj] * H), proj, a_log, dt_bias, norm_w, o_ab)


def _delta_bwd(name, c, proj, a_log, dt_bias, norm_w, s_saved, t_saved, d_oab, dproj):
    T = c.shape[0]
    C = B_CHUNK
    N = T // C
    Dh = B_HEAD_DIM
    H = B_HEADS

    def body(*refs):
        c_ref, z_refs, (lg, al, dt, nw) = refs[0], refs[1:1 + H], refs[1 + H:5 + H]
        s_ref, t_ref, do_ref = refs[5 + H:8 + H]
        dc, dtail, dal, ddt, dnw, dS = refs[9 + H:]

        @pl.when(pl.program_id(0) == 0)
        def _():
            dS[...] = jnp.zeros_like(dS)
            dal[...] = jnp.zeros_like(dal)
            ddt[...] = jnp.zeros_like(ddt)
            dnw[...] = jnp.zeros_like(dnw)

        ds = dS[...]
        for u in reversed(range(U)):
            rows = slice(u * C, (u + 1) * C)
            _, vjp = jax.vjp(functools.partial(_delta_chunk_fn, t_saved=t_ref[:, u]),
                             *_delta_inputs(u, c_ref, z_refs, lg, al, dt, nw), s_ref[:, u])
            do = jnp.stack([do_ref[rows, LANE * h:LANE * (h + 1)] for h in range(H)])
            g = vjp((do, ds))
            for h in range(H):
                for p in range(3):
                    dc[rows, LANE * (p * H + h):LANE * (p * H + h + 1)] = g[p][h]
                dtail[rows, LANE * h:LANE * (h + 1)] = g[3][h].astype(dtail.dtype)
            dtail[rows, LANE * H:LANE * (H + 1)] = g[4].astype(dtail.dtype)
            dtail[rows, LANE * (H + 1):] = jnp.zeros((C, LANE), dtail.dtype)
            dal[...] += g[5]
            ddt[...] += g[6]
            dnw[...] += g[7]
            ds = g[8]
        dS[...] = ds

    U = DELTA_CHUNKS_PER_STEP
    NB = N // U
    rn = lambda n: NB - 1 - n
    return pl.pallas_call(
        body, name=name, grid=(NB,),
        in_specs=_delta_in_specs(True, NB) + [
            pl.BlockSpec((H, U, Dh, Dh), lambda n: (0, rn(n), 0, 0)),
            pl.BlockSpec((H, U, C, C), lambda n: (0, rn(n), 0, 0)),
            pl.BlockSpec((U * C, H * LANE), lambda n: (rn(n), 1)),
            pl.BlockSpec(memory_space=pl.ANY),
        ],
        out_specs=[
            pl.BlockSpec((U * C, 3 * H * LANE), lambda n: (rn(n), 0)),
            pl.BlockSpec((U * C, (H + 2) * LANE), lambda n: (rn(n), CB_Z // (H + 2))),
            pl.BlockSpec((H, 1, 1), lambda n: (0, 0, 0)),
            pl.BlockSpec((H, 1, 1), lambda n: (0, 0, 0)),
            pl.BlockSpec((1, LANE), lambda n: (0, 0)),
        ],
        out_shape=[jax.ShapeDtypeStruct((T, 3 * H * Dh), f32), jax.ShapeDtypeStruct(dproj.shape, dproj.dtype),
                   jax.ShapeDtypeStruct((H, 1, 1), f32), jax.ShapeDtypeStruct((H, 1, 1), f32),
                   jax.ShapeDtypeStruct((1, LANE), f32)],
        input_output_aliases={8 + H: 1},
        scratch_shapes=[pltpu.VMEM((H, Dh, Dh), f32)],
        compiler_params=_cparams("arbitrary"),
    )(c, *([proj] * H), proj, a_log, dt_bias, norm_w, s_saved, t_saved, d_oab, dproj)


def _gate_matmuls(xc, wa_ref, wx_ref):
    bw = wa_ref.shape[-1]
    xb = xc.astype(bf16)
    blocks = [xb[:, bw * h:bw * (h + 1)] for h in range(LRU_BLOCKS)]
    return (jnp.concatenate([_dot(blocks[h], wa_ref[h], NN) for h in range(LRU_BLOCKS)], axis=1),
            jnp.concatenate([_dot(blocks[h], wx_ref[h], NN) for h in range(LRU_BLOCKS)], axis=1))


def _gates_fwd(name, xc, w_a, w_x, pars, tm=512):
    T, Wd = xc.shape
    tm = min(tm, T)

    def body(x_ref, wa_ref, wx_ref, ba, bx, lam, a_ref, b_ref):
        x = x_ref[...]
        pr, pi = _gate_matmuls(x, wa_ref, wx_ref)
        a_ref[...], b_ref[...] = _rglru_pre_fn(pr, pi, x, ba[...], bx[...], lam[...])

    row = pl.BlockSpec((tm, Wd), lambda i: (i, 0))
    return pl.pallas_call(
        body, name=name, grid=(T // tm,),
        in_specs=[row, _whole_spec(w_a), _whole_spec(w_x)] + [_whole_spec(p) for p in pars],
        out_specs=[row, row], out_shape=[jax.ShapeDtypeStruct((T, Wd), f32)] * 2,
        compiler_params=_cparams("parallel"),
    )(xc, w_a, w_x, *pars)


def _gates_bwd(name, xc, w_a, w_x, pars, lam_t, h_prev, tm=512):
    T, Wd = xc.shape
    tm = min(tm, T)
    bw = Wd // LRU_BLOCKS

    def body(x_ref, wa_ref, wx_ref, ba, bx, lam, lt_ref, hp_ref, dx_ref, dr_ref, di_ref, dba, dbx, dlam):
        x = x_ref[...]
        pr, pi = _gate_matmuls(x, wa_ref, wx_ref)
        _, vjp = jax.vjp(_rglru_pre_fn, pr, pi, x, ba[...], bx[...], lam[...])
        lt = lt_ref[...]
        dpr, dpi, dxc, g_ba, g_bx, g_lam = vjp((lt * hp_ref[...], lt))
        dprb, dpib = dpr.astype(bf16), dpi.astype(bf16)
        dx_ref[...] = dxc + jnp.concatenate(
            [_dot(dprb[:, bw * h:bw * (h + 1)], wa_ref[h], NT) + _dot(dpib[:, bw * h:bw * (h + 1)], wx_ref[h], NT)
             for h in range(LRU_BLOCKS)], axis=1)
        dr_ref[...] = dprb
        di_ref[...] = dpib

        @pl.when(pl.program_id(0) == 0)
        def _():
            dba[...] = jnp.zeros_like(dba)
            dbx[...] = jnp.zeros_like(dbx)
            dlam[...] = jnp.zeros_like(dlam)

        dba[...] += g_ba
        dbx[...] += g_bx
        dlam[...] += g_lam

    row = pl.BlockSpec((tm, Wd), lambda i: (i, 0))
    vec = pl.BlockSpec((1, Wd), lambda i: (0, 0))
    return pl.pallas_call(
        body, name=name, grid=(T // tm,),
        in_specs=[row, _whole_spec(w_a), _whole_spec(w_x)] + [_whole_spec(p) for p in pars] + [row, row],
        out_specs=[row, row, row, vec, vec, vec],
        out_shape=[jax.ShapeDtypeStruct((T, Wd), f32), jax.ShapeDtypeStruct((T, Wd), bf16),
                   jax.ShapeDtypeStruct((T, Wd), bf16)] + [jax.ShapeDtypeStruct((1, Wd), f32)] * 3,
        compiler_params=_cparams("arbitrary"),
    )(xc, w_a, w_x, *pars, lam_t, h_prev)


def _blockdiag_bwd_dw(name, xc, dpr, dpi, tk=512):
    T, Wd = xc.shape
    bw = Wd // LRU_BLOCKS
    tk = min(tk, T)

    def body(x_ref, dr, di, oa, ox):
        @pl.when(pl.program_id(1) == 0)
        def _():
            oa[...] = jnp.zeros_like(oa)
            ox[...] = jnp.zeros_like(ox)

        xb = x_ref[...].astype(bf16)
        oa[...] += _dot(xb, dr[...].astype(bf16), TN)
        ox[...] += _dot(xb, di[...].astype(bf16), TN)

    xs = pl.BlockSpec((tk, bw), lambda h, k: (k, h))
    ws = pl.BlockSpec((None, bw, bw), lambda h, k: (h, 0, 0))
    return pl.pallas_call(
        body, name=name, grid=(LRU_BLOCKS, T // tk), in_specs=[xs, xs, xs], out_specs=[ws, ws],
        out_shape=[jax.ShapeDtypeStruct((LRU_BLOCKS, bw, bw), f32)] * 2,
        compiler_params=_cparams("parallel", "arbitrary"),
    )(xc, dpr, dpi)


def _scan(name, a, proj, reverse, b=None, h=None, dhg=None, tt=512, cb=512):
    T, Wd = a.shape
    tt, cb = min(tt, T), min(cb, Wd)
    nt = T // tt
    ng = tt // SUBLANE

    def body(a_ref, g_ref, *rest):
        n_in = 2 if reverse else 1
        ins, outs, (carry, carry_a) = rest[:n_in], rest[n_in:-2], rest[-2:]

        @pl.when(pl.program_id(1) == 0)
        def _():
            carry[...] = jnp.zeros_like(carry)
            carry_a[...] = jnp.zeros_like(carry_a)

        row = lax.broadcasted_iota(jnp.int32, (SUBLANE, cb), 0)

        def group(g, c):
            hp, ap = c
            rows = pl.ds(pl.multiple_of(g * SUBLANE, SUBLANE), SUBLANE)
            A = a_ref[rows, :]
            gate = g_ref[rows, :]
            a_first = jnp.broadcast_to(A[0:1, :], (SUBLANE, cb))
            if reverse:
                _, vjp = jax.vjp(_rec_gate_fn, ins[0][rows, :], gate)
                B, narrow = vjp((ins[1][rows, :],))
                A = jnp.where(row == SUBLANE - 1, ap, pltpu.roll(A, SUBLANE - 1, axis=0))
            else:
                B = ins[0][rows, :]
            for s in (1, 2, 4):
                sh = (SUBLANE - s) if reverse else s
                As = pltpu.roll(A, sh, axis=0)
                Bs = pltpu.roll(B, sh, axis=0)
                valid = (row < SUBLANE - s) if reverse else (row >= s)
                B = jnp.where(valid, A * Bs + B, B)
                A = jnp.where(valid, A * As, A)
            hcur = A * hp + B
            outs[0][rows, :] = hcur
            if not reverse:
                outs[1][rows, :] = jnp.where(row == 0, hp, pltpu.roll(hcur, 1, axis=0))
                narrow = _rec_gate_fn(hcur, gate)[0]
            edge = hcur[0:1, :] if reverse else hcur[SUBLANE - 1:SUBLANE, :]
            return (jnp.broadcast_to(edge, (SUBLANE, cb)), a_first), narrow

        def pair(pi, c):
            p = (ng // 2 - 1 - pi) if reverse else pi
            c, first = group(2 * p + (1 if reverse else 0), c)
            c, second = group(2 * p + (0 if reverse else 1), c)
            lo, hi = (second, first) if reverse else (first, second)
            rows = pl.ds(pl.multiple_of(p * 2 * SUBLANE, 2 * SUBLANE), 2 * SUBLANE)
            outs[-1][rows, :] = jnp.concatenate([lo, hi], axis=0).astype(bf16)
            return c

        carry[...], carry_a[...] = lax.fori_loop(0, ng // 2, pair, (carry[...], carry_a[...]))

    nc = Wd // cb
    tok = (lambda i: nt - 1 - i) if reverse else (lambda i: i)
    spec = pl.BlockSpec((tt, cb), lambda c, i: (tok(i), c))
    gate_half = pl.BlockSpec((tt, cb), lambda c, i: (tok(i), nc + c))
    if reverse:
        args, out_specs = (a, proj, h, dhg), [spec, gate_half]
        out_shape = [jax.ShapeDtypeStruct((T, Wd), f32), jax.ShapeDtypeStruct((T, 2 * Wd), bf16)]
    else:
        args, out_specs = (a, proj, b), [spec] * 3
        out_shape = [jax.ShapeDtypeStruct((T, Wd), f32)] * 2 + [jax.ShapeDtypeStruct((T, Wd), bf16)]
    return pl.pallas_call(
        body, name=name, grid=(nc, nt), in_specs=[spec, gate_half] + [spec] * (len(args) - 2), out_specs=out_specs,
        out_shape=out_shape,
        scratch_shapes=[pltpu.VMEM((SUBLANE, cb), f32), pltpu.VMEM((SUBLANE, cb), f32)],
        compiler_params=_cparams("parallel", "arbitrary"),
    )(*args)


def _relu2_epilogue(r):
    h = jnp.maximum(r, 0.0)
    return r, h * h


def _drelu2_epilogue(r, a):
    return (r * (2.0 * jnp.maximum(a.astype(f32), 0.0)),)


def _residual_cot(through, upper):
    return (through + DN_ALPHA * upper,)


def _merge_cols(name, g, tm=256):
    _, L, R, s = g.shape

    def body(g_ref, o_ref):
        for d in range(N_DEV):
            o_ref[:, s * d:s * (d + 1)] = g_ref[d].astype(bf16)
        o_ref[:, N_DEV * s:] = jnp.zeros((tm, HYB_PROJ_PAD - N_DEV * s), bf16)

    return pl.pallas_call(
        body, name=name, grid=(L, R // tm),
        in_specs=[pl.BlockSpec((N_DEV, None, tm, s), lambda l, i: (0, l, i, 0))],
        out_specs=pl.BlockSpec((None, tm, HYB_PROJ_PAD), lambda l, i: (l, i, 0)),
        out_shape=jax.ShapeDtypeStruct((L, R, HYB_PROJ_PAD), bf16),
        compiler_params=_cparams("parallel", "parallel"),
    )(g)


def _split_cols(name, dw, tm=256):
    R = dw.shape[0]
    s = HYB_PROJ // N_DEV

    def body(g_ref, o_ref):
        for d in range(N_DEV):
            o_ref[d] = g_ref[:, s * d:s * (d + 1)].astype(bf16)

    return pl.pallas_call(
        body, name=name, grid=(R // tm,),
        in_specs=[pl.BlockSpec((tm, HYB_PROJ_PAD), lambda i: (i, 0))],
        out_specs=pl.BlockSpec((N_DEV, tm, s), lambda i: (0, i, 0)),
        out_shape=jax.ShapeDtypeStruct((N_DEV, R, s), bf16),
        compiler_params=_cparams("parallel"),
    )(dw)


def _rows_to_dev(dw):
    nb, r, c = dw.shape
    t = dw.reshape(nb, N_DEV, r // N_DEV, c)
    return jnp.moveaxis(t, 1, 0).reshape(N_DEV, nb * (r // N_DEV), c).astype(bf16)


def _ln_epilogue(r, x, g, b):
    y = _ln_res_fn(x, r, g, b)[0]
    return r, y, y


def _hybrid_fwd(tag, x, xb, W, j, cos, sin, ln):
    proj = _mm(f"{tag}_proj", xb, W["hyb_w_in"][j], "nn", b_kind="lead", b_lead=0)
    o_a = _attn_fwd(f"{tag}_attn", proj, cos, sin, W["hyb_sinks"][j][None, :])
    c = _conv_fwd(f"{tag}_conv", proj, CB_CONV, 12, W["hyb_conv_w"][j], None)
    o_ab, s_saved, t_saved = _delta_fwd(f"{tag}_delta", c, proj, W["hyb_a_log"][j].reshape(B_HEADS, 1, 1),
                                        W["hyb_dt_bias"][j].reshape(B_HEADS, 1, 1), W["hyb_norm_w"][j][None, :], o_a)
    mix, x1, x1b = _mm(f"{tag}_out", o_ab, W["hyb_w_out"][j], "nn", b_kind="lead", b_lead=0, epilogue=_ln_epilogue,
                       extras=(x,), params=ln, out_dtypes=(f32, f32, bf16))
    return mix, x1, x1b, (proj, c, s_saved, t_saved, o_ab)


def _hybrid_bwd(tag, x, dmix, addend, W, j, cos, sin, saved, G, send_early):
    proj, c, s_saved, t_saved, o_ab = saved
    T = x.shape[0]
    d_oab = _mm(f"{tag}_dout", dmix, W["hyb_w_out"][j], "nt", b_kind="lead", b_lead=0)
    G["hyb_w_out"][j] = _mm(f"{tag}_dwout", o_ab, dmix, "tn", out_dtypes=(bf16,)).reshape(N_DEV, -1, D_MODEL)
    sinks = W["hyb_sinks"][j][None, :] + send_early({("hyb_w_out", j): G["hyb_w_out"][j]})
    dproj, dsinks = _attn_bwd(f"{tag}_dattn", proj, cos, sin, sinks, d_oab)
    a_log = W["hyb_a_log"][j].reshape(B_HEADS, 1, 1)
    dt_bias = W["hyb_dt_bias"][j].reshape(B_HEADS, 1, 1)
    dc, dproj, dal, ddt, dnw = _delta_bwd(f"{tag}_ddelta", c, proj, a_log, dt_bias, W["hyb_norm_w"][j][None, :],
                                          s_saved, t_saved, d_oab, dproj)
    dproj, dconv_w, _ = _conv_bwd(f"{tag}_dconv", dc, proj, CB_CONV, 12, W["hyb_conv_w"][j], dproj, CB_CONV)
    dx = _mm(f"{tag}_dx", dproj, W["hyb_w_in"][j], "nt", b_kind="lead", b_lead=0,
             **({} if addend is None else dict(epilogue=_residual_cot, extras=(addend,))))
    G["hyb_w_in"][j] = _split_cols(f"{tag}_dwin_split", _mm(f"{tag}_dwin", x, dproj, "tn", tn=1536))
    G["hyb_sinks"][j] = dsinks[0]
    G["hyb_conv_w"][j] = dconv_w
    G["hyb_a_log"][j] = dal.reshape(B_HEADS)
    G["hyb_dt_bias"][j] = ddt.reshape(B_HEADS)
    G["hyb_norm_w"][j] = dnw[0]
    return dx


def _rec_fwd(tag, x, xb, W, j, ln):
    Wd = D_MODEL
    proj = _mm(f"{tag}_proj", xb, W["rec_w_in"][j], "nn", b_kind="devcol", b_lead=0, tm=2048)
    xc = _conv_fwd(f"{tag}_conv", proj, 0, Wd // LANE, W["rec_conv_w"][j], W["rec_conv_b"][j][None, :])
    pars = [W["rec_b_a"][j][None, :], W["rec_b_x"][j][None, :], W["rec_lambda"][j][None, :]]
    a, b = _gates_fwd(f"{tag}_gates", xc, W["rec_w_a"][j][0], W["rec_w_x"][j][0], pars)
    h, h_prev, hg = _scan(f"{tag}_scan", a, proj, False, b=b)
    mix, x1, x1b = _mm(f"{tag}_out", hg, W["rec_w_out"][j], "nn", b_kind="lead", b_lead=0, epilogue=_ln_epilogue,
                       extras=(x,), params=ln, out_dtypes=(f32, f32, bf16))
    return mix, x1, x1b, (proj, xc, a, h, h_prev, hg)


def _rec_bwd(tag, x, dmix, addend, W, j, saved, G, send_early):
    proj, xc, a, h, h_prev, hg = saved
    Wd = D_MODEL
    dhg = _mm(f"{tag}_dout", dmix, W["rec_w_out"][j], "nt", b_kind="lead", b_lead=0)
    G["rec_w_out"][j] = _mm(f"{tag}_dwout", hg, dmix, "tn", out_dtypes=(bf16,)).reshape(N_DEV, -1, D_MODEL)
    sent = send_early({("rec_w_out", j): G["rec_w_out"][j]})
    lam_t, dproj = _scan(f"{tag}_dscan", a, proj, True, h=h, dhg=dhg)
    pars = [W["rec_b_a"][j][None, :] + sent, W["rec_b_x"][j][None, :], W["rec_lambda"][j][None, :]]
    dxc, dpr, dpi, db_a, db_x, dlam = _gates_bwd(f"{tag}_dgates", xc, W["rec_w_a"][j][0], W["rec_w_x"][j][0], pars,
                                                 lam_t, h_prev)
    dwa, dwx = _blockdiag_bwd_dw(f"{tag}_dgates_dw", xc, dpr, dpi)
    G["rec_w_a"][j], G["rec_w_x"][j] = _rows_to_dev(dwa), _rows_to_dev(dwx)
    dproj, dconv_w, dconv_b = _conv_bwd(f"{tag}_dconv", dxc, proj, 0, Wd // LANE, W["rec_conv_w"][j], dproj, 0)
    dx = _mm(f"{tag}_dx", dproj, W["rec_w_in"][j], "nt", b_kind="devcol", b_lead=0,
             **({} if addend is None else dict(epilogue=_residual_cot, extras=(addend,))))
    G["rec_w_in"][j] = _mm(f"{tag}_dwin", x, dproj, "tn", o_kind="devcol", out_dtypes=(bf16,), tn=2048)
    G["rec_conv_w"][j] = dconv_w
    G["rec_conv_b"][j] = dconv_b
    G["rec_b_a"][j] = db_a[0]
    G["rec_b_x"][j] = db_x[0]
    G["rec_lambda"][j] = dlam[0]
    return dx


def _local_step(x, target, W, load_layer, grads_ready):
    T = x.shape[0]
    cos, sin = _rope_tables(T)
    saved = []
    xb = x
    for layer in range(DEPTH):
        j = layer // 2
        tag = f"L{layer}"
        load_layer(layer, "mixer", x)
        ln1 = (W["ln1_g"][layer][None, :], W["ln1_b"][layer][None, :])
        if layer % 2 == 0:
            mix, x1, x1b, sv = _hybrid_fwd(tag, x, xb, W, j, cos, sin, ln1)
        else:
            mix, x1, x1b, sv = _rec_fwd(tag, x, xb, W, j, ln1)
        load_layer(layer, "mlp", x1)
        a, h2 = _mm(f"{tag}_mlp1", x1b, W["mlp_w1"][layer], "nn", b_kind="devcol", b_lead=0, epilogue=_relu2_epilogue,
                    out_dtypes=(bf16, bf16), tm=2048)
        ln2 = (W["ln2_g"][layer][None, :], W["ln2_b"][layer][None, :])
        y, x2, x2b = _mm(f"{tag}_mlp2", h2, W["mlp_w2"][layer], "nn", b_kind="lead", b_lead=0, epilogue=_ln_epilogue,
                         extras=(x1,), params=ln2, out_dtypes=(f32, f32, bf16))
        saved.append((x, xb, sv, mix, x1, x1b, a, h2, y))
        x, xb = x2, x2b
    loss, dx = _loss_head(x, target)

    G = {k: [None] * (DEPTH if k.startswith(("ln", "mlp")) else DEPTH // 2) for k in (
        "hyb_w_in", "hyb_sinks", "hyb_conv_w", "hyb_a_log", "hyb_dt_bias", "hyb_norm_w", "hyb_w_out",
        "rec_w_in", "rec_conv_w", "rec_conv_b", "rec_w_a", "rec_b_a", "rec_w_x", "rec_b_x", "rec_lambda", "rec_w_out",
        "ln1_g", "ln1_b", "mlp_w1", "mlp_w2", "ln2_g", "ln2_b")}
    held = {}
    cot_rows, cot_fn = [(dx, 0, D_MODEL)], None
    for layer in reversed(range(DEPTH)):
        j = layer // 2
        tag = f"L{layer}"
        x0, x0b, sv, mix, x1, x1b, a, h2, y = saved[layer]
        ln2 = [W["ln2_g"][layer][None, :], W["ln2_b"][layer][None, :]]
        (dy, dyb), (dg2, db2) = _tl_bwd(f"{tag}_dln2", _ln_res_fn, [(x1, 0, D_MODEL), (y, 0, D_MODEL)], ln2,
                                        cot_rows, cot_fn=cot_fn, skip=(0,), bf16_copy=True)
        G["ln2_g"][layer], G["ln2_b"][layer] = dg2[0], db2[0]
        da = _mm(f"{tag}_dmlp2", dyb, W["mlp_w2"][layer], "nt", b_kind="lead", b_lead=0, epilogue=_drelu2_epilogue,
                 extras=(a,), out_dtypes=(bf16,), tm=2048, tn=512)
        G["mlp_w2"][layer] = _mm(f"{tag}_dw2", h2, dyb, "tn", out_dtypes=(bf16,), tm=2048).reshape(N_DEV, -1, D_MODEL)
        dx1 = _mm(f"{tag}_dmlp1", da, W["mlp_w1"][layer], "nt", b_kind="devcol", b_lead=0, tm=2048)
        G["mlp_w1"][layer] = _mm(f"{tag}_dw1", x1b, da, "tn", o_kind="devcol", out_dtypes=(bf16,), tn=2048)
        ln1 = [W["ln1_g"][layer][None, :], W["ln1_b"][layer][None, :]]
        (dmix, dmixb), (dg1, db1) = _tl_bwd(f"{tag}_dln1", _ln_res_fn, [(x0, 0, D_MODEL), (mix, 0, D_MODEL)], ln1,
                                            [(dx1, 0, D_MODEL), (dy, 0, D_MODEL)], cot_fn=_residual_cot, skip=(0,),
                                            bf16_copy=True)
        G["ln1_g"][layer], G["ln1_b"][layer] = dg1[0], db1[0]
        dx0_a = dmix if layer == 0 else None
        held.update({(k, layer): G[k][layer] for k in ("mlp_w1", "mlp_w2")})
        early = functools.partial(grads_ready, f"l{layer}_early", held)
        if layer % 2 == 0:
            dx = _hybrid_bwd(tag, x0b, dmixb, dx0_a, W, j, cos, sin, sv, G, early)
        else:
            dx = _rec_bwd(tag, x0b, dmixb, dx0_a, W, j, sv, G, early)
        held = {(k, i): G[k][i] for k, i in _layer_weights(layer)[:-2] if not k.endswith("w_out")}
        cot_rows, cot_fn = [(dx, 0, D_MODEL), (dmix, 0, D_MODEL)], _residual_cot
    grads_ready("l0_late", held, {})
    big = {k for k, _ in BIG}
    return loss, dx, {k: jnp.stack(v) for k, v in G.items() if k not in big}


def _layer_weights(layer):
    j = layer // 2
    mixer = ["hyb_w_in", "hyb_w_out"] if layer % 2 == 0 else ["rec_w_in", "rec_w_out", "rec_w_a", "rec_w_x"]
    return [(k, j) for k in mixer] + [("mlp_w1", layer), ("mlp_w2", layer)]


def _my_coords():
    return lax.axis_index("x"), lax.axis_index("y"), lax.axis_index("c")


def _all_gather(name, arrays):
    na = len(arrays)

    def body(*refs):
        x_refs, out_refs = refs[:na], refs[na:2 * na]
        send_sems, recv_sems, local_sems = refs[2 * na:]
        x, y, c = _my_coords()
        me, sibling = (x, y, c), (x, y, 1 - c)
        chips = [(1 - x, y), (x, 1 - y), (1 - x, 1 - y)]

        def blk(a, px, py, pc):
            return out_refs[a].at[4 * px + 2 * py + pc]

        def copy(a, k, block, to, src=None):
            return pltpu.make_async_remote_copy(
                src_ref=blk(a, *block) if src is None else src, dst_ref=blk(a, *block),
                send_sem=send_sems.at[a, k], recv_sem=recv_sems.at[a, k],
                device_id=to, device_id_type=pl.DeviceIdType.MESH)

        mine = [pltpu.make_async_copy(x_refs[a], blk(a, *me), local_sems.at[a]) for a in range(na)]
        for cp in mine:
            cp.start()
        first = []
        for a in range(na):
            first.append(copy(a, 0, me, sibling, src=x_refs[a]))
            first += [copy(a, 1 + j, me, (*chip, c), src=x_refs[a]) for j, chip in enumerate(chips)]
        for cp in first:
            cp.start()
        passed = []
        for a in range(na):
            for j, chip in enumerate(chips):
                copy(a, 1 + j, (*chip, c), me).wait_recv()
                passed.append(copy(a, 4 + j, (*chip, c), sibling))
                passed[-1].start()
        for a in range(na):
            copy(a, 0, sibling, me).wait_recv()
            for j, chip in enumerate(chips):
                copy(a, 4 + j, (*chip, 1 - c), me).wait_recv()
        for cp in first + passed:
            cp.wait_send()
        for cp in mine:
            cp.wait()

    return pl.pallas_call(
        body, name=name,
        out_shape=[jax.ShapeDtypeStruct((N_DEV,) + a.shape, a.dtype) for a in arrays],
        in_specs=[pl.BlockSpec(memory_space=pl.ANY)] * na,
        out_specs=[pl.BlockSpec(memory_space=pl.ANY)] * na,
        scratch_shapes=[pltpu.SemaphoreType.DMA((na, 7)), pltpu.SemaphoreType.DMA((na, 7)),
                        pltpu.SemaphoreType.DMA((na,))],
    )(*arrays)


_HBM = pl.BlockSpec(memory_space=pltpu.HBM)
_SEM = pl.BlockSpec(memory_space=pltpu.SEMAPHORE)


def _flip(k, x, y, c):
    return ((1 - x) if k & 4 else x, (1 - y) if k & 2 else y, (1 - c) if k & 1 else c)


_PEERS = {"gather": (1, 2, 4, 6), "scatter": (1, 2, 3, 4, 5, 6, 7)}


def _push_copies(kind, x_refs, land_refs, send_sems, recv_sems, local_sems):
    x, y, c = _my_coords()
    me = 4 * x + 2 * y + c
    peers = _PEERS[kind]
    remote, local = [], []
    for a in range(len(x_refs)):
        local.append(pltpu.make_async_copy(x_refs[a] if kind == "gather" else x_refs[a].at[me], land_refs[a].at[me],
                                           local_sems.at[a]))
        for n, k in enumerate(peers):
            px, py, pc = _flip(k, x, y, c)
            remote.append(pltpu.make_async_remote_copy(
                src_ref=x_refs[a] if kind == "gather" else x_refs[a].at[4 * px + 2 * py + pc],
                dst_ref=land_refs[a].at[me],
                send_sem=send_sems.at[a * len(peers) + n], recv_sem=recv_sems.at[a * len(peers) + n],
                device_id=(px, py, pc), device_id_type=pl.DeviceIdType.MESH))
    return remote, local


def _pass_to_sibling(name, lands):
    na = len(lands)
    chips = (2, 4, 6)

    def body(*refs):
        out_refs, send_sems, recv_sems = refs[na:2 * na], refs[2 * na], refs[2 * na + 1]
        x, y, c = _my_coords()
        cps = []
        for a in range(na):
            for n, k in enumerate(chips):
                px, py, _ = _flip(k, x, y, c)
                cps.append(pltpu.make_async_remote_copy(
                    src_ref=out_refs[a].at[4 * px + 2 * py + c], dst_ref=out_refs[a].at[4 * px + 2 * py + c],
                    send_sem=send_sems.at[a * 3 + n], recv_sem=recv_sems.at[a * 3 + n],
                    device_id=(x, y, 1 - c), device_id_type=pl.DeviceIdType.MESH))
        for cp in cps:
            cp.start()
        for a in range(na):
            for n, k in enumerate(chips):
                px, py, _ = _flip(k, x, y, c)
                blk = out_refs[a].at[4 * px + 2 * py + (1 - c)]
                pltpu.make_async_remote_copy(src_ref=blk, dst_ref=blk, send_sem=send_sems.at[a * 3 + n],
                                             recv_sem=recv_sems.at[a * 3 + n], device_id=(x, y, 1 - c),
                                             device_id_type=pl.DeviceIdType.MESH).wait_recv()
        for cp in cps:
            cp.wait_send()

    return pl.pallas_call(
        body, name=name,
        out_shape=[jax.ShapeDtypeStruct(l.shape, l.dtype) for l in lands],
        in_specs=[pl.BlockSpec(memory_space=pl.ANY)] * na,
        out_specs=[pl.BlockSpec(memory_space=pl.ANY)] * na,
        input_output_aliases={a: a for a in range(na)},
        scratch_shapes=[pltpu.SemaphoreType.DMA((3 * na,)), pltpu.SemaphoreType.DMA((3 * na,))],
    )(*lands)


_SIDE_EFFECT = pltpu.CompilerParams(has_side_effects=pltpu.SideEffectType.DATAFLOW_SIDE_EFFECTING)


def _push_start(name, kind, srcs, lands):
    na = len(srcs)

    def body(*refs):
        remote, local = _push_copies(kind, refs[:na], refs[na:2 * na], *refs[2 * na:2 * na + 3])
        for cp in remote + local:
            cp.start()
        token = refs[-1]
        token[...] = jnp.zeros_like(token)

    arrays = list(srcs) + list(lands)
    n_remote = na * len(_PEERS[kind])
    res = pl.pallas_call(
        body, name=name,
        out_shape=(pltpu.SemaphoreType.DMA((n_remote,)), pltpu.SemaphoreType.DMA((n_remote,)),
                   pltpu.SemaphoreType.DMA((na,)), *[pltpu.HBM(t.shape, t.dtype) for t in arrays],
                   jax.ShapeDtypeStruct((SUBLANE, LANE), f32)),
        in_specs=[_HBM] * (2 * na),
        out_specs=(_SEM, _SEM, _SEM, *[_HBM] * (2 * na), pl.BlockSpec(memory_space=pltpu.VMEM)),
        input_output_aliases={i: 3 + i for i in range(2 * na)},
        compiler_params=_SIDE_EFFECT,
    )(*[pltpu.with_memory_space_constraint(t, pltpu.HBM) for t in arrays])
    return list(res[:3]), res[3:3 + na], res[3 + na:3 + 2 * na], res[-1][:1, :1]


def _push_wait(name, kind, sems, srcs, lands, after):
    na = len(srcs)

    def body(*refs):
        remote, local = _push_copies(kind, refs[:na], refs[na:2 * na], *refs[2 * na:2 * na + 3])
        for cp in remote:
            cp.wait_send()
            cp.wait_recv()
        for cp in local:
            cp.wait()

    arrays = list(srcs) + list(lands)
    res = pl.pallas_call(
        body, name=name,
        out_shape=tuple(pltpu.HBM(t.shape, t.dtype) for t in arrays),
        in_specs=[_HBM] * (2 * na) + [_SEM] * 3 + [pl.BlockSpec(memory_space=pl.ANY)],
        out_specs=tuple([_HBM] * (2 * na)),
        input_output_aliases={i: i for i in range(2 * na)},
        compiler_params=_SIDE_EFFECT,
    )(*arrays, *sems, after)
    return res[na:]


def _sum_blocks(name, land):
    _, R, n = land.shape
    tr = R

    def body(l_ref, o_ref):
        acc = l_ref[0].astype(f32)
        for s in range(1, N_DEV):
            acc = acc + l_ref[s].astype(f32)
        o_ref[...] = acc

    return pl.pallas_call(
        body, name=name, grid=(R // tr,),
        in_specs=[pl.BlockSpec((N_DEV, tr, n), lambda i: (0, i, 0))],
        out_specs=pl.BlockSpec((tr, n), lambda i: (i, 0)),
        out_shape=jax.ShapeDtypeStruct((R, n), f32),
        compiler_params=_cparams("parallel"),
    )(land)


def _adamw(name, w, g, m, v):
    shape = w.shape
    last = shape[-1]
    rows = math.prod(shape[:-1])
    tm = 256 if rows % 256 == 0 and rows > 256 else rows
    w2, g2, m2, v2 = (t.reshape(rows, last) for t in (w, g, m, v))

    def body(w_ref, g_ref, m_ref, v_ref, d_ref, mo_ref, vo_ref):
        gg = g_ref[...]
        mn = ADAM_B1 * m_ref[...] + (1.0 - ADAM_B1) * gg
        vn = ADAM_B2 * v_ref[...] + (1.0 - ADAM_B2) * jnp.square(gg)
        m_hat = mn / (1.0 - ADAM_B1 ** ADAM_STEP)
        v_hat = vn / (1.0 - ADAM_B2 ** ADAM_STEP)
        d_ref[...] = -ADAM_LR * (m_hat / (jnp.sqrt(v_hat) + ADAM_EPS) + ADAM_WD * w_ref[...])
        mo_ref[...] = mn
        vo_ref[...] = vn

    spec = pl.BlockSpec((tm, last), lambda i: (i, 0))
    d, mn, vn = pl.pallas_call(
        body, name=name, grid=(rows // tm,), in_specs=[spec] * 4, out_specs=[spec] * 3,
        out_shape=[jax.ShapeDtypeStruct((rows, last), f32)] * 3,
        compiler_params=_cparams("parallel"),
    )(w2, g2, m2, v2)
    return d.reshape(shape), mn.reshape(shape), vn.reshape(shape)


def _adamw_land(name, lands, w, m, v, tm=256):
    L = len(lands)
    _, R, C = lands[0].shape
    tm = min(tm, R)

    def body(*refs):
        l_refs, (w_ref, m_ref, v_ref, g_ref, d_ref, mo_ref, vo_ref) = refs[:L], refs[L:]
        for k in range(L):
            @pl.when(pl.program_id(0) == k)
            def _(k=k):
                gg = l_refs[k][0].astype(f32)
                for s in range(1, N_DEV):
                    gg = gg + l_refs[k][s].astype(f32)
                g_ref[...] = gg
                mn = ADAM_B1 * m_ref[...] + (1.0 - ADAM_B1) * gg
                vn = ADAM_B2 * v_ref[...] + (1.0 - ADAM_B2) * jnp.square(gg)
                m_hat = mn / (1.0 - ADAM_B1 ** ADAM_STEP)
                v_hat = vn / (1.0 - ADAM_B2 ** ADAM_STEP)
                d_ref[...] = -ADAM_LR * (m_hat / (jnp.sqrt(v_hat) + ADAM_EPS) + ADAM_WD * w_ref[...])
                mo_ref[...] = mn
                vo_ref[...] = vn

    land_specs = [pl.BlockSpec((N_DEV, tm, C), lambda l, i, k=k: (0, jnp.where(l == k, i, 0), 0)) for k in range(L)]
    spec = pl.BlockSpec((None, tm, C), lambda l, i: (l, i, 0))
    return pl.pallas_call(
        body, name=name, grid=(L, R // tm),
        in_specs=land_specs + [spec] * 3,
        out_specs=[spec] * 4,
        out_shape=[jax.ShapeDtypeStruct((L, R, C), f32)] * 4,
        compiler_params=_cparams("arbitrary", "arbitrary"),
    )(*lands, w, m, v)


BIG = [("hyb_w_in", 2), ("hyb_w_out", 1), ("rec_w_in", 2), ("rec_w_out", 1), ("rec_w_a", 2), ("rec_w_x", 2),
       ("mlp_w1", 2), ("mlp_w2", 1)]
SMALL = [("hyb_conv_w", 2), ("rec_conv_w", 2), ("rec_conv_b", 1), ("rec_b_a", 1), ("rec_b_x", 1), ("rec_lambda", 1)]
REPL = ["hyb_sinks", "hyb_a_log", "hyb_dt_bias", "hyb_norm_w", "ln1_g", "ln1_b", "ln2_g", "ln2_b"]
WEIGHTS = ["hyb_w_in", "hyb_sinks", "hyb_conv_w", "hyb_a_log", "hyb_dt_bias", "hyb_norm_w", "hyb_w_out", "rec_w_in",
           "rec_conv_w", "rec_conv_b", "rec_w_a", "rec_b_a", "rec_w_x", "rec_b_x", "rec_lambda", "rec_w_out",
           "ln1_g", "ln1_b", "mlp_w1", "mlp_w2", "ln2_g", "ln2_b"]


def _pack_rows(parts, dtype, row_mult):
    lead = parts[0].shape[:-1]
    flat = jnp.concatenate([p.astype(dtype) for p in parts], axis=-1)
    n = flat.shape[-1]
    unit = row_mult * LANE
    pad = (-n) % unit
    if pad:
        flat = jnp.concatenate([flat, jnp.zeros(lead + (pad,), dtype)], axis=-1)
    return flat.reshape(lead + ((n + pad) // LANE, LANE))


def _gather_full(gathered, shard_shapes, table):
    flat = gathered.reshape(N_DEV, -1)
    out, off = {}, 0
    for name, ax in table:
        shp = shard_shapes[name]
        n = math.prod(shp)
        arr = flat[:, off:off + n].reshape((N_DEV,) + shp)
        off += n
        arr = jnp.moveaxis(arr, 0, ax)
        out[name] = arr.reshape(shp[:ax] + (N_DEV * shp[ax],) + shp[ax + 1:])
    return out


def _matmul_layouts(tag, gw):
    out = {}
    bw = D_MODEL // LRU_BLOCKS
    for k, g in gw.items():
        L = g.shape[1]
        if k == "hyb_w_in":
            out[k] = _merge_cols(f"{tag}_w_in_merge", g)
        elif k in ("hyb_w_out", "rec_w_out", "mlp_w2"):
            out[k] = jnp.swapaxes(g, 0, 1).reshape(L, N_DEV * g.shape[2], g.shape[3])
        elif k in ("rec_w_a", "rec_w_x"):
            out[k] = jnp.moveaxis(g, 0, 2).reshape(L, LRU_BLOCKS, bw, bw)
        else:
            out[k] = g
    return out


def kernel(x, hyb_w_in, hyb_sinks, hyb_conv_w, hyb_a_log, hyb_dt_bias, hyb_norm_w, hyb_w_out, rec_w_in, rec_conv_w, rec_conv_b, rec_w_a, rec_b_a, rec_w_x, rec_b_x, rec_lambda, rec_w_out, ln1_g, ln1_b, mlp_w1, mlp_w2, ln2_g, ln2_b, loss_target, m_hyb_w_in, m_hyb_sinks, m_hyb_conv_w, m_hyb_a_log, m_hyb_dt_bias, m_hyb_norm_w, m_hyb_w_out, m_rec_w_in, m_rec_conv_w, m_rec_conv_b, m_rec_w_a, m_rec_b_a, m_rec_w_x, m_rec_b_x, m_rec_lambda, m_rec_w_out, m_ln1_g, m_ln1_b, m_mlp_w1, m_mlp_w2, m_ln2_g, m_ln2_b, v_hyb_w_in, v_hyb_sinks, v_hyb_conv_w, v_hyb_a_log, v_hyb_dt_bias, v_hyb_norm_w, v_hyb_w_out, v_rec_w_in, v_rec_conv_w, v_rec_conv_b, v_rec_w_a, v_rec_b_a, v_rec_w_x, v_rec_b_x, v_rec_lambda, v_rec_w_out, v_ln1_g, v_ln1_b, v_mlp_w1, v_mlp_w2, v_ln2_g, v_ln2_b):
    args = locals()
    w = {k: args[k] for k in WEIGHTS}
    m = {k: args["m_" + k] for k in WEIGHTS}
    v = {k: args["v_" + k] for k in WEIGHTS}
    shard_shapes = {k: tuple(t.shape) for k, t in w.items()}
    xi, yi, ci = _my_coords()
    me = 4 * xi + 2 * yi + ci

    in_flight = {}

    def install(tag, names, got):
        for (k, i), arr in zip(names, _matmul_layouts(tag, {k: g for (k, _), g in zip(names, got)}).values()):
            W[k][i] = arr

    def start_gather(tag, names):
        srcs = [w[k][i:i + 1].astype(bf16) for k, i in names]
        *pending, zero = _push_start(f"gather_{tag}_start", "gather", srcs,
                                     [lax.empty((N_DEV,) + s.shape, bf16) for s in srcs])
        in_flight[tag] = (names, pending)
        return zero

    def finish_gather(tag, after):
        names, pending = in_flight.pop(tag)
        half = _push_wait(f"gather_{tag}_wait", "gather", *pending, after)
        install(tag, names, _pass_to_sibling(f"gather_{tag}_pass", half))

    def started(k, zero):
        W[k] = W[k] + zero

    def mixer_w(layer):
        return _layer_weights(layer)[:-2]

    def mlp_w(layer):
        return _layer_weights(layer)[-2:]

    gathered0 = _all_gather("gather_first", [w[k][i:i + 1].astype(bf16) for k, i in mixer_w(0)]
                            + [_pack_rows([w[k].reshape(-1) for k, _ in SMALL], f32, SUBLANE)])
    W = _gather_full(gathered0[-1], shard_shapes, SMALL)
    W.update({k: w[k] for k in REPL})
    W.update({k: {} for k, _ in BIG})
    install("l0a", mixer_w(0), gathered0[:-1])
    started("hyb_sinks", start_gather("l0b", mlp_w(0)) + start_gather("l1a", mixer_w(1)))

    def load_layer(layer, part, after):
        if part == "mixer":
            if layer == 1:
                finish_gather("l1a", after)
            if layer >= 2:
                finish_gather(f"l{layer}", after)
            if 1 <= layer < DEPTH - 1:
                started("hyb_sinks" if layer % 2 == 0 else "rec_conv_b",
                        start_gather(f"l{layer + 1}", _layer_weights(layer + 1)))
        elif layer == 0:
            finish_gather("l0b", after)
            started("ln2_g", start_gather("l1b", mlp_w(1)))
        elif layer == 1:
            finish_gather("l1b", after)

    grads_in_flight = {}

    def grads_ready(tag, a, b):
        g = {**a, **b}
        srcs = list(g.values())
        *pending, zero = _push_start(f"scatter_{tag}_start", "scatter", srcs, [lax.empty(s.shape, bf16) for s in srcs])
        grads_in_flight[tag] = (list(g.keys()), pending)
        return zero

    loss_local, grad_x, G = _local_step(x[0], loss_target[0], W, load_layer, grads_ready)
    loss = lax.psum(loss_local, MESH_AXES)

    landed = {}

    def land(tag, after):
        keys, pending = grads_in_flight[tag]
        landed.update(zip(keys, _push_wait(f"scatter_{tag}_wait", "scatter", *pending, after)))

    tags = list(grads_in_flight)
    for tag in tags[:-1]:
        land(tag, grad_x)
    rest = _pack_rows([G[k].reshape(-1) for k, _ in SMALL] + [G[k].reshape(-1) for k in REPL], f32, SUBLANE)
    g_rest = _sum_blocks("sum_rest", _all_gather("gather_rest", [rest])[0]).reshape(-1)

    grads, delta, new_m, new_v = {}, {}, {}, {}

    def adamw_big(k):
        shp = shard_shapes[k]
        s3 = (shp[0], math.prod(shp[1:-1]), shp[-1])
        lands = [landed[(k, i)].reshape((N_DEV,) + s3[1:]) for i in range(shp[0])]
        res = _adamw_land("adamw_" + k, lands, w[k].reshape(s3), m[k].reshape(s3), v[k].reshape(s3))
        grads[k], delta[k], new_m[k], new_v[k] = (r.reshape(shp) for r in res)

    late = {k for k, _ in grads_in_flight[tags[-1]][0]}
    for k in [k for k, _ in BIG if k not in late]:
        adamw_big(k)
        done = new_v[k]
    land(tags[-1], done)
    for k in [k for k, _ in BIG if k in late]:
        adamw_big(k)
    off = 0
    for k, ax in SMALL:
        full_shape = G[k].shape
        n = math.prod(full_shape)
        full = g_rest[off:off + n].reshape(full_shape)
        off += n
        s = shard_shapes[k][ax]
        grads[k] = lax.dynamic_slice_in_dim(full, me * s, s, axis=ax)
    for k in REPL:
        n = math.prod(shard_shapes[k])
        grads[k] = g_rest[off:off + n].reshape(shard_shapes[k])
        off += n

    for k in [k for k, _ in SMALL] + REPL:
        delta[k], new_m[k], new_v[k] = _adamw("adamw_" + k, w[k], grads[k], m[k], v[k])

    return (loss, grad_x[None], *[grads[k] for k in WEIGHTS], *[delta[k] for k in WEIGHTS],
            *[new_m[k] for k in WEIGHTS], *[new_v[k] for k in WEIGHTS])
```

```python
import functools
import math

import jax
import jax.numpy as jnp
from jax import lax
from jax.experimental import pallas as pl
from jax.experimental.pallas import tpu as pltpu

f32 = jnp.float32
bf16 = jnp.bfloat16

N_DEV = 8
D_MODEL = 1024
DEPTH = 4
A_HEAD_DIM = 64
A_Q_HEADS = 8
WINDOW = 128
ROPE_THETA = 10000.0
B_HEADS = 4
B_HEAD_DIM = 128
B_CHUNK = 64
LRU_BLOCKS = 4
LRU_C = 8.0
D_FF = 4 * D_MODEL
HYB_PROJ = 2824
HYB_PROJ_PAD = 3072
DN_ALPHA = (2 * DEPTH) ** 0.25
LN_EPS = 1e-5
NORM_EPS = 1e-6
ADAM_LR = 0.001
ADAM_B1 = 0.9
ADAM_B2 = 0.999
ADAM_EPS = 1e-08
ADAM_WD = 0.01
ADAM_STEP = 10

LANE = 128
SUBLANE = 8
VMEM_LIMIT = 48 * 1024 * 1024

CB_QA, CB_KA, CB_VA, CB_CONV, CB_Z, CB_LG = 0, 4, 5, 6, 18, 22

MESH_AXES = ("x", "y", "c")


def _cparams(*sem):
    return pltpu.CompilerParams(dimension_semantics=sem, vmem_limit_bytes=VMEM_LIMIT)


def _dot(a, b, dims, precision=None):
    return lax.dot_general(a, b, (dims, ((), ())), preferred_element_type=f32, precision=precision)


NN = ((1,), (0,))
NT = ((1,), (1,))
TN = ((0,), (0,))


def _mat_spec(arr, kind, lead, br, bc, rb, cb):
    if kind == "plain":
        return pl.BlockSpec((br, bc), lambda i, j, k: (rb(i, j, k), cb(i, j, k)))
    if kind == "lead":
        return pl.BlockSpec((None, br, bc), lambda i, j, k: (lead, rb(i, j, k), cb(i, j, k)))
    assert kind == "devcol" and bc == arr.shape[-1]
    return pl.BlockSpec((None, None, br, bc), lambda i, j, k: (cb(i, j, k), lead, rb(i, j, k), 0))


def _mm(name, a, b, mode, *, b_kind="plain", b_lead=0, o_kind="plain", epilogue=None, extras=(), params=(),
        out_dtypes=(f32,), tm=1024, tn=1024, tk=None):
    if tk is None:
        tk = 512 if mode == "tn" else 1024
    if b_kind in ("plain", "lead"):
        b_rows, b_cols = b.shape[-2:]
    else:
        b_rows, b_cols = b.shape[-2], N_DEV * b.shape[-1]
    if mode == "nn":
        (M, K), (K2, N) = a.shape, (b_rows, b_cols)
    elif mode == "nt":
        (M, K), (N, K2) = a.shape, (b_rows, b_cols)
    else:
        (K, M), (K2, N) = a.shape, (b_rows, b_cols)
    assert K == K2, (name, a.shape, b.shape, mode)
    tm, tn, tk = min(tm, M), min(tn, N), min(tk, K)
    cols_are_n = mode != "nt"
    if b_kind == "devcol":
        tn, tk = (b.shape[-1], tk) if cols_are_n else (tn, b.shape[-1])
    shard = N // N_DEV
    if o_kind == "devcol":
        tn = max(shard, tn // shard * shard)
    assert M % tm == 0 and N % tn == 0 and K % tk == 0, (name, M, N, K, tm, tn, tk)
    nk = K // tk
    dims = {"nn": NN, "nt": NT, "tn": TN}[mode]
    n_ex, n_out = len(extras) + len(params), len(out_dtypes)

    def body(*refs):
        a_ref, b_ref = refs[:2]
        ex = refs[2:2 + n_ex]
        outs = refs[2 + n_ex:2 + n_ex + n_out]
        acc = refs[-1]
        k = pl.program_id(2)

        @pl.when(k == 0)
        def _():
            acc[...] = jnp.zeros_like(acc)

        acc[...] += _dot(a_ref[...].astype(bf16), b_ref[...].astype(bf16), dims)

        @pl.when(k == nk - 1)
        def _():
            r = acc[...]
            res = epilogue(r, *[e[...] for e in ex]) if epilogue is not None else (r,)
            for o, v in zip(outs, res):
                if o_kind == "plain":
                    o[...] = v.astype(o.dtype)
                else:
                    for q in range(tn // shard):
                        o[q] = v[:, q * shard:(q + 1) * shard].astype(o.dtype)

    if mode == "tn":
        a_spec = pl.BlockSpec((tk, tm), lambda i, j, k: (k, i))
    else:
        a_spec = pl.BlockSpec((tm, tk), lambda i, j, k: (i, k))
    jb, kb = (lambda i, j, k: j), (lambda i, j, k: k)
    if mode == "nt":
        b_spec = _mat_spec(b, b_kind, b_lead, tn, tk, jb, kb)
    else:
        b_spec = _mat_spec(b, b_kind, b_lead, tk, tn, kb, jb)
    e_spec = pl.BlockSpec((tm, tn), lambda i, j, k: (i, j))
    if o_kind == "plain":
        o_spec, o_shape = e_spec, (M, N)
    else:
        o_spec, o_shape = pl.BlockSpec((tn // shard, tm, shard), lambda i, j, k: (j, i, 0)), (N_DEV, M, shard)
    res = pl.pallas_call(
        body, name=name,
        grid=(M // tm, N // tn, nk),
        in_specs=[a_spec, b_spec] + [e_spec] * len(extras)
        + [pl.BlockSpec(p.shape, lambda i, j, k: (0, 0)) for p in params],
        out_specs=[o_spec] * n_out,
        out_shape=[jax.ShapeDtypeStruct(o_shape, dt) for dt in out_dtypes],
        scratch_shapes=[pltpu.VMEM((tm, tn), f32)],
        compiler_params=_cparams("parallel", "parallel", "arbitrary"),
    )(a, b, *extras, *params)
    return res[0] if n_out == 1 else res


def _row_spec(tm, cb, width):
    assert (cb * LANE) % width == 0
    blk = (cb * LANE) // width
    return pl.BlockSpec((tm, width), lambda i: (i, blk))


def _whole_spec(p):
    nd = p.ndim
    return pl.BlockSpec(p.shape, lambda i: (0,) * nd)


def _tl_bwd(name, fn, rows, params, cot_rows, cot_fn=None, skip=(), bf16_copy=False, tm=512):
    T = rows[0][0].shape[0]
    tm = min(tm, T)
    nr, npar, nc = len(rows), len(params), len(cot_rows)
    keep = [k for k in range(nr) if k not in skip]
    n_rows = len(keep) + int(bf16_copy)
    row_dtypes = [(rows[k][2], f32) for k in keep] + ([(rows[keep[0]][2], bf16)] if bf16_copy else [])

    def body(*refs):
        vals = [r[...] for r in refs[:nr + npar]]
        cots = [r[...] for r in refs[nr + npar:nr + npar + nc]]
        outs = refs[nr + npar + nc:]
        cot = tuple(cot_fn(*cots)) if cot_fn is not None else tuple(cots)
        _, vjp = jax.vjp(fn, *vals)
        grads = vjp(cot)
        for o, k in zip(outs, keep):
            o[...] = grads[k].astype(o.dtype)
        if bf16_copy:
            outs[len(keep)][...] = grads[keep[0]].astype(bf16)
        i = pl.program_id(0)
        for o, g in zip(outs[n_rows:], grads[nr:]):
            @pl.when(i == 0)
            def _(o=o):
                o[...] = jnp.zeros_like(o)
            o[...] += g

    res = pl.pallas_call(
        body, name=name, grid=(T // tm,),
        in_specs=[_row_spec(tm, cb, w) for (_, cb, w) in rows] + [_whole_spec(p) for p in params]
        + [_row_spec(tm, cb, w) for (_, cb, w) in cot_rows],
        out_specs=[pl.BlockSpec((tm, w), lambda i: (i, 0)) for w, _ in row_dtypes] + [_whole_spec(p) for p in params],
        out_shape=[jax.ShapeDtypeStruct((T, w), dt) for w, dt in row_dtypes]
        + [jax.ShapeDtypeStruct(p.shape, f32) for p in params],
        compiler_params=_cparams("arbitrary"),
    )(*[r[0] for r in rows], *params, *[r[0] for r in cot_rows])
    return res[:n_rows], res[n_rows:]


def _ln_res_fn(x, mix, g, b):
    pre = DN_ALPHA * x + mix
    mu = jnp.mean(pre, axis=-1, keepdims=True)
    var = jnp.mean(jnp.square(pre - mu), axis=-1, keepdims=True)
    return ((pre - mu) * lax.rsqrt(var + LN_EPS) * g + b,)


@jax.custom_jvp
def _expm1(x):
    small = jnp.abs(x) < 0.3
    xs = jnp.where(small, x, 0.0)
    poly = xs * (1.0 + xs * (1 / 2 + xs * (1 / 6 + xs * (1 / 24 + xs * (1 / 120 + xs * (
        1 / 720 + xs * (1 / 5040 + xs * (1 / 40320 + xs * (1 / 362880)))))))))
    return jnp.where(small, poly, jnp.exp(x) - 1.0)


@_expm1.defjvp
def _expm1_jvp(primals, tangents):
    (x,), (t,) = primals, tangents
    return _expm1(x), t * jnp.exp(x)


def _rglru_pre_fn(pre_r, pre_i, xc, b_a, b_x, lam):
    r = jax.nn.sigmoid(pre_r + b_a)
    i = jax.nn.sigmoid(pre_i + b_x)
    log_a = -LRU_C * r * jax.nn.softplus(-lam)
    a = jnp.exp(log_a)
    b = jnp.sqrt(-_expm1(2.0 * log_a)) * (i * xc)
    return a, b


def _rec_gate_fn(h, gate):
    return (h * jax.nn.gelu(gate),)


def _loss_head(y, t, tm=512):
    T, Dm = y.shape
    tm = min(tm, T)

    def body(y_ref, t_ref, dy_ref, loss_ref):
        e = y_ref[...] - t_ref[...]
        dy_ref[...] = e * (1.0 / Dm)

        @pl.when(pl.program_id(0) == 0)
        def _():
            loss_ref[...] = jnp.zeros_like(loss_ref)

        loss_ref[...] += 0.5 * jnp.sum(jnp.mean(e * e, axis=-1, keepdims=True), axis=0, keepdims=True)

    dy, loss = pl.pallas_call(
        body, name="loss_head", grid=(T // tm,),
        in_specs=[pl.BlockSpec((tm, Dm), lambda i: (i, 0))] * 2,
        out_specs=[pl.BlockSpec((tm, Dm), lambda i: (i, 0)), pl.BlockSpec((SUBLANE, LANE), lambda i: (0, 0))],
        out_shape=[jax.ShapeDtypeStruct((T, Dm), f32), jax.ShapeDtypeStruct((SUBLANE, LANE), f32)],
        compiler_params=_cparams("arbitrary"),
    )(y, t)
    return loss[0, 0], dy


def _conv_fwd(name, x, cb0, nblk, w, bias, tm=4096):
    T = x.shape[0]
    tm = min(tm, T)
    hb = tm // SUBLANE
    has_b = bias is not None

    def body(*refs):
        cur, prev, w_ref = refs[:3]
        b_ref = refs[3] if has_b else None
        o = refs[-1]
        i = pl.program_id(1)
        p = jnp.where(i > 0, prev[...], 0.0)
        xcat = jnp.concatenate([p, cur[...]], axis=0)
        acc = cur[...] * w_ref[3:4, :]
        for j in range(3):
            acc = acc + pltpu.roll(xcat, 3 - j, axis=0)[SUBLANE:] * w_ref[j:j + 1, :]
        if has_b:
            acc = acc + b_ref[...]
        o[...] = acc

    in_specs = [
        pl.BlockSpec((tm, LANE), lambda c, i: (i, cb0 + c)),
        pl.BlockSpec((SUBLANE, LANE), lambda c, i: (jnp.maximum(i * hb - 1, 0), cb0 + c)),
        pl.BlockSpec((4, LANE), lambda c, i: (0, c)),
    ]
    args = [x, x, w]
    if has_b:
        in_specs.append(pl.BlockSpec((1, LANE), lambda c, i: (0, c)))
        args.append(bias)
    return pl.pallas_call(
        body, name=name, grid=(nblk, T // tm),
        in_specs=in_specs,
        out_specs=pl.BlockSpec((tm, LANE), lambda c, i: (i, c)),
        out_shape=jax.ShapeDtypeStruct((T, nblk * LANE), f32),
        compiler_params=_cparams("parallel", "parallel"),
    )(*args)


def _conv_bwd(name, dy, x, cb0, nblk, w, into, into_cb, tm=4096):
    T = x.shape[0]
    tm = min(tm, T)
    hb = tm // SUBLANE
    nt = T // tm

    def body(dcur, dnext, xcur, xprev, w_ref, _, dx_ref, dw_ref, db_ref):
        i = pl.program_id(1)
        d = dcur[...]
        dn = jnp.where(i < nt - 1, dnext[...], 0.0)
        dcat = jnp.concatenate([d, dn], axis=0)
        acc = d * w_ref[3:4, :]
        for j in range(3):
            s = 3 - j
            acc = acc + pltpu.roll(dcat, tm + SUBLANE - s, axis=0)[:tm] * w_ref[j:j + 1, :]
        dx_ref[...] = acc.astype(dx_ref.dtype)

        p = jnp.where(i > 0, xprev[...], 0.0)
        xcat = jnp.concatenate([p, xcur[...]], axis=0)
        rows = [jnp.sum(d * pltpu.roll(xcat, 3 - j, axis=0)[SUBLANE:], axis=0, keepdims=True) for j in range(3)]
        rows.append(jnp.sum(d * xcur[...], axis=0, keepdims=True))
        rows.append(jnp.zeros((SUBLANE - 4, LANE), f32))

        @pl.when(i == 0)
        def _():
            dw_ref[...] = jnp.zeros_like(dw_ref)
            db_ref[...] = jnp.zeros_like(db_ref)

        dw_ref[...] += jnp.concatenate(rows, axis=0)
        db_ref[...] += jnp.broadcast_to(jnp.sum(d, axis=0, keepdims=True), (SUBLANE, LANE))

    nh = T // SUBLANE
    dx, dw, db = pl.pallas_call(
        body, name=name, grid=(nblk, nt),
        in_specs=[
            pl.BlockSpec((tm, LANE), lambda c, i: (i, c)),
            pl.BlockSpec((SUBLANE, LANE), lambda c, i: (jnp.minimum((i + 1) * hb, nh - 1), c)),
            pl.BlockSpec((tm, LANE), lambda c, i: (i, cb0 + c)),
            pl.BlockSpec((SUBLANE, LANE), lambda c, i: (jnp.maximum(i * hb - 1, 0), cb0 + c)),
            pl.BlockSpec((4, LANE), lambda c, i: (0, c)),
            pl.BlockSpec(memory_space=pl.ANY),
        ],
        out_specs=[
            pl.BlockSpec((tm, LANE), lambda c, i: (i, into_cb + c)),
            pl.BlockSpec((SUBLANE, LANE), lambda c, i: (0, c)),
            pl.BlockSpec((SUBLANE, LANE), lambda c, i: (0, c)),
        ],
        out_shape=[jax.ShapeDtypeStruct(into.shape, into.dtype),
                   jax.ShapeDtypeStruct((SUBLANE, nblk * LANE), f32),
                   jax.ShapeDtypeStruct((SUBLANE, nblk * LANE), f32)],
        input_output_aliases={5: 0},
        compiler_params=_cparams("parallel", "arbitrary"),
    )(dy, dy, x, x, w, into)
    return dx, dw[:4], db[0]


@functools.partial(jax.custom_vjp, nondiff_argnums=(1,))
def _lroll(x, s):
    return pltpu.roll(x, s, axis=1)


def _lroll_fwd(x, s):
    return _lroll(x, s), None


def _lroll_bwd(s, _, g):
    return (_lroll(g, (LANE - s) % LANE),)


_lroll.defvjp(_lroll_fwd, _lroll_bwd)


def _rope_tables(T):
    half = A_HEAD_DIM // 2
    inv_freq = ROPE_THETA ** (-jnp.arange(half, dtype=f32) / half)
    ang = jnp.arange(T, dtype=f32)[:, None] * inv_freq[None, :]
    cos, sin = jnp.cos(ang), jnp.sin(ang)
    return jnp.tile(jnp.concatenate([cos, cos], axis=1), (1, 2)), jnp.tile(jnp.concatenate([-sin, sin], axis=1), (1, 2))


def _attn_block_fn(n, q, kp, kc, vp, vc, cq, sq, cp, sp, sinks):
    W = WINDOW
    lane = lax.broadcasted_iota(jnp.int32, (W, LANE), 1)
    lo_half = (lane % A_HEAD_DIM) < (A_HEAD_DIM // 2)
    lane8 = lax.broadcasted_iota(jnp.int32, sinks.shape, 1)

    def rope(x, c, s):
        return x * c + jnp.where(lo_half, _lroll(x, LANE - A_HEAD_DIM // 2), _lroll(x, A_HEAD_DIM // 2)) * s

    k2 = jnp.concatenate([rope(kp, cp, sp), rope(kc, cq, sq)], axis=0).astype(bf16)
    v2 = jnp.concatenate([vp, vc], axis=0).astype(bf16)
    qs = []
    for t in range(4):
        qt = rope(q[:, LANE * t:LANE * (t + 1)], cq, sq)
        g = t // 2
        for hh in range(2):
            qa = jnp.where((lane // A_HEAD_DIM) == hh, qt, 0.0)
            qs.append(_lroll(qa, A_HEAD_DIM) if hh != g else qa)
    s_all = _dot(jnp.concatenate(qs, axis=0).astype(bf16), k2, NT) * (A_HEAD_DIM ** -0.5)
    row = lax.broadcasted_iota(jnp.int32, (W, 2 * W), 0)
    col = lax.broadcasted_iota(jnp.int32, (W, 2 * W), 1)
    dist = row + W - col
    mask = (dist >= 0) & (dist < W) & ((col >= W) | (n > 0))
    ps = []
    for j in range(A_Q_HEADS):
        s = jnp.where(mask, s_all[W * j:W * (j + 1)], -jnp.inf)
        sink = jnp.sum(jnp.where(lane8 == j, sinks, 0.0), axis=1, keepdims=True)
        m = jnp.maximum(jnp.max(s, axis=-1, keepdims=True), sink)
        e = jnp.exp(s - m)
        ps.append((e / (jnp.sum(e, axis=-1, keepdims=True) + jnp.exp(sink - m))).astype(bf16))
    o = _dot(jnp.concatenate(ps, axis=0), v2, NN)
    outs = []
    for t in range(4):
        g = t // 2
        ot = jnp.zeros((W, LANE), f32)
        for hh in range(2):
            j = 2 * t + hh
            oj = jnp.where((lane // A_HEAD_DIM) == g, o[W * j:W * (j + 1)], 0.0)
            ot = ot + (_lroll(oj, A_HEAD_DIM) if hh != g else oj)
        outs.append(ot)
    return jnp.concatenate(outs, axis=1)


def _attn_specs():
    W = WINDOW
    prev = lambda n: jnp.maximum(n - 1, 0)
    return [
        pl.BlockSpec((W, 4 * LANE), lambda n: (n, CB_QA // 4)),
        pl.BlockSpec((W, LANE), lambda n: (prev(n), CB_KA)),
        pl.BlockSpec((W, LANE), lambda n: (n, CB_KA)),
        pl.BlockSpec((W, LANE), lambda n: (prev(n), CB_VA)),
        pl.BlockSpec((W, LANE), lambda n: (n, CB_VA)),
        pl.BlockSpec((W, LANE), lambda n: (n, 0)),
        pl.BlockSpec((W, LANE), lambda n: (n, 0)),
        pl.BlockSpec((W, LANE), lambda n: (prev(n), 0)),
        pl.BlockSpec((W, LANE), lambda n: (prev(n), 0)),
        pl.BlockSpec((1, A_Q_HEADS), lambda n: (0, 0)),
    ]


def _attn_fwd(name, proj, cos, sin, sinks):
    T = proj.shape[0]
    W = WINDOW

    def body(*refs):
        o = refs[-1]
        o[...] = _attn_block_fn(pl.program_id(0), *[r[...] for r in refs[:-1]]).astype(o.dtype)

    return pl.pallas_call(
        body, name=name, grid=(T // W,),
        in_specs=_attn_specs(),
        out_specs=pl.BlockSpec((W, 4 * LANE), lambda n: (n, 0)),
        out_shape=jax.ShapeDtypeStruct((T, 2 * 4 * LANE), bf16),
        compiler_params=_cparams("parallel"),
    )(proj, proj, proj, proj, proj, cos, sin, cos, sin, sinks)


def _attn_bwd(name, proj, cos, sin, sinks, d_oab):
    T = proj.shape[0]
    W = WINDOW
    Q = 4 * LANE
    nb = T // W

    def body(*refs):
        ins = [r[...] for r in refs[:10]]
        do = refs[10][...]
        out_ref, ds_ref, d_ref = refs[11:]
        n = pl.program_id(0)
        _, vjp = jax.vjp(functools.partial(_attn_block_fn, n), *ins)
        dq, dkp, dkc, dvp, dvc, _, _, _, _, dsk = vjp(do)

        @pl.when(n == 0)
        def _():
            d_ref[:, Q:] = jnp.zeros((T, 2 * LANE), f32)
            ds_ref[...] = jnp.zeros_like(ds_ref)

        cur = pl.ds(pl.multiple_of(n * W, W), W)
        d_ref[cur, :Q] = dq
        d_ref[cur, Q:Q + LANE] += dkc
        d_ref[cur, Q + LANE:] += dvc
        ds_ref[...] += dsk

        @pl.when(n > 0)
        def _():
            prv = pl.ds(pl.multiple_of((n - 1) * W, W), W)
            d_ref[prv, Q:Q + LANE] += dkp
            d_ref[prv, Q + LANE:] += dvp

        @pl.when(n == nb - 1)
        def _():
            out_ref[...] = d_ref[...].astype(out_ref.dtype)

    return pl.pallas_call(
        body, name=name, grid=(nb,),
        in_specs=_attn_specs() + [pl.BlockSpec((W, Q), lambda n: (n, 0))],
        out_specs=[pl.BlockSpec((T, Q + 2 * LANE), lambda n: (0, 0)),
                   pl.BlockSpec((1, A_Q_HEADS), lambda n: (0, 0))],
        out_shape=[jax.ShapeDtypeStruct((T, HYB_PROJ_PAD), bf16), jax.ShapeDtypeStruct((1, A_Q_HEADS), f32)],
        scratch_shapes=[pltpu.VMEM((T, Q + 2 * LANE), f32)],
        compiler_params=_cparams("arbitrary"),
    )(proj, proj, proj, proj, proj, cos, sin, cos, sin, sinks, d_oab)


def _bdot(spec, a, b, precision=None):
    return jnp.einsum(spec, a, b, preferred_element_type=f32, precision=precision)


@jax.custom_vjp
def _tri_inv(a):
    H, C, _ = a.shape
    B = 2 * SUBLANE
    nb = C // B
    r = lax.broadcasted_iota(jnp.int32, (C, C), 0)
    c = lax.broadcasted_iota(jnp.int32, (C, C), 1)
    a4 = jnp.where((r // B) == (c // B), a, 0.0).reshape(H, nb, B, C)
    t4 = jnp.broadcast_to(jnp.where(r == c, 1.0, 0.0).astype(f32), a.shape).reshape(H, nb, B, C)
    for j in range(B - 1):
        col = jnp.concatenate([a4[:, b:b + 1, :, B * b + j:B * b + j + 1] for b in range(nb)], axis=1)
        t4 = t4 - col * t4[:, :, j:j + 1, :]
    x = t4.reshape(H, C, C)
    hi = lax.Precision.HIGH
    while B < C:
        m = jnp.where(((r // (2 * B)) == (c // (2 * B))) & ((r // B) > (c // B)), a, 0.0)
        x = x - _bdot("hij,hjk->hik", x, _bdot("hij,hjk->hik", m, x, precision=hi), precision=hi)
        B *= 2
    return x


def _tri_inv_fwd(a):
    t = _tri_inv(a)
    return t, t


def _tri_inv_bwd(t, g):
    C = t.shape[-1]
    r = lax.broadcasted_iota(jnp.int32, (C, C), 0)
    c = lax.broadcasted_iota(jnp.int32, (C, C), 1)
    x = _bdot("hki,hkj->hij", t, g, precision=lax.Precision.HIGHEST)
    y = _bdot("hik,hjk->hij", x, t, precision=lax.Precision.HIGHEST)
    return (jnp.where(r > c, -y, 0.0),)


_tri_inv.defvjp(_tri_inv_fwd, _tri_inv_bwd)


@jax.custom_vjp
def _tri_inv_saved(a, t):
    return t


_tri_inv_saved.defvjp(lambda a, t: (t, t), lambda t, g: (_tri_inv_bwd(t, g)[0], jnp.zeros_like(t)))


def _silu(x):
    return x * jax.nn.sigmoid(x)


def _l2n(x):
    return x * lax.rsqrt(jnp.sum(x * x, axis=-1, keepdims=True) + NORM_EPS)


def _delta_chunk_fn(cq, ck, cv, z, lg, a_log, dt_bias, norm_w, S, t_saved=None, want_t=False):
    C = B_CHUNK
    lane = lax.broadcasted_iota(jnp.int32, (C, LANE), 1)
    pick = lambda l0: jnp.concatenate(
        [jnp.sum(jnp.where(lane == l0 + h, lg, 0.0), axis=1, keepdims=True)[None] for h in range(B_HEADS)], axis=0)
    bl, al = pick(0), pick(B_HEADS)
    q = _l2n(_silu(cq)) * (B_HEAD_DIM ** -0.5)
    k = _l2n(_silu(ck))
    v = _silu(cv)
    beta = jax.nn.sigmoid(bl)
    g = -jnp.exp(a_log) * jax.nn.softplus(al + dt_bias)
    r = lax.broadcasted_iota(jnp.int32, (C, C), 0)
    c = lax.broadcasted_iota(jnp.int32, (C, C), 1)
    eye = r == c
    g_row = jnp.sum(jnp.where(eye, g, 0.0), axis=1, keepdims=True)
    gc = jnp.sum(jnp.where(c <= r, g_row, 0.0), axis=2, keepdims=True)
    gc_row = jnp.sum(jnp.where(eye, gc, 0.0), axis=1, keepdims=True)
    decay_incl = jnp.exp(jnp.where(r >= c, gc - gc_row, -jnp.inf))
    decay_strict = jnp.where(r > c, decay_incl, 0.0)
    kb = k * beta
    vb = v * beta
    kbf = k.astype(bf16)
    a_mat = _bdot("hik,hjk->hij", kb.astype(bf16), kbf) * decay_strict
    t_f32 = _tri_inv(a_mat) if t_saved is None else _tri_inv_saved(a_mat, t_saved)
    t_mat = t_f32.astype(bf16)
    eg = jnp.exp(gc)
    u = _bdot("hij,hjv->hiv", t_mat, vb.astype(bf16))
    w = _bdot("hij,hjk->hik", t_mat, (kb * eg).astype(bf16))
    qk = _bdot("hik,hjk->hij", q.astype(bf16), kbf) * decay_incl
    g_last = jnp.sum(g, axis=1, keepdims=True)
    k_tail = k * jnp.exp(g_last - gc)
    Sb = S.astype(bf16)
    v_new = u - _bdot("hck,hkv->hcv", w.astype(bf16), Sb)
    o = _bdot("hck,hkv->hcv", (q * eg).astype(bf16), Sb) + _bdot("hij,hjv->hiv", qk.astype(bf16), v_new.astype(bf16))
    S_new = S * jnp.exp(g_last) + _bdot("hck,hcv->hkv", k_tail.astype(bf16), v_new.astype(bf16))
    ob = o * lax.rsqrt(jnp.mean(o * o, axis=-1, keepdims=True) + NORM_EPS) * norm_w
    return (ob * _silu(z), S_new) + ((t_f32,) if want_t else ())


DELTA_CHUNKS_PER_STEP = 8


def _delta_in_specs(rev, N):
    C = DELTA_CHUNKS_PER_STEP * B_CHUNK
    ix = (lambda n: N - 1 - n) if rev else (lambda n: n)
    specs = [pl.BlockSpec((C, 3 * B_HEADS * LANE), lambda n: (ix(n), 0))]
    specs += [pl.BlockSpec((C, LANE), lambda n, h=h: (ix(n), CB_Z + h)) for h in range(B_HEADS)]
    specs += [
        pl.BlockSpec((C, LANE), lambda n: (ix(n), CB_LG)),
        pl.BlockSpec((B_HEADS, 1, 1), lambda n: (0, 0, 0)),
        pl.BlockSpec((B_HEADS, 1, 1), lambda n: (0, 0, 0)),
        pl.BlockSpec((1, LANE), lambda n: (0, 0)),
    ]
    return specs


def _delta_inputs(u, c_ref, z_refs, lg, al, dt, nw):
    H = B_HEADS
    rows = slice(u * B_CHUNK, (u + 1) * B_CHUNK)
    part = lambda p: jnp.stack([c_ref[rows, LANE * (p * H + h):LANE * (p * H + h + 1)] for h in range(H)])
    return (part(0), part(1), part(2), jnp.stack([z[rows, :] for z in z_refs]), lg[rows, :], al[...], dt[...], nw[...])


def _delta_fwd(name, c, proj, a_log, dt_bias, norm_w, o_ab):
    T = c.shape[0]
    C = B_CHUNK
    N = T // C
    Dh = B_HEAD_DIM
    H = B_HEADS

    def body(*refs):
        c_ref, z_refs, (lg, al, dt, nw) = refs[0], refs[1:1 + H], refs[1 + H:5 + H]
        o_ref, s_ref, t_ref, S = refs[6 + H:]

        @pl.when(pl.program_id(0) == 0)
        def _():
            S[...] = jnp.zeros_like(S)

        s = S[...]
        for u in range(U):
            s_ref[:, u] = s
            ob, s, t = _delta_chunk_fn(*_delta_inputs(u, c_ref, z_refs, lg, al, dt, nw), s, want_t=True)
            for h in range(H):
                o_ref[u * C:(u + 1) * C, LANE * h:LANE * (h + 1)] = ob[h].astype(o_ref.dtype)
            t_ref[:, u] = t
        S[...] = s

    U = DELTA_CHUNKS_PER_STEP
    return pl.pallas_call(
        body, name=name, grid=(N // U,),
        in_specs=_delta_in_specs(False, N // U) + [pl.BlockSpec(memory_space=pl.ANY)],
        out_specs=[pl.BlockSpec((U * C, H * LANE), lambda n: (n, 1)),
                   pl.BlockSpec((H, U, Dh, Dh), lambda n: (0, n, 0, 0)),
                   pl.BlockSpec((H, U, C, C), lambda n: (0, n, 0, 0))],
        out_shape=[jax.ShapeDtypeStruct(o_ab.shape, o_ab.dtype), jax.ShapeDtypeStruct((H, N, Dh, Dh), f32),
                   jax.ShapeDtypeStruct((H, N, C, C), f32)],
        input_output_aliases={5 + H: 0},
        scratch_shapes=[pltpu.VMEM((H, Dh, Dh), f32)],
        compiler_params=_cparams("arbitrary"),
    )(c, *([proj] * H), proj, a_log, dt_bias, norm_w, o_ab)


def _delta_bwd(name, c, proj, a_log, dt_bias, norm_w, s_saved, t_saved, d_oab, dproj):
    T = c.shape[0]
    C = B_CHUNK
    N = T // C
    Dh = B_HEAD_DIM
    H = B_HEADS

    def body(*refs):
        c_ref, z_refs, (lg, al, dt, nw) = refs[0], refs[1:1 + H], refs[1 + H:5 + H]
        s_ref, t_ref, do_ref = refs[5 + H:8 + H]
        dc, dtail, dal, ddt, dnw, dS = refs[9 + H:]

        @pl.when(pl.program_id(0) == 0)
        def _():
            dS[...] = jnp.zeros_like(dS)
            dal[...] = jnp.zeros_like(dal)
            ddt[...] = jnp.zeros_like(ddt)
            dnw[...] = jnp.zeros_like(dnw)

        ds = dS[...]
        for u in reversed(range(U)):
            rows = slice(u * C, (u + 1) * C)
            _, vjp = jax.vjp(functools.partial(_delta_chunk_fn, t_saved=t_ref[:, u]),
                             *_delta_inputs(u, c_ref, z_refs, lg, al, dt, nw), s_ref[:, u])
            do = jnp.stack([do_ref[rows, LANE * h:LANE * (h + 1)] for h in range(H)])
            g = vjp((do, ds))
            for h in range(H):
                for p in range(3):
                    dc[rows, LANE * (p * H + h):LANE * (p * H + h + 1)] = g[p][h]
                dtail[rows, LANE * h:LANE * (h + 1)] = g[3][h].astype(dtail.dtype)
            dtail[rows, LANE * H:LANE * (H + 1)] = g[4].astype(dtail.dtype)
            dtail[rows, LANE * (H + 1):] = jnp.zeros((C, LANE), dtail.dtype)
            dal[...] += g[5]
            ddt[...] += g[6]
            dnw[...] += g[7]
            ds = g[8]
        dS[...] = ds

    U = DELTA_CHUNKS_PER_STEP
    NB = N // U
    rn = lambda n: NB - 1 - n
    return pl.pallas_call(
        body, name=name, grid=(NB,),
        in_specs=_delta_in_specs(True, NB) + [
            pl.BlockSpec((H, U, Dh, Dh), lambda n: (0, rn(n), 0, 0)),
            pl.BlockSpec((H, U, C, C), lambda n: (0, rn(n), 0, 0)),
            pl.BlockSpec((U * C, H * LANE), lambda n: (rn(n), 1)),
            pl.BlockSpec(memory_space=pl.ANY),
        ],
        out_specs=[
            pl.BlockSpec((U * C, 3 * H * LANE), lambda n: (rn(n), 0)),
            pl.BlockSpec((U * C, (H + 2) * LANE), lambda n: (rn(n), CB_Z // (H + 2))),
            pl.BlockSpec((H, 1, 1), lambda n: (0, 0, 0)),
            pl.BlockSpec((H, 1, 1), lambda n: (0, 0, 0)),
            pl.BlockSpec((1, LANE), lambda n: (0, 0)),
        ],
        out_shape=[jax.ShapeDtypeStruct((T, 3 * H * Dh), f32), jax.ShapeDtypeStruct(dproj.shape, dproj.dtype),
                   jax.ShapeDtypeStruct((H, 1, 1), f32), jax.ShapeDtypeStruct((H, 1, 1), f32),
                   jax.ShapeDtypeStruct((1, LANE), f32)],
        input_output_aliases={8 + H: 1},
        scratch_shapes=[pltpu.VMEM((H, Dh, Dh), f32)],
        compiler_params=_cparams("arbitrary"),
    )(c, *([proj] * H), proj, a_log, dt_bias, norm_w, s_saved, t_saved, d_oab, dproj)


def _gate_matmuls(xc, wa_ref, wx_ref):
    bw = wa_ref.shape[-1]
    xb = xc.astype(bf16)
    blocks = [xb[:, bw * h:bw * (h + 1)] for h in range(LRU_BLOCKS)]
    return (jnp.concatenate([_dot(blocks[h], wa_ref[h], NN) for h in range(LRU_BLOCKS)], axis=1),
            jnp.concatenate([_dot(blocks[h], wx_ref[h], NN) for h in range(LRU_BLOCKS)], axis=1))


def _gates_fwd(name, xc, w_a, w_x, pars, tm=512):
    T, Wd = xc.shape
    tm = min(tm, T)

    def body(x_ref, wa_ref, wx_ref, ba, bx, lam, a_ref, b_ref):
        x = x_ref[...]
        pr, pi = _gate_matmuls(x, wa_ref, wx_ref)
        a_ref[...], b_ref[...] = _rglru_pre_fn(pr, pi, x, ba[...], bx[...], lam[...])

    row = pl.BlockSpec((tm, Wd), lambda i: (i, 0))
    return pl.pallas_call(
        body, name=name, grid=(T // tm,),
        in_specs=[row, _whole_spec(w_a), _whole_spec(w_x)] + [_whole_spec(p) for p in pars],
        out_specs=[row, row], out_shape=[jax.ShapeDtypeStruct((T, Wd), f32)] * 2,
        compiler_params=_cparams("parallel"),
    )(xc, w_a, w_x, *pars)


def _gates_bwd(name, xc, w_a, w_x, pars, lam_t, h_prev, tm=512):
    T, Wd = xc.shape
    tm = min(tm, T)
    bw = Wd // LRU_BLOCKS

    def body(x_ref, wa_ref, wx_ref, ba, bx, lam, lt_ref, hp_ref, dx_ref, dr_ref, di_ref, dba, dbx, dlam):
        x = x_ref[...]
        pr, pi = _gate_matmuls(x, wa_ref, wx_ref)
        _, vjp = jax.vjp(_rglru_pre_fn, pr, pi, x, ba[...], bx[...], lam[...])
        lt = lt_ref[...]
        dpr, dpi, dxc, g_ba, g_bx, g_lam = vjp((lt * hp_ref[...], lt))
        dprb, dpib = dpr.astype(bf16), dpi.astype(bf16)
        dx_ref[...] = dxc + jnp.concatenate(
            [_dot(dprb[:, bw * h:bw * (h + 1)], wa_ref[h], NT) + _dot(dpib[:, bw * h:bw * (h + 1)], wx_ref[h], NT)
             for h in range(LRU_BLOCKS)], axis=1)
        dr_ref[...] = dprb
        di_ref[...] = dpib

        @pl.when(pl.program_id(0) == 0)
        def _():
            dba[...] = jnp.zeros_like(dba)
            dbx[...] = jnp.zeros_like(dbx)
            dlam[...] = jnp.zeros_like(dlam)

        dba[...] += g_ba
        dbx[...] += g_bx
        dlam[...] += g_lam

    row = pl.BlockSpec((tm, Wd), lambda i: (i, 0))
    vec = pl.BlockSpec((1, Wd), lambda i: (0, 0))
    return pl.pallas_call(
        body, name=name, grid=(T // tm,),
        in_specs=[row, _whole_spec(w_a), _whole_spec(w_x)] + [_whole_spec(p) for p in pars] + [row, row],
        out_specs=[row, row, row, vec, vec, vec],
        out_shape=[jax.ShapeDtypeStruct((T, Wd), f32), jax.ShapeDtypeStruct((T, Wd), bf16),
                   jax.ShapeDtypeStruct((T, Wd), bf16)] + [jax.ShapeDtypeStruct((1, Wd), f32)] * 3,
        compiler_params=_cparams("arbitrary"),
    )(xc, w_a, w_x, *pars, lam_t, h_prev)


def _blockdiag_bwd_dw(name, xc, dpr, dpi, tk=512):
    T, Wd = xc.shape
    bw = Wd // LRU_BLOCKS
    tk = min(tk, T)

    def body(x_ref, dr, di, oa, ox):
        @pl.when(pl.program_id(1) == 0)
        def _():
            oa[...] = jnp.zeros_like(oa)
            ox[...] = jnp.zeros_like(ox)

        xb = x_ref[...].astype(bf16)
        oa[...] += _dot(xb, dr[...].astype(bf16), TN)
        ox[...] += _dot(xb, di[...].astype(bf16), TN)

    xs = pl.BlockSpec((tk, bw), lambda h, k: (k, h))
    ws = pl.BlockSpec((None, bw, bw), lambda h, k: (h, 0, 0))
    return pl.pallas_call(
        body, name=name, grid=(LRU_BLOCKS, T // tk), in_specs=[xs, xs, xs], out_specs=[ws, ws],
        out_shape=[jax.ShapeDtypeStruct((LRU_BLOCKS, bw, bw), f32)] * 2,
        compiler_params=_cparams("parallel", "arbitrary"),
    )(xc, dpr, dpi)


def _scan(name, a, proj, reverse, b=None, h=None, dhg=None, tt=512, cb=512):
    T, Wd = a.shape
    tt, cb = min(tt, T), min(cb, Wd)
    nt = T // tt
    ng = tt // SUBLANE

    def body(a_ref, g_ref, *rest):
        n_in = 2 if reverse else 1
        ins, outs, (carry, carry_a) = rest[:n_in], rest[n_in:-2], rest[-2:]

        @pl.when(pl.program_id(1) == 0)
        def _():
            carry[...] = jnp.zeros_like(carry)
            carry_a[...] = jnp.zeros_like(carry_a)

        row = lax.broadcasted_iota(jnp.int32, (SUBLANE, cb), 0)

        def group(g, c):
            hp, ap = c
            rows = pl.ds(pl.multiple_of(g * SUBLANE, SUBLANE), SUBLANE)
            A = a_ref[rows, :]
            gate = g_ref[rows, :]
            a_first = jnp.broadcast_to(A[0:1, :], (SUBLANE, cb))
            if reverse:
                _, vjp = jax.vjp(_rec_gate_fn, ins[0][rows, :], gate)
                B, narrow = vjp((ins[1][rows, :],))
                A = jnp.where(row == SUBLANE - 1, ap, pltpu.roll(A, SUBLANE - 1, axis=0))
            else:
                B = ins[0][rows, :]
            for s in (1, 2, 4):
                sh = (SUBLANE - s) if reverse else s
                As = pltpu.roll(A, sh, axis=0)
                Bs = pltpu.roll(B, sh, axis=0)
                valid = (row < SUBLANE - s) if reverse else (row >= s)
                B = jnp.where(valid, A * Bs + B, B)
                A = jnp.where(valid, A * As, A)
            hcur = A * hp + B
            outs[0][rows, :] = hcur
            if not reverse:
                outs[1][rows, :] = jnp.where(row == 0, hp, pltpu.roll(hcur, 1, axis=0))
                narrow = _rec_gate_fn(hcur, gate)[0]
            edge = hcur[0:1, :] if reverse else hcur[SUBLANE - 1:SUBLANE, :]
            return (jnp.broadcast_to(edge, (SUBLANE, cb)), a_first), narrow

        def pair(pi, c):
            p = (ng // 2 - 1 - pi) if reverse else pi
            c, first = group(2 * p + (1 if reverse else 0), c)
            c, second = group(2 * p + (0 if reverse else 1), c)
            lo, hi = (second, first) if reverse else (first, second)
            rows = pl.ds(pl.multiple_of(p * 2 * SUBLANE, 2 * SUBLANE), 2 * SUBLANE)
            outs[-1][rows, :] = jnp.concatenate([lo, hi], axis=0).astype(bf16)
            return c

        carry[...], carry_a[...] = lax.fori_loop(0, ng // 2, pair, (carry[...], carry_a[...]))

    nc = Wd // cb
    tok = (lambda i: nt - 1 - i) if reverse else (lambda i: i)
    spec = pl.BlockSpec((tt, cb), lambda c, i: (tok(i), c))
    gate_half = pl.BlockSpec((tt, cb), lambda c, i: (tok(i), nc + c))
    if reverse:
        args, out_specs = (a, proj, h, dhg), [spec, gate_half]
        out_shape = [jax.ShapeDtypeStruct((T, Wd), f32), jax.ShapeDtypeStruct((T, 2 * Wd), bf16)]
    else:
        args, out_specs = (a, proj, b), [spec] * 3
        out_shape = [jax.ShapeDtypeStruct((T, Wd), f32)] * 2 + [jax.ShapeDtypeStruct((T, Wd), bf16)]
    return pl.pallas_call(
        body, name=name, grid=(nc, nt), in_specs=[spec, gate_half] + [spec] * (len(args) - 2), out_specs=out_specs,
        out_shape=out_shape,
        scratch_shapes=[pltpu.VMEM((SUBLANE, cb), f32), pltpu.VMEM((SUBLANE, cb), f32)],
        compiler_params=_cparams("parallel", "arbitrary"),
    )(*args)


def _relu2_epilogue(r):
    h = jnp.maximum(r, 0.0)
    return r, h * h


def _drelu2_epilogue(r, a):
    return (r * (2.0 * jnp.maximum(a.astype(f32), 0.0)),)


def _residual_cot(through, upper):
    return (through + DN_ALPHA * upper,)


def _merge_cols(name, g, tm=256):
    _, L, R, s = g.shape

    def body(g_ref, o_ref):
        for d in range(N_DEV):
            o_ref[:, s * d:s * (d + 1)] = g_ref[d].astype(bf16)
        o_ref[:, N_DEV * s:] = jnp.zeros((tm, HYB_PROJ_PAD - N_DEV * s), bf16)

    return pl.pallas_call(
        body, name=name, grid=(L, R // tm),
        in_specs=[pl.BlockSpec((N_DEV, None, tm, s), lambda l, i: (0, l, i, 0))],
        out_specs=pl.BlockSpec((None, tm, HYB_PROJ_PAD), lambda l, i: (l, i, 0)),
        out_shape=jax.ShapeDtypeStruct((L, R, HYB_PROJ_PAD), bf16),
        compiler_params=_cparams("parallel", "parallel"),
    )(g)


def _split_cols(name, dw, tm=256):
    R = dw.shape[0]
    s = HYB_PROJ // N_DEV

    def body(g_ref, o_ref):
        for d in range(N_DEV):
            o_ref[d] = g_ref[:, s * d:s * (d + 1)].astype(bf16)

    return pl.pallas_call(
        body, name=name, grid=(R // tm,),
        in_specs=[pl.BlockSpec((tm, HYB_PROJ_PAD), lambda i: (i, 0))],
        out_specs=pl.BlockSpec((N_DEV, tm, s), lambda i: (0, i, 0)),
        out_shape=jax.ShapeDtypeStruct((N_DEV, R, s), bf16),
        compiler_params=_cparams("parallel"),
    )(dw)


def _rows_to_dev(dw):
    nb, r, c = dw.shape
    t = dw.reshape(nb, N_DEV, r // N_DEV, c)
    return jnp.moveaxis(t, 1, 0).reshape(N_DEV, nb * (r // N_DEV), c).astype(bf16)


def _ln_epilogue(r, x, g, b):
    y = _ln_res_fn(x, r, g, b)[0]
    return r, y, y


def _hybrid_fwd(tag, x, xb, W, j, cos, sin, ln):
    proj = _mm(f"{tag}_proj", xb, W["hyb_w_in"][j], "nn", b_kind="lead", b_lead=0,
               tm=2048 if xb.dtype == bf16 else 1024)
    o_a = _attn_fwd(f"{tag}_attn", proj, cos, sin, W["hyb_sinks"][j][None, :])
    c = _conv_fwd(f"{tag}_conv", proj, CB_CONV, 12, W["hyb_conv_w"][j], None)
    o_ab, s_saved, t_saved = _delta_fwd(f"{tag}_delta", c, proj, W["hyb_a_log"][j].reshape(B_HEADS, 1, 1),
                                        W["hyb_dt_bias"][j].reshape(B_HEADS, 1, 1), W["hyb_norm_w"][j][None, :], o_a)
    mix, x1, x1b = _mm(f"{tag}_out", o_ab, W["hyb_w_out"][j], "nn", b_kind="lead", b_lead=0, epilogue=_ln_epilogue,
                       extras=(x,), params=ln, out_dtypes=(f32, f32, bf16))
    return mix, x1, x1b, (proj, c, s_saved, t_saved, o_ab)


def _hybrid_bwd(tag, x, dmix, addend, W, j, cos, sin, saved, G, send_early):
    proj, c, s_saved, t_saved, o_ab = saved
    T = x.shape[0]
    d_oab = _mm(f"{tag}_dout", dmix, W["hyb_w_out"][j], "nt", b_kind="lead", b_lead=0)
    G["hyb_w_out"][j] = _mm(f"{tag}_dwout", o_ab, dmix, "tn", out_dtypes=(bf16,)).reshape(N_DEV, -1, D_MODEL)
    sinks = W["hyb_sinks"][j][None, :] + send_early({("hyb_w_out", j): G["hyb_w_out"][j]})
    dproj, dsinks = _attn_bwd(f"{tag}_dattn", proj, cos, sin, sinks, d_oab)
    a_log = W["hyb_a_log"][j].reshape(B_HEADS, 1, 1)
    dt_bias = W["hyb_dt_bias"][j].reshape(B_HEADS, 1, 1)
    dc, dproj, dal, ddt, dnw = _delta_bwd(f"{tag}_ddelta", c, proj, a_log, dt_bias, W["hyb_norm_w"][j][None, :],
                                          s_saved, t_saved, d_oab, dproj)
    dproj, dconv_w, _ = _conv_bwd(f"{tag}_dconv", dc, proj, CB_CONV, 12, W["hyb_conv_w"][j], dproj, CB_CONV)
    dx = _mm(f"{tag}_dx", dproj, W["hyb_w_in"][j], "nt", b_kind="lead", b_lead=0,
             **({} if addend is None else dict(epilogue=_residual_cot, extras=(addend,))))
    G["hyb_w_in"][j] = _split_cols(f"{tag}_dwin_split", _mm(f"{tag}_dwin", x, dproj, "tn", tn=1536))
    G["hyb_sinks"][j] = dsinks[0]
    G["hyb_conv_w"][j] = dconv_w
    G["hyb_a_log"][j] = dal.reshape(B_HEADS)
    G["hyb_dt_bias"][j] = ddt.reshape(B_HEADS)
    G["hyb_norm_w"][j] = dnw[0]
    return dx


def _rec_fwd(tag, x, xb, W, j, ln):
    Wd = D_MODEL
    proj = _mm(f"{tag}_proj", xb, W["rec_w_in"][j], "nn", b_kind="devcol", b_lead=0, tm=2048)
    xc = _conv_fwd(f"{tag}_conv", proj, 0, Wd // LANE, W["rec_conv_w"][j], W["rec_conv_b"][j][None, :])
    pars = [W["rec_b_a"][j][None, :], W["rec_b_x"][j][None, :], W["rec_lambda"][j][None, :]]
    a, b = _gates_fwd(f"{tag}_gates", xc, W["rec_w_a"][j][0], W["rec_w_x"][j][0], pars)
    h, h_prev, hg = _scan(f"{tag}_scan", a, proj, False, b=b)
    mix, x1, x1b = _mm(f"{tag}_out", hg, W["rec_w_out"][j], "nn", b_kind="lead", b_lead=0, epilogue=_ln_epilogue,
                       extras=(x,), params=ln, out_dtypes=(f32, f32, bf16))
    return mix, x1, x1b, (proj, xc, a, h, h_prev, hg)


def _rec_bwd(tag, x, dmix, addend, W, j, saved, G, send_early):
    proj, xc, a, h, h_prev, hg = saved
    Wd = D_MODEL
    dhg = _mm(f"{tag}_dout", dmix, W["rec_w_out"][j], "nt", b_kind="lead", b_lead=0)
    G["rec_w_out"][j] = _mm(f"{tag}_dwout", hg, dmix, "tn", out_dtypes=(bf16,)).reshape(N_DEV, -1, D_MODEL)
    sent = send_early({("rec_w_out", j): G["rec_w_out"][j]})
    lam_t, dproj = _scan(f"{tag}_dscan", a, proj, True, h=h, dhg=dhg)
    pars = [W["rec_b_a"][j][None, :] + sent, W["rec_b_x"][j][None, :], W["rec_lambda"][j][None, :]]
    dxc, dpr, dpi, db_a, db_x, dlam = _gates_bwd(f"{tag}_dgates", xc, W["rec_w_a"][j][0], W["rec_w_x"][j][0], pars,
                                                 lam_t, h_prev)
    dwa, dwx = _blockdiag_bwd_dw(f"{tag}_dgates_dw", xc, dpr, dpi)
    G["rec_w_a"][j], G["rec_w_x"][j] = _rows_to_dev(dwa), _rows_to_dev(dwx)
    dproj, dconv_w, dconv_b = _conv_bwd(f"{tag}_dconv", dxc, proj, 0, Wd // LANE, W["rec_conv_w"][j], dproj, 0)
    dx = _mm(f"{tag}_dx", dproj, W["rec_w_in"][j], "nt", b_kind="devcol", b_lead=0,
             **({} if addend is None else dict(epilogue=_residual_cot, extras=(addend,))))
    G["rec_w_in"][j] = _mm(f"{tag}_dwin", x, dproj, "tn", o_kind="devcol", out_dtypes=(bf16,), tn=2048)
    G["rec_conv_w"][j] = dconv_w
    G["rec_conv_b"][j] = dconv_b
    G["rec_b_a"][j] = db_a[0]
    G["rec_b_x"][j] = db_x[0]
    G["rec_lambda"][j] = dlam[0]
    return dx


def _local_step(x, target, W, load_layer, grads_ready):
    T = x.shape[0]
    cos, sin = _rope_tables(T)
    saved = []
    xb = x
    for layer in range(DEPTH):
        j = layer // 2
        tag = f"L{layer}"
        load_layer(layer, "mixer", x)
        ln1 = (W["ln1_g"][layer][None, :], W["ln1_b"][layer][None, :])
        if layer % 2 == 0:
            mix, x1, x1b, sv = _hybrid_fwd(tag, x, xb, W, j, cos, sin, ln1)
        else:
            mix, x1, x1b, sv = _rec_fwd(tag, x, xb, W, j, ln1)
        load_layer(layer, "mlp", x1)
        a, h2 = _mm(f"{tag}_mlp1", x1b, W["mlp_w1"][layer], "nn", b_kind="devcol", b_lead=0, epilogue=_relu2_epilogue,
                    out_dtypes=(bf16, bf16), tm=2048)
        ln2 = (W["ln2_g"][layer][None, :], W["ln2_b"][layer][None, :])
        y, x2, x2b = _mm(f"{tag}_mlp2", h2, W["mlp_w2"][layer], "nn", b_kind="lead", b_lead=0, epilogue=_ln_epilogue,
                         extras=(x1,), params=ln2, out_dtypes=(f32, f32, bf16))
        saved.append((x, xb, sv, mix, x1, x1b, a, h2, y))
        x, xb = x2, x2b
    loss, dx = _loss_head(x, target)

    G = {k: [None] * (DEPTH if k.startswith(("ln", "mlp")) else DEPTH // 2) for k in (
        "hyb_w_in", "hyb_sinks", "hyb_conv_w", "hyb_a_log", "hyb_dt_bias", "hyb_norm_w", "hyb_w_out",
        "rec_w_in", "rec_conv_w", "rec_conv_b", "rec_w_a", "rec_b_a", "rec_w_x", "rec_b_x", "rec_lambda", "rec_w_out",
        "ln1_g", "ln1_b", "mlp_w1", "mlp_w2", "ln2_g", "ln2_b")}
    held = {}
    cot_rows, cot_fn = [(dx, 0, D_MODEL)], None
    for layer in reversed(range(DEPTH)):
        j = layer // 2
        tag = f"L{layer}"
        x0, x0b, sv, mix, x1, x1b, a, h2, y = saved[layer]
        ln2 = [W["ln2_g"][layer][None, :], W["ln2_b"][layer][None, :]]
        (dy, dyb), (dg2, db2) = _tl_bwd(f"{tag}_dln2", _ln_res_fn, [(x1, 0, D_MODEL), (y, 0, D_MODEL)], ln2,
                                        cot_rows, cot_fn=cot_fn, skip=(0,), bf16_copy=True)
        G["ln2_g"][layer], G["ln2_b"][layer] = dg2[0], db2[0]
        da = _mm(f"{tag}_dmlp2", dyb, W["mlp_w2"][layer], "nt", b_kind="lead", b_lead=0, epilogue=_drelu2_epilogue,
                 extras=(a,), out_dtypes=(bf16,), tm=2048, tn=512)
        G["mlp_w2"][layer] = _mm(f"{tag}_dw2", h2, dyb, "tn", out_dtypes=(bf16,), tm=2048).reshape(N_DEV, -1, D_MODEL)
        dx1 = _mm(f"{tag}_dmlp1", da, W["mlp_w1"][layer], "nt", b_kind="devcol", b_lead=0, tm=2048)
        G["mlp_w1"][layer] = _mm(f"{tag}_dw1", x1b, da, "tn", o_kind="devcol", out_dtypes=(bf16,), tn=2048)
        ln1 = [W["ln1_g"][layer][None, :], W["ln1_b"][layer][None, :]]
        (dmix, dmixb), (dg1, db1) = _tl_bwd(f"{tag}_dln1", _ln_res_fn, [(x0, 0, D_MODEL), (mix, 0, D_MODEL)], ln1,
                                            [(dx1, 0, D_MODEL), (dy, 0, D_MODEL)], cot_fn=_residual_cot, skip=(0,),
                                            bf16_copy=True)
        G["ln1_g"][layer], G["ln1_b"][layer] = dg1[0], db1[0]
        dx0_a = dmix if layer == 0 else None
        held.update({(k, layer): G[k][layer] for k in ("mlp_w1", "mlp_w2")})
        early = functools.partial(grads_ready, f"l{layer}_early", held)
        if layer % 2 == 0:
            dx = _hybrid_bwd(tag, x0b, dmixb, dx0_a, W, j, cos, sin, sv, G, early)
        else:
            dx = _rec_bwd(tag, x0b, dmixb, dx0_a, W, j, sv, G, early)
        held = {(k, i): G[k][i] for k, i in _layer_weights(layer)[:-2] if not k.endswith("w_out")}
        cot_rows, cot_fn = [(dx, 0, D_MODEL), (dmix, 0, D_MODEL)], _residual_cot
    grads_ready("l0_late", held, {})
    big = {k for k, _ in BIG}
    return loss, dx, {k: jnp.stack(v) for k, v in G.items() if k not in big}


def _layer_weights(layer):
    j = layer // 2
    mixer = ["hyb_w_in", "hyb_w_out"] if layer % 2 == 0 else ["rec_w_in", "rec_w_out", "rec_w_a", "rec_w_x"]
    return [(k, j) for k in mixer] + [("mlp_w1", layer), ("mlp_w2", layer)]


def _my_coords():
    return lax.axis_index("x"), lax.axis_index("y"), lax.axis_index("c")


def _all_gather(name, arrays):
    na = len(arrays)

    def body(*refs):
        x_refs, out_refs = refs[:na], refs[na:2 * na]
        send_sems, recv_sems, local_sems = refs[2 * na:]
        x, y, c = _my_coords()
        me, sibling = (x, y, c), (x, y, 1 - c)
        chips = [(1 - x, y), (x, 1 - y), (1 - x, 1 - y)]

        def blk(a, px, py, pc):
            return out_refs[a].at[4 * px + 2 * py + pc]

        def copy(a, k, block, to, src=None):
            return pltpu.make_async_remote_copy(
                src_ref=blk(a, *block) if src is None else src, dst_ref=blk(a, *block),
                send_sem=send_sems.at[a, k], recv_sem=recv_sems.at[a, k],
                device_id=to, device_id_type=pl.DeviceIdType.MESH)

        mine = [pltpu.make_async_copy(x_refs[a], blk(a, *me), local_sems.at[a]) for a in range(na)]
        for cp in mine:
            cp.start()
        first = []
        for a in range(na):
            first.append(copy(a, 0, me, sibling, src=x_refs[a]))
            first += [copy(a, 1 + j, me, (*chip, c), src=x_refs[a]) for j, chip in enumerate(chips)]
        for cp in first:
            cp.start()
        passed = []
        for a in range(na):
            for j, chip in enumerate(chips):
                copy(a, 1 + j, (*chip, c), me).wait_recv()
                passed.append(copy(a, 4 + j, (*chip, c), sibling))
                passed[-1].start()
        for a in range(na):
            copy(a, 0, sibling, me).wait_recv()
            for j, chip in enumerate(chips):
                copy(a, 4 + j, (*chip, 1 - c), me).wait_recv()
        for cp in first + passed:
            cp.wait_send()
        for cp in mine:
            cp.wait()

    return pl.pallas_call(
        body, name=name,
        out_shape=[jax.ShapeDtypeStruct((N_DEV,) + a.shape, a.dtype) for a in arrays],
        in_specs=[pl.BlockSpec(memory_space=pl.ANY)] * na,
        out_specs=[pl.BlockSpec(memory_space=pl.ANY)] * na,
        scratch_shapes=[pltpu.SemaphoreType.DMA((na, 7)), pltpu.SemaphoreType.DMA((na, 7)),
                        pltpu.SemaphoreType.DMA((na,))],
    )(*arrays)


_HBM = pl.BlockSpec(memory_space=pltpu.HBM)
_SEM = pl.BlockSpec(memory_space=pltpu.SEMAPHORE)


def _flip(k, x, y, c):
    return ((1 - x) if k & 4 else x, (1 - y) if k & 2 else y, (1 - c) if k & 1 else c)


_PEERS = {"gather": (1, 2, 4, 6), "scatter": (1, 2, 3, 4, 5, 6, 7)}


def _push_copies(kind, x_refs, land_refs, send_sems, recv_sems, local_sems):
    x, y, c = _my_coords()
    me = 4 * x + 2 * y + c
    peers = _PEERS[kind]
    remote, local = [], []
    for a in range(len(x_refs)):
        local.append(pltpu.make_async_copy(x_refs[a] if kind == "gather" else x_refs[a].at[me], land_refs[a].at[me],
                                           local_sems.at[a]))
        for n, k in enumerate(peers):
            px, py, pc = _flip(k, x, y, c)
            remote.append(pltpu.make_async_remote_copy(
                src_ref=x_refs[a] if kind == "gather" else x_refs[a].at[4 * px + 2 * py + pc],
                dst_ref=land_refs[a].at[me],
                send_sem=send_sems.at[a * len(peers) + n], recv_sem=recv_sems.at[a * len(peers) + n],
                device_id=(px, py, pc), device_id_type=pl.DeviceIdType.MESH))
    return remote, local


def _pass_to_sibling(name, lands):
    na = len(lands)
    chips = (2, 4, 6)

    def body(*refs):
        out_refs, send_sems, recv_sems = refs[na:2 * na], refs[2 * na], refs[2 * na + 1]
        x, y, c = _my_coords()
        cps = []
        for a in range(na):
            for n, k in enumerate(chips):
                px, py, _ = _flip(k, x, y, c)
                cps.append(pltpu.make_async_remote_copy(
                    src_ref=out_refs[a].at[4 * px + 2 * py + c], dst_ref=out_refs[a].at[4 * px + 2 * py + c],
                    send_sem=send_sems.at[a * 3 + n], recv_sem=recv_sems.at[a * 3 + n],
                    device_id=(x, y, 1 - c), device_id_type=pl.DeviceIdType.MESH))
        for cp in cps:
            cp.start()
        for a in range(na):
            for n, k in enumerate(chips):
                px, py, _ = _flip(k, x, y, c)
                blk = out_refs[a].at[4 * px + 2 * py + (1 - c)]
                pltpu.make_async_remote_copy(src_ref=blk, dst_ref=blk, send_sem=send_sems.at[a * 3 + n],
                                             recv_sem=recv_sems.at[a * 3 + n], device_id=(x, y, 1 - c),
                                             device_id_type=pl.DeviceIdType.MESH).wait_recv()
        for cp in cps:
            cp.wait_send()

    return pl.pallas_call(
        body, name=name,
        out_shape=[jax.ShapeDtypeStruct(l.shape, l.dtype) for l in lands],
        in_specs=[pl.BlockSpec(memory_space=pl.ANY)] * na,
        out_specs=[pl.BlockSpec(memory_space=pl.ANY)] * na,
        input_output_aliases={a: a for a in range(na)},
        scratch_shapes=[pltpu.SemaphoreType.DMA((3 * na,)), pltpu.SemaphoreType.DMA((3 * na,))],
    )(*lands)


_SIDE_EFFECT = pltpu.CompilerParams(has_side_effects=pltpu.SideEffectType.DATAFLOW_SIDE_EFFECTING)


def _push_start(name, kind, srcs, lands):
    na = len(srcs)

    def body(*refs):
        remote, local = _push_copies(kind, refs[:na], refs[na:2 * na], *refs[2 * na:2 * na + 3])
        for cp in remote + local:
            cp.start()
        token = refs[-1]
        token[...] = jnp.zeros_like(token)

    arrays = list(srcs) + list(lands)
    n_remote = na * len(_PEERS[kind])
    res = pl.pallas_call(
        body, name=name,
        out_shape=(pltpu.SemaphoreType.DMA((n_remote,)), pltpu.SemaphoreType.DMA((n_remote,)),
                   pltpu.SemaphoreType.DMA((na,)), *[pltpu.HBM(t.shape, t.dtype) for t in arrays],
                   jax.ShapeDtypeStruct((SUBLANE, LANE), f32)),
        in_specs=[_HBM] * (2 * na),
        out_specs=(_SEM, _SEM, _SEM, *[_HBM] * (2 * na), pl.BlockSpec(memory_space=pltpu.VMEM)),
        input_output_aliases={i: 3 + i for i in range(2 * na)},
        compiler_params=_SIDE_EFFECT,
    )(*[pltpu.with_memory_space_constraint(t, pltpu.HBM) for t in arrays])
    return list(res[:3]), res[3:3 + na], res[3 + na:3 + 2 * na], res[-1][:1, :1]


def _push_wait(name, kind, sems, srcs, lands, after):
    na = len(srcs)

    def body(*refs):
        remote, local = _push_copies(kind, refs[:na], refs[na:2 * na], *refs[2 * na:2 * na + 3])
        for cp in remote:
            cp.wait_send()
            cp.wait_recv()
        for cp in local:
            cp.wait()

    arrays = list(srcs) + list(lands)
    res = pl.pallas_call(
        body, name=name,
        out_shape=tuple(pltpu.HBM(t.shape, t.dtype) for t in arrays),
        in_specs=[_HBM] * (2 * na) + [_SEM] * 3 + [pl.BlockSpec(memory_space=pl.ANY)],
        out_specs=tuple([_HBM] * (2 * na)),
        input_output_aliases={i: i for i in range(2 * na)},
        compiler_params=_SIDE_EFFECT,
    )(*arrays, *sems, after)
    return res[na:]


def _sum_blocks(name, land):
    _, R, n = land.shape
    tr = R

    def body(l_ref, o_ref):
        acc = l_ref[0].astype(f32)
        for s in range(1, N_DEV):
            acc = acc + l_ref[s].astype(f32)
        o_ref[...] = acc

    return pl.pallas_call(
        body, name=name, grid=(R // tr,),
        in_specs=[pl.BlockSpec((N_DEV, tr, n), lambda i: (0, i, 0))],
        out_specs=pl.BlockSpec((tr, n), lambda i: (i, 0)),
        out_shape=jax.ShapeDtypeStruct((R, n), f32),
        compiler_params=_cparams("parallel"),
    )(land)


def _adamw(name, w, g, m, v):
    shape = w.shape
    last = shape[-1]
    rows = math.prod(shape[:-1])
    tm = 256 if rows % 256 == 0 and rows > 256 else rows
    w2, g2, m2, v2 = (t.reshape(rows, last) for t in (w, g, m, v))

    def body(w_ref, g_ref, m_ref, v_ref, d_ref, mo_ref, vo_ref):
        gg = g_ref[...]
        mn = ADAM_B1 * m_ref[...] + (1.0 - ADAM_B1) * gg
        vn = ADAM_B2 * v_ref[...] + (1.0 - ADAM_B2) * jnp.square(gg)
        m_hat = mn / (1.0 - ADAM_B1 ** ADAM_STEP)
        v_hat = vn / (1.0 - ADAM_B2 ** ADAM_STEP)
        d_ref[...] = -ADAM_LR * (m_hat / (jnp.sqrt(v_hat) + ADAM_EPS) + ADAM_WD * w_ref[...])
        mo_ref[...] = mn
        vo_ref[...] = vn

    spec = pl.BlockSpec((tm, last), lambda i: (i, 0))
    d, mn, vn = pl.pallas_call(
        body, name=name, grid=(rows // tm,), in_specs=[spec] * 4, out_specs=[spec] * 3,
        out_shape=[jax.ShapeDtypeStruct((rows, last), f32)] * 3,
        compiler_params=_cparams("parallel"),
    )(w2, g2, m2, v2)
    return d.reshape(shape), mn.reshape(shape), vn.reshape(shape)


def _adamw_land(name, lands, w, m, v, tm=256):
    L = len(lands)
    _, R, C = lands[0].shape
    tm = min(tm, R)

    def body(*refs):
        l_refs, (w_ref, m_ref, v_ref, g_ref, d_ref, mo_ref, vo_ref) = refs[:L], refs[L:]
        for k in range(L):
            @pl.when(pl.program_id(0) == k)
            def _(k=k):
                gg = l_refs[k][0].astype(f32)
                for s in range(1, N_DEV):
                    gg = gg + l_refs[k][s].astype(f32)
                g_ref[...] = gg
                mn = ADAM_B1 * m_ref[...] + (1.0 - ADAM_B1) * gg
                vn = ADAM_B2 * v_ref[...] + (1.0 - ADAM_B2) * jnp.square(gg)
                m_hat = mn / (1.0 - ADAM_B1 ** ADAM_STEP)
                v_hat = vn / (1.0 - ADAM_B2 ** ADAM_STEP)
                d_ref[...] = -ADAM_LR * (m_hat / (jnp.sqrt(v_hat) + ADAM_EPS) + ADAM_WD * w_ref[...])
                mo_ref[...] = mn
                vo_ref[...] = vn

    land_specs = [pl.BlockSpec((N_DEV, tm, C), lambda l, i, k=k: (0, jnp.where(l == k, i, 0), 0)) for k in range(L)]
    spec = pl.BlockSpec((None, tm, C), lambda l, i: (l, i, 0))
    return pl.pallas_call(
        body, name=name, grid=(L, R // tm),
        in_specs=land_specs + [spec] * 3,
        out_specs=[spec] * 4,
        out_shape=[jax.ShapeDtypeStruct((L, R, C), f32)] * 4,
        compiler_params=_cparams("arbitrary", "arbitrary"),
    )(*lands, w, m, v)


BIG = [("hyb_w_in", 2), ("hyb_w_out", 1), ("rec_w_in", 2), ("rec_w_out", 1), ("rec_w_a", 2), ("rec_w_x", 2),
       ("mlp_w1", 2), ("mlp_w2", 1)]
SMALL = [("hyb_conv_w", 2), ("rec_conv_w", 2), ("rec_conv_b", 1), ("rec_b_a", 1), ("rec_b_x", 1), ("rec_lambda", 1)]
REPL = ["hyb_sinks", "hyb_a_log", "hyb_dt_bias", "hyb_norm_w", "ln1_g", "ln1_b", "ln2_g", "ln2_b"]
WEIGHTS = ["hyb_w_in", "hyb_sinks", "hyb_conv_w", "hyb_a_log", "hyb_dt_bias", "hyb_norm_w", "hyb_w_out", "rec_w_in",
           "rec_conv_w", "rec_conv_b", "rec_w_a", "rec_b_a", "rec_w_x", "rec_b_x", "rec_lambda", "rec_w_out",
           "ln1_g", "ln1_b", "mlp_w1", "mlp_w2", "ln2_g", "ln2_b"]


def _pack_rows(parts, dtype, row_mult):
    lead = parts[0].shape[:-1]
    flat = jnp.concatenate([p.astype(dtype) for p in parts], axis=-1)
    n = flat.shape[-1]
    unit = row_mult * LANE
    pad = (-n) % unit
    if pad:
        flat = jnp.concatenate([flat, jnp.zeros(lead + (pad,), dtype)], axis=-1)
    return flat.reshape(lead + ((n + pad) // LANE, LANE))


def _gather_full(gathered, shard_shapes, table):
    flat = gathered.reshape(N_DEV, -1)
    out, off = {}, 0
    for name, ax in table:
        shp = shard_shapes[name]
        n = math.prod(shp)
        arr = flat[:, off:off + n].reshape((N_DEV,) + shp)
        off += n
        arr = jnp.moveaxis(arr, 0, ax)
        out[name] = arr.reshape(shp[:ax] + (N_DEV * shp[ax],) + shp[ax + 1:])
    return out


def _matmul_layouts(tag, gw):
    out = {}
    bw = D_MODEL // LRU_BLOCKS
    for k, g in gw.items():
        L = g.shape[1]
        if k == "hyb_w_in":
            out[k] = _merge_cols(f"{tag}_w_in_merge", g)
        elif k in ("hyb_w_out", "rec_w_out", "mlp_w2"):
            out[k] = jnp.swapaxes(g, 0, 1).reshape(L, N_DEV * g.shape[2], g.shape[3])
        elif k in ("rec_w_a", "rec_w_x"):
            out[k] = jnp.moveaxis(g, 0, 2).reshape(L, LRU_BLOCKS, bw, bw)
        else:
            out[k] = g
    return out


def kernel(x, hyb_w_in, hyb_sinks, hyb_conv_w, hyb_a_log, hyb_dt_bias, hyb_norm_w, hyb_w_out, rec_w_in, rec_conv_w, rec_conv_b, rec_w_a, rec_b_a, rec_w_x, rec_b_x, rec_lambda, rec_w_out, ln1_g, ln1_b, mlp_w1, mlp_w2, ln2_g, ln2_b, loss_target, m_hyb_w_in, m_hyb_sinks, m_hyb_conv_w, m_hyb_a_log, m_hyb_dt_bias, m_hyb_norm_w, m_hyb_w_out, m_rec_w_in, m_rec_conv_w, m_rec_conv_b, m_rec_w_a, m_rec_b_a, m_rec_w_x, m_rec_b_x, m_rec_lambda, m_rec_w_out, m_ln1_g, m_ln1_b, m_mlp_w1, m_mlp_w2, m_ln2_g, m_ln2_b, v_hyb_w_in, v_hyb_sinks, v_hyb_conv_w, v_hyb_a_log, v_hyb_dt_bias, v_hyb_norm_w, v_hyb_w_out, v_rec_w_in, v_rec_conv_w, v_rec_conv_b, v_rec_w_a, v_rec_b_a, v_rec_w_x, v_rec_b_x, v_rec_lambda, v_rec_w_out, v_ln1_g, v_ln1_b, v_mlp_w1, v_mlp_w2, v_ln2_g, v_ln2_b):
    args = locals()
    w = {k: args[k] for k in WEIGHTS}
    m = {k: args["m_" + k] for k in WEIGHTS}
    v = {k: args["v_" + k] for k in WEIGHTS}
    shard_shapes = {k: tuple(t.shape) for k, t in w.items()}
    xi, yi, ci = _my_coords()
    me = 4 * xi + 2 * yi + ci

    in_flight = {}

    def install(tag, names, got):
        for (k, i), arr in zip(names, _matmul_layouts(tag, {k: g for (k, _), g in zip(names, got)}).values()):
            W[k][i] = arr

    def start_gather(tag, names):
        srcs = [w[k][i:i + 1].astype(bf16) for k, i in names]
        *pending, zero = _push_start(f"gather_{tag}_start", "gather", srcs,
                                     [lax.empty((N_DEV,) + s.shape, bf16) for s in srcs])
        in_flight[tag] = (names, pending)
        return zero

    def finish_gather(tag, after):
        names, pending = in_flight.pop(tag)
        half = _push_wait(f"gather_{tag}_wait", "gather", *pending, after)
        install(tag, names, _pass_to_sibling(f"gather_{tag}_pass", half))

    def started(k, zero):
        W[k] = W[k] + zero

    def mixer_w(layer):
        return _layer_weights(layer)[:-2]

    def mlp_w(layer):
        return _layer_weights(layer)[-2:]

    gathered0 = _all_gather("gather_first", [w[k][i:i + 1].astype(bf16) for k, i in mixer_w(0)]
                            + [_pack_rows([w[k].reshape(-1) for k, _ in SMALL], f32, SUBLANE)])
    W = _gather_full(gathered0[-1], shard_shapes, SMALL)
    W.update({k: w[k] for k in REPL})
    W.update({k: {} for k, _ in BIG})
    install("l0a", mixer_w(0), gathered0[:-1])
    started("hyb_sinks", start_gather("l0b", mlp_w(0)) + start_gather("l1a", mixer_w(1)))

    def load_layer(layer, part, after):
        if part == "mixer":
            if layer == 1:
                finish_gather("l1a", after)
            if layer >= 2:
                finish_gather(f"l{layer}", after)
            if 1 <= layer < DEPTH - 1:
                started("hyb_sinks" if layer % 2 == 0 else "rec_conv_b",
                        start_gather(f"l{layer + 1}", _layer_weights(layer + 1)))
        elif layer == 0:
            finish_gather("l0b", after)
            started("ln2_g", start_gather("l1b", mlp_w(1)))
        elif layer == 1:
            finish_gather("l1b", after)

    grads_in_flight = {}

    def grads_ready(tag, a, b):
        g = {**a, **b}
        srcs = list(g.values())
        *pending, zero = _push_start(f"scatter_{tag}_start", "scatter", srcs, [lax.empty(s.shape, bf16) for s in srcs])
        grads_in_flight[tag] = (list(g.keys()), pending)
        return zero

    loss_local, grad_x, G = _local_step(x[0], loss_target[0], W, load_layer, grads_ready)
    loss = lax.psum(loss_local, MESH_AXES)

    landed = {}

    def land(tag, after):
        keys, pending = grads_in_flight[tag]
        landed.update(zip(keys, _push_wait(f"scatter_{tag}_wait", "scatter", *pending, after)))

    tags = list(grads_in_flight)
    for tag in tags[:-1]:
        land(tag, grad_x)
    rest = _pack_rows([G[k].reshape(-1) for k, _ in SMALL] + [G[k].reshape(-1) for k in REPL], f32, SUBLANE)
    g_rest = _sum_blocks("sum_rest", _all_gather("gather_rest", [rest])[0]).reshape(-1)

    grads, delta, new_m, new_v = {}, {}, {}, {}

    def adamw_big(k):
        shp = shard_shapes[k]
        s3 = (shp[0], math.prod(shp[1:-1]), shp[-1])
        lands = [landed[(k, i)].reshape((N_DEV,) + s3[1:]) for i in range(shp[0])]
        res = _adamw_land("adamw_" + k, lands, w[k].reshape(s3), m[k].reshape(s3), v[k].reshape(s3))
        grads[k], delta[k], new_m[k], new_v[k] = (r.reshape(shp) for r in res)

    late = {k for k, _ in grads_in_flight[tags[-1]][0]}
    for k in [k for k, _ in BIG if k not in late]:
        adamw_big(k)
        done = new_v[k]
    land(tags[-1], done)
    for k in [k for k, _ in BIG if k in late]:
        adamw_big(k)
    off = 0
    for k, ax in SMALL:
        full_shape = G[k].shape
        n = math.prod(full_shape)
        full = g_rest[off:off + n].reshape(full_shape)
        off += n
        s = shard_shapes[k][ax]
        grads[k] = lax.dynamic_slice_in_dim(full, me * s, s, axis=ax)
    for k in REPL:
        n = math.prod(shard_shapes[k])
        grads[k] = g_rest[off:off + n].reshape(shard_shapes[k])
        off += n

    for k in [k for k, _ in SMALL] + REPL:
        delta[k], new_m[k], new_v[k] = _adamw("adamw_" + k, w[k], grads[k], m[k], v[k])

    return (loss, grad_x[None], *[grads[k] for k in WEIGHTS], *[delta[k] for k in WEIGHTS],
            *[new_m[k] for k in WEIGHTS], *[new_v[k] for k in WEIGHTS])
```

```python
import functools
import math

import jax
import jax.numpy as jnp
from jax import lax
from jax.experimental import pallas as pl
from jax.experimental.pallas import tpu as pltpu

f32 = jnp.float32
bf16 = jnp.bfloat16

N_DEV = 8
D_MODEL = 1024
DEPTH = 4
A_HEAD_DIM = 64
A_Q_HEADS = 8
WINDOW = 128
ROPE_THETA = 10000.0
B_HEADS = 4
B_HEAD_DIM = 128
B_CHUNK = 64
LRU_BLOCKS = 4
LRU_C = 8.0
D_FF = 4 * D_MODEL
HYB_PROJ = 2824
HYB_PROJ_PAD = 3072
DN_ALPHA = (2 * DEPTH) ** 0.25
LN_EPS = 1e-5
NORM_EPS = 1e-6
ADAM_LR = 0.001
ADAM_B1 = 0.9
ADAM_B2 = 0.999
ADAM_EPS = 1e-08
ADAM_WD = 0.01
ADAM_STEP = 10

LANE = 128
SUBLANE = 8
VMEM_LIMIT = 48 * 1024 * 1024

CB_QA, CB_KA, CB_VA, CB_CONV, CB_Z, CB_LG = 0, 4, 5, 6, 18, 22

MESH_AXES = ("x", "y", "c")


def _cparams(*sem):
    return pltpu.CompilerParams(dimension_semantics=sem, vmem_limit_bytes=VMEM_LIMIT)


def _dot(a, b, dims, precision=None):
    return lax.dot_general(a, b, (dims, ((), ())), preferred_element_type=f32, precision=precision)


NN = ((1,), (0,))
NT = ((1,), (1,))
TN = ((0,), (0,))


def _mat_spec(arr, kind, lead, br, bc, rb, cb):
    if kind == "plain":
        return pl.BlockSpec((br, bc), lambda i, j, k: (rb(i, j, k), cb(i, j, k)))
    if kind == "lead":
        return pl.BlockSpec((None, br, bc), lambda i, j, k: (lead, rb(i, j, k), cb(i, j, k)))
    assert kind == "devcol" and bc == arr.shape[-1]
    return pl.BlockSpec((None, None, br, bc), lambda i, j, k: (cb(i, j, k), lead, rb(i, j, k), 0))


def _mm(name, a, b, mode, *, b_kind="plain", b_lead=0, o_kind="plain", epilogue=None, extras=(), params=(),
        out_dtypes=(f32,), tm=1024, tn=1024, tk=None):
    if tk is None:
        tk = 512 if mode == "tn" else 1024
    if b_kind in ("plain", "lead"):
        b_rows, b_cols = b.shape[-2:]
    else:
        b_rows, b_cols = b.shape[-2], N_DEV * b.shape[-1]
    if mode == "nn":
        (M, K), (K2, N) = a.shape, (b_rows, b_cols)
    elif mode == "nt":
        (M, K), (N, K2) = a.shape, (b_rows, b_cols)
    else:
        (K, M), (K2, N) = a.shape, (b_rows, b_cols)
    assert K == K2, (name, a.shape, b.shape, mode)
    tm, tn, tk = min(tm, M), min(tn, N), min(tk, K)
    cols_are_n = mode != "nt"
    if b_kind == "devcol":
        tn, tk = (b.shape[-1], tk) if cols_are_n else (tn, b.shape[-1])
    shard = N // N_DEV
    if o_kind == "devcol":
        tn = max(shard, tn // shard * shard)
    assert M % tm == 0 and N % tn == 0 and K % tk == 0, (name, M, N, K, tm, tn, tk)
    nk = K // tk
    dims = {"nn": NN, "nt": NT, "tn": TN}[mode]
    n_ex, n_out = len(extras) + len(params), len(out_dtypes)

    def body(*refs):
        a_ref, b_ref = refs[:2]
        ex = refs[2:2 + n_ex]
        outs = refs[2 + n_ex:2 + n_ex + n_out]
        acc = refs[-1]
        k = pl.program_id(2)

        @pl.when(k == 0)
        def _():
            acc[...] = jnp.zeros_like(acc)

        acc[...] += _dot(a_ref[...].astype(bf16), b_ref[...].astype(bf16), dims)

        @pl.when(k == nk - 1)
        def _():
            r = acc[...]
            res = epilogue(r, *[e[...] for e in ex]) if epilogue is not None else (r,)
            for o, v in zip(outs, res):
                if o_kind == "plain":
                    o[...] = v.astype(o.dtype)
                else:
                    for q in range(tn // shard):
                        o[q] = v[:, q * shard:(q + 1) * shard].astype(o.dtype)

    if mode == "tn":
        a_spec = pl.BlockSpec((tk, tm), lambda i, j, k: (k, i))
    else:
        a_spec = pl.BlockSpec((tm, tk), lambda i, j, k: (i, k))
    jb, kb = (lambda i, j, k: j), (lambda i, j, k: k)
    if mode == "nt":
        b_spec = _mat_spec(b, b_kind, b_lead, tn, tk, jb, kb)
    else:
        b_spec = _mat_spec(b, b_kind, b_lead, tk, tn, kb, jb)
    e_spec = pl.BlockSpec((tm, tn), lambda i, j, k: (i, j))
    if o_kind == "plain":
        o_spec, o_shape = e_spec, (M, N)
    else:
        o_spec, o_shape = pl.BlockSpec((tn // shard, tm, shard), lambda i, j, k: (j, i, 0)), (N_DEV, M, shard)
    res = pl.pallas_call(
        body, name=name,
        grid=(M // tm, N // tn, nk),
        in_specs=[a_spec, b_spec] + [e_spec] * len(extras)
        + [pl.BlockSpec(p.shape, lambda i, j, k: (0, 0)) for p in params],
        out_specs=[o_spec] * n_out,
        out_shape=[jax.ShapeDtypeStruct(o_shape, dt) for dt in out_dtypes],
        scratch_shapes=[pltpu.VMEM((tm, tn), f32)],
        compiler_params=_cparams("parallel", "parallel", "arbitrary"),
    )(a, b, *extras, *params)
    return res[0] if n_out == 1 else res


def _row_spec(tm, cb, width):
    assert (cb * LANE) % width == 0
    blk = (cb * LANE) // width
    return pl.BlockSpec((tm, width), lambda i: (i, blk))


def _whole_spec(p):
    nd = p.ndim
    return pl.BlockSpec(p.shape, lambda i: (0,) * nd)


def _tl_bwd(name, fn, rows, params, cot_rows, cot_fn=None, skip=(), bf16_copy=False, tm=512):
    T = rows[0][0].shape[0]
    tm = min(tm, T)
    nr, npar, nc = len(rows), len(params), len(cot_rows)
    keep = [k for k in range(nr) if k not in skip]
    n_rows = len(keep) + int(bf16_copy)
    row_dtypes = [(rows[k][2], f32) for k in keep] + ([(rows[keep[0]][2], bf16)] if bf16_copy else [])

    def body(*refs):
        vals = [r[...] for r in refs[:nr + npar]]
        cots = [r[...] for r in refs[nr + npar:nr + npar + nc]]
        outs = refs[nr + npar + nc:]
        cot = tuple(cot_fn(*cots)) if cot_fn is not None else tuple(cots)
        _, vjp = jax.vjp(fn, *vals)
        grads = vjp(cot)
        for o, k in zip(outs, keep):
            o[...] = grads[k].astype(o.dtype)
        if bf16_copy:
            outs[len(keep)][...] = grads[keep[0]].astype(bf16)
        i = pl.program_id(0)
        for o, g in zip(outs[n_rows:], grads[nr:]):
            @pl.when(i == 0)
            def _(o=o):
                o[...] = jnp.zeros_like(o)
            o[...] += g

    res = pl.pallas_call(
        body, name=name, grid=(T // tm,),
        in_specs=[_row_spec(tm, cb, w) for (_, cb, w) in rows] + [_whole_spec(p) for p in params]
        + [_row_spec(tm, cb, w) for (_, cb, w) in cot_rows],
        out_specs=[pl.BlockSpec((tm, w), lambda i: (i, 0)) for w, _ in row_dtypes] + [_whole_spec(p) for p in params],
        out_shape=[jax.ShapeDtypeStruct((T, w), dt) for w, dt in row_dtypes]
        + [jax.ShapeDtypeStruct(p.shape, f32) for p in params],
        compiler_params=_cparams("arbitrary"),
    )(*[r[0] for r in rows], *params, *[r[0] for r in cot_rows])
    return res[:n_rows], res[n_rows:]


def _ln_res_fn(x, mix, g, b):
    pre = DN_ALPHA * x + mix
    mu = jnp.mean(pre, axis=-1, keepdims=True)
    var = jnp.mean(jnp.square(pre - mu), axis=-1, keepdims=True)
    return ((pre - mu) * lax.rsqrt(var + LN_EPS) * g + b,)


@jax.custom_jvp
def _expm1(x):
    small = jnp.abs(x) < 0.3
    xs = jnp.where(small, x, 0.0)
    poly = xs * (1.0 + xs * (1 / 2 + xs * (1 / 6 + xs * (1 / 24 + xs * (1 / 120 + xs * (
        1 / 720 + xs * (1 / 5040 + xs * (1 / 40320 + xs * (1 / 362880)))))))))
    return jnp.where(small, poly, jnp.exp(x) - 1.0)


@_expm1.defjvp
def _expm1_jvp(primals, tangents):
    (x,), (t,) = primals, tangents
    return _expm1(x), t * jnp.exp(x)


def _rglru_pre_fn(pre_r, pre_i, xc, b_a, b_x, lam):
    r = jax.nn.sigmoid(pre_r + b_a)
    i = jax.nn.sigmoid(pre_i + b_x)
    log_a = -LRU_C * r * jax.nn.softplus(-lam)
    a = jnp.exp(log_a)
    b = jnp.sqrt(-_expm1(2.0 * log_a)) * (i * xc)
    return a, b


def _rec_gate_fn(h, gate):
    return (h * jax.nn.gelu(gate),)


def _loss_head(y, t, tm=512):
    T, Dm = y.shape
    tm = min(tm, T)

    def body(y_ref, t_ref, dy_ref, loss_ref):
        e = y_ref[...] - t_ref[...]
        dy_ref[...] = e * (1.0 / Dm)

        @pl.when(pl.program_id(0) == 0)
        def _():
            loss_ref[...] = jnp.zeros_like(loss_ref)

        loss_ref[...] += 0.5 * jnp.sum(jnp.mean(e * e, axis=-1, keepdims=True), axis=0, keepdims=True)

    dy, loss = pl.pallas_call(
        body, name="loss_head", grid=(T // tm,),
        in_specs=[pl.BlockSpec((tm, Dm), lambda i: (i, 0))] * 2,
        out_specs=[pl.BlockSpec((tm, Dm), lambda i: (i, 0)), pl.BlockSpec((SUBLANE, LANE), lambda i: (0, 0))],
        out_shape=[jax.ShapeDtypeStruct((T, Dm), f32), jax.ShapeDtypeStruct((SUBLANE, LANE), f32)],
        compiler_params=_cparams("arbitrary"),
    )(y, t)
    return loss[0, 0], dy


def _conv_fwd(name, x, cb0, nblk, w, bias, tm=2048):
    T = x.shape[0]
    tm = min(tm, T)
    hb = tm // SUBLANE
    has_b = bias is not None

    def body(*refs):
        cur, prev, w_ref = refs[:3]
        b_ref = refs[3] if has_b else None
        o = refs[-1]
        i = pl.program_id(1)
        p = jnp.where(i > 0, prev[...], 0.0)
        xcat = jnp.concatenate([p, cur[...]], axis=0)
        acc = cur[...] * w_ref[3:4, :]
        for j in range(3):
            acc = acc + pltpu.roll(xcat, 3 - j, axis=0)[SUBLANE:] * w_ref[j:j + 1, :]
        if has_b:
            acc = acc + b_ref[...]
        o[...] = acc

    in_specs = [
        pl.BlockSpec((tm, LANE), lambda c, i: (i, cb0 + c)),
        pl.BlockSpec((SUBLANE, LANE), lambda c, i: (jnp.maximum(i * hb - 1, 0), cb0 + c)),
        pl.BlockSpec((4, LANE), lambda c, i: (0, c)),
    ]
    args = [x, x, w]
    if has_b:
        in_specs.append(pl.BlockSpec((1, LANE), lambda c, i: (0, c)))
        args.append(bias)
    return pl.pallas_call(
        body, name=name, grid=(nblk, T // tm),
        in_specs=in_specs,
        out_specs=pl.BlockSpec((tm, LANE), lambda c, i: (i, c)),
        out_shape=jax.ShapeDtypeStruct((T, nblk * LANE), f32),
        compiler_params=_cparams("parallel", "parallel"),
    )(*args)


def _conv_bwd(name, dy, x, cb0, nblk, w, into, into_cb, tm=2048):
    T = x.shape[0]
    tm = min(tm, T)
    hb = tm // SUBLANE
    nt = T // tm

    def body(dcur, dnext, xcur, xprev, w_ref, _, dx_ref, dw_ref, db_ref):
        i = pl.program_id(1)
        d = dcur[...]
        dn = jnp.where(i < nt - 1, dnext[...], 0.0)
        dcat = jnp.concatenate([d, dn], axis=0)
        acc = d * w_ref[3:4, :]
        for j in range(3):
            s = 3 - j
            acc = acc + pltpu.roll(dcat, tm + SUBLANE - s, axis=0)[:tm] * w_ref[j:j + 1, :]
        dx_ref[...] = acc.astype(dx_ref.dtype)

        p = jnp.where(i > 0, xprev[...], 0.0)
        xcat = jnp.concatenate([p, xcur[...]], axis=0)
        rows = [jnp.sum(d * pltpu.roll(xcat, 3 - j, axis=0)[SUBLANE:], axis=0, keepdims=True) for j in range(3)]
        rows.append(jnp.sum(d * xcur[...], axis=0, keepdims=True))
        rows.append(jnp.zeros((SUBLANE - 4, LANE), f32))

        @pl.when(i == 0)
        def _():
            dw_ref[...] = jnp.zeros_like(dw_ref)
            db_ref[...] = jnp.zeros_like(db_ref)

        dw_ref[...] += jnp.concatenate(rows, axis=0)
        db_ref[...] += jnp.broadcast_to(jnp.sum(d, axis=0, keepdims=True), (SUBLANE, LANE))

    nh = T // SUBLANE
    dx, dw, db = pl.pallas_call(
        body, name=name, grid=(nblk, nt),
        in_specs=[
            pl.BlockSpec((tm, LANE), lambda c, i: (i, c)),
            pl.BlockSpec((SUBLANE, LANE), lambda c, i: (jnp.minimum((i + 1) * hb, nh - 1), c)),
            pl.BlockSpec((tm, LANE), lambda c, i: (i, cb0 + c)),
            pl.BlockSpec((SUBLANE, LANE), lambda c, i: (jnp.maximum(i * hb - 1, 0), cb0 + c)),
            pl.BlockSpec((4, LANE), lambda c, i: (0, c)),
            pl.BlockSpec(memory_space=pl.ANY),
        ],
        out_specs=[
            pl.BlockSpec((tm, LANE), lambda c, i: (i, into_cb + c)),
            pl.BlockSpec((SUBLANE, LANE), lambda c, i: (0, c)),
            pl.BlockSpec((SUBLANE, LANE), lambda c, i: (0, c)),
        ],
        out_shape=[jax.ShapeDtypeStruct(into.shape, into.dtype),
                   jax.ShapeDtypeStruct((SUBLANE, nblk * LANE), f32),
                   jax.ShapeDtypeStruct((SUBLANE, nblk * LANE), f32)],
        input_output_aliases={5: 0},
        compiler_params=_cparams("parallel", "arbitrary"),
    )(dy, dy, x, x, w, into)
    return dx, dw[:4], db[0]


@functools.partial(jax.custom_vjp, nondiff_argnums=(1,))
def _lroll(x, s):
    return pltpu.roll(x, s, axis=1)


def _lroll_fwd(x, s):
    return _lroll(x, s), None


def _lroll_bwd(s, _, g):
    return (_lroll(g, (LANE - s) % LANE),)


_lroll.defvjp(_lroll_fwd, _lroll_bwd)


def _rope_tables(T):
    half = A_HEAD_DIM // 2
    inv_freq = ROPE_THETA ** (-jnp.arange(half, dtype=f32) / half)
    ang = jnp.arange(T, dtype=f32)[:, None] * inv_freq[None, :]
    cos, sin = jnp.cos(ang), jnp.sin(ang)
    return jnp.tile(jnp.concatenate([cos, cos], axis=1), (1, 2)), jnp.tile(jnp.concatenate([-sin, sin], axis=1), (1, 2))


def _attn_block_fn(n, q, kp, kc, vp, vc, cq, sq, cp, sp, sinks):
    W = WINDOW
    lane = lax.broadcasted_iota(jnp.int32, (W, LANE), 1)
    lo_half = (lane % A_HEAD_DIM) < (A_HEAD_DIM // 2)
    lane8 = lax.broadcasted_iota(jnp.int32, sinks.shape, 1)

    def rope(x, c, s):
        return x * c + jnp.where(lo_half, _lroll(x, LANE - A_HEAD_DIM // 2), _lroll(x, A_HEAD_DIM // 2)) * s

    k2 = jnp.concatenate([rope(kp, cp, sp), rope(kc, cq, sq)], axis=0).astype(bf16)
    v2 = jnp.concatenate([vp, vc], axis=0).astype(bf16)
    qs = []
    for t in range(4):
        qt = rope(q[:, LANE * t:LANE * (t + 1)], cq, sq)
        g = t // 2
        for hh in range(2):
            qa = jnp.where((lane // A_HEAD_DIM) == hh, qt, 0.0)
            qs.append(_lroll(qa, A_HEAD_DIM) if hh != g else qa)
    s_all = _dot(jnp.concatenate(qs, axis=0).astype(bf16), k2, NT) * (A_HEAD_DIM ** -0.5)
    row = lax.broadcasted_iota(jnp.int32, (W, 2 * W), 0)
    col = lax.broadcasted_iota(jnp.int32, (W, 2 * W), 1)
    dist = row + W - col
    mask = (dist >= 0) & (dist < W) & ((col >= W) | (n > 0))
    ps = []
    for j in range(A_Q_HEADS):
        s = jnp.where(mask, s_all[W * j:W * (j + 1)], -jnp.inf)
        sink = jnp.sum(jnp.where(lane8 == j, sinks, 0.0), axis=1, keepdims=True)
        m = jnp.maximum(jnp.max(s, axis=-1, keepdims=True), sink)
        e = jnp.exp(s - m)
        ps.append((e / (jnp.sum(e, axis=-1, keepdims=True) + jnp.exp(sink - m))).astype(bf16))
    o = _dot(jnp.concatenate(ps, axis=0), v2, NN)
    outs = []
    for t in range(4):
        g = t // 2
        ot = jnp.zeros((W, LANE), f32)
        for hh in range(2):
            j = 2 * t + hh
            oj = jnp.where((lane // A_HEAD_DIM) == g, o[W * j:W * (j + 1)], 0.0)
            ot = ot + (_lroll(oj, A_HEAD_DIM) if hh != g else oj)
        outs.append(ot)
    return jnp.concatenate(outs, axis=1)


def _attn_specs():
    W = WINDOW
    prev = lambda n: jnp.maximum(n - 1, 0)
    return [
        pl.BlockSpec((W, 4 * LANE), lambda n: (n, CB_QA // 4)),
        pl.BlockSpec((W, LANE), lambda n: (prev(n), CB_KA)),
        pl.BlockSpec((W, LANE), lambda n: (n, CB_KA)),
        pl.BlockSpec((W, LANE), lambda n: (prev(n), CB_VA)),
        pl.BlockSpec((W, LANE), lambda n: (n, CB_VA)),
        pl.BlockSpec((W, LANE), lambda n: (n, 0)),
        pl.BlockSpec((W, LANE), lambda n: (n, 0)),
        pl.BlockSpec((W, LANE), lambda n: (prev(n), 0)),
        pl.BlockSpec((W, LANE), lambda n: (prev(n), 0)),
        pl.BlockSpec((1, A_Q_HEADS), lambda n: (0, 0)),
    ]


def _attn_fwd(name, proj, cos, sin, sinks):
    T = proj.shape[0]
    W = WINDOW

    def body(*refs):
        o = refs[-1]
        o[...] = _attn_block_fn(pl.program_id(0), *[r[...] for r in refs[:-1]]).astype(o.dtype)

    return pl.pallas_call(
        body, name=name, grid=(T // W,),
        in_specs=_attn_specs(),
        out_specs=pl.BlockSpec((W, 4 * LANE), lambda n: (n, 0)),
        out_shape=jax.ShapeDtypeStruct((T, 2 * 4 * LANE), bf16),
        compiler_params=_cparams("parallel"),
    )(proj, proj, proj, proj, proj, cos, sin, cos, sin, sinks)


def _attn_bwd(name, proj, cos, sin, sinks, d_oab):
    T = proj.shape[0]
    W = WINDOW
    Q = 4 * LANE
    nb = T // W

    def body(*refs):
        ins = [r[...] for r in refs[:10]]
        do = refs[10][...]
        out_ref, ds_ref, d_ref = refs[11:]
        n = pl.program_id(0)
        _, vjp = jax.vjp(functools.partial(_attn_block_fn, n), *ins)
        dq, dkp, dkc, dvp, dvc, _, _, _, _, dsk = vjp(do)

        @pl.when(n == 0)
        def _():
            d_ref[:, Q:] = jnp.zeros((T, 2 * LANE), f32)
            ds_ref[...] = jnp.zeros_like(ds_ref)

        cur = pl.ds(pl.multiple_of(n * W, W), W)
        d_ref[cur, :Q] = dq
        d_ref[cur, Q:Q + LANE] += dkc
        d_ref[cur, Q + LANE:] += dvc
        ds_ref[...] += dsk

        @pl.when(n > 0)
        def _():
            prv = pl.ds(pl.multiple_of((n - 1) * W, W), W)
            d_ref[prv, Q:Q + LANE] += dkp
            d_ref[prv, Q + LANE:] += dvp

        @pl.when(n == nb - 1)
        def _():
            out_ref[...] = d_ref[...].astype(out_ref.dtype)

    return pl.pallas_call(
        body, name=name, grid=(nb,),
        in_specs=_attn_specs() + [pl.BlockSpec((W, Q), lambda n: (n, 0))],
        out_specs=[pl.BlockSpec((T, Q + 2 * LANE), lambda n: (0, 0)),
                   pl.BlockSpec((1, A_Q_HEADS), lambda n: (0, 0))],
        out_shape=[jax.ShapeDtypeStruct((T, HYB_PROJ_PAD), bf16), jax.ShapeDtypeStruct((1, A_Q_HEADS), f32)],
        scratch_shapes=[pltpu.VMEM((T, Q + 2 * LANE), f32)],
        compiler_params=_cparams("arbitrary"),
    )(proj, proj, proj, proj, proj, cos, sin, cos, sin, sinks, d_oab)


def _bdot(spec, a, b, precision=None):
    return jnp.einsum(spec, a, b, preferred_element_type=f32, precision=precision)


@jax.custom_vjp
def _tri_inv(a):
    H, C, _ = a.shape
    B = 2 * SUBLANE
    nb = C // B
    r = lax.broadcasted_iota(jnp.int32, (C, C), 0)
    c = lax.broadcasted_iota(jnp.int32, (C, C), 1)
    a4 = jnp.where((r // B) == (c // B), a, 0.0).reshape(H, nb, B, C)
    t4 = jnp.broadcast_to(jnp.where(r == c, 1.0, 0.0).astype(f32), a.shape).reshape(H, nb, B, C)
    for j in range(B - 1):
        col = jnp.concatenate([a4[:, b:b + 1, :, B * b + j:B * b + j + 1] for b in range(nb)], axis=1)
        t4 = t4 - col * t4[:, :, j:j + 1, :]
    x = t4.reshape(H, C, C)
    hi = lax.Precision.HIGH
    while B < C:
        m = jnp.where(((r // (2 * B)) == (c // (2 * B))) & ((r // B) > (c // B)), a, 0.0)
        x = x - _bdot("hij,hjk->hik", x, _bdot("hij,hjk->hik", m, x, precision=hi), precision=hi)
        B *= 2
    return x


def _tri_inv_fwd(a):
    t = _tri_inv(a)
    return t, t


def _tri_inv_bwd(t, g):
    C = t.shape[-1]
    r = lax.broadcasted_iota(jnp.int32, (C, C), 0)
    c = lax.broadcasted_iota(jnp.int32, (C, C), 1)
    x = _bdot("hki,hkj->hij", t, g, precision=lax.Precision.HIGHEST)
    y = _bdot("hik,hjk->hij", x, t, precision=lax.Precision.HIGHEST)
    return (jnp.where(r > c, -y, 0.0),)


_tri_inv.defvjp(_tri_inv_fwd, _tri_inv_bwd)


@jax.custom_vjp
def _tri_inv_saved(a, t):
    return t


_tri_inv_saved.defvjp(lambda a, t: (t, t), lambda t, g: (_tri_inv_bwd(t, g)[0], jnp.zeros_like(t)))


def _silu(x):
    return x * jax.nn.sigmoid(x)


def _l2n(x):
    return x * lax.rsqrt(jnp.sum(x * x, axis=-1, keepdims=True) + NORM_EPS)


def _delta_chunk_fn(cq, ck, cv, z, lg, a_log, dt_bias, norm_w, S, t_saved=None, want_t=False):
    C = B_CHUNK
    lane = lax.broadcasted_iota(jnp.int32, (C, LANE), 1)
    pick = lambda l0: jnp.concatenate(
        [jnp.sum(jnp.where(lane == l0 + h, lg, 0.0), axis=1, keepdims=True)[None] for h in range(B_HEADS)], axis=0)
    bl, al = pick(0), pick(B_HEADS)
    q = _l2n(_silu(cq)) * (B_HEAD_DIM ** -0.5)
    k = _l2n(_silu(ck))
    v = _silu(cv)
    beta = jax.nn.sigmoid(bl)
    g = -jnp.exp(a_log) * jax.nn.softplus(al + dt_bias)
    r = lax.broadcasted_iota(jnp.int32, (C, C), 0)
    c = lax.broadcasted_iota(jnp.int32, (C, C), 1)
    eye = r == c
    g_row = jnp.sum(jnp.where(eye, g, 0.0), axis=1, keepdims=True)
    gc = jnp.sum(jnp.where(c <= r, g_row, 0.0), axis=2, keepdims=True)
    gc_row = jnp.sum(jnp.where(eye, gc, 0.0), axis=1, keepdims=True)
    decay_incl = jnp.exp(jnp.where(r >= c, gc - gc_row, -jnp.inf))
    decay_strict = jnp.where(r > c, decay_incl, 0.0)
    kb = k * beta
    vb = v * beta
    kbf = k.astype(bf16)
    a_mat = _bdot("hik,hjk->hij", kb.astype(bf16), kbf) * decay_strict
    t_f32 = _tri_inv(a_mat) if t_saved is None else _tri_inv_saved(a_mat, t_saved)
    t_mat = t_f32.astype(bf16)
    eg = jnp.exp(gc)
    u = _bdot("hij,hjv->hiv", t_mat, vb.astype(bf16))
    w = _bdot("hij,hjk->hik", t_mat, (kb * eg).astype(bf16))
    qk = _bdot("hik,hjk->hij", q.astype(bf16), kbf) * decay_incl
    g_last = jnp.sum(g, axis=1, keepdims=True)
    k_tail = k * jnp.exp(g_last - gc)
    Sb = S.astype(bf16)
    v_new = u - _bdot("hck,hkv->hcv", w.astype(bf16), Sb)
    o = _bdot("hck,hkv->hcv", (q * eg).astype(bf16), Sb) + _bdot("hij,hjv->hiv", qk.astype(bf16), v_new.astype(bf16))
    S_new = S * jnp.exp(g_last) + _bdot("hck,hcv->hkv", k_tail.astype(bf16), v_new.astype(bf16))
    ob = o * lax.rsqrt(jnp.mean(o * o, axis=-1, keepdims=True) + NORM_EPS) * norm_w
    return (ob * _silu(z), S_new) + ((t_f32,) if want_t else ())


DELTA_CHUNKS_PER_STEP = 8


def _delta_in_specs(rev, N):
    C = DELTA_CHUNKS_PER_STEP * B_CHUNK
    ix = (lambda n: N - 1 - n) if rev else (lambda n: n)
    specs = [pl.BlockSpec((C, 3 * B_HEADS * LANE), lambda n: (ix(n), 0))]
    specs += [pl.BlockSpec((C, LANE), lambda n, h=h: (ix(n), CB_Z + h)) for h in range(B_HEADS)]
    specs += [
        pl.BlockSpec((C, LANE), lambda n: (ix(n), CB_LG)),
        pl.BlockSpec((B_HEADS, 1, 1), lambda n: (0, 0, 0)),
        pl.BlockSpec((B_HEADS, 1, 1), lambda n: (0, 0, 0)),
        pl.BlockSpec((1, LANE), lambda n: (0, 0)),
    ]
    return specs


def _delta_inputs(u, c_ref, z_refs, lg, al, dt, nw):
    H = B_HEADS
    rows = slice(u * B_CHUNK, (u + 1) * B_CHUNK)
    part = lambda p: jnp.stack([c_ref[rows, LANE * (p * H + h):LANE * (p * H + h + 1)] for h in range(H)])
    return (part(0), part(1), part(2), jnp.stack([z[rows, :] for z in z_refs]), lg[rows, :], al[...], dt[...], nw[...])


def _delta_fwd(name, c, proj, a_log, dt_bias, norm_w, o_ab):
    T = c.shape[0]
    C = B_CHUNK
    N = T // C
    Dh = B_HEAD_DIM
    H = B_HEADS

    def body(*refs):
        c_ref, z_refs, (lg, al, dt, nw) = refs[0], refs[1:1 + H], refs[1 + H:5 + H]
        o_ref, s_ref, t_ref, S = refs[6 + H:]

        @pl.when(pl.program_id(0) == 0)
        def _():
            S[...] = jnp.zeros_like(S)

        s = S[...]
        for u in range(U):
            s_ref[:, u] = s
            ob, s, t = _delta_chunk_fn(*_delta_inputs(u, c_ref, z_refs, lg, al, dt, nw), s, want_t=True)
            for h in range(H):
                o_ref[u * C:(u + 1) * C, LANE * h:LANE * (h + 1)] = ob[h].astype(o_ref.dtype)
            t_ref[:, u] = t
        S[...] = s

    U = DELTA_CHUNKS_PER_STEP
    return pl.pallas_call(
        body, name=name, grid=(N // U,),
        in_specs=_delta_in_specs(False, N // U) + [pl.BlockSpec(memory_space=pl.ANY)],
        out_specs=[pl.BlockSpec((U * C, H * LANE), lambda n: (n, 1)),
                   pl.BlockSpec((H, U, Dh, Dh), lambda n: (0, n, 0, 0)),
                   pl.BlockSpec((H, U, C, C), lambda n: (0, n, 0, 0))],
        out_shape=[jax.ShapeDtypeStruct(o_ab.shape, o_ab.dtype), jax.ShapeDtypeStruct((H, N, Dh, Dh), f32),
                   jax.ShapeDtypeStruct((H, N, C, C), f32)],
        input_output_aliases={5 + H: 0},
        scratch_shapes=[pltpu.VMEM((H, Dh, Dh), f32)],
        compiler_params=_cparams("arbitrary"),
    )(c, *([proj] * H), proj, a_log, dt_bias, norm_w, o_ab)


def _delta_bwd(name, c, proj, a_log, dt_bias, norm_w, s_saved, t_saved, d_oab, dproj):
    T = c.shape[0]
    C = B_CHUNK
    N = T // C
    Dh = B_HEAD_DIM
    H = B_HEADS

    def body(*refs):
        c_ref, z_refs, (lg, al, dt, nw) = refs[0], refs[1:1 + H], refs[1 + H:5 + H]
        s_ref, t_ref, do_ref = refs[5 + H:8 + H]
        dc, dtail, dal, ddt, dnw, dS = refs[9 + H:]

        @pl.when(pl.program_id(0) == 0)
        def _():
            dS[...] = jnp.zeros_like(dS)
            dal[...] = jnp.zeros_like(dal)
            ddt[...] = jnp.zeros_like(ddt)
            dnw[...] = jnp.zeros_like(dnw)

        ds = dS[...]
        for u in reversed(range(U)):
            rows = slice(u * C, (u + 1) * C)
            _, vjp = jax.vjp(functools.partial(_delta_chunk_fn, t_saved=t_ref[:, u]),
                             *_delta_inputs(u, c_ref, z_refs, lg, al, dt, nw), s_ref[:, u])
            do = jnp.stack([do_ref[rows, LANE * h:LANE * (h + 1)] for h in range(H)])
            g = vjp((do, ds))
            for h in range(H):
                for p in range(3):
                    dc[rows, LANE * (p * H + h):LANE * (p * H + h + 1)] = g[p][h]
                dtail[rows, LANE * h:LANE * (h + 1)] = g[3][h].astype(dtail.dtype)
            dtail[rows, LANE * H:LANE * (H + 1)] = g[4].astype(dtail.dtype)
            dtail[rows, LANE * (H + 1):] = jnp.zeros((C, LANE), dtail.dtype)
            dal[...] += g[5]
            ddt[...] += g[6]
            dnw[...] += g[7]
            ds = g[8]
        dS[...] = ds

    U = DELTA_CHUNKS_PER_STEP
    NB = N // U
    rn = lambda n: NB - 1 - n
    return pl.pallas_call(
        body, name=name, grid=(NB,),
        in_specs=_delta_in_specs(True, NB) + [
            pl.BlockSpec((H, U, Dh, Dh), lambda n: (0, rn(n), 0, 0)),
            pl.BlockSpec((H, U, C, C), lambda n: (0, rn(n), 0, 0)),
            pl.BlockSpec((U * C, H * LANE), lambda n: (rn(n), 1)),
            pl.BlockSpec(memory_space=pl.ANY),
        ],
        out_specs=[
            pl.BlockSpec((U * C, 3 * H * LANE), lambda n: (rn(n), 0)),
            pl.BlockSpec((U * C, (H + 2) * LANE), lambda n: (rn(n), CB_Z // (H + 2))),
            pl.BlockSpec((H, 1, 1), lambda n: (0, 0, 0)),
            pl.BlockSpec((H, 1, 1), lambda n: (0, 0, 0)),
            pl.BlockSpec((1, LANE), lambda n: (0, 0)),
        ],
        out_shape=[jax.ShapeDtypeStruct((T, 3 * H * Dh), f32), jax.ShapeDtypeStruct(dproj.shape, dproj.dtype),
                   jax.ShapeDtypeStruct((H, 1, 1), f32), jax.ShapeDtypeStruct((H, 1, 1), f32),
                   jax.ShapeDtypeStruct((1, LANE), f32)],
        input_output_aliases={8 + H: 1},
        scratch_shapes=[pltpu.VMEM((H, Dh, Dh), f32)],
        compiler_params=_cparams("arbitrary"),
    )(c, *([proj] * H), proj, a_log, dt_bias, norm_w, s_saved, t_saved, d_oab, dproj)


def _gate_matmuls(xc, wa_ref, wx_ref):
    bw = wa_ref.shape[-1]
    xb = xc.astype(bf16)
    blocks = [xb[:, bw * h:bw * (h + 1)] for h in range(LRU_BLOCKS)]
    return (jnp.concatenate([_dot(blocks[h], wa_ref[h], NN) for h in range(LRU_BLOCKS)], axis=1),
            jnp.concatenate([_dot(blocks[h], wx_ref[h], NN) for h in range(LRU_BLOCKS)], axis=1))


def _gates_fwd(name, xc, w_a, w_x, pars, tm=512):
    T, Wd = xc.shape
    tm = min(tm, T)

    def body(x_ref, wa_ref, wx_ref, ba, bx, lam, a_ref, b_ref):
        x = x_ref[...]
        pr, pi = _gate_matmuls(x, wa_ref, wx_ref)
        a_ref[...], b_ref[...] = _rglru_pre_fn(pr, pi, x, ba[...], bx[...], lam[...])

    row = pl.BlockSpec((tm, Wd), lambda i: (i, 0))
    return pl.pallas_call(
        body, name=name, grid=(T // tm,),
        in_specs=[row, _whole_spec(w_a), _whole_spec(w_x)] + [_whole_spec(p) for p in pars],
        out_specs=[row, row], out_shape=[jax.ShapeDtypeStruct((T, Wd), f32)] * 2,
        compiler_params=_cparams("parallel"),
    )(xc, w_a, w_x, *pars)


def _gates_bwd(name, xc, w_a, w_x, pars, lam_t, h_prev, tm=512):
    T, Wd = xc.shape
    tm = min(tm, T)
    bw = Wd // LRU_BLOCKS

    def body(x_ref, wa_ref, wx_ref, ba, bx, lam, lt_ref, hp_ref, dx_ref, dr_ref, di_ref, dba, dbx, dlam):
        x = x_ref[...]
        pr, pi = _gate_matmuls(x, wa_ref, wx_ref)
        _, vjp = jax.vjp(_rglru_pre_fn, pr, pi, x, ba[...], bx[...], lam[...])
        lt = lt_ref[...]
        dpr, dpi, dxc, g_ba, g_bx, g_lam = vjp((lt * hp_ref[...], lt))
        dprb, dpib = dpr.astype(bf16), dpi.astype(bf16)
        dx_ref[...] = dxc + jnp.concatenate(
            [_dot(dprb[:, bw * h:bw * (h + 1)], wa_ref[h], NT) + _dot(dpib[:, bw * h:bw * (h + 1)], wx_ref[h], NT)
             for h in range(LRU_BLOCKS)], axis=1)
        dr_ref[...] = dprb
        di_ref[...] = dpib

        @pl.when(pl.program_id(0) == 0)
        def _():
            dba[...] = jnp.zeros_like(dba)
            dbx[...] = jnp.zeros_like(dbx)
            dlam[...] = jnp.zeros_like(dlam)

        dba[...] += g_ba
        dbx[...] += g_bx
        dlam[...] += g_lam

    row = pl.BlockSpec((tm, Wd), lambda i: (i, 0))
    vec = pl.BlockSpec((1, Wd), lambda i: (0, 0))
    return pl.pallas_call(
        body, name=name, grid=(T // tm,),
        in_specs=[row, _whole_spec(w_a), _whole_spec(w_x)] + [_whole_spec(p) for p in pars] + [row, row],
        out_specs=[row, row, row, vec, vec, vec],
        out_shape=[jax.ShapeDtypeStruct((T, Wd), f32), jax.ShapeDtypeStruct((T, Wd), bf16),
                   jax.ShapeDtypeStruct((T, Wd), bf16)] + [jax.ShapeDtypeStruct((1, Wd), f32)] * 3,
        compiler_params=_cparams("arbitrary"),
    )(xc, w_a, w_x, *pars, lam_t, h_prev)


def _blockdiag_bwd_dw(name, xc, dpr, dpi, tk=512):
    T, Wd = xc.shape
    bw = Wd // LRU_BLOCKS
    tk = min(tk, T)

    def body(x_ref, dr, di, oa, ox):
        @pl.when(pl.program_id(1) == 0)
        def _():
            oa[...] = jnp.zeros_like(oa)
            ox[...] = jnp.zeros_like(ox)

        xb = x_ref[...].astype(bf16)
        oa[...] += _dot(xb, dr[...].astype(bf16), TN)
        ox[...] += _dot(xb, di[...].astype(bf16), TN)

    xs = pl.BlockSpec((tk, bw), lambda h, k: (k, h))
    ws = pl.BlockSpec((None, bw, bw), lambda h, k: (h, 0, 0))
    return pl.pallas_call(
        body, name=name, grid=(LRU_BLOCKS, T // tk), in_specs=[xs, xs, xs], out_specs=[ws, ws],
        out_shape=[jax.ShapeDtypeStruct((LRU_BLOCKS, bw, bw), f32)] * 2,
        compiler_params=_cparams("parallel", "arbitrary"),
    )(xc, dpr, dpi)


def _scan(name, a, proj, reverse, b=None, h=None, dhg=None, tt=512, cb=512):
    T, Wd = a.shape
    tt, cb = min(tt, T), min(cb, Wd)
    nt = T // tt
    ng = tt // SUBLANE

    def body(a_ref, g_ref, *rest):
        n_in = 2 if reverse else 1
        ins, outs, (carry, carry_a) = rest[:n_in], rest[n_in:-2], rest[-2:]

        @pl.when(pl.program_id(1) == 0)
        def _():
            carry[...] = jnp.zeros_like(carry)
            carry_a[...] = jnp.zeros_like(carry_a)

        row = lax.broadcasted_iota(jnp.int32, (SUBLANE, cb), 0)

        def group(g, c):
            hp, ap = c
            rows = pl.ds(pl.multiple_of(g * SUBLANE, SUBLANE), SUBLANE)
            A = a_ref[rows, :]
            gate = g_ref[rows, :]
            a_first = jnp.broadcast_to(A[0:1, :], (SUBLANE, cb))
            if reverse:
                _, vjp = jax.vjp(_rec_gate_fn, ins[0][rows, :], gate)
                B, narrow = vjp((ins[1][rows, :],))
                A = jnp.where(row == SUBLANE - 1, ap, pltpu.roll(A, SUBLANE - 1, axis=0))
            else:
                B = ins[0][rows, :]
            for s in (1, 2, 4):
                sh = (SUBLANE - s) if reverse else s
                As = pltpu.roll(A, sh, axis=0)
                Bs = pltpu.roll(B, sh, axis=0)
                valid = (row < SUBLANE - s) if reverse else (row >= s)
                B = jnp.where(valid, A * Bs + B, B)
                A = jnp.where(valid, A * As, A)
            hcur = A * hp + B
            outs[0][rows, :] = hcur
            if not reverse:
                outs[1][rows, :] = jnp.where(row == 0, hp, pltpu.roll(hcur, 1, axis=0))
                narrow = _rec_gate_fn(hcur, gate)[0]
            edge = hcur[0:1, :] if reverse else hcur[SUBLANE - 1:SUBLANE, :]
            return (jnp.broadcast_to(edge, (SUBLANE, cb)), a_first), narrow

        def pair(pi, c):
            p = (ng // 2 - 1 - pi) if reverse else pi
            c, first = group(2 * p + (1 if reverse else 0), c)
            c, second = group(2 * p + (0 if reverse else 1), c)
            lo, hi = (second, first) if reverse else (first, second)
            rows = pl.ds(pl.multiple_of(p * 2 * SUBLANE, 2 * SUBLANE), 2 * SUBLANE)
            outs[-1][rows, :] = jnp.concatenate([lo, hi], axis=0).astype(bf16)
            return c

        carry[...], carry_a[...] = lax.fori_loop(0, ng // 2, pair, (carry[...], carry_a[...]))

    nc = Wd // cb
    tok = (lambda i: nt - 1 - i) if reverse else (lambda i: i)
    spec = pl.BlockSpec((tt, cb), lambda c, i: (tok(i), c))
    gate_half = pl.BlockSpec((tt, cb), lambda c, i: (tok(i), nc + c))
    if reverse:
        args, out_specs = (a, proj, h, dhg), [spec, gate_half]
        out_shape = [jax.ShapeDtypeStruct((T, Wd), f32), jax.ShapeDtypeStruct((T, 2 * Wd), bf16)]
    else:
        args, out_specs = (a, proj, b), [spec] * 3
        out_shape = [jax.ShapeDtypeStruct((T, Wd), f32)] * 2 + [jax.ShapeDtypeStruct((T, Wd), bf16)]
    return pl.pallas_call(
        body, name=name, grid=(nc, nt), in_specs=[spec, gate_half] + [spec] * (len(args) - 2), out_specs=out_specs,
        out_shape=out_shape,
        scratch_shapes=[pltpu.VMEM((SUBLANE, cb), f32), pltpu.VMEM((SUBLANE, cb), f32)],
        compiler_params=_cparams("parallel", "arbitrary"),
    )(*args)


def _relu2_epilogue(r):
    h = jnp.maximum(r, 0.0)
    return r, h * h


def _drelu2_epilogue(r, a):
    return (r * (2.0 * jnp.maximum(a.astype(f32), 0.0)),)


def _residual_cot(through, upper):
    return (through + DN_ALPHA * upper,)


def _merge_cols(name, g, tm=256):
    _, L, R, s = g.shape

    def body(g_ref, o_ref):
        for d in range(N_DEV):
            o_ref[:, s * d:s * (d + 1)] = g_ref[d].astype(bf16)
        o_ref[:, N_DEV * s:] = jnp.zeros((tm, HYB_PROJ_PAD - N_DEV * s), bf16)

    return pl.pallas_call(
        body, name=name, grid=(L, R // tm),
        in_specs=[pl.BlockSpec((N_DEV, None, tm, s), lambda l, i: (0, l, i, 0))],
        out_specs=pl.BlockSpec((None, tm, HYB_PROJ_PAD), lambda l, i: (l, i, 0)),
        out_shape=jax.ShapeDtypeStruct((L, R, HYB_PROJ_PAD), bf16),
        compiler_params=_cparams("parallel", "parallel"),
    )(g)


def _split_cols(name, dw, tm=256):
    R = dw.shape[0]
    s = HYB_PROJ // N_DEV

    def body(g_ref, o_ref):
        for d in range(N_DEV):
            o_ref[d] = g_ref[:, s * d:s * (d + 1)].astype(bf16)

    return pl.pallas_call(
        body, name=name, grid=(R // tm,),
        in_specs=[pl.BlockSpec((tm, HYB_PROJ_PAD), lambda i: (i, 0))],
        out_specs=pl.BlockSpec((N_DEV, tm, s), lambda i: (0, i, 0)),
        out_shape=jax.ShapeDtypeStruct((N_DEV, R, s), bf16),
        compiler_params=_cparams("parallel"),
    )(dw)


def _rows_to_dev(dw):
    nb, r, c = dw.shape
    t = dw.reshape(nb, N_DEV, r // N_DEV, c)
    return jnp.moveaxis(t, 1, 0).reshape(N_DEV, nb * (r // N_DEV), c).astype(bf16)


def _ln_epilogue(r, x, g, b):
    y = _ln_res_fn(x, r, g, b)[0]
    return r, y, y


def _hybrid_fwd(tag, x, xb, W, j, cos, sin, ln):
    proj = _mm(f"{tag}_proj", xb, W["hyb_w_in"][j], "nn", b_kind="lead", b_lead=0)
    o_a = _attn_fwd(f"{tag}_attn", proj, cos, sin, W["hyb_sinks"][j][None, :])
    c = _conv_fwd(f"{tag}_conv", proj, CB_CONV, 12, W["hyb_conv_w"][j], None)
    o_ab, s_saved, t_saved = _delta_fwd(f"{tag}_delta", c, proj, W["hyb_a_log"][j].reshape(B_HEADS, 1, 1),
                                        W["hyb_dt_bias"][j].reshape(B_HEADS, 1, 1), W["hyb_norm_w"][j][None, :], o_a)
    mix, x1, x1b = _mm(f"{tag}_out", o_ab, W["hyb_w_out"][j], "nn", b_kind="lead", b_lead=0, epilogue=_ln_epilogue,
                       extras=(x,), params=ln, out_dtypes=(f32, f32, bf16))
    return mix, x1, x1b, (proj, c, s_saved, t_saved, o_ab)


def _hybrid_bwd(tag, x, dmix, addend, W, j, cos, sin, saved, G, send_early):
    proj, c, s_saved, t_saved, o_ab = saved
    T = x.shape[0]
    d_oab = _mm(f"{tag}_dout", dmix, W["hyb_w_out"][j], "nt", b_kind="lead", b_lead=0)
    G["hyb_w_out"][j] = _mm(f"{tag}_dwout", o_ab, dmix, "tn", out_dtypes=(bf16,)).reshape(N_DEV, -1, D_MODEL)
    sinks = W["hyb_sinks"][j][None, :] + send_early({("hyb_w_out", j): G["hyb_w_out"][j]})
    dproj, dsinks = _attn_bwd(f"{tag}_dattn", proj, cos, sin, sinks, d_oab)
    a_log = W["hyb_a_log"][j].reshape(B_HEADS, 1, 1)
    dt_bias = W["hyb_dt_bias"][j].reshape(B_HEADS, 1, 1)
    dc, dproj, dal, ddt, dnw = _delta_bwd(f"{tag}_ddelta", c, proj, a_log, dt_bias, W["hyb_norm_w"][j][None, :],
                                          s_saved, t_saved, d_oab, dproj)
    dproj, dconv_w, _ = _conv_bwd(f"{tag}_dconv", dc, proj, CB_CONV, 12, W["hyb_conv_w"][j], dproj, CB_CONV)
    dx = _mm(f"{tag}_dx", dproj, W["hyb_w_in"][j], "nt", b_kind="lead", b_lead=0,
             **({} if addend is None else dict(epilogue=_residual_cot, extras=(addend,))))
    G["hyb_w_in"][j] = _split_cols(f"{tag}_dwin_split", _mm(f"{tag}_dwin", x, dproj, "tn", tn=1536))
    G["hyb_sinks"][j] = dsinks[0]
    G["hyb_conv_w"][j] = dconv_w
    G["hyb_a_log"][j] = dal.reshape(B_HEADS)
    G["hyb_dt_bias"][j] = ddt.reshape(B_HEADS)
    G["hyb_norm_w"][j] = dnw[0]
    return dx


def _rec_fwd(tag, x, xb, W, j, ln):
    Wd = D_MODEL
    proj = _mm(f"{tag}_proj", xb, W["rec_w_in"][j], "nn", b_kind="lead", b_lead=0)
    xc = _conv_fwd(f"{tag}_conv", proj, 0, Wd // LANE, W["rec_conv_w"][j], W["rec_conv_b"][j][None, :])
    pars = [W["rec_b_a"][j][None, :], W["rec_b_x"][j][None, :], W["rec_lambda"][j][None, :]]
    a, b = _gates_fwd(f"{tag}_gates", xc, W["rec_w_a"][j][0], W["rec_w_x"][j][0], pars)
    h, h_prev, hg = _scan(f"{tag}_scan", a, proj, False, b=b)
    mix, x1, x1b = _mm(f"{tag}_out", hg, W["rec_w_out"][j], "nn", b_kind="lead", b_lead=0, epilogue=_ln_epilogue,
                       extras=(x,), params=ln, out_dtypes=(f32, f32, bf16))
    return mix, x1, x1b, (proj, xc, a, h, h_prev, hg)


def _rec_bwd(tag, x, dmix, addend, W, j, saved, G, send_early):
    proj, xc, a, h, h_prev, hg = saved
    Wd = D_MODEL
    dhg = _mm(f"{tag}_dout", dmix, W["rec_w_out"][j], "nt", b_kind="lead", b_lead=0)
    G["rec_w_out"][j] = _mm(f"{tag}_dwout", hg, dmix, "tn", out_dtypes=(bf16,)).reshape(N_DEV, -1, D_MODEL)
    sent = send_early({("rec_w_out", j): G["rec_w_out"][j]})
    lam_t, dproj = _scan(f"{tag}_dscan", a, proj, True, h=h, dhg=dhg)
    pars = [W["rec_b_a"][j][None, :] + sent, W["rec_b_x"][j][None, :], W["rec_lambda"][j][None, :]]
    dxc, dpr, dpi, db_a, db_x, dlam = _gates_bwd(f"{tag}_dgates", xc, W["rec_w_a"][j][0], W["rec_w_x"][j][0], pars,
                                                 lam_t, h_prev)
    dwa, dwx = _blockdiag_bwd_dw(f"{tag}_dgates_dw", xc, dpr, dpi)
    G["rec_w_a"][j], G["rec_w_x"][j] = _rows_to_dev(dwa), _rows_to_dev(dwx)
    dproj, dconv_w, dconv_b = _conv_bwd(f"{tag}_dconv", dxc, proj, 0, Wd // LANE, W["rec_conv_w"][j], dproj, 0)
    dx = _mm(f"{tag}_dx", dproj, W["rec_w_in"][j], "nt", b_kind="lead", b_lead=0,
             **({} if addend is None else dict(epilogue=_residual_cot, extras=(addend,))))
    G["rec_w_in"][j] = _mm(f"{tag}_dwin", x, dproj, "tn", o_kind="devcol", out_dtypes=(bf16,), tn=2048)
    G["rec_conv_w"][j] = dconv_w
    G["rec_conv_b"][j] = dconv_b
    G["rec_b_a"][j] = db_a[0]
    G["rec_b_x"][j] = db_x[0]
    G["rec_lambda"][j] = dlam[0]
    return dx


def _local_step(x, target, W, load_layer, grads_ready):
    T = x.shape[0]
    cos, sin = _rope_tables(T)
    saved = []
    xb = x
    for layer in range(DEPTH):
        j = layer // 2
        tag = f"L{layer}"
        load_layer(layer, "mixer", x)
        ln1 = (W["ln1_g"][layer][None, :], W["ln1_b"][layer][None, :])
        if layer % 2 == 0:
            mix, x1, x1b, sv = _hybrid_fwd(tag, x, xb, W, j, cos, sin, ln1)
        else:
            mix, x1, x1b, sv = _rec_fwd(tag, x, xb, W, j, ln1)
        load_layer(layer, "mlp", x1)
        a, h2 = _mm(f"{tag}_mlp1", x1b, W["mlp_w1"][layer], "nn", b_kind="devcol", b_lead=0, epilogue=_relu2_epilogue,
                    out_dtypes=(bf16, bf16), tm=2048)
        ln2 = (W["ln2_g"][layer][None, :], W["ln2_b"][layer][None, :])
        y, x2, x2b = _mm(f"{tag}_mlp2", h2, W["mlp_w2"][layer], "nn", b_kind="lead", b_lead=0, epilogue=_ln_epilogue,
                         extras=(x1,), params=ln2, out_dtypes=(f32, f32, bf16))
        saved.append((x, xb, sv, mix, x1, x1b, a, h2, y))
        x, xb = x2, x2b
    loss, dx = _loss_head(x, target)

    G = {k: [None] * (DEPTH if k.startswith(("ln", "mlp")) else DEPTH // 2) for k in (
        "hyb_w_in", "hyb_sinks", "hyb_conv_w", "hyb_a_log", "hyb_dt_bias", "hyb_norm_w", "hyb_w_out",
        "rec_w_in", "rec_conv_w", "rec_conv_b", "rec_w_a", "rec_b_a", "rec_w_x", "rec_b_x", "rec_lambda", "rec_w_out",
        "ln1_g", "ln1_b", "mlp_w1", "mlp_w2", "ln2_g", "ln2_b")}
    held = {}
    cot_rows, cot_fn = [(dx, 0, D_MODEL)], None
    for layer in reversed(range(DEPTH)):
        j = layer // 2
        tag = f"L{layer}"
        x0, x0b, sv, mix, x1, x1b, a, h2, y = saved[layer]
        ln2 = [W["ln2_g"][layer][None, :], W["ln2_b"][layer][None, :]]
        (dy, dyb), (dg2, db2) = _tl_bwd(f"{tag}_dln2", _ln_res_fn, [(x1, 0, D_MODEL), (y, 0, D_MODEL)], ln2,
                                        cot_rows, cot_fn=cot_fn, skip=(0,), bf16_copy=True)
        G["ln2_g"][layer], G["ln2_b"][layer] = dg2[0], db2[0]
        da = _mm(f"{tag}_dmlp2", dyb, W["mlp_w2"][layer], "nt", b_kind="lead", b_lead=0, epilogue=_drelu2_epilogue,
                 extras=(a,), out_dtypes=(bf16,), tm=2048, tn=512)
        G["mlp_w2"][layer] = _mm(f"{tag}_dw2", h2, dyb, "tn", out_dtypes=(bf16,), tm=2048).reshape(N_DEV, -1, D_MODEL)
        dx1 = _mm(f"{tag}_dmlp1", da, W["mlp_w1"][layer], "nt", b_kind="devcol", b_lead=0, tm=2048)
        G["mlp_w1"][layer] = _mm(f"{tag}_dw1", x1b, da, "tn", o_kind="devcol", out_dtypes=(bf16,), tn=2048)
        ln1 = [W["ln1_g"][layer][None, :], W["ln1_b"][layer][None, :]]
        (dmix, dmixb), (dg1, db1) = _tl_bwd(f"{tag}_dln1", _ln_res_fn, [(x0, 0, D_MODEL), (mix, 0, D_MODEL)], ln1,
                                            [(dx1, 0, D_MODEL), (dy, 0, D_MODEL)], cot_fn=_residual_cot, skip=(0,),
                                            bf16_copy=True)
        G["ln1_g"][layer], G["ln1_b"][layer] = dg1[0], db1[0]
        dx0_a = dmix if layer == 0 else None
        held.update({(k, layer): G[k][layer] for k in ("mlp_w1", "mlp_w2")})
        early = functools.partial(grads_ready, f"l{layer}_early", held)
        if layer % 2 == 0:
            dx = _hybrid_bwd(tag, x0b, dmixb, dx0_a, W, j, cos, sin, sv, G, early)
        else:
            dx = _rec_bwd(tag, x0b, dmixb, dx0_a, W, j, sv, G, early)
        held = {(k, i): G[k][i] for k, i in _layer_weights(layer)[:-2] if not k.endswith("w_out")}
        cot_rows, cot_fn = [(dx, 0, D_MODEL), (dmix, 0, D_MODEL)], _residual_cot
    grads_ready("l0_late", held, {})
    big = {k for k, _ in BIG}
    return loss, dx, {k: jnp.stack(v) for k, v in G.items() if k not in big}


def _layer_weights(layer):
    j = layer // 2
    mixer = ["hyb_w_in", "hyb_w_out"] if layer % 2 == 0 else ["rec_w_in", "rec_w_out", "rec_w_a", "rec_w_x"]
    return [(k, j) for k in mixer] + [("mlp_w1", layer), ("mlp_w2", layer)]


def _my_coords():
    return lax.axis_index("x"), lax.axis_index("y"), lax.axis_index("c")


def _all_gather(name, arrays):
    na = len(arrays)

    def body(*refs):
        x_refs, out_refs = refs[:na], refs[na:2 * na]
        send_sems, recv_sems, local_sems = refs[2 * na:]
        x, y, c = _my_coords()
        me, sibling = (x, y, c), (x, y, 1 - c)
        chips = [(1 - x, y), (x, 1 - y), (1 - x, 1 - y)]

        def blk(a, px, py, pc):
            return out_refs[a].at[4 * px + 2 * py + pc]

        def copy(a, k, block, to, src=None):
            return pltpu.make_async_remote_copy(
                src_ref=blk(a, *block) if src is None else src, dst_ref=blk(a, *block),
                send_sem=send_sems.at[a, k], recv_sem=recv_sems.at[a, k],
                device_id=to, device_id_type=pl.DeviceIdType.MESH)

        mine = [pltpu.make_async_copy(x_refs[a], blk(a, *me), local_sems.at[a]) for a in range(na)]
        for cp in mine:
            cp.start()
        first = []
        for a in range(na):
            first.append(copy(a, 0, me, sibling, src=x_refs[a]))
            first += [copy(a, 1 + j, me, (*chip, c), src=x_refs[a]) for j, chip in enumerate(chips)]
        for cp in first:
            cp.start()
        passed = []
        for a in range(na):
            for j, chip in enumerate(chips):
                copy(a, 1 + j, (*chip, c), me).wait_recv()
                passed.append(copy(a, 4 + j, (*chip, c), sibling))
                passed[-1].start()
        for a in range(na):
            copy(a, 0, sibling, me).wait_recv()
            for j, chip in enumerate(chips):
                copy(a, 4 + j, (*chip, 1 - c), me).wait_recv()
        for cp in first + passed:
            cp.wait_send()
        for cp in mine:
            cp.wait()

    return pl.pallas_call(
        body, name=name,
        out_shape=[jax.ShapeDtypeStruct((N_DEV,) + a.shape, a.dtype) for a in arrays],
        in_specs=[pl.BlockSpec(memory_space=pl.ANY)] * na,
        out_specs=[pl.BlockSpec(memory_space=pl.ANY)] * na,
        scratch_shapes=[pltpu.SemaphoreType.DMA((na, 7)), pltpu.SemaphoreType.DMA((na, 7)),
                        pltpu.SemaphoreType.DMA((na,))],
    )(*arrays)


_HBM = pl.BlockSpec(memory_space=pltpu.HBM)
_SEM = pl.BlockSpec(memory_space=pltpu.SEMAPHORE)


def _flip(k, x, y, c):
    return ((1 - x) if k & 4 else x, (1 - y) if k & 2 else y, (1 - c) if k & 1 else c)


_PEERS = {"gather": (1, 2, 4, 6), "scatter": (1, 2, 3, 4, 5, 6, 7)}


def _push_copies(kind, x_refs, land_refs, send_sems, recv_sems, local_sems):
    x, y, c = _my_coords()
    me = 4 * x + 2 * y + c
    peers = _PEERS[kind]
    remote, local = [], []
    for a in range(len(x_refs)):
        local.append(pltpu.make_async_copy(x_refs[a] if kind == "gather" else x_refs[a].at[me], land_refs[a].at[me],
                                           local_sems.at[a]))
        for n, k in enumerate(peers):
            px, py, pc = _flip(k, x, y, c)
            remote.append(pltpu.make_async_remote_copy(
                src_ref=x_refs[a] if kind == "gather" else x_refs[a].at[4 * px + 2 * py + pc],
                dst_ref=land_refs[a].at[me],
                send_sem=send_sems.at[a * len(peers) + n], recv_sem=recv_sems.at[a * len(peers) + n],
                device_id=(px, py, pc), device_id_type=pl.DeviceIdType.MESH))
    return remote, local


def _pass_to_sibling(name, lands):
    na = len(lands)
    chips = (2, 4, 6)

    def body(*refs):
        out_refs, send_sems, recv_sems = refs[na:2 * na], refs[2 * na], refs[2 * na + 1]
        x, y, c = _my_coords()
        cps = []
        for a in range(na):
            for n, k in enumerate(chips):
                px, py, _ = _flip(k, x, y, c)
                cps.append(pltpu.make_async_remote_copy(
                    src_ref=out_refs[a].at[4 * px + 2 * py + c], dst_ref=out_refs[a].at[4 * px + 2 * py + c],
                    send_sem=send_sems.at[a * 3 + n], recv_sem=recv_sems.at[a * 3 + n],
                    device_id=(x, y, 1 - c), device_id_type=pl.DeviceIdType.MESH))
        for cp in cps:
            cp.start()
        for a in range(na):
            for n, k in enumerate(chips):
                px, py, _ = _flip(k, x, y, c)
                blk = out_refs[a].at[4 * px + 2 * py + (1 - c)]
                pltpu.make_async_remote_copy(src_ref=blk, dst_ref=blk, send_sem=send_sems.at[a * 3 + n],
                                             recv_sem=recv_sems.at[a * 3 + n], device_id=(x, y, 1 - c),
                                             device_id_type=pl.DeviceIdType.MESH).wait_recv()
        for cp in cps:
            cp.wait_send()

    return pl.pallas_call(
        body, name=name,
        out_shape=[jax.ShapeDtypeStruct(l.shape, l.dtype) for l in lands],
        in_specs=[pl.BlockSpec(memory_space=pl.ANY)] * na,
        out_specs=[pl.BlockSpec(memory_space=pl.ANY)] * na,
        input_output_aliases={a: a for a in range(na)},
        scratch_shapes=[pltpu.SemaphoreType.DMA((3 * na,)), pltpu.SemaphoreType.DMA((3 * na,))],
    )(*lands)


_SIDE_EFFECT = pltpu.CompilerParams(has_side_effects=pltpu.SideEffectType.DATAFLOW_SIDE_EFFECTING)


def _push_start(name, kind, srcs, lands):
    na = len(srcs)

    def body(*refs):
        remote, local = _push_copies(kind, refs[:na], refs[na:2 * na], *refs[2 * na:2 * na + 3])
        for cp in remote + local:
            cp.start()
        token = refs[-1]
        token[...] = jnp.zeros_like(token)

    arrays = list(srcs) + list(lands)
    n_remote = na * len(_PEERS[kind])
    res = pl.pallas_call(
        body, name=name,
        out_shape=(pltpu.SemaphoreType.DMA((n_remote,)), pltpu.SemaphoreType.DMA((n_remote,)),
                   pltpu.SemaphoreType.DMA((na,)), *[pltpu.HBM(t.shape, t.dtype) for t in arrays],
                   jax.ShapeDtypeStruct((SUBLANE, LANE), f32)),
        in_specs=[_HBM] * (2 * na),
        out_specs=(_SEM, _SEM, _SEM, *[_HBM] * (2 * na), pl.BlockSpec(memory_space=pltpu.VMEM)),
        input_output_aliases={i: 3 + i for i in range(2 * na)},
        compiler_params=_SIDE_EFFECT,
    )(*[pltpu.with_memory_space_constraint(t, pltpu.HBM) for t in arrays])
    return list(res[:3]), res[3:3 + na], res[3 + na:3 + 2 * na], res[-1][:1, :1]


def _push_wait(name, kind, sems, srcs, lands, after):
    na = len(srcs)

    def body(*refs):
        remote, local = _push_copies(kind, refs[:na], refs[na:2 * na], *refs[2 * na:2 * na + 3])
        for cp in remote:
            cp.wait_send()
            cp.wait_recv()
        for cp in local:
            cp.wait()

    arrays = list(srcs) + list(lands)
    res = pl.pallas_call(
        body, name=name,
        out_shape=tuple(pltpu.HBM(t.shape, t.dtype) for t in arrays),
        in_specs=[_HBM] * (2 * na) + [_SEM] * 3 + [pl.BlockSpec(memory_space=pl.ANY)],
        out_specs=tuple([_HBM] * (2 * na)),
        input_output_aliases={i: i for i in range(2 * na)},
        compiler_params=_SIDE_EFFECT,
    )(*arrays, *sems, after)
    return res[na:]


def _sum_blocks(name, land):
    _, R, n = land.shape
    tr = R

    def body(l_ref, o_ref):
        acc = l_ref[0].astype(f32)
        for s in range(1, N_DEV):
            acc = acc + l_ref[s].astype(f32)
        o_ref[...] = acc

    return pl.pallas_call(
        body, name=name, grid=(R // tr,),
        in_specs=[pl.BlockSpec((N_DEV, tr, n), lambda i: (0, i, 0))],
        out_specs=pl.BlockSpec((tr, n), lambda i: (i, 0)),
        out_shape=jax.ShapeDtypeStruct((R, n), f32),
        compiler_params=_cparams("parallel"),
    )(land)


def _adamw(name, w, g, m, v):
    shape = w.shape
    last = shape[-1]
    rows = math.prod(shape[:-1])
    tm = 256 if rows % 256 == 0 and rows > 256 else rows
    w2, g2, m2, v2 = (t.reshape(rows, last) for t in (w, g, m, v))

    def body(w_ref, g_ref, m_ref, v_ref, d_ref, mo_ref, vo_ref):
        gg = g_ref[...]
        mn = ADAM_B1 * m_ref[...] + (1.0 - ADAM_B1) * gg
        vn = ADAM_B2 * v_ref[...] + (1.0 - ADAM_B2) * jnp.square(gg)
        m_hat = mn / (1.0 - ADAM_B1 ** ADAM_STEP)
        v_hat = vn / (1.0 - ADAM_B2 ** ADAM_STEP)
        d_ref[...] = -ADAM_LR * (m_hat / (jnp.sqrt(v_hat) + ADAM_EPS) + ADAM_WD * w_ref[...])
        mo_ref[...] = mn
        vo_ref[...] = vn

    spec = pl.BlockSpec((tm, last), lambda i: (i, 0))
    d, mn, vn = pl.pallas_call(
        body, name=name, grid=(rows // tm,), in_specs=[spec] * 4, out_specs=[spec] * 3,
        out_shape=[jax.ShapeDtypeStruct((rows, last), f32)] * 3,
        compiler_params=_cparams("parallel"),
    )(w2, g2, m2, v2)
    return d.reshape(shape), mn.reshape(shape), vn.reshape(shape)


def _adamw_land(name, lands, w, m, v, tm=256):
    L = len(lands)
    _, R, C = lands[0].shape
    tm = min(tm, R)

    def body(*refs):
        l_refs, (w_ref, m_ref, v_ref, g_ref, d_ref, mo_ref, vo_ref) = refs[:L], refs[L:]
        for k in range(L):
            @pl.when(pl.program_id(0) == k)
            def _(k=k):
                gg = l_refs[k][0].astype(f32)
                for s in range(1, N_DEV):
                    gg = gg + l_refs[k][s].astype(f32)
                g_ref[...] = gg
                mn = ADAM_B1 * m_ref[...] + (1.0 - ADAM_B1) * gg
                vn = ADAM_B2 * v_ref[...] + (1.0 - ADAM_B2) * jnp.square(gg)
                m_hat = mn / (1.0 - ADAM_B1 ** ADAM_STEP)
                v_hat = vn / (1.0 - ADAM_B2 ** ADAM_STEP)
                d_ref[...] = -ADAM_LR * (m_hat / (jnp.sqrt(v_hat) + ADAM_EPS) + ADAM_WD * w_ref[...])
                mo_ref[...] = mn
                vo_ref[...] = vn

    land_specs = [pl.BlockSpec((N_DEV, tm, C), lambda l, i, k=k: (0, jnp.where(l == k, i, 0), 0)) for k in range(L)]
    spec = pl.BlockSpec((None, tm, C), lambda l, i: (l, i, 0))
    return pl.pallas_call(
        body, name=name, grid=(L, R // tm),
        in_specs=land_specs + [spec] * 3,
        out_specs=[spec] * 4,
        out_shape=[jax.ShapeDtypeStruct((L, R, C), f32)] * 4,
        compiler_params=_cparams("arbitrary", "arbitrary"),
    )(*lands, w, m, v)


BIG = [("hyb_w_in", 2), ("hyb_w_out", 1), ("rec_w_in", 2), ("rec_w_out", 1), ("rec_w_a", 2), ("rec_w_x", 2),
       ("mlp_w1", 2), ("mlp_w2", 1)]
SMALL = [("hyb_conv_w", 2), ("rec_conv_w", 2), ("rec_conv_b", 1), ("rec_b_a", 1), ("rec_b_x", 1), ("rec_lambda", 1)]
REPL = ["hyb_sinks", "hyb_a_log", "hyb_dt_bias", "hyb_norm_w", "ln1_g", "ln1_b", "ln2_g", "ln2_b"]
WEIGHTS = ["hyb_w_in", "hyb_sinks", "hyb_conv_w", "hyb_a_log", "hyb_dt_bias", "hyb_norm_w", "hyb_w_out", "rec_w_in",
           "rec_conv_w", "rec_conv_b", "rec_w_a", "rec_b_a", "rec_w_x", "rec_b_x", "rec_lambda", "rec_w_out",
           "ln1_g", "ln1_b", "mlp_w1", "mlp_w2", "ln2_g", "ln2_b"]


def _pack_rows(parts, dtype, row_mult):
    lead = parts[0].shape[:-1]
    flat = jnp.concatenate([p.astype(dtype) for p in parts], axis=-1)
    n = flat.shape[-1]
    unit = row_mult * LANE
    pad = (-n) % unit
    if pad:
        flat = jnp.concatenate([flat, jnp.zeros(lead + (pad,), dtype)], axis=-1)
    return flat.reshape(lead + ((n + pad) // LANE, LANE))


def _gather_full(gathered, shard_shapes, table):
    flat = gathered.reshape(N_DEV, -1)
    out, off = {}, 0
    for name, ax in table:
        shp = shard_shapes[name]
        n = math.prod(shp)
        arr = flat[:, off:off + n].reshape((N_DEV,) + shp)
        off += n
        arr = jnp.moveaxis(arr, 0, ax)
        out[name] = arr.reshape(shp[:ax] + (N_DEV * shp[ax],) + shp[ax + 1:])
    return out


def _matmul_layouts(tag, gw):
    out = {}
    bw = D_MODEL // LRU_BLOCKS
    for k, g in gw.items():
        L = g.shape[1]
        if k == "hyb_w_in":
            out[k] = _merge_cols(f"{tag}_w_in_merge", g)
        elif k in ("hyb_w_out", "rec_w_out", "mlp_w2"):
            out[k] = jnp.swapaxes(g, 0, 1).reshape(L, N_DEV * g.shape[2], g.shape[3])
        elif k == "rec_w_in":
            out[k] = jnp.moveaxis(g, 0, 2).reshape(L, g.shape[2], N_DEV * g.shape[3])
        elif k in ("rec_w_a", "rec_w_x"):
            out[k] = jnp.moveaxis(g, 0, 2).reshape(L, LRU_BLOCKS, bw, bw)
        else:
            out[k] = g
    return out


def kernel(x, hyb_w_in, hyb_sinks, hyb_conv_w, hyb_a_log, hyb_dt_bias, hyb_norm_w, hyb_w_out, rec_w_in, rec_conv_w, rec_conv_b, rec_w_a, rec_b_a, rec_w_x, rec_b_x, rec_lambda, rec_w_out, ln1_g, ln1_b, mlp_w1, mlp_w2, ln2_g, ln2_b, loss_target, m_hyb_w_in, m_hyb_sinks, m_hyb_conv_w, m_hyb_a_log, m_hyb_dt_bias, m_hyb_norm_w, m_hyb_w_out, m_rec_w_in, m_rec_conv_w, m_rec_conv_b, m_rec_w_a, m_rec_b_a, m_rec_w_x, m_rec_b_x, m_rec_lambda, m_rec_w_out, m_ln1_g, m_ln1_b, m_mlp_w1, m_mlp_w2, m_ln2_g, m_ln2_b, v_hyb_w_in, v_hyb_sinks, v_hyb_conv_w, v_hyb_a_log, v_hyb_dt_bias, v_hyb_norm_w, v_hyb_w_out, v_rec_w_in, v_rec_conv_w, v_rec_conv_b, v_rec_w_a, v_rec_b_a, v_rec_w_x, v_rec_b_x, v_rec_lambda, v_rec_w_out, v_ln1_g, v_ln1_b, v_mlp_w1, v_mlp_w2, v_ln2_g, v_ln2_b):
    args = locals()
    w = {k: args[k] for k in WEIGHTS}
    m = {k: args["m_" + k] for k in WEIGHTS}
    v = {k: args["v_" + k] for k in WEIGHTS}
    shard_shapes = {k: tuple(t.shape) for k, t in w.items()}
    xi, yi, ci = _my_coords()
    me = 4 * xi + 2 * yi + ci

    in_flight = {}

    def install(tag, names, got):
        for (k, i), arr in zip(names, _matmul_layouts(tag, {k: g for (k, _), g in zip(names, got)}).values()):
            W[k][i] = arr

    def start_gather(tag, names):
        srcs = [w[k][i:i + 1].astype(bf16) for k, i in names]
        *pending, zero = _push_start(f"gather_{tag}_start", "gather", srcs,
                                     [lax.empty((N_DEV,) + s.shape, bf16) for s in srcs])
        in_flight[tag] = (names, pending)
        return zero

    def finish_gather(tag, after):
        names, pending = in_flight.pop(tag)
        half = _push_wait(f"gather_{tag}_wait", "gather", *pending, after)
        install(tag, names, _pass_to_sibling(f"gather_{tag}_pass", half))

    def started(k, zero):
        W[k] = W[k] + zero

    def mixer_w(layer):
        return _layer_weights(layer)[:-2]

    def mlp_w(layer):
        return _layer_weights(layer)[-2:]

    gathered0 = _all_gather("gather_first", [w[k][i:i + 1].astype(bf16) for k, i in mixer_w(0)]
                            + [_pack_rows([w[k].reshape(-1) for k, _ in SMALL], f32, SUBLANE)])
    W = _gather_full(gathered0[-1], shard_shapes, SMALL)
    W.update({k: w[k] for k in REPL})
    W.update({k: {} for k, _ in BIG})
    install("l0a", mixer_w(0), gathered0[:-1])
    started("hyb_sinks", start_gather("l0b", mlp_w(0)) + start_gather("l1a", mixer_w(1)))

    def load_layer(layer, part, after):
        if part == "mixer":
            if layer == 1:
                finish_gather("l1a", after)
            if layer >= 2:
                finish_gather(f"l{layer}", after)
            if 1 <= layer < DEPTH - 1:
                started("hyb_sinks" if layer % 2 == 0 else "rec_conv_b",
                        start_gather(f"l{layer + 1}", _layer_weights(layer + 1)))
        elif layer == 0:
            finish_gather("l0b", after)
            started("ln2_g", start_gather("l1b", mlp_w(1)))
        elif layer == 1:
            finish_gather("l1b", after)

    grads_in_flight = {}

    def grads_ready(tag, a, b):
        g = {**a, **b}
        srcs = list(g.values())
        *pending, zero = _push_start(f"scatter_{tag}_start", "scatter", srcs, [lax.empty(s.shape, bf16) for s in srcs])
        grads_in_flight[tag] = (list(g.keys()), pending)
        return zero

    loss_local, grad_x, G = _local_step(x[0], loss_target[0], W, load_layer, grads_ready)
    loss = lax.psum(loss_local, MESH_AXES)

    landed = {}

    def land(tag, after):
        keys, pending = grads_in_flight[tag]
        landed.update(zip(keys, _push_wait(f"scatter_{tag}_wait", "scatter", *pending, after)))

    tags = list(grads_in_flight)
    for tag in tags[:-1]:
        land(tag, grad_x)
    rest = _pack_rows([G[k].reshape(-1) for k, _ in SMALL] + [G[k].reshape(-1) for k in REPL], f32, SUBLANE)
    g_rest = _sum_blocks("sum_rest", _all_gather("gather_rest", [rest])[0]).reshape(-1)

    grads, delta, new_m, new_v = {}, {}, {}, {}

    def adamw_big(k):
        shp = shard_shapes[k]
        s3 = (shp[0], math.prod(shp[1:-1]), shp[-1])
        lands = [landed[(k, i)].reshape((N_DEV,) + s3[1:]) for i in range(shp[0])]
        res = _adamw_land("adamw_" + k, lands, w[k].reshape(s3), m[k].reshape(s3), v[k].reshape(s3))
        grads[k], delta[k], new_m[k], new_v[k] = (r.reshape(shp) for r in res)

    late = {k for k, _ in grads_in_flight[tags[-1]][0]}
    for k in [k for k, _ in BIG if k not in late]:
        adamw_big(k)
        done = new_v[k]
    land(tags[-1], done)
    for k in [k for k, _ in BIG if k in late]:
        adamw_big(k)
    off = 0
    for k, ax in SMALL:
        full_shape = G[k].shape
        n = math.prod(full_shape)
        full = g_rest[off:off + n].reshape(full_shape)
        off += n
        s = shard_shapes[k][ax]
        grads[k] = lax.dynamic_slice_in_dim(full, me * s, s, axis=ax)
    for k in REPL:
        n = math.prod(shard_shapes[k])
        grads[k] = g_rest[off:off + n].reshape(shard_shapes[k])
        off += n

    for k in [k for k, _ in SMALL] + REPL:
        delta[k], new_m[k], new_v[k] = _adamw("adamw_" + k, w[k], grads[k], m[k], v[k])

    return (loss, grad_x[None], *[grads[k] for k in WEIGHTS], *[delta[k] for k in WEIGHTS],
            *[new_m[k] for k in WEIGHTS], *[new_v[k] for k in WEIGHTS])
```
